```python
import math
import jax, jax.numpy as jnp
from jax import lax
import numpy as np

D_MODEL = 1024
BATCH = 8
SEQ = 4096
DEPTH = 1

GRID_W = 64
CTX_LEN = 256
S5_WIDTH = D_MODEL // 2
S5_GROUP = 16
S5_GROUPS = S5_WIDTH // S5_GROUP
S5_STATE = 64
CONV_WIDTH = D_MODEL - S5_WIDTH
CONV_K = 31
MIX_WIDTH = S5_WIDTH + CONV_WIDTH
IN_COLS = S5_WIDTH + 2 * CONV_WIDTH
D_FF = 4 * D_MODEL
EPS_RMS = 1e-6
EPS_LN = 1e-5
DT_MIN = 1e-3
DT_MAX = 1e-1

kernel_name = "hybrid_s5_conformer_prefix_dit_block"


def _rms_norm(x, g):
    xf = x.astype(jnp.float32)
    y = xf * lax.rsqrt(jnp.mean(xf * xf, axis=-1, keepdims=True) + EPS_RMS)
    return (y * g.astype(jnp.float32)).astype(x.dtype)


def _layer_norm(x, g, b):
    xf = x.astype(jnp.float32)
    mu = jnp.mean(xf, axis=-1, keepdims=True)
    var = jnp.mean(jnp.square(xf - mu), axis=-1, keepdims=True)
    return (xf - mu) * lax.rsqrt(var + EPS_LN) * g.astype(jnp.float32) + b.astype(jnp.float32)


def _modulate(x, shift, scale):
    return x * (1 + scale) + shift


def _s5_discretise(lam_re, lam_im, log_dt, b_re, b_im):
    lam_re = lam_re.astype(jnp.float32)
    lam_im = lam_im.astype(jnp.float32)
    dt = jnp.exp(log_dt.astype(jnp.float32))[..., None]
    mag = jnp.exp(lam_re * dt)
    abar_re = mag * jnp.cos(lam_im * dt)
    abar_im = mag * jnp.sin(lam_im * dt)
    den = lam_re * lam_re + lam_im * lam_im
    num_re = abar_re - 1
    f_re = (num_re * lam_re + abar_im * lam_im) / den
    f_im = (abar_im * lam_re - num_re * lam_im) / den
    b_re = b_re.astype(jnp.float32)
    b_im = b_im.astype(jnp.float32)
    bbar_re = f_re[..., None] * b_re - f_im[..., None] * b_im
    bbar_im = f_re[..., None] * b_im + f_im[..., None] * b_re
    return abar_re, abar_im, bbar_re, bbar_im


def _ccombine(e1, e2):
    a1r, a1i, b1r, b1i = e1
    a2r, a2i, b2r, b2i = e2
    return (a2r * a1r - a2i * a1i,
            a2r * a1i + a2i * a1r,
            a2r * b1r - a2i * b1i + b2r,
            a2r * b1i + a2i * b1r + b2i)


def _cscan(abar_re, abar_im, bu_re, bu_im, reverse):
    shp = (1, bu_re.shape[1]) + abar_re.shape
    a_re = jnp.broadcast_to(abar_re, shp)
    a_im = jnp.broadcast_to(abar_im, shp)
    return lax.associative_scan(_ccombine, (a_re, a_im, bu_re, bu_im), reverse=reverse, axis=1)


def _s5_readout(s_re, s_im, c_re, c_im):
    return (jnp.einsum('blgp,ghp->blgh', s_re, c_re.astype(jnp.float32))
            - jnp.einsum('blgp,ghp->blgh', s_im, c_im.astype(jnp.float32)))


def _s5_glu(y, w_glu):
    g = jax.nn.gelu(y)
    return g * jax.nn.sigmoid(g @ w_glu.astype(jnp.float32))


def _s5_mixer(u_lat, u_ctx, lam_re, lam_im, log_dt, b_re, b_im, c_re, c_im, d_skip, w_glu, need_ctx):
    abar_re, abar_im, bbar_re, bbar_im = _s5_discretise(lam_re, lam_im, log_dt, b_re, b_im)
    bsz, n_lat = u_lat.shape[0], u_lat.shape[1]
    n_ctx = u_ctx.shape[1]
    u4 = u_lat.astype(jnp.float32).reshape(bsz, n_lat, S5_GROUPS, S5_GROUP)
    uc4 = u_ctx.astype(jnp.float32).reshape(bsz, n_ctx, S5_GROUPS, S5_GROUP)
    d_f = d_skip.astype(jnp.float32)
    y = d_f * u4
    yc = d_f * uc4 if need_ctx else None
    for dirn, reverse in ((0, False), (1, True)):
        ar, ai = abar_re[dirn], abar_im[dirn]
        br, bi = bbar_re[dirn], bbar_im[dirn]
        cbu_re = jnp.einsum('bngh,gph->bngp', uc4, br)
        cbu_im = jnp.einsum('bngh,gph->bngp', uc4, bi)
        _, _, cs_re, cs_im = _cscan(ar, ai, cbu_re, cbu_im, reverse)
        end = 0 if reverse else -1
        h0_re = cs_re[:, end][:, None]
        h0_im = cs_im[:, end][:, None]
        bu_re = jnp.einsum('blgh,gph->blgp', u4, br)
        bu_im = jnp.einsum('blgh,gph->blgp', u4, bi)
        ap_re, ap_im, s_re, s_im = _cscan(ar, ai, bu_re, bu_im, reverse)
        s_re = s_re + ap_re * h0_re - ap_im * h0_im
        s_im = s_im + ap_re * h0_im + ap_im * h0_re
        y = y + _s5_readout(s_re, s_im, c_re[dirn], c_im[dirn])
        if need_ctx:
            yc = yc + _s5_readout(cs_re, cs_im, c_re[dirn], c_im[dirn])
    out = _s5_glu(y.reshape(bsz, n_lat, S5_WIDTH), w_glu).astype(u_lat.dtype)
    out_c = _s5_glu(yc.reshape(bsz, n_ctx, S5_WIDTH), w_glu).astype(u_ctx.dtype) if need_ctx else None
    return out, out_c


def _conv_module(v, gate, w_dw, b_dw, ln_g, ln_b, on_grid):
    bsz, n, ch = v.shape
    h = v * jax.nn.sigmoid(gate)
    pad = (CONV_K // 2, CONV_K // 2)
    if on_grid:
        rows = n // GRID_W
        h = lax.conv_general_dilated(
            h.reshape(bsz, rows, GRID_W, ch),
            w_dw.reshape(CONV_K, 1, 1, ch).astype(h.dtype),
            window_strides=(1, 1), padding=(pad, (0, 0)),
            dimension_numbers=('NHWC', 'HWIO', 'NHWC'), feature_group_count=ch)
        h = h.reshape(bsz, n, ch)
    else:
        h = lax.conv_general_dilated(
            h, w_dw.reshape(CONV_K, 1, ch).astype(h.dtype),
            window_strides=(1,), padding=(pad,),
            dimension_numbers=('NWC', 'WIO', 'NWC'), feature_group_count=ch)
    h = h + b_dw
    return jax.nn.silu(_layer_norm(h, ln_g, ln_b)).astype(v.dtype)


def _sq_relu_mlp(h, w1, w2):
    return jnp.square(jax.nn.relu(h @ w1)) @ w2


def _fwd_setup_inputs(seed: int = 0) -> dict:
    key = jax.random.key(seed)
    ks = jax.random.split(key, 26)
    f32 = jnp.float32
    G, P, H = S5_GROUPS, S5_STATE, S5_GROUP

    def nrm(k, shape, scale):
        return jax.random.normal(k, shape, f32) * scale

    return {
        'x': nrm(ks[0], (BATCH, SEQ, D_MODEL), 1.0),
        'c': nrm(ks[1], (BATCH, D_MODEL), 1.0),
        'ctx': nrm(ks[2], (BATCH, CTX_LEN, D_MODEL), 1.0),
        'c_ctx': nrm(ks[3], (D_MODEL,), 1.0),
        'ada_w': nrm(ks[4], (DEPTH, D_MODEL, 6 * D_MODEL), 0.5 * D_MODEL ** -0.5),
        'ada_b': nrm(ks[5], (DEPTH, 6 * D_MODEL), 0.01),
        'norm1_g': 1.0 + nrm(ks[6], (DEPTH, D_MODEL), 0.02),
        'w_in': nrm(ks[7], (DEPTH, D_MODEL, IN_COLS), D_MODEL ** -0.5),
        's5_lam_re': -0.5 + nrm(ks[8], (DEPTH, 2, G, P), 0.01),
        's5_lam_im': jnp.pi * jnp.arange(P, dtype=f32) + nrm(ks[9], (DEPTH, 2, G, P), 0.01),
        's5_log_dt': jax.random.uniform(ks[10], (DEPTH, 2, G), f32, math.log(DT_MIN), math.log(DT_MAX)),
        's5_b_re': nrm(ks[11], (DEPTH, 2, G, P, H), (2 * H) ** -0.5),
        's5_b_im': nrm(ks[12], (DEPTH, 2, G, P, H), (2 * H) ** -0.5),
        's5_c_re': nrm(ks[13], (DEPTH, 2, G, H, P), P ** -0.5),
        's5_c_im': nrm(ks[14], (DEPTH, 2, G, H, P), P ** -0.5),
        's5_d': nrm(ks[15], (DEPTH, G, H), 1.0),
        's5_w_glu': nrm(ks[16], (DEPTH, S5_WIDTH, S5_WIDTH), S5_WIDTH ** -0.5),
        'conv_w': nrm(ks[17], (DEPTH, CONV_K, CONV_WIDTH), CONV_K ** -0.5),
        'conv_b': nrm(ks[18], (DEPTH, CONV_WIDTH), 0.01),
        'conv_ln_g': 1.0 + nrm(ks[19], (DEPTH, CONV_WIDTH), 0.02),
        'conv_ln_b': nrm(ks[20], (DEPTH, CONV_WIDTH), 0.01),
        'w_out': nrm(ks[21], (DEPTH, MIX_WIDTH, D_MODEL), MIX_WIDTH ** -0.5),
        'norm2_g': 1.0 + nrm(ks[22], (DEPTH, D_MODEL), 0.02),
        'mlp_w1': nrm(ks[23], (DEPTH, D_MODEL, D_FF), D_MODEL ** -0.5),
        'mlp_w2': nrm(ks[24], (DEPTH, D_FF, D_MODEL), D_FF ** -0.5),
        'final_g': 1.0 + nrm(ks[25], (D_MODEL,), 0.02),
    }


def _fwd_reference(x, c, ctx, c_ctx, ada_w, ada_b, norm1_g, w_in, s5_lam_re, s5_lam_im, s5_log_dt,
              s5_b_re, s5_b_im, s5_c_re, s5_c_im, s5_d, s5_w_glu, conv_w, conv_b, conv_ln_g,
              conv_ln_b, w_out, norm2_g, mlp_w1, mlp_w2, final_g):
    h = x
    hc = ctx
    for layer in range(DEPTH):
        need_ctx = layer < DEPTH - 1
        mod = jax.nn.silu(c) @ ada_w[layer] + ada_b[layer]
        sh1, sc1, g1, sh2, sc2, g2 = jnp.split(mod[:, None, :], 6, axis=-1)
        n_cmod = 6 if need_ctx else 2
        modc = jax.nn.silu(c_ctx) @ ada_w[layer][:, :n_cmod * D_MODEL] + ada_b[layer][:n_cmod * D_MODEL]
        cmods = jnp.split(modc, n_cmod)

        a = _modulate(_rms_norm(h, norm1_g[layer]), sh1, sc1)
        ac = _modulate(_rms_norm(hc, norm1_g[layer]), cmods[1 - 1], cmods[1])
        z = a @ w_in[layer]
        zc = ac @ (w_in[layer] if need_ctx else w_in[layer][:, :S5_WIDTH])
        y_s5, yc_s5 = _s5_mixer(z[..., :S5_WIDTH], zc[..., :S5_WIDTH],
                                s5_lam_re[layer], s5_lam_im[layer], s5_log_dt[layer],
                                s5_b_re[layer], s5_b_im[layer], s5_c_re[layer], s5_c_im[layer],
                                s5_d[layer], s5_w_glu[layer], need_ctx)
        y_conv = _conv_module(z[..., S5_WIDTH:S5_WIDTH + CONV_WIDTH], z[..., S5_WIDTH + CONV_WIDTH:],
                              conv_w[layer], conv_b[layer], conv_ln_g[layer], conv_ln_b[layer], True)
        h = h + g1 * (jnp.concatenate([y_s5, y_conv], axis=-1) @ w_out[layer])

        h = h + g2 * _sq_relu_mlp(_modulate(_rms_norm(h, norm2_g[layer]), sh2, sc2),
                                  mlp_w1[layer], mlp_w2[layer])

        if need_ctx:
            csh1, csc1, cg1, csh2, csc2, cg2 = cmods
            yc_conv = _conv_module(zc[..., S5_WIDTH:S5_WIDTH + CONV_WIDTH], zc[..., S5_WIDTH + CONV_WIDTH:],
                                   conv_w[layer], conv_b[layer], conv_ln_g[layer], conv_ln_b[layer], False)
            hc = hc + cg1 * (jnp.concatenate([yc_s5, yc_conv], axis=-1) @ w_out[layer])
            hc = hc + cg2 * _sq_relu_mlp(_modulate(_rms_norm(hc, norm2_g[layer]), csh2, csc2),
                                         mlp_w1[layer], mlp_w2[layer])
    return _rms_norm(h, final_g)


import jax as _jax
import jax.numpy as _jnp

TWIN_FORMAT = 'train_step'
FWD_PARAMS = ['x', 'c', 'ctx', 'c_ctx', 'ada_w', 'ada_b', 'norm1_g', 'w_in', 's5_lam_re', 's5_lam_im', 's5_log_dt', 's5_b_re', 's5_b_im', 's5_c_re', 's5_c_im', 's5_d', 's5_w_glu', 'conv_w', 'conv_b', 'conv_ln_g', 'conv_ln_b', 'w_out', 'norm2_g', 'mlp_w1', 'mlp_w2', 'final_g']
TWIN_WEIGHTS = ['c_ctx', 'ada_w', 'ada_b', 'norm1_g', 'w_in', 's5_lam_re', 's5_lam_im', 's5_log_dt', 's5_b_re', 's5_b_im', 's5_c_re', 's5_c_im', 's5_d', 's5_w_glu', 'conv_w', 'conv_b', 'conv_ln_g', 'conv_ln_b', 'w_out', 'norm2_g', 'mlp_w1', 'mlp_w2', 'final_g']
TWIN_DIFF_INPUT = 'x'
TWIN_INPUTS = ['x', 'c', 'ctx', 'c_ctx', 'ada_w', 'ada_b', 'norm1_g', 'w_in', 's5_lam_re', 's5_lam_im', 's5_log_dt', 's5_b_re', 's5_b_im', 's5_c_re', 's5_c_im', 's5_d', 's5_w_glu', 'conv_w', 'conv_b', 'conv_ln_g', 'conv_ln_b', 'w_out', 'norm2_g', 'mlp_w1', 'mlp_w2', 'final_g', 'loss_target', 'm_c_ctx', 'm_ada_w', 'm_ada_b', 'm_norm1_g', 'm_w_in', 'm_s5_lam_re', 'm_s5_lam_im', 'm_s5_log_dt', 'm_s5_b_re', 'm_s5_b_im', 'm_s5_c_re', 'm_s5_c_im', 'm_s5_d', 'm_s5_w_glu', 'm_conv_w', 'm_conv_b', 'm_conv_ln_g', 'm_conv_ln_b', 'm_w_out', 'm_norm2_g', 'm_mlp_w1', 'm_mlp_w2', 'm_final_g', 'v_c_ctx', 'v_ada_w', 'v_ada_b', 'v_norm1_g', 'v_w_in', 'v_s5_lam_re', 'v_s5_lam_im', 'v_s5_log_dt', 'v_s5_b_re', 'v_s5_b_im', 'v_s5_c_re', 'v_s5_c_im', 'v_s5_d', 'v_s5_w_glu', 'v_conv_w', 'v_conv_b', 'v_conv_ln_g', 'v_conv_ln_b', 'v_w_out', 'v_norm2_g', 'v_mlp_w1', 'v_mlp_w2', 'v_final_g']
TWIN_OUTPUTS = ['loss', 'grad_x', 'grad_c_ctx', 'grad_ada_w', 'grad_ada_b', 'grad_norm1_g', 'grad_w_in', 'grad_s5_lam_re', 'grad_s5_lam_im', 'grad_s5_log_dt', 'grad_s5_b_re', 'grad_s5_b_im', 'grad_s5_c_re', 'grad_s5_c_im', 'grad_s5_d', 'grad_s5_w_glu', 'grad_conv_w', 'grad_conv_b', 'grad_conv_ln_g', 'grad_conv_ln_b', 'grad_w_out', 'grad_norm2_g', 'grad_mlp_w1', 'grad_mlp_w2', 'grad_final_g', 'delta_c_ctx', 'delta_ada_w', 'delta_ada_b', 'delta_norm1_g', 'delta_w_in', 'delta_s5_lam_re', 'delta_s5_lam_im', 'delta_s5_log_dt', 'delta_s5_b_re', 'delta_s5_b_im', 'delta_s5_c_re', 'delta_s5_c_im', 'delta_s5_d', 'delta_s5_w_glu', 'delta_conv_w', 'delta_conv_b', 'delta_conv_ln_g', 'delta_conv_ln_b', 'delta_w_out', 'delta_norm2_g', 'delta_mlp_w1', 'delta_mlp_w2', 'delta_final_g', 'new_m_c_ctx', 'new_m_ada_w', 'new_m_ada_b', 'new_m_norm1_g', 'new_m_w_in', 'new_m_s5_lam_re', 'new_m_s5_lam_im', 'new_m_s5_log_dt', 'new_m_s5_b_re', 'new_m_s5_b_im', 'new_m_s5_c_re', 'new_m_s5_c_im', 'new_m_s5_d', 'new_m_s5_w_glu', 'new_m_conv_w', 'new_m_conv_b', 'new_m_conv_ln_g', 'new_m_conv_ln_b', 'new_m_w_out', 'new_m_norm2_g', 'new_m_mlp_w1', 'new_m_mlp_w2', 'new_m_final_g', 'new_v_c_ctx', 'new_v_ada_w', 'new_v_ada_b', 'new_v_norm1_g', 'new_v_w_in', 'new_v_s5_lam_re', 'new_v_s5_lam_im', 'new_v_s5_log_dt', 'new_v_s5_b_re', 'new_v_s5_b_im', 'new_v_s5_c_re', 'new_v_s5_c_im', 'new_v_s5_d', 'new_v_s5_w_glu', 'new_v_conv_w', 'new_v_conv_b', 'new_v_conv_ln_g', 'new_v_conv_ln_b', 'new_v_w_out', 'new_v_norm2_g', 'new_v_mlp_w1', 'new_v_mlp_w2', 'new_v_final_g']
TWIN_LEAF_KINDS = {'loss': 'loss', 'grad_x': 'grad_x', 'grad_c_ctx': 'grad_w', 'grad_ada_w': 'grad_w', 'grad_ada_b': 'grad_w', 'grad_norm1_g': 'grad_w', 'grad_w_in': 'grad_w', 'grad_s5_lam_re': 'grad_w', 'grad_s5_lam_im': 'grad_w', 'grad_s5_log_dt': 'grad_w', 'grad_s5_b_re': 'grad_w', 'grad_s5_b_im': 'grad_w', 'grad_s5_c_re': 'grad_w', 'grad_s5_c_im': 'grad_w', 'grad_s5_d': 'grad_w', 'grad_s5_w_glu': 'grad_w', 'grad_conv_w': 'grad_w', 'grad_conv_b': 'grad_w', 'grad_conv_ln_g': 'grad_w', 'grad_conv_ln_b': 'grad_w', 'grad_w_out': 'grad_w', 'grad_norm2_g': 'grad_w', 'grad_mlp_w1': 'grad_w', 'grad_mlp_w2': 'grad_w', 'grad_final_g': 'grad_w', 'delta_c_ctx': 'delta_w', 'delta_ada_w': 'delta_w', 'delta_ada_b': 'delta_w', 'delta_norm1_g': 'delta_w', 'delta_w_in': 'delta_w', 'delta_s5_lam_re': 'delta_w', 'delta_s5_lam_im': 'delta_w', 'delta_s5_log_dt': 'delta_w', 'delta_s5_b_re': 'delta_w', 'delta_s5_b_im': 'delta_w', 'delta_s5_c_re': 'delta_w', 'delta_s5_c_im': 'delta_w', 'delta_s5_d': 'delta_w', 'delta_s5_w_glu': 'delta_w', 'delta_conv_w': 'delta_w', 'delta_conv_b': 'delta_w', 'delta_conv_ln_g': 'delta_w', 'delta_conv_ln_b': 'delta_w', 'delta_w_out': 'delta_w', 'delta_norm2_g': 'delta_w', 'delta_mlp_w1': 'delta_w', 'delta_mlp_w2': 'delta_w', 'delta_final_g': 'delta_w', 'new_m_c_ctx': 'new_m', 'new_m_ada_w': 'new_m', 'new_m_ada_b': 'new_m', 'new_m_norm1_g': 'new_m', 'new_m_w_in': 'new_m', 'new_m_s5_lam_re': 'new_m', 'new_m_s5_lam_im': 'new_m', 'new_m_s5_log_dt': 'new_m', 'new_m_s5_b_re': 'new_m', 'new_m_s5_b_im': 'new_m', 'new_m_s5_c_re': 'new_m', 'new_m_s5_c_im': 'new_m', 'new_m_s5_d': 'new_m', 'new_m_s5_w_glu': 'new_m', 'new_m_conv_w': 'new_m', 'new_m_conv_b': 'new_m', 'new_m_conv_ln_g': 'new_m', 'new_m_conv_ln_b': 'new_m', 'new_m_w_out': 'new_m', 'new_m_norm2_g': 'new_m', 'new_m_mlp_w1': 'new_m', 'new_m_mlp_w2': 'new_m', 'new_m_final_g': 'new_m', 'new_v_c_ctx': 'new_v', 'new_v_ada_w': 'new_v', 'new_v_ada_b': 'new_v', 'new_v_norm1_g': 'new_v', 'new_v_w_in': 'new_v', 'new_v_s5_lam_re': 'new_v', 'new_v_s5_lam_im': 'new_v', 'new_v_s5_log_dt': 'new_v', 'new_v_s5_b_re': 'new_v', 'new_v_s5_b_im': 'new_v', 'new_v_s5_c_re': 'new_v', 'new_v_s5_c_im': 'new_v', 'new_v_s5_d': 'new_v', 'new_v_s5_w_glu': 'new_v', 'new_v_conv_w': 'new_v', 'new_v_conv_b': 'new_v', 'new_v_conv_ln_g': 'new_v', 'new_v_conv_ln_b': 'new_v', 'new_v_w_out': 'new_v', 'new_v_norm2_g': 'new_v', 'new_v_mlp_w1': 'new_v', 'new_v_mlp_w2': 'new_v', 'new_v_final_g': 'new_v'}


def _forward(args):
    return _fwd_reference(*[args[k] for k in FWD_PARAMS])


def _output_shape():
    out = _jax.eval_shape(lambda: _forward(_fwd_setup_inputs(0)))
    return out.shape, out.dtype

N_MICROBATCH = 1
ADAM_LR = 0.001
ADAM_B1 = 0.9
ADAM_B2 = 0.999
ADAM_EPS = 1e-08
ADAM_WD = 0.01
ADAM_STEP = 10
PER_EXAMPLE_BATCH_AXIS = {'x': 0, 'c': 0, 'ctx': 0, 'loss_target': 0}
SHARED_INPUTS = []
_WEIGHT_DTYPES = {'c_ctx': _jnp.float32, 'ada_w': _jnp.float32, 'ada_b': _jnp.float32, 'norm1_g': _jnp.float32, 'w_in': _jnp.float32, 's5_lam_re': _jnp.float32, 's5_lam_im': _jnp.float32, 's5_log_dt': _jnp.float32, 's5_b_re': _jnp.float32, 's5_b_im': _jnp.float32, 's5_c_re': _jnp.float32, 's5_c_im': _jnp.float32, 's5_d': _jnp.float32, 's5_w_glu': _jnp.float32, 'conv_w': _jnp.float32, 'conv_b': _jnp.float32, 'conv_ln_g': _jnp.float32, 'conv_ln_b': _jnp.float32, 'w_out': _jnp.float32, 'norm2_g': _jnp.float32, 'mlp_w1': _jnp.float32, 'mlp_w2': _jnp.float32, 'final_g': _jnp.float32}
MOMENT_SCALE = {'c_ctx': 5.923547e-04, 'ada_w': 8.776105e-02, 'ada_b': 1.516055e-01, 'norm1_g': 3.067260e-02, 'w_in': 2.604929e-02, 's5_lam_re': 1.609068e-03, 's5_lam_im': 1.816366e-03, 's5_log_dt': 9.428715e-01, 's5_b_re': 1.157694e-03, 's5_b_im': 1.176160e-03, 's5_c_re': 1.671467e-03, 's5_c_im': 1.773819e-03, 's5_d': 2.481871e-02, 's5_w_glu': 7.384469e-03, 'conv_w': 3.485201e-02, 'conv_b': 1.550835e-01, 'conv_ln_g': 4.566062e-02, 'conv_ln_b': 3.389885e-02, 'w_out': 2.878734e-02, 'norm2_g': 9.020949e-02, 'mlp_w1': 4.241371e-02, 'mlp_w2': 7.731275e-02, 'final_g': 3.224433e+01}


def _to_microbatches(a, axis):
    t = _jnp.moveaxis(a, axis, 0)
    t = t.reshape((N_MICROBATCH, t.shape[0] // N_MICROBATCH) + t.shape[1:])
    return _jnp.moveaxis(t, 1, axis + 1)


def setup_inputs(seed: int = 0) -> dict:
    inp = _fwd_setup_inputs(seed)
    key = _jax.random.fold_in(_jax.random.key(seed), 7919)
    shape, _ = _output_shape()
    out = dict(inp)
    out["loss_target"] = _jax.random.normal(_jax.random.fold_in(key, 0), shape, _jnp.float32)
    for i, name in enumerate(TWIN_WEIGHTS):
        w = inp[name].astype(_jnp.float32)
        if MOMENT_SCALE is None:
            s = _jnp.sqrt(_jnp.mean(_jnp.square(w)) + 1e-30)
        else:
            s = MOMENT_SCALE[name]
        km, kv = _jax.random.split(_jax.random.fold_in(key, i + 1))
        out[name] = w
        out["m_" + name] = s * _jax.random.normal(km, w.shape, _jnp.float32)
        out["v_" + name] = (s * s) * _jax.random.uniform(kv, w.shape, _jnp.float32, 0.5, 1.5)
    if N_MICROBATCH > 1:
        for name, axis in PER_EXAMPLE_BATCH_AXIS.items():
            out[name] = _to_microbatches(out[name], axis)
    return {'x': out['x'], 'c': out['c'], 'ctx': out['ctx'], 'c_ctx': out['c_ctx'], 'ada_w': out['ada_w'], 'ada_b': out['ada_b'], 'norm1_g': out['norm1_g'], 'w_in': out['w_in'], 's5_lam_re': out['s5_lam_re'], 's5_lam_im': out['s5_lam_im'], 's5_log_dt': out['s5_log_dt'], 's5_b_re': out['s5_b_re'], 's5_b_im': out['s5_b_im'], 's5_c_re': out['s5_c_re'], 's5_c_im': out['s5_c_im'], 's5_d': out['s5_d'], 's5_w_glu': out['s5_w_glu'], 'conv_w': out['conv_w'], 'conv_b': out['conv_b'], 'conv_ln_g': out['conv_ln_g'], 'conv_ln_b': out['conv_ln_b'], 'w_out': out['w_out'], 'norm2_g': out['norm2_g'], 'mlp_w1': out['mlp_w1'], 'mlp_w2': out['mlp_w2'], 'final_g': out['final_g'], 'loss_target': out['loss_target'], 'm_c_ctx': out['m_c_ctx'], 'm_ada_w': out['m_ada_w'], 'm_ada_b': out['m_ada_b'], 'm_norm1_g': out['m_norm1_g'], 'm_w_in': out['m_w_in'], 'm_s5_lam_re': out['m_s5_lam_re'], 'm_s5_lam_im': out['m_s5_lam_im'], 'm_s5_log_dt': out['m_s5_log_dt'], 'm_s5_b_re': out['m_s5_b_re'], 'm_s5_b_im': out['m_s5_b_im'], 'm_s5_c_re': out['m_s5_c_re'], 'm_s5_c_im': out['m_s5_c_im'], 'm_s5_d': out['m_s5_d'], 'm_s5_w_glu': out['m_s5_w_glu'], 'm_conv_w': out['m_conv_w'], 'm_conv_b': out['m_conv_b'], 'm_conv_ln_g': out['m_conv_ln_g'], 'm_conv_ln_b': out['m_conv_ln_b'], 'm_w_out': out['m_w_out'], 'm_norm2_g': out['m_norm2_g'], 'm_mlp_w1': out['m_mlp_w1'], 'm_mlp_w2': out['m_mlp_w2'], 'm_final_g': out['m_final_g'], 'v_c_ctx': out['v_c_ctx'], 'v_ada_w': out['v_ada_w'], 'v_ada_b': out['v_ada_b'], 'v_norm1_g': out['v_norm1_g'], 'v_w_in': out['v_w_in'], 'v_s5_lam_re': out['v_s5_lam_re'], 'v_s5_lam_im': out['v_s5_lam_im'], 'v_s5_log_dt': out['v_s5_log_dt'], 'v_s5_b_re': out['v_s5_b_re'], 'v_s5_b_im': out['v_s5_b_im'], 'v_s5_c_re': out['v_s5_c_re'], 'v_s5_c_im': out['v_s5_c_im'], 'v_s5_d': out['v_s5_d'], 'v_s5_w_glu': out['v_s5_w_glu'], 'v_conv_w': out['v_conv_w'], 'v_conv_b': out['v_conv_b'], 'v_conv_ln_g': out['v_conv_ln_g'], 'v_conv_ln_b': out['v_conv_ln_b'], 'v_w_out': out['v_w_out'], 'v_norm2_g': out['v_norm2_g'], 'v_mlp_w1': out['v_mlp_w1'], 'v_mlp_w2': out['v_mlp_w2'], 'v_final_g': out['v_final_g']}


def _loss(weights, diff, rest, loss_target):
    with _jax.named_scope("forward"):
        args = {**rest, TWIN_DIFF_INPUT: diff, **{k: w.astype(_WEIGHT_DTYPES[k]) for k, w in weights.items()}}
        y = _forward(args)
    with _jax.named_scope("loss_head"):
        err = _jnp.square(y.astype(_jnp.float32) - loss_target)
        return 0.5 * _jnp.sum(_jnp.mean(err, axis=-1)) if err.ndim else 0.5 * err


def _adamw(w, g, m, v):
    m = ADAM_B1 * m + (1.0 - ADAM_B1) * g
    v = ADAM_B2 * v + (1.0 - ADAM_B2) * _jnp.square(g)
    m_hat = m / (1.0 - ADAM_B1 ** ADAM_STEP)
    v_hat = v / (1.0 - ADAM_B2 ** ADAM_STEP)
    delta = -ADAM_LR * (m_hat / (_jnp.sqrt(v_hat) + ADAM_EPS) + ADAM_WD * w)
    return delta, m, v


def reference(x, c, ctx, c_ctx, ada_w, ada_b, norm1_g, w_in, s5_lam_re, s5_lam_im, s5_log_dt, s5_b_re, s5_b_im, s5_c_re, s5_c_im, s5_d, s5_w_glu, conv_w, conv_b, conv_ln_g, conv_ln_b, w_out, norm2_g, mlp_w1, mlp_w2, final_g, loss_target, m_c_ctx, m_ada_w, m_ada_b, m_norm1_g, m_w_in, m_s5_lam_re, m_s5_lam_im, m_s5_log_dt, m_s5_b_re, m_s5_b_im, m_s5_c_re, m_s5_c_im, m_s5_d, m_s5_w_glu, m_conv_w, m_conv_b, m_conv_ln_g, m_conv_ln_b, m_w_out, m_norm2_g, m_mlp_w1, m_mlp_w2, m_final_g, v_c_ctx, v_ada_w, v_ada_b, v_norm1_g, v_w_in, v_s5_lam_re, v_s5_lam_im, v_s5_log_dt, v_s5_b_re, v_s5_b_im, v_s5_c_re, v_s5_c_im, v_s5_d, v_s5_w_glu, v_conv_w, v_conv_b, v_conv_ln_g, v_conv_ln_b, v_w_out, v_norm2_g, v_mlp_w1, v_mlp_w2, v_final_g):
    given = dict(x=x, c=c, ctx=ctx, c_ctx=c_ctx, ada_w=ada_w, ada_b=ada_b, norm1_g=norm1_g, w_in=w_in, s5_lam_re=s5_lam_re, s5_lam_im=s5_lam_im, s5_log_dt=s5_log_dt, s5_b_re=s5_b_re, s5_b_im=s5_b_im, s5_c_re=s5_c_re, s5_c_im=s5_c_im, s5_d=s5_d, s5_w_glu=s5_w_glu, conv_w=conv_w, conv_b=conv_b, conv_ln_g=conv_ln_g, conv_ln_b=conv_ln_b, w_out=w_out, norm2_g=norm2_g, mlp_w1=mlp_w1, mlp_w2=mlp_w2, final_g=final_g, loss_target=loss_target, m_c_ctx=m_c_ctx, m_ada_w=m_ada_w, m_ada_b=m_ada_b, m_norm1_g=m_norm1_g, m_w_in=m_w_in, m_s5_lam_re=m_s5_lam_re, m_s5_lam_im=m_s5_lam_im, m_s5_log_dt=m_s5_log_dt, m_s5_b_re=m_s5_b_re, m_s5_b_im=m_s5_b_im, m_s5_c_re=m_s5_c_re, m_s5_c_im=m_s5_c_im, m_s5_d=m_s5_d, m_s5_w_glu=m_s5_w_glu, m_conv_w=m_conv_w, m_conv_b=m_conv_b, m_conv_ln_g=m_conv_ln_g, m_conv_ln_b=m_conv_ln_b, m_w_out=m_w_out, m_norm2_g=m_norm2_g, m_mlp_w1=m_mlp_w1, m_mlp_w2=m_mlp_w2, m_final_g=m_final_g, v_c_ctx=v_c_ctx, v_ada_w=v_ada_w, v_ada_b=v_ada_b, v_norm1_g=v_norm1_g, v_w_in=v_w_in, v_s5_lam_re=v_s5_lam_re, v_s5_lam_im=v_s5_lam_im, v_s5_log_dt=v_s5_log_dt, v_s5_b_re=v_s5_b_re, v_s5_b_im=v_s5_b_im, v_s5_c_re=v_s5_c_re, v_s5_c_im=v_s5_c_im, v_s5_d=v_s5_d, v_s5_w_glu=v_s5_w_glu, v_conv_w=v_conv_w, v_conv_b=v_conv_b, v_conv_ln_g=v_conv_ln_g, v_conv_ln_b=v_conv_ln_b, v_w_out=v_w_out, v_norm2_g=v_norm2_g, v_mlp_w1=v_mlp_w1, v_mlp_w2=v_mlp_w2, v_final_g=v_final_g)
    weights = {n: given[n] for n in TWIN_WEIGHTS}
    shared = {n: given[n] for n in SHARED_INPUTS}
    per_example = {n: given[n] for n in ['x', 'c', 'ctx']}
    grad_fn = _jax.value_and_grad(_loss, argnums=(0, 1))

    def one_microbatch(ex, loss_target):
        ex = dict(ex)
        diff = ex.pop(TWIN_DIFF_INPUT)
        return grad_fn(weights, diff, {**shared, **ex}, loss_target)

    if N_MICROBATCH == 1:
        loss, (grad_w, grad_x) = one_microbatch(per_example, given["loss_target"])
    else:
        def body(carry, xs):
            loss_sum, grad_sum = carry
            l_k, (gw_k, gx_k) = one_microbatch(xs[0], xs[1])
            with _jax.named_scope("update"):
                return (loss_sum + l_k, _jax.tree.map(_jnp.add, grad_sum, gw_k)), gx_k

        init = (_jnp.zeros((), _jnp.float32), _jax.tree.map(_jnp.zeros_like, weights))
        (loss, grad_w), grad_x = _jax.lax.scan(body, init, (per_example, given["loss_target"]))
    with _jax.named_scope("update"):
        delta_w, new_m, new_v = {}, {}, {}
        for n in TWIN_WEIGHTS:
            delta_w[n], new_m[n], new_v[n] = _adamw(weights[n], grad_w[n], given["m_" + n], given["v_" + n])
    return (loss, grad_x, *[grad_w[n] for n in TWIN_WEIGHTS], *[delta_w[n] for n in TWIN_WEIGHTS],
            *[new_m[n] for n in TWIN_WEIGHTS], *[new_v[n] for n in TWIN_WEIGHTS])
```

```python
import functools

import jax
import jax.numpy as jnp
from jax import lax
from jax.experimental import pallas as pl
from jax.experimental.pallas import tpu as pltpu

F32 = jnp.float32
BF16 = jnp.bfloat16
MESH = pl.DeviceIdType.MESH
ANY = pl.BlockSpec(memory_space=pl.ANY)

NDEV = 8
D_MODEL = 1024
GRID_W = 64
S5_WIDTH = 512
S5_GROUP = 16
S5_GROUPS = 32
S5_STATE = 64
NSTATE = S5_GROUPS * S5_STATE
CONV_WIDTH = 512
CONV_K = 31
IN_COLS = S5_WIDTH + 2 * CONV_WIDTH
D_FF = 4 * D_MODEL
EPS_RMS = 1e-6
EPS_LN = 1e-5
ADAM_LR = 0.001
ADAM_B1 = 0.9
ADAM_B2 = 0.999
ADAM_EPS = 1e-08
ADAM_WD = 0.01
ADAM_STEP = 10

SUBLANES = 8
LANES = 128
ROW_BLOCK = 256
SCAN_LANES = 256
CONV_ROWS = 64
VMEM_LIMIT = 48 * 1024 * 1024
SMALL_ROWS = 320


def _params(sem=None):
    kw = dict(vmem_limit_bytes=VMEM_LIMIT)
    if sem is not None:
        kw["dimension_semantics"] = sem
    return pltpu.CompilerParams(**kw)


def _sds(shape, dtype=F32):
    return jax.ShapeDtypeStruct(tuple(shape), dtype)


def _fold8(x):
    return x.reshape(x.shape[0] // SUBLANES, SUBLANES, x.shape[1]).sum(axis=0)


def _sigmoid(x):
    return 1.0 / (1.0 + jnp.exp(-x))


def _silu(x):
    return x * _sigmoid(x)


def _dsilu(x):
    s = _sigmoid(x)
    return s * (1.0 + x * (1.0 - s))


_GELU_C = 0.7978845608028654


def _gelu(x):
    return 0.5 * x * (1.0 + jnp.tanh(_GELU_C * (x + 0.044715 * x * x * x)))


def _dgelu(x):
    t = jnp.tanh(_GELU_C * (x + 0.044715 * x * x * x))
    return 0.5 * (1.0 + t) + 0.5 * x * (1.0 - t * t) * _GELU_C * (1.0 + 3.0 * 0.044715 * x * x)


def _dot(a, b, mode):
    dims = {"nn": (((1,), (0,)), ((), ())), "nt": (((1,), (1,)), ((), ())), "tn": (((0,), (0,)), ((), ()))}[mode]
    return lax.dot_general(a, b, dims, preferred_element_type=F32)


def _exchange(name, srcs, gather):
    n = len(srcs)
    outs = [_sds(((NDEV,) + s.shape) if g else s.shape, s.dtype) for s, g in zip(srcs, gather)]

    def body(*refs):
        src, dst = refs[:n], refs[n:2 * n]
        send_sems, recv_sems, local_sems = refs[2 * n:]
        x, y, c = lax.axis_index("x"), lax.axis_index("y"), lax.axis_index("c")
        me = 4 * x + 2 * y + c
        started = []
        for a in range(n):
            def chunk(dest, a=a):
                return src[a] if gather[a] else src[a].at[dest]

            local = pltpu.make_async_copy(chunk(me), dst[a].at[me], local_sems.at[a])
            local.start()
            for k in range(1, NDEV):
                px = 1 - x if k & 4 else x
                py = 1 - y if k & 2 else y
                pc = 1 - c if k & 1 else c
                peer = 4 * px + 2 * py + pc
                copy = pltpu.make_async_remote_copy(
                    src_ref=chunk(peer), dst_ref=dst[a].at[me],
                    send_sem=send_sems.at[a * (NDEV - 1) + k - 1], recv_sem=recv_sems.at[a * (NDEV - 1) + k - 1],
                    device_id=(px, py, pc), device_id_type=MESH)
                copy.start()
                landing = pltpu.make_async_remote_copy(
                    src_ref=chunk(peer), dst_ref=dst[a].at[peer],
                    send_sem=send_sems.at[a * (NDEV - 1) + k - 1], recv_sem=recv_sems.at[a * (NDEV - 1) + k - 1],
                    device_id=(px, py, pc), device_id_type=MESH)
                started.append((copy, landing))
            started.append((local, None))
        for copy, landing in started:
            if landing is None:
                copy.wait()
            else:
                copy.wait_send()
                landing.wait_recv()

    return pl.pallas_call(
        body, name=name, out_shape=outs, in_specs=[ANY] * n, out_specs=[ANY] * n,
        scratch_shapes=[pltpu.SemaphoreType.DMA((n * (NDEV - 1),)), pltpu.SemaphoreType.DMA((n * (NDEV - 1),)),
                        pltpu.SemaphoreType.DMA((n,))],
    )(*srcs)


def _matmul(name, a, b, mode, mnk, tiles, outs, a_spec=None, b_spec=None, a_fn=None, a_extra=(),
            epi=None, epi_extra=(), out_specs=None):
    m_, n_, k_ = mnk
    tm, tn, tk = tiles
    nk = k_ // tk
    if a_spec is None:
        a_spec = (pl.BlockSpec((tk, tm), lambda i, j, k: (k, i)) if mode == "tn"
                  else pl.BlockSpec((tm, tk), lambda i, j, k: (i, k)))
    if b_spec is None:
        b_spec = (pl.BlockSpec((tn, tk), lambda i, j, k: (j, k)) if mode == "nt"
                  else pl.BlockSpec((tk, tn), lambda i, j, k: (k, j)))
    if out_specs is None:
        out_specs = [pl.BlockSpec((tm, tn), lambda i, j, k: (i, j)) for _ in outs]
    na, ne, no = len(a_extra), len(epi_extra), len(outs)

    def body(*refs):
        a_ref, b_ref = refs[0], refs[1]
        ax = refs[2:2 + na]
        ex = refs[2 + na:2 + na + ne]
        o = refs[2 + na + ne:2 + na + ne + no]
        acc = refs[-1]
        k = pl.program_id(2)

        @pl.when(k == 0)
        def _():
            acc[...] = jnp.zeros_like(acc)

        at = a_ref[...]
        if a_fn is not None:
            at = a_fn(at, *[r[...] for r in ax])
        acc[...] += _dot(at.astype(BF16), b_ref[...].astype(BF16), mode)

        @pl.when(k == nk - 1)
        def _():
            res = acc[...]
            res = epi(res, *[r[...] for r in ex]) if epi is not None else (res,)
            for ref, val in zip(o, res):
                ref[...] = val.astype(ref.dtype)

    return pl.pallas_call(
        body, name=name, grid=(m_ // tm, n_ // tn, nk),
        in_specs=[a_spec, b_spec] + [s for _, s in a_extra] + [s for _, s in epi_extra],
        out_specs=out_specs, out_shape=[_sds(s, d) for s, d in outs],
        scratch_shapes=[pltpu.VMEM((tm, tn), F32)],
        compiler_params=_params(("parallel", "parallel", "arbitrary")),
    )(a, b, *[x for x, _ in a_extra], *[x for x, _ in epi_extra])


def _prenorm(name, x, ctx, gain, shsc):
    n_lat = x.shape[0] // ROW_BLOCK
    n_ctx = 0 if ctx is None else ctx.shape[0] // ROW_BLOCK
    d = x.shape[1]

    def norm(src, g_ref, m_ref, o_ref):
        xv = src[...]
        xh = xv * lax.rsqrt(jnp.mean(xv * xv, axis=-1, keepdims=True) + EPS_RMS)
        o_ref[...] = ((xh * g_ref[...]) * (1.0 + m_ref[1:2, :]) + m_ref[0:1, :]).astype(o_ref.dtype)

    def body(*refs):
        if ctx is None:
            x_ref, g_ref, m_ref, o_ref = refs
            norm(x_ref, g_ref, m_ref, o_ref)
        else:
            x_ref, c_ref, g_ref, m_ref, o_ref = refs
            i = pl.program_id(0)

            @pl.when(i < n_lat)
            def _():
                norm(x_ref, g_ref, m_ref, o_ref)

            @pl.when(i >= n_lat)
            def _():
                norm(c_ref, g_ref, m_ref, o_ref)

    in_specs = [pl.BlockSpec((ROW_BLOCK, d), lambda i: (jnp.minimum(i, n_lat - 1), 0))]
    args = [x]
    if ctx is not None:
        in_specs.append(pl.BlockSpec((ROW_BLOCK, d), lambda i: (jnp.maximum(i - n_lat, 0), 0)))
        args.append(ctx)
    in_specs += [pl.BlockSpec((1, d), lambda i: (0, 0)),
                 pl.BlockSpec((None, 2, d), lambda i: (jnp.minimum(i // n_lat, 1), 0, 0))]
    args += [gain, shsc]
    return pl.pallas_call(
        body, name=name, grid=(n_lat + n_ctx,), in_specs=in_specs,
        out_specs=pl.BlockSpec((ROW_BLOCK, d), lambda i: (i, 0)),
        out_shape=_sds(((n_lat + n_ctx) * ROW_BLOCK, d), BF16),
        compiler_params=_params(("parallel",)),
    )(*args)


def _norm_bwd(name, x, d_act, d_act_row0, gain, scale, res=None, aux=None):
    rows, d = x.shape
    nb = rows // ROW_BLOCK
    has_res = res is not None

    def body(*refs):
        if has_res:
            x_ref, da_ref, g_ref, sc_ref, r_ref, aux_ref, dx_ref, sums = refs
        else:
            x_ref, da_ref, g_ref, sc_ref, sums = refs
        i = pl.program_id(0)

        @pl.when(i == 0)
        def _():
            sums[...] = jnp.zeros_like(sums)

        xv, da = x_ref[...], da_ref[...]
        rstd = lax.rsqrt(jnp.mean(xv * xv, axis=-1, keepdims=True) + EPS_RMS)
        xh = xv * rstd
        g = g_ref[...]
        dn = da * (1.0 + sc_ref[...])
        sums[0] += _fold8(da)
        sums[1] += _fold8(da * (xh * g))
        sums[2] += _fold8(dn * xh)
        if has_res:
            dxh = dn * g
            dx = rstd * (dxh - xh * jnp.mean(dxh * xh, axis=-1, keepdims=True))
            rv = r_ref[...]
            dx_ref[...] = rv + dx
            sums[3] += _fold8(rv * aux_ref[...])

    row = lambda i: (i, 0)
    vec = pl.BlockSpec((1, d), lambda i: (0, 0))
    in_specs = [pl.BlockSpec((ROW_BLOCK, d), row), pl.BlockSpec((ROW_BLOCK, d), lambda i: (i + d_act_row0, 0)), vec, vec]
    args = [x, d_act, gain, scale]
    out_shape = [_sds((4, SUBLANES, d))]
    out_specs = [pl.BlockSpec((4, SUBLANES, d), lambda i: (0, 0, 0))]
    if has_res:
        in_specs += [pl.BlockSpec((ROW_BLOCK, d), row), pl.BlockSpec((ROW_BLOCK, d), row)]
        args += [res, aux]
        out_shape = [_sds((rows, d))] + out_shape
        out_specs = [pl.BlockSpec((ROW_BLOCK, d), row)] + out_specs
    out = pl.pallas_call(
        body, name=name, grid=(nb,), in_specs=in_specs, out_specs=out_specs, out_shape=out_shape,
        compiler_params=_params(("arbitrary",)),
    )(*args)
    return (out[0], out[1]) if has_res else (None, out[0])


def _loss_head(h2, target, gain):
    rows, d = h2.shape

    def body(h_ref, t_ref, g_ref, dh_ref, err_ref, dg_ref):
        i = pl.program_id(0)

        @pl.when(i == 0)
        def _():
            err_ref[...] = jnp.zeros_like(err_ref)
            dg_ref[...] = jnp.zeros_like(dg_ref)

        hv = h_ref[...]
        rstd = lax.rsqrt(jnp.mean(hv * hv, axis=-1, keepdims=True) + EPS_RMS)
        xh = hv * rstd
        g = g_ref[...]
        err = xh * g - t_ref[...]
        err_ref[...] += _fold8(err * err)
        dy = err * (1.0 / d)
        dg_ref[...] += _fold8(dy * xh)
        dxh = dy * g
        dh_ref[...] = rstd * (dxh - xh * jnp.mean(dxh * xh, axis=-1, keepdims=True))

    row = pl.BlockSpec((ROW_BLOCK, d), lambda i: (i, 0))
    acc = pl.BlockSpec((SUBLANES, d), lambda i: (0, 0))
    return pl.pallas_call(
        body, name="loss_head", grid=(rows // ROW_BLOCK,),
        in_specs=[row, row, pl.BlockSpec((1, d), lambda i: (0, 0))],
        out_specs=[row, acc, acc], out_shape=[_sds((rows, d)), _sds((SUBLANES, d)), _sds((SUBLANES, d))],
        compiler_params=_params(("arbitrary",)),
    )(h2, target, gain)


def _ada_fwd(cond16, ada_w_loc, ada_b_loc):
    cols = ada_w_loc.shape[1]

    def body(c_ref, w_ref, b_ref, o_ref):
        s = _silu(c_ref[...]).astype(BF16)
        o_ref[...] = _dot(s, w_ref[...].astype(BF16), "nn") + b_ref[...]

    return pl.pallas_call(body, name="ada_fwd", out_shape=_sds((16, cols)), compiler_params=_params())(
        cond16, ada_w_loc, ada_b_loc)


def _ada_bwd(cond16, dmod16, ada_w_loc, c_ctx_row):
    k_, cols = ada_w_loc.shape

    def body(c_ref, dm_ref, w_ref, cc_ref, gw_ref, gc_ref):
        s = _silu(c_ref[...]).astype(BF16)
        dm = dm_ref[...]
        gw_ref[...] = _dot(s, dm.astype(BF16), "tn")
        dmc = jnp.sum(dm[8:16, :], axis=0, keepdims=True)
        dmc8 = jnp.broadcast_to(dmc, (SUBLANES, cols)).astype(BF16)
        ds = _dot(dmc8, w_ref[...].astype(BF16), "nt")
        row = lax.broadcasted_iota(jnp.int32, ds.shape, 0)
        gc_ref[...] = jnp.where(row == 0, ds * _dsilu(cc_ref[...]), 0.0)

    return pl.pallas_call(body, name="ada_bwd", out_shape=[_sds((k_, cols)), _sds((SUBLANES, k_))],
                          compiler_params=_params())(cond16, dmod16, ada_w_loc, c_ctx_row)


def _cmul(a, b):
    return a[0] * b[0] - a[1] * b[1], a[0] * b[1] + a[1] * b[0]


def _disc(lam_re, lam_im, ldt):
    dt = jnp.exp(ldt)
    mag = jnp.exp(lam_re * dt)
    th = lam_im * dt
    a_re, a_im = mag * jnp.cos(th), mag * jnp.sin(th)
    den = lam_re * lam_re + lam_im * lam_im
    n_re = a_re - 1.0
    f_re = (n_re * lam_re + a_im * lam_im) / den
    f_im = (a_im * lam_re - n_re * lam_im) / den
    return dt, mag, th, a_re, a_im, den, n_re, f_re, f_im


def _block_diag_mask(shape):
    row = lax.broadcasted_iota(jnp.int32, shape, 0)
    col = lax.broadcasted_iota(jnp.int32, shape, 1)
    return lax.shift_right_logical(row, 4) == lax.shift_right_logical(col, 6)


def _s5_discretise(name, ascending, lam_re, lam_im, ldt, bt_re, bt_im, ct_re, ct_im):
    def tables(pw, asc, sign):
        row = lax.broadcasted_iota(jnp.int32, (SUBLANES, NSTATE), 0)
        ap_re = jnp.zeros((SUBLANES, NSTATE), F32)
        ap_im = jnp.zeros((SUBLANES, NSTATE), F32)
        for r in range(SUBLANES):
            p = pw[r] if asc else pw[SUBLANES - 1 - r]
            ap_re = jnp.where(row == r, p[0], ap_re)
            ap_im = jnp.where(row == r, sign * p[1], ap_im)
        full = lambda v: jnp.broadcast_to(v, (SUBLANES, NSTATE))
        return [full(pw[0][0]), full(sign * pw[0][1]), full(pw[1][0]), full(sign * pw[1][1]),
                full(pw[3][0]), full(sign * pw[3][1]), ap_re, ap_im]

    def body(lr_ref, li_ref, ldt_ref, br_ref, bi_ref, cr_ref, ci_ref, bb_ref, tab_ref, adj_ref, bm_ref, cm_ref):
        _, _, _, a_re, a_im, _, _, f_re, f_im = _disc(lr_ref[...], li_ref[...], ldt_ref[...])
        bre, bim = br_ref[...], bi_ref[...]
        bb_re = f_re * bre - f_im * bim
        bb_im = f_re * bim + f_im * bre
        bb_ref[0:S5_GROUP, :] = bb_re
        bb_ref[S5_GROUP:2 * S5_GROUP, :] = bb_im
        pw = [(a_re, a_im)]
        for _ in range(SUBLANES - 1):
            pw.append(_cmul(pw[-1], (a_re, a_im)))
        for t, v in enumerate(tables(pw, ascending, 1.0)):
            tab_ref[t] = v
        for t, v in enumerate(tables(pw, not ascending, -1.0)):
            adj_ref[t] = v
        mask = _block_diag_mask((S5_WIDTH, NSTATE))
        tile = lambda v: jnp.broadcast_to(v[None], (S5_GROUPS, S5_GROUP, NSTATE)).reshape(S5_WIDTH, NSTATE)
        bm_ref[:, 0:NSTATE] = jnp.where(mask, tile(bb_re), 0.0).astype(BF16)
        bm_ref[:, NSTATE:2 * NSTATE] = jnp.where(mask, tile(bb_im), 0.0).astype(BF16)
        cm_ref[:, 0:NSTATE] = jnp.where(mask, cr_ref[...], 0.0).astype(BF16)
        cm_ref[:, NSTATE:2 * NSTATE] = jnp.where(mask, -ci_ref[...], 0.0).astype(BF16)

    return pl.pallas_call(
        body, name=name,
        out_shape=[_sds((2 * S5_GROUP, NSTATE)), _sds((8, SUBLANES, NSTATE)), _sds((8, SUBLANES, NSTATE)),
                   _sds((S5_WIDTH, 2 * NSTATE), BF16), _sds((S5_WIDTH, 2 * NSTATE), BF16)],
        compiler_params=_params(),
    )(lam_re, lam_im, ldt, bt_re, bt_im, ct_re, ct_im)


def _s5_discretise_bwd(name, lam_re, lam_im, ldt, bt_re, bt_im, d_abar8, d_bbar):
    def body(lr_ref, li_ref, ldt_ref, br_ref, bi_ref, da_ref, db_ref, dl_ref, dbt_ref):
        lam_re, lam_im = lr_ref[...], li_ref[...]
        dt, mag, _, a_re, a_im, den, n_re, f_re, f_im = _disc(lam_re, lam_im, ldt_ref[...])
        bre, bim = br_ref[...], bi_ref[...]
        dbr, dbi = db_ref[0:S5_GROUP, :], db_ref[S5_GROUP:2 * S5_GROUP, :]
        dbt_ref[0:S5_GROUP, :] = f_re * dbr + f_im * dbi
        dbt_ref[S5_GROUP:2 * S5_GROUP, :] = f_re * dbi - f_im * dbr
        df_re = jnp.sum(bre * dbr + bim * dbi, axis=0, keepdims=True)
        df_im = jnp.sum(bre * dbi - bim * dbr, axis=0, keepdims=True)
        da = da_ref[...]
        da_re = jnp.sum(da[:, 0:NSTATE], axis=0, keepdims=True)
        da_im = jnp.sum(da[:, NSTATE:2 * NSTATE], axis=0, keepdims=True)
        da_re = da_re + (df_re * lam_re - df_im * lam_im) / den
        da_im = da_im + (df_re * lam_im + df_im * lam_re) / den
        ff = (f_re * df_re + f_im * df_im) * 2.0 / den
        d_lr = (df_re * n_re + df_im * a_im) / den - ff * lam_re
        d_li = (df_re * a_im - df_im * n_re) / den - ff * lam_im
        d_mag = (da_re * a_re + da_im * a_im) / mag
        d_th = da_im * a_re - da_re * a_im
        d_lr = d_lr + d_mag * mag * dt
        d_li = d_li + d_th * dt
        d_ldt = (d_mag * mag * lam_re + d_th * lam_im) * dt
        row = lax.broadcasted_iota(jnp.int32, (SUBLANES, NSTATE), 0)
        dl_ref[...] = jnp.where(row == 0, d_lr, jnp.where(row == 1, d_li, jnp.where(row == 2, d_ldt, 0.0)))

    return pl.pallas_call(
        body, name=name, out_shape=[_sds((SUBLANES, NSTATE)), _sds((2 * S5_GROUP, NSTATE))],
        compiler_params=_params(),
    )(lam_re, lam_im, ldt, bt_re, bt_im, d_abar8, d_bbar)


def _scan_chunk(x_ref, out_ref, tab_ref, carry_re, carry_im, ascending, pair_ref=None, acc_ref=None):
    rows = x_ref.shape[0]
    nblk = rows // SUBLANES
    row = lax.broadcasted_iota(jnp.int32, (SUBLANES, SCAN_LANES), 0)
    edge = (SUBLANES - 1) if ascending else 0

    def shifted(v, k):
        if ascending:
            return jnp.where(row >= k, pltpu.roll(v, k, 0), 0.0)
        return jnp.where(row < SUBLANES - k, pltpu.roll(v, SUBLANES - k, 0), 0.0)

    for j in range(NSTATE // SCAN_LANES):
        re_l = pl.ds(j * SCAN_LANES, SCAN_LANES)
        im_l = pl.ds(NSTATE + j * SCAN_LANES, SCAN_LANES)
        tabs = [tab_ref[t, :, re_l] for t in range(8)]

        def blk(b, carry):
            cr, ci = carry[0], carry[1]
            r0 = pl.multiple_of((b if ascending else nblk - 1 - b) * SUBLANES, SUBLANES)
            rs = pl.ds(r0, SUBLANES)
            xr, xi = x_ref[rs, re_l], x_ref[rs, im_l]
            for t, k in ((0, 1), (2, 2), (4, 4)):
                sr, si = shifted(xr, k), shifted(xi, k)
                xr, xi = xr + (tabs[t] * sr - tabs[t + 1] * si), xi + (tabs[t] * si + tabs[t + 1] * sr)
            xr = xr + (tabs[6] * cr - tabs[7] * ci)
            xi = xi + (tabs[6] * ci + tabs[7] * cr)
            out_ref[rs, re_l] = xr.astype(out_ref.dtype)
            out_ref[rs, im_l] = xi.astype(out_ref.dtype)
            new = (jnp.broadcast_to(xr[edge:edge + 1, :], xr.shape), jnp.broadcast_to(xi[edge:edge + 1, :], xi.shape))
            if pair_ref is None:
                return new
            if ascending:
                pr = jnp.where(row >= 1, pltpu.roll(xr, 1, 0), cr)
                pi = jnp.where(row >= 1, pltpu.roll(xi, 1, 0), ci)
            else:
                pr = jnp.where(row < SUBLANES - 1, pltpu.roll(xr, SUBLANES - 1, 0), cr)
                pi = jnp.where(row < SUBLANES - 1, pltpu.roll(xi, SUBLANES - 1, 0), ci)
            sr, si = pair_ref[rs, re_l], pair_ref[rs, im_l]
            return new + (carry[2] + (pr * sr + pi * si), carry[3] + (pi * sr - pr * si))

        init = (carry_re[:, re_l], carry_im[:, re_l])
        if pair_ref is not None:
            init = init + (jnp.zeros((SUBLANES, SCAN_LANES), F32), jnp.zeros((SUBLANES, SCAN_LANES), F32))
        fin = lax.fori_loop(0, nblk, blk, init)
        carry_re[:, re_l] = fin[0]
        carry_im[:, re_l] = fin[1]
        if pair_ref is not None:
            acc_ref[:, re_l] += fin[2]
            acc_ref[:, im_l] += fin[3]


def _scan_block_index(i, n_lat, ctx_first_then_ascending):
    if ctx_first_then_ascending:
        return jnp.where(i == 0, n_lat, i - 1)
    return jnp.where(i == 0, n_lat, n_lat - i)


def _s5_scan_fwd(name, ascending, z_all, bmat, cmat, tab):
    rows = z_all.shape[0]
    nb = rows // ROW_BLOCK
    n_lat = nb - 1

    def body(u_ref, bm_ref, cm_ref, tab_ref, s_ref, y_ref, bu, carry_re, carry_im):
        @pl.when(pl.program_id(0) == 0)
        def _():
            carry_re[...] = jnp.zeros_like(carry_re)
            carry_im[...] = jnp.zeros_like(carry_im)

        bu[...] = _dot(u_ref[...].astype(BF16), bm_ref[...], "nn")
        _scan_chunk(bu, s_ref, tab_ref, carry_re, carry_im, ascending)
        y_ref[...] = _dot(s_ref[...].astype(BF16), cm_ref[...], "nt")

    blk = lambda i: (_scan_block_index(i, n_lat, ascending), 0)
    full = lambda shape: pl.BlockSpec(shape, lambda i: (0,) * len(shape))
    return pl.pallas_call(
        body, name=name, grid=(nb,),
        in_specs=[pl.BlockSpec((ROW_BLOCK, S5_WIDTH), blk), full((S5_WIDTH, 2 * NSTATE)), full((S5_WIDTH, 2 * NSTATE)),
                  full((8, SUBLANES, NSTATE))],
        out_specs=[pl.BlockSpec((ROW_BLOCK, 2 * NSTATE), blk), pl.BlockSpec((ROW_BLOCK, S5_WIDTH), blk)],
        out_shape=[_sds((rows, 2 * NSTATE)), _sds((rows, S5_WIDTH))],
        scratch_shapes=[pltpu.VMEM((ROW_BLOCK, 2 * NSTATE), F32), pltpu.VMEM((SUBLANES, NSTATE), F32),
                        pltpu.VMEM((SUBLANES, NSTATE), F32)],
        compiler_params=_params(("arbitrary",)),
    )(z_all, bmat, cmat, tab)


def _s5_scan_bwd(name, ascending, dy, states, cmat, adj):
    rows = states.shape[0]
    nb = rows // ROW_BLOCK
    n_lat = nb - 1

    def block_index(i):
        if ascending:
            return jnp.where(i == nb - 1, n_lat, n_lat - 1 - i)
        return jnp.where(i == nb - 1, n_lat, i)

    def body(dy_ref, s_ref, cm_ref, adj_ref, g_ref, da_ref, q, carry_re, carry_im):
        i = pl.program_id(0)

        @pl.when(i == 0)
        def _():
            carry_re[...] = jnp.zeros_like(carry_re)
            carry_im[...] = jnp.zeros_like(carry_im)
            da_ref[...] = jnp.zeros_like(da_ref)

        @pl.when(i < nb - 1)
        def _():
            q[...] = _dot(dy_ref[...].astype(BF16), cm_ref[...], "nn")

        @pl.when(i == nb - 1)
        def _():
            q[...] = jnp.zeros_like(q)

        _scan_chunk(q, g_ref, adj_ref, carry_re, carry_im, not ascending, pair_ref=s_ref, acc_ref=da_ref)

    blk = lambda i: (block_index(i), 0)
    full = lambda shape: pl.BlockSpec(shape, lambda i: (0,) * len(shape))
    return pl.pallas_call(
        body, name=name, grid=(nb,),
        in_specs=[pl.BlockSpec((ROW_BLOCK, S5_WIDTH), lambda i: (jnp.minimum(block_index(i), n_lat - 1), 0)),
                  pl.BlockSpec((ROW_BLOCK, 2 * NSTATE), blk), full((S5_WIDTH, 2 * NSTATE)), full((8, SUBLANES, NSTATE))],
        out_specs=[pl.BlockSpec((ROW_BLOCK, 2 * NSTATE), blk), full((SUBLANES, 2 * NSTATE))],
        out_shape=[_sds((rows, 2 * NSTATE), BF16), _sds((SUBLANES, 2 * NSTATE))],
        scratch_shapes=[pltpu.VMEM((ROW_BLOCK, 2 * NSTATE), F32), pltpu.VMEM((SUBLANES, NSTATE), F32),
                        pltpu.VMEM((SUBLANES, NSTATE), F32)],
        compiler_params=_params(("arbitrary",)),
    )(dy, states, cmat, adj)


def _glu_fwd(z_all, y0, y1, d_skip, w_glu, n_rows):
    def body(u_ref, y0_ref, y1_ref, d_ref, w_ref, o_ref):
        y = d_ref[...] * u_ref[...] + y0_ref[...] + y1_ref[...]
        g = _gelu(y)
        t = _dot(g.astype(BF16), w_ref[...], "nn")
        o_ref[...] = (g * _sigmoid(t)).astype(o_ref.dtype)

    row = pl.BlockSpec((ROW_BLOCK, S5_WIDTH), lambda i: (i, 0))
    return pl.pallas_call(
        body, name="glu_fwd", grid=(n_rows // ROW_BLOCK,),
        in_specs=[row, row, row, pl.BlockSpec((1, S5_WIDTH), lambda i: (0, 0)),
                  pl.BlockSpec((S5_WIDTH, S5_WIDTH), lambda i: (0, 0))],
        out_specs=row, out_shape=_sds((n_rows, S5_WIDTH), BF16), compiler_params=_params(("parallel",)),
    )(z_all, y0, y1, d_skip, w_glu)


def _glu_bwd(d_ycat, z_all, y0, y1, d_skip, w_glu, n_rows):
    def body(do_ref, u_ref, y0_ref, y1_ref, d_ref, w_ref, dy_ref, dw_ref, dd_ref):
        @pl.when(pl.program_id(0) == 0)
        def _():
            dw_ref[...] = jnp.zeros_like(dw_ref)
            dd_ref[...] = jnp.zeros_like(dd_ref)

        u = u_ref[...]
        y = d_ref[...] * u + y0_ref[...] + y1_ref[...]
        g = _gelu(y)
        gb = g.astype(BF16)
        w = w_ref[...]
        sg = _sigmoid(_dot(gb, w, "nn"))
        do = do_ref[...]
        dt = do * g * sg * (1.0 - sg)
        dtb = dt.astype(BF16)
        dg = do * sg + _dot(dtb, w, "nt")
        dy = dg * _dgelu(y)
        dy_ref[...] = dy
        dw_ref[...] += _dot(gb, dtb, "tn")
        dd_ref[...] += _fold8(dy * u)

    row = pl.BlockSpec((ROW_BLOCK, S5_WIDTH), lambda i: (i, 0))
    sq = pl.BlockSpec((S5_WIDTH, S5_WIDTH), lambda i: (0, 0))
    return pl.pallas_call(
        body, name="glu_bwd", grid=(n_rows // ROW_BLOCK,),
        in_specs=[row, row, row, row, pl.BlockSpec((1, S5_WIDTH), lambda i: (0, 0)), sq],
        out_specs=[row, sq, pl.BlockSpec((SUBLANES, S5_WIDTH), lambda i: (0, 0))],
        out_shape=[_sds((n_rows, S5_WIDTH)), _sds((S5_WIDTH, S5_WIDTH)), _sds((SUBLANES, S5_WIDTH))],
        compiler_params=_params(("arbitrary",)),
    )(d_ycat, z_all, y0, y1, d_skip, w_glu)


CONV_HALF = CONV_K // 2


def _conv_block(n_rows):
    blk = min(1024, n_rows)
    assert blk >= CONV_HALF * GRID_W and n_rows % blk == 0
    return blk


def _conv_gate(z_all, n_rows):
    blk = _conv_block(n_rows)
    nb = n_rows // blk

    def body(v_ref, g_ref, o_ref):
        i = pl.program_id(0)
        inside = jnp.logical_and(i >= 1, i <= nb)

        @pl.when(inside)
        def _():
            o_ref[...] = v_ref[...] * _sigmoid(g_ref[...])

        @pl.when(jnp.logical_not(inside))
        def _():
            o_ref[...] = jnp.zeros_like(o_ref)

    src = lambda col: pl.BlockSpec((blk, CONV_WIDTH), lambda i: (jnp.clip(i - 1, 0, nb - 1), col))
    return pl.pallas_call(
        body, name="conv_gate", grid=(nb + 2,), in_specs=[src(1), src(2)],
        out_specs=pl.BlockSpec((blk, CONV_WIDTH), lambda i: (i, 0)),
        out_shape=_sds(((nb + 2) * blk, CONV_WIDTH)), compiler_params=_params(("parallel",)),
    )(z_all, z_all)


def _load_window(pad_ref, win, sem, blk):
    start = pl.multiple_of(pl.program_id(0) * blk, blk)
    copy = pltpu.make_async_copy(pad_ref.at[pl.ds(start, 3 * blk), :], win, sem)
    copy.start()
    copy.wait()


def _conv_fwd(hh_pad, w, b, ln_g, ln_b, n_rows):
    blk = _conv_block(n_rows)

    def body(hh_ref, w_ref, b_ref, g_ref, lb_ref, hc_ref, y_ref, win, sem):
        _load_window(hh_ref, win, sem, blk)

        def tile(t, _):
            r0 = pl.multiple_of(t * CONV_ROWS, CONV_ROWS)
            acc = jnp.zeros((CONV_ROWS, CONV_WIDTH), F32)
            for k in range(CONV_K):
                acc = acc + w_ref[k:k + 1, :] * win[pl.ds(r0 + blk + (k - CONV_HALF) * GRID_W, CONV_ROWS), :]
            hc = acc + b_ref[...]
            hc_ref[pl.ds(r0, CONV_ROWS), :] = hc
            mu = jnp.mean(hc, axis=-1, keepdims=True)
            xc = hc - mu
            ln = xc * lax.rsqrt(jnp.mean(xc * xc, axis=-1, keepdims=True) + EPS_LN) * g_ref[...] + lb_ref[...]
            y_ref[pl.ds(r0, CONV_ROWS), :] = _silu(ln).astype(y_ref.dtype)
            return 0

        lax.fori_loop(0, blk // CONV_ROWS, tile, 0)

    vec = pl.BlockSpec((1, CONV_WIDTH), lambda i: (0, 0))
    row = pl.BlockSpec((blk, CONV_WIDTH), lambda i: (i, 0))
    return pl.pallas_call(
        body, name="conv_fwd", grid=(n_rows // blk,),
        in_specs=[ANY, pl.BlockSpec((CONV_K, CONV_WIDTH), lambda i: (0, 0)), vec, vec, vec],
        out_specs=[row, row], out_shape=[_sds((n_rows, CONV_WIDTH)), _sds((n_rows, CONV_WIDTH), BF16)],
        scratch_shapes=[pltpu.VMEM((3 * blk, CONV_WIDTH), F32), pltpu.SemaphoreType.DMA],
        compiler_params=_params(("arbitrary",)),
    )(hh_pad, w, b, ln_g, ln_b)


def _conv_bwd_norm(d_ycat, hc, ln_g, ln_b, n_rows):
    blk = _conv_block(n_rows)
    nb = n_rows // blk

    def body(dy_ref, hc_ref, g_ref, lb_ref, o_ref, sums):
        i = pl.program_id(0)

        @pl.when(i == 0)
        def _():
            sums[...] = jnp.zeros_like(sums)

        inside = jnp.logical_and(i >= 1, i <= nb)

        @pl.when(inside)
        def _():
            hcv = hc_ref[...]
            mu = jnp.mean(hcv, axis=-1, keepdims=True)
            xc = hcv - mu
            rstd = lax.rsqrt(jnp.mean(xc * xc, axis=-1, keepdims=True) + EPS_LN)
            xh = xc * rstd
            g = g_ref[...]
            dln = dy_ref[...] * _dsilu(xh * g + lb_ref[...])
            dxh = dln * g
            dhc = rstd * (dxh - jnp.mean(dxh, axis=-1, keepdims=True) - xh * jnp.mean(dxh * xh, axis=-1, keepdims=True))
            o_ref[...] = dhc
            sums[0] += _fold8(dhc)
            sums[1] += _fold8(dln * xh)
            sums[2] += _fold8(dln)

        @pl.when(jnp.logical_not(inside))
        def _():
            o_ref[...] = jnp.zeros_like(o_ref)

    vec = pl.BlockSpec((1, CONV_WIDTH), lambda i: (0, 0))
    return pl.pallas_call(
        body, name="conv_bwd_norm", grid=(nb + 2,),
        in_specs=[pl.BlockSpec((blk, CONV_WIDTH), lambda i: (jnp.clip(i - 1, 0, nb - 1), 1)),
                  pl.BlockSpec((blk, CONV_WIDTH), lambda i: (jnp.clip(i - 1, 0, nb - 1), 0)), vec, vec],
        out_specs=[pl.BlockSpec((blk, CONV_WIDTH), lambda i: (i, 0)),
                   pl.BlockSpec((3, SUBLANES, CONV_WIDTH), lambda i: (0, 0, 0))],
        out_shape=[_sds(((nb + 2) * blk, CONV_WIDTH)), _sds((3, SUBLANES, CONV_WIDTH))],
        compiler_params=_params(("arbitrary",)),
    )(d_ycat, hc, ln_g, ln_b)


def _conv_bwd_taps(dhc_pad, hh_pad, z_all, w, n_rows):
    blk = _conv_block(n_rows)

    def body(dhc_ref, hh_ref, v_ref, g_ref, w_ref, dv_ref, dg_ref, dw_ref, dwin, hwin, sems):
        @pl.when(pl.program_id(0) == 0)
        def _():
            dw_ref[...] = jnp.zeros_like(dw_ref)

        _load_window(dhc_ref, dwin, sems.at[0], blk)
        _load_window(hh_ref, hwin, sems.at[1], blk)

        def tile(t, _):
            r0 = pl.multiple_of(t * CONV_ROWS, CONV_ROWS)
            dh = dwin[pl.ds(r0 + blk, CONV_ROWS), :]
            acc = jnp.zeros((CONV_ROWS, CONV_WIDTH), F32)
            for k in range(CONV_K):
                off = (k - CONV_HALF) * GRID_W
                acc = acc + w_ref[k:k + 1, :] * dwin[pl.ds(r0 + blk - off, CONV_ROWS), :]
                dw_ref[k] += _fold8(dh * hwin[pl.ds(r0 + blk + off, CONV_ROWS), :])
            rs = pl.ds(r0, CONV_ROWS)
            sg = _sigmoid(g_ref[rs, :])
            vv = v_ref[rs, :]
            dv_ref[rs, :] = acc * sg
            dg_ref[rs, :] = acc * vv * sg * (1.0 - sg)
            return 0

        lax.fori_loop(0, blk // CONV_ROWS, tile, 0)

    row = pl.BlockSpec((blk, CONV_WIDTH), lambda i: (i, 0))
    return pl.pallas_call(
        body, name="conv_bwd_taps", grid=(n_rows // blk,),
        in_specs=[ANY, ANY,
            pl.BlockSpec((blk, CONV_WIDTH), lambda i: (i, 1)), pl.BlockSpec((blk, CONV_WIDTH), lambda i: (i, 2)),
            pl.BlockSpec((CONV_K, CONV_WIDTH), lambda i: (0, 0))],
        out_specs=[row, row, pl.BlockSpec((CONV_K, SUBLANES, CONV_WIDTH), lambda i: (0, 0, 0))],
        out_shape=[_sds((n_rows, CONV_WIDTH)), _sds((n_rows, CONV_WIDTH)), _sds((CONV_K, SUBLANES, CONV_WIDTH))],
        scratch_shapes=[pltpu.VMEM((3 * blk, CONV_WIDTH), F32), pltpu.VMEM((3 * blk, CONV_WIDTH), F32),
                        pltpu.SemaphoreType.DMA((2,))],
        compiler_params=_params(("arbitrary",)),
    )(dhc_pad, hh_pad, z_all, z_all, w)


def _dz_assemble(du0, du1, dy, d_skip, dv, dgate, n_lat):
    rows = du0.shape[0]
    nb = rows // ROW_BLOCK

    def body(a_ref, b_ref, dy_ref, d_ref, dv_ref, dg_ref, o_ref):
        i, j = pl.program_id(0), pl.program_id(1)
        lat = i < n_lat

        @pl.when(jnp.logical_and(j == 0, lat))
        def _():
            o_ref[...] = (a_ref[...] + b_ref[...] + dy_ref[...] * d_ref[...]).astype(o_ref.dtype)

        @pl.when(jnp.logical_and(j == 0, jnp.logical_not(lat)))
        def _():
            o_ref[...] = (a_ref[...] + b_ref[...]).astype(o_ref.dtype)

        @pl.when(jnp.logical_and(j == 1, lat))
        def _():
            o_ref[...] = dv_ref[...].astype(o_ref.dtype)

        @pl.when(jnp.logical_and(j == 2, lat))
        def _():
            o_ref[...] = dg_ref[...].astype(o_ref.dtype)

        @pl.when(jnp.logical_and(j >= 1, jnp.logical_not(lat)))
        def _():
            o_ref[...] = jnp.zeros_like(o_ref)

    all_rows = pl.BlockSpec((ROW_BLOCK, S5_WIDTH), lambda i, j: (i, 0))
    lat_rows = pl.BlockSpec((ROW_BLOCK, S5_WIDTH), lambda i, j: (jnp.minimum(i, n_lat - 1), 0))
    return pl.pallas_call(
        body, name="dz_assemble", grid=(nb, 3),
        in_specs=[all_rows, all_rows, lat_rows, pl.BlockSpec((1, S5_WIDTH), lambda i, j: (0, 0)), lat_rows, lat_rows],
        out_specs=pl.BlockSpec((ROW_BLOCK, S5_WIDTH), lambda i, j: (i, j)),
        out_shape=_sds((rows, IN_COLS), BF16), compiler_params=_params(("parallel", "parallel")),
    )(du0, du1, dy, d_skip, dv, dgate)


def _sum_parts(parts):
    _, r, c = parts.shape

    def body(p_ref, o_ref):
        acc = p_ref[0]
        for q in range(1, NDEV):
            acc = acc + p_ref[q]
        o_ref[...] = acc

    return pl.pallas_call(body, name="sum_parts", out_shape=_sds((r, c)), compiler_params=_params())(parts)


def _row_tile(r, c):
    best = r
    for t in (1024, 512, 256, 128, 64, 32, 16, 8):
        if r % t == 0 and t * c <= 128 * 1024:
            return t
    return best


def _adamw(name, w, gparts, m, v):
    r, c = w.shape
    np_ = gparts.shape[0]
    tr = _row_tile(r, c)

    def body(w_ref, g_ref, m_ref, v_ref, go_ref, d_ref, mo_ref, vo_ref):
        g = g_ref[0]
        for q in range(1, np_):
            g = g + g_ref[q]
        m2 = ADAM_B1 * m_ref[...] + (1.0 - ADAM_B1) * g
        v2 = ADAM_B2 * v_ref[...] + (1.0 - ADAM_B2) * jnp.square(g)
        m_hat = m2 / (1.0 - ADAM_B1 ** ADAM_STEP)
        v_hat = v2 / (1.0 - ADAM_B2 ** ADAM_STEP)
        go_ref[...] = g
        d_ref[...] = -ADAM_LR * (m_hat / (jnp.sqrt(v_hat) + ADAM_EPS) + ADAM_WD * w_ref[...])
        mo_ref[...] = m2
        vo_ref[...] = v2

    row = pl.BlockSpec((tr, c), lambda i: (i, 0))
    return pl.pallas_call(
        body, name=name, grid=(r // tr,),
        in_specs=[row, pl.BlockSpec((np_, tr, c), lambda i: (0, i, 0)), row, row],
        out_specs=[row] * 4, out_shape=[_sds((r, c))] * 4, compiler_params=_params(("parallel",)),
    )(w, gparts, m, v)


SMALL = ["c_ctx", "ada_b", "norm1_g", "s5_lam_re", "s5_lam_im", "s5_log_dt", "s5_b_re", "s5_b_im", "s5_c_re",
         "s5_c_im", "s5_d", "conv_b", "conv_ln_g", "conv_ln_b", "norm2_g", "final_g"]


def _pack_small(parts):
    flat = jnp.concatenate([p.reshape(-1).astype(F32) for p in parts])
    return jnp.pad(flat, (0, SMALL_ROWS * D_MODEL - flat.shape[0])).reshape(SMALL_ROWS, D_MODEL)


def _unpack_small(packed, like):
    flat = packed.reshape(-1)
    out, off = [], 0
    for ref in like:
        out.append(flat[off:off + ref.size].reshape(ref.shape))
        off += ref.size
    return out


def kernel(x, c, ctx, c_ctx, ada_w, ada_b, norm1_g, w_in, s5_lam_re, s5_lam_im, s5_log_dt, s5_b_re, s5_b_im, s5_c_re, s5_c_im, s5_d, s5_w_glu, conv_w, conv_b, conv_ln_g, conv_ln_b, w_out, norm2_g, mlp_w1, mlp_w2, final_g, loss_target, m_c_ctx, m_ada_w, m_ada_b, m_norm1_g, m_w_in, m_s5_lam_re, m_s5_lam_im, m_s5_log_dt, m_s5_b_re, m_s5_b_im, m_s5_c_re, m_s5_c_im, m_s5_d, m_s5_w_glu, m_conv_w, m_conv_b, m_conv_ln_g, m_conv_ln_b, m_w_out, m_norm2_g, m_mlp_w1, m_mlp_w2, m_final_g, v_c_ctx, v_ada_w, v_ada_b, v_norm1_g, v_w_in, v_s5_lam_re, v_s5_lam_im, v_s5_log_dt, v_s5_b_re, v_s5_b_im, v_s5_c_re, v_s5_c_im, v_s5_d, v_s5_w_glu, v_conv_w, v_conv_b, v_conv_ln_g, v_conv_ln_b, v_w_out, v_norm2_g, v_mlp_w1, v_mlp_w2, v_final_g):
    weights = dict(c_ctx=c_ctx, ada_w=ada_w, ada_b=ada_b, norm1_g=norm1_g, w_in=w_in, s5_lam_re=s5_lam_re, s5_lam_im=s5_lam_im, s5_log_dt=s5_log_dt, s5_b_re=s5_b_re, s5_b_im=s5_b_im, s5_c_re=s5_c_re, s5_c_im=s5_c_im, s5_d=s5_d, s5_w_glu=s5_w_glu, conv_w=conv_w, conv_b=conv_b, conv_ln_g=conv_ln_g, conv_ln_b=conv_ln_b, w_out=w_out, norm2_g=norm2_g, mlp_w1=mlp_w1, mlp_w2=mlp_w2, final_g=final_g)
    mom1 = dict(c_ctx=m_c_ctx, ada_w=m_ada_w, ada_b=m_ada_b, norm1_g=m_norm1_g, w_in=m_w_in, s5_lam_re=m_s5_lam_re, s5_lam_im=m_s5_lam_im, s5_log_dt=m_s5_log_dt, s5_b_re=m_s5_b_re, s5_b_im=m_s5_b_im, s5_c_re=m_s5_c_re, s5_c_im=m_s5_c_im, s5_d=m_s5_d, s5_w_glu=m_s5_w_glu, conv_w=m_conv_w, conv_b=m_conv_b, conv_ln_g=m_conv_ln_g, conv_ln_b=m_conv_ln_b, w_out=m_w_out, norm2_g=m_norm2_g, mlp_w1=m_mlp_w1, mlp_w2=m_mlp_w2, final_g=m_final_g)
    mom2 = dict(c_ctx=v_c_ctx, ada_w=v_ada_w, ada_b=v_ada_b, norm1_g=v_norm1_g, w_in=v_w_in, s5_lam_re=v_s5_lam_re, s5_lam_im=v_s5_lam_im, s5_log_dt=v_s5_log_dt, s5_b_re=v_s5_b_re, s5_b_im=v_s5_b_im, s5_c_re=v_s5_c_re, s5_c_im=v_s5_c_im, s5_d=v_s5_d, s5_w_glu=v_s5_w_glu, conv_w=v_conv_w, conv_b=v_conv_b, conv_ln_g=v_conv_ln_g, conv_ln_b=v_conv_ln_b, w_out=v_w_out, norm2_g=v_norm2_g, mlp_w1=v_mlp_w1, mlp_w2=v_mlp_w2, final_g=v_final_g)
    order = list(weights)

    me = 4 * lax.axis_index("x") + 2 * lax.axis_index("y") + lax.axis_index("c")
    xs, cs, tgt = x[0], ctx[0], loss_target[0]
    n_lat_rows, n_ctx_rows = xs.shape[0], cs.shape[0]
    n_rows = n_lat_rows + n_ctx_rows
    n_lat = n_lat_rows // ROW_BLOCK
    ada_cols = ada_w.shape[2]

    w_in_loc, w1_loc = w_in[0].astype(BF16), mlp_w1[0].astype(BF16)
    w_in_g, glu_g, conv_w_g, w_out_g, w1_g, w2_g, c_all = _exchange(
        "gather_weights",
        [w_in_loc, s5_w_glu[0].astype(BF16), conv_w[0], w_out[0].astype(BF16), w1_loc, mlp_w2[0].astype(BF16), c],
        [True] * 7)
    w_in_full = jnp.transpose(w_in_g, (1, 0, 2)).reshape(D_MODEL, IN_COLS)
    glu_full = glu_g.reshape(S5_WIDTH, S5_WIDTH)
    conv_w_full = jnp.transpose(conv_w_g, (1, 0, 2)).reshape(CONV_K, CONV_WIDTH)
    w_out_full = w_out_g.reshape(D_MODEL, D_MODEL)
    w2_full = w2_g.reshape(D_FF, D_MODEL)
    c_all = c_all.reshape(NDEV, D_MODEL)

    cond_fwd = jnp.concatenate([c_all, c_ctx[None], jnp.zeros((7, D_MODEL), F32)])
    ada_b_loc = lax.dynamic_slice(ada_b, (0, me * ada_cols), (1, ada_cols))
    (mod_g,) = _exchange("gather_mod", [_ada_fwd(cond_fwd, ada_w[0], ada_b_loc)], [True])
    mod_rows = jnp.transpose(mod_g, (1, 0, 2)).reshape(16, 6 * D_MODEL)
    mod = lax.dynamic_slice(mod_rows, (me, 0), (1, 6 * D_MODEL)).reshape(6, D_MODEL)
    modc = mod_rows[8, :2 * D_MODEL].reshape(2, D_MODEL)
    sh1, sc1, g1, sh2, sc2, g2 = [mod[i:i + 1] for i in range(6)]

    a_all = _prenorm("prenorm1", xs, cs, norm1_g, jnp.stack([mod[0:2], modc]))
    (z_all,) = _matmul("in_proj", a_all, w_in_full, "nn", (n_rows, IN_COLS, D_MODEL), (ROW_BLOCK, 512, 512),
                       [((n_rows, IN_COLS), F32)])

    lam_re, lam_im = s5_lam_re[0].reshape(2, 1, NSTATE), s5_lam_im[0].reshape(2, 1, NSTATE)
    ldt = jnp.repeat(s5_log_dt[0], S5_STATE, axis=-1).reshape(2, 1, NSTATE)
    bt_re = jnp.transpose(s5_b_re[0], (0, 3, 1, 2)).reshape(2, S5_GROUP, NSTATE)
    bt_im = jnp.transpose(s5_b_im[0], (0, 3, 1, 2)).reshape(2, S5_GROUP, NSTATE)
    ct_re = jnp.tile(s5_c_re[0].reshape(2, S5_WIDTH, S5_STATE), (1, 1, S5_GROUPS))
    ct_im = jnp.tile(s5_c_im[0].reshape(2, S5_WIDTH, S5_STATE), (1, 1, S5_GROUPS))
    d_skip = s5_d[0].reshape(1, S5_WIDTH)
    disc, states, y_dir = [], [], []
    for d in range(2):
        disc.append(_s5_discretise(f"s5_disc{d}", d == 0, lam_re[d], lam_im[d], ldt[d], bt_re[d], bt_im[d], ct_re[d], ct_im[d]))
        _, tab, _, bmat, cmat = disc[d]
        s, yd = _s5_scan_fwd(f"s5_scan_fwd{d}", d == 0, z_all, bmat, cmat, tab)
        states.append(s)
        y_dir.append(yd)
    y_s5 = _glu_fwd(z_all, y_dir[0], y_dir[1], d_skip, glu_full, n_lat_rows)

    hh_pad = _conv_gate(z_all, n_lat_rows)
    hc, y_conv = _conv_fwd(hh_pad, conv_w_full, conv_b, conv_ln_g, conv_ln_b, n_lat_rows)

    ycat = jnp.concatenate([y_s5, y_conv], axis=1)
    row_vec = lambda: pl.BlockSpec((1, 512), lambda i, j, k: (0, j))
    out_tile = lambda: pl.BlockSpec((512, 512), lambda i, j, k: (i, j))
    gated = lambda acc, res, gate: (acc, res + gate * acc)
    mix, h1 = _matmul("out_proj", ycat, w_out_full, "nn", (n_lat_rows, D_MODEL, D_MODEL), (512, 512, 512),
                      [((n_lat_rows, D_MODEL), F32)] * 2, epi=gated, epi_extra=[(xs, out_tile()), (g1, row_vec())])
    a2 = _prenorm("prenorm2", h1, None, norm2_g, mod[3:5][None])
    (f,) = _matmul("mlp_up", a2, w1_g, "nn", (n_lat_rows, D_FF, D_MODEL), (512, 512, 512), [((n_lat_rows, D_FF), F32)],
                   b_spec=pl.BlockSpec((None, 512, 512), lambda i, j, k: (j, k, 0)))
    sq_relu = lambda t: jnp.square(jnp.maximum(t, 0.0))
    mlp_out, h2 = _matmul("mlp_down", f, w2_full, "nn", (n_lat_rows, D_MODEL, D_FF), (512, 512, 512),
                          [((n_lat_rows, D_MODEL), F32)] * 2, a_fn=sq_relu, epi=gated,
                          epi_extra=[(h1, out_tile()), (g2, row_vec())])

    d_h2, err_sums, d_final_g8 = _loss_head(h2, tgt, final_g[None])
    loss = lax.psum(0.5 / D_MODEL * jnp.sum(err_sums), ("x", "y", "c"))

    k_vec = lambda: pl.BlockSpec((1, 512), lambda i, j, k: (0, k))
    scale_cols = lambda t, gate: t * gate
    (d_f,) = _matmul("mlp_down_dx", d_h2, w2_full, "nt", (n_lat_rows, D_FF, D_MODEL), (512, 512, 512),
                     [((n_lat_rows, D_FF), BF16)], a_fn=scale_cols, a_extra=[(g2, k_vec())],
                     epi=lambda acc, ft: (acc * 2.0 * jnp.maximum(ft, 0.0),), epi_extra=[(f, out_tile())])
    (g_w2,) = _matmul("mlp_down_dw", f, d_h2, "tn", (D_FF, D_MODEL, n_lat_rows), (512, 512, 512),
                      [((D_FF, D_MODEL), F32)], a_fn=sq_relu, epi=lambda acc, gate: (acc * gate,),
                      epi_extra=[(g2, row_vec())])
    (d_a2,) = _matmul("mlp_up_dx", d_f, w1_g, "nt", (n_lat_rows, D_MODEL, D_FF), (512, 512, 512),
                      [((n_lat_rows, D_MODEL), F32)], b_spec=pl.BlockSpec((None, 512, 512), lambda i, j, k: (k, j, 0)))
    (g_w1,) = _matmul("mlp_up_dw", a2, d_f, "tn", (D_MODEL, D_FF, n_lat_rows), (512, 512, 512),
                      [((NDEV, D_MODEL, D_FF // NDEV), F32)],
                      out_specs=[pl.BlockSpec((None, 512, 512), lambda i, j, k: (j, i, 0))])
    d_h1, sums2 = _norm_bwd("norm2_bwd", h1, d_a2, 0, norm2_g, sc2, res=d_h2, aux=mlp_out)

    (d_ycat,) = _matmul("out_proj_dx", d_h1, w_out_full, "nt", (n_lat_rows, D_MODEL, D_MODEL), (512, 512, 512),
                        [((n_lat_rows, D_MODEL), F32)], a_fn=scale_cols, a_extra=[(g1, k_vec())])
    (g_w_out,) = _matmul("out_proj_dw", ycat, d_h1, "tn", (D_MODEL, D_MODEL, n_lat_rows), (512, 512, 512),
                         [((D_MODEL, D_MODEL), F32)], epi=lambda acc, gate: (acc * gate,), epi_extra=[(g1, row_vec())])

    dy, g_glu, dd8 = _glu_bwd(d_ycat, z_all, y_dir[0], y_dir[1], d_skip, glu_full, n_lat_rows)
    du, g_lam_re, g_lam_im, g_ldt, g_b_re, g_b_im, g_c_re, g_c_im = [], [], [], [], [], [], [], []
    diag = lambda mat: jnp.diagonal(mat.reshape(S5_GROUPS, S5_GROUP, S5_GROUPS, S5_STATE), axis1=0, axis2=2)
    for d in range(2):
        _, _, adj, bmat, cmat = disc[d]
        g_adj, d_abar8 = _s5_scan_bwd(f"s5_scan_bwd{d}", d == 0, dy, states[d], cmat, adj)
        (du_d,) = _matmul(f"s5_du{d}", g_adj, bmat, "nt", (n_rows, S5_WIDTH, 2 * NSTATE), (ROW_BLOCK, 512, 1024),
                          [((n_rows, S5_WIDTH), F32)])
        du.append(du_d)
        (d_bmat,) = _matmul(f"s5_db{d}", z_all, g_adj, "tn", (S5_WIDTH, 2 * NSTATE, n_rows), (512, 1024, ROW_BLOCK),
                            [((S5_WIDTH, 2 * NSTATE), F32)])
        (d_cmat,) = _matmul(f"s5_dc{d}", dy, states[d], "tn", (S5_WIDTH, 2 * NSTATE, n_lat_rows), (512, 1024, ROW_BLOCK),
                            [((S5_WIDTH, 2 * NSTATE), F32)])
        to_hn = lambda t: jnp.transpose(t, (0, 2, 1)).reshape(S5_GROUP, NSTATE)
        d_bbar = jnp.concatenate([to_hn(diag(d_bmat[:, :NSTATE])), to_hn(diag(d_bmat[:, NSTATE:]))])
        d_lam8, d_bt = _s5_discretise_bwd(f"s5_disc_bwd{d}", lam_re[d], lam_im[d], ldt[d], bt_re[d], bt_im[d], d_abar8, d_bbar)
        g_lam_re.append(d_lam8[0].reshape(S5_GROUPS, S5_STATE))
        g_lam_im.append(d_lam8[1].reshape(S5_GROUPS, S5_STATE))
        g_ldt.append(d_lam8[2].reshape(S5_GROUPS, S5_STATE).sum(axis=-1))
        to_gph = lambda t: jnp.transpose(t.reshape(S5_GROUP, S5_GROUPS, S5_STATE), (1, 2, 0))
        g_b_re.append(to_gph(d_bt[:S5_GROUP]))
        g_b_im.append(to_gph(d_bt[S5_GROUP:]))
        g_c_re.append(jnp.transpose(diag(d_cmat[:, :NSTATE]), (2, 0, 1)))
        g_c_im.append(-jnp.transpose(diag(d_cmat[:, NSTATE:]), (2, 0, 1)))

    dhc_pad, conv_sums = _conv_bwd_norm(d_ycat, hc, conv_ln_g, conv_ln_b, n_lat_rows)
    d_v, d_gate, g_conv_w8 = _conv_bwd_taps(dhc_pad, hh_pad, z_all, conv_w_full, n_lat_rows)

    dz_all = _dz_assemble(du[0], du[1], dy, d_skip, d_v, d_gate, n_lat)
    (d_a_all,) = _matmul("in_proj_dx", dz_all, w_in_full, "nt", (n_rows, D_MODEL, IN_COLS), (ROW_BLOCK, 512, 512),
                         [((n_rows, D_MODEL), F32)])
    (g_w_in_full,) = _matmul("in_proj_dw", a_all, dz_all, "tn", (D_MODEL, IN_COLS, n_rows), (512, 512, ROW_BLOCK),
                             [((D_MODEL, IN_COLS), F32)])
    grad_x, sums1 = _norm_bwd("norm1_bwd", xs, d_a_all, 0, norm1_g, sc1, res=d_h1, aux=mix)
    _, sums1c = _norm_bwd("norm1_bwd_ctx", cs, d_a_all, n_lat, norm1_g, modc[1:2])

    s1, s1c, s2 = sums1.sum(axis=1), sums1c.sum(axis=1), sums2.sum(axis=1)
    d_mod = jnp.concatenate([s1[0], s1[1], s1[3], s2[0], s2[1], s2[3]])
    d_modc = jnp.concatenate([s1c[0], s1c[1], jnp.zeros((4 * D_MODEL,), F32)])
    (dmod_g,) = _exchange("gather_dmod", [jnp.stack([d_mod, d_modc])], [True])
    dmod16 = jnp.concatenate([dmod_g[:, 0], dmod_g[:, 1]])
    dmod16_loc = lax.dynamic_slice(dmod16, (0, me * ada_cols), (16, ada_cols))
    cond_bwd = jnp.concatenate([c_all, jnp.broadcast_to(c_ctx[None], (NDEV, D_MODEL))])
    g_ada_w, g_c_ctx8 = _ada_bwd(cond_bwd, dmod16_loc, ada_w[0], c_ctx[None])

    small_parts = dict(
        c_ctx=g_c_ctx8[0], ada_b=d_mod + d_modc, norm1_g=s1[2] + s1c[2],
        s5_lam_re=jnp.stack(g_lam_re), s5_lam_im=jnp.stack(g_lam_im), s5_log_dt=jnp.stack(g_ldt),
        s5_b_re=jnp.stack(g_b_re), s5_b_im=jnp.stack(g_b_im), s5_c_re=jnp.stack(g_c_re), s5_c_im=jnp.stack(g_c_im),
        s5_d=dd8.sum(axis=0), conv_b=conv_sums[0].sum(axis=0), conv_ln_g=conv_sums[1].sum(axis=0),
        conv_ln_b=conv_sums[2].sum(axis=0), norm2_g=s2[2], final_g=d_final_g8.sum(axis=0))
    small_g = _pack_small([small_parts[n] for n in SMALL]).reshape(NDEV, SMALL_ROWS // NDEV, D_MODEL)
    g_w_in_parts = jnp.transpose(g_w_in_full.reshape(D_MODEL, NDEV, IN_COLS // NDEV), (1, 0, 2))
    g_conv_w_parts = jnp.transpose(g_conv_w8.sum(axis=1).reshape(CONV_K, NDEV, CONV_WIDTH // NDEV), (1, 0, 2))
    p_w_in, p_glu, p_conv_w, p_w_out, p_w1, p_w2, p_small = _exchange(
        "scatter_grads",
        [g_w_in_parts, g_glu.reshape(NDEV, S5_WIDTH // NDEV, S5_WIDTH), g_conv_w_parts,
         g_w_out.reshape(NDEV, D_MODEL // NDEV, D_MODEL), g_w1, g_w2.reshape(NDEV, D_FF // NDEV, D_MODEL), small_g],
        [False] * 7)
    (small_all,) = _exchange("gather_small", [_sum_parts(p_small)], [True])
    small_all = small_all.reshape(1, SMALL_ROWS, D_MODEL)

    res = {}
    big = dict(ada_w=g_ada_w[None], w_in=p_w_in, s5_w_glu=p_glu, conv_w=p_conv_w, w_out=p_w_out, mlp_w1=p_w1, mlp_w2=p_w2)
    for name, parts in big.items():
        outs = _adamw("adamw_" + name, weights[name][0], parts, mom1[name][0], mom2[name][0])
        res[name] = [o[None] for o in outs]
    small_like = [weights[n] for n in SMALL]
    outs = _adamw("adamw_small", _pack_small(small_like), small_all, _pack_small([mom1[n] for n in SMALL]),
                  _pack_small([mom2[n] for n in SMALL]))
    unpacked = [_unpack_small(o, small_like) for o in outs]
    for i, name in enumerate(SMALL):
        res[name] = [u[i] for u in unpacked]

    return (loss, grad_x[None], *[res[n][0] for n in order], *[res[n][1] for n in order],
            *[res[n][2] for n in order], *[res[n][3] for n in order])
```

```python
import functools

import jax
import jax.numpy as jnp
from jax import lax
from jax.experimental import pallas as pl
from jax.experimental.pallas import tpu as pltpu

F32 = jnp.float32
BF16 = jnp.bfloat16
MESH = pl.DeviceIdType.MESH
ANY = pl.BlockSpec(memory_space=pl.ANY)

NDEV = 8
D_MODEL = 1024
GRID_W = 64
S5_WIDTH = 512
S5_GROUP = 16
S5_GROUPS = 32
S5_STATE = 64
NSTATE = S5_GROUPS * S5_STATE
CONV_WIDTH = 512
CONV_K = 31
IN_COLS = S5_WIDTH + 2 * CONV_WIDTH
D_FF = 4 * D_MODEL
EPS_RMS = 1e-6
EPS_LN = 1e-5
ADAM_LR = 0.001
ADAM_B1 = 0.9
ADAM_B2 = 0.999
ADAM_EPS = 1e-08
ADAM_WD = 0.01
ADAM_STEP = 10

SUBLANES = 8
LANES = 128
ROW_BLOCK = 256
SCAN_LANES = 256
CONV_ROWS = 64
VMEM_LIMIT = 48 * 1024 * 1024
SMALL_ROWS = 320


def _params(sem=None):
    kw = dict(vmem_limit_bytes=VMEM_LIMIT)
    if sem is not None:
        kw["dimension_semantics"] = sem
    return pltpu.CompilerParams(**kw)


def _sds(shape, dtype=F32):
    return jax.ShapeDtypeStruct(tuple(shape), dtype)


def _fold8(x):
    return x.reshape(x.shape[0] // SUBLANES, SUBLANES, x.shape[1]).sum(axis=0)


def _sigmoid(x):
    return 1.0 / (1.0 + jnp.exp(-x))


def _silu(x):
    return x * _sigmoid(x)


def _dsilu(x):
    s = _sigmoid(x)
    return s * (1.0 + x * (1.0 - s))


_GELU_C = 0.7978845608028654


def _gelu(x):
    return 0.5 * x * (1.0 + jnp.tanh(_GELU_C * (x + 0.044715 * x * x * x)))


def _dgelu(x):
    t = jnp.tanh(_GELU_C * (x + 0.044715 * x * x * x))
    return 0.5 * (1.0 + t) + 0.5 * x * (1.0 - t * t) * _GELU_C * (1.0 + 3.0 * 0.044715 * x * x)


def _dot(a, b, mode):
    dims = {"nn": (((1,), (0,)), ((), ())), "nt": (((1,), (1,)), ((), ())), "tn": (((0,), (0,)), ((), ()))}[mode]
    return lax.dot_general(a, b, dims, preferred_element_type=F32)


def _exchange(name, srcs, gather):
    n = len(srcs)
    outs = [_sds(((NDEV,) + s.shape) if g else s.shape, s.dtype) for s, g in zip(srcs, gather)]

    def body(*refs):
        src, dst = refs[:n], refs[n:2 * n]
        send_sems, recv_sems, local_sems = refs[2 * n:]
        x, y, c = lax.axis_index("x"), lax.axis_index("y"), lax.axis_index("c")
        me = 4 * x + 2 * y + c
        started = []
        for a in range(n):
            def chunk(dest, a=a):
                return src[a] if gather[a] else src[a].at[dest]

            local = pltpu.make_async_copy(chunk(me), dst[a].at[me], local_sems.at[a])
            local.start()
            for k in range(1, NDEV):
                px = 1 - x if k & 4 else x
                py = 1 - y if k & 2 else y
                pc = 1 - c if k & 1 else c
                peer = 4 * px + 2 * py + pc
                copy = pltpu.make_async_remote_copy(
                    src_ref=chunk(peer), dst_ref=dst[a].at[me],
                    send_sem=send_sems.at[a * (NDEV - 1) + k - 1], recv_sem=recv_sems.at[a * (NDEV - 1) + k - 1],
                    device_id=(px, py, pc), device_id_type=MESH)
                copy.start()
                landing = pltpu.make_async_remote_copy(
                    src_ref=chunk(peer), dst_ref=dst[a].at[peer],
                    send_sem=send_sems.at[a * (NDEV - 1) + k - 1], recv_sem=recv_sems.at[a * (NDEV - 1) + k - 1],
                    device_id=(px, py, pc), device_id_type=MESH)
                started.append((copy, landing))
            started.append((local, None))
        for copy, landing in started:
            if landing is None:
                copy.wait()
            else:
                copy.wait_send()
                landing.wait_recv()

    return pl.pallas_call(
        body, name=name, out_shape=outs, in_specs=[ANY] * n, out_specs=[ANY] * n,
        scratch_shapes=[pltpu.SemaphoreType.DMA((n * (NDEV - 1),)), pltpu.SemaphoreType.DMA((n * (NDEV - 1),)),
                        pltpu.SemaphoreType.DMA((n,))],
    )(*srcs)


def _matmul(name, a, b, mode, mnk, tiles, outs, a_spec=None, b_spec=None, a_fn=None, a_extra=(),
            epi=None, epi_extra=(), out_specs=None):
    m_, n_, k_ = mnk
    tm, tn, tk = tiles
    nk = k_ // tk
    if a_spec is None:
        a_spec = (pl.BlockSpec((tk, tm), lambda i, j, k: (k, i)) if mode == "tn"
                  else pl.BlockSpec((tm, tk), lambda i, j, k: (i, k)))
    if b_spec is None:
        b_spec = (pl.BlockSpec((tn, tk), lambda i, j, k: (j, k)) if mode == "nt"
                  else pl.BlockSpec((tk, tn), lambda i, j, k: (k, j)))
    if out_specs is None:
        out_specs = [pl.BlockSpec((tm, tn), lambda i, j, k: (i, j)) for _ in outs]
    na, ne, no = len(a_extra), len(epi_extra), len(outs)

    def body(*refs):
        a_ref, b_ref = refs[0], refs[1]
        ax = refs[2:2 + na]
        ex = refs[2 + na:2 + na + ne]
        o = refs[2 + na + ne:2 + na + ne + no]

        def finish(res):
            res = epi(res, *[r[...] for r in ex]) if epi is not None else (res,)
            for ref, val in zip(o, res):
                ref[...] = val.astype(ref.dtype)

        at = a_ref[...]
        if a_fn is not None:
            at = a_fn(at, *[r[...] for r in ax])
        part = _dot(at.astype(BF16), b_ref[...].astype(BF16), mode)
        if nk == 1:
            finish(part)
            return
        acc = refs[-1]
        k = pl.program_id(2)

        @pl.when(k == 0)
        def _():
            acc[...] = part

        @pl.when(k > 0)
        def _():
            acc[...] += part

        @pl.when(k == nk - 1)
        def _():
            finish(acc[...])

    return pl.pallas_call(
        body, name=name, grid=(m_ // tm, n_ // tn, nk),
        in_specs=[a_spec, b_spec] + [s for _, s in a_extra] + [s for _, s in epi_extra],
        out_specs=out_specs, out_shape=[_sds(s, d) for s, d in outs],
        scratch_shapes=[pltpu.VMEM((tm, tn), F32)] if nk > 1 else [],
        compiler_params=_params(("parallel", "parallel", "arbitrary")),
    )(a, b, *[x for x, _ in a_extra], *[x for x, _ in epi_extra])


def _prenorm(name, x, ctx, gain, shsc):
    n_lat = x.shape[0] // ROW_BLOCK
    n_ctx = 0 if ctx is None else ctx.shape[0] // ROW_BLOCK
    d = x.shape[1]

    def norm(src, g_ref, m_ref, o_ref):
        xv = src[...]
        xh = xv * lax.rsqrt(jnp.mean(xv * xv, axis=-1, keepdims=True) + EPS_RMS)
        o_ref[...] = ((xh * g_ref[...]) * (1.0 + m_ref[1:2, :]) + m_ref[0:1, :]).astype(o_ref.dtype)

    def body(*refs):
        if ctx is None:
            x_ref, g_ref, m_ref, o_ref = refs
            norm(x_ref, g_ref, m_ref, o_ref)
        else:
            x_ref, c_ref, g_ref, m_ref, o_ref = refs
            i = pl.program_id(0)

            @pl.when(i < n_lat)
            def _():
                norm(x_ref, g_ref, m_ref, o_ref)

            @pl.when(i >= n_lat)
            def _():
                norm(c_ref, g_ref, m_ref, o_ref)

    in_specs = [pl.BlockSpec((ROW_BLOCK, d), lambda i: (jnp.minimum(i, n_lat - 1), 0))]
    args = [x]
    if ctx is not None:
        in_specs.append(pl.BlockSpec((ROW_BLOCK, d), lambda i: (jnp.maximum(i - n_lat, 0), 0)))
        args.append(ctx)
    in_specs += [pl.BlockSpec((1, d), lambda i: (0, 0)),
                 pl.BlockSpec((None, 2, d), lambda i: (jnp.minimum(i // n_lat, 1), 0, 0))]
    args += [gain, shsc]
    return pl.pallas_call(
        body, name=name, grid=(n_lat + n_ctx,), in_specs=in_specs,
        out_specs=pl.BlockSpec((ROW_BLOCK, d), lambda i: (i, 0)),
        out_shape=_sds(((n_lat + n_ctx) * ROW_BLOCK, d), BF16),
        compiler_params=_params(("parallel",)),
    )(*args)


def _norm_bwd(name, x, d_act, d_act_row0, gain, scale, res=None, aux=None, gate=None):
    rows, d = x.shape
    nb = rows // ROW_BLOCK
    has_res = res is not None
    has_gate = gate is not None

    def body(*refs):
        if has_gate:
            x_ref, da_ref, g_ref, sc_ref, r_ref, aux_ref, gate_ref, dx_ref, dm_ref, sums = refs
        elif has_res:
            x_ref, da_ref, g_ref, sc_ref, r_ref, aux_ref, dx_ref, sums = refs
        else:
            x_ref, da_ref, g_ref, sc_ref, sums = refs
        i = pl.program_id(0)

        @pl.when(i == 0)
        def _():
            sums[...] = jnp.zeros_like(sums)

        xv, da = x_ref[...], da_ref[...]
        rstd = lax.rsqrt(jnp.mean(xv * xv, axis=-1, keepdims=True) + EPS_RMS)
        xh = xv * rstd
        g = g_ref[...]
        dn = da * (1.0 + sc_ref[...])
        sums[0] += _fold8(da)
        sums[1] += _fold8(da * (xh * g))
        sums[2] += _fold8(dn * xh)
        if has_res:
            dxh = dn * g
            dx = rstd * (dxh - xh * jnp.mean(dxh * xh, axis=-1, keepdims=True))
            rv = r_ref[...]
            dx_ref[...] = rv + dx
            sums[3] += _fold8(rv * aux_ref[...])
            if has_gate:
                dm_ref[...] = ((rv + dx) * gate_ref[...]).astype(dm_ref.dtype)

    row = lambda i: (i, 0)
    vec = pl.BlockSpec((1, d), lambda i: (0, 0))
    in_specs = [pl.BlockSpec((ROW_BLOCK, d), row), pl.BlockSpec((ROW_BLOCK, d), lambda i: (i + d_act_row0, 0)), vec, vec]
    args = [x, d_act, gain, scale]
    out_shape = [_sds((4, SUBLANES, d))]
    out_specs = [pl.BlockSpec((4, SUBLANES, d), lambda i: (0, 0, 0))]
    if has_res:
        in_specs += [pl.BlockSpec((ROW_BLOCK, d), row), pl.BlockSpec((ROW_BLOCK, d), row)]
        args += [res, aux]
        if has_gate:
            in_specs.append(vec)
            args.append(gate)
            out_shape = [_sds((rows, d), BF16)] + out_shape
            out_specs = [pl.BlockSpec((ROW_BLOCK, d), row)] + out_specs
        out_shape = [_sds((rows, d))] + out_shape
        out_specs = [pl.BlockSpec((ROW_BLOCK, d), row)] + out_specs
    return pl.pallas_call(
        body, name=name, grid=(nb,), in_specs=in_specs, out_specs=out_specs, out_shape=out_shape,
        compiler_params=_params(("arbitrary",)),
    )(*args)


def _loss_head(h2, target, gain, gate):
    rows, d = h2.shape

    def body(h_ref, t_ref, g_ref, gate_ref, dh_ref, dm_ref, err_ref, dg_ref):
        i = pl.program_id(0)

        @pl.when(i == 0)
        def _():
            err_ref[...] = jnp.zeros_like(err_ref)
            dg_ref[...] = jnp.zeros_like(dg_ref)

        hv = h_ref[...]
        rstd = lax.rsqrt(jnp.mean(hv * hv, axis=-1, keepdims=True) + EPS_RMS)
        xh = hv * rstd
        g = g_ref[...]
        err = xh * g - t_ref[...]
        err_ref[...] += _fold8(err * err)
        dy = err * (1.0 / d)
        dg_ref[...] += _fold8(dy * xh)
        dxh = dy * g
        dh = rstd * (dxh - xh * jnp.mean(dxh * xh, axis=-1, keepdims=True))
        dh_ref[...] = dh
        dm_ref[...] = (dh * gate_ref[...]).astype(dm_ref.dtype)

    row = pl.BlockSpec((ROW_BLOCK, d), lambda i: (i, 0))
    acc = pl.BlockSpec((SUBLANES, d), lambda i: (0, 0))
    vec = pl.BlockSpec((1, d), lambda i: (0, 0))
    return pl.pallas_call(
        body, name="loss_head", grid=(rows // ROW_BLOCK,),
        in_specs=[row, row, vec, vec], out_specs=[row, row, acc, acc],
        out_shape=[_sds((rows, d)), _sds((rows, d), BF16), _sds((SUBLANES, d)), _sds((SUBLANES, d))],
        compiler_params=_params(("arbitrary",)),
    )(h2, target, gain, gate)


def _ada_fwd(cond16, ada_w_loc, ada_b_loc):
    cols = ada_w_loc.shape[1]

    def body(c_ref, w_ref, b_ref, o_ref):
        s = _silu(c_ref[...]).astype(BF16)
        o_ref[...] = _dot(s, w_ref[...].astype(BF16), "nn") + b_ref[...]

    return pl.pallas_call(body, name="ada_fwd", out_shape=_sds((16, cols)), compiler_params=_params())(
        cond16, ada_w_loc, ada_b_loc)


def _ada_bwd(cond16, dmod16, ada_w_loc, c_ctx_row):
    k_, cols = ada_w_loc.shape

    def body(c_ref, dm_ref, w_ref, cc_ref, gw_ref, gc_ref):
        s = _silu(c_ref[...]).astype(BF16)
        dm = dm_ref[...]
        gw_ref[...] = _dot(s, dm.astype(BF16), "tn")
        dmc = jnp.sum(dm[8:16, :], axis=0, keepdims=True)
        dmc8 = jnp.broadcast_to(dmc, (SUBLANES, cols)).astype(BF16)
        ds = _dot(dmc8, w_ref[...].astype(BF16), "nt")
        row = lax.broadcasted_iota(jnp.int32, ds.shape, 0)
        gc_ref[...] = jnp.where(row == 0, ds * _dsilu(cc_ref[...]), 0.0)

    return pl.pallas_call(body, name="ada_bwd", out_shape=[_sds((k_, cols)), _sds((SUBLANES, k_))],
                          compiler_params=_params())(cond16, dmod16, ada_w_loc, c_ctx_row)


def _cmul(a, b):
    return a[0] * b[0] - a[1] * b[1], a[0] * b[1] + a[1] * b[0]


def _disc(lam_re, lam_im, ldt):
    dt = jnp.exp(ldt)
    mag = jnp.exp(lam_re * dt)
    th = lam_im * dt
    a_re, a_im = mag * jnp.cos(th), mag * jnp.sin(th)
    den = lam_re * lam_re + lam_im * lam_im
    n_re = a_re - 1.0
    f_re = (n_re * lam_re + a_im * lam_im) / den
    f_im = (a_im * lam_re - n_re * lam_im) / den
    return dt, mag, th, a_re, a_im, den, n_re, f_re, f_im


def _block_diag_mask(shape):
    row = lax.broadcasted_iota(jnp.int32, shape, 0)
    col = lax.broadcasted_iota(jnp.int32, shape, 1)
    return lax.shift_right_logical(row, 4) == lax.shift_right_logical(col, 6)


def _s5_discretise(name, ascending, lam_re, lam_im, ldt, bt_re, bt_im, ct_re, ct_im):
    def tables(pw, asc, sign):
        row = lax.broadcasted_iota(jnp.int32, (SUBLANES, NSTATE), 0)
        ap_re = jnp.zeros((SUBLANES, NSTATE), F32)
        ap_im = jnp.zeros((SUBLANES, NSTATE), F32)
        for r in range(SUBLANES):
            p = pw[r] if asc else pw[SUBLANES - 1 - r]
            ap_re = jnp.where(row == r, p[0], ap_re)
            ap_im = jnp.where(row == r, sign * p[1], ap_im)
        full = lambda v: jnp.broadcast_to(v, (SUBLANES, NSTATE))
        return [full(pw[0][0]), full(sign * pw[0][1]), full(pw[1][0]), full(sign * pw[1][1]),
                full(pw[3][0]), full(sign * pw[3][1]), ap_re, ap_im]

    def body(lr_ref, li_ref, ldt_ref, br_ref, bi_ref, cr_ref, ci_ref, bb_ref, tab_ref, adj_ref, bm_ref, cm_ref):
        _, _, _, a_re, a_im, _, _, f_re, f_im = _disc(lr_ref[...], li_ref[...], ldt_ref[...])
        bre, bim = br_ref[...], bi_ref[...]
        bb_re = f_re * bre - f_im * bim
        bb_im = f_re * bim + f_im * bre
        bb_ref[0:S5_GROUP, :] = bb_re
        bb_ref[S5_GROUP:2 * S5_GROUP, :] = bb_im
        pw = [(a_re, a_im)]
        for _ in range(SUBLANES - 1):
            pw.append(_cmul(pw[-1], (a_re, a_im)))
        for t, v in enumerate(tables(pw, ascending, 1.0)):
            tab_ref[t] = v
        for t, v in enumerate(tables(pw, not ascending, -1.0)):
            adj_ref[t] = v
        mask = _block_diag_mask((S5_WIDTH, NSTATE))
        tile = lambda v: jnp.broadcast_to(v[None], (S5_GROUPS, S5_GROUP, NSTATE)).reshape(S5_WIDTH, NSTATE)
        bm_ref[:, 0:NSTATE] = jnp.where(mask, tile(bb_re), 0.0).astype(BF16)
        bm_ref[:, NSTATE:2 * NSTATE] = jnp.where(mask, tile(bb_im), 0.0).astype(BF16)
        cm_ref[:, 0:NSTATE] = jnp.where(mask, cr_ref[...], 0.0).astype(BF16)
        cm_ref[:, NSTATE:2 * NSTATE] = jnp.where(mask, -ci_ref[...], 0.0).astype(BF16)

    return pl.pallas_call(
        body, name=name,
        out_shape=[_sds((2 * S5_GROUP, NSTATE)), _sds((8, SUBLANES, NSTATE)), _sds((8, SUBLANES, NSTATE)),
                   _sds((S5_WIDTH, 2 * NSTATE), BF16), _sds((S5_WIDTH, 2 * NSTATE), BF16)],
        compiler_params=_params(),
    )(lam_re, lam_im, ldt, bt_re, bt_im, ct_re, ct_im)


def _s5_discretise_bwd(name, lam_re, lam_im, ldt, bt_re, bt_im, d_abar8, d_bbar):
    def body(lr_ref, li_ref, ldt_ref, br_ref, bi_ref, da_ref, db_ref, dl_ref, dbt_ref):
        lam_re, lam_im = lr_ref[...], li_ref[...]
        dt, mag, _, a_re, a_im, den, n_re, f_re, f_im = _disc(lam_re, lam_im, ldt_ref[...])
        bre, bim = br_ref[...], bi_ref[...]
        dbr, dbi = db_ref[0:S5_GROUP, :], db_ref[S5_GROUP:2 * S5_GROUP, :]
        dbt_ref[0:S5_GROUP, :] = f_re * dbr + f_im * dbi
        dbt_ref[S5_GROUP:2 * S5_GROUP, :] = f_re * dbi - f_im * dbr
        df_re = jnp.sum(bre * dbr + bim * dbi, axis=0, keepdims=True)
        df_im = jnp.sum(bre * dbi - bim * dbr, axis=0, keepdims=True)
        da = da_ref[...]
        da_re = jnp.sum(da[:, 0:NSTATE], axis=0, keepdims=True)
        da_im = jnp.sum(da[:, NSTATE:2 * NSTATE], axis=0, keepdims=True)
        da_re = da_re + (df_re * lam_re - df_im * lam_im) / den
        da_im = da_im + (df_re * lam_im + df_im * lam_re) / den
        ff = (f_re * df_re + f_im * df_im) * 2.0 / den
        d_lr = (df_re * n_re + df_im * a_im) / den - ff * lam_re
        d_li = (df_re * a_im - df_im * n_re) / den - ff * lam_im
        d_mag = (da_re * a_re + da_im * a_im) / mag
        d_th = da_im * a_re - da_re * a_im
        d_lr = d_lr + d_mag * mag * dt
        d_li = d_li + d_th * dt
        d_ldt = (d_mag * mag * lam_re + d_th * lam_im) * dt
        row = lax.broadcasted_iota(jnp.int32, (SUBLANES, NSTATE), 0)
        dl_ref[...] = jnp.where(row == 0, d_lr, jnp.where(row == 1, d_li, jnp.where(row == 2, d_ldt, 0.0)))

    return pl.pallas_call(
        body, name=name, out_shape=[_sds((SUBLANES, NSTATE)), _sds((2 * S5_GROUP, NSTATE))],
        compiler_params=_params(),
    )(lam_re, lam_im, ldt, bt_re, bt_im, d_abar8, d_bbar)


def _scan_chunk(x_ref, out_ref, tab_ref, carry_re, carry_im, ascending, pair_ref=None, acc_ref=None):
    rows = x_ref.shape[0]
    nblk = rows // SUBLANES
    row = lax.broadcasted_iota(jnp.int32, (SUBLANES, SCAN_LANES), 0)
    edge = (SUBLANES - 1) if ascending else 0

    def shifted(v, k):
        if ascending:
            return jnp.where(row >= k, pltpu.roll(v, k, 0), 0.0)
        return jnp.where(row < SUBLANES - k, pltpu.roll(v, SUBLANES - k, 0), 0.0)

    for j in range(NSTATE // SCAN_LANES):
        re_l = pl.ds(j * SCAN_LANES, SCAN_LANES)
        im_l = pl.ds(NSTATE + j * SCAN_LANES, SCAN_LANES)
        tabs = [tab_ref[t, :, re_l] for t in range(8)]

        def blk(b, carry):
            cr, ci = carry[0], carry[1]
            r0 = pl.multiple_of((b if ascending else nblk - 1 - b) * SUBLANES, SUBLANES)
            rs = pl.ds(r0, SUBLANES)
            xr, xi = x_ref[rs, re_l], x_ref[rs, im_l]
            for t, k in ((0, 1), (2, 2), (4, 4)):
                sr, si = shifted(xr, k), shifted(xi, k)
                xr, xi = xr + (tabs[t] * sr - tabs[t + 1] * si), xi + (tabs[t] * si + tabs[t + 1] * sr)
            xr = xr + (tabs[6] * cr - tabs[7] * ci)
            xi = xi + (tabs[6] * ci + tabs[7] * cr)
            out_ref[rs, re_l] = xr.astype(out_ref.dtype)
            out_ref[rs, im_l] = xi.astype(out_ref.dtype)
            new = (jnp.broadcast_to(xr[edge:edge + 1, :], xr.shape), jnp.broadcast_to(xi[edge:edge + 1, :], xi.shape))
            if pair_ref is None:
                return new
            if ascending:
                pr = jnp.where(row >= 1, pltpu.roll(xr, 1, 0), cr)
                pi = jnp.where(row >= 1, pltpu.roll(xi, 1, 0), ci)
            else:
                pr = jnp.where(row < SUBLANES - 1, pltpu.roll(xr, SUBLANES - 1, 0), cr)
                pi = jnp.where(row < SUBLANES - 1, pltpu.roll(xi, SUBLANES - 1, 0), ci)
            sr, si = pair_ref[rs, re_l], pair_ref[rs, im_l]
            return new + (carry[2] + (pr * sr + pi * si), carry[3] + (pi * sr - pr * si))

        init = (carry_re[:, re_l], carry_im[:, re_l])
        if pair_ref is not None:
            init = init + (jnp.zeros((SUBLANES, SCAN_LANES), F32), jnp.zeros((SUBLANES, SCAN_LANES), F32))
        fin = lax.fori_loop(0, nblk, blk, init)
        carry_re[:, re_l] = fin[0]
        carry_im[:, re_l] = fin[1]
        if pair_ref is not None:
            acc_ref[:, re_l] += fin[2]
            acc_ref[:, im_l] += fin[3]


def _scan_block_index(i, n_lat, ctx_first_then_ascending):
    if ctx_first_then_ascending:
        return jnp.where(i == 0, n_lat, i - 1)
    return jnp.where(i == 0, n_lat, n_lat - i)


def _s5_scan_fwd(name, ascending, z_all, bmat, cmat, tab):
    rows = z_all.shape[0]
    nb = rows // ROW_BLOCK
    n_lat = nb - 1

    def body(u_ref, bm_ref, cm_ref, tab_ref, s_ref, y_ref, bu, carry_re, carry_im):
        @pl.when(pl.program_id(0) == 0)
        def _():
            carry_re[...] = jnp.zeros_like(carry_re)
            carry_im[...] = jnp.zeros_like(carry_im)

        bu[...] = _dot(u_ref[...].astype(BF16), bm_ref[...], "nn")
        _scan_chunk(bu, s_ref, tab_ref, carry_re, carry_im, ascending)
        y_ref[...] = _dot(s_ref[...].astype(BF16), cm_ref[...], "nt")

    blk = lambda i: (_scan_block_index(i, n_lat, ascending), 0)
    full = lambda shape: pl.BlockSpec(shape, lambda i: (0,) * len(shape))
    return pl.pallas_call(
        body, name=name, grid=(nb,),
        in_specs=[pl.BlockSpec((ROW_BLOCK, S5_WIDTH), blk), full((S5_WIDTH, 2 * NSTATE)), full((S5_WIDTH, 2 * NSTATE)),
                  full((8, SUBLANES, NSTATE))],
        out_specs=[pl.BlockSpec((ROW_BLOCK, 2 * NSTATE), blk), pl.BlockSpec((ROW_BLOCK, S5_WIDTH), blk)],
        out_shape=[_sds((rows, 2 * NSTATE)), _sds((rows, S5_WIDTH))],
        scratch_shapes=[pltpu.VMEM((ROW_BLOCK, 2 * NSTATE), F32), pltpu.VMEM((SUBLANES, NSTATE), F32),
                        pltpu.VMEM((SUBLANES, NSTATE), F32)],
        compiler_params=_params(("arbitrary",)),
    )(z_all, bmat, cmat, tab)


def _s5_scan_bwd(name, ascending, dy, states, cmat, adj):
    rows = states.shape[0]
    nb = rows // ROW_BLOCK
    n_lat = nb - 1

    def block_index(i):
        if ascending:
            return jnp.where(i == nb - 1, n_lat, n_lat - 1 - i)
        return jnp.where(i == nb - 1, n_lat, i)

    def body(dy_ref, s_ref, cm_ref, adj_ref, g_ref, da_ref, q, carry_re, carry_im):
        i = pl.program_id(0)

        @pl.when(i == 0)
        def _():
            carry_re[...] = jnp.zeros_like(carry_re)
            carry_im[...] = jnp.zeros_like(carry_im)
            da_ref[...] = jnp.zeros_like(da_ref)

        @pl.when(i < nb - 1)
        def _():
            q[...] = _dot(dy_ref[...].astype(BF16), cm_ref[...], "nn")

        @pl.when(i == nb - 1)
        def _():
            q[...] = jnp.zeros_like(q)

        _scan_chunk(q, g_ref, adj_ref, carry_re, carry_im, not ascending, pair_ref=s_ref, acc_ref=da_ref)

    blk = lambda i: (block_index(i), 0)
    full = lambda shape: pl.BlockSpec(shape, lambda i: (0,) * len(shape))
    return pl.pallas_call(
        body, name=name, grid=(nb,),
        in_specs=[pl.BlockSpec((ROW_BLOCK, S5_WIDTH), lambda i: (jnp.minimum(block_index(i), n_lat - 1), 0)),
                  pl.BlockSpec((ROW_BLOCK, 2 * NSTATE), blk), full((S5_WIDTH, 2 * NSTATE)), full((8, SUBLANES, NSTATE))],
        out_specs=[pl.BlockSpec((ROW_BLOCK, 2 * NSTATE), blk), full((SUBLANES, 2 * NSTATE))],
        out_shape=[_sds((rows, 2 * NSTATE), BF16), _sds((SUBLANES, 2 * NSTATE))],
        scratch_shapes=[pltpu.VMEM((ROW_BLOCK, 2 * NSTATE), F32), pltpu.VMEM((SUBLANES, NSTATE), F32),
                        pltpu.VMEM((SUBLANES, NSTATE), F32)],
        compiler_params=_params(("arbitrary",)),
    )(dy, states, cmat, adj)


def _glu_fwd(z_all, y0, y1, d_skip, w_glu, n_rows):
    def body(u_ref, y0_ref, y1_ref, d_ref, w_ref, o_ref):
        y = d_ref[...] * u_ref[...] + y0_ref[...] + y1_ref[...]
        g = _gelu(y)
        t = _dot(g.astype(BF16), w_ref[...], "nn")
        o_ref[...] = (g * _sigmoid(t)).astype(o_ref.dtype)

    row = pl.BlockSpec((ROW_BLOCK, S5_WIDTH), lambda i: (i, 0))
    return pl.pallas_call(
        body, name="glu_fwd", grid=(n_rows // ROW_BLOCK,),
        in_specs=[row, row, row, pl.BlockSpec((1, S5_WIDTH), lambda i: (0, 0)),
                  pl.BlockSpec((S5_WIDTH, S5_WIDTH), lambda i: (0, 0))],
        out_specs=row, out_shape=_sds((n_rows, S5_WIDTH), BF16), compiler_params=_params(("parallel",)),
    )(z_all, y0, y1, d_skip, w_glu)


def _glu_bwd(d_ycat, z_all, y0, y1, d_skip, w_glu, n_rows):
    def body(do_ref, u_ref, y0_ref, y1_ref, d_ref, w_ref, dy_ref, dw_ref, dd_ref):
        @pl.when(pl.program_id(0) == 0)
        def _():
            dw_ref[...] = jnp.zeros_like(dw_ref)
            dd_ref[...] = jnp.zeros_like(dd_ref)

        u = u_ref[...]
        y = d_ref[...] * u + y0_ref[...] + y1_ref[...]
        g = _gelu(y)
        gb = g.astype(BF16)
        w = w_ref[...]
        sg = _sigmoid(_dot(gb, w, "nn"))
        do = do_ref[...]
        dt = do * g * sg * (1.0 - sg)
        dtb = dt.astype(BF16)
        dg = do * sg + _dot(dtb, w, "nt")
        dy = dg * _dgelu(y)
        dy_ref[...] = dy
        dw_ref[...] += _dot(gb, dtb, "tn")
        dd_ref[...] += _fold8(dy * u)

    row = pl.BlockSpec((ROW_BLOCK, S5_WIDTH), lambda i: (i, 0))
    sq = pl.BlockSpec((S5_WIDTH, S5_WIDTH), lambda i: (0, 0))
    return pl.pallas_call(
        body, name="glu_bwd", grid=(n_rows // ROW_BLOCK,),
        in_specs=[row, row, row, row, pl.BlockSpec((1, S5_WIDTH), lambda i: (0, 0)), sq],
        out_specs=[row, sq, pl.BlockSpec((SUBLANES, S5_WIDTH), lambda i: (0, 0))],
        out_shape=[_sds((n_rows, S5_WIDTH)), _sds((S5_WIDTH, S5_WIDTH)), _sds((SUBLANES, S5_WIDTH))],
        compiler_params=_params(("arbitrary",)),
    )(d_ycat, z_all, y0, y1, d_skip, w_glu)


CONV_HALF = CONV_K // 2


def _conv_block(n_rows):
    blk = min(1024, n_rows)
    assert blk >= CONV_HALF * GRID_W and n_rows % blk == 0
    return blk


def _conv_gate(z_all, n_rows):
    blk = _conv_block(n_rows)
    nb = n_rows // blk

    def body(v_ref, g_ref, o_ref):
        i = pl.program_id(0)
        inside = jnp.logical_and(i >= 1, i <= nb)

        @pl.when(inside)
        def _():
            o_ref[...] = v_ref[...] * _sigmoid(g_ref[...])

        @pl.when(jnp.logical_not(inside))
        def _():
            o_ref[...] = jnp.zeros_like(o_ref)

    src = lambda col: pl.BlockSpec((blk, CONV_WIDTH), lambda i: (jnp.clip(i - 1, 0, nb - 1), col))
    return pl.pallas_call(
        body, name="conv_gate", grid=(nb + 2,), in_specs=[src(1), src(2)],
        out_specs=pl.BlockSpec((blk, CONV_WIDTH), lambda i: (i, 0)),
        out_shape=_sds(((nb + 2) * blk, CONV_WIDTH)), compiler_params=_params(("parallel",)),
    )(z_all, z_all)


def _load_window(pad_ref, win, sem, blk):
    start = pl.multiple_of(pl.program_id(0) * blk, blk)
    copy = pltpu.make_async_copy(pad_ref.at[pl.ds(start, 3 * blk), :], win, sem)
    copy.start()
    copy.wait()


def _conv_fwd(hh_pad, w, b, ln_g, ln_b, n_rows):
    blk = _conv_block(n_rows)

    def body(hh_ref, w_ref, b_ref, g_ref, lb_ref, hc_ref, y_ref, win, sem):
        _load_window(hh_ref, win, sem, blk)

        def tile(t, _):
            r0 = pl.multiple_of(t * CONV_ROWS, CONV_ROWS)
            acc = jnp.zeros((CONV_ROWS, CONV_WIDTH), F32)
            for k in range(CONV_K):
                acc = acc + w_ref[k:k + 1, :] * win[pl.ds(r0 + blk + (k - CONV_HALF) * GRID_W, CONV_ROWS), :]
            hc = acc + b_ref[...]
            hc_ref[pl.ds(r0, CONV_ROWS), :] = hc
            mu = jnp.mean(hc, axis=-1, keepdims=True)
            xc = hc - mu
            ln = xc * lax.rsqrt(jnp.mean(xc * xc, axis=-1, keepdims=True) + EPS_LN) * g_ref[...] + lb_ref[...]
            y_ref[pl.ds(r0, CONV_ROWS), :] = _silu(ln).astype(y_ref.dtype)
            return 0

        lax.fori_loop(0, blk // CONV_ROWS, tile, 0)

    vec = pl.BlockSpec((1, CONV_WIDTH), lambda i: (0, 0))
    row = pl.BlockSpec((blk, CONV_WIDTH), lambda i: (i, 0))
    return pl.pallas_call(
        body, name="conv_fwd", grid=(n_rows // blk,),
        in_specs=[ANY, pl.BlockSpec((CONV_K, CONV_WIDTH), lambda i: (0, 0)), vec, vec, vec],
        out_specs=[row, row], out_shape=[_sds((n_rows, CONV_WIDTH)), _sds((n_rows, CONV_WIDTH), BF16)],
        scratch_shapes=[pltpu.VMEM((3 * blk, CONV_WIDTH), F32), pltpu.SemaphoreType.DMA],
        compiler_params=_params(("arbitrary",)),
    )(hh_pad, w, b, ln_g, ln_b)


def _conv_bwd_norm(d_ycat, hc, ln_g, ln_b, n_rows):
    blk = _conv_block(n_rows)
    nb = n_rows // blk

    def body(dy_ref, hc_ref, g_ref, lb_ref, o_ref, sums):
        i = pl.program_id(0)

        @pl.when(i == 0)
        def _():
            sums[...] = jnp.zeros_like(sums)

        inside = jnp.logical_and(i >= 1, i <= nb)

        @pl.when(inside)
        def _():
            hcv = hc_ref[...]
            mu = jnp.mean(hcv, axis=-1, keepdims=True)
            xc = hcv - mu
            rstd = lax.rsqrt(jnp.mean(xc * xc, axis=-1, keepdims=True) + EPS_LN)
            xh = xc * rstd
            g = g_ref[...]
            dln = dy_ref[...] * _dsilu(xh * g + lb_ref[...])
            dxh = dln * g
            dhc = rstd * (dxh - jnp.mean(dxh, axis=-1, keepdims=True) - xh * jnp.mean(dxh * xh, axis=-1, keepdims=True))
            o_ref[...] = dhc
            sums[0] += _fold8(dhc)
            sums[1] += _fold8(dln * xh)
            sums[2] += _fold8(dln)

        @pl.when(jnp.logical_not(inside))
        def _():
            o_ref[...] = jnp.zeros_like(o_ref)

    vec = pl.BlockSpec((1, CONV_WIDTH), lambda i: (0, 0))
    return pl.pallas_call(
        body, name="conv_bwd_norm", grid=(nb + 2,),
        in_specs=[pl.BlockSpec((blk, CONV_WIDTH), lambda i: (jnp.clip(i - 1, 0, nb - 1), 1)),
                  pl.BlockSpec((blk, CONV_WIDTH), lambda i: (jnp.clip(i - 1, 0, nb - 1), 0)), vec, vec],
        out_specs=[pl.BlockSpec((blk, CONV_WIDTH), lambda i: (i, 0)),
                   pl.BlockSpec((3, SUBLANES, CONV_WIDTH), lambda i: (0, 0, 0))],
        out_shape=[_sds(((nb + 2) * blk, CONV_WIDTH)), _sds((3, SUBLANES, CONV_WIDTH))],
        compiler_params=_params(("arbitrary",)),
    )(d_ycat, hc, ln_g, ln_b)


def _conv_bwd_taps(dhc_pad, hh_pad, z_all, w, n_rows):
    blk = _conv_block(n_rows)

    def body(dhc_ref, hh_ref, v_ref, g_ref, w_ref, dv_ref, dg_ref, dw_ref, dwin, hwin, sems):
        @pl.when(pl.program_id(0) == 0)
        def _():
            dw_ref[...] = jnp.zeros_like(dw_ref)

        _load_window(dhc_ref, dwin, sems.at[0], blk)
        _load_window(hh_ref, hwin, sems.at[1], blk)

        def tile(t, _):
            r0 = pl.multiple_of(t * CONV_ROWS, CONV_ROWS)
            dh = dwin[pl.ds(r0 + blk, CONV_ROWS), :]
            acc = jnp.zeros((CONV_ROWS, CONV_WIDTH), F32)
            for k in range(CONV_K):
                off = (k - CONV_HALF) * GRID_W
                acc = acc + w_ref[k:k + 1, :] * dwin[pl.ds(r0 + blk - off, CONV_ROWS), :]
                dw_ref[k] += _fold8(dh * hwin[pl.ds(r0 + blk + off, CONV_ROWS), :])
            rs = pl.ds(r0, CONV_ROWS)
            sg = _sigmoid(g_ref[rs, :])
            vv = v_ref[rs, :]
            dv_ref[rs, :] = acc * sg
            dg_ref[rs, :] = acc * vv * sg * (1.0 - sg)
            return 0

        lax.fori_loop(0, blk // CONV_ROWS, tile, 0)

    row = pl.BlockSpec((blk, CONV_WIDTH), lambda i: (i, 0))
    return pl.pallas_call(
        body, name="conv_bwd_taps", grid=(n_rows // blk,),
        in_specs=[ANY, ANY,
            pl.BlockSpec((blk, CONV_WIDTH), lambda i: (i, 1)), pl.BlockSpec((blk, CONV_WIDTH), lambda i: (i, 2)),
            pl.BlockSpec((CONV_K, CONV_WIDTH), lambda i: (0, 0))],
        out_specs=[row, row, pl.BlockSpec((CONV_K, SUBLANES, CONV_WIDTH), lambda i: (0, 0, 0))],
        out_shape=[_sds((n_rows, CONV_WIDTH)), _sds((n_rows, CONV_WIDTH)), _sds((CONV_K, SUBLANES, CONV_WIDTH))],
        scratch_shapes=[pltpu.VMEM((3 * blk, CONV_WIDTH), F32), pltpu.VMEM((3 * blk, CONV_WIDTH), F32),
                        pltpu.SemaphoreType.DMA((2,))],
        compiler_params=_params(("arbitrary",)),
    )(dhc_pad, hh_pad, z_all, z_all, w)


def _dz_assemble(du0, du1, dy, d_skip, dv, dgate, n_lat):
    rows = du0.shape[0]
    nb = rows // ROW_BLOCK

    def body(a_ref, b_ref, dy_ref, d_ref, dv_ref, dg_ref, o_ref):
        i, j = pl.program_id(0), pl.program_id(1)
        lat = i < n_lat

        @pl.when(jnp.logical_and(j == 0, lat))
        def _():
            o_ref[...] = (a_ref[...] + b_ref[...] + dy_ref[...] * d_ref[...]).astype(o_ref.dtype)

        @pl.when(jnp.logical_and(j == 0, jnp.logical_not(lat)))
        def _():
            o_ref[...] = (a_ref[...] + b_ref[...]).astype(o_ref.dtype)

        @pl.when(jnp.logical_and(j == 1, lat))
        def _():
            o_ref[...] = dv_ref[...].astype(o_ref.dtype)

        @pl.when(jnp.logical_and(j == 2, lat))
        def _():
            o_ref[...] = dg_ref[...].astype(o_ref.dtype)

        @pl.when(jnp.logical_and(j >= 1, jnp.logical_not(lat)))
        def _():
            o_ref[...] = jnp.zeros_like(o_ref)

    all_rows = pl.BlockSpec((ROW_BLOCK, S5_WIDTH), lambda i, j: (i, 0))
    lat_rows = pl.BlockSpec((ROW_BLOCK, S5_WIDTH), lambda i, j: (jnp.minimum(i, n_lat - 1), 0))
    return pl.pallas_call(
        body, name="dz_assemble", grid=(nb, 3),
        in_specs=[all_rows, all_rows, lat_rows, pl.BlockSpec((1, S5_WIDTH), lambda i, j: (0, 0)), lat_rows, lat_rows],
        out_specs=pl.BlockSpec((ROW_BLOCK, S5_WIDTH), lambda i, j: (i, j)),
        out_shape=_sds((rows, IN_COLS), BF16), compiler_params=_params(("parallel", "parallel")),
    )(du0, du1, dy, d_skip, dv, dgate)


def _sum_parts(parts):
    _, r, c = parts.shape

    def body(p_ref, o_ref):
        acc = p_ref[0]
        for q in range(1, NDEV):
            acc = acc + p_ref[q]
        o_ref[...] = acc

    return pl.pallas_call(body, name="sum_parts", out_shape=_sds((r, c)), compiler_params=_params())(parts)


def _row_tile(r, c):
    best = r
    for t in (1024, 512, 256, 128, 64, 32, 16, 8):
        if r % t == 0 and t * c <= 128 * 1024:
            return t
    return best


def _adamw(name, w, gparts, m, v):
    r, c = w.shape
    np_ = gparts.shape[0]
    tr = _row_tile(r, c)

    def body(w_ref, g_ref, m_ref, v_ref, go_ref, d_ref, mo_ref, vo_ref):
        g = g_ref[0]
        for q in range(1, np_):
            g = g + g_ref[q]
        m2 = ADAM_B1 * m_ref[...] + (1.0 - ADAM_B1) * g
        v2 = ADAM_B2 * v_ref[...] + (1.0 - ADAM_B2) * jnp.square(g)
        m_hat = m2 / (1.0 - ADAM_B1 ** ADAM_STEP)
        v_hat = v2 / (1.0 - ADAM_B2 ** ADAM_STEP)
        go_ref[...] = g
        d_ref[...] = -ADAM_LR * (m_hat / (jnp.sqrt(v_hat) + ADAM_EPS) + ADAM_WD * w_ref[...])
        mo_ref[...] = m2
        vo_ref[...] = v2

    row = pl.BlockSpec((tr, c), lambda i: (i, 0))
    return pl.pallas_call(
        body, name=name, grid=(r // tr,),
        in_specs=[row, pl.BlockSpec((np_, tr, c), lambda i: (0, i, 0)), row, row],
        out_specs=[row] * 4, out_shape=[_sds((r, c))] * 4, compiler_params=_params(("parallel",)),
    )(w, gparts, m, v)


SMALL = ["c_ctx", "ada_b", "norm1_g", "s5_lam_re", "s5_lam_im", "s5_log_dt", "s5_b_re", "s5_b_im", "s5_c_re",
         "s5_c_im", "s5_d", "conv_b", "conv_ln_g", "conv_ln_b", "norm2_g", "final_g"]


def _pack_small(parts):
    flat = jnp.concatenate([p.reshape(-1).astype(F32) for p in parts])
    return jnp.pad(flat, (0, SMALL_ROWS * D_MODEL - flat.shape[0])).reshape(SMALL_ROWS, D_MODEL)


def _unpack_small(packed, like):
    flat = packed.reshape(-1)
    out, off = [], 0
    for ref in like:
        out.append(flat[off:off + ref.size].reshape(ref.shape))
        off += ref.size
    return out


def kernel(x, c, ctx, c_ctx, ada_w, ada_b, norm1_g, w_in, s5_lam_re, s5_lam_im, s5_log_dt, s5_b_re, s5_b_im, s5_c_re, s5_c_im, s5_d, s5_w_glu, conv_w, conv_b, conv_ln_g, conv_ln_b, w_out, norm2_g, mlp_w1, mlp_w2, final_g, loss_target, m_c_ctx, m_ada_w, m_ada_b, m_norm1_g, m_w_in, m_s5_lam_re, m_s5_lam_im, m_s5_log_dt, m_s5_b_re, m_s5_b_im, m_s5_c_re, m_s5_c_im, m_s5_d, m_s5_w_glu, m_conv_w, m_conv_b, m_conv_ln_g, m_conv_ln_b, m_w_out, m_norm2_g, m_mlp_w1, m_mlp_w2, m_final_g, v_c_ctx, v_ada_w, v_ada_b, v_norm1_g, v_w_in, v_s5_lam_re, v_s5_lam_im, v_s5_log_dt, v_s5_b_re, v_s5_b_im, v_s5_c_re, v_s5_c_im, v_s5_d, v_s5_w_glu, v_conv_w, v_conv_b, v_conv_ln_g, v_conv_ln_b, v_w_out, v_norm2_g, v_mlp_w1, v_mlp_w2, v_final_g):
    weights = dict(c_ctx=c_ctx, ada_w=ada_w, ada_b=ada_b, norm1_g=norm1_g, w_in=w_in, s5_lam_re=s5_lam_re, s5_lam_im=s5_lam_im, s5_log_dt=s5_log_dt, s5_b_re=s5_b_re, s5_b_im=s5_b_im, s5_c_re=s5_c_re, s5_c_im=s5_c_im, s5_d=s5_d, s5_w_glu=s5_w_glu, conv_w=conv_w, conv_b=conv_b, conv_ln_g=conv_ln_g, conv_ln_b=conv_ln_b, w_out=w_out, norm2_g=norm2_g, mlp_w1=mlp_w1, mlp_w2=mlp_w2, final_g=final_g)
    mom1 = dict(c_ctx=m_c_ctx, ada_w=m_ada_w, ada_b=m_ada_b, norm1_g=m_norm1_g, w_in=m_w_in, s5_lam_re=m_s5_lam_re, s5_lam_im=m_s5_lam_im, s5_log_dt=m_s5_log_dt, s5_b_re=m_s5_b_re, s5_b_im=m_s5_b_im, s5_c_re=m_s5_c_re, s5_c_im=m_s5_c_im, s5_d=m_s5_d, s5_w_glu=m_s5_w_glu, conv_w=m_conv_w, conv_b=m_conv_b, conv_ln_g=m_conv_ln_g, conv_ln_b=m_conv_ln_b, w_out=m_w_out, norm2_g=m_norm2_g, mlp_w1=m_mlp_w1, mlp_w2=m_mlp_w2, final_g=m_final_g)
    mom2 = dict(c_ctx=v_c_ctx, ada_w=v_ada_w, ada_b=v_ada_b, norm1_g=v_norm1_g, w_in=v_w_in, s5_lam_re=v_s5_lam_re, s5_lam_im=v_s5_lam_im, s5_log_dt=v_s5_log_dt, s5_b_re=v_s5_b_re, s5_b_im=v_s5_b_im, s5_c_re=v_s5_c_re, s5_c_im=v_s5_c_im, s5_d=v_s5_d, s5_w_glu=v_s5_w_glu, conv_w=v_conv_w, conv_b=v_conv_b, conv_ln_g=v_conv_ln_g, conv_ln_b=v_conv_ln_b, w_out=v_w_out, norm2_g=v_norm2_g, mlp_w1=v_mlp_w1, mlp_w2=v_mlp_w2, final_g=v_final_g)
    order = list(weights)

    me = 4 * lax.axis_index("x") + 2 * lax.axis_index("y") + lax.axis_index("c")
    xs, cs, tgt = x[0], ctx[0], loss_target[0]
    n_lat_rows, n_ctx_rows = xs.shape[0], cs.shape[0]
    n_rows = n_lat_rows + n_ctx_rows
    n_lat = n_lat_rows // ROW_BLOCK
    ada_cols = ada_w.shape[2]

    w_in_loc, w1_loc = w_in[0].astype(BF16), mlp_w1[0].astype(BF16)
    w_in_g, glu_g, conv_w_g, w_out_g, w1_g, w2_g, c_all = _exchange(
        "gather_weights",
        [w_in_loc, s5_w_glu[0].astype(BF16), conv_w[0], w_out[0].astype(BF16), w1_loc, mlp_w2[0].astype(BF16), c],
        [True] * 7)
    w_in_full = jnp.transpose(w_in_g, (1, 0, 2)).reshape(D_MODEL, IN_COLS)
    glu_full = glu_g.reshape(S5_WIDTH, S5_WIDTH)
    conv_w_full = jnp.transpose(conv_w_g, (1, 0, 2)).reshape(CONV_K, CONV_WIDTH)
    w_out_full = w_out_g.reshape(D_MODEL, D_MODEL)
    w2_full = w2_g.reshape(D_FF, D_MODEL)
    c_all = c_all.reshape(NDEV, D_MODEL)

    cond_fwd = jnp.concatenate([c_all, c_ctx[None], jnp.zeros((7, D_MODEL), F32)])
    ada_b_loc = lax.dynamic_slice(ada_b, (0, me * ada_cols), (1, ada_cols))
    (mod_g,) = _exchange("gather_mod", [_ada_fwd(cond_fwd, ada_w[0], ada_b_loc)], [True])
    mod_rows = jnp.transpose(mod_g, (1, 0, 2)).reshape(16, 6 * D_MODEL)
    mod = lax.dynamic_slice(mod_rows, (me, 0), (1, 6 * D_MODEL)).reshape(6, D_MODEL)
    modc = mod_rows[8, :2 * D_MODEL].reshape(2, D_MODEL)
    sh1, sc1, g1, sh2, sc2, g2 = [mod[i:i + 1] for i in range(6)]

    a_all = _prenorm("prenorm1", xs, cs, norm1_g, jnp.stack([mod[0:2], modc]))
    (z_all,) = _matmul("in_proj", a_all, w_in_full, "nn", (n_rows, IN_COLS, D_MODEL), (ROW_BLOCK, IN_COLS, D_MODEL),
                       [((n_rows, IN_COLS), F32)])

    lam_re, lam_im = s5_lam_re[0].reshape(2, 1, NSTATE), s5_lam_im[0].reshape(2, 1, NSTATE)
    ldt = jnp.repeat(s5_log_dt[0], S5_STATE, axis=-1).reshape(2, 1, NSTATE)
    bt_re = jnp.transpose(s5_b_re[0], (0, 3, 1, 2)).reshape(2, S5_GROUP, NSTATE)
    bt_im = jnp.transpose(s5_b_im[0], (0, 3, 1, 2)).reshape(2, S5_GROUP, NSTATE)
    ct_re = jnp.tile(s5_c_re[0].reshape(2, S5_WIDTH, S5_STATE), (1, 1, S5_GROUPS))
    ct_im = jnp.tile(s5_c_im[0].reshape(2, S5_WIDTH, S5_STATE), (1, 1, S5_GROUPS))
    d_skip = s5_d[0].reshape(1, S5_WIDTH)
    disc, states, y_dir = [], [], []
    for d in range(2):
        disc.append(_s5_discretise(f"s5_disc{d}", d == 0, lam_re[d], lam_im[d], ldt[d], bt_re[d], bt_im[d], ct_re[d], ct_im[d]))
        _, tab, _, bmat, cmat = disc[d]
        s, yd = _s5_scan_fwd(f"s5_scan_fwd{d}", d == 0, z_all, bmat, cmat, tab)
        states.append(s)
        y_dir.append(yd)
    y_s5 = _glu_fwd(z_all, y_dir[0], y_dir[1], d_skip, glu_full, n_lat_rows)

    hh_pad = _conv_gate(z_all, n_lat_rows)
    hc, y_conv = _conv_fwd(hh_pad, conv_w_full, conv_b, conv_ln_g, conv_ln_b, n_lat_rows)

    ycat = jnp.concatenate([y_s5, y_conv], axis=1)
    tm = min(1024, n_lat_rows)
    w1_cols = D_FF // NDEV
    row_vec = lambda tn: pl.BlockSpec((1, tn), lambda i, j, k: (0, j))
    out_tile = lambda t_m, t_n: pl.BlockSpec((t_m, t_n), lambda i, j, k: (i, j))
    gated = lambda acc, res, gate: (acc, res + gate * acc)
    mix, h1 = _matmul("out_proj", ycat, w_out_full, "nn", (n_lat_rows, D_MODEL, D_MODEL), (tm, D_MODEL, D_MODEL),
                      [((n_lat_rows, D_MODEL), F32)] * 2, epi=gated,
                      epi_extra=[(xs, out_tile(tm, D_MODEL)), (g1, row_vec(D_MODEL))])
    a2 = _prenorm("prenorm2", h1, None, norm2_g, mod[3:5][None])
    tm_up = min(2048, n_lat_rows)
    (f,) = _matmul("mlp_up", a2, w1_g, "nn", (n_lat_rows, D_FF, D_MODEL), (tm_up, w1_cols, D_MODEL),
                   [((n_lat_rows, D_FF), BF16)], b_spec=pl.BlockSpec((None, D_MODEL, w1_cols), lambda i, j, k: (j, 0, 0)))
    sq_relu = lambda t: jnp.square(jnp.maximum(t, 0.0))
    mlp_out, h2 = _matmul("mlp_down", f, w2_full, "nn", (n_lat_rows, D_MODEL, D_FF), (tm, D_MODEL, 1024),
                          [((n_lat_rows, D_MODEL), F32)] * 2, a_fn=sq_relu, epi=gated,
                          epi_extra=[(h1, out_tile(tm, D_MODEL)), (g2, row_vec(D_MODEL))])

    d_h2, dm2, err_sums, d_final_g8 = _loss_head(h2, tgt, final_g[None], g2)
    loss = lax.psum(0.5 / D_MODEL * jnp.sum(err_sums), ("x", "y", "c"))

    (d_f,) = _matmul("mlp_down_dx", dm2, w2_full, "nt", (n_lat_rows, D_FF, D_MODEL), (tm, 512, D_MODEL),
                     [((n_lat_rows, D_FF), BF16)],
                     epi=lambda acc, ft: (acc * 2.0 * jnp.maximum(ft.astype(F32), 0.0),), epi_extra=[(f, out_tile(tm, 512))])
    (g_w2,) = _matmul("mlp_down_dw", f, dm2, "tn", (D_FF, D_MODEL, n_lat_rows), (1024, D_MODEL, tm),
                      [((D_FF, D_MODEL), F32)], a_fn=sq_relu)
    (d_a2,) = _matmul("mlp_up_dx", d_f, w1_g, "nt", (n_lat_rows, D_MODEL, D_FF), (tm, D_MODEL, w1_cols),
                      [((n_lat_rows, D_MODEL), F32)],
                      b_spec=pl.BlockSpec((None, D_MODEL, w1_cols), lambda i, j, k: (k, 0, 0)))
    (g_w1,) = _matmul("mlp_up_dw", a2, d_f, "tn", (D_MODEL, D_FF, n_lat_rows), (D_MODEL, w1_cols, tm),
                      [((NDEV, D_MODEL, w1_cols), F32)],
                      out_specs=[pl.BlockSpec((None, D_MODEL, w1_cols), lambda i, j, k: (j, 0, 0))])
    d_h1, dm1, sums2 = _norm_bwd("norm2_bwd", h1, d_a2, 0, norm2_g, sc2, res=d_h2, aux=mlp_out, gate=g1)

    (d_ycat,) = _matmul("out_proj_dx", dm1, w_out_full, "nt", (n_lat_rows, D_MODEL, D_MODEL), (tm, D_MODEL, D_MODEL),
                        [((n_lat_rows, D_MODEL), F32)])
    (g_w_out,) = _matmul("out_proj_dw", ycat, dm1, "tn", (D_MODEL, D_MODEL, n_lat_rows), (D_MODEL, D_MODEL, 512),
                         [((D_MODEL, D_MODEL), F32)])

    dy, g_glu, dd8 = _glu_bwd(d_ycat, z_all, y_dir[0], y_dir[1], d_skip, glu_full, n_lat_rows)
    du, g_lam_re, g_lam_im, g_ldt, g_b_re, g_b_im, g_c_re, g_c_im = [], [], [], [], [], [], [], []
    diag = lambda mat: jnp.diagonal(mat.reshape(S5_GROUPS, S5_GROUP, S5_GROUPS, S5_STATE), axis1=0, axis2=2)
    for d in range(2):
        _, _, adj, bmat, cmat = disc[d]
        g_adj, d_abar8 = _s5_scan_bwd(f"s5_scan_bwd{d}", d == 0, dy, states[d], cmat, adj)
        (du_d,) = _matmul(f"s5_du{d}", g_adj, bmat, "nt", (n_rows, S5_WIDTH, 2 * NSTATE), (ROW_BLOCK, 512, 1024),
                          [((n_rows, S5_WIDTH), F32)])
        du.append(du_d)
        (d_bmat,) = _matmul(f"s5_db{d}", z_all, g_adj, "tn", (S5_WIDTH, 2 * NSTATE, n_rows), (512, 1024, ROW_BLOCK),
                            [((S5_WIDTH, 2 * NSTATE), F32)])
        (d_cmat,) = _matmul(f"s5_dc{d}", dy, states[d], "tn", (S5_WIDTH, 2 * NSTATE, n_lat_rows), (512, 1024, ROW_BLOCK),
                            [((S5_WIDTH, 2 * NSTATE), F32)])
        to_hn = lambda t: jnp.transpose(t, (0, 2, 1)).reshape(S5_GROUP, NSTATE)
        d_bbar = jnp.concatenate([to_hn(diag(d_bmat[:, :NSTATE])), to_hn(diag(d_bmat[:, NSTATE:]))])
        d_lam8, d_bt = _s5_discretise_bwd(f"s5_disc_bwd{d}", lam_re[d], lam_im[d], ldt[d], bt_re[d], bt_im[d], d_abar8, d_bbar)
        g_lam_re.append(d_lam8[0].reshape(S5_GROUPS, S5_STATE))
        g_lam_im.append(d_lam8[1].reshape(S5_GROUPS, S5_STATE))
        g_ldt.append(d_lam8[2].reshape(S5_GROUPS, S5_STATE).sum(axis=-1))
        to_gph = lambda t: jnp.transpose(t.reshape(S5_GROUP, S5_GROUPS, S5_STATE), (1, 2, 0))
        g_b_re.append(to_gph(d_bt[:S5_GROUP]))
        g_b_im.append(to_gph(d_bt[S5_GROUP:]))
        g_c_re.append(jnp.transpose(diag(d_cmat[:, :NSTATE]), (2, 0, 1)))
        g_c_im.append(-jnp.transpose(diag(d_cmat[:, NSTATE:]), (2, 0, 1)))

    dhc_pad, conv_sums = _conv_bwd_norm(d_ycat, hc, conv_ln_g, conv_ln_b, n_lat_rows)
    d_v, d_gate, g_conv_w8 = _conv_bwd_taps(dhc_pad, hh_pad, z_all, conv_w_full, n_lat_rows)

    dz_all = _dz_assemble(du[0], du[1], dy, d_skip, d_v, d_gate, n_lat)
    (d_a_all,) = _matmul("in_proj_dx", dz_all, w_in_full, "nt", (n_rows, D_MODEL, IN_COLS), (ROW_BLOCK, D_MODEL, IN_COLS),
                         [((n_rows, D_MODEL), F32)])
    (g_w_in_full,) = _matmul("in_proj_dw", a_all, dz_all, "tn", (D_MODEL, IN_COLS, n_rows), (D_MODEL, IN_COLS, ROW_BLOCK),
                             [((D_MODEL, IN_COLS), F32)])
    grad_x, sums1 = _norm_bwd("norm1_bwd", xs, d_a_all, 0, norm1_g, sc1, res=d_h1, aux=mix)
    (sums1c,) = _norm_bwd("norm1_bwd_ctx", cs, d_a_all, n_lat, norm1_g, modc[1:2])

    s1, s1c, s2 = sums1.sum(axis=1), sums1c.sum(axis=1), sums2.sum(axis=1)
    d_mod = jnp.concatenate([s1[0], s1[1], s1[3], s2[0], s2[1], s2[3]])
    d_modc = jnp.concatenate([s1c[0], s1c[1], jnp.zeros((4 * D_MODEL,), F32)])
    (dmod_g,) = _exchange("gather_dmod", [jnp.stack([d_mod, d_modc])], [True])
    dmod16 = jnp.concatenate([dmod_g[:, 0], dmod_g[:, 1]])
    dmod16_loc = lax.dynamic_slice(dmod16, (0, me * ada_cols), (16, ada_cols))
    cond_bwd = jnp.concatenate([c_all, jnp.broadcast_to(c_ctx[None], (NDEV, D_MODEL))])
    g_ada_w, g_c_ctx8 = _ada_bwd(cond_bwd, dmod16_loc, ada_w[0], c_ctx[None])

    small_parts = dict(
        c_ctx=g_c_ctx8[0], ada_b=d_mod + d_modc, norm1_g=s1[2] + s1c[2],
        s5_lam_re=jnp.stack(g_lam_re), s5_lam_im=jnp.stack(g_lam_im), s5_log_dt=jnp.stack(g_ldt),
        s5_b_re=jnp.stack(g_b_re), s5_b_im=jnp.stack(g_b_im), s5_c_re=jnp.stack(g_c_re), s5_c_im=jnp.stack(g_c_im),
        s5_d=dd8.sum(axis=0), conv_b=conv_sums[0].sum(axis=0), conv_ln_g=conv_sums[1].sum(axis=0),
        conv_ln_b=conv_sums[2].sum(axis=0), norm2_g=s2[2], final_g=d_final_g8.sum(axis=0))
    small_g = _pack_small([small_parts[n] for n in SMALL]).reshape(NDEV, SMALL_ROWS // NDEV, D_MODEL)
    g_w_in_parts = jnp.transpose(g_w_in_full.reshape(D_MODEL, NDEV, IN_COLS // NDEV), (1, 0, 2))
    g_conv_w_parts = jnp.transpose(g_conv_w8.sum(axis=1).reshape(CONV_K, NDEV, CONV_WIDTH // NDEV), (1, 0, 2))
    p_w_in, p_glu, p_conv_w, p_w_out, p_w1, p_w2, p_small = _exchange(
        "scatter_grads",
        [g_w_in_parts, g_glu.reshape(NDEV, S5_WIDTH // NDEV, S5_WIDTH), g_conv_w_parts,
         g_w_out.reshape(NDEV, D_MODEL // NDEV, D_MODEL), g_w1, g_w2.reshape(NDEV, D_FF // NDEV, D_MODEL), small_g],
        [False] * 7)
    (small_all,) = _exchange("gather_small", [_sum_parts(p_small)], [True])
    small_all = small_all.reshape(1, SMALL_ROWS, D_MODEL)

    res = {}
    big = dict(ada_w=g_ada_w[None], w_in=p_w_in, s5_w_glu=p_glu, conv_w=p_conv_w, w_out=p_w_out, mlp_w1=p_w1, mlp_w2=p_w2)
    for name, parts in big.items():
        outs = _adamw("adamw_" + name, weights[name][0], parts, mom1[name][0], mom2[name][0])
        res[name] = [o[None] for o in outs]
    small_like = [weights[n] for n in SMALL]
    outs = _adamw("adamw_small", _pack_small(small_like), small_all, _pack_small([mom1[n] for n in SMALL]),
                  _pack_small([mom2[n] for n in SMALL]))
    unpacked = [_unpack_small(o, small_like) for o in outs]
    for i, name in enumerate(SMALL):
        res[name] = [u[i] for u in unpacked]

    return (loss, grad_x[None], *[res[n][0] for n in order], *[res[n][1] for n in order],
            *[res[n][2] for n in order], *[res[n][3] for n in order])
```

```python
import functools

import jax
import jax.numpy as jnp
from jax import lax
from jax.experimental import pallas as pl
from jax.experimental.pallas import tpu as pltpu

F32 = jnp.float32
BF16 = jnp.bfloat16
MESH = pl.DeviceIdType.MESH
ANY = pl.BlockSpec(memory_space=pl.ANY)

NDEV = 8
D_MODEL = 1024
GRID_W = 64
S5_WIDTH = 512
S5_GROUP = 16
S5_GROUPS = 32
S5_STATE = 64
NSTATE = S5_GROUPS * S5_STATE
CONV_WIDTH = 512
CONV_K = 31
IN_COLS = S5_WIDTH + 2 * CONV_WIDTH
D_FF = 4 * D_MODEL
EPS_RMS = 1e-6
EPS_LN = 1e-5
ADAM_LR = 0.001
ADAM_B1 = 0.9
ADAM_B2 = 0.999
ADAM_EPS = 1e-08
ADAM_WD = 0.01
ADAM_STEP = 10

SUBLANES = 8
LANES = 128
ROW_BLOCK = 256
SCAN_LANES = 512
SEGMENTS = SUBLANES
STEPS = ROW_BLOCK // SEGMENTS
S5_BLOCKS = 4
S5_BLOCK_WIDTH = S5_WIDTH // S5_BLOCKS
CONV_ROWS = 64
VMEM_LIMIT = 48 * 1024 * 1024
SMALL_ROWS = 320


def _params(sem=None):
    kw = dict(vmem_limit_bytes=VMEM_LIMIT)
    if sem is not None:
        kw["dimension_semantics"] = sem
    return pltpu.CompilerParams(**kw)


def _sds(shape, dtype=F32):
    return jax.ShapeDtypeStruct(tuple(shape), dtype)


def _fold8(x):
    return x.reshape(x.shape[0] // SUBLANES, SUBLANES, x.shape[1]).sum(axis=0)


def _sigmoid(x):
    return 1.0 / (1.0 + jnp.exp(-x))


def _silu(x):
    return x * _sigmoid(x)


def _dsilu(x):
    s = _sigmoid(x)
    return s * (1.0 + x * (1.0 - s))


_GELU_C = 0.7978845608028654


def _gelu(x):
    return 0.5 * x * (1.0 + jnp.tanh(_GELU_C * (x + 0.044715 * x * x * x)))


def _dgelu(x):
    t = jnp.tanh(_GELU_C * (x + 0.044715 * x * x * x))
    return 0.5 * (1.0 + t) + 0.5 * x * (1.0 - t * t) * _GELU_C * (1.0 + 3.0 * 0.044715 * x * x)


def _dot(a, b, mode):
    dims = {"nn": (((1,), (0,)), ((), ())), "nt": (((1,), (1,)), ((), ())), "tn": (((0,), (0,)), ((), ()))}[mode]
    return lax.dot_general(a, b, dims, preferred_element_type=F32)


def _exchange(name, srcs, gather):
    n = len(srcs)
    outs = [_sds(((NDEV,) + s.shape) if g else s.shape, s.dtype) for s, g in zip(srcs, gather)]

    def body(*refs):
        src, dst = refs[:n], refs[n:2 * n]
        send_sems, recv_sems, local_sems = refs[2 * n:]
        x, y, c = lax.axis_index("x"), lax.axis_index("y"), lax.axis_index("c")
        me = 4 * x + 2 * y + c
        started = []
        for a in range(n):
            def chunk(dest, a=a):
                return src[a] if gather[a] else src[a].at[dest]

            local = pltpu.make_async_copy(chunk(me), dst[a].at[me], local_sems.at[a])
            local.start()
            for k in range(1, NDEV):
                px = 1 - x if k & 4 else x
                py = 1 - y if k & 2 else y
                pc = 1 - c if k & 1 else c
                peer = 4 * px + 2 * py + pc
                copy = pltpu.make_async_remote_copy(
                    src_ref=chunk(peer), dst_ref=dst[a].at[me],
                    send_sem=send_sems.at[a * (NDEV - 1) + k - 1], recv_sem=recv_sems.at[a * (NDEV - 1) + k - 1],
                    device_id=(px, py, pc), device_id_type=MESH)
                copy.start()
                landing = pltpu.make_async_remote_copy(
                    src_ref=chunk(peer), dst_ref=dst[a].at[peer],
                    send_sem=send_sems.at[a * (NDEV - 1) + k - 1], recv_sem=recv_sems.at[a * (NDEV - 1) + k - 1],
                    device_id=(px, py, pc), device_id_type=MESH)
                started.append((copy, landing))
            started.append((local, None))
        for copy, landing in started:
            if landing is None:
                copy.wait()
            else:
                copy.wait_send()
                landing.wait_recv()

    return pl.pallas_call(
        body, name=name, out_shape=outs, in_specs=[ANY] * n, out_specs=[ANY] * n,
        scratch_shapes=[pltpu.SemaphoreType.DMA((n * (NDEV - 1),)), pltpu.SemaphoreType.DMA((n * (NDEV - 1),)),
                        pltpu.SemaphoreType.DMA((n,))],
    )(*srcs)


def _matmul(name, a, b, mode, mnk, tiles, outs, a_spec=None, b_spec=None, a_fn=None, a_extra=(),
            epi=None, epi_extra=(), out_specs=None):
    m_, n_, k_ = mnk
    tm, tn, tk = tiles
    nk = k_ // tk
    if a_spec is None:
        a_spec = (pl.BlockSpec((tk, tm), lambda i, j, k: (k, i)) if mode == "tn"
                  else pl.BlockSpec((tm, tk), lambda i, j, k: (i, k)))
    if b_spec is None:
        b_spec = (pl.BlockSpec((tn, tk), lambda i, j, k: (j, k)) if mode == "nt"
                  else pl.BlockSpec((tk, tn), lambda i, j, k: (k, j)))
    if out_specs is None:
        out_specs = [pl.BlockSpec((tm, tn), lambda i, j, k: (i, j)) for _ in outs]
    na, ne, no = len(a_extra), len(epi_extra), len(outs)

    def body(*refs):
        a_ref, b_ref = refs[0], refs[1]
        ax = refs[2:2 + na]
        ex = refs[2 + na:2 + na + ne]
        o = refs[2 + na + ne:2 + na + ne + no]

        def finish(res):
            res = epi(res, *[r[...] for r in ex]) if epi is not None else (res,)
            for ref, val in zip(o, res):
                ref[...] = val.astype(ref.dtype)

        at = a_ref[...]
        if a_fn is not None:
            at = a_fn(at, *[r[...] for r in ax])
        part = _dot(at.astype(BF16), b_ref[...].astype(BF16), mode)
        if nk == 1:
            finish(part)
            return
        acc = refs[-1]
        k = pl.program_id(2)

        @pl.when(k == 0)
        def _():
            acc[...] = part

        @pl.when(k > 0)
        def _():
            acc[...] += part

        @pl.when(k == nk - 1)
        def _():
            finish(acc[...])

    return pl.pallas_call(
        body, name=name, grid=(m_ // tm, n_ // tn, nk),
        in_specs=[a_spec, b_spec] + [s for _, s in a_extra] + [s for _, s in epi_extra],
        out_specs=out_specs, out_shape=[_sds(s, d) for s, d in outs],
        scratch_shapes=[pltpu.VMEM((tm, tn), F32)] if nk > 1 else [],
        compiler_params=_params(("parallel", "parallel", "arbitrary")),
    )(a, b, *[x for x, _ in a_extra], *[x for x, _ in epi_extra])


def _prenorm(name, x, ctx, gain, shsc):
    n_lat = x.shape[0] // ROW_BLOCK
    n_ctx = 0 if ctx is None else ctx.shape[0] // ROW_BLOCK
    d = x.shape[1]

    def norm(src, g_ref, m_ref, o_ref):
        xv = src[...]
        xh = xv * lax.rsqrt(jnp.mean(xv * xv, axis=-1, keepdims=True) + EPS_RMS)
        o_ref[...] = ((xh * g_ref[...]) * (1.0 + m_ref[1:2, :]) + m_ref[0:1, :]).astype(o_ref.dtype)

    def body(*refs):
        if ctx is None:
            x_ref, g_ref, m_ref, o_ref = refs
            norm(x_ref, g_ref, m_ref, o_ref)
        else:
            x_ref, c_ref, g_ref, m_ref, o_ref = refs
            i = pl.program_id(0)

            @pl.when(i < n_lat)
            def _():
                norm(x_ref, g_ref, m_ref, o_ref)

            @pl.when(i >= n_lat)
            def _():
                norm(c_ref, g_ref, m_ref, o_ref)

    in_specs = [pl.BlockSpec((ROW_BLOCK, d), lambda i: (jnp.minimum(i, n_lat - 1), 0))]
    args = [x]
    if ctx is not None:
        in_specs.append(pl.BlockSpec((ROW_BLOCK, d), lambda i: (jnp.maximum(i - n_lat, 0), 0)))
        args.append(ctx)
    in_specs += [pl.BlockSpec((1, d), lambda i: (0, 0)),
                 pl.BlockSpec((None, 2, d), lambda i: (jnp.minimum(i // n_lat, 1), 0, 0))]
    args += [gain, shsc]
    return pl.pallas_call(
        body, name=name, grid=(n_lat + n_ctx,), in_specs=in_specs,
        out_specs=pl.BlockSpec((ROW_BLOCK, d), lambda i: (i, 0)),
        out_shape=_sds(((n_lat + n_ctx) * ROW_BLOCK, d), BF16),
        compiler_params=_params(("parallel",)),
    )(*args)


def _norm_bwd(name, x, d_act, d_act_row0, gain, scale, res=None, aux=None, gate=None):
    rows, d = x.shape
    nb = rows // ROW_BLOCK
    has_res = res is not None
    has_gate = gate is not None

    def body(*refs):
        if has_gate:
            x_ref, da_ref, g_ref, sc_ref, r_ref, aux_ref, gate_ref, dx_ref, dm_ref, sums = refs
        elif has_res:
            x_ref, da_ref, g_ref, sc_ref, r_ref, aux_ref, dx_ref, sums = refs
        else:
            x_ref, da_ref, g_ref, sc_ref, sums = refs
        i = pl.program_id(0)

        @pl.when(i == 0)
        def _():
            sums[...] = jnp.zeros_like(sums)

        xv, da = x_ref[...], da_ref[...]
        rstd = lax.rsqrt(jnp.mean(xv * xv, axis=-1, keepdims=True) + EPS_RMS)
        xh = xv * rstd
        g = g_ref[...]
        dn = da * (1.0 + sc_ref[...])
        sums[0] += _fold8(da)
        sums[1] += _fold8(da * (xh * g))
        sums[2] += _fold8(dn * xh)
        if has_res:
            dxh = dn * g
            dx = rstd * (dxh - xh * jnp.mean(dxh * xh, axis=-1, keepdims=True))
            rv = r_ref[...]
            dx_ref[...] = rv + dx
            sums[3] += _fold8(rv * aux_ref[...])
            if has_gate:
                dm_ref[...] = ((rv + dx) * gate_ref[...]).astype(dm_ref.dtype)

    row = lambda i: (i, 0)
    vec = pl.BlockSpec((1, d), lambda i: (0, 0))
    in_specs = [pl.BlockSpec((ROW_BLOCK, d), row), pl.BlockSpec((ROW_BLOCK, d), lambda i: (i + d_act_row0, 0)), vec, vec]
    args = [x, d_act, gain, scale]
    out_shape = [_sds((4, SUBLANES, d))]
    out_specs = [pl.BlockSpec((4, SUBLANES, d), lambda i: (0, 0, 0))]
    if has_res:
        in_specs += [pl.BlockSpec((ROW_BLOCK, d), row), pl.BlockSpec((ROW_BLOCK, d), row)]
        args += [res, aux]
        if has_gate:
            in_specs.append(vec)
            args.append(gate)
            out_shape = [_sds((rows, d), BF16)] + out_shape
            out_specs = [pl.BlockSpec((ROW_BLOCK, d), row)] + out_specs
        out_shape = [_sds((rows, d))] + out_shape
        out_specs = [pl.BlockSpec((ROW_BLOCK, d), row)] + out_specs
    return pl.pallas_call(
        body, name=name, grid=(nb,), in_specs=in_specs, out_specs=out_specs, out_shape=out_shape,
        compiler_params=_params(("arbitrary",)),
    )(*args)


def _loss_head(h2, target, gain, gate):
    rows, d = h2.shape

    def body(h_ref, t_ref, g_ref, gate_ref, dh_ref, dm_ref, err_ref, dg_ref):
        i = pl.program_id(0)

        @pl.when(i == 0)
        def _():
            err_ref[...] = jnp.zeros_like(err_ref)
            dg_ref[...] = jnp.zeros_like(dg_ref)

        hv = h_ref[...]
        rstd = lax.rsqrt(jnp.mean(hv * hv, axis=-1, keepdims=True) + EPS_RMS)
        xh = hv * rstd
        g = g_ref[...]
        err = xh * g - t_ref[...]
        err_ref[...] += _fold8(err * err)
        dy = err * (1.0 / d)
        dg_ref[...] += _fold8(dy * xh)
        dxh = dy * g
        dh = rstd * (dxh - xh * jnp.mean(dxh * xh, axis=-1, keepdims=True))
        dh_ref[...] = dh
        dm_ref[...] = (dh * gate_ref[...]).astype(dm_ref.dtype)

    row = pl.BlockSpec((ROW_BLOCK, d), lambda i: (i, 0))
    acc = pl.BlockSpec((SUBLANES, d), lambda i: (0, 0))
    vec = pl.BlockSpec((1, d), lambda i: (0, 0))
    return pl.pallas_call(
        body, name="loss_head", grid=(rows // ROW_BLOCK,),
        in_specs=[row, row, vec, vec], out_specs=[row, row, acc, acc],
        out_shape=[_sds((rows, d)), _sds((rows, d), BF16), _sds((SUBLANES, d)), _sds((SUBLANES, d))],
        compiler_params=_params(("arbitrary",)),
    )(h2, target, gain, gate)


def _ada_fwd(cond16, ada_w_loc, ada_b_loc):
    cols = ada_w_loc.shape[1]

    def body(c_ref, w_ref, b_ref, o_ref):
        s = _silu(c_ref[...]).astype(BF16)
        o_ref[...] = _dot(s, w_ref[...].astype(BF16), "nn") + b_ref[...]

    return pl.pallas_call(body, name="ada_fwd", out_shape=_sds((16, cols)), compiler_params=_params())(
        cond16, ada_w_loc, ada_b_loc)


def _ada_bwd(cond16, dmod16, ada_w_loc, c_ctx_row):
    k_, cols = ada_w_loc.shape

    def body(c_ref, dm_ref, w_ref, cc_ref, gw_ref, gc_ref):
        s = _silu(c_ref[...]).astype(BF16)
        dm = dm_ref[...]
        gw_ref[...] = _dot(s, dm.astype(BF16), "tn")
        dmc = jnp.sum(dm[8:16, :], axis=0, keepdims=True)
        dmc8 = jnp.broadcast_to(dmc, (SUBLANES, cols)).astype(BF16)
        ds = _dot(dmc8, w_ref[...].astype(BF16), "nt")
        row = lax.broadcasted_iota(jnp.int32, ds.shape, 0)
        gc_ref[...] = jnp.where(row == 0, ds * _dsilu(cc_ref[...]), 0.0)

    return pl.pallas_call(body, name="ada_bwd", out_shape=[_sds((k_, cols)), _sds((SUBLANES, k_))],
                          compiler_params=_params())(cond16, dmod16, ada_w_loc, c_ctx_row)


def _cmul(a, b):
    return a[0] * b[0] - a[1] * b[1], a[0] * b[1] + a[1] * b[0]


def _disc(lam_re, lam_im, ldt):
    dt = jnp.exp(ldt)
    mag = jnp.exp(lam_re * dt)
    th = lam_im * dt
    a_re, a_im = mag * jnp.cos(th), mag * jnp.sin(th)
    den = lam_re * lam_re + lam_im * lam_im
    n_re = a_re - 1.0
    f_re = (n_re * lam_re + a_im * lam_im) / den
    f_im = (a_im * lam_re - n_re * lam_im) / den
    return dt, mag, th, a_re, a_im, den, n_re, f_re, f_im


def _block_diag_mask(shape):
    row = lax.broadcasted_iota(jnp.int32, shape, 0)
    col = lax.broadcasted_iota(jnp.int32, shape, 1)
    return lax.shift_right_logical(row, 4) == lax.shift_right_logical(col, 6)


TAB_A = 0
TAB_BIG = 1
TAB_SEG = 4
TAB_PW = 5
TAB_ROWS = TAB_PW + STEPS


def _s5_discretise(name, ascending, lam_re, lam_im, ldt, bt_re, bt_im, ct_re, ct_im):
    def write_tables(ref, pw, big, asc, sign):
        row = lax.broadcasted_iota(jnp.int32, (SUBLANES, NSTATE), 0)
        full = lambda v: jnp.broadcast_to(v, (SUBLANES, NSTATE))

        def put(t, p):
            ref[0, t] = full(p[0])
            ref[1, t] = full(sign * p[1])

        put(TAB_A, pw[0])
        for t in range(3):
            put(TAB_BIG + t, big[t])
        seg = [big[0]]
        for _ in range(SEGMENTS - 1):
            seg.append(_cmul(seg[-1], big[0]))
        seg_re = jnp.zeros((SUBLANES, NSTATE), F32)
        seg_im = jnp.zeros((SUBLANES, NSTATE), F32)
        for r in range(SEGMENTS):
            p = seg[r] if asc else seg[SEGMENTS - 1 - r]
            seg_re = jnp.where(row == r, p[0], seg_re)
            seg_im = jnp.where(row == r, sign * p[1], seg_im)
        ref[0, TAB_SEG] = seg_re
        ref[1, TAB_SEG] = seg_im
        for k in range(STEPS):
            put(TAB_PW + k, pw[k])

    def body(lr_ref, li_ref, ldt_ref, br_ref, bi_ref, cr_ref, ci_ref, bb_ref, tab_ref, adj_ref, bm_ref, cm_ref):
        _, _, _, a_re, a_im, _, _, f_re, f_im = _disc(lr_ref[...], li_ref[...], ldt_ref[...])
        bre, bim = br_ref[...], bi_ref[...]
        bb_re = f_re * bre - f_im * bim
        bb_im = f_re * bim + f_im * bre
        bb_ref[0:S5_GROUP, :] = bb_re
        bb_ref[S5_GROUP:2 * S5_GROUP, :] = bb_im
        pw = [(a_re, a_im)]
        for _ in range(STEPS - 1):
            pw.append(_cmul(pw[-1], (a_re, a_im)))
        big = [pw[STEPS - 1]]
        for _ in range(2):
            big.append(_cmul(big[-1], big[-1]))
        write_tables(tab_ref, pw, big, ascending, 1.0)
        write_tables(adj_ref, pw, big, not ascending, -1.0)
        half = NSTATE // S5_BLOCKS
        mask = _block_diag_mask((S5_BLOCK_WIDTH, half))
        tile = lambda v: jnp.broadcast_to(v[None], (S5_BLOCK_WIDTH // S5_GROUP, S5_GROUP, half)).reshape(S5_BLOCK_WIDTH, half)
        for c in range(S5_BLOCKS):
            cols = slice(c * half, (c + 1) * half)
            rows = slice(c * S5_BLOCK_WIDTH, (c + 1) * S5_BLOCK_WIDTH)
            bm_ref[c, :, 0:half] = jnp.where(mask, tile(bb_re[:, cols]), 0.0).astype(BF16)
            bm_ref[c, :, half:2 * half] = jnp.where(mask, tile(bb_im[:, cols]), 0.0).astype(BF16)
            cm_ref[c, :, 0:half] = jnp.where(mask, cr_ref[rows, :], 0.0).astype(BF16)
            cm_ref[c, :, half:2 * half] = jnp.where(mask, -ci_ref[rows, :], 0.0).astype(BF16)

    blocked = _sds((S5_BLOCKS, S5_BLOCK_WIDTH, 2 * NSTATE // S5_BLOCKS), BF16)
    return pl.pallas_call(
        body, name=name,
        out_shape=[_sds((2 * S5_GROUP, NSTATE)), _sds((2, TAB_ROWS, SUBLANES, NSTATE)),
                   _sds((2, TAB_ROWS, SUBLANES, NSTATE)), blocked, blocked],
        compiler_params=_params(),
    )(lam_re, lam_im, ldt, bt_re, bt_im, ct_re, ct_im)


def _s5_discretise_bwd(name, lam_re, lam_im, ldt, bt_re, bt_im, d_abar8, d_bbar):
    def body(lr_ref, li_ref, ldt_ref, br_ref, bi_ref, da_ref, db_ref, dl_ref, dbt_ref):
        lam_re, lam_im = lr_ref[...], li_ref[...]
        dt, mag, _, a_re, a_im, den, n_re, f_re, f_im = _disc(lam_re, lam_im, ldt_ref[...])
        bre, bim = br_ref[...], bi_ref[...]
        dbr, dbi = db_ref[0:S5_GROUP, :], db_ref[S5_GROUP:2 * S5_GROUP, :]
        dbt_ref[0:S5_GROUP, :] = f_re * dbr + f_im * dbi
        dbt_ref[S5_GROUP:2 * S5_GROUP, :] = f_re * dbi - f_im * dbr
        df_re = jnp.sum(bre * dbr + bim * dbi, axis=0, keepdims=True)
        df_im = jnp.sum(bre * dbi - bim * dbr, axis=0, keepdims=True)
        da = da_ref[...]
        da_re = jnp.sum(da[:, 0:NSTATE], axis=0, keepdims=True)
        da_im = jnp.sum(da[:, NSTATE:2 * NSTATE], axis=0, keepdims=True)
        da_re = da_re + (df_re * lam_re - df_im * lam_im) / den
        da_im = da_im + (df_re * lam_im + df_im * lam_re) / den
        ff = (f_re * df_re + f_im * df_im) * 2.0 / den
        d_lr = (df_re * n_re + df_im * a_im) / den - ff * lam_re
        d_li = (df_re * a_im - df_im * n_re) / den - ff * lam_im
        d_mag = (da_re * a_re + da_im * a_im) / mag
        d_th = da_im * a_re - da_re * a_im
        d_lr = d_lr + d_mag * mag * dt
        d_li = d_li + d_th * dt
        d_ldt = (d_mag * mag * lam_re + d_th * lam_im) * dt
        row = lax.broadcasted_iota(jnp.int32, (SUBLANES, NSTATE), 0)
        dl_ref[...] = jnp.where(row == 0, d_lr, jnp.where(row == 1, d_li, jnp.where(row == 2, d_ldt, 0.0)))

    return pl.pallas_call(
        body, name=name, out_shape=[_sds((SUBLANES, NSTATE)), _sds((2 * S5_GROUP, NSTATE))],
        compiler_params=_params(),
    )(lam_re, lam_im, ldt, bt_re, bt_im, d_abar8, d_bbar)


def _segment_permutation():
    rho = jnp.arange(ROW_BLOCK)
    src = STEPS * (rho % SEGMENTS) + rho // SEGMENTS
    return (src[:, None] == jnp.arange(ROW_BLOCK)[None, :]).astype(BF16)


def _permute_rows(perm_ref, v):
    return _dot(perm_ref[...], v, "nn").astype(BF16)


def _unpermute_rows(perm_t_ref, v):
    hi = v.astype(BF16)
    lo = (v - hi.astype(F32)).astype(BF16)
    return _dot(perm_t_ref[...], hi, "nn") + _dot(perm_t_ref[...], lo, "nn")


def _scan_chunk(x_ref, out_ref, tab_ref, carry_re, carry_im, ascending, pair_ref=None, acc_ref=None):
    w = SCAN_LANES
    half = NSTATE // S5_BLOCKS
    row = lax.broadcasted_iota(jnp.int32, (SUBLANES, w), 0)
    last = (SEGMENTS - 1) if ascending else 0

    def from_previous_segment(v, k, fill):
        if ascending:
            return jnp.where(row >= k, pltpu.roll(v, k, 0), fill)
        return jnp.where(row < SEGMENTS - k, pltpu.roll(v, SEGMENTS - k, 0), fill)

    def tile_rows(k):
        return pl.ds(pl.multiple_of((k if ascending else STEPS - 1 - k) * SUBLANES, SUBLANES), SUBLANES)

    for j in range(NSTATE // w):
        n_l = pl.ds(j * w, w)
        lane0 = (j * w // half) * 2 * half + (j * w) % half
        re_l, im_l = pl.ds(lane0, w), pl.ds(lane0 + half, w)
        tab = lambda t, n_l=n_l: (tab_ref[0, t, :, n_l], tab_ref[1, t, :, n_l])
        a_re, a_im = tab(TAB_A)

        def local_step(k, h):
            rs = tile_rows(k)
            h_re = a_re * h[0] - a_im * h[1] + x_ref[rs, re_l]
            h_im = a_re * h[1] + a_im * h[0] + x_ref[rs, im_l]
            out_ref[rs, re_l] = h_re
            out_ref[rs, im_l] = h_im
            return h_re, h_im

        zero = jnp.zeros((SUBLANES, w), F32)
        end_re, end_im = lax.fori_loop(0, STEPS, local_step, (zero, zero))
        for t, k in ((TAB_BIG, 1), (TAB_BIG + 1, 2), (TAB_BIG + 2, 4)):
            p_re, p_im = tab(t)
            s_re, s_im = from_previous_segment(end_re, k, 0.0), from_previous_segment(end_im, k, 0.0)
            end_re, end_im = end_re + (p_re * s_re - p_im * s_im), end_im + (p_re * s_im + p_im * s_re)
        c0_re, c0_im = carry_re[:, n_l], carry_im[:, n_l]
        p_re, p_im = tab(TAB_SEG)
        end_re = end_re + (p_re * c0_re - p_im * c0_im)
        end_im = end_im + (p_re * c0_im + p_im * c0_re)
        carry_re[:, n_l] = jnp.broadcast_to(end_re[last:last + 1, :], end_re.shape)
        carry_im[:, n_l] = jnp.broadcast_to(end_im[last:last + 1, :], end_im.shape)
        in_re = from_previous_segment(end_re, 1, c0_re)
        in_im = from_previous_segment(end_im, 1, c0_im)

        def carry_step(k, st):
            rs = tile_rows(k)
            p_re, p_im = tab_ref[0, TAB_PW + k, :, n_l], tab_ref[1, TAB_PW + k, :, n_l]
            o_re = out_ref[rs, re_l] + (p_re * in_re - p_im * in_im)
            o_im = out_ref[rs, im_l] + (p_re * in_im + p_im * in_re)
            out_ref[rs, re_l] = o_re
            out_ref[rs, im_l] = o_im
            if pair_ref is None:
                return st
            s_re, s_im = pair_ref[rs, re_l], pair_ref[rs, im_l]
            return (o_re, o_im, st[2] + (st[0] * s_re + st[1] * s_im), st[3] + (st[1] * s_re - st[0] * s_im))

        if pair_ref is None:
            lax.fori_loop(0, STEPS, carry_step, 0)
        else:
            fin = lax.fori_loop(0, STEPS, carry_step, (in_re, in_im, zero, zero))
            acc_ref[:, n_l] += fin[2]
            acc_ref[:, pl.ds(NSTATE + j * w, w)] += fin[3]


def _scan_block_index(i, n_lat, ctx_first_then_ascending):
    if ctx_first_then_ascending:
        return jnp.where(i == 0, n_lat, i - 1)
    return jnp.where(i == 0, n_lat, n_lat - i)


def _full_spec(shape):
    return pl.BlockSpec(shape, lambda i: (0,) * len(shape))


_S5_BLOCKED = (S5_BLOCKS, S5_BLOCK_WIDTH, 2 * NSTATE // S5_BLOCKS)
_S5_TABLES = (2, TAB_ROWS, SUBLANES, NSTATE)


def _s5_scan_fwd(name, ascending, z_all, bmat, cmat, tab, perm, perm_t):
    rows = z_all.shape[0]
    nb = rows // ROW_BLOCK
    n_lat = nb - 1
    bw, sw = S5_BLOCK_WIDTH, 2 * NSTATE // S5_BLOCKS

    def body(u_ref, bm_ref, cm_ref, tab_ref, p_ref, pt_ref, s_ref, y_ref, bu, yp, carry_re, carry_im):
        @pl.when(pl.program_id(0) == 0)
        def _():
            carry_re[...] = jnp.zeros_like(carry_re)
            carry_im[...] = jnp.zeros_like(carry_im)

        up = _permute_rows(p_ref, u_ref[...].astype(BF16))
        for c in range(S5_BLOCKS):
            bu[:, c * sw:(c + 1) * sw] = _dot(up[:, c * bw:(c + 1) * bw], bm_ref[c], "nn")
        _scan_chunk(bu, s_ref, tab_ref, carry_re, carry_im, ascending)
        for c in range(S5_BLOCKS):
            yp[:, c * bw:(c + 1) * bw] = _dot(s_ref[:, c * sw:(c + 1) * sw].astype(BF16), cm_ref[c], "nt")
        y_ref[...] = _unpermute_rows(pt_ref, yp[...])

    blk = lambda i: (_scan_block_index(i, n_lat, ascending), 0)
    return pl.pallas_call(
        body, name=name, grid=(nb,),
        in_specs=[pl.BlockSpec((ROW_BLOCK, S5_WIDTH), blk), _full_spec(_S5_BLOCKED), _full_spec(_S5_BLOCKED),
                  _full_spec(_S5_TABLES), _full_spec((ROW_BLOCK, ROW_BLOCK)), _full_spec((ROW_BLOCK, ROW_BLOCK))],
        out_specs=[pl.BlockSpec((ROW_BLOCK, 2 * NSTATE), blk), pl.BlockSpec((ROW_BLOCK, S5_WIDTH), blk)],
        out_shape=[_sds((rows, 2 * NSTATE)), _sds((rows, S5_WIDTH))],
        scratch_shapes=[pltpu.VMEM((ROW_BLOCK, 2 * NSTATE), F32), pltpu.VMEM((ROW_BLOCK, S5_WIDTH), F32),
                        pltpu.VMEM((SUBLANES, NSTATE), F32), pltpu.VMEM((SUBLANES, NSTATE), F32)],
        compiler_params=_params(("arbitrary",)),
    )(z_all, bmat, cmat, tab, perm, perm_t)


def _s5_scan_bwd(name, ascending, dy, z_all, states, bmat, cmat, adj, perm, perm_t):
    rows = states.shape[0]
    nb = rows // ROW_BLOCK
    n_lat = nb - 1
    bw, sw = S5_BLOCK_WIDTH, 2 * NSTATE // S5_BLOCKS

    def block_index(i):
        if ascending:
            return jnp.where(i == nb - 1, n_lat, n_lat - 1 - i)
        return jnp.where(i == nb - 1, n_lat, i)

    def body(dy_ref, u_ref, s_ref, bm_ref, cm_ref, adj_ref, p_ref, pt_ref, du_ref, db_ref, dc_ref, da_ref,
             g, dup, carry_re, carry_im):
        i = pl.program_id(0)

        @pl.when(i == 0)
        def _():
            carry_re[...] = jnp.zeros_like(carry_re)
            carry_im[...] = jnp.zeros_like(carry_im)
            da_ref[...] = jnp.zeros_like(da_ref)
            db_ref[...] = jnp.zeros_like(db_ref)
            dc_ref[...] = jnp.zeros_like(dc_ref)

        @pl.when(i < nb - 1)
        def _():
            dyp = _permute_rows(p_ref, dy_ref[...].astype(BF16))
            for c in range(S5_BLOCKS):
                g[:, c * sw:(c + 1) * sw] = _dot(dyp[:, c * bw:(c + 1) * bw], cm_ref[c], "nn")
                dc_ref[c] += _dot(dyp[:, c * bw:(c + 1) * bw], s_ref[:, c * sw:(c + 1) * sw].astype(BF16), "tn")

        @pl.when(i == nb - 1)
        def _():
            g[...] = jnp.zeros_like(g)

        _scan_chunk(g, g, adj_ref, carry_re, carry_im, not ascending, pair_ref=s_ref, acc_ref=da_ref)
        up = _permute_rows(p_ref, u_ref[...].astype(BF16))
        for c in range(S5_BLOCKS):
            gc = g[:, c * sw:(c + 1) * sw].astype(BF16)
            dup[:, c * bw:(c + 1) * bw] = _dot(gc, bm_ref[c], "nt")
            db_ref[c] += _dot(up[:, c * bw:(c + 1) * bw], gc, "tn")
        du_ref[...] = _unpermute_rows(pt_ref, dup[...])

    blk = lambda i: (block_index(i), 0)
    return pl.pallas_call(
        body, name=name, grid=(nb,),
        in_specs=[pl.BlockSpec((ROW_BLOCK, S5_WIDTH), lambda i: (jnp.minimum(block_index(i), n_lat - 1), 0)),
                  pl.BlockSpec((ROW_BLOCK, S5_WIDTH), blk), pl.BlockSpec((ROW_BLOCK, 2 * NSTATE), blk),
                  _full_spec(_S5_BLOCKED), _full_spec(_S5_BLOCKED), _full_spec(_S5_TABLES),
                  _full_spec((ROW_BLOCK, ROW_BLOCK)), _full_spec((ROW_BLOCK, ROW_BLOCK))],
        out_specs=[pl.BlockSpec((ROW_BLOCK, S5_WIDTH), blk), _full_spec(_S5_BLOCKED), _full_spec(_S5_BLOCKED),
                   _full_spec((SUBLANES, 2 * NSTATE))],
        out_shape=[_sds((rows, S5_WIDTH)), _sds(_S5_BLOCKED), _sds(_S5_BLOCKED), _sds((SUBLANES, 2 * NSTATE))],
        scratch_shapes=[pltpu.VMEM((ROW_BLOCK, 2 * NSTATE), F32), pltpu.VMEM((ROW_BLOCK, S5_WIDTH), F32),
                        pltpu.VMEM((SUBLANES, NSTATE), F32), pltpu.VMEM((SUBLANES, NSTATE), F32)],
        compiler_params=_params(("arbitrary",)),
    )(dy, z_all, states, bmat, cmat, adj, perm, perm_t)


def _glu_fwd(z_all, y0, y1, d_skip, w_glu, n_rows):
    def body(u_ref, y0_ref, y1_ref, d_ref, w_ref, o_ref):
        y = d_ref[...] * u_ref[...] + y0_ref[...] + y1_ref[...]
        g = _gelu(y)
        t = _dot(g.astype(BF16), w_ref[...], "nn")
        o_ref[...] = (g * _sigmoid(t)).astype(o_ref.dtype)

    row = pl.BlockSpec((ROW_BLOCK, S5_WIDTH), lambda i: (i, 0))
    return pl.pallas_call(
        body, name="glu_fwd", grid=(n_rows // ROW_BLOCK,),
        in_specs=[row, row, row, pl.BlockSpec((1, S5_WIDTH), lambda i: (0, 0)),
                  pl.BlockSpec((S5_WIDTH, S5_WIDTH), lambda i: (0, 0))],
        out_specs=row, out_shape=_sds((n_rows, S5_WIDTH), BF16), compiler_params=_params(("parallel",)),
    )(z_all, y0, y1, d_skip, w_glu)


def _glu_bwd(d_ycat, z_all, y0, y1, d_skip, w_glu, n_rows):
    def body(do_ref, u_ref, y0_ref, y1_ref, d_ref, w_ref, dy_ref, dw_ref, dd_ref):
        @pl.when(pl.program_id(0) == 0)
        def _():
            dw_ref[...] = jnp.zeros_like(dw_ref)
            dd_ref[...] = jnp.zeros_like(dd_ref)

        u = u_ref[...]
        y = d_ref[...] * u + y0_ref[...] + y1_ref[...]
        g = _gelu(y)
        gb = g.astype(BF16)
        w = w_ref[...]
        sg = _sigmoid(_dot(gb, w, "nn"))
        do = do_ref[...]
        dt = do * g * sg * (1.0 - sg)
        dtb = dt.astype(BF16)
        dg = do * sg + _dot(dtb, w, "nt")
        dy = dg * _dgelu(y)
        dy_ref[...] = dy
        dw_ref[...] += _dot(gb, dtb, "tn")
        dd_ref[...] += _fold8(dy * u)

    row = pl.BlockSpec((ROW_BLOCK, S5_WIDTH), lambda i: (i, 0))
    sq = pl.BlockSpec((S5_WIDTH, S5_WIDTH), lambda i: (0, 0))
    return pl.pallas_call(
        body, name="glu_bwd", grid=(n_rows // ROW_BLOCK,),
        in_specs=[row, row, row, row, pl.BlockSpec((1, S5_WIDTH), lambda i: (0, 0)), sq],
        out_specs=[row, sq, pl.BlockSpec((SUBLANES, S5_WIDTH), lambda i: (0, 0))],
        out_shape=[_sds((n_rows, S5_WIDTH)), _sds((S5_WIDTH, S5_WIDTH)), _sds((SUBLANES, S5_WIDTH))],
        compiler_params=_params(("arbitrary",)),
    )(d_ycat, z_all, y0, y1, d_skip, w_glu)


CONV_HALF = CONV_K // 2


def _conv_block(n_rows):
    blk = min(1024, n_rows)
    assert blk >= CONV_HALF * GRID_W and n_rows % blk == 0
    return blk


def _conv_gate(z_all, n_rows):
    blk = _conv_block(n_rows)
    nb = n_rows // blk

    def body(v_ref, g_ref, o_ref):
        i = pl.program_id(0)
        inside = jnp.logical_and(i >= 1, i <= nb)

        @pl.when(inside)
        def _():
            o_ref[...] = v_ref[...] * _sigmoid(g_ref[...])

        @pl.when(jnp.logical_not(inside))
        def _():
            o_ref[...] = jnp.zeros_like(o_ref)

    src = lambda col: pl.BlockSpec((blk, CONV_WIDTH), lambda i: (jnp.clip(i - 1, 0, nb - 1), col))
    return pl.pallas_call(
        body, name="conv_gate", grid=(nb + 2,), in_specs=[src(1), src(2)],
        out_specs=pl.BlockSpec((blk, CONV_WIDTH), lambda i: (i, 0)),
        out_shape=_sds(((nb + 2) * blk, CONV_WIDTH)), compiler_params=_params(("parallel",)),
    )(z_all, z_all)


def _load_window(pad_ref, win, sem, blk):
    start = pl.multiple_of(pl.program_id(0) * blk, blk)
    copy = pltpu.make_async_copy(pad_ref.at[pl.ds(start, 3 * blk), :], win, sem)
    copy.start()
    copy.wait()


def _conv_fwd(hh_pad, w, b, ln_g, ln_b, n_rows):
    blk = _conv_block(n_rows)

    def body(hh_ref, w_ref, b_ref, g_ref, lb_ref, hc_ref, y_ref, win, sem):
        _load_window(hh_ref, win, sem, blk)

        def tile(t, _):
            r0 = pl.multiple_of(t * CONV_ROWS, CONV_ROWS)
            acc = jnp.zeros((CONV_ROWS, CONV_WIDTH), F32)
            for k in range(CONV_K):
                acc = acc + w_ref[k:k + 1, :] * win[pl.ds(r0 + blk + (k - CONV_HALF) * GRID_W, CONV_ROWS), :]
            hc = acc + b_ref[...]
            hc_ref[pl.ds(r0, CONV_ROWS), :] = hc
            mu = jnp.mean(hc, axis=-1, keepdims=True)
            xc = hc - mu
            ln = xc * lax.rsqrt(jnp.mean(xc * xc, axis=-1, keepdims=True) + EPS_LN) * g_ref[...] + lb_ref[...]
            y_ref[pl.ds(r0, CONV_ROWS), :] = _silu(ln).astype(y_ref.dtype)
            return 0

        lax.fori_loop(0, blk // CONV_ROWS, tile, 0)

    vec = pl.BlockSpec((1, CONV_WIDTH), lambda i: (0, 0))
    row = pl.BlockSpec((blk, CONV_WIDTH), lambda i: (i, 0))
    return pl.pallas_call(
        body, name="conv_fwd", grid=(n_rows // blk,),
        in_specs=[ANY, pl.BlockSpec((CONV_K, CONV_WIDTH), lambda i: (0, 0)), vec, vec, vec],
        out_specs=[row, row], out_shape=[_sds((n_rows, CONV_WIDTH)), _sds((n_rows, CONV_WIDTH), BF16)],
        scratch_shapes=[pltpu.VMEM((3 * blk, CONV_WIDTH), F32), pltpu.SemaphoreType.DMA],
        compiler_params=_params(("arbitrary",)),
    )(hh_pad, w, b, ln_g, ln_b)


def _conv_bwd_norm(d_ycat, hc, ln_g, ln_b, n_rows):
    blk = _conv_block(n_rows)
    nb = n_rows // blk

    def body(dy_ref, hc_ref, g_ref, lb_ref, o_ref, sums):
        i = pl.program_id(0)

        @pl.when(i == 0)
        def _():
            sums[...] = jnp.zeros_like(sums)

        inside = jnp.logical_and(i >= 1, i <= nb)

        @pl.when(inside)
        def _():
            hcv = hc_ref[...]
            mu = jnp.mean(hcv, axis=-1, keepdims=True)
            xc = hcv - mu
            rstd = lax.rsqrt(jnp.mean(xc * xc, axis=-1, keepdims=True) + EPS_LN)
            xh = xc * rstd
            g = g_ref[...]
            dln = dy_ref[...] * _dsilu(xh * g + lb_ref[...])
            dxh = dln * g
            dhc = rstd * (dxh - jnp.mean(dxh, axis=-1, keepdims=True) - xh * jnp.mean(dxh * xh, axis=-1, keepdims=True))
            o_ref[...] = dhc
            sums[0] += _fold8(dhc)
            sums[1] += _fold8(dln * xh)
            sums[2] += _fold8(dln)

        @pl.when(jnp.logical_not(inside))
        def _():
            o_ref[...] = jnp.zeros_like(o_ref)

    vec = pl.BlockSpec((1, CONV_WIDTH), lambda i: (0, 0))
    return pl.pallas_call(
        body, name="conv_bwd_norm", grid=(nb + 2,),
        in_specs=[pl.BlockSpec((blk, CONV_WIDTH), lambda i: (jnp.clip(i - 1, 0, nb - 1), 1)),
                  pl.BlockSpec((blk, CONV_WIDTH), lambda i: (jnp.clip(i - 1, 0, nb - 1), 0)), vec, vec],
        out_specs=[pl.BlockSpec((blk, CONV_WIDTH), lambda i: (i, 0)),
                   pl.BlockSpec((3, SUBLANES, CONV_WIDTH), lambda i: (0, 0, 0))],
        out_shape=[_sds(((nb + 2) * blk, CONV_WIDTH)), _sds((3, SUBLANES, CONV_WIDTH))],
        compiler_params=_params(("arbitrary",)),
    )(d_ycat, hc, ln_g, ln_b)


def _conv_bwd_taps(dhc_pad, hh_pad, z_all, w, n_rows):
    blk = _conv_block(n_rows)

    def body(dhc_ref, hh_ref, v_ref, g_ref, w_ref, dv_ref, dg_ref, dw_ref, dwin, hwin, sems):
        @pl.when(pl.program_id(0) == 0)
        def _():
            dw_ref[...] = jnp.zeros_like(dw_ref)

        _load_window(dhc_ref, dwin, sems.at[0], blk)
        _load_window(hh_ref, hwin, sems.at[1], blk)

        def tile(t, _):
            r0 = pl.multiple_of(t * CONV_ROWS, CONV_ROWS)
            dh = dwin[pl.ds(r0 + blk, CONV_ROWS), :]
            acc = jnp.zeros((CONV_ROWS, CONV_WIDTH), F32)
            for k in range(CONV_K):
                off = (k - CONV_HALF) * GRID_W
                acc = acc + w_ref[k:k + 1, :] * dwin[pl.ds(r0 + blk - off, CONV_ROWS), :]
                dw_ref[k] += _fold8(dh * hwin[pl.ds(r0 + blk + off, CONV_ROWS), :])
            rs = pl.ds(r0, CONV_ROWS)
            sg = _sigmoid(g_ref[rs, :])
            vv = v_ref[rs, :]
            dv_ref[rs, :] = acc * sg
            dg_ref[rs, :] = acc * vv * sg * (1.0 - sg)
            return 0

        lax.fori_loop(0, blk // CONV_ROWS, tile, 0)

    row = pl.BlockSpec((blk, CONV_WIDTH), lambda i: (i, 0))
    return pl.pallas_call(
        body, name="conv_bwd_taps", grid=(n_rows // blk,),
        in_specs=[ANY, ANY,
            pl.BlockSpec((blk, CONV_WIDTH), lambda i: (i, 1)), pl.BlockSpec((blk, CONV_WIDTH), lambda i: (i, 2)),
            pl.BlockSpec((CONV_K, CONV_WIDTH), lambda i: (0, 0))],
        out_specs=[row, row, pl.BlockSpec((CONV_K, SUBLANES, CONV_WIDTH), lambda i: (0, 0, 0))],
        out_shape=[_sds((n_rows, CONV_WIDTH)), _sds((n_rows, CONV_WIDTH)), _sds((CONV_K, SUBLANES, CONV_WIDTH))],
        scratch_shapes=[pltpu.VMEM((3 * blk, CONV_WIDTH), F32), pltpu.VMEM((3 * blk, CONV_WIDTH), F32),
                        pltpu.SemaphoreType.DMA((2,))],
        compiler_params=_params(("arbitrary",)),
    )(dhc_pad, hh_pad, z_all, z_all, w)


def _dz_assemble(du0, du1, dy, d_skip, dv, dgate, n_lat):
    rows = du0.shape[0]
    nb = rows // ROW_BLOCK

    def body(a_ref, b_ref, dy_ref, d_ref, dv_ref, dg_ref, o_ref):
        i, j = pl.program_id(0), pl.program_id(1)
        lat = i < n_lat

        @pl.when(jnp.logical_and(j == 0, lat))
        def _():
            o_ref[...] = (a_ref[...] + b_ref[...] + dy_ref[...] * d_ref[...]).astype(o_ref.dtype)

        @pl.when(jnp.logical_and(j == 0, jnp.logical_not(lat)))
        def _():
            o_ref[...] = (a_ref[...] + b_ref[...]).astype(o_ref.dtype)

        @pl.when(jnp.logical_and(j == 1, lat))
        def _():
            o_ref[...] = dv_ref[...].astype(o_ref.dtype)

        @pl.when(jnp.logical_and(j == 2, lat))
        def _():
            o_ref[...] = dg_ref[...].astype(o_ref.dtype)

        @pl.when(jnp.logical_and(j >= 1, jnp.logical_not(lat)))
        def _():
            o_ref[...] = jnp.zeros_like(o_ref)

    all_rows = pl.BlockSpec((ROW_BLOCK, S5_WIDTH), lambda i, j: (i, 0))
    lat_rows = pl.BlockSpec((ROW_BLOCK, S5_WIDTH), lambda i, j: (jnp.minimum(i, n_lat - 1), 0))
    return pl.pallas_call(
        body, name="dz_assemble", grid=(nb, 3),
        in_specs=[all_rows, all_rows, lat_rows, pl.BlockSpec((1, S5_WIDTH), lambda i, j: (0, 0)), lat_rows, lat_rows],
        out_specs=pl.BlockSpec((ROW_BLOCK, S5_WIDTH), lambda i, j: (i, j)),
        out_shape=_sds((rows, IN_COLS), BF16), compiler_params=_params(("parallel", "parallel")),
    )(du0, du1, dy, d_skip, dv, dgate)


def _sum_parts(parts):
    _, r, c = parts.shape

    def body(p_ref, o_ref):
        acc = p_ref[0]
        for q in range(1, NDEV):
            acc = acc + p_ref[q]
        o_ref[...] = acc

    return pl.pallas_call(body, name="sum_parts", out_shape=_sds((r, c)), compiler_params=_params())(parts)


def _row_tile(r, c):
    best = r
    for t in (1024, 512, 256, 128, 64, 32, 16, 8):
        if r % t == 0 and t * c <= 128 * 1024:
            return t
    return best


def _adamw(name, w, gparts, m, v):
    r, c = w.shape
    np_ = gparts.shape[0]
    tr = _row_tile(r, c)

    def body(w_ref, g_ref, m_ref, v_ref, go_ref, d_ref, mo_ref, vo_ref):
        g = g_ref[0]
        for q in range(1, np_):
            g = g + g_ref[q]
        m2 = ADAM_B1 * m_ref[...] + (1.0 - ADAM_B1) * g
        v2 = ADAM_B2 * v_ref[...] + (1.0 - ADAM_B2) * jnp.square(g)
        m_hat = m2 / (1.0 - ADAM_B1 ** ADAM_STEP)
        v_hat = v2 / (1.0 - ADAM_B2 ** ADAM_STEP)
        go_ref[...] = g
        d_ref[...] = -ADAM_LR * (m_hat / (jnp.sqrt(v_hat) + ADAM_EPS) + ADAM_WD * w_ref[...])
        mo_ref[...] = m2
        vo_ref[...] = v2

    row = pl.BlockSpec((tr, c), lambda i: (i, 0))
    return pl.pallas_call(
        body, name=name, grid=(r // tr,),
        in_specs=[row, pl.BlockSpec((np_, tr, c), lambda i: (0, i, 0)), row, row],
        out_specs=[row] * 4, out_shape=[_sds((r, c))] * 4, compiler_params=_params(("parallel",)),
    )(w, gparts, m, v)


SMALL = ["c_ctx", "ada_b", "norm1_g", "s5_lam_re", "s5_lam_im", "s5_log_dt", "s5_b_re", "s5_b_im", "s5_c_re",
         "s5_c_im", "s5_d", "conv_b", "conv_ln_g", "conv_ln_b", "norm2_g", "final_g"]


def _pack_small(parts):
    flat = jnp.concatenate([p.reshape(-1).astype(F32) for p in parts])
    return jnp.pad(flat, (0, SMALL_ROWS * D_MODEL - flat.shape[0])).reshape(SMALL_ROWS, D_MODEL)


def _unpack_small(packed, like):
    flat = packed.reshape(-1)
    out, off = [], 0
    for ref in like:
        out.append(flat[off:off + ref.size].reshape(ref.shape))
        off += ref.size
    return out


def kernel(x, c, ctx, c_ctx, ada_w, ada_b, norm1_g, w_in, s5_lam_re, s5_lam_im, s5_log_dt, s5_b_re, s5_b_im, s5_c_re, s5_c_im, s5_d, s5_w_glu, conv_w, conv_b, conv_ln_g, conv_ln_b, w_out, norm2_g, mlp_w1, mlp_w2, final_g, loss_target, m_c_ctx, m_ada_w, m_ada_b, m_norm1_g, m_w_in, m_s5_lam_re, m_s5_lam_im, m_s5_log_dt, m_s5_b_re, m_s5_b_im, m_s5_c_re, m_s5_c_im, m_s5_d, m_s5_w_glu, m_conv_w, m_conv_b, m_conv_ln_g, m_conv_ln_b, m_w_out, m_norm2_g, m_mlp_w1, m_mlp_w2, m_final_g, v_c_ctx, v_ada_w, v_ada_b, v_norm1_g, v_w_in, v_s5_lam_re, v_s5_lam_im, v_s5_log_dt, v_s5_b_re, v_s5_b_im, v_s5_c_re, v_s5_c_im, v_s5_d, v_s5_w_glu, v_conv_w, v_conv_b, v_conv_ln_g, v_conv_ln_b, v_w_out, v_norm2_g, v_mlp_w1, v_mlp_w2, v_final_g):
    weights = dict(c_ctx=c_ctx, ada_w=ada_w, ada_b=ada_b, norm1_g=norm1_g, w_in=w_in, s5_lam_re=s5_lam_re, s5_lam_im=s5_lam_im, s5_log_dt=s5_log_dt, s5_b_re=s5_b_re, s5_b_im=s5_b_im, s5_c_re=s5_c_re, s5_c_im=s5_c_im, s5_d=s5_d, s5_w_glu=s5_w_glu, conv_w=conv_w, conv_b=conv_b, conv_ln_g=conv_ln_g, conv_ln_b=conv_ln_b, w_out=w_out, norm2_g=norm2_g, mlp_w1=mlp_w1, mlp_w2=mlp_w2, final_g=final_g)
    mom1 = dict(c_ctx=m_c_ctx, ada_w=m_ada_w, ada_b=m_ada_b, norm1_g=m_norm1_g, w_in=m_w_in, s5_lam_re=m_s5_lam_re, s5_lam_im=m_s5_lam_im, s5_log_dt=m_s5_log_dt, s5_b_re=m_s5_b_re, s5_b_im=m_s5_b_im, s5_c_re=m_s5_c_re, s5_c_im=m_s5_c_im, s5_d=m_s5_d, s5_w_glu=m_s5_w_glu, conv_w=m_conv_w, conv_b=m_conv_b, conv_ln_g=m_conv_ln_g, conv_ln_b=m_conv_ln_b, w_out=m_w_out, norm2_g=m_norm2_g, mlp_w1=m_mlp_w1, mlp_w2=m_mlp_w2, final_g=m_final_g)
    mom2 = dict(c_ctx=v_c_ctx, ada_w=v_ada_w, ada_b=v_ada_b, norm1_g=v_norm1_g, w_in=v_w_in, s5_lam_re=v_s5_lam_re, s5_lam_im=v_s5_lam_im, s5_log_dt=v_s5_log_dt, s5_b_re=v_s5_b_re, s5_b_im=v_s5_b_im, s5_c_re=v_s5_c_re, s5_c_im=v_s5_c_im, s5_d=v_s5_d, s5_w_glu=v_s5_w_glu, conv_w=v_conv_w, conv_b=v_conv_b, conv_ln_g=v_conv_ln_g, conv_ln_b=v_conv_ln_b, w_out=v_w_out, norm2_g=v_norm2_g, mlp_w1=v_mlp_w1, mlp_w2=v_mlp_w2, final_g=v_final_g)
    order = list(weights)

    me = 4 * lax.axis_index("x") + 2 * lax.axis_index("y") + lax.axis_index("c")
    xs, cs, tgt = x[0], ctx[0], loss_target[0]
    n_lat_rows, n_ctx_rows = xs.shape[0], cs.shape[0]
    n_rows = n_lat_rows + n_ctx_rows
    n_lat = n_lat_rows // ROW_BLOCK
    ada_cols = ada_w.shape[2]

    w_in_loc, w1_loc = w_in[0].astype(BF16), mlp_w1[0].astype(BF16)
    w_in_g, glu_g, conv_w_g, w_out_g, w1_g, w2_g, c_all = _exchange(
        "gather_weights",
        [w_in_loc, s5_w_glu[0].astype(BF16), conv_w[0], w_out[0].astype(BF16), w1_loc, mlp_w2[0].astype(BF16), c],
        [True] * 7)
    w_in_full = jnp.transpose(w_in_g, (1, 0, 2)).reshape(D_MODEL, IN_COLS)
    glu_full = glu_g.reshape(S5_WIDTH, S5_WIDTH)
    conv_w_full = jnp.transpose(conv_w_g, (1, 0, 2)).reshape(CONV_K, CONV_WIDTH)
    w_out_full = w_out_g.reshape(D_MODEL, D_MODEL)
    w2_full = w2_g.reshape(D_FF, D_MODEL)
    c_all = c_all.reshape(NDEV, D_MODEL)

    cond_fwd = jnp.concatenate([c_all, c_ctx[None], jnp.zeros((7, D_MODEL), F32)])
    ada_b_loc = lax.dynamic_slice(ada_b, (0, me * ada_cols), (1, ada_cols))
    (mod_g,) = _exchange("gather_mod", [_ada_fwd(cond_fwd, ada_w[0], ada_b_loc)], [True])
    mod_rows = jnp.transpose(mod_g, (1, 0, 2)).reshape(16, 6 * D_MODEL)
    mod = lax.dynamic_slice(mod_rows, (me, 0), (1, 6 * D_MODEL)).reshape(6, D_MODEL)
    modc = mod_rows[8, :2 * D_MODEL].reshape(2, D_MODEL)
    sh1, sc1, g1, sh2, sc2, g2 = [mod[i:i + 1] for i in range(6)]

    a_all = _prenorm("prenorm1", xs, cs, norm1_g, jnp.stack([mod[0:2], modc]))
    (z_all,) = _matmul("in_proj", a_all, w_in_full, "nn", (n_rows, IN_COLS, D_MODEL), (ROW_BLOCK, IN_COLS, D_MODEL),
                       [((n_rows, IN_COLS), F32)])

    lam_re, lam_im = s5_lam_re[0].reshape(2, 1, NSTATE), s5_lam_im[0].reshape(2, 1, NSTATE)
    ldt = jnp.repeat(s5_log_dt[0], S5_STATE, axis=-1).reshape(2, 1, NSTATE)
    bt_re = jnp.transpose(s5_b_re[0], (0, 3, 1, 2)).reshape(2, S5_GROUP, NSTATE)
    bt_im = jnp.transpose(s5_b_im[0], (0, 3, 1, 2)).reshape(2, S5_GROUP, NSTATE)
    groups_per_block = S5_GROUPS // S5_BLOCKS
    ct_re = jnp.tile(s5_c_re[0].reshape(2, S5_WIDTH, S5_STATE), (1, 1, groups_per_block))
    ct_im = jnp.tile(s5_c_im[0].reshape(2, S5_WIDTH, S5_STATE), (1, 1, groups_per_block))
    d_skip = s5_d[0].reshape(1, S5_WIDTH)
    perm = _segment_permutation()
    perm_t = perm.T
    disc, states, y_dir = [], [], []
    for d in range(2):
        disc.append(_s5_discretise(f"s5_disc{d}", d == 0, lam_re[d], lam_im[d], ldt[d], bt_re[d], bt_im[d], ct_re[d], ct_im[d]))
        _, tab, _, bmat, cmat = disc[d]
        s, yd = _s5_scan_fwd(f"s5_scan_fwd{d}", d == 0, z_all, bmat, cmat, tab, perm, perm_t)
        states.append(s)
        y_dir.append(yd)
    y_s5 = _glu_fwd(z_all, y_dir[0], y_dir[1], d_skip, glu_full, n_lat_rows)

    hh_pad = _conv_gate(z_all, n_lat_rows)
    hc, y_conv = _conv_fwd(hh_pad, conv_w_full, conv_b, conv_ln_g, conv_ln_b, n_lat_rows)

    ycat = jnp.concatenate([y_s5, y_conv], axis=1)
    tm = min(1024, n_lat_rows)
    w1_cols = D_FF // NDEV
    row_vec = lambda tn: pl.BlockSpec((1, tn), lambda i, j, k: (0, j))
    out_tile = lambda t_m, t_n: pl.BlockSpec((t_m, t_n), lambda i, j, k: (i, j))
    gated = lambda acc, res, gate: (acc, res + gate * acc)
    mix, h1 = _matmul("out_proj", ycat, w_out_full, "nn", (n_lat_rows, D_MODEL, D_MODEL), (tm, D_MODEL, D_MODEL),
                      [((n_lat_rows, D_MODEL), F32)] * 2, epi=gated,
                      epi_extra=[(xs, out_tile(tm, D_MODEL)), (g1, row_vec(D_MODEL))])
    a2 = _prenorm("prenorm2", h1, None, norm2_g, mod[3:5][None])
    tm_up = min(2048, n_lat_rows)
    (f,) = _matmul("mlp_up", a2, w1_g, "nn", (n_lat_rows, D_FF, D_MODEL), (tm_up, w1_cols, D_MODEL),
                   [((n_lat_rows, D_FF), BF16)], b_spec=pl.BlockSpec((None, D_MODEL, w1_cols), lambda i, j, k: (j, 0, 0)))
    sq_relu = lambda t: jnp.square(jnp.maximum(t, 0.0))
    mlp_out, h2 = _matmul("mlp_down", f, w2_full, "nn", (n_lat_rows, D_MODEL, D_FF), (tm, D_MODEL, 1024),
                          [((n_lat_rows, D_MODEL), F32)] * 2, a_fn=sq_relu, epi=gated,
                          epi_extra=[(h1, out_tile(tm, D_MODEL)), (g2, row_vec(D_MODEL))])

    d_h2, dm2, err_sums, d_final_g8 = _loss_head(h2, tgt, final_g[None], g2)
    loss = lax.psum(0.5 / D_MODEL * jnp.sum(err_sums), ("x", "y", "c"))

    (d_f,) = _matmul("mlp_down_dx", dm2, w2_full, "nt", (n_lat_rows, D_FF, D_MODEL), (tm, 512, D_MODEL),
                     [((n_lat_rows, D_FF), BF16)],
                     epi=lambda acc, ft: (acc * 2.0 * jnp.maximum(ft.astype(F32), 0.0),), epi_extra=[(f, out_tile(tm, 512))])
    (g_w2,) = _matmul("mlp_down_dw", f, dm2, "tn", (D_FF, D_MODEL, n_lat_rows), (1024, D_MODEL, tm),
                      [((D_FF, D_MODEL), F32)], a_fn=sq_relu)
    (d_a2,) = _matmul("mlp_up_dx", d_f, w1_g, "nt", (n_lat_rows, D_MODEL, D_FF), (tm, D_MODEL, w1_cols),
                      [((n_lat_rows, D_MODEL), F32)],
                      b_spec=pl.BlockSpec((None, D_MODEL, w1_cols), lambda i, j, k: (k, 0, 0)))
    (g_w1,) = _matmul("mlp_up_dw", a2, d_f, "tn", (D_MODEL, D_FF, n_lat_rows), (D_MODEL, w1_cols, tm),
                      [((NDEV, D_MODEL, w1_cols), F32)],
                      out_specs=[pl.BlockSpec((None, D_MODEL, w1_cols), lambda i, j, k: (j, 0, 0))])
    d_h1, dm1, sums2 = _norm_bwd("norm2_bwd", h1, d_a2, 0, norm2_g, sc2, res=d_h2, aux=mlp_out, gate=g1)

    (d_ycat,) = _matmul("out_proj_dx", dm1, w_out_full, "nt", (n_lat_rows, D_MODEL, D_MODEL), (tm, D_MODEL, D_MODEL),
                        [((n_lat_rows, D_MODEL), F32)])
    (g_w_out,) = _matmul("out_proj_dw", ycat, dm1, "tn", (D_MODEL, D_MODEL, n_lat_rows), (D_MODEL, D_MODEL, 512),
                         [((D_MODEL, D_MODEL), F32)])

    dy, g_glu, dd8 = _glu_bwd(d_ycat, z_all, y_dir[0], y_dir[1], d_skip, glu_full, n_lat_rows)
    du, g_lam_re, g_lam_im, g_ldt, g_b_re, g_b_im, g_c_re, g_c_im = [], [], [], [], [], [], [], []

    def diag(mat):
        return jnp.diagonal(mat.reshape(S5_BLOCKS, groups_per_block, S5_GROUP, 2, groups_per_block, S5_STATE), axis1=1, axis2=4)

    for d in range(2):
        _, _, adj, bmat, cmat = disc[d]
        du_d, d_bmat, d_cmat, d_abar8 = _s5_scan_bwd(f"s5_scan_bwd{d}", d == 0, dy, z_all, states[d], bmat, cmat, adj,
                                                     perm, perm_t)
        du.append(du_d)
        d_bbar = jnp.transpose(diag(d_bmat), (2, 1, 0, 4, 3)).reshape(2 * S5_GROUP, NSTATE)
        d_c = jnp.transpose(diag(d_cmat), (2, 0, 4, 1, 3)).reshape(2, S5_GROUPS, S5_GROUP, S5_STATE)
        d_lam8, d_bt = _s5_discretise_bwd(f"s5_disc_bwd{d}", lam_re[d], lam_im[d], ldt[d], bt_re[d], bt_im[d], d_abar8, d_bbar)
        g_lam_re.append(d_lam8[0].reshape(S5_GROUPS, S5_STATE))
        g_lam_im.append(d_lam8[1].reshape(S5_GROUPS, S5_STATE))
        g_ldt.append(d_lam8[2].reshape(S5_GROUPS, S5_STATE).sum(axis=-1))
        to_gph = lambda t: jnp.transpose(t.reshape(S5_GROUP, S5_GROUPS, S5_STATE), (1, 2, 0))
        g_b_re.append(to_gph(d_bt[:S5_GROUP]))
        g_b_im.append(to_gph(d_bt[S5_GROUP:]))
        g_c_re.append(d_c[0])
        g_c_im.append(-d_c[1])

    dhc_pad, conv_sums = _conv_bwd_norm(d_ycat, hc, conv_ln_g, conv_ln_b, n_lat_rows)
    d_v, d_gate, g_conv_w8 = _conv_bwd_taps(dhc_pad, hh_pad, z_all, conv_w_full, n_lat_rows)

    dz_all = _dz_assemble(du[0], du[1], dy, d_skip, d_v, d_gate, n_lat)
    (d_a_all,) = _matmul("in_proj_dx", dz_all, w_in_full, "nt", (n_rows, D_MODEL, IN_COLS), (ROW_BLOCK, D_MODEL, IN_COLS),
                         [((n_rows, D_MODEL), F32)])
    (g_w_in_full,) = _matmul("in_proj_dw", a_all, dz_all, "tn", (D_MODEL, IN_COLS, n_rows), (D_MODEL, IN_COLS, ROW_BLOCK),
                             [((D_MODEL, IN_COLS), F32)])
    grad_x, sums1 = _norm_bwd("norm1_bwd", xs, d_a_all, 0, norm1_g, sc1, res=d_h1, aux=mix)
    (sums1c,) = _norm_bwd("norm1_bwd_ctx", cs, d_a_all, n_lat, norm1_g, modc[1:2])

    s1, s1c, s2 = sums1.sum(axis=1), sums1c.sum(axis=1), sums2.sum(axis=1)
    d_mod = jnp.concatenate([s1[0], s1[1], s1[3], s2[0], s2[1], s2[3]])
    d_modc = jnp.concatenate([s1c[0], s1c[1], jnp.zeros((4 * D_MODEL,), F32)])
    (dmod_g,) = _exchange("gather_dmod", [jnp.stack([d_mod, d_modc])], [True])
    dmod16 = jnp.concatenate([dmod_g[:, 0], dmod_g[:, 1]])
    dmod16_loc = lax.dynamic_slice(dmod16, (0, me * ada_cols), (16, ada_cols))
    cond_bwd = jnp.concatenate([c_all, jnp.broadcast_to(c_ctx[None], (NDEV, D_MODEL))])
    g_ada_w, g_c_ctx8 = _ada_bwd(cond_bwd, dmod16_loc, ada_w[0], c_ctx[None])

    small_parts = dict(
        c_ctx=g_c_ctx8[0], ada_b=d_mod + d_modc, norm1_g=s1[2] + s1c[2],
        s5_lam_re=jnp.stack(g_lam_re), s5_lam_im=jnp.stack(g_lam_im), s5_log_dt=jnp.stack(g_ldt),
        s5_b_re=jnp.stack(g_b_re), s5_b_im=jnp.stack(g_b_im), s5_c_re=jnp.stack(g_c_re), s5_c_im=jnp.stack(g_c_im),
        s5_d=dd8.sum(axis=0), conv_b=conv_sums[0].sum(axis=0), conv_ln_g=conv_sums[1].sum(axis=0),
        conv_ln_b=conv_sums[2].sum(axis=0), norm2_g=s2[2], final_g=d_final_g8.sum(axis=0))
    small_g = _pack_small([small_parts[n] for n in SMALL]).reshape(NDEV, SMALL_ROWS // NDEV, D_MODEL)
    g_w_in_parts = jnp.transpose(g_w_in_full.reshape(D_MODEL, NDEV, IN_COLS // NDEV), (1, 0, 2))
    g_conv_w_parts = jnp.transpose(g_conv_w8.sum(axis=1).reshape(CONV_K, NDEV, CONV_WIDTH // NDEV), (1, 0, 2))
    p_w_in, p_glu, p_conv_w, p_w_out, p_w1, p_w2, p_small = _exchange(
        "scatter_grads",
        [g_w_in_parts, g_glu.reshape(NDEV, S5_WIDTH // NDEV, S5_WIDTH), g_conv_w_parts,
         g_w_out.reshape(NDEV, D_MODEL // NDEV, D_MODEL), g_w1, g_w2.reshape(NDEV, D_FF // NDEV, D_MODEL), small_g],
        [False] * 7)
    (small_all,) = _exchange("gather_small", [_sum_parts(p_small)], [True])
    small_all = small_all.reshape(1, SMALL_ROWS, D_MODEL)

    res = {}
    big = dict(ada_w=g_ada_w[None], w_in=p_w_in, s5_w_glu=p_glu, conv_w=p_conv_w, w_out=p_w_out, mlp_w1=p_w1, mlp_w2=p_w2)
    for name, parts in big.items():
        outs = _adamw("adamw_" + name, weights[name][0], parts, mom1[name][0], mom2[name][0])
        res[name] = [o[None] for o in outs]
    small_like = [weights[n] for n in SMALL]
    outs = _adamw("adamw_small", _pack_small(small_like), small_all, _pack_small([mom1[n] for n in SMALL]),
                  _pack_small([mom2[n] for n in SMALL]))
    unpacked = [_unpack_small(o, small_like) for o in outs]
    for i, name in enumerate(SMALL):
        res[name] = [u[i] for u in unpacked]

    return (loss, grad_x[None], *[res[n][0] for n in order], *[res[n][1] for n in order],
            *[res[n][2] for n in order], *[res[n][3] for n in order])
```

```python
import functools

import jax
import jax.numpy as jnp
from jax import lax
from jax.experimental import pallas as pl
from jax.experimental.pallas import tpu as pltpu

F32 = jnp.float32
BF16 = jnp.bfloat16
MESH = pl.DeviceIdType.MESH
ANY = pl.BlockSpec(memory_space=pl.ANY)

NDEV = 8
D_MODEL = 1024
GRID_W = 64
S5_WIDTH = 512
S5_GROUP = 16
S5_GROUPS = 32
S5_STATE = 64
NSTATE = S5_GROUPS * S5_STATE
CONV_WIDTH = 512
CONV_K = 31
IN_COLS = S5_WIDTH + 2 * CONV_WIDTH
D_FF = 4 * D_MODEL
EPS_RMS = 1e-6
EPS_LN = 1e-5
ADAM_LR = 0.001
ADAM_B1 = 0.9
ADAM_B2 = 0.999
ADAM_EPS = 1e-08
ADAM_WD = 0.01
ADAM_STEP = 10

SUBLANES = 8
LANES = 128
ROW_BLOCK = 256
SCAN_LANES = 512
SEGMENTS = SUBLANES
STEPS = ROW_BLOCK // SEGMENTS
S5_BLOCKS = 4
S5_BLOCK_WIDTH = S5_WIDTH // S5_BLOCKS
CONV_ROWS = 64
VMEM_LIMIT = 48 * 1024 * 1024
SMALL_ROWS = 320


def _params(sem=None):
    kw = dict(vmem_limit_bytes=VMEM_LIMIT)
    if sem is not None:
        kw["dimension_semantics"] = sem
    return pltpu.CompilerParams(**kw)


def _sds(shape, dtype=F32):
    return jax.ShapeDtypeStruct(tuple(shape), dtype)


def _fold8(x):
    return x.reshape(x.shape[0] // SUBLANES, SUBLANES, x.shape[1]).sum(axis=0)


def _sigmoid(x):
    return 1.0 / (1.0 + jnp.exp(-x))


def _silu(x):
    return x * _sigmoid(x)


def _dsilu(x):
    s = _sigmoid(x)
    return s * (1.0 + x * (1.0 - s))


_GELU_C = 0.7978845608028654


def _gelu(x):
    return 0.5 * x * (1.0 + jnp.tanh(_GELU_C * (x + 0.044715 * x * x * x)))


def _dgelu(x):
    t = jnp.tanh(_GELU_C * (x + 0.044715 * x * x * x))
    return 0.5 * (1.0 + t) + 0.5 * x * (1.0 - t * t) * _GELU_C * (1.0 + 3.0 * 0.044715 * x * x)


def _dot(a, b, mode):
    dims = {"nn": (((1,), (0,)), ((), ())), "nt": (((1,), (1,)), ((), ())), "tn": (((0,), (0,)), ((), ()))}[mode]
    return lax.dot_general(a, b, dims, preferred_element_type=F32)


def _exchange(name, srcs, gather):
    n = len(srcs)
    outs = [_sds(((NDEV,) + s.shape) if g else s.shape, s.dtype) for s, g in zip(srcs, gather)]

    def body(*refs):
        src, dst = refs[:n], refs[n:2 * n]
        send_sems, recv_sems, local_sems = refs[2 * n:]
        x, y, c = lax.axis_index("x"), lax.axis_index("y"), lax.axis_index("c")
        me = 4 * x + 2 * y + c
        started = []
        for a in range(n):
            def chunk(dest, a=a):
                return src[a] if gather[a] else src[a].at[dest]

            local = pltpu.make_async_copy(chunk(me), dst[a].at[me], local_sems.at[a])
            local.start()
            for k in range(1, NDEV):
                px = 1 - x if k & 4 else x
                py = 1 - y if k & 2 else y
                pc = 1 - c if k & 1 else c
                peer = 4 * px + 2 * py + pc
                copy = pltpu.make_async_remote_copy(
                    src_ref=chunk(peer), dst_ref=dst[a].at[me],
                    send_sem=send_sems.at[a * (NDEV - 1) + k - 1], recv_sem=recv_sems.at[a * (NDEV - 1) + k - 1],
                    device_id=(px, py, pc), device_id_type=MESH)
                copy.start()
                landing = pltpu.make_async_remote_copy(
                    src_ref=chunk(peer), dst_ref=dst[a].at[peer],
                    send_sem=send_sems.at[a * (NDEV - 1) + k - 1], recv_sem=recv_sems.at[a * (NDEV - 1) + k - 1],
                    device_id=(px, py, pc), device_id_type=MESH)
                started.append((copy, landing))
            started.append((local, None))
        for copy, landing in started:
            if landing is None:
                copy.wait()
            else:
                copy.wait_send()
                landing.wait_recv()

    return pl.pallas_call(
        body, name=name, out_shape=outs, in_specs=[ANY] * n, out_specs=[ANY] * n,
        scratch_shapes=[pltpu.SemaphoreType.DMA((n * (NDEV - 1),)), pltpu.SemaphoreType.DMA((n * (NDEV - 1),)),
                        pltpu.SemaphoreType.DMA((n,))],
    )(*srcs)


HBM = pl.BlockSpec(memory_space=pltpu.HBM)
SEM = pl.BlockSpec(memory_space=pltpu.SEMAPHORE)
EFFECT = pltpu.SideEffectType.DATAFLOW_SIDE_EFFECTING


def _peers(x, y, c):
    out = []
    for k in range(1, NDEV):
        px = 1 - x if k & 4 else x
        py = 1 - y if k & 2 else y
        pc = 1 - c if k & 1 else c
        out.append(((px, py, pc), 4 * px + 2 * py + pc))
    return out


def _exchange_copies(src, land, send_sems, recv_sems, gather):
    x, y, c = lax.axis_index("x"), lax.axis_index("y"), lax.axis_index("c")
    me = 4 * x + 2 * y + c
    out = []
    for a in range(len(src)):
        for k, (peer, plin) in enumerate(_peers(x, y, c)):
            chunk = src[a] if gather[a] else src[a].at[plin]
            sems = dict(send_sem=send_sems.at[a * (NDEV - 1) + k], recv_sem=recv_sems.at[a * (NDEV - 1) + k],
                        device_id=peer, device_id_type=MESH)
            out.append((pltpu.make_async_remote_copy(src_ref=chunk, dst_ref=land[a].at[me], **sems),
                        pltpu.make_async_remote_copy(src_ref=chunk, dst_ref=land[a].at[plin], **sems)))
    return out


def _exchange_start(name, srcs, gather):
    n = len(srcs)
    lands = [lax.empty(((NDEV,) + s.shape) if g else s.shape, s.dtype) for s, g in zip(srcs, gather)]

    def body(*refs):
        src, land = refs[:n], refs[n:2 * n]
        send_sems, recv_sems = refs[2 * n], refs[2 * n + 1]
        token = refs[-1]
        for copy, _ in _exchange_copies(src, land, send_sems, recv_sems, gather):
            copy.start()
        token[...] = jnp.zeros_like(token)

    hbm = lambda v: pltpu.HBM(v.shape, v.dtype)
    nsem = n * (NDEV - 1)
    out = pl.pallas_call(
        body, name=name,
        out_shape=(pltpu.SemaphoreType.DMA((nsem,)), pltpu.SemaphoreType.DMA((nsem,)), *[hbm(v) for v in srcs],
                   *[hbm(v) for v in lands], _sds((SUBLANES, LANES))),
        in_specs=[HBM] * (2 * n), out_specs=(SEM, SEM, *([HBM] * (2 * n)), pl.BlockSpec(memory_space=pltpu.VMEM)),
        input_output_aliases={i: 2 + i for i in range(2 * n)},
        compiler_params=pltpu.CompilerParams(has_side_effects=EFFECT),
    )(*[pltpu.with_memory_space_constraint(v, pltpu.HBM) for v in list(srcs) + lands])
    return out[0], out[1], out[2:2 + n], out[2 + n:2 + 2 * n], out[-1]


def _exchange_wait(name, send_sems, recv_sems, srcs, lands, gather, after):
    n = len(srcs)

    def body(*refs):
        src, land = refs[:n], refs[n:2 * n]
        send_ref, recv_ref = refs[2 * n], refs[2 * n + 1]
        for copy, landing in _exchange_copies(src, land, send_ref, recv_ref, gather):
            copy.wait_send()
            landing.wait_recv()

    hbm = lambda v: pltpu.HBM(v.shape, v.dtype)
    out = pl.pallas_call(
        body, name=name, out_shape=[hbm(v) for v in list(srcs) + list(lands)],
        in_specs=[HBM] * (2 * n) + [SEM, SEM, ANY], out_specs=[HBM] * (2 * n),
        input_output_aliases={i: i for i in range(2 * n)},
        compiler_params=pltpu.CompilerParams(has_side_effects=EFFECT),
    )(*srcs, *lands, send_sems, recv_sems, after)
    return out[:n], out[n:]


def _with_own(landed, own, me):
    return lax.dynamic_update_slice(landed, own[None], (me,) + (0,) * own.ndim)


def _matmul(name, a, b, mode, mnk, tiles, outs, a_spec=None, b_spec=None, a_fn=None, a_extra=(),
            epi=None, epi_extra=(), out_specs=None):
    m_, n_, k_ = mnk
    tm, tn, tk = tiles
    nk = k_ // tk
    if a_spec is None:
        a_spec = (pl.BlockSpec((tk, tm), lambda i, j, k: (k, i)) if mode == "tn"
                  else pl.BlockSpec((tm, tk), lambda i, j, k: (i, k)))
    if b_spec is None:
        b_spec = (pl.BlockSpec((tn, tk), lambda i, j, k: (j, k)) if mode == "nt"
                  else pl.BlockSpec((tk, tn), lambda i, j, k: (k, j)))
    if out_specs is None:
        out_specs = [pl.BlockSpec((tm, tn), lambda i, j, k: (i, j)) for _ in outs]
    na, ne, no = len(a_extra), len(epi_extra), len(outs)

    def body(*refs):
        a_ref, b_ref = refs[0], refs[1]
        ax = refs[2:2 + na]
        ex = refs[2 + na:2 + na + ne]
        o = refs[2 + na + ne:2 + na + ne + no]

        def finish(res):
            res = epi(res, *[r[...] for r in ex]) if epi is not None else (res,)
            for ref, val in zip(o, res):
                ref[...] = val.astype(ref.dtype)

        at = a_ref[...]
        if a_fn is not None:
            at = a_fn(at, *[r[...] for r in ax])
        part = _dot(at.astype(BF16), b_ref[...].astype(BF16), mode)
        if nk == 1:
            finish(part)
            return
        acc = refs[-1]
        k = pl.program_id(2)

        @pl.when(k == 0)
        def _():
            acc[...] = part

        @pl.when(k > 0)
        def _():
            acc[...] += part

        @pl.when(k == nk - 1)
        def _():
            finish(acc[...])

    return pl.pallas_call(
        body, name=name, grid=(m_ // tm, n_ // tn, nk),
        in_specs=[a_spec, b_spec] + [s for _, s in a_extra] + [s for _, s in epi_extra],
        out_specs=out_specs, out_shape=[_sds(s, d) for s, d in outs],
        scratch_shapes=[pltpu.VMEM((tm, tn), F32)] if nk > 1 else [],
        compiler_params=_params(("parallel", "parallel", "arbitrary")),
    )(a, b, *[x for x, _ in a_extra], *[x for x, _ in epi_extra])


def _prenorm(name, x, ctx, gain, shsc):
    n_lat = x.shape[0] // ROW_BLOCK
    n_ctx = 0 if ctx is None else ctx.shape[0] // ROW_BLOCK
    d = x.shape[1]

    def norm(src, g_ref, m_ref, o_ref):
        xv = src[...]
        xh = xv * lax.rsqrt(jnp.mean(xv * xv, axis=-1, keepdims=True) + EPS_RMS)
        o_ref[...] = ((xh * g_ref[...]) * (1.0 + m_ref[1:2, :]) + m_ref[0:1, :]).astype(o_ref.dtype)

    def body(*refs):
        if ctx is None:
            x_ref, g_ref, m_ref, o_ref = refs
            norm(x_ref, g_ref, m_ref, o_ref)
        else:
            x_ref, c_ref, g_ref, m_ref, o_ref = refs
            i = pl.program_id(0)

            @pl.when(i < n_lat)
            def _():
                norm(x_ref, g_ref, m_ref, o_ref)

            @pl.when(i >= n_lat)
            def _():
                norm(c_ref, g_ref, m_ref, o_ref)

    in_specs = [pl.BlockSpec((ROW_BLOCK, d), lambda i: (jnp.minimum(i, n_lat - 1), 0))]
    args = [x]
    if ctx is not None:
        in_specs.append(pl.BlockSpec((ROW_BLOCK, d), lambda i: (jnp.maximum(i - n_lat, 0), 0)))
        args.append(ctx)
    in_specs += [pl.BlockSpec((1, d), lambda i: (0, 0)),
                 pl.BlockSpec((None, 2, d), lambda i: (jnp.minimum(i // n_lat, 1), 0, 0))]
    args += [gain, shsc]
    return pl.pallas_call(
        body, name=name, grid=(n_lat + n_ctx,), in_specs=in_specs,
        out_specs=pl.BlockSpec((ROW_BLOCK, d), lambda i: (i, 0)),
        out_shape=_sds(((n_lat + n_ctx) * ROW_BLOCK, d), BF16),
        compiler_params=_params(("parallel",)),
    )(*args)


def _norm_bwd(name, x, d_act, d_act_row0, gain, scale, res=None, aux=None, gate=None):
    rows, d = x.shape
    nb = rows // ROW_BLOCK
    has_res = res is not None
    has_gate = gate is not None

    def body(*refs):
        if has_gate:
            x_ref, da_ref, g_ref, sc_ref, r_ref, aux_ref, gate_ref, dx_ref, dm_ref, sums = refs
        elif has_res:
            x_ref, da_ref, g_ref, sc_ref, r_ref, aux_ref, dx_ref, sums = refs
        else:
            x_ref, da_ref, g_ref, sc_ref, sums = refs
        i = pl.program_id(0)

        @pl.when(i == 0)
        def _():
            sums[...] = jnp.zeros_like(sums)

        xv, da = x_ref[...], da_ref[...]
        rstd = lax.rsqrt(jnp.mean(xv * xv, axis=-1, keepdims=True) + EPS_RMS)
        xh = xv * rstd
        g = g_ref[...]
        dn = da * (1.0 + sc_ref[...])
        sums[0] += _fold8(da)
        sums[1] += _fold8(da * (xh * g))
        sums[2] += _fold8(dn * xh)
        if has_res:
            dxh = dn * g
            dx = rstd * (dxh - xh * jnp.mean(dxh * xh, axis=-1, keepdims=True))
            rv = r_ref[...]
            dx_ref[...] = rv + dx
            sums[3] += _fold8(rv * aux_ref[...])
            if has_gate:
                dm_ref[...] = ((rv + dx) * gate_ref[...]).astype(dm_ref.dtype)

    row = lambda i: (i, 0)
    vec = pl.BlockSpec((1, d), lambda i: (0, 0))
    in_specs = [pl.BlockSpec((ROW_BLOCK, d), row), pl.BlockSpec((ROW_BLOCK, d), lambda i: (i + d_act_row0, 0)), vec, vec]
    args = [x, d_act, gain, scale]
    out_shape = [_sds((4, SUBLANES, d))]
    out_specs = [pl.BlockSpec((4, SUBLANES, d), lambda i: (0, 0, 0))]
    if has_res:
        in_specs += [pl.BlockSpec((ROW_BLOCK, d), row), pl.BlockSpec((ROW_BLOCK, d), row)]
        args += [res, aux]
        if has_gate:
            in_specs.append(vec)
            args.append(gate)
            out_shape = [_sds((rows, d), BF16)] + out_shape
            out_specs = [pl.BlockSpec((ROW_BLOCK, d), row)] + out_specs
        out_shape = [_sds((rows, d))] + out_shape
        out_specs = [pl.BlockSpec((ROW_BLOCK, d), row)] + out_specs
    return pl.pallas_call(
        body, name=name, grid=(nb,), in_specs=in_specs, out_specs=out_specs, out_shape=out_shape,
        compiler_params=_params(("arbitrary",)),
    )(*args)


def _loss_head(h2, target, gain, gate):
    rows, d = h2.shape

    def body(h_ref, t_ref, g_ref, gate_ref, dh_ref, dm_ref, err_ref, dg_ref):
        i = pl.program_id(0)

        @pl.when(i == 0)
        def _():
            err_ref[...] = jnp.zeros_like(err_ref)
            dg_ref[...] = jnp.zeros_like(dg_ref)

        hv = h_ref[...]
        rstd = lax.rsqrt(jnp.mean(hv * hv, axis=-1, keepdims=True) + EPS_RMS)
        xh = hv * rstd
        g = g_ref[...]
        err = xh * g - t_ref[...]
        err_ref[...] += _fold8(err * err)
        dy = err * (1.0 / d)
        dg_ref[...] += _fold8(dy * xh)
        dxh = dy * g
        dh = rstd * (dxh - xh * jnp.mean(dxh * xh, axis=-1, keepdims=True))
        dh_ref[...] = dh
        dm_ref[...] = (dh * gate_ref[...]).astype(dm_ref.dtype)

    row = pl.BlockSpec((ROW_BLOCK, d), lambda i: (i, 0))
    acc = pl.BlockSpec((SUBLANES, d), lambda i: (0, 0))
    vec = pl.BlockSpec((1, d), lambda i: (0, 0))
    return pl.pallas_call(
        body, name="loss_head", grid=(rows // ROW_BLOCK,),
        in_specs=[row, row, vec, vec], out_specs=[row, row, acc, acc],
        out_shape=[_sds((rows, d)), _sds((rows, d), BF16), _sds((SUBLANES, d)), _sds((SUBLANES, d))],
        compiler_params=_params(("arbitrary",)),
    )(h2, target, gain, gate)


def _ada_fwd(cond16, ada_w_loc, ada_b_loc):
    cols = ada_w_loc.shape[1]

    def body(c_ref, w_ref, b_ref, o_ref):
        s = _silu(c_ref[...]).astype(BF16)
        o_ref[...] = _dot(s, w_ref[...].astype(BF16), "nn") + b_ref[...]

    return pl.pallas_call(body, name="ada_fwd", out_shape=_sds((16, cols)), compiler_params=_params())(
        cond16, ada_w_loc, ada_b_loc)


def _ada_bwd(cond16, dmod16, ada_w_loc, c_ctx_row):
    k_, cols = ada_w_loc.shape

    def body(c_ref, dm_ref, w_ref, cc_ref, gw_ref, gc_ref):
        s = _silu(c_ref[...]).astype(BF16)
        dm = dm_ref[...]
        gw_ref[...] = _dot(s, dm.astype(BF16), "tn")
        dmc = jnp.sum(dm[8:16, :], axis=0, keepdims=True)
        dmc8 = jnp.broadcast_to(dmc, (SUBLANES, cols)).astype(BF16)
        ds = _dot(dmc8, w_ref[...].astype(BF16), "nt")
        row = lax.broadcasted_iota(jnp.int32, ds.shape, 0)
        gc_ref[...] = jnp.where(row == 0, ds * _dsilu(cc_ref[...]), 0.0)

    return pl.pallas_call(body, name="ada_bwd", out_shape=[_sds((k_, cols)), _sds((SUBLANES, k_))],
                          compiler_params=_params())(cond16, dmod16, ada_w_loc, c_ctx_row)


def _cmul(a, b):
    return a[0] * b[0] - a[1] * b[1], a[0] * b[1] + a[1] * b[0]


def _disc(lam_re, lam_im, ldt):
    dt = jnp.exp(ldt)
    mag = jnp.exp(lam_re * dt)
    th = lam_im * dt
    a_re, a_im = mag * jnp.cos(th), mag * jnp.sin(th)
    den = lam_re * lam_re + lam_im * lam_im
    n_re = a_re - 1.0
    f_re = (n_re * lam_re + a_im * lam_im) / den
    f_im = (a_im * lam_re - n_re * lam_im) / den
    return dt, mag, th, a_re, a_im, den, n_re, f_re, f_im


def _block_diag_mask(shape):
    row = lax.broadcasted_iota(jnp.int32, shape, 0)
    col = lax.broadcasted_iota(jnp.int32, shape, 1)
    return lax.shift_right_logical(row, 4) == lax.shift_right_logical(col, 6)


TAB_A = 0
TAB_BIG = 1
TAB_SEG = 4
TAB_PW = 5
TAB_ROWS = TAB_PW + STEPS


def _s5_discretise(name, ascending, lam_re, lam_im, ldt, bt_re, bt_im, ct_re, ct_im):
    def write_tables(ref, pw, big, asc, sign):
        row = lax.broadcasted_iota(jnp.int32, (SUBLANES, NSTATE), 0)
        full = lambda v: jnp.broadcast_to(v, (SUBLANES, NSTATE))

        def put(t, p):
            ref[0, t] = full(p[0])
            ref[1, t] = full(sign * p[1])

        put(TAB_A, pw[0])
        for t in range(3):
            put(TAB_BIG + t, big[t])
        seg = [big[0]]
        for _ in range(SEGMENTS - 1):
            seg.append(_cmul(seg[-1], big[0]))
        seg_re = jnp.zeros((SUBLANES, NSTATE), F32)
        seg_im = jnp.zeros((SUBLANES, NSTATE), F32)
        for r in range(SEGMENTS):
            p = seg[r] if asc else seg[SEGMENTS - 1 - r]
            seg_re = jnp.where(row == r, p[0], seg_re)
            seg_im = jnp.where(row == r, sign * p[1], seg_im)
        ref[0, TAB_SEG] = seg_re
        ref[1, TAB_SEG] = seg_im
        for k in range(STEPS):
            put(TAB_PW + k, pw[k])

    def body(lr_ref, li_ref, ldt_ref, br_ref, bi_ref, cr_ref, ci_ref, bb_ref, tab_ref, adj_ref, bm_ref, cm_ref):
        _, _, _, a_re, a_im, _, _, f_re, f_im = _disc(lr_ref[...], li_ref[...], ldt_ref[...])
        bre, bim = br_ref[...], bi_ref[...]
        bb_re = f_re * bre - f_im * bim
        bb_im = f_re * bim + f_im * bre
        bb_ref[0:S5_GROUP, :] = bb_re
        bb_ref[S5_GROUP:2 * S5_GROUP, :] = bb_im
        pw = [(a_re, a_im)]
        for _ in range(STEPS - 1):
            pw.append(_cmul(pw[-1], (a_re, a_im)))
        big = [pw[STEPS - 1]]
        for _ in range(2):
            big.append(_cmul(big[-1], big[-1]))
        write_tables(tab_ref, pw, big, ascending, 1.0)
        write_tables(adj_ref, pw, big, not ascending, -1.0)
        half = NSTATE // S5_BLOCKS
        mask = _block_diag_mask((S5_BLOCK_WIDTH, half))
        tile = lambda v: jnp.broadcast_to(v[None], (S5_BLOCK_WIDTH // S5_GROUP, S5_GROUP, half)).reshape(S5_BLOCK_WIDTH, half)
        for c in range(S5_BLOCKS):
            cols = slice(c * half, (c + 1) * half)
            rows = slice(c * S5_BLOCK_WIDTH, (c + 1) * S5_BLOCK_WIDTH)
            bm_ref[c, :, 0:half] = jnp.where(mask, tile(bb_re[:, cols]), 0.0).astype(BF16)
            bm_ref[c, :, half:2 * half] = jnp.where(mask, tile(bb_im[:, cols]), 0.0).astype(BF16)
            cm_ref[c, :, 0:half] = jnp.where(mask, cr_ref[rows, :], 0.0).astype(BF16)
            cm_ref[c, :, half:2 * half] = jnp.where(mask, -ci_ref[rows, :], 0.0).astype(BF16)

    blocked = _sds((S5_BLOCKS, S5_BLOCK_WIDTH, 2 * NSTATE // S5_BLOCKS), BF16)
    return pl.pallas_call(
        body, name=name,
        out_shape=[_sds((2 * S5_GROUP, NSTATE)), _sds((2, TAB_ROWS, SUBLANES, NSTATE)),
                   _sds((2, TAB_ROWS, SUBLANES, NSTATE)), blocked, blocked],
        compiler_params=_params(),
    )(lam_re, lam_im, ldt, bt_re, bt_im, ct_re, ct_im)


def _s5_discretise_bwd(name, lam_re, lam_im, ldt, bt_re, bt_im, d_abar8, d_bbar):
    def body(lr_ref, li_ref, ldt_ref, br_ref, bi_ref, da_ref, db_ref, dl_ref, dbt_ref):
        lam_re, lam_im = lr_ref[...], li_ref[...]
        dt, mag, _, a_re, a_im, den, n_re, f_re, f_im = _disc(lam_re, lam_im, ldt_ref[...])
        bre, bim = br_ref[...], bi_ref[...]
        dbr, dbi = db_ref[0:S5_GROUP, :], db_ref[S5_GROUP:2 * S5_GROUP, :]
        dbt_ref[0:S5_GROUP, :] = f_re * dbr + f_im * dbi
        dbt_ref[S5_GROUP:2 * S5_GROUP, :] = f_re * dbi - f_im * dbr
        df_re = jnp.sum(bre * dbr + bim * dbi, axis=0, keepdims=True)
        df_im = jnp.sum(bre * dbi - bim * dbr, axis=0, keepdims=True)
        da = da_ref[...]
        da_re = jnp.sum(da[:, 0:NSTATE], axis=0, keepdims=True)
        da_im = jnp.sum(da[:, NSTATE:2 * NSTATE], axis=0, keepdims=True)
        da_re = da_re + (df_re * lam_re - df_im * lam_im) / den
        da_im = da_im + (df_re * lam_im + df_im * lam_re) / den
        ff = (f_re * df_re + f_im * df_im) * 2.0 / den
        d_lr = (df_re * n_re + df_im * a_im) / den - ff * lam_re
        d_li = (df_re * a_im - df_im * n_re) / den - ff * lam_im
        d_mag = (da_re * a_re + da_im * a_im) / mag
        d_th = da_im * a_re - da_re * a_im
        d_lr = d_lr + d_mag * mag * dt
        d_li = d_li + d_th * dt
        d_ldt = (d_mag * mag * lam_re + d_th * lam_im) * dt
        row = lax.broadcasted_iota(jnp.int32, (SUBLANES, NSTATE), 0)
        dl_ref[...] = jnp.where(row == 0, d_lr, jnp.where(row == 1, d_li, jnp.where(row == 2, d_ldt, 0.0)))

    return pl.pallas_call(
        body, name=name, out_shape=[_sds((SUBLANES, NSTATE)), _sds((2 * S5_GROUP, NSTATE))],
        compiler_params=_params(),
    )(lam_re, lam_im, ldt, bt_re, bt_im, d_abar8, d_bbar)


def _segment_permutation():
    rho = jnp.arange(ROW_BLOCK)
    src = STEPS * (rho % SEGMENTS) + rho // SEGMENTS
    return (src[:, None] == jnp.arange(ROW_BLOCK)[None, :]).astype(BF16)


def _permute_rows(perm_ref, v):
    return _dot(perm_ref[...], v, "nn").astype(BF16)


def _unpermute_rows(perm_t_ref, v):
    hi = v.astype(BF16)
    lo = (v - hi.astype(F32)).astype(BF16)
    return _dot(perm_t_ref[...], hi, "nn") + _dot(perm_t_ref[...], lo, "nn")


def _scan_chunk(x_ref, out_ref, tab_ref, carry_re, carry_im, ascending, pair_ref=None, acc_ref=None):
    w = SCAN_LANES
    half = NSTATE // S5_BLOCKS
    row = lax.broadcasted_iota(jnp.int32, (SUBLANES, w), 0)
    last = (SEGMENTS - 1) if ascending else 0

    def from_previous_segment(v, k, fill):
        if ascending:
            return jnp.where(row >= k, pltpu.roll(v, k, 0), fill)
        return jnp.where(row < SEGMENTS - k, pltpu.roll(v, SEGMENTS - k, 0), fill)

    def tile_rows(k):
        return pl.ds(pl.multiple_of((k if ascending else STEPS - 1 - k) * SUBLANES, SUBLANES), SUBLANES)

    for j in range(NSTATE // w):
        n_l = pl.ds(j * w, w)
        lane0 = (j * w // half) * 2 * half + (j * w) % half
        re_l, im_l = pl.ds(lane0, w), pl.ds(lane0 + half, w)
        tab = lambda t, n_l=n_l: (tab_ref[0, t, :, n_l], tab_ref[1, t, :, n_l])
        a_re, a_im = tab(TAB_A)

        def local_step(k, h):
            rs = tile_rows(k)
            h_re = a_re * h[0] - a_im * h[1] + x_ref[rs, re_l]
            h_im = a_re * h[1] + a_im * h[0] + x_ref[rs, im_l]
            out_ref[rs, re_l] = h_re
            out_ref[rs, im_l] = h_im
            return h_re, h_im

        zero = jnp.zeros((SUBLANES, w), F32)
        end_re, end_im = lax.fori_loop(0, STEPS, local_step, (zero, zero))
        for t, k in ((TAB_BIG, 1), (TAB_BIG + 1, 2), (TAB_BIG + 2, 4)):
            p_re, p_im = tab(t)
            s_re, s_im = from_previous_segment(end_re, k, 0.0), from_previous_segment(end_im, k, 0.0)
            end_re, end_im = end_re + (p_re * s_re - p_im * s_im), end_im + (p_re * s_im + p_im * s_re)
        c0_re, c0_im = carry_re[:, n_l], carry_im[:, n_l]
        p_re, p_im = tab(TAB_SEG)
        end_re = end_re + (p_re * c0_re - p_im * c0_im)
        end_im = end_im + (p_re * c0_im + p_im * c0_re)
        carry_re[:, n_l] = jnp.broadcast_to(end_re[last:last + 1, :], end_re.shape)
        carry_im[:, n_l] = jnp.broadcast_to(end_im[last:last + 1, :], end_im.shape)
        in_re = from_previous_segment(end_re, 1, c0_re)
        in_im = from_previous_segment(end_im, 1, c0_im)

        def carry_step(k, st):
            rs = tile_rows(k)
            p_re, p_im = tab_ref[0, TAB_PW + k, :, n_l], tab_ref[1, TAB_PW + k, :, n_l]
            o_re = out_ref[rs, re_l] + (p_re * in_re - p_im * in_im)
            o_im = out_ref[rs, im_l] + (p_re * in_im + p_im * in_re)
            out_ref[rs, re_l] = o_re
            out_ref[rs, im_l] = o_im
            if pair_ref is None:
                return st
            s_re, s_im = pair_ref[rs, re_l], pair_ref[rs, im_l]
            return (o_re, o_im, st[2] + (st[0] * s_re + st[1] * s_im), st[3] + (st[1] * s_re - st[0] * s_im))

        if pair_ref is None:
            lax.fori_loop(0, STEPS, carry_step, 0)
        else:
            fin = lax.fori_loop(0, STEPS, carry_step, (in_re, in_im, zero, zero))
            acc_ref[:, n_l] += fin[2]
            acc_ref[:, pl.ds(NSTATE + j * w, w)] += fin[3]


def _scan_block_index(i, n_lat, ctx_first_then_ascending):
    if ctx_first_then_ascending:
        return jnp.where(i == 0, n_lat, i - 1)
    return jnp.where(i == 0, n_lat, n_lat - i)


def _full_spec(shape):
    return pl.BlockSpec(shape, lambda i: (0,) * len(shape))


_S5_BLOCKED = (S5_BLOCKS, S5_BLOCK_WIDTH, 2 * NSTATE // S5_BLOCKS)
_S5_TABLES = (2, TAB_ROWS, SUBLANES, NSTATE)


def _s5_scan_fwd(name, ascending, z_all, bmat, cmat, tab, perm, perm_t):
    rows = z_all.shape[0]
    nb = rows // ROW_BLOCK
    n_lat = nb - 1
    bw, sw = S5_BLOCK_WIDTH, 2 * NSTATE // S5_BLOCKS

    def body(u_ref, bm_ref, cm_ref, tab_ref, p_ref, pt_ref, s_ref, y_ref, bu, yp, carry_re, carry_im):
        @pl.when(pl.program_id(0) == 0)
        def _():
            carry_re[...] = jnp.zeros_like(carry_re)
            carry_im[...] = jnp.zeros_like(carry_im)

        up = _permute_rows(p_ref, u_ref[...].astype(BF16))
        for c in range(S5_BLOCKS):
            bu[:, c * sw:(c + 1) * sw] = _dot(up[:, c * bw:(c + 1) * bw], bm_ref[c], "nn")
        _scan_chunk(bu, s_ref, tab_ref, carry_re, carry_im, ascending)
        for c in range(S5_BLOCKS):
            yp[:, c * bw:(c + 1) * bw] = _dot(s_ref[:, c * sw:(c + 1) * sw].astype(BF16), cm_ref[c], "nt")
        y_ref[...] = _unpermute_rows(pt_ref, yp[...])

    blk = lambda i: (_scan_block_index(i, n_lat, ascending), 0)
    return pl.pallas_call(
        body, name=name, grid=(nb,),
        in_specs=[pl.BlockSpec((ROW_BLOCK, S5_WIDTH), blk), _full_spec(_S5_BLOCKED), _full_spec(_S5_BLOCKED),
                  _full_spec(_S5_TABLES), _full_spec((ROW_BLOCK, ROW_BLOCK)), _full_spec((ROW_BLOCK, ROW_BLOCK))],
        out_specs=[pl.BlockSpec((ROW_BLOCK, 2 * NSTATE), blk), pl.BlockSpec((ROW_BLOCK, S5_WIDTH), blk)],
        out_shape=[_sds((rows, 2 * NSTATE)), _sds((rows, S5_WIDTH))],
        scratch_shapes=[pltpu.VMEM((ROW_BLOCK, 2 * NSTATE), F32), pltpu.VMEM((ROW_BLOCK, S5_WIDTH), F32),
                        pltpu.VMEM((SUBLANES, NSTATE), F32), pltpu.VMEM((SUBLANES, NSTATE), F32)],
        compiler_params=_params(("arbitrary",)),
    )(z_all, bmat, cmat, tab, perm, perm_t)


def _s5_scan_bwd(name, ascending, dy, z_all, states, bmat, cmat, adj, perm, perm_t):
    rows = states.shape[0]
    nb = rows // ROW_BLOCK
    n_lat = nb - 1
    bw, sw = S5_BLOCK_WIDTH, 2 * NSTATE // S5_BLOCKS

    def block_index(i):
        if ascending:
            return jnp.where(i == nb - 1, n_lat, n_lat - 1 - i)
        return jnp.where(i == nb - 1, n_lat, i)

    def body(dy_ref, u_ref, s_ref, bm_ref, cm_ref, adj_ref, p_ref, pt_ref, du_ref, db_ref, dc_ref, da_ref,
             g, dup, carry_re, carry_im):
        i = pl.program_id(0)

        @pl.when(i == 0)
        def _():
            carry_re[...] = jnp.zeros_like(carry_re)
            carry_im[...] = jnp.zeros_like(carry_im)
            da_ref[...] = jnp.zeros_like(da_ref)
            db_ref[...] = jnp.zeros_like(db_ref)
            dc_ref[...] = jnp.zeros_like(dc_ref)

        @pl.when(i < nb - 1)
        def _():
            dyp = _permute_rows(p_ref, dy_ref[...].astype(BF16))
            for c in range(S5_BLOCKS):
                g[:, c * sw:(c + 1) * sw] = _dot(dyp[:, c * bw:(c + 1) * bw], cm_ref[c], "nn")
                dc_ref[c] += _dot(dyp[:, c * bw:(c + 1) * bw], s_ref[:, c * sw:(c + 1) * sw].astype(BF16), "tn")

        @pl.when(i == nb - 1)
        def _():
            g[...] = jnp.zeros_like(g)

        _scan_chunk(g, g, adj_ref, carry_re, carry_im, not ascending, pair_ref=s_ref, acc_ref=da_ref)
        up = _permute_rows(p_ref, u_ref[...].astype(BF16))
        for c in range(S5_BLOCKS):
            gc = g[:, c * sw:(c + 1) * sw].astype(BF16)
            dup[:, c * bw:(c + 1) * bw] = _dot(gc, bm_ref[c], "nt")
            db_ref[c] += _dot(up[:, c * bw:(c + 1) * bw], gc, "tn")
        du_ref[...] = _unpermute_rows(pt_ref, dup[...])

    blk = lambda i: (block_index(i), 0)
    return pl.pallas_call(
        body, name=name, grid=(nb,),
        in_specs=[pl.BlockSpec((ROW_BLOCK, S5_WIDTH), lambda i: (jnp.minimum(block_index(i), n_lat - 1), 0)),
                  pl.BlockSpec((ROW_BLOCK, S5_WIDTH), blk), pl.BlockSpec((ROW_BLOCK, 2 * NSTATE), blk),
                  _full_spec(_S5_BLOCKED), _full_spec(_S5_BLOCKED), _full_spec(_S5_TABLES),
                  _full_spec((ROW_BLOCK, ROW_BLOCK)), _full_spec((ROW_BLOCK, ROW_BLOCK))],
        out_specs=[pl.BlockSpec((ROW_BLOCK, S5_WIDTH), blk), _full_spec(_S5_BLOCKED), _full_spec(_S5_BLOCKED),
                   _full_spec((SUBLANES, 2 * NSTATE))],
        out_shape=[_sds((rows, S5_WIDTH)), _sds(_S5_BLOCKED), _sds(_S5_BLOCKED), _sds((SUBLANES, 2 * NSTATE))],
        scratch_shapes=[pltpu.VMEM((ROW_BLOCK, 2 * NSTATE), F32), pltpu.VMEM((ROW_BLOCK, S5_WIDTH), F32),
                        pltpu.VMEM((SUBLANES, NSTATE), F32), pltpu.VMEM((SUBLANES, NSTATE), F32)],
        compiler_params=_params(("arbitrary",)),
    )(dy, z_all, states, bmat, cmat, adj, perm, perm_t)


def _glu_fwd(z_all, y0, y1, d_skip, w_glu, n_rows):
    def body(u_ref, y0_ref, y1_ref, d_ref, w_ref, o_ref):
        y = d_ref[...] * u_ref[...] + y0_ref[...] + y1_ref[...]
        g = _gelu(y)
        t = _dot(g.astype(BF16), w_ref[...], "nn")
        o_ref[...] = (g * _sigmoid(t)).astype(o_ref.dtype)

    row = pl.BlockSpec((ROW_BLOCK, S5_WIDTH), lambda i: (i, 0))
    return pl.pallas_call(
        body, name="glu_fwd", grid=(n_rows // ROW_BLOCK,),
        in_specs=[row, row, row, pl.BlockSpec((1, S5_WIDTH), lambda i: (0, 0)),
                  pl.BlockSpec((S5_WIDTH, S5_WIDTH), lambda i: (0, 0))],
        out_specs=row, out_shape=_sds((n_rows, S5_WIDTH), BF16), compiler_params=_params(("parallel",)),
    )(z_all, y0, y1, d_skip, w_glu)


def _glu_bwd(d_ycat, z_all, y0, y1, d_skip, w_glu, n_rows):
    def body(do_ref, u_ref, y0_ref, y1_ref, d_ref, w_ref, dy_ref, dw_ref, dd_ref):
        @pl.when(pl.program_id(0) == 0)
        def _():
            dw_ref[...] = jnp.zeros_like(dw_ref)
            dd_ref[...] = jnp.zeros_like(dd_ref)

        u = u_ref[...]
        y = d_ref[...] * u + y0_ref[...] + y1_ref[...]
        g = _gelu(y)
        gb = g.astype(BF16)
        w = w_ref[...]
        sg = _sigmoid(_dot(gb, w, "nn"))
        do = do_ref[...]
        dt = do * g * sg * (1.0 - sg)
        dtb = dt.astype(BF16)
        dg = do * sg + _dot(dtb, w, "nt")
        dy = dg * _dgelu(y)
        dy_ref[...] = dy
        dw_ref[...] += _dot(gb, dtb, "tn")
        dd_ref[...] += _fold8(dy * u)

    row = pl.BlockSpec((ROW_BLOCK, S5_WIDTH), lambda i: (i, 0))
    sq = pl.BlockSpec((S5_WIDTH, S5_WIDTH), lambda i: (0, 0))
    return pl.pallas_call(
        body, name="glu_bwd", grid=(n_rows // ROW_BLOCK,),
        in_specs=[row, row, row, row, pl.BlockSpec((1, S5_WIDTH), lambda i: (0, 0)), sq],
        out_specs=[row, sq, pl.BlockSpec((SUBLANES, S5_WIDTH), lambda i: (0, 0))],
        out_shape=[_sds((n_rows, S5_WIDTH)), _sds((S5_WIDTH, S5_WIDTH)), _sds((SUBLANES, S5_WIDTH))],
        compiler_params=_params(("arbitrary",)),
    )(d_ycat, z_all, y0, y1, d_skip, w_glu)


CONV_HALF = CONV_K // 2


def _conv_block(n_rows):
    blk = min(1024, n_rows)
    assert blk >= CONV_HALF * GRID_W and n_rows % blk == 0
    return blk


def _conv_gate(z_all, n_rows):
    blk = _conv_block(n_rows)
    nb = n_rows // blk

    def body(v_ref, g_ref, o_ref):
        i = pl.program_id(0)
        inside = jnp.logical_and(i >= 1, i <= nb)

        @pl.when(inside)
        def _():
            o_ref[...] = v_ref[...] * _sigmoid(g_ref[...])

        @pl.when(jnp.logical_not(inside))
        def _():
            o_ref[...] = jnp.zeros_like(o_ref)

    src = lambda col: pl.BlockSpec((blk, CONV_WIDTH), lambda i: (jnp.clip(i - 1, 0, nb - 1), col))
    return pl.pallas_call(
        body, name="conv_gate", grid=(nb + 2,), in_specs=[src(1), src(2)],
        out_specs=pl.BlockSpec((blk, CONV_WIDTH), lambda i: (i, 0)),
        out_shape=_sds(((nb + 2) * blk, CONV_WIDTH)), compiler_params=_params(("parallel",)),
    )(z_all, z_all)


def _load_window(pad_ref, win, sem, blk):
    start = pl.multiple_of(pl.program_id(0) * blk, blk)
    copy = pltpu.make_async_copy(pad_ref.at[pl.ds(start, 3 * blk), :], win, sem)
    copy.start()
    copy.wait()


def _conv_fwd(hh_pad, w, b, ln_g, ln_b, n_rows):
    blk = _conv_block(n_rows)

    def body(hh_ref, w_ref, b_ref, g_ref, lb_ref, hc_ref, y_ref, win, sem):
        _load_window(hh_ref, win, sem, blk)

        def tile(t, _):
            r0 = pl.multiple_of(t * CONV_ROWS, CONV_ROWS)
            acc = jnp.zeros((CONV_ROWS, CONV_WIDTH), F32)
            for k in range(CONV_K):
                acc = acc + w_ref[k:k + 1, :] * win[pl.ds(r0 + blk + (k - CONV_HALF) * GRID_W, CONV_ROWS), :]
            hc = acc + b_ref[...]
            hc_ref[pl.ds(r0, CONV_ROWS), :] = hc
            mu = jnp.mean(hc, axis=-1, keepdims=True)
            xc = hc - mu
            ln = xc * lax.rsqrt(jnp.mean(xc * xc, axis=-1, keepdims=True) + EPS_LN) * g_ref[...] + lb_ref[...]
            y_ref[pl.ds(r0, CONV_ROWS), :] = _silu(ln).astype(y_ref.dtype)
            return 0

        lax.fori_loop(0, blk // CONV_ROWS, tile, 0)

    vec = pl.BlockSpec((1, CONV_WIDTH), lambda i: (0, 0))
    row = pl.BlockSpec((blk, CONV_WIDTH), lambda i: (i, 0))
    return pl.pallas_call(
        body, name="conv_fwd", grid=(n_rows // blk,),
        in_specs=[ANY, pl.BlockSpec((CONV_K, CONV_WIDTH), lambda i: (0, 0)), vec, vec, vec],
        out_specs=[row, row], out_shape=[_sds((n_rows, CONV_WIDTH)), _sds((n_rows, CONV_WIDTH), BF16)],
        scratch_shapes=[pltpu.VMEM((3 * blk, CONV_WIDTH), F32), pltpu.SemaphoreType.DMA],
        compiler_params=_params(("arbitrary",)),
    )(hh_pad, w, b, ln_g, ln_b)


def _conv_bwd_norm(d_ycat, hc, ln_g, ln_b, n_rows):
    blk = _conv_block(n_rows)
    nb = n_rows // blk

    def body(dy_ref, hc_ref, g_ref, lb_ref, o_ref, sums):
        i = pl.program_id(0)

        @pl.when(i == 0)
        def _():
            sums[...] = jnp.zeros_like(sums)

        inside = jnp.logical_and(i >= 1, i <= nb)

        @pl.when(inside)
        def _():
            hcv = hc_ref[...]
            mu = jnp.mean(hcv, axis=-1, keepdims=True)
            xc = hcv - mu
            rstd = lax.rsqrt(jnp.mean(xc * xc, axis=-1, keepdims=True) + EPS_LN)
            xh = xc * rstd
            g = g_ref[...]
            dln = dy_ref[...] * _dsilu(xh * g + lb_ref[...])
            dxh = dln * g
            dhc = rstd * (dxh - jnp.mean(dxh, axis=-1, keepdims=True) - xh * jnp.mean(dxh * xh, axis=-1, keepdims=True))
            o_ref[...] = dhc
            sums[0] += _fold8(dhc)
            sums[1] += _fold8(dln * xh)
            sums[2] += _fold8(dln)

        @pl.when(jnp.logical_not(inside))
        def _():
            o_ref[...] = jnp.zeros_like(o_ref)

    vec = pl.BlockSpec((1, CONV_WIDTH), lambda i: (0, 0))
    return pl.pallas_call(
        body, name="conv_bwd_norm", grid=(nb + 2,),
        in_specs=[pl.BlockSpec((blk, CONV_WIDTH), lambda i: (jnp.clip(i - 1, 0, nb - 1), 1)),
                  pl.BlockSpec((blk, CONV_WIDTH), lambda i: (jnp.clip(i - 1, 0, nb - 1), 0)), vec, vec],
        out_specs=[pl.BlockSpec((blk, CONV_WIDTH), lambda i: (i, 0)),
                   pl.BlockSpec((3, SUBLANES, CONV_WIDTH), lambda i: (0, 0, 0))],
        out_shape=[_sds(((nb + 2) * blk, CONV_WIDTH)), _sds((3, SUBLANES, CONV_WIDTH))],
        compiler_params=_params(("arbitrary",)),
    )(d_ycat, hc, ln_g, ln_b)


def _conv_bwd_taps(dhc_pad, hh_pad, z_all, w, n_rows):
    blk = _conv_block(n_rows)

    def body(dhc_ref, hh_ref, v_ref, g_ref, w_ref, dv_ref, dg_ref, dw_ref, dwin, hwin, sems):
        @pl.when(pl.program_id(0) == 0)
        def _():
            dw_ref[...] = jnp.zeros_like(dw_ref)

        _load_window(dhc_ref, dwin, sems.at[0], blk)
        _load_window(hh_ref, hwin, sems.at[1], blk)

        def tile(t, _):
            r0 = pl.multiple_of(t * CONV_ROWS, CONV_ROWS)
            dh = dwin[pl.ds(r0 + blk, CONV_ROWS), :]
            acc = jnp.zeros((CONV_ROWS, CONV_WIDTH), F32)
            for k in range(CONV_K):
                off = (k - CONV_HALF) * GRID_W
                acc = acc + w_ref[k:k + 1, :] * dwin[pl.ds(r0 + blk - off, CONV_ROWS), :]
                dw_ref[k] += _fold8(dh * hwin[pl.ds(r0 + blk + off, CONV_ROWS), :])
            rs = pl.ds(r0, CONV_ROWS)
            sg = _sigmoid(g_ref[rs, :])
            vv = v_ref[rs, :]
            dv_ref[rs, :] = acc * sg
            dg_ref[rs, :] = acc * vv * sg * (1.0 - sg)
            return 0

        lax.fori_loop(0, blk // CONV_ROWS, tile, 0)

    row = pl.BlockSpec((blk, CONV_WIDTH), lambda i: (i, 0))
    return pl.pallas_call(
        body, name="conv_bwd_taps", grid=(n_rows // blk,),
        in_specs=[ANY, ANY,
            pl.BlockSpec((blk, CONV_WIDTH), lambda i: (i, 1)), pl.BlockSpec((blk, CONV_WIDTH), lambda i: (i, 2)),
            pl.BlockSpec((CONV_K, CONV_WIDTH), lambda i: (0, 0))],
        out_specs=[row, row, pl.BlockSpec((CONV_K, SUBLANES, CONV_WIDTH), lambda i: (0, 0, 0))],
        out_shape=[_sds((n_rows, CONV_WIDTH)), _sds((n_rows, CONV_WIDTH)), _sds((CONV_K, SUBLANES, CONV_WIDTH))],
        scratch_shapes=[pltpu.VMEM((3 * blk, CONV_WIDTH), F32), pltpu.VMEM((3 * blk, CONV_WIDTH), F32),
                        pltpu.SemaphoreType.DMA((2,))],
        compiler_params=_params(("arbitrary",)),
    )(dhc_pad, hh_pad, z_all, z_all, w)


def _dz_assemble(du0, du1, dy, d_skip, dv, dgate, n_lat):
    rows = du0.shape[0]
    nb = rows // ROW_BLOCK

    def body(a_ref, b_ref, dy_ref, d_ref, dv_ref, dg_ref, o_ref):
        i, j = pl.program_id(0), pl.program_id(1)
        lat = i < n_lat

        @pl.when(jnp.logical_and(j == 0, lat))
        def _():
            o_ref[...] = (a_ref[...] + b_ref[...] + dy_ref[...] * d_ref[...]).astype(o_ref.dtype)

        @pl.when(jnp.logical_and(j == 0, jnp.logical_not(lat)))
        def _():
            o_ref[...] = (a_ref[...] + b_ref[...]).astype(o_ref.dtype)

        @pl.when(jnp.logical_and(j == 1, lat))
        def _():
            o_ref[...] = dv_ref[...].astype(o_ref.dtype)

        @pl.when(jnp.logical_and(j == 2, lat))
        def _():
            o_ref[...] = dg_ref[...].astype(o_ref.dtype)

        @pl.when(jnp.logical_and(j >= 1, jnp.logical_not(lat)))
        def _():
            o_ref[...] = jnp.zeros_like(o_ref)

    all_rows = pl.BlockSpec((ROW_BLOCK, S5_WIDTH), lambda i, j: (i, 0))
    lat_rows = pl.BlockSpec((ROW_BLOCK, S5_WIDTH), lambda i, j: (jnp.minimum(i, n_lat - 1), 0))
    return pl.pallas_call(
        body, name="dz_assemble", grid=(nb, 3),
        in_specs=[all_rows, all_rows, lat_rows, pl.BlockSpec((1, S5_WIDTH), lambda i, j: (0, 0)), lat_rows, lat_rows],
        out_specs=pl.BlockSpec((ROW_BLOCK, S5_WIDTH), lambda i, j: (i, j)),
        out_shape=_sds((rows, IN_COLS), BF16), compiler_params=_params(("parallel", "parallel")),
    )(du0, du1, dy, d_skip, dv, dgate)


def _sum_parts(parts):
    _, r, c = parts.shape

    def body(p_ref, o_ref):
        acc = p_ref[0]
        for q in range(1, NDEV):
            acc = acc + p_ref[q]
        o_ref[...] = acc

    return pl.pallas_call(body, name="sum_parts", out_shape=_sds((r, c)), compiler_params=_params())(parts)


def _row_tile(r, c):
    best = r
    for t in (1024, 512, 256, 128, 64, 32, 16, 8):
        if r % t == 0 and t * c <= 128 * 1024:
            return t
    return best


def _adamw(name, w, gparts, m, v):
    r, c = w.shape
    np_ = gparts.shape[0]
    tr = _row_tile(r, c)

    def body(w_ref, g_ref, m_ref, v_ref, go_ref, d_ref, mo_ref, vo_ref):
        g = g_ref[0]
        for q in range(1, np_):
            g = g + g_ref[q]
        m2 = ADAM_B1 * m_ref[...] + (1.0 - ADAM_B1) * g
        v2 = ADAM_B2 * v_ref[...] + (1.0 - ADAM_B2) * jnp.square(g)
        m_hat = m2 / (1.0 - ADAM_B1 ** ADAM_STEP)
        v_hat = v2 / (1.0 - ADAM_B2 ** ADAM_STEP)
        go_ref[...] = g
        d_ref[...] = -ADAM_LR * (m_hat / (jnp.sqrt(v_hat) + ADAM_EPS) + ADAM_WD * w_ref[...])
        mo_ref[...] = m2
        vo_ref[...] = v2

    row = pl.BlockSpec((tr, c), lambda i: (i, 0))
    return pl.pallas_call(
        body, name=name, grid=(r // tr,),
        in_specs=[row, pl.BlockSpec((np_, tr, c), lambda i: (0, i, 0)), row, row],
        out_specs=[row] * 4, out_shape=[_sds((r, c))] * 4, compiler_params=_params(("parallel",)),
    )(w, gparts, m, v)


SMALL = ["c_ctx", "ada_b", "norm1_g", "s5_lam_re", "s5_lam_im", "s5_log_dt", "s5_b_re", "s5_b_im", "s5_c_re",
         "s5_c_im", "s5_d", "conv_b", "conv_ln_g", "conv_ln_b", "norm2_g", "final_g"]


def _pack_small(parts):
    flat = jnp.concatenate([p.reshape(-1).astype(F32) for p in parts])
    return jnp.pad(flat, (0, SMALL_ROWS * D_MODEL - flat.shape[0])).reshape(SMALL_ROWS, D_MODEL)


def _unpack_small(packed, like):
    flat = packed.reshape(-1)
    out, off = [], 0
    for ref in like:
        out.append(flat[off:off + ref.size].reshape(ref.shape))
        off += ref.size
    return out


def kernel(x, c, ctx, c_ctx, ada_w, ada_b, norm1_g, w_in, s5_lam_re, s5_lam_im, s5_log_dt, s5_b_re, s5_b_im, s5_c_re, s5_c_im, s5_d, s5_w_glu, conv_w, conv_b, conv_ln_g, conv_ln_b, w_out, norm2_g, mlp_w1, mlp_w2, final_g, loss_target, m_c_ctx, m_ada_w, m_ada_b, m_norm1_g, m_w_in, m_s5_lam_re, m_s5_lam_im, m_s5_log_dt, m_s5_b_re, m_s5_b_im, m_s5_c_re, m_s5_c_im, m_s5_d, m_s5_w_glu, m_conv_w, m_conv_b, m_conv_ln_g, m_conv_ln_b, m_w_out, m_norm2_g, m_mlp_w1, m_mlp_w2, m_final_g, v_c_ctx, v_ada_w, v_ada_b, v_norm1_g, v_w_in, v_s5_lam_re, v_s5_lam_im, v_s5_log_dt, v_s5_b_re, v_s5_b_im, v_s5_c_re, v_s5_c_im, v_s5_d, v_s5_w_glu, v_conv_w, v_conv_b, v_conv_ln_g, v_conv_ln_b, v_w_out, v_norm2_g, v_mlp_w1, v_mlp_w2, v_final_g):
    weights = dict(c_ctx=c_ctx, ada_w=ada_w, ada_b=ada_b, norm1_g=norm1_g, w_in=w_in, s5_lam_re=s5_lam_re, s5_lam_im=s5_lam_im, s5_log_dt=s5_log_dt, s5_b_re=s5_b_re, s5_b_im=s5_b_im, s5_c_re=s5_c_re, s5_c_im=s5_c_im, s5_d=s5_d, s5_w_glu=s5_w_glu, conv_w=conv_w, conv_b=conv_b, conv_ln_g=conv_ln_g, conv_ln_b=conv_ln_b, w_out=w_out, norm2_g=norm2_g, mlp_w1=mlp_w1, mlp_w2=mlp_w2, final_g=final_g)
    mom1 = dict(c_ctx=m_c_ctx, ada_w=m_ada_w, ada_b=m_ada_b, norm1_g=m_norm1_g, w_in=m_w_in, s5_lam_re=m_s5_lam_re, s5_lam_im=m_s5_lam_im, s5_log_dt=m_s5_log_dt, s5_b_re=m_s5_b_re, s5_b_im=m_s5_b_im, s5_c_re=m_s5_c_re, s5_c_im=m_s5_c_im, s5_d=m_s5_d, s5_w_glu=m_s5_w_glu, conv_w=m_conv_w, conv_b=m_conv_b, conv_ln_g=m_conv_ln_g, conv_ln_b=m_conv_ln_b, w_out=m_w_out, norm2_g=m_norm2_g, mlp_w1=m_mlp_w1, mlp_w2=m_mlp_w2, final_g=m_final_g)
    mom2 = dict(c_ctx=v_c_ctx, ada_w=v_ada_w, ada_b=v_ada_b, norm1_g=v_norm1_g, w_in=v_w_in, s5_lam_re=v_s5_lam_re, s5_lam_im=v_s5_lam_im, s5_log_dt=v_s5_log_dt, s5_b_re=v_s5_b_re, s5_b_im=v_s5_b_im, s5_c_re=v_s5_c_re, s5_c_im=v_s5_c_im, s5_d=v_s5_d, s5_w_glu=v_s5_w_glu, conv_w=v_conv_w, conv_b=v_conv_b, conv_ln_g=v_conv_ln_g, conv_ln_b=v_conv_ln_b, w_out=v_w_out, norm2_g=v_norm2_g, mlp_w1=v_mlp_w1, mlp_w2=v_mlp_w2, final_g=v_final_g)
    order = list(weights)

    me = 4 * lax.axis_index("x") + 2 * lax.axis_index("y") + lax.axis_index("c")
    xs, cs, tgt = x[0], ctx[0], loss_target[0]
    n_lat_rows, n_ctx_rows = xs.shape[0], cs.shape[0]
    n_rows = n_lat_rows + n_ctx_rows
    n_lat = n_lat_rows // ROW_BLOCK
    ada_cols = ada_w.shape[2]

    later = [s5_w_glu[0].astype(BF16), conv_w[0], w_out[0].astype(BF16), mlp_w1[0].astype(BF16), mlp_w2[0].astype(BF16)]
    later_send, later_recv, later_src, later_land, later_token = _exchange_start("gather_later_start", later, [True] * 5)
    w_in_g, c_all = _exchange("gather_w_in", [w_in[0].astype(BF16), c + later_token[0:1, 0:1]], [True] * 2)
    w_in_full = jnp.transpose(w_in_g, (1, 0, 2)).reshape(D_MODEL, IN_COLS)
    c_all = c_all.reshape(NDEV, D_MODEL)

    cond_fwd = jnp.concatenate([c_all, c_ctx[None], jnp.zeros((7, D_MODEL), F32)])
    ada_b_loc = lax.dynamic_slice(ada_b, (0, me * ada_cols), (1, ada_cols))
    (mod_g,) = _exchange("gather_mod", [_ada_fwd(cond_fwd, ada_w[0], ada_b_loc)], [True])
    mod_rows = jnp.transpose(mod_g, (1, 0, 2)).reshape(16, 6 * D_MODEL)
    mod = lax.dynamic_slice(mod_rows, (me, 0), (1, 6 * D_MODEL)).reshape(6, D_MODEL)
    modc = mod_rows[8, :2 * D_MODEL].reshape(2, D_MODEL)
    sh1, sc1, g1, sh2, sc2, g2 = [mod[i:i + 1] for i in range(6)]

    a_all = _prenorm("prenorm1", xs, cs, norm1_g, jnp.stack([mod[0:2], modc]))
    (z_all,) = _matmul("in_proj", a_all, w_in_full, "nn", (n_rows, IN_COLS, D_MODEL), (ROW_BLOCK, IN_COLS, D_MODEL),
                       [((n_rows, IN_COLS), F32)])

    lam_re, lam_im = s5_lam_re[0].reshape(2, 1, NSTATE), s5_lam_im[0].reshape(2, 1, NSTATE)
    ldt = jnp.repeat(s5_log_dt[0], S5_STATE, axis=-1).reshape(2, 1, NSTATE)
    bt_re = jnp.transpose(s5_b_re[0], (0, 3, 1, 2)).reshape(2, S5_GROUP, NSTATE)
    bt_im = jnp.transpose(s5_b_im[0], (0, 3, 1, 2)).reshape(2, S5_GROUP, NSTATE)
    groups_per_block = S5_GROUPS // S5_BLOCKS
    ct_re = jnp.tile(s5_c_re[0].reshape(2, S5_WIDTH, S5_STATE), (1, 1, groups_per_block))
    ct_im = jnp.tile(s5_c_im[0].reshape(2, S5_WIDTH, S5_STATE), (1, 1, groups_per_block))
    d_skip = s5_d[0].reshape(1, S5_WIDTH)
    perm = _segment_permutation()
    perm_t = perm.T
    disc, states, y_dir = [], [], []
    for d in range(2):
        disc.append(_s5_discretise(f"s5_disc{d}", d == 0, lam_re[d], lam_im[d], ldt[d], bt_re[d], bt_im[d], ct_re[d], ct_im[d]))
        _, tab, _, bmat, cmat = disc[d]
        s, yd = _s5_scan_fwd(f"s5_scan_fwd{d}", d == 0, z_all, bmat, cmat, tab, perm, perm_t)
        states.append(s)
        y_dir.append(yd)
    later_own, later_landed = _exchange_wait("gather_later_wait", later_send, later_recv, later_src, later_land,
                                             [True] * 5, y_dir[1])
    glu_g, conv_w_g, w_out_g, w1_g, w2_g = [_with_own(l, o, me) for l, o in zip(later_landed, later_own)]
    glu_full = glu_g.reshape(S5_WIDTH, S5_WIDTH)
    conv_w_full = jnp.transpose(conv_w_g, (1, 0, 2)).reshape(CONV_K, CONV_WIDTH)
    w_out_full = w_out_g.reshape(D_MODEL, D_MODEL)
    w2_full = w2_g.reshape(D_FF, D_MODEL)
    y_s5 = _glu_fwd(z_all, y_dir[0], y_dir[1], d_skip, glu_full, n_lat_rows)

    hh_pad = _conv_gate(z_all, n_lat_rows)
    hc, y_conv = _conv_fwd(hh_pad, conv_w_full, conv_b, conv_ln_g, conv_ln_b, n_lat_rows)

    ycat = jnp.concatenate([y_s5, y_conv], axis=1)
    tm = min(1024, n_lat_rows)
    w1_cols = D_FF // NDEV
    row_vec = lambda tn: pl.BlockSpec((1, tn), lambda i, j, k: (0, j))
    out_tile = lambda t_m, t_n: pl.BlockSpec((t_m, t_n), lambda i, j, k: (i, j))
    gated = lambda acc, res, gate: (acc, res + gate * acc)
    mix, h1 = _matmul("out_proj", ycat, w_out_full, "nn", (n_lat_rows, D_MODEL, D_MODEL), (tm, D_MODEL, D_MODEL),
                      [((n_lat_rows, D_MODEL), F32)] * 2, epi=gated,
                      epi_extra=[(xs, out_tile(tm, D_MODEL)), (g1, row_vec(D_MODEL))])
    a2 = _prenorm("prenorm2", h1, None, norm2_g, mod[3:5][None])
    tm_up = min(2048, n_lat_rows)
    (f,) = _matmul("mlp_up", a2, w1_g, "nn", (n_lat_rows, D_FF, D_MODEL), (tm_up, w1_cols, D_MODEL),
                   [((n_lat_rows, D_FF), BF16)], b_spec=pl.BlockSpec((None, D_MODEL, w1_cols), lambda i, j, k: (j, 0, 0)))
    sq_relu = lambda t: jnp.square(jnp.maximum(t, 0.0))
    mlp_out, h2 = _matmul("mlp_down", f, w2_full, "nn", (n_lat_rows, D_MODEL, D_FF), (tm, D_MODEL, 1024),
                          [((n_lat_rows, D_MODEL), F32)] * 2, a_fn=sq_relu, epi=gated,
                          epi_extra=[(h1, out_tile(tm, D_MODEL)), (g2, row_vec(D_MODEL))])

    d_h2, dm2, err_sums, d_final_g8 = _loss_head(h2, tgt, final_g[None], g2)
    loss = lax.psum(0.5 / D_MODEL * jnp.sum(err_sums), ("x", "y", "c"))

    (d_f,) = _matmul("mlp_down_dx", dm2, w2_full, "nt", (n_lat_rows, D_FF, D_MODEL), (tm, 512, D_MODEL),
                     [((n_lat_rows, D_FF), BF16)],
                     epi=lambda acc, ft: (acc * 2.0 * jnp.maximum(ft.astype(F32), 0.0),), epi_extra=[(f, out_tile(tm, 512))])
    (g_w2,) = _matmul("mlp_down_dw", f, dm2, "tn", (D_FF, D_MODEL, n_lat_rows), (1024, D_MODEL, tm),
                      [((D_FF, D_MODEL), F32)], a_fn=sq_relu)
    (d_a2,) = _matmul("mlp_up_dx", d_f, w1_g, "nt", (n_lat_rows, D_MODEL, D_FF), (tm, D_MODEL, w1_cols),
                      [((n_lat_rows, D_MODEL), F32)],
                      b_spec=pl.BlockSpec((None, D_MODEL, w1_cols), lambda i, j, k: (k, 0, 0)))
    (g_w1,) = _matmul("mlp_up_dw", a2, d_f, "tn", (D_MODEL, D_FF, n_lat_rows), (D_MODEL, w1_cols, tm),
                      [((NDEV, D_MODEL, w1_cols), F32)],
                      out_specs=[pl.BlockSpec((None, D_MODEL, w1_cols), lambda i, j, k: (j, 0, 0))])
    mlp_send, mlp_recv, mlp_src, mlp_land, mlp_token = _exchange_start(
        "scatter_mlp_start", [g_w1, g_w2.reshape(NDEV, D_FF // NDEV, D_MODEL)], [False] * 2)
    d_h1, dm1, sums2 = _norm_bwd("norm2_bwd", h1, d_a2, 0, norm2_g, sc2 + mlp_token[0:1, 0:1], res=d_h2, aux=mlp_out,
                                 gate=g1)

    (d_ycat,) = _matmul("out_proj_dx", dm1, w_out_full, "nt", (n_lat_rows, D_MODEL, D_MODEL), (tm, D_MODEL, D_MODEL),
                        [((n_lat_rows, D_MODEL), F32)])
    (g_w_out,) = _matmul("out_proj_dw", ycat, dm1, "tn", (D_MODEL, D_MODEL, n_lat_rows), (D_MODEL, D_MODEL, 512),
                         [((D_MODEL, D_MODEL), F32)])

    dy, g_glu, dd8 = _glu_bwd(d_ycat, z_all, y_dir[0], y_dir[1], d_skip, glu_full, n_lat_rows)
    proj_send, proj_recv, proj_src, proj_land, proj_token = _exchange_start(
        "scatter_proj_start",
        [g_w_out.reshape(NDEV, D_MODEL // NDEV, D_MODEL), g_glu.reshape(NDEV, S5_WIDTH // NDEV, S5_WIDTH)], [False] * 2)
    perm = perm + proj_token[0:1, 0:1].astype(BF16)
    du, g_lam_re, g_lam_im, g_ldt, g_b_re, g_b_im, g_c_re, g_c_im = [], [], [], [], [], [], [], []

    def diag(mat):
        return jnp.diagonal(mat.reshape(S5_BLOCKS, groups_per_block, S5_GROUP, 2, groups_per_block, S5_STATE), axis1=1, axis2=4)

    for d in range(2):
        _, _, adj, bmat, cmat = disc[d]
        du_d, d_bmat, d_cmat, d_abar8 = _s5_scan_bwd(f"s5_scan_bwd{d}", d == 0, dy, z_all, states[d], bmat, cmat, adj,
                                                     perm, perm_t)
        du.append(du_d)
        d_bbar = jnp.transpose(diag(d_bmat), (2, 1, 0, 4, 3)).reshape(2 * S5_GROUP, NSTATE)
        d_c = jnp.transpose(diag(d_cmat), (2, 0, 4, 1, 3)).reshape(2, S5_GROUPS, S5_GROUP, S5_STATE)
        d_lam8, d_bt = _s5_discretise_bwd(f"s5_disc_bwd{d}", lam_re[d], lam_im[d], ldt[d], bt_re[d], bt_im[d], d_abar8, d_bbar)
        g_lam_re.append(d_lam8[0].reshape(S5_GROUPS, S5_STATE))
        g_lam_im.append(d_lam8[1].reshape(S5_GROUPS, S5_STATE))
        g_ldt.append(d_lam8[2].reshape(S5_GROUPS, S5_STATE).sum(axis=-1))
        to_gph = lambda t: jnp.transpose(t.reshape(S5_GROUP, S5_GROUPS, S5_STATE), (1, 2, 0))
        g_b_re.append(to_gph(d_bt[:S5_GROUP]))
        g_b_im.append(to_gph(d_bt[S5_GROUP:]))
        g_c_re.append(d_c[0])
        g_c_im.append(-d_c[1])

    dhc_pad, conv_sums = _conv_bwd_norm(d_ycat, hc, conv_ln_g, conv_ln_b, n_lat_rows)
    d_v, d_gate, g_conv_w8 = _conv_bwd_taps(dhc_pad, hh_pad, z_all, conv_w_full, n_lat_rows)

    dz_all = _dz_assemble(du[0], du[1], dy, d_skip, d_v, d_gate, n_lat)
    (d_a_all,) = _matmul("in_proj_dx", dz_all, w_in_full, "nt", (n_rows, D_MODEL, IN_COLS), (ROW_BLOCK, D_MODEL, IN_COLS),
                         [((n_rows, D_MODEL), F32)])
    (g_w_in_full,) = _matmul("in_proj_dw", a_all, dz_all, "tn", (D_MODEL, IN_COLS, n_rows), (D_MODEL, IN_COLS, ROW_BLOCK),
                             [((D_MODEL, IN_COLS), F32)])
    grad_x, sums1 = _norm_bwd("norm1_bwd", xs, d_a_all, 0, norm1_g, sc1, res=d_h1, aux=mix)
    (sums1c,) = _norm_bwd("norm1_bwd_ctx", cs, d_a_all, n_lat, norm1_g, modc[1:2])

    s1, s1c, s2 = sums1.sum(axis=1), sums1c.sum(axis=1), sums2.sum(axis=1)
    d_mod = jnp.concatenate([s1[0], s1[1], s1[3], s2[0], s2[1], s2[3]])
    d_modc = jnp.concatenate([s1c[0], s1c[1], jnp.zeros((4 * D_MODEL,), F32)])
    (dmod_g,) = _exchange("gather_dmod", [jnp.stack([d_mod, d_modc])], [True])
    dmod16 = jnp.concatenate([dmod_g[:, 0], dmod_g[:, 1]])
    dmod16_loc = lax.dynamic_slice(dmod16, (0, me * ada_cols), (16, ada_cols))
    cond_bwd = jnp.concatenate([c_all, jnp.broadcast_to(c_ctx[None], (NDEV, D_MODEL))])
    g_ada_w, g_c_ctx8 = _ada_bwd(cond_bwd, dmod16_loc, ada_w[0], c_ctx[None])

    small_parts = dict(
        c_ctx=g_c_ctx8[0], ada_b=d_mod + d_modc, norm1_g=s1[2] + s1c[2],
        s5_lam_re=jnp.stack(g_lam_re), s5_lam_im=jnp.stack(g_lam_im), s5_log_dt=jnp.stack(g_ldt),
        s5_b_re=jnp.stack(g_b_re), s5_b_im=jnp.stack(g_b_im), s5_c_re=jnp.stack(g_c_re), s5_c_im=jnp.stack(g_c_im),
        s5_d=dd8.sum(axis=0), conv_b=conv_sums[0].sum(axis=0), conv_ln_g=conv_sums[1].sum(axis=0),
        conv_ln_b=conv_sums[2].sum(axis=0), norm2_g=s2[2], final_g=d_final_g8.sum(axis=0))
    small_g = _pack_small([small_parts[n] for n in SMALL]).reshape(NDEV, SMALL_ROWS // NDEV, D_MODEL)
    g_w_in_parts = jnp.transpose(g_w_in_full.reshape(D_MODEL, NDEV, IN_COLS // NDEV), (1, 0, 2))
    g_conv_w_parts = jnp.transpose(g_conv_w8.sum(axis=1).reshape(CONV_K, NDEV, CONV_WIDTH // NDEV), (1, 0, 2))
    p_w_in, p_conv_w, p_small = _exchange("scatter_grads", [g_w_in_parts, g_conv_w_parts, small_g], [False] * 3)
    (small_all,) = _exchange("gather_small", [_sum_parts(p_small)], [True])
    small_all = small_all.reshape(1, SMALL_ROWS, D_MODEL)

    def own_chunk(src):
        return lax.dynamic_index_in_dim(src, me, 0, keepdims=False)

    mlp_src, mlp_landed = _exchange_wait("scatter_mlp_wait", mlp_send, mlp_recv, mlp_src, mlp_land, [False] * 2, small_all)
    p_w1, p_w2 = [_with_own(l, own_chunk(s), me) for l, s in zip(mlp_landed, mlp_src)]
    proj_src, proj_landed = _exchange_wait("scatter_proj_wait", proj_send, proj_recv, proj_src, proj_land, [False] * 2,
                                           small_all)
    p_w_out, p_glu = [_with_own(l, own_chunk(s), me) for l, s in zip(proj_landed, proj_src)]

    res = {}
    big = dict(ada_w=g_ada_w[None], w_in=p_w_in, s5_w_glu=p_glu, conv_w=p_conv_w, w_out=p_w_out, mlp_w1=p_w1, mlp_w2=p_w2)
    for name, parts in big.items():
        outs = _adamw("adamw_" + name, weights[name][0], parts, mom1[name][0], mom2[name][0])
        res[name] = [o[None] for o in outs]
    small_like = [weights[n] for n in SMALL]
    outs = _adamw("adamw_small", _pack_small(small_like), small_all, _pack_small([mom1[n] for n in SMALL]),
                  _pack_small([mom2[n] for n in SMALL]))
    unpacked = [_unpack_small(o, small_like) for o in outs]
    for i, name in enumerate(SMALL):
        res[name] = [u[i] for u in unpacked]

    return (loss, grad_x[None], *[res[n][0] for n in order], *[res[n][1] for n in order],
            *[res[n][2] for n in order], *[res[n][3] for n in order])
```

```python
import functools

import jax
import jax.numpy as jnp
from jax import lax
from jax.experimental import pallas as pl
from jax.experimental.pallas import tpu as pltpu

F32 = jnp.float32
BF16 = jnp.bfloat16
MESH = pl.DeviceIdType.MESH
ANY = pl.BlockSpec(memory_space=pl.ANY)

NDEV = 8
D_MODEL = 1024
GRID_W = 64
S5_WIDTH = 512
S5_GROUP = 16
S5_GROUPS = 32
S5_STATE = 64
NSTATE = S5_GROUPS * S5_STATE
CONV_WIDTH = 512
CONV_K = 31
IN_COLS = S5_WIDTH + 2 * CONV_WIDTH
D_FF = 4 * D_MODEL
EPS_RMS = 1e-6
EPS_LN = 1e-5
ADAM_LR = 0.001
ADAM_B1 = 0.9
ADAM_B2 = 0.999
ADAM_EPS = 1e-08
ADAM_WD = 0.01
ADAM_STEP = 10

SUBLANES = 8
LANES = 128
ROW_BLOCK = 256
SCAN_LANES = 512
SEGMENTS = SUBLANES
STEPS = ROW_BLOCK // SEGMENTS
S5_BLOCKS = 4
S5_BLOCK_WIDTH = S5_WIDTH // S5_BLOCKS
CONV_ROWS = 64
VMEM_LIMIT = 48 * 1024 * 1024
SMALL_ROWS = 320


def _params(sem=None):
    kw = dict(vmem_limit_bytes=VMEM_LIMIT)
    if sem is not None:
        kw["dimension_semantics"] = sem
    return pltpu.CompilerParams(**kw)


def _sds(shape, dtype=F32):
    return jax.ShapeDtypeStruct(tuple(shape), dtype)


def _fold8(x):
    return x.reshape(x.shape[0] // SUBLANES, SUBLANES, x.shape[1]).sum(axis=0)


def _sigmoid(x):
    return 1.0 / (1.0 + jnp.exp(-x))


def _silu(x):
    return x * _sigmoid(x)


def _dsilu(x):
    s = _sigmoid(x)
    return s * (1.0 + x * (1.0 - s))


_GELU_C = 0.7978845608028654


def _gelu(x):
    return 0.5 * x * (1.0 + jnp.tanh(_GELU_C * (x + 0.044715 * x * x * x)))


def _dgelu(x):
    t = jnp.tanh(_GELU_C * (x + 0.044715 * x * x * x))
    return 0.5 * (1.0 + t) + 0.5 * x * (1.0 - t * t) * _GELU_C * (1.0 + 3.0 * 0.044715 * x * x)


def _dot(a, b, mode):
    dims = {"nn": (((1,), (0,)), ((), ())), "nt": (((1,), (1,)), ((), ())), "tn": (((0,), (0,)), ((), ()))}[mode]
    return lax.dot_general(a, b, dims, preferred_element_type=F32)


def _peers(x, y, c):
    out = []
    for k in range(1, NDEV):
        px = 1 - x if k & 4 else x
        py = 1 - y if k & 2 else y
        pc = 1 - c if k & 1 else c
        out.append(((px, py, pc), 4 * px + 2 * py + pc))
    return out


def _exchange_copies(src, land, send_sems, recv_sems, gather):
    x, y, c = lax.axis_index("x"), lax.axis_index("y"), lax.axis_index("c")
    me = 4 * x + 2 * y + c
    out = []
    for a in range(len(src)):
        for k, (peer, plin) in enumerate(_peers(x, y, c)):
            chunk = src[a] if gather[a] else src[a].at[plin]
            sems = dict(send_sem=send_sems.at[a * (NDEV - 1) + k], recv_sem=recv_sems.at[a * (NDEV - 1) + k],
                        device_id=peer, device_id_type=MESH)
            out.append((pltpu.make_async_remote_copy(src_ref=chunk, dst_ref=land[a].at[me], **sems),
                        pltpu.make_async_remote_copy(src_ref=chunk, dst_ref=land[a].at[plin], **sems)))
    return out


def _exchange(name, srcs, gather):
    n = len(srcs)
    outs = [_sds(((NDEV,) + s.shape) if g else s.shape, s.dtype) for s, g in zip(srcs, gather)]

    def body(*refs):
        src, dst, token = refs[:n], refs[n:2 * n], refs[2 * n]
        send_sems, recv_sems, local_sems = refs[2 * n + 1:]
        me = 4 * lax.axis_index("x") + 2 * lax.axis_index("y") + lax.axis_index("c")
        local = [pltpu.make_async_copy(src[a] if gather[a] else src[a].at[me], dst[a].at[me], local_sems.at[a])
                 for a in range(n)]
        for copy in local:
            copy.start()
        copies = _exchange_copies(src, dst, send_sems, recv_sems, gather)
        for copy, _ in copies:
            copy.start()
        token[...] = jnp.zeros_like(token)
        for copy, landing in copies:
            copy.wait_send()
            landing.wait_recv()
        for copy in local:
            copy.wait()

    nsem = n * (NDEV - 1)
    out = pl.pallas_call(
        body, name=name, out_shape=outs + [_sds((SUBLANES, LANES))], in_specs=[ANY] * n,
        out_specs=[ANY] * n + [pl.BlockSpec(memory_space=pltpu.VMEM)],
        scratch_shapes=[pltpu.SemaphoreType.DMA((nsem,)), pltpu.SemaphoreType.DMA((nsem,)), pltpu.SemaphoreType.DMA((n,))],
    )(*srcs)
    return out[:n], out[n]


HBM = pl.BlockSpec(memory_space=pltpu.HBM)
SEM = pl.BlockSpec(memory_space=pltpu.SEMAPHORE)
EFFECT = pltpu.SideEffectType.DATAFLOW_SIDE_EFFECTING


def _exchange_start(name, srcs, gather):
    n = len(srcs)
    lands = [lax.empty(((NDEV,) + s.shape) if g else s.shape, s.dtype) for s, g in zip(srcs, gather)]

    def body(*refs):
        src, land = refs[:n], refs[n:2 * n]
        send_sems, recv_sems = refs[2 * n], refs[2 * n + 1]
        token = refs[-1]
        for copy, _ in _exchange_copies(src, land, send_sems, recv_sems, gather):
            copy.start()
        token[...] = jnp.zeros_like(token)

    hbm = lambda v: pltpu.HBM(v.shape, v.dtype)
    nsem = n * (NDEV - 1)
    out = pl.pallas_call(
        body, name=name,
        out_shape=(pltpu.SemaphoreType.DMA((nsem,)), pltpu.SemaphoreType.DMA((nsem,)), *[hbm(v) for v in srcs],
                   *[hbm(v) for v in lands], _sds((SUBLANES, LANES))),
        in_specs=[HBM] * (2 * n), out_specs=(SEM, SEM, *([HBM] * (2 * n)), pl.BlockSpec(memory_space=pltpu.VMEM)),
        input_output_aliases={i: 2 + i for i in range(2 * n)},
        compiler_params=pltpu.CompilerParams(has_side_effects=EFFECT),
    )(*[pltpu.with_memory_space_constraint(v, pltpu.HBM) for v in list(srcs) + lands])
    return out[0], out[1], out[2:2 + n], out[2 + n:2 + 2 * n], out[-1]


def _exchange_wait(name, send_sems, recv_sems, srcs, lands, gather, after):
    n = len(srcs)

    def body(*refs):
        src, land = refs[:n], refs[n:2 * n]
        send_ref, recv_ref = refs[2 * n], refs[2 * n + 1]
        for copy, landing in _exchange_copies(src, land, send_ref, recv_ref, gather):
            copy.wait_send()
            landing.wait_recv()

    hbm = lambda v: pltpu.HBM(v.shape, v.dtype)
    out = pl.pallas_call(
        body, name=name, out_shape=[hbm(v) for v in list(srcs) + list(lands)],
        in_specs=[HBM] * (2 * n) + [SEM, SEM, ANY], out_specs=[HBM] * (2 * n),
        input_output_aliases={i: i for i in range(2 * n)},
        compiler_params=pltpu.CompilerParams(has_side_effects=EFFECT),
    )(*srcs, *lands, send_sems, recv_sems, after)
    return out[:n], out[n:]


def _with_own(landed, own, me):
    return lax.dynamic_update_slice(landed, own[None], (me,) + (0,) * own.ndim)


def _matmul(name, a, b, mode, mnk, tiles, outs, a_spec=None, b_spec=None, a_fn=None, a_extra=(),
            epi=None, epi_extra=(), out_specs=None):
    m_, n_, k_ = mnk
    tm, tn, tk = tiles
    nk = k_ // tk
    if a_spec is None:
        a_spec = (pl.BlockSpec((tk, tm), lambda i, j, k: (k, i)) if mode == "tn"
                  else pl.BlockSpec((tm, tk), lambda i, j, k: (i, k)))
    if b_spec is None:
        b_spec = (pl.BlockSpec((tn, tk), lambda i, j, k: (j, k)) if mode == "nt"
                  else pl.BlockSpec((tk, tn), lambda i, j, k: (k, j)))
    if out_specs is None:
        out_specs = [pl.BlockSpec((tm, tn), lambda i, j, k: (i, j)) for _ in outs]
    na, ne, no = len(a_extra), len(epi_extra), len(outs)

    def body(*refs):
        a_ref, b_ref = refs[0], refs[1]
        ax = refs[2:2 + na]
        ex = refs[2 + na:2 + na + ne]
        o = refs[2 + na + ne:2 + na + ne + no]

        def finish(res):
            res = epi(res, *[r[...] for r in ex]) if epi is not None else (res,)
            for ref, val in zip(o, res):
                ref[...] = val.astype(ref.dtype)

        at = a_ref[...]
        if a_fn is not None:
            at = a_fn(at, *[r[...] for r in ax])
        part = _dot(at.astype(BF16), b_ref[...].astype(BF16), mode)
        if nk == 1:
            finish(part)
            return
        acc = refs[-1]
        k = pl.program_id(2)

        @pl.when(k == 0)
        def _():
            acc[...] = part

        @pl.when(k > 0)
        def _():
            acc[...] += part

        @pl.when(k == nk - 1)
        def _():
            finish(acc[...])

    return pl.pallas_call(
        body, name=name, grid=(m_ // tm, n_ // tn, nk),
        in_specs=[a_spec, b_spec] + [s for _, s in a_extra] + [s for _, s in epi_extra],
        out_specs=out_specs, out_shape=[_sds(s, d) for s, d in outs],
        scratch_shapes=[pltpu.VMEM((tm, tn), F32)] if nk > 1 else [],
        compiler_params=_params(("parallel", "parallel", "arbitrary")),
    )(a, b, *[x for x, _ in a_extra], *[x for x, _ in epi_extra])


def _prenorm(name, x, ctx, gain, shsc):
    n_lat = x.shape[0] // ROW_BLOCK
    n_ctx = 0 if ctx is None else ctx.shape[0] // ROW_BLOCK
    d = x.shape[1]

    def norm(src, g_ref, m_ref, o_ref):
        xv = src[...]
        xh = xv * lax.rsqrt(jnp.mean(xv * xv, axis=-1, keepdims=True) + EPS_RMS)
        o_ref[...] = ((xh * g_ref[...]) * (1.0 + m_ref[1:2, :]) + m_ref[0:1, :]).astype(o_ref.dtype)

    def body(*refs):
        if ctx is None:
            x_ref, g_ref, m_ref, o_ref = refs
            norm(x_ref, g_ref, m_ref, o_ref)
        else:
            x_ref, c_ref, g_ref, m_ref, o_ref = refs
            i = pl.program_id(0)

            @pl.when(i < n_lat)
            def _():
                norm(x_ref, g_ref, m_ref, o_ref)

            @pl.when(i >= n_lat)
            def _():
                norm(c_ref, g_ref, m_ref, o_ref)

    in_specs = [pl.BlockSpec((ROW_BLOCK, d), lambda i: (jnp.minimum(i, n_lat - 1), 0))]
    args = [x]
    if ctx is not None:
        in_specs.append(pl.BlockSpec((ROW_BLOCK, d), lambda i: (jnp.maximum(i - n_lat, 0), 0)))
        args.append(ctx)
    in_specs += [pl.BlockSpec((1, d), lambda i: (0, 0)),
                 pl.BlockSpec((None, 2, d), lambda i: (jnp.minimum(i // n_lat, 1), 0, 0))]
    args += [gain, shsc]
    return pl.pallas_call(
        body, name=name, grid=(n_lat + n_ctx,), in_specs=in_specs,
        out_specs=pl.BlockSpec((ROW_BLOCK, d), lambda i: (i, 0)),
        out_shape=_sds(((n_lat + n_ctx) * ROW_BLOCK, d), BF16),
        compiler_params=_params(("parallel",)),
    )(*args)


def _norm_bwd(name, x, d_act, d_act_row0, gain, scale, res=None, aux=None, gate=None):
    rows, d = x.shape
    nb = rows // ROW_BLOCK
    has_res = res is not None
    has_gate = gate is not None

    def body(*refs):
        if has_gate:
            x_ref, da_ref, g_ref, sc_ref, r_ref, aux_ref, gate_ref, dx_ref, dm_ref, sums = refs
        elif has_res:
            x_ref, da_ref, g_ref, sc_ref, r_ref, aux_ref, dx_ref, sums = refs
        else:
            x_ref, da_ref, g_ref, sc_ref, sums = refs
        i = pl.program_id(0)

        @pl.when(i == 0)
        def _():
            sums[...] = jnp.zeros_like(sums)

        xv, da = x_ref[...], da_ref[...]
        rstd = lax.rsqrt(jnp.mean(xv * xv, axis=-1, keepdims=True) + EPS_RMS)
        xh = xv * rstd
        g = g_ref[...]
        dn = da * (1.0 + sc_ref[...])
        sums[0] += _fold8(da)
        sums[1] += _fold8(da * (xh * g))
        sums[2] += _fold8(dn * xh)
        if has_res:
            dxh = dn * g
            dx = rstd * (dxh - xh * jnp.mean(dxh * xh, axis=-1, keepdims=True))
            rv = r_ref[...]
            dx_ref[...] = rv + dx
            sums[3] += _fold8(rv * aux_ref[...])
            if has_gate:
                dm_ref[...] = ((rv + dx) * gate_ref[...]).astype(dm_ref.dtype)

    row = lambda i: (i, 0)
    vec = pl.BlockSpec((1, d), lambda i: (0, 0))
    in_specs = [pl.BlockSpec((ROW_BLOCK, d), row), pl.BlockSpec((ROW_BLOCK, d), lambda i: (i + d_act_row0, 0)), vec, vec]
    args = [x, d_act, gain, scale]
    out_shape = [_sds((4, SUBLANES, d))]
    out_specs = [pl.BlockSpec((4, SUBLANES, d), lambda i: (0, 0, 0))]
    if has_res:
        in_specs += [pl.BlockSpec((ROW_BLOCK, d), row), pl.BlockSpec((ROW_BLOCK, d), row)]
        args += [res, aux]
        if has_gate:
            in_specs.append(vec)
            args.append(gate)
            out_shape = [_sds((rows, d), BF16)] + out_shape
            out_specs = [pl.BlockSpec((ROW_BLOCK, d), row)] + out_specs
        out_shape = [_sds((rows, d))] + out_shape
        out_specs = [pl.BlockSpec((ROW_BLOCK, d), row)] + out_specs
    return pl.pallas_call(
        body, name=name, grid=(nb,), in_specs=in_specs, out_specs=out_specs, out_shape=out_shape,
        compiler_params=_params(("arbitrary",)),
    )(*args)


def _loss_head(h2, target, gain, gate):
    rows, d = h2.shape

    def body(h_ref, t_ref, g_ref, gate_ref, dh_ref, dm_ref, err_ref, dg_ref):
        i = pl.program_id(0)

        @pl.when(i == 0)
        def _():
            err_ref[...] = jnp.zeros_like(err_ref)
            dg_ref[...] = jnp.zeros_like(dg_ref)

        hv = h_ref[...]
        rstd = lax.rsqrt(jnp.mean(hv * hv, axis=-1, keepdims=True) + EPS_RMS)
        xh = hv * rstd
        g = g_ref[...]
        err = xh * g - t_ref[...]
        err_ref[...] += _fold8(err * err)
        dy = err * (1.0 / d)
        dg_ref[...] += _fold8(dy * xh)
        dxh = dy * g
        dh = rstd * (dxh - xh * jnp.mean(dxh * xh, axis=-1, keepdims=True))
        dh_ref[...] = dh
        dm_ref[...] = (dh * gate_ref[...]).astype(dm_ref.dtype)

    row = pl.BlockSpec((ROW_BLOCK, d), lambda i: (i, 0))
    acc = pl.BlockSpec((SUBLANES, d), lambda i: (0, 0))
    vec = pl.BlockSpec((1, d), lambda i: (0, 0))
    return pl.pallas_call(
        body, name="loss_head", grid=(rows // ROW_BLOCK,),
        in_specs=[row, row, vec, vec], out_specs=[row, row, acc, acc],
        out_shape=[_sds((rows, d)), _sds((rows, d), BF16), _sds((SUBLANES, d)), _sds((SUBLANES, d))],
        compiler_params=_params(("arbitrary",)),
    )(h2, target, gain, gate)


def _ada_fwd(cond16, ada_w_loc, ada_b_loc):
    cols = ada_w_loc.shape[1]

    def body(c_ref, w_ref, b_ref, o_ref):
        s = _silu(c_ref[...]).astype(BF16)
        o_ref[...] = _dot(s, w_ref[...].astype(BF16), "nn") + b_ref[...]

    return pl.pallas_call(body, name="ada_fwd", out_shape=_sds((16, cols)), compiler_params=_params())(
        cond16, ada_w_loc, ada_b_loc)


def _ada_bwd(cond16, dmod16, ada_w_loc, c_ctx_row):
    k_, cols = ada_w_loc.shape

    def body(c_ref, dm_ref, w_ref, cc_ref, gw_ref, gc_ref):
        s = _silu(c_ref[...]).astype(BF16)
        dm = dm_ref[...]
        gw_ref[...] = _dot(s, dm.astype(BF16), "tn")
        dmc = jnp.sum(dm[8:16, :], axis=0, keepdims=True)
        dmc8 = jnp.broadcast_to(dmc, (SUBLANES, cols)).astype(BF16)
        ds = _dot(dmc8, w_ref[...].astype(BF16), "nt")
        row = lax.broadcasted_iota(jnp.int32, ds.shape, 0)
        gc_ref[...] = jnp.where(row == 0, ds * _dsilu(cc_ref[...]), 0.0)

    return pl.pallas_call(body, name="ada_bwd", out_shape=[_sds((k_, cols)), _sds((SUBLANES, k_))],
                          compiler_params=_params())(cond16, dmod16, ada_w_loc, c_ctx_row)


def _cmul(a, b):
    return a[0] * b[0] - a[1] * b[1], a[0] * b[1] + a[1] * b[0]


def _disc(lam_re, lam_im, ldt):
    dt = jnp.exp(ldt)
    mag = jnp.exp(lam_re * dt)
    th = lam_im * dt
    a_re, a_im = mag * jnp.cos(th), mag * jnp.sin(th)
    den = lam_re * lam_re + lam_im * lam_im
    n_re = a_re - 1.0
    f_re = (n_re * lam_re + a_im * lam_im) / den
    f_im = (a_im * lam_re - n_re * lam_im) / den
    return dt, mag, th, a_re, a_im, den, n_re, f_re, f_im


def _block_diag_mask(shape):
    row = lax.broadcasted_iota(jnp.int32, shape, 0)
    col = lax.broadcasted_iota(jnp.int32, shape, 1)
    return lax.shift_right_logical(row, 4) == lax.shift_right_logical(col, 6)


TAB_A = 0
TAB_BIG = 1
TAB_SEG = 4
TAB_PW = 5
TAB_ROWS = TAB_PW + STEPS


def _s5_discretise(name, ascending, lam_re, lam_im, ldt, bt_re, bt_im, ct_re, ct_im):
    def write_tables(ref, pw, big, asc, sign):
        row = lax.broadcasted_iota(jnp.int32, (SUBLANES, NSTATE), 0)
        full = lambda v: jnp.broadcast_to(v, (SUBLANES, NSTATE))

        def put(t, p):
            ref[0, t] = full(p[0])
            ref[1, t] = full(sign * p[1])

        put(TAB_A, pw[0])
        for t in range(3):
            put(TAB_BIG + t, big[t])
        seg = [big[0]]
        for _ in range(SEGMENTS - 1):
            seg.append(_cmul(seg[-1], big[0]))
        seg_re = jnp.zeros((SUBLANES, NSTATE), F32)
        seg_im = jnp.zeros((SUBLANES, NSTATE), F32)
        for r in range(SEGMENTS):
            p = seg[r] if asc else seg[SEGMENTS - 1 - r]
            seg_re = jnp.where(row == r, p[0], seg_re)
            seg_im = jnp.where(row == r, sign * p[1], seg_im)
        ref[0, TAB_SEG] = seg_re
        ref[1, TAB_SEG] = seg_im
        for k in range(STEPS):
            put(TAB_PW + k, pw[k])

    def body(lr_ref, li_ref, ldt_ref, br_ref, bi_ref, cr_ref, ci_ref, bb_ref, tab_ref, adj_ref, bm_ref, cm_ref):
        _, _, _, a_re, a_im, _, _, f_re, f_im = _disc(lr_ref[...], li_ref[...], ldt_ref[...])
        bre, bim = br_ref[...], bi_ref[...]
        bb_re = f_re * bre - f_im * bim
        bb_im = f_re * bim + f_im * bre
        bb_ref[0:S5_GROUP, :] = bb_re
        bb_ref[S5_GROUP:2 * S5_GROUP, :] = bb_im
        pw = [(a_re, a_im)]
        for _ in range(STEPS - 1):
            pw.append(_cmul(pw[-1], (a_re, a_im)))
        big = [pw[STEPS - 1]]
        for _ in range(2):
            big.append(_cmul(big[-1], big[-1]))
        write_tables(tab_ref, pw, big, ascending, 1.0)
        write_tables(adj_ref, pw, big, not ascending, -1.0)
        half = NSTATE // S5_BLOCKS
        mask = _block_diag_mask((S5_BLOCK_WIDTH, half))
        tile = lambda v: jnp.broadcast_to(v[None], (S5_BLOCK_WIDTH // S5_GROUP, S5_GROUP, half)).reshape(S5_BLOCK_WIDTH, half)
        for c in range(S5_BLOCKS):
            cols = slice(c * half, (c + 1) * half)
            rows = slice(c * S5_BLOCK_WIDTH, (c + 1) * S5_BLOCK_WIDTH)
            bm_ref[c, :, 0:half] = jnp.where(mask, tile(bb_re[:, cols]), 0.0).astype(BF16)
            bm_ref[c, :, half:2 * half] = jnp.where(mask, tile(bb_im[:, cols]), 0.0).astype(BF16)
            cm_ref[c, :, 0:half] = jnp.where(mask, cr_ref[rows, :], 0.0).astype(BF16)
            cm_ref[c, :, half:2 * half] = jnp.where(mask, -ci_ref[rows, :], 0.0).astype(BF16)

    blocked = _sds((S5_BLOCKS, S5_BLOCK_WIDTH, 2 * NSTATE // S5_BLOCKS), BF16)
    return pl.pallas_call(
        body, name=name,
        out_shape=[_sds((2 * S5_GROUP, NSTATE)), _sds((2, TAB_ROWS, SUBLANES, NSTATE)),
                   _sds((2, TAB_ROWS, SUBLANES, NSTATE)), blocked, blocked],
        compiler_params=_params(),
    )(lam_re, lam_im, ldt, bt_re, bt_im, ct_re, ct_im)


def _s5_discretise_bwd(name, lam_re, lam_im, ldt, bt_re, bt_im, d_abar8, d_bbar):
    def body(lr_ref, li_ref, ldt_ref, br_ref, bi_ref, da_ref, db_ref, dl_ref, dbt_ref):
        lam_re, lam_im = lr_ref[...], li_ref[...]
        dt, mag, _, a_re, a_im, den, n_re, f_re, f_im = _disc(lam_re, lam_im, ldt_ref[...])
        bre, bim = br_ref[...], bi_ref[...]
        dbr, dbi = db_ref[0:S5_GROUP, :], db_ref[S5_GROUP:2 * S5_GROUP, :]
        dbt_ref[0:S5_GROUP, :] = f_re * dbr + f_im * dbi
        dbt_ref[S5_GROUP:2 * S5_GROUP, :] = f_re * dbi - f_im * dbr
        df_re = jnp.sum(bre * dbr + bim * dbi, axis=0, keepdims=True)
        df_im = jnp.sum(bre * dbi - bim * dbr, axis=0, keepdims=True)
        da = da_ref[...]
        da_re = jnp.sum(da[:, 0:NSTATE], axis=0, keepdims=True)
        da_im = jnp.sum(da[:, NSTATE:2 * NSTATE], axis=0, keepdims=True)
        da_re = da_re + (df_re * lam_re - df_im * lam_im) / den
        da_im = da_im + (df_re * lam_im + df_im * lam_re) / den
        ff = (f_re * df_re + f_im * df_im) * 2.0 / den
        d_lr = (df_re * n_re + df_im * a_im) / den - ff * lam_re
        d_li = (df_re * a_im - df_im * n_re) / den - ff * lam_im
        d_mag = (da_re * a_re + da_im * a_im) / mag
        d_th = da_im * a_re - da_re * a_im
        d_lr = d_lr + d_mag * mag * dt
        d_li = d_li + d_th * dt
        d_ldt = (d_mag * mag * lam_re + d_th * lam_im) * dt
        row = lax.broadcasted_iota(jnp.int32, (SUBLANES, NSTATE), 0)
        dl_ref[...] = jnp.where(row == 0, d_lr, jnp.where(row == 1, d_li, jnp.where(row == 2, d_ldt, 0.0)))

    return pl.pallas_call(
        body, name=name, out_shape=[_sds((SUBLANES, NSTATE)), _sds((2 * S5_GROUP, NSTATE))],
        compiler_params=_params(),
    )(lam_re, lam_im, ldt, bt_re, bt_im, d_abar8, d_bbar)


def _segment_permutation():
    rho = jnp.arange(ROW_BLOCK)
    src = STEPS * (rho % SEGMENTS) + rho // SEGMENTS
    return (src[:, None] == jnp.arange(ROW_BLOCK)[None, :]).astype(BF16)


def _permute_rows(perm_ref, v):
    return _dot(perm_ref[...], v, "nn").astype(BF16)


def _unpermute_rows(perm_t_ref, v):
    hi = v.astype(BF16)
    lo = (v - hi.astype(F32)).astype(BF16)
    return _dot(perm_t_ref[...], hi, "nn") + _dot(perm_t_ref[...], lo, "nn")


def _scan_chunk(x_ref, out_ref, tab_ref, carry_re, carry_im, ascending, pair_ref=None, acc_ref=None):
    w = SCAN_LANES
    half = NSTATE // S5_BLOCKS
    row = lax.broadcasted_iota(jnp.int32, (SUBLANES, w), 0)
    last = (SEGMENTS - 1) if ascending else 0

    def from_previous_segment(v, k, fill):
        if ascending:
            return jnp.where(row >= k, pltpu.roll(v, k, 0), fill)
        return jnp.where(row < SEGMENTS - k, pltpu.roll(v, SEGMENTS - k, 0), fill)

    def tile_rows(k):
        return pl.ds(pl.multiple_of((k if ascending else STEPS - 1 - k) * SUBLANES, SUBLANES), SUBLANES)

    for j in range(NSTATE // w):
        n_l = pl.ds(j * w, w)
        lane0 = (j * w // half) * 2 * half + (j * w) % half
        re_l, im_l = pl.ds(lane0, w), pl.ds(lane0 + half, w)
        tab = lambda t, n_l=n_l: (tab_ref[0, t, :, n_l], tab_ref[1, t, :, n_l])
        a_re, a_im = tab(TAB_A)

        def local_step(k, h):
            rs = tile_rows(k)
            h_re = a_re * h[0] - a_im * h[1] + x_ref[rs, re_l]
            h_im = a_re * h[1] + a_im * h[0] + x_ref[rs, im_l]
            out_ref[rs, re_l] = h_re
            out_ref[rs, im_l] = h_im
            return h_re, h_im

        zero = jnp.zeros((SUBLANES, w), F32)
        end_re, end_im = lax.fori_loop(0, STEPS, local_step, (zero, zero))
        for t, k in ((TAB_BIG, 1), (TAB_BIG + 1, 2), (TAB_BIG + 2, 4)):
            p_re, p_im = tab(t)
            s_re, s_im = from_previous_segment(end_re, k, 0.0), from_previous_segment(end_im, k, 0.0)
            end_re, end_im = end_re + (p_re * s_re - p_im * s_im), end_im + (p_re * s_im + p_im * s_re)
        c0_re, c0_im = carry_re[:, n_l], carry_im[:, n_l]
        p_re, p_im = tab(TAB_SEG)
        end_re = end_re + (p_re * c0_re - p_im * c0_im)
        end_im = end_im + (p_re * c0_im + p_im * c0_re)
        carry_re[:, n_l] = jnp.broadcast_to(end_re[last:last + 1, :], end_re.shape)
        carry_im[:, n_l] = jnp.broadcast_to(end_im[last:last + 1, :], end_im.shape)
        in_re = from_previous_segment(end_re, 1, c0_re)
        in_im = from_previous_segment(end_im, 1, c0_im)

        def carry_step(k, st):
            rs = tile_rows(k)
            p_re, p_im = tab_ref[0, TAB_PW + k, :, n_l], tab_ref[1, TAB_PW + k, :, n_l]
            o_re = out_ref[rs, re_l] + (p_re * in_re - p_im * in_im)
            o_im = out_ref[rs, im_l] + (p_re * in_im + p_im * in_re)
            out_ref[rs, re_l] = o_re
            out_ref[rs, im_l] = o_im
            if pair_ref is None:
                return st
            s_re, s_im = pair_ref[rs, re_l], pair_ref[rs, im_l]
            return (o_re, o_im, st[2] + (st[0] * s_re + st[1] * s_im), st[3] + (st[1] * s_re - st[0] * s_im))

        if pair_ref is None:
            lax.fori_loop(0, STEPS, carry_step, 0)
        else:
            fin = lax.fori_loop(0, STEPS, carry_step, (in_re, in_im, zero, zero))
            acc_ref[:, n_l] += fin[2]
            acc_ref[:, pl.ds(NSTATE + j * w, w)] += fin[3]


def _scan_block_index(i, n_lat, ctx_first_then_ascending):
    if ctx_first_then_ascending:
        return jnp.where(i == 0, n_lat, i - 1)
    return jnp.where(i == 0, n_lat, n_lat - i)


def _full_spec(shape):
    return pl.BlockSpec(shape, lambda i: (0,) * len(shape))


_S5_BLOCKED = (S5_BLOCKS, S5_BLOCK_WIDTH, 2 * NSTATE // S5_BLOCKS)
_S5_TABLES = (2, TAB_ROWS, SUBLANES, NSTATE)


def _s5_scan_fwd(name, ascending, z_all, bmat, cmat, tab, perm, perm_t):
    rows = z_all.shape[0]
    nb = rows // ROW_BLOCK
    n_lat = nb - 1
    bw, sw = S5_BLOCK_WIDTH, 2 * NSTATE // S5_BLOCKS

    def body(u_ref, bm_ref, cm_ref, tab_ref, p_ref, pt_ref, s_ref, y_ref, bu, yp, carry_re, carry_im):
        @pl.when(pl.program_id(0) == 0)
        def _():
            carry_re[...] = jnp.zeros_like(carry_re)
            carry_im[...] = jnp.zeros_like(carry_im)

        up = _permute_rows(p_ref, u_ref[...].astype(BF16))
        for c in range(S5_BLOCKS):
            bu[:, c * sw:(c + 1) * sw] = _dot(up[:, c * bw:(c + 1) * bw], bm_ref[c], "nn")
        _scan_chunk(bu, s_ref, tab_ref, carry_re, carry_im, ascending)
        for c in range(S5_BLOCKS):
            yp[:, c * bw:(c + 1) * bw] = _dot(s_ref[:, c * sw:(c + 1) * sw].astype(BF16), cm_ref[c], "nt")
        y_ref[...] = _unpermute_rows(pt_ref, yp[...])

    blk = lambda i: (_scan_block_index(i, n_lat, ascending), 0)
    return pl.pallas_call(
        body, name=name, grid=(nb,),
        in_specs=[pl.BlockSpec((ROW_BLOCK, S5_WIDTH), blk), _full_spec(_S5_BLOCKED), _full_spec(_S5_BLOCKED),
                  _full_spec(_S5_TABLES), _full_spec((ROW_BLOCK, ROW_BLOCK)), _full_spec((ROW_BLOCK, ROW_BLOCK))],
        out_specs=[pl.BlockSpec((ROW_BLOCK, 2 * NSTATE), blk), pl.BlockSpec((ROW_BLOCK, S5_WIDTH), blk)],
        out_shape=[_sds((rows, 2 * NSTATE)), _sds((rows, S5_WIDTH))],
        scratch_shapes=[pltpu.VMEM((ROW_BLOCK, 2 * NSTATE), F32), pltpu.VMEM((ROW_BLOCK, S5_WIDTH), F32),
                        pltpu.VMEM((SUBLANES, NSTATE), F32), pltpu.VMEM((SUBLANES, NSTATE), F32)],
        compiler_params=_params(("arbitrary",)),
    )(z_all, bmat, cmat, tab, perm, perm_t)


def _s5_scan_bwd(name, ascending, dy, z_all, states, bmat, cmat, adj, perm, perm_t):
    rows = states.shape[0]
    nb = rows // ROW_BLOCK
    n_lat = nb - 1
    bw, sw = S5_BLOCK_WIDTH, 2 * NSTATE // S5_BLOCKS

    def block_index(i):
        if ascending:
            return jnp.where(i == nb - 1, n_lat, n_lat - 1 - i)
        return jnp.where(i == nb - 1, n_lat, i)

    def body(dy_ref, u_ref, s_ref, bm_ref, cm_ref, adj_ref, p_ref, pt_ref, du_ref, db_ref, dc_ref, da_ref,
             g, dup, carry_re, carry_im):
        i = pl.program_id(0)

        @pl.when(i == 0)
        def _():
            carry_re[...] = jnp.zeros_like(carry_re)
            carry_im[...] = jnp.zeros_like(carry_im)
            da_ref[...] = jnp.zeros_like(da_ref)
            db_ref[...] = jnp.zeros_like(db_ref)
            dc_ref[...] = jnp.zeros_like(dc_ref)

        @pl.when(i < nb - 1)
        def _():
            dyp = _permute_rows(p_ref, dy_ref[...].astype(BF16))
            for c in range(S5_BLOCKS):
                g[:, c * sw:(c + 1) * sw] = _dot(dyp[:, c * bw:(c + 1) * bw], cm_ref[c], "nn")
                dc_ref[c] += _dot(dyp[:, c * bw:(c + 1) * bw], s_ref[:, c * sw:(c + 1) * sw].astype(BF16), "tn")

        @pl.when(i == nb - 1)
        def _():
            g[...] = jnp.zeros_like(g)

        _scan_chunk(g, g, adj_ref, carry_re, carry_im, not ascending, pair_ref=s_ref, acc_ref=da_ref)
        up = _permute_rows(p_ref, u_ref[...].astype(BF16))
        for c in range(S5_BLOCKS):
            gc = g[:, c * sw:(c + 1) * sw].astype(BF16)
            dup[:, c * bw:(c + 1) * bw] = _dot(gc, bm_ref[c], "nt")
            db_ref[c] += _dot(up[:, c * bw:(c + 1) * bw], gc, "tn")
        du_ref[...] = _unpermute_rows(pt_ref, dup[...])

    blk = lambda i: (block_index(i), 0)
    return pl.pallas_call(
        body, name=name, grid=(nb,),
        in_specs=[pl.BlockSpec((ROW_BLOCK, S5_WIDTH), lambda i: (jnp.minimum(block_index(i), n_lat - 1), 0)),
                  pl.BlockSpec((ROW_BLOCK, S5_WIDTH), blk), pl.BlockSpec((ROW_BLOCK, 2 * NSTATE), blk),
                  _full_spec(_S5_BLOCKED), _full_spec(_S5_BLOCKED), _full_spec(_S5_TABLES),
                  _full_spec((ROW_BLOCK, ROW_BLOCK)), _full_spec((ROW_BLOCK, ROW_BLOCK))],
        out_specs=[pl.BlockSpec((ROW_BLOCK, S5_WIDTH), blk), _full_spec(_S5_BLOCKED), _full_spec(_S5_BLOCKED),
                   _full_spec((SUBLANES, 2 * NSTATE))],
        out_shape=[_sds((rows, S5_WIDTH)), _sds(_S5_BLOCKED), _sds(_S5_BLOCKED), _sds((SUBLANES, 2 * NSTATE))],
        scratch_shapes=[pltpu.VMEM((ROW_BLOCK, 2 * NSTATE), F32), pltpu.VMEM((ROW_BLOCK, S5_WIDTH), F32),
                        pltpu.VMEM((SUBLANES, NSTATE), F32), pltpu.VMEM((SUBLANES, NSTATE), F32)],
        compiler_params=_params(("arbitrary",)),
    )(dy, z_all, states, bmat, cmat, adj, perm, perm_t)


def _glu_fwd(z_all, y0, y1, d_skip, w_glu, n_rows):
    def body(u_ref, y0_ref, y1_ref, d_ref, w_ref, o_ref):
        y = d_ref[...] * u_ref[...] + y0_ref[...] + y1_ref[...]
        g = _gelu(y)
        t = _dot(g.astype(BF16), w_ref[...], "nn")
        o_ref[...] = (g * _sigmoid(t)).astype(o_ref.dtype)

    row = pl.BlockSpec((ROW_BLOCK, S5_WIDTH), lambda i: (i, 0))
    return pl.pallas_call(
        body, name="glu_fwd", grid=(n_rows // ROW_BLOCK,),
        in_specs=[row, row, row, pl.BlockSpec((1, S5_WIDTH), lambda i: (0, 0)),
                  pl.BlockSpec((S5_WIDTH, S5_WIDTH), lambda i: (0, 0))],
        out_specs=row, out_shape=_sds((n_rows, S5_WIDTH), BF16), compiler_params=_params(("parallel",)),
    )(z_all, y0, y1, d_skip, w_glu)


def _glu_bwd(d_ycat, z_all, y0, y1, d_skip, w_glu, n_rows):
    def body(do_ref, u_ref, y0_ref, y1_ref, d_ref, w_ref, dy_ref, dw_ref, dd_ref):
        @pl.when(pl.program_id(0) == 0)
        def _():
            dw_ref[...] = jnp.zeros_like(dw_ref)
            dd_ref[...] = jnp.zeros_like(dd_ref)

        u = u_ref[...]
        y = d_ref[...] * u + y0_ref[...] + y1_ref[...]
        g = _gelu(y)
        gb = g.astype(BF16)
        w = w_ref[...]
        sg = _sigmoid(_dot(gb, w, "nn"))
        do = do_ref[...]
        dt = do * g * sg * (1.0 - sg)
        dtb = dt.astype(BF16)
        dg = do * sg + _dot(dtb, w, "nt")
        dy = dg * _dgelu(y)
        dy_ref[...] = dy
        dw_ref[...] += _dot(gb, dtb, "tn")
        dd_ref[...] += _fold8(dy * u)

    row = pl.BlockSpec((ROW_BLOCK, S5_WIDTH), lambda i: (i, 0))
    sq = pl.BlockSpec((S5_WIDTH, S5_WIDTH), lambda i: (0, 0))
    return pl.pallas_call(
        body, name="glu_bwd", grid=(n_rows // ROW_BLOCK,),
        in_specs=[row, row, row, row, pl.BlockSpec((1, S5_WIDTH), lambda i: (0, 0)), sq],
        out_specs=[row, sq, pl.BlockSpec((SUBLANES, S5_WIDTH), lambda i: (0, 0))],
        out_shape=[_sds((n_rows, S5_WIDTH)), _sds((S5_WIDTH, S5_WIDTH)), _sds((SUBLANES, S5_WIDTH))],
        compiler_params=_params(("arbitrary",)),
    )(d_ycat, z_all, y0, y1, d_skip, w_glu)


CONV_HALF = CONV_K // 2


def _conv_block(n_rows):
    blk = min(1024, n_rows)
    assert blk >= CONV_HALF * GRID_W and n_rows % blk == 0
    return blk


def _conv_gate(z_all, n_rows):
    blk = _conv_block(n_rows)
    nb = n_rows // blk

    def body(v_ref, g_ref, o_ref):
        i = pl.program_id(0)
        inside = jnp.logical_and(i >= 1, i <= nb)

        @pl.when(inside)
        def _():
            o_ref[...] = v_ref[...] * _sigmoid(g_ref[...])

        @pl.when(jnp.logical_not(inside))
        def _():
            o_ref[...] = jnp.zeros_like(o_ref)

    src = lambda col: pl.BlockSpec((blk, CONV_WIDTH), lambda i: (jnp.clip(i - 1, 0, nb - 1), col))
    return pl.pallas_call(
        body, name="conv_gate", grid=(nb + 2,), in_specs=[src(1), src(2)],
        out_specs=pl.BlockSpec((blk, CONV_WIDTH), lambda i: (i, 0)),
        out_shape=_sds(((nb + 2) * blk, CONV_WIDTH)), compiler_params=_params(("parallel",)),
    )(z_all, z_all)


def _load_window(pad_ref, win, sem, blk):
    start = pl.multiple_of(pl.program_id(0) * blk, blk)
    copy = pltpu.make_async_copy(pad_ref.at[pl.ds(start, 3 * blk), :], win, sem)
    copy.start()
    copy.wait()


def _conv_fwd(hh_pad, w, b, ln_g, ln_b, n_rows):
    blk = _conv_block(n_rows)

    def body(hh_ref, w_ref, b_ref, g_ref, lb_ref, hc_ref, y_ref, win, sem):
        _load_window(hh_ref, win, sem, blk)

        def tile(t, _):
            r0 = pl.multiple_of(t * CONV_ROWS, CONV_ROWS)
            acc = jnp.zeros((CONV_ROWS, CONV_WIDTH), F32)
            for k in range(CONV_K):
                acc = acc + w_ref[k:k + 1, :] * win[pl.ds(r0 + blk + (k - CONV_HALF) * GRID_W, CONV_ROWS), :]
            hc = acc + b_ref[...]
            hc_ref[pl.ds(r0, CONV_ROWS), :] = hc
            mu = jnp.mean(hc, axis=-1, keepdims=True)
            xc = hc - mu
            ln = xc * lax.rsqrt(jnp.mean(xc * xc, axis=-1, keepdims=True) + EPS_LN) * g_ref[...] + lb_ref[...]
            y_ref[pl.ds(r0, CONV_ROWS), :] = _silu(ln).astype(y_ref.dtype)
            return 0

        lax.fori_loop(0, blk // CONV_ROWS, tile, 0)

    vec = pl.BlockSpec((1, CONV_WIDTH), lambda i: (0, 0))
    row = pl.BlockSpec((blk, CONV_WIDTH), lambda i: (i, 0))
    return pl.pallas_call(
        body, name="conv_fwd", grid=(n_rows // blk,),
        in_specs=[ANY, pl.BlockSpec((CONV_K, CONV_WIDTH), lambda i: (0, 0)), vec, vec, vec],
        out_specs=[row, row], out_shape=[_sds((n_rows, CONV_WIDTH)), _sds((n_rows, CONV_WIDTH), BF16)],
        scratch_shapes=[pltpu.VMEM((3 * blk, CONV_WIDTH), F32), pltpu.SemaphoreType.DMA],
        compiler_params=_params(("arbitrary",)),
    )(hh_pad, w, b, ln_g, ln_b)


def _conv_bwd_norm(d_ycat, hc, ln_g, ln_b, n_rows):
    blk = _conv_block(n_rows)
    nb = n_rows // blk

    def body(dy_ref, hc_ref, g_ref, lb_ref, o_ref, sums):
        i = pl.program_id(0)

        @pl.when(i == 0)
        def _():
            sums[...] = jnp.zeros_like(sums)

        inside = jnp.logical_and(i >= 1, i <= nb)

        @pl.when(inside)
        def _():
            hcv = hc_ref[...]
            mu = jnp.mean(hcv, axis=-1, keepdims=True)
            xc = hcv - mu
            rstd = lax.rsqrt(jnp.mean(xc * xc, axis=-1, keepdims=True) + EPS_LN)
            xh = xc * rstd
            g = g_ref[...]
            dln = dy_ref[...] * _dsilu(xh * g + lb_ref[...])
            dxh = dln * g
            dhc = rstd * (dxh - jnp.mean(dxh, axis=-1, keepdims=True) - xh * jnp.mean(dxh * xh, axis=-1, keepdims=True))
            o_ref[...] = dhc
            sums[0] += _fold8(dhc)
            sums[1] += _fold8(dln * xh)
            sums[2] += _fold8(dln)

        @pl.when(jnp.logical_not(inside))
        def _():
            o_ref[...] = jnp.zeros_like(o_ref)

    vec = pl.BlockSpec((1, CONV_WIDTH), lambda i: (0, 0))
    return pl.pallas_call(
        body, name="conv_bwd_norm", grid=(nb + 2,),
        in_specs=[pl.BlockSpec((blk, CONV_WIDTH), lambda i: (jnp.clip(i - 1, 0, nb - 1), 1)),
                  pl.BlockSpec((blk, CONV_WIDTH), lambda i: (jnp.clip(i - 1, 0, nb - 1), 0)), vec, vec],
        out_specs=[pl.BlockSpec((blk, CONV_WIDTH), lambda i: (i, 0)),
                   pl.BlockSpec((3, SUBLANES, CONV_WIDTH), lambda i: (0, 0, 0))],
        out_shape=[_sds(((nb + 2) * blk, CONV_WIDTH)), _sds((3, SUBLANES, CONV_WIDTH))],
        compiler_params=_params(("arbitrary",)),
    )(d_ycat, hc, ln_g, ln_b)


def _conv_bwd_taps(dhc_pad, hh_pad, z_all, w, n_rows):
    blk = _conv_block(n_rows)

    def body(dhc_ref, hh_ref, v_ref, g_ref, w_ref, dv_ref, dg_ref, dw_ref, dwin, hwin, sems):
        @pl.when(pl.program_id(0) == 0)
        def _():
            dw_ref[...] = jnp.zeros_like(dw_ref)

        _load_window(dhc_ref, dwin, sems.at[0], blk)
        _load_window(hh_ref, hwin, sems.at[1], blk)

        def tile(t, _):
            r0 = pl.multiple_of(t * CONV_ROWS, CONV_ROWS)
            dh = dwin[pl.ds(r0 + blk, CONV_ROWS), :]
            acc = jnp.zeros((CONV_ROWS, CONV_WIDTH), F32)
            for k in range(CONV_K):
                off = (k - CONV_HALF) * GRID_W
                acc = acc + w_ref[k:k + 1, :] * dwin[pl.ds(r0 + blk - off, CONV_ROWS), :]
                dw_ref[k] += _fold8(dh * hwin[pl.ds(r0 + blk + off, CONV_ROWS), :])
            rs = pl.ds(r0, CONV_ROWS)
            sg = _sigmoid(g_ref[rs, :])
            vv = v_ref[rs, :]
            dv_ref[rs, :] = acc * sg
            dg_ref[rs, :] = acc * vv * sg * (1.0 - sg)
            return 0

        lax.fori_loop(0, blk // CONV_ROWS, tile, 0)

    row = pl.BlockSpec((blk, CONV_WIDTH), lambda i: (i, 0))
    return pl.pallas_call(
        body, name="conv_bwd_taps", grid=(n_rows // blk,),
        in_specs=[ANY, ANY,
            pl.BlockSpec((blk, CONV_WIDTH), lambda i: (i, 1)), pl.BlockSpec((blk, CONV_WIDTH), lambda i: (i, 2)),
            pl.BlockSpec((CONV_K, CONV_WIDTH), lambda i: (0, 0))],
        out_specs=[row, row, pl.BlockSpec((CONV_K, SUBLANES, CONV_WIDTH), lambda i: (0, 0, 0))],
        out_shape=[_sds((n_rows, CONV_WIDTH)), _sds((n_rows, CONV_WIDTH)), _sds((CONV_K, SUBLANES, CONV_WIDTH))],
        scratch_shapes=[pltpu.VMEM((3 * blk, CONV_WIDTH), F32), pltpu.VMEM((3 * blk, CONV_WIDTH), F32),
                        pltpu.SemaphoreType.DMA((2,))],
        compiler_params=_params(("arbitrary",)),
    )(dhc_pad, hh_pad, z_all, z_all, w)


def _dz_assemble(du0, du1, dy, d_skip, dv, dgate, n_lat):
    rows = du0.shape[0]
    nb = rows // ROW_BLOCK

    w = S5_WIDTH

    def body(a_ref, b_ref, dy_ref, d_ref, dv_ref, dg_ref, o_ref):
        lat = pl.program_id(0) < n_lat

        @pl.when(lat)
        def _():
            o_ref[:, 0:w] = (a_ref[...] + b_ref[...] + dy_ref[...] * d_ref[...]).astype(o_ref.dtype)
            o_ref[:, w:2 * w] = dv_ref[...].astype(o_ref.dtype)
            o_ref[:, 2 * w:3 * w] = dg_ref[...].astype(o_ref.dtype)

        @pl.when(jnp.logical_not(lat))
        def _():
            o_ref[:, 0:w] = (a_ref[...] + b_ref[...]).astype(o_ref.dtype)
            o_ref[:, w:3 * w] = jnp.zeros((ROW_BLOCK, 2 * w), o_ref.dtype)

    all_rows = pl.BlockSpec((ROW_BLOCK, w), lambda i: (i, 0))
    lat_rows = pl.BlockSpec((ROW_BLOCK, w), lambda i: (jnp.minimum(i, n_lat - 1), 0))
    return pl.pallas_call(
        body, name="dz_assemble", grid=(nb,),
        in_specs=[all_rows, all_rows, lat_rows, pl.BlockSpec((1, w), lambda i: (0, 0)), lat_rows, lat_rows],
        out_specs=pl.BlockSpec((ROW_BLOCK, IN_COLS), lambda i: (i, 0)),
        out_shape=_sds((rows, IN_COLS), BF16), compiler_params=_params(("parallel",)),
    )(du0, du1, dy, d_skip, dv, dgate)


def _sum_parts(parts):
    _, r, c = parts.shape

    def body(p_ref, o_ref):
        acc = p_ref[0]
        for q in range(1, NDEV):
            acc = acc + p_ref[q]
        o_ref[...] = acc

    return pl.pallas_call(body, name="sum_parts", out_shape=_sds((r, c)), compiler_params=_params())(parts)


def _row_tile(r, c):
    best = r
    for t in (1024, 512, 256, 128, 64, 32, 16, 8):
        if r % t == 0 and t * c <= 128 * 1024:
            return t
    return best


def _adamw(name, w, gparts, m, v):
    r, c = w.shape
    np_ = gparts.shape[0]
    tr = _row_tile(r, c)

    def body(w_ref, g_ref, m_ref, v_ref, go_ref, d_ref, mo_ref, vo_ref):
        g = g_ref[0]
        for q in range(1, np_):
            g = g + g_ref[q]
        m2 = ADAM_B1 * m_ref[...] + (1.0 - ADAM_B1) * g
        v2 = ADAM_B2 * v_ref[...] + (1.0 - ADAM_B2) * jnp.square(g)
        m_hat = m2 / (1.0 - ADAM_B1 ** ADAM_STEP)
        v_hat = v2 / (1.0 - ADAM_B2 ** ADAM_STEP)
        go_ref[...] = g
        d_ref[...] = -ADAM_LR * (m_hat / (jnp.sqrt(v_hat) + ADAM_EPS) + ADAM_WD * w_ref[...])
        mo_ref[...] = m2
        vo_ref[...] = v2

    row = pl.BlockSpec((tr, c), lambda i: (i, 0))
    return pl.pallas_call(
        body, name=name, grid=(r // tr,),
        in_specs=[row, pl.BlockSpec((np_, tr, c), lambda i: (0, i, 0)), row, row],
        out_specs=[row] * 4, out_shape=[_sds((r, c))] * 4, compiler_params=_params(("parallel",)),
    )(w, gparts, m, v)


SMALL = ["c_ctx", "ada_b", "norm1_g", "s5_lam_re", "s5_lam_im", "s5_log_dt", "s5_b_re", "s5_b_im", "s5_c_re",
         "s5_c_im", "s5_d", "conv_b", "conv_ln_g", "conv_ln_b", "norm2_g", "final_g"]


def _pack_small(parts):
    flat = jnp.concatenate([p.reshape(-1).astype(F32) for p in parts])
    return jnp.pad(flat, (0, SMALL_ROWS * D_MODEL - flat.shape[0])).reshape(SMALL_ROWS, D_MODEL)


def _unpack_small(packed, like):
    flat = packed.reshape(-1)
    out, off = [], 0
    for ref in like:
        out.append(flat[off:off + ref.size].reshape(ref.shape))
        off += ref.size
    return out


def kernel(x, c, ctx, c_ctx, ada_w, ada_b, norm1_g, w_in, s5_lam_re, s5_lam_im, s5_log_dt, s5_b_re, s5_b_im, s5_c_re, s5_c_im, s5_d, s5_w_glu, conv_w, conv_b, conv_ln_g, conv_ln_b, w_out, norm2_g, mlp_w1, mlp_w2, final_g, loss_target, m_c_ctx, m_ada_w, m_ada_b, m_norm1_g, m_w_in, m_s5_lam_re, m_s5_lam_im, m_s5_log_dt, m_s5_b_re, m_s5_b_im, m_s5_c_re, m_s5_c_im, m_s5_d, m_s5_w_glu, m_conv_w, m_conv_b, m_conv_ln_g, m_conv_ln_b, m_w_out, m_norm2_g, m_mlp_w1, m_mlp_w2, m_final_g, v_c_ctx, v_ada_w, v_ada_b, v_norm1_g, v_w_in, v_s5_lam_re, v_s5_lam_im, v_s5_log_dt, v_s5_b_re, v_s5_b_im, v_s5_c_re, v_s5_c_im, v_s5_d, v_s5_w_glu, v_conv_w, v_conv_b, v_conv_ln_g, v_conv_ln_b, v_w_out, v_norm2_g, v_mlp_w1, v_mlp_w2, v_final_g):
    weights = dict(c_ctx=c_ctx, ada_w=ada_w, ada_b=ada_b, norm1_g=norm1_g, w_in=w_in, s5_lam_re=s5_lam_re, s5_lam_im=s5_lam_im, s5_log_dt=s5_log_dt, s5_b_re=s5_b_re, s5_b_im=s5_b_im, s5_c_re=s5_c_re, s5_c_im=s5_c_im, s5_d=s5_d, s5_w_glu=s5_w_glu, conv_w=conv_w, conv_b=conv_b, conv_ln_g=conv_ln_g, conv_ln_b=conv_ln_b, w_out=w_out, norm2_g=norm2_g, mlp_w1=mlp_w1, mlp_w2=mlp_w2, final_g=final_g)
    mom1 = dict(c_ctx=m_c_ctx, ada_w=m_ada_w, ada_b=m_ada_b, norm1_g=m_norm1_g, w_in=m_w_in, s5_lam_re=m_s5_lam_re, s5_lam_im=m_s5_lam_im, s5_log_dt=m_s5_log_dt, s5_b_re=m_s5_b_re, s5_b_im=m_s5_b_im, s5_c_re=m_s5_c_re, s5_c_im=m_s5_c_im, s5_d=m_s5_d, s5_w_glu=m_s5_w_glu, conv_w=m_conv_w, conv_b=m_conv_b, conv_ln_g=m_conv_ln_g, conv_ln_b=m_conv_ln_b, w_out=m_w_out, norm2_g=m_norm2_g, mlp_w1=m_mlp_w1, mlp_w2=m_mlp_w2, final_g=m_final_g)
    mom2 = dict(c_ctx=v_c_ctx, ada_w=v_ada_w, ada_b=v_ada_b, norm1_g=v_norm1_g, w_in=v_w_in, s5_lam_re=v_s5_lam_re, s5_lam_im=v_s5_lam_im, s5_log_dt=v_s5_log_dt, s5_b_re=v_s5_b_re, s5_b_im=v_s5_b_im, s5_c_re=v_s5_c_re, s5_c_im=v_s5_c_im, s5_d=v_s5_d, s5_w_glu=v_s5_w_glu, conv_w=v_conv_w, conv_b=v_conv_b, conv_ln_g=v_conv_ln_g, conv_ln_b=v_conv_ln_b, w_out=v_w_out, norm2_g=v_norm2_g, mlp_w1=v_mlp_w1, mlp_w2=v_mlp_w2, final_g=v_final_g)
    order = list(weights)

    me = 4 * lax.axis_index("x") + 2 * lax.axis_index("y") + lax.axis_index("c")
    xs, cs, tgt = x[0], ctx[0], loss_target[0]
    n_lat_rows, n_ctx_rows = xs.shape[0], cs.shape[0]
    n_rows = n_lat_rows + n_ctx_rows
    n_lat = n_lat_rows // ROW_BLOCK
    ada_cols = ada_w.shape[2]

    (w_in_g, c_all), w_in_token = _exchange("gather_w_in", [w_in[0].astype(BF16), c], [True] * 2)
    w_in_full = jnp.transpose(w_in_g, (1, 0, 2)).reshape(D_MODEL, IN_COLS)
    mixer_w = [s5_w_glu[0].astype(BF16), conv_w[0] + w_in_token[0:1, 0:1], w_out[0].astype(BF16)]
    mixer_send, mixer_recv, mixer_src, mixer_land, mixer_token = _exchange_start("gather_mixer_start", mixer_w, [True] * 3)
    mlp_w = [mlp_w1[0].astype(BF16), mlp_w2[0].astype(BF16) + mixer_token[0:1, 0:1].astype(BF16)]
    mlpw_send, mlpw_recv, mlpw_src, mlpw_land, mlpw_token = _exchange_start("gather_mlp_start", mlp_w, [True] * 2)
    c_all = c_all.reshape(NDEV, D_MODEL) + mlpw_token[0:1, 0:1]

    cond_fwd = jnp.concatenate([c_all, c_ctx[None], jnp.zeros((7, D_MODEL), F32)])
    ada_b_loc = lax.dynamic_slice(ada_b, (0, me * ada_cols), (1, ada_cols))
    (mod_g,), _ = _exchange("gather_mod", [_ada_fwd(cond_fwd, ada_w[0], ada_b_loc)], [True])
    mod_rows = jnp.transpose(mod_g, (1, 0, 2)).reshape(16, 6 * D_MODEL)
    mod = lax.dynamic_slice(mod_rows, (me, 0), (1, 6 * D_MODEL)).reshape(6, D_MODEL)
    modc = mod_rows[8, :2 * D_MODEL].reshape(2, D_MODEL)
    sh1, sc1, g1, sh2, sc2, g2 = [mod[i:i + 1] for i in range(6)]

    a_all = _prenorm("prenorm1", xs, cs, norm1_g, jnp.stack([mod[0:2], modc]))
    (z_all,) = _matmul("in_proj", a_all, w_in_full, "nn", (n_rows, IN_COLS, D_MODEL), (ROW_BLOCK, IN_COLS, D_MODEL),
                       [((n_rows, IN_COLS), F32)])

    lam_re, lam_im = s5_lam_re[0].reshape(2, 1, NSTATE), s5_lam_im[0].reshape(2, 1, NSTATE)
    ldt = jnp.repeat(s5_log_dt[0], S5_STATE, axis=-1).reshape(2, 1, NSTATE)
    bt_re = jnp.transpose(s5_b_re[0], (0, 3, 1, 2)).reshape(2, S5_GROUP, NSTATE)
    bt_im = jnp.transpose(s5_b_im[0], (0, 3, 1, 2)).reshape(2, S5_GROUP, NSTATE)
    groups_per_block = S5_GROUPS // S5_BLOCKS
    ct_re = jnp.tile(s5_c_re[0].reshape(2, S5_WIDTH, S5_STATE), (1, 1, groups_per_block))
    ct_im = jnp.tile(s5_c_im[0].reshape(2, S5_WIDTH, S5_STATE), (1, 1, groups_per_block))
    d_skip = s5_d[0].reshape(1, S5_WIDTH)
    perm = _segment_permutation()
    perm_t = perm.T
    disc, states, y_dir = [], [], []
    for d in range(2):
        disc.append(_s5_discretise(f"s5_disc{d}", d == 0, lam_re[d], lam_im[d], ldt[d], bt_re[d], bt_im[d], ct_re[d], ct_im[d]))
        _, tab, _, bmat, cmat = disc[d]
        s, yd = _s5_scan_fwd(f"s5_scan_fwd{d}", d == 0, z_all, bmat, cmat, tab, perm, perm_t)
        states.append(s)
        y_dir.append(yd)
    mixer_own, mixer_landed = _exchange_wait("gather_mixer_wait", mixer_send, mixer_recv, mixer_src, mixer_land,
                                             [True] * 3, y_dir[1])
    glu_g, conv_w_g, w_out_g = [_with_own(l, o, me) for l, o in zip(mixer_landed, mixer_own)]
    glu_full = glu_g.reshape(S5_WIDTH, S5_WIDTH)
    conv_w_full = jnp.transpose(conv_w_g, (1, 0, 2)).reshape(CONV_K, CONV_WIDTH)
    w_out_full = w_out_g.reshape(D_MODEL, D_MODEL)
    y_s5 = _glu_fwd(z_all, y_dir[0], y_dir[1], d_skip, glu_full, n_lat_rows)

    hh_pad = _conv_gate(z_all, n_lat_rows)
    hc, y_conv = _conv_fwd(hh_pad, conv_w_full, conv_b, conv_ln_g, conv_ln_b, n_lat_rows)

    ycat = jnp.concatenate([y_s5, y_conv], axis=1)
    tm = min(1024, n_lat_rows)
    w1_cols = D_FF // NDEV
    row_vec = lambda tn: pl.BlockSpec((1, tn), lambda i, j, k: (0, j))
    out_tile = lambda t_m, t_n: pl.BlockSpec((t_m, t_n), lambda i, j, k: (i, j))
    gated = lambda acc, res, gate: (acc, res + gate * acc)
    mix, h1 = _matmul("out_proj", ycat, w_out_full, "nn", (n_lat_rows, D_MODEL, D_MODEL), (tm, D_MODEL, D_MODEL),
                      [((n_lat_rows, D_MODEL), F32)] * 2, epi=gated,
                      epi_extra=[(xs, out_tile(tm, D_MODEL)), (g1, row_vec(D_MODEL))])
    a2 = _prenorm("prenorm2", h1, None, norm2_g, mod[3:5][None])
    mlpw_own, mlpw_landed = _exchange_wait("gather_mlp_wait", mlpw_send, mlpw_recv, mlpw_src, mlpw_land, [True] * 2, a2)
    w1_g, w2_g = [_with_own(l, o, me) for l, o in zip(mlpw_landed, mlpw_own)]
    w2_full = w2_g.reshape(D_FF, D_MODEL)
    tm_up = min(2048, n_lat_rows)
    (f,) = _matmul("mlp_up", a2, w1_g, "nn", (n_lat_rows, D_FF, D_MODEL), (tm_up, w1_cols, D_MODEL),
                   [((n_lat_rows, D_FF), BF16)], b_spec=pl.BlockSpec((None, D_MODEL, w1_cols), lambda i, j, k: (j, 0, 0)))
    sq_relu = lambda t: jnp.square(jnp.maximum(t, 0.0))
    mlp_out, h2 = _matmul("mlp_down", f, w2_full, "nn", (n_lat_rows, D_MODEL, D_FF), (tm, D_MODEL, 1024),
                          [((n_lat_rows, D_MODEL), F32)] * 2, a_fn=sq_relu, epi=gated,
                          epi_extra=[(h1, out_tile(tm, D_MODEL)), (g2, row_vec(D_MODEL))])

    d_h2, dm2, err_sums, d_final_g8 = _loss_head(h2, tgt, final_g[None], g2)
    loss = lax.psum(0.5 / D_MODEL * jnp.sum(err_sums), ("x", "y", "c"))

    (d_f,) = _matmul("mlp_down_dx", dm2, w2_full, "nt", (n_lat_rows, D_FF, D_MODEL), (tm, 512, D_MODEL),
                     [((n_lat_rows, D_FF), BF16)],
                     epi=lambda acc, ft: (acc * 2.0 * jnp.maximum(ft.astype(F32), 0.0),), epi_extra=[(f, out_tile(tm, 512))])
    (g_w2,) = _matmul("mlp_down_dw", f, dm2, "tn", (D_FF, D_MODEL, n_lat_rows), (1024, D_MODEL, tm),
                      [((D_FF, D_MODEL), F32)], a_fn=sq_relu)
    (d_a2,) = _matmul("mlp_up_dx", d_f, w1_g, "nt", (n_lat_rows, D_MODEL, D_FF), (tm, D_MODEL, w1_cols),
                      [((n_lat_rows, D_MODEL), F32)],
                      b_spec=pl.BlockSpec((None, D_MODEL, w1_cols), lambda i, j, k: (k, 0, 0)))
    (g_w1,) = _matmul("mlp_up_dw", a2, d_f, "tn", (D_MODEL, D_FF, n_lat_rows), (D_MODEL, w1_cols, tm),
                      [((NDEV, D_MODEL, w1_cols), F32)],
                      out_specs=[pl.BlockSpec((None, D_MODEL, w1_cols), lambda i, j, k: (j, 0, 0))])
    mlp_send, mlp_recv, mlp_src, mlp_land, mlp_token = _exchange_start(
        "scatter_mlp_start", [g_w1, g_w2.reshape(NDEV, D_FF // NDEV, D_MODEL)], [False] * 2)
    d_h1, dm1, sums2 = _norm_bwd("norm2_bwd", h1, d_a2, 0, norm2_g, sc2 + mlp_token[0:1, 0:1], res=d_h2, aux=mlp_out,
                                 gate=g1)

    (d_ycat,) = _matmul("out_proj_dx", dm1, w_out_full, "nt", (n_lat_rows, D_MODEL, D_MODEL), (tm, D_MODEL, D_MODEL),
                        [((n_lat_rows, D_MODEL), F32)])
    (g_w_out,) = _matmul("out_proj_dw", ycat, dm1, "tn", (D_MODEL, D_MODEL, n_lat_rows), (D_MODEL, D_MODEL, 512),
                         [((D_MODEL, D_MODEL), F32)])

    dy, g_glu, dd8 = _glu_bwd(d_ycat, z_all, y_dir[0], y_dir[1], d_skip, glu_full, n_lat_rows)
    proj_send, proj_recv, proj_src, proj_land, proj_token = _exchange_start(
        "scatter_proj_start",
        [g_w_out.reshape(NDEV, D_MODEL // NDEV, D_MODEL), g_glu.reshape(NDEV, S5_WIDTH // NDEV, S5_WIDTH)], [False] * 2)
    perm = perm + proj_token[0:1, 0:1].astype(BF16)
    du, g_lam_re, g_lam_im, g_ldt, g_b_re, g_b_im, g_c_re, g_c_im = [], [], [], [], [], [], [], []

    def diag(mat):
        return jnp.diagonal(mat.reshape(S5_BLOCKS, groups_per_block, S5_GROUP, 2, groups_per_block, S5_STATE), axis1=1, axis2=4)

    for d in range(2):
        _, _, adj, bmat, cmat = disc[d]
        du_d, d_bmat, d_cmat, d_abar8 = _s5_scan_bwd(f"s5_scan_bwd{d}", d == 0, dy, z_all, states[d], bmat, cmat, adj,
                                                     perm, perm_t)
        du.append(du_d)
        d_bbar = jnp.transpose(diag(d_bmat), (2, 1, 0, 4, 3)).reshape(2 * S5_GROUP, NSTATE)
        d_c = jnp.transpose(diag(d_cmat), (2, 0, 4, 1, 3)).reshape(2, S5_GROUPS, S5_GROUP, S5_STATE)
        d_lam8, d_bt = _s5_discretise_bwd(f"s5_disc_bwd{d}", lam_re[d], lam_im[d], ldt[d], bt_re[d], bt_im[d], d_abar8, d_bbar)
        g_lam_re.append(d_lam8[0].reshape(S5_GROUPS, S5_STATE))
        g_lam_im.append(d_lam8[1].reshape(S5_GROUPS, S5_STATE))
        g_ldt.append(d_lam8[2].reshape(S5_GROUPS, S5_STATE).sum(axis=-1))
        to_gph = lambda t: jnp.transpose(t.reshape(S5_GROUP, S5_GROUPS, S5_STATE), (1, 2, 0))
        g_b_re.append(to_gph(d_bt[:S5_GROUP]))
        g_b_im.append(to_gph(d_bt[S5_GROUP:]))
        g_c_re.append(d_c[0])
        g_c_im.append(-d_c[1])

    dhc_pad, conv_sums = _conv_bwd_norm(d_ycat, hc, conv_ln_g, conv_ln_b, n_lat_rows)
    d_v, d_gate, g_conv_w8 = _conv_bwd_taps(dhc_pad, hh_pad, z_all, conv_w_full, n_lat_rows)

    dz_all = _dz_assemble(du[0], du[1], dy, d_skip, d_v, d_gate, n_lat)
    (d_a_all,) = _matmul("in_proj_dx", dz_all, w_in_full, "nt", (n_rows, D_MODEL, IN_COLS), (ROW_BLOCK, D_MODEL, IN_COLS),
                         [((n_rows, D_MODEL), F32)])
    (g_w_in_full,) = _matmul("in_proj_dw", a_all, dz_all, "tn", (D_MODEL, IN_COLS, n_rows), (D_MODEL, IN_COLS, ROW_BLOCK),
                             [((D_MODEL, IN_COLS), F32)])
    g_w_in_parts = jnp.transpose(g_w_in_full.reshape(D_MODEL, NDEV, IN_COLS // NDEV), (1, 0, 2))
    win_send, win_recv, win_src, win_land, win_token = _exchange_start("scatter_w_in_start", [g_w_in_parts], [False])
    grad_x, sums1 = _norm_bwd("norm1_bwd", xs, d_a_all, 0, norm1_g, sc1 + win_token[0:1, 0:1], res=d_h1, aux=mix)
    (sums1c,) = _norm_bwd("norm1_bwd_ctx", cs, d_a_all, n_lat, norm1_g, modc[1:2])

    s1, s1c, s2 = sums1.sum(axis=1), sums1c.sum(axis=1), sums2.sum(axis=1)
    d_mod = jnp.concatenate([s1[0], s1[1], s1[3], s2[0], s2[1], s2[3]])
    d_modc = jnp.concatenate([s1c[0], s1c[1], jnp.zeros((4 * D_MODEL,), F32)])
    (dmod_g,), _ = _exchange("gather_dmod", [jnp.stack([d_mod, d_modc])], [True])
    dmod16 = jnp.concatenate([dmod_g[:, 0], dmod_g[:, 1]])
    dmod16_loc = lax.dynamic_slice(dmod16, (0, me * ada_cols), (16, ada_cols))
    cond_bwd = jnp.concatenate([c_all, jnp.broadcast_to(c_ctx[None], (NDEV, D_MODEL))])
    g_ada_w, g_c_ctx8 = _ada_bwd(cond_bwd, dmod16_loc, ada_w[0], c_ctx[None])

    small_parts = dict(
        c_ctx=g_c_ctx8[0], ada_b=d_mod + d_modc, norm1_g=s1[2] + s1c[2],
        s5_lam_re=jnp.stack(g_lam_re), s5_lam_im=jnp.stack(g_lam_im), s5_log_dt=jnp.stack(g_ldt),
        s5_b_re=jnp.stack(g_b_re), s5_b_im=jnp.stack(g_b_im), s5_c_re=jnp.stack(g_c_re), s5_c_im=jnp.stack(g_c_im),
        s5_d=dd8.sum(axis=0), conv_b=conv_sums[0].sum(axis=0), conv_ln_g=conv_sums[1].sum(axis=0),
        conv_ln_b=conv_sums[2].sum(axis=0), norm2_g=s2[2], final_g=d_final_g8.sum(axis=0))
    small_g = _pack_small([small_parts[n] for n in SMALL]).reshape(NDEV, SMALL_ROWS // NDEV, D_MODEL)
    g_conv_w_parts = jnp.transpose(g_conv_w8.sum(axis=1).reshape(CONV_K, NDEV, CONV_WIDTH // NDEV), (1, 0, 2))
    (p_conv_w, p_small), _ = _exchange("scatter_grads", [g_conv_w_parts, small_g], [False] * 2)
    (small_all,), _ = _exchange("gather_small", [_sum_parts(p_small)], [True])
    small_all = small_all.reshape(1, SMALL_ROWS, D_MODEL)

    def own_chunk(src):
        return lax.dynamic_index_in_dim(src, me, 0, keepdims=False)

    win_src, win_landed = _exchange_wait("scatter_w_in_wait", win_send, win_recv, win_src, win_land, [False], small_all)
    p_w_in = _with_own(win_landed[0], own_chunk(win_src[0]), me)
    mlp_src, mlp_landed = _exchange_wait("scatter_mlp_wait", mlp_send, mlp_recv, mlp_src, mlp_land, [False] * 2, small_all)
    p_w1, p_w2 = [_with_own(l, own_chunk(s), me) for l, s in zip(mlp_landed, mlp_src)]
    proj_src, proj_landed = _exchange_wait("scatter_proj_wait", proj_send, proj_recv, proj_src, proj_land, [False] * 2,
                                           small_all)
    p_w_out, p_glu = [_with_own(l, own_chunk(s), me) for l, s in zip(proj_landed, proj_src)]

    res = {}
    big = dict(ada_w=g_ada_w[None], w_in=p_w_in, s5_w_glu=p_glu, conv_w=p_conv_w, w_out=p_w_out, mlp_w1=p_w1, mlp_w2=p_w2)
    for name, parts in big.items():
        outs = _adamw("adamw_" + name, weights[name][0], parts, mom1[name][0], mom2[name][0])
        res[name] = [o[None] for o in outs]
    small_like = [weights[n] for n in SMALL]
    outs = _adamw("adamw_small", _pack_small(small_like), small_all, _pack_small([mom1[n] for n in SMALL]),
                  _pack_small([mom2[n] for n in SMALL]))
    unpacked = [_unpack_small(o, small_like) for o in outs]
    for i, name in enumerate(SMALL):
        res[name] = [u[i] for u in unpacked]

    return (loss, grad_x[None], *[res[n][0] for n in order], *[res[n][1] for n in order],
            *[res[n][2] for n in order], *[res[n][3] for n in order])
```

```python
import functools

import jax
import jax.numpy as jnp
from jax import lax
from jax.experimental import pallas as pl
from jax.experimental.pallas import tpu as pltpu

F32 = jnp.float32
BF16 = jnp.bfloat16
MESH = pl.DeviceIdType.MESH
ANY = pl.BlockSpec(memory_space=pl.ANY)

NDEV = 8
D_MODEL = 1024
GRID_W = 64
S5_WIDTH = 512
S5_GROUP = 16
S5_GROUPS = 32
S5_STATE = 64
NSTATE = S5_GROUPS * S5_STATE
CONV_WIDTH = 512
CONV_K = 31
IN_COLS = S5_WIDTH + 2 * CONV_WIDTH
D_FF = 4 * D_MODEL
EPS_RMS = 1e-6
EPS_LN = 1e-5
ADAM_LR = 0.001
ADAM_B1 = 0.9
ADAM_B2 = 0.999
ADAM_EPS = 1e-08
ADAM_WD = 0.01
ADAM_STEP = 10

SUBLANES = 8
LANES = 128
ROW_BLOCK = 256
SCAN_LANES = 512
SEGMENTS = SUBLANES
STEPS = ROW_BLOCK // SEGMENTS
S5_BLOCKS = 4
S5_BLOCK_WIDTH = S5_WIDTH // S5_BLOCKS
CONV_ROWS = 64
VMEM_LIMIT = 48 * 1024 * 1024
SMALL_ROWS = 320


def _params(sem=None):
    kw = dict(vmem_limit_bytes=VMEM_LIMIT)
    if sem is not None:
        kw["dimension_semantics"] = sem
    return pltpu.CompilerParams(**kw)


def _sds(shape, dtype=F32):
    return jax.ShapeDtypeStruct(tuple(shape), dtype)


def _fold8(x):
    return x.reshape(x.shape[0] // SUBLANES, SUBLANES, x.shape[1]).sum(axis=0)


def _sigmoid(x):
    return 1.0 / (1.0 + jnp.exp(-x))


def _silu(x):
    return x * _sigmoid(x)


def _dsilu(x):
    s = _sigmoid(x)
    return s * (1.0 + x * (1.0 - s))


_GELU_C = 0.7978845608028654


def _gelu(x):
    return 0.5 * x * (1.0 + jnp.tanh(_GELU_C * (x + 0.044715 * x * x * x)))


def _dgelu(x):
    t = jnp.tanh(_GELU_C * (x + 0.044715 * x * x * x))
    return 0.5 * (1.0 + t) + 0.5 * x * (1.0 - t * t) * _GELU_C * (1.0 + 3.0 * 0.044715 * x * x)


def _dot(a, b, mode):
    dims = {"nn": (((1,), (0,)), ((), ())), "nt": (((1,), (1,)), ((), ())), "tn": (((0,), (0,)), ((), ()))}[mode]
    return lax.dot_general(a, b, dims, preferred_element_type=F32)


def _peers(x, y, c):
    out = []
    for k in range(1, NDEV):
        px = 1 - x if k & 4 else x
        py = 1 - y if k & 2 else y
        pc = 1 - c if k & 1 else c
        out.append(((px, py, pc), 4 * px + 2 * py + pc))
    return out


def _exchange_copies(src, land, send_sems, recv_sems, gather):
    x, y, c = lax.axis_index("x"), lax.axis_index("y"), lax.axis_index("c")
    me = 4 * x + 2 * y + c
    out = []
    for a in range(len(src)):
        for k, (peer, plin) in enumerate(_peers(x, y, c)):
            chunk = src[a] if gather[a] else src[a].at[plin]
            sems = dict(send_sem=send_sems.at[a * (NDEV - 1) + k], recv_sem=recv_sems.at[a * (NDEV - 1) + k],
                        device_id=peer, device_id_type=MESH)
            out.append((pltpu.make_async_remote_copy(src_ref=chunk, dst_ref=land[a].at[me], **sems),
                        pltpu.make_async_remote_copy(src_ref=chunk, dst_ref=land[a].at[plin], **sems)))
    return out


def _exchange(name, srcs, gather):
    n = len(srcs)
    outs = [_sds(((NDEV,) + s.shape) if g else s.shape, s.dtype) for s, g in zip(srcs, gather)]

    def body(*refs):
        src, dst, token = refs[:n], refs[n:2 * n], refs[2 * n]
        send_sems, recv_sems, local_sems = refs[2 * n + 1:]
        me = 4 * lax.axis_index("x") + 2 * lax.axis_index("y") + lax.axis_index("c")
        local = [pltpu.make_async_copy(src[a] if gather[a] else src[a].at[me], dst[a].at[me], local_sems.at[a])
                 for a in range(n)]
        for copy in local:
            copy.start()
        copies = _exchange_copies(src, dst, send_sems, recv_sems, gather)
        for copy, _ in copies:
            copy.start()
        token[...] = jnp.zeros_like(token)
        for copy, landing in copies:
            copy.wait_send()
            landing.wait_recv()
        for copy in local:
            copy.wait()

    nsem = n * (NDEV - 1)
    out = pl.pallas_call(
        body, name=name, out_shape=outs + [_sds((SUBLANES, LANES))], in_specs=[ANY] * n,
        out_specs=[ANY] * n + [pl.BlockSpec(memory_space=pltpu.VMEM)],
        scratch_shapes=[pltpu.SemaphoreType.DMA((nsem,)), pltpu.SemaphoreType.DMA((nsem,)), pltpu.SemaphoreType.DMA((n,))],
    )(*srcs)
    return out[:n], out[n]


HBM = pl.BlockSpec(memory_space=pltpu.HBM)
SEM = pl.BlockSpec(memory_space=pltpu.SEMAPHORE)
EFFECT = pltpu.SideEffectType.DATAFLOW_SIDE_EFFECTING


def _exchange_start(name, srcs, gather):
    n = len(srcs)
    lands = [lax.empty(((NDEV,) + s.shape) if g else s.shape, s.dtype) for s, g in zip(srcs, gather)]

    def body(*refs):
        src, land = refs[:n], refs[n:2 * n]
        send_sems, recv_sems = refs[2 * n], refs[2 * n + 1]
        token = refs[-1]
        for copy, _ in _exchange_copies(src, land, send_sems, recv_sems, gather):
            copy.start()
        token[...] = jnp.zeros_like(token)

    hbm = lambda v: pltpu.HBM(v.shape, v.dtype)
    nsem = n * (NDEV - 1)
    out = pl.pallas_call(
        body, name=name,
        out_shape=(pltpu.SemaphoreType.DMA((nsem,)), pltpu.SemaphoreType.DMA((nsem,)), *[hbm(v) for v in srcs],
                   *[hbm(v) for v in lands], _sds((SUBLANES, LANES))),
        in_specs=[HBM] * (2 * n), out_specs=(SEM, SEM, *([HBM] * (2 * n)), pl.BlockSpec(memory_space=pltpu.VMEM)),
        input_output_aliases={i: 2 + i for i in range(2 * n)},
        compiler_params=pltpu.CompilerParams(has_side_effects=EFFECT),
    )(*[pltpu.with_memory_space_constraint(v, pltpu.HBM) for v in list(srcs) + lands])
    return out[0], out[1], out[2:2 + n], out[2 + n:2 + 2 * n], out[-1]


def _exchange_wait(name, send_sems, recv_sems, srcs, lands, gather, after):
    n = len(srcs)

    def body(*refs):
        src, land = refs[:n], refs[n:2 * n]
        send_ref, recv_ref = refs[2 * n], refs[2 * n + 1]
        for copy, landing in _exchange_copies(src, land, send_ref, recv_ref, gather):
            copy.wait_send()
            landing.wait_recv()

    hbm = lambda v: pltpu.HBM(v.shape, v.dtype)
    out = pl.pallas_call(
        body, name=name, out_shape=[hbm(v) for v in list(srcs) + list(lands)],
        in_specs=[HBM] * (2 * n) + [SEM, SEM, ANY], out_specs=[HBM] * (2 * n),
        input_output_aliases={i: i for i in range(2 * n)},
        compiler_params=pltpu.CompilerParams(has_side_effects=EFFECT),
    )(*srcs, *lands, send_sems, recv_sems, after)
    return out[:n], out[n:]


def _with_own(landed, own, me):
    return lax.dynamic_update_slice(landed, own[None], (me,) + (0,) * own.ndim)


def _matmul(name, a, b, mode, mnk, tiles, outs, a_spec=None, b_spec=None, a_fn=None, a_extra=(),
            epi=None, epi_extra=(), out_specs=None):
    m_, n_, k_ = mnk
    tm, tn, tk = tiles
    nk = k_ // tk
    if a_spec is None:
        a_spec = (pl.BlockSpec((tk, tm), lambda i, j, k: (k, i)) if mode == "tn"
                  else pl.BlockSpec((tm, tk), lambda i, j, k: (i, k)))
    if b_spec is None:
        b_spec = (pl.BlockSpec((tn, tk), lambda i, j, k: (j, k)) if mode == "nt"
                  else pl.BlockSpec((tk, tn), lambda i, j, k: (k, j)))
    if out_specs is None:
        out_specs = [pl.BlockSpec((tm, tn), lambda i, j, k: (i, j)) for _ in outs]
    na, ne, no = len(a_extra), len(epi_extra), len(outs)

    def body(*refs):
        a_ref, b_ref = refs[0], refs[1]
        ax = refs[2:2 + na]
        ex = refs[2 + na:2 + na + ne]
        o = refs[2 + na + ne:2 + na + ne + no]

        def finish(res):
            res = epi(res, *[r[...] for r in ex]) if epi is not None else (res,)
            for ref, val in zip(o, res):
                ref[...] = val.astype(ref.dtype)

        at = a_ref[...]
        if a_fn is not None:
            at = a_fn(at, *[r[...] for r in ax])
        part = _dot(at.astype(BF16), b_ref[...].astype(BF16), mode)
        if nk == 1:
            finish(part)
            return
        acc = refs[-1]
        k = pl.program_id(2)

        @pl.when(k == 0)
        def _():
            acc[...] = part

        @pl.when(k > 0)
        def _():
            acc[...] += part

        @pl.when(k == nk - 1)
        def _():
            finish(acc[...])

    return pl.pallas_call(
        body, name=name, grid=(m_ // tm, n_ // tn, nk),
        in_specs=[a_spec, b_spec] + [s for _, s in a_extra] + [s for _, s in epi_extra],
        out_specs=out_specs, out_shape=[_sds(s, d) for s, d in outs],
        scratch_shapes=[pltpu.VMEM((tm, tn), F32)] if nk > 1 else [],
        compiler_params=_params(("parallel", "parallel", "arbitrary")),
    )(a, b, *[x for x, _ in a_extra], *[x for x, _ in epi_extra])


def _prenorm(name, x, ctx, gain, shsc):
    n_lat = x.shape[0] // ROW_BLOCK
    n_ctx = 0 if ctx is None else ctx.shape[0] // ROW_BLOCK
    d = x.shape[1]

    def norm(src, g_ref, m_ref, o_ref):
        xv = src[...]
        xh = xv * lax.rsqrt(jnp.mean(xv * xv, axis=-1, keepdims=True) + EPS_RMS)
        o_ref[...] = ((xh * g_ref[...]) * (1.0 + m_ref[1:2, :]) + m_ref[0:1, :]).astype(o_ref.dtype)

    def body(*refs):
        if ctx is None:
            x_ref, g_ref, m_ref, o_ref = refs
            norm(x_ref, g_ref, m_ref, o_ref)
        else:
            x_ref, c_ref, g_ref, m_ref, o_ref = refs
            i = pl.program_id(0)

            @pl.when(i < n_lat)
            def _():
                norm(x_ref, g_ref, m_ref, o_ref)

            @pl.when(i >= n_lat)
            def _():
                norm(c_ref, g_ref, m_ref, o_ref)

    in_specs = [pl.BlockSpec((ROW_BLOCK, d), lambda i: (jnp.minimum(i, n_lat - 1), 0))]
    args = [x]
    if ctx is not None:
        in_specs.append(pl.BlockSpec((ROW_BLOCK, d), lambda i: (jnp.maximum(i - n_lat, 0), 0)))
        args.append(ctx)
    in_specs += [pl.BlockSpec((1, d), lambda i: (0, 0)),
                 pl.BlockSpec((None, 2, d), lambda i: (jnp.minimum(i // n_lat, 1), 0, 0))]
    args += [gain, shsc]
    return pl.pallas_call(
        body, name=name, grid=(n_lat + n_ctx,), in_specs=in_specs,
        out_specs=pl.BlockSpec((ROW_BLOCK, d), lambda i: (i, 0)),
        out_shape=_sds(((n_lat + n_ctx) * ROW_BLOCK, d), BF16),
        compiler_params=_params(("parallel",)),
    )(*args)


def _norm_bwd(name, x, d_act, d_act_row0, gain, scale, res=None, aux=None, gate=None):
    rows, d = x.shape
    nb = rows // ROW_BLOCK
    has_res = res is not None
    has_gate = gate is not None

    def body(*refs):
        if has_gate:
            x_ref, da_ref, g_ref, sc_ref, r_ref, aux_ref, gate_ref, dx_ref, dm_ref, sums = refs
        elif has_res:
            x_ref, da_ref, g_ref, sc_ref, r_ref, aux_ref, dx_ref, sums = refs
        else:
            x_ref, da_ref, g_ref, sc_ref, sums = refs
        i = pl.program_id(0)

        @pl.when(i == 0)
        def _():
            sums[...] = jnp.zeros_like(sums)

        xv, da = x_ref[...], da_ref[...]
        rstd = lax.rsqrt(jnp.mean(xv * xv, axis=-1, keepdims=True) + EPS_RMS)
        xh = xv * rstd
        g = g_ref[...]
        dn = da * (1.0 + sc_ref[...])
        sums[0] += _fold8(da)
        sums[1] += _fold8(da * (xh * g))
        sums[2] += _fold8(dn * xh)
        if has_res:
            dxh = dn * g
            dx = rstd * (dxh - xh * jnp.mean(dxh * xh, axis=-1, keepdims=True))
            rv = r_ref[...]
            dx_ref[...] = rv + dx
            sums[3] += _fold8(rv * aux_ref[...])
            if has_gate:
                dm_ref[...] = ((rv + dx) * gate_ref[...]).astype(dm_ref.dtype)

    row = lambda i: (i, 0)
    vec = pl.BlockSpec((1, d), lambda i: (0, 0))
    in_specs = [pl.BlockSpec((ROW_BLOCK, d), row), pl.BlockSpec((ROW_BLOCK, d), lambda i: (i + d_act_row0, 0)), vec, vec]
    args = [x, d_act, gain, scale]
    out_shape = [_sds((4, SUBLANES, d))]
    out_specs = [pl.BlockSpec((4, SUBLANES, d), lambda i: (0, 0, 0))]
    if has_res:
        in_specs += [pl.BlockSpec((ROW_BLOCK, d), row), pl.BlockSpec((ROW_BLOCK, d), row)]
        args += [res, aux]
        if has_gate:
            in_specs.append(vec)
            args.append(gate)
            out_shape = [_sds((rows, d), BF16)] + out_shape
            out_specs = [pl.BlockSpec((ROW_BLOCK, d), row)] + out_specs
        out_shape = [_sds((rows, d))] + out_shape
        out_specs = [pl.BlockSpec((ROW_BLOCK, d), row)] + out_specs
    return pl.pallas_call(
        body, name=name, grid=(nb,), in_specs=in_specs, out_specs=out_specs, out_shape=out_shape,
        compiler_params=_params(("arbitrary",)),
    )(*args)


def _loss_head(h2, target, gain, gate):
    rows, d = h2.shape

    def body(h_ref, t_ref, g_ref, gate_ref, dh_ref, dm_ref, err_ref, dg_ref):
        i = pl.program_id(0)

        @pl.when(i == 0)
        def _():
            err_ref[...] = jnp.zeros_like(err_ref)
            dg_ref[...] = jnp.zeros_like(dg_ref)

        hv = h_ref[...]
        rstd = lax.rsqrt(jnp.mean(hv * hv, axis=-1, keepdims=True) + EPS_RMS)
        xh = hv * rstd
        g = g_ref[...]
        err = xh * g - t_ref[...]
        err_ref[...] += _fold8(err * err)
        dy = err * (1.0 / d)
        dg_ref[...] += _fold8(dy * xh)
        dxh = dy * g
        dh = rstd * (dxh - xh * jnp.mean(dxh * xh, axis=-1, keepdims=True))
        dh_ref[...] = dh
        dm_ref[...] = (dh * gate_ref[...]).astype(dm_ref.dtype)

    row = pl.BlockSpec((ROW_BLOCK, d), lambda i: (i, 0))
    acc = pl.BlockSpec((SUBLANES, d), lambda i: (0, 0))
    vec = pl.BlockSpec((1, d), lambda i: (0, 0))
    return pl.pallas_call(
        body, name="loss_head", grid=(rows // ROW_BLOCK,),
        in_specs=[row, row, vec, vec], out_specs=[row, row, acc, acc],
        out_shape=[_sds((rows, d)), _sds((rows, d), BF16), _sds((SUBLANES, d)), _sds((SUBLANES, d))],
        compiler_params=_params(("arbitrary",)),
    )(h2, target, gain, gate)


def _ada_fwd(cond16, ada_w_loc, ada_b_loc):
    cols = ada_w_loc.shape[1]

    def body(c_ref, w_ref, b_ref, o_ref):
        s = _silu(c_ref[...]).astype(BF16)
        o_ref[...] = _dot(s, w_ref[...].astype(BF16), "nn") + b_ref[...]

    return pl.pallas_call(body, name="ada_fwd", out_shape=_sds((16, cols)), compiler_params=_params())(
        cond16, ada_w_loc, ada_b_loc)


def _ada_bwd(cond16, dmod16, ada_w_loc, c_ctx_row):
    k_, cols = ada_w_loc.shape

    def body(c_ref, dm_ref, w_ref, cc_ref, gw_ref, gc_ref):
        s = _silu(c_ref[...]).astype(BF16)
        dm = dm_ref[...]
        gw_ref[...] = _dot(s, dm.astype(BF16), "tn")
        dmc = jnp.sum(dm[8:16, :], axis=0, keepdims=True)
        dmc8 = jnp.broadcast_to(dmc, (SUBLANES, cols)).astype(BF16)
        ds = _dot(dmc8, w_ref[...].astype(BF16), "nt")
        row = lax.broadcasted_iota(jnp.int32, ds.shape, 0)
        gc_ref[...] = jnp.where(row == 0, ds * _dsilu(cc_ref[...]), 0.0)

    return pl.pallas_call(body, name="ada_bwd", out_shape=[_sds((k_, cols)), _sds((SUBLANES, k_))],
                          compiler_params=_params())(cond16, dmod16, ada_w_loc, c_ctx_row)


def _cmul(a, b):
    return a[0] * b[0] - a[1] * b[1], a[0] * b[1] + a[1] * b[0]


def _disc(lam_re, lam_im, ldt):
    dt = jnp.exp(ldt)
    mag = jnp.exp(lam_re * dt)
    th = lam_im * dt
    a_re, a_im = mag * jnp.cos(th), mag * jnp.sin(th)
    den = lam_re * lam_re + lam_im * lam_im
    n_re = a_re - 1.0
    f_re = (n_re * lam_re + a_im * lam_im) / den
    f_im = (a_im * lam_re - n_re * lam_im) / den
    return dt, mag, th, a_re, a_im, den, n_re, f_re, f_im


def _block_diag_mask(shape):
    row = lax.broadcasted_iota(jnp.int32, shape, 0)
    col = lax.broadcasted_iota(jnp.int32, shape, 1)
    return lax.shift_right_logical(row, 4) == lax.shift_right_logical(col, 6)


TAB_A = 0
TAB_BIG = 1
TAB_SEG = 4
TAB_PW = 5
TAB_ROWS = TAB_PW + STEPS


def _s5_discretise(name, ascending, lam_re, lam_im, ldt, bt_re, bt_im, ct_re, ct_im):
    def write_tables(ref, pw, big, asc, sign):
        row = lax.broadcasted_iota(jnp.int32, (SUBLANES, NSTATE), 0)
        full = lambda v: jnp.broadcast_to(v, (SUBLANES, NSTATE))

        def put(t, p):
            ref[0, t] = full(p[0])
            ref[1, t] = full(sign * p[1])

        put(TAB_A, pw[0])
        for t in range(3):
            put(TAB_BIG + t, big[t])
        seg = [big[0]]
        for _ in range(SEGMENTS - 1):
            seg.append(_cmul(seg[-1], big[0]))
        seg_re = jnp.zeros((SUBLANES, NSTATE), F32)
        seg_im = jnp.zeros((SUBLANES, NSTATE), F32)
        for r in range(SEGMENTS):
            p = seg[r] if asc else seg[SEGMENTS - 1 - r]
            seg_re = jnp.where(row == r, p[0], seg_re)
            seg_im = jnp.where(row == r, sign * p[1], seg_im)
        ref[0, TAB_SEG] = seg_re
        ref[1, TAB_SEG] = seg_im
        for k in range(STEPS):
            put(TAB_PW + k, pw[k])

    def body(lr_ref, li_ref, ldt_ref, br_ref, bi_ref, cr_ref, ci_ref, bb_ref, tab_ref, adj_ref, bm_ref, cm_ref):
        _, _, _, a_re, a_im, _, _, f_re, f_im = _disc(lr_ref[...], li_ref[...], ldt_ref[...])
        bre, bim = br_ref[...], bi_ref[...]
        bb_re = f_re * bre - f_im * bim
        bb_im = f_re * bim + f_im * bre
        bb_ref[0:S5_GROUP, :] = bb_re
        bb_ref[S5_GROUP:2 * S5_GROUP, :] = bb_im
        pw = [(a_re, a_im)]
        for _ in range(STEPS - 1):
            pw.append(_cmul(pw[-1], (a_re, a_im)))
        big = [pw[STEPS - 1]]
        for _ in range(2):
            big.append(_cmul(big[-1], big[-1]))
        write_tables(tab_ref, pw, big, ascending, 1.0)
        write_tables(adj_ref, pw, big, not ascending, -1.0)
        half = NSTATE // S5_BLOCKS
        mask = _block_diag_mask((S5_BLOCK_WIDTH, half))
        tile = lambda v: jnp.broadcast_to(v[None], (S5_BLOCK_WIDTH // S5_GROUP, S5_GROUP, half)).reshape(S5_BLOCK_WIDTH, half)
        for c in range(S5_BLOCKS):
            cols = slice(c * half, (c + 1) * half)
            rows = slice(c * S5_BLOCK_WIDTH, (c + 1) * S5_BLOCK_WIDTH)
            bm_ref[c, :, 0:half] = jnp.where(mask, tile(bb_re[:, cols]), 0.0).astype(BF16)
            bm_ref[c, :, half:2 * half] = jnp.where(mask, tile(bb_im[:, cols]), 0.0).astype(BF16)
            cm_ref[c, :, 0:half] = jnp.where(mask, cr_ref[rows, :], 0.0).astype(BF16)
            cm_ref[c, :, half:2 * half] = jnp.where(mask, -ci_ref[rows, :], 0.0).astype(BF16)

    blocked = _sds((S5_BLOCKS, S5_BLOCK_WIDTH, 2 * NSTATE // S5_BLOCKS), BF16)
    return pl.pallas_call(
        body, name=name,
        out_shape=[_sds((2 * S5_GROUP, NSTATE)), _sds((2, TAB_ROWS, SUBLANES, NSTATE)),
                   _sds((2, TAB_ROWS, SUBLANES, NSTATE)), blocked, blocked],
        compiler_params=_params(),
    )(lam_re, lam_im, ldt, bt_re, bt_im, ct_re, ct_im)


def _s5_discretise_bwd(name, lam_re, lam_im, ldt, bt_re, bt_im, d_abar8, d_bbar):
    def body(lr_ref, li_ref, ldt_ref, br_ref, bi_ref, da_ref, db_ref, dl_ref, dbt_ref):
        lam_re, lam_im = lr_ref[...], li_ref[...]
        dt, mag, _, a_re, a_im, den, n_re, f_re, f_im = _disc(lam_re, lam_im, ldt_ref[...])
        bre, bim = br_ref[...], bi_ref[...]
        dbr, dbi = db_ref[0:S5_GROUP, :], db_ref[S5_GROUP:2 * S5_GROUP, :]
        dbt_ref[0:S5_GROUP, :] = f_re * dbr + f_im * dbi
        dbt_ref[S5_GROUP:2 * S5_GROUP, :] = f_re * dbi - f_im * dbr
        df_re = jnp.sum(bre * dbr + bim * dbi, axis=0, keepdims=True)
        df_im = jnp.sum(bre * dbi - bim * dbr, axis=0, keepdims=True)
        da = da_ref[...]
        da_re = jnp.sum(da[:, 0:NSTATE], axis=0, keepdims=True)
        da_im = jnp.sum(da[:, NSTATE:2 * NSTATE], axis=0, keepdims=True)
        da_re = da_re + (df_re * lam_re - df_im * lam_im) / den
        da_im = da_im + (df_re * lam_im + df_im * lam_re) / den
        ff = (f_re * df_re + f_im * df_im) * 2.0 / den
        d_lr = (df_re * n_re + df_im * a_im) / den - ff * lam_re
        d_li = (df_re * a_im - df_im * n_re) / den - ff * lam_im
        d_mag = (da_re * a_re + da_im * a_im) / mag
        d_th = da_im * a_re - da_re * a_im
        d_lr = d_lr + d_mag * mag * dt
        d_li = d_li + d_th * dt
        d_ldt = (d_mag * mag * lam_re + d_th * lam_im) * dt
        row = lax.broadcasted_iota(jnp.int32, (SUBLANES, NSTATE), 0)
        dl_ref[...] = jnp.where(row == 0, d_lr, jnp.where(row == 1, d_li, jnp.where(row == 2, d_ldt, 0.0)))

    return pl.pallas_call(
        body, name=name, out_shape=[_sds((SUBLANES, NSTATE)), _sds((2 * S5_GROUP, NSTATE))],
        compiler_params=_params(),
    )(lam_re, lam_im, ldt, bt_re, bt_im, d_abar8, d_bbar)


def _segment_permutation():
    rho = jnp.arange(ROW_BLOCK)
    src = STEPS * (rho % SEGMENTS) + rho // SEGMENTS
    return (src[:, None] == jnp.arange(ROW_BLOCK)[None, :]).astype(BF16)


def _permute_rows(perm_ref, v):
    return _dot(perm_ref[...], v, "nn").astype(BF16)


def _unpermute_rows(perm_t_ref, v):
    hi = v.astype(BF16)
    lo = (v - hi.astype(F32)).astype(BF16)
    return _dot(perm_t_ref[...], hi, "nn") + _dot(perm_t_ref[...], lo, "nn")


def _scan_chunk(x_ref, out_ref, tab_ref, carry_re, carry_im, ascending, pair_ref=None, acc_ref=None):
    w = SCAN_LANES
    half = NSTATE // S5_BLOCKS
    row = lax.broadcasted_iota(jnp.int32, (SUBLANES, w), 0)
    last = (SEGMENTS - 1) if ascending else 0

    def from_previous_segment(v, k, fill):
        if ascending:
            return jnp.where(row >= k, pltpu.roll(v, k, 0), fill)
        return jnp.where(row < SEGMENTS - k, pltpu.roll(v, SEGMENTS - k, 0), fill)

    def tile_rows(k):
        return pl.ds(pl.multiple_of((k if ascending else STEPS - 1 - k) * SUBLANES, SUBLANES), SUBLANES)

    for j in range(NSTATE // w):
        n_l = pl.ds(j * w, w)
        lane0 = (j * w // half) * 2 * half + (j * w) % half
        re_l, im_l = pl.ds(lane0, w), pl.ds(lane0 + half, w)
        tab = lambda t, n_l=n_l: (tab_ref[0, t, :, n_l], tab_ref[1, t, :, n_l])
        a_re, a_im = tab(TAB_A)

        def local_step(k, h):
            rs = tile_rows(k)
            h_re = a_re * h[0] - a_im * h[1] + x_ref[rs, re_l]
            h_im = a_re * h[1] + a_im * h[0] + x_ref[rs, im_l]
            out_ref[rs, re_l] = h_re
            out_ref[rs, im_l] = h_im
            return h_re, h_im

        zero = jnp.zeros((SUBLANES, w), F32)
        end_re, end_im = lax.fori_loop(0, STEPS, local_step, (zero, zero))
        for t, k in ((TAB_BIG, 1), (TAB_BIG + 1, 2), (TAB_BIG + 2, 4)):
            p_re, p_im = tab(t)
            s_re, s_im = from_previous_segment(end_re, k, 0.0), from_previous_segment(end_im, k, 0.0)
            end_re, end_im = end_re + (p_re * s_re - p_im * s_im), end_im + (p_re * s_im + p_im * s_re)
        c0_re, c0_im = carry_re[:, n_l], carry_im[:, n_l]
        p_re, p_im = tab(TAB_SEG)
        end_re = end_re + (p_re * c0_re - p_im * c0_im)
        end_im = end_im + (p_re * c0_im + p_im * c0_re)
        carry_re[:, n_l] = jnp.broadcast_to(end_re[last:last + 1, :], end_re.shape)
        carry_im[:, n_l] = jnp.broadcast_to(end_im[last:last + 1, :], end_im.shape)
        in_re = from_previous_segment(end_re, 1, c0_re)
        in_im = from_previous_segment(end_im, 1, c0_im)

        def carry_step(k, st):
            rs = tile_rows(k)
            p_re, p_im = tab_ref[0, TAB_PW + k, :, n_l], tab_ref[1, TAB_PW + k, :, n_l]
            o_re = out_ref[rs, re_l] + (p_re * in_re - p_im * in_im)
            o_im = out_ref[rs, im_l] + (p_re * in_im + p_im * in_re)
            out_ref[rs, re_l] = o_re
            out_ref[rs, im_l] = o_im
            if pair_ref is None:
                return st
            s_re, s_im = pair_ref[rs, re_l], pair_ref[rs, im_l]
            return (o_re, o_im, st[2] + (st[0] * s_re + st[1] * s_im), st[3] + (st[1] * s_re - st[0] * s_im))

        if pair_ref is None:
            lax.fori_loop(0, STEPS, carry_step, 0)
        else:
            fin = lax.fori_loop(0, STEPS, carry_step, (in_re, in_im, zero, zero))
            acc_ref[:, n_l] += fin[2]
            acc_ref[:, pl.ds(NSTATE + j * w, w)] += fin[3]


def _scan_block_index(i, n_lat, ctx_first_then_ascending):
    if ctx_first_then_ascending:
        return jnp.where(i == 0, n_lat, i - 1)
    return jnp.where(i == 0, n_lat, n_lat - i)


def _full_spec(shape):
    return pl.BlockSpec(shape, lambda i: (0,) * len(shape))


_S5_BLOCKED = (S5_BLOCKS, S5_BLOCK_WIDTH, 2 * NSTATE // S5_BLOCKS)
_S5_TABLES = (2, TAB_ROWS, SUBLANES, NSTATE)


def _s5_scan_fwd(name, ascending, z_all, bmat, cmat, tab, perm, perm_t):
    rows = z_all.shape[0]
    nb = rows // ROW_BLOCK
    n_lat = nb - 1
    bw, sw = S5_BLOCK_WIDTH, 2 * NSTATE // S5_BLOCKS

    def body(u_ref, bm_ref, cm_ref, tab_ref, p_ref, pt_ref, s_ref, y_ref, bu, yp, carry_re, carry_im):
        @pl.when(pl.program_id(0) == 0)
        def _():
            carry_re[...] = jnp.zeros_like(carry_re)
            carry_im[...] = jnp.zeros_like(carry_im)

        up = _permute_rows(p_ref, u_ref[...].astype(BF16))
        for c in range(S5_BLOCKS):
            bu[:, c * sw:(c + 1) * sw] = _dot(up[:, c * bw:(c + 1) * bw], bm_ref[c], "nn")
        _scan_chunk(bu, s_ref, tab_ref, carry_re, carry_im, ascending)
        for c in range(S5_BLOCKS):
            yp[:, c * bw:(c + 1) * bw] = _dot(s_ref[:, c * sw:(c + 1) * sw].astype(BF16), cm_ref[c], "nt")
        y_ref[...] = _unpermute_rows(pt_ref, yp[...])

    blk = lambda i: (_scan_block_index(i, n_lat, ascending), 0)
    return pl.pallas_call(
        body, name=name, grid=(nb,),
        in_specs=[pl.BlockSpec((ROW_BLOCK, S5_WIDTH), blk), _full_spec(_S5_BLOCKED), _full_spec(_S5_BLOCKED),
                  _full_spec(_S5_TABLES), _full_spec((ROW_BLOCK, ROW_BLOCK)), _full_spec((ROW_BLOCK, ROW_BLOCK))],
        out_specs=[pl.BlockSpec((ROW_BLOCK, 2 * NSTATE), blk), pl.BlockSpec((ROW_BLOCK, S5_WIDTH), blk)],
        out_shape=[_sds((rows, 2 * NSTATE)), _sds((rows, S5_WIDTH))],
        scratch_shapes=[pltpu.VMEM((ROW_BLOCK, 2 * NSTATE), F32), pltpu.VMEM((ROW_BLOCK, S5_WIDTH), F32),
                        pltpu.VMEM((SUBLANES, NSTATE), F32), pltpu.VMEM((SUBLANES, NSTATE), F32)],
        compiler_params=_params(("arbitrary",)),
    )(z_all, bmat, cmat, tab, perm, perm_t)


def _s5_scan_bwd(name, ascending, dy, z_all, states, bmat, cmat, adj, perm, perm_t):
    rows = states.shape[0]
    nb = rows // ROW_BLOCK
    n_lat = nb - 1
    bw, sw = S5_BLOCK_WIDTH, 2 * NSTATE // S5_BLOCKS

    def block_index(i):
        if ascending:
            return jnp.where(i == nb - 1, n_lat, n_lat - 1 - i)
        return jnp.where(i == nb - 1, n_lat, i)

    def body(dy_ref, u_ref, s_ref, bm_ref, cm_ref, adj_ref, p_ref, pt_ref, du_ref, db_ref, dc_ref, da_ref,
             g, dup, carry_re, carry_im):
        i = pl.program_id(0)

        @pl.when(i == 0)
        def _():
            carry_re[...] = jnp.zeros_like(carry_re)
            carry_im[...] = jnp.zeros_like(carry_im)
            da_ref[...] = jnp.zeros_like(da_ref)
            db_ref[...] = jnp.zeros_like(db_ref)
            dc_ref[...] = jnp.zeros_like(dc_ref)

        @pl.when(i < nb - 1)
        def _():
            dyp = _permute_rows(p_ref, dy_ref[...].astype(BF16))
            for c in range(S5_BLOCKS):
                g[:, c * sw:(c + 1) * sw] = _dot(dyp[:, c * bw:(c + 1) * bw], cm_ref[c], "nn")
                dc_ref[c] += _dot(dyp[:, c * bw:(c + 1) * bw], s_ref[:, c * sw:(c + 1) * sw].astype(BF16), "tn")

        @pl.when(i == nb - 1)
        def _():
            g[...] = jnp.zeros_like(g)

        _scan_chunk(g, g, adj_ref, carry_re, carry_im, not ascending, pair_ref=s_ref, acc_ref=da_ref)
        up = _permute_rows(p_ref, u_ref[...].astype(BF16))
        for c in range(S5_BLOCKS):
            gc = g[:, c * sw:(c + 1) * sw].astype(BF16)
            dup[:, c * bw:(c + 1) * bw] = _dot(gc, bm_ref[c], "nt")
            db_ref[c] += _dot(up[:, c * bw:(c + 1) * bw], gc, "tn")
        du_ref[...] = _unpermute_rows(pt_ref, dup[...])

    blk = lambda i: (block_index(i), 0)
    return pl.pallas_call(
        body, name=name, grid=(nb,),
        in_specs=[pl.BlockSpec((ROW_BLOCK, S5_WIDTH), lambda i: (jnp.minimum(block_index(i), n_lat - 1), 0)),
                  pl.BlockSpec((ROW_BLOCK, S5_WIDTH), blk), pl.BlockSpec((ROW_BLOCK, 2 * NSTATE), blk),
                  _full_spec(_S5_BLOCKED), _full_spec(_S5_BLOCKED), _full_spec(_S5_TABLES),
                  _full_spec((ROW_BLOCK, ROW_BLOCK)), _full_spec((ROW_BLOCK, ROW_BLOCK))],
        out_specs=[pl.BlockSpec((ROW_BLOCK, S5_WIDTH), blk), _full_spec(_S5_BLOCKED), _full_spec(_S5_BLOCKED),
                   _full_spec((SUBLANES, 2 * NSTATE))],
        out_shape=[_sds((rows, S5_WIDTH)), _sds(_S5_BLOCKED), _sds(_S5_BLOCKED), _sds((SUBLANES, 2 * NSTATE))],
        scratch_shapes=[pltpu.VMEM((ROW_BLOCK, 2 * NSTATE), F32), pltpu.VMEM((ROW_BLOCK, S5_WIDTH), F32),
                        pltpu.VMEM((SUBLANES, NSTATE), F32), pltpu.VMEM((SUBLANES, NSTATE), F32)],
        compiler_params=_params(("arbitrary",)),
    )(dy, z_all, states, bmat, cmat, adj, perm, perm_t)


def _glu_fwd(z_all, y0, y1, d_skip, w_glu, n_rows):
    def body(u_ref, y0_ref, y1_ref, d_ref, w_ref, o_ref):
        y = d_ref[...] * u_ref[...] + y0_ref[...] + y1_ref[...]
        g = _gelu(y)
        t = _dot(g.astype(BF16), w_ref[...], "nn")
        o_ref[...] = (g * _sigmoid(t)).astype(o_ref.dtype)

    row = pl.BlockSpec((ROW_BLOCK, S5_WIDTH), lambda i: (i, 0))
    return pl.pallas_call(
        body, name="glu_fwd", grid=(n_rows // ROW_BLOCK,),
        in_specs=[row, row, row, pl.BlockSpec((1, S5_WIDTH), lambda i: (0, 0)),
                  pl.BlockSpec((S5_WIDTH, S5_WIDTH), lambda i: (0, 0))],
        out_specs=row, out_shape=_sds((n_rows, S5_WIDTH), BF16), compiler_params=_params(("parallel",)),
    )(z_all, y0, y1, d_skip, w_glu)


def _glu_bwd(d_ycat, z_all, y0, y1, d_skip, w_glu, n_rows):
    def body(do_ref, u_ref, y0_ref, y1_ref, d_ref, w_ref, dy_ref, dw_ref, dd_ref):
        @pl.when(pl.program_id(0) == 0)
        def _():
            dw_ref[...] = jnp.zeros_like(dw_ref)
            dd_ref[...] = jnp.zeros_like(dd_ref)

        u = u_ref[...]
        y = d_ref[...] * u + y0_ref[...] + y1_ref[...]
        g = _gelu(y)
        gb = g.astype(BF16)
        w = w_ref[...]
        sg = _sigmoid(_dot(gb, w, "nn"))
        do = do_ref[...]
        dt = do * g * sg * (1.0 - sg)
        dtb = dt.astype(BF16)
        dg = do * sg + _dot(dtb, w, "nt")
        dy = dg * _dgelu(y)
        dy_ref[...] = dy
        dw_ref[...] += _dot(gb, dtb, "tn")
        dd_ref[...] += _fold8(dy * u)

    row = pl.BlockSpec((ROW_BLOCK, S5_WIDTH), lambda i: (i, 0))
    sq = pl.BlockSpec((S5_WIDTH, S5_WIDTH), lambda i: (0, 0))
    return pl.pallas_call(
        body, name="glu_bwd", grid=(n_rows // ROW_BLOCK,),
        in_specs=[row, row, row, row, pl.BlockSpec((1, S5_WIDTH), lambda i: (0, 0)), sq],
        out_specs=[row, sq, pl.BlockSpec((SUBLANES, S5_WIDTH), lambda i: (0, 0))],
        out_shape=[_sds((n_rows, S5_WIDTH)), _sds((S5_WIDTH, S5_WIDTH)), _sds((SUBLANES, S5_WIDTH))],
        compiler_params=_params(("arbitrary",)),
    )(d_ycat, z_all, y0, y1, d_skip, w_glu)


CONV_HALF = CONV_K // 2


def _conv_block(n_rows):
    blk = min(1024, n_rows)
    assert blk >= CONV_HALF * GRID_W and n_rows % blk == 0
    return blk


def _conv_gate(z_all, n_rows):
    blk = _conv_block(n_rows)
    nb = n_rows // blk

    def body(v_ref, g_ref, o_ref):
        i = pl.program_id(0)
        inside = jnp.logical_and(i >= 1, i <= nb)

        @pl.when(inside)
        def _():
            o_ref[...] = v_ref[...] * _sigmoid(g_ref[...])

        @pl.when(jnp.logical_not(inside))
        def _():
            o_ref[...] = jnp.zeros_like(o_ref)

    src = lambda col: pl.BlockSpec((blk, CONV_WIDTH), lambda i: (jnp.clip(i - 1, 0, nb - 1), col))
    return pl.pallas_call(
        body, name="conv_gate", grid=(nb + 2,), in_specs=[src(1), src(2)],
        out_specs=pl.BlockSpec((blk, CONV_WIDTH), lambda i: (i, 0)),
        out_shape=_sds(((nb + 2) * blk, CONV_WIDTH)), compiler_params=_params(("parallel",)),
    )(z_all, z_all)


def _load_window(pad_ref, win, sem, blk):
    start = pl.multiple_of(pl.program_id(0) * blk, blk)
    copy = pltpu.make_async_copy(pad_ref.at[pl.ds(start, 3 * blk), :], win, sem)
    copy.start()
    copy.wait()


def _conv_fwd(hh_pad, w, b, ln_g, ln_b, n_rows):
    blk = _conv_block(n_rows)

    def body(hh_ref, w_ref, b_ref, g_ref, lb_ref, hc_ref, y_ref, win, sem):
        _load_window(hh_ref, win, sem, blk)

        def tile(t, _):
            r0 = pl.multiple_of(t * CONV_ROWS, CONV_ROWS)
            acc = jnp.zeros((CONV_ROWS, CONV_WIDTH), F32)
            for k in range(CONV_K):
                acc = acc + w_ref[k:k + 1, :] * win[pl.ds(r0 + blk + (k - CONV_HALF) * GRID_W, CONV_ROWS), :]
            hc = acc + b_ref[...]
            hc_ref[pl.ds(r0, CONV_ROWS), :] = hc
            mu = jnp.mean(hc, axis=-1, keepdims=True)
            xc = hc - mu
            ln = xc * lax.rsqrt(jnp.mean(xc * xc, axis=-1, keepdims=True) + EPS_LN) * g_ref[...] + lb_ref[...]
            y_ref[pl.ds(r0, CONV_ROWS), :] = _silu(ln).astype(y_ref.dtype)
            return 0

        lax.fori_loop(0, blk // CONV_ROWS, tile, 0)

    vec = pl.BlockSpec((1, CONV_WIDTH), lambda i: (0, 0))
    row = pl.BlockSpec((blk, CONV_WIDTH), lambda i: (i, 0))
    return pl.pallas_call(
        body, name="conv_fwd", grid=(n_rows // blk,),
        in_specs=[ANY, pl.BlockSpec((CONV_K, CONV_WIDTH), lambda i: (0, 0)), vec, vec, vec],
        out_specs=[row, row], out_shape=[_sds((n_rows, CONV_WIDTH)), _sds((n_rows, CONV_WIDTH), BF16)],
        scratch_shapes=[pltpu.VMEM((3 * blk, CONV_WIDTH), F32), pltpu.SemaphoreType.DMA],
        compiler_params=_params(("arbitrary",)),
    )(hh_pad, w, b, ln_g, ln_b)


def _conv_bwd_norm(d_ycat, hc, ln_g, ln_b, n_rows):
    blk = _conv_block(n_rows)
    nb = n_rows // blk

    def body(dy_ref, hc_ref, g_ref, lb_ref, o_ref, sums):
        i = pl.program_id(0)

        @pl.when(i == 0)
        def _():
            sums[...] = jnp.zeros_like(sums)

        inside = jnp.logical_and(i >= 1, i <= nb)

        @pl.when(inside)
        def _():
            hcv = hc_ref[...]
            mu = jnp.mean(hcv, axis=-1, keepdims=True)
            xc = hcv - mu
            rstd = lax.rsqrt(jnp.mean(xc * xc, axis=-1, keepdims=True) + EPS_LN)
            xh = xc * rstd
            g = g_ref[...]
            dln = dy_ref[...] * _dsilu(xh * g + lb_ref[...])
            dxh = dln * g
            dhc = rstd * (dxh - jnp.mean(dxh, axis=-1, keepdims=True) - xh * jnp.mean(dxh * xh, axis=-1, keepdims=True))
            o_ref[...] = dhc
            sums[0] += _fold8(dhc)
            sums[1] += _fold8(dln * xh)
            sums[2] += _fold8(dln)

        @pl.when(jnp.logical_not(inside))
        def _():
            o_ref[...] = jnp.zeros_like(o_ref)

    vec = pl.BlockSpec((1, CONV_WIDTH), lambda i: (0, 0))
    return pl.pallas_call(
        body, name="conv_bwd_norm", grid=(nb + 2,),
        in_specs=[pl.BlockSpec((blk, CONV_WIDTH), lambda i: (jnp.clip(i - 1, 0, nb - 1), 1)),
                  pl.BlockSpec((blk, CONV_WIDTH), lambda i: (jnp.clip(i - 1, 0, nb - 1), 0)), vec, vec],
        out_specs=[pl.BlockSpec((blk, CONV_WIDTH), lambda i: (i, 0)),
                   pl.BlockSpec((3, SUBLANES, CONV_WIDTH), lambda i: (0, 0, 0))],
        out_shape=[_sds(((nb + 2) * blk, CONV_WIDTH)), _sds((3, SUBLANES, CONV_WIDTH))],
        compiler_params=_params(("arbitrary",)),
    )(d_ycat, hc, ln_g, ln_b)


def _conv_bwd_taps(dhc_pad, hh_pad, z_all, w, n_rows):
    blk = _conv_block(n_rows)

    def body(dhc_ref, hh_ref, v_ref, g_ref, w_ref, dv_ref, dg_ref, dw_ref, dwin, hwin, sems):
        @pl.when(pl.program_id(0) == 0)
        def _():
            dw_ref[...] = jnp.zeros_like(dw_ref)

        _load_window(dhc_ref, dwin, sems.at[0], blk)
        _load_window(hh_ref, hwin, sems.at[1], blk)

        def tile(t, _):
            r0 = pl.multiple_of(t * CONV_ROWS, CONV_ROWS)
            dh = dwin[pl.ds(r0 + blk, CONV_ROWS), :]
            acc = jnp.zeros((CONV_ROWS, CONV_WIDTH), F32)
            for k in range(CONV_K):
                off = (k - CONV_HALF) * GRID_W
                acc = acc + w_ref[k:k + 1, :] * dwin[pl.ds(r0 + blk - off, CONV_ROWS), :]
                dw_ref[k] += _fold8(dh * hwin[pl.ds(r0 + blk + off, CONV_ROWS), :])
            rs = pl.ds(r0, CONV_ROWS)
            sg = _sigmoid(g_ref[rs, :])
            vv = v_ref[rs, :]
            dv_ref[rs, :] = acc * sg
            dg_ref[rs, :] = acc * vv * sg * (1.0 - sg)
            return 0

        lax.fori_loop(0, blk // CONV_ROWS, tile, 0)

    row = pl.BlockSpec((blk, CONV_WIDTH), lambda i: (i, 0))
    return pl.pallas_call(
        body, name="conv_bwd_taps", grid=(n_rows // blk,),
        in_specs=[ANY, ANY,
            pl.BlockSpec((blk, CONV_WIDTH), lambda i: (i, 1)), pl.BlockSpec((blk, CONV_WIDTH), lambda i: (i, 2)),
            pl.BlockSpec((CONV_K, CONV_WIDTH), lambda i: (0, 0))],
        out_specs=[row, row, pl.BlockSpec((CONV_K, SUBLANES, CONV_WIDTH), lambda i: (0, 0, 0))],
        out_shape=[_sds((n_rows, CONV_WIDTH)), _sds((n_rows, CONV_WIDTH)), _sds((CONV_K, SUBLANES, CONV_WIDTH))],
        scratch_shapes=[pltpu.VMEM((3 * blk, CONV_WIDTH), F32), pltpu.VMEM((3 * blk, CONV_WIDTH), F32),
                        pltpu.SemaphoreType.DMA((2,))],
        compiler_params=_params(("arbitrary",)),
    )(dhc_pad, hh_pad, z_all, z_all, w)


def _dz_assemble(du0, du1, dy, d_skip, dv, dgate, n_lat):
    rows = du0.shape[0]
    nb = rows // ROW_BLOCK

    w = S5_WIDTH

    def body(a_ref, b_ref, dy_ref, d_ref, dv_ref, dg_ref, o_ref):
        lat = pl.program_id(0) < n_lat

        @pl.when(lat)
        def _():
            o_ref[:, 0:w] = (a_ref[...] + b_ref[...] + dy_ref[...] * d_ref[...]).astype(o_ref.dtype)
            o_ref[:, w:2 * w] = dv_ref[...].astype(o_ref.dtype)
            o_ref[:, 2 * w:3 * w] = dg_ref[...].astype(o_ref.dtype)

        @pl.when(jnp.logical_not(lat))
        def _():
            o_ref[:, 0:w] = (a_ref[...] + b_ref[...]).astype(o_ref.dtype)
            o_ref[:, w:3 * w] = jnp.zeros((ROW_BLOCK, 2 * w), o_ref.dtype)

    all_rows = pl.BlockSpec((ROW_BLOCK, w), lambda i: (i, 0))
    lat_rows = pl.BlockSpec((ROW_BLOCK, w), lambda i: (jnp.minimum(i, n_lat - 1), 0))
    return pl.pallas_call(
        body, name="dz_assemble", grid=(nb,),
        in_specs=[all_rows, all_rows, lat_rows, pl.BlockSpec((1, w), lambda i: (0, 0)), lat_rows, lat_rows],
        out_specs=pl.BlockSpec((ROW_BLOCK, IN_COLS), lambda i: (i, 0)),
        out_shape=_sds((rows, IN_COLS), BF16), compiler_params=_params(("parallel",)),
    )(du0, du1, dy, d_skip, dv, dgate)


def _sum_parts(parts):
    _, r, c = parts.shape

    def body(p_ref, o_ref):
        acc = p_ref[0]
        for q in range(1, NDEV):
            acc = acc + p_ref[q]
        o_ref[...] = acc

    return pl.pallas_call(body, name="sum_parts", out_shape=_sds((r, c)), compiler_params=_params())(parts)


def _row_tile(r, c):
    best = r
    for t in (1024, 512, 256, 128, 64, 32, 16, 8):
        if r % t == 0 and t * c <= 128 * 1024:
            return t
    return best


def _adamw(name, w, gparts, m, v):
    r, c = w.shape
    np_ = gparts.shape[0]
    tr = _row_tile(r, c)

    def body(w_ref, g_ref, m_ref, v_ref, go_ref, d_ref, mo_ref, vo_ref):
        g = g_ref[0].astype(F32)
        for q in range(1, np_):
            g = g + g_ref[q].astype(F32)
        m2 = ADAM_B1 * m_ref[...] + (1.0 - ADAM_B1) * g
        v2 = ADAM_B2 * v_ref[...] + (1.0 - ADAM_B2) * jnp.square(g)
        m_hat = m2 / (1.0 - ADAM_B1 ** ADAM_STEP)
        v_hat = v2 / (1.0 - ADAM_B2 ** ADAM_STEP)
        go_ref[...] = g
        d_ref[...] = -ADAM_LR * (m_hat / (jnp.sqrt(v_hat) + ADAM_EPS) + ADAM_WD * w_ref[...])
        mo_ref[...] = m2
        vo_ref[...] = v2

    row = pl.BlockSpec((tr, c), lambda i: (i, 0))
    return pl.pallas_call(
        body, name=name, grid=(r // tr,),
        in_specs=[row, pl.BlockSpec((np_, tr, c), lambda i: (0, i, 0)), row, row],
        out_specs=[row] * 4, out_shape=[_sds((r, c))] * 4, compiler_params=_params(("parallel",)),
    )(w, gparts, m, v)


SMALL = ["c_ctx", "ada_b", "norm1_g", "s5_lam_re", "s5_lam_im", "s5_log_dt", "s5_b_re", "s5_b_im", "s5_c_re",
         "s5_c_im", "s5_d", "conv_b", "conv_ln_g", "conv_ln_b", "norm2_g", "final_g"]


def _pack_small(parts):
    flat = jnp.concatenate([p.reshape(-1).astype(F32) for p in parts])
    return jnp.pad(flat, (0, SMALL_ROWS * D_MODEL - flat.shape[0])).reshape(SMALL_ROWS, D_MODEL)


def _unpack_small(packed, like):
    flat = packed.reshape(-1)
    out, off = [], 0
    for ref in like:
        out.append(flat[off:off + ref.size].reshape(ref.shape))
        off += ref.size
    return out


def kernel(x, c, ctx, c_ctx, ada_w, ada_b, norm1_g, w_in, s5_lam_re, s5_lam_im, s5_log_dt, s5_b_re, s5_b_im, s5_c_re, s5_c_im, s5_d, s5_w_glu, conv_w, conv_b, conv_ln_g, conv_ln_b, w_out, norm2_g, mlp_w1, mlp_w2, final_g, loss_target, m_c_ctx, m_ada_w, m_ada_b, m_norm1_g, m_w_in, m_s5_lam_re, m_s5_lam_im, m_s5_log_dt, m_s5_b_re, m_s5_b_im, m_s5_c_re, m_s5_c_im, m_s5_d, m_s5_w_glu, m_conv_w, m_conv_b, m_conv_ln_g, m_conv_ln_b, m_w_out, m_norm2_g, m_mlp_w1, m_mlp_w2, m_final_g, v_c_ctx, v_ada_w, v_ada_b, v_norm1_g, v_w_in, v_s5_lam_re, v_s5_lam_im, v_s5_log_dt, v_s5_b_re, v_s5_b_im, v_s5_c_re, v_s5_c_im, v_s5_d, v_s5_w_glu, v_conv_w, v_conv_b, v_conv_ln_g, v_conv_ln_b, v_w_out, v_norm2_g, v_mlp_w1, v_mlp_w2, v_final_g):
    weights = dict(c_ctx=c_ctx, ada_w=ada_w, ada_b=ada_b, norm1_g=norm1_g, w_in=w_in, s5_lam_re=s5_lam_re, s5_lam_im=s5_lam_im, s5_log_dt=s5_log_dt, s5_b_re=s5_b_re, s5_b_im=s5_b_im, s5_c_re=s5_c_re, s5_c_im=s5_c_im, s5_d=s5_d, s5_w_glu=s5_w_glu, conv_w=conv_w, conv_b=conv_b, conv_ln_g=conv_ln_g, conv_ln_b=conv_ln_b, w_out=w_out, norm2_g=norm2_g, mlp_w1=mlp_w1, mlp_w2=mlp_w2, final_g=final_g)
    mom1 = dict(c_ctx=m_c_ctx, ada_w=m_ada_w, ada_b=m_ada_b, norm1_g=m_norm1_g, w_in=m_w_in, s5_lam_re=m_s5_lam_re, s5_lam_im=m_s5_lam_im, s5_log_dt=m_s5_log_dt, s5_b_re=m_s5_b_re, s5_b_im=m_s5_b_im, s5_c_re=m_s5_c_re, s5_c_im=m_s5_c_im, s5_d=m_s5_d, s5_w_glu=m_s5_w_glu, conv_w=m_conv_w, conv_b=m_conv_b, conv_ln_g=m_conv_ln_g, conv_ln_b=m_conv_ln_b, w_out=m_w_out, norm2_g=m_norm2_g, mlp_w1=m_mlp_w1, mlp_w2=m_mlp_w2, final_g=m_final_g)
    mom2 = dict(c_ctx=v_c_ctx, ada_w=v_ada_w, ada_b=v_ada_b, norm1_g=v_norm1_g, w_in=v_w_in, s5_lam_re=v_s5_lam_re, s5_lam_im=v_s5_lam_im, s5_log_dt=v_s5_log_dt, s5_b_re=v_s5_b_re, s5_b_im=v_s5_b_im, s5_c_re=v_s5_c_re, s5_c_im=v_s5_c_im, s5_d=v_s5_d, s5_w_glu=v_s5_w_glu, conv_w=v_conv_w, conv_b=v_conv_b, conv_ln_g=v_conv_ln_g, conv_ln_b=v_conv_ln_b, w_out=v_w_out, norm2_g=v_norm2_g, mlp_w1=v_mlp_w1, mlp_w2=v_mlp_w2, final_g=v_final_g)
    order = list(weights)

    me = 4 * lax.axis_index("x") + 2 * lax.axis_index("y") + lax.axis_index("c")
    xs, cs, tgt = x[0], ctx[0], loss_target[0]
    n_lat_rows, n_ctx_rows = xs.shape[0], cs.shape[0]
    n_rows = n_lat_rows + n_ctx_rows
    n_lat = n_lat_rows // ROW_BLOCK
    ada_cols = ada_w.shape[2]

    (w_in_g, c_all), _ = _exchange("gather_w_in", [w_in[0].astype(BF16), c], [True] * 2)
    w_in_full = jnp.transpose(w_in_g, (1, 0, 2)).reshape(D_MODEL, IN_COLS)
    c_all = c_all.reshape(NDEV, D_MODEL)

    cond_fwd = jnp.concatenate([c_all, c_ctx[None], jnp.zeros((7, D_MODEL), F32)])
    ada_b_loc = lax.dynamic_slice(ada_b, (0, me * ada_cols), (1, ada_cols))
    (mod_g,), mod_token = _exchange("gather_mod", [_ada_fwd(cond_fwd, ada_w[0], ada_b_loc)], [True])
    mixer_w = [s5_w_glu[0].astype(BF16), conv_w[0] + mod_token[0:1, 0:1], w_out[0].astype(BF16)]
    mixer_send, mixer_recv, mixer_src, mixer_land, mixer_token = _exchange_start("gather_mixer_start", mixer_w, [True] * 3)
    mlp_w = [mlp_w1[0].astype(BF16), mlp_w2[0].astype(BF16) + mixer_token[0:1, 0:1].astype(BF16)]
    mlpw_send, mlpw_recv, mlpw_src, mlpw_land, mlpw_token = _exchange_start("gather_mlp_start", mlp_w, [True] * 2)
    mod_rows = jnp.transpose(mod_g, (1, 0, 2)).reshape(16, 6 * D_MODEL) + mlpw_token[0:1, 0:1]
    mod = lax.dynamic_slice(mod_rows, (me, 0), (1, 6 * D_MODEL)).reshape(6, D_MODEL)
    modc = mod_rows[8, :2 * D_MODEL].reshape(2, D_MODEL)
    sh1, sc1, g1, sh2, sc2, g2 = [mod[i:i + 1] for i in range(6)]

    a_all = _prenorm("prenorm1", xs, cs, norm1_g, jnp.stack([mod[0:2], modc]))
    (z_all,) = _matmul("in_proj", a_all, w_in_full, "nn", (n_rows, IN_COLS, D_MODEL), (ROW_BLOCK, IN_COLS, D_MODEL),
                       [((n_rows, IN_COLS), F32)])

    lam_re, lam_im = s5_lam_re[0].reshape(2, 1, NSTATE), s5_lam_im[0].reshape(2, 1, NSTATE)
    ldt = jnp.repeat(s5_log_dt[0], S5_STATE, axis=-1).reshape(2, 1, NSTATE)
    bt_re = jnp.transpose(s5_b_re[0], (0, 3, 1, 2)).reshape(2, S5_GROUP, NSTATE)
    bt_im = jnp.transpose(s5_b_im[0], (0, 3, 1, 2)).reshape(2, S5_GROUP, NSTATE)
    groups_per_block = S5_GROUPS // S5_BLOCKS
    ct_re = jnp.tile(s5_c_re[0].reshape(2, S5_WIDTH, S5_STATE), (1, 1, groups_per_block))
    ct_im = jnp.tile(s5_c_im[0].reshape(2, S5_WIDTH, S5_STATE), (1, 1, groups_per_block))
    d_skip = s5_d[0].reshape(1, S5_WIDTH)
    perm = _segment_permutation()
    perm_t = perm.T
    disc, states, y_dir = [], [], []
    for d in range(2):
        disc.append(_s5_discretise(f"s5_disc{d}", d == 0, lam_re[d], lam_im[d], ldt[d], bt_re[d], bt_im[d], ct_re[d], ct_im[d]))
        _, tab, _, bmat, cmat = disc[d]
        s, yd = _s5_scan_fwd(f"s5_scan_fwd{d}", d == 0, z_all, bmat, cmat, tab, perm, perm_t)
        states.append(s)
        y_dir.append(yd)
    mixer_own, mixer_landed = _exchange_wait("gather_mixer_wait", mixer_send, mixer_recv, mixer_src, mixer_land,
                                             [True] * 3, y_dir[1])
    glu_g, conv_w_g, w_out_g = [_with_own(l, o, me) for l, o in zip(mixer_landed, mixer_own)]
    glu_full = glu_g.reshape(S5_WIDTH, S5_WIDTH)
    conv_w_full = jnp.transpose(conv_w_g, (1, 0, 2)).reshape(CONV_K, CONV_WIDTH)
    w_out_full = w_out_g.reshape(D_MODEL, D_MODEL)
    y_s5 = _glu_fwd(z_all, y_dir[0], y_dir[1], d_skip, glu_full, n_lat_rows)

    hh_pad = _conv_gate(z_all, n_lat_rows)
    hc, y_conv = _conv_fwd(hh_pad, conv_w_full, conv_b, conv_ln_g, conv_ln_b, n_lat_rows)

    ycat = jnp.concatenate([y_s5, y_conv], axis=1)
    tm = min(1024, n_lat_rows)
    w1_cols = D_FF // NDEV
    row_vec = lambda tn: pl.BlockSpec((1, tn), lambda i, j, k: (0, j))
    out_tile = lambda t_m, t_n: pl.BlockSpec((t_m, t_n), lambda i, j, k: (i, j))
    gated = lambda acc, res, gate: (acc, res + gate * acc)
    mix, h1 = _matmul("out_proj", ycat, w_out_full, "nn", (n_lat_rows, D_MODEL, D_MODEL), (tm, D_MODEL, D_MODEL),
                      [((n_lat_rows, D_MODEL), F32)] * 2, epi=gated,
                      epi_extra=[(xs, out_tile(tm, D_MODEL)), (g1, row_vec(D_MODEL))])
    a2 = _prenorm("prenorm2", h1, None, norm2_g, mod[3:5][None])
    mlpw_own, mlpw_landed = _exchange_wait("gather_mlp_wait", mlpw_send, mlpw_recv, mlpw_src, mlpw_land, [True] * 2, a2)
    w1_g, w2_g = [_with_own(l, o, me) for l, o in zip(mlpw_landed, mlpw_own)]
    w2_full = w2_g.reshape(D_FF, D_MODEL)
    tm_up = min(2048, n_lat_rows)
    (f,) = _matmul("mlp_up", a2, w1_g, "nn", (n_lat_rows, D_FF, D_MODEL), (tm_up, w1_cols, D_MODEL),
                   [((n_lat_rows, D_FF), BF16)], b_spec=pl.BlockSpec((None, D_MODEL, w1_cols), lambda i, j, k: (j, 0, 0)))
    sq_relu = lambda t: jnp.square(jnp.maximum(t, 0.0))
    mlp_out, h2 = _matmul("mlp_down", f, w2_full, "nn", (n_lat_rows, D_MODEL, D_FF), (tm, D_MODEL, 1024),
                          [((n_lat_rows, D_MODEL), F32)] * 2, a_fn=sq_relu, epi=gated,
                          epi_extra=[(h1, out_tile(tm, D_MODEL)), (g2, row_vec(D_MODEL))])

    d_h2, dm2, err_sums, d_final_g8 = _loss_head(h2, tgt, final_g[None], g2)
    loss = lax.psum(0.5 / D_MODEL * jnp.sum(err_sums), ("x", "y", "c"))

    (d_f,) = _matmul("mlp_down_dx", dm2, w2_full, "nt", (n_lat_rows, D_FF, D_MODEL), (tm, 512, D_MODEL),
                     [((n_lat_rows, D_FF), BF16)],
                     epi=lambda acc, ft: (acc * 2.0 * jnp.maximum(ft.astype(F32), 0.0),), epi_extra=[(f, out_tile(tm, 512))])
    (g_w2,) = _matmul("mlp_down_dw", f, dm2, "tn", (D_FF, D_MODEL, n_lat_rows), (1024, D_MODEL, tm),
                      [((D_FF, D_MODEL), F32)], a_fn=sq_relu)
    (d_a2,) = _matmul("mlp_up_dx", d_f, w1_g, "nt", (n_lat_rows, D_MODEL, D_FF), (tm, D_MODEL, w1_cols),
                      [((n_lat_rows, D_MODEL), F32)],
                      b_spec=pl.BlockSpec((None, D_MODEL, w1_cols), lambda i, j, k: (k, 0, 0)))
    (g_w1,) = _matmul("mlp_up_dw", a2, d_f, "tn", (D_MODEL, D_FF, n_lat_rows), (D_MODEL, w1_cols, tm),
                      [((NDEV, D_MODEL, w1_cols), F32)],
                      out_specs=[pl.BlockSpec((None, D_MODEL, w1_cols), lambda i, j, k: (j, 0, 0))])
    mlp_send, mlp_recv, mlp_src, mlp_land, mlp_token = _exchange_start(
        "scatter_mlp_start", [g_w1, g_w2.reshape(NDEV, D_FF // NDEV, D_MODEL)], [False] * 2)
    d_h1, dm1, sums2 = _norm_bwd("norm2_bwd", h1, d_a2, 0, norm2_g, sc2 + mlp_token[0:1, 0:1], res=d_h2, aux=mlp_out,
                                 gate=g1)

    (d_ycat,) = _matmul("out_proj_dx", dm1, w_out_full, "nt", (n_lat_rows, D_MODEL, D_MODEL), (tm, D_MODEL, D_MODEL),
                        [((n_lat_rows, D_MODEL), F32)])
    (g_w_out,) = _matmul("out_proj_dw", ycat, dm1, "tn", (D_MODEL, D_MODEL, n_lat_rows), (D_MODEL, D_MODEL, 512),
                         [((D_MODEL, D_MODEL), F32)])

    dy, g_glu, dd8 = _glu_bwd(d_ycat, z_all, y_dir[0], y_dir[1], d_skip, glu_full, n_lat_rows)
    proj_send, proj_recv, proj_src, proj_land, proj_token = _exchange_start(
        "scatter_proj_start",
        [g_w_out.reshape(NDEV, D_MODEL // NDEV, D_MODEL), g_glu.reshape(NDEV, S5_WIDTH // NDEV, S5_WIDTH)], [False] * 2)
    perm = perm + proj_token[0:1, 0:1].astype(BF16)
    du, g_lam_re, g_lam_im, g_ldt, g_b_re, g_b_im, g_c_re, g_c_im = [], [], [], [], [], [], [], []

    def diag(mat):
        return jnp.diagonal(mat.reshape(S5_BLOCKS, groups_per_block, S5_GROUP, 2, groups_per_block, S5_STATE), axis1=1, axis2=4)

    for d in range(2):
        _, _, adj, bmat, cmat = disc[d]
        du_d, d_bmat, d_cmat, d_abar8 = _s5_scan_bwd(f"s5_scan_bwd{d}", d == 0, dy, z_all, states[d], bmat, cmat, adj,
                                                     perm, perm_t)
        du.append(du_d)
        d_bbar = jnp.transpose(diag(d_bmat), (2, 1, 0, 4, 3)).reshape(2 * S5_GROUP, NSTATE)
        d_c = jnp.transpose(diag(d_cmat), (2, 0, 4, 1, 3)).reshape(2, S5_GROUPS, S5_GROUP, S5_STATE)
        d_lam8, d_bt = _s5_discretise_bwd(f"s5_disc_bwd{d}", lam_re[d], lam_im[d], ldt[d], bt_re[d], bt_im[d], d_abar8, d_bbar)
        g_lam_re.append(d_lam8[0].reshape(S5_GROUPS, S5_STATE))
        g_lam_im.append(d_lam8[1].reshape(S5_GROUPS, S5_STATE))
        g_ldt.append(d_lam8[2].reshape(S5_GROUPS, S5_STATE).sum(axis=-1))
        to_gph = lambda t: jnp.transpose(t.reshape(S5_GROUP, S5_GROUPS, S5_STATE), (1, 2, 0))
        g_b_re.append(to_gph(d_bt[:S5_GROUP]))
        g_b_im.append(to_gph(d_bt[S5_GROUP:]))
        g_c_re.append(d_c[0])
        g_c_im.append(-d_c[1])

    dhc_pad, conv_sums = _conv_bwd_norm(d_ycat, hc, conv_ln_g, conv_ln_b, n_lat_rows)
    d_v, d_gate, g_conv_w8 = _conv_bwd_taps(dhc_pad, hh_pad, z_all, conv_w_full, n_lat_rows)

    dz_all = _dz_assemble(du[0], du[1], dy, d_skip, d_v, d_gate, n_lat)
    (g_w_in_full,) = _matmul("in_proj_dw", a_all, dz_all, "tn", (D_MODEL, IN_COLS, n_rows), (D_MODEL, IN_COLS, ROW_BLOCK),
                             [((D_MODEL, IN_COLS), F32)])
    g_w_in_parts = jnp.transpose(g_w_in_full.reshape(D_MODEL, NDEV, IN_COLS // NDEV), (1, 0, 2)).astype(BF16)
    win_send, win_recv, win_src, win_land, win_token = _exchange_start("scatter_w_in_start", [g_w_in_parts], [False])
    (d_a_all,) = _matmul("in_proj_dx", dz_all, w_in_full + win_token[0:1, 0:1].astype(BF16), "nt",
                         (n_rows, D_MODEL, IN_COLS), (ROW_BLOCK, D_MODEL, IN_COLS), [((n_rows, D_MODEL), F32)])
    grad_x, sums1 = _norm_bwd("norm1_bwd", xs, d_a_all, 0, norm1_g, sc1, res=d_h1, aux=mix)
    (sums1c,) = _norm_bwd("norm1_bwd_ctx", cs, d_a_all, n_lat, norm1_g, modc[1:2])

    s1, s1c, s2 = sums1.sum(axis=1), sums1c.sum(axis=1), sums2.sum(axis=1)
    d_mod = jnp.concatenate([s1[0], s1[1], s1[3], s2[0], s2[1], s2[3]])
    d_modc = jnp.concatenate([s1c[0], s1c[1], jnp.zeros((4 * D_MODEL,), F32)])
    (dmod_g,), _ = _exchange("gather_dmod", [jnp.stack([d_mod, d_modc])], [True])
    dmod16 = jnp.concatenate([dmod_g[:, 0], dmod_g[:, 1]])
    dmod16_loc = lax.dynamic_slice(dmod16, (0, me * ada_cols), (16, ada_cols))
    cond_bwd = jnp.concatenate([c_all, jnp.broadcast_to(c_ctx[None], (NDEV, D_MODEL))])
    g_ada_w, g_c_ctx8 = _ada_bwd(cond_bwd, dmod16_loc, ada_w[0], c_ctx[None])

    small_parts = dict(
        c_ctx=g_c_ctx8[0], ada_b=d_mod + d_modc, norm1_g=s1[2] + s1c[2],
        s5_lam_re=jnp.stack(g_lam_re), s5_lam_im=jnp.stack(g_lam_im), s5_log_dt=jnp.stack(g_ldt),
        s5_b_re=jnp.stack(g_b_re), s5_b_im=jnp.stack(g_b_im), s5_c_re=jnp.stack(g_c_re), s5_c_im=jnp.stack(g_c_im),
        s5_d=dd8.sum(axis=0), conv_b=conv_sums[0].sum(axis=0), conv_ln_g=conv_sums[1].sum(axis=0),
        conv_ln_b=conv_sums[2].sum(axis=0), norm2_g=s2[2], final_g=d_final_g8.sum(axis=0))
    small_g = _pack_small([small_parts[n] for n in SMALL]).reshape(NDEV, SMALL_ROWS // NDEV, D_MODEL)
    g_conv_w_parts = jnp.transpose(g_conv_w8.sum(axis=1).reshape(CONV_K, NDEV, CONV_WIDTH // NDEV), (1, 0, 2))
    (p_conv_w, p_small), _ = _exchange("scatter_grads", [g_conv_w_parts, small_g], [False] * 2)
    (small_all,), _ = _exchange("gather_small", [_sum_parts(p_small)], [True])
    small_all = small_all.reshape(1, SMALL_ROWS, D_MODEL)

    def own_chunk(src):
        return lax.dynamic_index_in_dim(src, me, 0, keepdims=False)

    win_src, win_landed = _exchange_wait("scatter_w_in_wait", win_send, win_recv, win_src, win_land, [False], small_all)
    p_w_in = _with_own(win_landed[0], own_chunk(win_src[0]), me)
    mlp_src, mlp_landed = _exchange_wait("scatter_mlp_wait", mlp_send, mlp_recv, mlp_src, mlp_land, [False] * 2, small_all)
    p_w1, p_w2 = [_with_own(l, own_chunk(s), me) for l, s in zip(mlp_landed, mlp_src)]
    proj_src, proj_landed = _exchange_wait("scatter_proj_wait", proj_send, proj_recv, proj_src, proj_land, [False] * 2,
                                           small_all)
    p_w_out, p_glu = [_with_own(l, own_chunk(s), me) for l, s in zip(proj_landed, proj_src)]

    res = {}
    big = dict(ada_w=g_ada_w[None], w_in=p_w_in, s5_w_glu=p_glu, conv_w=p_conv_w, w_out=p_w_out, mlp_w1=p_w1, mlp_w2=p_w2)
    for name, parts in big.items():
        outs = _adamw("adamw_" + name, weights[name][0], parts, mom1[name][0], mom2[name][0])
        res[name] = [o[None] for o in outs]
    small_like = [weights[n] for n in SMALL]
    outs = _adamw("adamw_small", _pack_small(small_like), small_all, _pack_small([mom1[n] for n in SMALL]),
                  _pack_small([mom2[n] for n in SMALL]))
    unpacked = [_unpack_small(o, small_like) for o in outs]
    for i, name in enumerate(SMALL):
        res[name] = [u[i] for u in unpacked]

    return (loss, grad_x[None], *[res[n][0] for n in order], *[res[n][1] for n in order],
            *[res[n][2] for n in order], *[res[n][3] for n in order])
```

```python
import functools

import jax
import jax.numpy as jnp
from jax import lax
from jax.experimental import pallas as pl
from jax.experimental.pallas import tpu as pltpu

F32 = jnp.float32
BF16 = jnp.bfloat16
MESH = pl.DeviceIdType.MESH
ANY = pl.BlockSpec(memory_space=pl.ANY)

NDEV = 8
D_MODEL = 1024
GRID_W = 64
S5_WIDTH = 512
S5_GROUP = 16
S5_GROUPS = 32
S5_STATE = 64
NSTATE = S5_GROUPS * S5_STATE
CONV_WIDTH = 512
CONV_K = 31
IN_COLS = S5_WIDTH + 2 * CONV_WIDTH
D_FF = 4 * D_MODEL
EPS_RMS = 1e-6
EPS_LN = 1e-5
ADAM_LR = 0.001
ADAM_B1 = 0.9
ADAM_B2 = 0.999
ADAM_EPS = 1e-08
ADAM_WD = 0.01
ADAM_STEP = 10

SUBLANES = 8
LANES = 128
ROW_BLOCK = 256
SCAN_LANES = 512
SEGMENTS = SUBLANES
STEPS = ROW_BLOCK // SEGMENTS
S5_BLOCKS = 4
S5_BLOCK_WIDTH = S5_WIDTH // S5_BLOCKS
CONV_ROWS = 64
VMEM_LIMIT = 48 * 1024 * 1024
SMALL_ROWS = 320


def _params(sem=None):
    kw = dict(vmem_limit_bytes=VMEM_LIMIT)
    if sem is not None:
        kw["dimension_semantics"] = sem
    return pltpu.CompilerParams(**kw)


def _sds(shape, dtype=F32):
    return jax.ShapeDtypeStruct(tuple(shape), dtype)


def _fold8(x):
    return x.reshape(x.shape[0] // SUBLANES, SUBLANES, x.shape[1]).sum(axis=0)


def _sigmoid(x):
    return 1.0 / (1.0 + jnp.exp(-x))


def _silu(x):
    return x * _sigmoid(x)


def _dsilu(x):
    s = _sigmoid(x)
    return s * (1.0 + x * (1.0 - s))


_GELU_C = 0.7978845608028654


def _gelu(x):
    return 0.5 * x * (1.0 + jnp.tanh(_GELU_C * (x + 0.044715 * x * x * x)))


def _dgelu(x):
    t = jnp.tanh(_GELU_C * (x + 0.044715 * x * x * x))
    return 0.5 * (1.0 + t) + 0.5 * x * (1.0 - t * t) * _GELU_C * (1.0 + 3.0 * 0.044715 * x * x)


def _rms(x):
    rstd = lax.rsqrt(jnp.mean(x * x, axis=-1, keepdims=True) + EPS_RMS)
    return x * rstd, rstd


def _epi_residual_prenorm(acc, res, gate, gain, scale, shift):
    h = res + gate * acc
    xh, _ = _rms(h)
    return acc, h, (xh * gain) * (1.0 + scale) + shift


def _epi_residual_loss(acc, res, gate, target, gain):
    h = res + gate * acc
    xh, rstd = _rms(h)
    err = xh * gain - target
    dy = err * (1.0 / h.shape[-1])
    dxh = dy * gain
    dh = rstd * (dxh - xh * jnp.mean(dxh * xh, axis=-1, keepdims=True))
    return acc, dh, dh * gate, _fold8(err * err), _fold8(dy * xh)


def _epi_norm_bwd(d_act, x, res, aux, gain, scale, gate):
    xh, rstd = _rms(x)
    dn = d_act * (1.0 + scale)
    dxh = dn * gain
    dx = res + rstd * (dxh - xh * jnp.mean(dxh * xh, axis=-1, keepdims=True))
    return dx, dx * gate, _fold8(d_act), _fold8(d_act * (xh * gain)), _fold8(dn * xh), _fold8(res * aux)


def _dot(a, b, mode):
    dims = {"nn": (((1,), (0,)), ((), ())), "nt": (((1,), (1,)), ((), ())), "tn": (((0,), (0,)), ((), ()))}[mode]
    return lax.dot_general(a, b, dims, preferred_element_type=F32)


def _peers(x, y, c):
    out = []
    for k in range(1, NDEV):
        px = 1 - x if k & 4 else x
        py = 1 - y if k & 2 else y
        pc = 1 - c if k & 1 else c
        out.append(((px, py, pc), 4 * px + 2 * py + pc))
    return out


def _exchange_copies(src, land, send_sems, recv_sems, gather):
    x, y, c = lax.axis_index("x"), lax.axis_index("y"), lax.axis_index("c")
    me = 4 * x + 2 * y + c
    out = []
    for a in range(len(src)):
        for k, (peer, plin) in enumerate(_peers(x, y, c)):
            chunk = src[a] if gather[a] else src[a].at[plin]
            sems = dict(send_sem=send_sems.at[a * (NDEV - 1) + k], recv_sem=recv_sems.at[a * (NDEV - 1) + k],
                        device_id=peer, device_id_type=MESH)
            out.append((pltpu.make_async_remote_copy(src_ref=chunk, dst_ref=land[a].at[me], **sems),
                        pltpu.make_async_remote_copy(src_ref=chunk, dst_ref=land[a].at[plin], **sems)))
    return out


def _exchange(name, srcs, gather):
    n = len(srcs)
    outs = [_sds(((NDEV,) + s.shape) if g else s.shape, s.dtype) for s, g in zip(srcs, gather)]

    def body(*refs):
        src, dst, token = refs[:n], refs[n:2 * n], refs[2 * n]
        send_sems, recv_sems, local_sems = refs[2 * n + 1:]
        me = 4 * lax.axis_index("x") + 2 * lax.axis_index("y") + lax.axis_index("c")
        local = [pltpu.make_async_copy(src[a] if gather[a] else src[a].at[me], dst[a].at[me], local_sems.at[a])
                 for a in range(n)]
        for copy in local:
            copy.start()
        copies = _exchange_copies(src, dst, send_sems, recv_sems, gather)
        for copy, _ in copies:
            copy.start()
        token[...] = jnp.zeros_like(token)
        for copy, landing in copies:
            copy.wait_send()
            landing.wait_recv()
        for copy in local:
            copy.wait()

    nsem = n * (NDEV - 1)
    out = pl.pallas_call(
        body, name=name, out_shape=outs + [_sds((SUBLANES, LANES))], in_specs=[ANY] * n,
        out_specs=[ANY] * n + [pl.BlockSpec(memory_space=pltpu.VMEM)],
        scratch_shapes=[pltpu.SemaphoreType.DMA((nsem,)), pltpu.SemaphoreType.DMA((nsem,)), pltpu.SemaphoreType.DMA((n,))],
    )(*srcs)
    return out[:n], out[n]


HBM = pl.BlockSpec(memory_space=pltpu.HBM)
SEM = pl.BlockSpec(memory_space=pltpu.SEMAPHORE)
EFFECT = pltpu.SideEffectType.DATAFLOW_SIDE_EFFECTING


def _exchange_start(name, srcs, gather):
    n = len(srcs)
    lands = [lax.empty(((NDEV,) + s.shape) if g else s.shape, s.dtype) for s, g in zip(srcs, gather)]

    def body(*refs):
        src, land = refs[:n], refs[n:2 * n]
        send_sems, recv_sems = refs[2 * n], refs[2 * n + 1]
        token = refs[-1]
        for copy, _ in _exchange_copies(src, land, send_sems, recv_sems, gather):
            copy.start()
        token[...] = jnp.zeros_like(token)

    hbm = lambda v: pltpu.HBM(v.shape, v.dtype)
    nsem = n * (NDEV - 1)
    out = pl.pallas_call(
        body, name=name,
        out_shape=(pltpu.SemaphoreType.DMA((nsem,)), pltpu.SemaphoreType.DMA((nsem,)), *[hbm(v) for v in srcs],
                   *[hbm(v) for v in lands], _sds((SUBLANES, LANES))),
        in_specs=[HBM] * (2 * n), out_specs=(SEM, SEM, *([HBM] * (2 * n)), pl.BlockSpec(memory_space=pltpu.VMEM)),
        input_output_aliases={i: 2 + i for i in range(2 * n)},
        compiler_params=pltpu.CompilerParams(has_side_effects=EFFECT),
    )(*[pltpu.with_memory_space_constraint(v, pltpu.HBM) for v in list(srcs) + lands])
    return out[0], out[1], out[2:2 + n], out[2 + n:2 + 2 * n], out[-1]


def _exchange_wait(name, send_sems, recv_sems, srcs, lands, gather, after):
    n = len(srcs)

    def body(*refs):
        src, land = refs[:n], refs[n:2 * n]
        send_ref, recv_ref = refs[2 * n], refs[2 * n + 1]
        for copy, landing in _exchange_copies(src, land, send_ref, recv_ref, gather):
            copy.wait_send()
            landing.wait_recv()

    hbm = lambda v: pltpu.HBM(v.shape, v.dtype)
    out = pl.pallas_call(
        body, name=name, out_shape=[hbm(v) for v in list(srcs) + list(lands)],
        in_specs=[HBM] * (2 * n) + [SEM, SEM, ANY], out_specs=[HBM] * (2 * n),
        input_output_aliases={i: i for i in range(2 * n)},
        compiler_params=pltpu.CompilerParams(has_side_effects=EFFECT),
    )(*srcs, *lands, send_sems, recv_sems, after)
    return out[:n], out[n:]


def _with_own(landed, own, me):
    return lax.dynamic_update_slice(landed, own[None], (me,) + (0,) * own.ndim)


def _matmul(name, a, b, mode, mnk, tiles, outs, a_spec=None, b_spec=None, a_fn=None, a_extra=(),
            epi=None, epi_extra=(), out_specs=None):
    m_, n_, k_ = mnk
    tm, tn, tk = tiles
    nk = k_ // tk
    if a_spec is None:
        a_spec = (pl.BlockSpec((tk, tm), lambda i, j, k: (k, i)) if mode == "tn"
                  else pl.BlockSpec((tm, tk), lambda i, j, k: (i, k)))
    if b_spec is None:
        b_spec = (pl.BlockSpec((tn, tk), lambda i, j, k: (j, k)) if mode == "nt"
                  else pl.BlockSpec((tk, tn), lambda i, j, k: (k, j)))
    if out_specs is None:
        out_specs = [pl.BlockSpec((tm, tn), lambda i, j, k: (i, j)) for _ in outs]
    na, ne, no = len(a_extra), len(epi_extra), len(outs)

    def body(*refs):
        a_ref, b_ref = refs[0], refs[1]
        ax = refs[2:2 + na]
        ex = refs[2 + na:2 + na + ne]
        o = refs[2 + na + ne:2 + na + ne + no]

        def finish(res):
            res = epi(res, *[r[...] for r in ex]) if epi is not None else (res,)
            for ref, val in zip(o, res):
                ref[...] = val.astype(ref.dtype)

        at = a_ref[...]
        if a_fn is not None:
            at = a_fn(at, *[r[...] for r in ax])
        part = _dot(at.astype(BF16), b_ref[...].astype(BF16), mode)
        if nk == 1:
            finish(part)
            return
        acc = refs[-1]
        k = pl.program_id(2)

        @pl.when(k == 0)
        def _():
            acc[...] = part

        @pl.when(k > 0)
        def _():
            acc[...] += part

        @pl.when(k == nk - 1)
        def _():
            finish(acc[...])

    return pl.pallas_call(
        body, name=name, grid=(m_ // tm, n_ // tn, nk),
        in_specs=[a_spec, b_spec] + [s for _, s in a_extra] + [s for _, s in epi_extra],
        out_specs=out_specs, out_shape=[_sds(s, d) for s, d in outs],
        scratch_shapes=[pltpu.VMEM((tm, tn), F32)] if nk > 1 else [],
        compiler_params=_params(("parallel", "parallel", "arbitrary")),
    )(a, b, *[x for x, _ in a_extra], *[x for x, _ in epi_extra])


def _prenorm(name, x, ctx, gain, shsc):
    n_lat = x.shape[0] // ROW_BLOCK
    n_ctx = 0 if ctx is None else ctx.shape[0] // ROW_BLOCK
    d = x.shape[1]

    def norm(src, g_ref, m_ref, o_ref):
        xv = src[...]
        xh = xv * lax.rsqrt(jnp.mean(xv * xv, axis=-1, keepdims=True) + EPS_RMS)
        o_ref[...] = ((xh * g_ref[...]) * (1.0 + m_ref[1:2, :]) + m_ref[0:1, :]).astype(o_ref.dtype)

    def body(*refs):
        if ctx is None:
            x_ref, g_ref, m_ref, o_ref = refs
            norm(x_ref, g_ref, m_ref, o_ref)
        else:
            x_ref, c_ref, g_ref, m_ref, o_ref = refs
            i = pl.program_id(0)

            @pl.when(i < n_lat)
            def _():
                norm(x_ref, g_ref, m_ref, o_ref)

            @pl.when(i >= n_lat)
            def _():
                norm(c_ref, g_ref, m_ref, o_ref)

    in_specs = [pl.BlockSpec((ROW_BLOCK, d), lambda i: (jnp.minimum(i, n_lat - 1), 0))]
    args = [x]
    if ctx is not None:
        in_specs.append(pl.BlockSpec((ROW_BLOCK, d), lambda i: (jnp.maximum(i - n_lat, 0), 0)))
        args.append(ctx)
    in_specs += [pl.BlockSpec((1, d), lambda i: (0, 0)),
                 pl.BlockSpec((None, 2, d), lambda i: (jnp.minimum(i // n_lat, 1), 0, 0))]
    args += [gain, shsc]
    return pl.pallas_call(
        body, name=name, grid=(n_lat + n_ctx,), in_specs=in_specs,
        out_specs=pl.BlockSpec((ROW_BLOCK, d), lambda i: (i, 0)),
        out_shape=_sds(((n_lat + n_ctx) * ROW_BLOCK, d), BF16),
        compiler_params=_params(("parallel",)),
    )(*args)


def _norm_bwd(name, x, d_act, d_act_row0, gain, scale, res=None, aux=None, gate=None):
    rows, d = x.shape
    nb = rows // ROW_BLOCK
    has_res = res is not None
    has_gate = gate is not None

    def body(*refs):
        if has_gate:
            x_ref, da_ref, g_ref, sc_ref, r_ref, aux_ref, gate_ref, dx_ref, dm_ref, sums = refs
        elif has_res:
            x_ref, da_ref, g_ref, sc_ref, r_ref, aux_ref, dx_ref, sums = refs
        else:
            x_ref, da_ref, g_ref, sc_ref, sums = refs
        i = pl.program_id(0)

        @pl.when(i == 0)
        def _():
            sums[...] = jnp.zeros_like(sums)

        xv, da = x_ref[...], da_ref[...]
        rstd = lax.rsqrt(jnp.mean(xv * xv, axis=-1, keepdims=True) + EPS_RMS)
        xh = xv * rstd
        g = g_ref[...]
        dn = da * (1.0 + sc_ref[...])
        sums[0] += _fold8(da)
        sums[1] += _fold8(da * (xh * g))
        sums[2] += _fold8(dn * xh)
        if has_res:
            dxh = dn * g
            dx = rstd * (dxh - xh * jnp.mean(dxh * xh, axis=-1, keepdims=True))
            rv = r_ref[...]
            dx_ref[...] = rv + dx
            sums[3] += _fold8(rv * aux_ref[...])
            if has_gate:
                dm_ref[...] = ((rv + dx) * gate_ref[...]).astype(dm_ref.dtype)

    row = lambda i: (i, 0)
    vec = pl.BlockSpec((1, d), lambda i: (0, 0))
    in_specs = [pl.BlockSpec((ROW_BLOCK, d), row), pl.BlockSpec((ROW_BLOCK, d), lambda i: (i + d_act_row0, 0)), vec, vec]
    args = [x, d_act, gain, scale]
    out_shape = [_sds((4, SUBLANES, d))]
    out_specs = [pl.BlockSpec((4, SUBLANES, d), lambda i: (0, 0, 0))]
    if has_res:
        in_specs += [pl.BlockSpec((ROW_BLOCK, d), row), pl.BlockSpec((ROW_BLOCK, d), row)]
        args += [res, aux]
        if has_gate:
            in_specs.append(vec)
            args.append(gate)
            out_shape = [_sds((rows, d), BF16)] + out_shape
            out_specs = [pl.BlockSpec((ROW_BLOCK, d), row)] + out_specs
        out_shape = [_sds((rows, d))] + out_shape
        out_specs = [pl.BlockSpec((ROW_BLOCK, d), row)] + out_specs
    return pl.pallas_call(
        body, name=name, grid=(nb,), in_specs=in_specs, out_specs=out_specs, out_shape=out_shape,
        compiler_params=_params(("arbitrary",)),
    )(*args)


def _loss_head(h2, target, gain, gate):
    rows, d = h2.shape

    def body(h_ref, t_ref, g_ref, gate_ref, dh_ref, dm_ref, err_ref, dg_ref):
        i = pl.program_id(0)

        @pl.when(i == 0)
        def _():
            err_ref[...] = jnp.zeros_like(err_ref)
            dg_ref[...] = jnp.zeros_like(dg_ref)

        hv = h_ref[...]
        rstd = lax.rsqrt(jnp.mean(hv * hv, axis=-1, keepdims=True) + EPS_RMS)
        xh = hv * rstd
        g = g_ref[...]
        err = xh * g - t_ref[...]
        err_ref[...] += _fold8(err * err)
        dy = err * (1.0 / d)
        dg_ref[...] += _fold8(dy * xh)
        dxh = dy * g
        dh = rstd * (dxh - xh * jnp.mean(dxh * xh, axis=-1, keepdims=True))
        dh_ref[...] = dh
        dm_ref[...] = (dh * gate_ref[...]).astype(dm_ref.dtype)

    row = pl.BlockSpec((ROW_BLOCK, d), lambda i: (i, 0))
    acc = pl.BlockSpec((SUBLANES, d), lambda i: (0, 0))
    vec = pl.BlockSpec((1, d), lambda i: (0, 0))
    return pl.pallas_call(
        body, name="loss_head", grid=(rows // ROW_BLOCK,),
        in_specs=[row, row, vec, vec], out_specs=[row, row, acc, acc],
        out_shape=[_sds((rows, d)), _sds((rows, d), BF16), _sds((SUBLANES, d)), _sds((SUBLANES, d))],
        compiler_params=_params(("arbitrary",)),
    )(h2, target, gain, gate)


def _ada_fwd(cond16, ada_w_loc, ada_b_loc):
    cols = ada_w_loc.shape[1]

    def body(c_ref, w_ref, b_ref, o_ref):
        s = _silu(c_ref[...]).astype(BF16)
        o_ref[...] = _dot(s, w_ref[...].astype(BF16), "nn") + b_ref[...]

    return pl.pallas_call(body, name="ada_fwd", out_shape=_sds((16, cols)), compiler_params=_params())(
        cond16, ada_w_loc, ada_b_loc)


def _ada_bwd(cond16, dmod16, ada_w_loc, c_ctx_row):
    k_, cols = ada_w_loc.shape

    def body(c_ref, dm_ref, w_ref, cc_ref, gw_ref, gc_ref):
        s = _silu(c_ref[...]).astype(BF16)
        dm = dm_ref[...]
        gw_ref[...] = _dot(s, dm.astype(BF16), "tn")
        dmc = jnp.sum(dm[8:16, :], axis=0, keepdims=True)
        dmc8 = jnp.broadcast_to(dmc, (SUBLANES, cols)).astype(BF16)
        ds = _dot(dmc8, w_ref[...].astype(BF16), "nt")
        row = lax.broadcasted_iota(jnp.int32, ds.shape, 0)
        gc_ref[...] = jnp.where(row == 0, ds * _dsilu(cc_ref[...]), 0.0)

    return pl.pallas_call(body, name="ada_bwd", out_shape=[_sds((k_, cols)), _sds((SUBLANES, k_))],
                          compiler_params=_params())(cond16, dmod16, ada_w_loc, c_ctx_row)


def _cmul(a, b):
    return a[0] * b[0] - a[1] * b[1], a[0] * b[1] + a[1] * b[0]


def _disc(lam_re, lam_im, ldt):
    dt = jnp.exp(ldt)
    mag = jnp.exp(lam_re * dt)
    th = lam_im * dt
    a_re, a_im = mag * jnp.cos(th), mag * jnp.sin(th)
    den = lam_re * lam_re + lam_im * lam_im
    n_re = a_re - 1.0
    f_re = (n_re * lam_re + a_im * lam_im) / den
    f_im = (a_im * lam_re - n_re * lam_im) / den
    return dt, mag, th, a_re, a_im, den, n_re, f_re, f_im


def _block_diag_mask(shape):
    row = lax.broadcasted_iota(jnp.int32, shape, 0)
    col = lax.broadcasted_iota(jnp.int32, shape, 1)
    return lax.shift_right_logical(row, 4) == lax.shift_right_logical(col, 6)


TAB_A = 0
TAB_BIG = 1
TAB_SEG = 4
TAB_PW = 5
TAB_ROWS = TAB_PW + STEPS


def _s5_discretise(name, ascending, lam_re, lam_im, ldt, bt_re, bt_im, ct_re, ct_im):
    def write_tables(ref, pw, big, asc, sign):
        row = lax.broadcasted_iota(jnp.int32, (SUBLANES, NSTATE), 0)
        full = lambda v: jnp.broadcast_to(v, (SUBLANES, NSTATE))

        def put(t, p):
            ref[0, t] = full(p[0])
            ref[1, t] = full(sign * p[1])

        put(TAB_A, pw[0])
        for t in range(3):
            put(TAB_BIG + t, big[t])
        seg = [big[0]]
        for _ in range(SEGMENTS - 1):
            seg.append(_cmul(seg[-1], big[0]))
        seg_re = jnp.zeros((SUBLANES, NSTATE), F32)
        seg_im = jnp.zeros((SUBLANES, NSTATE), F32)
        for r in range(SEGMENTS):
            p = seg[r] if asc else seg[SEGMENTS - 1 - r]
            seg_re = jnp.where(row == r, p[0], seg_re)
            seg_im = jnp.where(row == r, sign * p[1], seg_im)
        ref[0, TAB_SEG] = seg_re
        ref[1, TAB_SEG] = seg_im
        for k in range(STEPS):
            put(TAB_PW + k, pw[k])

    def body(lr_ref, li_ref, ldt_ref, br_ref, bi_ref, cr_ref, ci_ref, bb_ref, tab_ref, adj_ref, bm_ref, cm_ref):
        _, _, _, a_re, a_im, _, _, f_re, f_im = _disc(lr_ref[...], li_ref[...], ldt_ref[...])
        bre, bim = br_ref[...], bi_ref[...]
        bb_re = f_re * bre - f_im * bim
        bb_im = f_re * bim + f_im * bre
        bb_ref[0:S5_GROUP, :] = bb_re
        bb_ref[S5_GROUP:2 * S5_GROUP, :] = bb_im
        pw = [(a_re, a_im)]
        for _ in range(STEPS - 1):
            pw.append(_cmul(pw[-1], (a_re, a_im)))
        big = [pw[STEPS - 1]]
        for _ in range(2):
            big.append(_cmul(big[-1], big[-1]))
        write_tables(tab_ref, pw, big, ascending, 1.0)
        write_tables(adj_ref, pw, big, not ascending, -1.0)
        half = NSTATE // S5_BLOCKS
        mask = _block_diag_mask((S5_BLOCK_WIDTH, half))
        tile = lambda v: jnp.broadcast_to(v[None], (S5_BLOCK_WIDTH // S5_GROUP, S5_GROUP, half)).reshape(S5_BLOCK_WIDTH, half)
        for c in range(S5_BLOCKS):
            cols = slice(c * half, (c + 1) * half)
            rows = slice(c * S5_BLOCK_WIDTH, (c + 1) * S5_BLOCK_WIDTH)
            bm_ref[c, :, 0:half] = jnp.where(mask, tile(bb_re[:, cols]), 0.0).astype(BF16)
            bm_ref[c, :, half:2 * half] = jnp.where(mask, tile(bb_im[:, cols]), 0.0).astype(BF16)
            cm_ref[c, :, 0:half] = jnp.where(mask, cr_ref[rows, :], 0.0).astype(BF16)
            cm_ref[c, :, half:2 * half] = jnp.where(mask, -ci_ref[rows, :], 0.0).astype(BF16)

    blocked = _sds((S5_BLOCKS, S5_BLOCK_WIDTH, 2 * NSTATE // S5_BLOCKS), BF16)
    return pl.pallas_call(
        body, name=name,
        out_shape=[_sds((2 * S5_GROUP, NSTATE)), _sds((2, TAB_ROWS, SUBLANES, NSTATE)),
                   _sds((2, TAB_ROWS, SUBLANES, NSTATE)), blocked, blocked],
        compiler_params=_params(),
    )(lam_re, lam_im, ldt, bt_re, bt_im, ct_re, ct_im)


def _s5_discretise_bwd(name, lam_re, lam_im, ldt, bt_re, bt_im, d_abar8, d_bbar):
    def body(lr_ref, li_ref, ldt_ref, br_ref, bi_ref, da_ref, db_ref, dl_ref, dbt_ref):
        lam_re, lam_im = lr_ref[...], li_ref[...]
        dt, mag, _, a_re, a_im, den, n_re, f_re, f_im = _disc(lam_re, lam_im, ldt_ref[...])
        bre, bim = br_ref[...], bi_ref[...]
        dbr, dbi = db_ref[0:S5_GROUP, :], db_ref[S5_GROUP:2 * S5_GROUP, :]
        dbt_ref[0:S5_GROUP, :] = f_re * dbr + f_im * dbi
        dbt_ref[S5_GROUP:2 * S5_GROUP, :] = f_re * dbi - f_im * dbr
        df_re = jnp.sum(bre * dbr + bim * dbi, axis=0, keepdims=True)
        df_im = jnp.sum(bre * dbi - bim * dbr, axis=0, keepdims=True)
        da = da_ref[...]
        da_re = jnp.sum(da[:, 0:NSTATE], axis=0, keepdims=True)
        da_im = jnp.sum(da[:, NSTATE:2 * NSTATE], axis=0, keepdims=True)
        da_re = da_re + (df_re * lam_re - df_im * lam_im) / den
        da_im = da_im + (df_re * lam_im + df_im * lam_re) / den
        ff = (f_re * df_re + f_im * df_im) * 2.0 / den
        d_lr = (df_re * n_re + df_im * a_im) / den - ff * lam_re
        d_li = (df_re * a_im - df_im * n_re) / den - ff * lam_im
        d_mag = (da_re * a_re + da_im * a_im) / mag
        d_th = da_im * a_re - da_re * a_im
        d_lr = d_lr + d_mag * mag * dt
        d_li = d_li + d_th * dt
        d_ldt = (d_mag * mag * lam_re + d_th * lam_im) * dt
        row = lax.broadcasted_iota(jnp.int32, (SUBLANES, NSTATE), 0)
        dl_ref[...] = jnp.where(row == 0, d_lr, jnp.where(row == 1, d_li, jnp.where(row == 2, d_ldt, 0.0)))

    return pl.pallas_call(
        body, name=name, out_shape=[_sds((SUBLANES, NSTATE)), _sds((2 * S5_GROUP, NSTATE))],
        compiler_params=_params(),
    )(lam_re, lam_im, ldt, bt_re, bt_im, d_abar8, d_bbar)


def _segment_permutation():
    rho = jnp.arange(ROW_BLOCK)
    src = STEPS * (rho % SEGMENTS) + rho // SEGMENTS
    return (src[:, None] == jnp.arange(ROW_BLOCK)[None, :]).astype(BF16)


def _permute_rows(perm_ref, v):
    return _dot(perm_ref[...], v, "nn").astype(BF16)


def _unpermute_rows(perm_t_ref, v):
    hi = v.astype(BF16)
    lo = (v - hi.astype(F32)).astype(BF16)
    return _dot(perm_t_ref[...], hi, "nn") + _dot(perm_t_ref[...], lo, "nn")


def _scan_chunk(x_ref, out_ref, tab_ref, carry_re, carry_im, ascending, pair_ref=None, acc_ref=None):
    w = SCAN_LANES
    half = NSTATE // S5_BLOCKS
    row = lax.broadcasted_iota(jnp.int32, (SUBLANES, w), 0)
    last = (SEGMENTS - 1) if ascending else 0

    def from_previous_segment(v, k, fill):
        if ascending:
            return jnp.where(row >= k, pltpu.roll(v, k, 0), fill)
        return jnp.where(row < SEGMENTS - k, pltpu.roll(v, SEGMENTS - k, 0), fill)

    def tile_rows(k):
        return pl.ds(pl.multiple_of((k if ascending else STEPS - 1 - k) * SUBLANES, SUBLANES), SUBLANES)

    for j in range(NSTATE // w):
        n_l = pl.ds(j * w, w)
        lane0 = (j * w // half) * 2 * half + (j * w) % half
        re_l, im_l = pl.ds(lane0, w), pl.ds(lane0 + half, w)
        tab = lambda t, n_l=n_l: (tab_ref[0, t, :, n_l], tab_ref[1, t, :, n_l])
        a_re, a_im = tab(TAB_A)

        def local_step(k, h):
            rs = tile_rows(k)
            h_re = a_re * h[0] - a_im * h[1] + x_ref[rs, re_l]
            h_im = a_re * h[1] + a_im * h[0] + x_ref[rs, im_l]
            out_ref[rs, re_l] = h_re
            out_ref[rs, im_l] = h_im
            return h_re, h_im

        zero = jnp.zeros((SUBLANES, w), F32)
        end_re, end_im = lax.fori_loop(0, STEPS, local_step, (zero, zero))
        for t, k in ((TAB_BIG, 1), (TAB_BIG + 1, 2), (TAB_BIG + 2, 4)):
            p_re, p_im = tab(t)
            s_re, s_im = from_previous_segment(end_re, k, 0.0), from_previous_segment(end_im, k, 0.0)
            end_re, end_im = end_re + (p_re * s_re - p_im * s_im), end_im + (p_re * s_im + p_im * s_re)
        c0_re, c0_im = carry_re[:, n_l], carry_im[:, n_l]
        p_re, p_im = tab(TAB_SEG)
        end_re = end_re + (p_re * c0_re - p_im * c0_im)
        end_im = end_im + (p_re * c0_im + p_im * c0_re)
        carry_re[:, n_l] = jnp.broadcast_to(end_re[last:last + 1, :], end_re.shape)
        carry_im[:, n_l] = jnp.broadcast_to(end_im[last:last + 1, :], end_im.shape)
        in_re = from_previous_segment(end_re, 1, c0_re)
        in_im = from_previous_segment(end_im, 1, c0_im)

        def carry_step(k, st):
            rs = tile_rows(k)
            p_re, p_im = tab_ref[0, TAB_PW + k, :, n_l], tab_ref[1, TAB_PW + k, :, n_l]
            o_re = out_ref[rs, re_l] + (p_re * in_re - p_im * in_im)
            o_im = out_ref[rs, im_l] + (p_re * in_im + p_im * in_re)
            out_ref[rs, re_l] = o_re
            out_ref[rs, im_l] = o_im
            if pair_ref is None:
                return st
            s_re, s_im = pair_ref[rs, re_l], pair_ref[rs, im_l]
            return (o_re, o_im, st[2] + (st[0] * s_re + st[1] * s_im), st[3] + (st[1] * s_re - st[0] * s_im))

        if pair_ref is None:
            lax.fori_loop(0, STEPS, carry_step, 0)
        else:
            fin = lax.fori_loop(0, STEPS, carry_step, (in_re, in_im, zero, zero))
            acc_ref[:, n_l] += fin[2]
            acc_ref[:, pl.ds(NSTATE + j * w, w)] += fin[3]


def _scan_block_index(i, n_lat, ctx_first_then_ascending):
    if ctx_first_then_ascending:
        return jnp.where(i == 0, n_lat, i - 1)
    return jnp.where(i == 0, n_lat, n_lat - i)


def _full_spec(shape):
    return pl.BlockSpec(shape, lambda i: (0,) * len(shape))


_S5_BLOCKED = (S5_BLOCKS, S5_BLOCK_WIDTH, 2 * NSTATE // S5_BLOCKS)
_S5_TABLES = (2, TAB_ROWS, SUBLANES, NSTATE)


def _s5_scan_fwd(name, ascending, z_all, bmat, cmat, tab, perm, perm_t):
    rows = z_all.shape[0]
    nb = rows // ROW_BLOCK
    n_lat = nb - 1
    bw, sw = S5_BLOCK_WIDTH, 2 * NSTATE // S5_BLOCKS

    def body(u_ref, bm_ref, cm_ref, tab_ref, p_ref, pt_ref, s_ref, y_ref, bu, yp, carry_re, carry_im):
        @pl.when(pl.program_id(0) == 0)
        def _():
            carry_re[...] = jnp.zeros_like(carry_re)
            carry_im[...] = jnp.zeros_like(carry_im)

        up = _permute_rows(p_ref, u_ref[...].astype(BF16))
        for c in range(S5_BLOCKS):
            bu[:, c * sw:(c + 1) * sw] = _dot(up[:, c * bw:(c + 1) * bw], bm_ref[c], "nn")
        _scan_chunk(bu, s_ref, tab_ref, carry_re, carry_im, ascending)
        for c in range(S5_BLOCKS):
            yp[:, c * bw:(c + 1) * bw] = _dot(s_ref[:, c * sw:(c + 1) * sw].astype(BF16), cm_ref[c], "nt")
        y_ref[...] = _unpermute_rows(pt_ref, yp[...])

    blk = lambda i: (_scan_block_index(i, n_lat, ascending), 0)
    return pl.pallas_call(
        body, name=name, grid=(nb,),
        in_specs=[pl.BlockSpec((ROW_BLOCK, S5_WIDTH), blk), _full_spec(_S5_BLOCKED), _full_spec(_S5_BLOCKED),
                  _full_spec(_S5_TABLES), _full_spec((ROW_BLOCK, ROW_BLOCK)), _full_spec((ROW_BLOCK, ROW_BLOCK))],
        out_specs=[pl.BlockSpec((ROW_BLOCK, 2 * NSTATE), blk), pl.BlockSpec((ROW_BLOCK, S5_WIDTH), blk)],
        out_shape=[_sds((rows, 2 * NSTATE)), _sds((rows, S5_WIDTH))],
        scratch_shapes=[pltpu.VMEM((ROW_BLOCK, 2 * NSTATE), F32), pltpu.VMEM((ROW_BLOCK, S5_WIDTH), F32),
                        pltpu.VMEM((SUBLANES, NSTATE), F32), pltpu.VMEM((SUBLANES, NSTATE), F32)],
        compiler_params=_params(("arbitrary",)),
    )(z_all, bmat, cmat, tab, perm, perm_t)


def _s5_scan_bwd(name, ascending, dy, z_all, states, bmat, cmat, adj, perm, perm_t):
    rows = states.shape[0]
    nb = rows // ROW_BLOCK
    n_lat = nb - 1
    bw, sw = S5_BLOCK_WIDTH, 2 * NSTATE // S5_BLOCKS

    def block_index(i):
        if ascending:
            return jnp.where(i == nb - 1, n_lat, n_lat - 1 - i)
        return jnp.where(i == nb - 1, n_lat, i)

    def body(dy_ref, u_ref, s_ref, bm_ref, cm_ref, adj_ref, p_ref, pt_ref, du_ref, db_ref, dc_ref, da_ref,
             g, dup, carry_re, carry_im):
        i = pl.program_id(0)

        @pl.when(i == 0)
        def _():
            carry_re[...] = jnp.zeros_like(carry_re)
            carry_im[...] = jnp.zeros_like(carry_im)
            da_ref[...] = jnp.zeros_like(da_ref)
            db_ref[...] = jnp.zeros_like(db_ref)
            dc_ref[...] = jnp.zeros_like(dc_ref)

        @pl.when(i < nb - 1)
        def _():
            dyp = _permute_rows(p_ref, dy_ref[...].astype(BF16))
            for c in range(S5_BLOCKS):
                g[:, c * sw:(c + 1) * sw] = _dot(dyp[:, c * bw:(c + 1) * bw], cm_ref[c], "nn")
                dc_ref[c] += _dot(dyp[:, c * bw:(c + 1) * bw], s_ref[:, c * sw:(c + 1) * sw].astype(BF16), "tn")

        @pl.when(i == nb - 1)
        def _():
            g[...] = jnp.zeros_like(g)

        _scan_chunk(g, g, adj_ref, carry_re, carry_im, not ascending, pair_ref=s_ref, acc_ref=da_ref)
        up = _permute_rows(p_ref, u_ref[...].astype(BF16))
        for c in range(S5_BLOCKS):
            gc = g[:, c * sw:(c + 1) * sw].astype(BF16)
            dup[:, c * bw:(c + 1) * bw] = _dot(gc, bm_ref[c], "nt")
            db_ref[c] += _dot(up[:, c * bw:(c + 1) * bw], gc, "tn")
        du_ref[...] = _unpermute_rows(pt_ref, dup[...])

    blk = lambda i: (block_index(i), 0)
    return pl.pallas_call(
        body, name=name, grid=(nb,),
        in_specs=[pl.BlockSpec((ROW_BLOCK, S5_WIDTH), lambda i: (jnp.minimum(block_index(i), n_lat - 1), 0)),
                  pl.BlockSpec((ROW_BLOCK, S5_WIDTH), blk), pl.BlockSpec((ROW_BLOCK, 2 * NSTATE), blk),
                  _full_spec(_S5_BLOCKED), _full_spec(_S5_BLOCKED), _full_spec(_S5_TABLES),
                  _full_spec((ROW_BLOCK, ROW_BLOCK)), _full_spec((ROW_BLOCK, ROW_BLOCK))],
        out_specs=[pl.BlockSpec((ROW_BLOCK, S5_WIDTH), blk), _full_spec(_S5_BLOCKED), _full_spec(_S5_BLOCKED),
                   _full_spec((SUBLANES, 2 * NSTATE))],
        out_shape=[_sds((rows, S5_WIDTH)), _sds(_S5_BLOCKED), _sds(_S5_BLOCKED), _sds((SUBLANES, 2 * NSTATE))],
        scratch_shapes=[pltpu.VMEM((ROW_BLOCK, 2 * NSTATE), F32), pltpu.VMEM((ROW_BLOCK, S5_WIDTH), F32),
                        pltpu.VMEM((SUBLANES, NSTATE), F32), pltpu.VMEM((SUBLANES, NSTATE), F32)],
        compiler_params=_params(("arbitrary",)),
    )(dy, z_all, states, bmat, cmat, adj, perm, perm_t)


def _glu_fwd(z_all, y0, y1, d_skip, w_glu, n_rows):
    def body(u_ref, y0_ref, y1_ref, d_ref, w_ref, o_ref):
        y = d_ref[...] * u_ref[...] + y0_ref[...] + y1_ref[...]
        g = _gelu(y)
        t = _dot(g.astype(BF16), w_ref[...], "nn")
        o_ref[...] = (g * _sigmoid(t)).astype(o_ref.dtype)

    row = pl.BlockSpec((ROW_BLOCK, S5_WIDTH), lambda i: (i, 0))
    return pl.pallas_call(
        body, name="glu_fwd", grid=(n_rows // ROW_BLOCK,),
        in_specs=[row, row, row, pl.BlockSpec((1, S5_WIDTH), lambda i: (0, 0)),
                  pl.BlockSpec((S5_WIDTH, S5_WIDTH), lambda i: (0, 0))],
        out_specs=row, out_shape=_sds((n_rows, S5_WIDTH + CONV_WIDTH), BF16), compiler_params=_params(("parallel",)),
    )(z_all, y0, y1, d_skip, w_glu)


def _glu_bwd(d_ycat, z_all, y0, y1, d_skip, w_glu, n_rows):
    def body(do_ref, u_ref, y0_ref, y1_ref, d_ref, w_ref, dy_ref, dw_ref, dd_ref):
        @pl.when(pl.program_id(0) == 0)
        def _():
            dw_ref[...] = jnp.zeros_like(dw_ref)
            dd_ref[...] = jnp.zeros_like(dd_ref)

        u = u_ref[...]
        y = d_ref[...] * u + y0_ref[...] + y1_ref[...]
        g = _gelu(y)
        gb = g.astype(BF16)
        w = w_ref[...]
        sg = _sigmoid(_dot(gb, w, "nn"))
        do = do_ref[...]
        dt = do * g * sg * (1.0 - sg)
        dtb = dt.astype(BF16)
        dg = do * sg + _dot(dtb, w, "nt")
        dy = dg * _dgelu(y)
        dy_ref[...] = dy
        dw_ref[...] += _dot(gb, dtb, "tn")
        dd_ref[...] += _fold8(dy * u)

    row = pl.BlockSpec((ROW_BLOCK, S5_WIDTH), lambda i: (i, 0))
    sq = pl.BlockSpec((S5_WIDTH, S5_WIDTH), lambda i: (0, 0))
    return pl.pallas_call(
        body, name="glu_bwd", grid=(n_rows // ROW_BLOCK,),
        in_specs=[row, row, row, row, pl.BlockSpec((1, S5_WIDTH), lambda i: (0, 0)), sq],
        out_specs=[row, sq, pl.BlockSpec((SUBLANES, S5_WIDTH), lambda i: (0, 0))],
        out_shape=[_sds((n_rows, S5_WIDTH)), _sds((S5_WIDTH, S5_WIDTH)), _sds((SUBLANES, S5_WIDTH))],
        compiler_params=_params(("arbitrary",)),
    )(d_ycat, z_all, y0, y1, d_skip, w_glu)


CONV_HALF = CONV_K // 2


def _conv_block(n_rows):
    blk = min(1024, n_rows)
    assert blk >= CONV_HALF * GRID_W and n_rows % blk == 0
    return blk


def _conv_gate(z_all, n_rows):
    blk = _conv_block(n_rows)
    nb = n_rows // blk

    def body(v_ref, g_ref, o_ref):
        i = pl.program_id(0)
        inside = jnp.logical_and(i >= 1, i <= nb)

        @pl.when(inside)
        def _():
            o_ref[...] = v_ref[...] * _sigmoid(g_ref[...])

        @pl.when(jnp.logical_not(inside))
        def _():
            o_ref[...] = jnp.zeros_like(o_ref)

    src = lambda col: pl.BlockSpec((blk, CONV_WIDTH), lambda i: (jnp.clip(i - 1, 0, nb - 1), col))
    return pl.pallas_call(
        body, name="conv_gate", grid=(nb + 2,), in_specs=[src(1), src(2)],
        out_specs=pl.BlockSpec((blk, CONV_WIDTH), lambda i: (i, 0)),
        out_shape=_sds(((nb + 2) * blk, CONV_WIDTH)), compiler_params=_params(("parallel",)),
    )(z_all, z_all)


def _load_window(pad_ref, win, sem, blk):
    start = pl.multiple_of(pl.program_id(0) * blk, blk)
    copy = pltpu.make_async_copy(pad_ref.at[pl.ds(start, 3 * blk), :], win, sem)
    copy.start()
    copy.wait()


def _conv_fwd(hh_pad, w, b, ln_g, ln_b, ycat, n_rows):
    blk = _conv_block(n_rows)

    def body(hh_ref, w_ref, b_ref, g_ref, lb_ref, ycat_ref, hc_ref, y_ref, win, sem):
        _load_window(hh_ref, win, sem, blk)

        def tile(t, _):
            r0 = pl.multiple_of(t * CONV_ROWS, CONV_ROWS)
            acc = jnp.zeros((CONV_ROWS, CONV_WIDTH), F32)
            for k in range(CONV_K):
                acc = acc + w_ref[k:k + 1, :] * win[pl.ds(r0 + blk + (k - CONV_HALF) * GRID_W, CONV_ROWS), :]
            hc = acc + b_ref[...]
            hc_ref[pl.ds(r0, CONV_ROWS), :] = hc
            mu = jnp.mean(hc, axis=-1, keepdims=True)
            xc = hc - mu
            ln = xc * lax.rsqrt(jnp.mean(xc * xc, axis=-1, keepdims=True) + EPS_LN) * g_ref[...] + lb_ref[...]
            y_ref[pl.ds(r0, CONV_ROWS), :] = _silu(ln).astype(y_ref.dtype)
            return 0

        lax.fori_loop(0, blk // CONV_ROWS, tile, 0)

    vec = pl.BlockSpec((1, CONV_WIDTH), lambda i: (0, 0))
    row = pl.BlockSpec((blk, CONV_WIDTH), lambda i: (i, 0))
    return pl.pallas_call(
        body, name="conv_fwd", grid=(n_rows // blk,),
        in_specs=[ANY, pl.BlockSpec((CONV_K, CONV_WIDTH), lambda i: (0, 0)), vec, vec, vec, ANY],
        out_specs=[row, pl.BlockSpec((blk, CONV_WIDTH), lambda i: (i, 1))],
        out_shape=[_sds((n_rows, CONV_WIDTH)), _sds(ycat.shape, ycat.dtype)], input_output_aliases={5: 1},
        scratch_shapes=[pltpu.VMEM((3 * blk, CONV_WIDTH), F32), pltpu.SemaphoreType.DMA],
        compiler_params=_params(("arbitrary",)),
    )(hh_pad, w, b, ln_g, ln_b, ycat)


def _conv_bwd_norm(d_ycat, hc, ln_g, ln_b, n_rows):
    blk = _conv_block(n_rows)
    nb = n_rows // blk

    def body(dy_ref, hc_ref, g_ref, lb_ref, o_ref, sums):
        i = pl.program_id(0)

        @pl.when(i == 0)
        def _():
            sums[...] = jnp.zeros_like(sums)

        inside = jnp.logical_and(i >= 1, i <= nb)

        @pl.when(inside)
        def _():
            hcv = hc_ref[...]
            mu = jnp.mean(hcv, axis=-1, keepdims=True)
            xc = hcv - mu
            rstd = lax.rsqrt(jnp.mean(xc * xc, axis=-1, keepdims=True) + EPS_LN)
            xh = xc * rstd
            g = g_ref[...]
            dln = dy_ref[...] * _dsilu(xh * g + lb_ref[...])
            dxh = dln * g
            dhc = rstd * (dxh - jnp.mean(dxh, axis=-1, keepdims=True) - xh * jnp.mean(dxh * xh, axis=-1, keepdims=True))
            o_ref[...] = dhc
            sums[0] += _fold8(dhc)
            sums[1] += _fold8(dln * xh)
            sums[2] += _fold8(dln)

        @pl.when(jnp.logical_not(inside))
        def _():
            o_ref[...] = jnp.zeros_like(o_ref)

    vec = pl.BlockSpec((1, CONV_WIDTH), lambda i: (0, 0))
    return pl.pallas_call(
        body, name="conv_bwd_norm", grid=(nb + 2,),
        in_specs=[pl.BlockSpec((blk, CONV_WIDTH), lambda i: (jnp.clip(i - 1, 0, nb - 1), 1)),
                  pl.BlockSpec((blk, CONV_WIDTH), lambda i: (jnp.clip(i - 1, 0, nb - 1), 0)), vec, vec],
        out_specs=[pl.BlockSpec((blk, CONV_WIDTH), lambda i: (i, 0)),
                   pl.BlockSpec((3, SUBLANES, CONV_WIDTH), lambda i: (0, 0, 0))],
        out_shape=[_sds(((nb + 2) * blk, CONV_WIDTH)), _sds((3, SUBLANES, CONV_WIDTH))],
        compiler_params=_params(("arbitrary",)),
    )(d_ycat, hc, ln_g, ln_b)


def _conv_bwd_taps(dhc_pad, hh_pad, z_all, w, n_rows):
    blk = _conv_block(n_rows)

    def body(dhc_ref, hh_ref, v_ref, g_ref, w_ref, dv_ref, dg_ref, dw_ref, dwin, hwin, sems):
        @pl.when(pl.program_id(0) == 0)
        def _():
            dw_ref[...] = jnp.zeros_like(dw_ref)

        _load_window(dhc_ref, dwin, sems.at[0], blk)
        _load_window(hh_ref, hwin, sems.at[1], blk)

        def tile(t, _):
            r0 = pl.multiple_of(t * CONV_ROWS, CONV_ROWS)
            dh = dwin[pl.ds(r0 + blk, CONV_ROWS), :]
            acc = jnp.zeros((CONV_ROWS, CONV_WIDTH), F32)
            for k in range(CONV_K):
                off = (k - CONV_HALF) * GRID_W
                acc = acc + w_ref[k:k + 1, :] * dwin[pl.ds(r0 + blk - off, CONV_ROWS), :]
                dw_ref[k] += _fold8(dh * hwin[pl.ds(r0 + blk + off, CONV_ROWS), :])
            rs = pl.ds(r0, CONV_ROWS)
            sg = _sigmoid(g_ref[rs, :])
            vv = v_ref[rs, :]
            dv_ref[rs, :] = acc * sg
            dg_ref[rs, :] = acc * vv * sg * (1.0 - sg)
            return 0

        lax.fori_loop(0, blk // CONV_ROWS, tile, 0)

    row = pl.BlockSpec((blk, CONV_WIDTH), lambda i: (i, 0))
    return pl.pallas_call(
        body, name="conv_bwd_taps", grid=(n_rows // blk,),
        in_specs=[ANY, ANY,
            pl.BlockSpec((blk, CONV_WIDTH), lambda i: (i, 1)), pl.BlockSpec((blk, CONV_WIDTH), lambda i: (i, 2)),
            pl.BlockSpec((CONV_K, CONV_WIDTH), lambda i: (0, 0))],
        out_specs=[row, row, pl.BlockSpec((CONV_K, SUBLANES, CONV_WIDTH), lambda i: (0, 0, 0))],
        out_shape=[_sds((n_rows, CONV_WIDTH)), _sds((n_rows, CONV_WIDTH)), _sds((CONV_K, SUBLANES, CONV_WIDTH))],
        scratch_shapes=[pltpu.VMEM((3 * blk, CONV_WIDTH), F32), pltpu.VMEM((3 * blk, CONV_WIDTH), F32),
                        pltpu.SemaphoreType.DMA((2,))],
        compiler_params=_params(("arbitrary",)),
    )(dhc_pad, hh_pad, z_all, z_all, w)


def _dz_assemble(du0, du1, dy, d_skip, dv, dgate, n_lat):
    rows = du0.shape[0]
    nb = rows // ROW_BLOCK

    w = S5_WIDTH

    def body(a_ref, b_ref, dy_ref, d_ref, dv_ref, dg_ref, o_ref):
        lat = pl.program_id(0) < n_lat

        @pl.when(lat)
        def _():
            o_ref[:, 0:w] = (a_ref[...] + b_ref[...] + dy_ref[...] * d_ref[...]).astype(o_ref.dtype)
            o_ref[:, w:2 * w] = dv_ref[...].astype(o_ref.dtype)
            o_ref[:, 2 * w:3 * w] = dg_ref[...].astype(o_ref.dtype)

        @pl.when(jnp.logical_not(lat))
        def _():
            o_ref[:, 0:w] = (a_ref[...] + b_ref[...]).astype(o_ref.dtype)
            o_ref[:, w:3 * w] = jnp.zeros((ROW_BLOCK, 2 * w), o_ref.dtype)

    all_rows = pl.BlockSpec((ROW_BLOCK, w), lambda i: (i, 0))
    lat_rows = pl.BlockSpec((ROW_BLOCK, w), lambda i: (jnp.minimum(i, n_lat - 1), 0))
    return pl.pallas_call(
        body, name="dz_assemble", grid=(nb,),
        in_specs=[all_rows, all_rows, lat_rows, pl.BlockSpec((1, w), lambda i: (0, 0)), lat_rows, lat_rows],
        out_specs=pl.BlockSpec((ROW_BLOCK, IN_COLS), lambda i: (i, 0)),
        out_shape=_sds((rows, IN_COLS), BF16), compiler_params=_params(("parallel",)),
    )(du0, du1, dy, d_skip, dv, dgate)


def _sum_parts(parts):
    _, r, c = parts.shape

    def body(p_ref, o_ref):
        acc = p_ref[0]
        for q in range(1, NDEV):
            acc = acc + p_ref[q]
        o_ref[...] = acc

    return pl.pallas_call(body, name="sum_parts", out_shape=_sds((r, c)), compiler_params=_params())(parts)


def _row_tile(r, c):
    best = r
    for t in (1024, 512, 256, 128, 64, 32, 16, 8):
        if r % t == 0 and t * c <= 128 * 1024:
            return t
    return best


def _adamw(name, w, gparts, m, v):
    r, c = w.shape
    np_ = gparts.shape[0]
    tr = _row_tile(r, c)

    def body(w_ref, g_ref, m_ref, v_ref, go_ref, d_ref, mo_ref, vo_ref):
        g = g_ref[0].astype(F32)
        for q in range(1, np_):
            g = g + g_ref[q].astype(F32)
        m2 = ADAM_B1 * m_ref[...] + (1.0 - ADAM_B1) * g
        v2 = ADAM_B2 * v_ref[...] + (1.0 - ADAM_B2) * jnp.square(g)
        m_hat = m2 / (1.0 - ADAM_B1 ** ADAM_STEP)
        v_hat = v2 / (1.0 - ADAM_B2 ** ADAM_STEP)
        go_ref[...] = g
        d_ref[...] = -ADAM_LR * (m_hat / (jnp.sqrt(v_hat) + ADAM_EPS) + ADAM_WD * w_ref[...])
        mo_ref[...] = m2
        vo_ref[...] = v2

    row = pl.BlockSpec((tr, c), lambda i: (i, 0))
    return pl.pallas_call(
        body, name=name, grid=(r // tr,),
        in_specs=[row, pl.BlockSpec((np_, tr, c), lambda i: (0, i, 0)), row, row],
        out_specs=[row] * 4, out_shape=[_sds((r, c))] * 4, compiler_params=_params(("parallel",)),
    )(w, gparts, m, v)


SMALL = ["c_ctx", "ada_b", "norm1_g", "s5_lam_re", "s5_lam_im", "s5_log_dt", "s5_b_re", "s5_b_im", "s5_c_re",
         "s5_c_im", "s5_d", "conv_b", "conv_ln_g", "conv_ln_b", "norm2_g", "final_g"]


def _pack_small(parts):
    flat = jnp.concatenate([p.reshape(-1).astype(F32) for p in parts])
    return jnp.pad(flat, (0, SMALL_ROWS * D_MODEL - flat.shape[0])).reshape(SMALL_ROWS, D_MODEL)


def _unpack_small(packed, like):
    flat = packed.reshape(-1)
    out, off = [], 0
    for ref in like:
        out.append(flat[off:off + ref.size].reshape(ref.shape))
        off += ref.size
    return out


def kernel(x, c, ctx, c_ctx, ada_w, ada_b, norm1_g, w_in, s5_lam_re, s5_lam_im, s5_log_dt, s5_b_re, s5_b_im, s5_c_re, s5_c_im, s5_d, s5_w_glu, conv_w, conv_b, conv_ln_g, conv_ln_b, w_out, norm2_g, mlp_w1, mlp_w2, final_g, loss_target, m_c_ctx, m_ada_w, m_ada_b, m_norm1_g, m_w_in, m_s5_lam_re, m_s5_lam_im, m_s5_log_dt, m_s5_b_re, m_s5_b_im, m_s5_c_re, m_s5_c_im, m_s5_d, m_s5_w_glu, m_conv_w, m_conv_b, m_conv_ln_g, m_conv_ln_b, m_w_out, m_norm2_g, m_mlp_w1, m_mlp_w2, m_final_g, v_c_ctx, v_ada_w, v_ada_b, v_norm1_g, v_w_in, v_s5_lam_re, v_s5_lam_im, v_s5_log_dt, v_s5_b_re, v_s5_b_im, v_s5_c_re, v_s5_c_im, v_s5_d, v_s5_w_glu, v_conv_w, v_conv_b, v_conv_ln_g, v_conv_ln_b, v_w_out, v_norm2_g, v_mlp_w1, v_mlp_w2, v_final_g):
    weights = dict(c_ctx=c_ctx, ada_w=ada_w, ada_b=ada_b, norm1_g=norm1_g, w_in=w_in, s5_lam_re=s5_lam_re, s5_lam_im=s5_lam_im, s5_log_dt=s5_log_dt, s5_b_re=s5_b_re, s5_b_im=s5_b_im, s5_c_re=s5_c_re, s5_c_im=s5_c_im, s5_d=s5_d, s5_w_glu=s5_w_glu, conv_w=conv_w, conv_b=conv_b, conv_ln_g=conv_ln_g, conv_ln_b=conv_ln_b, w_out=w_out, norm2_g=norm2_g, mlp_w1=mlp_w1, mlp_w2=mlp_w2, final_g=final_g)
    mom1 = dict(c_ctx=m_c_ctx, ada_w=m_ada_w, ada_b=m_ada_b, norm1_g=m_norm1_g, w_in=m_w_in, s5_lam_re=m_s5_lam_re, s5_lam_im=m_s5_lam_im, s5_log_dt=m_s5_log_dt, s5_b_re=m_s5_b_re, s5_b_im=m_s5_b_im, s5_c_re=m_s5_c_re, s5_c_im=m_s5_c_im, s5_d=m_s5_d, s5_w_glu=m_s5_w_glu, conv_w=m_conv_w, conv_b=m_conv_b, conv_ln_g=m_conv_ln_g, conv_ln_b=m_conv_ln_b, w_out=m_w_out, norm2_g=m_norm2_g, mlp_w1=m_mlp_w1, mlp_w2=m_mlp_w2, final_g=m_final_g)
    mom2 = dict(c_ctx=v_c_ctx, ada_w=v_ada_w, ada_b=v_ada_b, norm1_g=v_norm1_g, w_in=v_w_in, s5_lam_re=v_s5_lam_re, s5_lam_im=v_s5_lam_im, s5_log_dt=v_s5_log_dt, s5_b_re=v_s5_b_re, s5_b_im=v_s5_b_im, s5_c_re=v_s5_c_re, s5_c_im=v_s5_c_im, s5_d=v_s5_d, s5_w_glu=v_s5_w_glu, conv_w=v_conv_w, conv_b=v_conv_b, conv_ln_g=v_conv_ln_g, conv_ln_b=v_conv_ln_b, w_out=v_w_out, norm2_g=v_norm2_g, mlp_w1=v_mlp_w1, mlp_w2=v_mlp_w2, final_g=v_final_g)
    order = list(weights)

    me = 4 * lax.axis_index("x") + 2 * lax.axis_index("y") + lax.axis_index("c")
    xs, cs, tgt = x[0], ctx[0], loss_target[0]
    n_lat_rows, n_ctx_rows = xs.shape[0], cs.shape[0]
    n_rows = n_lat_rows + n_ctx_rows
    n_lat = n_lat_rows // ROW_BLOCK
    ada_cols = ada_w.shape[2]

    (w_in_g, c_all), _ = _exchange("gather_w_in", [w_in[0].astype(BF16), c], [True] * 2)
    w_in_full = jnp.transpose(w_in_g, (1, 0, 2)).reshape(D_MODEL, IN_COLS)
    c_all = c_all.reshape(NDEV, D_MODEL)

    cond_fwd = jnp.concatenate([c_all, c_ctx[None], jnp.zeros((7, D_MODEL), F32)])
    ada_b_loc = lax.dynamic_slice(ada_b, (0, me * ada_cols), (1, ada_cols))
    (mod_g,), mod_token = _exchange("gather_mod", [_ada_fwd(cond_fwd, ada_w[0], ada_b_loc)], [True])
    mixer_w = [s5_w_glu[0].astype(BF16), conv_w[0] + mod_token[0:1, 0:1], w_out[0].astype(BF16)]
    mixer_send, mixer_recv, mixer_src, mixer_land, mixer_token = _exchange_start("gather_mixer_start", mixer_w, [True] * 3)
    mlp_w = [mlp_w1[0].astype(BF16), mlp_w2[0].astype(BF16) + mixer_token[0:1, 0:1].astype(BF16)]
    mlpw_send, mlpw_recv, mlpw_src, mlpw_land, mlpw_token = _exchange_start("gather_mlp_start", mlp_w, [True] * 2)
    mod_rows = jnp.transpose(mod_g, (1, 0, 2)).reshape(16, 6 * D_MODEL) + mlpw_token[0:1, 0:1]
    mod = lax.dynamic_slice(mod_rows, (me, 0), (1, 6 * D_MODEL)).reshape(6, D_MODEL)
    modc = mod_rows[8, :2 * D_MODEL].reshape(2, D_MODEL)
    sh1, sc1, g1, sh2, sc2, g2 = [mod[i:i + 1] for i in range(6)]

    a_all = _prenorm("prenorm1", xs, cs, norm1_g, jnp.stack([mod[0:2], modc]))
    (z_all,) = _matmul("in_proj", a_all, w_in_full, "nn", (n_rows, IN_COLS, D_MODEL), (ROW_BLOCK, IN_COLS, D_MODEL),
                       [((n_rows, IN_COLS), F32)])

    lam_re, lam_im = s5_lam_re[0].reshape(2, 1, NSTATE), s5_lam_im[0].reshape(2, 1, NSTATE)
    ldt = jnp.repeat(s5_log_dt[0], S5_STATE, axis=-1).reshape(2, 1, NSTATE)
    bt_re = jnp.transpose(s5_b_re[0], (0, 3, 1, 2)).reshape(2, S5_GROUP, NSTATE)
    bt_im = jnp.transpose(s5_b_im[0], (0, 3, 1, 2)).reshape(2, S5_GROUP, NSTATE)
    groups_per_block = S5_GROUPS // S5_BLOCKS
    ct_re = jnp.tile(s5_c_re[0].reshape(2, S5_WIDTH, S5_STATE), (1, 1, groups_per_block))
    ct_im = jnp.tile(s5_c_im[0].reshape(2, S5_WIDTH, S5_STATE), (1, 1, groups_per_block))
    d_skip = s5_d[0].reshape(1, S5_WIDTH)
    perm = _segment_permutation()
    perm_t = perm.T
    disc, states, y_dir = [], [], []
    for d in range(2):
        disc.append(_s5_discretise(f"s5_disc{d}", d == 0, lam_re[d], lam_im[d], ldt[d], bt_re[d], bt_im[d], ct_re[d], ct_im[d]))
        _, tab, _, bmat, cmat = disc[d]
        s, yd = _s5_scan_fwd(f"s5_scan_fwd{d}", d == 0, z_all, bmat, cmat, tab, perm, perm_t)
        states.append(s)
        y_dir.append(yd)
    mixer_own, mixer_landed = _exchange_wait("gather_mixer_wait", mixer_send, mixer_recv, mixer_src, mixer_land,
                                             [True] * 3, y_dir[1])
    glu_g, conv_w_g, w_out_g = [_with_own(l, o, me) for l, o in zip(mixer_landed, mixer_own)]
    glu_full = glu_g.reshape(S5_WIDTH, S5_WIDTH)
    conv_w_full = jnp.transpose(conv_w_g, (1, 0, 2)).reshape(CONV_K, CONV_WIDTH)
    w_out_full = w_out_g.reshape(D_MODEL, D_MODEL)
    ycat = _glu_fwd(z_all, y_dir[0], y_dir[1], d_skip, glu_full, n_lat_rows)

    hh_pad = _conv_gate(z_all, n_lat_rows)
    hc, ycat = _conv_fwd(hh_pad, conv_w_full, conv_b, conv_ln_g, conv_ln_b, ycat, n_lat_rows)

    tm = min(1024, n_lat_rows)
    tm_e = min(512, n_lat_rows)
    w1_cols = D_FF // NDEV
    row_vec = lambda tn: pl.BlockSpec((1, tn), lambda i, j, k: (0, j))
    out_tile = lambda t_m, t_n: pl.BlockSpec((t_m, t_n), lambda i, j, k: (i, j))
    full_rows = ((n_lat_rows, D_MODEL), F32)
    sums = ((n_lat_rows // tm_e, SUBLANES, D_MODEL), F32)
    sums_spec = pl.BlockSpec((None, SUBLANES, D_MODEL), lambda i, j, k: (i, 0, 0))
    vec = lambda v: (v, row_vec(D_MODEL))
    mix, h1, a2 = _matmul("out_proj", ycat, w_out_full, "nn", (n_lat_rows, D_MODEL, D_MODEL), (tm_e, D_MODEL, D_MODEL),
                          [full_rows, full_rows, ((n_lat_rows, D_MODEL), BF16)], epi=_epi_residual_prenorm,
                          epi_extra=[(xs, out_tile(tm_e, D_MODEL)), vec(g1), vec(norm2_g), vec(sc2), vec(sh2)])
    mlpw_own, mlpw_landed = _exchange_wait("gather_mlp_wait", mlpw_send, mlpw_recv, mlpw_src, mlpw_land, [True] * 2, a2)
    w1_g, w2_g = [_with_own(l, o, me) for l, o in zip(mlpw_landed, mlpw_own)]
    w2_full = w2_g.reshape(D_FF, D_MODEL)
    tm_up = min(2048, n_lat_rows)
    (f,) = _matmul("mlp_up", a2, w1_g, "nn", (n_lat_rows, D_FF, D_MODEL), (tm_up, w1_cols, D_MODEL),
                   [((n_lat_rows, D_FF), BF16)], b_spec=pl.BlockSpec((None, D_MODEL, w1_cols), lambda i, j, k: (j, 0, 0)))
    sq_relu = lambda t: jnp.square(jnp.maximum(t, 0.0))
    mlp_out, d_h2, dm2, err_sums, d_final_g8 = _matmul(
        "mlp_down", f, w2_full, "nn", (n_lat_rows, D_MODEL, D_FF), (tm_e, D_MODEL, 1024),
        [full_rows, full_rows, ((n_lat_rows, D_MODEL), BF16), sums, sums], a_fn=sq_relu, epi=_epi_residual_loss,
        epi_extra=[(h1, out_tile(tm_e, D_MODEL)), vec(g2), (tgt, out_tile(tm_e, D_MODEL)), vec(final_g[None])],
        out_specs=[out_tile(tm_e, D_MODEL)] * 3 + [sums_spec] * 2)
    loss = lax.psum(0.5 / D_MODEL * jnp.sum(err_sums), ("x", "y", "c"))

    (d_f,) = _matmul("mlp_down_dx", dm2, w2_full, "nt", (n_lat_rows, D_FF, D_MODEL), (tm, 512, D_MODEL),
                     [((n_lat_rows, D_FF), BF16)],
                     epi=lambda acc, ft: (acc * 2.0 * jnp.maximum(ft.astype(F32), 0.0),), epi_extra=[(f, out_tile(tm, 512))])
    (g_w2,) = _matmul("mlp_down_dw", f, dm2, "tn", (D_FF, D_MODEL, n_lat_rows), (1024, D_MODEL, tm),
                      [((D_FF, D_MODEL), F32)], a_fn=sq_relu)
    (g_w1,) = _matmul("mlp_up_dw", a2, d_f, "tn", (D_MODEL, D_FF, n_lat_rows), (D_MODEL, w1_cols, tm),
                      [((NDEV, D_MODEL, w1_cols), F32)],
                      out_specs=[pl.BlockSpec((None, D_MODEL, w1_cols), lambda i, j, k: (j, 0, 0))])
    mlp_send, mlp_recv, mlp_src, mlp_land, mlp_token = _exchange_start(
        "scatter_mlp_start", [g_w1, g_w2.reshape(NDEV, D_FF // NDEV, D_MODEL)], [False] * 2)
    d_h1, dm1, *sums2 = _matmul(
        "mlp_up_dx", d_f, w1_g, "nt", (n_lat_rows, D_MODEL, D_FF), (tm_e, D_MODEL, w1_cols),
        [full_rows, ((n_lat_rows, D_MODEL), BF16)] + [sums] * 4, epi=_epi_norm_bwd,
        epi_extra=[(h1, out_tile(tm_e, D_MODEL)), (d_h2, out_tile(tm_e, D_MODEL)), (mlp_out, out_tile(tm_e, D_MODEL)),
                   vec(norm2_g), vec(sc2 + mlp_token[0:1, 0:1]), vec(g1)],
        b_spec=pl.BlockSpec((None, D_MODEL, w1_cols), lambda i, j, k: (k, 0, 0)),
        out_specs=[out_tile(tm_e, D_MODEL)] * 2 + [sums_spec] * 4)

    (d_ycat,) = _matmul("out_proj_dx", dm1, w_out_full, "nt", (n_lat_rows, D_MODEL, D_MODEL), (tm, D_MODEL, D_MODEL),
                        [((n_lat_rows, D_MODEL), F32)])
    (g_w_out,) = _matmul("out_proj_dw", ycat, dm1, "tn", (D_MODEL, D_MODEL, n_lat_rows), (D_MODEL, D_MODEL, 512),
                         [((D_MODEL, D_MODEL), F32)])

    dy, g_glu, dd8 = _glu_bwd(d_ycat, z_all, y_dir[0], y_dir[1], d_skip, glu_full, n_lat_rows)
    proj_send, proj_recv, proj_src, proj_land, proj_token = _exchange_start(
        "scatter_proj_start",
        [g_w_out.reshape(NDEV, D_MODEL // NDEV, D_MODEL), g_glu.reshape(NDEV, S5_WIDTH // NDEV, S5_WIDTH)], [False] * 2)
    perm = perm + proj_token[0:1, 0:1].astype(BF16)
    du, g_lam_re, g_lam_im, g_ldt, g_b_re, g_b_im, g_c_re, g_c_im = [], [], [], [], [], [], [], []

    def diag(mat):
        return jnp.diagonal(mat.reshape(S5_BLOCKS, groups_per_block, S5_GROUP, 2, groups_per_block, S5_STATE), axis1=1, axis2=4)

    for d in range(2):
        _, _, adj, bmat, cmat = disc[d]
        du_d, d_bmat, d_cmat, d_abar8 = _s5_scan_bwd(f"s5_scan_bwd{d}", d == 0, dy, z_all, states[d], bmat, cmat, adj,
                                                     perm, perm_t)
        du.append(du_d)
        d_bbar = jnp.transpose(diag(d_bmat), (2, 1, 0, 4, 3)).reshape(2 * S5_GROUP, NSTATE)
        d_c = jnp.transpose(diag(d_cmat), (2, 0, 4, 1, 3)).reshape(2, S5_GROUPS, S5_GROUP, S5_STATE)
        d_lam8, d_bt = _s5_discretise_bwd(f"s5_disc_bwd{d}", lam_re[d], lam_im[d], ldt[d], bt_re[d], bt_im[d], d_abar8, d_bbar)
        g_lam_re.append(d_lam8[0].reshape(S5_GROUPS, S5_STATE))
        g_lam_im.append(d_lam8[1].reshape(S5_GROUPS, S5_STATE))
        g_ldt.append(d_lam8[2].reshape(S5_GROUPS, S5_STATE).sum(axis=-1))
        to_gph = lambda t: jnp.transpose(t.reshape(S5_GROUP, S5_GROUPS, S5_STATE), (1, 2, 0))
        g_b_re.append(to_gph(d_bt[:S5_GROUP]))
        g_b_im.append(to_gph(d_bt[S5_GROUP:]))
        g_c_re.append(d_c[0])
        g_c_im.append(-d_c[1])

    dhc_pad, conv_sums = _conv_bwd_norm(d_ycat, hc, conv_ln_g, conv_ln_b, n_lat_rows)
    d_v, d_gate, g_conv_w8 = _conv_bwd_taps(dhc_pad, hh_pad, z_all, conv_w_full, n_lat_rows)

    dz_all = _dz_assemble(du[0], du[1], dy, d_skip, d_v, d_gate, n_lat)
    (g_w_in_full,) = _matmul("in_proj_dw", a_all, dz_all, "tn", (D_MODEL, IN_COLS, n_rows), (D_MODEL, IN_COLS, ROW_BLOCK),
                             [((D_MODEL, IN_COLS), F32)])
    g_w_in_parts = jnp.transpose(g_w_in_full.reshape(D_MODEL, NDEV, IN_COLS // NDEV), (1, 0, 2)).astype(BF16)
    win_send, win_recv, win_src, win_land, win_token = _exchange_start("scatter_w_in_start", [g_w_in_parts], [False])
    (d_a_all,) = _matmul("in_proj_dx", dz_all, w_in_full + win_token[0:1, 0:1].astype(BF16), "nt",
                         (n_rows, D_MODEL, IN_COLS), (ROW_BLOCK, D_MODEL, IN_COLS), [((n_rows, D_MODEL), F32)])
    grad_x, sums1 = _norm_bwd("norm1_bwd", xs, d_a_all, 0, norm1_g, sc1, res=d_h1, aux=mix)
    (sums1c,) = _norm_bwd("norm1_bwd_ctx", cs, d_a_all, n_lat, norm1_g, modc[1:2])

    s1, s1c, s2 = sums1.sum(axis=1), sums1c.sum(axis=1), [p.sum(axis=(0, 1)) for p in sums2]
    d_mod = jnp.concatenate([s1[0], s1[1], s1[3], s2[0], s2[1], s2[3]])
    d_modc = jnp.concatenate([s1c[0], s1c[1], jnp.zeros((4 * D_MODEL,), F32)])
    (dmod_g,), _ = _exchange("gather_dmod", [jnp.stack([d_mod, d_modc])], [True])
    dmod16 = jnp.concatenate([dmod_g[:, 0], dmod_g[:, 1]])
    dmod16_loc = lax.dynamic_slice(dmod16, (0, me * ada_cols), (16, ada_cols))
    cond_bwd = jnp.concatenate([c_all, jnp.broadcast_to(c_ctx[None], (NDEV, D_MODEL))])
    g_ada_w, g_c_ctx8 = _ada_bwd(cond_bwd, dmod16_loc, ada_w[0], c_ctx[None])

    small_parts = dict(
        c_ctx=g_c_ctx8[0], ada_b=d_mod + d_modc, norm1_g=s1[2] + s1c[2],
        s5_lam_re=jnp.stack(g_lam_re), s5_lam_im=jnp.stack(g_lam_im), s5_log_dt=jnp.stack(g_ldt),
        s5_b_re=jnp.stack(g_b_re), s5_b_im=jnp.stack(g_b_im), s5_c_re=jnp.stack(g_c_re), s5_c_im=jnp.stack(g_c_im),
        s5_d=dd8.sum(axis=0), conv_b=conv_sums[0].sum(axis=0), conv_ln_g=conv_sums[1].sum(axis=0),
        conv_ln_b=conv_sums[2].sum(axis=0), norm2_g=s2[2], final_g=d_final_g8.sum(axis=(0, 1)))
    small_g = _pack_small([small_parts[n] for n in SMALL]).reshape(NDEV, SMALL_ROWS // NDEV, D_MODEL)
    g_conv_w_parts = jnp.transpose(g_conv_w8.sum(axis=1).reshape(CONV_K, NDEV, CONV_WIDTH // NDEV), (1, 0, 2))
    (p_conv_w, p_small), _ = _exchange("scatter_grads", [g_conv_w_parts, small_g], [False] * 2)
    (small_all,), _ = _exchange("gather_small", [_sum_parts(p_small)], [True])
    small_all = small_all.reshape(1, SMALL_ROWS, D_MODEL)

    def own_chunk(src):
        return lax.dynamic_index_in_dim(src, me, 0, keepdims=False)

    win_src, win_landed = _exchange_wait("scatter_w_in_wait", win_send, win_recv, win_src, win_land, [False], small_all)
    p_w_in = _with_own(win_landed[0], own_chunk(win_src[0]), me)
    mlp_src, mlp_landed = _exchange_wait("scatter_mlp_wait", mlp_send, mlp_recv, mlp_src, mlp_land, [False] * 2, small_all)
    p_w1, p_w2 = [_with_own(l, own_chunk(s), me) for l, s in zip(mlp_landed, mlp_src)]
    proj_src, proj_landed = _exchange_wait("scatter_proj_wait", proj_send, proj_recv, proj_src, proj_land, [False] * 2,
                                           small_all)
    p_w_out, p_glu = [_with_own(l, own_chunk(s), me) for l, s in zip(proj_landed, proj_src)]

    res = {}
    big = dict(ada_w=g_ada_w[None], w_in=p_w_in, s5_w_glu=p_glu, conv_w=p_conv_w, w_out=p_w_out, mlp_w1=p_w1, mlp_w2=p_w2)
    for name, parts in big.items():
        outs = _adamw("adamw_" + name, weights[name][0], parts, mom1[name][0], mom2[name][0])
        res[name] = [o[None] for o in outs]
    small_like = [weights[n] for n in SMALL]
    outs = _adamw("adamw_small", _pack_small(small_like), small_all, _pack_small([mom1[n] for n in SMALL]),
                  _pack_small([mom2[n] for n in SMALL]))
    unpacked = [_unpack_small(o, small_like) for o in outs]
    for i, name in enumerate(SMALL):
        res[name] = [u[i] for u in unpacked]

    return (loss, grad_x[None], *[res[n][0] for n in order], *[res[n][1] for n in order],
            *[res[n][2] for n in order], *[res[n][3] for n in order])
```

```python
import functools

import jax
import jax.numpy as jnp
from jax import lax
from jax.experimental import pallas as pl
from jax.experimental.pallas import tpu as pltpu

F32 = jnp.float32
BF16 = jnp.bfloat16
MESH = pl.DeviceIdType.MESH
ANY = pl.BlockSpec(memory_space=pl.ANY)

NDEV = 8
D_MODEL = 1024
GRID_W = 64
S5_WIDTH = 512
S5_GROUP = 16
S5_GROUPS = 32
S5_STATE = 64
NSTATE = S5_GROUPS * S5_STATE
CONV_WIDTH = 512
CONV_K = 31
IN_COLS = S5_WIDTH + 2 * CONV_WIDTH
D_FF = 4 * D_MODEL
EPS_RMS = 1e-6
EPS_LN = 1e-5
ADAM_LR = 0.001
ADAM_B1 = 0.9
ADAM_B2 = 0.999
ADAM_EPS = 1e-08
ADAM_WD = 0.01
ADAM_STEP = 10

SUBLANES = 8
LANES = 128
ROW_BLOCK = 256
SCAN_LANES = 512
SEGMENTS = SUBLANES
STEPS = ROW_BLOCK // SEGMENTS
S5_BLOCKS = 4
S5_BLOCK_WIDTH = S5_WIDTH // S5_BLOCKS
CONV_ROWS = 64
VMEM_LIMIT = 48 * 1024 * 1024
SMALL_ROWS = 320


def _params(sem=None):
    kw = dict(vmem_limit_bytes=VMEM_LIMIT)
    if sem is not None:
        kw["dimension_semantics"] = sem
    return pltpu.CompilerParams(**kw)


def _sds(shape, dtype=F32):
    return jax.ShapeDtypeStruct(tuple(shape), dtype)


def _fold8(x):
    return x.reshape(x.shape[0] // SUBLANES, SUBLANES, x.shape[1]).sum(axis=0)


def _sigmoid(x):
    return 1.0 / (1.0 + jnp.exp(-x))


def _silu(x):
    return x * _sigmoid(x)


def _dsilu(x):
    s = _sigmoid(x)
    return s * (1.0 + x * (1.0 - s))


_GELU_C = 0.7978845608028654


def _gelu(x):
    return 0.5 * x * (1.0 + jnp.tanh(_GELU_C * (x + 0.044715 * x * x * x)))


def _dgelu(x):
    t = jnp.tanh(_GELU_C * (x + 0.044715 * x * x * x))
    return 0.5 * (1.0 + t) + 0.5 * x * (1.0 - t * t) * _GELU_C * (1.0 + 3.0 * 0.044715 * x * x)


def _rms(x):
    rstd = lax.rsqrt(jnp.mean(x * x, axis=-1, keepdims=True) + EPS_RMS)
    return x * rstd, rstd


def _epi_residual_prenorm(acc, res, gate, gain, scale, shift):
    h = res + gate * acc
    xh, _ = _rms(h)
    return acc, h, (xh * gain) * (1.0 + scale) + shift


def _epi_residual_loss(acc, res, gate, target, gain):
    h = res + gate * acc
    xh, rstd = _rms(h)
    err = xh * gain - target
    dy = err * (1.0 / h.shape[-1])
    dxh = dy * gain
    dh = rstd * (dxh - xh * jnp.mean(dxh * xh, axis=-1, keepdims=True))
    return acc, dh, dh * gate, _fold8(err * err), _fold8(dy * xh)


def _epi_norm_bwd(d_act, x, res, aux, gain, scale, gate):
    xh, rstd = _rms(x)
    dn = d_act * (1.0 + scale)
    dxh = dn * gain
    dx = res + rstd * (dxh - xh * jnp.mean(dxh * xh, axis=-1, keepdims=True))
    return dx, dx * gate, _fold8(d_act), _fold8(d_act * (xh * gain)), _fold8(dn * xh), _fold8(res * aux)


def _dot(a, b, mode):
    dims = {"nn": (((1,), (0,)), ((), ())), "nt": (((1,), (1,)), ((), ())), "tn": (((0,), (0,)), ((), ()))}[mode]
    return lax.dot_general(a, b, dims, preferred_element_type=F32)


def _peers(x, y, c):
    out = []
    for k in range(1, NDEV):
        px = 1 - x if k & 4 else x
        py = 1 - y if k & 2 else y
        pc = 1 - c if k & 1 else c
        out.append(((px, py, pc), 4 * px + 2 * py + pc))
    return out


def _exchange_copies(src, land, send_sems, recv_sems, gather):
    x, y, c = lax.axis_index("x"), lax.axis_index("y"), lax.axis_index("c")
    me = 4 * x + 2 * y + c
    out = []
    for a in range(len(src)):
        for k, (peer, plin) in enumerate(_peers(x, y, c)):
            chunk = src[a] if gather[a] else src[a].at[plin]
            sems = dict(send_sem=send_sems.at[a * (NDEV - 1) + k], recv_sem=recv_sems.at[a * (NDEV - 1) + k],
                        device_id=peer, device_id_type=MESH)
            out.append((pltpu.make_async_remote_copy(src_ref=chunk, dst_ref=land[a].at[me], **sems),
                        pltpu.make_async_remote_copy(src_ref=chunk, dst_ref=land[a].at[plin], **sems)))
    return out


def _exchange(name, srcs, gather):
    n = len(srcs)
    outs = [_sds(((NDEV,) + s.shape) if g else s.shape, s.dtype) for s, g in zip(srcs, gather)]

    def body(*refs):
        src, dst, token = refs[:n], refs[n:2 * n], refs[2 * n]
        send_sems, recv_sems, local_sems = refs[2 * n + 1:]
        me = 4 * lax.axis_index("x") + 2 * lax.axis_index("y") + lax.axis_index("c")
        local = [pltpu.make_async_copy(src[a] if gather[a] else src[a].at[me], dst[a].at[me], local_sems.at[a])
                 for a in range(n)]
        for copy in local:
            copy.start()
        copies = _exchange_copies(src, dst, send_sems, recv_sems, gather)
        for copy, _ in copies:
            copy.start()
        token[...] = jnp.zeros_like(token)
        for copy, landing in copies:
            copy.wait_send()
            landing.wait_recv()
        for copy in local:
            copy.wait()

    nsem = n * (NDEV - 1)
    out = pl.pallas_call(
        body, name=name, out_shape=outs + [_sds((SUBLANES, LANES))], in_specs=[ANY] * n,
        out_specs=[ANY] * n + [pl.BlockSpec(memory_space=pltpu.VMEM)],
        scratch_shapes=[pltpu.SemaphoreType.DMA((nsem,)), pltpu.SemaphoreType.DMA((nsem,)), pltpu.SemaphoreType.DMA((n,))],
    )(*srcs)
    return out[:n], out[n]


HBM = pl.BlockSpec(memory_space=pltpu.HBM)
SEM = pl.BlockSpec(memory_space=pltpu.SEMAPHORE)
EFFECT = pltpu.SideEffectType.DATAFLOW_SIDE_EFFECTING


def _exchange_start(name, srcs, gather):
    n = len(srcs)
    lands = [lax.empty(((NDEV,) + s.shape) if g else s.shape, s.dtype) for s, g in zip(srcs, gather)]

    def body(*refs):
        src, land = refs[:n], refs[n:2 * n]
        send_sems, recv_sems = refs[2 * n], refs[2 * n + 1]
        token = refs[-1]
        for copy, _ in _exchange_copies(src, land, send_sems, recv_sems, gather):
            copy.start()
        token[...] = jnp.zeros_like(token)

    hbm = lambda v: pltpu.HBM(v.shape, v.dtype)
    nsem = n * (NDEV - 1)
    out = pl.pallas_call(
        body, name=name,
        out_shape=(pltpu.SemaphoreType.DMA((nsem,)), pltpu.SemaphoreType.DMA((nsem,)), *[hbm(v) for v in srcs],
                   *[hbm(v) for v in lands], _sds((SUBLANES, LANES))),
        in_specs=[HBM] * (2 * n), out_specs=(SEM, SEM, *([HBM] * (2 * n)), pl.BlockSpec(memory_space=pltpu.VMEM)),
        input_output_aliases={i: 2 + i for i in range(2 * n)},
        compiler_params=pltpu.CompilerParams(has_side_effects=EFFECT),
    )(*[pltpu.with_memory_space_constraint(v, pltpu.HBM) for v in list(srcs) + lands])
    return out[0], out[1], out[2:2 + n], out[2 + n:2 + 2 * n], out[-1]


def _exchange_wait(name, send_sems, recv_sems, srcs, lands, gather, after):
    n = len(srcs)

    def body(*refs):
        src, land = refs[:n], refs[n:2 * n]
        send_ref, recv_ref = refs[2 * n], refs[2 * n + 1]
        for copy, landing in _exchange_copies(src, land, send_ref, recv_ref, gather):
            copy.wait_send()
            landing.wait_recv()

    hbm = lambda v: pltpu.HBM(v.shape, v.dtype)
    out = pl.pallas_call(
        body, name=name, out_shape=[hbm(v) for v in list(srcs) + list(lands)],
        in_specs=[HBM] * (2 * n) + [SEM, SEM, ANY], out_specs=[HBM] * (2 * n),
        input_output_aliases={i: i for i in range(2 * n)},
        compiler_params=pltpu.CompilerParams(has_side_effects=EFFECT),
    )(*srcs, *lands, send_sems, recv_sems, after)
    return out[:n], out[n:]


def _with_own(landed, own, me):
    return lax.dynamic_update_slice(landed, own[None], (me,) + (0,) * own.ndim)


def _matmul(name, a, b, mode, mnk, tiles, outs, a_spec=None, b_spec=None, a_fn=None, a_extra=(),
            epi=None, epi_extra=(), out_specs=None):
    m_, n_, k_ = mnk
    tm, tn, tk = tiles
    nk = k_ // tk
    if a_spec is None:
        a_spec = (pl.BlockSpec((tk, tm), lambda i, j, k: (k, i)) if mode == "tn"
                  else pl.BlockSpec((tm, tk), lambda i, j, k: (i, k)))
    if b_spec is None:
        b_spec = (pl.BlockSpec((tn, tk), lambda i, j, k: (j, k)) if mode == "nt"
                  else pl.BlockSpec((tk, tn), lambda i, j, k: (k, j)))
    if out_specs is None:
        out_specs = [pl.BlockSpec((tm, tn), lambda i, j, k: (i, j)) for _ in outs]
    na, ne, no = len(a_extra), len(epi_extra), len(outs)

    def body(*refs):
        a_ref, b_ref = refs[0], refs[1]
        ax = refs[2:2 + na]
        ex = refs[2 + na:2 + na + ne]
        o = refs[2 + na + ne:2 + na + ne + no]

        def finish(res):
            res = epi(res, *[r[...] for r in ex]) if epi is not None else (res,)
            for ref, val in zip(o, res):
                ref[...] = val.astype(ref.dtype)

        at = a_ref[...]
        if a_fn is not None:
            at = a_fn(at, *[r[...] for r in ax])
        part = _dot(at.astype(BF16), b_ref[...].astype(BF16), mode)
        if nk == 1:
            finish(part)
            return
        acc = refs[-1]
        k = pl.program_id(2)

        @pl.when(k == 0)
        def _():
            acc[...] = part

        @pl.when(k > 0)
        def _():
            acc[...] += part

        @pl.when(k == nk - 1)
        def _():
            finish(acc[...])

    return pl.pallas_call(
        body, name=name, grid=(m_ // tm, n_ // tn, nk),
        in_specs=[a_spec, b_spec] + [s for _, s in a_extra] + [s for _, s in epi_extra],
        out_specs=out_specs, out_shape=[_sds(s, d) for s, d in outs],
        scratch_shapes=[pltpu.VMEM((tm, tn), F32)] if nk > 1 else [],
        compiler_params=_params(("parallel", "parallel", "arbitrary")),
    )(a, b, *[x for x, _ in a_extra], *[x for x, _ in epi_extra])


def _prenorm(name, x, ctx, gain, shsc):
    n_lat = x.shape[0] // ROW_BLOCK
    n_ctx = 0 if ctx is None else ctx.shape[0] // ROW_BLOCK
    d = x.shape[1]

    def norm(src, g_ref, m_ref, o_ref):
        xv = src[...]
        xh = xv * lax.rsqrt(jnp.mean(xv * xv, axis=-1, keepdims=True) + EPS_RMS)
        o_ref[...] = ((xh * g_ref[...]) * (1.0 + m_ref[1:2, :]) + m_ref[0:1, :]).astype(o_ref.dtype)

    def body(*refs):
        if ctx is None:
            x_ref, g_ref, m_ref, o_ref = refs
            norm(x_ref, g_ref, m_ref, o_ref)
        else:
            x_ref, c_ref, g_ref, m_ref, o_ref = refs
            i = pl.program_id(0)

            @pl.when(i < n_lat)
            def _():
                norm(x_ref, g_ref, m_ref, o_ref)

            @pl.when(i >= n_lat)
            def _():
                norm(c_ref, g_ref, m_ref, o_ref)

    in_specs = [pl.BlockSpec((ROW_BLOCK, d), lambda i: (jnp.minimum(i, n_lat - 1), 0))]
    args = [x]
    if ctx is not None:
        in_specs.append(pl.BlockSpec((ROW_BLOCK, d), lambda i: (jnp.maximum(i - n_lat, 0), 0)))
        args.append(ctx)
    in_specs += [pl.BlockSpec((1, d), lambda i: (0, 0)),
                 pl.BlockSpec((None, 2, d), lambda i: (jnp.minimum(i // n_lat, 1), 0, 0))]
    args += [gain, shsc]
    return pl.pallas_call(
        body, name=name, grid=(n_lat + n_ctx,), in_specs=in_specs,
        out_specs=pl.BlockSpec((ROW_BLOCK, d), lambda i: (i, 0)),
        out_shape=_sds(((n_lat + n_ctx) * ROW_BLOCK, d), BF16),
        compiler_params=_params(("parallel",)),
    )(*args)


def _norm_bwd(name, x, d_act, d_act_row0, gain, scale, res=None, aux=None, gate=None):
    rows, d = x.shape
    nb = rows // ROW_BLOCK
    has_res = res is not None
    has_gate = gate is not None

    def body(*refs):
        if has_gate:
            x_ref, da_ref, g_ref, sc_ref, r_ref, aux_ref, gate_ref, dx_ref, dm_ref, sums = refs
        elif has_res:
            x_ref, da_ref, g_ref, sc_ref, r_ref, aux_ref, dx_ref, sums = refs
        else:
            x_ref, da_ref, g_ref, sc_ref, sums = refs
        i = pl.program_id(0)

        @pl.when(i == 0)
        def _():
            sums[...] = jnp.zeros_like(sums)

        xv, da = x_ref[...], da_ref[...]
        rstd = lax.rsqrt(jnp.mean(xv * xv, axis=-1, keepdims=True) + EPS_RMS)
        xh = xv * rstd
        g = g_ref[...]
        dn = da * (1.0 + sc_ref[...])
        sums[0] += _fold8(da)
        sums[1] += _fold8(da * (xh * g))
        sums[2] += _fold8(dn * xh)
        if has_res:
            dxh = dn * g
            dx = rstd * (dxh - xh * jnp.mean(dxh * xh, axis=-1, keepdims=True))
            rv = r_ref[...]
            dx_ref[...] = rv + dx
            sums[3] += _fold8(rv * aux_ref[...])
            if has_gate:
                dm_ref[...] = ((rv + dx) * gate_ref[...]).astype(dm_ref.dtype)

    row = lambda i: (i, 0)
    vec = pl.BlockSpec((1, d), lambda i: (0, 0))
    in_specs = [pl.BlockSpec((ROW_BLOCK, d), row), pl.BlockSpec((ROW_BLOCK, d), lambda i: (i + d_act_row0, 0)), vec, vec]
    args = [x, d_act, gain, scale]
    out_shape = [_sds((4, SUBLANES, d))]
    out_specs = [pl.BlockSpec((4, SUBLANES, d), lambda i: (0, 0, 0))]
    if has_res:
        in_specs += [pl.BlockSpec((ROW_BLOCK, d), row), pl.BlockSpec((ROW_BLOCK, d), row)]
        args += [res, aux]
        if has_gate:
            in_specs.append(vec)
            args.append(gate)
            out_shape = [_sds((rows, d), BF16)] + out_shape
            out_specs = [pl.BlockSpec((ROW_BLOCK, d), row)] + out_specs
        out_shape = [_sds((rows, d))] + out_shape
        out_specs = [pl.BlockSpec((ROW_BLOCK, d), row)] + out_specs
    return pl.pallas_call(
        body, name=name, grid=(nb,), in_specs=in_specs, out_specs=out_specs, out_shape=out_shape,
        compiler_params=_params(("arbitrary",)),
    )(*args)


def _loss_head(h2, target, gain, gate):
    rows, d = h2.shape

    def body(h_ref, t_ref, g_ref, gate_ref, dh_ref, dm_ref, err_ref, dg_ref):
        i = pl.program_id(0)

        @pl.when(i == 0)
        def _():
            err_ref[...] = jnp.zeros_like(err_ref)
            dg_ref[...] = jnp.zeros_like(dg_ref)

        hv = h_ref[...]
        rstd = lax.rsqrt(jnp.mean(hv * hv, axis=-1, keepdims=True) + EPS_RMS)
        xh = hv * rstd
        g = g_ref[...]
        err = xh * g - t_ref[...]
        err_ref[...] += _fold8(err * err)
        dy = err * (1.0 / d)
        dg_ref[...] += _fold8(dy * xh)
        dxh = dy * g
        dh = rstd * (dxh - xh * jnp.mean(dxh * xh, axis=-1, keepdims=True))
        dh_ref[...] = dh
        dm_ref[...] = (dh * gate_ref[...]).astype(dm_ref.dtype)

    row = pl.BlockSpec((ROW_BLOCK, d), lambda i: (i, 0))
    acc = pl.BlockSpec((SUBLANES, d), lambda i: (0, 0))
    vec = pl.BlockSpec((1, d), lambda i: (0, 0))
    return pl.pallas_call(
        body, name="loss_head", grid=(rows // ROW_BLOCK,),
        in_specs=[row, row, vec, vec], out_specs=[row, row, acc, acc],
        out_shape=[_sds((rows, d)), _sds((rows, d), BF16), _sds((SUBLANES, d)), _sds((SUBLANES, d))],
        compiler_params=_params(("arbitrary",)),
    )(h2, target, gain, gate)


def _ada_fwd(cond16, ada_w_loc, ada_b_loc):
    cols = ada_w_loc.shape[1]

    def body(c_ref, w_ref, b_ref, o_ref):
        s = _silu(c_ref[...]).astype(BF16)
        o_ref[...] = _dot(s, w_ref[...].astype(BF16), "nn") + b_ref[...]

    return pl.pallas_call(body, name="ada_fwd", out_shape=_sds((16, cols)), compiler_params=_params())(
        cond16, ada_w_loc, ada_b_loc)


def _ada_bwd(cond16, dmod16, ada_w_loc, c_ctx_row):
    k_, cols = ada_w_loc.shape

    def body(c_ref, dm_ref, w_ref, cc_ref, gw_ref, gc_ref):
        s = _silu(c_ref[...]).astype(BF16)
        dm = dm_ref[...]
        gw_ref[...] = _dot(s, dm.astype(BF16), "tn")
        dmc = jnp.sum(dm[8:16, :], axis=0, keepdims=True)
        dmc8 = jnp.broadcast_to(dmc, (SUBLANES, cols)).astype(BF16)
        ds = _dot(dmc8, w_ref[...].astype(BF16), "nt")
        row = lax.broadcasted_iota(jnp.int32, ds.shape, 0)
        gc_ref[...] = jnp.where(row == 0, ds * _dsilu(cc_ref[...]), 0.0)

    return pl.pallas_call(body, name="ada_bwd", out_shape=[_sds((k_, cols)), _sds((SUBLANES, k_))],
                          compiler_params=_params())(cond16, dmod16, ada_w_loc, c_ctx_row)


def _cmul(a, b):
    return a[0] * b[0] - a[1] * b[1], a[0] * b[1] + a[1] * b[0]


def _disc(lam_re, lam_im, ldt):
    dt = jnp.exp(ldt)
    mag = jnp.exp(lam_re * dt)
    th = lam_im * dt
    a_re, a_im = mag * jnp.cos(th), mag * jnp.sin(th)
    den = lam_re * lam_re + lam_im * lam_im
    n_re = a_re - 1.0
    f_re = (n_re * lam_re + a_im * lam_im) / den
    f_im = (a_im * lam_re - n_re * lam_im) / den
    return dt, mag, th, a_re, a_im, den, n_re, f_re, f_im


def _block_diag_mask(shape):
    row = lax.broadcasted_iota(jnp.int32, shape, 0)
    col = lax.broadcasted_iota(jnp.int32, shape, 1)
    return lax.shift_right_logical(row, 4) == lax.shift_right_logical(col, 6)


TAB_A = 0
TAB_BIG = 1
TAB_SEG = 4
TAB_PW = 5
TAB_ROWS = TAB_PW + STEPS


def _s5_discretise(name, ascending, lam_re, lam_im, ldt, bt_re, bt_im, ct_re, ct_im):
    def write_tables(ref, pw, big, asc, sign):
        row = lax.broadcasted_iota(jnp.int32, (SUBLANES, NSTATE), 0)
        full = lambda v: jnp.broadcast_to(v, (SUBLANES, NSTATE))

        def put(t, p):
            ref[0, t] = full(p[0])
            ref[1, t] = full(sign * p[1])

        put(TAB_A, pw[0])
        for t in range(3):
            put(TAB_BIG + t, big[t])
        seg = [big[0]]
        for _ in range(SEGMENTS - 1):
            seg.append(_cmul(seg[-1], big[0]))
        seg_re = jnp.zeros((SUBLANES, NSTATE), F32)
        seg_im = jnp.zeros((SUBLANES, NSTATE), F32)
        for r in range(SEGMENTS):
            p = seg[r] if asc else seg[SEGMENTS - 1 - r]
            seg_re = jnp.where(row == r, p[0], seg_re)
            seg_im = jnp.where(row == r, sign * p[1], seg_im)
        ref[0, TAB_SEG] = seg_re
        ref[1, TAB_SEG] = seg_im
        for k in range(STEPS):
            put(TAB_PW + k, pw[k])

    def body(lr_ref, li_ref, ldt_ref, br_ref, bi_ref, cr_ref, ci_ref, bb_ref, tab_ref, adj_ref, bm_ref, cm_ref):
        _, _, _, a_re, a_im, _, _, f_re, f_im = _disc(lr_ref[...], li_ref[...], ldt_ref[...])
        bre, bim = br_ref[...], bi_ref[...]
        bb_re = f_re * bre - f_im * bim
        bb_im = f_re * bim + f_im * bre
        bb_ref[0:S5_GROUP, :] = bb_re
        bb_ref[S5_GROUP:2 * S5_GROUP, :] = bb_im
        pw = [(a_re, a_im)]
        for _ in range(STEPS - 1):
            pw.append(_cmul(pw[-1], (a_re, a_im)))
        big = [pw[STEPS - 1]]
        for _ in range(2):
            big.append(_cmul(big[-1], big[-1]))
        write_tables(tab_ref, pw, big, ascending, 1.0)
        write_tables(adj_ref, pw, big, not ascending, -1.0)
        half = NSTATE // S5_BLOCKS
        mask = _block_diag_mask((S5_BLOCK_WIDTH, half))
        tile = lambda v: jnp.broadcast_to(v[None], (S5_BLOCK_WIDTH // S5_GROUP, S5_GROUP, half)).reshape(S5_BLOCK_WIDTH, half)
        for c in range(S5_BLOCKS):
            cols = slice(c * half, (c + 1) * half)
            rows = slice(c * S5_BLOCK_WIDTH, (c + 1) * S5_BLOCK_WIDTH)
            bm_ref[c, :, 0:half] = jnp.where(mask, tile(bb_re[:, cols]), 0.0).astype(BF16)
            bm_ref[c, :, half:2 * half] = jnp.where(mask, tile(bb_im[:, cols]), 0.0).astype(BF16)
            cm_ref[c, :, 0:half] = jnp.where(mask, cr_ref[rows, :], 0.0).astype(BF16)
            cm_ref[c, :, half:2 * half] = jnp.where(mask, -ci_ref[rows, :], 0.0).astype(BF16)

    blocked = _sds((S5_BLOCKS, S5_BLOCK_WIDTH, 2 * NSTATE // S5_BLOCKS), BF16)
    return pl.pallas_call(
        body, name=name,
        out_shape=[_sds((2 * S5_GROUP, NSTATE)), _sds((2, TAB_ROWS, SUBLANES, NSTATE)),
                   _sds((2, TAB_ROWS, SUBLANES, NSTATE)), blocked, blocked],
        compiler_params=_params(),
    )(lam_re, lam_im, ldt, bt_re, bt_im, ct_re, ct_im)


def _s5_discretise_bwd(name, lam_re, lam_im, ldt, bt_re, bt_im, d_abar8, d_bbar):
    def body(lr_ref, li_ref, ldt_ref, br_ref, bi_ref, da_ref, db_ref, dl_ref, dbt_ref):
        lam_re, lam_im = lr_ref[...], li_ref[...]
        dt, mag, _, a_re, a_im, den, n_re, f_re, f_im = _disc(lam_re, lam_im, ldt_ref[...])
        bre, bim = br_ref[...], bi_ref[...]
        dbr, dbi = db_ref[0:S5_GROUP, :], db_ref[S5_GROUP:2 * S5_GROUP, :]
        dbt_ref[0:S5_GROUP, :] = f_re * dbr + f_im * dbi
        dbt_ref[S5_GROUP:2 * S5_GROUP, :] = f_re * dbi - f_im * dbr
        df_re = jnp.sum(bre * dbr + bim * dbi, axis=0, keepdims=True)
        df_im = jnp.sum(bre * dbi - bim * dbr, axis=0, keepdims=True)
        da = da_ref[...]
        da_re = jnp.sum(da[:, 0:NSTATE], axis=0, keepdims=True)
        da_im = jnp.sum(da[:, NSTATE:2 * NSTATE], axis=0, keepdims=True)
        da_re = da_re + (df_re * lam_re - df_im * lam_im) / den
        da_im = da_im + (df_re * lam_im + df_im * lam_re) / den
        ff = (f_re * df_re + f_im * df_im) * 2.0 / den
        d_lr = (df_re * n_re + df_im * a_im) / den - ff * lam_re
        d_li = (df_re * a_im - df_im * n_re) / den - ff * lam_im
        d_mag = (da_re * a_re + da_im * a_im) / mag
        d_th = da_im * a_re - da_re * a_im
        d_lr = d_lr + d_mag * mag * dt
        d_li = d_li + d_th * dt
        d_ldt = (d_mag * mag * lam_re + d_th * lam_im) * dt
        row = lax.broadcasted_iota(jnp.int32, (SUBLANES, NSTATE), 0)
        dl_ref[...] = jnp.where(row == 0, d_lr, jnp.where(row == 1, d_li, jnp.where(row == 2, d_ldt, 0.0)))

    return pl.pallas_call(
        body, name=name, out_shape=[_sds((SUBLANES, NSTATE)), _sds((2 * S5_GROUP, NSTATE))],
        compiler_params=_params(),
    )(lam_re, lam_im, ldt, bt_re, bt_im, d_abar8, d_bbar)


def _segment_permutation():
    rho = jnp.arange(ROW_BLOCK)
    src = STEPS * (rho % SEGMENTS) + rho // SEGMENTS
    return (src[:, None] == jnp.arange(ROW_BLOCK)[None, :]).astype(BF16)


def _permute_rows(perm_ref, v):
    return _dot(perm_ref[...], v, "nn").astype(BF16)


def _unpermute_rows(perm_t_ref, v):
    hi = v.astype(BF16)
    lo = (v - hi.astype(F32)).astype(BF16)
    return _dot(perm_t_ref[...], hi, "nn") + _dot(perm_t_ref[...], lo, "nn")


def _scan_chunk(x_ref, out_ref, tab_ref, carry_re, carry_im, ascending, pair_ref=None, acc_ref=None):
    w = SCAN_LANES
    half = NSTATE // S5_BLOCKS
    row = lax.broadcasted_iota(jnp.int32, (SUBLANES, w), 0)
    last = (SEGMENTS - 1) if ascending else 0

    def from_previous_segment(v, k, fill):
        if ascending:
            return jnp.where(row >= k, pltpu.roll(v, k, 0), fill)
        return jnp.where(row < SEGMENTS - k, pltpu.roll(v, SEGMENTS - k, 0), fill)

    def tile_rows(k):
        return pl.ds(pl.multiple_of((k if ascending else STEPS - 1 - k) * SUBLANES, SUBLANES), SUBLANES)

    for j in range(NSTATE // w):
        n_l = pl.ds(j * w, w)
        lane0 = (j * w // half) * 2 * half + (j * w) % half
        re_l, im_l = pl.ds(lane0, w), pl.ds(lane0 + half, w)
        tab = lambda t, n_l=n_l: (tab_ref[0, t, :, n_l], tab_ref[1, t, :, n_l])
        a_re, a_im = tab(TAB_A)

        def local_step(k, h):
            rs = tile_rows(k)
            h_re = a_re * h[0] - a_im * h[1] + x_ref[rs, re_l]
            h_im = a_re * h[1] + a_im * h[0] + x_ref[rs, im_l]
            out_ref[rs, re_l] = h_re
            out_ref[rs, im_l] = h_im
            return h_re, h_im

        zero = jnp.zeros((SUBLANES, w), F32)
        end_re, end_im = lax.fori_loop(0, STEPS, local_step, (zero, zero))
        for t, k in ((TAB_BIG, 1), (TAB_BIG + 1, 2), (TAB_BIG + 2, 4)):
            p_re, p_im = tab(t)
            s_re, s_im = from_previous_segment(end_re, k, 0.0), from_previous_segment(end_im, k, 0.0)
            end_re, end_im = end_re + (p_re * s_re - p_im * s_im), end_im + (p_re * s_im + p_im * s_re)
        c0_re, c0_im = carry_re[:, n_l], carry_im[:, n_l]
        p_re, p_im = tab(TAB_SEG)
        end_re = end_re + (p_re * c0_re - p_im * c0_im)
        end_im = end_im + (p_re * c0_im + p_im * c0_re)
        carry_re[:, n_l] = jnp.broadcast_to(end_re[last:last + 1, :], end_re.shape)
        carry_im[:, n_l] = jnp.broadcast_to(end_im[last:last + 1, :], end_im.shape)
        in_re = from_previous_segment(end_re, 1, c0_re)
        in_im = from_previous_segment(end_im, 1, c0_im)

        def carry_step(k, st):
            rs = tile_rows(k)
            p_re, p_im = tab_ref[0, TAB_PW + k, :, n_l], tab_ref[1, TAB_PW + k, :, n_l]
            o_re = out_ref[rs, re_l] + (p_re * in_re - p_im * in_im)
            o_im = out_ref[rs, im_l] + (p_re * in_im + p_im * in_re)
            out_ref[rs, re_l] = o_re
            out_ref[rs, im_l] = o_im
            if pair_ref is None:
                return st
            s_re, s_im = pair_ref[rs, re_l], pair_ref[rs, im_l]
            return (o_re, o_im, st[2] + (st[0] * s_re + st[1] * s_im), st[3] + (st[1] * s_re - st[0] * s_im))

        if pair_ref is None:
            lax.fori_loop(0, STEPS, carry_step, 0)
        else:
            fin = lax.fori_loop(0, STEPS, carry_step, (in_re, in_im, zero, zero))
            acc_ref[:, n_l] += fin[2]
            acc_ref[:, pl.ds(NSTATE + j * w, w)] += fin[3]


def _scan_block_index(i, n_lat, ctx_first_then_ascending):
    if ctx_first_then_ascending:
        return jnp.where(i == 0, n_lat, i - 1)
    return jnp.where(i == 0, n_lat, n_lat - i)


def _full_spec(shape):
    return pl.BlockSpec(shape, lambda i: (0,) * len(shape))


_S5_BLOCKED = (S5_BLOCKS, S5_BLOCK_WIDTH, 2 * NSTATE // S5_BLOCKS)
_S5_TABLES = (2, TAB_ROWS, SUBLANES, NSTATE)


def _s5_scan_fwd(name, ascending, z_all, bmat, cmat, tab, perm, perm_t):
    rows = z_all.shape[0]
    nb = rows // ROW_BLOCK
    n_lat = nb - 1
    bw, sw = S5_BLOCK_WIDTH, 2 * NSTATE // S5_BLOCKS

    def body(u_ref, bm_ref, cm_ref, tab_ref, p_ref, pt_ref, s_ref, y_ref, bu, yp, carry_re, carry_im):
        @pl.when(pl.program_id(0) == 0)
        def _():
            carry_re[...] = jnp.zeros_like(carry_re)
            carry_im[...] = jnp.zeros_like(carry_im)

        up = _permute_rows(p_ref, u_ref[...].astype(BF16))
        for c in range(S5_BLOCKS):
            bu[:, c * sw:(c + 1) * sw] = _dot(up[:, c * bw:(c + 1) * bw], bm_ref[c], "nn")
        _scan_chunk(bu, s_ref, tab_ref, carry_re, carry_im, ascending)
        for c in range(S5_BLOCKS):
            yp[:, c * bw:(c + 1) * bw] = _dot(s_ref[:, c * sw:(c + 1) * sw].astype(BF16), cm_ref[c], "nt")
        y_ref[...] = _unpermute_rows(pt_ref, yp[...])

    blk = lambda i: (_scan_block_index(i, n_lat, ascending), 0)
    return pl.pallas_call(
        body, name=name, grid=(nb,),
        in_specs=[pl.BlockSpec((ROW_BLOCK, S5_WIDTH), blk), _full_spec(_S5_BLOCKED), _full_spec(_S5_BLOCKED),
                  _full_spec(_S5_TABLES), _full_spec((ROW_BLOCK, ROW_BLOCK)), _full_spec((ROW_BLOCK, ROW_BLOCK))],
        out_specs=[pl.BlockSpec((ROW_BLOCK, 2 * NSTATE), blk), pl.BlockSpec((ROW_BLOCK, S5_WIDTH), blk)],
        out_shape=[_sds((rows, 2 * NSTATE)), _sds((rows, S5_WIDTH))],
        scratch_shapes=[pltpu.VMEM((ROW_BLOCK, 2 * NSTATE), F32), pltpu.VMEM((ROW_BLOCK, S5_WIDTH), F32),
                        pltpu.VMEM((SUBLANES, NSTATE), F32), pltpu.VMEM((SUBLANES, NSTATE), F32)],
        compiler_params=_params(("arbitrary",)),
    )(z_all, bmat, cmat, tab, perm, perm_t)


def _s5_scan_bwd(name, ascending, dy, z_all, states, bmat, cmat, adj, perm, perm_t):
    rows = states.shape[0]
    nb = rows // ROW_BLOCK
    n_lat = nb - 1
    bw, sw = S5_BLOCK_WIDTH, 2 * NSTATE // S5_BLOCKS

    def block_index(i):
        if ascending:
            return jnp.where(i == nb - 1, n_lat, n_lat - 1 - i)
        return jnp.where(i == nb - 1, n_lat, i)

    def body(dy_ref, u_ref, s_ref, bm_ref, cm_ref, adj_ref, p_ref, pt_ref, du_ref, db_ref, dc_ref, da_ref,
             g, dup, carry_re, carry_im):
        i = pl.program_id(0)

        @pl.when(i == 0)
        def _():
            carry_re[...] = jnp.zeros_like(carry_re)
            carry_im[...] = jnp.zeros_like(carry_im)
            da_ref[...] = jnp.zeros_like(da_ref)
            db_ref[...] = jnp.zeros_like(db_ref)
            dc_ref[...] = jnp.zeros_like(dc_ref)

        @pl.when(i < nb - 1)
        def _():
            dyp = _permute_rows(p_ref, dy_ref[...].astype(BF16))
            for c in range(S5_BLOCKS):
                g[:, c * sw:(c + 1) * sw] = _dot(dyp[:, c * bw:(c + 1) * bw], cm_ref[c], "nn")
                dc_ref[c] += _dot(dyp[:, c * bw:(c + 1) * bw], s_ref[:, c * sw:(c + 1) * sw].astype(BF16), "tn")

        @pl.when(i == nb - 1)
        def _():
            g[...] = jnp.zeros_like(g)

        _scan_chunk(g, g, adj_ref, carry_re, carry_im, not ascending, pair_ref=s_ref, acc_ref=da_ref)
        up = _permute_rows(p_ref, u_ref[...].astype(BF16))
        for c in range(S5_BLOCKS):
            gc = g[:, c * sw:(c + 1) * sw].astype(BF16)
            dup[:, c * bw:(c + 1) * bw] = _dot(gc, bm_ref[c], "nt")
            db_ref[c] += _dot(up[:, c * bw:(c + 1) * bw], gc, "tn")
        du_ref[...] = _unpermute_rows(pt_ref, dup[...])

    blk = lambda i: (block_index(i), 0)
    return pl.pallas_call(
        body, name=name, grid=(nb,),
        in_specs=[pl.BlockSpec((ROW_BLOCK, S5_WIDTH), lambda i: (jnp.minimum(block_index(i), n_lat - 1), 0)),
                  pl.BlockSpec((ROW_BLOCK, S5_WIDTH), blk), pl.BlockSpec((ROW_BLOCK, 2 * NSTATE), blk),
                  _full_spec(_S5_BLOCKED), _full_spec(_S5_BLOCKED), _full_spec(_S5_TABLES),
                  _full_spec((ROW_BLOCK, ROW_BLOCK)), _full_spec((ROW_BLOCK, ROW_BLOCK))],
        out_specs=[pl.BlockSpec((ROW_BLOCK, S5_WIDTH), blk), _full_spec(_S5_BLOCKED), _full_spec(_S5_BLOCKED),
                   _full_spec((SUBLANES, 2 * NSTATE))],
        out_shape=[_sds((rows, S5_WIDTH)), _sds(_S5_BLOCKED), _sds(_S5_BLOCKED), _sds((SUBLANES, 2 * NSTATE))],
        scratch_shapes=[pltpu.VMEM((ROW_BLOCK, 2 * NSTATE), F32), pltpu.VMEM((ROW_BLOCK, S5_WIDTH), F32),
                        pltpu.VMEM((SUBLANES, NSTATE), F32), pltpu.VMEM((SUBLANES, NSTATE), F32)],
        compiler_params=_params(("arbitrary",)),
    )(dy, z_all, states, bmat, cmat, adj, perm, perm_t)


def _glu_fwd(z_all, y0, y1, d_skip, w_glu, n_rows):
    def body(u_ref, y0_ref, y1_ref, d_ref, w_ref, o_ref):
        y = d_ref[...] * u_ref[...] + y0_ref[...] + y1_ref[...]
        g = _gelu(y)
        t = _dot(g.astype(BF16), w_ref[...], "nn")
        o_ref[...] = (g * _sigmoid(t)).astype(o_ref.dtype)

    row = pl.BlockSpec((ROW_BLOCK, S5_WIDTH), lambda i: (i, 0))
    return pl.pallas_call(
        body, name="glu_fwd", grid=(n_rows // ROW_BLOCK,),
        in_specs=[row, row, row, pl.BlockSpec((1, S5_WIDTH), lambda i: (0, 0)),
                  pl.BlockSpec((S5_WIDTH, S5_WIDTH), lambda i: (0, 0))],
        out_specs=row, out_shape=_sds((n_rows, S5_WIDTH + CONV_WIDTH), BF16), compiler_params=_params(("parallel",)),
    )(z_all, y0, y1, d_skip, w_glu)


def _glu_bwd(d_ycat, z_all, y0, y1, d_skip, w_glu, n_rows):
    def body(do_ref, u_ref, y0_ref, y1_ref, d_ref, w_ref, dy_ref, dw_ref, dd_ref):
        @pl.when(pl.program_id(0) == 0)
        def _():
            dw_ref[...] = jnp.zeros_like(dw_ref)
            dd_ref[...] = jnp.zeros_like(dd_ref)

        u = u_ref[...]
        y = d_ref[...] * u + y0_ref[...] + y1_ref[...]
        g = _gelu(y)
        gb = g.astype(BF16)
        w = w_ref[...]
        sg = _sigmoid(_dot(gb, w, "nn"))
        do = do_ref[...]
        dt = do * g * sg * (1.0 - sg)
        dtb = dt.astype(BF16)
        dg = do * sg + _dot(dtb, w, "nt")
        dy = dg * _dgelu(y)
        dy_ref[...] = dy
        dw_ref[...] += _dot(gb, dtb, "tn")
        dd_ref[...] += _fold8(dy * u)

    row = pl.BlockSpec((ROW_BLOCK, S5_WIDTH), lambda i: (i, 0))
    sq = pl.BlockSpec((S5_WIDTH, S5_WIDTH), lambda i: (0, 0))
    return pl.pallas_call(
        body, name="glu_bwd", grid=(n_rows // ROW_BLOCK,),
        in_specs=[row, row, row, row, pl.BlockSpec((1, S5_WIDTH), lambda i: (0, 0)), sq],
        out_specs=[row, sq, pl.BlockSpec((SUBLANES, S5_WIDTH), lambda i: (0, 0))],
        out_shape=[_sds((n_rows, S5_WIDTH)), _sds((S5_WIDTH, S5_WIDTH)), _sds((SUBLANES, S5_WIDTH))],
        compiler_params=_params(("arbitrary",)),
    )(d_ycat, z_all, y0, y1, d_skip, w_glu)


CONV_HALF = CONV_K // 2


def _conv_block(n_rows):
    blk = min(1024, n_rows)
    assert blk >= CONV_HALF * GRID_W and n_rows % blk == 0
    return blk


def _conv_gate(z_all, n_rows):
    blk = _conv_block(n_rows)
    nb = n_rows // blk

    def body(v_ref, g_ref, o_ref):
        i = pl.program_id(0)
        inside = jnp.logical_and(i >= 1, i <= nb)

        @pl.when(inside)
        def _():
            o_ref[...] = v_ref[...] * _sigmoid(g_ref[...])

        @pl.when(jnp.logical_not(inside))
        def _():
            o_ref[...] = jnp.zeros_like(o_ref)

    src = lambda col: pl.BlockSpec((blk, CONV_WIDTH), lambda i: (jnp.clip(i - 1, 0, nb - 1), col))
    return pl.pallas_call(
        body, name="conv_gate", grid=(nb + 2,), in_specs=[src(1), src(2)],
        out_specs=pl.BlockSpec((blk, CONV_WIDTH), lambda i: (i, 0)),
        out_shape=_sds(((nb + 2) * blk, CONV_WIDTH)), compiler_params=_params(("parallel",)),
    )(z_all, z_all)


def _load_window(pad_ref, win, sem, blk):
    start = pl.multiple_of(pl.program_id(0) * blk, blk)
    copy = pltpu.make_async_copy(pad_ref.at[pl.ds(start, 3 * blk), :], win, sem)
    copy.start()
    copy.wait()


def _conv_fwd(hh_pad, w, b, ln_g, ln_b, ycat, n_rows):
    blk = _conv_block(n_rows)

    def body(hh_ref, w_ref, b_ref, g_ref, lb_ref, ycat_ref, hc_ref, y_ref, win, sem):
        _load_window(hh_ref, win, sem, blk)

        def tile(t, _):
            r0 = pl.multiple_of(t * CONV_ROWS, CONV_ROWS)
            acc = jnp.zeros((CONV_ROWS, CONV_WIDTH), F32)
            for k in range(CONV_K):
                acc = acc + w_ref[k:k + 1, :] * win[pl.ds(r0 + blk + (k - CONV_HALF) * GRID_W, CONV_ROWS), :]
            hc = acc + b_ref[...]
            hc_ref[pl.ds(r0, CONV_ROWS), :] = hc
            mu = jnp.mean(hc, axis=-1, keepdims=True)
            xc = hc - mu
            ln = xc * lax.rsqrt(jnp.mean(xc * xc, axis=-1, keepdims=True) + EPS_LN) * g_ref[...] + lb_ref[...]
            y_ref[pl.ds(r0, CONV_ROWS), :] = _silu(ln).astype(y_ref.dtype)
            return 0

        lax.fori_loop(0, blk // CONV_ROWS, tile, 0)

    vec = pl.BlockSpec((1, CONV_WIDTH), lambda i: (0, 0))
    row = pl.BlockSpec((blk, CONV_WIDTH), lambda i: (i, 0))
    return pl.pallas_call(
        body, name="conv_fwd", grid=(n_rows // blk,),
        in_specs=[ANY, pl.BlockSpec((CONV_K, CONV_WIDTH), lambda i: (0, 0)), vec, vec, vec, ANY],
        out_specs=[row, pl.BlockSpec((blk, CONV_WIDTH), lambda i: (i, 1))],
        out_shape=[_sds((n_rows, CONV_WIDTH)), _sds(ycat.shape, ycat.dtype)], input_output_aliases={5: 1},
        scratch_shapes=[pltpu.VMEM((3 * blk, CONV_WIDTH), F32), pltpu.SemaphoreType.DMA],
        compiler_params=_params(("arbitrary",)),
    )(hh_pad, w, b, ln_g, ln_b, ycat)


def _conv_bwd_norm(d_ycat, hc, ln_g, ln_b, n_rows):
    blk = _conv_block(n_rows)
    nb = n_rows // blk

    def body(dy_ref, hc_ref, g_ref, lb_ref, o_ref, sums):
        i = pl.program_id(0)

        @pl.when(i == 0)
        def _():
            sums[...] = jnp.zeros_like(sums)

        inside = jnp.logical_and(i >= 1, i <= nb)

        @pl.when(inside)
        def _():
            hcv = hc_ref[...]
            mu = jnp.mean(hcv, axis=-1, keepdims=True)
            xc = hcv - mu
            rstd = lax.rsqrt(jnp.mean(xc * xc, axis=-1, keepdims=True) + EPS_LN)
            xh = xc * rstd
            g = g_ref[...]
            dln = dy_ref[...] * _dsilu(xh * g + lb_ref[...])
            dxh = dln * g
            dhc = rstd * (dxh - jnp.mean(dxh, axis=-1, keepdims=True) - xh * jnp.mean(dxh * xh, axis=-1, keepdims=True))
            o_ref[...] = dhc
            sums[0] += _fold8(dhc)
            sums[1] += _fold8(dln * xh)
            sums[2] += _fold8(dln)

        @pl.when(jnp.logical_not(inside))
        def _():
            o_ref[...] = jnp.zeros_like(o_ref)

    vec = pl.BlockSpec((1, CONV_WIDTH), lambda i: (0, 0))
    return pl.pallas_call(
        body, name="conv_bwd_norm", grid=(nb + 2,),
        in_specs=[pl.BlockSpec((blk, CONV_WIDTH), lambda i: (jnp.clip(i - 1, 0, nb - 1), 1)),
                  pl.BlockSpec((blk, CONV_WIDTH), lambda i: (jnp.clip(i - 1, 0, nb - 1), 0)), vec, vec],
        out_specs=[pl.BlockSpec((blk, CONV_WIDTH), lambda i: (i, 0)),
                   pl.BlockSpec((3, SUBLANES, CONV_WIDTH), lambda i: (0, 0, 0))],
        out_shape=[_sds(((nb + 2) * blk, CONV_WIDTH)), _sds((3, SUBLANES, CONV_WIDTH))],
        compiler_params=_params(("arbitrary",)),
    )(d_ycat, hc, ln_g, ln_b)


def _conv_bwd_taps(dhc_pad, hh_pad, z_all, w, n_rows):
    blk = _conv_block(n_rows)

    def body(dhc_ref, hh_ref, v_ref, g_ref, w_ref, dv_ref, dg_ref, dw_ref, dwin, hwin, sems):
        @pl.when(pl.program_id(0) == 0)
        def _():
            dw_ref[...] = jnp.zeros_like(dw_ref)

        _load_window(dhc_ref, dwin, sems.at[0], blk)
        _load_window(hh_ref, hwin, sems.at[1], blk)

        def tile(t, _):
            r0 = pl.multiple_of(t * CONV_ROWS, CONV_ROWS)
            dh = dwin[pl.ds(r0 + blk, CONV_ROWS), :]
            acc = jnp.zeros((CONV_ROWS, CONV_WIDTH), F32)
            for k in range(CONV_K):
                off = (k - CONV_HALF) * GRID_W
                acc = acc + w_ref[k:k + 1, :] * dwin[pl.ds(r0 + blk - off, CONV_ROWS), :]
                dw_ref[k] += _fold8(dh * hwin[pl.ds(r0 + blk + off, CONV_ROWS), :])
            rs = pl.ds(r0, CONV_ROWS)
            sg = _sigmoid(g_ref[rs, :])
            vv = v_ref[rs, :]
            dv_ref[rs, :] = acc * sg
            dg_ref[rs, :] = acc * vv * sg * (1.0 - sg)
            return 0

        lax.fori_loop(0, blk // CONV_ROWS, tile, 0)

    row = pl.BlockSpec((blk, CONV_WIDTH), lambda i: (i, 0))
    return pl.pallas_call(
        body, name="conv_bwd_taps", grid=(n_rows // blk,),
        in_specs=[ANY, ANY,
            pl.BlockSpec((blk, CONV_WIDTH), lambda i: (i, 1)), pl.BlockSpec((blk, CONV_WIDTH), lambda i: (i, 2)),
            pl.BlockSpec((CONV_K, CONV_WIDTH), lambda i: (0, 0))],
        out_specs=[row, row, pl.BlockSpec((CONV_K, SUBLANES, CONV_WIDTH), lambda i: (0, 0, 0))],
        out_shape=[_sds((n_rows, CONV_WIDTH)), _sds((n_rows, CONV_WIDTH)), _sds((CONV_K, SUBLANES, CONV_WIDTH))],
        scratch_shapes=[pltpu.VMEM((3 * blk, CONV_WIDTH), F32), pltpu.VMEM((3 * blk, CONV_WIDTH), F32),
                        pltpu.SemaphoreType.DMA((2,))],
        compiler_params=_params(("arbitrary",)),
    )(dhc_pad, hh_pad, z_all, z_all, w)


def _dz_assemble(du0, du1, dy, d_skip, dv, dgate, n_lat):
    rows = du0.shape[0]
    nb = rows // ROW_BLOCK

    w = S5_WIDTH

    def body(a_ref, b_ref, dy_ref, d_ref, dv_ref, dg_ref, o_ref):
        lat = pl.program_id(0) < n_lat

        @pl.when(lat)
        def _():
            o_ref[:, 0:w] = (a_ref[...] + b_ref[...] + dy_ref[...] * d_ref[...]).astype(o_ref.dtype)
            o_ref[:, w:2 * w] = dv_ref[...].astype(o_ref.dtype)
            o_ref[:, 2 * w:3 * w] = dg_ref[...].astype(o_ref.dtype)

        @pl.when(jnp.logical_not(lat))
        def _():
            o_ref[:, 0:w] = (a_ref[...] + b_ref[...]).astype(o_ref.dtype)
            o_ref[:, w:3 * w] = jnp.zeros((ROW_BLOCK, 2 * w), o_ref.dtype)

    all_rows = pl.BlockSpec((ROW_BLOCK, w), lambda i: (i, 0))
    lat_rows = pl.BlockSpec((ROW_BLOCK, w), lambda i: (jnp.minimum(i, n_lat - 1), 0))
    return pl.pallas_call(
        body, name="dz_assemble", grid=(nb,),
        in_specs=[all_rows, all_rows, lat_rows, pl.BlockSpec((1, w), lambda i: (0, 0)), lat_rows, lat_rows],
        out_specs=pl.BlockSpec((ROW_BLOCK, IN_COLS), lambda i: (i, 0)),
        out_shape=_sds((rows, IN_COLS), BF16), compiler_params=_params(("parallel",)),
    )(du0, du1, dy, d_skip, dv, dgate)


def _sum_parts(parts):
    _, r, c = parts.shape

    def body(p_ref, o_ref):
        acc = p_ref[0]
        for q in range(1, NDEV):
            acc = acc + p_ref[q]
        o_ref[...] = acc

    return pl.pallas_call(body, name="sum_parts", out_shape=_sds((r, c)), compiler_params=_params())(parts)


def _row_tile(r, c):
    best = r
    for t in (1024, 512, 256, 128, 64, 32, 16, 8):
        if r % t == 0 and t * c <= 128 * 1024:
            return t
    return best


def _adamw(name, w, gparts, m, v):
    r, c = w.shape
    np_ = gparts.shape[0]
    tr = _row_tile(r, c)

    def body(w_ref, g_ref, m_ref, v_ref, go_ref, d_ref, mo_ref, vo_ref):
        g = g_ref[0].astype(F32)
        for q in range(1, np_):
            g = g + g_ref[q].astype(F32)
        m2 = ADAM_B1 * m_ref[...] + (1.0 - ADAM_B1) * g
        v2 = ADAM_B2 * v_ref[...] + (1.0 - ADAM_B2) * jnp.square(g)
        m_hat = m2 / (1.0 - ADAM_B1 ** ADAM_STEP)
        v_hat = v2 / (1.0 - ADAM_B2 ** ADAM_STEP)
        go_ref[...] = g
        d_ref[...] = -ADAM_LR * (m_hat / (jnp.sqrt(v_hat) + ADAM_EPS) + ADAM_WD * w_ref[...])
        mo_ref[...] = m2
        vo_ref[...] = v2

    row = pl.BlockSpec((tr, c), lambda i: (i, 0))
    return pl.pallas_call(
        body, name=name, grid=(r // tr,),
        in_specs=[row, pl.BlockSpec((np_, tr, c), lambda i: (0, i, 0)), row, row],
        out_specs=[row] * 4, out_shape=[_sds((r, c))] * 4, compiler_params=_params(("parallel",)),
    )(w, gparts, m, v)


SMALL = ["c_ctx", "ada_b", "norm1_g", "s5_lam_re", "s5_lam_im", "s5_log_dt", "s5_b_re", "s5_b_im", "s5_c_re",
         "s5_c_im", "s5_d", "conv_b", "conv_ln_g", "conv_ln_b", "norm2_g", "final_g"]


def _pack_small(parts):
    flat = jnp.concatenate([p.reshape(-1).astype(F32) for p in parts])
    return jnp.pad(flat, (0, SMALL_ROWS * D_MODEL - flat.shape[0])).reshape(SMALL_ROWS, D_MODEL)


def _unpack_small(packed, like):
    flat = packed.reshape(-1)
    out, off = [], 0
    for ref in like:
        out.append(flat[off:off + ref.size].reshape(ref.shape))
        off += ref.size
    return out


def kernel(x, c, ctx, c_ctx, ada_w, ada_b, norm1_g, w_in, s5_lam_re, s5_lam_im, s5_log_dt, s5_b_re, s5_b_im, s5_c_re, s5_c_im, s5_d, s5_w_glu, conv_w, conv_b, conv_ln_g, conv_ln_b, w_out, norm2_g, mlp_w1, mlp_w2, final_g, loss_target, m_c_ctx, m_ada_w, m_ada_b, m_norm1_g, m_w_in, m_s5_lam_re, m_s5_lam_im, m_s5_log_dt, m_s5_b_re, m_s5_b_im, m_s5_c_re, m_s5_c_im, m_s5_d, m_s5_w_glu, m_conv_w, m_conv_b, m_conv_ln_g, m_conv_ln_b, m_w_out, m_norm2_g, m_mlp_w1, m_mlp_w2, m_final_g, v_c_ctx, v_ada_w, v_ada_b, v_norm1_g, v_w_in, v_s5_lam_re, v_s5_lam_im, v_s5_log_dt, v_s5_b_re, v_s5_b_im, v_s5_c_re, v_s5_c_im, v_s5_d, v_s5_w_glu, v_conv_w, v_conv_b, v_conv_ln_g, v_conv_ln_b, v_w_out, v_norm2_g, v_mlp_w1, v_mlp_w2, v_final_g):
    weights = dict(c_ctx=c_ctx, ada_w=ada_w, ada_b=ada_b, norm1_g=norm1_g, w_in=w_in, s5_lam_re=s5_lam_re, s5_lam_im=s5_lam_im, s5_log_dt=s5_log_dt, s5_b_re=s5_b_re, s5_b_im=s5_b_im, s5_c_re=s5_c_re, s5_c_im=s5_c_im, s5_d=s5_d, s5_w_glu=s5_w_glu, conv_w=conv_w, conv_b=conv_b, conv_ln_g=conv_ln_g, conv_ln_b=conv_ln_b, w_out=w_out, norm2_g=norm2_g, mlp_w1=mlp_w1, mlp_w2=mlp_w2, final_g=final_g)
    mom1 = dict(c_ctx=m_c_ctx, ada_w=m_ada_w, ada_b=m_ada_b, norm1_g=m_norm1_g, w_in=m_w_in, s5_lam_re=m_s5_lam_re, s5_lam_im=m_s5_lam_im, s5_log_dt=m_s5_log_dt, s5_b_re=m_s5_b_re, s5_b_im=m_s5_b_im, s5_c_re=m_s5_c_re, s5_c_im=m_s5_c_im, s5_d=m_s5_d, s5_w_glu=m_s5_w_glu, conv_w=m_conv_w, conv_b=m_conv_b, conv_ln_g=m_conv_ln_g, conv_ln_b=m_conv_ln_b, w_out=m_w_out, norm2_g=m_norm2_g, mlp_w1=m_mlp_w1, mlp_w2=m_mlp_w2, final_g=m_final_g)
    mom2 = dict(c_ctx=v_c_ctx, ada_w=v_ada_w, ada_b=v_ada_b, norm1_g=v_norm1_g, w_in=v_w_in, s5_lam_re=v_s5_lam_re, s5_lam_im=v_s5_lam_im, s5_log_dt=v_s5_log_dt, s5_b_re=v_s5_b_re, s5_b_im=v_s5_b_im, s5_c_re=v_s5_c_re, s5_c_im=v_s5_c_im, s5_d=v_s5_d, s5_w_glu=v_s5_w_glu, conv_w=v_conv_w, conv_b=v_conv_b, conv_ln_g=v_conv_ln_g, conv_ln_b=v_conv_ln_b, w_out=v_w_out, norm2_g=v_norm2_g, mlp_w1=v_mlp_w1, mlp_w2=v_mlp_w2, final_g=v_final_g)
    order = list(weights)

    me = 4 * lax.axis_index("x") + 2 * lax.axis_index("y") + lax.axis_index("c")
    xs, cs, tgt = x[0], ctx[0], loss_target[0]
    n_lat_rows, n_ctx_rows = xs.shape[0], cs.shape[0]
    n_rows = n_lat_rows + n_ctx_rows
    n_lat = n_lat_rows // ROW_BLOCK
    ada_cols = ada_w.shape[2]

    (c_all,), _ = _exchange("gather_c", [c], [True])
    c_all = c_all.reshape(NDEV, D_MODEL)

    cond_fwd = jnp.concatenate([c_all, c_ctx[None], jnp.zeros((7, D_MODEL), F32)])
    ada_b_loc = lax.dynamic_slice(ada_b, (0, me * ada_cols), (1, ada_cols))
    (mod_g,), mod_token = _exchange("gather_mod", [_ada_fwd(cond_fwd, ada_w[0], ada_b_loc)], [True])
    wi_send, wi_recv, wi_src, wi_land, wi_token = _exchange_start(
        "gather_w_in_start", [w_in[0].astype(BF16) + mod_token[0:1, 0:1].astype(BF16)], [True])
    mixer_w = [s5_w_glu[0].astype(BF16), conv_w[0] + wi_token[0:1, 0:1], w_out[0].astype(BF16)]
    mixer_send, mixer_recv, mixer_src, mixer_land, mixer_token = _exchange_start("gather_mixer_start", mixer_w, [True] * 3)
    mlp_w = [mlp_w1[0].astype(BF16), mlp_w2[0].astype(BF16) + mixer_token[0:1, 0:1].astype(BF16)]
    mlpw_send, mlpw_recv, mlpw_src, mlpw_land, mlpw_token = _exchange_start("gather_mlp_start", mlp_w, [True] * 2)
    mod_rows = jnp.transpose(mod_g, (1, 0, 2)).reshape(16, 6 * D_MODEL) + mlpw_token[0:1, 0:1]
    mod = lax.dynamic_slice(mod_rows, (me, 0), (1, 6 * D_MODEL)).reshape(6, D_MODEL)
    modc = mod_rows[8, :2 * D_MODEL].reshape(2, D_MODEL)
    sh1, sc1, g1, sh2, sc2, g2 = [mod[i:i + 1] for i in range(6)]

    a_all = _prenorm("prenorm1", xs, cs, norm1_g, jnp.stack([mod[0:2], modc]))
    wi_own, wi_landed = _exchange_wait("gather_w_in_wait", wi_send, wi_recv, wi_src, wi_land, [True], a_all)
    w_in_full = jnp.transpose(_with_own(wi_landed[0], wi_own[0], me), (1, 0, 2)).reshape(D_MODEL, IN_COLS)
    tm_all = 1088 if n_rows % 1088 == 0 else ROW_BLOCK
    (z_all,) = _matmul("in_proj", a_all, w_in_full, "nn", (n_rows, IN_COLS, D_MODEL), (tm_all, IN_COLS, D_MODEL),
                       [((n_rows, IN_COLS), F32)])

    lam_re, lam_im = s5_lam_re[0].reshape(2, 1, NSTATE), s5_lam_im[0].reshape(2, 1, NSTATE)
    ldt = jnp.repeat(s5_log_dt[0], S5_STATE, axis=-1).reshape(2, 1, NSTATE)
    bt_re = jnp.transpose(s5_b_re[0], (0, 3, 1, 2)).reshape(2, S5_GROUP, NSTATE)
    bt_im = jnp.transpose(s5_b_im[0], (0, 3, 1, 2)).reshape(2, S5_GROUP, NSTATE)
    groups_per_block = S5_GROUPS // S5_BLOCKS
    ct_re = jnp.tile(s5_c_re[0].reshape(2, S5_WIDTH, S5_STATE), (1, 1, groups_per_block))
    ct_im = jnp.tile(s5_c_im[0].reshape(2, S5_WIDTH, S5_STATE), (1, 1, groups_per_block))
    d_skip = s5_d[0].reshape(1, S5_WIDTH)
    perm = _segment_permutation()
    perm_t = perm.T
    disc, states, y_dir = [], [], []
    for d in range(2):
        disc.append(_s5_discretise(f"s5_disc{d}", d == 0, lam_re[d], lam_im[d], ldt[d], bt_re[d], bt_im[d], ct_re[d], ct_im[d]))
        _, tab, _, bmat, cmat = disc[d]
        s, yd = _s5_scan_fwd(f"s5_scan_fwd{d}", d == 0, z_all, bmat, cmat, tab, perm, perm_t)
        states.append(s)
        y_dir.append(yd)
    mixer_own, mixer_landed = _exchange_wait("gather_mixer_wait", mixer_send, mixer_recv, mixer_src, mixer_land,
                                             [True] * 3, y_dir[1])
    glu_g, conv_w_g, w_out_g = [_with_own(l, o, me) for l, o in zip(mixer_landed, mixer_own)]
    glu_full = glu_g.reshape(S5_WIDTH, S5_WIDTH)
    conv_w_full = jnp.transpose(conv_w_g, (1, 0, 2)).reshape(CONV_K, CONV_WIDTH)
    w_out_full = w_out_g.reshape(D_MODEL, D_MODEL)
    ycat = _glu_fwd(z_all, y_dir[0], y_dir[1], d_skip, glu_full, n_lat_rows)

    hh_pad = _conv_gate(z_all, n_lat_rows)
    hc, ycat = _conv_fwd(hh_pad, conv_w_full, conv_b, conv_ln_g, conv_ln_b, ycat, n_lat_rows)

    tm = min(1024, n_lat_rows)
    tm_e = min(512, n_lat_rows)
    w1_cols = D_FF // NDEV
    row_vec = lambda tn: pl.BlockSpec((1, tn), lambda i, j, k: (0, j))
    out_tile = lambda t_m, t_n: pl.BlockSpec((t_m, t_n), lambda i, j, k: (i, j))
    full_rows = ((n_lat_rows, D_MODEL), F32)
    sums = ((n_lat_rows // tm_e, SUBLANES, D_MODEL), F32)
    sums_spec = pl.BlockSpec((None, SUBLANES, D_MODEL), lambda i, j, k: (i, 0, 0))
    vec = lambda v: (v, row_vec(D_MODEL))
    mix, h1, a2 = _matmul("out_proj", ycat, w_out_full, "nn", (n_lat_rows, D_MODEL, D_MODEL), (tm_e, D_MODEL, D_MODEL),
                          [full_rows, full_rows, ((n_lat_rows, D_MODEL), BF16)], epi=_epi_residual_prenorm,
                          epi_extra=[(xs, out_tile(tm_e, D_MODEL)), vec(g1), vec(norm2_g), vec(sc2), vec(sh2)])
    mlpw_own, mlpw_landed = _exchange_wait("gather_mlp_wait", mlpw_send, mlpw_recv, mlpw_src, mlpw_land, [True] * 2, a2)
    w1_g, w2_g = [_with_own(l, o, me) for l, o in zip(mlpw_landed, mlpw_own)]
    w2_full = w2_g.reshape(D_FF, D_MODEL)
    tm_up = min(2048, n_lat_rows)
    (f,) = _matmul("mlp_up", a2, w1_g, "nn", (n_lat_rows, D_FF, D_MODEL), (tm_up, w1_cols, D_MODEL),
                   [((n_lat_rows, D_FF), BF16)], b_spec=pl.BlockSpec((None, D_MODEL, w1_cols), lambda i, j, k: (j, 0, 0)))
    sq_relu = lambda t: jnp.square(jnp.maximum(t, 0.0))
    mlp_out, d_h2, dm2, err_sums, d_final_g8 = _matmul(
        "mlp_down", f, w2_full, "nn", (n_lat_rows, D_MODEL, D_FF), (tm_e, D_MODEL, 1024),
        [full_rows, full_rows, ((n_lat_rows, D_MODEL), BF16), sums, sums], a_fn=sq_relu, epi=_epi_residual_loss,
        epi_extra=[(h1, out_tile(tm_e, D_MODEL)), vec(g2), (tgt, out_tile(tm_e, D_MODEL)), vec(final_g[None])],
        out_specs=[out_tile(tm_e, D_MODEL)] * 3 + [sums_spec] * 2)
    loss = lax.psum(0.5 / D_MODEL * jnp.sum(err_sums), ("x", "y", "c"))

    (d_f,) = _matmul("mlp_down_dx", dm2, w2_full, "nt", (n_lat_rows, D_FF, D_MODEL), (tm, 512, D_MODEL),
                     [((n_lat_rows, D_FF), BF16)],
                     epi=lambda acc, ft: (acc * 2.0 * jnp.maximum(ft.astype(F32), 0.0),), epi_extra=[(f, out_tile(tm, 512))])
    (g_w2,) = _matmul("mlp_down_dw", f, dm2, "tn", (D_FF, D_MODEL, n_lat_rows), (1024, D_MODEL, tm),
                      [((D_FF, D_MODEL), F32)], a_fn=sq_relu)
    (g_w1,) = _matmul("mlp_up_dw", a2, d_f, "tn", (D_MODEL, D_FF, n_lat_rows), (D_MODEL, w1_cols, tm),
                      [((NDEV, D_MODEL, w1_cols), F32)],
                      out_specs=[pl.BlockSpec((None, D_MODEL, w1_cols), lambda i, j, k: (j, 0, 0))])
    mlp_send, mlp_recv, mlp_src, mlp_land, mlp_token = _exchange_start(
        "scatter_mlp_start", [g_w1, g_w2.reshape(NDEV, D_FF // NDEV, D_MODEL)], [False] * 2)
    d_h1, dm1, *sums2 = _matmul(
        "mlp_up_dx", d_f, w1_g, "nt", (n_lat_rows, D_MODEL, D_FF), (tm_e, D_MODEL, w1_cols),
        [full_rows, ((n_lat_rows, D_MODEL), BF16)] + [sums] * 4, epi=_epi_norm_bwd,
        epi_extra=[(h1, out_tile(tm_e, D_MODEL)), (d_h2, out_tile(tm_e, D_MODEL)), (mlp_out, out_tile(tm_e, D_MODEL)),
                   vec(norm2_g), vec(sc2 + mlp_token[0:1, 0:1]), vec(g1)],
        b_spec=pl.BlockSpec((None, D_MODEL, w1_cols), lambda i, j, k: (k, 0, 0)),
        out_specs=[out_tile(tm_e, D_MODEL)] * 2 + [sums_spec] * 4)

    (d_ycat,) = _matmul("out_proj_dx", dm1, w_out_full, "nt", (n_lat_rows, D_MODEL, D_MODEL), (tm, D_MODEL, D_MODEL),
                        [((n_lat_rows, D_MODEL), F32)])
    (g_w_out,) = _matmul("out_proj_dw", ycat, dm1, "tn", (D_MODEL, D_MODEL, n_lat_rows), (D_MODEL, D_MODEL, 512),
                         [((D_MODEL, D_MODEL), F32)])

    dy, g_glu, dd8 = _glu_bwd(d_ycat, z_all, y_dir[0], y_dir[1], d_skip, glu_full, n_lat_rows)
    proj_send, proj_recv, proj_src, proj_land, proj_token = _exchange_start(
        "scatter_proj_start",
        [g_w_out.reshape(NDEV, D_MODEL // NDEV, D_MODEL), g_glu.reshape(NDEV, S5_WIDTH // NDEV, S5_WIDTH)], [False] * 2)
    perm = perm + proj_token[0:1, 0:1].astype(BF16)
    du, g_lam_re, g_lam_im, g_ldt, g_b_re, g_b_im, g_c_re, g_c_im = [], [], [], [], [], [], [], []

    def diag(mat):
        return jnp.diagonal(mat.reshape(S5_BLOCKS, groups_per_block, S5_GROUP, 2, groups_per_block, S5_STATE), axis1=1, axis2=4)

    for d in range(2):
        _, _, adj, bmat, cmat = disc[d]
        du_d, d_bmat, d_cmat, d_abar8 = _s5_scan_bwd(f"s5_scan_bwd{d}", d == 0, dy, z_all, states[d], bmat, cmat, adj,
                                                     perm, perm_t)
        du.append(du_d)
        d_bbar = jnp.transpose(diag(d_bmat), (2, 1, 0, 4, 3)).reshape(2 * S5_GROUP, NSTATE)
        d_c = jnp.transpose(diag(d_cmat), (2, 0, 4, 1, 3)).reshape(2, S5_GROUPS, S5_GROUP, S5_STATE)
        d_lam8, d_bt = _s5_discretise_bwd(f"s5_disc_bwd{d}", lam_re[d], lam_im[d], ldt[d], bt_re[d], bt_im[d], d_abar8, d_bbar)
        g_lam_re.append(d_lam8[0].reshape(S5_GROUPS, S5_STATE))
        g_lam_im.append(d_lam8[1].reshape(S5_GROUPS, S5_STATE))
        g_ldt.append(d_lam8[2].reshape(S5_GROUPS, S5_STATE).sum(axis=-1))
        to_gph = lambda t: jnp.transpose(t.reshape(S5_GROUP, S5_GROUPS, S5_STATE), (1, 2, 0))
        g_b_re.append(to_gph(d_bt[:S5_GROUP]))
        g_b_im.append(to_gph(d_bt[S5_GROUP:]))
        g_c_re.append(d_c[0])
        g_c_im.append(-d_c[1])

    dhc_pad, conv_sums = _conv_bwd_norm(d_ycat, hc, conv_ln_g, conv_ln_b, n_lat_rows)
    d_v, d_gate, g_conv_w8 = _conv_bwd_taps(dhc_pad, hh_pad, z_all, conv_w_full, n_lat_rows)

    dz_all = _dz_assemble(du[0], du[1], dy, d_skip, d_v, d_gate, n_lat)
    (g_w_in_full,) = _matmul("in_proj_dw", a_all, dz_all, "tn", (D_MODEL, IN_COLS, n_rows), (D_MODEL, IN_COLS, tm_all),
                             [((D_MODEL, IN_COLS), F32)])
    g_w_in_parts = jnp.transpose(g_w_in_full.reshape(D_MODEL, NDEV, IN_COLS // NDEV), (1, 0, 2)).astype(BF16)
    win_send, win_recv, win_src, win_land, win_token = _exchange_start("scatter_w_in_start", [g_w_in_parts], [False])
    (d_a_all,) = _matmul("in_proj_dx", dz_all, w_in_full + win_token[0:1, 0:1].astype(BF16), "nt",
                         (n_rows, D_MODEL, IN_COLS), (tm_all, D_MODEL, IN_COLS), [((n_rows, D_MODEL), F32)])
    grad_x, sums1 = _norm_bwd("norm1_bwd", xs, d_a_all, 0, norm1_g, sc1, res=d_h1, aux=mix)
    (sums1c,) = _norm_bwd("norm1_bwd_ctx", cs, d_a_all, n_lat, norm1_g, modc[1:2])

    s1, s1c, s2 = sums1.sum(axis=1), sums1c.sum(axis=1), [p.sum(axis=(0, 1)) for p in sums2]
    d_mod = jnp.concatenate([s1[0], s1[1], s1[3], s2[0], s2[1], s2[3]])
    d_modc = jnp.concatenate([s1c[0], s1c[1], jnp.zeros((4 * D_MODEL,), F32)])
    (dmod_g,), _ = _exchange("gather_dmod", [jnp.stack([d_mod, d_modc])], [True])
    dmod16 = jnp.concatenate([dmod_g[:, 0], dmod_g[:, 1]])
    dmod16_loc = lax.dynamic_slice(dmod16, (0, me * ada_cols), (16, ada_cols))
    cond_bwd = jnp.concatenate([c_all, jnp.broadcast_to(c_ctx[None], (NDEV, D_MODEL))])
    g_ada_w, g_c_ctx8 = _ada_bwd(cond_bwd, dmod16_loc, ada_w[0], c_ctx[None])

    small_parts = dict(
        c_ctx=g_c_ctx8[0], ada_b=d_mod + d_modc, norm1_g=s1[2] + s1c[2],
        s5_lam_re=jnp.stack(g_lam_re), s5_lam_im=jnp.stack(g_lam_im), s5_log_dt=jnp.stack(g_ldt),
        s5_b_re=jnp.stack(g_b_re), s5_b_im=jnp.stack(g_b_im), s5_c_re=jnp.stack(g_c_re), s5_c_im=jnp.stack(g_c_im),
        s5_d=dd8.sum(axis=0), conv_b=conv_sums[0].sum(axis=0), conv_ln_g=conv_sums[1].sum(axis=0),
        conv_ln_b=conv_sums[2].sum(axis=0), norm2_g=s2[2], final_g=d_final_g8.sum(axis=(0, 1)))
    small_g = _pack_small([small_parts[n] for n in SMALL]).reshape(NDEV, SMALL_ROWS // NDEV, D_MODEL)
    g_conv_w_parts = jnp.transpose(g_conv_w8.sum(axis=1).reshape(CONV_K, NDEV, CONV_WIDTH // NDEV), (1, 0, 2))

    res = {}

    def own_chunk(src):
        return lax.dynamic_index_in_dim(src, me, 0, keepdims=False)

    def adamw_big(name, parts):
        outs = _adamw("adamw_" + name, weights[name][0], parts, mom1[name][0], mom2[name][0])
        res[name] = [o[None] for o in outs]
        return outs[0]

    sm_send, sm_recv, sm_src, sm_land, sm_token = _exchange_start("scatter_small_start", [g_conv_w_parts, small_g],
                                                                  [False] * 2)
    done = adamw_big("ada_w", g_ada_w[None] + sm_token[0:1, 0:1])
    mlp_src, mlp_landed = _exchange_wait("scatter_mlp_wait", mlp_send, mlp_recv, mlp_src, mlp_land, [False] * 2, done)
    p_w1, p_w2 = [_with_own(l, own_chunk(s), me) for l, s in zip(mlp_landed, mlp_src)]
    adamw_big("mlp_w1", p_w1)
    done = adamw_big("mlp_w2", p_w2)
    sm_src, sm_landed = _exchange_wait("scatter_small_wait", sm_send, sm_recv, sm_src, sm_land, [False] * 2, done)
    p_conv_w, p_small = [_with_own(l, own_chunk(s), me) for l, s in zip(sm_landed, sm_src)]
    ga_send, ga_recv, ga_src, ga_land, ga_token = _exchange_start("gather_small_start", [_sum_parts(p_small)], [True])
    proj_src, proj_landed = _exchange_wait("scatter_proj_wait", proj_send, proj_recv, proj_src, proj_land, [False] * 2,
                                           ga_token)
    p_w_out, p_glu = [_with_own(l, own_chunk(s), me) for l, s in zip(proj_landed, proj_src)]
    adamw_big("w_out", p_w_out)
    done = adamw_big("s5_w_glu", p_glu)
    win_src, win_landed = _exchange_wait("scatter_w_in_wait", win_send, win_recv, win_src, win_land, [False], done)
    adamw_big("w_in", _with_own(win_landed[0], own_chunk(win_src[0]), me))
    done = adamw_big("conv_w", p_conv_w)
    ga_own, ga_landed = _exchange_wait("gather_small_wait", ga_send, ga_recv, ga_src, ga_land, [True], done)
    small_all = _with_own(ga_landed[0], ga_own[0], me).reshape(1, SMALL_ROWS, D_MODEL)
    small_like = [weights[n] for n in SMALL]
    outs = _adamw("adamw_small", _pack_small(small_like), small_all, _pack_small([mom1[n] for n in SMALL]),
                  _pack_small([mom2[n] for n in SMALL]))
    unpacked = [_unpack_small(o, small_like) for o in outs]
    for i, name in enumerate(SMALL):
        res[name] = [u[i] for u in unpacked]

    return (loss, grad_x[None], *[res[n][0] for n in order], *[res[n][1] for n in order],
            *[res[n][2] for n in order], *[res[n][3] for n in order])
```

```python
import functools

import jax
import jax.numpy as jnp
from jax import lax
from jax.experimental import pallas as pl
from jax.experimental.pallas import tpu as pltpu

F32 = jnp.float32
BF16 = jnp.bfloat16
MESH = pl.DeviceIdType.MESH
ANY = pl.BlockSpec(memory_space=pl.ANY)

NDEV = 8
D_MODEL = 1024
GRID_W = 64
S5_WIDTH = 512
S5_GROUP = 16
S5_GROUPS = 32
S5_STATE = 64
NSTATE = S5_GROUPS * S5_STATE
CONV_WIDTH = 512
CONV_K = 31
IN_COLS = S5_WIDTH + 2 * CONV_WIDTH
D_FF = 4 * D_MODEL
EPS_RMS = 1e-6
EPS_LN = 1e-5
ADAM_LR = 0.001
ADAM_B1 = 0.9
ADAM_B2 = 0.999
ADAM_EPS = 1e-08
ADAM_WD = 0.01
ADAM_STEP = 10

SUBLANES = 8
LANES = 128
ROW_BLOCK = 256
SCAN_LANES = 512
SEGMENTS = SUBLANES
STEPS = ROW_BLOCK // SEGMENTS
S5_BLOCKS = 4
S5_BLOCK_WIDTH = S5_WIDTH // S5_BLOCKS
CONV_ROWS = 64
VMEM_LIMIT = 48 * 1024 * 1024
SMALL_ROWS = 320


def _params(sem=None):
    kw = dict(vmem_limit_bytes=VMEM_LIMIT)
    if sem is not None:
        kw["dimension_semantics"] = sem
    return pltpu.CompilerParams(**kw)


def _sds(shape, dtype=F32):
    return jax.ShapeDtypeStruct(tuple(shape), dtype)


def _fold8(x):
    return x.reshape(x.shape[0] // SUBLANES, SUBLANES, x.shape[1]).sum(axis=0)


def _sigmoid(x):
    return 1.0 / (1.0 + jnp.exp(-x))


def _silu(x):
    return x * _sigmoid(x)


def _dsilu(x):
    s = _sigmoid(x)
    return s * (1.0 + x * (1.0 - s))


_GELU_C = 0.7978845608028654


def _gelu(x):
    return 0.5 * x * (1.0 + jnp.tanh(_GELU_C * (x + 0.044715 * x * x * x)))


def _dgelu(x):
    t = jnp.tanh(_GELU_C * (x + 0.044715 * x * x * x))
    return 0.5 * (1.0 + t) + 0.5 * x * (1.0 - t * t) * _GELU_C * (1.0 + 3.0 * 0.044715 * x * x)


def _rms(x):
    rstd = lax.rsqrt(jnp.mean(x * x, axis=-1, keepdims=True) + EPS_RMS)
    return x * rstd, rstd


def _epi_residual_prenorm(acc, res, gate, gain, scale, shift):
    h = res + gate * acc
    xh, _ = _rms(h)
    return acc, h, (xh * gain) * (1.0 + scale) + shift


def _epi_residual_loss(acc, res, gate, target, gain):
    h = res + gate * acc
    xh, rstd = _rms(h)
    err = xh * gain - target
    dy = err * (1.0 / h.shape[-1])
    dxh = dy * gain
    dh = rstd * (dxh - xh * jnp.mean(dxh * xh, axis=-1, keepdims=True))
    return acc, dh, dh * gate, _fold8(err * err), _fold8(dy * xh)


def _epi_norm_bwd(d_act, x, res, aux, gain, scale, gate):
    xh, rstd = _rms(x)
    dn = d_act * (1.0 + scale)
    dxh = dn * gain
    dx = res + rstd * (dxh - xh * jnp.mean(dxh * xh, axis=-1, keepdims=True))
    return dx, dx * gate, _fold8(d_act), _fold8(d_act * (xh * gain)), _fold8(dn * xh), _fold8(res * aux)


def _dot(a, b, mode):
    dims = {"nn": (((1,), (0,)), ((), ())), "nt": (((1,), (1,)), ((), ())), "tn": (((0,), (0,)), ((), ()))}[mode]
    return lax.dot_general(a, b, dims, preferred_element_type=F32)


def _peers(x, y, c):
    out = []
    for k in range(1, NDEV):
        px = 1 - x if k & 4 else x
        py = 1 - y if k & 2 else y
        pc = 1 - c if k & 1 else c
        out.append(((px, py, pc), 4 * px + 2 * py + pc))
    return out


def _exchange_copies(src, land, send_sems, recv_sems, gather):
    x, y, c = lax.axis_index("x"), lax.axis_index("y"), lax.axis_index("c")
    me = 4 * x + 2 * y + c
    out = []
    for a in range(len(src)):
        for k, (peer, plin) in enumerate(_peers(x, y, c)):
            chunk = src[a] if gather[a] else src[a].at[plin]
            sems = dict(send_sem=send_sems.at[a * (NDEV - 1) + k], recv_sem=recv_sems.at[a * (NDEV - 1) + k],
                        device_id=peer, device_id_type=MESH)
            out.append((pltpu.make_async_remote_copy(src_ref=chunk, dst_ref=land[a].at[me], **sems),
                        pltpu.make_async_remote_copy(src_ref=chunk, dst_ref=land[a].at[plin], **sems)))
    return out


def _exchange(name, srcs, gather):
    n = len(srcs)
    outs = [_sds(((NDEV,) + s.shape) if g else s.shape, s.dtype) for s, g in zip(srcs, gather)]

    def body(*refs):
        src, dst, token = refs[:n], refs[n:2 * n], refs[2 * n]
        send_sems, recv_sems, local_sems = refs[2 * n + 1:]
        me = 4 * lax.axis_index("x") + 2 * lax.axis_index("y") + lax.axis_index("c")
        local = [pltpu.make_async_copy(src[a] if gather[a] else src[a].at[me], dst[a].at[me], local_sems.at[a])
                 for a in range(n)]
        for copy in local:
            copy.start()
        copies = _exchange_copies(src, dst, send_sems, recv_sems, gather)
        for copy, _ in copies:
            copy.start()
        token[...] = jnp.zeros_like(token)
        for copy, landing in copies:
            copy.wait_send()
            landing.wait_recv()
        for copy in local:
            copy.wait()

    nsem = n * (NDEV - 1)
    out = pl.pallas_call(
        body, name=name, out_shape=outs + [_sds((SUBLANES, LANES))], in_specs=[ANY] * n,
        out_specs=[ANY] * n + [pl.BlockSpec(memory_space=pltpu.VMEM)],
        scratch_shapes=[pltpu.SemaphoreType.DMA((nsem,)), pltpu.SemaphoreType.DMA((nsem,)), pltpu.SemaphoreType.DMA((n,))],
    )(*srcs)
    return out[:n], out[n]


HBM = pl.BlockSpec(memory_space=pltpu.HBM)
SEM = pl.BlockSpec(memory_space=pltpu.SEMAPHORE)
EFFECT = pltpu.SideEffectType.DATAFLOW_SIDE_EFFECTING


def _exchange_start(name, srcs, gather):
    n = len(srcs)
    lands = [lax.empty(((NDEV,) + s.shape) if g else s.shape, s.dtype) for s, g in zip(srcs, gather)]

    def body(*refs):
        src, land = refs[:n], refs[n:2 * n]
        send_sems, recv_sems = refs[2 * n], refs[2 * n + 1]
        token = refs[-1]
        for copy, _ in _exchange_copies(src, land, send_sems, recv_sems, gather):
            copy.start()
        token[...] = jnp.zeros_like(token)

    hbm = lambda v: pltpu.HBM(v.shape, v.dtype)
    nsem = n * (NDEV - 1)
    out = pl.pallas_call(
        body, name=name,
        out_shape=(pltpu.SemaphoreType.DMA((nsem,)), pltpu.SemaphoreType.DMA((nsem,)), *[hbm(v) for v in srcs],
                   *[hbm(v) for v in lands], _sds((SUBLANES, LANES))),
        in_specs=[HBM] * (2 * n), out_specs=(SEM, SEM, *([HBM] * (2 * n)), pl.BlockSpec(memory_space=pltpu.VMEM)),
        input_output_aliases={i: 2 + i for i in range(2 * n)},
        compiler_params=pltpu.CompilerParams(has_side_effects=EFFECT),
    )(*[pltpu.with_memory_space_constraint(v, pltpu.HBM) for v in list(srcs) + lands])
    return out[0], out[1], out[2:2 + n], out[2 + n:2 + 2 * n], out[-1]


def _exchange_wait(name, send_sems, recv_sems, srcs, lands, gather, after):
    n = len(srcs)

    def body(*refs):
        src, land = refs[:n], refs[n:2 * n]
        send_ref, recv_ref = refs[2 * n], refs[2 * n + 1]
        for copy, landing in _exchange_copies(src, land, send_ref, recv_ref, gather):
            copy.wait_send()
            landing.wait_recv()

    hbm = lambda v: pltpu.HBM(v.shape, v.dtype)
    out = pl.pallas_call(
        body, name=name, out_shape=[hbm(v) for v in list(srcs) + list(lands)],
        in_specs=[HBM] * (2 * n) + [SEM, SEM, ANY], out_specs=[HBM] * (2 * n),
        input_output_aliases={i: i for i in range(2 * n)},
        compiler_params=pltpu.CompilerParams(has_side_effects=EFFECT),
    )(*srcs, *lands, send_sems, recv_sems, after)
    return out[:n], out[n:]


def _with_own(landed, own, me):
    return lax.dynamic_update_slice(landed, own[None], (me,) + (0,) * own.ndim)


def _matmul(name, a, b, mode, mnk, tiles, outs, a_spec=None, b_spec=None, a_fn=None, a_extra=(),
            epi=None, epi_extra=(), out_specs=None):
    m_, n_, k_ = mnk
    tm, tn, tk = tiles
    nk = k_ // tk
    if a_spec is None:
        a_spec = (pl.BlockSpec((tk, tm), lambda i, j, k: (k, i)) if mode == "tn"
                  else pl.BlockSpec((tm, tk), lambda i, j, k: (i, k)))
    if b_spec is None:
        b_spec = (pl.BlockSpec((tn, tk), lambda i, j, k: (j, k)) if mode == "nt"
                  else pl.BlockSpec((tk, tn), lambda i, j, k: (k, j)))
    if out_specs is None:
        out_specs = [pl.BlockSpec((tm, tn), lambda i, j, k: (i, j)) for _ in outs]
    na, ne, no = len(a_extra), len(epi_extra), len(outs)

    def body(*refs):
        a_ref, b_ref = refs[0], refs[1]
        ax = refs[2:2 + na]
        ex = refs[2 + na:2 + na + ne]
        o = refs[2 + na + ne:2 + na + ne + no]

        def finish(res):
            res = epi(res, *[r[...] for r in ex]) if epi is not None else (res,)
            for ref, val in zip(o, res):
                ref[...] = val.astype(ref.dtype)

        at = a_ref[...]
        if a_fn is not None:
            at = a_fn(at, *[r[...] for r in ax])
        part = _dot(at.astype(BF16), b_ref[...].astype(BF16), mode)
        if nk == 1:
            finish(part)
            return
        acc = refs[-1]
        k = pl.program_id(2)

        @pl.when(k == 0)
        def _():
            acc[...] = part

        @pl.when(k > 0)
        def _():
            acc[...] += part

        @pl.when(k == nk - 1)
        def _():
            finish(acc[...])

    return pl.pallas_call(
        body, name=name, grid=(m_ // tm, n_ // tn, nk),
        in_specs=[a_spec, b_spec] + [s for _, s in a_extra] + [s for _, s in epi_extra],
        out_specs=out_specs, out_shape=[_sds(s, d) for s, d in outs],
        scratch_shapes=[pltpu.VMEM((tm, tn), F32)] if nk > 1 else [],
        compiler_params=_params(("parallel", "parallel", "arbitrary")),
    )(a, b, *[x for x, _ in a_extra], *[x for x, _ in epi_extra])


def _prenorm(name, x, ctx, gain, shsc):
    n_lat = x.shape[0] // ROW_BLOCK
    n_ctx = 0 if ctx is None else ctx.shape[0] // ROW_BLOCK
    d = x.shape[1]

    def norm(src, g_ref, m_ref, o_ref):
        xv = src[...]
        xh = xv * lax.rsqrt(jnp.mean(xv * xv, axis=-1, keepdims=True) + EPS_RMS)
        o_ref[...] = ((xh * g_ref[...]) * (1.0 + m_ref[1:2, :]) + m_ref[0:1, :]).astype(o_ref.dtype)

    def body(*refs):
        if ctx is None:
            x_ref, g_ref, m_ref, o_ref = refs
            norm(x_ref, g_ref, m_ref, o_ref)
        else:
            x_ref, c_ref, g_ref, m_ref, o_ref = refs
            i = pl.program_id(0)

            @pl.when(i < n_lat)
            def _():
                norm(x_ref, g_ref, m_ref, o_ref)

            @pl.when(i >= n_lat)
            def _():
                norm(c_ref, g_ref, m_ref, o_ref)

    in_specs = [pl.BlockSpec((ROW_BLOCK, d), lambda i: (jnp.minimum(i, n_lat - 1), 0))]
    args = [x]
    if ctx is not None:
        in_specs.append(pl.BlockSpec((ROW_BLOCK, d), lambda i: (jnp.maximum(i - n_lat, 0), 0)))
        args.append(ctx)
    in_specs += [pl.BlockSpec((1, d), lambda i: (0, 0)),
                 pl.BlockSpec((None, 2, d), lambda i: (jnp.minimum(i // n_lat, 1), 0, 0))]
    args += [gain, shsc]
    return pl.pallas_call(
        body, name=name, grid=(n_lat + n_ctx,), in_specs=in_specs,
        out_specs=pl.BlockSpec((ROW_BLOCK, d), lambda i: (i, 0)),
        out_shape=_sds(((n_lat + n_ctx) * ROW_BLOCK, d), BF16),
        compiler_params=_params(("parallel",)),
    )(*args)


def _norm_bwd(name, x, d_act, d_act_row0, gain, scale, res=None, aux=None, gate=None):
    rows, d = x.shape
    nb = rows // ROW_BLOCK
    has_res = res is not None
    has_gate = gate is not None

    def body(*refs):
        if has_gate:
            x_ref, da_ref, g_ref, sc_ref, r_ref, aux_ref, gate_ref, dx_ref, dm_ref, sums = refs
        elif has_res:
            x_ref, da_ref, g_ref, sc_ref, r_ref, aux_ref, dx_ref, sums = refs
        else:
            x_ref, da_ref, g_ref, sc_ref, sums = refs
        i = pl.program_id(0)

        @pl.when(i == 0)
        def _():
            sums[...] = jnp.zeros_like(sums)

        xv, da = x_ref[...], da_ref[...]
        rstd = lax.rsqrt(jnp.mean(xv * xv, axis=-1, keepdims=True) + EPS_RMS)
        xh = xv * rstd
        g = g_ref[...]
        dn = da * (1.0 + sc_ref[...])
        sums[0] += _fold8(da)
        sums[1] += _fold8(da * (xh * g))
        sums[2] += _fold8(dn * xh)
        if has_res:
            dxh = dn * g
            dx = rstd * (dxh - xh * jnp.mean(dxh * xh, axis=-1, keepdims=True))
            rv = r_ref[...]
            dx_ref[...] = rv + dx
            sums[3] += _fold8(rv * aux_ref[...])
            if has_gate:
                dm_ref[...] = ((rv + dx) * gate_ref[...]).astype(dm_ref.dtype)

    row = lambda i: (i, 0)
    vec = pl.BlockSpec((1, d), lambda i: (0, 0))
    in_specs = [pl.BlockSpec((ROW_BLOCK, d), row), pl.BlockSpec((ROW_BLOCK, d), lambda i: (i + d_act_row0, 0)), vec, vec]
    args = [x, d_act, gain, scale]
    out_shape = [_sds((4, SUBLANES, d))]
    out_specs = [pl.BlockSpec((4, SUBLANES, d), lambda i: (0, 0, 0))]
    if has_res:
        in_specs += [pl.BlockSpec((ROW_BLOCK, d), row), pl.BlockSpec((ROW_BLOCK, d), row)]
        args += [res, aux]
        if has_gate:
            in_specs.append(vec)
            args.append(gate)
            out_shape = [_sds((rows, d), BF16)] + out_shape
            out_specs = [pl.BlockSpec((ROW_BLOCK, d), row)] + out_specs
        out_shape = [_sds((rows, d))] + out_shape
        out_specs = [pl.BlockSpec((ROW_BLOCK, d), row)] + out_specs
    return pl.pallas_call(
        body, name=name, grid=(nb,), in_specs=in_specs, out_specs=out_specs, out_shape=out_shape,
        compiler_params=_params(("arbitrary",)),
    )(*args)


def _loss_head(h2, target, gain, gate):
    rows, d = h2.shape

    def body(h_ref, t_ref, g_ref, gate_ref, dh_ref, dm_ref, err_ref, dg_ref):
        i = pl.program_id(0)

        @pl.when(i == 0)
        def _():
            err_ref[...] = jnp.zeros_like(err_ref)
            dg_ref[...] = jnp.zeros_like(dg_ref)

        hv = h_ref[...]
        rstd = lax.rsqrt(jnp.mean(hv * hv, axis=-1, keepdims=True) + EPS_RMS)
        xh = hv * rstd
        g = g_ref[...]
        err = xh * g - t_ref[...]
        err_ref[...] += _fold8(err * err)
        dy = err * (1.0 / d)
        dg_ref[...] += _fold8(dy * xh)
        dxh = dy * g
        dh = rstd * (dxh - xh * jnp.mean(dxh * xh, axis=-1, keepdims=True))
        dh_ref[...] = dh
        dm_ref[...] = (dh * gate_ref[...]).astype(dm_ref.dtype)

    row = pl.BlockSpec((ROW_BLOCK, d), lambda i: (i, 0))
    acc = pl.BlockSpec((SUBLANES, d), lambda i: (0, 0))
    vec = pl.BlockSpec((1, d), lambda i: (0, 0))
    return pl.pallas_call(
        body, name="loss_head", grid=(rows // ROW_BLOCK,),
        in_specs=[row, row, vec, vec], out_specs=[row, row, acc, acc],
        out_shape=[_sds((rows, d)), _sds((rows, d), BF16), _sds((SUBLANES, d)), _sds((SUBLANES, d))],
        compiler_params=_params(("arbitrary",)),
    )(h2, target, gain, gate)


def _ada_fwd(cond16, ada_w_loc, ada_b_loc):
    cols = ada_w_loc.shape[1]

    def body(c_ref, w_ref, b_ref, o_ref):
        s = _silu(c_ref[...]).astype(BF16)
        o_ref[...] = _dot(s, w_ref[...].astype(BF16), "nn") + b_ref[...]

    return pl.pallas_call(body, name="ada_fwd", out_shape=_sds((16, cols)), compiler_params=_params())(
        cond16, ada_w_loc, ada_b_loc)


def _ada_bwd(cond16, dmod16, ada_w_loc, c_ctx_row):
    k_, cols = ada_w_loc.shape

    def body(c_ref, dm_ref, w_ref, cc_ref, gw_ref, gc_ref):
        s = _silu(c_ref[...]).astype(BF16)
        dm = dm_ref[...]
        gw_ref[...] = _dot(s, dm.astype(BF16), "tn")
        dmc = jnp.sum(dm[8:16, :], axis=0, keepdims=True)
        dmc8 = jnp.broadcast_to(dmc, (SUBLANES, cols)).astype(BF16)
        ds = _dot(dmc8, w_ref[...].astype(BF16), "nt")
        row = lax.broadcasted_iota(jnp.int32, ds.shape, 0)
        gc_ref[...] = jnp.where(row == 0, ds * _dsilu(cc_ref[...]), 0.0)

    return pl.pallas_call(body, name="ada_bwd", out_shape=[_sds((k_, cols)), _sds((SUBLANES, k_))],
                          compiler_params=_params())(cond16, dmod16, ada_w_loc, c_ctx_row)


def _cmul(a, b):
    return a[0] * b[0] - a[1] * b[1], a[0] * b[1] + a[1] * b[0]


def _disc(lam_re, lam_im, ldt):
    dt = jnp.exp(ldt)
    mag = jnp.exp(lam_re * dt)
    th = lam_im * dt
    a_re, a_im = mag * jnp.cos(th), mag * jnp.sin(th)
    den = lam_re * lam_re + lam_im * lam_im
    n_re = a_re - 1.0
    f_re = (n_re * lam_re + a_im * lam_im) / den
    f_im = (a_im * lam_re - n_re * lam_im) / den
    return dt, mag, th, a_re, a_im, den, n_re, f_re, f_im


def _block_diag_mask(shape):
    row = lax.broadcasted_iota(jnp.int32, shape, 0)
    col = lax.broadcasted_iota(jnp.int32, shape, 1)
    return lax.shift_right_logical(row, 4) == lax.shift_right_logical(col, 6)


TAB_A = 0
TAB_BIG = 1
TAB_SEG = 4
TAB_PW = 5
TAB_ROWS = TAB_PW + STEPS


def _s5_discretise(name, ascending, lam_re, lam_im, ldt, bt_re, bt_im, ct_re, ct_im):
    def write_tables(ref, pw, big, asc, sign):
        row = lax.broadcasted_iota(jnp.int32, (SUBLANES, NSTATE), 0)
        full = lambda v: jnp.broadcast_to(v, (SUBLANES, NSTATE))

        def put(t, p):
            ref[0, t] = full(p[0])
            ref[1, t] = full(sign * p[1])

        put(TAB_A, pw[0])
        for t in range(3):
            put(TAB_BIG + t, big[t])
        seg = [big[0]]
        for _ in range(SEGMENTS - 1):
            seg.append(_cmul(seg[-1], big[0]))
        seg_re = jnp.zeros((SUBLANES, NSTATE), F32)
        seg_im = jnp.zeros((SUBLANES, NSTATE), F32)
        for r in range(SEGMENTS):
            p = seg[r] if asc else seg[SEGMENTS - 1 - r]
            seg_re = jnp.where(row == r, p[0], seg_re)
            seg_im = jnp.where(row == r, sign * p[1], seg_im)
        ref[0, TAB_SEG] = seg_re
        ref[1, TAB_SEG] = seg_im
        for k in range(STEPS):
            put(TAB_PW + k, pw[k])

    def body(lr_ref, li_ref, ldt_ref, br_ref, bi_ref, cr_ref, ci_ref, bb_ref, tab_ref, adj_ref, bm_ref, cm_ref):
        _, _, _, a_re, a_im, _, _, f_re, f_im = _disc(lr_ref[...], li_ref[...], ldt_ref[...])
        bre, bim = br_ref[...], bi_ref[...]
        bb_re = f_re * bre - f_im * bim
        bb_im = f_re * bim + f_im * bre
        bb_ref[0:S5_GROUP, :] = bb_re
        bb_ref[S5_GROUP:2 * S5_GROUP, :] = bb_im
        pw = [(a_re, a_im)]
        for _ in range(STEPS - 1):
            pw.append(_cmul(pw[-1], (a_re, a_im)))
        big = [pw[STEPS - 1]]
        for _ in range(2):
            big.append(_cmul(big[-1], big[-1]))
        write_tables(tab_ref, pw, big, ascending, 1.0)
        write_tables(adj_ref, pw, big, not ascending, -1.0)
        half = NSTATE // S5_BLOCKS
        mask = _block_diag_mask((S5_BLOCK_WIDTH, half))
        tile = lambda v: jnp.broadcast_to(v[None], (S5_BLOCK_WIDTH // S5_GROUP, S5_GROUP, half)).reshape(S5_BLOCK_WIDTH, half)
        for c in range(S5_BLOCKS):
            cols = slice(c * half, (c + 1) * half)
            rows = slice(c * S5_BLOCK_WIDTH, (c + 1) * S5_BLOCK_WIDTH)
            bm_ref[c, :, 0:half] = jnp.where(mask, tile(bb_re[:, cols]), 0.0).astype(BF16)
            bm_ref[c, :, half:2 * half] = jnp.where(mask, tile(bb_im[:, cols]), 0.0).astype(BF16)
            cm_ref[c, :, 0:half] = jnp.where(mask, cr_ref[rows, :], 0.0).astype(BF16)
            cm_ref[c, :, half:2 * half] = jnp.where(mask, -ci_ref[rows, :], 0.0).astype(BF16)

    blocked = _sds((S5_BLOCKS, S5_BLOCK_WIDTH, 2 * NSTATE // S5_BLOCKS), BF16)
    return pl.pallas_call(
        body, name=name,
        out_shape=[_sds((2 * S5_GROUP, NSTATE)), _sds((2, TAB_ROWS, SUBLANES, NSTATE)),
                   _sds((2, TAB_ROWS, SUBLANES, NSTATE)), blocked, blocked],
        compiler_params=_params(),
    )(lam_re, lam_im, ldt, bt_re, bt_im, ct_re, ct_im)


def _s5_discretise_bwd(name, lam_re, lam_im, ldt, bt_re, bt_im, d_abar8, d_bbar):
    def body(lr_ref, li_ref, ldt_ref, br_ref, bi_ref, da_ref, db_ref, dl_ref, dbt_ref):
        lam_re, lam_im = lr_ref[...], li_ref[...]
        dt, mag, _, a_re, a_im, den, n_re, f_re, f_im = _disc(lam_re, lam_im, ldt_ref[...])
        bre, bim = br_ref[...], bi_ref[...]
        dbr, dbi = db_ref[0:S5_GROUP, :], db_ref[S5_GROUP:2 * S5_GROUP, :]
        dbt_ref[0:S5_GROUP, :] = f_re * dbr + f_im * dbi
        dbt_ref[S5_GROUP:2 * S5_GROUP, :] = f_re * dbi - f_im * dbr
        df_re = jnp.sum(bre * dbr + bim * dbi, axis=0, keepdims=True)
        df_im = jnp.sum(bre * dbi - bim * dbr, axis=0, keepdims=True)
        da = da_ref[...]
        da_re = jnp.sum(da[:, 0:NSTATE], axis=0, keepdims=True)
        da_im = jnp.sum(da[:, NSTATE:2 * NSTATE], axis=0, keepdims=True)
        da_re = da_re + (df_re * lam_re - df_im * lam_im) / den
        da_im = da_im + (df_re * lam_im + df_im * lam_re) / den
        ff = (f_re * df_re + f_im * df_im) * 2.0 / den
        d_lr = (df_re * n_re + df_im * a_im) / den - ff * lam_re
        d_li = (df_re * a_im - df_im * n_re) / den - ff * lam_im
        d_mag = (da_re * a_re + da_im * a_im) / mag
        d_th = da_im * a_re - da_re * a_im
        d_lr = d_lr + d_mag * mag * dt
        d_li = d_li + d_th * dt
        d_ldt = (d_mag * mag * lam_re + d_th * lam_im) * dt
        row = lax.broadcasted_iota(jnp.int32, (SUBLANES, NSTATE), 0)
        dl_ref[...] = jnp.where(row == 0, d_lr, jnp.where(row == 1, d_li, jnp.where(row == 2, d_ldt, 0.0)))

    return pl.pallas_call(
        body, name=name, out_shape=[_sds((SUBLANES, NSTATE)), _sds((2 * S5_GROUP, NSTATE))],
        compiler_params=_params(),
    )(lam_re, lam_im, ldt, bt_re, bt_im, d_abar8, d_bbar)


def _segment_permutation():
    rho = jnp.arange(ROW_BLOCK)
    src = STEPS * (rho % SEGMENTS) + rho // SEGMENTS
    return (src[:, None] == jnp.arange(ROW_BLOCK)[None, :]).astype(BF16)


def _permute_rows(perm_ref, v):
    return _dot(perm_ref[...], v, "nn").astype(BF16)


def _unpermute_rows(perm_t_ref, v):
    hi = v.astype(BF16)
    lo = (v - hi.astype(F32)).astype(BF16)
    return _dot(perm_t_ref[...], hi, "nn") + _dot(perm_t_ref[...], lo, "nn")


def _scan_chunk(x_ref, out_ref, tab_ref, carry_re, carry_im, ascending, pair_ref=None, acc_ref=None):
    w = SCAN_LANES
    half = NSTATE // S5_BLOCKS
    row = lax.broadcasted_iota(jnp.int32, (SUBLANES, w), 0)
    last = (SEGMENTS - 1) if ascending else 0

    def from_previous_segment(v, k, fill):
        if ascending:
            return jnp.where(row >= k, pltpu.roll(v, k, 0), fill)
        return jnp.where(row < SEGMENTS - k, pltpu.roll(v, SEGMENTS - k, 0), fill)

    def tile_rows(k):
        return pl.ds(pl.multiple_of((k if ascending else STEPS - 1 - k) * SUBLANES, SUBLANES), SUBLANES)

    for j in range(NSTATE // w):
        n_l = pl.ds(j * w, w)
        lane0 = (j * w // half) * 2 * half + (j * w) % half
        re_l, im_l = pl.ds(lane0, w), pl.ds(lane0 + half, w)
        tab = lambda t, n_l=n_l: (tab_ref[0, t, :, n_l], tab_ref[1, t, :, n_l])
        a_re, a_im = tab(TAB_A)

        def local_step(k, h):
            rs = tile_rows(k)
            h_re = a_re * h[0] - a_im * h[1] + x_ref[rs, re_l]
            h_im = a_re * h[1] + a_im * h[0] + x_ref[rs, im_l]
            out_ref[rs, re_l] = h_re
            out_ref[rs, im_l] = h_im
            return h_re, h_im

        zero = jnp.zeros((SUBLANES, w), F32)
        end_re, end_im = lax.fori_loop(0, STEPS, local_step, (zero, zero))
        for t, k in ((TAB_BIG, 1), (TAB_BIG + 1, 2), (TAB_BIG + 2, 4)):
            p_re, p_im = tab(t)
            s_re, s_im = from_previous_segment(end_re, k, 0.0), from_previous_segment(end_im, k, 0.0)
            end_re, end_im = end_re + (p_re * s_re - p_im * s_im), end_im + (p_re * s_im + p_im * s_re)
        c0_re, c0_im = carry_re[:, n_l], carry_im[:, n_l]
        p_re, p_im = tab(TAB_SEG)
        end_re = end_re + (p_re * c0_re - p_im * c0_im)
        end_im = end_im + (p_re * c0_im + p_im * c0_re)
        carry_re[:, n_l] = jnp.broadcast_to(end_re[last:last + 1, :], end_re.shape)
        carry_im[:, n_l] = jnp.broadcast_to(end_im[last:last + 1, :], end_im.shape)
        in_re = from_previous_segment(end_re, 1, c0_re)
        in_im = from_previous_segment(end_im, 1, c0_im)

        def carry_step(k, st):
            rs = tile_rows(k)
            p_re, p_im = tab_ref[0, TAB_PW + k, :, n_l], tab_ref[1, TAB_PW + k, :, n_l]
            o_re = out_ref[rs, re_l] + (p_re * in_re - p_im * in_im)
            o_im = out_ref[rs, im_l] + (p_re * in_im + p_im * in_re)
            out_ref[rs, re_l] = o_re
            out_ref[rs, im_l] = o_im
            if pair_ref is None:
                return st
            s_re, s_im = pair_ref[rs, re_l], pair_ref[rs, im_l]
            return (o_re, o_im, st[2] + (st[0] * s_re + st[1] * s_im), st[3] + (st[1] * s_re - st[0] * s_im))

        if pair_ref is None:
            lax.fori_loop(0, STEPS, carry_step, 0)
        else:
            fin = lax.fori_loop(0, STEPS, carry_step, (in_re, in_im, zero, zero))
            acc_ref[:, n_l] += fin[2]
            acc_ref[:, pl.ds(NSTATE + j * w, w)] += fin[3]


def _scan_block_index(i, n_lat, ctx_first_then_ascending):
    if ctx_first_then_ascending:
        return jnp.where(i == 0, n_lat, i - 1)
    return jnp.where(i == 0, n_lat, n_lat - i)


def _full_spec(shape):
    return pl.BlockSpec(shape, lambda i: (0,) * len(shape))


_S5_BLOCKED = (S5_BLOCKS, S5_BLOCK_WIDTH, 2 * NSTATE // S5_BLOCKS)
_S5_TABLES = (2, TAB_ROWS, SUBLANES, NSTATE)
_S5_DIAG = (S5_BLOCKS, S5_GROUP, 2 * NSTATE // S5_BLOCKS)


def _s5_scan_fwd(name, ascending, z_all, bmat, cmat, tab, perm, perm_t):
    rows = z_all.shape[0]
    nb = rows // ROW_BLOCK
    n_lat = nb - 1
    bw, sw = S5_BLOCK_WIDTH, 2 * NSTATE // S5_BLOCKS

    def body(u_ref, bm_ref, cm_ref, tab_ref, p_ref, pt_ref, s_ref, y_ref, bu, yp, carry_re, carry_im):
        @pl.when(pl.program_id(0) == 0)
        def _():
            carry_re[...] = jnp.zeros_like(carry_re)
            carry_im[...] = jnp.zeros_like(carry_im)

        up = _permute_rows(p_ref, u_ref[...].astype(BF16))
        for c in range(S5_BLOCKS):
            bu[:, c * sw:(c + 1) * sw] = _dot(up[:, c * bw:(c + 1) * bw], bm_ref[c], "nn")
        _scan_chunk(bu, s_ref, tab_ref, carry_re, carry_im, ascending)
        for c in range(S5_BLOCKS):
            yp[:, c * bw:(c + 1) * bw] = _dot(s_ref[:, c * sw:(c + 1) * sw].astype(BF16), cm_ref[c], "nt")
        y_ref[...] = _unpermute_rows(pt_ref, yp[...])

    blk = lambda i: (_scan_block_index(i, n_lat, ascending), 0)
    return pl.pallas_call(
        body, name=name, grid=(nb,),
        in_specs=[pl.BlockSpec((ROW_BLOCK, S5_WIDTH), blk), _full_spec(_S5_BLOCKED), _full_spec(_S5_BLOCKED),
                  _full_spec(_S5_TABLES), _full_spec((ROW_BLOCK, ROW_BLOCK)), _full_spec((ROW_BLOCK, ROW_BLOCK))],
        out_specs=[pl.BlockSpec((ROW_BLOCK, 2 * NSTATE), blk), pl.BlockSpec((ROW_BLOCK, S5_WIDTH), blk)],
        out_shape=[_sds((rows, 2 * NSTATE)), _sds((rows, S5_WIDTH))],
        scratch_shapes=[pltpu.VMEM((ROW_BLOCK, 2 * NSTATE), F32), pltpu.VMEM((ROW_BLOCK, S5_WIDTH), F32),
                        pltpu.VMEM((SUBLANES, NSTATE), F32), pltpu.VMEM((SUBLANES, NSTATE), F32)],
        compiler_params=_params(("arbitrary",)),
    )(z_all, bmat, cmat, tab, perm, perm_t)


def _s5_scan_bwd(name, ascending, dy, z_all, states, bmat, cmat, adj, perm, perm_t):
    rows = states.shape[0]
    nb = rows // ROW_BLOCK
    n_lat = nb - 1
    bw, sw = S5_BLOCK_WIDTH, 2 * NSTATE // S5_BLOCKS

    def block_index(i):
        if ascending:
            return jnp.where(i == nb - 1, n_lat, n_lat - 1 - i)
        return jnp.where(i == nb - 1, n_lat, i)

    def body(dy_ref, u_ref, s_ref, bm_ref, cm_ref, adj_ref, p_ref, pt_ref, du_ref, db_ref, dc_ref, da_ref,
             g, dup, db_acc, dc_acc, carry_re, carry_im):
        i = pl.program_id(0)

        @pl.when(i == 0)
        def _():
            carry_re[...] = jnp.zeros_like(carry_re)
            carry_im[...] = jnp.zeros_like(carry_im)
            da_ref[...] = jnp.zeros_like(da_ref)
            db_acc[...] = jnp.zeros_like(db_acc)
            dc_acc[...] = jnp.zeros_like(dc_acc)

        @pl.when(i < nb - 1)
        def _():
            dyp = _permute_rows(p_ref, dy_ref[...].astype(BF16))
            for c in range(S5_BLOCKS):
                g[:, c * sw:(c + 1) * sw] = _dot(dyp[:, c * bw:(c + 1) * bw], cm_ref[c], "nn")
                dc_acc[c] += _dot(dyp[:, c * bw:(c + 1) * bw], s_ref[:, c * sw:(c + 1) * sw].astype(BF16), "tn")

        @pl.when(i == nb - 1)
        def _():
            g[...] = jnp.zeros_like(g)

        _scan_chunk(g, g, adj_ref, carry_re, carry_im, not ascending, pair_ref=s_ref, acc_ref=da_ref)
        up = _permute_rows(p_ref, u_ref[...].astype(BF16))
        for c in range(S5_BLOCKS):
            gc = g[:, c * sw:(c + 1) * sw].astype(BF16)
            dup[:, c * bw:(c + 1) * bw] = _dot(gc, bm_ref[c], "nt")
            db_acc[c] += _dot(up[:, c * bw:(c + 1) * bw], gc, "tn")
        du_ref[...] = _unpermute_rows(pt_ref, dup[...])

        @pl.when(i == nb - 1)
        def _():
            mask = _block_diag_mask((bw, sw // 2))
            for acc, out in ((db_acc, db_ref), (dc_acc, dc_ref)):
                for c in range(S5_BLOCKS):
                    for part in range(2):
                        cols = slice(part * (sw // 2), (part + 1) * (sw // 2))
                        kept = jnp.where(mask, acc[c, :, cols], 0.0)
                        out[c, :, cols] = kept.reshape(bw // S5_GROUP, S5_GROUP, sw // 2).sum(axis=0)

    blk = lambda i: (block_index(i), 0)
    return pl.pallas_call(
        body, name=name, grid=(nb,),
        in_specs=[pl.BlockSpec((ROW_BLOCK, S5_WIDTH), lambda i: (jnp.minimum(block_index(i), n_lat - 1), 0)),
                  pl.BlockSpec((ROW_BLOCK, S5_WIDTH), blk), pl.BlockSpec((ROW_BLOCK, 2 * NSTATE), blk),
                  _full_spec(_S5_BLOCKED), _full_spec(_S5_BLOCKED), _full_spec(_S5_TABLES),
                  _full_spec((ROW_BLOCK, ROW_BLOCK)), _full_spec((ROW_BLOCK, ROW_BLOCK))],
        out_specs=[pl.BlockSpec((ROW_BLOCK, S5_WIDTH), blk), _full_spec(_S5_DIAG), _full_spec(_S5_DIAG),
                   _full_spec((SUBLANES, 2 * NSTATE))],
        out_shape=[_sds((rows, S5_WIDTH)), _sds(_S5_DIAG), _sds(_S5_DIAG), _sds((SUBLANES, 2 * NSTATE))],
        scratch_shapes=[pltpu.VMEM((ROW_BLOCK, 2 * NSTATE), F32), pltpu.VMEM((ROW_BLOCK, S5_WIDTH), F32),
                        pltpu.VMEM(_S5_BLOCKED, F32), pltpu.VMEM(_S5_BLOCKED, F32),
                        pltpu.VMEM((SUBLANES, NSTATE), F32), pltpu.VMEM((SUBLANES, NSTATE), F32)],
        compiler_params=_params(("arbitrary",)),
    )(dy, z_all, states, bmat, cmat, adj, perm, perm_t)


def _glu_fwd(z_all, y0, y1, d_skip, w_glu, n_rows):
    def body(u_ref, y0_ref, y1_ref, d_ref, w_ref, o_ref):
        y = d_ref[...] * u_ref[...] + y0_ref[...] + y1_ref[...]
        g = _gelu(y)
        t = _dot(g.astype(BF16), w_ref[...], "nn")
        o_ref[...] = (g * _sigmoid(t)).astype(o_ref.dtype)

    row = pl.BlockSpec((ROW_BLOCK, S5_WIDTH), lambda i: (i, 0))
    return pl.pallas_call(
        body, name="glu_fwd", grid=(n_rows // ROW_BLOCK,),
        in_specs=[row, row, row, pl.BlockSpec((1, S5_WIDTH), lambda i: (0, 0)),
                  pl.BlockSpec((S5_WIDTH, S5_WIDTH), lambda i: (0, 0))],
        out_specs=row, out_shape=_sds((n_rows, S5_WIDTH + CONV_WIDTH), BF16), compiler_params=_params(("parallel",)),
    )(z_all, y0, y1, d_skip, w_glu)


def _glu_bwd(d_ycat, z_all, y0, y1, d_skip, w_glu, n_rows):
    def body(do_ref, u_ref, y0_ref, y1_ref, d_ref, w_ref, dy_ref, dw_ref, dd_ref):
        @pl.when(pl.program_id(0) == 0)
        def _():
            dw_ref[...] = jnp.zeros_like(dw_ref)
            dd_ref[...] = jnp.zeros_like(dd_ref)

        u = u_ref[...]
        y = d_ref[...] * u + y0_ref[...] + y1_ref[...]
        g = _gelu(y)
        gb = g.astype(BF16)
        w = w_ref[...]
        sg = _sigmoid(_dot(gb, w, "nn"))
        do = do_ref[...]
        dt = do * g * sg * (1.0 - sg)
        dtb = dt.astype(BF16)
        dg = do * sg + _dot(dtb, w, "nt")
        dy = dg * _dgelu(y)
        dy_ref[...] = dy
        dw_ref[...] += _dot(gb, dtb, "tn")
        dd_ref[...] += _fold8(dy * u)

    row = pl.BlockSpec((ROW_BLOCK, S5_WIDTH), lambda i: (i, 0))
    sq = pl.BlockSpec((S5_WIDTH, S5_WIDTH), lambda i: (0, 0))
    return pl.pallas_call(
        body, name="glu_bwd", grid=(n_rows // ROW_BLOCK,),
        in_specs=[row, row, row, row, pl.BlockSpec((1, S5_WIDTH), lambda i: (0, 0)), sq],
        out_specs=[row, sq, pl.BlockSpec((SUBLANES, S5_WIDTH), lambda i: (0, 0))],
        out_shape=[_sds((n_rows, S5_WIDTH)), _sds((S5_WIDTH, S5_WIDTH)), _sds((SUBLANES, S5_WIDTH))],
        compiler_params=_params(("arbitrary",)),
    )(d_ycat, z_all, y0, y1, d_skip, w_glu)


CONV_HALF = CONV_K // 2


def _conv_block(n_rows):
    blk = min(1024, n_rows)
    assert blk >= CONV_HALF * GRID_W and n_rows % blk == 0
    return blk


def _conv_gate(z_all, n_rows):
    blk = _conv_block(n_rows)
    nb = n_rows // blk

    def body(v_ref, g_ref, o_ref):
        i = pl.program_id(0)
        inside = jnp.logical_and(i >= 1, i <= nb)

        @pl.when(inside)
        def _():
            o_ref[...] = v_ref[...] * _sigmoid(g_ref[...])

        @pl.when(jnp.logical_not(inside))
        def _():
            o_ref[...] = jnp.zeros_like(o_ref)

    src = lambda col: pl.BlockSpec((blk, CONV_WIDTH), lambda i: (jnp.clip(i - 1, 0, nb - 1), col))
    return pl.pallas_call(
        body, name="conv_gate", grid=(nb + 2,), in_specs=[src(1), src(2)],
        out_specs=pl.BlockSpec((blk, CONV_WIDTH), lambda i: (i, 0)),
        out_shape=_sds(((nb + 2) * blk, CONV_WIDTH)), compiler_params=_params(("parallel",)),
    )(z_all, z_all)


def _load_window(pad_ref, win, sem, blk):
    start = pl.multiple_of(pl.program_id(0) * blk, blk)
    copy = pltpu.make_async_copy(pad_ref.at[pl.ds(start, 3 * blk), :], win, sem)
    copy.start()
    copy.wait()


def _conv_fwd(hh_pad, w, b, ln_g, ln_b, ycat, n_rows):
    blk = _conv_block(n_rows)

    def body(hh_ref, w_ref, b_ref, g_ref, lb_ref, ycat_ref, hc_ref, y_ref, win, sem):
        _load_window(hh_ref, win, sem, blk)

        def tile(t, _):
            r0 = pl.multiple_of(t * CONV_ROWS, CONV_ROWS)
            acc = jnp.zeros((CONV_ROWS, CONV_WIDTH), F32)
            for k in range(CONV_K):
                acc = acc + w_ref[k:k + 1, :] * win[pl.ds(r0 + blk + (k - CONV_HALF) * GRID_W, CONV_ROWS), :]
            hc = acc + b_ref[...]
            hc_ref[pl.ds(r0, CONV_ROWS), :] = hc
            mu = jnp.mean(hc, axis=-1, keepdims=True)
            xc = hc - mu
            ln = xc * lax.rsqrt(jnp.mean(xc * xc, axis=-1, keepdims=True) + EPS_LN) * g_ref[...] + lb_ref[...]
            y_ref[pl.ds(r0, CONV_ROWS), :] = _silu(ln).astype(y_ref.dtype)
            return 0

        lax.fori_loop(0, blk // CONV_ROWS, tile, 0)

    vec = pl.BlockSpec((1, CONV_WIDTH), lambda i: (0, 0))
    row = pl.BlockSpec((blk, CONV_WIDTH), lambda i: (i, 0))
    return pl.pallas_call(
        body, name="conv_fwd", grid=(n_rows // blk,),
        in_specs=[ANY, pl.BlockSpec((CONV_K, CONV_WIDTH), lambda i: (0, 0)), vec, vec, vec, ANY],
        out_specs=[row, pl.BlockSpec((blk, CONV_WIDTH), lambda i: (i, 1))],
        out_shape=[_sds((n_rows, CONV_WIDTH)), _sds(ycat.shape, ycat.dtype)], input_output_aliases={5: 1},
        scratch_shapes=[pltpu.VMEM((3 * blk, CONV_WIDTH), F32), pltpu.SemaphoreType.DMA],
        compiler_params=_params(("arbitrary",)),
    )(hh_pad, w, b, ln_g, ln_b, ycat)


def _conv_bwd_norm(d_ycat, hc, ln_g, ln_b, n_rows):
    blk = _conv_block(n_rows)
    nb = n_rows // blk

    def body(dy_ref, hc_ref, g_ref, lb_ref, o_ref, sums):
        i = pl.program_id(0)

        @pl.when(i == 0)
        def _():
            sums[...] = jnp.zeros_like(sums)

        inside = jnp.logical_and(i >= 1, i <= nb)

        @pl.when(inside)
        def _():
            hcv = hc_ref[...]
            mu = jnp.mean(hcv, axis=-1, keepdims=True)
            xc = hcv - mu
            rstd = lax.rsqrt(jnp.mean(xc * xc, axis=-1, keepdims=True) + EPS_LN)
            xh = xc * rstd
            g = g_ref[...]
            dln = dy_ref[...] * _dsilu(xh * g + lb_ref[...])
            dxh = dln * g
            dhc = rstd * (dxh - jnp.mean(dxh, axis=-1, keepdims=True) - xh * jnp.mean(dxh * xh, axis=-1, keepdims=True))
            o_ref[...] = dhc
            sums[0] += _fold8(dhc)
            sums[1] += _fold8(dln * xh)
            sums[2] += _fold8(dln)

        @pl.when(jnp.logical_not(inside))
        def _():
            o_ref[...] = jnp.zeros_like(o_ref)

    vec = pl.BlockSpec((1, CONV_WIDTH), lambda i: (0, 0))
    return pl.pallas_call(
        body, name="conv_bwd_norm", grid=(nb + 2,),
        in_specs=[pl.BlockSpec((blk, CONV_WIDTH), lambda i: (jnp.clip(i - 1, 0, nb - 1), 1)),
                  pl.BlockSpec((blk, CONV_WIDTH), lambda i: (jnp.clip(i - 1, 0, nb - 1), 0)), vec, vec],
        out_specs=[pl.BlockSpec((blk, CONV_WIDTH), lambda i: (i, 0)),
                   pl.BlockSpec((3, SUBLANES, CONV_WIDTH), lambda i: (0, 0, 0))],
        out_shape=[_sds(((nb + 2) * blk, CONV_WIDTH)), _sds((3, SUBLANES, CONV_WIDTH))],
        compiler_params=_params(("arbitrary",)),
    )(d_ycat, hc, ln_g, ln_b)


def _conv_bwd_taps(dhc_pad, hh_pad, z_all, w, n_rows):
    blk = _conv_block(n_rows)

    def body(dhc_ref, hh_ref, v_ref, g_ref, w_ref, dv_ref, dg_ref, dw_ref, dwin, hwin, sems):
        @pl.when(pl.program_id(0) == 0)
        def _():
            dw_ref[...] = jnp.zeros_like(dw_ref)

        _load_window(dhc_ref, dwin, sems.at[0], blk)
        _load_window(hh_ref, hwin, sems.at[1], blk)

        def tile(t, _):
            r0 = pl.multiple_of(t * CONV_ROWS, CONV_ROWS)
            dh = dwin[pl.ds(r0 + blk, CONV_ROWS), :]
            acc = jnp.zeros((CONV_ROWS, CONV_WIDTH), F32)
            for k in range(CONV_K):
                off = (k - CONV_HALF) * GRID_W
                acc = acc + w_ref[k:k + 1, :] * dwin[pl.ds(r0 + blk - off, CONV_ROWS), :]
                dw_ref[k] += _fold8(dh * hwin[pl.ds(r0 + blk + off, CONV_ROWS), :])
            rs = pl.ds(r0, CONV_ROWS)
            sg = _sigmoid(g_ref[rs, :])
            vv = v_ref[rs, :]
            dv_ref[rs, :] = acc * sg
            dg_ref[rs, :] = acc * vv * sg * (1.0 - sg)
            return 0

        lax.fori_loop(0, blk // CONV_ROWS, tile, 0)

    row = pl.BlockSpec((blk, CONV_WIDTH), lambda i: (i, 0))
    return pl.pallas_call(
        body, name="conv_bwd_taps", grid=(n_rows // blk,),
        in_specs=[ANY, ANY,
            pl.BlockSpec((blk, CONV_WIDTH), lambda i: (i, 1)), pl.BlockSpec((blk, CONV_WIDTH), lambda i: (i, 2)),
            pl.BlockSpec((CONV_K, CONV_WIDTH), lambda i: (0, 0))],
        out_specs=[row, row, pl.BlockSpec((CONV_K, SUBLANES, CONV_WIDTH), lambda i: (0, 0, 0))],
        out_shape=[_sds((n_rows, CONV_WIDTH)), _sds((n_rows, CONV_WIDTH)), _sds((CONV_K, SUBLANES, CONV_WIDTH))],
        scratch_shapes=[pltpu.VMEM((3 * blk, CONV_WIDTH), F32), pltpu.VMEM((3 * blk, CONV_WIDTH), F32),
                        pltpu.SemaphoreType.DMA((2,))],
        compiler_params=_params(("arbitrary",)),
    )(dhc_pad, hh_pad, z_all, z_all, w)


def _dz_assemble(du0, du1, dy, d_skip, dv, dgate, n_lat):
    rows = du0.shape[0]
    nb = rows // ROW_BLOCK

    w = S5_WIDTH

    def body(a_ref, b_ref, dy_ref, d_ref, dv_ref, dg_ref, o_ref):
        lat = pl.program_id(0) < n_lat

        @pl.when(lat)
        def _():
            o_ref[:, 0:w] = (a_ref[...] + b_ref[...] + dy_ref[...] * d_ref[...]).astype(o_ref.dtype)
            o_ref[:, w:2 * w] = dv_ref[...].astype(o_ref.dtype)
            o_ref[:, 2 * w:3 * w] = dg_ref[...].astype(o_ref.dtype)

        @pl.when(jnp.logical_not(lat))
        def _():
            o_ref[:, 0:w] = (a_ref[...] + b_ref[...]).astype(o_ref.dtype)
            o_ref[:, w:3 * w] = jnp.zeros((ROW_BLOCK, 2 * w), o_ref.dtype)

    all_rows = pl.BlockSpec((ROW_BLOCK, w), lambda i: (i, 0))
    lat_rows = pl.BlockSpec((ROW_BLOCK, w), lambda i: (jnp.minimum(i, n_lat - 1), 0))
    return pl.pallas_call(
        body, name="dz_assemble", grid=(nb,),
        in_specs=[all_rows, all_rows, lat_rows, pl.BlockSpec((1, w), lambda i: (0, 0)), lat_rows, lat_rows],
        out_specs=pl.BlockSpec((ROW_BLOCK, IN_COLS), lambda i: (i, 0)),
        out_shape=_sds((rows, IN_COLS), BF16), compiler_params=_params(("parallel",)),
    )(du0, du1, dy, d_skip, dv, dgate)


def _sum_parts(parts):
    _, r, c = parts.shape

    def body(p_ref, o_ref):
        acc = p_ref[0]
        for q in range(1, NDEV):
            acc = acc + p_ref[q]
        o_ref[...] = acc

    return pl.pallas_call(body, name="sum_parts", out_shape=_sds((r, c)), compiler_params=_params())(parts)


def _row_tile(r, c):
    best = r
    for t in (1024, 512, 256, 128, 64, 32, 16, 8):
        if r % t == 0 and t * c <= 128 * 1024:
            return t
    return best


def _adamw(name, w, gparts, m, v):
    r, c = w.shape
    np_ = gparts.shape[0]
    tr = _row_tile(r, c)

    def body(w_ref, g_ref, m_ref, v_ref, go_ref, d_ref, mo_ref, vo_ref):
        g = g_ref[0].astype(F32)
        for q in range(1, np_):
            g = g + g_ref[q].astype(F32)
        m2 = ADAM_B1 * m_ref[...] + (1.0 - ADAM_B1) * g
        v2 = ADAM_B2 * v_ref[...] + (1.0 - ADAM_B2) * jnp.square(g)
        m_hat = m2 / (1.0 - ADAM_B1 ** ADAM_STEP)
        v_hat = v2 / (1.0 - ADAM_B2 ** ADAM_STEP)
        go_ref[...] = g
        d_ref[...] = -ADAM_LR * (m_hat / (jnp.sqrt(v_hat) + ADAM_EPS) + ADAM_WD * w_ref[...])
        mo_ref[...] = m2
        vo_ref[...] = v2

    row = pl.BlockSpec((tr, c), lambda i: (i, 0))
    return pl.pallas_call(
        body, name=name, grid=(r // tr,),
        in_specs=[row, pl.BlockSpec((np_, tr, c), lambda i: (0, i, 0)), row, row],
        out_specs=[row] * 4, out_shape=[_sds((r, c))] * 4, compiler_params=_params(("parallel",)),
    )(w, gparts, m, v)


def _adamw_native(name, w, g, m, v):
    def body(w_ref, g_ref, m_ref, v_ref, d_ref, mo_ref, vo_ref):
        gv = g_ref[...]
        m2 = ADAM_B1 * m_ref[...] + (1.0 - ADAM_B1) * gv
        v2 = ADAM_B2 * v_ref[...] + (1.0 - ADAM_B2) * jnp.square(gv)
        m_hat = m2 / (1.0 - ADAM_B1 ** ADAM_STEP)
        v_hat = v2 / (1.0 - ADAM_B2 ** ADAM_STEP)
        d_ref[...] = -ADAM_LR * (m_hat / (jnp.sqrt(v_hat) + ADAM_EPS) + ADAM_WD * w_ref[...])
        mo_ref[...] = m2
        vo_ref[...] = v2

    return pl.pallas_call(body, name=name, out_shape=[_sds(w.shape)] * 3, compiler_params=_params())(w, g, m, v)


SMALL = ["c_ctx", "ada_b", "norm1_g", "s5_lam_re", "s5_lam_im", "s5_log_dt", "s5_d", "conv_b", "conv_ln_g", "conv_ln_b",
         "norm2_g", "final_g"]
SMALL_PACKED_ROWS = 24


def _pack_rows(parts, rows):
    flat = jnp.concatenate([p.reshape(-1).astype(F32) for p in parts])
    return jnp.pad(flat, (0, rows * D_MODEL - flat.shape[0])).reshape(rows, D_MODEL)


def _unpack_rows(packed, shapes):
    flat = packed.reshape(-1)
    out, off = [], 0
    for shape in shapes:
        size = 1
        for s in shape:
            size *= s
        out.append(flat[off:off + size].reshape(shape))
        off += size
    return out


def kernel(x, c, ctx, c_ctx, ada_w, ada_b, norm1_g, w_in, s5_lam_re, s5_lam_im, s5_log_dt, s5_b_re, s5_b_im, s5_c_re, s5_c_im, s5_d, s5_w_glu, conv_w, conv_b, conv_ln_g, conv_ln_b, w_out, norm2_g, mlp_w1, mlp_w2, final_g, loss_target, m_c_ctx, m_ada_w, m_ada_b, m_norm1_g, m_w_in, m_s5_lam_re, m_s5_lam_im, m_s5_log_dt, m_s5_b_re, m_s5_b_im, m_s5_c_re, m_s5_c_im, m_s5_d, m_s5_w_glu, m_conv_w, m_conv_b, m_conv_ln_g, m_conv_ln_b, m_w_out, m_norm2_g, m_mlp_w1, m_mlp_w2, m_final_g, v_c_ctx, v_ada_w, v_ada_b, v_norm1_g, v_w_in, v_s5_lam_re, v_s5_lam_im, v_s5_log_dt, v_s5_b_re, v_s5_b_im, v_s5_c_re, v_s5_c_im, v_s5_d, v_s5_w_glu, v_conv_w, v_conv_b, v_conv_ln_g, v_conv_ln_b, v_w_out, v_norm2_g, v_mlp_w1, v_mlp_w2, v_final_g):
    weights = dict(c_ctx=c_ctx, ada_w=ada_w, ada_b=ada_b, norm1_g=norm1_g, w_in=w_in, s5_lam_re=s5_lam_re, s5_lam_im=s5_lam_im, s5_log_dt=s5_log_dt, s5_b_re=s5_b_re, s5_b_im=s5_b_im, s5_c_re=s5_c_re, s5_c_im=s5_c_im, s5_d=s5_d, s5_w_glu=s5_w_glu, conv_w=conv_w, conv_b=conv_b, conv_ln_g=conv_ln_g, conv_ln_b=conv_ln_b, w_out=w_out, norm2_g=norm2_g, mlp_w1=mlp_w1, mlp_w2=mlp_w2, final_g=final_g)
    mom1 = dict(c_ctx=m_c_ctx, ada_w=m_ada_w, ada_b=m_ada_b, norm1_g=m_norm1_g, w_in=m_w_in, s5_lam_re=m_s5_lam_re, s5_lam_im=m_s5_lam_im, s5_log_dt=m_s5_log_dt, s5_b_re=m_s5_b_re, s5_b_im=m_s5_b_im, s5_c_re=m_s5_c_re, s5_c_im=m_s5_c_im, s5_d=m_s5_d, s5_w_glu=m_s5_w_glu, conv_w=m_conv_w, conv_b=m_conv_b, conv_ln_g=m_conv_ln_g, conv_ln_b=m_conv_ln_b, w_out=m_w_out, norm2_g=m_norm2_g, mlp_w1=m_mlp_w1, mlp_w2=m_mlp_w2, final_g=m_final_g)
    mom2 = dict(c_ctx=v_c_ctx, ada_w=v_ada_w, ada_b=v_ada_b, norm1_g=v_norm1_g, w_in=v_w_in, s5_lam_re=v_s5_lam_re, s5_lam_im=v_s5_lam_im, s5_log_dt=v_s5_log_dt, s5_b_re=v_s5_b_re, s5_b_im=v_s5_b_im, s5_c_re=v_s5_c_re, s5_c_im=v_s5_c_im, s5_d=v_s5_d, s5_w_glu=v_s5_w_glu, conv_w=v_conv_w, conv_b=v_conv_b, conv_ln_g=v_conv_ln_g, conv_ln_b=v_conv_ln_b, w_out=v_w_out, norm2_g=v_norm2_g, mlp_w1=v_mlp_w1, mlp_w2=v_mlp_w2, final_g=v_final_g)
    order = list(weights)

    me = 4 * lax.axis_index("x") + 2 * lax.axis_index("y") + lax.axis_index("c")
    xs, cs, tgt = x[0], ctx[0], loss_target[0]
    n_lat_rows, n_ctx_rows = xs.shape[0], cs.shape[0]
    n_rows = n_lat_rows + n_ctx_rows
    n_lat = n_lat_rows // ROW_BLOCK
    ada_cols = ada_w.shape[2]

    (c_all,), _ = _exchange("gather_c", [c], [True])
    c_all = c_all.reshape(NDEV, D_MODEL)

    cond_fwd = jnp.concatenate([c_all, c_ctx[None], jnp.zeros((7, D_MODEL), F32)])
    ada_b_loc = lax.dynamic_slice(ada_b, (0, me * ada_cols), (1, ada_cols))
    (mod_g,), mod_token = _exchange("gather_mod", [_ada_fwd(cond_fwd, ada_w[0], ada_b_loc)], [True])
    wi_send, wi_recv, wi_src, wi_land, wi_token = _exchange_start(
        "gather_w_in_start", [w_in[0].astype(BF16) + mod_token[0:1, 0:1].astype(BF16)], [True])
    mixer_w = [s5_w_glu[0].astype(BF16), conv_w[0] + wi_token[0:1, 0:1], w_out[0].astype(BF16)]
    mixer_send, mixer_recv, mixer_src, mixer_land, mixer_token = _exchange_start("gather_mixer_start", mixer_w, [True] * 3)
    mlp_w = [mlp_w1[0].astype(BF16), mlp_w2[0].astype(BF16) + mixer_token[0:1, 0:1].astype(BF16)]
    mlpw_send, mlpw_recv, mlpw_src, mlpw_land, mlpw_token = _exchange_start("gather_mlp_start", mlp_w, [True] * 2)
    mod_rows = jnp.transpose(mod_g, (1, 0, 2)).reshape(16, 6 * D_MODEL) + mlpw_token[0:1, 0:1]
    mod = lax.dynamic_slice(mod_rows, (me, 0), (1, 6 * D_MODEL)).reshape(6, D_MODEL)
    modc = mod_rows[8, :2 * D_MODEL].reshape(2, D_MODEL)
    sh1, sc1, g1, sh2, sc2, g2 = [mod[i:i + 1] for i in range(6)]

    a_all = _prenorm("prenorm1", xs, cs, norm1_g, jnp.stack([mod[0:2], modc]))
    wi_own, wi_landed = _exchange_wait("gather_w_in_wait", wi_send, wi_recv, wi_src, wi_land, [True], a_all)
    w_in_full = jnp.transpose(_with_own(wi_landed[0], wi_own[0], me), (1, 0, 2)).reshape(D_MODEL, IN_COLS)
    tm_all = 1088 if n_rows % 1088 == 0 else ROW_BLOCK
    (z_all,) = _matmul("in_proj", a_all, w_in_full, "nn", (n_rows, IN_COLS, D_MODEL), (tm_all, IN_COLS, D_MODEL),
                       [((n_rows, IN_COLS), F32)])

    lam_re, lam_im = s5_lam_re[0].reshape(2, 1, NSTATE), s5_lam_im[0].reshape(2, 1, NSTATE)
    ldt = jnp.repeat(s5_log_dt[0], S5_STATE, axis=-1).reshape(2, 1, NSTATE)
    bt_re = jnp.transpose(s5_b_re[0], (0, 3, 1, 2)).reshape(2, S5_GROUP, NSTATE)
    bt_im = jnp.transpose(s5_b_im[0], (0, 3, 1, 2)).reshape(2, S5_GROUP, NSTATE)
    groups_per_block = S5_GROUPS // S5_BLOCKS
    ct_re = jnp.tile(s5_c_re[0].reshape(2, S5_WIDTH, S5_STATE), (1, 1, groups_per_block))
    ct_im = jnp.tile(s5_c_im[0].reshape(2, S5_WIDTH, S5_STATE), (1, 1, groups_per_block))
    d_skip = s5_d[0].reshape(1, S5_WIDTH)
    perm = _segment_permutation()
    perm_t = perm.T
    disc, states, y_dir = [], [], []
    for d in range(2):
        disc.append(_s5_discretise(f"s5_disc{d}", d == 0, lam_re[d], lam_im[d], ldt[d], bt_re[d], bt_im[d], ct_re[d], ct_im[d]))
        _, tab, _, bmat, cmat = disc[d]
        s, yd = _s5_scan_fwd(f"s5_scan_fwd{d}", d == 0, z_all, bmat, cmat, tab, perm, perm_t)
        states.append(s)
        y_dir.append(yd)
    mixer_own, mixer_landed = _exchange_wait("gather_mixer_wait", mixer_send, mixer_recv, mixer_src, mixer_land,
                                             [True] * 3, y_dir[1])
    glu_g, conv_w_g, w_out_g = [_with_own(l, o, me) for l, o in zip(mixer_landed, mixer_own)]
    glu_full = glu_g.reshape(S5_WIDTH, S5_WIDTH)
    conv_w_full = jnp.transpose(conv_w_g, (1, 0, 2)).reshape(CONV_K, CONV_WIDTH)
    w_out_full = w_out_g.reshape(D_MODEL, D_MODEL)
    ycat = _glu_fwd(z_all, y_dir[0], y_dir[1], d_skip, glu_full, n_lat_rows)

    hh_pad = _conv_gate(z_all, n_lat_rows)
    hc, ycat = _conv_fwd(hh_pad, conv_w_full, conv_b, conv_ln_g, conv_ln_b, ycat, n_lat_rows)

    tm = min(1024, n_lat_rows)
    tm_e = min(512, n_lat_rows)
    w1_cols = D_FF // NDEV
    row_vec = lambda tn: pl.BlockSpec((1, tn), lambda i, j, k: (0, j))
    out_tile = lambda t_m, t_n: pl.BlockSpec((t_m, t_n), lambda i, j, k: (i, j))
    full_rows = ((n_lat_rows, D_MODEL), F32)
    sums = ((n_lat_rows // tm_e, SUBLANES, D_MODEL), F32)
    sums_spec = pl.BlockSpec((None, SUBLANES, D_MODEL), lambda i, j, k: (i, 0, 0))
    vec = lambda v: (v, row_vec(D_MODEL))
    mix, h1, a2 = _matmul("out_proj", ycat, w_out_full, "nn", (n_lat_rows, D_MODEL, D_MODEL), (tm_e, D_MODEL, D_MODEL),
                          [full_rows, full_rows, ((n_lat_rows, D_MODEL), BF16)], epi=_epi_residual_prenorm,
                          epi_extra=[(xs, out_tile(tm_e, D_MODEL)), vec(g1), vec(norm2_g), vec(sc2), vec(sh2)])
    mlpw_own, mlpw_landed = _exchange_wait("gather_mlp_wait", mlpw_send, mlpw_recv, mlpw_src, mlpw_land, [True] * 2, a2)
    w1_g, w2_g = [_with_own(l, o, me) for l, o in zip(mlpw_landed, mlpw_own)]
    w2_full = w2_g.reshape(D_FF, D_MODEL)
    tm_up = min(2048, n_lat_rows)
    (f,) = _matmul("mlp_up", a2, w1_g, "nn", (n_lat_rows, D_FF, D_MODEL), (tm_up, w1_cols, D_MODEL),
                   [((n_lat_rows, D_FF), BF16)], b_spec=pl.BlockSpec((None, D_MODEL, w1_cols), lambda i, j, k: (j, 0, 0)))
    sq_relu = lambda t: jnp.square(jnp.maximum(t, 0.0))
    mlp_out, d_h2, dm2, err_sums, d_final_g8 = _matmul(
        "mlp_down", f, w2_full, "nn", (n_lat_rows, D_MODEL, D_FF), (tm_e, D_MODEL, 1024),
        [full_rows, full_rows, ((n_lat_rows, D_MODEL), BF16), sums, sums], a_fn=sq_relu, epi=_epi_residual_loss,
        epi_extra=[(h1, out_tile(tm_e, D_MODEL)), vec(g2), (tgt, out_tile(tm_e, D_MODEL)), vec(final_g[None])],
        out_specs=[out_tile(tm_e, D_MODEL)] * 3 + [sums_spec] * 2)

    (d_f,) = _matmul("mlp_down_dx", dm2, w2_full, "nt", (n_lat_rows, D_FF, D_MODEL), (tm, 512, D_MODEL),
                     [((n_lat_rows, D_FF), BF16)],
                     epi=lambda acc, ft: (acc * 2.0 * jnp.maximum(ft.astype(F32), 0.0),), epi_extra=[(f, out_tile(tm, 512))])
    (g_w2,) = _matmul("mlp_down_dw", f, dm2, "tn", (D_FF, D_MODEL, n_lat_rows), (1024, D_MODEL, tm),
                      [((D_FF, D_MODEL), F32)], a_fn=sq_relu)
    (g_w1,) = _matmul("mlp_up_dw", a2, d_f, "tn", (D_MODEL, D_FF, n_lat_rows), (D_MODEL, w1_cols, tm),
                      [((NDEV, D_MODEL, w1_cols), F32)],
                      out_specs=[pl.BlockSpec((None, D_MODEL, w1_cols), lambda i, j, k: (j, 0, 0))])
    mlp_send, mlp_recv, mlp_src, mlp_land, mlp_token = _exchange_start(
        "scatter_mlp_start", [g_w1, g_w2.reshape(NDEV, D_FF // NDEV, D_MODEL)], [False] * 2)
    d_h1, dm1, *sums2 = _matmul(
        "mlp_up_dx", d_f, w1_g, "nt", (n_lat_rows, D_MODEL, D_FF), (tm_e, D_MODEL, w1_cols),
        [full_rows, ((n_lat_rows, D_MODEL), BF16)] + [sums] * 4, epi=_epi_norm_bwd,
        epi_extra=[(h1, out_tile(tm_e, D_MODEL)), (d_h2, out_tile(tm_e, D_MODEL)), (mlp_out, out_tile(tm_e, D_MODEL)),
                   vec(norm2_g), vec(sc2 + mlp_token[0:1, 0:1]), vec(g1)],
        b_spec=pl.BlockSpec((None, D_MODEL, w1_cols), lambda i, j, k: (k, 0, 0)),
        out_specs=[out_tile(tm_e, D_MODEL)] * 2 + [sums_spec] * 4)

    (d_ycat,) = _matmul("out_proj_dx", dm1, w_out_full, "nt", (n_lat_rows, D_MODEL, D_MODEL), (tm, D_MODEL, D_MODEL),
                        [((n_lat_rows, D_MODEL), F32)])
    (g_w_out,) = _matmul("out_proj_dw", ycat, dm1, "tn", (D_MODEL, D_MODEL, n_lat_rows), (D_MODEL, D_MODEL, 512),
                         [((D_MODEL, D_MODEL), F32)])

    dy, g_glu, dd8 = _glu_bwd(d_ycat, z_all, y_dir[0], y_dir[1], d_skip, glu_full, n_lat_rows)
    proj_send, proj_recv, proj_src, proj_land, proj_token = _exchange_start(
        "scatter_proj_start",
        [g_w_out.reshape(NDEV, D_MODEL // NDEV, D_MODEL), g_glu.reshape(NDEV, S5_WIDTH // NDEV, S5_WIDTH)], [False] * 2)
    perm = perm + proj_token[0:1, 0:1].astype(BF16)
    du, g_lam_re, g_lam_im, g_ldt, g_bt, g_cdiag = [], [], [], [], [], []
    for d in range(2):
        _, _, adj, bmat, cmat = disc[d]
        du_d, d_bdiag, d_cdiag, d_abar8 = _s5_scan_bwd(f"s5_scan_bwd{d}", d == 0, dy, z_all, states[d], bmat, cmat, adj,
                                                       perm, perm_t)
        du.append(du_d)
        d_bbar = jnp.transpose(d_bdiag.reshape(S5_BLOCKS, S5_GROUP, 2, NSTATE // S5_BLOCKS), (2, 1, 0, 3)).reshape(
            2 * S5_GROUP, NSTATE)
        d_lam8, d_bt = _s5_discretise_bwd(f"s5_disc_bwd{d}", lam_re[d], lam_im[d], ldt[d], bt_re[d], bt_im[d], d_abar8, d_bbar)
        g_lam_re.append(d_lam8[0].reshape(S5_GROUPS, S5_STATE))
        g_lam_im.append(d_lam8[1].reshape(S5_GROUPS, S5_STATE))
        g_ldt.append(d_lam8[2].reshape(S5_GROUPS, S5_STATE).sum(axis=-1))
        g_bt.append(d_bt)
        g_cdiag.append(d_cdiag)

    dhc_pad, conv_sums = _conv_bwd_norm(d_ycat, hc, conv_ln_g, conv_ln_b, n_lat_rows)
    d_v, d_gate, g_conv_w8 = _conv_bwd_taps(dhc_pad, hh_pad, z_all, conv_w_full, n_lat_rows)

    dz_all = _dz_assemble(du[0], du[1], dy, d_skip, d_v, d_gate, n_lat)
    (g_w_in_full,) = _matmul("in_proj_dw", a_all, dz_all, "tn", (D_MODEL, IN_COLS, n_rows), (D_MODEL, IN_COLS, tm_all),
                             [((D_MODEL, IN_COLS), F32)])
    g_w_in_parts = jnp.transpose(g_w_in_full.reshape(D_MODEL, NDEV, IN_COLS // NDEV), (1, 0, 2)).astype(BF16)
    win_send, win_recv, win_src, win_land, win_token = _exchange_start("scatter_w_in_start", [g_w_in_parts], [False])
    (d_a_all,) = _matmul("in_proj_dx", dz_all, w_in_full + win_token[0:1, 0:1].astype(BF16), "nt",
                         (n_rows, D_MODEL, IN_COLS), (tm_all, D_MODEL, IN_COLS), [((n_rows, D_MODEL), F32)])
    grad_x, sums1 = _norm_bwd("norm1_bwd", xs, d_a_all, 0, norm1_g, sc1, res=d_h1, aux=mix)
    (sums1c,) = _norm_bwd("norm1_bwd_ctx", cs, d_a_all, n_lat, norm1_g, modc[1:2])

    s1, s1c, s2 = sums1.sum(axis=1), sums1c.sum(axis=1), [p.sum(axis=(0, 1)) for p in sums2]
    d_mod = jnp.concatenate([s1[0], s1[1], s1[3], s2[0], s2[1], s2[3]])
    d_modc = jnp.concatenate([s1c[0], s1c[1], jnp.zeros((4 * D_MODEL,), F32)])
    (dmod_g,), _ = _exchange("gather_dmod", [jnp.stack([d_mod, d_modc])], [True])
    dmod16 = jnp.concatenate([dmod_g[:, 0], dmod_g[:, 1]])
    dmod16_loc = lax.dynamic_slice(dmod16, (0, me * ada_cols), (16, ada_cols))
    cond_bwd = jnp.concatenate([c_all, jnp.broadcast_to(c_ctx[None], (NDEV, D_MODEL))])
    g_ada_w, g_c_ctx8 = _ada_bwd(cond_bwd, dmod16_loc, ada_w[0], c_ctx[None])

    small_parts = dict(
        c_ctx=g_c_ctx8[0], ada_b=d_mod + d_modc, norm1_g=s1[2] + s1c[2],
        s5_lam_re=jnp.stack(g_lam_re), s5_lam_im=jnp.stack(g_lam_im), s5_log_dt=jnp.stack(g_ldt),
        s5_d=dd8.sum(axis=0), conv_b=conv_sums[0].sum(axis=0), conv_ln_g=conv_sums[1].sum(axis=0),
        conv_ln_b=conv_sums[2].sum(axis=0), norm2_g=s2[2], final_g=d_final_g8.sum(axis=(0, 1)))
    reduced_shapes = [(SMALL_PACKED_ROWS, D_MODEL), (2, 2 * S5_GROUP, NSTATE), (2,) + _S5_DIAG, (1,)]
    small_g = _pack_rows(
        [_pack_rows([small_parts[n] for n in SMALL], SMALL_PACKED_ROWS), jnp.stack(g_bt), jnp.stack(g_cdiag),
         (0.5 / D_MODEL * jnp.sum(err_sums)).reshape(1)], SMALL_ROWS).reshape(NDEV, SMALL_ROWS // NDEV, D_MODEL)
    g_conv_w_parts = jnp.transpose(g_conv_w8.sum(axis=1).reshape(CONV_K, NDEV, CONV_WIDTH // NDEV), (1, 0, 2))

    res = {}

    def own_chunk(src):
        return lax.dynamic_index_in_dim(src, me, 0, keepdims=False)

    def adamw_big(name, parts):
        outs = _adamw("adamw_" + name, weights[name][0], parts, mom1[name][0], mom2[name][0])
        res[name] = [o[None] for o in outs]
        return outs[0]

    sm_send, sm_recv, sm_src, sm_land, sm_token = _exchange_start("scatter_small_start", [g_conv_w_parts, small_g],
                                                                  [False] * 2)
    done = adamw_big("ada_w", g_ada_w[None] + sm_token[0:1, 0:1])
    mlp_src, mlp_landed = _exchange_wait("scatter_mlp_wait", mlp_send, mlp_recv, mlp_src, mlp_land, [False] * 2, done)
    p_w1, p_w2 = [_with_own(l, own_chunk(s), me) for l, s in zip(mlp_landed, mlp_src)]
    adamw_big("mlp_w1", p_w1)
    done = adamw_big("mlp_w2", p_w2)
    sm_src, sm_landed = _exchange_wait("scatter_small_wait", sm_send, sm_recv, sm_src, sm_land, [False] * 2, done)
    p_conv_w, p_small = [_with_own(l, own_chunk(s), me) for l, s in zip(sm_landed, sm_src)]
    ga_send, ga_recv, ga_src, ga_land, ga_token = _exchange_start("gather_small_start", [_sum_parts(p_small)], [True])
    proj_src, proj_landed = _exchange_wait("scatter_proj_wait", proj_send, proj_recv, proj_src, proj_land, [False] * 2,
                                           ga_token)
    p_w_out, p_glu = [_with_own(l, own_chunk(s), me) for l, s in zip(proj_landed, proj_src)]
    adamw_big("w_out", p_w_out)
    done = adamw_big("s5_w_glu", p_glu)
    win_src, win_landed = _exchange_wait("scatter_w_in_wait", win_send, win_recv, win_src, win_land, [False], done)
    adamw_big("w_in", _with_own(win_landed[0], own_chunk(win_src[0]), me))
    done = adamw_big("conv_w", p_conv_w)
    ga_own, ga_landed = _exchange_wait("gather_small_wait", ga_send, ga_recv, ga_src, ga_land, [True], done)
    small_all = _with_own(ga_landed[0], ga_own[0], me).reshape(1, SMALL_ROWS, D_MODEL)
    _, r_bt, r_cdiag, loss = _unpack_rows(small_all, reduced_shapes)
    loss = loss.reshape(())
    pack = lambda src: _pack_rows([src[n] for n in SMALL], SMALL_PACKED_ROWS)
    outs = _adamw("adamw_small", pack(weights), small_all, pack(mom1), pack(mom2))
    unpacked = [_unpack_rows(o, [weights[n].shape for n in SMALL]) for o in outs]
    for i, name in enumerate(SMALL):
        res[name] = [u[i] for u in unpacked]
    to_gph = lambda t: jnp.transpose(t.reshape(2, S5_GROUP, S5_GROUPS, S5_STATE), (0, 2, 3, 1))[None]
    r_c = jnp.transpose(r_cdiag.reshape(2, S5_BLOCKS, S5_GROUP, 2, groups_per_block, S5_STATE), (3, 0, 1, 4, 2, 5)).reshape(
        2, 1, 2, S5_GROUPS, S5_GROUP, S5_STATE)
    native = dict(s5_b_re=to_gph(r_bt[:, :S5_GROUP]), s5_b_im=to_gph(r_bt[:, S5_GROUP:]), s5_c_re=r_c[0], s5_c_im=-r_c[1])
    for name, grad in native.items():
        res[name] = [grad, *_adamw_native("adamw_" + name, weights[name], grad, mom1[name], mom2[name])]

    return (loss, grad_x[None], *[res[n][0] for n in order], *[res[n][1] for n in order],
            *[res[n][2] for n in order], *[res[n][3] for n in order])
```

```python
import functools

import jax
import jax.numpy as jnp
from jax import lax
from jax.experimental import pallas as pl
from jax.experimental.pallas import tpu as pltpu

F32 = jnp.float32
BF16 = jnp.bfloat16
MESH = pl.DeviceIdType.MESH
ANY = pl.BlockSpec(memory_space=pl.ANY)

NDEV = 8
D_MODEL = 1024
GRID_W = 64
S5_WIDTH = 512
S5_GROUP = 16
S5_GROUPS = 32
S5_STATE = 64
NSTATE = S5_GROUPS * S5_STATE
CONV_WIDTH = 512
CONV_K = 31
IN_COLS = S5_WIDTH + 2 * CONV_WIDTH
D_FF = 4 * D_MODEL
EPS_RMS = 1e-6
EPS_LN = 1e-5
ADAM_LR = 0.001
ADAM_B1 = 0.9
ADAM_B2 = 0.999
ADAM_EPS = 1e-08
ADAM_WD = 0.01
ADAM_STEP = 10

SUBLANES = 8
LANES = 128
ROW_BLOCK = 256
SCAN_LANES = 512
SCAN_UNROLL = 4
SEGMENTS = SUBLANES
STEPS = ROW_BLOCK // SEGMENTS
S5_BLOCKS = 4
S5_BLOCK_WIDTH = S5_WIDTH // S5_BLOCKS
CONV_ROWS = 64
VMEM_LIMIT = 48 * 1024 * 1024
SMALL_ROWS = 320


def _params(sem=None):
    kw = dict(vmem_limit_bytes=VMEM_LIMIT)
    if sem is not None:
        kw["dimension_semantics"] = sem
    return pltpu.CompilerParams(**kw)


def _sds(shape, dtype=F32):
    return jax.ShapeDtypeStruct(tuple(shape), dtype)


def _fold8(x):
    return x.reshape(x.shape[0] // SUBLANES, SUBLANES, x.shape[1]).sum(axis=0)


def _sigmoid(x):
    return 1.0 / (1.0 + jnp.exp(-x))


def _silu(x):
    return x * _sigmoid(x)


def _dsilu(x):
    s = _sigmoid(x)
    return s * (1.0 + x * (1.0 - s))


_GELU_C = 0.7978845608028654


def _gelu(x):
    return 0.5 * x * (1.0 + jnp.tanh(_GELU_C * (x + 0.044715 * x * x * x)))


def _dgelu(x):
    t = jnp.tanh(_GELU_C * (x + 0.044715 * x * x * x))
    return 0.5 * (1.0 + t) + 0.5 * x * (1.0 - t * t) * _GELU_C * (1.0 + 3.0 * 0.044715 * x * x)


def _rms(x):
    rstd = lax.rsqrt(jnp.mean(x * x, axis=-1, keepdims=True) + EPS_RMS)
    return x * rstd, rstd


def _epi_residual_prenorm(acc, res, gate, gain, scale, shift):
    h = res + gate * acc
    xh, _ = _rms(h)
    return acc, h, (xh * gain) * (1.0 + scale) + shift


def _epi_residual_loss(acc, res, gate, target, gain):
    h = res + gate * acc
    xh, rstd = _rms(h)
    err = xh * gain - target
    dy = err * (1.0 / h.shape[-1])
    dxh = dy * gain
    dh = rstd * (dxh - xh * jnp.mean(dxh * xh, axis=-1, keepdims=True))
    return acc, dh, dh * gate, _fold8(err * err), _fold8(dy * xh)


def _epi_norm_bwd(d_act, x, res, aux, gain, scale, gate):
    xh, rstd = _rms(x)
    dn = d_act * (1.0 + scale)
    dxh = dn * gain
    dx = res + rstd * (dxh - xh * jnp.mean(dxh * xh, axis=-1, keepdims=True))
    return dx, dx * gate, _fold8(d_act), _fold8(d_act * (xh * gain)), _fold8(dn * xh), _fold8(res * aux)


def _dot(a, b, mode):
    dims = {"nn": (((1,), (0,)), ((), ())), "nt": (((1,), (1,)), ((), ())), "tn": (((0,), (0,)), ((), ()))}[mode]
    return lax.dot_general(a, b, dims, preferred_element_type=F32)


def _peers(x, y, c):
    out = []
    for k in range(1, NDEV):
        px = 1 - x if k & 4 else x
        py = 1 - y if k & 2 else y
        pc = 1 - c if k & 1 else c
        out.append(((px, py, pc), 4 * px + 2 * py + pc))
    return out


def _exchange_copies(src, land, send_sems, recv_sems, gather):
    x, y, c = lax.axis_index("x"), lax.axis_index("y"), lax.axis_index("c")
    me = 4 * x + 2 * y + c
    out = []
    for a in range(len(src)):
        for k, (peer, plin) in enumerate(_peers(x, y, c)):
            chunk = src[a] if gather[a] else src[a].at[plin]
            sems = dict(send_sem=send_sems.at[a * (NDEV - 1) + k], recv_sem=recv_sems.at[a * (NDEV - 1) + k],
                        device_id=peer, device_id_type=MESH)
            out.append((pltpu.make_async_remote_copy(src_ref=chunk, dst_ref=land[a].at[me], **sems),
                        pltpu.make_async_remote_copy(src_ref=chunk, dst_ref=land[a].at[plin], **sems)))
    return out


def _exchange(name, srcs, gather):
    n = len(srcs)
    outs = [_sds(((NDEV,) + s.shape) if g else s.shape, s.dtype) for s, g in zip(srcs, gather)]

    def body(*refs):
        src, dst, token = refs[:n], refs[n:2 * n], refs[2 * n]
        send_sems, recv_sems, local_sems = refs[2 * n + 1:]
        me = 4 * lax.axis_index("x") + 2 * lax.axis_index("y") + lax.axis_index("c")
        local = [pltpu.make_async_copy(src[a] if gather[a] else src[a].at[me], dst[a].at[me], local_sems.at[a])
                 for a in range(n)]
        for copy in local:
            copy.start()
        copies = _exchange_copies(src, dst, send_sems, recv_sems, gather)
        for copy, _ in copies:
            copy.start()
        token[...] = jnp.zeros_like(token)
        for copy, landing in copies:
            copy.wait_send()
            landing.wait_recv()
        for copy in local:
            copy.wait()

    nsem = n * (NDEV - 1)
    out = pl.pallas_call(
        body, name=name, out_shape=outs + [_sds((SUBLANES, LANES))], in_specs=[ANY] * n,
        out_specs=[ANY] * n + [pl.BlockSpec(memory_space=pltpu.VMEM)],
        scratch_shapes=[pltpu.SemaphoreType.DMA((nsem,)), pltpu.SemaphoreType.DMA((nsem,)), pltpu.SemaphoreType.DMA((n,))],
    )(*srcs)
    return out[:n], out[n]


HBM = pl.BlockSpec(memory_space=pltpu.HBM)
SEM = pl.BlockSpec(memory_space=pltpu.SEMAPHORE)
EFFECT = pltpu.SideEffectType.DATAFLOW_SIDE_EFFECTING


def _exchange_start(name, srcs, gather):
    n = len(srcs)
    lands = [lax.empty(((NDEV,) + s.shape) if g else s.shape, s.dtype) for s, g in zip(srcs, gather)]

    def body(*refs):
        src, land = refs[:n], refs[n:2 * n]
        send_sems, recv_sems = refs[2 * n], refs[2 * n + 1]
        token = refs[-1]
        for copy, _ in _exchange_copies(src, land, send_sems, recv_sems, gather):
            copy.start()
        token[...] = jnp.zeros_like(token)

    hbm = lambda v: pltpu.HBM(v.shape, v.dtype)
    nsem = n * (NDEV - 1)
    out = pl.pallas_call(
        body, name=name,
        out_shape=(pltpu.SemaphoreType.DMA((nsem,)), pltpu.SemaphoreType.DMA((nsem,)), *[hbm(v) for v in srcs],
                   *[hbm(v) for v in lands], _sds((SUBLANES, LANES))),
        in_specs=[HBM] * (2 * n), out_specs=(SEM, SEM, *([HBM] * (2 * n)), pl.BlockSpec(memory_space=pltpu.VMEM)),
        input_output_aliases={i: 2 + i for i in range(2 * n)},
        compiler_params=pltpu.CompilerParams(has_side_effects=EFFECT),
    )(*[pltpu.with_memory_space_constraint(v, pltpu.HBM) for v in list(srcs) + lands])
    return out[0], out[1], out[2:2 + n], out[2 + n:2 + 2 * n], out[-1]


def _exchange_wait(name, send_sems, recv_sems, srcs, lands, gather, after):
    n = len(srcs)

    def body(*refs):
        src, land = refs[:n], refs[n:2 * n]
        send_ref, recv_ref = refs[2 * n], refs[2 * n + 1]
        for copy, landing in _exchange_copies(src, land, send_ref, recv_ref, gather):
            copy.wait_send()
            landing.wait_recv()

    hbm = lambda v: pltpu.HBM(v.shape, v.dtype)
    out = pl.pallas_call(
        body, name=name, out_shape=[hbm(v) for v in list(srcs) + list(lands)],
        in_specs=[HBM] * (2 * n) + [SEM, SEM, ANY], out_specs=[HBM] * (2 * n),
        input_output_aliases={i: i for i in range(2 * n)},
        compiler_params=pltpu.CompilerParams(has_side_effects=EFFECT),
    )(*srcs, *lands, send_sems, recv_sems, after)
    return out[:n], out[n:]


def _with_own(landed, own, me):
    return lax.dynamic_update_slice(landed, own[None], (me,) + (0,) * own.ndim)


def _matmul(name, a, b, mode, mnk, tiles, outs, a_spec=None, b_spec=None, a_fn=None, a_extra=(),
            epi=None, epi_extra=(), out_specs=None):
    m_, n_, k_ = mnk
    tm, tn, tk = tiles
    nk = k_ // tk
    if a_spec is None:
        a_spec = (pl.BlockSpec((tk, tm), lambda i, j, k: (k, i)) if mode == "tn"
                  else pl.BlockSpec((tm, tk), lambda i, j, k: (i, k)))
    if b_spec is None:
        b_spec = (pl.BlockSpec((tn, tk), lambda i, j, k: (j, k)) if mode == "nt"
                  else pl.BlockSpec((tk, tn), lambda i, j, k: (k, j)))
    if out_specs is None:
        out_specs = [pl.BlockSpec((tm, tn), lambda i, j, k: (i, j)) for _ in outs]
    na, ne, no = len(a_extra), len(epi_extra), len(outs)

    def body(*refs):
        a_ref, b_ref = refs[0], refs[1]
        ax = refs[2:2 + na]
        ex = refs[2 + na:2 + na + ne]
        o = refs[2 + na + ne:2 + na + ne + no]

        def finish(res):
            res = epi(res, *[r[...] for r in ex]) if epi is not None else (res,)
            for ref, val in zip(o, res):
                ref[...] = val.astype(ref.dtype)

        at = a_ref[...]
        if a_fn is not None:
            at = a_fn(at, *[r[...] for r in ax])
        part = _dot(at.astype(BF16), b_ref[...].astype(BF16), mode)
        if nk == 1:
            finish(part)
            return
        acc = refs[-1]
        k = pl.program_id(2)

        @pl.when(k == 0)
        def _():
            acc[...] = part

        @pl.when(k > 0)
        def _():
            acc[...] += part

        @pl.when(k == nk - 1)
        def _():
            finish(acc[...])

    return pl.pallas_call(
        body, name=name, grid=(m_ // tm, n_ // tn, nk),
        in_specs=[a_spec, b_spec] + [s for _, s in a_extra] + [s for _, s in epi_extra],
        out_specs=out_specs, out_shape=[_sds(s, d) for s, d in outs],
        scratch_shapes=[pltpu.VMEM((tm, tn), F32)] if nk > 1 else [],
        compiler_params=_params(("parallel", "parallel", "arbitrary")),
    )(a, b, *[x for x, _ in a_extra], *[x for x, _ in epi_extra])


def _prenorm(name, x, ctx, gain, shsc):
    n_lat = x.shape[0] // ROW_BLOCK
    n_ctx = 0 if ctx is None else ctx.shape[0] // ROW_BLOCK
    d = x.shape[1]

    def norm(src, g_ref, m_ref, o_ref):
        xv = src[...]
        xh = xv * lax.rsqrt(jnp.mean(xv * xv, axis=-1, keepdims=True) + EPS_RMS)
        o_ref[...] = ((xh * g_ref[...]) * (1.0 + m_ref[1:2, :]) + m_ref[0:1, :]).astype(o_ref.dtype)

    def body(*refs):
        if ctx is None:
            x_ref, g_ref, m_ref, o_ref = refs
            norm(x_ref, g_ref, m_ref, o_ref)
        else:
            x_ref, c_ref, g_ref, m_ref, o_ref = refs
            i = pl.program_id(0)

            @pl.when(i < n_lat)
            def _():
                norm(x_ref, g_ref, m_ref, o_ref)

            @pl.when(i >= n_lat)
            def _():
                norm(c_ref, g_ref, m_ref, o_ref)

    in_specs = [pl.BlockSpec((ROW_BLOCK, d), lambda i: (jnp.minimum(i, n_lat - 1), 0))]
    args = [x]
    if ctx is not None:
        in_specs.append(pl.BlockSpec((ROW_BLOCK, d), lambda i: (jnp.maximum(i - n_lat, 0), 0)))
        args.append(ctx)
    in_specs += [pl.BlockSpec((1, d), lambda i: (0, 0)),
                 pl.BlockSpec((None, 2, d), lambda i: (jnp.minimum(i // n_lat, 1), 0, 0))]
    args += [gain, shsc]
    return pl.pallas_call(
        body, name=name, grid=(n_lat + n_ctx,), in_specs=in_specs,
        out_specs=pl.BlockSpec((ROW_BLOCK, d), lambda i: (i, 0)),
        out_shape=_sds(((n_lat + n_ctx) * ROW_BLOCK, d), BF16),
        compiler_params=_params(("parallel",)),
    )(*args)


def _norm_bwd(name, x, d_act, d_act_row0, gain, scale, res=None, aux=None, gate=None):
    rows, d = x.shape
    nb = rows // ROW_BLOCK
    has_res = res is not None
    has_gate = gate is not None

    def body(*refs):
        if has_gate:
            x_ref, da_ref, g_ref, sc_ref, r_ref, aux_ref, gate_ref, dx_ref, dm_ref, sums = refs
        elif has_res:
            x_ref, da_ref, g_ref, sc_ref, r_ref, aux_ref, dx_ref, sums = refs
        else:
            x_ref, da_ref, g_ref, sc_ref, sums = refs
        i = pl.program_id(0)

        @pl.when(i == 0)
        def _():
            sums[...] = jnp.zeros_like(sums)

        xv, da = x_ref[...], da_ref[...]
        rstd = lax.rsqrt(jnp.mean(xv * xv, axis=-1, keepdims=True) + EPS_RMS)
        xh = xv * rstd
        g = g_ref[...]
        dn = da * (1.0 + sc_ref[...])
        sums[0] += _fold8(da)
        sums[1] += _fold8(da * (xh * g))
        sums[2] += _fold8(dn * xh)
        if has_res:
            dxh = dn * g
            dx = rstd * (dxh - xh * jnp.mean(dxh * xh, axis=-1, keepdims=True))
            rv = r_ref[...]
            dx_ref[...] = rv + dx
            sums[3] += _fold8(rv * aux_ref[...])
            if has_gate:
                dm_ref[...] = ((rv + dx) * gate_ref[...]).astype(dm_ref.dtype)

    row = lambda i: (i, 0)
    vec = pl.BlockSpec((1, d), lambda i: (0, 0))
    in_specs = [pl.BlockSpec((ROW_BLOCK, d), row), pl.BlockSpec((ROW_BLOCK, d), lambda i: (i + d_act_row0, 0)), vec, vec]
    args = [x, d_act, gain, scale]
    out_shape = [_sds((4, SUBLANES, d))]
    out_specs = [pl.BlockSpec((4, SUBLANES, d), lambda i: (0, 0, 0))]
    if has_res:
        in_specs += [pl.BlockSpec((ROW_BLOCK, d), row), pl.BlockSpec((ROW_BLOCK, d), row)]
        args += [res, aux]
        if has_gate:
            in_specs.append(vec)
            args.append(gate)
            out_shape = [_sds((rows, d), BF16)] + out_shape
            out_specs = [pl.BlockSpec((ROW_BLOCK, d), row)] + out_specs
        out_shape = [_sds((rows, d))] + out_shape
        out_specs = [pl.BlockSpec((ROW_BLOCK, d), row)] + out_specs
    return pl.pallas_call(
        body, name=name, grid=(nb,), in_specs=in_specs, out_specs=out_specs, out_shape=out_shape,
        compiler_params=_params(("arbitrary",)),
    )(*args)


def _loss_head(h2, target, gain, gate):
    rows, d = h2.shape

    def body(h_ref, t_ref, g_ref, gate_ref, dh_ref, dm_ref, err_ref, dg_ref):
        i = pl.program_id(0)

        @pl.when(i == 0)
        def _():
            err_ref[...] = jnp.zeros_like(err_ref)
            dg_ref[...] = jnp.zeros_like(dg_ref)

        hv = h_ref[...]
        rstd = lax.rsqrt(jnp.mean(hv * hv, axis=-1, keepdims=True) + EPS_RMS)
        xh = hv * rstd
        g = g_ref[...]
        err = xh * g - t_ref[...]
        err_ref[...] += _fold8(err * err)
        dy = err * (1.0 / d)
        dg_ref[...] += _fold8(dy * xh)
        dxh = dy * g
        dh = rstd * (dxh - xh * jnp.mean(dxh * xh, axis=-1, keepdims=True))
        dh_ref[...] = dh
        dm_ref[...] = (dh * gate_ref[...]).astype(dm_ref.dtype)

    row = pl.BlockSpec((ROW_BLOCK, d), lambda i: (i, 0))
    acc = pl.BlockSpec((SUBLANES, d), lambda i: (0, 0))
    vec = pl.BlockSpec((1, d), lambda i: (0, 0))
    return pl.pallas_call(
        body, name="loss_head", grid=(rows // ROW_BLOCK,),
        in_specs=[row, row, vec, vec], out_specs=[row, row, acc, acc],
        out_shape=[_sds((rows, d)), _sds((rows, d), BF16), _sds((SUBLANES, d)), _sds((SUBLANES, d))],
        compiler_params=_params(("arbitrary",)),
    )(h2, target, gain, gate)


def _ada_fwd(cond16, ada_w_loc, ada_b_loc):
    cols = ada_w_loc.shape[1]

    def body(c_ref, w_ref, b_ref, o_ref):
        s = _silu(c_ref[...]).astype(BF16)
        o_ref[...] = _dot(s, w_ref[...].astype(BF16), "nn") + b_ref[...]

    return pl.pallas_call(body, name="ada_fwd", out_shape=_sds((16, cols)), compiler_params=_params())(
        cond16, ada_w_loc, ada_b_loc)


def _ada_bwd(cond16, dmod16, ada_w_loc, c_ctx_row):
    k_, cols = ada_w_loc.shape

    def body(c_ref, dm_ref, w_ref, cc_ref, gw_ref, gc_ref):
        s = _silu(c_ref[...]).astype(BF16)
        dm = dm_ref[...]
        gw_ref[...] = _dot(s, dm.astype(BF16), "tn")
        dmc = jnp.sum(dm[8:16, :], axis=0, keepdims=True)
        dmc8 = jnp.broadcast_to(dmc, (SUBLANES, cols)).astype(BF16)
        ds = _dot(dmc8, w_ref[...].astype(BF16), "nt")
        row = lax.broadcasted_iota(jnp.int32, ds.shape, 0)
        gc_ref[...] = jnp.where(row == 0, ds * _dsilu(cc_ref[...]), 0.0)

    return pl.pallas_call(body, name="ada_bwd", out_shape=[_sds((k_, cols)), _sds((SUBLANES, k_))],
                          compiler_params=_params())(cond16, dmod16, ada_w_loc, c_ctx_row)


def _cmul(a, b):
    return a[0] * b[0] - a[1] * b[1], a[0] * b[1] + a[1] * b[0]


def _disc(lam_re, lam_im, ldt):
    dt = jnp.exp(ldt)
    mag = jnp.exp(lam_re * dt)
    th = lam_im * dt
    a_re, a_im = mag * jnp.cos(th), mag * jnp.sin(th)
    den = lam_re * lam_re + lam_im * lam_im
    n_re = a_re - 1.0
    f_re = (n_re * lam_re + a_im * lam_im) / den
    f_im = (a_im * lam_re - n_re * lam_im) / den
    return dt, mag, th, a_re, a_im, den, n_re, f_re, f_im


def _block_diag_mask(shape):
    row = lax.broadcasted_iota(jnp.int32, shape, 0)
    col = lax.broadcasted_iota(jnp.int32, shape, 1)
    return lax.shift_right_logical(row, 4) == lax.shift_right_logical(col, 6)


TAB_A = 0
TAB_BIG = 1
TAB_SEG = 4
TAB_PW = 5
TAB_ROWS = TAB_PW + STEPS


def _s5_discretise(name, ascending, lam_re, lam_im, ldt, bt_re, bt_im, ct_re, ct_im):
    def write_tables(ref, pw, big, asc, sign):
        row = lax.broadcasted_iota(jnp.int32, (SUBLANES, NSTATE), 0)
        full = lambda v: jnp.broadcast_to(v, (SUBLANES, NSTATE))

        def put(t, p):
            ref[0, t] = full(p[0])
            ref[1, t] = full(sign * p[1])

        put(TAB_A, pw[0])
        for t in range(3):
            put(TAB_BIG + t, big[t])
        seg = [big[0]]
        for _ in range(SEGMENTS - 1):
            seg.append(_cmul(seg[-1], big[0]))
        seg_re = jnp.zeros((SUBLANES, NSTATE), F32)
        seg_im = jnp.zeros((SUBLANES, NSTATE), F32)
        for r in range(SEGMENTS):
            p = seg[r] if asc else seg[SEGMENTS - 1 - r]
            seg_re = jnp.where(row == r, p[0], seg_re)
            seg_im = jnp.where(row == r, sign * p[1], seg_im)
        ref[0, TAB_SEG] = seg_re
        ref[1, TAB_SEG] = seg_im
        for k in range(STEPS):
            put(TAB_PW + k, pw[k])

    def body(lr_ref, li_ref, ldt_ref, br_ref, bi_ref, cr_ref, ci_ref, bb_ref, tab_ref, adj_ref, bm_ref, cm_ref):
        _, _, _, a_re, a_im, _, _, f_re, f_im = _disc(lr_ref[...], li_ref[...], ldt_ref[...])
        bre, bim = br_ref[...], bi_ref[...]
        bb_re = f_re * bre - f_im * bim
        bb_im = f_re * bim + f_im * bre
        bb_ref[0:S5_GROUP, :] = bb_re
        bb_ref[S5_GROUP:2 * S5_GROUP, :] = bb_im
        pw = [(a_re, a_im)]
        for _ in range(STEPS - 1):
            pw.append(_cmul(pw[-1], (a_re, a_im)))
        big = [pw[STEPS - 1]]
        for _ in range(2):
            big.append(_cmul(big[-1], big[-1]))
        write_tables(tab_ref, pw, big, ascending, 1.0)
        write_tables(adj_ref, pw, big, not ascending, -1.0)
        half = NSTATE // S5_BLOCKS
        mask = _block_diag_mask((S5_BLOCK_WIDTH, half))
        tile = lambda v: jnp.broadcast_to(v[None], (S5_BLOCK_WIDTH // S5_GROUP, S5_GROUP, half)).reshape(S5_BLOCK_WIDTH, half)
        for c in range(S5_BLOCKS):
            cols = slice(c * half, (c + 1) * half)
            rows = slice(c * S5_BLOCK_WIDTH, (c + 1) * S5_BLOCK_WIDTH)
            bm_ref[c, :, 0:half] = jnp.where(mask, tile(bb_re[:, cols]), 0.0).astype(BF16)
            bm_ref[c, :, half:2 * half] = jnp.where(mask, tile(bb_im[:, cols]), 0.0).astype(BF16)
            cm_ref[c, :, 0:half] = jnp.where(mask, cr_ref[rows, :], 0.0).astype(BF16)
            cm_ref[c, :, half:2 * half] = jnp.where(mask, -ci_ref[rows, :], 0.0).astype(BF16)

    blocked = _sds((S5_BLOCKS, S5_BLOCK_WIDTH, 2 * NSTATE // S5_BLOCKS), BF16)
    return pl.pallas_call(
        body, name=name,
        out_shape=[_sds((2 * S5_GROUP, NSTATE)), _sds((2, TAB_ROWS, SUBLANES, NSTATE)),
                   _sds((2, TAB_ROWS, SUBLANES, NSTATE)), blocked, blocked],
        compiler_params=_params(),
    )(lam_re, lam_im, ldt, bt_re, bt_im, ct_re, ct_im)


def _s5_discretise_bwd(name, lam_re, lam_im, ldt, bt_re, bt_im, d_abar8, d_bbar):
    def body(lr_ref, li_ref, ldt_ref, br_ref, bi_ref, da_ref, db_ref, dl_ref, dbt_ref):
        lam_re, lam_im = lr_ref[...], li_ref[...]
        dt, mag, _, a_re, a_im, den, n_re, f_re, f_im = _disc(lam_re, lam_im, ldt_ref[...])
        bre, bim = br_ref[...], bi_ref[...]
        dbr, dbi = db_ref[0:S5_GROUP, :], db_ref[S5_GROUP:2 * S5_GROUP, :]
        dbt_ref[0:S5_GROUP, :] = f_re * dbr + f_im * dbi
        dbt_ref[S5_GROUP:2 * S5_GROUP, :] = f_re * dbi - f_im * dbr
        df_re = jnp.sum(bre * dbr + bim * dbi, axis=0, keepdims=True)
        df_im = jnp.sum(bre * dbi - bim * dbr, axis=0, keepdims=True)
        da = da_ref[...]
        da_re = jnp.sum(da[:, 0:NSTATE], axis=0, keepdims=True)
        da_im = jnp.sum(da[:, NSTATE:2 * NSTATE], axis=0, keepdims=True)
        da_re = da_re + (df_re * lam_re - df_im * lam_im) / den
        da_im = da_im + (df_re * lam_im + df_im * lam_re) / den
        ff = (f_re * df_re + f_im * df_im) * 2.0 / den
        d_lr = (df_re * n_re + df_im * a_im) / den - ff * lam_re
        d_li = (df_re * a_im - df_im * n_re) / den - ff * lam_im
        d_mag = (da_re * a_re + da_im * a_im) / mag
        d_th = da_im * a_re - da_re * a_im
        d_lr = d_lr + d_mag * mag * dt
        d_li = d_li + d_th * dt
        d_ldt = (d_mag * mag * lam_re + d_th * lam_im) * dt
        row = lax.broadcasted_iota(jnp.int32, (SUBLANES, NSTATE), 0)
        dl_ref[...] = jnp.where(row == 0, d_lr, jnp.where(row == 1, d_li, jnp.where(row == 2, d_ldt, 0.0)))

    return pl.pallas_call(
        body, name=name, out_shape=[_sds((SUBLANES, NSTATE)), _sds((2 * S5_GROUP, NSTATE))],
        compiler_params=_params(),
    )(lam_re, lam_im, ldt, bt_re, bt_im, d_abar8, d_bbar)


def _segment_permutation():
    rho = jnp.arange(ROW_BLOCK)
    src = STEPS * (rho % SEGMENTS) + rho // SEGMENTS
    return (src[:, None] == jnp.arange(ROW_BLOCK)[None, :]).astype(BF16)


def _permute_rows(perm_ref, v):
    return _dot(perm_ref[...], v, "nn").astype(BF16)


def _unpermute_rows(perm_t_ref, v):
    hi = v.astype(BF16)
    lo = (v - hi.astype(F32)).astype(BF16)
    return _dot(perm_t_ref[...], hi, "nn") + _dot(perm_t_ref[...], lo, "nn")


def _unrolled_loop(step, init):
    def trip(o, state):
        for u in range(SCAN_UNROLL):
            state = step(o * SCAN_UNROLL + u, state)
        return state

    return lax.fori_loop(0, STEPS // SCAN_UNROLL, trip, init)


def _scan_chunk(x_ref, out_ref, tab_ref, carry_re, carry_im, ascending, pair_ref=None, acc_ref=None):
    w = SCAN_LANES
    half = NSTATE // S5_BLOCKS
    row = lax.broadcasted_iota(jnp.int32, (SUBLANES, w), 0)
    last = (SEGMENTS - 1) if ascending else 0

    def from_previous_segment(v, k, fill):
        if ascending:
            return jnp.where(row >= k, pltpu.roll(v, k, 0), fill)
        return jnp.where(row < SEGMENTS - k, pltpu.roll(v, SEGMENTS - k, 0), fill)

    def tile_rows(k):
        return pl.ds(pl.multiple_of((k if ascending else STEPS - 1 - k) * SUBLANES, SUBLANES), SUBLANES)

    for j in range(NSTATE // w):
        n_l = pl.ds(j * w, w)
        lane0 = (j * w // half) * 2 * half + (j * w) % half
        re_l, im_l = pl.ds(lane0, w), pl.ds(lane0 + half, w)
        tab = lambda t, n_l=n_l: (tab_ref[0, t, :, n_l], tab_ref[1, t, :, n_l])
        a_re, a_im = tab(TAB_A)

        def local_step(k, h):
            rs = tile_rows(k)
            h_re = a_re * h[0] - a_im * h[1] + x_ref[rs, re_l]
            h_im = a_re * h[1] + a_im * h[0] + x_ref[rs, im_l]
            out_ref[rs, re_l] = h_re
            out_ref[rs, im_l] = h_im
            return h_re, h_im

        zero = jnp.zeros((SUBLANES, w), F32)
        end_re, end_im = lax.fori_loop(0, STEPS, local_step, (zero, zero))
        for t, k in ((TAB_BIG, 1), (TAB_BIG + 1, 2), (TAB_BIG + 2, 4)):
            p_re, p_im = tab(t)
            s_re, s_im = from_previous_segment(end_re, k, 0.0), from_previous_segment(end_im, k, 0.0)
            end_re, end_im = end_re + (p_re * s_re - p_im * s_im), end_im + (p_re * s_im + p_im * s_re)
        c0_re, c0_im = carry_re[:, n_l], carry_im[:, n_l]
        p_re, p_im = tab(TAB_SEG)
        end_re = end_re + (p_re * c0_re - p_im * c0_im)
        end_im = end_im + (p_re * c0_im + p_im * c0_re)
        carry_re[:, n_l] = jnp.broadcast_to(end_re[last:last + 1, :], end_re.shape)
        carry_im[:, n_l] = jnp.broadcast_to(end_im[last:last + 1, :], end_im.shape)
        in_re = from_previous_segment(end_re, 1, c0_re)
        in_im = from_previous_segment(end_im, 1, c0_im)

        def carry_step(k, st):
            rs = tile_rows(k)
            p_re, p_im = tab_ref[0, TAB_PW + k, :, n_l], tab_ref[1, TAB_PW + k, :, n_l]
            o_re = out_ref[rs, re_l] + (p_re * in_re - p_im * in_im)
            o_im = out_ref[rs, im_l] + (p_re * in_im + p_im * in_re)
            out_ref[rs, re_l] = o_re
            out_ref[rs, im_l] = o_im
            if pair_ref is None:
                return st
            s_re, s_im = pair_ref[rs, re_l], pair_ref[rs, im_l]
            return (o_re, o_im, st[2] + (st[0] * s_re + st[1] * s_im), st[3] + (st[1] * s_re - st[0] * s_im))

        if pair_ref is None:
            _unrolled_loop(carry_step, 0)
        else:
            fin = _unrolled_loop(carry_step, (in_re, in_im, zero, zero))
            acc_ref[:, n_l] += fin[2]
            acc_ref[:, pl.ds(NSTATE + j * w, w)] += fin[3]


def _scan_block_index(i, n_lat, ctx_first_then_ascending):
    if ctx_first_then_ascending:
        return jnp.where(i == 0, n_lat, i - 1)
    return jnp.where(i == 0, n_lat, n_lat - i)


def _full_spec(shape):
    return pl.BlockSpec(shape, lambda i: (0,) * len(shape))


_S5_BLOCKED = (S5_BLOCKS, S5_BLOCK_WIDTH, 2 * NSTATE // S5_BLOCKS)
_S5_TABLES = (2, TAB_ROWS, SUBLANES, NSTATE)
_S5_DIAG = (S5_BLOCKS, S5_GROUP, 2 * NSTATE // S5_BLOCKS)


def _s5_scan_fwd(name, ascending, z_all, bmat, cmat, tab, perm, perm_t):
    rows = z_all.shape[0]
    nb = rows // ROW_BLOCK
    n_lat = nb - 1
    bw, sw = S5_BLOCK_WIDTH, 2 * NSTATE // S5_BLOCKS

    def body(u_ref, bm_ref, cm_ref, tab_ref, p_ref, pt_ref, s_ref, y_ref, bu, yp, carry_re, carry_im):
        @pl.when(pl.program_id(0) == 0)
        def _():
            carry_re[...] = jnp.zeros_like(carry_re)
            carry_im[...] = jnp.zeros_like(carry_im)

        up = _permute_rows(p_ref, u_ref[...].astype(BF16))
        for c in range(S5_BLOCKS):
            bu[:, c * sw:(c + 1) * sw] = _dot(up[:, c * bw:(c + 1) * bw], bm_ref[c], "nn")
        _scan_chunk(bu, s_ref, tab_ref, carry_re, carry_im, ascending)
        for c in range(S5_BLOCKS):
            yp[:, c * bw:(c + 1) * bw] = _dot(s_ref[:, c * sw:(c + 1) * sw].astype(BF16), cm_ref[c], "nt")
        y_ref[...] = _unpermute_rows(pt_ref, yp[...])

    blk = lambda i: (_scan_block_index(i, n_lat, ascending), 0)
    return pl.pallas_call(
        body, name=name, grid=(nb,),
        in_specs=[pl.BlockSpec((ROW_BLOCK, S5_WIDTH), blk), _full_spec(_S5_BLOCKED), _full_spec(_S5_BLOCKED),
                  _full_spec(_S5_TABLES), _full_spec((ROW_BLOCK, ROW_BLOCK)), _full_spec((ROW_BLOCK, ROW_BLOCK))],
        out_specs=[pl.BlockSpec((ROW_BLOCK, 2 * NSTATE), blk), pl.BlockSpec((ROW_BLOCK, S5_WIDTH), blk)],
        out_shape=[_sds((rows, 2 * NSTATE)), _sds((rows, S5_WIDTH))],
        scratch_shapes=[pltpu.VMEM((ROW_BLOCK, 2 * NSTATE), F32), pltpu.VMEM((ROW_BLOCK, S5_WIDTH), F32),
                        pltpu.VMEM((SUBLANES, NSTATE), F32), pltpu.VMEM((SUBLANES, NSTATE), F32)],
        compiler_params=_params(("arbitrary",)),
    )(z_all, bmat, cmat, tab, perm, perm_t)


def _s5_scan_bwd(name, ascending, dy, z_all, states, bmat, cmat, adj, perm, perm_t):
    rows = states.shape[0]
    nb = rows // ROW_BLOCK
    n_lat = nb - 1
    bw, sw = S5_BLOCK_WIDTH, 2 * NSTATE // S5_BLOCKS

    def block_index(i):
        if ascending:
            return jnp.where(i == nb - 1, n_lat, n_lat - 1 - i)
        return jnp.where(i == nb - 1, n_lat, i)

    def body(dy_ref, u_ref, s_ref, bm_ref, cm_ref, adj_ref, p_ref, pt_ref, du_ref, db_ref, dc_ref, da_ref,
             g, dup, db_acc, dc_acc, carry_re, carry_im):
        i = pl.program_id(0)

        @pl.when(i == 0)
        def _():
            carry_re[...] = jnp.zeros_like(carry_re)
            carry_im[...] = jnp.zeros_like(carry_im)
            da_ref[...] = jnp.zeros_like(da_ref)
            db_acc[...] = jnp.zeros_like(db_acc)
            dc_acc[...] = jnp.zeros_like(dc_acc)

        @pl.when(i < nb - 1)
        def _():
            dyp = _permute_rows(p_ref, dy_ref[...].astype(BF16))
            for c in range(S5_BLOCKS):
                g[:, c * sw:(c + 1) * sw] = _dot(dyp[:, c * bw:(c + 1) * bw], cm_ref[c], "nn")
                dc_acc[c] += _dot(dyp[:, c * bw:(c + 1) * bw], s_ref[:, c * sw:(c + 1) * sw].astype(BF16), "tn")

        @pl.when(i == nb - 1)
        def _():
            g[...] = jnp.zeros_like(g)

        _scan_chunk(g, g, adj_ref, carry_re, carry_im, not ascending, pair_ref=s_ref, acc_ref=da_ref)
        up = _permute_rows(p_ref, u_ref[...].astype(BF16))
        for c in range(S5_BLOCKS):
            gc = g[:, c * sw:(c + 1) * sw].astype(BF16)
            dup[:, c * bw:(c + 1) * bw] = _dot(gc, bm_ref[c], "nt")
            db_acc[c] += _dot(up[:, c * bw:(c + 1) * bw], gc, "tn")
        du_ref[...] = _unpermute_rows(pt_ref, dup[...])

        @pl.when(i == nb - 1)
        def _():
            mask = _block_diag_mask((bw, sw // 2))
            for acc, out in ((db_acc, db_ref), (dc_acc, dc_ref)):
                for c in range(S5_BLOCKS):
                    for part in range(2):
                        cols = slice(part * (sw // 2), (part + 1) * (sw // 2))
                        kept = jnp.where(mask, acc[c, :, cols], 0.0)
                        out[c, :, cols] = kept.reshape(bw // S5_GROUP, S5_GROUP, sw // 2).sum(axis=0)

    blk = lambda i: (block_index(i), 0)
    return pl.pallas_call(
        body, name=name, grid=(nb,),
        in_specs=[pl.BlockSpec((ROW_BLOCK, S5_WIDTH), lambda i: (jnp.minimum(block_index(i), n_lat - 1), 0)),
                  pl.BlockSpec((ROW_BLOCK, S5_WIDTH), blk), pl.BlockSpec((ROW_BLOCK, 2 * NSTATE), blk),
                  _full_spec(_S5_BLOCKED), _full_spec(_S5_BLOCKED), _full_spec(_S5_TABLES),
                  _full_spec((ROW_BLOCK, ROW_BLOCK)), _full_spec((ROW_BLOCK, ROW_BLOCK))],
        out_specs=[pl.BlockSpec((ROW_BLOCK, S5_WIDTH), blk), _full_spec(_S5_DIAG), _full_spec(_S5_DIAG),
                   _full_spec((SUBLANES, 2 * NSTATE))],
        out_shape=[_sds((rows, S5_WIDTH)), _sds(_S5_DIAG), _sds(_S5_DIAG), _sds((SUBLANES, 2 * NSTATE))],
        scratch_shapes=[pltpu.VMEM((ROW_BLOCK, 2 * NSTATE), F32), pltpu.VMEM((ROW_BLOCK, S5_WIDTH), F32),
                        pltpu.VMEM(_S5_BLOCKED, F32), pltpu.VMEM(_S5_BLOCKED, F32),
                        pltpu.VMEM((SUBLANES, NSTATE), F32), pltpu.VMEM((SUBLANES, NSTATE), F32)],
        compiler_params=_params(("arbitrary",)),
    )(dy, z_all, states, bmat, cmat, adj, perm, perm_t)


def _glu_fwd(z_all, y0, y1, d_skip, w_glu, n_rows):
    def body(u_ref, y0_ref, y1_ref, d_ref, w_ref, o_ref):
        y = d_ref[...] * u_ref[...] + y0_ref[...] + y1_ref[...]
        g = _gelu(y)
        t = _dot(g.astype(BF16), w_ref[...], "nn")
        o_ref[...] = (g * _sigmoid(t)).astype(o_ref.dtype)

    row = pl.BlockSpec((ROW_BLOCK, S5_WIDTH), lambda i: (i, 0))
    return pl.pallas_call(
        body, name="glu_fwd", grid=(n_rows // ROW_BLOCK,),
        in_specs=[row, row, row, pl.BlockSpec((1, S5_WIDTH), lambda i: (0, 0)),
                  pl.BlockSpec((S5_WIDTH, S5_WIDTH), lambda i: (0, 0))],
        out_specs=row, out_shape=_sds((n_rows, S5_WIDTH + CONV_WIDTH), BF16), compiler_params=_params(("parallel",)),
    )(z_all, y0, y1, d_skip, w_glu)


def _glu_bwd(d_ycat, z_all, y0, y1, d_skip, w_glu, n_rows):
    def body(do_ref, u_ref, y0_ref, y1_ref, d_ref, w_ref, dy_ref, dw_ref, dd_ref):
        @pl.when(pl.program_id(0) == 0)
        def _():
            dw_ref[...] = jnp.zeros_like(dw_ref)
            dd_ref[...] = jnp.zeros_like(dd_ref)

        u = u_ref[...]
        y = d_ref[...] * u + y0_ref[...] + y1_ref[...]
        g = _gelu(y)
        gb = g.astype(BF16)
        w = w_ref[...]
        sg = _sigmoid(_dot(gb, w, "nn"))
        do = do_ref[...]
        dt = do * g * sg * (1.0 - sg)
        dtb = dt.astype(BF16)
        dg = do * sg + _dot(dtb, w, "nt")
        dy = dg * _dgelu(y)
        dy_ref[...] = dy
        dw_ref[...] += _dot(gb, dtb, "tn")
        dd_ref[...] += _fold8(dy * u)

    row = pl.BlockSpec((ROW_BLOCK, S5_WIDTH), lambda i: (i, 0))
    sq = pl.BlockSpec((S5_WIDTH, S5_WIDTH), lambda i: (0, 0))
    return pl.pallas_call(
        body, name="glu_bwd", grid=(n_rows // ROW_BLOCK,),
        in_specs=[row, row, row, row, pl.BlockSpec((1, S5_WIDTH), lambda i: (0, 0)), sq],
        out_specs=[row, sq, pl.BlockSpec((SUBLANES, S5_WIDTH), lambda i: (0, 0))],
        out_shape=[_sds((n_rows, S5_WIDTH)), _sds((S5_WIDTH, S5_WIDTH)), _sds((SUBLANES, S5_WIDTH))],
        compiler_params=_params(("arbitrary",)),
    )(d_ycat, z_all, y0, y1, d_skip, w_glu)


CONV_HALF = CONV_K // 2


def _conv_block(n_rows):
    blk = min(1024, n_rows)
    assert blk >= CONV_HALF * GRID_W and n_rows % blk == 0
    return blk


def _conv_gate(z_all, n_rows):
    blk = _conv_block(n_rows)
    nb = n_rows // blk

    def body(v_ref, g_ref, o_ref):
        i = pl.program_id(0)
        inside = jnp.logical_and(i >= 1, i <= nb)

        @pl.when(inside)
        def _():
            o_ref[...] = v_ref[...] * _sigmoid(g_ref[...])

        @pl.when(jnp.logical_not(inside))
        def _():
            o_ref[...] = jnp.zeros_like(o_ref)

    src = lambda col: pl.BlockSpec((blk, CONV_WIDTH), lambda i: (jnp.clip(i - 1, 0, nb - 1), col))
    return pl.pallas_call(
        body, name="conv_gate", grid=(nb + 2,), in_specs=[src(1), src(2)],
        out_specs=pl.BlockSpec((blk, CONV_WIDTH), lambda i: (i, 0)),
        out_shape=_sds(((nb + 2) * blk, CONV_WIDTH)), compiler_params=_params(("parallel",)),
    )(z_all, z_all)


def _stream_padded(pad_ref, buf, sems, blk, n_blocks):
    i = pl.program_id(0)

    def copy(b):
        rows = pl.ds(pl.multiple_of(b * blk, blk), blk)
        return pltpu.make_async_copy(pad_ref.at[rows, :], buf.at[rows, :], sems.at[b])

    @pl.when(i == 0)
    def _():
        for b in range(n_blocks):
            copy(b).start()
        copy(0).wait()
        copy(1).wait()

    copy(i + 2).wait()
    return pl.multiple_of(i * blk, blk)


def _conv_fwd(hh_pad, w, b, ln_g, ln_b, ycat, n_rows):
    blk = _conv_block(n_rows)
    nblk = n_rows // blk + 2

    def body(hh_ref, w_ref, b_ref, g_ref, lb_ref, ycat_ref, hc_ref, y_ref, win, sems):
        base = _stream_padded(hh_ref, win, sems, blk, nblk)

        def tile(t, _):
            r0 = pl.multiple_of(t * CONV_ROWS, CONV_ROWS)
            acc = jnp.zeros((CONV_ROWS, CONV_WIDTH), F32)
            for k in range(CONV_K):
                acc = acc + w_ref[k:k + 1, :] * win[pl.ds(base + r0 + blk + (k - CONV_HALF) * GRID_W, CONV_ROWS), :]
            hc = acc + b_ref[...]
            hc_ref[pl.ds(r0, CONV_ROWS), :] = hc
            mu = jnp.mean(hc, axis=-1, keepdims=True)
            xc = hc - mu
            ln = xc * lax.rsqrt(jnp.mean(xc * xc, axis=-1, keepdims=True) + EPS_LN) * g_ref[...] + lb_ref[...]
            y_ref[pl.ds(r0, CONV_ROWS), :] = _silu(ln).astype(y_ref.dtype)
            return 0

        lax.fori_loop(0, blk // CONV_ROWS, tile, 0)

    vec = pl.BlockSpec((1, CONV_WIDTH), lambda i: (0, 0))
    row = pl.BlockSpec((blk, CONV_WIDTH), lambda i: (i, 0))
    return pl.pallas_call(
        body, name="conv_fwd", grid=(n_rows // blk,),
        in_specs=[ANY, pl.BlockSpec((CONV_K, CONV_WIDTH), lambda i: (0, 0)), vec, vec, vec, ANY],
        out_specs=[row, pl.BlockSpec((blk, CONV_WIDTH), lambda i: (i, 1))],
        out_shape=[_sds((n_rows, CONV_WIDTH)), _sds(ycat.shape, ycat.dtype)], input_output_aliases={5: 1},
        scratch_shapes=[pltpu.VMEM((nblk * blk, CONV_WIDTH), F32), pltpu.SemaphoreType.DMA((nblk,))],
        compiler_params=_params(("arbitrary",)),
    )(hh_pad, w, b, ln_g, ln_b, ycat)


def _conv_bwd_norm(d_ycat, hc, ln_g, ln_b, n_rows):
    blk = _conv_block(n_rows)
    nb = n_rows // blk

    def body(dy_ref, hc_ref, g_ref, lb_ref, o_ref, sums):
        i = pl.program_id(0)

        @pl.when(i == 0)
        def _():
            sums[...] = jnp.zeros_like(sums)

        inside = jnp.logical_and(i >= 1, i <= nb)

        @pl.when(inside)
        def _():
            hcv = hc_ref[...]
            mu = jnp.mean(hcv, axis=-1, keepdims=True)
            xc = hcv - mu
            rstd = lax.rsqrt(jnp.mean(xc * xc, axis=-1, keepdims=True) + EPS_LN)
            xh = xc * rstd
            g = g_ref[...]
            dln = dy_ref[...] * _dsilu(xh * g + lb_ref[...])
            dxh = dln * g
            dhc = rstd * (dxh - jnp.mean(dxh, axis=-1, keepdims=True) - xh * jnp.mean(dxh * xh, axis=-1, keepdims=True))
            o_ref[...] = dhc
            sums[0] += _fold8(dhc)
            sums[1] += _fold8(dln * xh)
            sums[2] += _fold8(dln)

        @pl.when(jnp.logical_not(inside))
        def _():
            o_ref[...] = jnp.zeros_like(o_ref)

    vec = pl.BlockSpec((1, CONV_WIDTH), lambda i: (0, 0))
    return pl.pallas_call(
        body, name="conv_bwd_norm", grid=(nb + 2,),
        in_specs=[pl.BlockSpec((blk, CONV_WIDTH), lambda i: (jnp.clip(i - 1, 0, nb - 1), 1)),
                  pl.BlockSpec((blk, CONV_WIDTH), lambda i: (jnp.clip(i - 1, 0, nb - 1), 0)), vec, vec],
        out_specs=[pl.BlockSpec((blk, CONV_WIDTH), lambda i: (i, 0)),
                   pl.BlockSpec((3, SUBLANES, CONV_WIDTH), lambda i: (0, 0, 0))],
        out_shape=[_sds(((nb + 2) * blk, CONV_WIDTH)), _sds((3, SUBLANES, CONV_WIDTH))],
        compiler_params=_params(("arbitrary",)),
    )(d_ycat, hc, ln_g, ln_b)


def _conv_bwd_taps(dhc_pad, hh_pad, z_all, w, n_rows):
    blk = _conv_block(n_rows)
    nblk = n_rows // blk + 2

    def body(dhc_ref, hh_ref, v_ref, g_ref, w_ref, dv_ref, dg_ref, dw_ref, dwin, hwin, dsems, hsems):
        @pl.when(pl.program_id(0) == 0)
        def _():
            dw_ref[...] = jnp.zeros_like(dw_ref)

        base = _stream_padded(dhc_ref, dwin, dsems, blk, nblk)
        _stream_padded(hh_ref, hwin, hsems, blk, nblk)

        def tile(t, _):
            r0 = pl.multiple_of(t * CONV_ROWS, CONV_ROWS) + base
            dh = dwin[pl.ds(r0 + blk, CONV_ROWS), :]
            acc = jnp.zeros((CONV_ROWS, CONV_WIDTH), F32)
            for k in range(CONV_K):
                off = (k - CONV_HALF) * GRID_W
                acc = acc + w_ref[k:k + 1, :] * dwin[pl.ds(r0 + blk - off, CONV_ROWS), :]
                dw_ref[k] += _fold8(dh * hwin[pl.ds(r0 + blk + off, CONV_ROWS), :])
            rs = pl.ds(pl.multiple_of(t * CONV_ROWS, CONV_ROWS), CONV_ROWS)
            sg = _sigmoid(g_ref[rs, :])
            vv = v_ref[rs, :]
            dv_ref[rs, :] = acc * sg
            dg_ref[rs, :] = acc * vv * sg * (1.0 - sg)
            return 0

        lax.fori_loop(0, blk // CONV_ROWS, tile, 0)

    row = pl.BlockSpec((blk, CONV_WIDTH), lambda i: (i, 0))
    return pl.pallas_call(
        body, name="conv_bwd_taps", grid=(n_rows // blk,),
        in_specs=[ANY, ANY,
            pl.BlockSpec((blk, CONV_WIDTH), lambda i: (i, 1)), pl.BlockSpec((blk, CONV_WIDTH), lambda i: (i, 2)),
            pl.BlockSpec((CONV_K, CONV_WIDTH), lambda i: (0, 0))],
        out_specs=[row, row, pl.BlockSpec((CONV_K, SUBLANES, CONV_WIDTH), lambda i: (0, 0, 0))],
        out_shape=[_sds((n_rows, CONV_WIDTH)), _sds((n_rows, CONV_WIDTH)), _sds((CONV_K, SUBLANES, CONV_WIDTH))],
        scratch_shapes=[pltpu.VMEM((nblk * blk, CONV_WIDTH), F32), pltpu.VMEM((nblk * blk, CONV_WIDTH), F32),
                        pltpu.SemaphoreType.DMA((nblk,)), pltpu.SemaphoreType.DMA((nblk,))],
        compiler_params=_params(("arbitrary",)),
    )(dhc_pad, hh_pad, z_all, z_all, w)


def _dz_assemble(du0, du1, dy, d_skip, dv, dgate, n_lat):
    rows = du0.shape[0]
    nb = rows // ROW_BLOCK

    w = S5_WIDTH

    def body(a_ref, b_ref, dy_ref, d_ref, dv_ref, dg_ref, o_ref):
        lat = pl.program_id(0) < n_lat

        @pl.when(lat)
        def _():
            o_ref[:, 0:w] = (a_ref[...] + b_ref[...] + dy_ref[...] * d_ref[...]).astype(o_ref.dtype)
            o_ref[:, w:2 * w] = dv_ref[...].astype(o_ref.dtype)
            o_ref[:, 2 * w:3 * w] = dg_ref[...].astype(o_ref.dtype)

        @pl.when(jnp.logical_not(lat))
        def _():
            o_ref[:, 0:w] = (a_ref[...] + b_ref[...]).astype(o_ref.dtype)
            o_ref[:, w:3 * w] = jnp.zeros((ROW_BLOCK, 2 * w), o_ref.dtype)

    all_rows = pl.BlockSpec((ROW_BLOCK, w), lambda i: (i, 0))
    lat_rows = pl.BlockSpec((ROW_BLOCK, w), lambda i: (jnp.minimum(i, n_lat - 1), 0))
    return pl.pallas_call(
        body, name="dz_assemble", grid=(nb,),
        in_specs=[all_rows, all_rows, lat_rows, pl.BlockSpec((1, w), lambda i: (0, 0)), lat_rows, lat_rows],
        out_specs=pl.BlockSpec((ROW_BLOCK, IN_COLS), lambda i: (i, 0)),
        out_shape=_sds((rows, IN_COLS), BF16), compiler_params=_params(("parallel",)),
    )(du0, du1, dy, d_skip, dv, dgate)


def _sum_parts(parts):
    _, r, c = parts.shape

    def body(p_ref, o_ref):
        acc = p_ref[0]
        for q in range(1, NDEV):
            acc = acc + p_ref[q]
        o_ref[...] = acc

    return pl.pallas_call(body, name="sum_parts", out_shape=_sds((r, c)), compiler_params=_params())(parts)


def _row_tile(r, c):
    best = r
    for t in (1024, 512, 256, 128, 64, 32, 16, 8):
        if r % t == 0 and t * c <= 128 * 1024:
            return t
    return best


def _adamw(name, w, gparts, m, v):
    r, c = w.shape
    np_ = gparts.shape[0]
    tr = _row_tile(r, c)

    def body(w_ref, g_ref, m_ref, v_ref, go_ref, d_ref, mo_ref, vo_ref):
        g = g_ref[0].astype(F32)
        for q in range(1, np_):
            g = g + g_ref[q].astype(F32)
        m2 = ADAM_B1 * m_ref[...] + (1.0 - ADAM_B1) * g
        v2 = ADAM_B2 * v_ref[...] + (1.0 - ADAM_B2) * jnp.square(g)
        m_hat = m2 / (1.0 - ADAM_B1 ** ADAM_STEP)
        v_hat = v2 / (1.0 - ADAM_B2 ** ADAM_STEP)
        go_ref[...] = g
        d_ref[...] = -ADAM_LR * (m_hat / (jnp.sqrt(v_hat) + ADAM_EPS) + ADAM_WD * w_ref[...])
        mo_ref[...] = m2
        vo_ref[...] = v2

    row = pl.BlockSpec((tr, c), lambda i: (i, 0))
    return pl.pallas_call(
        body, name=name, grid=(r // tr,),
        in_specs=[row, pl.BlockSpec((np_, tr, c), lambda i: (0, i, 0)), row, row],
        out_specs=[row] * 4, out_shape=[_sds((r, c))] * 4, compiler_params=_params(("parallel",)),
    )(w, gparts, m, v)


def _adamw_native(name, w, g, m, v):
    def body(w_ref, g_ref, m_ref, v_ref, d_ref, mo_ref, vo_ref):
        gv = g_ref[...]
        m2 = ADAM_B1 * m_ref[...] + (1.0 - ADAM_B1) * gv
        v2 = ADAM_B2 * v_ref[...] + (1.0 - ADAM_B2) * jnp.square(gv)
        m_hat = m2 / (1.0 - ADAM_B1 ** ADAM_STEP)
        v_hat = v2 / (1.0 - ADAM_B2 ** ADAM_STEP)
        d_ref[...] = -ADAM_LR * (m_hat / (jnp.sqrt(v_hat) + ADAM_EPS) + ADAM_WD * w_ref[...])
        mo_ref[...] = m2
        vo_ref[...] = v2

    return pl.pallas_call(body, name=name, out_shape=[_sds(w.shape)] * 3, compiler_params=_params())(w, g, m, v)


SMALL = ["c_ctx", "ada_b", "norm1_g", "s5_lam_re", "s5_lam_im", "s5_log_dt", "s5_d", "conv_b", "conv_ln_g", "conv_ln_b",
         "norm2_g", "final_g"]
SMALL_PACKED_ROWS = 24


def _pack_rows(parts, rows):
    flat = jnp.concatenate([p.reshape(-1).astype(F32) for p in parts])
    return jnp.pad(flat, (0, rows * D_MODEL - flat.shape[0])).reshape(rows, D_MODEL)


def _unpack_rows(packed, shapes):
    flat = packed.reshape(-1)
    out, off = [], 0
    for shape in shapes:
        size = 1
        for s in shape:
            size *= s
        out.append(flat[off:off + size].reshape(shape))
        off += size
    return out


def kernel(x, c, ctx, c_ctx, ada_w, ada_b, norm1_g, w_in, s5_lam_re, s5_lam_im, s5_log_dt, s5_b_re, s5_b_im, s5_c_re, s5_c_im, s5_d, s5_w_glu, conv_w, conv_b, conv_ln_g, conv_ln_b, w_out, norm2_g, mlp_w1, mlp_w2, final_g, loss_target, m_c_ctx, m_ada_w, m_ada_b, m_norm1_g, m_w_in, m_s5_lam_re, m_s5_lam_im, m_s5_log_dt, m_s5_b_re, m_s5_b_im, m_s5_c_re, m_s5_c_im, m_s5_d, m_s5_w_glu, m_conv_w, m_conv_b, m_conv_ln_g, m_conv_ln_b, m_w_out, m_norm2_g, m_mlp_w1, m_mlp_w2, m_final_g, v_c_ctx, v_ada_w, v_ada_b, v_norm1_g, v_w_in, v_s5_lam_re, v_s5_lam_im, v_s5_log_dt, v_s5_b_re, v_s5_b_im, v_s5_c_re, v_s5_c_im, v_s5_d, v_s5_w_glu, v_conv_w, v_conv_b, v_conv_ln_g, v_conv_ln_b, v_w_out, v_norm2_g, v_mlp_w1, v_mlp_w2, v_final_g):
    weights = dict(c_ctx=c_ctx, ada_w=ada_w, ada_b=ada_b, norm1_g=norm1_g, w_in=w_in, s5_lam_re=s5_lam_re, s5_lam_im=s5_lam_im, s5_log_dt=s5_log_dt, s5_b_re=s5_b_re, s5_b_im=s5_b_im, s5_c_re=s5_c_re, s5_c_im=s5_c_im, s5_d=s5_d, s5_w_glu=s5_w_glu, conv_w=conv_w, conv_b=conv_b, conv_ln_g=conv_ln_g, conv_ln_b=conv_ln_b, w_out=w_out, norm2_g=norm2_g, mlp_w1=mlp_w1, mlp_w2=mlp_w2, final_g=final_g)
    mom1 = dict(c_ctx=m_c_ctx, ada_w=m_ada_w, ada_b=m_ada_b, norm1_g=m_norm1_g, w_in=m_w_in, s5_lam_re=m_s5_lam_re, s5_lam_im=m_s5_lam_im, s5_log_dt=m_s5_log_dt, s5_b_re=m_s5_b_re, s5_b_im=m_s5_b_im, s5_c_re=m_s5_c_re, s5_c_im=m_s5_c_im, s5_d=m_s5_d, s5_w_glu=m_s5_w_glu, conv_w=m_conv_w, conv_b=m_conv_b, conv_ln_g=m_conv_ln_g, conv_ln_b=m_conv_ln_b, w_out=m_w_out, norm2_g=m_norm2_g, mlp_w1=m_mlp_w1, mlp_w2=m_mlp_w2, final_g=m_final_g)
    mom2 = dict(c_ctx=v_c_ctx, ada_w=v_ada_w, ada_b=v_ada_b, norm1_g=v_norm1_g, w_in=v_w_in, s5_lam_re=v_s5_lam_re, s5_lam_im=v_s5_lam_im, s5_log_dt=v_s5_log_dt, s5_b_re=v_s5_b_re, s5_b_im=v_s5_b_im, s5_c_re=v_s5_c_re, s5_c_im=v_s5_c_im, s5_d=v_s5_d, s5_w_glu=v_s5_w_glu, conv_w=v_conv_w, conv_b=v_conv_b, conv_ln_g=v_conv_ln_g, conv_ln_b=v_conv_ln_b, w_out=v_w_out, norm2_g=v_norm2_g, mlp_w1=v_mlp_w1, mlp_w2=v_mlp_w2, final_g=v_final_g)
    order = list(weights)

    me = 4 * lax.axis_index("x") + 2 * lax.axis_index("y") + lax.axis_index("c")
    xs, cs, tgt = x[0], ctx[0], loss_target[0]
    n_lat_rows, n_ctx_rows = xs.shape[0], cs.shape[0]
    n_rows = n_lat_rows + n_ctx_rows
    n_lat = n_lat_rows // ROW_BLOCK
    ada_cols = ada_w.shape[2]

    (c_all,), _ = _exchange("gather_c", [c], [True])
    c_all = c_all.reshape(NDEV, D_MODEL)

    cond_fwd = jnp.concatenate([c_all, c_ctx[None], jnp.zeros((7, D_MODEL), F32)])
    ada_b_loc = lax.dynamic_slice(ada_b, (0, me * ada_cols), (1, ada_cols))
    (mod_g,), mod_token = _exchange("gather_mod", [_ada_fwd(cond_fwd, ada_w[0], ada_b_loc)], [True])
    wi_send, wi_recv, wi_src, wi_land, wi_token = _exchange_start(
        "gather_w_in_start", [w_in[0].astype(BF16) + mod_token[0:1, 0:1].astype(BF16)], [True])
    mixer_w = [s5_w_glu[0].astype(BF16), conv_w[0] + wi_token[0:1, 0:1], w_out[0].astype(BF16)]
    mixer_send, mixer_recv, mixer_src, mixer_land, mixer_token = _exchange_start("gather_mixer_start", mixer_w, [True] * 3)
    mlp_w = [mlp_w1[0].astype(BF16), mlp_w2[0].astype(BF16) + mixer_token[0:1, 0:1].astype(BF16)]
    mlpw_send, mlpw_recv, mlpw_src, mlpw_land, mlpw_token = _exchange_start("gather_mlp_start", mlp_w, [True] * 2)
    mod_rows = jnp.transpose(mod_g, (1, 0, 2)).reshape(16, 6 * D_MODEL) + mlpw_token[0:1, 0:1]
    mod = lax.dynamic_slice(mod_rows, (me, 0), (1, 6 * D_MODEL)).reshape(6, D_MODEL)
    modc = mod_rows[8, :2 * D_MODEL].reshape(2, D_MODEL)
    sh1, sc1, g1, sh2, sc2, g2 = [mod[i:i + 1] for i in range(6)]

    a_all = _prenorm("prenorm1", xs, cs, norm1_g, jnp.stack([mod[0:2], modc]))
    wi_own, wi_landed = _exchange_wait("gather_w_in_wait", wi_send, wi_recv, wi_src, wi_land, [True], a_all)
    w_in_full = jnp.transpose(_with_own(wi_landed[0], wi_own[0], me), (1, 0, 2)).reshape(D_MODEL, IN_COLS)
    tm_all = 1088 if n_rows % 1088 == 0 else ROW_BLOCK
    (z_all,) = _matmul("in_proj", a_all, w_in_full, "nn", (n_rows, IN_COLS, D_MODEL), (tm_all, IN_COLS, D_MODEL),
                       [((n_rows, IN_COLS), F32)])

    lam_re, lam_im = s5_lam_re[0].reshape(2, 1, NSTATE), s5_lam_im[0].reshape(2, 1, NSTATE)
    ldt = jnp.repeat(s5_log_dt[0], S5_STATE, axis=-1).reshape(2, 1, NSTATE)
    bt_re = jnp.transpose(s5_b_re[0], (0, 3, 1, 2)).reshape(2, S5_GROUP, NSTATE)
    bt_im = jnp.transpose(s5_b_im[0], (0, 3, 1, 2)).reshape(2, S5_GROUP, NSTATE)
    groups_per_block = S5_GROUPS // S5_BLOCKS
    ct_re = jnp.tile(s5_c_re[0].reshape(2, S5_WIDTH, S5_STATE), (1, 1, groups_per_block))
    ct_im = jnp.tile(s5_c_im[0].reshape(2, S5_WIDTH, S5_STATE), (1, 1, groups_per_block))
    d_skip = s5_d[0].reshape(1, S5_WIDTH)
    perm = _segment_permutation()
    perm_t = perm.T
    disc, states, y_dir = [], [], []
    for d in range(2):
        disc.append(_s5_discretise(f"s5_disc{d}", d == 0, lam_re[d], lam_im[d], ldt[d], bt_re[d], bt_im[d], ct_re[d], ct_im[d]))
        _, tab, _, bmat, cmat = disc[d]
        s, yd = _s5_scan_fwd(f"s5_scan_fwd{d}", d == 0, z_all, bmat, cmat, tab, perm, perm_t)
        states.append(s)
        y_dir.append(yd)
    mixer_own, mixer_landed = _exchange_wait("gather_mixer_wait", mixer_send, mixer_recv, mixer_src, mixer_land,
                                             [True] * 3, y_dir[1])
    glu_g, conv_w_g, w_out_g = [_with_own(l, o, me) for l, o in zip(mixer_landed, mixer_own)]
    glu_full = glu_g.reshape(S5_WIDTH, S5_WIDTH)
    conv_w_full = jnp.transpose(conv_w_g, (1, 0, 2)).reshape(CONV_K, CONV_WIDTH)
    w_out_full = w_out_g.reshape(D_MODEL, D_MODEL)
    ycat = _glu_fwd(z_all, y_dir[0], y_dir[1], d_skip, glu_full, n_lat_rows)

    hh_pad = _conv_gate(z_all, n_lat_rows)
    hc, ycat = _conv_fwd(hh_pad, conv_w_full, conv_b, conv_ln_g, conv_ln_b, ycat, n_lat_rows)

    tm = min(1024, n_lat_rows)
    tm_e = min(512, n_lat_rows)
    w1_cols = D_FF // NDEV
    row_vec = lambda tn: pl.BlockSpec((1, tn), lambda i, j, k: (0, j))
    out_tile = lambda t_m, t_n: pl.BlockSpec((t_m, t_n), lambda i, j, k: (i, j))
    full_rows = ((n_lat_rows, D_MODEL), F32)
    sums = ((n_lat_rows // tm_e, SUBLANES, D_MODEL), F32)
    sums_spec = pl.BlockSpec((None, SUBLANES, D_MODEL), lambda i, j, k: (i, 0, 0))
    vec = lambda v: (v, row_vec(D_MODEL))
    mix, h1, a2 = _matmul("out_proj", ycat, w_out_full, "nn", (n_lat_rows, D_MODEL, D_MODEL), (tm_e, D_MODEL, D_MODEL),
                          [full_rows, full_rows, ((n_lat_rows, D_MODEL), BF16)], epi=_epi_residual_prenorm,
                          epi_extra=[(xs, out_tile(tm_e, D_MODEL)), vec(g1), vec(norm2_g), vec(sc2), vec(sh2)])
    mlpw_own, mlpw_landed = _exchange_wait("gather_mlp_wait", mlpw_send, mlpw_recv, mlpw_src, mlpw_land, [True] * 2, a2)
    w1_g, w2_g = [_with_own(l, o, me) for l, o in zip(mlpw_landed, mlpw_own)]
    w2_full = w2_g.reshape(D_FF, D_MODEL)
    tm_up = min(2048, n_lat_rows)
    (f,) = _matmul("mlp_up", a2, w1_g, "nn", (n_lat_rows, D_FF, D_MODEL), (tm_up, w1_cols, D_MODEL),
                   [((n_lat_rows, D_FF), BF16)], b_spec=pl.BlockSpec((None, D_MODEL, w1_cols), lambda i, j, k: (j, 0, 0)))
    sq_relu = lambda t: jnp.square(jnp.maximum(t, 0.0))
    mlp_out, d_h2, dm2, err_sums, d_final_g8 = _matmul(
        "mlp_down", f, w2_full, "nn", (n_lat_rows, D_MODEL, D_FF), (tm_e, D_MODEL, 1024),
        [full_rows, full_rows, ((n_lat_rows, D_MODEL), BF16), sums, sums], a_fn=sq_relu, epi=_epi_residual_loss,
        epi_extra=[(h1, out_tile(tm_e, D_MODEL)), vec(g2), (tgt, out_tile(tm_e, D_MODEL)), vec(final_g[None])],
        out_specs=[out_tile(tm_e, D_MODEL)] * 3 + [sums_spec] * 2)

    (d_f,) = _matmul("mlp_down_dx", dm2, w2_full, "nt", (n_lat_rows, D_FF, D_MODEL), (tm, 512, D_MODEL),
                     [((n_lat_rows, D_FF), BF16)],
                     epi=lambda acc, ft: (acc * 2.0 * jnp.maximum(ft.astype(F32), 0.0),), epi_extra=[(f, out_tile(tm, 512))])
    (g_w2,) = _matmul("mlp_down_dw", f, dm2, "tn", (D_FF, D_MODEL, n_lat_rows), (1024, D_MODEL, tm),
                      [((D_FF, D_MODEL), F32)], a_fn=sq_relu)
    (g_w1,) = _matmul("mlp_up_dw", a2, d_f, "tn", (D_MODEL, D_FF, n_lat_rows), (D_MODEL, w1_cols, tm),
                      [((NDEV, D_MODEL, w1_cols), F32)],
                      out_specs=[pl.BlockSpec((None, D_MODEL, w1_cols), lambda i, j, k: (j, 0, 0))])
    mlp_send, mlp_recv, mlp_src, mlp_land, mlp_token = _exchange_start(
        "scatter_mlp_start", [g_w1, g_w2.reshape(NDEV, D_FF // NDEV, D_MODEL)], [False] * 2)
    d_h1, dm1, *sums2 = _matmul(
        "mlp_up_dx", d_f, w1_g, "nt", (n_lat_rows, D_MODEL, D_FF), (tm_e, D_MODEL, w1_cols),
        [full_rows, ((n_lat_rows, D_MODEL), BF16)] + [sums] * 4, epi=_epi_norm_bwd,
        epi_extra=[(h1, out_tile(tm_e, D_MODEL)), (d_h2, out_tile(tm_e, D_MODEL)), (mlp_out, out_tile(tm_e, D_MODEL)),
                   vec(norm2_g), vec(sc2 + mlp_token[0:1, 0:1]), vec(g1)],
        b_spec=pl.BlockSpec((None, D_MODEL, w1_cols), lambda i, j, k: (k, 0, 0)),
        out_specs=[out_tile(tm_e, D_MODEL)] * 2 + [sums_spec] * 4)

    (d_ycat,) = _matmul("out_proj_dx", dm1, w_out_full, "nt", (n_lat_rows, D_MODEL, D_MODEL), (tm, D_MODEL, D_MODEL),
                        [((n_lat_rows, D_MODEL), F32)])
    (g_w_out,) = _matmul("out_proj_dw", ycat, dm1, "tn", (D_MODEL, D_MODEL, n_lat_rows), (D_MODEL, D_MODEL, 512),
                         [((D_MODEL, D_MODEL), F32)])

    dy, g_glu, dd8 = _glu_bwd(d_ycat, z_all, y_dir[0], y_dir[1], d_skip, glu_full, n_lat_rows)
    proj_send, proj_recv, proj_src, proj_land, proj_token = _exchange_start(
        "scatter_proj_start",
        [g_w_out.reshape(NDEV, D_MODEL // NDEV, D_MODEL), g_glu.reshape(NDEV, S5_WIDTH // NDEV, S5_WIDTH)], [False] * 2)
    perm = perm + proj_token[0:1, 0:1].astype(BF16)
    du, g_lam_re, g_lam_im, g_ldt, g_bt, g_cdiag = [], [], [], [], [], []
    for d in range(2):
        _, _, adj, bmat, cmat = disc[d]
        du_d, d_bdiag, d_cdiag, d_abar8 = _s5_scan_bwd(f"s5_scan_bwd{d}", d == 0, dy, z_all, states[d], bmat, cmat, adj,
                                                       perm, perm_t)
        du.append(du_d)
        d_bbar = jnp.transpose(d_bdiag.reshape(S5_BLOCKS, S5_GROUP, 2, NSTATE // S5_BLOCKS), (2, 1, 0, 3)).reshape(
            2 * S5_GROUP, NSTATE)
        d_lam8, d_bt = _s5_discretise_bwd(f"s5_disc_bwd{d}", lam_re[d], lam_im[d], ldt[d], bt_re[d], bt_im[d], d_abar8, d_bbar)
        g_lam_re.append(d_lam8[0].reshape(S5_GROUPS, S5_STATE))
        g_lam_im.append(d_lam8[1].reshape(S5_GROUPS, S5_STATE))
        g_ldt.append(d_lam8[2].reshape(S5_GROUPS, S5_STATE).sum(axis=-1))
        g_bt.append(d_bt)
        g_cdiag.append(d_cdiag)

    dhc_pad, conv_sums = _conv_bwd_norm(d_ycat, hc, conv_ln_g, conv_ln_b, n_lat_rows)
    d_v, d_gate, g_conv_w8 = _conv_bwd_taps(dhc_pad, hh_pad, z_all, conv_w_full, n_lat_rows)

    dz_all = _dz_assemble(du[0], du[1], dy, d_skip, d_v, d_gate, n_lat)
    (g_w_in_full,) = _matmul("in_proj_dw", a_all, dz_all, "tn", (D_MODEL, IN_COLS, n_rows), (D_MODEL, IN_COLS, tm_all),
                             [((D_MODEL, IN_COLS), F32)])
    g_w_in_parts = jnp.transpose(g_w_in_full.reshape(D_MODEL, NDEV, IN_COLS // NDEV), (1, 0, 2)).astype(BF16)
    win_send, win_recv, win_src, win_land, win_token = _exchange_start("scatter_w_in_start", [g_w_in_parts], [False])
    (d_a_all,) = _matmul("in_proj_dx", dz_all, w_in_full + win_token[0:1, 0:1].astype(BF16), "nt",
                         (n_rows, D_MODEL, IN_COLS), (tm_all, D_MODEL, IN_COLS), [((n_rows, D_MODEL), F32)])
    grad_x, sums1 = _norm_bwd("norm1_bwd", xs, d_a_all, 0, norm1_g, sc1, res=d_h1, aux=mix)
    (sums1c,) = _norm_bwd("norm1_bwd_ctx", cs, d_a_all, n_lat, norm1_g, modc[1:2])

    s1, s1c, s2 = sums1.sum(axis=1), sums1c.sum(axis=1), [p.sum(axis=(0, 1)) for p in sums2]
    d_mod = jnp.concatenate([s1[0], s1[1], s1[3], s2[0], s2[1], s2[3]])
    d_modc = jnp.concatenate([s1c[0], s1c[1], jnp.zeros((4 * D_MODEL,), F32)])
    (dmod_g,), _ = _exchange("gather_dmod", [jnp.stack([d_mod, d_modc])], [True])
    dmod16 = jnp.concatenate([dmod_g[:, 0], dmod_g[:, 1]])
    dmod16_loc = lax.dynamic_slice(dmod16, (0, me * ada_cols), (16, ada_cols))
    cond_bwd = jnp.concatenate([c_all, jnp.broadcast_to(c_ctx[None], (NDEV, D_MODEL))])
    g_ada_w, g_c_ctx8 = _ada_bwd(cond_bwd, dmod16_loc, ada_w[0], c_ctx[None])

    small_parts = dict(
        c_ctx=g_c_ctx8[0], ada_b=d_mod + d_modc, norm1_g=s1[2] + s1c[2],
        s5_lam_re=jnp.stack(g_lam_re), s5_lam_im=jnp.stack(g_lam_im), s5_log_dt=jnp.stack(g_ldt),
        s5_d=dd8.sum(axis=0), conv_b=conv_sums[0].sum(axis=0), conv_ln_g=conv_sums[1].sum(axis=0),
        conv_ln_b=conv_sums[2].sum(axis=0), norm2_g=s2[2], final_g=d_final_g8.sum(axis=(0, 1)))
    reduced_shapes = [(SMALL_PACKED_ROWS, D_MODEL), (2, 2 * S5_GROUP, NSTATE), (2,) + _S5_DIAG, (1,)]
    small_g = _pack_rows(
        [_pack_rows([small_parts[n] for n in SMALL], SMALL_PACKED_ROWS), jnp.stack(g_bt), jnp.stack(g_cdiag),
         (0.5 / D_MODEL * jnp.sum(err_sums)).reshape(1)], SMALL_ROWS).reshape(NDEV, SMALL_ROWS // NDEV, D_MODEL)
    g_conv_w_parts = jnp.transpose(g_conv_w8.sum(axis=1).reshape(CONV_K, NDEV, CONV_WIDTH // NDEV), (1, 0, 2))

    res = {}

    def own_chunk(src):
        return lax.dynamic_index_in_dim(src, me, 0, keepdims=False)

    def adamw_big(name, parts):
        outs = _adamw("adamw_" + name, weights[name][0], parts, mom1[name][0], mom2[name][0])
        res[name] = [o[None] for o in outs]
        return outs[0]

    sm_send, sm_recv, sm_src, sm_land, sm_token = _exchange_start("scatter_small_start", [g_conv_w_parts, small_g],
                                                                  [False] * 2)
    done = adamw_big("ada_w", g_ada_w[None] + sm_token[0:1, 0:1])
    mlp_src, mlp_landed = _exchange_wait("scatter_mlp_wait", mlp_send, mlp_recv, mlp_src, mlp_land, [False] * 2, done)
    p_w1, p_w2 = [_with_own(l, own_chunk(s), me) for l, s in zip(mlp_landed, mlp_src)]
    adamw_big("mlp_w1", p_w1)
    done = adamw_big("mlp_w2", p_w2)
    sm_src, sm_landed = _exchange_wait("scatter_small_wait", sm_send, sm_recv, sm_src, sm_land, [False] * 2, done)
    p_conv_w, p_small = [_with_own(l, own_chunk(s), me) for l, s in zip(sm_landed, sm_src)]
    ga_send, ga_recv, ga_src, ga_land, ga_token = _exchange_start("gather_small_start", [_sum_parts(p_small)], [True])
    proj_src, proj_landed = _exchange_wait("scatter_proj_wait", proj_send, proj_recv, proj_src, proj_land, [False] * 2,
                                           ga_token)
    p_w_out, p_glu = [_with_own(l, own_chunk(s), me) for l, s in zip(proj_landed, proj_src)]
    adamw_big("w_out", p_w_out)
    done = adamw_big("s5_w_glu", p_glu)
    win_src, win_landed = _exchange_wait("scatter_w_in_wait", win_send, win_recv, win_src, win_land, [False], done)
    adamw_big("w_in", _with_own(win_landed[0], own_chunk(win_src[0]), me))
    done = adamw_big("conv_w", p_conv_w)
    ga_own, ga_landed = _exchange_wait("gather_small_wait", ga_send, ga_recv, ga_src, ga_land, [True], done)
    small_all = _with_own(ga_landed[0], ga_own[0], me).reshape(1, SMALL_ROWS, D_MODEL)
    _, r_bt, r_cdiag, loss = _unpack_rows(small_all, reduced_shapes)
    loss = loss.reshape(())
    pack = lambda src: _pack_rows([src[n] for n in SMALL], SMALL_PACKED_ROWS)
    outs = _adamw("adamw_small", pack(weights), small_all, pack(mom1), pack(mom2))
    unpacked = [_unpack_rows(o, [weights[n].shape for n in SMALL]) for o in outs]
    for i, name in enumerate(SMALL):
        res[name] = [u[i] for u in unpacked]
    to_gph = lambda t: jnp.transpose(t.reshape(2, S5_GROUP, S5_GROUPS, S5_STATE), (0, 2, 3, 1))[None]
    r_c = jnp.transpose(r_cdiag.reshape(2, S5_BLOCKS, S5_GROUP, 2, groups_per_block, S5_STATE), (3, 0, 1, 4, 2, 5)).reshape(
        2, 1, 2, S5_GROUPS, S5_GROUP, S5_STATE)
    native = dict(s5_b_re=to_gph(r_bt[:, :S5_GROUP]), s5_b_im=to_gph(r_bt[:, S5_GROUP:]), s5_c_re=r_c[0], s5_c_im=-r_c[1])
    for name, grad in native.items():
        res[name] = [grad, *_adamw_native("adamw_" + name, weights[name], grad, mom1[name], mom2[name])]

    return (loss, grad_x[None], *[res[n][0] for n in order], *[res[n][1] for n in order],
            *[res[n][2] for n in order], *[res[n][3] for n in order])
```

```python
import functools

import jax
import jax.numpy as jnp
from jax import lax
from jax.experimental import pallas as pl
from jax.experimental.pallas import tpu as pltpu

F32 = jnp.float32
BF16 = jnp.bfloat16
MESH = pl.DeviceIdType.MESH
ANY = pl.BlockSpec(memory_space=pl.ANY)

NDEV = 8
D_MODEL = 1024
GRID_W = 64
S5_WIDTH = 512
S5_GROUP = 16
S5_GROUPS = 32
S5_STATE = 64
NSTATE = S5_GROUPS * S5_STATE
CONV_WIDTH = 512
CONV_K = 31
IN_COLS = S5_WIDTH + 2 * CONV_WIDTH
D_FF = 4 * D_MODEL
EPS_RMS = 1e-6
EPS_LN = 1e-5
ADAM_LR = 0.001
ADAM_B1 = 0.9
ADAM_B2 = 0.999
ADAM_EPS = 1e-08
ADAM_WD = 0.01
ADAM_STEP = 10

SUBLANES = 8
LANES = 128
ROW_BLOCK = 256
SCAN_LANES = 512
SCAN_UNROLL = 4
SEGMENTS = SUBLANES
STEPS = ROW_BLOCK // SEGMENTS
S5_BLOCKS = 4
S5_BLOCK_WIDTH = S5_WIDTH // S5_BLOCKS
CONV_ROWS = 64
VMEM_LIMIT = 48 * 1024 * 1024
SMALL_ROWS = 320


def _params(sem=None):
    kw = dict(vmem_limit_bytes=VMEM_LIMIT)
    if sem is not None:
        kw["dimension_semantics"] = sem
    return pltpu.CompilerParams(**kw)


def _sds(shape, dtype=F32):
    return jax.ShapeDtypeStruct(tuple(shape), dtype)


def _fold8(x):
    return x.reshape(x.shape[0] // SUBLANES, SUBLANES, x.shape[1]).sum(axis=0)


def _sigmoid(x):
    return 1.0 / (1.0 + jnp.exp(-x))


def _silu(x):
    return x * _sigmoid(x)


def _dsilu(x):
    s = _sigmoid(x)
    return s * (1.0 + x * (1.0 - s))


_GELU_C = 0.7978845608028654


def _gelu(x):
    return 0.5 * x * (1.0 + jnp.tanh(_GELU_C * (x + 0.044715 * x * x * x)))


def _dgelu(x):
    t = jnp.tanh(_GELU_C * (x + 0.044715 * x * x * x))
    return 0.5 * (1.0 + t) + 0.5 * x * (1.0 - t * t) * _GELU_C * (1.0 + 3.0 * 0.044715 * x * x)


def _rms(x):
    rstd = lax.rsqrt(jnp.mean(x * x, axis=-1, keepdims=True) + EPS_RMS)
    return x * rstd, rstd


def _epi_residual_prenorm(acc, res, gate, gain, scale, shift):
    h = res + gate * acc
    xh, _ = _rms(h)
    a = (xh * gain) * (1.0 + scale) + shift
    return acc, h, a, a.T


def _epi_residual_loss(acc, res, gate, target, gain):
    h = res + gate * acc
    xh, rstd = _rms(h)
    err = xh * gain - target
    dy = err * (1.0 / h.shape[-1])
    dxh = dy * gain
    dh = rstd * (dxh - xh * jnp.mean(dxh * xh, axis=-1, keepdims=True))
    return acc, dh, dh * gate, _fold8(err * err), _fold8(dy * xh)


def _epi_norm_bwd(d_act, x, res, aux, gain, scale, gate):
    xh, rstd = _rms(x)
    dn = d_act * (1.0 + scale)
    dxh = dn * gain
    dx = res + rstd * (dxh - xh * jnp.mean(dxh * xh, axis=-1, keepdims=True))
    return dx, dx * gate, _fold8(d_act), _fold8(d_act * (xh * gain)), _fold8(dn * xh), _fold8(res * aux)


def _dot(a, b, mode):
    dims = {"nn": (((1,), (0,)), ((), ())), "nt": (((1,), (1,)), ((), ())), "tn": (((0,), (0,)), ((), ()))}[mode]
    return lax.dot_general(a, b, dims, preferred_element_type=F32)


def _peers(x, y, c):
    out = []
    for k in range(1, NDEV):
        px = 1 - x if k & 4 else x
        py = 1 - y if k & 2 else y
        pc = 1 - c if k & 1 else c
        out.append(((px, py, pc), 4 * px + 2 * py + pc))
    return out


def _exchange_copies(src, land, send_sems, recv_sems, gather):
    x, y, c = lax.axis_index("x"), lax.axis_index("y"), lax.axis_index("c")
    me = 4 * x + 2 * y + c
    out = []
    for a in range(len(src)):
        for k, (peer, plin) in enumerate(_peers(x, y, c)):
            chunk = src[a] if gather[a] else src[a].at[plin]
            sems = dict(send_sem=send_sems.at[a * (NDEV - 1) + k], recv_sem=recv_sems.at[a * (NDEV - 1) + k],
                        device_id=peer, device_id_type=MESH)
            out.append((pltpu.make_async_remote_copy(src_ref=chunk, dst_ref=land[a].at[me], **sems),
                        pltpu.make_async_remote_copy(src_ref=chunk, dst_ref=land[a].at[plin], **sems)))
    return out


def _exchange(name, srcs, gather):
    n = len(srcs)
    outs = [_sds(((NDEV,) + s.shape) if g else s.shape, s.dtype) for s, g in zip(srcs, gather)]

    def body(*refs):
        src, dst, token = refs[:n], refs[n:2 * n], refs[2 * n]
        send_sems, recv_sems, local_sems = refs[2 * n + 1:]
        me = 4 * lax.axis_index("x") + 2 * lax.axis_index("y") + lax.axis_index("c")
        local = [pltpu.make_async_copy(src[a] if gather[a] else src[a].at[me], dst[a].at[me], local_sems.at[a])
                 for a in range(n)]
        for copy in local:
            copy.start()
        copies = _exchange_copies(src, dst, send_sems, recv_sems, gather)
        for copy, _ in copies:
            copy.start()
        token[...] = jnp.zeros_like(token)
        for copy, landing in copies:
            copy.wait_send()
            landing.wait_recv()
        for copy in local:
            copy.wait()

    nsem = n * (NDEV - 1)
    out = pl.pallas_call(
        body, name=name, out_shape=outs + [_sds((SUBLANES, LANES))], in_specs=[ANY] * n,
        out_specs=[ANY] * n + [pl.BlockSpec(memory_space=pltpu.VMEM)],
        scratch_shapes=[pltpu.SemaphoreType.DMA((nsem,)), pltpu.SemaphoreType.DMA((nsem,)), pltpu.SemaphoreType.DMA((n,))],
    )(*srcs)
    return out[:n], out[n]


HBM = pl.BlockSpec(memory_space=pltpu.HBM)
SEM = pl.BlockSpec(memory_space=pltpu.SEMAPHORE)
EFFECT = pltpu.SideEffectType.DATAFLOW_SIDE_EFFECTING


def _exchange_start(name, srcs, gather):
    n = len(srcs)
    lands = [lax.empty(((NDEV,) + s.shape) if g else s.shape, s.dtype) for s, g in zip(srcs, gather)]

    def body(*refs):
        src, land = refs[:n], refs[n:2 * n]
        send_sems, recv_sems = refs[2 * n], refs[2 * n + 1]
        token = refs[-1]
        for copy, _ in _exchange_copies(src, land, send_sems, recv_sems, gather):
            copy.start()
        token[...] = jnp.zeros_like(token)

    hbm = lambda v: pltpu.HBM(v.shape, v.dtype)
    nsem = n * (NDEV - 1)
    out = pl.pallas_call(
        body, name=name,
        out_shape=(pltpu.SemaphoreType.DMA((nsem,)), pltpu.SemaphoreType.DMA((nsem,)), *[hbm(v) for v in srcs],
                   *[hbm(v) for v in lands], _sds((SUBLANES, LANES))),
        in_specs=[HBM] * (2 * n), out_specs=(SEM, SEM, *([HBM] * (2 * n)), pl.BlockSpec(memory_space=pltpu.VMEM)),
        input_output_aliases={i: 2 + i for i in range(2 * n)},
        compiler_params=pltpu.CompilerParams(has_side_effects=EFFECT),
    )(*[pltpu.with_memory_space_constraint(v, pltpu.HBM) for v in list(srcs) + lands])
    return out[0], out[1], out[2:2 + n], out[2 + n:2 + 2 * n], out[-1]


def _exchange_wait(name, send_sems, recv_sems, srcs, lands, gather, after):
    n = len(srcs)

    def body(*refs):
        src, land = refs[:n], refs[n:2 * n]
        send_ref, recv_ref = refs[2 * n], refs[2 * n + 1]
        for copy, landing in _exchange_copies(src, land, send_ref, recv_ref, gather):
            copy.wait_send()
            landing.wait_recv()

    hbm = lambda v: pltpu.HBM(v.shape, v.dtype)
    out = pl.pallas_call(
        body, name=name, out_shape=[hbm(v) for v in list(srcs) + list(lands)],
        in_specs=[HBM] * (2 * n) + [SEM, SEM, ANY], out_specs=[HBM] * (2 * n),
        input_output_aliases={i: i for i in range(2 * n)},
        compiler_params=pltpu.CompilerParams(has_side_effects=EFFECT),
    )(*srcs, *lands, send_sems, recv_sems, after)
    return out[:n], out[n:]


def _with_own(landed, own, me):
    return lax.dynamic_update_slice(landed, own[None], (me,) + (0,) * own.ndim)


def _matmul(name, a, b, mode, mnk, tiles, outs, a_spec=None, b_spec=None, a_fn=None, a_extra=(),
            epi=None, epi_extra=(), out_specs=None):
    m_, n_, k_ = mnk
    tm, tn, tk = tiles
    nk = k_ // tk
    if a_spec is None:
        a_spec = (pl.BlockSpec((tk, tm), lambda i, j, k: (k, i)) if mode == "tn"
                  else pl.BlockSpec((tm, tk), lambda i, j, k: (i, k)))
    if b_spec is None:
        b_spec = (pl.BlockSpec((tn, tk), lambda i, j, k: (j, k)) if mode == "nt"
                  else pl.BlockSpec((tk, tn), lambda i, j, k: (k, j)))
    if out_specs is None:
        out_specs = [pl.BlockSpec((tm, tn), lambda i, j, k: (i, j)) for _ in outs]
    na, ne, no = len(a_extra), len(epi_extra), len(outs)

    def body(*refs):
        a_ref, b_ref = refs[0], refs[1]
        ax = refs[2:2 + na]
        ex = refs[2 + na:2 + na + ne]
        o = refs[2 + na + ne:2 + na + ne + no]

        def finish(res):
            res = epi(res, *[r[...] for r in ex]) if epi is not None else (res,)
            for ref, val in zip(o, res):
                ref[...] = val.astype(ref.dtype)

        at = a_ref[...]
        if a_fn is not None:
            at = a_fn(at, *[r[...] for r in ax])
        part = _dot(at.astype(BF16), b_ref[...].astype(BF16), mode)
        if nk == 1:
            finish(part)
            return
        acc = refs[-1]
        k = pl.program_id(2)

        @pl.when(k == 0)
        def _():
            acc[...] = part

        @pl.when(k > 0)
        def _():
            acc[...] += part

        @pl.when(k == nk - 1)
        def _():
            finish(acc[...])

    return pl.pallas_call(
        body, name=name, grid=(m_ // tm, n_ // tn, nk),
        in_specs=[a_spec, b_spec] + [s for _, s in a_extra] + [s for _, s in epi_extra],
        out_specs=out_specs, out_shape=[_sds(s, d) for s, d in outs],
        scratch_shapes=[pltpu.VMEM((tm, tn), F32)] if nk > 1 else [],
        compiler_params=_params(("parallel", "parallel", "arbitrary")),
    )(a, b, *[x for x, _ in a_extra], *[x for x, _ in epi_extra])


def _prenorm(name, x, ctx, gain, shsc):
    n_lat = x.shape[0] // ROW_BLOCK
    n_ctx = 0 if ctx is None else ctx.shape[0] // ROW_BLOCK
    d = x.shape[1]

    def norm(src, g_ref, m_ref, o_ref):
        xv = src[...]
        xh = xv * lax.rsqrt(jnp.mean(xv * xv, axis=-1, keepdims=True) + EPS_RMS)
        o_ref[...] = ((xh * g_ref[...]) * (1.0 + m_ref[1:2, :]) + m_ref[0:1, :]).astype(o_ref.dtype)

    def body(*refs):
        if ctx is None:
            x_ref, g_ref, m_ref, o_ref = refs
            norm(x_ref, g_ref, m_ref, o_ref)
        else:
            x_ref, c_ref, g_ref, m_ref, o_ref = refs
            i = pl.program_id(0)

            @pl.when(i < n_lat)
            def _():
                norm(x_ref, g_ref, m_ref, o_ref)

            @pl.when(i >= n_lat)
            def _():
                norm(c_ref, g_ref, m_ref, o_ref)

    in_specs = [pl.BlockSpec((ROW_BLOCK, d), lambda i: (jnp.minimum(i, n_lat - 1), 0))]
    args = [x]
    if ctx is not None:
        in_specs.append(pl.BlockSpec((ROW_BLOCK, d), lambda i: (jnp.maximum(i - n_lat, 0), 0)))
        args.append(ctx)
    in_specs += [pl.BlockSpec((1, d), lambda i: (0, 0)),
                 pl.BlockSpec((None, 2, d), lambda i: (jnp.minimum(i // n_lat, 1), 0, 0))]
    args += [gain, shsc]
    return pl.pallas_call(
        body, name=name, grid=(n_lat + n_ctx,), in_specs=in_specs,
        out_specs=pl.BlockSpec((ROW_BLOCK, d), lambda i: (i, 0)),
        out_shape=_sds(((n_lat + n_ctx) * ROW_BLOCK, d), BF16),
        compiler_params=_params(("parallel",)),
    )(*args)


def _norm_bwd(name, x, d_act, d_act_row0, gain, scale, res=None, aux=None, gate=None):
    rows, d = x.shape
    nb = rows // ROW_BLOCK
    has_res = res is not None
    has_gate = gate is not None

    def body(*refs):
        if has_gate:
            x_ref, da_ref, g_ref, sc_ref, r_ref, aux_ref, gate_ref, dx_ref, dm_ref, sums = refs
        elif has_res:
            x_ref, da_ref, g_ref, sc_ref, r_ref, aux_ref, dx_ref, sums = refs
        else:
            x_ref, da_ref, g_ref, sc_ref, sums = refs
        i = pl.program_id(0)

        @pl.when(i == 0)
        def _():
            sums[...] = jnp.zeros_like(sums)

        xv, da = x_ref[...], da_ref[...]
        rstd = lax.rsqrt(jnp.mean(xv * xv, axis=-1, keepdims=True) + EPS_RMS)
        xh = xv * rstd
        g = g_ref[...]
        dn = da * (1.0 + sc_ref[...])
        sums[0] += _fold8(da)
        sums[1] += _fold8(da * (xh * g))
        sums[2] += _fold8(dn * xh)
        if has_res:
            dxh = dn * g
            dx = rstd * (dxh - xh * jnp.mean(dxh * xh, axis=-1, keepdims=True))
            rv = r_ref[...]
            dx_ref[...] = rv + dx
            sums[3] += _fold8(rv * aux_ref[...])
            if has_gate:
                dm_ref[...] = ((rv + dx) * gate_ref[...]).astype(dm_ref.dtype)

    row = lambda i: (i, 0)
    vec = pl.BlockSpec((1, d), lambda i: (0, 0))
    in_specs = [pl.BlockSpec((ROW_BLOCK, d), row), pl.BlockSpec((ROW_BLOCK, d), lambda i: (i + d_act_row0, 0)), vec, vec]
    args = [x, d_act, gain, scale]
    out_shape = [_sds((4, SUBLANES, d))]
    out_specs = [pl.BlockSpec((4, SUBLANES, d), lambda i: (0, 0, 0))]
    if has_res:
        in_specs += [pl.BlockSpec((ROW_BLOCK, d), row), pl.BlockSpec((ROW_BLOCK, d), row)]
        args += [res, aux]
        if has_gate:
            in_specs.append(vec)
            args.append(gate)
            out_shape = [_sds((rows, d), BF16)] + out_shape
            out_specs = [pl.BlockSpec((ROW_BLOCK, d), row)] + out_specs
        out_shape = [_sds((rows, d))] + out_shape
        out_specs = [pl.BlockSpec((ROW_BLOCK, d), row)] + out_specs
    return pl.pallas_call(
        body, name=name, grid=(nb,), in_specs=in_specs, out_specs=out_specs, out_shape=out_shape,
        compiler_params=_params(("arbitrary",)),
    )(*args)


def _loss_head(h2, target, gain, gate):
    rows, d = h2.shape

    def body(h_ref, t_ref, g_ref, gate_ref, dh_ref, dm_ref, err_ref, dg_ref):
        i = pl.program_id(0)

        @pl.when(i == 0)
        def _():
            err_ref[...] = jnp.zeros_like(err_ref)
            dg_ref[...] = jnp.zeros_like(dg_ref)

        hv = h_ref[...]
        rstd = lax.rsqrt(jnp.mean(hv * hv, axis=-1, keepdims=True) + EPS_RMS)
        xh = hv * rstd
        g = g_ref[...]
        err = xh * g - t_ref[...]
        err_ref[...] += _fold8(err * err)
        dy = err * (1.0 / d)
        dg_ref[...] += _fold8(dy * xh)
        dxh = dy * g
        dh = rstd * (dxh - xh * jnp.mean(dxh * xh, axis=-1, keepdims=True))
        dh_ref[...] = dh
        dm_ref[...] = (dh * gate_ref[...]).astype(dm_ref.dtype)

    row = pl.BlockSpec((ROW_BLOCK, d), lambda i: (i, 0))
    acc = pl.BlockSpec((SUBLANES, d), lambda i: (0, 0))
    vec = pl.BlockSpec((1, d), lambda i: (0, 0))
    return pl.pallas_call(
        body, name="loss_head", grid=(rows // ROW_BLOCK,),
        in_specs=[row, row, vec, vec], out_specs=[row, row, acc, acc],
        out_shape=[_sds((rows, d)), _sds((rows, d), BF16), _sds((SUBLANES, d)), _sds((SUBLANES, d))],
        compiler_params=_params(("arbitrary",)),
    )(h2, target, gain, gate)


def _ada_fwd(cond16, ada_w_loc, ada_b_loc):
    cols = ada_w_loc.shape[1]

    def body(c_ref, w_ref, b_ref, o_ref):
        s = _silu(c_ref[...]).astype(BF16)
        o_ref[...] = _dot(s, w_ref[...].astype(BF16), "nn") + b_ref[...]

    return pl.pallas_call(body, name="ada_fwd", out_shape=_sds((16, cols)), compiler_params=_params())(
        cond16, ada_w_loc, ada_b_loc)


def _ada_bwd(cond16, dmod16, ada_w_loc, c_ctx_row):
    k_, cols = ada_w_loc.shape

    def body(c_ref, dm_ref, w_ref, cc_ref, gw_ref, gc_ref):
        s = _silu(c_ref[...]).astype(BF16)
        dm = dm_ref[...]
        gw_ref[...] = _dot(s, dm.astype(BF16), "tn")
        dmc = jnp.sum(dm[8:16, :], axis=0, keepdims=True)
        dmc8 = jnp.broadcast_to(dmc, (SUBLANES, cols)).astype(BF16)
        ds = _dot(dmc8, w_ref[...].astype(BF16), "nt")
        row = lax.broadcasted_iota(jnp.int32, ds.shape, 0)
        gc_ref[...] = jnp.where(row == 0, ds * _dsilu(cc_ref[...]), 0.0)

    return pl.pallas_call(body, name="ada_bwd", out_shape=[_sds((k_, cols)), _sds((SUBLANES, k_))],
                          compiler_params=_params())(cond16, dmod16, ada_w_loc, c_ctx_row)


def _cmul(a, b):
    return a[0] * b[0] - a[1] * b[1], a[0] * b[1] + a[1] * b[0]


def _disc(lam_re, lam_im, ldt):
    dt = jnp.exp(ldt)
    mag = jnp.exp(lam_re * dt)
    th = lam_im * dt
    a_re, a_im = mag * jnp.cos(th), mag * jnp.sin(th)
    den = lam_re * lam_re + lam_im * lam_im
    n_re = a_re - 1.0
    f_re = (n_re * lam_re + a_im * lam_im) / den
    f_im = (a_im * lam_re - n_re * lam_im) / den
    return dt, mag, th, a_re, a_im, den, n_re, f_re, f_im


def _block_diag_mask(shape):
    row = lax.broadcasted_iota(jnp.int32, shape, 0)
    col = lax.broadcasted_iota(jnp.int32, shape, 1)
    return lax.shift_right_logical(row, 4) == lax.shift_right_logical(col, 6)


TAB_A = 0
TAB_BIG = 1
TAB_SEG = 4
TAB_PW = 5
TAB_ROWS = TAB_PW + STEPS


def _s5_discretise(name, ascending, lam_re, lam_im, ldt, bt_re, bt_im, ct_re, ct_im):
    def write_tables(ref, pw, big, asc, sign):
        row = lax.broadcasted_iota(jnp.int32, (SUBLANES, NSTATE), 0)
        full = lambda v: jnp.broadcast_to(v, (SUBLANES, NSTATE))

        def put(t, p):
            ref[0, t] = full(p[0])
            ref[1, t] = full(sign * p[1])

        put(TAB_A, pw[0])
        for t in range(3):
            put(TAB_BIG + t, big[t])
        seg = [big[0]]
        for _ in range(SEGMENTS - 1):
            seg.append(_cmul(seg[-1], big[0]))
        seg_re = jnp.zeros((SUBLANES, NSTATE), F32)
        seg_im = jnp.zeros((SUBLANES, NSTATE), F32)
        for r in range(SEGMENTS):
            p = seg[r] if asc else seg[SEGMENTS - 1 - r]
            seg_re = jnp.where(row == r, p[0], seg_re)
            seg_im = jnp.where(row == r, sign * p[1], seg_im)
        ref[0, TAB_SEG] = seg_re
        ref[1, TAB_SEG] = seg_im
        for k in range(STEPS):
            put(TAB_PW + k, pw[k])

    def body(lr_ref, li_ref, ldt_ref, br_ref, bi_ref, cr_ref, ci_ref, bb_ref, tab_ref, adj_ref, bm_ref, cm_ref):
        _, _, _, a_re, a_im, _, _, f_re, f_im = _disc(lr_ref[...], li_ref[...], ldt_ref[...])
        bre, bim = br_ref[...], bi_ref[...]
        bb_re = f_re * bre - f_im * bim
        bb_im = f_re * bim + f_im * bre
        bb_ref[0:S5_GROUP, :] = bb_re
        bb_ref[S5_GROUP:2 * S5_GROUP, :] = bb_im
        pw = [(a_re, a_im)]
        for _ in range(STEPS - 1):
            pw.append(_cmul(pw[-1], (a_re, a_im)))
        big = [pw[STEPS - 1]]
        for _ in range(2):
            big.append(_cmul(big[-1], big[-1]))
        write_tables(tab_ref, pw, big, ascending, 1.0)
        write_tables(adj_ref, pw, big, not ascending, -1.0)
        half = NSTATE // S5_BLOCKS
        mask = _block_diag_mask((S5_BLOCK_WIDTH, half))
        tile = lambda v: jnp.broadcast_to(v[None], (S5_BLOCK_WIDTH // S5_GROUP, S5_GROUP, half)).reshape(S5_BLOCK_WIDTH, half)
        for c in range(S5_BLOCKS):
            cols = slice(c * half, (c + 1) * half)
            rows = slice(c * S5_BLOCK_WIDTH, (c + 1) * S5_BLOCK_WIDTH)
            bm_ref[c, :, 0:half] = jnp.where(mask, tile(bb_re[:, cols]), 0.0).astype(BF16)
            bm_ref[c, :, half:2 * half] = jnp.where(mask, tile(bb_im[:, cols]), 0.0).astype(BF16)
            cm_ref[c, :, 0:half] = jnp.where(mask, cr_ref[rows, :], 0.0).astype(BF16)
            cm_ref[c, :, half:2 * half] = jnp.where(mask, -ci_ref[rows, :], 0.0).astype(BF16)

    blocked = _sds((S5_BLOCKS, S5_BLOCK_WIDTH, 2 * NSTATE // S5_BLOCKS), BF16)
    return pl.pallas_call(
        body, name=name,
        out_shape=[_sds((2 * S5_GROUP, NSTATE)), _sds((2, TAB_ROWS, SUBLANES, NSTATE)),
                   _sds((2, TAB_ROWS, SUBLANES, NSTATE)), blocked, blocked],
        compiler_params=_params(),
    )(lam_re, lam_im, ldt, bt_re, bt_im, ct_re, ct_im)


def _s5_discretise_bwd(name, lam_re, lam_im, ldt, bt_re, bt_im, d_abar8, d_bbar):
    def body(lr_ref, li_ref, ldt_ref, br_ref, bi_ref, da_ref, db_ref, dl_ref, dbt_ref):
        lam_re, lam_im = lr_ref[...], li_ref[...]
        dt, mag, _, a_re, a_im, den, n_re, f_re, f_im = _disc(lam_re, lam_im, ldt_ref[...])
        bre, bim = br_ref[...], bi_ref[...]
        dbr, dbi = db_ref[0:S5_GROUP, :], db_ref[S5_GROUP:2 * S5_GROUP, :]
        dbt_ref[0:S5_GROUP, :] = f_re * dbr + f_im * dbi
        dbt_ref[S5_GROUP:2 * S5_GROUP, :] = f_re * dbi - f_im * dbr
        df_re = jnp.sum(bre * dbr + bim * dbi, axis=0, keepdims=True)
        df_im = jnp.sum(bre * dbi - bim * dbr, axis=0, keepdims=True)
        da = da_ref[...]
        da_re = jnp.sum(da[:, 0:NSTATE], axis=0, keepdims=True)
        da_im = jnp.sum(da[:, NSTATE:2 * NSTATE], axis=0, keepdims=True)
        da_re = da_re + (df_re * lam_re - df_im * lam_im) / den
        da_im = da_im + (df_re * lam_im + df_im * lam_re) / den
        ff = (f_re * df_re + f_im * df_im) * 2.0 / den
        d_lr = (df_re * n_re + df_im * a_im) / den - ff * lam_re
        d_li = (df_re * a_im - df_im * n_re) / den - ff * lam_im
        d_mag = (da_re * a_re + da_im * a_im) / mag
        d_th = da_im * a_re - da_re * a_im
        d_lr = d_lr + d_mag * mag * dt
        d_li = d_li + d_th * dt
        d_ldt = (d_mag * mag * lam_re + d_th * lam_im) * dt
        row = lax.broadcasted_iota(jnp.int32, (SUBLANES, NSTATE), 0)
        dl_ref[...] = jnp.where(row == 0, d_lr, jnp.where(row == 1, d_li, jnp.where(row == 2, d_ldt, 0.0)))

    return pl.pallas_call(
        body, name=name, out_shape=[_sds((SUBLANES, NSTATE)), _sds((2 * S5_GROUP, NSTATE))],
        compiler_params=_params(),
    )(lam_re, lam_im, ldt, bt_re, bt_im, d_abar8, d_bbar)


def _segment_permutation():
    rho = jnp.arange(ROW_BLOCK)
    src = STEPS * (rho % SEGMENTS) + rho // SEGMENTS
    return (src[:, None] == jnp.arange(ROW_BLOCK)[None, :]).astype(BF16)


def _permute_rows(perm_ref, v):
    return _dot(perm_ref[...], v, "nn").astype(BF16)


def _unpermute_rows(perm_t_ref, v):
    hi = v.astype(BF16)
    lo = (v - hi.astype(F32)).astype(BF16)
    return _dot(perm_t_ref[...], hi, "nn") + _dot(perm_t_ref[...], lo, "nn")


def _unrolled_loop(step, init):
    def trip(o, state):
        for u in range(SCAN_UNROLL):
            state = step(o * SCAN_UNROLL + u, state)
        return state

    return lax.fori_loop(0, STEPS // SCAN_UNROLL, trip, init)


def _scan_chunk(x_ref, out_ref, tab_ref, carry_re, carry_im, ascending, pair_ref=None, acc_ref=None):
    w = SCAN_LANES
    half = NSTATE // S5_BLOCKS
    row = lax.broadcasted_iota(jnp.int32, (SUBLANES, w), 0)
    last = (SEGMENTS - 1) if ascending else 0

    def from_previous_segment(v, k, fill):
        if ascending:
            return jnp.where(row >= k, pltpu.roll(v, k, 0), fill)
        return jnp.where(row < SEGMENTS - k, pltpu.roll(v, SEGMENTS - k, 0), fill)

    def tile_rows(k):
        return pl.ds(pl.multiple_of((k if ascending else STEPS - 1 - k) * SUBLANES, SUBLANES), SUBLANES)

    for j in range(NSTATE // w):
        n_l = pl.ds(j * w, w)
        lane0 = (j * w // half) * 2 * half + (j * w) % half
        re_l, im_l = pl.ds(lane0, w), pl.ds(lane0 + half, w)
        tab = lambda t, n_l=n_l: (tab_ref[0, t, :, n_l], tab_ref[1, t, :, n_l])
        a_re, a_im = tab(TAB_A)

        def local_step(k, h):
            rs = tile_rows(k)
            h_re = a_re * h[0] - a_im * h[1] + x_ref[rs, re_l]
            h_im = a_re * h[1] + a_im * h[0] + x_ref[rs, im_l]
            out_ref[rs, re_l] = h_re
            out_ref[rs, im_l] = h_im
            return h_re, h_im

        zero = jnp.zeros((SUBLANES, w), F32)
        end_re, end_im = lax.fori_loop(0, STEPS, local_step, (zero, zero))
        for t, k in ((TAB_BIG, 1), (TAB_BIG + 1, 2), (TAB_BIG + 2, 4)):
            p_re, p_im = tab(t)
            s_re, s_im = from_previous_segment(end_re, k, 0.0), from_previous_segment(end_im, k, 0.0)
            end_re, end_im = end_re + (p_re * s_re - p_im * s_im), end_im + (p_re * s_im + p_im * s_re)
        c0_re, c0_im = carry_re[:, n_l], carry_im[:, n_l]
        p_re, p_im = tab(TAB_SEG)
        end_re = end_re + (p_re * c0_re - p_im * c0_im)
        end_im = end_im + (p_re * c0_im + p_im * c0_re)
        carry_re[:, n_l] = jnp.broadcast_to(end_re[last:last + 1, :], end_re.shape)
        carry_im[:, n_l] = jnp.broadcast_to(end_im[last:last + 1, :], end_im.shape)
        in_re = from_previous_segment(end_re, 1, c0_re)
        in_im = from_previous_segment(end_im, 1, c0_im)

        def carry_step(k, st):
            rs = tile_rows(k)
            p_re, p_im = tab_ref[0, TAB_PW + k, :, n_l], tab_ref[1, TAB_PW + k, :, n_l]
            o_re = out_ref[rs, re_l] + (p_re * in_re - p_im * in_im)
            o_im = out_ref[rs, im_l] + (p_re * in_im + p_im * in_re)
            out_ref[rs, re_l] = o_re
            out_ref[rs, im_l] = o_im
            if pair_ref is None:
                return st
            s_re, s_im = pair_ref[rs, re_l], pair_ref[rs, im_l]
            return (o_re, o_im, st[2] + (st[0] * s_re + st[1] * s_im), st[3] + (st[1] * s_re - st[0] * s_im))

        if pair_ref is None:
            _unrolled_loop(carry_step, 0)
        else:
            fin = _unrolled_loop(carry_step, (in_re, in_im, zero, zero))
            acc_ref[:, n_l] += fin[2]
            acc_ref[:, pl.ds(NSTATE + j * w, w)] += fin[3]


def _scan_block_index(i, n_lat, ctx_first_then_ascending):
    if ctx_first_then_ascending:
        return jnp.where(i == 0, n_lat, i - 1)
    return jnp.where(i == 0, n_lat, n_lat - i)


def _full_spec(shape):
    return pl.BlockSpec(shape, lambda i: (0,) * len(shape))


_S5_BLOCKED = (S5_BLOCKS, S5_BLOCK_WIDTH, 2 * NSTATE // S5_BLOCKS)
_S5_TABLES = (2, TAB_ROWS, SUBLANES, NSTATE)
_S5_DIAG = (S5_BLOCKS, S5_GROUP, 2 * NSTATE // S5_BLOCKS)


def _s5_scan_fwd(name, ascending, z_all, bmat, cmat, tab, perm, perm_t):
    rows = z_all.shape[0]
    nb = rows // ROW_BLOCK
    n_lat = nb - 1
    bw, sw = S5_BLOCK_WIDTH, 2 * NSTATE // S5_BLOCKS

    def body(u_ref, bm_ref, cm_ref, tab_ref, p_ref, pt_ref, s_ref, y_ref, bu, yp, carry_re, carry_im):
        @pl.when(pl.program_id(0) == 0)
        def _():
            carry_re[...] = jnp.zeros_like(carry_re)
            carry_im[...] = jnp.zeros_like(carry_im)

        up = _permute_rows(p_ref, u_ref[...].astype(BF16))
        for c in range(S5_BLOCKS):
            bu[:, c * sw:(c + 1) * sw] = _dot(up[:, c * bw:(c + 1) * bw], bm_ref[c], "nn")
        _scan_chunk(bu, s_ref, tab_ref, carry_re, carry_im, ascending)
        for c in range(S5_BLOCKS):
            yp[:, c * bw:(c + 1) * bw] = _dot(s_ref[:, c * sw:(c + 1) * sw].astype(BF16), cm_ref[c], "nt")
        y_ref[...] = _unpermute_rows(pt_ref, yp[...])

    blk = lambda i: (_scan_block_index(i, n_lat, ascending), 0)
    return pl.pallas_call(
        body, name=name, grid=(nb,),
        in_specs=[pl.BlockSpec((ROW_BLOCK, S5_WIDTH), blk), _full_spec(_S5_BLOCKED), _full_spec(_S5_BLOCKED),
                  _full_spec(_S5_TABLES), _full_spec((ROW_BLOCK, ROW_BLOCK)), _full_spec((ROW_BLOCK, ROW_BLOCK))],
        out_specs=[pl.BlockSpec((ROW_BLOCK, 2 * NSTATE), blk), pl.BlockSpec((ROW_BLOCK, S5_WIDTH), blk)],
        out_shape=[_sds((rows, 2 * NSTATE)), _sds((rows, S5_WIDTH))],
        scratch_shapes=[pltpu.VMEM((ROW_BLOCK, 2 * NSTATE), F32), pltpu.VMEM((ROW_BLOCK, S5_WIDTH), F32),
                        pltpu.VMEM((SUBLANES, NSTATE), F32), pltpu.VMEM((SUBLANES, NSTATE), F32)],
        compiler_params=_params(("arbitrary",)),
    )(z_all, bmat, cmat, tab, perm, perm_t)


def _s5_scan_bwd(name, ascending, dy, z_all, states, bmat, cmat, adj, perm, perm_t):
    rows = states.shape[0]
    nb = rows // ROW_BLOCK
    n_lat = nb - 1
    bw, sw = S5_BLOCK_WIDTH, 2 * NSTATE // S5_BLOCKS

    def block_index(i):
        if ascending:
            return jnp.where(i == nb - 1, n_lat, n_lat - 1 - i)
        return jnp.where(i == nb - 1, n_lat, i)

    def body(dy_ref, u_ref, s_ref, bm_ref, cm_ref, adj_ref, p_ref, pt_ref, du_ref, db_ref, dc_ref, da_ref,
             g, dup, db_acc, dc_acc, carry_re, carry_im):
        i = pl.program_id(0)

        @pl.when(i == 0)
        def _():
            carry_re[...] = jnp.zeros_like(carry_re)
            carry_im[...] = jnp.zeros_like(carry_im)
            da_ref[...] = jnp.zeros_like(da_ref)
            db_acc[...] = jnp.zeros_like(db_acc)
            dc_acc[...] = jnp.zeros_like(dc_acc)

        @pl.when(i < nb - 1)
        def _():
            dyp = _permute_rows(p_ref, dy_ref[...].astype(BF16))
            for c in range(S5_BLOCKS):
                g[:, c * sw:(c + 1) * sw] = _dot(dyp[:, c * bw:(c + 1) * bw], cm_ref[c], "nn")
                dc_acc[c] += _dot(dyp[:, c * bw:(c + 1) * bw], s_ref[:, c * sw:(c + 1) * sw].astype(BF16), "tn")

        @pl.when(i == nb - 1)
        def _():
            g[...] = jnp.zeros_like(g)

        _scan_chunk(g, g, adj_ref, carry_re, carry_im, not ascending, pair_ref=s_ref, acc_ref=da_ref)
        up = _permute_rows(p_ref, u_ref[...].astype(BF16))
        for c in range(S5_BLOCKS):
            gc = g[:, c * sw:(c + 1) * sw].astype(BF16)
            dup[:, c * bw:(c + 1) * bw] = _dot(gc, bm_ref[c], "nt")
            db_acc[c] += _dot(up[:, c * bw:(c + 1) * bw], gc, "tn")
        du_ref[...] = _unpermute_rows(pt_ref, dup[...])

        @pl.when(i == nb - 1)
        def _():
            mask = _block_diag_mask((bw, sw // 2))
            for acc, out in ((db_acc, db_ref), (dc_acc, dc_ref)):
                for c in range(S5_BLOCKS):
                    for part in range(2):
                        cols = slice(part * (sw // 2), (part + 1) * (sw // 2))
                        kept = jnp.where(mask, acc[c, :, cols], 0.0)
                        out[c, :, cols] = kept.reshape(bw // S5_GROUP, S5_GROUP, sw // 2).sum(axis=0)

    blk = lambda i: (block_index(i), 0)
    return pl.pallas_call(
        body, name=name, grid=(nb,),
        in_specs=[pl.BlockSpec((ROW_BLOCK, S5_WIDTH), lambda i: (jnp.minimum(block_index(i), n_lat - 1), 0)),
                  pl.BlockSpec((ROW_BLOCK, S5_WIDTH), blk), pl.BlockSpec((ROW_BLOCK, 2 * NSTATE), blk),
                  _full_spec(_S5_BLOCKED), _full_spec(_S5_BLOCKED), _full_spec(_S5_TABLES),
                  _full_spec((ROW_BLOCK, ROW_BLOCK)), _full_spec((ROW_BLOCK, ROW_BLOCK))],
        out_specs=[pl.BlockSpec((ROW_BLOCK, S5_WIDTH), blk), _full_spec(_S5_DIAG), _full_spec(_S5_DIAG),
                   _full_spec((SUBLANES, 2 * NSTATE))],
        out_shape=[_sds((rows, S5_WIDTH)), _sds(_S5_DIAG), _sds(_S5_DIAG), _sds((SUBLANES, 2 * NSTATE))],
        scratch_shapes=[pltpu.VMEM((ROW_BLOCK, 2 * NSTATE), F32), pltpu.VMEM((ROW_BLOCK, S5_WIDTH), F32),
                        pltpu.VMEM(_S5_BLOCKED, F32), pltpu.VMEM(_S5_BLOCKED, F32),
                        pltpu.VMEM((SUBLANES, NSTATE), F32), pltpu.VMEM((SUBLANES, NSTATE), F32)],
        compiler_params=_params(("arbitrary",)),
    )(dy, z_all, states, bmat, cmat, adj, perm, perm_t)


def _glu_fwd(z_all, y0, y1, d_skip, w_glu, n_rows):
    def body(u_ref, y0_ref, y1_ref, d_ref, w_ref, o_ref):
        y = d_ref[...] * u_ref[...] + y0_ref[...] + y1_ref[...]
        g = _gelu(y)
        t = _dot(g.astype(BF16), w_ref[...], "nn")
        o_ref[...] = (g * _sigmoid(t)).astype(o_ref.dtype)

    row = pl.BlockSpec((ROW_BLOCK, S5_WIDTH), lambda i: (i, 0))
    return pl.pallas_call(
        body, name="glu_fwd", grid=(n_rows // ROW_BLOCK,),
        in_specs=[row, row, row, pl.BlockSpec((1, S5_WIDTH), lambda i: (0, 0)),
                  pl.BlockSpec((S5_WIDTH, S5_WIDTH), lambda i: (0, 0))],
        out_specs=row, out_shape=_sds((n_rows, S5_WIDTH + CONV_WIDTH), BF16), compiler_params=_params(("parallel",)),
    )(z_all, y0, y1, d_skip, w_glu)


def _glu_bwd(d_ycat, z_all, y0, y1, d_skip, w_glu, n_rows):
    def body(do_ref, u_ref, y0_ref, y1_ref, d_ref, w_ref, dy_ref, dw_ref, dd_ref):
        @pl.when(pl.program_id(0) == 0)
        def _():
            dw_ref[...] = jnp.zeros_like(dw_ref)
            dd_ref[...] = jnp.zeros_like(dd_ref)

        u = u_ref[...]
        y = d_ref[...] * u + y0_ref[...] + y1_ref[...]
        g = _gelu(y)
        gb = g.astype(BF16)
        w = w_ref[...]
        sg = _sigmoid(_dot(gb, w, "nn"))
        do = do_ref[...]
        dt = do * g * sg * (1.0 - sg)
        dtb = dt.astype(BF16)
        dg = do * sg + _dot(dtb, w, "nt")
        dy = dg * _dgelu(y)
        dy_ref[...] = dy
        dw_ref[...] += _dot(gb, dtb, "tn")
        dd_ref[...] += _fold8(dy * u)

    row = pl.BlockSpec((ROW_BLOCK, S5_WIDTH), lambda i: (i, 0))
    sq = pl.BlockSpec((S5_WIDTH, S5_WIDTH), lambda i: (0, 0))
    return pl.pallas_call(
        body, name="glu_bwd", grid=(n_rows // ROW_BLOCK,),
        in_specs=[row, row, row, row, pl.BlockSpec((1, S5_WIDTH), lambda i: (0, 0)), sq],
        out_specs=[row, sq, pl.BlockSpec((SUBLANES, S5_WIDTH), lambda i: (0, 0))],
        out_shape=[_sds((n_rows, S5_WIDTH)), _sds((S5_WIDTH, S5_WIDTH)), _sds((SUBLANES, S5_WIDTH))],
        compiler_params=_params(("arbitrary",)),
    )(d_ycat, z_all, y0, y1, d_skip, w_glu)


CONV_HALF = CONV_K // 2


def _conv_block(n_rows):
    blk = min(1024, n_rows)
    assert blk >= CONV_HALF * GRID_W and n_rows % blk == 0
    return blk


def _conv_gate(z_all, n_rows):
    blk = _conv_block(n_rows)
    nb = n_rows // blk

    def body(v_ref, g_ref, o_ref):
        i = pl.program_id(0)
        inside = jnp.logical_and(i >= 1, i <= nb)

        @pl.when(inside)
        def _():
            o_ref[...] = v_ref[...] * _sigmoid(g_ref[...])

        @pl.when(jnp.logical_not(inside))
        def _():
            o_ref[...] = jnp.zeros_like(o_ref)

    src = lambda col: pl.BlockSpec((blk, CONV_WIDTH), lambda i: (jnp.clip(i - 1, 0, nb - 1), col))
    return pl.pallas_call(
        body, name="conv_gate", grid=(nb + 2,), in_specs=[src(1), src(2)],
        out_specs=pl.BlockSpec((blk, CONV_WIDTH), lambda i: (i, 0)),
        out_shape=_sds(((nb + 2) * blk, CONV_WIDTH)), compiler_params=_params(("parallel",)),
    )(z_all, z_all)


def _stream_padded(pad_ref, buf, sems, blk, n_blocks):
    i = pl.program_id(0)

    def copy(b):
        rows = pl.ds(pl.multiple_of(b * blk, blk), blk)
        return pltpu.make_async_copy(pad_ref.at[rows, :], buf.at[rows, :], sems.at[b])

    @pl.when(i == 0)
    def _():
        for b in range(n_blocks):
            copy(b).start()
        copy(0).wait()
        copy(1).wait()

    copy(i + 2).wait()
    return pl.multiple_of(i * blk, blk)


def _conv_fwd(hh_pad, w, b, ln_g, ln_b, ycat, n_rows):
    blk = _conv_block(n_rows)
    nblk = n_rows // blk + 2

    def body(hh_ref, w_ref, b_ref, g_ref, lb_ref, ycat_ref, hc_ref, y_ref, win, sems):
        base = _stream_padded(hh_ref, win, sems, blk, nblk)

        def tile(t, _):
            r0 = pl.multiple_of(t * CONV_ROWS, CONV_ROWS)
            acc = jnp.zeros((CONV_ROWS, CONV_WIDTH), F32)
            for k in range(CONV_K):
                acc = acc + w_ref[k:k + 1, :] * win[pl.ds(base + r0 + blk + (k - CONV_HALF) * GRID_W, CONV_ROWS), :]
            hc = acc + b_ref[...]
            hc_ref[pl.ds(r0, CONV_ROWS), :] = hc
            mu = jnp.mean(hc, axis=-1, keepdims=True)
            xc = hc - mu
            ln = xc * lax.rsqrt(jnp.mean(xc * xc, axis=-1, keepdims=True) + EPS_LN) * g_ref[...] + lb_ref[...]
            y_ref[pl.ds(r0, CONV_ROWS), :] = _silu(ln).astype(y_ref.dtype)
            return 0

        lax.fori_loop(0, blk // CONV_ROWS, tile, 0)

    vec = pl.BlockSpec((1, CONV_WIDTH), lambda i: (0, 0))
    row = pl.BlockSpec((blk, CONV_WIDTH), lambda i: (i, 0))
    return pl.pallas_call(
        body, name="conv_fwd", grid=(n_rows // blk,),
        in_specs=[ANY, pl.BlockSpec((CONV_K, CONV_WIDTH), lambda i: (0, 0)), vec, vec, vec, ANY],
        out_specs=[row, pl.BlockSpec((blk, CONV_WIDTH), lambda i: (i, 1))],
        out_shape=[_sds((n_rows, CONV_WIDTH)), _sds(ycat.shape, ycat.dtype)], input_output_aliases={5: 1},
        scratch_shapes=[pltpu.VMEM((nblk * blk, CONV_WIDTH), F32), pltpu.SemaphoreType.DMA((nblk,))],
        compiler_params=_params(("arbitrary",)),
    )(hh_pad, w, b, ln_g, ln_b, ycat)


def _conv_bwd_norm(d_ycat, hc, ln_g, ln_b, n_rows):
    blk = _conv_block(n_rows)
    nb = n_rows // blk

    def body(dy_ref, hc_ref, g_ref, lb_ref, o_ref, sums):
        i = pl.program_id(0)

        @pl.when(i == 0)
        def _():
            sums[...] = jnp.zeros_like(sums)

        inside = jnp.logical_and(i >= 1, i <= nb)

        @pl.when(inside)
        def _():
            hcv = hc_ref[...]
            mu = jnp.mean(hcv, axis=-1, keepdims=True)
            xc = hcv - mu
            rstd = lax.rsqrt(jnp.mean(xc * xc, axis=-1, keepdims=True) + EPS_LN)
            xh = xc * rstd
            g = g_ref[...]
            dln = dy_ref[...] * _dsilu(xh * g + lb_ref[...])
            dxh = dln * g
            dhc = rstd * (dxh - jnp.mean(dxh, axis=-1, keepdims=True) - xh * jnp.mean(dxh * xh, axis=-1, keepdims=True))
            o_ref[...] = dhc
            sums[0] += _fold8(dhc)
            sums[1] += _fold8(dln * xh)
            sums[2] += _fold8(dln)

        @pl.when(jnp.logical_not(inside))
        def _():
            o_ref[...] = jnp.zeros_like(o_ref)

    vec = pl.BlockSpec((1, CONV_WIDTH), lambda i: (0, 0))
    return pl.pallas_call(
        body, name="conv_bwd_norm", grid=(nb + 2,),
        in_specs=[pl.BlockSpec((blk, CONV_WIDTH), lambda i: (jnp.clip(i - 1, 0, nb - 1), 1)),
                  pl.BlockSpec((blk, CONV_WIDTH), lambda i: (jnp.clip(i - 1, 0, nb - 1), 0)), vec, vec],
        out_specs=[pl.BlockSpec((blk, CONV_WIDTH), lambda i: (i, 0)),
                   pl.BlockSpec((3, SUBLANES, CONV_WIDTH), lambda i: (0, 0, 0))],
        out_shape=[_sds(((nb + 2) * blk, CONV_WIDTH)), _sds((3, SUBLANES, CONV_WIDTH))],
        compiler_params=_params(("arbitrary",)),
    )(d_ycat, hc, ln_g, ln_b)


def _conv_bwd_taps(dhc_pad, hh_pad, z_all, w, n_rows):
    blk = _conv_block(n_rows)
    nblk = n_rows // blk + 2

    def body(dhc_ref, hh_ref, v_ref, g_ref, w_ref, dv_ref, dg_ref, dw_ref, dwin, hwin, dsems, hsems):
        @pl.when(pl.program_id(0) == 0)
        def _():
            dw_ref[...] = jnp.zeros_like(dw_ref)

        base = _stream_padded(dhc_ref, dwin, dsems, blk, nblk)
        _stream_padded(hh_ref, hwin, hsems, blk, nblk)

        def tile(t, _):
            r0 = pl.multiple_of(t * CONV_ROWS, CONV_ROWS) + base
            dh = dwin[pl.ds(r0 + blk, CONV_ROWS), :]
            acc = jnp.zeros((CONV_ROWS, CONV_WIDTH), F32)
            for k in range(CONV_K):
                off = (k - CONV_HALF) * GRID_W
                acc = acc + w_ref[k:k + 1, :] * dwin[pl.ds(r0 + blk - off, CONV_ROWS), :]
                dw_ref[k] += _fold8(dh * hwin[pl.ds(r0 + blk + off, CONV_ROWS), :])
            rs = pl.ds(pl.multiple_of(t * CONV_ROWS, CONV_ROWS), CONV_ROWS)
            sg = _sigmoid(g_ref[rs, :])
            vv = v_ref[rs, :]
            dv_ref[rs, :] = acc * sg
            dg_ref[rs, :] = acc * vv * sg * (1.0 - sg)
            return 0

        lax.fori_loop(0, blk // CONV_ROWS, tile, 0)

    row = pl.BlockSpec((blk, CONV_WIDTH), lambda i: (i, 0))
    return pl.pallas_call(
        body, name="conv_bwd_taps", grid=(n_rows // blk,),
        in_specs=[ANY, ANY,
            pl.BlockSpec((blk, CONV_WIDTH), lambda i: (i, 1)), pl.BlockSpec((blk, CONV_WIDTH), lambda i: (i, 2)),
            pl.BlockSpec((CONV_K, CONV_WIDTH), lambda i: (0, 0))],
        out_specs=[row, row, pl.BlockSpec((CONV_K, SUBLANES, CONV_WIDTH), lambda i: (0, 0, 0))],
        out_shape=[_sds((n_rows, CONV_WIDTH)), _sds((n_rows, CONV_WIDTH)), _sds((CONV_K, SUBLANES, CONV_WIDTH))],
        scratch_shapes=[pltpu.VMEM((nblk * blk, CONV_WIDTH), F32), pltpu.VMEM((nblk * blk, CONV_WIDTH), F32),
                        pltpu.SemaphoreType.DMA((nblk,)), pltpu.SemaphoreType.DMA((nblk,))],
        compiler_params=_params(("arbitrary",)),
    )(dhc_pad, hh_pad, z_all, z_all, w)


def _dz_assemble(du0, du1, dy, d_skip, dv, dgate, n_lat):
    rows = du0.shape[0]
    nb = rows // ROW_BLOCK

    w = S5_WIDTH

    def body(a_ref, b_ref, dy_ref, d_ref, dv_ref, dg_ref, o_ref):
        lat = pl.program_id(0) < n_lat

        @pl.when(lat)
        def _():
            o_ref[:, 0:w] = (a_ref[...] + b_ref[...] + dy_ref[...] * d_ref[...]).astype(o_ref.dtype)
            o_ref[:, w:2 * w] = dv_ref[...].astype(o_ref.dtype)
            o_ref[:, 2 * w:3 * w] = dg_ref[...].astype(o_ref.dtype)

        @pl.when(jnp.logical_not(lat))
        def _():
            o_ref[:, 0:w] = (a_ref[...] + b_ref[...]).astype(o_ref.dtype)
            o_ref[:, w:3 * w] = jnp.zeros((ROW_BLOCK, 2 * w), o_ref.dtype)

    all_rows = pl.BlockSpec((ROW_BLOCK, w), lambda i: (i, 0))
    lat_rows = pl.BlockSpec((ROW_BLOCK, w), lambda i: (jnp.minimum(i, n_lat - 1), 0))
    return pl.pallas_call(
        body, name="dz_assemble", grid=(nb,),
        in_specs=[all_rows, all_rows, lat_rows, pl.BlockSpec((1, w), lambda i: (0, 0)), lat_rows, lat_rows],
        out_specs=pl.BlockSpec((ROW_BLOCK, IN_COLS), lambda i: (i, 0)),
        out_shape=_sds((rows, IN_COLS), BF16), compiler_params=_params(("parallel",)),
    )(du0, du1, dy, d_skip, dv, dgate)


def _sum_parts(parts):
    _, r, c = parts.shape

    def body(p_ref, o_ref):
        acc = p_ref[0]
        for q in range(1, NDEV):
            acc = acc + p_ref[q]
        o_ref[...] = acc

    return pl.pallas_call(body, name="sum_parts", out_shape=_sds((r, c)), compiler_params=_params())(parts)


def _row_tile(r, c):
    best = r
    for t in (1024, 512, 256, 128, 64, 32, 16, 8):
        if r % t == 0 and t * c <= 128 * 1024:
            return t
    return best


def _adamw(name, w, gparts, m, v):
    r, c = w.shape
    np_ = gparts.shape[0]
    tr = _row_tile(r, c)

    def body(w_ref, g_ref, m_ref, v_ref, go_ref, d_ref, mo_ref, vo_ref):
        g = g_ref[0].astype(F32)
        for q in range(1, np_):
            g = g + g_ref[q].astype(F32)
        m2 = ADAM_B1 * m_ref[...] + (1.0 - ADAM_B1) * g
        v2 = ADAM_B2 * v_ref[...] + (1.0 - ADAM_B2) * jnp.square(g)
        m_hat = m2 / (1.0 - ADAM_B1 ** ADAM_STEP)
        v_hat = v2 / (1.0 - ADAM_B2 ** ADAM_STEP)
        go_ref[...] = g
        d_ref[...] = -ADAM_LR * (m_hat / (jnp.sqrt(v_hat) + ADAM_EPS) + ADAM_WD * w_ref[...])
        mo_ref[...] = m2
        vo_ref[...] = v2

    row = pl.BlockSpec((tr, c), lambda i: (i, 0))
    return pl.pallas_call(
        body, name=name, grid=(r // tr,),
        in_specs=[row, pl.BlockSpec((np_, tr, c), lambda i: (0, i, 0)), row, row],
        out_specs=[row] * 4, out_shape=[_sds((r, c))] * 4, compiler_params=_params(("parallel",)),
    )(w, gparts, m, v)


def _adamw_native(name, w, g, m, v):
    def body(w_ref, g_ref, m_ref, v_ref, d_ref, mo_ref, vo_ref):
        gv = g_ref[...]
        m2 = ADAM_B1 * m_ref[...] + (1.0 - ADAM_B1) * gv
        v2 = ADAM_B2 * v_ref[...] + (1.0 - ADAM_B2) * jnp.square(gv)
        m_hat = m2 / (1.0 - ADAM_B1 ** ADAM_STEP)
        v_hat = v2 / (1.0 - ADAM_B2 ** ADAM_STEP)
        d_ref[...] = -ADAM_LR * (m_hat / (jnp.sqrt(v_hat) + ADAM_EPS) + ADAM_WD * w_ref[...])
        mo_ref[...] = m2
        vo_ref[...] = v2

    return pl.pallas_call(body, name=name, out_shape=[_sds(w.shape)] * 3, compiler_params=_params())(w, g, m, v)


SMALL = ["c_ctx", "ada_b", "norm1_g", "s5_lam_re", "s5_lam_im", "s5_log_dt", "s5_d", "conv_b", "conv_ln_g", "conv_ln_b",
         "norm2_g", "final_g"]
SMALL_PACKED_ROWS = 24


def _pack_rows(parts, rows):
    flat = jnp.concatenate([p.reshape(-1).astype(F32) for p in parts])
    return jnp.pad(flat, (0, rows * D_MODEL - flat.shape[0])).reshape(rows, D_MODEL)


def _unpack_rows(packed, shapes):
    flat = packed.reshape(-1)
    out, off = [], 0
    for shape in shapes:
        size = 1
        for s in shape:
            size *= s
        out.append(flat[off:off + size].reshape(shape))
        off += size
    return out


def kernel(x, c, ctx, c_ctx, ada_w, ada_b, norm1_g, w_in, s5_lam_re, s5_lam_im, s5_log_dt, s5_b_re, s5_b_im, s5_c_re, s5_c_im, s5_d, s5_w_glu, conv_w, conv_b, conv_ln_g, conv_ln_b, w_out, norm2_g, mlp_w1, mlp_w2, final_g, loss_target, m_c_ctx, m_ada_w, m_ada_b, m_norm1_g, m_w_in, m_s5_lam_re, m_s5_lam_im, m_s5_log_dt, m_s5_b_re, m_s5_b_im, m_s5_c_re, m_s5_c_im, m_s5_d, m_s5_w_glu, m_conv_w, m_conv_b, m_conv_ln_g, m_conv_ln_b, m_w_out, m_norm2_g, m_mlp_w1, m_mlp_w2, m_final_g, v_c_ctx, v_ada_w, v_ada_b, v_norm1_g, v_w_in, v_s5_lam_re, v_s5_lam_im, v_s5_log_dt, v_s5_b_re, v_s5_b_im, v_s5_c_re, v_s5_c_im, v_s5_d, v_s5_w_glu, v_conv_w, v_conv_b, v_conv_ln_g, v_conv_ln_b, v_w_out, v_norm2_g, v_mlp_w1, v_mlp_w2, v_final_g):
    weights = dict(c_ctx=c_ctx, ada_w=ada_w, ada_b=ada_b, norm1_g=norm1_g, w_in=w_in, s5_lam_re=s5_lam_re, s5_lam_im=s5_lam_im, s5_log_dt=s5_log_dt, s5_b_re=s5_b_re, s5_b_im=s5_b_im, s5_c_re=s5_c_re, s5_c_im=s5_c_im, s5_d=s5_d, s5_w_glu=s5_w_glu, conv_w=conv_w, conv_b=conv_b, conv_ln_g=conv_ln_g, conv_ln_b=conv_ln_b, w_out=w_out, norm2_g=norm2_g, mlp_w1=mlp_w1, mlp_w2=mlp_w2, final_g=final_g)
    mom1 = dict(c_ctx=m_c_ctx, ada_w=m_ada_w, ada_b=m_ada_b, norm1_g=m_norm1_g, w_in=m_w_in, s5_lam_re=m_s5_lam_re, s5_lam_im=m_s5_lam_im, s5_log_dt=m_s5_log_dt, s5_b_re=m_s5_b_re, s5_b_im=m_s5_b_im, s5_c_re=m_s5_c_re, s5_c_im=m_s5_c_im, s5_d=m_s5_d, s5_w_glu=m_s5_w_glu, conv_w=m_conv_w, conv_b=m_conv_b, conv_ln_g=m_conv_ln_g, conv_ln_b=m_conv_ln_b, w_out=m_w_out, norm2_g=m_norm2_g, mlp_w1=m_mlp_w1, mlp_w2=m_mlp_w2, final_g=m_final_g)
    mom2 = dict(c_ctx=v_c_ctx, ada_w=v_ada_w, ada_b=v_ada_b, norm1_g=v_norm1_g, w_in=v_w_in, s5_lam_re=v_s5_lam_re, s5_lam_im=v_s5_lam_im, s5_log_dt=v_s5_log_dt, s5_b_re=v_s5_b_re, s5_b_im=v_s5_b_im, s5_c_re=v_s5_c_re, s5_c_im=v_s5_c_im, s5_d=v_s5_d, s5_w_glu=v_s5_w_glu, conv_w=v_conv_w, conv_b=v_conv_b, conv_ln_g=v_conv_ln_g, conv_ln_b=v_conv_ln_b, w_out=v_w_out, norm2_g=v_norm2_g, mlp_w1=v_mlp_w1, mlp_w2=v_mlp_w2, final_g=v_final_g)
    order = list(weights)

    me = 4 * lax.axis_index("x") + 2 * lax.axis_index("y") + lax.axis_index("c")
    xs, cs, tgt = x[0], ctx[0], loss_target[0]
    n_lat_rows, n_ctx_rows = xs.shape[0], cs.shape[0]
    n_rows = n_lat_rows + n_ctx_rows
    n_lat = n_lat_rows // ROW_BLOCK
    ada_cols = ada_w.shape[2]

    (c_all,), _ = _exchange("gather_c", [c], [True])
    c_all = c_all.reshape(NDEV, D_MODEL)

    cond_fwd = jnp.concatenate([c_all, c_ctx[None], jnp.zeros((7, D_MODEL), F32)])
    ada_b_loc = lax.dynamic_slice(ada_b, (0, me * ada_cols), (1, ada_cols))
    (mod_g,), mod_token = _exchange("gather_mod", [_ada_fwd(cond_fwd, ada_w[0], ada_b_loc)], [True])
    wi_send, wi_recv, wi_src, wi_land, wi_token = _exchange_start(
        "gather_w_in_start", [w_in[0].astype(BF16) + mod_token[0:1, 0:1].astype(BF16)], [True])
    mixer_w = [s5_w_glu[0].astype(BF16), conv_w[0] + wi_token[0:1, 0:1], w_out[0].astype(BF16)]
    mixer_send, mixer_recv, mixer_src, mixer_land, mixer_token = _exchange_start("gather_mixer_start", mixer_w, [True] * 3)
    mlp_w = [mlp_w1[0].astype(BF16), mlp_w2[0].astype(BF16) + mixer_token[0:1, 0:1].astype(BF16)]
    mlpw_send, mlpw_recv, mlpw_src, mlpw_land, mlpw_token = _exchange_start("gather_mlp_start", mlp_w, [True] * 2)
    mod_rows = jnp.transpose(mod_g, (1, 0, 2)).reshape(16, 6 * D_MODEL) + mlpw_token[0:1, 0:1]
    mod = lax.dynamic_slice(mod_rows, (me, 0), (1, 6 * D_MODEL)).reshape(6, D_MODEL)
    modc = mod_rows[8, :2 * D_MODEL].reshape(2, D_MODEL)
    sh1, sc1, g1, sh2, sc2, g2 = [mod[i:i + 1] for i in range(6)]

    a_all = _prenorm("prenorm1", xs, cs, norm1_g, jnp.stack([mod[0:2], modc]))
    wi_own, wi_landed = _exchange_wait("gather_w_in_wait", wi_send, wi_recv, wi_src, wi_land, [True], a_all)
    w_in_full = jnp.transpose(_with_own(wi_landed[0], wi_own[0], me), (1, 0, 2)).reshape(D_MODEL, IN_COLS)
    tm_all = 1088 if n_rows % 1088 == 0 else ROW_BLOCK
    (z_all,) = _matmul("in_proj", a_all, w_in_full, "nn", (n_rows, IN_COLS, D_MODEL), (tm_all, IN_COLS, D_MODEL),
                       [((n_rows, IN_COLS), F32)])

    lam_re, lam_im = s5_lam_re[0].reshape(2, 1, NSTATE), s5_lam_im[0].reshape(2, 1, NSTATE)
    ldt = jnp.repeat(s5_log_dt[0], S5_STATE, axis=-1).reshape(2, 1, NSTATE)
    bt_re = jnp.transpose(s5_b_re[0], (0, 3, 1, 2)).reshape(2, S5_GROUP, NSTATE)
    bt_im = jnp.transpose(s5_b_im[0], (0, 3, 1, 2)).reshape(2, S5_GROUP, NSTATE)
    groups_per_block = S5_GROUPS // S5_BLOCKS
    ct_re = jnp.tile(s5_c_re[0].reshape(2, S5_WIDTH, S5_STATE), (1, 1, groups_per_block))
    ct_im = jnp.tile(s5_c_im[0].reshape(2, S5_WIDTH, S5_STATE), (1, 1, groups_per_block))
    d_skip = s5_d[0].reshape(1, S5_WIDTH)
    perm = _segment_permutation()
    perm_t = perm.T
    disc, states, y_dir = [], [], []
    for d in range(2):
        disc.append(_s5_discretise(f"s5_disc{d}", d == 0, lam_re[d], lam_im[d], ldt[d], bt_re[d], bt_im[d], ct_re[d], ct_im[d]))
        _, tab, _, bmat, cmat = disc[d]
        s, yd = _s5_scan_fwd(f"s5_scan_fwd{d}", d == 0, z_all, bmat, cmat, tab, perm, perm_t)
        states.append(s)
        y_dir.append(yd)
    mixer_own, mixer_landed = _exchange_wait("gather_mixer_wait", mixer_send, mixer_recv, mixer_src, mixer_land,
                                             [True] * 3, y_dir[1])
    glu_g, conv_w_g, w_out_g = [_with_own(l, o, me) for l, o in zip(mixer_landed, mixer_own)]
    glu_full = glu_g.reshape(S5_WIDTH, S5_WIDTH)
    conv_w_full = jnp.transpose(conv_w_g, (1, 0, 2)).reshape(CONV_K, CONV_WIDTH)
    w_out_full = w_out_g.reshape(D_MODEL, D_MODEL)
    ycat = _glu_fwd(z_all, y_dir[0], y_dir[1], d_skip, glu_full, n_lat_rows)

    hh_pad = _conv_gate(z_all, n_lat_rows)
    hc, ycat = _conv_fwd(hh_pad, conv_w_full, conv_b, conv_ln_g, conv_ln_b, ycat, n_lat_rows)

    tm = min(1024, n_lat_rows)
    tm_e = min(512, n_lat_rows)
    w1_cols = D_FF // NDEV
    row_vec = lambda tn: pl.BlockSpec((1, tn), lambda i, j, k: (0, j))
    out_tile = lambda t_m, t_n: pl.BlockSpec((t_m, t_n), lambda i, j, k: (i, j))
    full_rows = ((n_lat_rows, D_MODEL), F32)
    sums = ((n_lat_rows // tm_e, SUBLANES, D_MODEL), F32)
    sums_spec = pl.BlockSpec((None, SUBLANES, D_MODEL), lambda i, j, k: (i, 0, 0))
    vec = lambda v: (v, row_vec(D_MODEL))
    transposed_tile = lambda t_m, t_n: pl.BlockSpec((t_n, t_m), lambda i, j, k: (j, i))
    mix, h1, a2, a2_t = _matmul(
        "out_proj", ycat, w_out_full, "nn", (n_lat_rows, D_MODEL, D_MODEL), (tm_e, D_MODEL, D_MODEL),
        [full_rows, full_rows, ((n_lat_rows, D_MODEL), BF16), ((D_MODEL, n_lat_rows), BF16)],
        epi=_epi_residual_prenorm,
        epi_extra=[(xs, out_tile(tm_e, D_MODEL)), vec(g1), vec(norm2_g), vec(sc2), vec(sh2)],
        out_specs=[out_tile(tm_e, D_MODEL)] * 3 + [transposed_tile(tm_e, D_MODEL)])
    mlpw_own, mlpw_landed = _exchange_wait("gather_mlp_wait", mlpw_send, mlpw_recv, mlpw_src, mlpw_land, [True] * 2, a2)
    w1_g, w2_g = [_with_own(l, o, me) for l, o in zip(mlpw_landed, mlpw_own)]
    w2_full = w2_g.reshape(D_FF, D_MODEL)
    tm_up = min(2048, n_lat_rows)
    f, f_t = _matmul("mlp_up", a2, w1_g, "nn", (n_lat_rows, D_FF, D_MODEL), (tm_up, w1_cols, D_MODEL),
                     [((n_lat_rows, D_FF), BF16), ((D_FF, n_lat_rows), BF16)], epi=lambda acc: (acc, acc.T),
                     b_spec=pl.BlockSpec((None, D_MODEL, w1_cols), lambda i, j, k: (j, 0, 0)),
                     out_specs=[out_tile(tm_up, w1_cols), transposed_tile(tm_up, w1_cols)])
    sq_relu = lambda t: jnp.square(jnp.maximum(t, 0.0))
    mlp_out, d_h2, dm2, err_sums, d_final_g8 = _matmul(
        "mlp_down", f, w2_full, "nn", (n_lat_rows, D_MODEL, D_FF), (tm_e, D_MODEL, 1024),
        [full_rows, full_rows, ((n_lat_rows, D_MODEL), BF16), sums, sums], a_fn=sq_relu, epi=_epi_residual_loss,
        epi_extra=[(h1, out_tile(tm_e, D_MODEL)), vec(g2), (tgt, out_tile(tm_e, D_MODEL)), vec(final_g[None])],
        out_specs=[out_tile(tm_e, D_MODEL)] * 3 + [sums_spec] * 2)

    (d_f,) = _matmul("mlp_down_dx", dm2, w2_full, "nt", (n_lat_rows, D_FF, D_MODEL), (tm_up, 512, D_MODEL),
                     [((n_lat_rows, D_FF), BF16)],
                     epi=lambda acc, ft: (acc * 2.0 * jnp.maximum(ft.astype(F32), 0.0),),
                     epi_extra=[(f, out_tile(tm_up, 512))])
    tk_dw = min(2048, n_lat_rows)
    (g_w2,) = _matmul("mlp_down_dw", f_t, dm2, "nn", (D_FF, D_MODEL, n_lat_rows), (1024, D_MODEL, tk_dw),
                      [((D_FF, D_MODEL), F32)], a_fn=sq_relu)
    (g_w1,) = _matmul("mlp_up_dw", a2_t, d_f, "nn", (D_MODEL, D_FF, n_lat_rows), (D_MODEL, w1_cols, n_lat_rows),
                      [((NDEV, D_MODEL, w1_cols), F32)],
                      out_specs=[pl.BlockSpec((None, D_MODEL, w1_cols), lambda i, j, k: (j, 0, 0))])
    mlp_send, mlp_recv, mlp_src, mlp_land, mlp_token = _exchange_start(
        "scatter_mlp_start", [g_w1, g_w2.reshape(NDEV, D_FF // NDEV, D_MODEL)], [False] * 2)
    d_h1, dm1, *sums2 = _matmul(
        "mlp_up_dx", d_f, w1_g, "nt", (n_lat_rows, D_MODEL, D_FF), (tm_e, D_MODEL, w1_cols),
        [full_rows, ((n_lat_rows, D_MODEL), BF16)] + [sums] * 4, epi=_epi_norm_bwd,
        epi_extra=[(h1, out_tile(tm_e, D_MODEL)), (d_h2, out_tile(tm_e, D_MODEL)), (mlp_out, out_tile(tm_e, D_MODEL)),
                   vec(norm2_g), vec(sc2 + mlp_token[0:1, 0:1]), vec(g1)],
        b_spec=pl.BlockSpec((None, D_MODEL, w1_cols), lambda i, j, k: (k, 0, 0)),
        out_specs=[out_tile(tm_e, D_MODEL)] * 2 + [sums_spec] * 4)

    (d_ycat,) = _matmul("out_proj_dx", dm1, w_out_full, "nt", (n_lat_rows, D_MODEL, D_MODEL), (tm, D_MODEL, D_MODEL),
                        [((n_lat_rows, D_MODEL), F32)])
    (g_w_out,) = _matmul("out_proj_dw", ycat, dm1, "tn", (D_MODEL, D_MODEL, n_lat_rows), (D_MODEL, D_MODEL, 512),
                         [((D_MODEL, D_MODEL), F32)])

    dy, g_glu, dd8 = _glu_bwd(d_ycat, z_all, y_dir[0], y_dir[1], d_skip, glu_full, n_lat_rows)
    proj_send, proj_recv, proj_src, proj_land, proj_token = _exchange_start(
        "scatter_proj_start",
        [g_w_out.reshape(NDEV, D_MODEL // NDEV, D_MODEL), g_glu.reshape(NDEV, S5_WIDTH // NDEV, S5_WIDTH)], [False] * 2)
    perm = perm + proj_token[0:1, 0:1].astype(BF16)
    du, g_lam_re, g_lam_im, g_ldt, g_bt, g_cdiag = [], [], [], [], [], []
    for d in range(2):
        _, _, adj, bmat, cmat = disc[d]
        du_d, d_bdiag, d_cdiag, d_abar8 = _s5_scan_bwd(f"s5_scan_bwd{d}", d == 0, dy, z_all, states[d], bmat, cmat, adj,
                                                       perm, perm_t)
        du.append(du_d)
        d_bbar = jnp.transpose(d_bdiag.reshape(S5_BLOCKS, S5_GROUP, 2, NSTATE // S5_BLOCKS), (2, 1, 0, 3)).reshape(
            2 * S5_GROUP, NSTATE)
        d_lam8, d_bt = _s5_discretise_bwd(f"s5_disc_bwd{d}", lam_re[d], lam_im[d], ldt[d], bt_re[d], bt_im[d], d_abar8, d_bbar)
        g_lam_re.append(d_lam8[0].reshape(S5_GROUPS, S5_STATE))
        g_lam_im.append(d_lam8[1].reshape(S5_GROUPS, S5_STATE))
        g_ldt.append(d_lam8[2].reshape(S5_GROUPS, S5_STATE).sum(axis=-1))
        g_bt.append(d_bt)
        g_cdiag.append(d_cdiag)

    dhc_pad, conv_sums = _conv_bwd_norm(d_ycat, hc, conv_ln_g, conv_ln_b, n_lat_rows)
    d_v, d_gate, g_conv_w8 = _conv_bwd_taps(dhc_pad, hh_pad, z_all, conv_w_full, n_lat_rows)

    dz_all = _dz_assemble(du[0], du[1], dy, d_skip, d_v, d_gate, n_lat)
    (g_w_in_full,) = _matmul("in_proj_dw", a_all, dz_all, "tn", (D_MODEL, IN_COLS, n_rows), (D_MODEL, IN_COLS, tm_all),
                             [((D_MODEL, IN_COLS), F32)])
    g_w_in_parts = jnp.transpose(g_w_in_full.reshape(D_MODEL, NDEV, IN_COLS // NDEV), (1, 0, 2)).astype(BF16)
    win_send, win_recv, win_src, win_land, win_token = _exchange_start("scatter_w_in_start", [g_w_in_parts], [False])
    (d_a_all,) = _matmul("in_proj_dx", dz_all, w_in_full + win_token[0:1, 0:1].astype(BF16), "nt",
                         (n_rows, D_MODEL, IN_COLS), (tm_all, D_MODEL, IN_COLS), [((n_rows, D_MODEL), F32)])
    grad_x, sums1 = _norm_bwd("norm1_bwd", xs, d_a_all, 0, norm1_g, sc1, res=d_h1, aux=mix)
    (sums1c,) = _norm_bwd("norm1_bwd_ctx", cs, d_a_all, n_lat, norm1_g, modc[1:2])

    s1, s1c, s2 = sums1.sum(axis=1), sums1c.sum(axis=1), [p.sum(axis=(0, 1)) for p in sums2]
    d_mod = jnp.concatenate([s1[0], s1[1], s1[3], s2[0], s2[1], s2[3]])
    d_modc = jnp.concatenate([s1c[0], s1c[1], jnp.zeros((4 * D_MODEL,), F32)])
    (dmod_g,), _ = _exchange("gather_dmod", [jnp.stack([d_mod, d_modc])], [True])
    dmod16 = jnp.concatenate([dmod_g[:, 0], dmod_g[:, 1]])
    dmod16_loc = lax.dynamic_slice(dmod16, (0, me * ada_cols), (16, ada_cols))
    cond_bwd = jnp.concatenate([c_all, jnp.broadcast_to(c_ctx[None], (NDEV, D_MODEL))])
    g_ada_w, g_c_ctx8 = _ada_bwd(cond_bwd, dmod16_loc, ada_w[0], c_ctx[None])

    small_parts = dict(
        c_ctx=g_c_ctx8[0], ada_b=d_mod + d_modc, norm1_g=s1[2] + s1c[2],
        s5_lam_re=jnp.stack(g_lam_re), s5_lam_im=jnp.stack(g_lam_im), s5_log_dt=jnp.stack(g_ldt),
        s5_d=dd8.sum(axis=0), conv_b=conv_sums[0].sum(axis=0), conv_ln_g=conv_sums[1].sum(axis=0),
        conv_ln_b=conv_sums[2].sum(axis=0), norm2_g=s2[2], final_g=d_final_g8.sum(axis=(0, 1)))
    reduced_shapes = [(SMALL_PACKED_ROWS, D_MODEL), (2, 2 * S5_GROUP, NSTATE), (2,) + _S5_DIAG, (1,)]
    small_g = _pack_rows(
        [_pack_rows([small_parts[n] for n in SMALL], SMALL_PACKED_ROWS), jnp.stack(g_bt), jnp.stack(g_cdiag),
         (0.5 / D_MODEL * jnp.sum(err_sums)).reshape(1)], SMALL_ROWS).reshape(NDEV, SMALL_ROWS // NDEV, D_MODEL)
    g_conv_w_parts = jnp.transpose(g_conv_w8.sum(axis=1).reshape(CONV_K, NDEV, CONV_WIDTH // NDEV), (1, 0, 2))

    res = {}

    def own_chunk(src):
        return lax.dynamic_index_in_dim(src, me, 0, keepdims=False)

    def adamw_big(name, parts):
        outs = _adamw("adamw_" + name, weights[name][0], parts, mom1[name][0], mom2[name][0])
        res[name] = [o[None] for o in outs]
        return outs[0]

    sm_send, sm_recv, sm_src, sm_land, sm_token = _exchange_start("scatter_small_start", [g_conv_w_parts, small_g],
                                                                  [False] * 2)
    done = adamw_big("ada_w", g_ada_w[None] + sm_token[0:1, 0:1])
    mlp_src, mlp_landed = _exchange_wait("scatter_mlp_wait", mlp_send, mlp_recv, mlp_src, mlp_land, [False] * 2, done)
    p_w1, p_w2 = [_with_own(l, own_chunk(s), me) for l, s in zip(mlp_landed, mlp_src)]
    adamw_big("mlp_w1", p_w1)
    done = adamw_big("mlp_w2", p_w2)
    sm_src, sm_landed = _exchange_wait("scatter_small_wait", sm_send, sm_recv, sm_src, sm_land, [False] * 2, done)
    p_conv_w, p_small = [_with_own(l, own_chunk(s), me) for l, s in zip(sm_landed, sm_src)]
    ga_send, ga_recv, ga_src, ga_land, ga_token = _exchange_start("gather_small_start", [_sum_parts(p_small)], [True])
    proj_src, proj_landed = _exchange_wait("scatter_proj_wait", proj_send, proj_recv, proj_src, proj_land, [False] * 2,
                                           ga_token)
    p_w_out, p_glu = [_with_own(l, own_chunk(s), me) for l, s in zip(proj_landed, proj_src)]
    adamw_big("w_out", p_w_out)
    done = adamw_big("s5_w_glu", p_glu)
    win_src, win_landed = _exchange_wait("scatter_w_in_wait", win_send, win_recv, win_src, win_land, [False], done)
    adamw_big("w_in", _with_own(win_landed[0], own_chunk(win_src[0]), me))
    done = adamw_big("conv_w", p_conv_w)
    ga_own, ga_landed = _exchange_wait("gather_small_wait", ga_send, ga_recv, ga_src, ga_land, [True], done)
    small_all = _with_own(ga_landed[0], ga_own[0], me).reshape(1, SMALL_ROWS, D_MODEL)
    _, r_bt, r_cdiag, loss = _unpack_rows(small_all, reduced_shapes)
    loss = loss.reshape(())
    pack = lambda src: _pack_rows([src[n] for n in SMALL], SMALL_PACKED_ROWS)
    outs = _adamw("adamw_small", pack(weights), small_all, pack(mom1), pack(mom2))
    unpacked = [_unpack_rows(o, [weights[n].shape for n in SMALL]) for o in outs]
    for i, name in enumerate(SMALL):
        res[name] = [u[i] for u in unpacked]
    to_gph = lambda t: jnp.transpose(t.reshape(2, S5_GROUP, S5_GROUPS, S5_STATE), (0, 2, 3, 1))[None]
    r_c = jnp.transpose(r_cdiag.reshape(2, S5_BLOCKS, S5_GROUP, 2, groups_per_block, S5_STATE), (3, 0, 1, 4, 2, 5)).reshape(
        2, 1, 2, S5_GROUPS, S5_GROUP, S5_STATE)
    native = dict(s5_b_re=to_gph(r_bt[:, :S5_GROUP]), s5_b_im=to_gph(r_bt[:, S5_GROUP:]), s5_c_re=r_c[0], s5_c_im=-r_c[1])
    for name, grad in native.items():
        res[name] = [grad, *_adamw_native("adamw_" + name, weights[name], grad, mom1[name], mom2[name])]

    return (loss, grad_x[None], *[res[n][0] for n in order], *[res[n][1] for n in order],
            *[res[n][2] for n in order], *[res[n][3] for n in order])
```

```python
import functools

import jax
import jax.numpy as jnp
from jax import lax
from jax.experimental import pallas as pl
from jax.experimental.pallas import tpu as pltpu

F32 = jnp.float32
BF16 = jnp.bfloat16
MESH = pl.DeviceIdType.MESH
ANY = pl.BlockSpec(memory_space=pl.ANY)

NDEV = 8
D_MODEL = 1024
GRID_W = 64
S5_WIDTH = 512
S5_GROUP = 16
S5_GROUPS = 32
S5_STATE = 64
NSTATE = S5_GROUPS * S5_STATE
CONV_WIDTH = 512
CONV_K = 31
IN_COLS = S5_WIDTH + 2 * CONV_WIDTH
D_FF = 4 * D_MODEL
EPS_RMS = 1e-6
EPS_LN = 1e-5
ADAM_LR = 0.001
ADAM_B1 = 0.9
ADAM_B2 = 0.999
ADAM_EPS = 1e-08
ADAM_WD = 0.01
ADAM_STEP = 10

SUBLANES = 8
LANES = 128
ROW_BLOCK = 256
SCAN_LANES = 512
SCAN_UNROLL = 4
SEGMENTS = SUBLANES
STEPS = ROW_BLOCK // SEGMENTS
S5_BLOCKS = 4
S5_BLOCK_WIDTH = S5_WIDTH // S5_BLOCKS
CONV_ROWS = 64
VMEM_LIMIT = 48 * 1024 * 1024
SMALL_ROWS = 320


def _params(sem=None):
    kw = dict(vmem_limit_bytes=VMEM_LIMIT)
    if sem is not None:
        kw["dimension_semantics"] = sem
    return pltpu.CompilerParams(**kw)


def _sds(shape, dtype=F32):
    return jax.ShapeDtypeStruct(tuple(shape), dtype)


def _fold8(x):
    return x.reshape(x.shape[0] // SUBLANES, SUBLANES, x.shape[1]).sum(axis=0)


def _sigmoid(x):
    return 1.0 / (1.0 + jnp.exp(-x))


def _silu(x):
    return x * _sigmoid(x)


def _dsilu(x):
    s = _sigmoid(x)
    return s * (1.0 + x * (1.0 - s))


_GELU_C = 0.7978845608028654


def _gelu(x):
    return 0.5 * x * (1.0 + jnp.tanh(_GELU_C * (x + 0.044715 * x * x * x)))


def _dgelu(x):
    t = jnp.tanh(_GELU_C * (x + 0.044715 * x * x * x))
    return 0.5 * (1.0 + t) + 0.5 * x * (1.0 - t * t) * _GELU_C * (1.0 + 3.0 * 0.044715 * x * x)


def _rms(x):
    rstd = lax.rsqrt(jnp.mean(x * x, axis=-1, keepdims=True) + EPS_RMS)
    return x * rstd, rstd


def _epi_residual_prenorm(acc, res, gate, gain, scale, shift):
    h = res + gate * acc
    xh, _ = _rms(h)
    a = (xh * gain) * (1.0 + scale) + shift
    return acc, h, a, a.T


def _epi_residual_loss(acc, res, gate, target, gain):
    h = res + gate * acc
    xh, rstd = _rms(h)
    err = xh * gain - target
    dy = err * (1.0 / h.shape[-1])
    dxh = dy * gain
    dh = rstd * (dxh - xh * jnp.mean(dxh * xh, axis=-1, keepdims=True))
    return acc, dh, dh * gate, _fold8(err * err), _fold8(dy * xh)


def _epi_norm_bwd(d_act, x, res, aux, gain, scale, gate):
    xh, rstd = _rms(x)
    dn = d_act * (1.0 + scale)
    dxh = dn * gain
    dx = res + rstd * (dxh - xh * jnp.mean(dxh * xh, axis=-1, keepdims=True))
    return dx, dx * gate, _fold8(d_act), _fold8(d_act * (xh * gain)), _fold8(dn * xh), _fold8(res * aux)


def _dot(a, b, mode):
    dims = {"nn": (((1,), (0,)), ((), ())), "nt": (((1,), (1,)), ((), ())), "tn": (((0,), (0,)), ((), ()))}[mode]
    return lax.dot_general(a, b, dims, preferred_element_type=F32)


def _peers(x, y, c):
    out = []
    for k in range(1, NDEV):
        px = 1 - x if k & 4 else x
        py = 1 - y if k & 2 else y
        pc = 1 - c if k & 1 else c
        out.append(((px, py, pc), 4 * px + 2 * py + pc))
    return out


def _exchange_copies(src, land, send_sems, recv_sems, gather):
    x, y, c = lax.axis_index("x"), lax.axis_index("y"), lax.axis_index("c")
    me = 4 * x + 2 * y + c
    out = []
    for a in range(len(src)):
        for k, (peer, plin) in enumerate(_peers(x, y, c)):
            chunk = src[a] if gather[a] else src[a].at[plin]
            sems = dict(send_sem=send_sems.at[a * (NDEV - 1) + k], recv_sem=recv_sems.at[a * (NDEV - 1) + k],
                        device_id=peer, device_id_type=MESH)
            out.append((pltpu.make_async_remote_copy(src_ref=chunk, dst_ref=land[a].at[me], **sems),
                        pltpu.make_async_remote_copy(src_ref=chunk, dst_ref=land[a].at[plin], **sems)))
    return out


def _exchange(name, srcs, gather):
    n = len(srcs)
    outs = [_sds(((NDEV,) + s.shape) if g else s.shape, s.dtype) for s, g in zip(srcs, gather)]

    def body(*refs):
        src, dst, token = refs[:n], refs[n:2 * n], refs[2 * n]
        send_sems, recv_sems, local_sems = refs[2 * n + 1:]
        me = 4 * lax.axis_index("x") + 2 * lax.axis_index("y") + lax.axis_index("c")
        local = [pltpu.make_async_copy(src[a] if gather[a] else src[a].at[me], dst[a].at[me], local_sems.at[a])
                 for a in range(n)]
        for copy in local:
            copy.start()
        copies = _exchange_copies(src, dst, send_sems, recv_sems, gather)
        for copy, _ in copies:
            copy.start()
        token[...] = jnp.zeros_like(token)
        for copy, landing in copies:
            copy.wait_send()
            landing.wait_recv()
        for copy in local:
            copy.wait()

    nsem = n * (NDEV - 1)
    out = pl.pallas_call(
        body, name=name, out_shape=outs + [_sds((SUBLANES, LANES))], in_specs=[ANY] * n,
        out_specs=[ANY] * n + [pl.BlockSpec(memory_space=pltpu.VMEM)],
        scratch_shapes=[pltpu.SemaphoreType.DMA((nsem,)), pltpu.SemaphoreType.DMA((nsem,)), pltpu.SemaphoreType.DMA((n,))],
    )(*srcs)
    return out[:n], out[n]


HBM = pl.BlockSpec(memory_space=pltpu.HBM)
SEM = pl.BlockSpec(memory_space=pltpu.SEMAPHORE)
EFFECT = pltpu.SideEffectType.DATAFLOW_SIDE_EFFECTING


def _exchange_start(name, srcs, gather):
    n = len(srcs)
    lands = [lax.empty(((NDEV,) + s.shape) if g else s.shape, s.dtype) for s, g in zip(srcs, gather)]

    def body(*refs):
        src, land = refs[:n], refs[n:2 * n]
        send_sems, recv_sems = refs[2 * n], refs[2 * n + 1]
        token = refs[-1]
        for copy, _ in _exchange_copies(src, land, send_sems, recv_sems, gather):
            copy.start()
        token[...] = jnp.zeros_like(token)

    hbm = lambda v: pltpu.HBM(v.shape, v.dtype)
    nsem = n * (NDEV - 1)
    out = pl.pallas_call(
        body, name=name,
        out_shape=(pltpu.SemaphoreType.DMA((nsem,)), pltpu.SemaphoreType.DMA((nsem,)), *[hbm(v) for v in srcs],
                   *[hbm(v) for v in lands], _sds((SUBLANES, LANES))),
        in_specs=[HBM] * (2 * n), out_specs=(SEM, SEM, *([HBM] * (2 * n)), pl.BlockSpec(memory_space=pltpu.VMEM)),
        input_output_aliases={i: 2 + i for i in range(2 * n)},
        compiler_params=pltpu.CompilerParams(has_side_effects=EFFECT),
    )(*[pltpu.with_memory_space_constraint(v, pltpu.HBM) for v in list(srcs) + lands])
    return out[0], out[1], out[2:2 + n], out[2 + n:2 + 2 * n], out[-1]


def _exchange_wait(name, send_sems, recv_sems, srcs, lands, gather, after):
    n = len(srcs)

    def body(*refs):
        src, land = refs[:n], refs[n:2 * n]
        send_ref, recv_ref = refs[2 * n], refs[2 * n + 1]
        for copy, landing in _exchange_copies(src, land, send_ref, recv_ref, gather):
            copy.wait_send()
            landing.wait_recv()

    hbm = lambda v: pltpu.HBM(v.shape, v.dtype)
    out = pl.pallas_call(
        body, name=name, out_shape=[hbm(v) for v in list(srcs) + list(lands)],
        in_specs=[HBM] * (2 * n) + [SEM, SEM, ANY], out_specs=[HBM] * (2 * n),
        input_output_aliases={i: i for i in range(2 * n)},
        compiler_params=pltpu.CompilerParams(has_side_effects=EFFECT),
    )(*srcs, *lands, send_sems, recv_sems, after)
    return out[:n], out[n:]


def _with_own(landed, own, me):
    return lax.dynamic_update_slice(landed, own[None], (me,) + (0,) * own.ndim)


def _matmul(name, a, b, mode, mnk, tiles, outs, a_spec=None, b_spec=None, a_fn=None, a_extra=(),
            epi=None, epi_extra=(), out_specs=None, b_slabs=1):
    m_, n_, k_ = mnk
    tm, tn, tk = tiles
    nk = k_ // tk
    if a_spec is None:
        a_spec = (pl.BlockSpec((tk, tm), lambda i, j, k: (k, i)) if mode == "tn"
                  else pl.BlockSpec((tm, tk), lambda i, j, k: (i, k)))
    if b_spec is None:
        b_spec = (pl.BlockSpec((tn, tk), lambda i, j, k: (j, k)) if mode == "nt"
                  else pl.BlockSpec((tk, tn), lambda i, j, k: (k, j)))
    if out_specs is None:
        out_specs = [pl.BlockSpec((tm, tn), lambda i, j, k: (i, j)) for _ in outs]
    na, ne, no = len(a_extra), len(epi_extra), len(outs)

    def body(*refs):
        a_ref, b_ref = refs[0], refs[1]
        ax = refs[2:2 + na]
        ex = refs[2 + na:2 + na + ne]
        o = refs[2 + na + ne:2 + na + ne + no]

        def finish(res):
            res = epi(res, *[r[...] for r in ex]) if epi is not None else (res,)
            for ref, val in zip(o, res):
                ref[...] = val.astype(ref.dtype)

        at = a_ref[...]
        if a_fn is not None:
            at = a_fn(at, *[r[...] for r in ax])
        at = at.astype(BF16)
        if b_slabs == 1:
            part = _dot(at, b_ref[...].astype(BF16), mode)
        else:
            ks = tk // b_slabs
            part = _dot(at[:, 0:ks], b_ref[0].astype(BF16), mode)
            for s in range(1, b_slabs):
                part = part + _dot(at[:, s * ks:(s + 1) * ks], b_ref[s].astype(BF16), mode)
        if nk == 1:
            finish(part)
            return
        acc = refs[-1]
        k = pl.program_id(2)

        @pl.when(k == 0)
        def _():
            acc[...] = part

        @pl.when(k > 0)
        def _():
            acc[...] += part

        @pl.when(k == nk - 1)
        def _():
            finish(acc[...])

    return pl.pallas_call(
        body, name=name, grid=(m_ // tm, n_ // tn, nk),
        in_specs=[a_spec, b_spec] + [s for _, s in a_extra] + [s for _, s in epi_extra],
        out_specs=out_specs, out_shape=[_sds(s, d) for s, d in outs],
        scratch_shapes=[pltpu.VMEM((tm, tn), F32)] if nk > 1 else [],
        compiler_params=_params(("parallel", "parallel", "arbitrary")),
    )(a, b, *[x for x, _ in a_extra], *[x for x, _ in epi_extra])


def _prenorm(name, x, ctx, gain, shsc):
    n_lat = x.shape[0] // ROW_BLOCK
    n_ctx = 0 if ctx is None else ctx.shape[0] // ROW_BLOCK
    d = x.shape[1]

    def norm(src, g_ref, m_ref, o_ref):
        xv = src[...]
        xh = xv * lax.rsqrt(jnp.mean(xv * xv, axis=-1, keepdims=True) + EPS_RMS)
        o_ref[...] = ((xh * g_ref[...]) * (1.0 + m_ref[1:2, :]) + m_ref[0:1, :]).astype(o_ref.dtype)

    def body(*refs):
        if ctx is None:
            x_ref, g_ref, m_ref, o_ref = refs
            norm(x_ref, g_ref, m_ref, o_ref)
        else:
            x_ref, c_ref, g_ref, m_ref, o_ref = refs
            i = pl.program_id(0)

            @pl.when(i < n_lat)
            def _():
                norm(x_ref, g_ref, m_ref, o_ref)

            @pl.when(i >= n_lat)
            def _():
                norm(c_ref, g_ref, m_ref, o_ref)

    in_specs = [pl.BlockSpec((ROW_BLOCK, d), lambda i: (jnp.minimum(i, n_lat - 1), 0))]
    args = [x]
    if ctx is not None:
        in_specs.append(pl.BlockSpec((ROW_BLOCK, d), lambda i: (jnp.maximum(i - n_lat, 0), 0)))
        args.append(ctx)
    in_specs += [pl.BlockSpec((1, d), lambda i: (0, 0)),
                 pl.BlockSpec((None, 2, d), lambda i: (jnp.minimum(i // n_lat, 1), 0, 0))]
    args += [gain, shsc]
    return pl.pallas_call(
        body, name=name, grid=(n_lat + n_ctx,), in_specs=in_specs,
        out_specs=pl.BlockSpec((ROW_BLOCK, d), lambda i: (i, 0)),
        out_shape=_sds(((n_lat + n_ctx) * ROW_BLOCK, d), BF16),
        compiler_params=_params(("parallel",)),
    )(*args)


def _norm_bwd(name, x, d_act, d_act_row0, gain, scale, res=None, aux=None, gate=None):
    rows, d = x.shape
    nb = rows // ROW_BLOCK
    has_res = res is not None
    has_gate = gate is not None

    def body(*refs):
        if has_gate:
            x_ref, da_ref, g_ref, sc_ref, r_ref, aux_ref, gate_ref, dx_ref, dm_ref, sums = refs
        elif has_res:
            x_ref, da_ref, g_ref, sc_ref, r_ref, aux_ref, dx_ref, sums = refs
        else:
            x_ref, da_ref, g_ref, sc_ref, sums = refs
        i = pl.program_id(0)

        @pl.when(i == 0)
        def _():
            sums[...] = jnp.zeros_like(sums)

        xv, da = x_ref[...], da_ref[...]
        rstd = lax.rsqrt(jnp.mean(xv * xv, axis=-1, keepdims=True) + EPS_RMS)
        xh = xv * rstd
        g = g_ref[...]
        dn = da * (1.0 + sc_ref[...])
        sums[0] += _fold8(da)
        sums[1] += _fold8(da * (xh * g))
        sums[2] += _fold8(dn * xh)
        if has_res:
            dxh = dn * g
            dx = rstd * (dxh - xh * jnp.mean(dxh * xh, axis=-1, keepdims=True))
            rv = r_ref[...]
            dx_ref[...] = rv + dx
            sums[3] += _fold8(rv * aux_ref[...])
            if has_gate:
                dm_ref[...] = ((rv + dx) * gate_ref[...]).astype(dm_ref.dtype)

    row = lambda i: (i, 0)
    vec = pl.BlockSpec((1, d), lambda i: (0, 0))
    in_specs = [pl.BlockSpec((ROW_BLOCK, d), row), pl.BlockSpec((ROW_BLOCK, d), lambda i: (i + d_act_row0, 0)), vec, vec]
    args = [x, d_act, gain, scale]
    out_shape = [_sds((4, SUBLANES, d))]
    out_specs = [pl.BlockSpec((4, SUBLANES, d), lambda i: (0, 0, 0))]
    if has_res:
        in_specs += [pl.BlockSpec((ROW_BLOCK, d), row), pl.BlockSpec((ROW_BLOCK, d), row)]
        args += [res, aux]
        if has_gate:
            in_specs.append(vec)
            args.append(gate)
            out_shape = [_sds((rows, d), BF16)] + out_shape
            out_specs = [pl.BlockSpec((ROW_BLOCK, d), row)] + out_specs
        out_shape = [_sds((rows, d))] + out_shape
        out_specs = [pl.BlockSpec((ROW_BLOCK, d), row)] + out_specs
    return pl.pallas_call(
        body, name=name, grid=(nb,), in_specs=in_specs, out_specs=out_specs, out_shape=out_shape,
        compiler_params=_params(("arbitrary",)),
    )(*args)


def _loss_head(h2, target, gain, gate):
    rows, d = h2.shape

    def body(h_ref, t_ref, g_ref, gate_ref, dh_ref, dm_ref, err_ref, dg_ref):
        i = pl.program_id(0)

        @pl.when(i == 0)
        def _():
            err_ref[...] = jnp.zeros_like(err_ref)
            dg_ref[...] = jnp.zeros_like(dg_ref)

        hv = h_ref[...]
        rstd = lax.rsqrt(jnp.mean(hv * hv, axis=-1, keepdims=True) + EPS_RMS)
        xh = hv * rstd
        g = g_ref[...]
        err = xh * g - t_ref[...]
        err_ref[...] += _fold8(err * err)
        dy = err * (1.0 / d)
        dg_ref[...] += _fold8(dy * xh)
        dxh = dy * g
        dh = rstd * (dxh - xh * jnp.mean(dxh * xh, axis=-1, keepdims=True))
        dh_ref[...] = dh
        dm_ref[...] = (dh * gate_ref[...]).astype(dm_ref.dtype)

    row = pl.BlockSpec((ROW_BLOCK, d), lambda i: (i, 0))
    acc = pl.BlockSpec((SUBLANES, d), lambda i: (0, 0))
    vec = pl.BlockSpec((1, d), lambda i: (0, 0))
    return pl.pallas_call(
        body, name="loss_head", grid=(rows // ROW_BLOCK,),
        in_specs=[row, row, vec, vec], out_specs=[row, row, acc, acc],
        out_shape=[_sds((rows, d)), _sds((rows, d), BF16), _sds((SUBLANES, d)), _sds((SUBLANES, d))],
        compiler_params=_params(("arbitrary",)),
    )(h2, target, gain, gate)


def _ada_fwd(cond16, ada_w_loc, ada_b_loc):
    cols = ada_w_loc.shape[1]

    def body(c_ref, w_ref, b_ref, o_ref):
        s = _silu(c_ref[...]).astype(BF16)
        o_ref[...] = _dot(s, w_ref[...].astype(BF16), "nn") + b_ref[...]

    return pl.pallas_call(body, name="ada_fwd", out_shape=_sds((16, cols)), compiler_params=_params())(
        cond16, ada_w_loc, ada_b_loc)


def _ada_bwd(cond16, dmod16, ada_w_loc, c_ctx_row):
    k_, cols = ada_w_loc.shape

    def body(c_ref, dm_ref, w_ref, cc_ref, gw_ref, gc_ref):
        s = _silu(c_ref[...]).astype(BF16)
        dm = dm_ref[...]
        gw_ref[...] = _dot(s, dm.astype(BF16), "tn")
        dmc = jnp.sum(dm[8:16, :], axis=0, keepdims=True)
        dmc8 = jnp.broadcast_to(dmc, (SUBLANES, cols)).astype(BF16)
        ds = _dot(dmc8, w_ref[...].astype(BF16), "nt")
        row = lax.broadcasted_iota(jnp.int32, ds.shape, 0)
        gc_ref[...] = jnp.where(row == 0, ds * _dsilu(cc_ref[...]), 0.0)

    return pl.pallas_call(body, name="ada_bwd", out_shape=[_sds((k_, cols)), _sds((SUBLANES, k_))],
                          compiler_params=_params())(cond16, dmod16, ada_w_loc, c_ctx_row)


def _cmul(a, b):
    return a[0] * b[0] - a[1] * b[1], a[0] * b[1] + a[1] * b[0]


def _disc(lam_re, lam_im, ldt):
    dt = jnp.exp(ldt)
    mag = jnp.exp(lam_re * dt)
    th = lam_im * dt
    a_re, a_im = mag * jnp.cos(th), mag * jnp.sin(th)
    den = lam_re * lam_re + lam_im * lam_im
    n_re = a_re - 1.0
    f_re = (n_re * lam_re + a_im * lam_im) / den
    f_im = (a_im * lam_re - n_re * lam_im) / den
    return dt, mag, th, a_re, a_im, den, n_re, f_re, f_im


def _block_diag_mask(shape):
    row = lax.broadcasted_iota(jnp.int32, shape, 0)
    col = lax.broadcasted_iota(jnp.int32, shape, 1)
    return lax.shift_right_logical(row, 4) == lax.shift_right_logical(col, 6)


TAB_A = 0
TAB_BIG = 1
TAB_SEG = 4
TAB_PW = 5
TAB_ROWS = TAB_PW + STEPS


def _s5_discretise(name, ascending, lam_re, lam_im, ldt, bt_re, bt_im, ct_re, ct_im):
    def write_tables(ref, pw, big, asc, sign):
        row = lax.broadcasted_iota(jnp.int32, (SUBLANES, NSTATE), 0)
        full = lambda v: jnp.broadcast_to(v, (SUBLANES, NSTATE))

        def put(t, p):
            ref[0, t] = full(p[0])
            ref[1, t] = full(sign * p[1])

        put(TAB_A, pw[0])
        for t in range(3):
            put(TAB_BIG + t, big[t])
        seg = [big[0]]
        for _ in range(SEGMENTS - 1):
            seg.append(_cmul(seg[-1], big[0]))
        seg_re = jnp.zeros((SUBLANES, NSTATE), F32)
        seg_im = jnp.zeros((SUBLANES, NSTATE), F32)
        for r in range(SEGMENTS):
            p = seg[r] if asc else seg[SEGMENTS - 1 - r]
            seg_re = jnp.where(row == r, p[0], seg_re)
            seg_im = jnp.where(row == r, sign * p[1], seg_im)
        ref[0, TAB_SEG] = seg_re
        ref[1, TAB_SEG] = seg_im
        for k in range(STEPS):
            put(TAB_PW + k, pw[k])

    def body(lr_ref, li_ref, ldt_ref, br_ref, bi_ref, cr_ref, ci_ref, bb_ref, tab_ref, adj_ref, bm_ref, cm_ref):
        _, _, _, a_re, a_im, _, _, f_re, f_im = _disc(lr_ref[...], li_ref[...], ldt_ref[...])
        bre, bim = br_ref[...], bi_ref[...]
        bb_re = f_re * bre - f_im * bim
        bb_im = f_re * bim + f_im * bre
        bb_ref[0:S5_GROUP, :] = bb_re
        bb_ref[S5_GROUP:2 * S5_GROUP, :] = bb_im
        pw = [(a_re, a_im)]
        for _ in range(STEPS - 1):
            pw.append(_cmul(pw[-1], (a_re, a_im)))
        big = [pw[STEPS - 1]]
        for _ in range(2):
            big.append(_cmul(big[-1], big[-1]))
        write_tables(tab_ref, pw, big, ascending, 1.0)
        write_tables(adj_ref, pw, big, not ascending, -1.0)
        half = NSTATE // S5_BLOCKS
        mask = _block_diag_mask((S5_BLOCK_WIDTH, half))
        tile = lambda v: jnp.broadcast_to(v[None], (S5_BLOCK_WIDTH // S5_GROUP, S5_GROUP, half)).reshape(S5_BLOCK_WIDTH, half)
        for c in range(S5_BLOCKS):
            cols = slice(c * half, (c + 1) * half)
            rows = slice(c * S5_BLOCK_WIDTH, (c + 1) * S5_BLOCK_WIDTH)
            bm_ref[c, :, 0:half] = jnp.where(mask, tile(bb_re[:, cols]), 0.0).astype(BF16)
            bm_ref[c, :, half:2 * half] = jnp.where(mask, tile(bb_im[:, cols]), 0.0).astype(BF16)
            cm_ref[c, :, 0:half] = jnp.where(mask, cr_ref[rows, :], 0.0).astype(BF16)
            cm_ref[c, :, half:2 * half] = jnp.where(mask, -ci_ref[rows, :], 0.0).astype(BF16)

    blocked = _sds((S5_BLOCKS, S5_BLOCK_WIDTH, 2 * NSTATE // S5_BLOCKS), BF16)
    return pl.pallas_call(
        body, name=name,
        out_shape=[_sds((2 * S5_GROUP, NSTATE)), _sds((2, TAB_ROWS, SUBLANES, NSTATE)),
                   _sds((2, TAB_ROWS, SUBLANES, NSTATE)), blocked, blocked],
        compiler_params=_params(),
    )(lam_re, lam_im, ldt, bt_re, bt_im, ct_re, ct_im)


def _s5_discretise_bwd(name, lam_re, lam_im, ldt, bt_re, bt_im, d_abar8, d_bbar):
    def body(lr_ref, li_ref, ldt_ref, br_ref, bi_ref, da_ref, db_ref, dl_ref, dbt_ref):
        lam_re, lam_im = lr_ref[...], li_ref[...]
        dt, mag, _, a_re, a_im, den, n_re, f_re, f_im = _disc(lam_re, lam_im, ldt_ref[...])
        bre, bim = br_ref[...], bi_ref[...]
        dbr, dbi = db_ref[0:S5_GROUP, :], db_ref[S5_GROUP:2 * S5_GROUP, :]
        dbt_ref[0:S5_GROUP, :] = f_re * dbr + f_im * dbi
        dbt_ref[S5_GROUP:2 * S5_GROUP, :] = f_re * dbi - f_im * dbr
        df_re = jnp.sum(bre * dbr + bim * dbi, axis=0, keepdims=True)
        df_im = jnp.sum(bre * dbi - bim * dbr, axis=0, keepdims=True)
        da = da_ref[...]
        da_re = jnp.sum(da[:, 0:NSTATE], axis=0, keepdims=True)
        da_im = jnp.sum(da[:, NSTATE:2 * NSTATE], axis=0, keepdims=True)
        da_re = da_re + (df_re * lam_re - df_im * lam_im) / den
        da_im = da_im + (df_re * lam_im + df_im * lam_re) / den
        ff = (f_re * df_re + f_im * df_im) * 2.0 / den
        d_lr = (df_re * n_re + df_im * a_im) / den - ff * lam_re
        d_li = (df_re * a_im - df_im * n_re) / den - ff * lam_im
        d_mag = (da_re * a_re + da_im * a_im) / mag
        d_th = da_im * a_re - da_re * a_im
        d_lr = d_lr + d_mag * mag * dt
        d_li = d_li + d_th * dt
        d_ldt = (d_mag * mag * lam_re + d_th * lam_im) * dt
        row = lax.broadcasted_iota(jnp.int32, (SUBLANES, NSTATE), 0)
        dl_ref[...] = jnp.where(row == 0, d_lr, jnp.where(row == 1, d_li, jnp.where(row == 2, d_ldt, 0.0)))

    return pl.pallas_call(
        body, name=name, out_shape=[_sds((SUBLANES, NSTATE)), _sds((2 * S5_GROUP, NSTATE))],
        compiler_params=_params(),
    )(lam_re, lam_im, ldt, bt_re, bt_im, d_abar8, d_bbar)


def _segment_permutation():
    rho = jnp.arange(ROW_BLOCK)
    src = STEPS * (rho % SEGMENTS) + rho // SEGMENTS
    return (src[:, None] == jnp.arange(ROW_BLOCK)[None, :]).astype(BF16)


def _permute_rows(perm_ref, v):
    return _dot(perm_ref[...], v, "nn").astype(BF16)


def _unpermute_rows(perm_t_ref, v):
    hi = v.astype(BF16)
    lo = (v - hi.astype(F32)).astype(BF16)
    return _dot(perm_t_ref[...], hi, "nn") + _dot(perm_t_ref[...], lo, "nn")


def _unrolled_loop(step, init):
    def trip(o, state):
        for u in range(SCAN_UNROLL):
            state = step(o * SCAN_UNROLL + u, state)
        return state

    return lax.fori_loop(0, STEPS // SCAN_UNROLL, trip, init)


def _scan_chunk(x_ref, out_ref, tab_ref, carry_re, carry_im, ascending, pair_ref=None, acc_ref=None):
    w = SCAN_LANES
    half = NSTATE // S5_BLOCKS
    row = lax.broadcasted_iota(jnp.int32, (SUBLANES, w), 0)
    last = (SEGMENTS - 1) if ascending else 0

    def from_previous_segment(v, k, fill):
        if ascending:
            return jnp.where(row >= k, pltpu.roll(v, k, 0), fill)
        return jnp.where(row < SEGMENTS - k, pltpu.roll(v, SEGMENTS - k, 0), fill)

    def tile_rows(k):
        return pl.ds(pl.multiple_of((k if ascending else STEPS - 1 - k) * SUBLANES, SUBLANES), SUBLANES)

    for j in range(NSTATE // w):
        n_l = pl.ds(j * w, w)
        lane0 = (j * w // half) * 2 * half + (j * w) % half
        re_l, im_l = pl.ds(lane0, w), pl.ds(lane0 + half, w)
        tab = lambda t, n_l=n_l: (tab_ref[0, t, :, n_l], tab_ref[1, t, :, n_l])
        a_re, a_im = tab(TAB_A)

        def local_step(k, h):
            rs = tile_rows(k)
            h_re = a_re * h[0] - a_im * h[1] + x_ref[rs, re_l]
            h_im = a_re * h[1] + a_im * h[0] + x_ref[rs, im_l]
            out_ref[rs, re_l] = h_re
            out_ref[rs, im_l] = h_im
            return h_re, h_im

        zero = jnp.zeros((SUBLANES, w), F32)
        end_re, end_im = lax.fori_loop(0, STEPS, local_step, (zero, zero))
        for t, k in ((TAB_BIG, 1), (TAB_BIG + 1, 2), (TAB_BIG + 2, 4)):
            p_re, p_im = tab(t)
            s_re, s_im = from_previous_segment(end_re, k, 0.0), from_previous_segment(end_im, k, 0.0)
            end_re, end_im = end_re + (p_re * s_re - p_im * s_im), end_im + (p_re * s_im + p_im * s_re)
        c0_re, c0_im = carry_re[:, n_l], carry_im[:, n_l]
        p_re, p_im = tab(TAB_SEG)
        end_re = end_re + (p_re * c0_re - p_im * c0_im)
        end_im = end_im + (p_re * c0_im + p_im * c0_re)
        carry_re[:, n_l] = jnp.broadcast_to(end_re[last:last + 1, :], end_re.shape)
        carry_im[:, n_l] = jnp.broadcast_to(end_im[last:last + 1, :], end_im.shape)
        in_re = from_previous_segment(end_re, 1, c0_re)
        in_im = from_previous_segment(end_im, 1, c0_im)

        def carry_step(k, st):
            rs = tile_rows(k)
            p_re, p_im = tab_ref[0, TAB_PW + k, :, n_l], tab_ref[1, TAB_PW + k, :, n_l]
            o_re = out_ref[rs, re_l] + (p_re * in_re - p_im * in_im)
            o_im = out_ref[rs, im_l] + (p_re * in_im + p_im * in_re)
            out_ref[rs, re_l] = o_re
            out_ref[rs, im_l] = o_im
            if pair_ref is None:
                return st
            s_re, s_im = pair_ref[rs, re_l], pair_ref[rs, im_l]
            return (o_re, o_im, st[2] + (st[0] * s_re + st[1] * s_im), st[3] + (st[1] * s_re - st[0] * s_im))

        if pair_ref is None:
            _unrolled_loop(carry_step, 0)
        else:
            fin = _unrolled_loop(carry_step, (in_re, in_im, zero, zero))
            acc_ref[:, n_l] += fin[2]
            acc_ref[:, pl.ds(NSTATE + j * w, w)] += fin[3]


def _scan_block_index(i, n_lat, ctx_first_then_ascending):
    if ctx_first_then_ascending:
        return jnp.where(i == 0, n_lat, i - 1)
    return jnp.where(i == 0, n_lat, n_lat - i)


def _full_spec(shape):
    return pl.BlockSpec(shape, lambda i: (0,) * len(shape))


_S5_BLOCKED = (S5_BLOCKS, S5_BLOCK_WIDTH, 2 * NSTATE // S5_BLOCKS)
_S5_TABLES = (2, TAB_ROWS, SUBLANES, NSTATE)
_S5_DIAG = (S5_BLOCKS, S5_GROUP, 2 * NSTATE // S5_BLOCKS)


def _s5_scan_fwd(name, ascending, z_all, bmat, cmat, tab, perm, perm_t):
    rows = z_all.shape[0]
    nb = rows // ROW_BLOCK
    n_lat = nb - 1
    bw, sw = S5_BLOCK_WIDTH, 2 * NSTATE // S5_BLOCKS

    def body(u_ref, bm_ref, cm_ref, tab_ref, p_ref, pt_ref, s_ref, y_ref, bu, yp, carry_re, carry_im):
        @pl.when(pl.program_id(0) == 0)
        def _():
            carry_re[...] = jnp.zeros_like(carry_re)
            carry_im[...] = jnp.zeros_like(carry_im)

        up = _permute_rows(p_ref, u_ref[...].astype(BF16))
        for c in range(S5_BLOCKS):
            bu[:, c * sw:(c + 1) * sw] = _dot(up[:, c * bw:(c + 1) * bw], bm_ref[c], "nn")
        _scan_chunk(bu, s_ref, tab_ref, carry_re, carry_im, ascending)
        for c in range(S5_BLOCKS):
            yp[:, c * bw:(c + 1) * bw] = _dot(s_ref[:, c * sw:(c + 1) * sw].astype(BF16), cm_ref[c], "nt")
        y_ref[...] = _unpermute_rows(pt_ref, yp[...])

    blk = lambda i: (_scan_block_index(i, n_lat, ascending), 0)
    return pl.pallas_call(
        body, name=name, grid=(nb,),
        in_specs=[pl.BlockSpec((ROW_BLOCK, S5_WIDTH), blk), _full_spec(_S5_BLOCKED), _full_spec(_S5_BLOCKED),
                  _full_spec(_S5_TABLES), _full_spec((ROW_BLOCK, ROW_BLOCK)), _full_spec((ROW_BLOCK, ROW_BLOCK))],
        out_specs=[pl.BlockSpec((ROW_BLOCK, 2 * NSTATE), blk), pl.BlockSpec((ROW_BLOCK, S5_WIDTH), blk)],
        out_shape=[_sds((rows, 2 * NSTATE)), _sds((rows, S5_WIDTH))],
        scratch_shapes=[pltpu.VMEM((ROW_BLOCK, 2 * NSTATE), F32), pltpu.VMEM((ROW_BLOCK, S5_WIDTH), F32),
                        pltpu.VMEM((SUBLANES, NSTATE), F32), pltpu.VMEM((SUBLANES, NSTATE), F32)],
        compiler_params=_params(("arbitrary",)),
    )(z_all, bmat, cmat, tab, perm, perm_t)


def _s5_scan_bwd(name, ascending, dy, z_all, states, bmat, cmat, adj, perm, perm_t):
    rows = states.shape[0]
    nb = rows // ROW_BLOCK
    n_lat = nb - 1
    bw, sw = S5_BLOCK_WIDTH, 2 * NSTATE // S5_BLOCKS

    def block_index(i):
        if ascending:
            return jnp.where(i == nb - 1, n_lat, n_lat - 1 - i)
        return jnp.where(i == nb - 1, n_lat, i)

    def body(dy_ref, u_ref, s_ref, bm_ref, cm_ref, adj_ref, p_ref, pt_ref, du_ref, db_ref, dc_ref, da_ref,
             g, dup, db_acc, dc_acc, carry_re, carry_im):
        i = pl.program_id(0)

        @pl.when(i == 0)
        def _():
            carry_re[...] = jnp.zeros_like(carry_re)
            carry_im[...] = jnp.zeros_like(carry_im)
            da_ref[...] = jnp.zeros_like(da_ref)
            db_acc[...] = jnp.zeros_like(db_acc)
            dc_acc[...] = jnp.zeros_like(dc_acc)

        @pl.when(i < nb - 1)
        def _():
            dyp = _permute_rows(p_ref, dy_ref[...].astype(BF16))
            for c in range(S5_BLOCKS):
                g[:, c * sw:(c + 1) * sw] = _dot(dyp[:, c * bw:(c + 1) * bw], cm_ref[c], "nn")
                dc_acc[c] += _dot(dyp[:, c * bw:(c + 1) * bw], s_ref[:, c * sw:(c + 1) * sw].astype(BF16), "tn")

        @pl.when(i == nb - 1)
        def _():
            g[...] = jnp.zeros_like(g)

        _scan_chunk(g, g, adj_ref, carry_re, carry_im, not ascending, pair_ref=s_ref, acc_ref=da_ref)
        up = _permute_rows(p_ref, u_ref[...].astype(BF16))
        for c in range(S5_BLOCKS):
            gc = g[:, c * sw:(c + 1) * sw].astype(BF16)
            dup[:, c * bw:(c + 1) * bw] = _dot(gc, bm_ref[c], "nt")
            db_acc[c] += _dot(up[:, c * bw:(c + 1) * bw], gc, "tn")
        du_ref[...] = _unpermute_rows(pt_ref, dup[...])

        @pl.when(i == nb - 1)
        def _():
            mask = _block_diag_mask((bw, sw // 2))
            for acc, out in ((db_acc, db_ref), (dc_acc, dc_ref)):
                for c in range(S5_BLOCKS):
                    for part in range(2):
                        cols = slice(part * (sw // 2), (part + 1) * (sw // 2))
                        kept = jnp.where(mask, acc[c, :, cols], 0.0)
                        out[c, :, cols] = kept.reshape(bw // S5_GROUP, S5_GROUP, sw // 2).sum(axis=0)

    blk = lambda i: (block_index(i), 0)
    return pl.pallas_call(
        body, name=name, grid=(nb,),
        in_specs=[pl.BlockSpec((ROW_BLOCK, S5_WIDTH), lambda i: (jnp.minimum(block_index(i), n_lat - 1), 0)),
                  pl.BlockSpec((ROW_BLOCK, S5_WIDTH), blk), pl.BlockSpec((ROW_BLOCK, 2 * NSTATE), blk),
                  _full_spec(_S5_BLOCKED), _full_spec(_S5_BLOCKED), _full_spec(_S5_TABLES),
                  _full_spec((ROW_BLOCK, ROW_BLOCK)), _full_spec((ROW_BLOCK, ROW_BLOCK))],
        out_specs=[pl.BlockSpec((ROW_BLOCK, S5_WIDTH), blk), _full_spec(_S5_DIAG), _full_spec(_S5_DIAG),
                   _full_spec((SUBLANES, 2 * NSTATE))],
        out_shape=[_sds((rows, S5_WIDTH)), _sds(_S5_DIAG), _sds(_S5_DIAG), _sds((SUBLANES, 2 * NSTATE))],
        scratch_shapes=[pltpu.VMEM((ROW_BLOCK, 2 * NSTATE), F32), pltpu.VMEM((ROW_BLOCK, S5_WIDTH), F32),
                        pltpu.VMEM(_S5_BLOCKED, F32), pltpu.VMEM(_S5_BLOCKED, F32),
                        pltpu.VMEM((SUBLANES, NSTATE), F32), pltpu.VMEM((SUBLANES, NSTATE), F32)],
        compiler_params=_params(("arbitrary",)),
    )(dy, z_all, states, bmat, cmat, adj, perm, perm_t)


def _glu_fwd(z_all, y0, y1, d_skip, w_glu, n_rows):
    def body(u_ref, y0_ref, y1_ref, d_ref, w_ref, o_ref):
        y = d_ref[...] * u_ref[...] + y0_ref[...] + y1_ref[...]
        g = _gelu(y)
        t = _dot(g.astype(BF16), w_ref[...], "nn")
        o_ref[...] = (g * _sigmoid(t)).astype(o_ref.dtype)

    row = pl.BlockSpec((ROW_BLOCK, S5_WIDTH), lambda i: (i, 0))
    return pl.pallas_call(
        body, name="glu_fwd", grid=(n_rows // ROW_BLOCK,),
        in_specs=[row, row, row, pl.BlockSpec((1, S5_WIDTH), lambda i: (0, 0)),
                  pl.BlockSpec((S5_WIDTH, S5_WIDTH), lambda i: (0, 0))],
        out_specs=row, out_shape=_sds((n_rows, S5_WIDTH + CONV_WIDTH), BF16), compiler_params=_params(("parallel",)),
    )(z_all, y0, y1, d_skip, w_glu)


def _glu_bwd(d_ycat, z_all, y0, y1, d_skip, w_glu, n_rows):
    def body(do_ref, u_ref, y0_ref, y1_ref, d_ref, w_ref, dy_ref, dw_ref, dd_ref):
        @pl.when(pl.program_id(0) == 0)
        def _():
            dw_ref[...] = jnp.zeros_like(dw_ref)
            dd_ref[...] = jnp.zeros_like(dd_ref)

        u = u_ref[...]
        y = d_ref[...] * u + y0_ref[...] + y1_ref[...]
        g = _gelu(y)
        gb = g.astype(BF16)
        w = w_ref[...]
        sg = _sigmoid(_dot(gb, w, "nn"))
        do = do_ref[...]
        dt = do * g * sg * (1.0 - sg)
        dtb = dt.astype(BF16)
        dg = do * sg + _dot(dtb, w, "nt")
        dy = dg * _dgelu(y)
        dy_ref[...] = dy
        dw_ref[...] += _dot(gb, dtb, "tn")
        dd_ref[...] += _fold8(dy * u)

    row = pl.BlockSpec((ROW_BLOCK, S5_WIDTH), lambda i: (i, 0))
    sq = pl.BlockSpec((S5_WIDTH, S5_WIDTH), lambda i: (0, 0))
    return pl.pallas_call(
        body, name="glu_bwd", grid=(n_rows // ROW_BLOCK,),
        in_specs=[row, row, row, row, pl.BlockSpec((1, S5_WIDTH), lambda i: (0, 0)), sq],
        out_specs=[row, sq, pl.BlockSpec((SUBLANES, S5_WIDTH), lambda i: (0, 0))],
        out_shape=[_sds((n_rows, S5_WIDTH)), _sds((S5_WIDTH, S5_WIDTH)), _sds((SUBLANES, S5_WIDTH))],
        compiler_params=_params(("arbitrary",)),
    )(d_ycat, z_all, y0, y1, d_skip, w_glu)


CONV_HALF = CONV_K // 2


def _conv_block(n_rows):
    blk = min(1024, n_rows)
    assert blk >= CONV_HALF * GRID_W and n_rows % blk == 0
    return blk


def _conv_gate(z_all, n_rows):
    blk = _conv_block(n_rows)
    nb = n_rows // blk

    def body(v_ref, g_ref, o_ref):
        i = pl.program_id(0)
        inside = jnp.logical_and(i >= 1, i <= nb)

        @pl.when(inside)
        def _():
            o_ref[...] = v_ref[...] * _sigmoid(g_ref[...])

        @pl.when(jnp.logical_not(inside))
        def _():
            o_ref[...] = jnp.zeros_like(o_ref)

    src = lambda col: pl.BlockSpec((blk, CONV_WIDTH), lambda i: (jnp.clip(i - 1, 0, nb - 1), col))
    return pl.pallas_call(
        body, name="conv_gate", grid=(nb + 2,), in_specs=[src(1), src(2)],
        out_specs=pl.BlockSpec((blk, CONV_WIDTH), lambda i: (i, 0)),
        out_shape=_sds(((nb + 2) * blk, CONV_WIDTH)), compiler_params=_params(("parallel",)),
    )(z_all, z_all)


def _stream_padded(pad_ref, buf, sems, blk, n_blocks):
    i = pl.program_id(0)

    def copy(b):
        rows = pl.ds(pl.multiple_of(b * blk, blk), blk)
        return pltpu.make_async_copy(pad_ref.at[rows, :], buf.at[rows, :], sems.at[b])

    @pl.when(i == 0)
    def _():
        for b in range(n_blocks):
            copy(b).start()
        copy(0).wait()
        copy(1).wait()

    copy(i + 2).wait()
    return pl.multiple_of(i * blk, blk)


def _conv_fwd(hh_pad, w, b, ln_g, ln_b, ycat, n_rows):
    blk = _conv_block(n_rows)
    nblk = n_rows // blk + 2

    def body(hh_ref, w_ref, b_ref, g_ref, lb_ref, ycat_ref, hc_ref, y_ref, win, sems):
        base = _stream_padded(hh_ref, win, sems, blk, nblk)

        def tile(t, _):
            r0 = pl.multiple_of(t * CONV_ROWS, CONV_ROWS)
            acc = jnp.zeros((CONV_ROWS, CONV_WIDTH), F32)
            for k in range(CONV_K):
                acc = acc + w_ref[k:k + 1, :] * win[pl.ds(base + r0 + blk + (k - CONV_HALF) * GRID_W, CONV_ROWS), :]
            hc = acc + b_ref[...]
            hc_ref[pl.ds(r0, CONV_ROWS), :] = hc
            mu = jnp.mean(hc, axis=-1, keepdims=True)
            xc = hc - mu
            ln = xc * lax.rsqrt(jnp.mean(xc * xc, axis=-1, keepdims=True) + EPS_LN) * g_ref[...] + lb_ref[...]
            y_ref[pl.ds(r0, CONV_ROWS), :] = _silu(ln).astype(y_ref.dtype)
            return 0

        lax.fori_loop(0, blk // CONV_ROWS, tile, 0)

    vec = pl.BlockSpec((1, CONV_WIDTH), lambda i: (0, 0))
    row = pl.BlockSpec((blk, CONV_WIDTH), lambda i: (i, 0))
    return pl.pallas_call(
        body, name="conv_fwd", grid=(n_rows // blk,),
        in_specs=[ANY, pl.BlockSpec((CONV_K, CONV_WIDTH), lambda i: (0, 0)), vec, vec, vec, ANY],
        out_specs=[row, pl.BlockSpec((blk, CONV_WIDTH), lambda i: (i, 1))],
        out_shape=[_sds((n_rows, CONV_WIDTH)), _sds(ycat.shape, ycat.dtype)], input_output_aliases={5: 1},
        scratch_shapes=[pltpu.VMEM((nblk * blk, CONV_WIDTH), F32), pltpu.SemaphoreType.DMA((nblk,))],
        compiler_params=_params(("arbitrary",)),
    )(hh_pad, w, b, ln_g, ln_b, ycat)


def _conv_bwd_norm(d_ycat, hc, ln_g, ln_b, n_rows):
    blk = _conv_block(n_rows)
    nb = n_rows // blk

    def body(dy_ref, hc_ref, g_ref, lb_ref, o_ref, sums):
        i = pl.program_id(0)

        @pl.when(i == 0)
        def _():
            sums[...] = jnp.zeros_like(sums)

        inside = jnp.logical_and(i >= 1, i <= nb)

        @pl.when(inside)
        def _():
            hcv = hc_ref[...]
            mu = jnp.mean(hcv, axis=-1, keepdims=True)
            xc = hcv - mu
            rstd = lax.rsqrt(jnp.mean(xc * xc, axis=-1, keepdims=True) + EPS_LN)
            xh = xc * rstd
            g = g_ref[...]
            dln = dy_ref[...] * _dsilu(xh * g + lb_ref[...])
            dxh = dln * g
            dhc = rstd * (dxh - jnp.mean(dxh, axis=-1, keepdims=True) - xh * jnp.mean(dxh * xh, axis=-1, keepdims=True))
            o_ref[...] = dhc
            sums[0] += _fold8(dhc)
            sums[1] += _fold8(dln * xh)
            sums[2] += _fold8(dln)

        @pl.when(jnp.logical_not(inside))
        def _():
            o_ref[...] = jnp.zeros_like(o_ref)

    vec = pl.BlockSpec((1, CONV_WIDTH), lambda i: (0, 0))
    return pl.pallas_call(
        body, name="conv_bwd_norm", grid=(nb + 2,),
        in_specs=[pl.BlockSpec((blk, CONV_WIDTH), lambda i: (jnp.clip(i - 1, 0, nb - 1), 1)),
                  pl.BlockSpec((blk, CONV_WIDTH), lambda i: (jnp.clip(i - 1, 0, nb - 1), 0)), vec, vec],
        out_specs=[pl.BlockSpec((blk, CONV_WIDTH), lambda i: (i, 0)),
                   pl.BlockSpec((3, SUBLANES, CONV_WIDTH), lambda i: (0, 0, 0))],
        out_shape=[_sds(((nb + 2) * blk, CONV_WIDTH)), _sds((3, SUBLANES, CONV_WIDTH))],
        compiler_params=_params(("arbitrary",)),
    )(d_ycat, hc, ln_g, ln_b)


def _conv_bwd_taps(dhc_pad, hh_pad, z_all, w, n_rows):
    blk = _conv_block(n_rows)
    nblk = n_rows // blk + 2

    def body(dhc_ref, hh_ref, v_ref, g_ref, w_ref, dv_ref, dg_ref, dw_ref, dwin, hwin, dsems, hsems):
        @pl.when(pl.program_id(0) == 0)
        def _():
            dw_ref[...] = jnp.zeros_like(dw_ref)

        base = _stream_padded(dhc_ref, dwin, dsems, blk, nblk)
        _stream_padded(hh_ref, hwin, hsems, blk, nblk)

        def tile(t, _):
            r0 = pl.multiple_of(t * CONV_ROWS, CONV_ROWS) + base
            dh = dwin[pl.ds(r0 + blk, CONV_ROWS), :]
            acc = jnp.zeros((CONV_ROWS, CONV_WIDTH), F32)
            for k in range(CONV_K):
                off = (k - CONV_HALF) * GRID_W
                acc = acc + w_ref[k:k + 1, :] * dwin[pl.ds(r0 + blk - off, CONV_ROWS), :]
                dw_ref[k] += _fold8(dh * hwin[pl.ds(r0 + blk + off, CONV_ROWS), :])
            rs = pl.ds(pl.multiple_of(t * CONV_ROWS, CONV_ROWS), CONV_ROWS)
            sg = _sigmoid(g_ref[rs, :])
            vv = v_ref[rs, :]
            dv_ref[rs, :] = acc * sg
            dg_ref[rs, :] = acc * vv * sg * (1.0 - sg)
            return 0

        lax.fori_loop(0, blk // CONV_ROWS, tile, 0)

    row = pl.BlockSpec((blk, CONV_WIDTH), lambda i: (i, 0))
    return pl.pallas_call(
        body, name="conv_bwd_taps", grid=(n_rows // blk,),
        in_specs=[ANY, ANY,
            pl.BlockSpec((blk, CONV_WIDTH), lambda i: (i, 1)), pl.BlockSpec((blk, CONV_WIDTH), lambda i: (i, 2)),
            pl.BlockSpec((CONV_K, CONV_WIDTH), lambda i: (0, 0))],
        out_specs=[row, row, pl.BlockSpec((CONV_K, SUBLANES, CONV_WIDTH), lambda i: (0, 0, 0))],
        out_shape=[_sds((n_rows, CONV_WIDTH)), _sds((n_rows, CONV_WIDTH)), _sds((CONV_K, SUBLANES, CONV_WIDTH))],
        scratch_shapes=[pltpu.VMEM((nblk * blk, CONV_WIDTH), F32), pltpu.VMEM((nblk * blk, CONV_WIDTH), F32),
                        pltpu.SemaphoreType.DMA((nblk,)), pltpu.SemaphoreType.DMA((nblk,))],
        compiler_params=_params(("arbitrary",)),
    )(dhc_pad, hh_pad, z_all, z_all, w)


def _dz_assemble(du0, du1, dy, d_skip, dv, dgate, n_lat):
    rows = du0.shape[0]
    nb = rows // ROW_BLOCK

    w = S5_WIDTH

    def body(a_ref, b_ref, dy_ref, d_ref, dv_ref, dg_ref, o_ref):
        lat = pl.program_id(0) < n_lat

        @pl.when(lat)
        def _():
            o_ref[:, 0:w] = (a_ref[...] + b_ref[...] + dy_ref[...] * d_ref[...]).astype(o_ref.dtype)
            o_ref[:, w:2 * w] = dv_ref[...].astype(o_ref.dtype)
            o_ref[:, 2 * w:3 * w] = dg_ref[...].astype(o_ref.dtype)

        @pl.when(jnp.logical_not(lat))
        def _():
            o_ref[:, 0:w] = (a_ref[...] + b_ref[...]).astype(o_ref.dtype)
            o_ref[:, w:3 * w] = jnp.zeros((ROW_BLOCK, 2 * w), o_ref.dtype)

    all_rows = pl.BlockSpec((ROW_BLOCK, w), lambda i: (i, 0))
    lat_rows = pl.BlockSpec((ROW_BLOCK, w), lambda i: (jnp.minimum(i, n_lat - 1), 0))
    return pl.pallas_call(
        body, name="dz_assemble", grid=(nb,),
        in_specs=[all_rows, all_rows, lat_rows, pl.BlockSpec((1, w), lambda i: (0, 0)), lat_rows, lat_rows],
        out_specs=pl.BlockSpec((ROW_BLOCK, IN_COLS), lambda i: (i, 0)),
        out_shape=_sds((rows, IN_COLS), BF16), compiler_params=_params(("parallel",)),
    )(du0, du1, dy, d_skip, dv, dgate)


def _sum_parts(parts):
    _, r, c = parts.shape

    def body(p_ref, o_ref):
        acc = p_ref[0]
        for q in range(1, NDEV):
            acc = acc + p_ref[q]
        o_ref[...] = acc

    return pl.pallas_call(body, name="sum_parts", out_shape=_sds((r, c)), compiler_params=_params())(parts)


def _row_tile(r, c):
    best = r
    for t in (1024, 512, 256, 128, 64, 32, 16, 8):
        if r % t == 0 and t * c <= 128 * 1024:
            return t
    return best


def _adamw(name, w, gparts, m, v):
    r, c = w.shape
    np_ = gparts.shape[0]
    tr = _row_tile(r, c)

    def body(w_ref, g_ref, m_ref, v_ref, go_ref, d_ref, mo_ref, vo_ref):
        g = g_ref[0].astype(F32)
        for q in range(1, np_):
            g = g + g_ref[q].astype(F32)
        m2 = ADAM_B1 * m_ref[...] + (1.0 - ADAM_B1) * g
        v2 = ADAM_B2 * v_ref[...] + (1.0 - ADAM_B2) * jnp.square(g)
        m_hat = m2 / (1.0 - ADAM_B1 ** ADAM_STEP)
        v_hat = v2 / (1.0 - ADAM_B2 ** ADAM_STEP)
        go_ref[...] = g
        d_ref[...] = -ADAM_LR * (m_hat / (jnp.sqrt(v_hat) + ADAM_EPS) + ADAM_WD * w_ref[...])
        mo_ref[...] = m2
        vo_ref[...] = v2

    row = pl.BlockSpec((tr, c), lambda i: (i, 0))
    return pl.pallas_call(
        body, name=name, grid=(r // tr,),
        in_specs=[row, pl.BlockSpec((np_, tr, c), lambda i: (0, i, 0)), row, row],
        out_specs=[row] * 4, out_shape=[_sds((r, c))] * 4, compiler_params=_params(("parallel",)),
    )(w, gparts, m, v)


def _adamw_native(name, w, g, m, v):
    def body(w_ref, g_ref, m_ref, v_ref, d_ref, mo_ref, vo_ref):
        gv = g_ref[...]
        m2 = ADAM_B1 * m_ref[...] + (1.0 - ADAM_B1) * gv
        v2 = ADAM_B2 * v_ref[...] + (1.0 - ADAM_B2) * jnp.square(gv)
        m_hat = m2 / (1.0 - ADAM_B1 ** ADAM_STEP)
        v_hat = v2 / (1.0 - ADAM_B2 ** ADAM_STEP)
        d_ref[...] = -ADAM_LR * (m_hat / (jnp.sqrt(v_hat) + ADAM_EPS) + ADAM_WD * w_ref[...])
        mo_ref[...] = m2
        vo_ref[...] = v2

    return pl.pallas_call(body, name=name, out_shape=[_sds(w.shape)] * 3, compiler_params=_params())(w, g, m, v)


SMALL = ["c_ctx", "ada_b", "norm1_g", "s5_lam_re", "s5_lam_im", "s5_log_dt", "s5_d", "conv_b", "conv_ln_g", "conv_ln_b",
         "norm2_g", "final_g"]
SMALL_PACKED_ROWS = 24


def _pack_rows(parts, rows):
    flat = jnp.concatenate([p.reshape(-1).astype(F32) for p in parts])
    return jnp.pad(flat, (0, rows * D_MODEL - flat.shape[0])).reshape(rows, D_MODEL)


def _unpack_rows(packed, shapes):
    flat = packed.reshape(-1)
    out, off = [], 0
    for shape in shapes:
        size = 1
        for s in shape:
            size *= s
        out.append(flat[off:off + size].reshape(shape))
        off += size
    return out


def kernel(x, c, ctx, c_ctx, ada_w, ada_b, norm1_g, w_in, s5_lam_re, s5_lam_im, s5_log_dt, s5_b_re, s5_b_im, s5_c_re, s5_c_im, s5_d, s5_w_glu, conv_w, conv_b, conv_ln_g, conv_ln_b, w_out, norm2_g, mlp_w1, mlp_w2, final_g, loss_target, m_c_ctx, m_ada_w, m_ada_b, m_norm1_g, m_w_in, m_s5_lam_re, m_s5_lam_im, m_s5_log_dt, m_s5_b_re, m_s5_b_im, m_s5_c_re, m_s5_c_im, m_s5_d, m_s5_w_glu, m_conv_w, m_conv_b, m_conv_ln_g, m_conv_ln_b, m_w_out, m_norm2_g, m_mlp_w1, m_mlp_w2, m_final_g, v_c_ctx, v_ada_w, v_ada_b, v_norm1_g, v_w_in, v_s5_lam_re, v_s5_lam_im, v_s5_log_dt, v_s5_b_re, v_s5_b_im, v_s5_c_re, v_s5_c_im, v_s5_d, v_s5_w_glu, v_conv_w, v_conv_b, v_conv_ln_g, v_conv_ln_b, v_w_out, v_norm2_g, v_mlp_w1, v_mlp_w2, v_final_g):
    weights = dict(c_ctx=c_ctx, ada_w=ada_w, ada_b=ada_b, norm1_g=norm1_g, w_in=w_in, s5_lam_re=s5_lam_re, s5_lam_im=s5_lam_im, s5_log_dt=s5_log_dt, s5_b_re=s5_b_re, s5_b_im=s5_b_im, s5_c_re=s5_c_re, s5_c_im=s5_c_im, s5_d=s5_d, s5_w_glu=s5_w_glu, conv_w=conv_w, conv_b=conv_b, conv_ln_g=conv_ln_g, conv_ln_b=conv_ln_b, w_out=w_out, norm2_g=norm2_g, mlp_w1=mlp_w1, mlp_w2=mlp_w2, final_g=final_g)
    mom1 = dict(c_ctx=m_c_ctx, ada_w=m_ada_w, ada_b=m_ada_b, norm1_g=m_norm1_g, w_in=m_w_in, s5_lam_re=m_s5_lam_re, s5_lam_im=m_s5_lam_im, s5_log_dt=m_s5_log_dt, s5_b_re=m_s5_b_re, s5_b_im=m_s5_b_im, s5_c_re=m_s5_c_re, s5_c_im=m_s5_c_im, s5_d=m_s5_d, s5_w_glu=m_s5_w_glu, conv_w=m_conv_w, conv_b=m_conv_b, conv_ln_g=m_conv_ln_g, conv_ln_b=m_conv_ln_b, w_out=m_w_out, norm2_g=m_norm2_g, mlp_w1=m_mlp_w1, mlp_w2=m_mlp_w2, final_g=m_final_g)
    mom2 = dict(c_ctx=v_c_ctx, ada_w=v_ada_w, ada_b=v_ada_b, norm1_g=v_norm1_g, w_in=v_w_in, s5_lam_re=v_s5_lam_re, s5_lam_im=v_s5_lam_im, s5_log_dt=v_s5_log_dt, s5_b_re=v_s5_b_re, s5_b_im=v_s5_b_im, s5_c_re=v_s5_c_re, s5_c_im=v_s5_c_im, s5_d=v_s5_d, s5_w_glu=v_s5_w_glu, conv_w=v_conv_w, conv_b=v_conv_b, conv_ln_g=v_conv_ln_g, conv_ln_b=v_conv_ln_b, w_out=v_w_out, norm2_g=v_norm2_g, mlp_w1=v_mlp_w1, mlp_w2=v_mlp_w2, final_g=v_final_g)
    order = list(weights)

    me = 4 * lax.axis_index("x") + 2 * lax.axis_index("y") + lax.axis_index("c")
    xs, cs, tgt = x[0], ctx[0], loss_target[0]
    n_lat_rows, n_ctx_rows = xs.shape[0], cs.shape[0]
    n_rows = n_lat_rows + n_ctx_rows
    n_lat = n_lat_rows // ROW_BLOCK
    ada_cols = ada_w.shape[2]

    (c_all,), _ = _exchange("gather_c", [c], [True])
    c_all = c_all.reshape(NDEV, D_MODEL)

    cond_fwd = jnp.concatenate([c_all, c_ctx[None], jnp.zeros((7, D_MODEL), F32)])
    ada_b_loc = lax.dynamic_slice(ada_b, (0, me * ada_cols), (1, ada_cols))
    (mod_g,), mod_token = _exchange("gather_mod", [_ada_fwd(cond_fwd, ada_w[0], ada_b_loc)], [True])
    wi_send, wi_recv, wi_src, wi_land, wi_token = _exchange_start(
        "gather_w_in_start", [w_in[0].astype(BF16) + mod_token[0:1, 0:1].astype(BF16)], [True])
    mixer_w = [s5_w_glu[0].astype(BF16), conv_w[0] + wi_token[0:1, 0:1], w_out[0].astype(BF16)]
    mixer_send, mixer_recv, mixer_src, mixer_land, mixer_token = _exchange_start("gather_mixer_start", mixer_w, [True] * 3)
    mlp_w = [mlp_w1[0].astype(BF16), mlp_w2[0].astype(BF16) + mixer_token[0:1, 0:1].astype(BF16)]
    mlpw_send, mlpw_recv, mlpw_src, mlpw_land, mlpw_token = _exchange_start("gather_mlp_start", mlp_w, [True] * 2)
    mod_rows = jnp.transpose(mod_g, (1, 0, 2)).reshape(16, 6 * D_MODEL) + mlpw_token[0:1, 0:1]
    mod = lax.dynamic_slice(mod_rows, (me, 0), (1, 6 * D_MODEL)).reshape(6, D_MODEL)
    modc = mod_rows[8, :2 * D_MODEL].reshape(2, D_MODEL)
    sh1, sc1, g1, sh2, sc2, g2 = [mod[i:i + 1] for i in range(6)]

    a_all = _prenorm("prenorm1", xs, cs, norm1_g, jnp.stack([mod[0:2], modc]))
    wi_own, wi_landed = _exchange_wait("gather_w_in_wait", wi_send, wi_recv, wi_src, wi_land, [True], a_all)
    w_in_full = jnp.transpose(_with_own(wi_landed[0], wi_own[0], me), (1, 0, 2)).reshape(D_MODEL, IN_COLS)
    tm_all = 1088 if n_rows % 1088 == 0 else ROW_BLOCK
    (z_all,) = _matmul("in_proj", a_all, w_in_full, "nn", (n_rows, IN_COLS, D_MODEL), (tm_all, IN_COLS, D_MODEL),
                       [((n_rows, IN_COLS), F32)])

    lam_re, lam_im = s5_lam_re[0].reshape(2, 1, NSTATE), s5_lam_im[0].reshape(2, 1, NSTATE)
    ldt = jnp.repeat(s5_log_dt[0], S5_STATE, axis=-1).reshape(2, 1, NSTATE)
    bt_re = jnp.transpose(s5_b_re[0], (0, 3, 1, 2)).reshape(2, S5_GROUP, NSTATE)
    bt_im = jnp.transpose(s5_b_im[0], (0, 3, 1, 2)).reshape(2, S5_GROUP, NSTATE)
    groups_per_block = S5_GROUPS // S5_BLOCKS
    ct_re = jnp.tile(s5_c_re[0].reshape(2, S5_WIDTH, S5_STATE), (1, 1, groups_per_block))
    ct_im = jnp.tile(s5_c_im[0].reshape(2, S5_WIDTH, S5_STATE), (1, 1, groups_per_block))
    d_skip = s5_d[0].reshape(1, S5_WIDTH)
    perm = _segment_permutation()
    perm_t = perm.T
    disc, states, y_dir = [], [], []
    for d in range(2):
        disc.append(_s5_discretise(f"s5_disc{d}", d == 0, lam_re[d], lam_im[d], ldt[d], bt_re[d], bt_im[d], ct_re[d], ct_im[d]))
        _, tab, _, bmat, cmat = disc[d]
        s, yd = _s5_scan_fwd(f"s5_scan_fwd{d}", d == 0, z_all, bmat, cmat, tab, perm, perm_t)
        states.append(s)
        y_dir.append(yd)
    mixer_own, mixer_landed = _exchange_wait("gather_mixer_wait", mixer_send, mixer_recv, mixer_src, mixer_land,
                                             [True] * 3, y_dir[1])
    glu_g, conv_w_g, w_out_g = [_with_own(l, o, me) for l, o in zip(mixer_landed, mixer_own)]
    glu_full = glu_g.reshape(S5_WIDTH, S5_WIDTH)
    conv_w_full = jnp.transpose(conv_w_g, (1, 0, 2)).reshape(CONV_K, CONV_WIDTH)
    w_out_full = w_out_g.reshape(D_MODEL, D_MODEL)
    ycat = _glu_fwd(z_all, y_dir[0], y_dir[1], d_skip, glu_full, n_lat_rows)

    hh_pad = _conv_gate(z_all, n_lat_rows)
    hc, ycat = _conv_fwd(hh_pad, conv_w_full, conv_b, conv_ln_g, conv_ln_b, ycat, n_lat_rows)

    tm = min(1024, n_lat_rows)
    tm_e = min(512, n_lat_rows)
    w1_cols = D_FF // NDEV
    row_vec = lambda tn: pl.BlockSpec((1, tn), lambda i, j, k: (0, j))
    out_tile = lambda t_m, t_n: pl.BlockSpec((t_m, t_n), lambda i, j, k: (i, j))
    full_rows = ((n_lat_rows, D_MODEL), F32)
    sums = ((n_lat_rows // tm_e, SUBLANES, D_MODEL), F32)
    sums_spec = pl.BlockSpec((None, SUBLANES, D_MODEL), lambda i, j, k: (i, 0, 0))
    vec = lambda v: (v, row_vec(D_MODEL))
    transposed_tile = lambda t_m, t_n: pl.BlockSpec((t_n, t_m), lambda i, j, k: (j, i))
    mix, h1, a2, a2_t = _matmul(
        "out_proj", ycat, w_out_full, "nn", (n_lat_rows, D_MODEL, D_MODEL), (tm_e, D_MODEL, D_MODEL),
        [full_rows, full_rows, ((n_lat_rows, D_MODEL), BF16), ((D_MODEL, n_lat_rows), BF16)],
        epi=_epi_residual_prenorm,
        epi_extra=[(xs, out_tile(tm_e, D_MODEL)), vec(g1), vec(norm2_g), vec(sc2), vec(sh2)],
        out_specs=[out_tile(tm_e, D_MODEL)] * 3 + [transposed_tile(tm_e, D_MODEL)])
    mlpw_own, mlpw_landed = _exchange_wait("gather_mlp_wait", mlpw_send, mlpw_recv, mlpw_src, mlpw_land, [True] * 2, a2)
    w1_g, w2_g = [_with_own(l, o, me) for l, o in zip(mlpw_landed, mlpw_own)]
    w2_full = w2_g.reshape(D_FF, D_MODEL)
    tm_up = min(2048, n_lat_rows)
    f, f_t = _matmul("mlp_up", a2, w1_g, "nn", (n_lat_rows, D_FF, D_MODEL), (tm_up, w1_cols, D_MODEL),
                     [((n_lat_rows, D_FF), BF16), ((D_FF, n_lat_rows), BF16)], epi=lambda acc: (acc, acc.T),
                     b_spec=pl.BlockSpec((None, D_MODEL, w1_cols), lambda i, j, k: (j, 0, 0)),
                     out_specs=[out_tile(tm_up, w1_cols), transposed_tile(tm_up, w1_cols)])
    sq_relu = lambda t: jnp.square(jnp.maximum(t, 0.0))
    mlp_out, d_h2, dm2, err_sums, d_final_g8 = _matmul(
        "mlp_down", f, w2_full, "nn", (n_lat_rows, D_MODEL, D_FF), (tm_e, D_MODEL, 2048),
        [full_rows, full_rows, ((n_lat_rows, D_MODEL), BF16), sums, sums], a_fn=sq_relu, epi=_epi_residual_loss,
        epi_extra=[(h1, out_tile(tm_e, D_MODEL)), vec(g2), (tgt, out_tile(tm_e, D_MODEL)), vec(final_g[None])],
        out_specs=[out_tile(tm_e, D_MODEL)] * 3 + [sums_spec] * 2)

    (d_f,) = _matmul("mlp_down_dx", dm2, w2_full, "nt", (n_lat_rows, D_FF, D_MODEL), (tm_up, 512, D_MODEL),
                     [((n_lat_rows, D_FF), BF16)],
                     epi=lambda acc, ft: (acc * 2.0 * jnp.maximum(ft.astype(F32), 0.0),),
                     epi_extra=[(f, out_tile(tm_up, 512))])
    tk_dw = min(2048, n_lat_rows)
    (g_w2,) = _matmul("mlp_down_dw", f_t, dm2, "nn", (D_FF, D_MODEL, n_lat_rows), (1024, D_MODEL, tk_dw),
                      [((D_FF, D_MODEL), F32)], a_fn=sq_relu)
    (g_w1,) = _matmul("mlp_up_dw", a2_t, d_f, "nn", (D_MODEL, D_FF, n_lat_rows), (D_MODEL, w1_cols, n_lat_rows),
                      [((NDEV, D_MODEL, w1_cols), F32)],
                      out_specs=[pl.BlockSpec((None, D_MODEL, w1_cols), lambda i, j, k: (j, 0, 0))])
    mlp_send, mlp_recv, mlp_src, mlp_land, mlp_token = _exchange_start(
        "scatter_mlp_start", [g_w1, g_w2.reshape(NDEV, D_FF // NDEV, D_MODEL)], [False] * 2)
    d_h1, dm1, *sums2 = _matmul(
        "mlp_up_dx", d_f, w1_g, "nt", (n_lat_rows, D_MODEL, D_FF), (tm_e, D_MODEL, 4 * w1_cols),
        [full_rows, ((n_lat_rows, D_MODEL), BF16)] + [sums] * 4, epi=_epi_norm_bwd,
        epi_extra=[(h1, out_tile(tm_e, D_MODEL)), (d_h2, out_tile(tm_e, D_MODEL)), (mlp_out, out_tile(tm_e, D_MODEL)),
                   vec(norm2_g), vec(sc2 + mlp_token[0:1, 0:1]), vec(g1)],
        b_spec=pl.BlockSpec((4, D_MODEL, w1_cols), lambda i, j, k: (k, 0, 0)), b_slabs=4,
        out_specs=[out_tile(tm_e, D_MODEL)] * 2 + [sums_spec] * 4)

    (d_ycat,) = _matmul("out_proj_dx", dm1, w_out_full, "nt", (n_lat_rows, D_MODEL, D_MODEL), (tm, D_MODEL, D_MODEL),
                        [((n_lat_rows, D_MODEL), F32)])
    (g_w_out,) = _matmul("out_proj_dw", ycat, dm1, "tn", (D_MODEL, D_MODEL, n_lat_rows), (D_MODEL, D_MODEL, 512),
                         [((D_MODEL, D_MODEL), F32)])

    dy, g_glu, dd8 = _glu_bwd(d_ycat, z_all, y_dir[0], y_dir[1], d_skip, glu_full, n_lat_rows)
    proj_send, proj_recv, proj_src, proj_land, proj_token = _exchange_start(
        "scatter_proj_start",
        [g_w_out.reshape(NDEV, D_MODEL // NDEV, D_MODEL), g_glu.reshape(NDEV, S5_WIDTH // NDEV, S5_WIDTH)], [False] * 2)
    perm = perm + proj_token[0:1, 0:1].astype(BF16)
    du, g_lam_re, g_lam_im, g_ldt, g_bt, g_cdiag = [], [], [], [], [], []
    for d in range(2):
        _, _, adj, bmat, cmat = disc[d]
        du_d, d_bdiag, d_cdiag, d_abar8 = _s5_scan_bwd(f"s5_scan_bwd{d}", d == 0, dy, z_all, states[d], bmat, cmat, adj,
                                                       perm, perm_t)
        du.append(du_d)
        d_bbar = jnp.transpose(d_bdiag.reshape(S5_BLOCKS, S5_GROUP, 2, NSTATE // S5_BLOCKS), (2, 1, 0, 3)).reshape(
            2 * S5_GROUP, NSTATE)
        d_lam8, d_bt = _s5_discretise_bwd(f"s5_disc_bwd{d}", lam_re[d], lam_im[d], ldt[d], bt_re[d], bt_im[d], d_abar8, d_bbar)
        g_lam_re.append(d_lam8[0].reshape(S5_GROUPS, S5_STATE))
        g_lam_im.append(d_lam8[1].reshape(S5_GROUPS, S5_STATE))
        g_ldt.append(d_lam8[2].reshape(S5_GROUPS, S5_STATE).sum(axis=-1))
        g_bt.append(d_bt)
        g_cdiag.append(d_cdiag)

    dhc_pad, conv_sums = _conv_bwd_norm(d_ycat, hc, conv_ln_g, conv_ln_b, n_lat_rows)
    d_v, d_gate, g_conv_w8 = _conv_bwd_taps(dhc_pad, hh_pad, z_all, conv_w_full, n_lat_rows)

    dz_all = _dz_assemble(du[0], du[1], dy, d_skip, d_v, d_gate, n_lat)
    (g_w_in_full,) = _matmul("in_proj_dw", a_all, dz_all, "tn", (D_MODEL, IN_COLS, n_rows), (D_MODEL, IN_COLS, tm_all),
                             [((D_MODEL, IN_COLS), F32)])
    g_w_in_parts = jnp.transpose(g_w_in_full.reshape(D_MODEL, NDEV, IN_COLS // NDEV), (1, 0, 2)).astype(BF16)
    win_send, win_recv, win_src, win_land, win_token = _exchange_start("scatter_w_in_start", [g_w_in_parts], [False])
    (d_a_all,) = _matmul("in_proj_dx", dz_all, w_in_full + win_token[0:1, 0:1].astype(BF16), "nt",
                         (n_rows, D_MODEL, IN_COLS), (tm_all, D_MODEL, IN_COLS), [((n_rows, D_MODEL), F32)])
    grad_x, sums1 = _norm_bwd("norm1_bwd", xs, d_a_all, 0, norm1_g, sc1, res=d_h1, aux=mix)
    (sums1c,) = _norm_bwd("norm1_bwd_ctx", cs, d_a_all, n_lat, norm1_g, modc[1:2])

    s1, s1c, s2 = sums1.sum(axis=1), sums1c.sum(axis=1), [p.sum(axis=(0, 1)) for p in sums2]
    d_mod = jnp.concatenate([s1[0], s1[1], s1[3], s2[0], s2[1], s2[3]])
    d_modc = jnp.concatenate([s1c[0], s1c[1], jnp.zeros((4 * D_MODEL,), F32)])
    (dmod_g,), _ = _exchange("gather_dmod", [jnp.stack([d_mod, d_modc])], [True])
    dmod16 = jnp.concatenate([dmod_g[:, 0], dmod_g[:, 1]])
    dmod16_loc = lax.dynamic_slice(dmod16, (0, me * ada_cols), (16, ada_cols))
    cond_bwd = jnp.concatenate([c_all, jnp.broadcast_to(c_ctx[None], (NDEV, D_MODEL))])
    g_ada_w, g_c_ctx8 = _ada_bwd(cond_bwd, dmod16_loc, ada_w[0], c_ctx[None])

    small_parts = dict(
        c_ctx=g_c_ctx8[0], ada_b=d_mod + d_modc, norm1_g=s1[2] + s1c[2],
        s5_lam_re=jnp.stack(g_lam_re), s5_lam_im=jnp.stack(g_lam_im), s5_log_dt=jnp.stack(g_ldt),
        s5_d=dd8.sum(axis=0), conv_b=conv_sums[0].sum(axis=0), conv_ln_g=conv_sums[1].sum(axis=0),
        conv_ln_b=conv_sums[2].sum(axis=0), norm2_g=s2[2], final_g=d_final_g8.sum(axis=(0, 1)))
    reduced_shapes = [(SMALL_PACKED_ROWS, D_MODEL), (2, 2 * S5_GROUP, NSTATE), (2,) + _S5_DIAG, (1,)]
    small_g = _pack_rows(
        [_pack_rows([small_parts[n] for n in SMALL], SMALL_PACKED_ROWS), jnp.stack(g_bt), jnp.stack(g_cdiag),
         (0.5 / D_MODEL * jnp.sum(err_sums)).reshape(1)], SMALL_ROWS).reshape(NDEV, SMALL_ROWS // NDEV, D_MODEL)
    g_conv_w_parts = jnp.transpose(g_conv_w8.sum(axis=1).reshape(CONV_K, NDEV, CONV_WIDTH // NDEV), (1, 0, 2))

    res = {}

    def own_chunk(src):
        return lax.dynamic_index_in_dim(src, me, 0, keepdims=False)

    def adamw_big(name, parts):
        outs = _adamw("adamw_" + name, weights[name][0], parts, mom1[name][0], mom2[name][0])
        res[name] = [o[None] for o in outs]
        return outs[0]

    sm_send, sm_recv, sm_src, sm_land, sm_token = _exchange_start("scatter_small_start", [g_conv_w_parts, small_g],
                                                                  [False] * 2)
    done = adamw_big("ada_w", g_ada_w[None] + sm_token[0:1, 0:1])
    mlp_src, mlp_landed = _exchange_wait("scatter_mlp_wait", mlp_send, mlp_recv, mlp_src, mlp_land, [False] * 2, done)
    p_w1, p_w2 = [_with_own(l, own_chunk(s), me) for l, s in zip(mlp_landed, mlp_src)]
    adamw_big("mlp_w1", p_w1)
    done = adamw_big("mlp_w2", p_w2)
    sm_src, sm_landed = _exchange_wait("scatter_small_wait", sm_send, sm_recv, sm_src, sm_land, [False] * 2, done)
    p_conv_w, p_small = [_with_own(l, own_chunk(s), me) for l, s in zip(sm_landed, sm_src)]
    ga_send, ga_recv, ga_src, ga_land, ga_token = _exchange_start("gather_small_start", [_sum_parts(p_small)], [True])
    proj_src, proj_landed = _exchange_wait("scatter_proj_wait", proj_send, proj_recv, proj_src, proj_land, [False] * 2,
                                           ga_token)
    p_w_out, p_glu = [_with_own(l, own_chunk(s), me) for l, s in zip(proj_landed, proj_src)]
    adamw_big("w_out", p_w_out)
    done = adamw_big("s5_w_glu", p_glu)
    win_src, win_landed = _exchange_wait("scatter_w_in_wait", win_send, win_recv, win_src, win_land, [False], done)
    adamw_big("w_in", _with_own(win_landed[0], own_chunk(win_src[0]), me))
    done = adamw_big("conv_w", p_conv_w)
    ga_own, ga_landed = _exchange_wait("gather_small_wait", ga_send, ga_recv, ga_src, ga_land, [True], done)
    small_all = _with_own(ga_landed[0], ga_own[0], me).reshape(1, SMALL_ROWS, D_MODEL)
    _, r_bt, r_cdiag, loss = _unpack_rows(small_all, reduced_shapes)
    loss = loss.reshape(())
    pack = lambda src: _pack_rows([src[n] for n in SMALL], SMALL_PACKED_ROWS)
    outs = _adamw("adamw_small", pack(weights), small_all, pack(mom1), pack(mom2))
    unpacked = [_unpack_rows(o, [weights[n].shape for n in SMALL]) for o in outs]
    for i, name in enumerate(SMALL):
        res[name] = [u[i] for u in unpacked]
    to_gph = lambda t: jnp.transpose(t.reshape(2, S5_GROUP, S5_GROUPS, S5_STATE), (0, 2, 3, 1))[None]
    r_c = jnp.transpose(r_cdiag.reshape(2, S5_BLOCKS, S5_GROUP, 2, groups_per_block, S5_STATE), (3, 0, 1, 4, 2, 5)).reshape(
        2, 1, 2, S5_GROUPS, S5_GROUP, S5_STATE)
    native = dict(s5_b_re=to_gph(r_bt[:, :S5_GROUP]), s5_b_im=to_gph(r_bt[:, S5_GROUP:]), s5_c_re=r_c[0], s5_c_im=-r_c[1])
    for name, grad in native.items():
        res[name] = [grad, *_adamw_native("adamw_" + name, weights[name], grad, mom1[name], mom2[name])]

    return (loss, grad_x[None], *[res[n][0] for n in order], *[res[n][1] for n in order],
            *[res[n][2] for n in order], *[res[n][3] for n in order])
```

```python
import functools

import jax
import jax.numpy as jnp
from jax import lax
from jax.experimental import pallas as pl
from jax.experimental.pallas import tpu as pltpu

F32 = jnp.float32
BF16 = jnp.bfloat16
MESH = pl.DeviceIdType.MESH
ANY = pl.BlockSpec(memory_space=pl.ANY)

NDEV = 8
D_MODEL = 1024
GRID_W = 64
S5_WIDTH = 512
S5_GROUP = 16
S5_GROUPS = 32
S5_STATE = 64
NSTATE = S5_GROUPS * S5_STATE
CONV_WIDTH = 512
CONV_K = 31
IN_COLS = S5_WIDTH + 2 * CONV_WIDTH
D_FF = 4 * D_MODEL
EPS_RMS = 1e-6
EPS_LN = 1e-5
ADAM_LR = 0.001
ADAM_B1 = 0.9
ADAM_B2 = 0.999
ADAM_EPS = 1e-08
ADAM_WD = 0.01
ADAM_STEP = 10

SUBLANES = 8
LANES = 128
ROW_BLOCK = 256
SCAN_LANES = 512
SCAN_UNROLL = 4
SEGMENTS = SUBLANES
STEPS = ROW_BLOCK // SEGMENTS
S5_BLOCKS = 4
S5_BLOCK_WIDTH = S5_WIDTH // S5_BLOCKS
CONV_ROWS = 64
CONV_BWD_ROWS = 32
VMEM_LIMIT = 48 * 1024 * 1024
SMALL_ROWS = 320


def _params(sem=None):
    kw = dict(vmem_limit_bytes=VMEM_LIMIT)
    if sem is not None:
        kw["dimension_semantics"] = sem
    return pltpu.CompilerParams(**kw)


def _sds(shape, dtype=F32):
    return jax.ShapeDtypeStruct(tuple(shape), dtype)


def _fold8(x):
    return x.reshape(x.shape[0] // SUBLANES, SUBLANES, x.shape[1]).sum(axis=0)


def _sigmoid(x):
    return 1.0 / (1.0 + jnp.exp(-x))


def _silu(x):
    return x * _sigmoid(x)


def _dsilu(x):
    s = _sigmoid(x)
    return s * (1.0 + x * (1.0 - s))


_GELU_C = 0.7978845608028654


def _gelu(x):
    return 0.5 * x * (1.0 + jnp.tanh(_GELU_C * (x + 0.044715 * x * x * x)))


def _dgelu(x):
    t = jnp.tanh(_GELU_C * (x + 0.044715 * x * x * x))
    return 0.5 * (1.0 + t) + 0.5 * x * (1.0 - t * t) * _GELU_C * (1.0 + 3.0 * 0.044715 * x * x)


def _rms(x):
    rstd = lax.rsqrt(jnp.mean(x * x, axis=-1, keepdims=True) + EPS_RMS)
    return x * rstd, rstd


def _epi_residual_prenorm(acc, res, gate, gain, scale, shift):
    h = res + gate * acc
    xh, _ = _rms(h)
    a = (xh * gain) * (1.0 + scale) + shift
    return acc, h, a, a.T


def _epi_residual_loss(acc, res, gate, target, gain):
    h = res + gate * acc
    xh, rstd = _rms(h)
    err = xh * gain - target
    dy = err * (1.0 / h.shape[-1])
    dxh = dy * gain
    dh = rstd * (dxh - xh * jnp.mean(dxh * xh, axis=-1, keepdims=True))
    return acc, dh, dh * gate, _fold8(err * err), _fold8(dy * xh)


def _epi_norm_bwd(d_act, x, res, aux, gain, scale, gate):
    xh, rstd = _rms(x)
    dn = d_act * (1.0 + scale)
    dxh = dn * gain
    dx = res + rstd * (dxh - xh * jnp.mean(dxh * xh, axis=-1, keepdims=True))
    return dx, dx * gate, _fold8(d_act), _fold8(d_act * (xh * gain)), _fold8(dn * xh), _fold8(res * aux)


def _dot(a, b, mode):
    dims = {"nn": (((1,), (0,)), ((), ())), "nt": (((1,), (1,)), ((), ())), "tn": (((0,), (0,)), ((), ()))}[mode]
    return lax.dot_general(a, b, dims, preferred_element_type=F32)


def _peers(x, y, c):
    out = []
    for k in range(1, NDEV):
        px = 1 - x if k & 4 else x
        py = 1 - y if k & 2 else y
        pc = 1 - c if k & 1 else c
        out.append(((px, py, pc), 4 * px + 2 * py + pc))
    return out


def _exchange_copies(src, land, send_sems, recv_sems, gather):
    x, y, c = lax.axis_index("x"), lax.axis_index("y"), lax.axis_index("c")
    me = 4 * x + 2 * y + c
    out = []
    for a in range(len(src)):
        for k, (peer, plin) in enumerate(_peers(x, y, c)):
            chunk = src[a] if gather[a] else src[a].at[plin]
            sems = dict(send_sem=send_sems.at[a * (NDEV - 1) + k], recv_sem=recv_sems.at[a * (NDEV - 1) + k],
                        device_id=peer, device_id_type=MESH)
            out.append((pltpu.make_async_remote_copy(src_ref=chunk, dst_ref=land[a].at[me], **sems),
                        pltpu.make_async_remote_copy(src_ref=chunk, dst_ref=land[a].at[plin], **sems)))
    return out


def _exchange(name, srcs, gather):
    n = len(srcs)
    outs = [_sds(((NDEV,) + s.shape) if g else s.shape, s.dtype) for s, g in zip(srcs, gather)]

    def body(*refs):
        src, dst, token = refs[:n], refs[n:2 * n], refs[2 * n]
        send_sems, recv_sems, local_sems = refs[2 * n + 1:]
        me = 4 * lax.axis_index("x") + 2 * lax.axis_index("y") + lax.axis_index("c")
        local = [pltpu.make_async_copy(src[a] if gather[a] else src[a].at[me], dst[a].at[me], local_sems.at[a])
                 for a in range(n)]
        for copy in local:
            copy.start()
        copies = _exchange_copies(src, dst, send_sems, recv_sems, gather)
        for copy, _ in copies:
            copy.start()
        token[...] = jnp.zeros_like(token)
        for copy, landing in copies:
            copy.wait_send()
            landing.wait_recv()
        for copy in local:
            copy.wait()

    nsem = n * (NDEV - 1)
    out = pl.pallas_call(
        body, name=name, out_shape=outs + [_sds((SUBLANES, LANES))], in_specs=[ANY] * n,
        out_specs=[ANY] * n + [pl.BlockSpec(memory_space=pltpu.VMEM)],
        scratch_shapes=[pltpu.SemaphoreType.DMA((nsem,)), pltpu.SemaphoreType.DMA((nsem,)), pltpu.SemaphoreType.DMA((n,))],
    )(*srcs)
    return out[:n], out[n]


HBM = pl.BlockSpec(memory_space=pltpu.HBM)
SEM = pl.BlockSpec(memory_space=pltpu.SEMAPHORE)
EFFECT = pltpu.SideEffectType.DATAFLOW_SIDE_EFFECTING


def _exchange_start(name, srcs, gather):
    n = len(srcs)
    lands = [lax.empty(((NDEV,) + s.shape) if g else s.shape, s.dtype) for s, g in zip(srcs, gather)]

    def body(*refs):
        src, land = refs[:n], refs[n:2 * n]
        send_sems, recv_sems = refs[2 * n], refs[2 * n + 1]
        token = refs[-1]
        for copy, _ in _exchange_copies(src, land, send_sems, recv_sems, gather):
            copy.start()
        token[...] = jnp.zeros_like(token)

    hbm = lambda v: pltpu.HBM(v.shape, v.dtype)
    nsem = n * (NDEV - 1)
    out = pl.pallas_call(
        body, name=name,
        out_shape=(pltpu.SemaphoreType.DMA((nsem,)), pltpu.SemaphoreType.DMA((nsem,)), *[hbm(v) for v in srcs],
                   *[hbm(v) for v in lands], _sds((SUBLANES, LANES))),
        in_specs=[HBM] * (2 * n), out_specs=(SEM, SEM, *([HBM] * (2 * n)), pl.BlockSpec(memory_space=pltpu.VMEM)),
        input_output_aliases={i: 2 + i for i in range(2 * n)},
        compiler_params=pltpu.CompilerParams(has_side_effects=EFFECT),
    )(*[pltpu.with_memory_space_constraint(v, pltpu.HBM) for v in list(srcs) + lands])
    return out[0], out[1], out[2:2 + n], out[2 + n:2 + 2 * n], out[-1]


def _exchange_wait(name, send_sems, recv_sems, srcs, lands, gather, after):
    n = len(srcs)

    def body(*refs):
        src, land = refs[:n], refs[n:2 * n]
        send_ref, recv_ref = refs[2 * n], refs[2 * n + 1]
        for copy, landing in _exchange_copies(src, land, send_ref, recv_ref, gather):
            copy.wait_send()
            landing.wait_recv()

    hbm = lambda v: pltpu.HBM(v.shape, v.dtype)
    out = pl.pallas_call(
        body, name=name, out_shape=[hbm(v) for v in list(srcs) + list(lands)],
        in_specs=[HBM] * (2 * n) + [SEM, SEM, ANY], out_specs=[HBM] * (2 * n),
        input_output_aliases={i: i for i in range(2 * n)},
        compiler_params=pltpu.CompilerParams(has_side_effects=EFFECT),
    )(*srcs, *lands, send_sems, recv_sems, after)
    return out[:n], out[n:]


def _with_own(landed, own, me):
    return lax.dynamic_update_slice(landed, own[None], (me,) + (0,) * own.ndim)


def _matmul(name, a, b, mode, mnk, tiles, outs, a_spec=None, b_spec=None, a_fn=None, a_extra=(),
            epi=None, epi_extra=(), out_specs=None, b_slabs=1):
    m_, n_, k_ = mnk
    tm, tn, tk = tiles
    nk = k_ // tk
    if a_spec is None:
        a_spec = (pl.BlockSpec((tk, tm), lambda i, j, k: (k, i)) if mode == "tn"
                  else pl.BlockSpec((tm, tk), lambda i, j, k: (i, k)))
    if b_spec is None:
        b_spec = (pl.BlockSpec((tn, tk), lambda i, j, k: (j, k)) if mode == "nt"
                  else pl.BlockSpec((tk, tn), lambda i, j, k: (k, j)))
    if out_specs is None:
        out_specs = [pl.BlockSpec((tm, tn), lambda i, j, k: (i, j)) for _ in outs]
    na, ne, no = len(a_extra), len(epi_extra), len(outs)

    def body(*refs):
        a_ref, b_ref = refs[0], refs[1]
        ax = refs[2:2 + na]
        ex = refs[2 + na:2 + na + ne]
        o = refs[2 + na + ne:2 + na + ne + no]

        def finish(res):
            res = epi(res, *[r[...] for r in ex]) if epi is not None else (res,)
            for ref, val in zip(o, res):
                ref[...] = val.astype(ref.dtype)

        at = a_ref[...]
        if a_fn is not None:
            at = a_fn(at, *[r[...] for r in ax])
        at = at.astype(BF16)
        if b_slabs == 1:
            part = _dot(at, b_ref[...].astype(BF16), mode)
        else:
            ks = tk // b_slabs
            part = _dot(at[:, 0:ks], b_ref[0].astype(BF16), mode)
            for s in range(1, b_slabs):
                part = part + _dot(at[:, s * ks:(s + 1) * ks], b_ref[s].astype(BF16), mode)
        if nk == 1:
            finish(part)
            return
        acc = refs[-1]
        k = pl.program_id(2)

        @pl.when(k == 0)
        def _():
            acc[...] = part

        @pl.when(k > 0)
        def _():
            acc[...] += part

        @pl.when(k == nk - 1)
        def _():
            finish(acc[...])

    return pl.pallas_call(
        body, name=name, grid=(m_ // tm, n_ // tn, nk),
        in_specs=[a_spec, b_spec] + [s for _, s in a_extra] + [s for _, s in epi_extra],
        out_specs=out_specs, out_shape=[_sds(s, d) for s, d in outs],
        scratch_shapes=[pltpu.VMEM((tm, tn), F32)] if nk > 1 else [],
        compiler_params=_params(("parallel", "parallel", "arbitrary")),
    )(a, b, *[x for x, _ in a_extra], *[x for x, _ in epi_extra])


def _prenorm(name, x, ctx, gain, shsc):
    n_lat = x.shape[0] // ROW_BLOCK
    n_ctx = 0 if ctx is None else ctx.shape[0] // ROW_BLOCK
    d = x.shape[1]

    def norm(src, g_ref, m_ref, o_ref):
        xv = src[...]
        xh = xv * lax.rsqrt(jnp.mean(xv * xv, axis=-1, keepdims=True) + EPS_RMS)
        o_ref[...] = ((xh * g_ref[...]) * (1.0 + m_ref[1:2, :]) + m_ref[0:1, :]).astype(o_ref.dtype)

    def body(*refs):
        if ctx is None:
            x_ref, g_ref, m_ref, o_ref = refs
            norm(x_ref, g_ref, m_ref, o_ref)
        else:
            x_ref, c_ref, g_ref, m_ref, o_ref = refs
            i = pl.program_id(0)

            @pl.when(i < n_lat)
            def _():
                norm(x_ref, g_ref, m_ref, o_ref)

            @pl.when(i >= n_lat)
            def _():
                norm(c_ref, g_ref, m_ref, o_ref)

    in_specs = [pl.BlockSpec((ROW_BLOCK, d), lambda i: (jnp.minimum(i, n_lat - 1), 0))]
    args = [x]
    if ctx is not None:
        in_specs.append(pl.BlockSpec((ROW_BLOCK, d), lambda i: (jnp.maximum(i - n_lat, 0), 0)))
        args.append(ctx)
    in_specs += [pl.BlockSpec((1, d), lambda i: (0, 0)),
                 pl.BlockSpec((None, 2, d), lambda i: (jnp.minimum(i // n_lat, 1), 0, 0))]
    args += [gain, shsc]
    return pl.pallas_call(
        body, name=name, grid=(n_lat + n_ctx,), in_specs=in_specs,
        out_specs=pl.BlockSpec((ROW_BLOCK, d), lambda i: (i, 0)),
        out_shape=_sds(((n_lat + n_ctx) * ROW_BLOCK, d), BF16),
        compiler_params=_params(("parallel",)),
    )(*args)


def _norm_bwd(name, x, d_act, d_act_row0, gain, scale, res=None, aux=None, gate=None):
    rows, d = x.shape
    nb = rows // ROW_BLOCK
    has_res = res is not None
    has_gate = gate is not None

    def body(*refs):
        if has_gate:
            x_ref, da_ref, g_ref, sc_ref, r_ref, aux_ref, gate_ref, dx_ref, dm_ref, sums = refs
        elif has_res:
            x_ref, da_ref, g_ref, sc_ref, r_ref, aux_ref, dx_ref, sums = refs
        else:
            x_ref, da_ref, g_ref, sc_ref, sums = refs
        i = pl.program_id(0)

        @pl.when(i == 0)
        def _():
            sums[...] = jnp.zeros_like(sums)

        xv, da = x_ref[...], da_ref[...]
        rstd = lax.rsqrt(jnp.mean(xv * xv, axis=-1, keepdims=True) + EPS_RMS)
        xh = xv * rstd
        g = g_ref[...]
        dn = da * (1.0 + sc_ref[...])
        sums[0] += _fold8(da)
        sums[1] += _fold8(da * (xh * g))
        sums[2] += _fold8(dn * xh)
        if has_res:
            dxh = dn * g
            dx = rstd * (dxh - xh * jnp.mean(dxh * xh, axis=-1, keepdims=True))
            rv = r_ref[...]
            dx_ref[...] = rv + dx
            sums[3] += _fold8(rv * aux_ref[...])
            if has_gate:
                dm_ref[...] = ((rv + dx) * gate_ref[...]).astype(dm_ref.dtype)

    row = lambda i: (i, 0)
    vec = pl.BlockSpec((1, d), lambda i: (0, 0))
    in_specs = [pl.BlockSpec((ROW_BLOCK, d), row), pl.BlockSpec((ROW_BLOCK, d), lambda i: (i + d_act_row0, 0)), vec, vec]
    args = [x, d_act, gain, scale]
    out_shape = [_sds((4, SUBLANES, d))]
    out_specs = [pl.BlockSpec((4, SUBLANES, d), lambda i: (0, 0, 0))]
    if has_res:
        in_specs += [pl.BlockSpec((ROW_BLOCK, d), row), pl.BlockSpec((ROW_BLOCK, d), row)]
        args += [res, aux]
        if has_gate:
            in_specs.append(vec)
            args.append(gate)
            out_shape = [_sds((rows, d), BF16)] + out_shape
            out_specs = [pl.BlockSpec((ROW_BLOCK, d), row)] + out_specs
        out_shape = [_sds((rows, d))] + out_shape
        out_specs = [pl.BlockSpec((ROW_BLOCK, d), row)] + out_specs
    return pl.pallas_call(
        body, name=name, grid=(nb,), in_specs=in_specs, out_specs=out_specs, out_shape=out_shape,
        compiler_params=_params(("arbitrary",)),
    )(*args)


def _loss_head(h2, target, gain, gate):
    rows, d = h2.shape

    def body(h_ref, t_ref, g_ref, gate_ref, dh_ref, dm_ref, err_ref, dg_ref):
        i = pl.program_id(0)

        @pl.when(i == 0)
        def _():
            err_ref[...] = jnp.zeros_like(err_ref)
            dg_ref[...] = jnp.zeros_like(dg_ref)

        hv = h_ref[...]
        rstd = lax.rsqrt(jnp.mean(hv * hv, axis=-1, keepdims=True) + EPS_RMS)
        xh = hv * rstd
        g = g_ref[...]
        err = xh * g - t_ref[...]
        err_ref[...] += _fold8(err * err)
        dy = err * (1.0 / d)
        dg_ref[...] += _fold8(dy * xh)
        dxh = dy * g
        dh = rstd * (dxh - xh * jnp.mean(dxh * xh, axis=-1, keepdims=True))
        dh_ref[...] = dh
        dm_ref[...] = (dh * gate_ref[...]).astype(dm_ref.dtype)

    row = pl.BlockSpec((ROW_BLOCK, d), lambda i: (i, 0))
    acc = pl.BlockSpec((SUBLANES, d), lambda i: (0, 0))
    vec = pl.BlockSpec((1, d), lambda i: (0, 0))
    return pl.pallas_call(
        body, name="loss_head", grid=(rows // ROW_BLOCK,),
        in_specs=[row, row, vec, vec], out_specs=[row, row, acc, acc],
        out_shape=[_sds((rows, d)), _sds((rows, d), BF16), _sds((SUBLANES, d)), _sds((SUBLANES, d))],
        compiler_params=_params(("arbitrary",)),
    )(h2, target, gain, gate)


def _ada_fwd(cond16, ada_w_loc, ada_b_loc):
    cols = ada_w_loc.shape[1]

    def body(c_ref, w_ref, b_ref, o_ref):
        s = _silu(c_ref[...]).astype(BF16)
        o_ref[...] = _dot(s, w_ref[...].astype(BF16), "nn") + b_ref[...]

    return pl.pallas_call(body, name="ada_fwd", out_shape=_sds((16, cols)), compiler_params=_params())(
        cond16, ada_w_loc, ada_b_loc)


def _ada_bwd(cond16, dmod16, ada_w_loc, c_ctx_row):
    k_, cols = ada_w_loc.shape

    def body(c_ref, dm_ref, w_ref, cc_ref, gw_ref, gc_ref):
        s = _silu(c_ref[...]).astype(BF16)
        dm = dm_ref[...]
        gw_ref[...] = _dot(s, dm.astype(BF16), "tn")
        dmc = jnp.sum(dm[8:16, :], axis=0, keepdims=True)
        dmc8 = jnp.broadcast_to(dmc, (SUBLANES, cols)).astype(BF16)
        ds = _dot(dmc8, w_ref[...].astype(BF16), "nt")
        row = lax.broadcasted_iota(jnp.int32, ds.shape, 0)
        gc_ref[...] = jnp.where(row == 0, ds * _dsilu(cc_ref[...]), 0.0)

    return pl.pallas_call(body, name="ada_bwd", out_shape=[_sds((k_, cols)), _sds((SUBLANES, k_))],
                          compiler_params=_params())(cond16, dmod16, ada_w_loc, c_ctx_row)


def _cmul(a, b):
    return a[0] * b[0] - a[1] * b[1], a[0] * b[1] + a[1] * b[0]


def _disc(lam_re, lam_im, ldt):
    dt = jnp.exp(ldt)
    mag = jnp.exp(lam_re * dt)
    th = lam_im * dt
    a_re, a_im = mag * jnp.cos(th), mag * jnp.sin(th)
    den = lam_re * lam_re + lam_im * lam_im
    n_re = a_re - 1.0
    f_re = (n_re * lam_re + a_im * lam_im) / den
    f_im = (a_im * lam_re - n_re * lam_im) / den
    return dt, mag, th, a_re, a_im, den, n_re, f_re, f_im


def _block_diag_mask(shape):
    row = lax.broadcasted_iota(jnp.int32, shape, 0)
    col = lax.broadcasted_iota(jnp.int32, shape, 1)
    return lax.shift_right_logical(row, 4) == lax.shift_right_logical(col, 6)


TAB_A = 0
TAB_BIG = 1
TAB_SEG = 4
TAB_PW = 5
TAB_ROWS = TAB_PW + STEPS


def _s5_discretise(name, ascending, lam_re, lam_im, ldt, bt_re, bt_im, ct_re, ct_im):
    def write_tables(ref, pw, big, asc, sign):
        row = lax.broadcasted_iota(jnp.int32, (SUBLANES, NSTATE), 0)
        full = lambda v: jnp.broadcast_to(v, (SUBLANES, NSTATE))

        def put(t, p):
            ref[0, t] = full(p[0])
            ref[1, t] = full(sign * p[1])

        put(TAB_A, pw[0])
        for t in range(3):
            put(TAB_BIG + t, big[t])
        seg = [big[0]]
        for _ in range(SEGMENTS - 1):
            seg.append(_cmul(seg[-1], big[0]))
        seg_re = jnp.zeros((SUBLANES, NSTATE), F32)
        seg_im = jnp.zeros((SUBLANES, NSTATE), F32)
        for r in range(SEGMENTS):
            p = seg[r] if asc else seg[SEGMENTS - 1 - r]
            seg_re = jnp.where(row == r, p[0], seg_re)
            seg_im = jnp.where(row == r, sign * p[1], seg_im)
        ref[0, TAB_SEG] = seg_re
        ref[1, TAB_SEG] = seg_im
        for k in range(STEPS):
            put(TAB_PW + k, pw[k])

    def body(lr_ref, li_ref, ldt_ref, br_ref, bi_ref, cr_ref, ci_ref, bb_ref, tab_ref, adj_ref, bm_ref, cm_ref):
        _, _, _, a_re, a_im, _, _, f_re, f_im = _disc(lr_ref[...], li_ref[...], ldt_ref[...])
        bre, bim = br_ref[...], bi_ref[...]
        bb_re = f_re * bre - f_im * bim
        bb_im = f_re * bim + f_im * bre
        bb_ref[0:S5_GROUP, :] = bb_re
        bb_ref[S5_GROUP:2 * S5_GROUP, :] = bb_im
        pw = [(a_re, a_im)]
        for _ in range(STEPS - 1):
            pw.append(_cmul(pw[-1], (a_re, a_im)))
        big = [pw[STEPS - 1]]
        for _ in range(2):
            big.append(_cmul(big[-1], big[-1]))
        write_tables(tab_ref, pw, big, ascending, 1.0)
        write_tables(adj_ref, pw, big, not ascending, -1.0)
        half = NSTATE // S5_BLOCKS
        mask = _block_diag_mask((S5_BLOCK_WIDTH, half))
        tile = lambda v: jnp.broadcast_to(v[None], (S5_BLOCK_WIDTH // S5_GROUP, S5_GROUP, half)).reshape(S5_BLOCK_WIDTH, half)
        for c in range(S5_BLOCKS):
            cols = slice(c * half, (c + 1) * half)
            rows = slice(c * S5_BLOCK_WIDTH, (c + 1) * S5_BLOCK_WIDTH)
            bm_ref[c, :, 0:half] = jnp.where(mask, tile(bb_re[:, cols]), 0.0).astype(BF16)
            bm_ref[c, :, half:2 * half] = jnp.where(mask, tile(bb_im[:, cols]), 0.0).astype(BF16)
            cm_ref[c, :, 0:half] = jnp.where(mask, cr_ref[rows, :], 0.0).astype(BF16)
            cm_ref[c, :, half:2 * half] = jnp.where(mask, -ci_ref[rows, :], 0.0).astype(BF16)

    blocked = _sds((S5_BLOCKS, S5_BLOCK_WIDTH, 2 * NSTATE // S5_BLOCKS), BF16)
    return pl.pallas_call(
        body, name=name,
        out_shape=[_sds((2 * S5_GROUP, NSTATE)), _sds((2, TAB_ROWS, SUBLANES, NSTATE)),
                   _sds((2, TAB_ROWS, SUBLANES, NSTATE)), blocked, blocked],
        compiler_params=_params(),
    )(lam_re, lam_im, ldt, bt_re, bt_im, ct_re, ct_im)


def _s5_discretise_bwd(name, lam_re, lam_im, ldt, bt_re, bt_im, d_abar8, d_bbar):
    def body(lr_ref, li_ref, ldt_ref, br_ref, bi_ref, da_ref, db_ref, dl_ref, dbt_ref):
        lam_re, lam_im = lr_ref[...], li_ref[...]
        dt, mag, _, a_re, a_im, den, n_re, f_re, f_im = _disc(lam_re, lam_im, ldt_ref[...])
        bre, bim = br_ref[...], bi_ref[...]
        dbr, dbi = db_ref[0:S5_GROUP, :], db_ref[S5_GROUP:2 * S5_GROUP, :]
        dbt_ref[0:S5_GROUP, :] = f_re * dbr + f_im * dbi
        dbt_ref[S5_GROUP:2 * S5_GROUP, :] = f_re * dbi - f_im * dbr
        df_re = jnp.sum(bre * dbr + bim * dbi, axis=0, keepdims=True)
        df_im = jnp.sum(bre * dbi - bim * dbr, axis=0, keepdims=True)
        da = da_ref[...]
        da_re = jnp.sum(da[:, 0:NSTATE], axis=0, keepdims=True)
        da_im = jnp.sum(da[:, NSTATE:2 * NSTATE], axis=0, keepdims=True)
        da_re = da_re + (df_re * lam_re - df_im * lam_im) / den
        da_im = da_im + (df_re * lam_im + df_im * lam_re) / den
        ff = (f_re * df_re + f_im * df_im) * 2.0 / den
        d_lr = (df_re * n_re + df_im * a_im) / den - ff * lam_re
        d_li = (df_re * a_im - df_im * n_re) / den - ff * lam_im
        d_mag = (da_re * a_re + da_im * a_im) / mag
        d_th = da_im * a_re - da_re * a_im
        d_lr = d_lr + d_mag * mag * dt
        d_li = d_li + d_th * dt
        d_ldt = (d_mag * mag * lam_re + d_th * lam_im) * dt
        row = lax.broadcasted_iota(jnp.int32, (SUBLANES, NSTATE), 0)
        dl_ref[...] = jnp.where(row == 0, d_lr, jnp.where(row == 1, d_li, jnp.where(row == 2, d_ldt, 0.0)))

    return pl.pallas_call(
        body, name=name, out_shape=[_sds((SUBLANES, NSTATE)), _sds((2 * S5_GROUP, NSTATE))],
        compiler_params=_params(),
    )(lam_re, lam_im, ldt, bt_re, bt_im, d_abar8, d_bbar)


def _segment_permutation():
    rho = jnp.arange(ROW_BLOCK)
    src = STEPS * (rho % SEGMENTS) + rho // SEGMENTS
    return (src[:, None] == jnp.arange(ROW_BLOCK)[None, :]).astype(BF16)


def _permute_rows(perm_ref, v):
    return _dot(perm_ref[...], v, "nn").astype(BF16)


def _unpermute_rows(perm_t_ref, v):
    hi = v.astype(BF16)
    lo = (v - hi.astype(F32)).astype(BF16)
    return _dot(perm_t_ref[...], hi, "nn") + _dot(perm_t_ref[...], lo, "nn")


def _unrolled_loop(step, init):
    def trip(o, state):
        for u in range(SCAN_UNROLL):
            state = step(o * SCAN_UNROLL + u, state)
        return state

    return lax.fori_loop(0, STEPS // SCAN_UNROLL, trip, init)


def _scan_chunk(x_ref, out_ref, tab_ref, carry_re, carry_im, ascending, pair_ref=None, acc_ref=None):
    w = SCAN_LANES
    half = NSTATE // S5_BLOCKS
    row = lax.broadcasted_iota(jnp.int32, (SUBLANES, w), 0)
    last = (SEGMENTS - 1) if ascending else 0

    def from_previous_segment(v, k, fill):
        if ascending:
            return jnp.where(row >= k, pltpu.roll(v, k, 0), fill)
        return jnp.where(row < SEGMENTS - k, pltpu.roll(v, SEGMENTS - k, 0), fill)

    def tile_rows(k):
        return pl.ds(pl.multiple_of((k if ascending else STEPS - 1 - k) * SUBLANES, SUBLANES), SUBLANES)

    for j in range(NSTATE // w):
        n_l = pl.ds(j * w, w)
        lane0 = (j * w // half) * 2 * half + (j * w) % half
        re_l, im_l = pl.ds(lane0, w), pl.ds(lane0 + half, w)
        tab = lambda t, n_l=n_l: (tab_ref[0, t, :, n_l], tab_ref[1, t, :, n_l])
        a_re, a_im = tab(TAB_A)

        def local_step(k, h):
            rs = tile_rows(k)
            h_re = a_re * h[0] - a_im * h[1] + x_ref[rs, re_l]
            h_im = a_re * h[1] + a_im * h[0] + x_ref[rs, im_l]
            out_ref[rs, re_l] = h_re
            out_ref[rs, im_l] = h_im
            return h_re, h_im

        zero = jnp.zeros((SUBLANES, w), F32)
        end_re, end_im = lax.fori_loop(0, STEPS, local_step, (zero, zero))
        for t, k in ((TAB_BIG, 1), (TAB_BIG + 1, 2), (TAB_BIG + 2, 4)):
            p_re, p_im = tab(t)
            s_re, s_im = from_previous_segment(end_re, k, 0.0), from_previous_segment(end_im, k, 0.0)
            end_re, end_im = end_re + (p_re * s_re - p_im * s_im), end_im + (p_re * s_im + p_im * s_re)
        c0_re, c0_im = carry_re[:, n_l], carry_im[:, n_l]
        p_re, p_im = tab(TAB_SEG)
        end_re = end_re + (p_re * c0_re - p_im * c0_im)
        end_im = end_im + (p_re * c0_im + p_im * c0_re)
        carry_re[:, n_l] = jnp.broadcast_to(end_re[last:last + 1, :], end_re.shape)
        carry_im[:, n_l] = jnp.broadcast_to(end_im[last:last + 1, :], end_im.shape)
        in_re = from_previous_segment(end_re, 1, c0_re)
        in_im = from_previous_segment(end_im, 1, c0_im)

        def carry_step(k, st):
            rs = tile_rows(k)
            p_re, p_im = tab_ref[0, TAB_PW + k, :, n_l], tab_ref[1, TAB_PW + k, :, n_l]
            o_re = out_ref[rs, re_l] + (p_re * in_re - p_im * in_im)
            o_im = out_ref[rs, im_l] + (p_re * in_im + p_im * in_re)
            out_ref[rs, re_l] = o_re
            out_ref[rs, im_l] = o_im
            if pair_ref is None:
                return st
            s_re, s_im = pair_ref[rs, re_l], pair_ref[rs, im_l]
            return (o_re, o_im, st[2] + (st[0] * s_re + st[1] * s_im), st[3] + (st[1] * s_re - st[0] * s_im))

        if pair_ref is None:
            _unrolled_loop(carry_step, 0)
        else:
            fin = _unrolled_loop(carry_step, (in_re, in_im, zero, zero))
            acc_ref[:, n_l] += fin[2]
            acc_ref[:, pl.ds(NSTATE + j * w, w)] += fin[3]


def _scan_block_index(i, n_lat, ctx_first_then_ascending):
    if ctx_first_then_ascending:
        return jnp.where(i == 0, n_lat, i - 1)
    return jnp.where(i == 0, n_lat, n_lat - i)


def _full_spec(shape):
    return pl.BlockSpec(shape, lambda i: (0,) * len(shape))


_S5_BLOCKED = (S5_BLOCKS, S5_BLOCK_WIDTH, 2 * NSTATE // S5_BLOCKS)
_S5_TABLES = (2, TAB_ROWS, SUBLANES, NSTATE)
_S5_DIAG = (S5_BLOCKS, S5_GROUP, 2 * NSTATE // S5_BLOCKS)


def _s5_scan_fwd(name, ascending, z_all, bmat, cmat, tab, perm, perm_t):
    rows = z_all.shape[0]
    nb = rows // ROW_BLOCK
    n_lat = nb - 1
    bw, sw = S5_BLOCK_WIDTH, 2 * NSTATE // S5_BLOCKS

    def body(u_ref, bm_ref, cm_ref, tab_ref, p_ref, pt_ref, s_ref, y_ref, bu, yp, carry_re, carry_im):
        @pl.when(pl.program_id(0) == 0)
        def _():
            carry_re[...] = jnp.zeros_like(carry_re)
            carry_im[...] = jnp.zeros_like(carry_im)

        up = _permute_rows(p_ref, u_ref[...].astype(BF16))
        for c in range(S5_BLOCKS):
            bu[:, c * sw:(c + 1) * sw] = _dot(up[:, c * bw:(c + 1) * bw], bm_ref[c], "nn")
        _scan_chunk(bu, s_ref, tab_ref, carry_re, carry_im, ascending)
        for c in range(S5_BLOCKS):
            yp[:, c * bw:(c + 1) * bw] = _dot(s_ref[:, c * sw:(c + 1) * sw].astype(BF16), cm_ref[c], "nt")
        y_ref[...] = _unpermute_rows(pt_ref, yp[...])

    blk = lambda i: (_scan_block_index(i, n_lat, ascending), 0)
    return pl.pallas_call(
        body, name=name, grid=(nb,),
        in_specs=[pl.BlockSpec((ROW_BLOCK, S5_WIDTH), blk), _full_spec(_S5_BLOCKED), _full_spec(_S5_BLOCKED),
                  _full_spec(_S5_TABLES), _full_spec((ROW_BLOCK, ROW_BLOCK)), _full_spec((ROW_BLOCK, ROW_BLOCK))],
        out_specs=[pl.BlockSpec((ROW_BLOCK, 2 * NSTATE), blk), pl.BlockSpec((ROW_BLOCK, S5_WIDTH), blk)],
        out_shape=[_sds((rows, 2 * NSTATE)), _sds((rows, S5_WIDTH))],
        scratch_shapes=[pltpu.VMEM((ROW_BLOCK, 2 * NSTATE), F32), pltpu.VMEM((ROW_BLOCK, S5_WIDTH), F32),
                        pltpu.VMEM((SUBLANES, NSTATE), F32), pltpu.VMEM((SUBLANES, NSTATE), F32)],
        compiler_params=_params(("arbitrary",)),
    )(z_all, bmat, cmat, tab, perm, perm_t)


def _s5_scan_bwd(name, ascending, dy, z_all, states, bmat, cmat, adj, perm, perm_t):
    rows = states.shape[0]
    nb = rows // ROW_BLOCK
    n_lat = nb - 1
    bw, sw = S5_BLOCK_WIDTH, 2 * NSTATE // S5_BLOCKS

    def block_index(i):
        if ascending:
            return jnp.where(i == nb - 1, n_lat, n_lat - 1 - i)
        return jnp.where(i == nb - 1, n_lat, i)

    def body(dy_ref, u_ref, s_ref, bm_ref, cm_ref, adj_ref, p_ref, pt_ref, du_ref, db_ref, dc_ref, da_ref,
             g, dup, db_acc, dc_acc, carry_re, carry_im):
        i = pl.program_id(0)

        @pl.when(i == 0)
        def _():
            carry_re[...] = jnp.zeros_like(carry_re)
            carry_im[...] = jnp.zeros_like(carry_im)
            da_ref[...] = jnp.zeros_like(da_ref)
            db_acc[...] = jnp.zeros_like(db_acc)
            dc_acc[...] = jnp.zeros_like(dc_acc)

        @pl.when(i < nb - 1)
        def _():
            dyp = _permute_rows(p_ref, dy_ref[...].astype(BF16))
            for c in range(S5_BLOCKS):
                g[:, c * sw:(c + 1) * sw] = _dot(dyp[:, c * bw:(c + 1) * bw], cm_ref[c], "nn")
                dc_acc[c] += _dot(dyp[:, c * bw:(c + 1) * bw], s_ref[:, c * sw:(c + 1) * sw].astype(BF16), "tn")

        @pl.when(i == nb - 1)
        def _():
            g[...] = jnp.zeros_like(g)

        _scan_chunk(g, g, adj_ref, carry_re, carry_im, not ascending, pair_ref=s_ref, acc_ref=da_ref)
        up = _permute_rows(p_ref, u_ref[...].astype(BF16))
        for c in range(S5_BLOCKS):
            gc = g[:, c * sw:(c + 1) * sw].astype(BF16)
            dup[:, c * bw:(c + 1) * bw] = _dot(gc, bm_ref[c], "nt")
            db_acc[c] += _dot(up[:, c * bw:(c + 1) * bw], gc, "tn")
        du_ref[...] = _unpermute_rows(pt_ref, dup[...])

        @pl.when(i == nb - 1)
        def _():
            mask = _block_diag_mask((bw, sw // 2))
            for acc, out in ((db_acc, db_ref), (dc_acc, dc_ref)):
                for c in range(S5_BLOCKS):
                    for part in range(2):
                        cols = slice(part * (sw // 2), (part + 1) * (sw // 2))
                        kept = jnp.where(mask, acc[c, :, cols], 0.0)
                        out[c, :, cols] = kept.reshape(bw // S5_GROUP, S5_GROUP, sw // 2).sum(axis=0)

    blk = lambda i: (block_index(i), 0)
    return pl.pallas_call(
        body, name=name, grid=(nb,),
        in_specs=[pl.BlockSpec((ROW_BLOCK, S5_WIDTH), lambda i: (jnp.minimum(block_index(i), n_lat - 1), 0)),
                  pl.BlockSpec((ROW_BLOCK, S5_WIDTH), blk), pl.BlockSpec((ROW_BLOCK, 2 * NSTATE), blk),
                  _full_spec(_S5_BLOCKED), _full_spec(_S5_BLOCKED), _full_spec(_S5_TABLES),
                  _full_spec((ROW_BLOCK, ROW_BLOCK)), _full_spec((ROW_BLOCK, ROW_BLOCK))],
        out_specs=[pl.BlockSpec((ROW_BLOCK, S5_WIDTH), blk), _full_spec(_S5_DIAG), _full_spec(_S5_DIAG),
                   _full_spec((SUBLANES, 2 * NSTATE))],
        out_shape=[_sds((rows, S5_WIDTH)), _sds(_S5_DIAG), _sds(_S5_DIAG), _sds((SUBLANES, 2 * NSTATE))],
        scratch_shapes=[pltpu.VMEM((ROW_BLOCK, 2 * NSTATE), F32), pltpu.VMEM((ROW_BLOCK, S5_WIDTH), F32),
                        pltpu.VMEM(_S5_BLOCKED, F32), pltpu.VMEM(_S5_BLOCKED, F32),
                        pltpu.VMEM((SUBLANES, NSTATE), F32), pltpu.VMEM((SUBLANES, NSTATE), F32)],
        compiler_params=_params(("arbitrary",)),
    )(dy, z_all, states, bmat, cmat, adj, perm, perm_t)


def _glu_fwd(z_all, y0, y1, d_skip, w_glu, n_rows):
    def body(u_ref, y0_ref, y1_ref, d_ref, w_ref, o_ref):
        y = d_ref[...] * u_ref[...] + y0_ref[...] + y1_ref[...]
        g = _gelu(y)
        t = _dot(g.astype(BF16), w_ref[...], "nn")
        o_ref[...] = (g * _sigmoid(t)).astype(o_ref.dtype)

    row = pl.BlockSpec((ROW_BLOCK, S5_WIDTH), lambda i: (i, 0))
    return pl.pallas_call(
        body, name="glu_fwd", grid=(n_rows // ROW_BLOCK,),
        in_specs=[row, row, row, pl.BlockSpec((1, S5_WIDTH), lambda i: (0, 0)),
                  pl.BlockSpec((S5_WIDTH, S5_WIDTH), lambda i: (0, 0))],
        out_specs=row, out_shape=_sds((n_rows, S5_WIDTH + CONV_WIDTH), BF16), compiler_params=_params(("parallel",)),
    )(z_all, y0, y1, d_skip, w_glu)


def _glu_bwd(d_ycat, z_all, y0, y1, d_skip, w_glu, n_rows):
    def body(do_ref, u_ref, y0_ref, y1_ref, d_ref, w_ref, dy_ref, dw_ref, dd_ref):
        @pl.when(pl.program_id(0) == 0)
        def _():
            dw_ref[...] = jnp.zeros_like(dw_ref)
            dd_ref[...] = jnp.zeros_like(dd_ref)

        u = u_ref[...]
        y = d_ref[...] * u + y0_ref[...] + y1_ref[...]
        g = _gelu(y)
        gb = g.astype(BF16)
        w = w_ref[...]
        sg = _sigmoid(_dot(gb, w, "nn"))
        do = do_ref[...]
        dt = do * g * sg * (1.0 - sg)
        dtb = dt.astype(BF16)
        dg = do * sg + _dot(dtb, w, "nt")
        dy = dg * _dgelu(y)
        dy_ref[...] = dy
        dw_ref[...] += _dot(gb, dtb, "tn")
        dd_ref[...] += _fold8(dy * u)

    row = pl.BlockSpec((ROW_BLOCK, S5_WIDTH), lambda i: (i, 0))
    sq = pl.BlockSpec((S5_WIDTH, S5_WIDTH), lambda i: (0, 0))
    return pl.pallas_call(
        body, name="glu_bwd", grid=(n_rows // ROW_BLOCK,),
        in_specs=[row, row, row, row, pl.BlockSpec((1, S5_WIDTH), lambda i: (0, 0)), sq],
        out_specs=[row, sq, pl.BlockSpec((SUBLANES, S5_WIDTH), lambda i: (0, 0))],
        out_shape=[_sds((n_rows, S5_WIDTH)), _sds((S5_WIDTH, S5_WIDTH)), _sds((SUBLANES, S5_WIDTH))],
        compiler_params=_params(("arbitrary",)),
    )(d_ycat, z_all, y0, y1, d_skip, w_glu)


CONV_HALF = CONV_K // 2


def _conv_block(n_rows):
    blk = min(1024, n_rows)
    assert blk >= CONV_HALF * GRID_W and n_rows % blk == 0
    return blk


def _conv_gate(z_all, n_rows):
    blk = _conv_block(n_rows)
    nb = n_rows // blk

    def body(v_ref, g_ref, o_ref):
        i = pl.program_id(0)
        inside = jnp.logical_and(i >= 1, i <= nb)

        @pl.when(inside)
        def _():
            o_ref[...] = v_ref[...] * _sigmoid(g_ref[...])

        @pl.when(jnp.logical_not(inside))
        def _():
            o_ref[...] = jnp.zeros_like(o_ref)

    src = lambda col: pl.BlockSpec((blk, CONV_WIDTH), lambda i: (jnp.clip(i - 1, 0, nb - 1), col))
    return pl.pallas_call(
        body, name="conv_gate", grid=(nb + 2,), in_specs=[src(1), src(2)],
        out_specs=pl.BlockSpec((blk, CONV_WIDTH), lambda i: (i, 0)),
        out_shape=_sds(((nb + 2) * blk, CONV_WIDTH)), compiler_params=_params(("parallel",)),
    )(z_all, z_all)


def _stream_padded(pad_ref, buf, sems, blk, n_blocks):
    i = pl.program_id(0)

    def copy(b):
        rows = pl.ds(pl.multiple_of(b * blk, blk), blk)
        return pltpu.make_async_copy(pad_ref.at[rows, :], buf.at[rows, :], sems.at[b])

    @pl.when(i == 0)
    def _():
        for b in range(n_blocks):
            copy(b).start()
        copy(0).wait()
        copy(1).wait()

    copy(i + 2).wait()
    return pl.multiple_of(i * blk, blk)


def _conv_fwd(hh_pad, w, b, ln_g, ln_b, ycat, n_rows):
    blk = _conv_block(n_rows)
    nblk = n_rows // blk + 2

    def body(hh_ref, w_ref, b_ref, g_ref, lb_ref, ycat_ref, hc_ref, y_ref, win, sems):
        base = _stream_padded(hh_ref, win, sems, blk, nblk)

        def tile(t, _):
            r0 = pl.multiple_of(t * CONV_ROWS, CONV_ROWS)
            acc = jnp.zeros((CONV_ROWS, CONV_WIDTH), F32)
            for k in range(CONV_K):
                acc = acc + w_ref[k:k + 1, :] * win[pl.ds(base + r0 + blk + (k - CONV_HALF) * GRID_W, CONV_ROWS), :]
            hc = acc + b_ref[...]
            hc_ref[pl.ds(r0, CONV_ROWS), :] = hc
            mu = jnp.mean(hc, axis=-1, keepdims=True)
            xc = hc - mu
            ln = xc * lax.rsqrt(jnp.mean(xc * xc, axis=-1, keepdims=True) + EPS_LN) * g_ref[...] + lb_ref[...]
            y_ref[pl.ds(r0, CONV_ROWS), :] = _silu(ln).astype(y_ref.dtype)
            return 0

        lax.fori_loop(0, blk // CONV_ROWS, tile, 0)

    vec = pl.BlockSpec((1, CONV_WIDTH), lambda i: (0, 0))
    row = pl.BlockSpec((blk, CONV_WIDTH), lambda i: (i, 0))
    return pl.pallas_call(
        body, name="conv_fwd", grid=(n_rows // blk,),
        in_specs=[ANY, pl.BlockSpec((CONV_K, CONV_WIDTH), lambda i: (0, 0)), vec, vec, vec, ANY],
        out_specs=[row, pl.BlockSpec((blk, CONV_WIDTH), lambda i: (i, 1))],
        out_shape=[_sds((n_rows, CONV_WIDTH)), _sds(ycat.shape, ycat.dtype)], input_output_aliases={5: 1},
        scratch_shapes=[pltpu.VMEM((nblk * blk, CONV_WIDTH), F32), pltpu.SemaphoreType.DMA((nblk,))],
        compiler_params=_params(("arbitrary",)),
    )(hh_pad, w, b, ln_g, ln_b, ycat)


def _conv_bwd_norm(d_ycat, hc, ln_g, ln_b, n_rows):
    blk = _conv_block(n_rows)
    nb = n_rows // blk

    def body(dy_ref, hc_ref, g_ref, lb_ref, o_ref, sums):
        i = pl.program_id(0)

        @pl.when(i == 0)
        def _():
            sums[...] = jnp.zeros_like(sums)

        inside = jnp.logical_and(i >= 1, i <= nb)

        @pl.when(inside)
        def _():
            hcv = hc_ref[...]
            mu = jnp.mean(hcv, axis=-1, keepdims=True)
            xc = hcv - mu
            rstd = lax.rsqrt(jnp.mean(xc * xc, axis=-1, keepdims=True) + EPS_LN)
            xh = xc * rstd
            g = g_ref[...]
            dln = dy_ref[...] * _dsilu(xh * g + lb_ref[...])
            dxh = dln * g
            dhc = rstd * (dxh - jnp.mean(dxh, axis=-1, keepdims=True) - xh * jnp.mean(dxh * xh, axis=-1, keepdims=True))
            o_ref[...] = dhc
            sums[0] += _fold8(dhc)
            sums[1] += _fold8(dln * xh)
            sums[2] += _fold8(dln)

        @pl.when(jnp.logical_not(inside))
        def _():
            o_ref[...] = jnp.zeros_like(o_ref)

    vec = pl.BlockSpec((1, CONV_WIDTH), lambda i: (0, 0))
    return pl.pallas_call(
        body, name="conv_bwd_norm", grid=(nb + 2,),
        in_specs=[pl.BlockSpec((blk, CONV_WIDTH), lambda i: (jnp.clip(i - 1, 0, nb - 1), 1)),
                  pl.BlockSpec((blk, CONV_WIDTH), lambda i: (jnp.clip(i - 1, 0, nb - 1), 0)), vec, vec],
        out_specs=[pl.BlockSpec((blk, CONV_WIDTH), lambda i: (i, 0)),
                   pl.BlockSpec((3, SUBLANES, CONV_WIDTH), lambda i: (0, 0, 0))],
        out_shape=[_sds(((nb + 2) * blk, CONV_WIDTH)), _sds((3, SUBLANES, CONV_WIDTH))],
        compiler_params=_params(("arbitrary",)),
    )(d_ycat, hc, ln_g, ln_b)


def _conv_bwd_taps(dhc_pad, hh_pad, z_all, w, n_rows):
    blk = _conv_block(n_rows)
    nblk = n_rows // blk + 2

    def body(dhc_ref, hh_ref, v_ref, g_ref, w_ref, dv_ref, dg_ref, dw_ref, dwin, hwin, dsems, hsems):
        @pl.when(pl.program_id(0) == 0)
        def _():
            dw_ref[...] = jnp.zeros_like(dw_ref)

        base = _stream_padded(dhc_ref, dwin, dsems, blk, nblk)
        _stream_padded(hh_ref, hwin, hsems, blk, nblk)

        def tile(t, _):
            r0 = pl.multiple_of(t * CONV_BWD_ROWS, CONV_BWD_ROWS) + base
            dh = dwin[pl.ds(r0 + blk, CONV_BWD_ROWS), :]
            acc = jnp.zeros((CONV_BWD_ROWS, CONV_WIDTH), F32)
            for k in range(CONV_K):
                off = (k - CONV_HALF) * GRID_W
                acc = acc + w_ref[k:k + 1, :] * dwin[pl.ds(r0 + blk - off, CONV_BWD_ROWS), :]
                dw_ref[k] += _fold8(dh * hwin[pl.ds(r0 + blk + off, CONV_BWD_ROWS), :])
            rs = pl.ds(pl.multiple_of(t * CONV_BWD_ROWS, CONV_BWD_ROWS), CONV_BWD_ROWS)
            sg = _sigmoid(g_ref[rs, :])
            vv = v_ref[rs, :]
            dv_ref[rs, :] = acc * sg
            dg_ref[rs, :] = acc * vv * sg * (1.0 - sg)
            return 0

        lax.fori_loop(0, blk // CONV_BWD_ROWS, tile, 0)

    row = pl.BlockSpec((blk, CONV_WIDTH), lambda i: (i, 0))
    return pl.pallas_call(
        body, name="conv_bwd_taps", grid=(n_rows // blk,),
        in_specs=[ANY, ANY,
            pl.BlockSpec((blk, CONV_WIDTH), lambda i: (i, 1)), pl.BlockSpec((blk, CONV_WIDTH), lambda i: (i, 2)),
            pl.BlockSpec((CONV_K, CONV_WIDTH), lambda i: (0, 0))],
        out_specs=[row, row, pl.BlockSpec((CONV_K, SUBLANES, CONV_WIDTH), lambda i: (0, 0, 0))],
        out_shape=[_sds((n_rows, CONV_WIDTH)), _sds((n_rows, CONV_WIDTH)), _sds((CONV_K, SUBLANES, CONV_WIDTH))],
        scratch_shapes=[pltpu.VMEM((nblk * blk, CONV_WIDTH), F32), pltpu.VMEM((nblk * blk, CONV_WIDTH), F32),
                        pltpu.SemaphoreType.DMA((nblk,)), pltpu.SemaphoreType.DMA((nblk,))],
        compiler_params=_params(("arbitrary",)),
    )(dhc_pad, hh_pad, z_all, z_all, w)


def _dz_assemble(du0, du1, dy, d_skip, dv, dgate, n_lat):
    rows = du0.shape[0]
    nb = rows // ROW_BLOCK

    w = S5_WIDTH

    def body(a_ref, b_ref, dy_ref, d_ref, dv_ref, dg_ref, o_ref):
        lat = pl.program_id(0) < n_lat

        @pl.when(lat)
        def _():
            o_ref[:, 0:w] = (a_ref[...] + b_ref[...] + dy_ref[...] * d_ref[...]).astype(o_ref.dtype)
            o_ref[:, w:2 * w] = dv_ref[...].astype(o_ref.dtype)
            o_ref[:, 2 * w:3 * w] = dg_ref[...].astype(o_ref.dtype)

        @pl.when(jnp.logical_not(lat))
        def _():
            o_ref[:, 0:w] = (a_ref[...] + b_ref[...]).astype(o_ref.dtype)
            o_ref[:, w:3 * w] = jnp.zeros((ROW_BLOCK, 2 * w), o_ref.dtype)

    all_rows = pl.BlockSpec((ROW_BLOCK, w), lambda i: (i, 0))
    lat_rows = pl.BlockSpec((ROW_BLOCK, w), lambda i: (jnp.minimum(i, n_lat - 1), 0))
    return pl.pallas_call(
        body, name="dz_assemble", grid=(nb,),
        in_specs=[all_rows, all_rows, lat_rows, pl.BlockSpec((1, w), lambda i: (0, 0)), lat_rows, lat_rows],
        out_specs=pl.BlockSpec((ROW_BLOCK, IN_COLS), lambda i: (i, 0)),
        out_shape=_sds((rows, IN_COLS), BF16), compiler_params=_params(("parallel",)),
    )(du0, du1, dy, d_skip, dv, dgate)


def _sum_parts(parts):
    _, r, c = parts.shape

    def body(p_ref, o_ref):
        acc = p_ref[0]
        for q in range(1, NDEV):
            acc = acc + p_ref[q]
        o_ref[...] = acc

    return pl.pallas_call(body, name="sum_parts", out_shape=_sds((r, c)), compiler_params=_params())(parts)


def _row_tile(r, c):
    best = r
    for t in (1024, 512, 256, 128, 64, 32, 16, 8):
        if r % t == 0 and t * c <= 128 * 1024:
            return t
    return best


def _adamw(name, w, gparts, m, v):
    r, c = w.shape
    np_ = gparts.shape[0]
    tr = _row_tile(r, c)

    def body(w_ref, g_ref, m_ref, v_ref, go_ref, d_ref, mo_ref, vo_ref):
        g = g_ref[0].astype(F32)
        for q in range(1, np_):
            g = g + g_ref[q].astype(F32)
        m2 = ADAM_B1 * m_ref[...] + (1.0 - ADAM_B1) * g
        v2 = ADAM_B2 * v_ref[...] + (1.0 - ADAM_B2) * jnp.square(g)
        m_hat = m2 / (1.0 - ADAM_B1 ** ADAM_STEP)
        v_hat = v2 / (1.0 - ADAM_B2 ** ADAM_STEP)
        go_ref[...] = g
        d_ref[...] = -ADAM_LR * (m_hat / (jnp.sqrt(v_hat) + ADAM_EPS) + ADAM_WD * w_ref[...])
        mo_ref[...] = m2
        vo_ref[...] = v2

    row = pl.BlockSpec((tr, c), lambda i: (i, 0))
    return pl.pallas_call(
        body, name=name, grid=(r // tr,),
        in_specs=[row, pl.BlockSpec((np_, tr, c), lambda i: (0, i, 0)), row, row],
        out_specs=[row] * 4, out_shape=[_sds((r, c))] * 4, compiler_params=_params(("parallel",)),
    )(w, gparts, m, v)


def _adamw_native(name, w, g, m, v):
    def body(w_ref, g_ref, m_ref, v_ref, d_ref, mo_ref, vo_ref):
        gv = g_ref[...]
        m2 = ADAM_B1 * m_ref[...] + (1.0 - ADAM_B1) * gv
        v2 = ADAM_B2 * v_ref[...] + (1.0 - ADAM_B2) * jnp.square(gv)
        m_hat = m2 / (1.0 - ADAM_B1 ** ADAM_STEP)
        v_hat = v2 / (1.0 - ADAM_B2 ** ADAM_STEP)
        d_ref[...] = -ADAM_LR * (m_hat / (jnp.sqrt(v_hat) + ADAM_EPS) + ADAM_WD * w_ref[...])
        mo_ref[...] = m2
        vo_ref[...] = v2

    return pl.pallas_call(body, name=name, out_shape=[_sds(w.shape)] * 3, compiler_params=_params())(w, g, m, v)


SMALL = ["c_ctx", "ada_b", "norm1_g", "s5_lam_re", "s5_lam_im", "s5_log_dt", "s5_d", "conv_b", "conv_ln_g", "conv_ln_b",
         "norm2_g", "final_g"]
SMALL_PACKED_ROWS = 24


def _pack_rows(parts, rows):
    flat = jnp.concatenate([p.reshape(-1).astype(F32) for p in parts])
    return jnp.pad(flat, (0, rows * D_MODEL - flat.shape[0])).reshape(rows, D_MODEL)


def _unpack_rows(packed, shapes):
    flat = packed.reshape(-1)
    out, off = [], 0
    for shape in shapes:
        size = 1
        for s in shape:
            size *= s
        out.append(flat[off:off + size].reshape(shape))
        off += size
    return out


def kernel(x, c, ctx, c_ctx, ada_w, ada_b, norm1_g, w_in, s5_lam_re, s5_lam_im, s5_log_dt, s5_b_re, s5_b_im, s5_c_re, s5_c_im, s5_d, s5_w_glu, conv_w, conv_b, conv_ln_g, conv_ln_b, w_out, norm2_g, mlp_w1, mlp_w2, final_g, loss_target, m_c_ctx, m_ada_w, m_ada_b, m_norm1_g, m_w_in, m_s5_lam_re, m_s5_lam_im, m_s5_log_dt, m_s5_b_re, m_s5_b_im, m_s5_c_re, m_s5_c_im, m_s5_d, m_s5_w_glu, m_conv_w, m_conv_b, m_conv_ln_g, m_conv_ln_b, m_w_out, m_norm2_g, m_mlp_w1, m_mlp_w2, m_final_g, v_c_ctx, v_ada_w, v_ada_b, v_norm1_g, v_w_in, v_s5_lam_re, v_s5_lam_im, v_s5_log_dt, v_s5_b_re, v_s5_b_im, v_s5_c_re, v_s5_c_im, v_s5_d, v_s5_w_glu, v_conv_w, v_conv_b, v_conv_ln_g, v_conv_ln_b, v_w_out, v_norm2_g, v_mlp_w1, v_mlp_w2, v_final_g):
    weights = dict(c_ctx=c_ctx, ada_w=ada_w, ada_b=ada_b, norm1_g=norm1_g, w_in=w_in, s5_lam_re=s5_lam_re, s5_lam_im=s5_lam_im, s5_log_dt=s5_log_dt, s5_b_re=s5_b_re, s5_b_im=s5_b_im, s5_c_re=s5_c_re, s5_c_im=s5_c_im, s5_d=s5_d, s5_w_glu=s5_w_glu, conv_w=conv_w, conv_b=conv_b, conv_ln_g=conv_ln_g, conv_ln_b=conv_ln_b, w_out=w_out, norm2_g=norm2_g, mlp_w1=mlp_w1, mlp_w2=mlp_w2, final_g=final_g)
    mom1 = dict(c_ctx=m_c_ctx, ada_w=m_ada_w, ada_b=m_ada_b, norm1_g=m_norm1_g, w_in=m_w_in, s5_lam_re=m_s5_lam_re, s5_lam_im=m_s5_lam_im, s5_log_dt=m_s5_log_dt, s5_b_re=m_s5_b_re, s5_b_im=m_s5_b_im, s5_c_re=m_s5_c_re, s5_c_im=m_s5_c_im, s5_d=m_s5_d, s5_w_glu=m_s5_w_glu, conv_w=m_conv_w, conv_b=m_conv_b, conv_ln_g=m_conv_ln_g, conv_ln_b=m_conv_ln_b, w_out=m_w_out, norm2_g=m_norm2_g, mlp_w1=m_mlp_w1, mlp_w2=m_mlp_w2, final_g=m_final_g)
    mom2 = dict(c_ctx=v_c_ctx, ada_w=v_ada_w, ada_b=v_ada_b, norm1_g=v_norm1_g, w_in=v_w_in, s5_lam_re=v_s5_lam_re, s5_lam_im=v_s5_lam_im, s5_log_dt=v_s5_log_dt, s5_b_re=v_s5_b_re, s5_b_im=v_s5_b_im, s5_c_re=v_s5_c_re, s5_c_im=v_s5_c_im, s5_d=v_s5_d, s5_w_glu=v_s5_w_glu, conv_w=v_conv_w, conv_b=v_conv_b, conv_ln_g=v_conv_ln_g, conv_ln_b=v_conv_ln_b, w_out=v_w_out, norm2_g=v_norm2_g, mlp_w1=v_mlp_w1, mlp_w2=v_mlp_w2, final_g=v_final_g)
    order = list(weights)

    me = 4 * lax.axis_index("x") + 2 * lax.axis_index("y") + lax.axis_index("c")
    xs, cs, tgt = x[0], ctx[0], loss_target[0]
    n_lat_rows, n_ctx_rows = xs.shape[0], cs.shape[0]
    n_rows = n_lat_rows + n_ctx_rows
    n_lat = n_lat_rows // ROW_BLOCK
    ada_cols = ada_w.shape[2]

    (c_all,), _ = _exchange("gather_c", [c], [True])
    c_all = c_all.reshape(NDEV, D_MODEL)

    cond_fwd = jnp.concatenate([c_all, c_ctx[None], jnp.zeros((7, D_MODEL), F32)])
    ada_b_loc = lax.dynamic_slice(ada_b, (0, me * ada_cols), (1, ada_cols))
    (mod_g,), mod_token = _exchange("gather_mod", [_ada_fwd(cond_fwd, ada_w[0], ada_b_loc)], [True])
    wi_send, wi_recv, wi_src, wi_land, wi_token = _exchange_start(
        "gather_w_in_start", [w_in[0].astype(BF16) + mod_token[0:1, 0:1].astype(BF16)], [True])
    mixer_w = [s5_w_glu[0].astype(BF16), conv_w[0] + wi_token[0:1, 0:1], w_out[0].astype(BF16)]
    mixer_send, mixer_recv, mixer_src, mixer_land, mixer_token = _exchange_start("gather_mixer_start", mixer_w, [True] * 3)
    mlp_w = [mlp_w1[0].astype(BF16), mlp_w2[0].astype(BF16) + mixer_token[0:1, 0:1].astype(BF16)]
    mlpw_send, mlpw_recv, mlpw_src, mlpw_land, mlpw_token = _exchange_start("gather_mlp_start", mlp_w, [True] * 2)
    mod_rows = jnp.transpose(mod_g, (1, 0, 2)).reshape(16, 6 * D_MODEL) + mlpw_token[0:1, 0:1]
    mod = lax.dynamic_slice(mod_rows, (me, 0), (1, 6 * D_MODEL)).reshape(6, D_MODEL)
    modc = mod_rows[8, :2 * D_MODEL].reshape(2, D_MODEL)
    sh1, sc1, g1, sh2, sc2, g2 = [mod[i:i + 1] for i in range(6)]

    a_all = _prenorm("prenorm1", xs, cs, norm1_g, jnp.stack([mod[0:2], modc]))
    wi_own, wi_landed = _exchange_wait("gather_w_in_wait", wi_send, wi_recv, wi_src, wi_land, [True], a_all)
    w_in_full = jnp.transpose(_with_own(wi_landed[0], wi_own[0], me), (1, 0, 2)).reshape(D_MODEL, IN_COLS)
    tm_all = 1088 if n_rows % 1088 == 0 else ROW_BLOCK
    (z_all,) = _matmul("in_proj", a_all, w_in_full, "nn", (n_rows, IN_COLS, D_MODEL), (tm_all, IN_COLS, D_MODEL),
                       [((n_rows, IN_COLS), F32)])

    lam_re, lam_im = s5_lam_re[0].reshape(2, 1, NSTATE), s5_lam_im[0].reshape(2, 1, NSTATE)
    ldt = jnp.repeat(s5_log_dt[0], S5_STATE, axis=-1).reshape(2, 1, NSTATE)
    bt_re = jnp.transpose(s5_b_re[0], (0, 3, 1, 2)).reshape(2, S5_GROUP, NSTATE)
    bt_im = jnp.transpose(s5_b_im[0], (0, 3, 1, 2)).reshape(2, S5_GROUP, NSTATE)
    groups_per_block = S5_GROUPS // S5_BLOCKS
    ct_re = jnp.tile(s5_c_re[0].reshape(2, S5_WIDTH, S5_STATE), (1, 1, groups_per_block))
    ct_im = jnp.tile(s5_c_im[0].reshape(2, S5_WIDTH, S5_STATE), (1, 1, groups_per_block))
    d_skip = s5_d[0].reshape(1, S5_WIDTH)
    perm = _segment_permutation()
    perm_t = perm.T
    disc, states, y_dir = [], [], []
    for d in range(2):
        disc.append(_s5_discretise(f"s5_disc{d}", d == 0, lam_re[d], lam_im[d], ldt[d], bt_re[d], bt_im[d], ct_re[d], ct_im[d]))
        _, tab, _, bmat, cmat = disc[d]
        s, yd = _s5_scan_fwd(f"s5_scan_fwd{d}", d == 0, z_all, bmat, cmat, tab, perm, perm_t)
        states.append(s)
        y_dir.append(yd)
    mixer_own, mixer_landed = _exchange_wait("gather_mixer_wait", mixer_send, mixer_recv, mixer_src, mixer_land,
                                             [True] * 3, y_dir[1])
    glu_g, conv_w_g, w_out_g = [_with_own(l, o, me) for l, o in zip(mixer_landed, mixer_own)]
    glu_full = glu_g.reshape(S5_WIDTH, S5_WIDTH)
    conv_w_full = jnp.transpose(conv_w_g, (1, 0, 2)).reshape(CONV_K, CONV_WIDTH)
    w_out_full = w_out_g.reshape(D_MODEL, D_MODEL)
    ycat = _glu_fwd(z_all, y_dir[0], y_dir[1], d_skip, glu_full, n_lat_rows)

    hh_pad = _conv_gate(z_all, n_lat_rows)
    hc, ycat = _conv_fwd(hh_pad, conv_w_full, conv_b, conv_ln_g, conv_ln_b, ycat, n_lat_rows)

    tm = min(1024, n_lat_rows)
    tm_e = min(512, n_lat_rows)
    w1_cols = D_FF // NDEV
    row_vec = lambda tn: pl.BlockSpec((1, tn), lambda i, j, k: (0, j))
    out_tile = lambda t_m, t_n: pl.BlockSpec((t_m, t_n), lambda i, j, k: (i, j))
    full_rows = ((n_lat_rows, D_MODEL), F32)
    sums = ((n_lat_rows // tm_e, SUBLANES, D_MODEL), F32)
    sums_spec = pl.BlockSpec((None, SUBLANES, D_MODEL), lambda i, j, k: (i, 0, 0))
    vec = lambda v: (v, row_vec(D_MODEL))
    transposed_tile = lambda t_m, t_n: pl.BlockSpec((t_n, t_m), lambda i, j, k: (j, i))
    mix, h1, a2, a2_t = _matmul(
        "out_proj", ycat, w_out_full, "nn", (n_lat_rows, D_MODEL, D_MODEL), (tm_e, D_MODEL, D_MODEL),
        [full_rows, full_rows, ((n_lat_rows, D_MODEL), BF16), ((D_MODEL, n_lat_rows), BF16)],
        epi=_epi_residual_prenorm,
        epi_extra=[(xs, out_tile(tm_e, D_MODEL)), vec(g1), vec(norm2_g), vec(sc2), vec(sh2)],
        out_specs=[out_tile(tm_e, D_MODEL)] * 3 + [transposed_tile(tm_e, D_MODEL)])
    mlpw_own, mlpw_landed = _exchange_wait("gather_mlp_wait", mlpw_send, mlpw_recv, mlpw_src, mlpw_land, [True] * 2, a2)
    w1_g, w2_g = [_with_own(l, o, me) for l, o in zip(mlpw_landed, mlpw_own)]
    w2_full = w2_g.reshape(D_FF, D_MODEL)
    tm_up = min(2048, n_lat_rows)
    f, f_t = _matmul("mlp_up", a2, w1_g, "nn", (n_lat_rows, D_FF, D_MODEL), (tm_up, w1_cols, D_MODEL),
                     [((n_lat_rows, D_FF), BF16), ((D_FF, n_lat_rows), BF16)], epi=lambda acc: (acc, acc.T),
                     b_spec=pl.BlockSpec((None, D_MODEL, w1_cols), lambda i, j, k: (j, 0, 0)),
                     out_specs=[out_tile(tm_up, w1_cols), transposed_tile(tm_up, w1_cols)])
    sq_relu = lambda t: jnp.square(jnp.maximum(t, 0.0))
    mlp_out, d_h2, dm2, err_sums, d_final_g8 = _matmul(
        "mlp_down", f, w2_full, "nn", (n_lat_rows, D_MODEL, D_FF), (tm_e, D_MODEL, 2048),
        [full_rows, full_rows, ((n_lat_rows, D_MODEL), BF16), sums, sums], a_fn=sq_relu, epi=_epi_residual_loss,
        epi_extra=[(h1, out_tile(tm_e, D_MODEL)), vec(g2), (tgt, out_tile(tm_e, D_MODEL)), vec(final_g[None])],
        out_specs=[out_tile(tm_e, D_MODEL)] * 3 + [sums_spec] * 2)

    (d_f,) = _matmul("mlp_down_dx", dm2, w2_full, "nt", (n_lat_rows, D_FF, D_MODEL), (tm_up, 512, D_MODEL),
                     [((n_lat_rows, D_FF), BF16)],
                     epi=lambda acc, ft: (acc * 2.0 * jnp.maximum(ft.astype(F32), 0.0),),
                     epi_extra=[(f, out_tile(tm_up, 512))])
    tk_dw = min(2048, n_lat_rows)
    (g_w2,) = _matmul("mlp_down_dw", f_t, dm2, "nn", (D_FF, D_MODEL, n_lat_rows), (1024, D_MODEL, tk_dw),
                      [((D_FF, D_MODEL), F32)], a_fn=sq_relu)
    (g_w1,) = _matmul("mlp_up_dw", a2_t, d_f, "nn", (D_MODEL, D_FF, n_lat_rows), (D_MODEL, w1_cols, n_lat_rows),
                      [((NDEV, D_MODEL, w1_cols), F32)],
                      out_specs=[pl.BlockSpec((None, D_MODEL, w1_cols), lambda i, j, k: (j, 0, 0))])
    mlp_send, mlp_recv, mlp_src, mlp_land, mlp_token = _exchange_start(
        "scatter_mlp_start", [g_w1, g_w2.reshape(NDEV, D_FF // NDEV, D_MODEL)], [False] * 2)
    d_h1, dm1, *sums2 = _matmul(
        "mlp_up_dx", d_f, w1_g, "nt", (n_lat_rows, D_MODEL, D_FF), (tm_e, D_MODEL, 4 * w1_cols),
        [full_rows, ((n_lat_rows, D_MODEL), BF16)] + [sums] * 4, epi=_epi_norm_bwd,
        epi_extra=[(h1, out_tile(tm_e, D_MODEL)), (d_h2, out_tile(tm_e, D_MODEL)), (mlp_out, out_tile(tm_e, D_MODEL)),
                   vec(norm2_g), vec(sc2 + mlp_token[0:1, 0:1]), vec(g1)],
        b_spec=pl.BlockSpec((4, D_MODEL, w1_cols), lambda i, j, k: (k, 0, 0)), b_slabs=4,
        out_specs=[out_tile(tm_e, D_MODEL)] * 2 + [sums_spec] * 4)

    (d_ycat,) = _matmul("out_proj_dx", dm1, w_out_full, "nt", (n_lat_rows, D_MODEL, D_MODEL), (tm, D_MODEL, D_MODEL),
                        [((n_lat_rows, D_MODEL), F32)])
    (g_w_out,) = _matmul("out_proj_dw", ycat, dm1, "tn", (D_MODEL, D_MODEL, n_lat_rows), (D_MODEL, D_MODEL, 512),
                         [((D_MODEL, D_MODEL), F32)])

    dy, g_glu, dd8 = _glu_bwd(d_ycat, z_all, y_dir[0], y_dir[1], d_skip, glu_full, n_lat_rows)
    proj_send, proj_recv, proj_src, proj_land, proj_token = _exchange_start(
        "scatter_proj_start",
        [g_w_out.reshape(NDEV, D_MODEL // NDEV, D_MODEL), g_glu.reshape(NDEV, S5_WIDTH // NDEV, S5_WIDTH)], [False] * 2)
    perm = perm + proj_token[0:1, 0:1].astype(BF16)
    du, g_lam_re, g_lam_im, g_ldt, g_bt, g_cdiag = [], [], [], [], [], []
    for d in range(2):
        _, _, adj, bmat, cmat = disc[d]
        du_d, d_bdiag, d_cdiag, d_abar8 = _s5_scan_bwd(f"s5_scan_bwd{d}", d == 0, dy, z_all, states[d], bmat, cmat, adj,
                                                       perm, perm_t)
        du.append(du_d)
        d_bbar = jnp.transpose(d_bdiag.reshape(S5_BLOCKS, S5_GROUP, 2, NSTATE // S5_BLOCKS), (2, 1, 0, 3)).reshape(
            2 * S5_GROUP, NSTATE)
        d_lam8, d_bt = _s5_discretise_bwd(f"s5_disc_bwd{d}", lam_re[d], lam_im[d], ldt[d], bt_re[d], bt_im[d], d_abar8, d_bbar)
        g_lam_re.append(d_lam8[0].reshape(S5_GROUPS, S5_STATE))
        g_lam_im.append(d_lam8[1].reshape(S5_GROUPS, S5_STATE))
        g_ldt.append(d_lam8[2].reshape(S5_GROUPS, S5_STATE).sum(axis=-1))
        g_bt.append(d_bt)
        g_cdiag.append(d_cdiag)

    dhc_pad, conv_sums = _conv_bwd_norm(d_ycat, hc, conv_ln_g, conv_ln_b, n_lat_rows)
    d_v, d_gate, g_conv_w8 = _conv_bwd_taps(dhc_pad, hh_pad, z_all, conv_w_full, n_lat_rows)

    dz_all = _dz_assemble(du[0], du[1], dy, d_skip, d_v, d_gate, n_lat)
    (g_w_in_full,) = _matmul("in_proj_dw", a_all, dz_all, "tn", (D_MODEL, IN_COLS, n_rows), (D_MODEL, IN_COLS, tm_all),
                             [((D_MODEL, IN_COLS), F32)])
    g_w_in_parts = jnp.transpose(g_w_in_full.reshape(D_MODEL, NDEV, IN_COLS // NDEV), (1, 0, 2)).astype(BF16)
    win_send, win_recv, win_src, win_land, win_token = _exchange_start("scatter_w_in_start", [g_w_in_parts], [False])
    (d_a_all,) = _matmul("in_proj_dx", dz_all, w_in_full + win_token[0:1, 0:1].astype(BF16), "nt",
                         (n_rows, D_MODEL, IN_COLS), (tm_all, D_MODEL, IN_COLS), [((n_rows, D_MODEL), F32)])
    grad_x, sums1 = _norm_bwd("norm1_bwd", xs, d_a_all, 0, norm1_g, sc1, res=d_h1, aux=mix)
    (sums1c,) = _norm_bwd("norm1_bwd_ctx", cs, d_a_all, n_lat, norm1_g, modc[1:2])

    s1, s1c, s2 = sums1.sum(axis=1), sums1c.sum(axis=1), [p.sum(axis=(0, 1)) for p in sums2]
    d_mod = jnp.concatenate([s1[0], s1[1], s1[3], s2[0], s2[1], s2[3]])
    d_modc = jnp.concatenate([s1c[0], s1c[1], jnp.zeros((4 * D_MODEL,), F32)])
    (dmod_g,), _ = _exchange("gather_dmod", [jnp.stack([d_mod, d_modc])], [True])
    dmod16 = jnp.concatenate([dmod_g[:, 0], dmod_g[:, 1]])
    dmod16_loc = lax.dynamic_slice(dmod16, (0, me * ada_cols), (16, ada_cols))
    cond_bwd = jnp.concatenate([c_all, jnp.broadcast_to(c_ctx[None], (NDEV, D_MODEL))])
    g_ada_w, g_c_ctx8 = _ada_bwd(cond_bwd, dmod16_loc, ada_w[0], c_ctx[None])

    small_parts = dict(
        c_ctx=g_c_ctx8[0], ada_b=d_mod + d_modc, norm1_g=s1[2] + s1c[2],
        s5_lam_re=jnp.stack(g_lam_re), s5_lam_im=jnp.stack(g_lam_im), s5_log_dt=jnp.stack(g_ldt),
        s5_d=dd8.sum(axis=0), conv_b=conv_sums[0].sum(axis=0), conv_ln_g=conv_sums[1].sum(axis=0),
        conv_ln_b=conv_sums[2].sum(axis=0), norm2_g=s2[2], final_g=d_final_g8.sum(axis=(0, 1)))
    reduced_shapes = [(SMALL_PACKED_ROWS, D_MODEL), (2, 2 * S5_GROUP, NSTATE), (2,) + _S5_DIAG, (1,)]
    small_g = _pack_rows(
        [_pack_rows([small_parts[n] for n in SMALL], SMALL_PACKED_ROWS), jnp.stack(g_bt), jnp.stack(g_cdiag),
         (0.5 / D_MODEL * jnp.sum(err_sums)).reshape(1)], SMALL_ROWS).reshape(NDEV, SMALL_ROWS // NDEV, D_MODEL)
    g_conv_w_parts = jnp.transpose(g_conv_w8.sum(axis=1).reshape(CONV_K, NDEV, CONV_WIDTH // NDEV), (1, 0, 2))

    res = {}

    def own_chunk(src):
        return lax.dynamic_index_in_dim(src, me, 0, keepdims=False)

    def adamw_big(name, parts):
        outs = _adamw("adamw_" + name, weights[name][0], parts, mom1[name][0], mom2[name][0])
        res[name] = [o[None] for o in outs]
        return outs[0]

    sm_send, sm_recv, sm_src, sm_land, sm_token = _exchange_start("scatter_small_start", [g_conv_w_parts, small_g],
                                                                  [False] * 2)
    done = adamw_big("ada_w", g_ada_w[None] + sm_token[0:1, 0:1])
    mlp_src, mlp_landed = _exchange_wait("scatter_mlp_wait", mlp_send, mlp_recv, mlp_src, mlp_land, [False] * 2, done)
    p_w1, p_w2 = [_with_own(l, own_chunk(s), me) for l, s in zip(mlp_landed, mlp_src)]
    adamw_big("mlp_w1", p_w1)
    done = adamw_big("mlp_w2", p_w2)
    sm_src, sm_landed = _exchange_wait("scatter_small_wait", sm_send, sm_recv, sm_src, sm_land, [False] * 2, done)
    p_conv_w, p_small = [_with_own(l, own_chunk(s), me) for l, s in zip(sm_landed, sm_src)]
    ga_send, ga_recv, ga_src, ga_land, ga_token = _exchange_start("gather_small_start", [_sum_parts(p_small)], [True])
    proj_src, proj_landed = _exchange_wait("scatter_proj_wait", proj_send, proj_recv, proj_src, proj_land, [False] * 2,
                                           ga_token)
    p_w_out, p_glu = [_with_own(l, own_chunk(s), me) for l, s in zip(proj_landed, proj_src)]
    adamw_big("w_out", p_w_out)
    done = adamw_big("s5_w_glu", p_glu)
    win_src, win_landed = _exchange_wait("scatter_w_in_wait", win_send, win_recv, win_src, win_land, [False], done)
    adamw_big("w_in", _with_own(win_landed[0], own_chunk(win_src[0]), me))
    done = adamw_big("conv_w", p_conv_w)
    ga_own, ga_landed = _exchange_wait("gather_small_wait", ga_send, ga_recv, ga_src, ga_land, [True], done)
    small_all = _with_own(ga_landed[0], ga_own[0], me).reshape(1, SMALL_ROWS, D_MODEL)
    _, r_bt, r_cdiag, loss = _unpack_rows(small_all, reduced_shapes)
    loss = loss.reshape(())
    pack = lambda src: _pack_rows([src[n] for n in SMALL], SMALL_PACKED_ROWS)
    outs = _adamw("adamw_small", pack(weights), small_all, pack(mom1), pack(mom2))
    unpacked = [_unpack_rows(o, [weights[n].shape for n in SMALL]) for o in outs]
    for i, name in enumerate(SMALL):
        res[name] = [u[i] for u in unpacked]
    to_ghp = lambda t: jnp.transpose(t.reshape(2, S5_GROUP, S5_GROUPS, S5_STATE), (0, 2, 1, 3))[None]
    r_c = jnp.transpose(r_cdiag.reshape(2, S5_BLOCKS, S5_GROUP, 2, groups_per_block, S5_STATE), (3, 0, 1, 4, 2, 5)).reshape(
        2, 1, 2, S5_GROUPS, S5_GROUP, S5_STATE)
    swap = lambda t: jnp.swapaxes(t, -1, -2)
    for name, grad in (("s5_b_re", to_ghp(r_bt[:, :S5_GROUP])), ("s5_b_im", to_ghp(r_bt[:, S5_GROUP:]))):
        outs = _adamw_native("adamw_" + name, swap(weights[name]), grad, swap(mom1[name]), swap(mom2[name]))
        res[name] = [swap(grad), *[swap(o) for o in outs]]
    for name, grad in (("s5_c_re", r_c[0]), ("s5_c_im", -r_c[1])):
        res[name] = [grad, *_adamw_native("adamw_" + name, weights[name], grad, mom1[name], mom2[name])]

    return (loss, grad_x[None], *[res[n][0] for n in order], *[res[n][1] for n in order],
            *[res[n][2] for n in order], *[res[n][3] for n in order])
```

```python
import functools

import jax
import jax.numpy as jnp
from jax import lax
from jax.experimental import pallas as pl
from jax.experimental.pallas import tpu as pltpu

F32 = jnp.float32
BF16 = jnp.bfloat16
MESH = pl.DeviceIdType.MESH
ANY = pl.BlockSpec(memory_space=pl.ANY)

NDEV = 8
D_MODEL = 1024
GRID_W = 64
S5_WIDTH = 512
S5_GROUP = 16
S5_GROUPS = 32
S5_STATE = 64
NSTATE = S5_GROUPS * S5_STATE
CONV_WIDTH = 512
CONV_K = 31
IN_COLS = S5_WIDTH + 2 * CONV_WIDTH
D_FF = 4 * D_MODEL
EPS_RMS = 1e-6
EPS_LN = 1e-5
ADAM_LR = 0.001
ADAM_B1 = 0.9
ADAM_B2 = 0.999
ADAM_EPS = 1e-08
ADAM_WD = 0.01
ADAM_STEP = 10

SUBLANES = 8
LANES = 128
ROW_BLOCK = 256
SCAN_LANES = 512
SCAN_UNROLL = 32
SEGMENTS = SUBLANES
STEPS = ROW_BLOCK // SEGMENTS
S5_BLOCKS = 4
S5_BLOCK_WIDTH = S5_WIDTH // S5_BLOCKS
CONV_ROWS = 64
CONV_BWD_ROWS = 32
VMEM_LIMIT = 48 * 1024 * 1024
SMALL_ROWS = 320


def _params(sem=None):
    kw = dict(vmem_limit_bytes=VMEM_LIMIT)
    if sem is not None:
        kw["dimension_semantics"] = sem
    return pltpu.CompilerParams(**kw)


def _sds(shape, dtype=F32):
    return jax.ShapeDtypeStruct(tuple(shape), dtype)


def _fold8(x):
    return x.reshape(x.shape[0] // SUBLANES, SUBLANES, x.shape[1]).sum(axis=0)


def _sigmoid(x):
    return 1.0 / (1.0 + jnp.exp(-x))


def _silu(x):
    return x * _sigmoid(x)


def _dsilu(x):
    s = _sigmoid(x)
    return s * (1.0 + x * (1.0 - s))


_GELU_C = 0.7978845608028654


def _gelu(x):
    return 0.5 * x * (1.0 + jnp.tanh(_GELU_C * (x + 0.044715 * x * x * x)))


def _dgelu(x):
    t = jnp.tanh(_GELU_C * (x + 0.044715 * x * x * x))
    return 0.5 * (1.0 + t) + 0.5 * x * (1.0 - t * t) * _GELU_C * (1.0 + 3.0 * 0.044715 * x * x)


def _rms(x):
    rstd = lax.rsqrt(jnp.mean(x * x, axis=-1, keepdims=True) + EPS_RMS)
    return x * rstd, rstd


def _epi_residual_prenorm(acc, res, gate, gain, scale, shift):
    h = res + gate * acc
    xh, _ = _rms(h)
    a = (xh * gain) * (1.0 + scale) + shift
    return acc, h, a, a.T


def _epi_residual_loss(acc, res, gate, target, gain):
    h = res + gate * acc
    xh, rstd = _rms(h)
    err = xh * gain - target
    dy = err * (1.0 / h.shape[-1])
    dxh = dy * gain
    dh = rstd * (dxh - xh * jnp.mean(dxh * xh, axis=-1, keepdims=True))
    return acc, dh, dh * gate, _fold8(err * err), _fold8(dy * xh)


def _epi_norm_bwd(d_act, x, res, aux, gain, scale, gate):
    xh, rstd = _rms(x)
    dn = d_act * (1.0 + scale)
    dxh = dn * gain
    dx = res + rstd * (dxh - xh * jnp.mean(dxh * xh, axis=-1, keepdims=True))
    return dx, dx * gate, _fold8(d_act), _fold8(d_act * (xh * gain)), _fold8(dn * xh), _fold8(res * aux)


def _dot(a, b, mode):
    dims = {"nn": (((1,), (0,)), ((), ())), "nt": (((1,), (1,)), ((), ())), "tn": (((0,), (0,)), ((), ()))}[mode]
    return lax.dot_general(a, b, dims, preferred_element_type=F32)


def _peers(x, y, c):
    out = []
    for k in range(1, NDEV):
        px = 1 - x if k & 4 else x
        py = 1 - y if k & 2 else y
        pc = 1 - c if k & 1 else c
        out.append(((px, py, pc), 4 * px + 2 * py + pc))
    return out


def _exchange_copies(src, land, send_sems, recv_sems, gather):
    x, y, c = lax.axis_index("x"), lax.axis_index("y"), lax.axis_index("c")
    me = 4 * x + 2 * y + c
    out = []
    for a in range(len(src)):
        for k, (peer, plin) in enumerate(_peers(x, y, c)):
            chunk = src[a] if gather[a] else src[a].at[plin]
            sems = dict(send_sem=send_sems.at[a * (NDEV - 1) + k], recv_sem=recv_sems.at[a * (NDEV - 1) + k],
                        device_id=peer, device_id_type=MESH)
            out.append((pltpu.make_async_remote_copy(src_ref=chunk, dst_ref=land[a].at[me], **sems),
                        pltpu.make_async_remote_copy(src_ref=chunk, dst_ref=land[a].at[plin], **sems)))
    return out


def _exchange(name, srcs, gather):
    n = len(srcs)
    outs = [_sds(((NDEV,) + s.shape) if g else s.shape, s.dtype) for s, g in zip(srcs, gather)]

    def body(*refs):
        src, dst, token = refs[:n], refs[n:2 * n], refs[2 * n]
        send_sems, recv_sems, local_sems = refs[2 * n + 1:]
        me = 4 * lax.axis_index("x") + 2 * lax.axis_index("y") + lax.axis_index("c")
        local = [pltpu.make_async_copy(src[a] if gather[a] else src[a].at[me], dst[a].at[me], local_sems.at[a])
                 for a in range(n)]
        for copy in local:
            copy.start()
        copies = _exchange_copies(src, dst, send_sems, recv_sems, gather)
        for copy, _ in copies:
            copy.start()
        token[...] = jnp.zeros_like(token)
        for copy, landing in copies:
            copy.wait_send()
            landing.wait_recv()
        for copy in local:
            copy.wait()

    nsem = n * (NDEV - 1)
    out = pl.pallas_call(
        body, name=name, out_shape=outs + [_sds((SUBLANES, LANES))], in_specs=[ANY] * n,
        out_specs=[ANY] * n + [pl.BlockSpec(memory_space=pltpu.VMEM)],
        scratch_shapes=[pltpu.SemaphoreType.DMA((nsem,)), pltpu.SemaphoreType.DMA((nsem,)), pltpu.SemaphoreType.DMA((n,))],
    )(*srcs)
    return out[:n], out[n]


HBM = pl.BlockSpec(memory_space=pltpu.HBM)
SEM = pl.BlockSpec(memory_space=pltpu.SEMAPHORE)
EFFECT = pltpu.SideEffectType.DATAFLOW_SIDE_EFFECTING


def _exchange_start(name, srcs, gather):
    n = len(srcs)
    lands = [lax.empty(((NDEV,) + s.shape) if g else s.shape, s.dtype) for s, g in zip(srcs, gather)]

    def body(*refs):
        src, land = refs[:n], refs[n:2 * n]
        send_sems, recv_sems = refs[2 * n], refs[2 * n + 1]
        token = refs[-1]
        for copy, _ in _exchange_copies(src, land, send_sems, recv_sems, gather):
            copy.start()
        token[...] = jnp.zeros_like(token)

    hbm = lambda v: pltpu.HBM(v.shape, v.dtype)
    nsem = n * (NDEV - 1)
    out = pl.pallas_call(
        body, name=name,
        out_shape=(pltpu.SemaphoreType.DMA((nsem,)), pltpu.SemaphoreType.DMA((nsem,)), *[hbm(v) for v in srcs],
                   *[hbm(v) for v in lands], _sds((SUBLANES, LANES))),
        in_specs=[HBM] * (2 * n), out_specs=(SEM, SEM, *([HBM] * (2 * n)), pl.BlockSpec(memory_space=pltpu.VMEM)),
        input_output_aliases={i: 2 + i for i in range(2 * n)},
        compiler_params=pltpu.CompilerParams(has_side_effects=EFFECT),
    )(*[pltpu.with_memory_space_constraint(v, pltpu.HBM) for v in list(srcs) + lands])
    return out[0], out[1], out[2:2 + n], out[2 + n:2 + 2 * n], out[-1]


def _exchange_wait(name, send_sems, recv_sems, srcs, lands, gather, after):
    n = len(srcs)

    def body(*refs):
        src, land = refs[:n], refs[n:2 * n]
        send_ref, recv_ref = refs[2 * n], refs[2 * n + 1]
        for copy, landing in _exchange_copies(src, land, send_ref, recv_ref, gather):
            copy.wait_send()
            landing.wait_recv()

    hbm = lambda v: pltpu.HBM(v.shape, v.dtype)
    out = pl.pallas_call(
        body, name=name, out_shape=[hbm(v) for v in list(srcs) + list(lands)],
        in_specs=[HBM] * (2 * n) + [SEM, SEM, ANY], out_specs=[HBM] * (2 * n),
        input_output_aliases={i: i for i in range(2 * n)},
        compiler_params=pltpu.CompilerParams(has_side_effects=EFFECT),
    )(*srcs, *lands, send_sems, recv_sems, after)
    return out[:n], out[n:]


def _with_own(landed, own, me):
    return lax.dynamic_update_slice(landed, own[None], (me,) + (0,) * own.ndim)


def _matmul(name, a, b, mode, mnk, tiles, outs, a_spec=None, b_spec=None, a_fn=None, a_extra=(),
            epi=None, epi_extra=(), out_specs=None, b_slabs=1):
    m_, n_, k_ = mnk
    tm, tn, tk = tiles
    nk = k_ // tk
    if a_spec is None:
        a_spec = (pl.BlockSpec((tk, tm), lambda i, j, k: (k, i)) if mode == "tn"
                  else pl.BlockSpec((tm, tk), lambda i, j, k: (i, k)))
    if b_spec is None:
        b_spec = (pl.BlockSpec((tn, tk), lambda i, j, k: (j, k)) if mode == "nt"
                  else pl.BlockSpec((tk, tn), lambda i, j, k: (k, j)))
    if out_specs is None:
        out_specs = [pl.BlockSpec((tm, tn), lambda i, j, k: (i, j)) for _ in outs]
    na, ne, no = len(a_extra), len(epi_extra), len(outs)

    def body(*refs):
        a_ref, b_ref = refs[0], refs[1]
        ax = refs[2:2 + na]
        ex = refs[2 + na:2 + na + ne]
        o = refs[2 + na + ne:2 + na + ne + no]

        def finish(res):
            res = epi(res, *[r[...] for r in ex]) if epi is not None else (res,)
            for ref, val in zip(o, res):
                ref[...] = val.astype(ref.dtype)

        at = a_ref[...]
        if a_fn is not None:
            at = a_fn(at, *[r[...] for r in ax])
        at = at.astype(BF16)
        if b_slabs == 1:
            part = _dot(at, b_ref[...].astype(BF16), mode)
        else:
            ks = tk // b_slabs
            part = _dot(at[:, 0:ks], b_ref[0].astype(BF16), mode)
            for s in range(1, b_slabs):
                part = part + _dot(at[:, s * ks:(s + 1) * ks], b_ref[s].astype(BF16), mode)
        if nk == 1:
            finish(part)
            return
        acc = refs[-1]
        k = pl.program_id(2)

        @pl.when(k == 0)
        def _():
            acc[...] = part

        @pl.when(k > 0)
        def _():
            acc[...] += part

        @pl.when(k == nk - 1)
        def _():
            finish(acc[...])

    return pl.pallas_call(
        body, name=name, grid=(m_ // tm, n_ // tn, nk),
        in_specs=[a_spec, b_spec] + [s for _, s in a_extra] + [s for _, s in epi_extra],
        out_specs=out_specs, out_shape=[_sds(s, d) for s, d in outs],
        scratch_shapes=[pltpu.VMEM((tm, tn), F32)] if nk > 1 else [],
        compiler_params=_params(("parallel", "parallel", "arbitrary")),
    )(a, b, *[x for x, _ in a_extra], *[x for x, _ in epi_extra])


def _prenorm(name, x, ctx, gain, shsc):
    n_lat = x.shape[0] // ROW_BLOCK
    n_ctx = 0 if ctx is None else ctx.shape[0] // ROW_BLOCK
    d = x.shape[1]

    def norm(src, g_ref, m_ref, o_ref):
        xv = src[...]
        xh = xv * lax.rsqrt(jnp.mean(xv * xv, axis=-1, keepdims=True) + EPS_RMS)
        o_ref[...] = ((xh * g_ref[...]) * (1.0 + m_ref[1:2, :]) + m_ref[0:1, :]).astype(o_ref.dtype)

    def body(*refs):
        if ctx is None:
            x_ref, g_ref, m_ref, o_ref = refs
            norm(x_ref, g_ref, m_ref, o_ref)
        else:
            x_ref, c_ref, g_ref, m_ref, o_ref = refs
            i = pl.program_id(0)

            @pl.when(i < n_lat)
            def _():
                norm(x_ref, g_ref, m_ref, o_ref)

            @pl.when(i >= n_lat)
            def _():
                norm(c_ref, g_ref, m_ref, o_ref)

    in_specs = [pl.BlockSpec((ROW_BLOCK, d), lambda i: (jnp.minimum(i, n_lat - 1), 0))]
    args = [x]
    if ctx is not None:
        in_specs.append(pl.BlockSpec((ROW_BLOCK, d), lambda i: (jnp.maximum(i - n_lat, 0), 0)))
        args.append(ctx)
    in_specs += [pl.BlockSpec((1, d), lambda i: (0, 0)),
                 pl.BlockSpec((None, 2, d), lambda i: (jnp.minimum(i // n_lat, 1), 0, 0))]
    args += [gain, shsc]
    return pl.pallas_call(
        body, name=name, grid=(n_lat + n_ctx,), in_specs=in_specs,
        out_specs=pl.BlockSpec((ROW_BLOCK, d), lambda i: (i, 0)),
        out_shape=_sds(((n_lat + n_ctx) * ROW_BLOCK, d), BF16),
        compiler_params=_params(("parallel",)),
    )(*args)


def _norm_bwd(name, x, d_act, d_act_row0, gain, scale, res=None, aux=None, gate=None):
    rows, d = x.shape
    nb = rows // ROW_BLOCK
    has_res = res is not None
    has_gate = gate is not None

    def body(*refs):
        if has_gate:
            x_ref, da_ref, g_ref, sc_ref, r_ref, aux_ref, gate_ref, dx_ref, dm_ref, sums = refs
        elif has_res:
            x_ref, da_ref, g_ref, sc_ref, r_ref, aux_ref, dx_ref, sums = refs
        else:
            x_ref, da_ref, g_ref, sc_ref, sums = refs
        i = pl.program_id(0)

        @pl.when(i == 0)
        def _():
            sums[...] = jnp.zeros_like(sums)

        xv, da = x_ref[...], da_ref[...]
        rstd = lax.rsqrt(jnp.mean(xv * xv, axis=-1, keepdims=True) + EPS_RMS)
        xh = xv * rstd
        g = g_ref[...]
        dn = da * (1.0 + sc_ref[...])
        sums[0] += _fold8(da)
        sums[1] += _fold8(da * (xh * g))
        sums[2] += _fold8(dn * xh)
        if has_res:
            dxh = dn * g
            dx = rstd * (dxh - xh * jnp.mean(dxh * xh, axis=-1, keepdims=True))
            rv = r_ref[...]
            dx_ref[...] = rv + dx
            sums[3] += _fold8(rv * aux_ref[...])
            if has_gate:
                dm_ref[...] = ((rv + dx) * gate_ref[...]).astype(dm_ref.dtype)

    row = lambda i: (i, 0)
    vec = pl.BlockSpec((1, d), lambda i: (0, 0))
    in_specs = [pl.BlockSpec((ROW_BLOCK, d), row), pl.BlockSpec((ROW_BLOCK, d), lambda i: (i + d_act_row0, 0)), vec, vec]
    args = [x, d_act, gain, scale]
    out_shape = [_sds((4, SUBLANES, d))]
    out_specs = [pl.BlockSpec((4, SUBLANES, d), lambda i: (0, 0, 0))]
    if has_res:
        in_specs += [pl.BlockSpec((ROW_BLOCK, d), row), pl.BlockSpec((ROW_BLOCK, d), row)]
        args += [res, aux]
        if has_gate:
            in_specs.append(vec)
            args.append(gate)
            out_shape = [_sds((rows, d), BF16)] + out_shape
            out_specs = [pl.BlockSpec((ROW_BLOCK, d), row)] + out_specs
        out_shape = [_sds((rows, d))] + out_shape
        out_specs = [pl.BlockSpec((ROW_BLOCK, d), row)] + out_specs
    return pl.pallas_call(
        body, name=name, grid=(nb,), in_specs=in_specs, out_specs=out_specs, out_shape=out_shape,
        compiler_params=_params(("arbitrary",)),
    )(*args)


def _loss_head(h2, target, gain, gate):
    rows, d = h2.shape

    def body(h_ref, t_ref, g_ref, gate_ref, dh_ref, dm_ref, err_ref, dg_ref):
        i = pl.program_id(0)

        @pl.when(i == 0)
        def _():
            err_ref[...] = jnp.zeros_like(err_ref)
            dg_ref[...] = jnp.zeros_like(dg_ref)

        hv = h_ref[...]
        rstd = lax.rsqrt(jnp.mean(hv * hv, axis=-1, keepdims=True) + EPS_RMS)
        xh = hv * rstd
        g = g_ref[...]
        err = xh * g - t_ref[...]
        err_ref[...] += _fold8(err * err)
        dy = err * (1.0 / d)
        dg_ref[...] += _fold8(dy * xh)
        dxh = dy * g
        dh = rstd * (dxh - xh * jnp.mean(dxh * xh, axis=-1, keepdims=True))
        dh_ref[...] = dh
        dm_ref[...] = (dh * gate_ref[...]).astype(dm_ref.dtype)

    row = pl.BlockSpec((ROW_BLOCK, d), lambda i: (i, 0))
    acc = pl.BlockSpec((SUBLANES, d), lambda i: (0, 0))
    vec = pl.BlockSpec((1, d), lambda i: (0, 0))
    return pl.pallas_call(
        body, name="loss_head", grid=(rows // ROW_BLOCK,),
        in_specs=[row, row, vec, vec], out_specs=[row, row, acc, acc],
        out_shape=[_sds((rows, d)), _sds((rows, d), BF16), _sds((SUBLANES, d)), _sds((SUBLANES, d))],
        compiler_params=_params(("arbitrary",)),
    )(h2, target, gain, gate)


def _ada_fwd(cond16, ada_w_loc, ada_b_loc):
    cols = ada_w_loc.shape[1]

    def body(c_ref, w_ref, b_ref, o_ref):
        s = _silu(c_ref[...]).astype(BF16)
        o_ref[...] = _dot(s, w_ref[...].astype(BF16), "nn") + b_ref[...]

    return pl.pallas_call(body, name="ada_fwd", out_shape=_sds((16, cols)), compiler_params=_params())(
        cond16, ada_w_loc, ada_b_loc)


def _ada_bwd(cond16, dmod16, ada_w_loc, c_ctx_row):
    k_, cols = ada_w_loc.shape

    def body(c_ref, dm_ref, w_ref, cc_ref, gw_ref, gc_ref):
        s = _silu(c_ref[...]).astype(BF16)
        dm = dm_ref[...]
        gw_ref[...] = _dot(s, dm.astype(BF16), "tn")
        dmc = jnp.sum(dm[8:16, :], axis=0, keepdims=True)
        dmc8 = jnp.broadcast_to(dmc, (SUBLANES, cols)).astype(BF16)
        ds = _dot(dmc8, w_ref[...].astype(BF16), "nt")
        row = lax.broadcasted_iota(jnp.int32, ds.shape, 0)
        gc_ref[...] = jnp.where(row == 0, ds * _dsilu(cc_ref[...]), 0.0)

    return pl.pallas_call(body, name="ada_bwd", out_shape=[_sds((k_, cols)), _sds((SUBLANES, k_))],
                          compiler_params=_params())(cond16, dmod16, ada_w_loc, c_ctx_row)


def _cmul(a, b):
    return a[0] * b[0] - a[1] * b[1], a[0] * b[1] + a[1] * b[0]


def _disc(lam_re, lam_im, ldt):
    dt = jnp.exp(ldt)
    mag = jnp.exp(lam_re * dt)
    th = lam_im * dt
    a_re, a_im = mag * jnp.cos(th), mag * jnp.sin(th)
    den = lam_re * lam_re + lam_im * lam_im
    n_re = a_re - 1.0
    f_re = (n_re * lam_re + a_im * lam_im) / den
    f_im = (a_im * lam_re - n_re * lam_im) / den
    return dt, mag, th, a_re, a_im, den, n_re, f_re, f_im


def _block_diag_mask(shape):
    row = lax.broadcasted_iota(jnp.int32, shape, 0)
    col = lax.broadcasted_iota(jnp.int32, shape, 1)
    return lax.shift_right_logical(row, 4) == lax.shift_right_logical(col, 6)


TAB_A = 0
TAB_BIG = 1
TAB_SEG = 4
TAB_PW = 5
TAB_ROWS = TAB_PW + STEPS


def _s5_discretise(name, ascending, lam_re, lam_im, ldt, bt_re, bt_im, ct_re, ct_im):
    def write_tables(ref, pw, big, asc, sign):
        row = lax.broadcasted_iota(jnp.int32, (SUBLANES, NSTATE), 0)
        full = lambda v: jnp.broadcast_to(v, (SUBLANES, NSTATE))

        def put(t, p):
            ref[0, t] = full(p[0])
            ref[1, t] = full(sign * p[1])

        put(TAB_A, pw[0])
        for t in range(3):
            put(TAB_BIG + t, big[t])
        seg = [big[0]]
        for _ in range(SEGMENTS - 1):
            seg.append(_cmul(seg[-1], big[0]))
        seg_re = jnp.zeros((SUBLANES, NSTATE), F32)
        seg_im = jnp.zeros((SUBLANES, NSTATE), F32)
        for r in range(SEGMENTS):
            p = seg[r] if asc else seg[SEGMENTS - 1 - r]
            seg_re = jnp.where(row == r, p[0], seg_re)
            seg_im = jnp.where(row == r, sign * p[1], seg_im)
        ref[0, TAB_SEG] = seg_re
        ref[1, TAB_SEG] = seg_im
        for k in range(STEPS):
            put(TAB_PW + k, pw[k])

    def body(lr_ref, li_ref, ldt_ref, br_ref, bi_ref, cr_ref, ci_ref, bb_ref, tab_ref, adj_ref, bm_ref, cm_ref):
        _, _, _, a_re, a_im, _, _, f_re, f_im = _disc(lr_ref[...], li_ref[...], ldt_ref[...])
        bre, bim = br_ref[...], bi_ref[...]
        bb_re = f_re * bre - f_im * bim
        bb_im = f_re * bim + f_im * bre
        bb_ref[0:S5_GROUP, :] = bb_re
        bb_ref[S5_GROUP:2 * S5_GROUP, :] = bb_im
        pw = [(a_re, a_im)]
        for _ in range(STEPS - 1):
            pw.append(_cmul(pw[-1], (a_re, a_im)))
        big = [pw[STEPS - 1]]
        for _ in range(2):
            big.append(_cmul(big[-1], big[-1]))
        write_tables(tab_ref, pw, big, ascending, 1.0)
        write_tables(adj_ref, pw, big, not ascending, -1.0)
        half = NSTATE // S5_BLOCKS
        mask = _block_diag_mask((S5_BLOCK_WIDTH, half))
        tile = lambda v: jnp.broadcast_to(v[None], (S5_BLOCK_WIDTH // S5_GROUP, S5_GROUP, half)).reshape(S5_BLOCK_WIDTH, half)
        for c in range(S5_BLOCKS):
            cols = slice(c * half, (c + 1) * half)
            rows = slice(c * S5_BLOCK_WIDTH, (c + 1) * S5_BLOCK_WIDTH)
            bm_ref[c, :, 0:half] = jnp.where(mask, tile(bb_re[:, cols]), 0.0).astype(BF16)
            bm_ref[c, :, half:2 * half] = jnp.where(mask, tile(bb_im[:, cols]), 0.0).astype(BF16)
            cm_ref[c, :, 0:half] = jnp.where(mask, cr_ref[rows, :], 0.0).astype(BF16)
            cm_ref[c, :, half:2 * half] = jnp.where(mask, -ci_ref[rows, :], 0.0).astype(BF16)

    blocked = _sds((S5_BLOCKS, S5_BLOCK_WIDTH, 2 * NSTATE // S5_BLOCKS), BF16)
    return pl.pallas_call(
        body, name=name,
        out_shape=[_sds((2 * S5_GROUP, NSTATE)), _sds((2, TAB_ROWS, SUBLANES, NSTATE)),
                   _sds((2, TAB_ROWS, SUBLANES, NSTATE)), blocked, blocked],
        compiler_params=_params(),
    )(lam_re, lam_im, ldt, bt_re, bt_im, ct_re, ct_im)


def _s5_discretise_bwd(name, lam_re, lam_im, ldt, bt_re, bt_im, d_abar8, d_bbar):
    def body(lr_ref, li_ref, ldt_ref, br_ref, bi_ref, da_ref, db_ref, dl_ref, dbt_ref):
        lam_re, lam_im = lr_ref[...], li_ref[...]
        dt, mag, _, a_re, a_im, den, n_re, f_re, f_im = _disc(lam_re, lam_im, ldt_ref[...])
        bre, bim = br_ref[...], bi_ref[...]
        dbr, dbi = db_ref[0:S5_GROUP, :], db_ref[S5_GROUP:2 * S5_GROUP, :]
        dbt_ref[0:S5_GROUP, :] = f_re * dbr + f_im * dbi
        dbt_ref[S5_GROUP:2 * S5_GROUP, :] = f_re * dbi - f_im * dbr
        df_re = jnp.sum(bre * dbr + bim * dbi, axis=0, keepdims=True)
        df_im = jnp.sum(bre * dbi - bim * dbr, axis=0, keepdims=True)
        da = da_ref[...]
        da_re = jnp.sum(da[:, 0:NSTATE], axis=0, keepdims=True)
        da_im = jnp.sum(da[:, NSTATE:2 * NSTATE], axis=0, keepdims=True)
        da_re = da_re + (df_re * lam_re - df_im * lam_im) / den
        da_im = da_im + (df_re * lam_im + df_im * lam_re) / den
        ff = (f_re * df_re + f_im * df_im) * 2.0 / den
        d_lr = (df_re * n_re + df_im * a_im) / den - ff * lam_re
        d_li = (df_re * a_im - df_im * n_re) / den - ff * lam_im
        d_mag = (da_re * a_re + da_im * a_im) / mag
        d_th = da_im * a_re - da_re * a_im
        d_lr = d_lr + d_mag * mag * dt
        d_li = d_li + d_th * dt
        d_ldt = (d_mag * mag * lam_re + d_th * lam_im) * dt
        row = lax.broadcasted_iota(jnp.int32, (SUBLANES, NSTATE), 0)
        dl_ref[...] = jnp.where(row == 0, d_lr, jnp.where(row == 1, d_li, jnp.where(row == 2, d_ldt, 0.0)))

    return pl.pallas_call(
        body, name=name, out_shape=[_sds((SUBLANES, NSTATE)), _sds((2 * S5_GROUP, NSTATE))],
        compiler_params=_params(),
    )(lam_re, lam_im, ldt, bt_re, bt_im, d_abar8, d_bbar)


def _segment_permutation():
    rho = jnp.arange(ROW_BLOCK)
    src = STEPS * (rho % SEGMENTS) + rho // SEGMENTS
    return (src[:, None] == jnp.arange(ROW_BLOCK)[None, :]).astype(BF16)


def _permute_rows(perm_ref, v):
    return _dot(perm_ref[...], v, "nn").astype(BF16)


def _unpermute_rows(perm_t_ref, v):
    hi = v.astype(BF16)
    lo = (v - hi.astype(F32)).astype(BF16)
    return _dot(perm_t_ref[...], hi, "nn") + _dot(perm_t_ref[...], lo, "nn")


def _unrolled_loop(step, init):
    def trip(o, state):
        for u in range(SCAN_UNROLL):
            state = step(o * SCAN_UNROLL + u, state)
        return state

    if SCAN_UNROLL == STEPS:
        return trip(0, init)
    return lax.fori_loop(0, STEPS // SCAN_UNROLL, trip, init)


def _scan_chunk(x_ref, out_ref, tab_ref, carry_re, carry_im, ascending, pair_ref=None, acc_ref=None, lane_chunks=None):
    w = SCAN_LANES
    half = NSTATE // S5_BLOCKS
    row = lax.broadcasted_iota(jnp.int32, (SUBLANES, w), 0)
    last = (SEGMENTS - 1) if ascending else 0

    def from_previous_segment(v, k, fill):
        if ascending:
            return jnp.where(row >= k, pltpu.roll(v, k, 0), fill)
        return jnp.where(row < SEGMENTS - k, pltpu.roll(v, SEGMENTS - k, 0), fill)

    def tile_rows(k):
        return pl.ds(pl.multiple_of((k if ascending else STEPS - 1 - k) * SUBLANES, SUBLANES), SUBLANES)

    for j in (range(NSTATE // w) if lane_chunks is None else lane_chunks):
        n_l = pl.ds(j * w, w)
        lane0 = (j * w // half) * 2 * half + (j * w) % half
        re_l, im_l = pl.ds(lane0, w), pl.ds(lane0 + half, w)
        tab = lambda t, n_l=n_l: (tab_ref[0, t, :, n_l], tab_ref[1, t, :, n_l])
        a_re, a_im = tab(TAB_A)

        def local_step(k, h):
            rs = tile_rows(k)
            h_re = a_re * h[0] - a_im * h[1] + x_ref[rs, re_l]
            h_im = a_re * h[1] + a_im * h[0] + x_ref[rs, im_l]
            out_ref[rs, re_l] = h_re
            out_ref[rs, im_l] = h_im
            return h_re, h_im

        zero = jnp.zeros((SUBLANES, w), F32)
        end_re, end_im = _unrolled_loop(local_step, (zero, zero))
        for t, k in ((TAB_BIG, 1), (TAB_BIG + 1, 2), (TAB_BIG + 2, 4)):
            p_re, p_im = tab(t)
            s_re, s_im = from_previous_segment(end_re, k, 0.0), from_previous_segment(end_im, k, 0.0)
            end_re, end_im = end_re + (p_re * s_re - p_im * s_im), end_im + (p_re * s_im + p_im * s_re)
        c0_re, c0_im = carry_re[:, n_l], carry_im[:, n_l]
        p_re, p_im = tab(TAB_SEG)
        end_re = end_re + (p_re * c0_re - p_im * c0_im)
        end_im = end_im + (p_re * c0_im + p_im * c0_re)
        carry_re[:, n_l] = jnp.broadcast_to(end_re[last:last + 1, :], end_re.shape)
        carry_im[:, n_l] = jnp.broadcast_to(end_im[last:last + 1, :], end_im.shape)
        in_re = from_previous_segment(end_re, 1, c0_re)
        in_im = from_previous_segment(end_im, 1, c0_im)

        def carry_step(k, st):
            rs = tile_rows(k)
            p_re, p_im = tab_ref[0, TAB_PW + k, :, n_l], tab_ref[1, TAB_PW + k, :, n_l]
            o_re = out_ref[rs, re_l] + (p_re * in_re - p_im * in_im)
            o_im = out_ref[rs, im_l] + (p_re * in_im + p_im * in_re)
            out_ref[rs, re_l] = o_re
            out_ref[rs, im_l] = o_im
            if pair_ref is None:
                return st
            s_re, s_im = pair_ref[rs, re_l], pair_ref[rs, im_l]
            return (o_re, o_im, st[2] + (st[0] * s_re + st[1] * s_im), st[3] + (st[1] * s_re - st[0] * s_im))

        if pair_ref is None:
            _unrolled_loop(carry_step, 0)
        else:
            fin = _unrolled_loop(carry_step, (in_re, in_im, zero, zero))
            acc_ref[:, n_l] += fin[2]
            acc_ref[:, pl.ds(NSTATE + j * w, w)] += fin[3]


def _scan_block_index(i, n_lat, ctx_first_then_ascending):
    if ctx_first_then_ascending:
        return jnp.where(i == 0, n_lat, i - 1)
    return jnp.where(i == 0, n_lat, n_lat - i)


def _full_spec(shape):
    return pl.BlockSpec(shape, lambda i: (0,) * len(shape))


_S5_BLOCKED = (S5_BLOCKS, S5_BLOCK_WIDTH, 2 * NSTATE // S5_BLOCKS)
_S5_TABLES = (2, TAB_ROWS, SUBLANES, NSTATE)
_S5_DIAG = (S5_BLOCKS, S5_GROUP, 2 * NSTATE // S5_BLOCKS)


def _s5_scan_fwd(name, ascending, z_all, bmat, cmat, tab, perm, perm_t):
    rows = z_all.shape[0]
    nb = rows // ROW_BLOCK
    n_lat = nb - 1
    bw, sw = S5_BLOCK_WIDTH, 2 * NSTATE // S5_BLOCKS

    def body(u_ref, bm_ref, cm_ref, tab_ref, p_ref, pt_ref, s_ref, y_ref, bu, yp, carry_re, carry_im):
        @pl.when(pl.program_id(0) == 0)
        def _():
            carry_re[...] = jnp.zeros_like(carry_re)
            carry_im[...] = jnp.zeros_like(carry_im)

        up = _permute_rows(p_ref, u_ref[...].astype(BF16))
        for c in range(S5_BLOCKS):
            bu[:, c * sw:(c + 1) * sw] = _dot(up[:, c * bw:(c + 1) * bw], bm_ref[c], "nn")
        _scan_chunk(bu, s_ref, tab_ref, carry_re, carry_im, ascending)
        for c in range(S5_BLOCKS):
            yp[:, c * bw:(c + 1) * bw] = _dot(s_ref[:, c * sw:(c + 1) * sw].astype(BF16), cm_ref[c], "nt")
        y_ref[...] = _unpermute_rows(pt_ref, yp[...])

    blk = lambda i: (_scan_block_index(i, n_lat, ascending), 0)
    return pl.pallas_call(
        body, name=name, grid=(nb,),
        in_specs=[pl.BlockSpec((ROW_BLOCK, S5_WIDTH), blk), _full_spec(_S5_BLOCKED), _full_spec(_S5_BLOCKED),
                  _full_spec(_S5_TABLES), _full_spec((ROW_BLOCK, ROW_BLOCK)), _full_spec((ROW_BLOCK, ROW_BLOCK))],
        out_specs=[pl.BlockSpec((ROW_BLOCK, 2 * NSTATE), blk), pl.BlockSpec((ROW_BLOCK, S5_WIDTH), blk)],
        out_shape=[_sds((rows, 2 * NSTATE)), _sds((rows, S5_WIDTH))],
        scratch_shapes=[pltpu.VMEM((ROW_BLOCK, 2 * NSTATE), F32), pltpu.VMEM((ROW_BLOCK, S5_WIDTH), F32),
                        pltpu.VMEM((SUBLANES, NSTATE), F32), pltpu.VMEM((SUBLANES, NSTATE), F32)],
        compiler_params=_params(("arbitrary",)),
    )(z_all, bmat, cmat, tab, perm, perm_t)


def _s5_scan_bwd(name, ascending, dy, z_all, states, bmat, cmat, adj, perm, perm_t):
    rows = states.shape[0]
    nb = rows // ROW_BLOCK
    n_lat = nb - 1
    bw, sw = S5_BLOCK_WIDTH, 2 * NSTATE // S5_BLOCKS

    def block_index(i):
        if ascending:
            return jnp.where(i == nb - 1, n_lat, n_lat - 1 - i)
        return jnp.where(i == nb - 1, n_lat, i)

    def body(dy_ref, u_ref, s_ref, bm_ref, cm_ref, adj_ref, p_ref, pt_ref, du_ref, db_ref, dc_ref, da_ref,
             g, dup, db_acc, dc_acc, carry_re, carry_im):
        i = pl.program_id(0)

        @pl.when(i == 0)
        def _():
            carry_re[...] = jnp.zeros_like(carry_re)
            carry_im[...] = jnp.zeros_like(carry_im)
            da_ref[...] = jnp.zeros_like(da_ref)
            db_acc[...] = jnp.zeros_like(db_acc)
            dc_acc[...] = jnp.zeros_like(dc_acc)

        has_dy = (i < nb - 1).astype(F32)
        dyp = _permute_rows(p_ref, (dy_ref[...] * has_dy).astype(BF16))
        up = _permute_rows(p_ref, u_ref[...].astype(BF16))
        for c in range(S5_BLOCKS):
            g[:, c * sw:(c + 1) * sw] = _dot(dyp[:, c * bw:(c + 1) * bw], cm_ref[c], "nn")
            dc_acc[c] += _dot(dyp[:, c * bw:(c + 1) * bw], s_ref[:, c * sw:(c + 1) * sw].astype(BF16), "tn")
            _scan_chunk(g, g, adj_ref, carry_re, carry_im, not ascending, pair_ref=s_ref, acc_ref=da_ref, lane_chunks=[c])
            gc = g[:, c * sw:(c + 1) * sw].astype(BF16)
            dup[:, c * bw:(c + 1) * bw] = _dot(gc, bm_ref[c], "nt")
            db_acc[c] += _dot(up[:, c * bw:(c + 1) * bw], gc, "tn")
        du_ref[...] = _unpermute_rows(pt_ref, dup[...])

        @pl.when(i == nb - 1)
        def _():
            mask = _block_diag_mask((bw, sw // 2))
            for acc, out in ((db_acc, db_ref), (dc_acc, dc_ref)):
                for c in range(S5_BLOCKS):
                    for part in range(2):
                        cols = slice(part * (sw // 2), (part + 1) * (sw // 2))
                        kept = jnp.where(mask, acc[c, :, cols], 0.0)
                        out[c, :, cols] = kept.reshape(bw // S5_GROUP, S5_GROUP, sw // 2).sum(axis=0)

    blk = lambda i: (block_index(i), 0)
    return pl.pallas_call(
        body, name=name, grid=(nb,),
        in_specs=[pl.BlockSpec((ROW_BLOCK, S5_WIDTH), lambda i: (jnp.minimum(block_index(i), n_lat - 1), 0)),
                  pl.BlockSpec((ROW_BLOCK, S5_WIDTH), blk), pl.BlockSpec((ROW_BLOCK, 2 * NSTATE), blk),
                  _full_spec(_S5_BLOCKED), _full_spec(_S5_BLOCKED), _full_spec(_S5_TABLES),
                  _full_spec((ROW_BLOCK, ROW_BLOCK)), _full_spec((ROW_BLOCK, ROW_BLOCK))],
        out_specs=[pl.BlockSpec((ROW_BLOCK, S5_WIDTH), blk), _full_spec(_S5_DIAG), _full_spec(_S5_DIAG),
                   _full_spec((SUBLANES, 2 * NSTATE))],
        out_shape=[_sds((rows, S5_WIDTH)), _sds(_S5_DIAG), _sds(_S5_DIAG), _sds((SUBLANES, 2 * NSTATE))],
        scratch_shapes=[pltpu.VMEM((ROW_BLOCK, 2 * NSTATE), F32), pltpu.VMEM((ROW_BLOCK, S5_WIDTH), F32),
                        pltpu.VMEM(_S5_BLOCKED, F32), pltpu.VMEM(_S5_BLOCKED, F32),
                        pltpu.VMEM((SUBLANES, NSTATE), F32), pltpu.VMEM((SUBLANES, NSTATE), F32)],
        compiler_params=_params(("arbitrary",)),
    )(dy, z_all, states, bmat, cmat, adj, perm, perm_t)


def _glu_fwd(z_all, y0, y1, d_skip, w_glu, n_rows):
    def body(u_ref, y0_ref, y1_ref, d_ref, w_ref, o_ref):
        y = d_ref[...] * u_ref[...] + y0_ref[...] + y1_ref[...]
        g = _gelu(y)
        t = _dot(g.astype(BF16), w_ref[...], "nn")
        o_ref[...] = (g * _sigmoid(t)).astype(o_ref.dtype)

    row = pl.BlockSpec((ROW_BLOCK, S5_WIDTH), lambda i: (i, 0))
    return pl.pallas_call(
        body, name="glu_fwd", grid=(n_rows // ROW_BLOCK,),
        in_specs=[row, row, row, pl.BlockSpec((1, S5_WIDTH), lambda i: (0, 0)),
                  pl.BlockSpec((S5_WIDTH, S5_WIDTH), lambda i: (0, 0))],
        out_specs=row, out_shape=_sds((n_rows, S5_WIDTH + CONV_WIDTH), BF16), compiler_params=_params(("parallel",)),
    )(z_all, y0, y1, d_skip, w_glu)


def _glu_bwd(d_ycat, z_all, y0, y1, d_skip, w_glu, n_rows):
    def body(do_ref, u_ref, y0_ref, y1_ref, d_ref, w_ref, dy_ref, dw_ref, dd_ref):
        @pl.when(pl.program_id(0) == 0)
        def _():
            dw_ref[...] = jnp.zeros_like(dw_ref)
            dd_ref[...] = jnp.zeros_like(dd_ref)

        u = u_ref[...]
        y = d_ref[...] * u + y0_ref[...] + y1_ref[...]
        g = _gelu(y)
        gb = g.astype(BF16)
        w = w_ref[...]
        sg = _sigmoid(_dot(gb, w, "nn"))
        do = do_ref[...]
        dt = do * g * sg * (1.0 - sg)
        dtb = dt.astype(BF16)
        dg = do * sg + _dot(dtb, w, "nt")
        dy = dg * _dgelu(y)
        dy_ref[...] = dy
        dw_ref[...] += _dot(gb, dtb, "tn")
        dd_ref[...] += _fold8(dy * u)

    row = pl.BlockSpec((ROW_BLOCK, S5_WIDTH), lambda i: (i, 0))
    sq = pl.BlockSpec((S5_WIDTH, S5_WIDTH), lambda i: (0, 0))
    return pl.pallas_call(
        body, name="glu_bwd", grid=(n_rows // ROW_BLOCK,),
        in_specs=[row, row, row, row, pl.BlockSpec((1, S5_WIDTH), lambda i: (0, 0)), sq],
        out_specs=[row, sq, pl.BlockSpec((SUBLANES, S5_WIDTH), lambda i: (0, 0))],
        out_shape=[_sds((n_rows, S5_WIDTH)), _sds((S5_WIDTH, S5_WIDTH)), _sds((SUBLANES, S5_WIDTH))],
        compiler_params=_params(("arbitrary",)),
    )(d_ycat, z_all, y0, y1, d_skip, w_glu)


CONV_HALF = CONV_K // 2


def _conv_block(n_rows):
    blk = min(1024, n_rows)
    assert blk >= CONV_HALF * GRID_W and n_rows % blk == 0
    return blk


def _conv_gate(z_all, n_rows):
    blk = _conv_block(n_rows)
    nb = n_rows // blk

    def body(v_ref, g_ref, o_ref):
        i = pl.program_id(0)
        inside = jnp.logical_and(i >= 1, i <= nb)

        @pl.when(inside)
        def _():
            o_ref[...] = v_ref[...] * _sigmoid(g_ref[...])

        @pl.when(jnp.logical_not(inside))
        def _():
            o_ref[...] = jnp.zeros_like(o_ref)

    src = lambda col: pl.BlockSpec((blk, CONV_WIDTH), lambda i: (jnp.clip(i - 1, 0, nb - 1), col))
    return pl.pallas_call(
        body, name="conv_gate", grid=(nb + 2,), in_specs=[src(1), src(2)],
        out_specs=pl.BlockSpec((blk, CONV_WIDTH), lambda i: (i, 0)),
        out_shape=_sds(((nb + 2) * blk, CONV_WIDTH)), compiler_params=_params(("parallel",)),
    )(z_all, z_all)


def _stream_padded(pad_ref, buf, sems, blk, n_blocks):
    i = pl.program_id(0)

    def copy(b):
        rows = pl.ds(pl.multiple_of(b * blk, blk), blk)
        return pltpu.make_async_copy(pad_ref.at[rows, :], buf.at[rows, :], sems.at[b])

    @pl.when(i == 0)
    def _():
        for b in range(n_blocks):
            copy(b).start()
        copy(0).wait()
        copy(1).wait()

    copy(i + 2).wait()
    return pl.multiple_of(i * blk, blk)


def _conv_fwd(hh_pad, w, b, ln_g, ln_b, ycat, n_rows):
    blk = _conv_block(n_rows)
    nblk = n_rows // blk + 2

    def body(hh_ref, w_ref, b_ref, g_ref, lb_ref, ycat_ref, hc_ref, y_ref, win, sems):
        base = _stream_padded(hh_ref, win, sems, blk, nblk)

        def tile(t, _):
            r0 = pl.multiple_of(t * CONV_ROWS, CONV_ROWS)
            acc = jnp.zeros((CONV_ROWS, CONV_WIDTH), F32)
            for k in range(CONV_K):
                acc = acc + w_ref[k:k + 1, :] * win[pl.ds(base + r0 + blk + (k - CONV_HALF) * GRID_W, CONV_ROWS), :]
            hc = acc + b_ref[...]
            hc_ref[pl.ds(r0, CONV_ROWS), :] = hc
            mu = jnp.mean(hc, axis=-1, keepdims=True)
            xc = hc - mu
            ln = xc * lax.rsqrt(jnp.mean(xc * xc, axis=-1, keepdims=True) + EPS_LN) * g_ref[...] + lb_ref[...]
            y_ref[pl.ds(r0, CONV_ROWS), :] = _silu(ln).astype(y_ref.dtype)
            return 0

        lax.fori_loop(0, blk // CONV_ROWS, tile, 0)

    vec = pl.BlockSpec((1, CONV_WIDTH), lambda i: (0, 0))
    row = pl.BlockSpec((blk, CONV_WIDTH), lambda i: (i, 0))
    return pl.pallas_call(
        body, name="conv_fwd", grid=(n_rows // blk,),
        in_specs=[ANY, pl.BlockSpec((CONV_K, CONV_WIDTH), lambda i: (0, 0)), vec, vec, vec, ANY],
        out_specs=[row, pl.BlockSpec((blk, CONV_WIDTH), lambda i: (i, 1))],
        out_shape=[_sds((n_rows, CONV_WIDTH)), _sds(ycat.shape, ycat.dtype)], input_output_aliases={5: 1},
        scratch_shapes=[pltpu.VMEM((nblk * blk, CONV_WIDTH), F32), pltpu.SemaphoreType.DMA((nblk,))],
        compiler_params=_params(("arbitrary",)),
    )(hh_pad, w, b, ln_g, ln_b, ycat)


def _conv_bwd_norm(d_ycat, hc, ln_g, ln_b, n_rows):
    blk = _conv_block(n_rows)
    nb = n_rows // blk

    def body(dy_ref, hc_ref, g_ref, lb_ref, o_ref, sums):
        i = pl.program_id(0)

        @pl.when(i == 0)
        def _():
            sums[...] = jnp.zeros_like(sums)

        inside = jnp.logical_and(i >= 1, i <= nb)

        @pl.when(inside)
        def _():
            hcv = hc_ref[...]
            mu = jnp.mean(hcv, axis=-1, keepdims=True)
            xc = hcv - mu
            rstd = lax.rsqrt(jnp.mean(xc * xc, axis=-1, keepdims=True) + EPS_LN)
            xh = xc * rstd
            g = g_ref[...]
            dln = dy_ref[...] * _dsilu(xh * g + lb_ref[...])
            dxh = dln * g
            dhc = rstd * (dxh - jnp.mean(dxh, axis=-1, keepdims=True) - xh * jnp.mean(dxh * xh, axis=-1, keepdims=True))
            o_ref[...] = dhc
            sums[0] += _fold8(dhc)
            sums[1] += _fold8(dln * xh)
            sums[2] += _fold8(dln)

        @pl.when(jnp.logical_not(inside))
        def _():
            o_ref[...] = jnp.zeros_like(o_ref)

    vec = pl.BlockSpec((1, CONV_WIDTH), lambda i: (0, 0))
    return pl.pallas_call(
        body, name="conv_bwd_norm", grid=(nb + 2,),
        in_specs=[pl.BlockSpec((blk, CONV_WIDTH), lambda i: (jnp.clip(i - 1, 0, nb - 1), 1)),
                  pl.BlockSpec((blk, CONV_WIDTH), lambda i: (jnp.clip(i - 1, 0, nb - 1), 0)), vec, vec],
        out_specs=[pl.BlockSpec((blk, CONV_WIDTH), lambda i: (i, 0)),
                   pl.BlockSpec((3, SUBLANES, CONV_WIDTH), lambda i: (0, 0, 0))],
        out_shape=[_sds(((nb + 2) * blk, CONV_WIDTH)), _sds((3, SUBLANES, CONV_WIDTH))],
        compiler_params=_params(("arbitrary",)),
    )(d_ycat, hc, ln_g, ln_b)


def _conv_bwd_taps(dhc_pad, hh_pad, z_all, w, n_rows):
    blk = _conv_block(n_rows)
    nblk = n_rows // blk + 2

    def body(dhc_ref, hh_ref, v_ref, g_ref, w_ref, dv_ref, dg_ref, dw_ref, dwin, hwin, dsems, hsems):
        @pl.when(pl.program_id(0) == 0)
        def _():
            dw_ref[...] = jnp.zeros_like(dw_ref)

        base = _stream_padded(dhc_ref, dwin, dsems, blk, nblk)
        _stream_padded(hh_ref, hwin, hsems, blk, nblk)

        def tile(t, _):
            r0 = pl.multiple_of(t * CONV_BWD_ROWS, CONV_BWD_ROWS) + base
            dh = dwin[pl.ds(r0 + blk, CONV_BWD_ROWS), :]
            acc = jnp.zeros((CONV_BWD_ROWS, CONV_WIDTH), F32)
            for k in range(CONV_K):
                off = (k - CONV_HALF) * GRID_W
                acc = acc + w_ref[k:k + 1, :] * dwin[pl.ds(r0 + blk - off, CONV_BWD_ROWS), :]
                dw_ref[k] += _fold8(dh * hwin[pl.ds(r0 + blk + off, CONV_BWD_ROWS), :])
            rs = pl.ds(pl.multiple_of(t * CONV_BWD_ROWS, CONV_BWD_ROWS), CONV_BWD_ROWS)
            sg = _sigmoid(g_ref[rs, :])
            vv = v_ref[rs, :]
            dv_ref[rs, :] = acc * sg
            dg_ref[rs, :] = acc * vv * sg * (1.0 - sg)
            return 0

        lax.fori_loop(0, blk // CONV_BWD_ROWS, tile, 0)

    row = pl.BlockSpec((blk, CONV_WIDTH), lambda i: (i, 0))
    return pl.pallas_call(
        body, name="conv_bwd_taps", grid=(n_rows // blk,),
        in_specs=[ANY, ANY,
            pl.BlockSpec((blk, CONV_WIDTH), lambda i: (i, 1)), pl.BlockSpec((blk, CONV_WIDTH), lambda i: (i, 2)),
            pl.BlockSpec((CONV_K, CONV_WIDTH), lambda i: (0, 0))],
        out_specs=[row, row, pl.BlockSpec((CONV_K, SUBLANES, CONV_WIDTH), lambda i: (0, 0, 0))],
        out_shape=[_sds((n_rows, CONV_WIDTH)), _sds((n_rows, CONV_WIDTH)), _sds((CONV_K, SUBLANES, CONV_WIDTH))],
        scratch_shapes=[pltpu.VMEM((nblk * blk, CONV_WIDTH), F32), pltpu.VMEM((nblk * blk, CONV_WIDTH), F32),
                        pltpu.SemaphoreType.DMA((nblk,)), pltpu.SemaphoreType.DMA((nblk,))],
        compiler_params=_params(("arbitrary",)),
    )(dhc_pad, hh_pad, z_all, z_all, w)


def _dz_assemble(du0, du1, dy, d_skip, dv, dgate, n_lat):
    rows = du0.shape[0]
    nb = rows // ROW_BLOCK

    w = S5_WIDTH

    def body(a_ref, b_ref, dy_ref, d_ref, dv_ref, dg_ref, o_ref):
        lat = pl.program_id(0) < n_lat

        @pl.when(lat)
        def _():
            o_ref[:, 0:w] = (a_ref[...] + b_ref[...] + dy_ref[...] * d_ref[...]).astype(o_ref.dtype)
            o_ref[:, w:2 * w] = dv_ref[...].astype(o_ref.dtype)
            o_ref[:, 2 * w:3 * w] = dg_ref[...].astype(o_ref.dtype)

        @pl.when(jnp.logical_not(lat))
        def _():
            o_ref[:, 0:w] = (a_ref[...] + b_ref[...]).astype(o_ref.dtype)
            o_ref[:, w:3 * w] = jnp.zeros((ROW_BLOCK, 2 * w), o_ref.dtype)

    all_rows = pl.BlockSpec((ROW_BLOCK, w), lambda i: (i, 0))
    lat_rows = pl.BlockSpec((ROW_BLOCK, w), lambda i: (jnp.minimum(i, n_lat - 1), 0))
    return pl.pallas_call(
        body, name="dz_assemble", grid=(nb,),
        in_specs=[all_rows, all_rows, lat_rows, pl.BlockSpec((1, w), lambda i: (0, 0)), lat_rows, lat_rows],
        out_specs=pl.BlockSpec((ROW_BLOCK, IN_COLS), lambda i: (i, 0)),
        out_shape=_sds((rows, IN_COLS), BF16), compiler_params=_params(("parallel",)),
    )(du0, du1, dy, d_skip, dv, dgate)


def _sum_parts(parts):
    _, r, c = parts.shape

    def body(p_ref, o_ref):
        acc = p_ref[0]
        for q in range(1, NDEV):
            acc = acc + p_ref[q]
        o_ref[...] = acc

    return pl.pallas_call(body, name="sum_parts", out_shape=_sds((r, c)), compiler_params=_params())(parts)


def _row_tile(r, c):
    best = r
    for t in (1024, 512, 256, 128, 64, 32, 16, 8):
        if r % t == 0 and t * c <= 128 * 1024:
            return t
    return best


def _adamw(name, w, gparts, m, v):
    r, c = w.shape
    np_ = gparts.shape[0]
    tr = _row_tile(r, c)

    def body(w_ref, g_ref, m_ref, v_ref, go_ref, d_ref, mo_ref, vo_ref):
        g = g_ref[0].astype(F32)
        for q in range(1, np_):
            g = g + g_ref[q].astype(F32)
        m2 = ADAM_B1 * m_ref[...] + (1.0 - ADAM_B1) * g
        v2 = ADAM_B2 * v_ref[...] + (1.0 - ADAM_B2) * jnp.square(g)
        m_hat = m2 / (1.0 - ADAM_B1 ** ADAM_STEP)
        v_hat = v2 / (1.0 - ADAM_B2 ** ADAM_STEP)
        go_ref[...] = g
        d_ref[...] = -ADAM_LR * (m_hat / (jnp.sqrt(v_hat) + ADAM_EPS) + ADAM_WD * w_ref[...])
        mo_ref[...] = m2
        vo_ref[...] = v2

    row = pl.BlockSpec((tr, c), lambda i: (i, 0))
    return pl.pallas_call(
        body, name=name, grid=(r // tr,),
        in_specs=[row, pl.BlockSpec((np_, tr, c), lambda i: (0, i, 0)), row, row],
        out_specs=[row] * 4, out_shape=[_sds((r, c))] * 4, compiler_params=_params(("parallel",)),
    )(w, gparts, m, v)


def _adamw_native(name, w, g, m, v):
    def body(w_ref, g_ref, m_ref, v_ref, d_ref, mo_ref, vo_ref):
        gv = g_ref[...]
        m2 = ADAM_B1 * m_ref[...] + (1.0 - ADAM_B1) * gv
        v2 = ADAM_B2 * v_ref[...] + (1.0 - ADAM_B2) * jnp.square(gv)
        m_hat = m2 / (1.0 - ADAM_B1 ** ADAM_STEP)
        v_hat = v2 / (1.0 - ADAM_B2 ** ADAM_STEP)
        d_ref[...] = -ADAM_LR * (m_hat / (jnp.sqrt(v_hat) + ADAM_EPS) + ADAM_WD * w_ref[...])
        mo_ref[...] = m2
        vo_ref[...] = v2

    return pl.pallas_call(body, name=name, out_shape=[_sds(w.shape)] * 3, compiler_params=_params())(w, g, m, v)


SMALL = ["c_ctx", "ada_b", "norm1_g", "s5_lam_re", "s5_lam_im", "s5_log_dt", "s5_d", "conv_b", "conv_ln_g", "conv_ln_b",
         "norm2_g", "final_g"]
SMALL_PACKED_ROWS = 24


def _pack_rows(parts, rows):
    flat = jnp.concatenate([p.reshape(-1).astype(F32) for p in parts])
    return jnp.pad(flat, (0, rows * D_MODEL - flat.shape[0])).reshape(rows, D_MODEL)


def _unpack_rows(packed, shapes):
    flat = packed.reshape(-1)
    out, off = [], 0
    for shape in shapes:
        size = 1
        for s in shape:
            size *= s
        out.append(flat[off:off + size].reshape(shape))
        off += size
    return out


def kernel(x, c, ctx, c_ctx, ada_w, ada_b, norm1_g, w_in, s5_lam_re, s5_lam_im, s5_log_dt, s5_b_re, s5_b_im, s5_c_re, s5_c_im, s5_d, s5_w_glu, conv_w, conv_b, conv_ln_g, conv_ln_b, w_out, norm2_g, mlp_w1, mlp_w2, final_g, loss_target, m_c_ctx, m_ada_w, m_ada_b, m_norm1_g, m_w_in, m_s5_lam_re, m_s5_lam_im, m_s5_log_dt, m_s5_b_re, m_s5_b_im, m_s5_c_re, m_s5_c_im, m_s5_d, m_s5_w_glu, m_conv_w, m_conv_b, m_conv_ln_g, m_conv_ln_b, m_w_out, m_norm2_g, m_mlp_w1, m_mlp_w2, m_final_g, v_c_ctx, v_ada_w, v_ada_b, v_norm1_g, v_w_in, v_s5_lam_re, v_s5_lam_im, v_s5_log_dt, v_s5_b_re, v_s5_b_im, v_s5_c_re, v_s5_c_im, v_s5_d, v_s5_w_glu, v_conv_w, v_conv_b, v_conv_ln_g, v_conv_ln_b, v_w_out, v_norm2_g, v_mlp_w1, v_mlp_w2, v_final_g):
    weights = dict(c_ctx=c_ctx, ada_w=ada_w, ada_b=ada_b, norm1_g=norm1_g, w_in=w_in, s5_lam_re=s5_lam_re, s5_lam_im=s5_lam_im, s5_log_dt=s5_log_dt, s5_b_re=s5_b_re, s5_b_im=s5_b_im, s5_c_re=s5_c_re, s5_c_im=s5_c_im, s5_d=s5_d, s5_w_glu=s5_w_glu, conv_w=conv_w, conv_b=conv_b, conv_ln_g=conv_ln_g, conv_ln_b=conv_ln_b, w_out=w_out, norm2_g=norm2_g, mlp_w1=mlp_w1, mlp_w2=mlp_w2, final_g=final_g)
    mom1 = dict(c_ctx=m_c_ctx, ada_w=m_ada_w, ada_b=m_ada_b, norm1_g=m_norm1_g, w_in=m_w_in, s5_lam_re=m_s5_lam_re, s5_lam_im=m_s5_lam_im, s5_log_dt=m_s5_log_dt, s5_b_re=m_s5_b_re, s5_b_im=m_s5_b_im, s5_c_re=m_s5_c_re, s5_c_im=m_s5_c_im, s5_d=m_s5_d, s5_w_glu=m_s5_w_glu, conv_w=m_conv_w, conv_b=m_conv_b, conv_ln_g=m_conv_ln_g, conv_ln_b=m_conv_ln_b, w_out=m_w_out, norm2_g=m_norm2_g, mlp_w1=m_mlp_w1, mlp_w2=m_mlp_w2, final_g=m_final_g)
    mom2 = dict(c_ctx=v_c_ctx, ada_w=v_ada_w, ada_b=v_ada_b, norm1_g=v_norm1_g, w_in=v_w_in, s5_lam_re=v_s5_lam_re, s5_lam_im=v_s5_lam_im, s5_log_dt=v_s5_log_dt, s5_b_re=v_s5_b_re, s5_b_im=v_s5_b_im, s5_c_re=v_s5_c_re, s5_c_im=v_s5_c_im, s5_d=v_s5_d, s5_w_glu=v_s5_w_glu, conv_w=v_conv_w, conv_b=v_conv_b, conv_ln_g=v_conv_ln_g, conv_ln_b=v_conv_ln_b, w_out=v_w_out, norm2_g=v_norm2_g, mlp_w1=v_mlp_w1, mlp_w2=v_mlp_w2, final_g=v_final_g)
    order = list(weights)

    me = 4 * lax.axis_index("x") + 2 * lax.axis_index("y") + lax.axis_index("c")
    xs, cs, tgt = x[0], ctx[0], loss_target[0]
    n_lat_rows, n_ctx_rows = xs.shape[0], cs.shape[0]
    n_rows = n_lat_rows + n_ctx_rows
    n_lat = n_lat_rows // ROW_BLOCK
    ada_cols = ada_w.shape[2]

    (c_all,), _ = _exchange("gather_c", [c], [True])
    c_all = c_all.reshape(NDEV, D_MODEL)

    cond_fwd = jnp.concatenate([c_all, c_ctx[None], jnp.zeros((7, D_MODEL), F32)])
    ada_b_loc = lax.dynamic_slice(ada_b, (0, me * ada_cols), (1, ada_cols))
    (mod_g,), mod_token = _exchange("gather_mod", [_ada_fwd(cond_fwd, ada_w[0], ada_b_loc)], [True])
    wi_send, wi_recv, wi_src, wi_land, wi_token = _exchange_start(
        "gather_w_in_start", [w_in[0].astype(BF16) + mod_token[0:1, 0:1].astype(BF16)], [True])
    mixer_w = [s5_w_glu[0].astype(BF16), conv_w[0] + wi_token[0:1, 0:1], w_out[0].astype(BF16)]
    mixer_send, mixer_recv, mixer_src, mixer_land, mixer_token = _exchange_start("gather_mixer_start", mixer_w, [True] * 3)
    mlp_w = [mlp_w1[0].astype(BF16), mlp_w2[0].astype(BF16) + mixer_token[0:1, 0:1].astype(BF16)]
    mlpw_send, mlpw_recv, mlpw_src, mlpw_land, mlpw_token = _exchange_start("gather_mlp_start", mlp_w, [True] * 2)
    mod_rows = jnp.transpose(mod_g, (1, 0, 2)).reshape(16, 6 * D_MODEL) + mlpw_token[0:1, 0:1]
    mod = lax.dynamic_slice(mod_rows, (me, 0), (1, 6 * D_MODEL)).reshape(6, D_MODEL)
    modc = mod_rows[8, :2 * D_MODEL].reshape(2, D_MODEL)
    sh1, sc1, g1, sh2, sc2, g2 = [mod[i:i + 1] for i in range(6)]

    a_all = _prenorm("prenorm1", xs, cs, norm1_g, jnp.stack([mod[0:2], modc]))
    wi_own, wi_landed = _exchange_wait("gather_w_in_wait", wi_send, wi_recv, wi_src, wi_land, [True], a_all)
    w_in_full = jnp.transpose(_with_own(wi_landed[0], wi_own[0], me), (1, 0, 2)).reshape(D_MODEL, IN_COLS)
    tm_all = 1088 if n_rows % 1088 == 0 else ROW_BLOCK
    (z_all,) = _matmul("in_proj", a_all, w_in_full, "nn", (n_rows, IN_COLS, D_MODEL), (tm_all, IN_COLS, D_MODEL),
                       [((n_rows, IN_COLS), F32)])

    lam_re, lam_im = s5_lam_re[0].reshape(2, 1, NSTATE), s5_lam_im[0].reshape(2, 1, NSTATE)
    ldt = jnp.repeat(s5_log_dt[0], S5_STATE, axis=-1).reshape(2, 1, NSTATE)
    bt_re = jnp.transpose(s5_b_re[0], (0, 3, 1, 2)).reshape(2, S5_GROUP, NSTATE)
    bt_im = jnp.transpose(s5_b_im[0], (0, 3, 1, 2)).reshape(2, S5_GROUP, NSTATE)
    groups_per_block = S5_GROUPS // S5_BLOCKS
    ct_re = jnp.tile(s5_c_re[0].reshape(2, S5_WIDTH, S5_STATE), (1, 1, groups_per_block))
    ct_im = jnp.tile(s5_c_im[0].reshape(2, S5_WIDTH, S5_STATE), (1, 1, groups_per_block))
    d_skip = s5_d[0].reshape(1, S5_WIDTH)
    perm = _segment_permutation()
    perm_t = perm.T
    disc, states, y_dir = [], [], []
    for d in range(2):
        disc.append(_s5_discretise(f"s5_disc{d}", d == 0, lam_re[d], lam_im[d], ldt[d], bt_re[d], bt_im[d], ct_re[d], ct_im[d]))
        _, tab, _, bmat, cmat = disc[d]
        s, yd = _s5_scan_fwd(f"s5_scan_fwd{d}", d == 0, z_all, bmat, cmat, tab, perm, perm_t)
        states.append(s)
        y_dir.append(yd)
    mixer_own, mixer_landed = _exchange_wait("gather_mixer_wait", mixer_send, mixer_recv, mixer_src, mixer_land,
                                             [True] * 3, y_dir[1])
    glu_g, conv_w_g, w_out_g = [_with_own(l, o, me) for l, o in zip(mixer_landed, mixer_own)]
    glu_full = glu_g.reshape(S5_WIDTH, S5_WIDTH)
    conv_w_full = jnp.transpose(conv_w_g, (1, 0, 2)).reshape(CONV_K, CONV_WIDTH)
    w_out_full = w_out_g.reshape(D_MODEL, D_MODEL)
    ycat = _glu_fwd(z_all, y_dir[0], y_dir[1], d_skip, glu_full, n_lat_rows)

    hh_pad = _conv_gate(z_all, n_lat_rows)
    hc, ycat = _conv_fwd(hh_pad, conv_w_full, conv_b, conv_ln_g, conv_ln_b, ycat, n_lat_rows)

    tm = min(1024, n_lat_rows)
    tm_e = min(512, n_lat_rows)
    w1_cols = D_FF // NDEV
    row_vec = lambda tn: pl.BlockSpec((1, tn), lambda i, j, k: (0, j))
    out_tile = lambda t_m, t_n: pl.BlockSpec((t_m, t_n), lambda i, j, k: (i, j))
    full_rows = ((n_lat_rows, D_MODEL), F32)
    sums = ((n_lat_rows // tm_e, SUBLANES, D_MODEL), F32)
    sums_spec = pl.BlockSpec((None, SUBLANES, D_MODEL), lambda i, j, k: (i, 0, 0))
    vec = lambda v: (v, row_vec(D_MODEL))
    transposed_tile = lambda t_m, t_n: pl.BlockSpec((t_n, t_m), lambda i, j, k: (j, i))
    mix, h1, a2, a2_t = _matmul(
        "out_proj", ycat, w_out_full, "nn", (n_lat_rows, D_MODEL, D_MODEL), (tm_e, D_MODEL, D_MODEL),
        [full_rows, full_rows, ((n_lat_rows, D_MODEL), BF16), ((D_MODEL, n_lat_rows), BF16)],
        epi=_epi_residual_prenorm,
        epi_extra=[(xs, out_tile(tm_e, D_MODEL)), vec(g1), vec(norm2_g), vec(sc2), vec(sh2)],
        out_specs=[out_tile(tm_e, D_MODEL)] * 3 + [transposed_tile(tm_e, D_MODEL)])
    mlpw_own, mlpw_landed = _exchange_wait("gather_mlp_wait", mlpw_send, mlpw_recv, mlpw_src, mlpw_land, [True] * 2, a2)
    w1_g, w2_g = [_with_own(l, o, me) for l, o in zip(mlpw_landed, mlpw_own)]
    w2_full = w2_g.reshape(D_FF, D_MODEL)
    tm_up = min(2048, n_lat_rows)
    f, f_t = _matmul("mlp_up", a2, w1_g, "nn", (n_lat_rows, D_FF, D_MODEL), (tm_up, w1_cols, D_MODEL),
                     [((n_lat_rows, D_FF), BF16), ((D_FF, n_lat_rows), BF16)], epi=lambda acc: (acc, acc.T),
                     b_spec=pl.BlockSpec((None, D_MODEL, w1_cols), lambda i, j, k: (j, 0, 0)),
                     out_specs=[out_tile(tm_up, w1_cols), transposed_tile(tm_up, w1_cols)])
    sq_relu = lambda t: jnp.square(jnp.maximum(t, 0.0))
    mlp_out, d_h2, dm2, err_sums, d_final_g8 = _matmul(
        "mlp_down", f, w2_full, "nn", (n_lat_rows, D_MODEL, D_FF), (tm_e, D_MODEL, 2048),
        [full_rows, full_rows, ((n_lat_rows, D_MODEL), BF16), sums, sums], a_fn=sq_relu, epi=_epi_residual_loss,
        epi_extra=[(h1, out_tile(tm_e, D_MODEL)), vec(g2), (tgt, out_tile(tm_e, D_MODEL)), vec(final_g[None])],
        out_specs=[out_tile(tm_e, D_MODEL)] * 3 + [sums_spec] * 2)

    (d_f,) = _matmul("mlp_down_dx", dm2, w2_full, "nt", (n_lat_rows, D_FF, D_MODEL), (tm_up, 512, D_MODEL),
                     [((n_lat_rows, D_FF), BF16)],
                     epi=lambda acc, ft: (acc * 2.0 * jnp.maximum(ft.astype(F32), 0.0),),
                     epi_extra=[(f, out_tile(tm_up, 512))])
    tk_dw = min(2048, n_lat_rows)
    (g_w2,) = _matmul("mlp_down_dw", f_t, dm2, "nn", (D_FF, D_MODEL, n_lat_rows), (1024, D_MODEL, tk_dw),
                      [((D_FF, D_MODEL), F32)], a_fn=sq_relu)
    (g_w1,) = _matmul("mlp_up_dw", a2_t, d_f, "nn", (D_MODEL, D_FF, n_lat_rows), (D_MODEL, w1_cols, n_lat_rows),
                      [((NDEV, D_MODEL, w1_cols), F32)],
                      out_specs=[pl.BlockSpec((None, D_MODEL, w1_cols), lambda i, j, k: (j, 0, 0))])
    mlp_send, mlp_recv, mlp_src, mlp_land, mlp_token = _exchange_start(
        "scatter_mlp_start", [g_w1, g_w2.reshape(NDEV, D_FF // NDEV, D_MODEL)], [False] * 2)
    d_h1, dm1, *sums2 = _matmul(
        "mlp_up_dx", d_f, w1_g, "nt", (n_lat_rows, D_MODEL, D_FF), (tm_e, D_MODEL, 4 * w1_cols),
        [full_rows, ((n_lat_rows, D_MODEL), BF16)] + [sums] * 4, epi=_epi_norm_bwd,
        epi_extra=[(h1, out_tile(tm_e, D_MODEL)), (d_h2, out_tile(tm_e, D_MODEL)), (mlp_out, out_tile(tm_e, D_MODEL)),
                   vec(norm2_g), vec(sc2 + mlp_token[0:1, 0:1]), vec(g1)],
        b_spec=pl.BlockSpec((4, D_MODEL, w1_cols), lambda i, j, k: (k, 0, 0)), b_slabs=4,
        out_specs=[out_tile(tm_e, D_MODEL)] * 2 + [sums_spec] * 4)

    (d_ycat,) = _matmul("out_proj_dx", dm1, w_out_full, "nt", (n_lat_rows, D_MODEL, D_MODEL), (tm, D_MODEL, D_MODEL),
                        [((n_lat_rows, D_MODEL), F32)])
    (g_w_out,) = _matmul("out_proj_dw", ycat, dm1, "tn", (D_MODEL, D_MODEL, n_lat_rows), (D_MODEL, D_MODEL, 512),
                         [((D_MODEL, D_MODEL), F32)])

    dy, g_glu, dd8 = _glu_bwd(d_ycat, z_all, y_dir[0], y_dir[1], d_skip, glu_full, n_lat_rows)
    proj_send, proj_recv, proj_src, proj_land, proj_token = _exchange_start(
        "scatter_proj_start",
        [g_w_out.reshape(NDEV, D_MODEL // NDEV, D_MODEL), g_glu.reshape(NDEV, S5_WIDTH // NDEV, S5_WIDTH)], [False] * 2)
    perm = perm + proj_token[0:1, 0:1].astype(BF16)
    du, g_lam_re, g_lam_im, g_ldt, g_bt, g_cdiag = [], [], [], [], [], []
    for d in range(2):
        _, _, adj, bmat, cmat = disc[d]
        du_d, d_bdiag, d_cdiag, d_abar8 = _s5_scan_bwd(f"s5_scan_bwd{d}", d == 0, dy, z_all, states[d], bmat, cmat, adj,
                                                       perm, perm_t)
        du.append(du_d)
        d_bbar = jnp.transpose(d_bdiag.reshape(S5_BLOCKS, S5_GROUP, 2, NSTATE // S5_BLOCKS), (2, 1, 0, 3)).reshape(
            2 * S5_GROUP, NSTATE)
        d_lam8, d_bt = _s5_discretise_bwd(f"s5_disc_bwd{d}", lam_re[d], lam_im[d], ldt[d], bt_re[d], bt_im[d], d_abar8, d_bbar)
        g_lam_re.append(d_lam8[0].reshape(S5_GROUPS, S5_STATE))
        g_lam_im.append(d_lam8[1].reshape(S5_GROUPS, S5_STATE))
        g_ldt.append(d_lam8[2].reshape(S5_GROUPS, S5_STATE).sum(axis=-1))
        g_bt.append(d_bt)
        g_cdiag.append(d_cdiag)

    dhc_pad, conv_sums = _conv_bwd_norm(d_ycat, hc, conv_ln_g, conv_ln_b, n_lat_rows)
    d_v, d_gate, g_conv_w8 = _conv_bwd_taps(dhc_pad, hh_pad, z_all, conv_w_full, n_lat_rows)

    dz_all = _dz_assemble(du[0], du[1], dy, d_skip, d_v, d_gate, n_lat)
    (g_w_in_full,) = _matmul("in_proj_dw", a_all, dz_all, "tn", (D_MODEL, IN_COLS, n_rows), (D_MODEL, IN_COLS, tm_all),
                             [((D_MODEL, IN_COLS), F32)])
    g_w_in_parts = jnp.transpose(g_w_in_full.reshape(D_MODEL, NDEV, IN_COLS // NDEV), (1, 0, 2)).astype(BF16)
    win_send, win_recv, win_src, win_land, win_token = _exchange_start("scatter_w_in_start", [g_w_in_parts], [False])
    (d_a_all,) = _matmul("in_proj_dx", dz_all, w_in_full + win_token[0:1, 0:1].astype(BF16), "nt",
                         (n_rows, D_MODEL, IN_COLS), (tm_all, D_MODEL, IN_COLS), [((n_rows, D_MODEL), F32)])
    grad_x, sums1 = _norm_bwd("norm1_bwd", xs, d_a_all, 0, norm1_g, sc1, res=d_h1, aux=mix)
    (sums1c,) = _norm_bwd("norm1_bwd_ctx", cs, d_a_all, n_lat, norm1_g, modc[1:2])

    s1, s1c, s2 = sums1.sum(axis=1), sums1c.sum(axis=1), [p.sum(axis=(0, 1)) for p in sums2]
    d_mod = jnp.concatenate([s1[0], s1[1], s1[3], s2[0], s2[1], s2[3]])
    d_modc = jnp.concatenate([s1c[0], s1c[1], jnp.zeros((4 * D_MODEL,), F32)])
    (dmod_g,), _ = _exchange("gather_dmod", [jnp.stack([d_mod, d_modc])], [True])
    dmod16 = jnp.concatenate([dmod_g[:, 0], dmod_g[:, 1]])
    dmod16_loc = lax.dynamic_slice(dmod16, (0, me * ada_cols), (16, ada_cols))
    cond_bwd = jnp.concatenate([c_all, jnp.broadcast_to(c_ctx[None], (NDEV, D_MODEL))])
    g_ada_w, g_c_ctx8 = _ada_bwd(cond_bwd, dmod16_loc, ada_w[0], c_ctx[None])

    small_parts = dict(
        c_ctx=g_c_ctx8[0], ada_b=d_mod + d_modc, norm1_g=s1[2] + s1c[2],
        s5_lam_re=jnp.stack(g_lam_re), s5_lam_im=jnp.stack(g_lam_im), s5_log_dt=jnp.stack(g_ldt),
        s5_d=dd8.sum(axis=0), conv_b=conv_sums[0].sum(axis=0), conv_ln_g=conv_sums[1].sum(axis=0),
        conv_ln_b=conv_sums[2].sum(axis=0), norm2_g=s2[2], final_g=d_final_g8.sum(axis=(0, 1)))
    reduced_shapes = [(SMALL_PACKED_ROWS, D_MODEL), (2, 2 * S5_GROUP, NSTATE), (2,) + _S5_DIAG, (1,)]
    small_g = _pack_rows(
        [_pack_rows([small_parts[n] for n in SMALL], SMALL_PACKED_ROWS), jnp.stack(g_bt), jnp.stack(g_cdiag),
         (0.5 / D_MODEL * jnp.sum(err_sums)).reshape(1)], SMALL_ROWS).reshape(NDEV, SMALL_ROWS // NDEV, D_MODEL)
    g_conv_w_parts = jnp.transpose(g_conv_w8.sum(axis=1).reshape(CONV_K, NDEV, CONV_WIDTH // NDEV), (1, 0, 2))

    res = {}

    def own_chunk(src):
        return lax.dynamic_index_in_dim(src, me, 0, keepdims=False)

    def adamw_big(name, parts):
        outs = _adamw("adamw_" + name, weights[name][0], parts, mom1[name][0], mom2[name][0])
        res[name] = [o[None] for o in outs]
        return outs[0]

    sm_send, sm_recv, sm_src, sm_land, sm_token = _exchange_start("scatter_small_start", [g_conv_w_parts, small_g],
                                                                  [False] * 2)
    done = adamw_big("ada_w", g_ada_w[None] + sm_token[0:1, 0:1])
    mlp_src, mlp_landed = _exchange_wait("scatter_mlp_wait", mlp_send, mlp_recv, mlp_src, mlp_land, [False] * 2, done)
    p_w1, p_w2 = [_with_own(l, own_chunk(s), me) for l, s in zip(mlp_landed, mlp_src)]
    adamw_big("mlp_w1", p_w1)
    done = adamw_big("mlp_w2", p_w2)
    sm_src, sm_landed = _exchange_wait("scatter_small_wait", sm_send, sm_recv, sm_src, sm_land, [False] * 2, done)
    p_conv_w, p_small = [_with_own(l, own_chunk(s), me) for l, s in zip(sm_landed, sm_src)]
    ga_send, ga_recv, ga_src, ga_land, ga_token = _exchange_start("gather_small_start", [_sum_parts(p_small)], [True])
    proj_src, proj_landed = _exchange_wait("scatter_proj_wait", proj_send, proj_recv, proj_src, proj_land, [False] * 2,
                                           ga_token)
    p_w_out, p_glu = [_with_own(l, own_chunk(s), me) for l, s in zip(proj_landed, proj_src)]
    adamw_big("w_out", p_w_out)
    done = adamw_big("s5_w_glu", p_glu)
    win_src, win_landed = _exchange_wait("scatter_w_in_wait", win_send, win_recv, win_src, win_land, [False], done)
    adamw_big("w_in", _with_own(win_landed[0], own_chunk(win_src[0]), me))
    done = adamw_big("conv_w", p_conv_w)
    ga_own, ga_landed = _exchange_wait("gather_small_wait", ga_send, ga_recv, ga_src, ga_land, [True], done)
    small_all = _with_own(ga_landed[0], ga_own[0], me).reshape(1, SMALL_ROWS, D_MODEL)
    _, r_bt, r_cdiag, loss = _unpack_rows(small_all, reduced_shapes)
    loss = loss.reshape(())
    pack = lambda src: _pack_rows([src[n] for n in SMALL], SMALL_PACKED_ROWS)
    outs = _adamw("adamw_small", pack(weights), small_all, pack(mom1), pack(mom2))
    unpacked = [_unpack_rows(o, [weights[n].shape for n in SMALL]) for o in outs]
    for i, name in enumerate(SMALL):
        res[name] = [u[i] for u in unpacked]
    to_ghp = lambda t: jnp.transpose(t.reshape(2, S5_GROUP, S5_GROUPS, S5_STATE), (0, 2, 1, 3))[None]
    r_c = jnp.transpose(r_cdiag.reshape(2, S5_BLOCKS, S5_GROUP, 2, groups_per_block, S5_STATE), (3, 0, 1, 4, 2, 5)).reshape(
        2, 1, 2, S5_GROUPS, S5_GROUP, S5_STATE)
    swap = lambda t: jnp.swapaxes(t, -1, -2)
    for name, grad in (("s5_b_re", to_ghp(r_bt[:, :S5_GROUP])), ("s5_b_im", to_ghp(r_bt[:, S5_GROUP:]))):
        outs = _adamw_native("adamw_" + name, swap(weights[name]), grad, swap(mom1[name]), swap(mom2[name]))
        res[name] = [swap(grad), *[swap(o) for o in outs]]
    for name, grad in (("s5_c_re", r_c[0]), ("s5_c_im", -r_c[1])):
        res[name] = [grad, *_adamw_native("adamw_" + name, weights[name], grad, mom1[name], mom2[name])]

    return (loss, grad_x[None], *[res[n][0] for n in order], *[res[n][1] for n in order],
            *[res[n][2] for n in order], *[res[n][3] for n in order])
```

```python
import jax
import jax.numpy as jnp
from jax import lax
from jax.experimental import pallas as pl
from jax.experimental.pallas import tpu as pltpu

F32 = jnp.float32
BF16 = jnp.bfloat16
MESH = pl.DeviceIdType.MESH
ANY = pl.BlockSpec(memory_space=pl.ANY)

NDEV = 8
D_MODEL = 1024
GRID_W = 64
S5_WIDTH = 512
S5_GROUP = 16
S5_GROUPS = 32
S5_STATE = 64
NSTATE = S5_GROUPS * S5_STATE
CONV_WIDTH = 512
CONV_K = 31
IN_COLS = S5_WIDTH + 2 * CONV_WIDTH
D_FF = 4 * D_MODEL
EPS_RMS = 1e-6
EPS_LN = 1e-5
ADAM_LR = 0.001
ADAM_B1 = 0.9
ADAM_B2 = 0.999
ADAM_EPS = 1e-08
ADAM_WD = 0.01
ADAM_STEP = 10

SUBLANES = 8
LANES = 128
ROW_BLOCK = 256
SCAN_LANES = 512
SCAN_UNROLL = 32
SEGMENTS = SUBLANES
STEPS = ROW_BLOCK // SEGMENTS
S5_BLOCKS = 4
S5_BLOCK_WIDTH = S5_WIDTH // S5_BLOCKS
CONV_ROWS = 64
CONV_BWD_ROWS = 32
VMEM_LIMIT = 48 * 1024 * 1024
SMALL_ROWS = 320


def _params(sem=None):
    kw = dict(vmem_limit_bytes=VMEM_LIMIT)
    if sem is not None:
        kw["dimension_semantics"] = sem
    return pltpu.CompilerParams(**kw)


def _sds(shape, dtype=F32):
    return jax.ShapeDtypeStruct(tuple(shape), dtype)


def _fold8(x):
    return x.reshape(x.shape[0] // SUBLANES, SUBLANES, x.shape[1]).sum(axis=0)


def _sigmoid(x):
    return 1.0 / (1.0 + jnp.exp(-x))


def _silu(x):
    return x * _sigmoid(x)


def _dsilu(x):
    s = _sigmoid(x)
    return s * (1.0 + x * (1.0 - s))


_GELU_C = 0.7978845608028654


def _gelu(x):
    return 0.5 * x * (1.0 + jnp.tanh(_GELU_C * (x + 0.044715 * x * x * x)))


def _dgelu(x):
    t = jnp.tanh(_GELU_C * (x + 0.044715 * x * x * x))
    return 0.5 * (1.0 + t) + 0.5 * x * (1.0 - t * t) * _GELU_C * (1.0 + 3.0 * 0.044715 * x * x)


def _rms(x):
    rstd = lax.rsqrt(jnp.mean(x * x, axis=-1, keepdims=True) + EPS_RMS)
    return x * rstd, rstd


def _epi_residual_prenorm(acc, res, gate, gain, scale, shift):
    h = res + gate * acc
    xh, _ = _rms(h)
    a = (xh * gain) * (1.0 + scale) + shift
    return acc, h, a, a.T


def _epi_residual_loss(acc, res, gate, target, gain):
    h = res + gate * acc
    xh, rstd = _rms(h)
    err = xh * gain - target
    dy = err * (1.0 / h.shape[-1])
    dxh = dy * gain
    dh = rstd * (dxh - xh * jnp.mean(dxh * xh, axis=-1, keepdims=True))
    return acc, dh, dh * gate, _fold8(err * err), _fold8(dy * xh)


def _epi_norm_bwd(d_act, x, res, aux, gain, scale, gate):
    xh, rstd = _rms(x)
    dn = d_act * (1.0 + scale)
    dxh = dn * gain
    dx = res + rstd * (dxh - xh * jnp.mean(dxh * xh, axis=-1, keepdims=True))
    return dx, dx * gate, _fold8(d_act), _fold8(d_act * (xh * gain)), _fold8(dn * xh), _fold8(res * aux)


def _dot(a, b, mode):
    dims = {"nn": (((1,), (0,)), ((), ())), "nt": (((1,), (1,)), ((), ())), "tn": (((0,), (0,)), ((), ()))}[mode]
    return lax.dot_general(a, b, dims, preferred_element_type=F32)


def _peers(x, y, c):
    out = []
    for k in range(1, NDEV):
        px = 1 - x if k & 4 else x
        py = 1 - y if k & 2 else y
        pc = 1 - c if k & 1 else c
        out.append(((px, py, pc), 4 * px + 2 * py + pc))
    return out


def _exchange_copies(src, land, send_sems, recv_sems, gather):
    x, y, c = lax.axis_index("x"), lax.axis_index("y"), lax.axis_index("c")
    me = 4 * x + 2 * y + c
    out = []
    for a in range(len(src)):
        for k, (peer, plin) in enumerate(_peers(x, y, c)):
            chunk = src[a] if gather[a] else src[a].at[plin]
            sems = dict(send_sem=send_sems.at[a * (NDEV - 1) + k], recv_sem=recv_sems.at[a * (NDEV - 1) + k],
                        device_id=peer, device_id_type=MESH)
            out.append((pltpu.make_async_remote_copy(src_ref=chunk, dst_ref=land[a].at[me], **sems),
                        pltpu.make_async_remote_copy(src_ref=chunk, dst_ref=land[a].at[plin], **sems)))
    return out


def _exchange(name, srcs, gather):
    n = len(srcs)
    outs = [_sds(((NDEV,) + s.shape) if g else s.shape, s.dtype) for s, g in zip(srcs, gather)]

    def body(*refs):
        src, dst, token = refs[:n], refs[n:2 * n], refs[2 * n]
        send_sems, recv_sems, local_sems = refs[2 * n + 1:]
        me = 4 * lax.axis_index("x") + 2 * lax.axis_index("y") + lax.axis_index("c")
        local = [pltpu.make_async_copy(src[a] if gather[a] else src[a].at[me], dst[a].at[me], local_sems.at[a])
                 for a in range(n)]
        for copy in local:
            copy.start()
        copies = _exchange_copies(src, dst, send_sems, recv_sems, gather)
        for copy, _ in copies:
            copy.start()
        token[...] = jnp.zeros_like(token)
        for copy, landing in copies:
            copy.wait_send()
            landing.wait_recv()
        for copy in local:
            copy.wait()

    nsem = n * (NDEV - 1)
    out = pl.pallas_call(
        body, name=name, out_shape=outs + [_sds((SUBLANES, LANES))], in_specs=[ANY] * n,
        out_specs=[ANY] * n + [pl.BlockSpec(memory_space=pltpu.VMEM)],
        scratch_shapes=[pltpu.SemaphoreType.DMA((nsem,)), pltpu.SemaphoreType.DMA((nsem,)), pltpu.SemaphoreType.DMA((n,))],
    )(*srcs)
    return out[:n], out[n]


HBM = pl.BlockSpec(memory_space=pltpu.HBM)
SEM = pl.BlockSpec(memory_space=pltpu.SEMAPHORE)
EFFECT = pltpu.SideEffectType.DATAFLOW_SIDE_EFFECTING


def _exchange_start(name, srcs, gather):
    n = len(srcs)
    lands = [lax.empty(((NDEV,) + s.shape) if g else s.shape, s.dtype) for s, g in zip(srcs, gather)]

    def body(*refs):
        src, land = refs[:n], refs[n:2 * n]
        send_sems, recv_sems = refs[2 * n], refs[2 * n + 1]
        token = refs[-1]
        for copy, _ in _exchange_copies(src, land, send_sems, recv_sems, gather):
            copy.start()
        token[...] = jnp.zeros_like(token)

    hbm = lambda v: pltpu.HBM(v.shape, v.dtype)
    nsem = n * (NDEV - 1)
    out = pl.pallas_call(
        body, name=name,
        out_shape=(pltpu.SemaphoreType.DMA((nsem,)), pltpu.SemaphoreType.DMA((nsem,)), *[hbm(v) for v in srcs],
                   *[hbm(v) for v in lands], _sds((SUBLANES, LANES))),
        in_specs=[HBM] * (2 * n), out_specs=(SEM, SEM, *([HBM] * (2 * n)), pl.BlockSpec(memory_space=pltpu.VMEM)),
        input_output_aliases={i: 2 + i for i in range(2 * n)},
        compiler_params=pltpu.CompilerParams(has_side_effects=EFFECT),
    )(*[pltpu.with_memory_space_constraint(v, pltpu.HBM) for v in list(srcs) + lands])
    return out[0], out[1], out[2:2 + n], out[2 + n:2 + 2 * n], out[-1]


def _exchange_wait(name, send_sems, recv_sems, srcs, lands, gather, after):
    n = len(srcs)

    def body(*refs):
        src, land = refs[:n], refs[n:2 * n]
        send_ref, recv_ref = refs[2 * n], refs[2 * n + 1]
        for copy, landing in _exchange_copies(src, land, send_ref, recv_ref, gather):
            copy.wait_send()
            landing.wait_recv()

    hbm = lambda v: pltpu.HBM(v.shape, v.dtype)
    out = pl.pallas_call(
        body, name=name, out_shape=[hbm(v) for v in list(srcs) + list(lands)],
        in_specs=[HBM] * (2 * n) + [SEM, SEM, ANY], out_specs=[HBM] * (2 * n),
        input_output_aliases={i: i for i in range(2 * n)},
        compiler_params=pltpu.CompilerParams(has_side_effects=EFFECT),
    )(*srcs, *lands, send_sems, recv_sems, after)
    return out[:n], out[n:]


def _with_own(landed, own, me):
    return lax.dynamic_update_slice(landed, own[None], (me,) + (0,) * own.ndim)


def _matmul(name, a, b, mode, mnk, tiles, outs, a_spec=None, b_spec=None, a_fn=None, a_extra=(),
            epi=None, epi_extra=(), out_specs=None, b_slabs=1):
    m_, n_, k_ = mnk
    tm, tn, tk = tiles
    nk = k_ // tk
    if a_spec is None:
        a_spec = (pl.BlockSpec((tk, tm), lambda i, j, k: (k, i)) if mode == "tn"
                  else pl.BlockSpec((tm, tk), lambda i, j, k: (i, k)))
    if b_spec is None:
        b_spec = (pl.BlockSpec((tn, tk), lambda i, j, k: (j, k)) if mode == "nt"
                  else pl.BlockSpec((tk, tn), lambda i, j, k: (k, j)))
    if out_specs is None:
        out_specs = [pl.BlockSpec((tm, tn), lambda i, j, k: (i, j)) for _ in outs]
    na, ne, no = len(a_extra), len(epi_extra), len(outs)

    def body(*refs):
        a_ref, b_ref = refs[0], refs[1]
        ax = refs[2:2 + na]
        ex = refs[2 + na:2 + na + ne]
        o = refs[2 + na + ne:2 + na + ne + no]

        def finish(res):
            res = epi(res, *[r[...] for r in ex]) if epi is not None else (res,)
            for ref, val in zip(o, res):
                ref[...] = val.astype(ref.dtype)

        at = a_ref[...]
        if a_fn is not None:
            at = a_fn(at, *[r[...] for r in ax])
        at = at.astype(BF16)
        if b_slabs == 1:
            part = _dot(at, b_ref[...].astype(BF16), mode)
        else:
            ks = tk // b_slabs
            part = _dot(at[:, 0:ks], b_ref[0].astype(BF16), mode)
            for s in range(1, b_slabs):
                part = part + _dot(at[:, s * ks:(s + 1) * ks], b_ref[s].astype(BF16), mode)
        if nk == 1:
            finish(part)
            return
        acc = refs[-1]
        k = pl.program_id(2)

        @pl.when(k == 0)
        def _():
            acc[...] = part

        @pl.when(k > 0)
        def _():
            acc[...] += part

        @pl.when(k == nk - 1)
        def _():
            finish(acc[...])

    return pl.pallas_call(
        body, name=name, grid=(m_ // tm, n_ // tn, nk),
        in_specs=[a_spec, b_spec] + [s for _, s in a_extra] + [s for _, s in epi_extra],
        out_specs=out_specs, out_shape=[_sds(s, d) for s, d in outs],
        scratch_shapes=[pltpu.VMEM((tm, tn), F32)] if nk > 1 else [],
        compiler_params=_params(("parallel", "parallel", "arbitrary")),
    )(a, b, *[x for x, _ in a_extra], *[x for x, _ in epi_extra])


def _prenorm(name, x, ctx, gain, shsc):
    n_lat = x.shape[0] // ROW_BLOCK
    n_ctx = 0 if ctx is None else ctx.shape[0] // ROW_BLOCK
    d = x.shape[1]

    def norm(src, g_ref, m_ref, o_ref):
        xv = src[...]
        xh = xv * lax.rsqrt(jnp.mean(xv * xv, axis=-1, keepdims=True) + EPS_RMS)
        o_ref[...] = ((xh * g_ref[...]) * (1.0 + m_ref[1:2, :]) + m_ref[0:1, :]).astype(o_ref.dtype)

    def body(*refs):
        if ctx is None:
            x_ref, g_ref, m_ref, o_ref = refs
            norm(x_ref, g_ref, m_ref, o_ref)
        else:
            x_ref, c_ref, g_ref, m_ref, o_ref = refs
            i = pl.program_id(0)

            @pl.when(i < n_lat)
            def _():
                norm(x_ref, g_ref, m_ref, o_ref)

            @pl.when(i >= n_lat)
            def _():
                norm(c_ref, g_ref, m_ref, o_ref)

    in_specs = [pl.BlockSpec((ROW_BLOCK, d), lambda i: (jnp.minimum(i, n_lat - 1), 0))]
    args = [x]
    if ctx is not None:
        in_specs.append(pl.BlockSpec((ROW_BLOCK, d), lambda i: (jnp.maximum(i - n_lat, 0), 0)))
        args.append(ctx)
    in_specs += [pl.BlockSpec((1, d), lambda i: (0, 0)),
                 pl.BlockSpec((None, 2, d), lambda i: (jnp.minimum(i // n_lat, 1), 0, 0))]
    args += [gain, shsc]
    return pl.pallas_call(
        body, name=name, grid=(n_lat + n_ctx,), in_specs=in_specs,
        out_specs=pl.BlockSpec((ROW_BLOCK, d), lambda i: (i, 0)),
        out_shape=_sds(((n_lat + n_ctx) * ROW_BLOCK, d), BF16),
        compiler_params=_params(("parallel",)),
    )(*args)


def _norm_bwd(name, x, d_act, d_act_row0, gain, scale, res=None, aux=None):
    rows, d = x.shape
    nb = rows // ROW_BLOCK
    has_res = res is not None

    def body(*refs):
        if has_res:
            x_ref, da_ref, g_ref, sc_ref, r_ref, aux_ref, dx_ref, sums = refs
        else:
            x_ref, da_ref, g_ref, sc_ref, sums = refs
        i = pl.program_id(0)

        @pl.when(i == 0)
        def _():
            sums[...] = jnp.zeros_like(sums)

        xv, da = x_ref[...], da_ref[...]
        rstd = lax.rsqrt(jnp.mean(xv * xv, axis=-1, keepdims=True) + EPS_RMS)
        xh = xv * rstd
        g = g_ref[...]
        dn = da * (1.0 + sc_ref[...])
        sums[0] += _fold8(da)
        sums[1] += _fold8(da * (xh * g))
        sums[2] += _fold8(dn * xh)
        if has_res:
            dxh = dn * g
            dx = rstd * (dxh - xh * jnp.mean(dxh * xh, axis=-1, keepdims=True))
            rv = r_ref[...]
            dx_ref[...] = rv + dx
            sums[3] += _fold8(rv * aux_ref[...])

    row = lambda i: (i, 0)
    vec = pl.BlockSpec((1, d), lambda i: (0, 0))
    in_specs = [pl.BlockSpec((ROW_BLOCK, d), row), pl.BlockSpec((ROW_BLOCK, d), lambda i: (i + d_act_row0, 0)), vec, vec]
    args = [x, d_act, gain, scale]
    out_shape = [_sds((4, SUBLANES, d))]
    out_specs = [pl.BlockSpec((4, SUBLANES, d), lambda i: (0, 0, 0))]
    if has_res:
        in_specs += [pl.BlockSpec((ROW_BLOCK, d), row), pl.BlockSpec((ROW_BLOCK, d), row)]
        args += [res, aux]
        out_shape = [_sds((rows, d))] + out_shape
        out_specs = [pl.BlockSpec((ROW_BLOCK, d), row)] + out_specs
    return pl.pallas_call(
        body, name=name, grid=(nb,), in_specs=in_specs, out_specs=out_specs, out_shape=out_shape,
        compiler_params=_params(("arbitrary",)),
    )(*args)


def _ada_fwd(cond16, ada_w_loc, ada_b_loc):
    cols = ada_w_loc.shape[1]

    def body(c_ref, w_ref, b_ref, o_ref):
        s = _silu(c_ref[...]).astype(BF16)
        o_ref[...] = _dot(s, w_ref[...].astype(BF16), "nn") + b_ref[...]

    return pl.pallas_call(body, name="ada_fwd", out_shape=_sds((16, cols)), compiler_params=_params())(
        cond16, ada_w_loc, ada_b_loc)


def _ada_bwd(cond16, dmod16, ada_w_loc, c_ctx_row):
    k_, cols = ada_w_loc.shape

    def body(c_ref, dm_ref, w_ref, cc_ref, gw_ref, gc_ref):
        s = _silu(c_ref[...]).astype(BF16)
        dm = dm_ref[...]
        gw_ref[...] = _dot(s, dm.astype(BF16), "tn")
        dmc = jnp.sum(dm[8:16, :], axis=0, keepdims=True)
        dmc8 = jnp.broadcast_to(dmc, (SUBLANES, cols)).astype(BF16)
        ds = _dot(dmc8, w_ref[...].astype(BF16), "nt")
        row = lax.broadcasted_iota(jnp.int32, ds.shape, 0)
        gc_ref[...] = jnp.where(row == 0, ds * _dsilu(cc_ref[...]), 0.0)

    return pl.pallas_call(body, name="ada_bwd", out_shape=[_sds((k_, cols)), _sds((SUBLANES, k_))],
                          compiler_params=_params())(cond16, dmod16, ada_w_loc, c_ctx_row)


def _cmul(a, b):
    return a[0] * b[0] - a[1] * b[1], a[0] * b[1] + a[1] * b[0]


def _disc(lam_re, lam_im, ldt):
    dt = jnp.exp(ldt)
    mag = jnp.exp(lam_re * dt)
    th = lam_im * dt
    a_re, a_im = mag * jnp.cos(th), mag * jnp.sin(th)
    den = lam_re * lam_re + lam_im * lam_im
    n_re = a_re - 1.0
    f_re = (n_re * lam_re + a_im * lam_im) / den
    f_im = (a_im * lam_re - n_re * lam_im) / den
    return dt, mag, th, a_re, a_im, den, n_re, f_re, f_im


def _block_diag_mask(shape):
    row = lax.broadcasted_iota(jnp.int32, shape, 0)
    col = lax.broadcasted_iota(jnp.int32, shape, 1)
    return lax.shift_right_logical(row, 4) == lax.shift_right_logical(col, 6)


TAB_A = 0
TAB_BIG = 1
TAB_SEG = 4
TAB_PW = 5
TAB_ROWS = TAB_PW + STEPS


def _s5_discretise(name, ascending, lam_re, lam_im, ldt, bt_re, bt_im, ct_re, ct_im):
    def write_tables(ref, pw, big, asc, sign):
        row = lax.broadcasted_iota(jnp.int32, (SUBLANES, NSTATE), 0)
        full = lambda v: jnp.broadcast_to(v, (SUBLANES, NSTATE))

        def put(t, p):
            ref[0, t] = full(p[0])
            ref[1, t] = full(sign * p[1])

        put(TAB_A, pw[0])
        for t in range(3):
            put(TAB_BIG + t, big[t])
        seg = [big[0]]
        for _ in range(SEGMENTS - 1):
            seg.append(_cmul(seg[-1], big[0]))
        seg_re = jnp.zeros((SUBLANES, NSTATE), F32)
        seg_im = jnp.zeros((SUBLANES, NSTATE), F32)
        for r in range(SEGMENTS):
            p = seg[r] if asc else seg[SEGMENTS - 1 - r]
            seg_re = jnp.where(row == r, p[0], seg_re)
            seg_im = jnp.where(row == r, sign * p[1], seg_im)
        ref[0, TAB_SEG] = seg_re
        ref[1, TAB_SEG] = seg_im
        for k in range(STEPS):
            put(TAB_PW + k, pw[k])

    def body(lr_ref, li_ref, ldt_ref, br_ref, bi_ref, cr_ref, ci_ref, bb_ref, tab_ref, adj_ref, bm_ref, cm_ref):
        _, _, _, a_re, a_im, _, _, f_re, f_im = _disc(lr_ref[...], li_ref[...], ldt_ref[...])
        bre, bim = br_ref[...], bi_ref[...]
        bb_re = f_re * bre - f_im * bim
        bb_im = f_re * bim + f_im * bre
        bb_ref[0:S5_GROUP, :] = bb_re
        bb_ref[S5_GROUP:2 * S5_GROUP, :] = bb_im
        pw = [(a_re, a_im)]
        for _ in range(STEPS - 1):
            pw.append(_cmul(pw[-1], (a_re, a_im)))
        big = [pw[STEPS - 1]]
        for _ in range(2):
            big.append(_cmul(big[-1], big[-1]))
        write_tables(tab_ref, pw, big, ascending, 1.0)
        write_tables(adj_ref, pw, big, not ascending, -1.0)
        half = NSTATE // S5_BLOCKS
        mask = _block_diag_mask((S5_BLOCK_WIDTH, half))
        tile = lambda v: jnp.broadcast_to(v[None], (S5_BLOCK_WIDTH // S5_GROUP, S5_GROUP, half)).reshape(S5_BLOCK_WIDTH, half)
        for c in range(S5_BLOCKS):
            cols = slice(c * half, (c + 1) * half)
            rows = slice(c * S5_BLOCK_WIDTH, (c + 1) * S5_BLOCK_WIDTH)
            bm_ref[c, :, 0:half] = jnp.where(mask, tile(bb_re[:, cols]), 0.0).astype(BF16)
            bm_ref[c, :, half:2 * half] = jnp.where(mask, tile(bb_im[:, cols]), 0.0).astype(BF16)
            cm_ref[c, :, 0:half] = jnp.where(mask, cr_ref[rows, :], 0.0).astype(BF16)
            cm_ref[c, :, half:2 * half] = jnp.where(mask, -ci_ref[rows, :], 0.0).astype(BF16)

    blocked = _sds((S5_BLOCKS, S5_BLOCK_WIDTH, 2 * NSTATE // S5_BLOCKS), BF16)
    return pl.pallas_call(
        body, name=name,
        out_shape=[_sds((2 * S5_GROUP, NSTATE)), _sds((2, TAB_ROWS, SUBLANES, NSTATE)),
                   _sds((2, TAB_ROWS, SUBLANES, NSTATE)), blocked, blocked],
        compiler_params=_params(),
    )(lam_re, lam_im, ldt, bt_re, bt_im, ct_re, ct_im)


def _s5_discretise_bwd(name, lam_re, lam_im, ldt, bt_re, bt_im, d_abar8, d_bbar):
    def body(lr_ref, li_ref, ldt_ref, br_ref, bi_ref, da_ref, db_ref, dl_ref, dbt_ref):
        lam_re, lam_im = lr_ref[...], li_ref[...]
        dt, mag, _, a_re, a_im, den, n_re, f_re, f_im = _disc(lam_re, lam_im, ldt_ref[...])
        bre, bim = br_ref[...], bi_ref[...]
        dbr, dbi = db_ref[0:S5_GROUP, :], db_ref[S5_GROUP:2 * S5_GROUP, :]
        dbt_ref[0:S5_GROUP, :] = f_re * dbr + f_im * dbi
        dbt_ref[S5_GROUP:2 * S5_GROUP, :] = f_re * dbi - f_im * dbr
        df_re = jnp.sum(bre * dbr + bim * dbi, axis=0, keepdims=True)
        df_im = jnp.sum(bre * dbi - bim * dbr, axis=0, keepdims=True)
        da = da_ref[...]
        da_re = jnp.sum(da[:, 0:NSTATE], axis=0, keepdims=True)
        da_im = jnp.sum(da[:, NSTATE:2 * NSTATE], axis=0, keepdims=True)
        da_re = da_re + (df_re * lam_re - df_im * lam_im) / den
        da_im = da_im + (df_re * lam_im + df_im * lam_re) / den
        ff = (f_re * df_re + f_im * df_im) * 2.0 / den
        d_lr = (df_re * n_re + df_im * a_im) / den - ff * lam_re
        d_li = (df_re * a_im - df_im * n_re) / den - ff * lam_im
        d_mag = (da_re * a_re + da_im * a_im) / mag
        d_th = da_im * a_re - da_re * a_im
        d_lr = d_lr + d_mag * mag * dt
        d_li = d_li + d_th * dt
        d_ldt = (d_mag * mag * lam_re + d_th * lam_im) * dt
        row = lax.broadcasted_iota(jnp.int32, (SUBLANES, NSTATE), 0)
        dl_ref[...] = jnp.where(row == 0, d_lr, jnp.where(row == 1, d_li, jnp.where(row == 2, d_ldt, 0.0)))

    return pl.pallas_call(
        body, name=name, out_shape=[_sds((SUBLANES, NSTATE)), _sds((2 * S5_GROUP, NSTATE))],
        compiler_params=_params(),
    )(lam_re, lam_im, ldt, bt_re, bt_im, d_abar8, d_bbar)


def _segment_permutation():
    rho = jnp.arange(ROW_BLOCK)
    src = STEPS * (rho % SEGMENTS) + rho // SEGMENTS
    return (src[:, None] == jnp.arange(ROW_BLOCK)[None, :]).astype(BF16)


def _permute_rows(perm_ref, v):
    return _dot(perm_ref[...], v, "nn").astype(BF16)


def _unpermute_rows(perm_t_ref, v):
    hi = v.astype(BF16)
    lo = (v - hi.astype(F32)).astype(BF16)
    return _dot(perm_t_ref[...], hi, "nn") + _dot(perm_t_ref[...], lo, "nn")


def _unrolled_loop(step, init):
    def trip(o, state):
        for u in range(SCAN_UNROLL):
            state = step(o * SCAN_UNROLL + u, state)
        return state

    if SCAN_UNROLL == STEPS:
        return trip(0, init)
    return lax.fori_loop(0, STEPS // SCAN_UNROLL, trip, init)


def _scan_chunk(x_ref, out_ref, tab_ref, carry_re, carry_im, ascending, pair_ref=None, acc_ref=None, lane_chunks=None):
    w = SCAN_LANES
    half = NSTATE // S5_BLOCKS
    row = lax.broadcasted_iota(jnp.int32, (SUBLANES, w), 0)
    last = (SEGMENTS - 1) if ascending else 0

    def from_previous_segment(v, k, fill):
        if ascending:
            return jnp.where(row >= k, pltpu.roll(v, k, 0), fill)
        return jnp.where(row < SEGMENTS - k, pltpu.roll(v, SEGMENTS - k, 0), fill)

    def tile_rows(k):
        return pl.ds(pl.multiple_of((k if ascending else STEPS - 1 - k) * SUBLANES, SUBLANES), SUBLANES)

    for j in (range(NSTATE // w) if lane_chunks is None else lane_chunks):
        n_l = pl.ds(j * w, w)
        lane0 = (j * w // half) * 2 * half + (j * w) % half
        re_l, im_l = pl.ds(lane0, w), pl.ds(lane0 + half, w)
        tab = lambda t, n_l=n_l: (tab_ref[0, t, :, n_l], tab_ref[1, t, :, n_l])
        a_re, a_im = tab(TAB_A)

        def local_step(k, h):
            rs = tile_rows(k)
            h_re = a_re * h[0] - a_im * h[1] + x_ref[rs, re_l]
            h_im = a_re * h[1] + a_im * h[0] + x_ref[rs, im_l]
            out_ref[rs, re_l] = h_re
            out_ref[rs, im_l] = h_im
            return h_re, h_im

        zero = jnp.zeros((SUBLANES, w), F32)
        end_re, end_im = _unrolled_loop(local_step, (zero, zero))
        for t, k in ((TAB_BIG, 1), (TAB_BIG + 1, 2), (TAB_BIG + 2, 4)):
            p_re, p_im = tab(t)
            s_re, s_im = from_previous_segment(end_re, k, 0.0), from_previous_segment(end_im, k, 0.0)
            end_re, end_im = end_re + (p_re * s_re - p_im * s_im), end_im + (p_re * s_im + p_im * s_re)
        c0_re, c0_im = carry_re[:, n_l], carry_im[:, n_l]
        p_re, p_im = tab(TAB_SEG)
        end_re = end_re + (p_re * c0_re - p_im * c0_im)
        end_im = end_im + (p_re * c0_im + p_im * c0_re)
        carry_re[:, n_l] = jnp.broadcast_to(end_re[last:last + 1, :], end_re.shape)
        carry_im[:, n_l] = jnp.broadcast_to(end_im[last:last + 1, :], end_im.shape)
        in_re = from_previous_segment(end_re, 1, c0_re)
        in_im = from_previous_segment(end_im, 1, c0_im)

        def carry_step(k, st):
            rs = tile_rows(k)
            p_re, p_im = tab_ref[0, TAB_PW + k, :, n_l], tab_ref[1, TAB_PW + k, :, n_l]
            o_re = out_ref[rs, re_l] + (p_re * in_re - p_im * in_im)
            o_im = out_ref[rs, im_l] + (p_re * in_im + p_im * in_re)
            out_ref[rs, re_l] = o_re
            out_ref[rs, im_l] = o_im
            if pair_ref is None:
                return st
            s_re, s_im = pair_ref[rs, re_l], pair_ref[rs, im_l]
            return (o_re, o_im, st[2] + (st[0] * s_re + st[1] * s_im), st[3] + (st[1] * s_re - st[0] * s_im))

        if pair_ref is None:
            _unrolled_loop(carry_step, 0)
        else:
            fin = _unrolled_loop(carry_step, (in_re, in_im, zero, zero))
            acc_ref[:, n_l] += fin[2]
            acc_ref[:, pl.ds(NSTATE + j * w, w)] += fin[3]


def _scan_block_index(i, n_lat, ctx_first_then_ascending):
    if ctx_first_then_ascending:
        return jnp.where(i == 0, n_lat, i - 1)
    return jnp.where(i == 0, n_lat, n_lat - i)


def _full_spec(shape):
    return pl.BlockSpec(shape, lambda i: (0,) * len(shape))


_S5_BLOCKED = (S5_BLOCKS, S5_BLOCK_WIDTH, 2 * NSTATE // S5_BLOCKS)
_S5_TABLES = (2, TAB_ROWS, SUBLANES, NSTATE)
_S5_DIAG = (S5_BLOCKS, S5_GROUP, 2 * NSTATE // S5_BLOCKS)


def _s5_scan_fwd(name, ascending, z_all, bmat, cmat, tab, perm, perm_t):
    rows = z_all.shape[0]
    nb = rows // ROW_BLOCK
    n_lat = nb - 1
    bw, sw = S5_BLOCK_WIDTH, 2 * NSTATE // S5_BLOCKS

    def body(u_ref, bm_ref, cm_ref, tab_ref, p_ref, pt_ref, s_ref, y_ref, bu, yp, carry_re, carry_im):
        @pl.when(pl.program_id(0) == 0)
        def _():
            carry_re[...] = jnp.zeros_like(carry_re)
            carry_im[...] = jnp.zeros_like(carry_im)

        up = _permute_rows(p_ref, u_ref[...].astype(BF16))
        for c in range(S5_BLOCKS):
            bu[:, c * sw:(c + 1) * sw] = _dot(up[:, c * bw:(c + 1) * bw], bm_ref[c], "nn")
        _scan_chunk(bu, s_ref, tab_ref, carry_re, carry_im, ascending)
        for c in range(S5_BLOCKS):
            yp[:, c * bw:(c + 1) * bw] = _dot(s_ref[:, c * sw:(c + 1) * sw].astype(BF16), cm_ref[c], "nt")
        y_ref[...] = _unpermute_rows(pt_ref, yp[...])

    blk = lambda i: (_scan_block_index(i, n_lat, ascending), 0)
    return pl.pallas_call(
        body, name=name, grid=(nb,),
        in_specs=[pl.BlockSpec((ROW_BLOCK, S5_WIDTH), blk), _full_spec(_S5_BLOCKED), _full_spec(_S5_BLOCKED),
                  _full_spec(_S5_TABLES), _full_spec((ROW_BLOCK, ROW_BLOCK)), _full_spec((ROW_BLOCK, ROW_BLOCK))],
        out_specs=[pl.BlockSpec((ROW_BLOCK, 2 * NSTATE), blk), pl.BlockSpec((ROW_BLOCK, S5_WIDTH), blk)],
        out_shape=[_sds((rows, 2 * NSTATE)), _sds((rows, S5_WIDTH))],
        scratch_shapes=[pltpu.VMEM((ROW_BLOCK, 2 * NSTATE), F32), pltpu.VMEM((ROW_BLOCK, S5_WIDTH), F32),
                        pltpu.VMEM((SUBLANES, NSTATE), F32), pltpu.VMEM((SUBLANES, NSTATE), F32)],
        compiler_params=_params(("arbitrary",)),
    )(z_all, bmat, cmat, tab, perm, perm_t)


def _s5_scan_bwd(name, ascending, dy, z_all, states, bmat, cmat, adj, perm, perm_t):
    rows = states.shape[0]
    nb = rows // ROW_BLOCK
    n_lat = nb - 1
    bw, sw = S5_BLOCK_WIDTH, 2 * NSTATE // S5_BLOCKS

    def block_index(i):
        if ascending:
            return jnp.where(i == nb - 1, n_lat, n_lat - 1 - i)
        return jnp.where(i == nb - 1, n_lat, i)

    def body(dy_ref, u_ref, s_ref, bm_ref, cm_ref, adj_ref, p_ref, pt_ref, du_ref, db_ref, dc_ref, da_ref,
             g, dup, db_acc, dc_acc, carry_re, carry_im):
        i = pl.program_id(0)

        @pl.when(i == 0)
        def _():
            carry_re[...] = jnp.zeros_like(carry_re)
            carry_im[...] = jnp.zeros_like(carry_im)
            da_ref[...] = jnp.zeros_like(da_ref)
            db_acc[...] = jnp.zeros_like(db_acc)
            dc_acc[...] = jnp.zeros_like(dc_acc)

        has_dy = (i < nb - 1).astype(F32)
        dyp = _permute_rows(p_ref, (dy_ref[...] * has_dy).astype(BF16))
        up = _permute_rows(p_ref, u_ref[...].astype(BF16))
        for c in range(S5_BLOCKS):
            g[:, c * sw:(c + 1) * sw] = _dot(dyp[:, c * bw:(c + 1) * bw], cm_ref[c], "nn")
            dc_acc[c] += _dot(dyp[:, c * bw:(c + 1) * bw], s_ref[:, c * sw:(c + 1) * sw].astype(BF16), "tn")
            _scan_chunk(g, g, adj_ref, carry_re, carry_im, not ascending, pair_ref=s_ref, acc_ref=da_ref, lane_chunks=[c])
            gc = g[:, c * sw:(c + 1) * sw].astype(BF16)
            dup[:, c * bw:(c + 1) * bw] = _dot(gc, bm_ref[c], "nt")
            db_acc[c] += _dot(up[:, c * bw:(c + 1) * bw], gc, "tn")
        du_ref[...] = _unpermute_rows(pt_ref, dup[...])

        @pl.when(i == nb - 1)
        def _():
            mask = _block_diag_mask((bw, sw // 2))
            for acc, out in ((db_acc, db_ref), (dc_acc, dc_ref)):
                for c in range(S5_BLOCKS):
                    for part in range(2):
                        cols = slice(part * (sw // 2), (part + 1) * (sw // 2))
                        kept = jnp.where(mask, acc[c, :, cols], 0.0)
                        out[c, :, cols] = kept.reshape(bw // S5_GROUP, S5_GROUP, sw // 2).sum(axis=0)

    blk = lambda i: (block_index(i), 0)
    return pl.pallas_call(
        body, name=name, grid=(nb,),
        in_specs=[pl.BlockSpec((ROW_BLOCK, S5_WIDTH), lambda i: (jnp.minimum(block_index(i), n_lat - 1), 0)),
                  pl.BlockSpec((ROW_BLOCK, S5_WIDTH), blk), pl.BlockSpec((ROW_BLOCK, 2 * NSTATE), blk),
                  _full_spec(_S5_BLOCKED), _full_spec(_S5_BLOCKED), _full_spec(_S5_TABLES),
                  _full_spec((ROW_BLOCK, ROW_BLOCK)), _full_spec((ROW_BLOCK, ROW_BLOCK))],
        out_specs=[pl.BlockSpec((ROW_BLOCK, S5_WIDTH), blk), _full_spec(_S5_DIAG), _full_spec(_S5_DIAG),
                   _full_spec((SUBLANES, 2 * NSTATE))],
        out_shape=[_sds((rows, S5_WIDTH)), _sds(_S5_DIAG), _sds(_S5_DIAG), _sds((SUBLANES, 2 * NSTATE))],
        scratch_shapes=[pltpu.VMEM((ROW_BLOCK, 2 * NSTATE), F32), pltpu.VMEM((ROW_BLOCK, S5_WIDTH), F32),
                        pltpu.VMEM(_S5_BLOCKED, F32), pltpu.VMEM(_S5_BLOCKED, F32),
                        pltpu.VMEM((SUBLANES, NSTATE), F32), pltpu.VMEM((SUBLANES, NSTATE), F32)],
        compiler_params=_params(("arbitrary",)),
    )(dy, z_all, states, bmat, cmat, adj, perm, perm_t)


def _glu_fwd(z_all, y0, y1, d_skip, w_glu, n_rows):
    def body(u_ref, y0_ref, y1_ref, d_ref, w_ref, o_ref):
        y = d_ref[...] * u_ref[...] + y0_ref[...] + y1_ref[...]
        g = _gelu(y)
        t = _dot(g.astype(BF16), w_ref[...], "nn")
        o_ref[...] = (g * _sigmoid(t)).astype(o_ref.dtype)

    row = pl.BlockSpec((ROW_BLOCK, S5_WIDTH), lambda i: (i, 0))
    return pl.pallas_call(
        body, name="glu_fwd", grid=(n_rows // ROW_BLOCK,),
        in_specs=[row, row, row, pl.BlockSpec((1, S5_WIDTH), lambda i: (0, 0)),
                  pl.BlockSpec((S5_WIDTH, S5_WIDTH), lambda i: (0, 0))],
        out_specs=row, out_shape=_sds((n_rows, S5_WIDTH + CONV_WIDTH), BF16), compiler_params=_params(("parallel",)),
    )(z_all, y0, y1, d_skip, w_glu)


def _glu_bwd(d_ycat, z_all, y0, y1, d_skip, w_glu, n_rows):
    def body(do_ref, u_ref, y0_ref, y1_ref, d_ref, w_ref, dy_ref, dw_ref, dd_ref):
        @pl.when(pl.program_id(0) == 0)
        def _():
            dw_ref[...] = jnp.zeros_like(dw_ref)
            dd_ref[...] = jnp.zeros_like(dd_ref)

        u = u_ref[...]
        y = d_ref[...] * u + y0_ref[...] + y1_ref[...]
        g = _gelu(y)
        gb = g.astype(BF16)
        w = w_ref[...]
        sg = _sigmoid(_dot(gb, w, "nn"))
        do = do_ref[...]
        dt = do * g * sg * (1.0 - sg)
        dtb = dt.astype(BF16)
        dg = do * sg + _dot(dtb, w, "nt")
        dy = dg * _dgelu(y)
        dy_ref[...] = dy
        dw_ref[...] += _dot(gb, dtb, "tn")
        dd_ref[...] += _fold8(dy * u)

    row = pl.BlockSpec((ROW_BLOCK, S5_WIDTH), lambda i: (i, 0))
    sq = pl.BlockSpec((S5_WIDTH, S5_WIDTH), lambda i: (0, 0))
    return pl.pallas_call(
        body, name="glu_bwd", grid=(n_rows // ROW_BLOCK,),
        in_specs=[row, row, row, row, pl.BlockSpec((1, S5_WIDTH), lambda i: (0, 0)), sq],
        out_specs=[row, sq, pl.BlockSpec((SUBLANES, S5_WIDTH), lambda i: (0, 0))],
        out_shape=[_sds((n_rows, S5_WIDTH)), _sds((S5_WIDTH, S5_WIDTH)), _sds((SUBLANES, S5_WIDTH))],
        compiler_params=_params(("arbitrary",)),
    )(d_ycat, z_all, y0, y1, d_skip, w_glu)


CONV_HALF = CONV_K // 2


def _conv_block(n_rows):
    blk = min(1024, n_rows)
    assert blk >= CONV_HALF * GRID_W and n_rows % blk == 0
    return blk


def _conv_gate(z_all, n_rows):
    blk = _conv_block(n_rows)
    nb = n_rows // blk

    def body(v_ref, g_ref, o_ref):
        i = pl.program_id(0)
        inside = jnp.logical_and(i >= 1, i <= nb)

        @pl.when(inside)
        def _():
            o_ref[...] = v_ref[...] * _sigmoid(g_ref[...])

        @pl.when(jnp.logical_not(inside))
        def _():
            o_ref[...] = jnp.zeros_like(o_ref)

    src = lambda col: pl.BlockSpec((blk, CONV_WIDTH), lambda i: (jnp.clip(i - 1, 0, nb - 1), col))
    return pl.pallas_call(
        body, name="conv_gate", grid=(nb + 2,), in_specs=[src(1), src(2)],
        out_specs=pl.BlockSpec((blk, CONV_WIDTH), lambda i: (i, 0)),
        out_shape=_sds(((nb + 2) * blk, CONV_WIDTH)), compiler_params=_params(("parallel",)),
    )(z_all, z_all)


def _stream_padded(pad_ref, buf, sems, blk, n_blocks):
    i = pl.program_id(0)

    def copy(b):
        rows = pl.ds(pl.multiple_of(b * blk, blk), blk)
        return pltpu.make_async_copy(pad_ref.at[rows, :], buf.at[rows, :], sems.at[b])

    @pl.when(i == 0)
    def _():
        for b in range(n_blocks):
            copy(b).start()
        copy(0).wait()
        copy(1).wait()

    copy(i + 2).wait()
    return pl.multiple_of(i * blk, blk)


def _conv_fwd(hh_pad, w, b, ln_g, ln_b, ycat, n_rows):
    blk = _conv_block(n_rows)
    nblk = n_rows // blk + 2

    def body(hh_ref, w_ref, b_ref, g_ref, lb_ref, ycat_ref, hc_ref, y_ref, win, sems):
        base = _stream_padded(hh_ref, win, sems, blk, nblk)

        def tile(t, _):
            r0 = pl.multiple_of(t * CONV_ROWS, CONV_ROWS)
            acc = jnp.zeros((CONV_ROWS, CONV_WIDTH), F32)
            for k in range(CONV_K):
                acc = acc + w_ref[k:k + 1, :] * win[pl.ds(base + r0 + blk + (k - CONV_HALF) * GRID_W, CONV_ROWS), :]
            hc = acc + b_ref[...]
            hc_ref[pl.ds(r0, CONV_ROWS), :] = hc
            mu = jnp.mean(hc, axis=-1, keepdims=True)
            xc = hc - mu
            ln = xc * lax.rsqrt(jnp.mean(xc * xc, axis=-1, keepdims=True) + EPS_LN) * g_ref[...] + lb_ref[...]
            y_ref[pl.ds(r0, CONV_ROWS), :] = _silu(ln).astype(y_ref.dtype)
            return 0

        lax.fori_loop(0, blk // CONV_ROWS, tile, 0)

    vec = pl.BlockSpec((1, CONV_WIDTH), lambda i: (0, 0))
    row = pl.BlockSpec((blk, CONV_WIDTH), lambda i: (i, 0))
    return pl.pallas_call(
        body, name="conv_fwd", grid=(n_rows // blk,),
        in_specs=[ANY, pl.BlockSpec((CONV_K, CONV_WIDTH), lambda i: (0, 0)), vec, vec, vec, ANY],
        out_specs=[row, pl.BlockSpec((blk, CONV_WIDTH), lambda i: (i, 1))],
        out_shape=[_sds((n_rows, CONV_WIDTH)), _sds(ycat.shape, ycat.dtype)], input_output_aliases={5: 1},
        scratch_shapes=[pltpu.VMEM((nblk * blk, CONV_WIDTH), F32), pltpu.SemaphoreType.DMA((nblk,))],
        compiler_params=_params(("arbitrary",)),
    )(hh_pad, w, b, ln_g, ln_b, ycat)


def _conv_bwd_norm(d_ycat, hc, ln_g, ln_b, n_rows):
    blk = _conv_block(n_rows)
    nb = n_rows // blk

    def body(dy_ref, hc_ref, g_ref, lb_ref, o_ref, sums):
        i = pl.program_id(0)

        @pl.when(i == 0)
        def _():
            sums[...] = jnp.zeros_like(sums)

        inside = jnp.logical_and(i >= 1, i <= nb)

        @pl.when(inside)
        def _():
            hcv = hc_ref[...]
            mu = jnp.mean(hcv, axis=-1, keepdims=True)
            xc = hcv - mu
            rstd = lax.rsqrt(jnp.mean(xc * xc, axis=-1, keepdims=True) + EPS_LN)
            xh = xc * rstd
            g = g_ref[...]
            dln = dy_ref[...] * _dsilu(xh * g + lb_ref[...])
            dxh = dln * g
            dhc = rstd * (dxh - jnp.mean(dxh, axis=-1, keepdims=True) - xh * jnp.mean(dxh * xh, axis=-1, keepdims=True))
            o_ref[...] = dhc
            sums[0] += _fold8(dhc)
            sums[1] += _fold8(dln * xh)
            sums[2] += _fold8(dln)

        @pl.when(jnp.logical_not(inside))
        def _():
            o_ref[...] = jnp.zeros_like(o_ref)

    vec = pl.BlockSpec((1, CONV_WIDTH), lambda i: (0, 0))
    return pl.pallas_call(
        body, name="conv_bwd_norm", grid=(nb + 2,),
        in_specs=[pl.BlockSpec((blk, CONV_WIDTH), lambda i: (jnp.clip(i - 1, 0, nb - 1), 1)),
                  pl.BlockSpec((blk, CONV_WIDTH), lambda i: (jnp.clip(i - 1, 0, nb - 1), 0)), vec, vec],
        out_specs=[pl.BlockSpec((blk, CONV_WIDTH), lambda i: (i, 0)),
                   pl.BlockSpec((3, SUBLANES, CONV_WIDTH), lambda i: (0, 0, 0))],
        out_shape=[_sds(((nb + 2) * blk, CONV_WIDTH)), _sds((3, SUBLANES, CONV_WIDTH))],
        compiler_params=_params(("arbitrary",)),
    )(d_ycat, hc, ln_g, ln_b)


def _conv_bwd_taps(dhc_pad, hh_pad, z_all, w, n_rows):
    blk = _conv_block(n_rows)
    nblk = n_rows // blk + 2

    def body(dhc_ref, hh_ref, v_ref, g_ref, w_ref, dv_ref, dg_ref, dw_ref, dwin, hwin, dsems, hsems):
        @pl.when(pl.program_id(0) == 0)
        def _():
            dw_ref[...] = jnp.zeros_like(dw_ref)

        base = _stream_padded(dhc_ref, dwin, dsems, blk, nblk)
        _stream_padded(hh_ref, hwin, hsems, blk, nblk)

        def tile(t, _):
            r0 = pl.multiple_of(t * CONV_BWD_ROWS, CONV_BWD_ROWS) + base
            dh = dwin[pl.ds(r0 + blk, CONV_BWD_ROWS), :]
            acc = jnp.zeros((CONV_BWD_ROWS, CONV_WIDTH), F32)
            for k in range(CONV_K):
                off = (k - CONV_HALF) * GRID_W
                acc = acc + w_ref[k:k + 1, :] * dwin[pl.ds(r0 + blk - off, CONV_BWD_ROWS), :]
                dw_ref[k] += _fold8(dh * hwin[pl.ds(r0 + blk + off, CONV_BWD_ROWS), :])
            rs = pl.ds(pl.multiple_of(t * CONV_BWD_ROWS, CONV_BWD_ROWS), CONV_BWD_ROWS)
            sg = _sigmoid(g_ref[rs, :])
            vv = v_ref[rs, :]
            dv_ref[rs, :] = acc * sg
            dg_ref[rs, :] = acc * vv * sg * (1.0 - sg)
            return 0

        lax.fori_loop(0, blk // CONV_BWD_ROWS, tile, 0)

    row = pl.BlockSpec((blk, CONV_WIDTH), lambda i: (i, 0))
    return pl.pallas_call(
        body, name="conv_bwd_taps", grid=(n_rows // blk,),
        in_specs=[ANY, ANY,
            pl.BlockSpec((blk, CONV_WIDTH), lambda i: (i, 1)), pl.BlockSpec((blk, CONV_WIDTH), lambda i: (i, 2)),
            pl.BlockSpec((CONV_K, CONV_WIDTH), lambda i: (0, 0))],
        out_specs=[row, row, pl.BlockSpec((CONV_K, SUBLANES, CONV_WIDTH), lambda i: (0, 0, 0))],
        out_shape=[_sds((n_rows, CONV_WIDTH)), _sds((n_rows, CONV_WIDTH)), _sds((CONV_K, SUBLANES, CONV_WIDTH))],
        scratch_shapes=[pltpu.VMEM((nblk * blk, CONV_WIDTH), F32), pltpu.VMEM((nblk * blk, CONV_WIDTH), F32),
                        pltpu.SemaphoreType.DMA((nblk,)), pltpu.SemaphoreType.DMA((nblk,))],
        compiler_params=_params(("arbitrary",)),
    )(dhc_pad, hh_pad, z_all, z_all, w)


def _dz_assemble(du0, du1, dy, d_skip, dv, dgate, n_lat):
    rows = du0.shape[0]
    nb = rows // ROW_BLOCK

    w = S5_WIDTH

    def body(a_ref, b_ref, dy_ref, d_ref, dv_ref, dg_ref, o_ref):
        lat = pl.program_id(0) < n_lat

        @pl.when(lat)
        def _():
            o_ref[:, 0:w] = (a_ref[...] + b_ref[...] + dy_ref[...] * d_ref[...]).astype(o_ref.dtype)
            o_ref[:, w:2 * w] = dv_ref[...].astype(o_ref.dtype)
            o_ref[:, 2 * w:3 * w] = dg_ref[...].astype(o_ref.dtype)

        @pl.when(jnp.logical_not(lat))
        def _():
            o_ref[:, 0:w] = (a_ref[...] + b_ref[...]).astype(o_ref.dtype)
            o_ref[:, w:3 * w] = jnp.zeros((ROW_BLOCK, 2 * w), o_ref.dtype)

    all_rows = pl.BlockSpec((ROW_BLOCK, w), lambda i: (i, 0))
    lat_rows = pl.BlockSpec((ROW_BLOCK, w), lambda i: (jnp.minimum(i, n_lat - 1), 0))
    return pl.pallas_call(
        body, name="dz_assemble", grid=(nb,),
        in_specs=[all_rows, all_rows, lat_rows, pl.BlockSpec((1, w), lambda i: (0, 0)), lat_rows, lat_rows],
        out_specs=pl.BlockSpec((ROW_BLOCK, IN_COLS), lambda i: (i, 0)),
        out_shape=_sds((rows, IN_COLS), BF16), compiler_params=_params(("parallel",)),
    )(du0, du1, dy, d_skip, dv, dgate)


def _sum_parts(parts):
    _, r, c = parts.shape

    def body(p_ref, o_ref):
        acc = p_ref[0]
        for q in range(1, NDEV):
            acc = acc + p_ref[q]
        o_ref[...] = acc

    return pl.pallas_call(body, name="sum_parts", out_shape=_sds((r, c)), compiler_params=_params())(parts)


def _row_tile(r, c):
    best = r
    for t in (1024, 512, 256, 128, 64, 32, 16, 8):
        if r % t == 0 and t * c <= 128 * 1024:
            return t
    return best


def _adamw(name, w, gparts, m, v):
    r, c = w.shape
    np_ = gparts.shape[0]
    tr = _row_tile(r, c)

    def body(w_ref, g_ref, m_ref, v_ref, go_ref, d_ref, mo_ref, vo_ref):
        g = g_ref[0].astype(F32)
        for q in range(1, np_):
            g = g + g_ref[q].astype(F32)
        m2 = ADAM_B1 * m_ref[...] + (1.0 - ADAM_B1) * g
        v2 = ADAM_B2 * v_ref[...] + (1.0 - ADAM_B2) * jnp.square(g)
        m_hat = m2 / (1.0 - ADAM_B1 ** ADAM_STEP)
        v_hat = v2 / (1.0 - ADAM_B2 ** ADAM_STEP)
        go_ref[...] = g
        d_ref[...] = -ADAM_LR * (m_hat / (jnp.sqrt(v_hat) + ADAM_EPS) + ADAM_WD * w_ref[...])
        mo_ref[...] = m2
        vo_ref[...] = v2

    row = pl.BlockSpec((tr, c), lambda i: (i, 0))
    return pl.pallas_call(
        body, name=name, grid=(r // tr,),
        in_specs=[row, pl.BlockSpec((np_, tr, c), lambda i: (0, i, 0)), row, row],
        out_specs=[row] * 4, out_shape=[_sds((r, c))] * 4, compiler_params=_params(("parallel",)),
    )(w, gparts, m, v)


def _adamw_native(name, w, g, m, v):
    def body(w_ref, g_ref, m_ref, v_ref, d_ref, mo_ref, vo_ref):
        gv = g_ref[...]
        m2 = ADAM_B1 * m_ref[...] + (1.0 - ADAM_B1) * gv
        v2 = ADAM_B2 * v_ref[...] + (1.0 - ADAM_B2) * jnp.square(gv)
        m_hat = m2 / (1.0 - ADAM_B1 ** ADAM_STEP)
        v_hat = v2 / (1.0 - ADAM_B2 ** ADAM_STEP)
        d_ref[...] = -ADAM_LR * (m_hat / (jnp.sqrt(v_hat) + ADAM_EPS) + ADAM_WD * w_ref[...])
        mo_ref[...] = m2
        vo_ref[...] = v2

    return pl.pallas_call(body, name=name, out_shape=[_sds(w.shape)] * 3, compiler_params=_params())(w, g, m, v)


SMALL = ["c_ctx", "ada_b", "norm1_g", "s5_lam_re", "s5_lam_im", "s5_log_dt", "s5_d", "conv_b", "conv_ln_g", "conv_ln_b",
         "norm2_g", "final_g"]
SMALL_PACKED_ROWS = 24


def _pack_rows(parts, rows):
    flat = jnp.concatenate([p.reshape(-1).astype(F32) for p in parts])
    return jnp.pad(flat, (0, rows * D_MODEL - flat.shape[0])).reshape(rows, D_MODEL)


def _unpack_rows(packed, shapes):
    flat = packed.reshape(-1)
    out, off = [], 0
    for shape in shapes:
        size = 1
        for s in shape:
            size *= s
        out.append(flat[off:off + size].reshape(shape))
        off += size
    return out


def kernel(x, c, ctx, c_ctx, ada_w, ada_b, norm1_g, w_in, s5_lam_re, s5_lam_im, s5_log_dt, s5_b_re, s5_b_im, s5_c_re, s5_c_im, s5_d, s5_w_glu, conv_w, conv_b, conv_ln_g, conv_ln_b, w_out, norm2_g, mlp_w1, mlp_w2, final_g, loss_target, m_c_ctx, m_ada_w, m_ada_b, m_norm1_g, m_w_in, m_s5_lam_re, m_s5_lam_im, m_s5_log_dt, m_s5_b_re, m_s5_b_im, m_s5_c_re, m_s5_c_im, m_s5_d, m_s5_w_glu, m_conv_w, m_conv_b, m_conv_ln_g, m_conv_ln_b, m_w_out, m_norm2_g, m_mlp_w1, m_mlp_w2, m_final_g, v_c_ctx, v_ada_w, v_ada_b, v_norm1_g, v_w_in, v_s5_lam_re, v_s5_lam_im, v_s5_log_dt, v_s5_b_re, v_s5_b_im, v_s5_c_re, v_s5_c_im, v_s5_d, v_s5_w_glu, v_conv_w, v_conv_b, v_conv_ln_g, v_conv_ln_b, v_w_out, v_norm2_g, v_mlp_w1, v_mlp_w2, v_final_g):
    weights = dict(c_ctx=c_ctx, ada_w=ada_w, ada_b=ada_b, norm1_g=norm1_g, w_in=w_in, s5_lam_re=s5_lam_re, s5_lam_im=s5_lam_im, s5_log_dt=s5_log_dt, s5_b_re=s5_b_re, s5_b_im=s5_b_im, s5_c_re=s5_c_re, s5_c_im=s5_c_im, s5_d=s5_d, s5_w_glu=s5_w_glu, conv_w=conv_w, conv_b=conv_b, conv_ln_g=conv_ln_g, conv_ln_b=conv_ln_b, w_out=w_out, norm2_g=norm2_g, mlp_w1=mlp_w1, mlp_w2=mlp_w2, final_g=final_g)
    mom1 = dict(c_ctx=m_c_ctx, ada_w=m_ada_w, ada_b=m_ada_b, norm1_g=m_norm1_g, w_in=m_w_in, s5_lam_re=m_s5_lam_re, s5_lam_im=m_s5_lam_im, s5_log_dt=m_s5_log_dt, s5_b_re=m_s5_b_re, s5_b_im=m_s5_b_im, s5_c_re=m_s5_c_re, s5_c_im=m_s5_c_im, s5_d=m_s5_d, s5_w_glu=m_s5_w_glu, conv_w=m_conv_w, conv_b=m_conv_b, conv_ln_g=m_conv_ln_g, conv_ln_b=m_conv_ln_b, w_out=m_w_out, norm2_g=m_norm2_g, mlp_w1=m_mlp_w1, mlp_w2=m_mlp_w2, final_g=m_final_g)
    mom2 = dict(c_ctx=v_c_ctx, ada_w=v_ada_w, ada_b=v_ada_b, norm1_g=v_norm1_g, w_in=v_w_in, s5_lam_re=v_s5_lam_re, s5_lam_im=v_s5_lam_im, s5_log_dt=v_s5_log_dt, s5_b_re=v_s5_b_re, s5_b_im=v_s5_b_im, s5_c_re=v_s5_c_re, s5_c_im=v_s5_c_im, s5_d=v_s5_d, s5_w_glu=v_s5_w_glu, conv_w=v_conv_w, conv_b=v_conv_b, conv_ln_g=v_conv_ln_g, conv_ln_b=v_conv_ln_b, w_out=v_w_out, norm2_g=v_norm2_g, mlp_w1=v_mlp_w1, mlp_w2=v_mlp_w2, final_g=v_final_g)
    order = list(weights)

    me = 4 * lax.axis_index("x") + 2 * lax.axis_index("y") + lax.axis_index("c")
    xs, cs, tgt = x[0], ctx[0], loss_target[0]
    n_lat_rows, n_ctx_rows = xs.shape[0], cs.shape[0]
    n_rows = n_lat_rows + n_ctx_rows
    n_lat = n_lat_rows // ROW_BLOCK
    ada_cols = ada_w.shape[2]

    (c_all,), _ = _exchange("gather_c", [c], [True])
    c_all = c_all.reshape(NDEV, D_MODEL)

    cond_fwd = jnp.concatenate([c_all, c_ctx[None], jnp.zeros((7, D_MODEL), F32)])
    ada_b_loc = lax.dynamic_slice(ada_b, (0, me * ada_cols), (1, ada_cols))
    (mod_g,), mod_token = _exchange("gather_mod", [_ada_fwd(cond_fwd, ada_w[0], ada_b_loc)], [True])
    wi_send, wi_recv, wi_src, wi_land, wi_token = _exchange_start(
        "gather_w_in_start", [w_in[0].astype(BF16) + mod_token[0:1, 0:1].astype(BF16)], [True])
    mixer_w = [s5_w_glu[0].astype(BF16), conv_w[0] + wi_token[0:1, 0:1], w_out[0].astype(BF16)]
    mixer_send, mixer_recv, mixer_src, mixer_land, mixer_token = _exchange_start("gather_mixer_start", mixer_w, [True] * 3)
    mlp_w = [mlp_w1[0].astype(BF16), mlp_w2[0].astype(BF16) + mixer_token[0:1, 0:1].astype(BF16)]
    mlpw_send, mlpw_recv, mlpw_src, mlpw_land, mlpw_token = _exchange_start("gather_mlp_start", mlp_w, [True] * 2)
    mod_rows = jnp.transpose(mod_g, (1, 0, 2)).reshape(16, 6 * D_MODEL) + mlpw_token[0:1, 0:1]
    mod = lax.dynamic_slice(mod_rows, (me, 0), (1, 6 * D_MODEL)).reshape(6, D_MODEL)
    modc = mod_rows[8, :2 * D_MODEL].reshape(2, D_MODEL)
    sh1, sc1, g1, sh2, sc2, g2 = [mod[i:i + 1] for i in range(6)]

    lam_re, lam_im = s5_lam_re[0].reshape(2, 1, NSTATE), s5_lam_im[0].reshape(2, 1, NSTATE)
    ldt = jnp.repeat(s5_log_dt[0], S5_STATE, axis=-1).reshape(2, 1, NSTATE)
    bt_re = jnp.transpose(s5_b_re[0], (0, 3, 1, 2)).reshape(2, S5_GROUP, NSTATE)
    bt_im = jnp.transpose(s5_b_im[0], (0, 3, 1, 2)).reshape(2, S5_GROUP, NSTATE)
    groups_per_block = S5_GROUPS // S5_BLOCKS
    ct_re = jnp.tile(s5_c_re[0].reshape(2, S5_WIDTH, S5_STATE), (1, 1, groups_per_block))
    ct_im = jnp.tile(s5_c_im[0].reshape(2, S5_WIDTH, S5_STATE), (1, 1, groups_per_block))
    d_skip = s5_d[0].reshape(1, S5_WIDTH)
    perm = _segment_permutation()
    perm_t = perm.T
    disc = [_s5_discretise(f"s5_disc{d}", d == 0, lam_re[d], lam_im[d], ldt[d], bt_re[d], bt_im[d], ct_re[d], ct_im[d])
            for d in range(2)]

    a_all = _prenorm("prenorm1", xs, cs, norm1_g, jnp.stack([mod[0:2], modc]))
    before_w_in = a_all[0:SUBLANES, 0:LANES].astype(F32) + disc[0][0][0:SUBLANES, 0:LANES] + disc[1][0][0:SUBLANES, 0:LANES]
    wi_own, wi_landed = _exchange_wait("gather_w_in_wait", wi_send, wi_recv, wi_src, wi_land, [True], before_w_in)
    w_in_full = jnp.transpose(_with_own(wi_landed[0], wi_own[0], me), (1, 0, 2)).reshape(D_MODEL, IN_COLS)
    tm_all = 1088 if n_rows % 1088 == 0 else ROW_BLOCK
    (z_all,) = _matmul("in_proj", a_all, w_in_full, "nn", (n_rows, IN_COLS, D_MODEL), (tm_all, IN_COLS, D_MODEL),
                       [((n_rows, IN_COLS), F32)])

    states, y_dir = [], []
    for d in range(2):
        _, tab, _, bmat, cmat = disc[d]
        s, yd = _s5_scan_fwd(f"s5_scan_fwd{d}", d == 0, z_all, bmat, cmat, tab, perm, perm_t)
        states.append(s)
        y_dir.append(yd)
    mixer_own, mixer_landed = _exchange_wait("gather_mixer_wait", mixer_send, mixer_recv, mixer_src, mixer_land,
                                             [True] * 3, y_dir[1])
    glu_g, conv_w_g, w_out_g = [_with_own(l, o, me) for l, o in zip(mixer_landed, mixer_own)]
    glu_full = glu_g.reshape(S5_WIDTH, S5_WIDTH)
    conv_w_full = jnp.transpose(conv_w_g, (1, 0, 2)).reshape(CONV_K, CONV_WIDTH)
    w_out_full = w_out_g.reshape(D_MODEL, D_MODEL)
    ycat = _glu_fwd(z_all, y_dir[0], y_dir[1], d_skip, glu_full, n_lat_rows)

    hh_pad = _conv_gate(z_all, n_lat_rows)
    hc, ycat = _conv_fwd(hh_pad, conv_w_full, conv_b, conv_ln_g, conv_ln_b, ycat, n_lat_rows)

    tm = min(1024, n_lat_rows)
    tm_e = min(512, n_lat_rows)
    w1_cols = D_FF // NDEV
    row_vec = lambda tn: pl.BlockSpec((1, tn), lambda i, j, k: (0, j))
    out_tile = lambda t_m, t_n: pl.BlockSpec((t_m, t_n), lambda i, j, k: (i, j))
    full_rows = ((n_lat_rows, D_MODEL), F32)
    sums = ((n_lat_rows // tm_e, SUBLANES, D_MODEL), F32)
    sums_spec = pl.BlockSpec((None, SUBLANES, D_MODEL), lambda i, j, k: (i, 0, 0))
    vec = lambda v: (v, row_vec(D_MODEL))
    transposed_tile = lambda t_m, t_n: pl.BlockSpec((t_n, t_m), lambda i, j, k: (j, i))
    mix, h1, a2, a2_t = _matmul(
        "out_proj", ycat, w_out_full, "nn", (n_lat_rows, D_MODEL, D_MODEL), (tm_e, D_MODEL, D_MODEL),
        [full_rows, full_rows, ((n_lat_rows, D_MODEL), BF16), ((D_MODEL, n_lat_rows), BF16)],
        epi=_epi_residual_prenorm,
        epi_extra=[(xs, out_tile(tm_e, D_MODEL)), vec(g1), vec(norm2_g), vec(sc2), vec(sh2)],
        out_specs=[out_tile(tm_e, D_MODEL)] * 3 + [transposed_tile(tm_e, D_MODEL)])
    mlpw_own, mlpw_landed = _exchange_wait("gather_mlp_wait", mlpw_send, mlpw_recv, mlpw_src, mlpw_land, [True] * 2, a2)
    w1_g, w2_g = [_with_own(l, o, me) for l, o in zip(mlpw_landed, mlpw_own)]
    w2_full = w2_g.reshape(D_FF, D_MODEL)
    tm_up = min(2048, n_lat_rows)
    f, f_t = _matmul("mlp_up", a2, w1_g, "nn", (n_lat_rows, D_FF, D_MODEL), (tm_up, w1_cols, D_MODEL),
                     [((n_lat_rows, D_FF), BF16), ((D_FF, n_lat_rows), BF16)], epi=lambda acc: (acc, acc.T),
                     b_spec=pl.BlockSpec((None, D_MODEL, w1_cols), lambda i, j, k: (j, 0, 0)),
                     out_specs=[out_tile(tm_up, w1_cols), transposed_tile(tm_up, w1_cols)])
    sq_relu = lambda t: jnp.square(jnp.maximum(t, 0.0))
    mlp_out, d_h2, dm2, err_sums, d_final_g8 = _matmul(
        "mlp_down", f, w2_full, "nn", (n_lat_rows, D_MODEL, D_FF), (tm_e, D_MODEL, 2048),
        [full_rows, full_rows, ((n_lat_rows, D_MODEL), BF16), sums, sums], a_fn=sq_relu, epi=_epi_residual_loss,
        epi_extra=[(h1, out_tile(tm_e, D_MODEL)), vec(g2), (tgt, out_tile(tm_e, D_MODEL)), vec(final_g[None])],
        out_specs=[out_tile(tm_e, D_MODEL)] * 3 + [sums_spec] * 2)

    (d_f,) = _matmul("mlp_down_dx", dm2, w2_full, "nt", (n_lat_rows, D_FF, D_MODEL), (tm_up, 512, D_MODEL),
                     [((n_lat_rows, D_FF), BF16)],
                     epi=lambda acc, ft: (acc * 2.0 * jnp.maximum(ft.astype(F32), 0.0),),
                     epi_extra=[(f, out_tile(tm_up, 512))])
    tk_dw = min(2048, n_lat_rows)
    (g_w2,) = _matmul("mlp_down_dw", f_t, dm2, "nn", (D_FF, D_MODEL, n_lat_rows), (1024, D_MODEL, tk_dw),
                      [((D_FF, D_MODEL), F32)], a_fn=sq_relu)
    (g_w1,) = _matmul("mlp_up_dw", a2_t, d_f, "nn", (D_MODEL, D_FF, n_lat_rows), (D_MODEL, w1_cols, n_lat_rows),
                      [((NDEV, D_MODEL, w1_cols), F32)],
                      out_specs=[pl.BlockSpec((None, D_MODEL, w1_cols), lambda i, j, k: (j, 0, 0))])
    mlp_send, mlp_recv, mlp_src, mlp_land, mlp_token = _exchange_start(
        "scatter_mlp_start", [g_w1, g_w2.reshape(NDEV, D_FF // NDEV, D_MODEL)], [False] * 2)
    d_h1, dm1, *sums2 = _matmul(
        "mlp_up_dx", d_f, w1_g, "nt", (n_lat_rows, D_MODEL, D_FF), (tm_e, D_MODEL, 4 * w1_cols),
        [full_rows, ((n_lat_rows, D_MODEL), BF16)] + [sums] * 4, epi=_epi_norm_bwd,
        epi_extra=[(h1, out_tile(tm_e, D_MODEL)), (d_h2, out_tile(tm_e, D_MODEL)), (mlp_out, out_tile(tm_e, D_MODEL)),
                   vec(norm2_g), vec(sc2 + mlp_token[0:1, 0:1]), vec(g1)],
        b_spec=pl.BlockSpec((4, D_MODEL, w1_cols), lambda i, j, k: (k, 0, 0)), b_slabs=4,
        out_specs=[out_tile(tm_e, D_MODEL)] * 2 + [sums_spec] * 4)

    (d_ycat,) = _matmul("out_proj_dx", dm1, w_out_full, "nt", (n_lat_rows, D_MODEL, D_MODEL), (tm, D_MODEL, D_MODEL),
                        [((n_lat_rows, D_MODEL), F32)])
    (g_w_out,) = _matmul("out_proj_dw", ycat, dm1, "tn", (D_MODEL, D_MODEL, n_lat_rows), (D_MODEL, D_MODEL, 512),
                         [((D_MODEL, D_MODEL), F32)])

    dy, g_glu, dd8 = _glu_bwd(d_ycat, z_all, y_dir[0], y_dir[1], d_skip, glu_full, n_lat_rows)
    proj_send, proj_recv, proj_src, proj_land, proj_token = _exchange_start(
        "scatter_proj_start",
        [g_w_out.reshape(NDEV, D_MODEL // NDEV, D_MODEL), g_glu.reshape(NDEV, S5_WIDTH // NDEV, S5_WIDTH)], [False] * 2)
    perm = perm + proj_token[0:1, 0:1].astype(BF16)
    du, g_lam_re, g_lam_im, g_ldt, g_bt, g_cdiag = [], [], [], [], [], []
    for d in range(2):
        _, _, adj, bmat, cmat = disc[d]
        du_d, d_bdiag, d_cdiag, d_abar8 = _s5_scan_bwd(f"s5_scan_bwd{d}", d == 0, dy, z_all, states[d], bmat, cmat, adj,
                                                       perm, perm_t)
        du.append(du_d)
        d_bbar = jnp.transpose(d_bdiag.reshape(S5_BLOCKS, S5_GROUP, 2, NSTATE // S5_BLOCKS), (2, 1, 0, 3)).reshape(
            2 * S5_GROUP, NSTATE)
        d_lam8, d_bt = _s5_discretise_bwd(f"s5_disc_bwd{d}", lam_re[d], lam_im[d], ldt[d], bt_re[d], bt_im[d], d_abar8, d_bbar)
        g_lam_re.append(d_lam8[0].reshape(S5_GROUPS, S5_STATE))
        g_lam_im.append(d_lam8[1].reshape(S5_GROUPS, S5_STATE))
        g_ldt.append(d_lam8[2].reshape(S5_GROUPS, S5_STATE).sum(axis=-1))
        g_bt.append(d_bt)
        g_cdiag.append(d_cdiag)

    dhc_pad, conv_sums = _conv_bwd_norm(d_ycat, hc, conv_ln_g, conv_ln_b, n_lat_rows)
    d_v, d_gate, g_conv_w8 = _conv_bwd_taps(dhc_pad, hh_pad, z_all, conv_w_full, n_lat_rows)

    dz_all = _dz_assemble(du[0], du[1], dy, d_skip, d_v, d_gate, n_lat)
    (g_w_in_full,) = _matmul("in_proj_dw", a_all, dz_all, "tn", (D_MODEL, IN_COLS, n_rows), (D_MODEL, IN_COLS, tm_all),
                             [((D_MODEL, IN_COLS), F32)])
    g_w_in_parts = jnp.transpose(g_w_in_full.reshape(D_MODEL, NDEV, IN_COLS // NDEV), (1, 0, 2)).astype(BF16)
    win_send, win_recv, win_src, win_land, win_token = _exchange_start("scatter_w_in_start", [g_w_in_parts], [False])
    (d_a_all,) = _matmul("in_proj_dx", dz_all, w_in_full + win_token[0:1, 0:1].astype(BF16), "nt",
                         (n_rows, D_MODEL, IN_COLS), (tm_all, D_MODEL, IN_COLS), [((n_rows, D_MODEL), F32)])
    grad_x, sums1 = _norm_bwd("norm1_bwd", xs, d_a_all, 0, norm1_g, sc1, res=d_h1, aux=mix)
    (sums1c,) = _norm_bwd("norm1_bwd_ctx", cs, d_a_all, n_lat, norm1_g, modc[1:2])

    s1, s1c, s2 = sums1.sum(axis=1), sums1c.sum(axis=1), [p.sum(axis=(0, 1)) for p in sums2]
    d_mod = jnp.concatenate([s1[0], s1[1], s1[3], s2[0], s2[1], s2[3]])
    d_modc = jnp.concatenate([s1c[0], s1c[1], jnp.zeros((4 * D_MODEL,), F32)])
    (dmod_g,), _ = _exchange("gather_dmod", [jnp.stack([d_mod, d_modc])], [True])
    dmod16 = jnp.concatenate([dmod_g[:, 0], dmod_g[:, 1]])
    dmod16_loc = lax.dynamic_slice(dmod16, (0, me * ada_cols), (16, ada_cols))
    cond_bwd = jnp.concatenate([c_all, jnp.broadcast_to(c_ctx[None], (NDEV, D_MODEL))])
    g_ada_w, g_c_ctx8 = _ada_bwd(cond_bwd, dmod16_loc, ada_w[0], c_ctx[None])

    small_parts = dict(
        c_ctx=g_c_ctx8[0], ada_b=d_mod + d_modc, norm1_g=s1[2] + s1c[2],
        s5_lam_re=jnp.stack(g_lam_re), s5_lam_im=jnp.stack(g_lam_im), s5_log_dt=jnp.stack(g_ldt),
        s5_d=dd8.sum(axis=0), conv_b=conv_sums[0].sum(axis=0), conv_ln_g=conv_sums[1].sum(axis=0),
        conv_ln_b=conv_sums[2].sum(axis=0), norm2_g=s2[2], final_g=d_final_g8.sum(axis=(0, 1)))
    reduced_shapes = [(SMALL_PACKED_ROWS, D_MODEL), (2, 2 * S5_GROUP, NSTATE), (2,) + _S5_DIAG, (1,)]
    small_g = _pack_rows(
        [_pack_rows([small_parts[n] for n in SMALL], SMALL_PACKED_ROWS), jnp.stack(g_bt), jnp.stack(g_cdiag),
         (0.5 / D_MODEL * jnp.sum(err_sums)).reshape(1)], SMALL_ROWS).reshape(NDEV, SMALL_ROWS // NDEV, D_MODEL)
    g_conv_w_parts = jnp.transpose(g_conv_w8.sum(axis=1).reshape(CONV_K, NDEV, CONV_WIDTH // NDEV), (1, 0, 2))

    res = {}

    def own_chunk(src):
        return lax.dynamic_index_in_dim(src, me, 0, keepdims=False)

    def adamw_big(name, parts):
        outs = _adamw("adamw_" + name, weights[name][0], parts, mom1[name][0], mom2[name][0])
        res[name] = [o[None] for o in outs]
        return outs[0]

    sm_send, sm_recv, sm_src, sm_land, sm_token = _exchange_start("scatter_small_start", [g_conv_w_parts, small_g],
                                                                  [False] * 2)
    done = adamw_big("ada_w", g_ada_w[None] + sm_token[0:1, 0:1])
    mlp_src, mlp_landed = _exchange_wait("scatter_mlp_wait", mlp_send, mlp_recv, mlp_src, mlp_land, [False] * 2, done)
    p_w1, p_w2 = [_with_own(l, own_chunk(s), me) for l, s in zip(mlp_landed, mlp_src)]
    adamw_big("mlp_w1", p_w1)
    done = adamw_big("mlp_w2", p_w2)
    sm_src, sm_landed = _exchange_wait("scatter_small_wait", sm_send, sm_recv, sm_src, sm_land, [False] * 2, done)
    p_conv_w, p_small = [_with_own(l, own_chunk(s), me) for l, s in zip(sm_landed, sm_src)]
    ga_send, ga_recv, ga_src, ga_land, ga_token = _exchange_start("gather_small_start", [_sum_parts(p_small)], [True])
    proj_src, proj_landed = _exchange_wait("scatter_proj_wait", proj_send, proj_recv, proj_src, proj_land, [False] * 2,
                                           ga_token)
    p_w_out, p_glu = [_with_own(l, own_chunk(s), me) for l, s in zip(proj_landed, proj_src)]
    adamw_big("w_out", p_w_out)
    done = adamw_big("s5_w_glu", p_glu)
    win_src, win_landed = _exchange_wait("scatter_w_in_wait", win_send, win_recv, win_src, win_land, [False], done)
    adamw_big("w_in", _with_own(win_landed[0], own_chunk(win_src[0]), me))
    done = adamw_big("conv_w", p_conv_w)
    ga_own, ga_landed = _exchange_wait("gather_small_wait", ga_send, ga_recv, ga_src, ga_land, [True], done)
    small_all = _with_own(ga_landed[0], ga_own[0], me).reshape(1, SMALL_ROWS, D_MODEL)
    _, r_bt, r_cdiag, loss = _unpack_rows(small_all, reduced_shapes)
    loss = loss.reshape(())
    pack = lambda src: _pack_rows([src[n] for n in SMALL], SMALL_PACKED_ROWS)
    outs = _adamw("adamw_small", pack(weights), small_all, pack(mom1), pack(mom2))
    unpacked = [_unpack_rows(o, [weights[n].shape for n in SMALL]) for o in outs]
    for i, name in enumerate(SMALL):
        res[name] = [u[i] for u in unpacked]
    to_ghp = lambda t: jnp.transpose(t.reshape(2, S5_GROUP, S5_GROUPS, S5_STATE), (0, 2, 1, 3))[None]
    r_c = jnp.transpose(r_cdiag.reshape(2, S5_BLOCKS, S5_GROUP, 2, groups_per_block, S5_STATE), (3, 0, 1, 4, 2, 5)).reshape(
        2, 1, 2, S5_GROUPS, S5_GROUP, S5_STATE)
    swap = lambda t: jnp.swapaxes(t, -1, -2)
    for name, grad in (("s5_b_re", to_ghp(r_bt[:, :S5_GROUP])), ("s5_b_im", to_ghp(r_bt[:, S5_GROUP:]))):
        outs = _adamw_native("adamw_" + name, swap(weights[name]), grad, swap(mom1[name]), swap(mom2[name]))
        res[name] = [swap(grad), *[swap(o) for o in outs]]
    for name, grad in (("s5_c_re", r_c[0]), ("s5_c_im", -r_c[1])):
        res[name] = [grad, *_adamw_native("adamw_" + name, weights[name], grad, mom1[name], mom2[name])]

    return (loss, grad_x[None], *[res[n][0] for n in order], *[res[n][1] for n in order],
            *[res[n][2] for n in order], *[res[n][3] for n in order])
```

```python
import jax
import jax.numpy as jnp
from jax import lax
from jax.experimental import pallas as pl
from jax.experimental.pallas import tpu as pltpu

F32 = jnp.float32
BF16 = jnp.bfloat16
MESH = pl.DeviceIdType.MESH
ANY = pl.BlockSpec(memory_space=pl.ANY)

NDEV = 8
D_MODEL = 1024
GRID_W = 64
S5_WIDTH = 512
S5_GROUP = 16
S5_GROUPS = 32
S5_STATE = 64
NSTATE = S5_GROUPS * S5_STATE
CONV_WIDTH = 512
CONV_K = 31
IN_COLS = S5_WIDTH + 2 * CONV_WIDTH
D_FF = 4 * D_MODEL
EPS_RMS = 1e-6
EPS_LN = 1e-5
ADAM_LR = 0.001
ADAM_B1 = 0.9
ADAM_B2 = 0.999
ADAM_EPS = 1e-08
ADAM_WD = 0.01
ADAM_STEP = 10

SUBLANES = 8
LANES = 128
ROW_BLOCK = 256
SCAN_LANES = 512
SCAN_UNROLL = 32
SEGMENTS = SUBLANES
STEPS = ROW_BLOCK // SEGMENTS
S5_BLOCKS = 4
S5_BLOCK_WIDTH = S5_WIDTH // S5_BLOCKS
CONV_ROWS = 64
CONV_BWD_ROWS = 32
VMEM_LIMIT = 48 * 1024 * 1024
SMALL_ROWS = 320


def _params(sem=None):
    kw = dict(vmem_limit_bytes=VMEM_LIMIT)
    if sem is not None:
        kw["dimension_semantics"] = sem
    return pltpu.CompilerParams(**kw)


def _sds(shape, dtype=F32):
    return jax.ShapeDtypeStruct(tuple(shape), dtype)


def _fold8(x):
    return x.reshape(x.shape[0] // SUBLANES, SUBLANES, x.shape[1]).sum(axis=0)


def _sigmoid(x):
    return 1.0 / (1.0 + jnp.exp(-x))


def _silu(x):
    return x * _sigmoid(x)


def _dsilu(x):
    s = _sigmoid(x)
    return s * (1.0 + x * (1.0 - s))


_GELU_C = 0.7978845608028654


def _gelu(x):
    return 0.5 * x * (1.0 + jnp.tanh(_GELU_C * (x + 0.044715 * x * x * x)))


def _dgelu(x):
    t = jnp.tanh(_GELU_C * (x + 0.044715 * x * x * x))
    return 0.5 * (1.0 + t) + 0.5 * x * (1.0 - t * t) * _GELU_C * (1.0 + 3.0 * 0.044715 * x * x)


def _rms(x):
    rstd = lax.rsqrt(jnp.mean(x * x, axis=-1, keepdims=True) + EPS_RMS)
    return x * rstd, rstd


def _epi_residual_prenorm(acc, res, gate, gain, scale, shift):
    h = res + gate * acc
    xh, _ = _rms(h)
    a = (xh * gain) * (1.0 + scale) + shift
    return acc, h, a, a.T


def _epi_residual_loss(acc, res, gate, target, gain):
    h = res + gate * acc
    xh, rstd = _rms(h)
    err = xh * gain - target
    dy = err * (1.0 / h.shape[-1])
    dxh = dy * gain
    dh = rstd * (dxh - xh * jnp.mean(dxh * xh, axis=-1, keepdims=True))
    return acc, dh, dh * gate, _fold8(err * err), _fold8(dy * xh)


def _epi_norm_bwd(d_act, x, res, aux, gain, scale, gate=None):
    xh, rstd = _rms(x)
    dn = d_act * (1.0 + scale)
    dxh = dn * gain
    dx = res + rstd * (dxh - xh * jnp.mean(dxh * xh, axis=-1, keepdims=True))
    sums = (_fold8(d_act), _fold8(d_act * (xh * gain)), _fold8(dn * xh), _fold8(res * aux))
    return (dx, *sums) if gate is None else (dx, dx * gate, *sums)


def _dot(a, b, mode):
    dims = {"nn": (((1,), (0,)), ((), ())), "nt": (((1,), (1,)), ((), ())), "tn": (((0,), (0,)), ((), ()))}[mode]
    return lax.dot_general(a, b, dims, preferred_element_type=F32)


def _peers(x, y, c):
    out = []
    for k in range(1, NDEV):
        px = 1 - x if k & 4 else x
        py = 1 - y if k & 2 else y
        pc = 1 - c if k & 1 else c
        out.append(((px, py, pc), 4 * px + 2 * py + pc))
    return out


def _exchange_copies(src, land, send_sems, recv_sems, gather):
    x, y, c = lax.axis_index("x"), lax.axis_index("y"), lax.axis_index("c")
    me = 4 * x + 2 * y + c
    out = []
    for a in range(len(src)):
        for k, (peer, plin) in enumerate(_peers(x, y, c)):
            chunk = src[a] if gather[a] else src[a].at[plin]
            sems = dict(send_sem=send_sems.at[a * (NDEV - 1) + k], recv_sem=recv_sems.at[a * (NDEV - 1) + k],
                        device_id=peer, device_id_type=MESH)
            out.append((pltpu.make_async_remote_copy(src_ref=chunk, dst_ref=land[a].at[me], **sems),
                        pltpu.make_async_remote_copy(src_ref=chunk, dst_ref=land[a].at[plin], **sems)))
    return out


def _exchange(name, srcs, gather):
    n = len(srcs)
    outs = [_sds(((NDEV,) + s.shape) if g else s.shape, s.dtype) for s, g in zip(srcs, gather)]

    def body(*refs):
        src, dst, token = refs[:n], refs[n:2 * n], refs[2 * n]
        send_sems, recv_sems, local_sems = refs[2 * n + 1:]
        me = 4 * lax.axis_index("x") + 2 * lax.axis_index("y") + lax.axis_index("c")
        local = [pltpu.make_async_copy(src[a] if gather[a] else src[a].at[me], dst[a].at[me], local_sems.at[a])
                 for a in range(n)]
        for copy in local:
            copy.start()
        copies = _exchange_copies(src, dst, send_sems, recv_sems, gather)
        for copy, _ in copies:
            copy.start()
        token[...] = jnp.zeros_like(token)
        for copy, landing in copies:
            copy.wait_send()
            landing.wait_recv()
        for copy in local:
            copy.wait()

    nsem = n * (NDEV - 1)
    out = pl.pallas_call(
        body, name=name, out_shape=outs + [_sds((SUBLANES, LANES))], in_specs=[ANY] * n,
        out_specs=[ANY] * n + [pl.BlockSpec(memory_space=pltpu.VMEM)],
        scratch_shapes=[pltpu.SemaphoreType.DMA((nsem,)), pltpu.SemaphoreType.DMA((nsem,)), pltpu.SemaphoreType.DMA((n,))],
    )(*srcs)
    return out[:n], out[n]


HBM = pl.BlockSpec(memory_space=pltpu.HBM)
SEM = pl.BlockSpec(memory_space=pltpu.SEMAPHORE)
EFFECT = pltpu.SideEffectType.DATAFLOW_SIDE_EFFECTING


def _exchange_start(name, srcs, gather):
    n = len(srcs)
    lands = [lax.empty(((NDEV,) + s.shape) if g else s.shape, s.dtype) for s, g in zip(srcs, gather)]

    def body(*refs):
        src, land = refs[:n], refs[n:2 * n]
        send_sems, recv_sems = refs[2 * n], refs[2 * n + 1]
        token = refs[-1]
        for copy, _ in _exchange_copies(src, land, send_sems, recv_sems, gather):
            copy.start()
        token[...] = jnp.zeros_like(token)

    hbm = lambda v: pltpu.HBM(v.shape, v.dtype)
    nsem = n * (NDEV - 1)
    out = pl.pallas_call(
        body, name=name,
        out_shape=(pltpu.SemaphoreType.DMA((nsem,)), pltpu.SemaphoreType.DMA((nsem,)), *[hbm(v) for v in srcs],
                   *[hbm(v) for v in lands], _sds((SUBLANES, LANES))),
        in_specs=[HBM] * (2 * n), out_specs=(SEM, SEM, *([HBM] * (2 * n)), pl.BlockSpec(memory_space=pltpu.VMEM)),
        input_output_aliases={i: 2 + i for i in range(2 * n)},
        compiler_params=pltpu.CompilerParams(has_side_effects=EFFECT),
    )(*[pltpu.with_memory_space_constraint(v, pltpu.HBM) for v in list(srcs) + lands])
    return out[0], out[1], out[2:2 + n], out[2 + n:2 + 2 * n], out[-1]


def _exchange_wait(name, send_sems, recv_sems, srcs, lands, gather, after):
    n = len(srcs)

    def body(*refs):
        src, land = refs[:n], refs[n:2 * n]
        send_ref, recv_ref = refs[2 * n], refs[2 * n + 1]
        for copy, landing in _exchange_copies(src, land, send_ref, recv_ref, gather):
            copy.wait_send()
            landing.wait_recv()

    hbm = lambda v: pltpu.HBM(v.shape, v.dtype)
    out = pl.pallas_call(
        body, name=name, out_shape=[hbm(v) for v in list(srcs) + list(lands)],
        in_specs=[HBM] * (2 * n) + [SEM, SEM, ANY], out_specs=[HBM] * (2 * n),
        input_output_aliases={i: i for i in range(2 * n)},
        compiler_params=pltpu.CompilerParams(has_side_effects=EFFECT),
    )(*srcs, *lands, send_sems, recv_sems, after)
    return out[:n], out[n:]


def _with_own(landed, own, me):
    return lax.dynamic_update_slice(landed, own[None], (me,) + (0,) * own.ndim)


def _matmul(name, a, b, mode, mnk, tiles, outs, a_spec=None, b_spec=None, a_fn=None, a_extra=(),
            epi=None, epi_extra=(), out_specs=None, b_slabs=1):
    m_, n_, k_ = mnk
    tm, tn, tk = tiles
    nk = k_ // tk
    if a_spec is None:
        a_spec = (pl.BlockSpec((tk, tm), lambda i, j, k: (k, i)) if mode == "tn"
                  else pl.BlockSpec((tm, tk), lambda i, j, k: (i, k)))
    if b_spec is None:
        b_spec = (pl.BlockSpec((tn, tk), lambda i, j, k: (j, k)) if mode == "nt"
                  else pl.BlockSpec((tk, tn), lambda i, j, k: (k, j)))
    if out_specs is None:
        out_specs = [pl.BlockSpec((tm, tn), lambda i, j, k: (i, j)) for _ in outs]
    na, ne, no = len(a_extra), len(epi_extra), len(outs)

    def body(*refs):
        a_ref, b_ref = refs[0], refs[1]
        ax = refs[2:2 + na]
        ex = refs[2 + na:2 + na + ne]
        o = refs[2 + na + ne:2 + na + ne + no]

        def finish(res):
            res = epi(res, *[r[...] for r in ex]) if epi is not None else (res,)
            for ref, val in zip(o, res):
                ref[...] = val.astype(ref.dtype)

        at = a_ref[...]
        if a_fn is not None:
            at = a_fn(at, *[r[...] for r in ax])
        at = at.astype(BF16)
        if b_slabs == 1:
            part = _dot(at, b_ref[...].astype(BF16), mode)
        else:
            ks = tk // b_slabs
            part = _dot(at[:, 0:ks], b_ref[0].astype(BF16), mode)
            for s in range(1, b_slabs):
                part = part + _dot(at[:, s * ks:(s + 1) * ks], b_ref[s].astype(BF16), mode)
        if nk == 1:
            finish(part)
            return
        acc = refs[-1]
        k = pl.program_id(2)

        @pl.when(k == 0)
        def _():
            acc[...] = part

        @pl.when(k > 0)
        def _():
            acc[...] += part

        @pl.when(k == nk - 1)
        def _():
            finish(acc[...])

    return pl.pallas_call(
        body, name=name, grid=(m_ // tm, n_ // tn, nk),
        in_specs=[a_spec, b_spec] + [s for _, s in a_extra] + [s for _, s in epi_extra],
        out_specs=out_specs, out_shape=[_sds(s, d) for s, d in outs],
        scratch_shapes=[pltpu.VMEM((tm, tn), F32)] if nk > 1 else [],
        compiler_params=_params(("parallel", "parallel", "arbitrary")),
    )(a, b, *[x for x, _ in a_extra], *[x for x, _ in epi_extra])


def _prenorm(name, x, ctx, gain, shsc):
    n_lat = x.shape[0] // ROW_BLOCK
    n_ctx = 0 if ctx is None else ctx.shape[0] // ROW_BLOCK
    d = x.shape[1]

    def norm(src, g_ref, m_ref, o_ref):
        xv = src[...]
        xh = xv * lax.rsqrt(jnp.mean(xv * xv, axis=-1, keepdims=True) + EPS_RMS)
        o_ref[...] = ((xh * g_ref[...]) * (1.0 + m_ref[1:2, :]) + m_ref[0:1, :]).astype(o_ref.dtype)

    def body(*refs):
        if ctx is None:
            x_ref, g_ref, m_ref, o_ref = refs
            norm(x_ref, g_ref, m_ref, o_ref)
        else:
            x_ref, c_ref, g_ref, m_ref, o_ref = refs
            i = pl.program_id(0)

            @pl.when(i < n_lat)
            def _():
                norm(x_ref, g_ref, m_ref, o_ref)

            @pl.when(i >= n_lat)
            def _():
                norm(c_ref, g_ref, m_ref, o_ref)

    in_specs = [pl.BlockSpec((ROW_BLOCK, d), lambda i: (jnp.minimum(i, n_lat - 1), 0))]
    args = [x]
    if ctx is not None:
        in_specs.append(pl.BlockSpec((ROW_BLOCK, d), lambda i: (jnp.maximum(i - n_lat, 0), 0)))
        args.append(ctx)
    in_specs += [pl.BlockSpec((1, d), lambda i: (0, 0)),
                 pl.BlockSpec((None, 2, d), lambda i: (jnp.minimum(i // n_lat, 1), 0, 0))]
    args += [gain, shsc]
    return pl.pallas_call(
        body, name=name, grid=(n_lat + n_ctx,), in_specs=in_specs,
        out_specs=pl.BlockSpec((ROW_BLOCK, d), lambda i: (i, 0)),
        out_shape=_sds(((n_lat + n_ctx) * ROW_BLOCK, d), BF16),
        compiler_params=_params(("parallel",)),
    )(*args)


def _norm_bwd(name, x, d_act, d_act_row0, gain, scale, res=None, aux=None):
    rows, d = x.shape
    nb = rows // ROW_BLOCK
    has_res = res is not None

    def body(*refs):
        if has_res:
            x_ref, da_ref, g_ref, sc_ref, r_ref, aux_ref, dx_ref, sums = refs
        else:
            x_ref, da_ref, g_ref, sc_ref, sums = refs
        i = pl.program_id(0)

        @pl.when(i == 0)
        def _():
            sums[...] = jnp.zeros_like(sums)

        xv, da = x_ref[...], da_ref[...]
        rstd = lax.rsqrt(jnp.mean(xv * xv, axis=-1, keepdims=True) + EPS_RMS)
        xh = xv * rstd
        g = g_ref[...]
        dn = da * (1.0 + sc_ref[...])
        sums[0] += _fold8(da)
        sums[1] += _fold8(da * (xh * g))
        sums[2] += _fold8(dn * xh)
        if has_res:
            dxh = dn * g
            dx = rstd * (dxh - xh * jnp.mean(dxh * xh, axis=-1, keepdims=True))
            rv = r_ref[...]
            dx_ref[...] = rv + dx
            sums[3] += _fold8(rv * aux_ref[...])

    row = lambda i: (i, 0)
    vec = pl.BlockSpec((1, d), lambda i: (0, 0))
    in_specs = [pl.BlockSpec((ROW_BLOCK, d), row), pl.BlockSpec((ROW_BLOCK, d), lambda i: (i + d_act_row0, 0)), vec, vec]
    args = [x, d_act, gain, scale]
    out_shape = [_sds((4, SUBLANES, d))]
    out_specs = [pl.BlockSpec((4, SUBLANES, d), lambda i: (0, 0, 0))]
    if has_res:
        in_specs += [pl.BlockSpec((ROW_BLOCK, d), row), pl.BlockSpec((ROW_BLOCK, d), row)]
        args += [res, aux]
        out_shape = [_sds((rows, d))] + out_shape
        out_specs = [pl.BlockSpec((ROW_BLOCK, d), row)] + out_specs
    return pl.pallas_call(
        body, name=name, grid=(nb,), in_specs=in_specs, out_specs=out_specs, out_shape=out_shape,
        compiler_params=_params(("arbitrary",)),
    )(*args)


def _ada_fwd(cond16, ada_w_loc, ada_b_loc):
    cols = ada_w_loc.shape[1]

    def body(c_ref, w_ref, b_ref, o_ref):
        s = _silu(c_ref[...]).astype(BF16)
        o_ref[...] = _dot(s, w_ref[...].astype(BF16), "nn") + b_ref[...]

    return pl.pallas_call(body, name="ada_fwd", out_shape=_sds((16, cols)), compiler_params=_params())(
        cond16, ada_w_loc, ada_b_loc)


def _ada_bwd(cond16, dmod16, ada_w_loc, c_ctx_row):
    k_, cols = ada_w_loc.shape

    def body(c_ref, dm_ref, w_ref, cc_ref, gw_ref, gc_ref):
        s = _silu(c_ref[...]).astype(BF16)
        dm = dm_ref[...]
        gw_ref[...] = _dot(s, dm.astype(BF16), "tn")
        dmc = jnp.sum(dm[8:16, :], axis=0, keepdims=True)
        dmc8 = jnp.broadcast_to(dmc, (SUBLANES, cols)).astype(BF16)
        ds = _dot(dmc8, w_ref[...].astype(BF16), "nt")
        row = lax.broadcasted_iota(jnp.int32, ds.shape, 0)
        gc_ref[...] = jnp.where(row == 0, ds * _dsilu(cc_ref[...]), 0.0)

    return pl.pallas_call(body, name="ada_bwd", out_shape=[_sds((k_, cols)), _sds((SUBLANES, k_))],
                          compiler_params=_params())(cond16, dmod16, ada_w_loc, c_ctx_row)


def _cmul(a, b):
    return a[0] * b[0] - a[1] * b[1], a[0] * b[1] + a[1] * b[0]


def _disc(lam_re, lam_im, ldt):
    dt = jnp.exp(ldt)
    mag = jnp.exp(lam_re * dt)
    th = lam_im * dt
    a_re, a_im = mag * jnp.cos(th), mag * jnp.sin(th)
    den = lam_re * lam_re + lam_im * lam_im
    n_re = a_re - 1.0
    f_re = (n_re * lam_re + a_im * lam_im) / den
    f_im = (a_im * lam_re - n_re * lam_im) / den
    return dt, mag, th, a_re, a_im, den, n_re, f_re, f_im


def _block_diag_mask(shape):
    row = lax.broadcasted_iota(jnp.int32, shape, 0)
    col = lax.broadcasted_iota(jnp.int32, shape, 1)
    return lax.shift_right_logical(row, 4) == lax.shift_right_logical(col, 6)


TAB_A = 0
TAB_BIG = 1
TAB_SEG = 4
TAB_PW = 5
TAB_ROWS = TAB_PW + STEPS


def _s5_discretise(name, ascending, lam_re, lam_im, ldt, bt_re, bt_im, ct_re, ct_im):
    def write_tables(ref, pw, big, asc, sign):
        row = lax.broadcasted_iota(jnp.int32, (SUBLANES, NSTATE), 0)
        full = lambda v: jnp.broadcast_to(v, (SUBLANES, NSTATE))

        def put(t, p):
            ref[0, t] = full(p[0])
            ref[1, t] = full(sign * p[1])

        put(TAB_A, pw[0])
        for t in range(3):
            put(TAB_BIG + t, big[t])
        seg = [big[0]]
        for _ in range(SEGMENTS - 1):
            seg.append(_cmul(seg[-1], big[0]))
        seg_re = jnp.zeros((SUBLANES, NSTATE), F32)
        seg_im = jnp.zeros((SUBLANES, NSTATE), F32)
        for r in range(SEGMENTS):
            p = seg[r] if asc else seg[SEGMENTS - 1 - r]
            seg_re = jnp.where(row == r, p[0], seg_re)
            seg_im = jnp.where(row == r, sign * p[1], seg_im)
        ref[0, TAB_SEG] = seg_re
        ref[1, TAB_SEG] = seg_im
        for k in range(STEPS):
            put(TAB_PW + k, pw[k])

    def body(lr_ref, li_ref, ldt_ref, br_ref, bi_ref, cr_ref, ci_ref, bb_ref, tab_ref, adj_ref, bm_ref, cm_ref):
        _, _, _, a_re, a_im, _, _, f_re, f_im = _disc(lr_ref[...], li_ref[...], ldt_ref[...])
        bre, bim = br_ref[...], bi_ref[...]
        bb_re = f_re * bre - f_im * bim
        bb_im = f_re * bim + f_im * bre
        bb_ref[0:S5_GROUP, :] = bb_re
        bb_ref[S5_GROUP:2 * S5_GROUP, :] = bb_im
        pw = [(a_re, a_im)]
        for _ in range(STEPS - 1):
            pw.append(_cmul(pw[-1], (a_re, a_im)))
        big = [pw[STEPS - 1]]
        for _ in range(2):
            big.append(_cmul(big[-1], big[-1]))
        write_tables(tab_ref, pw, big, ascending, 1.0)
        write_tables(adj_ref, pw, big, not ascending, -1.0)
        half = NSTATE // S5_BLOCKS
        mask = _block_diag_mask((S5_BLOCK_WIDTH, half))
        tile = lambda v: jnp.broadcast_to(v[None], (S5_BLOCK_WIDTH // S5_GROUP, S5_GROUP, half)).reshape(S5_BLOCK_WIDTH, half)
        for c in range(S5_BLOCKS):
            cols = slice(c * half, (c + 1) * half)
            rows = slice(c * S5_BLOCK_WIDTH, (c + 1) * S5_BLOCK_WIDTH)
            bm_ref[c, :, 0:half] = jnp.where(mask, tile(bb_re[:, cols]), 0.0).astype(BF16)
            bm_ref[c, :, half:2 * half] = jnp.where(mask, tile(bb_im[:, cols]), 0.0).astype(BF16)
            cm_ref[c, :, 0:half] = jnp.where(mask, cr_ref[rows, :], 0.0).astype(BF16)
            cm_ref[c, :, half:2 * half] = jnp.where(mask, -ci_ref[rows, :], 0.0).astype(BF16)

    blocked = _sds((S5_BLOCKS, S5_BLOCK_WIDTH, 2 * NSTATE // S5_BLOCKS), BF16)
    return pl.pallas_call(
        body, name=name,
        out_shape=[_sds((2 * S5_GROUP, NSTATE)), _sds((2, TAB_ROWS, SUBLANES, NSTATE)),
                   _sds((2, TAB_ROWS, SUBLANES, NSTATE)), blocked, blocked],
        compiler_params=_params(),
    )(lam_re, lam_im, ldt, bt_re, bt_im, ct_re, ct_im)


def _s5_discretise_bwd(name, lam_re, lam_im, ldt, bt_re, bt_im, d_abar8, d_bbar):
    def body(lr_ref, li_ref, ldt_ref, br_ref, bi_ref, da_ref, db_ref, dl_ref, dbt_ref):
        lam_re, lam_im = lr_ref[...], li_ref[...]
        dt, mag, _, a_re, a_im, den, n_re, f_re, f_im = _disc(lam_re, lam_im, ldt_ref[...])
        bre, bim = br_ref[...], bi_ref[...]
        dbr, dbi = db_ref[0:S5_GROUP, :], db_ref[S5_GROUP:2 * S5_GROUP, :]
        dbt_ref[0:S5_GROUP, :] = f_re * dbr + f_im * dbi
        dbt_ref[S5_GROUP:2 * S5_GROUP, :] = f_re * dbi - f_im * dbr
        df_re = jnp.sum(bre * dbr + bim * dbi, axis=0, keepdims=True)
        df_im = jnp.sum(bre * dbi - bim * dbr, axis=0, keepdims=True)
        da = da_ref[...]
        da_re = jnp.sum(da[:, 0:NSTATE], axis=0, keepdims=True)
        da_im = jnp.sum(da[:, NSTATE:2 * NSTATE], axis=0, keepdims=True)
        da_re = da_re + (df_re * lam_re - df_im * lam_im) / den
        da_im = da_im + (df_re * lam_im + df_im * lam_re) / den
        ff = (f_re * df_re + f_im * df_im) * 2.0 / den
        d_lr = (df_re * n_re + df_im * a_im) / den - ff * lam_re
        d_li = (df_re * a_im - df_im * n_re) / den - ff * lam_im
        d_mag = (da_re * a_re + da_im * a_im) / mag
        d_th = da_im * a_re - da_re * a_im
        d_lr = d_lr + d_mag * mag * dt
        d_li = d_li + d_th * dt
        d_ldt = (d_mag * mag * lam_re + d_th * lam_im) * dt
        row = lax.broadcasted_iota(jnp.int32, (SUBLANES, NSTATE), 0)
        dl_ref[...] = jnp.where(row == 0, d_lr, jnp.where(row == 1, d_li, jnp.where(row == 2, d_ldt, 0.0)))

    return pl.pallas_call(
        body, name=name, out_shape=[_sds((SUBLANES, NSTATE)), _sds((2 * S5_GROUP, NSTATE))],
        compiler_params=_params(),
    )(lam_re, lam_im, ldt, bt_re, bt_im, d_abar8, d_bbar)


def _segment_permutation():
    rho = jnp.arange(ROW_BLOCK)
    src = STEPS * (rho % SEGMENTS) + rho // SEGMENTS
    return (src[:, None] == jnp.arange(ROW_BLOCK)[None, :]).astype(BF16)


def _permute_rows(perm_ref, v):
    return _dot(perm_ref[...], v, "nn").astype(BF16)


def _unpermute_rows(perm_t_ref, v):
    hi = v.astype(BF16)
    lo = (v - hi.astype(F32)).astype(BF16)
    return _dot(perm_t_ref[...], hi, "nn") + _dot(perm_t_ref[...], lo, "nn")


def _unrolled_loop(step, init):
    def trip(o, state):
        for u in range(SCAN_UNROLL):
            state = step(o * SCAN_UNROLL + u, state)
        return state

    if SCAN_UNROLL == STEPS:
        return trip(0, init)
    return lax.fori_loop(0, STEPS // SCAN_UNROLL, trip, init)


def _scan_chunk(x_ref, out_ref, tab_ref, carry_re, carry_im, ascending, pair_ref=None, acc_ref=None, lane_chunks=None):
    w = SCAN_LANES
    half = NSTATE // S5_BLOCKS
    row = lax.broadcasted_iota(jnp.int32, (SUBLANES, w), 0)
    last = (SEGMENTS - 1) if ascending else 0

    def from_previous_segment(v, k, fill):
        if ascending:
            return jnp.where(row >= k, pltpu.roll(v, k, 0), fill)
        return jnp.where(row < SEGMENTS - k, pltpu.roll(v, SEGMENTS - k, 0), fill)

    def tile_rows(k):
        return pl.ds(pl.multiple_of((k if ascending else STEPS - 1 - k) * SUBLANES, SUBLANES), SUBLANES)

    for j in (range(NSTATE // w) if lane_chunks is None else lane_chunks):
        n_l = pl.ds(j * w, w)
        lane0 = (j * w // half) * 2 * half + (j * w) % half
        re_l, im_l = pl.ds(lane0, w), pl.ds(lane0 + half, w)
        tab = lambda t, n_l=n_l: (tab_ref[0, t, :, n_l], tab_ref[1, t, :, n_l])
        a_re, a_im = tab(TAB_A)

        def local_step(k, h):
            rs = tile_rows(k)
            h_re = a_re * h[0] - a_im * h[1] + x_ref[rs, re_l]
            h_im = a_re * h[1] + a_im * h[0] + x_ref[rs, im_l]
            out_ref[rs, re_l] = h_re
            out_ref[rs, im_l] = h_im
            return h_re, h_im

        zero = jnp.zeros((SUBLANES, w), F32)
        end_re, end_im = _unrolled_loop(local_step, (zero, zero))
        for t, k in ((TAB_BIG, 1), (TAB_BIG + 1, 2), (TAB_BIG + 2, 4)):
            p_re, p_im = tab(t)
            s_re, s_im = from_previous_segment(end_re, k, 0.0), from_previous_segment(end_im, k, 0.0)
            end_re, end_im = end_re + (p_re * s_re - p_im * s_im), end_im + (p_re * s_im + p_im * s_re)
        c0_re, c0_im = carry_re[:, n_l], carry_im[:, n_l]
        p_re, p_im = tab(TAB_SEG)
        end_re = end_re + (p_re * c0_re - p_im * c0_im)
        end_im = end_im + (p_re * c0_im + p_im * c0_re)
        carry_re[:, n_l] = jnp.broadcast_to(end_re[last:last + 1, :], end_re.shape)
        carry_im[:, n_l] = jnp.broadcast_to(end_im[last:last + 1, :], end_im.shape)
        in_re = from_previous_segment(end_re, 1, c0_re)
        in_im = from_previous_segment(end_im, 1, c0_im)

        def carry_step(k, st):
            rs = tile_rows(k)
            p_re, p_im = tab_ref[0, TAB_PW + k, :, n_l], tab_ref[1, TAB_PW + k, :, n_l]
            o_re = out_ref[rs, re_l] + (p_re * in_re - p_im * in_im)
            o_im = out_ref[rs, im_l] + (p_re * in_im + p_im * in_re)
            out_ref[rs, re_l] = o_re
            out_ref[rs, im_l] = o_im
            if pair_ref is None:
                return st
            s_re, s_im = pair_ref[rs, re_l], pair_ref[rs, im_l]
            return (o_re, o_im, st[2] + (st[0] * s_re + st[1] * s_im), st[3] + (st[1] * s_re - st[0] * s_im))

        if pair_ref is None:
            _unrolled_loop(carry_step, 0)
        else:
            fin = _unrolled_loop(carry_step, (in_re, in_im, zero, zero))
            acc_ref[:, n_l] += fin[2]
            acc_ref[:, pl.ds(NSTATE + j * w, w)] += fin[3]


def _scan_block_index(i, n_lat, ctx_first_then_ascending):
    if ctx_first_then_ascending:
        return jnp.where(i == 0, n_lat, i - 1)
    return jnp.where(i == 0, n_lat, n_lat - i)


def _full_spec(shape):
    return pl.BlockSpec(shape, lambda i: (0,) * len(shape))


_S5_BLOCKED = (S5_BLOCKS, S5_BLOCK_WIDTH, 2 * NSTATE // S5_BLOCKS)
_S5_TABLES = (2, TAB_ROWS, SUBLANES, NSTATE)
_S5_DIAG = (S5_BLOCKS, S5_GROUP, 2 * NSTATE // S5_BLOCKS)


def _s5_scan_fwd(name, ascending, z_all, bmat, cmat, tab, perm, perm_t):
    rows = z_all.shape[0]
    nb = rows // ROW_BLOCK
    n_lat = nb - 1
    bw, sw = S5_BLOCK_WIDTH, 2 * NSTATE // S5_BLOCKS

    def body(u_ref, bm_ref, cm_ref, tab_ref, p_ref, pt_ref, s_ref, y_ref, bu, yp, carry_re, carry_im):
        @pl.when(pl.program_id(0) == 0)
        def _():
            carry_re[...] = jnp.zeros_like(carry_re)
            carry_im[...] = jnp.zeros_like(carry_im)

        up = _permute_rows(p_ref, u_ref[...].astype(BF16))
        for c in range(S5_BLOCKS):
            bu[:, c * sw:(c + 1) * sw] = _dot(up[:, c * bw:(c + 1) * bw], bm_ref[c], "nn")
        _scan_chunk(bu, s_ref, tab_ref, carry_re, carry_im, ascending)
        for c in range(S5_BLOCKS):
            yp[:, c * bw:(c + 1) * bw] = _dot(s_ref[:, c * sw:(c + 1) * sw].astype(BF16), cm_ref[c], "nt")
        y_ref[...] = _unpermute_rows(pt_ref, yp[...])

    blk = lambda i: (_scan_block_index(i, n_lat, ascending), 0)
    return pl.pallas_call(
        body, name=name, grid=(nb,),
        in_specs=[pl.BlockSpec((ROW_BLOCK, S5_WIDTH), blk), _full_spec(_S5_BLOCKED), _full_spec(_S5_BLOCKED),
                  _full_spec(_S5_TABLES), _full_spec((ROW_BLOCK, ROW_BLOCK)), _full_spec((ROW_BLOCK, ROW_BLOCK))],
        out_specs=[pl.BlockSpec((ROW_BLOCK, 2 * NSTATE), blk), pl.BlockSpec((ROW_BLOCK, S5_WIDTH), blk)],
        out_shape=[_sds((rows, 2 * NSTATE)), _sds((rows, S5_WIDTH))],
        scratch_shapes=[pltpu.VMEM((ROW_BLOCK, 2 * NSTATE), F32), pltpu.VMEM((ROW_BLOCK, S5_WIDTH), F32),
                        pltpu.VMEM((SUBLANES, NSTATE), F32), pltpu.VMEM((SUBLANES, NSTATE), F32)],
        compiler_params=_params(("arbitrary",)),
    )(z_all, bmat, cmat, tab, perm, perm_t)


def _s5_scan_bwd(name, ascending, dy, z_all, states, bmat, cmat, adj, perm, perm_t):
    rows = states.shape[0]
    nb = rows // ROW_BLOCK
    n_lat = nb - 1
    bw, sw = S5_BLOCK_WIDTH, 2 * NSTATE // S5_BLOCKS

    def block_index(i):
        if ascending:
            return jnp.where(i == nb - 1, n_lat, n_lat - 1 - i)
        return jnp.where(i == nb - 1, n_lat, i)

    def body(dy_ref, u_ref, s_ref, bm_ref, cm_ref, adj_ref, p_ref, pt_ref, du_ref, db_ref, dc_ref, da_ref,
             g, dup, db_acc, dc_acc, carry_re, carry_im):
        i = pl.program_id(0)

        @pl.when(i == 0)
        def _():
            carry_re[...] = jnp.zeros_like(carry_re)
            carry_im[...] = jnp.zeros_like(carry_im)
            da_ref[...] = jnp.zeros_like(da_ref)
            db_acc[...] = jnp.zeros_like(db_acc)
            dc_acc[...] = jnp.zeros_like(dc_acc)

        has_dy = (i < nb - 1).astype(F32)
        dyp = _permute_rows(p_ref, (dy_ref[...] * has_dy).astype(BF16))
        up = _permute_rows(p_ref, u_ref[...].astype(BF16))
        for c in range(S5_BLOCKS):
            g[:, c * sw:(c + 1) * sw] = _dot(dyp[:, c * bw:(c + 1) * bw], cm_ref[c], "nn")
            dc_acc[c] += _dot(dyp[:, c * bw:(c + 1) * bw], s_ref[:, c * sw:(c + 1) * sw].astype(BF16), "tn")
            _scan_chunk(g, g, adj_ref, carry_re, carry_im, not ascending, pair_ref=s_ref, acc_ref=da_ref, lane_chunks=[c])
            gc = g[:, c * sw:(c + 1) * sw].astype(BF16)
            dup[:, c * bw:(c + 1) * bw] = _dot(gc, bm_ref[c], "nt")
            db_acc[c] += _dot(up[:, c * bw:(c + 1) * bw], gc, "tn")
        du_ref[...] = _unpermute_rows(pt_ref, dup[...])

        @pl.when(i == nb - 1)
        def _():
            mask = _block_diag_mask((bw, sw // 2))
            for acc, out in ((db_acc, db_ref), (dc_acc, dc_ref)):
                for c in range(S5_BLOCKS):
                    for part in range(2):
                        cols = slice(part * (sw // 2), (part + 1) * (sw // 2))
                        kept = jnp.where(mask, acc[c, :, cols], 0.0)
                        out[c, :, cols] = kept.reshape(bw // S5_GROUP, S5_GROUP, sw // 2).sum(axis=0)

    blk = lambda i: (block_index(i), 0)
    return pl.pallas_call(
        body, name=name, grid=(nb,),
        in_specs=[pl.BlockSpec((ROW_BLOCK, S5_WIDTH), lambda i: (jnp.minimum(block_index(i), n_lat - 1), 0)),
                  pl.BlockSpec((ROW_BLOCK, S5_WIDTH), blk), pl.BlockSpec((ROW_BLOCK, 2 * NSTATE), blk),
                  _full_spec(_S5_BLOCKED), _full_spec(_S5_BLOCKED), _full_spec(_S5_TABLES),
                  _full_spec((ROW_BLOCK, ROW_BLOCK)), _full_spec((ROW_BLOCK, ROW_BLOCK))],
        out_specs=[pl.BlockSpec((ROW_BLOCK, S5_WIDTH), blk), _full_spec(_S5_DIAG), _full_spec(_S5_DIAG),
                   _full_spec((SUBLANES, 2 * NSTATE))],
        out_shape=[_sds((rows, S5_WIDTH)), _sds(_S5_DIAG), _sds(_S5_DIAG), _sds((SUBLANES, 2 * NSTATE))],
        scratch_shapes=[pltpu.VMEM((ROW_BLOCK, 2 * NSTATE), F32), pltpu.VMEM((ROW_BLOCK, S5_WIDTH), F32),
                        pltpu.VMEM(_S5_BLOCKED, F32), pltpu.VMEM(_S5_BLOCKED, F32),
                        pltpu.VMEM((SUBLANES, NSTATE), F32), pltpu.VMEM((SUBLANES, NSTATE), F32)],
        compiler_params=_params(("arbitrary",)),
    )(dy, z_all, states, bmat, cmat, adj, perm, perm_t)


def _glu_fwd(z_all, y0, y1, d_skip, w_glu, n_rows):
    def body(u_ref, y0_ref, y1_ref, d_ref, w_ref, o_ref):
        y = d_ref[...] * u_ref[...] + y0_ref[...] + y1_ref[...]
        g = _gelu(y)
        t = _dot(g.astype(BF16), w_ref[...], "nn")
        o_ref[...] = (g * _sigmoid(t)).astype(o_ref.dtype)

    row = pl.BlockSpec((ROW_BLOCK, S5_WIDTH), lambda i: (i, 0))
    return pl.pallas_call(
        body, name="glu_fwd", grid=(n_rows // ROW_BLOCK,),
        in_specs=[row, row, row, pl.BlockSpec((1, S5_WIDTH), lambda i: (0, 0)),
                  pl.BlockSpec((S5_WIDTH, S5_WIDTH), lambda i: (0, 0))],
        out_specs=row, out_shape=_sds((n_rows, S5_WIDTH + CONV_WIDTH), BF16), compiler_params=_params(("parallel",)),
    )(z_all, y0, y1, d_skip, w_glu)


def _glu_bwd(d_ycat, z_all, y0, y1, d_skip, w_glu, n_rows):
    def body(do_ref, u_ref, y0_ref, y1_ref, d_ref, w_ref, dy_ref, dw_ref, dd_ref):
        @pl.when(pl.program_id(0) == 0)
        def _():
            dw_ref[...] = jnp.zeros_like(dw_ref)
            dd_ref[...] = jnp.zeros_like(dd_ref)

        u = u_ref[...]
        y = d_ref[...] * u + y0_ref[...] + y1_ref[...]
        g = _gelu(y)
        gb = g.astype(BF16)
        w = w_ref[...]
        sg = _sigmoid(_dot(gb, w, "nn"))
        do = do_ref[...]
        dt = do * g * sg * (1.0 - sg)
        dtb = dt.astype(BF16)
        dg = do * sg + _dot(dtb, w, "nt")
        dy = dg * _dgelu(y)
        dy_ref[...] = dy
        dw_ref[...] += _dot(gb, dtb, "tn")
        dd_ref[...] += _fold8(dy * u)

    row = pl.BlockSpec((ROW_BLOCK, S5_WIDTH), lambda i: (i, 0))
    sq = pl.BlockSpec((S5_WIDTH, S5_WIDTH), lambda i: (0, 0))
    return pl.pallas_call(
        body, name="glu_bwd", grid=(n_rows // ROW_BLOCK,),
        in_specs=[row, row, row, row, pl.BlockSpec((1, S5_WIDTH), lambda i: (0, 0)), sq],
        out_specs=[row, sq, pl.BlockSpec((SUBLANES, S5_WIDTH), lambda i: (0, 0))],
        out_shape=[_sds((n_rows, S5_WIDTH)), _sds((S5_WIDTH, S5_WIDTH)), _sds((SUBLANES, S5_WIDTH))],
        compiler_params=_params(("arbitrary",)),
    )(d_ycat, z_all, y0, y1, d_skip, w_glu)


CONV_HALF = CONV_K // 2


def _conv_block(n_rows):
    blk = min(1024, n_rows)
    assert blk >= CONV_HALF * GRID_W and n_rows % blk == 0
    return blk


def _conv_gate(z_all, n_rows):
    blk = _conv_block(n_rows)
    nb = n_rows // blk

    def body(v_ref, g_ref, o_ref):
        i = pl.program_id(0)
        inside = jnp.logical_and(i >= 1, i <= nb)

        @pl.when(inside)
        def _():
            o_ref[...] = v_ref[...] * _sigmoid(g_ref[...])

        @pl.when(jnp.logical_not(inside))
        def _():
            o_ref[...] = jnp.zeros_like(o_ref)

    src = lambda col: pl.BlockSpec((blk, CONV_WIDTH), lambda i: (jnp.clip(i - 1, 0, nb - 1), col))
    return pl.pallas_call(
        body, name="conv_gate", grid=(nb + 2,), in_specs=[src(1), src(2)],
        out_specs=pl.BlockSpec((blk, CONV_WIDTH), lambda i: (i, 0)),
        out_shape=_sds(((nb + 2) * blk, CONV_WIDTH)), compiler_params=_params(("parallel",)),
    )(z_all, z_all)


def _stream_padded(pad_ref, buf, sems, blk, n_blocks):
    i = pl.program_id(0)

    def copy(b):
        rows = pl.ds(pl.multiple_of(b * blk, blk), blk)
        return pltpu.make_async_copy(pad_ref.at[rows, :], buf.at[rows, :], sems.at[b])

    @pl.when(i == 0)
    def _():
        for b in range(n_blocks):
            copy(b).start()
        copy(0).wait()
        copy(1).wait()

    copy(i + 2).wait()
    return pl.multiple_of(i * blk, blk)


def _conv_fwd(hh_pad, w, b, ln_g, ln_b, ycat, n_rows):
    blk = _conv_block(n_rows)
    nblk = n_rows // blk + 2

    def body(hh_ref, w_ref, b_ref, g_ref, lb_ref, ycat_ref, hc_ref, y_ref, win, sems):
        base = _stream_padded(hh_ref, win, sems, blk, nblk)

        def tile(t, _):
            r0 = pl.multiple_of(t * CONV_ROWS, CONV_ROWS)
            acc = jnp.zeros((CONV_ROWS, CONV_WIDTH), F32)
            for k in range(CONV_K):
                acc = acc + w_ref[k:k + 1, :] * win[pl.ds(base + r0 + blk + (k - CONV_HALF) * GRID_W, CONV_ROWS), :]
            hc = acc + b_ref[...]
            hc_ref[pl.ds(r0, CONV_ROWS), :] = hc
            mu = jnp.mean(hc, axis=-1, keepdims=True)
            xc = hc - mu
            ln = xc * lax.rsqrt(jnp.mean(xc * xc, axis=-1, keepdims=True) + EPS_LN) * g_ref[...] + lb_ref[...]
            y_ref[pl.ds(r0, CONV_ROWS), :] = _silu(ln).astype(y_ref.dtype)
            return 0

        lax.fori_loop(0, blk // CONV_ROWS, tile, 0)

    vec = pl.BlockSpec((1, CONV_WIDTH), lambda i: (0, 0))
    row = pl.BlockSpec((blk, CONV_WIDTH), lambda i: (i, 0))
    return pl.pallas_call(
        body, name="conv_fwd", grid=(n_rows // blk,),
        in_specs=[ANY, pl.BlockSpec((CONV_K, CONV_WIDTH), lambda i: (0, 0)), vec, vec, vec, ANY],
        out_specs=[row, pl.BlockSpec((blk, CONV_WIDTH), lambda i: (i, 1))],
        out_shape=[_sds((n_rows, CONV_WIDTH)), _sds(ycat.shape, ycat.dtype)], input_output_aliases={5: 1},
        scratch_shapes=[pltpu.VMEM((nblk * blk, CONV_WIDTH), F32), pltpu.SemaphoreType.DMA((nblk,))],
        compiler_params=_params(("arbitrary",)),
    )(hh_pad, w, b, ln_g, ln_b, ycat)


def _conv_bwd_norm(d_ycat, hc, ln_g, ln_b, n_rows):
    blk = _conv_block(n_rows)
    nb = n_rows // blk

    def body(dy_ref, hc_ref, g_ref, lb_ref, o_ref, sums):
        i = pl.program_id(0)

        @pl.when(i == 0)
        def _():
            sums[...] = jnp.zeros_like(sums)

        inside = jnp.logical_and(i >= 1, i <= nb)

        @pl.when(inside)
        def _():
            hcv = hc_ref[...]
            mu = jnp.mean(hcv, axis=-1, keepdims=True)
            xc = hcv - mu
            rstd = lax.rsqrt(jnp.mean(xc * xc, axis=-1, keepdims=True) + EPS_LN)
            xh = xc * rstd
            g = g_ref[...]
            dln = dy_ref[...] * _dsilu(xh * g + lb_ref[...])
            dxh = dln * g
            dhc = rstd * (dxh - jnp.mean(dxh, axis=-1, keepdims=True) - xh * jnp.mean(dxh * xh, axis=-1, keepdims=True))
            o_ref[...] = dhc
            sums[0] += _fold8(dhc)
            sums[1] += _fold8(dln * xh)
            sums[2] += _fold8(dln)

        @pl.when(jnp.logical_not(inside))
        def _():
            o_ref[...] = jnp.zeros_like(o_ref)

    vec = pl.BlockSpec((1, CONV_WIDTH), lambda i: (0, 0))
    return pl.pallas_call(
        body, name="conv_bwd_norm", grid=(nb + 2,),
        in_specs=[pl.BlockSpec((blk, CONV_WIDTH), lambda i: (jnp.clip(i - 1, 0, nb - 1), 1)),
                  pl.BlockSpec((blk, CONV_WIDTH), lambda i: (jnp.clip(i - 1, 0, nb - 1), 0)), vec, vec],
        out_specs=[pl.BlockSpec((blk, CONV_WIDTH), lambda i: (i, 0)),
                   pl.BlockSpec((3, SUBLANES, CONV_WIDTH), lambda i: (0, 0, 0))],
        out_shape=[_sds(((nb + 2) * blk, CONV_WIDTH)), _sds((3, SUBLANES, CONV_WIDTH))],
        compiler_params=_params(("arbitrary",)),
    )(d_ycat, hc, ln_g, ln_b)


def _conv_bwd_taps(dhc_pad, hh_pad, z_all, w, n_rows):
    blk = _conv_block(n_rows)
    nblk = n_rows // blk + 2

    def body(dhc_ref, hh_ref, v_ref, g_ref, w_ref, dv_ref, dg_ref, dw_ref, dwin, hwin, dsems, hsems):
        @pl.when(pl.program_id(0) == 0)
        def _():
            dw_ref[...] = jnp.zeros_like(dw_ref)

        base = _stream_padded(dhc_ref, dwin, dsems, blk, nblk)
        _stream_padded(hh_ref, hwin, hsems, blk, nblk)

        def tile(t, _):
            r0 = pl.multiple_of(t * CONV_BWD_ROWS, CONV_BWD_ROWS) + base
            dh = dwin[pl.ds(r0 + blk, CONV_BWD_ROWS), :]
            acc = jnp.zeros((CONV_BWD_ROWS, CONV_WIDTH), F32)
            for k in range(CONV_K):
                off = (k - CONV_HALF) * GRID_W
                acc = acc + w_ref[k:k + 1, :] * dwin[pl.ds(r0 + blk - off, CONV_BWD_ROWS), :]
                dw_ref[k] += _fold8(dh * hwin[pl.ds(r0 + blk + off, CONV_BWD_ROWS), :])
            rs = pl.ds(pl.multiple_of(t * CONV_BWD_ROWS, CONV_BWD_ROWS), CONV_BWD_ROWS)
            sg = _sigmoid(g_ref[rs, :])
            vv = v_ref[rs, :]
            dv_ref[rs, :] = acc * sg
            dg_ref[rs, :] = acc * vv * sg * (1.0 - sg)
            return 0

        lax.fori_loop(0, blk // CONV_BWD_ROWS, tile, 0)

    row = pl.BlockSpec((blk, CONV_WIDTH), lambda i: (i, 0))
    return pl.pallas_call(
        body, name="conv_bwd_taps", grid=(n_rows // blk,),
        in_specs=[ANY, ANY,
            pl.BlockSpec((blk, CONV_WIDTH), lambda i: (i, 1)), pl.BlockSpec((blk, CONV_WIDTH), lambda i: (i, 2)),
            pl.BlockSpec((CONV_K, CONV_WIDTH), lambda i: (0, 0))],
        out_specs=[row, row, pl.BlockSpec((CONV_K, SUBLANES, CONV_WIDTH), lambda i: (0, 0, 0))],
        out_shape=[_sds((n_rows, CONV_WIDTH)), _sds((n_rows, CONV_WIDTH)), _sds((CONV_K, SUBLANES, CONV_WIDTH))],
        scratch_shapes=[pltpu.VMEM((nblk * blk, CONV_WIDTH), F32), pltpu.VMEM((nblk * blk, CONV_WIDTH), F32),
                        pltpu.SemaphoreType.DMA((nblk,)), pltpu.SemaphoreType.DMA((nblk,))],
        compiler_params=_params(("arbitrary",)),
    )(dhc_pad, hh_pad, z_all, z_all, w)


def _dz_assemble(du0, du1, dy, d_skip, dv, dgate, n_lat):
    rows = du0.shape[0]
    nb = rows // ROW_BLOCK

    w = S5_WIDTH

    def body(a_ref, b_ref, dy_ref, d_ref, dv_ref, dg_ref, o_ref):
        lat = pl.program_id(0) < n_lat

        @pl.when(lat)
        def _():
            o_ref[:, 0:w] = (a_ref[...] + b_ref[...] + dy_ref[...] * d_ref[...]).astype(o_ref.dtype)
            o_ref[:, w:2 * w] = dv_ref[...].astype(o_ref.dtype)
            o_ref[:, 2 * w:3 * w] = dg_ref[...].astype(o_ref.dtype)

        @pl.when(jnp.logical_not(lat))
        def _():
            o_ref[:, 0:w] = (a_ref[...] + b_ref[...]).astype(o_ref.dtype)
            o_ref[:, w:3 * w] = jnp.zeros((ROW_BLOCK, 2 * w), o_ref.dtype)

    all_rows = pl.BlockSpec((ROW_BLOCK, w), lambda i: (i, 0))
    lat_rows = pl.BlockSpec((ROW_BLOCK, w), lambda i: (jnp.minimum(i, n_lat - 1), 0))
    return pl.pallas_call(
        body, name="dz_assemble", grid=(nb,),
        in_specs=[all_rows, all_rows, lat_rows, pl.BlockSpec((1, w), lambda i: (0, 0)), lat_rows, lat_rows],
        out_specs=pl.BlockSpec((ROW_BLOCK, IN_COLS), lambda i: (i, 0)),
        out_shape=_sds((rows, IN_COLS), BF16), compiler_params=_params(("parallel",)),
    )(du0, du1, dy, d_skip, dv, dgate)


def _sum_parts(parts):
    _, r, c = parts.shape

    def body(p_ref, o_ref):
        acc = p_ref[0]
        for q in range(1, NDEV):
            acc = acc + p_ref[q]
        o_ref[...] = acc

    return pl.pallas_call(body, name="sum_parts", out_shape=_sds((r, c)), compiler_params=_params())(parts)


def _row_tile(r, c):
    best = r
    for t in (1024, 512, 256, 128, 64, 32, 16, 8):
        if r % t == 0 and t * c <= 128 * 1024:
            return t
    return best


def _adamw(name, w, gparts, m, v):
    r, c = w.shape
    np_ = gparts.shape[0]
    tr = _row_tile(r, c)

    def body(w_ref, g_ref, m_ref, v_ref, go_ref, d_ref, mo_ref, vo_ref):
        g = g_ref[0].astype(F32)
        for q in range(1, np_):
            g = g + g_ref[q].astype(F32)
        m2 = ADAM_B1 * m_ref[...] + (1.0 - ADAM_B1) * g
        v2 = ADAM_B2 * v_ref[...] + (1.0 - ADAM_B2) * jnp.square(g)
        m_hat = m2 / (1.0 - ADAM_B1 ** ADAM_STEP)
        v_hat = v2 / (1.0 - ADAM_B2 ** ADAM_STEP)
        go_ref[...] = g
        d_ref[...] = -ADAM_LR * (m_hat / (jnp.sqrt(v_hat) + ADAM_EPS) + ADAM_WD * w_ref[...])
        mo_ref[...] = m2
        vo_ref[...] = v2

    row = pl.BlockSpec((tr, c), lambda i: (i, 0))
    return pl.pallas_call(
        body, name=name, grid=(r // tr,),
        in_specs=[row, pl.BlockSpec((np_, tr, c), lambda i: (0, i, 0)), row, row],
        out_specs=[row] * 4, out_shape=[_sds((r, c))] * 4, compiler_params=_params(("parallel",)),
    )(w, gparts, m, v)


def _adamw_native(name, w, g, m, v):
    def body(w_ref, g_ref, m_ref, v_ref, d_ref, mo_ref, vo_ref):
        gv = g_ref[...]
        m2 = ADAM_B1 * m_ref[...] + (1.0 - ADAM_B1) * gv
        v2 = ADAM_B2 * v_ref[...] + (1.0 - ADAM_B2) * jnp.square(gv)
        m_hat = m2 / (1.0 - ADAM_B1 ** ADAM_STEP)
        v_hat = v2 / (1.0 - ADAM_B2 ** ADAM_STEP)
        d_ref[...] = -ADAM_LR * (m_hat / (jnp.sqrt(v_hat) + ADAM_EPS) + ADAM_WD * w_ref[...])
        mo_ref[...] = m2
        vo_ref[...] = v2

    return pl.pallas_call(body, name=name, out_shape=[_sds(w.shape)] * 3, compiler_params=_params())(w, g, m, v)


SMALL = ["c_ctx", "ada_b", "norm1_g", "s5_lam_re", "s5_lam_im", "s5_log_dt", "s5_d", "conv_b", "conv_ln_g", "conv_ln_b",
         "norm2_g", "final_g"]
SMALL_PACKED_ROWS = 24


def _pack_rows(parts, rows):
    flat = jnp.concatenate([p.reshape(-1).astype(F32) for p in parts])
    return jnp.pad(flat, (0, rows * D_MODEL - flat.shape[0])).reshape(rows, D_MODEL)


def _unpack_rows(packed, shapes):
    flat = packed.reshape(-1)
    out, off = [], 0
    for shape in shapes:
        size = 1
        for s in shape:
            size *= s
        out.append(flat[off:off + size].reshape(shape))
        off += size
    return out


def kernel(x, c, ctx, c_ctx, ada_w, ada_b, norm1_g, w_in, s5_lam_re, s5_lam_im, s5_log_dt, s5_b_re, s5_b_im, s5_c_re, s5_c_im, s5_d, s5_w_glu, conv_w, conv_b, conv_ln_g, conv_ln_b, w_out, norm2_g, mlp_w1, mlp_w2, final_g, loss_target, m_c_ctx, m_ada_w, m_ada_b, m_norm1_g, m_w_in, m_s5_lam_re, m_s5_lam_im, m_s5_log_dt, m_s5_b_re, m_s5_b_im, m_s5_c_re, m_s5_c_im, m_s5_d, m_s5_w_glu, m_conv_w, m_conv_b, m_conv_ln_g, m_conv_ln_b, m_w_out, m_norm2_g, m_mlp_w1, m_mlp_w2, m_final_g, v_c_ctx, v_ada_w, v_ada_b, v_norm1_g, v_w_in, v_s5_lam_re, v_s5_lam_im, v_s5_log_dt, v_s5_b_re, v_s5_b_im, v_s5_c_re, v_s5_c_im, v_s5_d, v_s5_w_glu, v_conv_w, v_conv_b, v_conv_ln_g, v_conv_ln_b, v_w_out, v_norm2_g, v_mlp_w1, v_mlp_w2, v_final_g):
    weights = dict(c_ctx=c_ctx, ada_w=ada_w, ada_b=ada_b, norm1_g=norm1_g, w_in=w_in, s5_lam_re=s5_lam_re, s5_lam_im=s5_lam_im, s5_log_dt=s5_log_dt, s5_b_re=s5_b_re, s5_b_im=s5_b_im, s5_c_re=s5_c_re, s5_c_im=s5_c_im, s5_d=s5_d, s5_w_glu=s5_w_glu, conv_w=conv_w, conv_b=conv_b, conv_ln_g=conv_ln_g, conv_ln_b=conv_ln_b, w_out=w_out, norm2_g=norm2_g, mlp_w1=mlp_w1, mlp_w2=mlp_w2, final_g=final_g)
    mom1 = dict(c_ctx=m_c_ctx, ada_w=m_ada_w, ada_b=m_ada_b, norm1_g=m_norm1_g, w_in=m_w_in, s5_lam_re=m_s5_lam_re, s5_lam_im=m_s5_lam_im, s5_log_dt=m_s5_log_dt, s5_b_re=m_s5_b_re, s5_b_im=m_s5_b_im, s5_c_re=m_s5_c_re, s5_c_im=m_s5_c_im, s5_d=m_s5_d, s5_w_glu=m_s5_w_glu, conv_w=m_conv_w, conv_b=m_conv_b, conv_ln_g=m_conv_ln_g, conv_ln_b=m_conv_ln_b, w_out=m_w_out, norm2_g=m_norm2_g, mlp_w1=m_mlp_w1, mlp_w2=m_mlp_w2, final_g=m_final_g)
    mom2 = dict(c_ctx=v_c_ctx, ada_w=v_ada_w, ada_b=v_ada_b, norm1_g=v_norm1_g, w_in=v_w_in, s5_lam_re=v_s5_lam_re, s5_lam_im=v_s5_lam_im, s5_log_dt=v_s5_log_dt, s5_b_re=v_s5_b_re, s5_b_im=v_s5_b_im, s5_c_re=v_s5_c_re, s5_c_im=v_s5_c_im, s5_d=v_s5_d, s5_w_glu=v_s5_w_glu, conv_w=v_conv_w, conv_b=v_conv_b, conv_ln_g=v_conv_ln_g, conv_ln_b=v_conv_ln_b, w_out=v_w_out, norm2_g=v_norm2_g, mlp_w1=v_mlp_w1, mlp_w2=v_mlp_w2, final_g=v_final_g)
    order = list(weights)

    me = 4 * lax.axis_index("x") + 2 * lax.axis_index("y") + lax.axis_index("c")
    xs, cs, tgt = x[0], ctx[0], loss_target[0]
    n_lat_rows, n_ctx_rows = xs.shape[0], cs.shape[0]
    n_rows = n_lat_rows + n_ctx_rows
    n_lat = n_lat_rows // ROW_BLOCK
    ada_cols = ada_w.shape[2]

    (c_all,), _ = _exchange("gather_c", [c], [True])
    c_all = c_all.reshape(NDEV, D_MODEL)

    cond_fwd = jnp.concatenate([c_all, c_ctx[None], jnp.zeros((7, D_MODEL), F32)])
    ada_b_loc = lax.dynamic_slice(ada_b, (0, me * ada_cols), (1, ada_cols))
    (mod_g,), mod_token = _exchange("gather_mod", [_ada_fwd(cond_fwd, ada_w[0], ada_b_loc)], [True])
    wi_send, wi_recv, wi_src, wi_land, wi_token = _exchange_start(
        "gather_w_in_start", [w_in[0].astype(BF16) + mod_token[0:1, 0:1].astype(BF16)], [True])
    mixer_w = [s5_w_glu[0].astype(BF16), conv_w[0] + wi_token[0:1, 0:1], w_out[0].astype(BF16)]
    mixer_send, mixer_recv, mixer_src, mixer_land, mixer_token = _exchange_start("gather_mixer_start", mixer_w, [True] * 3)
    mlp_w = [mlp_w1[0].astype(BF16), mlp_w2[0].astype(BF16) + mixer_token[0:1, 0:1].astype(BF16)]
    mlpw_send, mlpw_recv, mlpw_src, mlpw_land, mlpw_token = _exchange_start("gather_mlp_start", mlp_w, [True] * 2)
    mod_rows = jnp.transpose(mod_g, (1, 0, 2)).reshape(16, 6 * D_MODEL) + mlpw_token[0:1, 0:1]
    mod = lax.dynamic_slice(mod_rows, (me, 0), (1, 6 * D_MODEL)).reshape(6, D_MODEL)
    modc = mod_rows[8, :2 * D_MODEL].reshape(2, D_MODEL)
    sh1, sc1, g1, sh2, sc2, g2 = [mod[i:i + 1] for i in range(6)]

    lam_re, lam_im = s5_lam_re[0].reshape(2, 1, NSTATE), s5_lam_im[0].reshape(2, 1, NSTATE)
    ldt = jnp.repeat(s5_log_dt[0], S5_STATE, axis=-1).reshape(2, 1, NSTATE)
    bt_re = jnp.transpose(s5_b_re[0], (0, 3, 1, 2)).reshape(2, S5_GROUP, NSTATE)
    bt_im = jnp.transpose(s5_b_im[0], (0, 3, 1, 2)).reshape(2, S5_GROUP, NSTATE)
    groups_per_block = S5_GROUPS // S5_BLOCKS
    ct_re = jnp.tile(s5_c_re[0].reshape(2, S5_WIDTH, S5_STATE), (1, 1, groups_per_block))
    ct_im = jnp.tile(s5_c_im[0].reshape(2, S5_WIDTH, S5_STATE), (1, 1, groups_per_block))
    d_skip = s5_d[0].reshape(1, S5_WIDTH)
    perm = _segment_permutation()
    perm_t = perm.T
    disc = [_s5_discretise(f"s5_disc{d}", d == 0, lam_re[d], lam_im[d], ldt[d], bt_re[d], bt_im[d], ct_re[d], ct_im[d])
            for d in range(2)]

    a_all = _prenorm("prenorm1", xs, cs, norm1_g, jnp.stack([mod[0:2], modc]))
    before_w_in = a_all[0:SUBLANES, 0:LANES].astype(F32) + disc[0][0][0:SUBLANES, 0:LANES] + disc[1][0][0:SUBLANES, 0:LANES]
    wi_own, wi_landed = _exchange_wait("gather_w_in_wait", wi_send, wi_recv, wi_src, wi_land, [True], before_w_in)
    w_in_full = jnp.transpose(_with_own(wi_landed[0], wi_own[0], me), (1, 0, 2)).reshape(D_MODEL, IN_COLS)
    tm_all = 1088 if n_rows % 1088 == 0 else ROW_BLOCK
    (z_all,) = _matmul("in_proj", a_all, w_in_full, "nn", (n_rows, IN_COLS, D_MODEL), (tm_all, IN_COLS, D_MODEL),
                       [((n_rows, IN_COLS), F32)])

    states, y_dir = [], []
    for d in range(2):
        _, tab, _, bmat, cmat = disc[d]
        s, yd = _s5_scan_fwd(f"s5_scan_fwd{d}", d == 0, z_all, bmat, cmat, tab, perm, perm_t)
        states.append(s)
        y_dir.append(yd)
    mixer_own, mixer_landed = _exchange_wait("gather_mixer_wait", mixer_send, mixer_recv, mixer_src, mixer_land,
                                             [True] * 3, y_dir[1])
    glu_g, conv_w_g, w_out_g = [_with_own(l, o, me) for l, o in zip(mixer_landed, mixer_own)]
    glu_full = glu_g.reshape(S5_WIDTH, S5_WIDTH)
    conv_w_full = jnp.transpose(conv_w_g, (1, 0, 2)).reshape(CONV_K, CONV_WIDTH)
    w_out_full = w_out_g.reshape(D_MODEL, D_MODEL)
    ycat = _glu_fwd(z_all, y_dir[0], y_dir[1], d_skip, glu_full, n_lat_rows)

    hh_pad = _conv_gate(z_all, n_lat_rows)
    hc, ycat = _conv_fwd(hh_pad, conv_w_full, conv_b, conv_ln_g, conv_ln_b, ycat, n_lat_rows)

    tm = min(1024, n_lat_rows)
    tm_e = min(512, n_lat_rows)
    w1_cols = D_FF // NDEV
    row_vec = lambda tn: pl.BlockSpec((1, tn), lambda i, j, k: (0, j))
    out_tile = lambda t_m, t_n: pl.BlockSpec((t_m, t_n), lambda i, j, k: (i, j))
    full_rows = ((n_lat_rows, D_MODEL), F32)
    sums = ((n_lat_rows // tm_e, SUBLANES, D_MODEL), F32)
    sums_spec = pl.BlockSpec((None, SUBLANES, D_MODEL), lambda i, j, k: (i, 0, 0))
    vec = lambda v: (v, row_vec(D_MODEL))
    transposed_tile = lambda t_m, t_n: pl.BlockSpec((t_n, t_m), lambda i, j, k: (j, i))
    mix, h1, a2, a2_t = _matmul(
        "out_proj", ycat, w_out_full, "nn", (n_lat_rows, D_MODEL, D_MODEL), (tm_e, D_MODEL, D_MODEL),
        [full_rows, full_rows, ((n_lat_rows, D_MODEL), BF16), ((D_MODEL, n_lat_rows), BF16)],
        epi=_epi_residual_prenorm,
        epi_extra=[(xs, out_tile(tm_e, D_MODEL)), vec(g1), vec(norm2_g), vec(sc2), vec(sh2)],
        out_specs=[out_tile(tm_e, D_MODEL)] * 3 + [transposed_tile(tm_e, D_MODEL)])
    mlpw_own, mlpw_landed = _exchange_wait("gather_mlp_wait", mlpw_send, mlpw_recv, mlpw_src, mlpw_land, [True] * 2, a2)
    w1_g, w2_g = [_with_own(l, o, me) for l, o in zip(mlpw_landed, mlpw_own)]
    w2_full = w2_g.reshape(D_FF, D_MODEL)
    tm_up = min(2048, n_lat_rows)
    f, f_t = _matmul("mlp_up", a2, w1_g, "nn", (n_lat_rows, D_FF, D_MODEL), (tm_up, w1_cols, D_MODEL),
                     [((n_lat_rows, D_FF), BF16), ((D_FF, n_lat_rows), BF16)], epi=lambda acc: (acc, acc.T),
                     b_spec=pl.BlockSpec((None, D_MODEL, w1_cols), lambda i, j, k: (j, 0, 0)),
                     out_specs=[out_tile(tm_up, w1_cols), transposed_tile(tm_up, w1_cols)])
    sq_relu = lambda t: jnp.square(jnp.maximum(t, 0.0))
    mlp_out, d_h2, dm2, err_sums, d_final_g8 = _matmul(
        "mlp_down", f, w2_full, "nn", (n_lat_rows, D_MODEL, D_FF), (tm_e, D_MODEL, 2048),
        [full_rows, full_rows, ((n_lat_rows, D_MODEL), BF16), sums, sums], a_fn=sq_relu, epi=_epi_residual_loss,
        epi_extra=[(h1, out_tile(tm_e, D_MODEL)), vec(g2), (tgt, out_tile(tm_e, D_MODEL)), vec(final_g[None])],
        out_specs=[out_tile(tm_e, D_MODEL)] * 3 + [sums_spec] * 2)

    (d_f,) = _matmul("mlp_down_dx", dm2, w2_full, "nt", (n_lat_rows, D_FF, D_MODEL), (tm_up, 512, D_MODEL),
                     [((n_lat_rows, D_FF), BF16)],
                     epi=lambda acc, ft: (acc * 2.0 * jnp.maximum(ft.astype(F32), 0.0),),
                     epi_extra=[(f, out_tile(tm_up, 512))])
    tk_dw = min(2048, n_lat_rows)
    (g_w2,) = _matmul("mlp_down_dw", f_t, dm2, "nn", (D_FF, D_MODEL, n_lat_rows), (1024, D_MODEL, tk_dw),
                      [((D_FF, D_MODEL), F32)], a_fn=sq_relu)
    (g_w1,) = _matmul("mlp_up_dw", a2_t, d_f, "nn", (D_MODEL, D_FF, n_lat_rows), (D_MODEL, w1_cols, n_lat_rows),
                      [((NDEV, D_MODEL, w1_cols), F32)],
                      out_specs=[pl.BlockSpec((None, D_MODEL, w1_cols), lambda i, j, k: (j, 0, 0))])
    mlp_send, mlp_recv, mlp_src, mlp_land, mlp_token = _exchange_start(
        "scatter_mlp_start", [g_w1, g_w2.reshape(NDEV, D_FF // NDEV, D_MODEL)], [False] * 2)
    d_h1, dm1, *sums2 = _matmul(
        "mlp_up_dx", d_f, w1_g, "nt", (n_lat_rows, D_MODEL, D_FF), (tm_e, D_MODEL, 4 * w1_cols),
        [full_rows, ((n_lat_rows, D_MODEL), BF16)] + [sums] * 4, epi=_epi_norm_bwd,
        epi_extra=[(h1, out_tile(tm_e, D_MODEL)), (d_h2, out_tile(tm_e, D_MODEL)), (mlp_out, out_tile(tm_e, D_MODEL)),
                   vec(norm2_g), vec(sc2 + mlp_token[0:1, 0:1]), vec(g1)],
        b_spec=pl.BlockSpec((4, D_MODEL, w1_cols), lambda i, j, k: (k, 0, 0)), b_slabs=4,
        out_specs=[out_tile(tm_e, D_MODEL)] * 2 + [sums_spec] * 4)

    (d_ycat,) = _matmul("out_proj_dx", dm1, w_out_full, "nt", (n_lat_rows, D_MODEL, D_MODEL), (tm, D_MODEL, D_MODEL),
                        [((n_lat_rows, D_MODEL), F32)])
    (g_w_out,) = _matmul("out_proj_dw", ycat, dm1, "tn", (D_MODEL, D_MODEL, n_lat_rows), (D_MODEL, D_MODEL, 512),
                         [((D_MODEL, D_MODEL), F32)])

    dy, g_glu, dd8 = _glu_bwd(d_ycat, z_all, y_dir[0], y_dir[1], d_skip, glu_full, n_lat_rows)
    proj_send, proj_recv, proj_src, proj_land, proj_token = _exchange_start(
        "scatter_proj_start",
        [g_w_out.reshape(NDEV, D_MODEL // NDEV, D_MODEL), g_glu.reshape(NDEV, S5_WIDTH // NDEV, S5_WIDTH)], [False] * 2)
    perm = perm + proj_token[0:1, 0:1].astype(BF16)
    du, g_lam_re, g_lam_im, g_ldt, g_bt, g_cdiag = [], [], [], [], [], []
    for d in range(2):
        _, _, adj, bmat, cmat = disc[d]
        du_d, d_bdiag, d_cdiag, d_abar8 = _s5_scan_bwd(f"s5_scan_bwd{d}", d == 0, dy, z_all, states[d], bmat, cmat, adj,
                                                       perm, perm_t)
        du.append(du_d)
        d_bbar = jnp.transpose(d_bdiag.reshape(S5_BLOCKS, S5_GROUP, 2, NSTATE // S5_BLOCKS), (2, 1, 0, 3)).reshape(
            2 * S5_GROUP, NSTATE)
        d_lam8, d_bt = _s5_discretise_bwd(f"s5_disc_bwd{d}", lam_re[d], lam_im[d], ldt[d], bt_re[d], bt_im[d], d_abar8, d_bbar)
        g_lam_re.append(d_lam8[0].reshape(S5_GROUPS, S5_STATE))
        g_lam_im.append(d_lam8[1].reshape(S5_GROUPS, S5_STATE))
        g_ldt.append(d_lam8[2].reshape(S5_GROUPS, S5_STATE).sum(axis=-1))
        g_bt.append(d_bt)
        g_cdiag.append(d_cdiag)

    dhc_pad, conv_sums = _conv_bwd_norm(d_ycat, hc, conv_ln_g, conv_ln_b, n_lat_rows)
    d_v, d_gate, g_conv_w8 = _conv_bwd_taps(dhc_pad, hh_pad, z_all, conv_w_full, n_lat_rows)

    dz_all = _dz_assemble(du[0], du[1], dy, d_skip, d_v, d_gate, n_lat)
    (g_w_in_full,) = _matmul("in_proj_dw", a_all, dz_all, "tn", (D_MODEL, IN_COLS, n_rows), (D_MODEL, IN_COLS, tm_all),
                             [((D_MODEL, IN_COLS), F32)])
    g_w_in_parts = jnp.transpose(g_w_in_full.reshape(D_MODEL, NDEV, IN_COLS // NDEV), (1, 0, 2)).astype(BF16)
    win_send, win_recv, win_src, win_land, win_token = _exchange_start("scatter_w_in_start", [g_w_in_parts], [False])
    w_in_late = w_in_full + win_token[0:1, 0:1].astype(BF16)
    grad_x, *sums1 = _matmul(
        "in_proj_dx", dz_all, w_in_late, "nt", (n_lat_rows, D_MODEL, IN_COLS), (tm_e, D_MODEL, IN_COLS),
        [full_rows] + [sums] * 4, epi=_epi_norm_bwd,
        epi_extra=[(xs, out_tile(tm_e, D_MODEL)), (d_h1, out_tile(tm_e, D_MODEL)), (mix, out_tile(tm_e, D_MODEL)),
                   vec(norm1_g), vec(sc1)],
        out_specs=[out_tile(tm_e, D_MODEL)] + [sums_spec] * 4)
    (d_a_ctx,) = _matmul("in_proj_dx_ctx", dz_all, w_in_late, "nt", (n_ctx_rows, D_MODEL, IN_COLS),
                         (ROW_BLOCK, D_MODEL, IN_COLS), [((n_ctx_rows, D_MODEL), F32)],
                         a_spec=pl.BlockSpec((ROW_BLOCK, IN_COLS), lambda i, j, k: (i + n_lat, 0)))
    (sums1c,) = _norm_bwd("norm1_bwd_ctx", cs, d_a_ctx, 0, norm1_g, modc[1:2])

    s1, s1c, s2 = [p.sum(axis=(0, 1)) for p in sums1], sums1c.sum(axis=1), [p.sum(axis=(0, 1)) for p in sums2]
    d_mod = jnp.concatenate([s1[0], s1[1], s1[3], s2[0], s2[1], s2[3]])
    d_modc = jnp.concatenate([s1c[0], s1c[1], jnp.zeros((4 * D_MODEL,), F32)])
    (dmod_g,), _ = _exchange("gather_dmod", [jnp.stack([d_mod, d_modc])], [True])
    dmod16 = jnp.concatenate([dmod_g[:, 0], dmod_g[:, 1]])
    dmod16_loc = lax.dynamic_slice(dmod16, (0, me * ada_cols), (16, ada_cols))
    cond_bwd = jnp.concatenate([c_all, jnp.broadcast_to(c_ctx[None], (NDEV, D_MODEL))])
    g_ada_w, g_c_ctx8 = _ada_bwd(cond_bwd, dmod16_loc, ada_w[0], c_ctx[None])

    small_parts = dict(
        c_ctx=g_c_ctx8[0], ada_b=d_mod + d_modc, norm1_g=s1[2] + s1c[2],
        s5_lam_re=jnp.stack(g_lam_re), s5_lam_im=jnp.stack(g_lam_im), s5_log_dt=jnp.stack(g_ldt),
        s5_d=dd8.sum(axis=0), conv_b=conv_sums[0].sum(axis=0), conv_ln_g=conv_sums[1].sum(axis=0),
        conv_ln_b=conv_sums[2].sum(axis=0), norm2_g=s2[2], final_g=d_final_g8.sum(axis=(0, 1)))
    reduced_shapes = [(SMALL_PACKED_ROWS, D_MODEL), (2, 2 * S5_GROUP, NSTATE), (2,) + _S5_DIAG, (1,)]
    small_g = _pack_rows(
        [_pack_rows([small_parts[n] for n in SMALL], SMALL_PACKED_ROWS), jnp.stack(g_bt), jnp.stack(g_cdiag),
         (0.5 / D_MODEL * jnp.sum(err_sums)).reshape(1)], SMALL_ROWS).reshape(NDEV, SMALL_ROWS // NDEV, D_MODEL)
    g_conv_w_parts = jnp.transpose(g_conv_w8.sum(axis=1).reshape(CONV_K, NDEV, CONV_WIDTH // NDEV), (1, 0, 2))

    res = {}

    def own_chunk(src):
        return lax.dynamic_index_in_dim(src, me, 0, keepdims=False)

    def adamw_big(name, parts):
        outs = _adamw("adamw_" + name, weights[name][0], parts, mom1[name][0], mom2[name][0])
        res[name] = [o[None] for o in outs]
        return outs[0]

    sm_send, sm_recv, sm_src, sm_land, sm_token = _exchange_start("scatter_small_start", [g_conv_w_parts, small_g],
                                                                  [False] * 2)
    done = adamw_big("ada_w", g_ada_w[None] + sm_token[0:1, 0:1])
    mlp_src, mlp_landed = _exchange_wait("scatter_mlp_wait", mlp_send, mlp_recv, mlp_src, mlp_land, [False] * 2, done)
    p_w1, p_w2 = [_with_own(l, own_chunk(s), me) for l, s in zip(mlp_landed, mlp_src)]
    adamw_big("mlp_w1", p_w1)
    done = adamw_big("mlp_w2", p_w2)
    sm_src, sm_landed = _exchange_wait("scatter_small_wait", sm_send, sm_recv, sm_src, sm_land, [False] * 2, done)
    p_conv_w, p_small = [_with_own(l, own_chunk(s), me) for l, s in zip(sm_landed, sm_src)]
    ga_send, ga_recv, ga_src, ga_land, ga_token = _exchange_start("gather_small_start", [_sum_parts(p_small)], [True])
    proj_src, proj_landed = _exchange_wait("scatter_proj_wait", proj_send, proj_recv, proj_src, proj_land, [False] * 2,
                                           ga_token)
    p_w_out, p_glu = [_with_own(l, own_chunk(s), me) for l, s in zip(proj_landed, proj_src)]
    adamw_big("w_out", p_w_out)
    done = adamw_big("s5_w_glu", p_glu)
    win_src, win_landed = _exchange_wait("scatter_w_in_wait", win_send, win_recv, win_src, win_land, [False], done)
    adamw_big("w_in", _with_own(win_landed[0], own_chunk(win_src[0]), me))
    done = adamw_big("conv_w", p_conv_w)
    ga_own, ga_landed = _exchange_wait("gather_small_wait", ga_send, ga_recv, ga_src, ga_land, [True], done)
    small_all = _with_own(ga_landed[0], ga_own[0], me).reshape(1, SMALL_ROWS, D_MODEL)
    _, r_bt, r_cdiag, loss = _unpack_rows(small_all, reduced_shapes)
    loss = loss.reshape(())
    pack = lambda src: _pack_rows([src[n] for n in SMALL], SMALL_PACKED_ROWS)
    outs = _adamw("adamw_small", pack(weights), small_all, pack(mom1), pack(mom2))
    unpacked = [_unpack_rows(o, [weights[n].shape for n in SMALL]) for o in outs]
    for i, name in enumerate(SMALL):
        res[name] = [u[i] for u in unpacked]
    to_ghp = lambda t: jnp.transpose(t.reshape(2, S5_GROUP, S5_GROUPS, S5_STATE), (0, 2, 1, 3))[None]
    r_c = jnp.transpose(r_cdiag.reshape(2, S5_BLOCKS, S5_GROUP, 2, groups_per_block, S5_STATE), (3, 0, 1, 4, 2, 5)).reshape(
        2, 1, 2, S5_GROUPS, S5_GROUP, S5_STATE)
    swap = lambda t: jnp.swapaxes(t, -1, -2)
    for name, grad in (("s5_b_re", to_ghp(r_bt[:, :S5_GROUP])), ("s5_b_im", to_ghp(r_bt[:, S5_GROUP:]))):
        outs = _adamw_native("adamw_" + name, swap(weights[name]), grad, swap(mom1[name]), swap(mom2[name]))
        res[name] = [swap(grad), *[swap(o) for o in outs]]
    for name, grad in (("s5_c_re", r_c[0]), ("s5_c_im", -r_c[1])):
        res[name] = [grad, *_adamw_native("adamw_" + name, weights[name], grad, mom1[name], mom2[name])]

    return (loss, grad_x[None], *[res[n][0] for n in order], *[res[n][1] for n in order],
            *[res[n][2] for n in order], *[res[n][3] for n in order])
```

```python
import jax
import jax.numpy as jnp
from jax import lax
from jax.experimental import pallas as pl
from jax.experimental.pallas import tpu as pltpu

F32 = jnp.float32
BF16 = jnp.bfloat16
MESH = pl.DeviceIdType.MESH
ANY = pl.BlockSpec(memory_space=pl.ANY)

NDEV = 8
D_MODEL = 1024
GRID_W = 64
S5_WIDTH = 512
S5_GROUP = 16
S5_GROUPS = 32
S5_STATE = 64
NSTATE = S5_GROUPS * S5_STATE
CONV_WIDTH = 512
CONV_K = 31
IN_COLS = S5_WIDTH + 2 * CONV_WIDTH
D_FF = 4 * D_MODEL
EPS_RMS = 1e-6
EPS_LN = 1e-5
ADAM_LR = 0.001
ADAM_B1 = 0.9
ADAM_B2 = 0.999
ADAM_EPS = 1e-08
ADAM_WD = 0.01
ADAM_STEP = 10

SUBLANES = 8
LANES = 128
ROW_BLOCK = 256
SCAN_LANES = 512
SCAN_UNROLL = 32
SEGMENTS = SUBLANES
STEPS = ROW_BLOCK // SEGMENTS
S5_BLOCKS = 4
S5_BLOCK_WIDTH = S5_WIDTH // S5_BLOCKS
CONV_ROWS = 64
CONV_BWD_ROWS = 32
VMEM_LIMIT = 48 * 1024 * 1024
SMALL_ROWS = 320


def _params(sem=None):
    kw = dict(vmem_limit_bytes=VMEM_LIMIT)
    if sem is not None:
        kw["dimension_semantics"] = sem
    return pltpu.CompilerParams(**kw)


def _sds(shape, dtype=F32):
    return jax.ShapeDtypeStruct(tuple(shape), dtype)


def _fold8(x):
    return x.reshape(x.shape[0] // SUBLANES, SUBLANES, x.shape[1]).sum(axis=0)


def _sigmoid(x):
    return 1.0 / (1.0 + jnp.exp(-x))


def _silu(x):
    return x * _sigmoid(x)


def _dsilu(x):
    s = _sigmoid(x)
    return s * (1.0 + x * (1.0 - s))


_GELU_C = 0.7978845608028654


def _gelu(x):
    return 0.5 * x * (1.0 + jnp.tanh(_GELU_C * (x + 0.044715 * x * x * x)))


def _dgelu(x):
    t = jnp.tanh(_GELU_C * (x + 0.044715 * x * x * x))
    return 0.5 * (1.0 + t) + 0.5 * x * (1.0 - t * t) * _GELU_C * (1.0 + 3.0 * 0.044715 * x * x)


def _rms(x):
    rstd = lax.rsqrt(jnp.mean(x * x, axis=-1, keepdims=True) + EPS_RMS)
    return x * rstd, rstd


def _epi_residual_prenorm(acc, res, gate, gain, scale, shift):
    h = res + gate * acc
    xh, _ = _rms(h)
    a = (xh * gain) * (1.0 + scale) + shift
    return acc, h, a, a.T


def _epi_residual_loss(acc, res, gate, target, gain):
    h = res + gate * acc
    xh, rstd = _rms(h)
    err = xh * gain - target
    dy = err * (1.0 / h.shape[-1])
    dxh = dy * gain
    dh = rstd * (dxh - xh * jnp.mean(dxh * xh, axis=-1, keepdims=True))
    return acc, dh, dh * gate, _fold8(err * err), _fold8(dy * xh)


def _epi_norm_bwd(d_act, x, res, aux, gain, scale, gate=None):
    xh, rstd = _rms(x)
    dn = d_act * (1.0 + scale)
    dxh = dn * gain
    dx = res + rstd * (dxh - xh * jnp.mean(dxh * xh, axis=-1, keepdims=True))
    sums = (_fold8(d_act), _fold8(d_act * (xh * gain)), _fold8(dn * xh), _fold8(res * aux))
    return (dx, *sums) if gate is None else (dx, dx * gate, *sums)


def _dot(a, b, mode):
    dims = {"nn": (((1,), (0,)), ((), ())), "nt": (((1,), (1,)), ((), ())), "tn": (((0,), (0,)), ((), ()))}[mode]
    return lax.dot_general(a, b, dims, preferred_element_type=F32)


def _peers(x, y, c):
    out = []
    for k in range(1, NDEV):
        px = 1 - x if k & 4 else x
        py = 1 - y if k & 2 else y
        pc = 1 - c if k & 1 else c
        out.append(((px, py, pc), 4 * px + 2 * py + pc))
    return out


def _exchange_copies(src, land, send_sems, recv_sems, gather):
    x, y, c = lax.axis_index("x"), lax.axis_index("y"), lax.axis_index("c")
    me = 4 * x + 2 * y + c
    out = []
    for a in range(len(src)):
        for k, (peer, plin) in enumerate(_peers(x, y, c)):
            chunk = src[a] if gather[a] else src[a].at[plin]
            sems = dict(send_sem=send_sems.at[a * (NDEV - 1) + k], recv_sem=recv_sems.at[a * (NDEV - 1) + k],
                        device_id=peer, device_id_type=MESH)
            out.append((pltpu.make_async_remote_copy(src_ref=chunk, dst_ref=land[a].at[me], **sems),
                        pltpu.make_async_remote_copy(src_ref=chunk, dst_ref=land[a].at[plin], **sems)))
    return out


def _exchange(name, srcs, gather):
    n = len(srcs)
    outs = [_sds(((NDEV,) + s.shape) if g else s.shape, s.dtype) for s, g in zip(srcs, gather)]

    def body(*refs):
        src, dst, token = refs[:n], refs[n:2 * n], refs[2 * n]
        send_sems, recv_sems, local_sems = refs[2 * n + 1:]
        me = 4 * lax.axis_index("x") + 2 * lax.axis_index("y") + lax.axis_index("c")
        local = [pltpu.make_async_copy(src[a] if gather[a] else src[a].at[me], dst[a].at[me], local_sems.at[a])
                 for a in range(n)]
        for copy in local:
            copy.start()
        copies = _exchange_copies(src, dst, send_sems, recv_sems, gather)
        for copy, _ in copies:
            copy.start()
        token[...] = jnp.zeros_like(token)
        for copy, landing in copies:
            copy.wait_send()
            landing.wait_recv()
        for copy in local:
            copy.wait()

    nsem = n * (NDEV - 1)
    out = pl.pallas_call(
        body, name=name, out_shape=outs + [_sds((SUBLANES, LANES))], in_specs=[ANY] * n,
        out_specs=[ANY] * n + [pl.BlockSpec(memory_space=pltpu.VMEM)],
        scratch_shapes=[pltpu.SemaphoreType.DMA((nsem,)), pltpu.SemaphoreType.DMA((nsem,)), pltpu.SemaphoreType.DMA((n,))],
    )(*srcs)
    return out[:n], out[n]


HBM = pl.BlockSpec(memory_space=pltpu.HBM)
SEM = pl.BlockSpec(memory_space=pltpu.SEMAPHORE)
EFFECT = pltpu.SideEffectType.DATAFLOW_SIDE_EFFECTING


def _exchange_start_groups(name, groups):
    srcs = [s for g_srcs, _ in groups for s in g_srcs]
    gathers = [g for _, g_gather in groups for g in g_gather]
    lands = [lax.empty(((NDEV,) + s.shape) if g else s.shape, s.dtype) for s, g in zip(srcs, gathers)]
    n, ng = len(srcs), len(groups)

    def body(*refs):
        src, land = refs[:n], refs[n:2 * n]
        sems = refs[2 * n:2 * n + 2 * ng]
        token = refs[-1]
        first = 0
        for g, (g_srcs, g_gather) in enumerate(groups):
            last = first + len(g_srcs)
            for copy, _ in _exchange_copies(src[first:last], land[first:last], sems[2 * g], sems[2 * g + 1], g_gather):
                copy.start()
            first = last
        token[...] = jnp.zeros_like(token)

    hbm = lambda v: pltpu.HBM(v.shape, v.dtype)
    sem_shapes = []
    for g_srcs, _ in groups:
        sem_shapes += [pltpu.SemaphoreType.DMA((len(g_srcs) * (NDEV - 1),))] * 2
    out = pl.pallas_call(
        body, name=name,
        out_shape=(*sem_shapes, *[hbm(v) for v in srcs], *[hbm(v) for v in lands], _sds((SUBLANES, LANES))),
        in_specs=[HBM] * (2 * n),
        out_specs=(*([SEM] * (2 * ng)), *([HBM] * (2 * n)), pl.BlockSpec(memory_space=pltpu.VMEM)),
        input_output_aliases={i: 2 * ng + i for i in range(2 * n)},
        compiler_params=pltpu.CompilerParams(has_side_effects=EFFECT),
    )(*[pltpu.with_memory_space_constraint(v, pltpu.HBM) for v in srcs + lands])
    src_out, land_out = out[2 * ng:2 * ng + n], out[2 * ng + n:2 * ng + 2 * n]
    result, first = [], 0
    for g, (g_srcs, _) in enumerate(groups):
        last = first + len(g_srcs)
        result.append((out[2 * g], out[2 * g + 1], src_out[first:last], land_out[first:last]))
        first = last
    return result, out[-1]


def _exchange_start(name, srcs, gather):
    (group,), token = _exchange_start_groups(name, [(srcs, gather)])
    return (*group, token)


def _exchange_wait(name, send_sems, recv_sems, srcs, lands, gather, after):
    n = len(srcs)

    def body(*refs):
        src, land = refs[:n], refs[n:2 * n]
        send_ref, recv_ref = refs[2 * n], refs[2 * n + 1]
        for copy, landing in _exchange_copies(src, land, send_ref, recv_ref, gather):
            copy.wait_send()
            landing.wait_recv()

    hbm = lambda v: pltpu.HBM(v.shape, v.dtype)
    out = pl.pallas_call(
        body, name=name, out_shape=[hbm(v) for v in list(srcs) + list(lands)],
        in_specs=[HBM] * (2 * n) + [SEM, SEM, ANY], out_specs=[HBM] * (2 * n),
        input_output_aliases={i: i for i in range(2 * n)},
        compiler_params=pltpu.CompilerParams(has_side_effects=EFFECT),
    )(*srcs, *lands, send_sems, recv_sems, after)
    return out[:n], out[n:]


def _with_own(landed, own, me):
    return lax.dynamic_update_slice(landed, own[None], (me,) + (0,) * own.ndim)


def _matmul(name, a, b, mode, mnk, tiles, outs, a_spec=None, b_spec=None, a_fn=None, a_extra=(),
            epi=None, epi_extra=(), out_specs=None, b_slabs=1):
    m_, n_, k_ = mnk
    tm, tn, tk = tiles
    nk = k_ // tk
    if a_spec is None:
        a_spec = (pl.BlockSpec((tk, tm), lambda i, j, k: (k, i)) if mode == "tn"
                  else pl.BlockSpec((tm, tk), lambda i, j, k: (i, k)))
    if b_spec is None:
        b_spec = (pl.BlockSpec((tn, tk), lambda i, j, k: (j, k)) if mode == "nt"
                  else pl.BlockSpec((tk, tn), lambda i, j, k: (k, j)))
    if out_specs is None:
        out_specs = [pl.BlockSpec((tm, tn), lambda i, j, k: (i, j)) for _ in outs]
    na, ne, no = len(a_extra), len(epi_extra), len(outs)

    def body(*refs):
        a_ref, b_ref = refs[0], refs[1]
        ax = refs[2:2 + na]
        ex = refs[2 + na:2 + na + ne]
        o = refs[2 + na + ne:2 + na + ne + no]

        def finish(res):
            res = epi(res, *[r[...] for r in ex]) if epi is not None else (res,)
            for ref, val in zip(o, res):
                ref[...] = val.astype(ref.dtype)

        at = a_ref[...]
        if a_fn is not None:
            at = a_fn(at, *[r[...] for r in ax])
        at = at.astype(BF16)
        if b_slabs == 1:
            part = _dot(at, b_ref[...].astype(BF16), mode)
        else:
            ks = tk // b_slabs
            part = _dot(at[:, 0:ks], b_ref[0].astype(BF16), mode)
            for s in range(1, b_slabs):
                part = part + _dot(at[:, s * ks:(s + 1) * ks], b_ref[s].astype(BF16), mode)
        if nk == 1:
            finish(part)
            return
        acc = refs[-1]
        k = pl.program_id(2)

        @pl.when(k == 0)
        def _():
            acc[...] = part

        @pl.when(k > 0)
        def _():
            acc[...] += part

        @pl.when(k == nk - 1)
        def _():
            finish(acc[...])

    return pl.pallas_call(
        body, name=name, grid=(m_ // tm, n_ // tn, nk),
        in_specs=[a_spec, b_spec] + [s for _, s in a_extra] + [s for _, s in epi_extra],
        out_specs=out_specs, out_shape=[_sds(s, d) for s, d in outs],
        scratch_shapes=[pltpu.VMEM((tm, tn), F32)] if nk > 1 else [],
        compiler_params=_params(("parallel", "parallel", "arbitrary")),
    )(a, b, *[x for x, _ in a_extra], *[x for x, _ in epi_extra])


def _prenorm(name, x, ctx, gain, shsc):
    n_lat = x.shape[0] // ROW_BLOCK
    n_ctx = 0 if ctx is None else ctx.shape[0] // ROW_BLOCK
    d = x.shape[1]

    def norm(src, g_ref, m_ref, o_ref):
        xv = src[...]
        xh = xv * lax.rsqrt(jnp.mean(xv * xv, axis=-1, keepdims=True) + EPS_RMS)
        o_ref[...] = ((xh * g_ref[...]) * (1.0 + m_ref[1:2, :]) + m_ref[0:1, :]).astype(o_ref.dtype)

    def body(*refs):
        if ctx is None:
            x_ref, g_ref, m_ref, o_ref = refs
            norm(x_ref, g_ref, m_ref, o_ref)
        else:
            x_ref, c_ref, g_ref, m_ref, o_ref = refs
            i = pl.program_id(0)

            @pl.when(i < n_lat)
            def _():
                norm(x_ref, g_ref, m_ref, o_ref)

            @pl.when(i >= n_lat)
            def _():
                norm(c_ref, g_ref, m_ref, o_ref)

    in_specs = [pl.BlockSpec((ROW_BLOCK, d), lambda i: (jnp.minimum(i, n_lat - 1), 0))]
    args = [x]
    if ctx is not None:
        in_specs.append(pl.BlockSpec((ROW_BLOCK, d), lambda i: (jnp.maximum(i - n_lat, 0), 0)))
        args.append(ctx)
    in_specs += [pl.BlockSpec((1, d), lambda i: (0, 0)),
                 pl.BlockSpec((None, 2, d), lambda i: (jnp.minimum(i // n_lat, 1), 0, 0))]
    args += [gain, shsc]
    return pl.pallas_call(
        body, name=name, grid=(n_lat + n_ctx,), in_specs=in_specs,
        out_specs=pl.BlockSpec((ROW_BLOCK, d), lambda i: (i, 0)),
        out_shape=_sds(((n_lat + n_ctx) * ROW_BLOCK, d), BF16),
        compiler_params=_params(("parallel",)),
    )(*args)


def _norm_bwd(name, x, d_act, d_act_row0, gain, scale, res=None, aux=None):
    rows, d = x.shape
    nb = rows // ROW_BLOCK
    has_res = res is not None

    def body(*refs):
        if has_res:
            x_ref, da_ref, g_ref, sc_ref, r_ref, aux_ref, dx_ref, sums = refs
        else:
            x_ref, da_ref, g_ref, sc_ref, sums = refs
        i = pl.program_id(0)

        @pl.when(i == 0)
        def _():
            sums[...] = jnp.zeros_like(sums)

        xv, da = x_ref[...], da_ref[...]
        rstd = lax.rsqrt(jnp.mean(xv * xv, axis=-1, keepdims=True) + EPS_RMS)
        xh = xv * rstd
        g = g_ref[...]
        dn = da * (1.0 + sc_ref[...])
        sums[0] += _fold8(da)
        sums[1] += _fold8(da * (xh * g))
        sums[2] += _fold8(dn * xh)
        if has_res:
            dxh = dn * g
            dx = rstd * (dxh - xh * jnp.mean(dxh * xh, axis=-1, keepdims=True))
            rv = r_ref[...]
            dx_ref[...] = rv + dx
            sums[3] += _fold8(rv * aux_ref[...])

    row = lambda i: (i, 0)
    vec = pl.BlockSpec((1, d), lambda i: (0, 0))
    in_specs = [pl.BlockSpec((ROW_BLOCK, d), row), pl.BlockSpec((ROW_BLOCK, d), lambda i: (i + d_act_row0, 0)), vec, vec]
    args = [x, d_act, gain, scale]
    out_shape = [_sds((4, SUBLANES, d))]
    out_specs = [pl.BlockSpec((4, SUBLANES, d), lambda i: (0, 0, 0))]
    if has_res:
        in_specs += [pl.BlockSpec((ROW_BLOCK, d), row), pl.BlockSpec((ROW_BLOCK, d), row)]
        args += [res, aux]
        out_shape = [_sds((rows, d))] + out_shape
        out_specs = [pl.BlockSpec((ROW_BLOCK, d), row)] + out_specs
    return pl.pallas_call(
        body, name=name, grid=(nb,), in_specs=in_specs, out_specs=out_specs, out_shape=out_shape,
        compiler_params=_params(("arbitrary",)),
    )(*args)


def _ada_fwd(cond16, ada_w_loc, ada_b_loc):
    cols = ada_w_loc.shape[1]

    def body(c_ref, w_ref, b_ref, o_ref):
        s = _silu(c_ref[...]).astype(BF16)
        o_ref[...] = _dot(s, w_ref[...].astype(BF16), "nn") + b_ref[...]

    return pl.pallas_call(body, name="ada_fwd", out_shape=_sds((16, cols)), compiler_params=_params())(
        cond16, ada_w_loc, ada_b_loc)


def _ada_bwd(cond16, dmod16, ada_w_loc, c_ctx_row):
    k_, cols = ada_w_loc.shape

    def body(c_ref, dm_ref, w_ref, cc_ref, gw_ref, gc_ref):
        s = _silu(c_ref[...]).astype(BF16)
        dm = dm_ref[...]
        gw_ref[...] = _dot(s, dm.astype(BF16), "tn")
        dmc = jnp.sum(dm[8:16, :], axis=0, keepdims=True)
        dmc8 = jnp.broadcast_to(dmc, (SUBLANES, cols)).astype(BF16)
        ds = _dot(dmc8, w_ref[...].astype(BF16), "nt")
        row = lax.broadcasted_iota(jnp.int32, ds.shape, 0)
        gc_ref[...] = jnp.where(row == 0, ds * _dsilu(cc_ref[...]), 0.0)

    return pl.pallas_call(body, name="ada_bwd", out_shape=[_sds((k_, cols)), _sds((SUBLANES, k_))],
                          compiler_params=_params())(cond16, dmod16, ada_w_loc, c_ctx_row)


def _cmul(a, b):
    return a[0] * b[0] - a[1] * b[1], a[0] * b[1] + a[1] * b[0]


def _disc(lam_re, lam_im, ldt):
    dt = jnp.exp(ldt)
    mag = jnp.exp(lam_re * dt)
    th = lam_im * dt
    a_re, a_im = mag * jnp.cos(th), mag * jnp.sin(th)
    den = lam_re * lam_re + lam_im * lam_im
    n_re = a_re - 1.0
    f_re = (n_re * lam_re + a_im * lam_im) / den
    f_im = (a_im * lam_re - n_re * lam_im) / den
    return dt, mag, th, a_re, a_im, den, n_re, f_re, f_im


def _block_diag_mask(shape):
    row = lax.broadcasted_iota(jnp.int32, shape, 0)
    col = lax.broadcasted_iota(jnp.int32, shape, 1)
    return lax.shift_right_logical(row, 4) == lax.shift_right_logical(col, 6)


TAB_A = 0
TAB_BIG = 1
TAB_SEG = 4
TAB_PW = 5
TAB_ROWS = TAB_PW + STEPS


def _s5_discretise(name, ascending, lam_re, lam_im, ldt, bt_re, bt_im, ct_re, ct_im):
    def write_tables(ref, pw, big, asc, sign):
        row = lax.broadcasted_iota(jnp.int32, (SUBLANES, NSTATE), 0)
        full = lambda v: jnp.broadcast_to(v, (SUBLANES, NSTATE))

        def put(t, p):
            ref[0, t] = full(p[0])
            ref[1, t] = full(sign * p[1])

        put(TAB_A, pw[0])
        for t in range(3):
            put(TAB_BIG + t, big[t])
        seg = [big[0]]
        for _ in range(SEGMENTS - 1):
            seg.append(_cmul(seg[-1], big[0]))
        seg_re = jnp.zeros((SUBLANES, NSTATE), F32)
        seg_im = jnp.zeros((SUBLANES, NSTATE), F32)
        for r in range(SEGMENTS):
            p = seg[r] if asc else seg[SEGMENTS - 1 - r]
            seg_re = jnp.where(row == r, p[0], seg_re)
            seg_im = jnp.where(row == r, sign * p[1], seg_im)
        ref[0, TAB_SEG] = seg_re
        ref[1, TAB_SEG] = seg_im
        for k in range(STEPS):
            put(TAB_PW + k, pw[k])

    def body(lr_ref, li_ref, ldt_ref, br_ref, bi_ref, cr_ref, ci_ref, bb_ref, tab_ref, adj_ref, bm_ref, cm_ref):
        _, _, _, a_re, a_im, _, _, f_re, f_im = _disc(lr_ref[...], li_ref[...], ldt_ref[...])
        bre, bim = br_ref[...], bi_ref[...]
        bb_re = f_re * bre - f_im * bim
        bb_im = f_re * bim + f_im * bre
        bb_ref[0:S5_GROUP, :] = bb_re
        bb_ref[S5_GROUP:2 * S5_GROUP, :] = bb_im
        pw = [(a_re, a_im)]
        for _ in range(STEPS - 1):
            pw.append(_cmul(pw[-1], (a_re, a_im)))
        big = [pw[STEPS - 1]]
        for _ in range(2):
            big.append(_cmul(big[-1], big[-1]))
        write_tables(tab_ref, pw, big, ascending, 1.0)
        write_tables(adj_ref, pw, big, not ascending, -1.0)
        half = NSTATE // S5_BLOCKS
        mask = _block_diag_mask((S5_BLOCK_WIDTH, half))
        tile = lambda v: jnp.broadcast_to(v[None], (S5_BLOCK_WIDTH // S5_GROUP, S5_GROUP, half)).reshape(S5_BLOCK_WIDTH, half)
        for c in range(S5_BLOCKS):
            cols = slice(c * half, (c + 1) * half)
            rows = slice(c * S5_BLOCK_WIDTH, (c + 1) * S5_BLOCK_WIDTH)
            bm_ref[c, :, 0:half] = jnp.where(mask, tile(bb_re[:, cols]), 0.0).astype(BF16)
            bm_ref[c, :, half:2 * half] = jnp.where(mask, tile(bb_im[:, cols]), 0.0).astype(BF16)
            cm_ref[c, :, 0:half] = jnp.where(mask, cr_ref[rows, :], 0.0).astype(BF16)
            cm_ref[c, :, half:2 * half] = jnp.where(mask, -ci_ref[rows, :], 0.0).astype(BF16)

    blocked = _sds((S5_BLOCKS, S5_BLOCK_WIDTH, 2 * NSTATE // S5_BLOCKS), BF16)
    return pl.pallas_call(
        body, name=name,
        out_shape=[_sds((2 * S5_GROUP, NSTATE)), _sds((2, TAB_ROWS, SUBLANES, NSTATE)),
                   _sds((2, TAB_ROWS, SUBLANES, NSTATE)), blocked, blocked],
        compiler_params=_params(),
    )(lam_re, lam_im, ldt, bt_re, bt_im, ct_re, ct_im)


def _s5_discretise_bwd(name, lam_re, lam_im, ldt, bt_re, bt_im, d_abar8, d_bbar):
    def body(lr_ref, li_ref, ldt_ref, br_ref, bi_ref, da_ref, db_ref, dl_ref, dbt_ref):
        lam_re, lam_im = lr_ref[...], li_ref[...]
        dt, mag, _, a_re, a_im, den, n_re, f_re, f_im = _disc(lam_re, lam_im, ldt_ref[...])
        bre, bim = br_ref[...], bi_ref[...]
        dbr, dbi = db_ref[0:S5_GROUP, :], db_ref[S5_GROUP:2 * S5_GROUP, :]
        dbt_ref[0:S5_GROUP, :] = f_re * dbr + f_im * dbi
        dbt_ref[S5_GROUP:2 * S5_GROUP, :] = f_re * dbi - f_im * dbr
        df_re = jnp.sum(bre * dbr + bim * dbi, axis=0, keepdims=True)
        df_im = jnp.sum(bre * dbi - bim * dbr, axis=0, keepdims=True)
        da = da_ref[...]
        da_re = jnp.sum(da[:, 0:NSTATE], axis=0, keepdims=True)
        da_im = jnp.sum(da[:, NSTATE:2 * NSTATE], axis=0, keepdims=True)
        da_re = da_re + (df_re * lam_re - df_im * lam_im) / den
        da_im = da_im + (df_re * lam_im + df_im * lam_re) / den
        ff = (f_re * df_re + f_im * df_im) * 2.0 / den
        d_lr = (df_re * n_re + df_im * a_im) / den - ff * lam_re
        d_li = (df_re * a_im - df_im * n_re) / den - ff * lam_im
        d_mag = (da_re * a_re + da_im * a_im) / mag
        d_th = da_im * a_re - da_re * a_im
        d_lr = d_lr + d_mag * mag * dt
        d_li = d_li + d_th * dt
        d_ldt = (d_mag * mag * lam_re + d_th * lam_im) * dt
        row = lax.broadcasted_iota(jnp.int32, (SUBLANES, NSTATE), 0)
        dl_ref[...] = jnp.where(row == 0, d_lr, jnp.where(row == 1, d_li, jnp.where(row == 2, d_ldt, 0.0)))

    return pl.pallas_call(
        body, name=name, out_shape=[_sds((SUBLANES, NSTATE)), _sds((2 * S5_GROUP, NSTATE))],
        compiler_params=_params(),
    )(lam_re, lam_im, ldt, bt_re, bt_im, d_abar8, d_bbar)


def _segment_permutation():
    rho = jnp.arange(ROW_BLOCK)
    src = STEPS * (rho % SEGMENTS) + rho // SEGMENTS
    return (src[:, None] == jnp.arange(ROW_BLOCK)[None, :]).astype(BF16)


def _permute_rows(perm_ref, v):
    return _dot(perm_ref[...], v, "nn").astype(BF16)


def _unpermute_rows(perm_t_ref, v):
    hi = v.astype(BF16)
    lo = (v - hi.astype(F32)).astype(BF16)
    return _dot(perm_t_ref[...], hi, "nn") + _dot(perm_t_ref[...], lo, "nn")


def _unrolled_loop(step, init):
    def trip(o, state):
        for u in range(SCAN_UNROLL):
            state = step(o * SCAN_UNROLL + u, state)
        return state

    if SCAN_UNROLL == STEPS:
        return trip(0, init)
    return lax.fori_loop(0, STEPS // SCAN_UNROLL, trip, init)


def _scan_chunk(x_ref, out_ref, tab_ref, carry_re, carry_im, ascending, pair_ref=None, acc_ref=None, lane_chunks=None):
    w = SCAN_LANES
    half = NSTATE // S5_BLOCKS
    row = lax.broadcasted_iota(jnp.int32, (SUBLANES, w), 0)
    last = (SEGMENTS - 1) if ascending else 0

    def from_previous_segment(v, k, fill):
        if ascending:
            return jnp.where(row >= k, pltpu.roll(v, k, 0), fill)
        return jnp.where(row < SEGMENTS - k, pltpu.roll(v, SEGMENTS - k, 0), fill)

    def tile_rows(k):
        return pl.ds(pl.multiple_of((k if ascending else STEPS - 1 - k) * SUBLANES, SUBLANES), SUBLANES)

    for j in (range(NSTATE // w) if lane_chunks is None else lane_chunks):
        n_l = pl.ds(j * w, w)
        lane0 = (j * w // half) * 2 * half + (j * w) % half
        re_l, im_l = pl.ds(lane0, w), pl.ds(lane0 + half, w)
        tab = lambda t, n_l=n_l: (tab_ref[0, t, :, n_l], tab_ref[1, t, :, n_l])
        a_re, a_im = tab(TAB_A)

        def local_step(k, h):
            rs = tile_rows(k)
            h_re = a_re * h[0] - a_im * h[1] + x_ref[rs, re_l]
            h_im = a_re * h[1] + a_im * h[0] + x_ref[rs, im_l]
            out_ref[rs, re_l] = h_re
            out_ref[rs, im_l] = h_im
            return h_re, h_im

        zero = jnp.zeros((SUBLANES, w), F32)
        end_re, end_im = _unrolled_loop(local_step, (zero, zero))
        for t, k in ((TAB_BIG, 1), (TAB_BIG + 1, 2), (TAB_BIG + 2, 4)):
            p_re, p_im = tab(t)
            s_re, s_im = from_previous_segment(end_re, k, 0.0), from_previous_segment(end_im, k, 0.0)
            end_re, end_im = end_re + (p_re * s_re - p_im * s_im), end_im + (p_re * s_im + p_im * s_re)
        c0_re, c0_im = carry_re[:, n_l], carry_im[:, n_l]
        p_re, p_im = tab(TAB_SEG)
        end_re = end_re + (p_re * c0_re - p_im * c0_im)
        end_im = end_im + (p_re * c0_im + p_im * c0_re)
        carry_re[:, n_l] = jnp.broadcast_to(end_re[last:last + 1, :], end_re.shape)
        carry_im[:, n_l] = jnp.broadcast_to(end_im[last:last + 1, :], end_im.shape)
        in_re = from_previous_segment(end_re, 1, c0_re)
        in_im = from_previous_segment(end_im, 1, c0_im)

        def carry_step(k, st):
            rs = tile_rows(k)
            p_re, p_im = tab_ref[0, TAB_PW + k, :, n_l], tab_ref[1, TAB_PW + k, :, n_l]
            o_re = out_ref[rs, re_l] + (p_re * in_re - p_im * in_im)
            o_im = out_ref[rs, im_l] + (p_re * in_im + p_im * in_re)
            out_ref[rs, re_l] = o_re
            out_ref[rs, im_l] = o_im
            if pair_ref is None:
                return st
            s_re, s_im = pair_ref[rs, re_l], pair_ref[rs, im_l]
            return (o_re, o_im, st[2] + (st[0] * s_re + st[1] * s_im), st[3] + (st[1] * s_re - st[0] * s_im))

        if pair_ref is None:
            _unrolled_loop(carry_step, 0)
        else:
            fin = _unrolled_loop(carry_step, (in_re, in_im, zero, zero))
            acc_ref[:, n_l] += fin[2]
            acc_ref[:, pl.ds(NSTATE + j * w, w)] += fin[3]


def _scan_block_index(i, n_lat, ctx_first_then_ascending):
    if ctx_first_then_ascending:
        return jnp.where(i == 0, n_lat, i - 1)
    return jnp.where(i == 0, n_lat, n_lat - i)


def _full_spec(shape):
    return pl.BlockSpec(shape, lambda i: (0,) * len(shape))


_S5_BLOCKED = (S5_BLOCKS, S5_BLOCK_WIDTH, 2 * NSTATE // S5_BLOCKS)
_S5_TABLES = (2, TAB_ROWS, SUBLANES, NSTATE)
_S5_DIAG = (S5_BLOCKS, S5_GROUP, 2 * NSTATE // S5_BLOCKS)


def _s5_scan_fwd(name, ascending, z_all, bmat, cmat, tab, perm, perm_t):
    rows = z_all.shape[0]
    nb = rows // ROW_BLOCK
    n_lat = nb - 1
    bw, sw = S5_BLOCK_WIDTH, 2 * NSTATE // S5_BLOCKS

    def body(u_ref, bm_ref, cm_ref, tab_ref, p_ref, pt_ref, s_ref, y_ref, bu, yp, carry_re, carry_im):
        @pl.when(pl.program_id(0) == 0)
        def _():
            carry_re[...] = jnp.zeros_like(carry_re)
            carry_im[...] = jnp.zeros_like(carry_im)

        up = _permute_rows(p_ref, u_ref[...].astype(BF16))
        for c in range(S5_BLOCKS):
            bu[:, c * sw:(c + 1) * sw] = _dot(up[:, c * bw:(c + 1) * bw], bm_ref[c], "nn")
        _scan_chunk(bu, s_ref, tab_ref, carry_re, carry_im, ascending)
        for c in range(S5_BLOCKS):
            yp[:, c * bw:(c + 1) * bw] = _dot(s_ref[:, c * sw:(c + 1) * sw].astype(BF16), cm_ref[c], "nt")
        y_ref[...] = _unpermute_rows(pt_ref, yp[...])

    blk = lambda i: (_scan_block_index(i, n_lat, ascending), 0)
    return pl.pallas_call(
        body, name=name, grid=(nb,),
        in_specs=[pl.BlockSpec((ROW_BLOCK, S5_WIDTH), blk), _full_spec(_S5_BLOCKED), _full_spec(_S5_BLOCKED),
                  _full_spec(_S5_TABLES), _full_spec((ROW_BLOCK, ROW_BLOCK)), _full_spec((ROW_BLOCK, ROW_BLOCK))],
        out_specs=[pl.BlockSpec((ROW_BLOCK, 2 * NSTATE), blk), pl.BlockSpec((ROW_BLOCK, S5_WIDTH), blk)],
        out_shape=[_sds((rows, 2 * NSTATE)), _sds((rows, S5_WIDTH))],
        scratch_shapes=[pltpu.VMEM((ROW_BLOCK, 2 * NSTATE), F32), pltpu.VMEM((ROW_BLOCK, S5_WIDTH), F32),
                        pltpu.VMEM((SUBLANES, NSTATE), F32), pltpu.VMEM((SUBLANES, NSTATE), F32)],
        compiler_params=_params(("arbitrary",)),
    )(z_all, bmat, cmat, tab, perm, perm_t)


def _s5_scan_bwd(name, ascending, dy, z_all, states, bmat, cmat, adj, perm, perm_t):
    rows = states.shape[0]
    nb = rows // ROW_BLOCK
    n_lat = nb - 1
    bw, sw = S5_BLOCK_WIDTH, 2 * NSTATE // S5_BLOCKS

    def block_index(i):
        if ascending:
            return jnp.where(i == nb - 1, n_lat, n_lat - 1 - i)
        return jnp.where(i == nb - 1, n_lat, i)

    def body(dy_ref, u_ref, s_ref, bm_ref, cm_ref, adj_ref, p_ref, pt_ref, du_ref, db_ref, dc_ref, da_ref,
             g, dup, db_acc, dc_acc, carry_re, carry_im):
        i = pl.program_id(0)

        @pl.when(i == 0)
        def _():
            carry_re[...] = jnp.zeros_like(carry_re)
            carry_im[...] = jnp.zeros_like(carry_im)
            da_ref[...] = jnp.zeros_like(da_ref)
            db_acc[...] = jnp.zeros_like(db_acc)
            dc_acc[...] = jnp.zeros_like(dc_acc)

        has_dy = (i < nb - 1).astype(F32)
        dyp = _permute_rows(p_ref, (dy_ref[...] * has_dy).astype(BF16))
        up = _permute_rows(p_ref, u_ref[...].astype(BF16))
        for c in range(S5_BLOCKS):
            g[:, c * sw:(c + 1) * sw] = _dot(dyp[:, c * bw:(c + 1) * bw], cm_ref[c], "nn")
            dc_acc[c] += _dot(dyp[:, c * bw:(c + 1) * bw], s_ref[:, c * sw:(c + 1) * sw].astype(BF16), "tn")
            _scan_chunk(g, g, adj_ref, carry_re, carry_im, not ascending, pair_ref=s_ref, acc_ref=da_ref, lane_chunks=[c])
            gc = g[:, c * sw:(c + 1) * sw].astype(BF16)
            dup[:, c * bw:(c + 1) * bw] = _dot(gc, bm_ref[c], "nt")
            db_acc[c] += _dot(up[:, c * bw:(c + 1) * bw], gc, "tn")
        du_ref[...] = _unpermute_rows(pt_ref, dup[...])

        @pl.when(i == nb - 1)
        def _():
            mask = _block_diag_mask((bw, sw // 2))
            for acc, out in ((db_acc, db_ref), (dc_acc, dc_ref)):
                for c in range(S5_BLOCKS):
                    for part in range(2):
                        cols = slice(part * (sw // 2), (part + 1) * (sw // 2))
                        kept = jnp.where(mask, acc[c, :, cols], 0.0)
                        out[c, :, cols] = kept.reshape(bw // S5_GROUP, S5_GROUP, sw // 2).sum(axis=0)

    blk = lambda i: (block_index(i), 0)
    return pl.pallas_call(
        body, name=name, grid=(nb,),
        in_specs=[pl.BlockSpec((ROW_BLOCK, S5_WIDTH), lambda i: (jnp.minimum(block_index(i), n_lat - 1), 0)),
                  pl.BlockSpec((ROW_BLOCK, S5_WIDTH), blk), pl.BlockSpec((ROW_BLOCK, 2 * NSTATE), blk),
                  _full_spec(_S5_BLOCKED), _full_spec(_S5_BLOCKED), _full_spec(_S5_TABLES),
                  _full_spec((ROW_BLOCK, ROW_BLOCK)), _full_spec((ROW_BLOCK, ROW_BLOCK))],
        out_specs=[pl.BlockSpec((ROW_BLOCK, S5_WIDTH), blk), _full_spec(_S5_DIAG), _full_spec(_S5_DIAG),
                   _full_spec((SUBLANES, 2 * NSTATE))],
        out_shape=[_sds((rows, S5_WIDTH)), _sds(_S5_DIAG), _sds(_S5_DIAG), _sds((SUBLANES, 2 * NSTATE))],
        scratch_shapes=[pltpu.VMEM((ROW_BLOCK, 2 * NSTATE), F32), pltpu.VMEM((ROW_BLOCK, S5_WIDTH), F32),
                        pltpu.VMEM(_S5_BLOCKED, F32), pltpu.VMEM(_S5_BLOCKED, F32),
                        pltpu.VMEM((SUBLANES, NSTATE), F32), pltpu.VMEM((SUBLANES, NSTATE), F32)],
        compiler_params=_params(("arbitrary",)),
    )(dy, z_all, states, bmat, cmat, adj, perm, perm_t)


def _glu_fwd(z_all, y0, y1, d_skip, w_glu, n_rows):
    def body(u_ref, y0_ref, y1_ref, d_ref, w_ref, o_ref):
        y = d_ref[...] * u_ref[...] + y0_ref[...] + y1_ref[...]
        g = _gelu(y)
        t = _dot(g.astype(BF16), w_ref[...], "nn")
        o_ref[...] = (g * _sigmoid(t)).astype(o_ref.dtype)

    row = pl.BlockSpec((ROW_BLOCK, S5_WIDTH), lambda i: (i, 0))
    return pl.pallas_call(
        body, name="glu_fwd", grid=(n_rows // ROW_BLOCK,),
        in_specs=[row, row, row, pl.BlockSpec((1, S5_WIDTH), lambda i: (0, 0)),
                  pl.BlockSpec((S5_WIDTH, S5_WIDTH), lambda i: (0, 0))],
        out_specs=row, out_shape=_sds((n_rows, S5_WIDTH + CONV_WIDTH), BF16), compiler_params=_params(("parallel",)),
    )(z_all, y0, y1, d_skip, w_glu)


def _glu_bwd(d_ycat, z_all, y0, y1, d_skip, w_glu, n_rows):
    def body(do_ref, u_ref, y0_ref, y1_ref, d_ref, w_ref, dy_ref, dw_ref, dd_ref):
        @pl.when(pl.program_id(0) == 0)
        def _():
            dw_ref[...] = jnp.zeros_like(dw_ref)
            dd_ref[...] = jnp.zeros_like(dd_ref)

        u = u_ref[...]
        y = d_ref[...] * u + y0_ref[...] + y1_ref[...]
        g = _gelu(y)
        gb = g.astype(BF16)
        w = w_ref[...]
        sg = _sigmoid(_dot(gb, w, "nn"))
        do = do_ref[...]
        dt = do * g * sg * (1.0 - sg)
        dtb = dt.astype(BF16)
        dg = do * sg + _dot(dtb, w, "nt")
        dy = dg * _dgelu(y)
        dy_ref[...] = dy
        dw_ref[...] += _dot(gb, dtb, "tn")
        dd_ref[...] += _fold8(dy * u)

    row = pl.BlockSpec((ROW_BLOCK, S5_WIDTH), lambda i: (i, 0))
    sq = pl.BlockSpec((S5_WIDTH, S5_WIDTH), lambda i: (0, 0))
    return pl.pallas_call(
        body, name="glu_bwd", grid=(n_rows // ROW_BLOCK,),
        in_specs=[row, row, row, row, pl.BlockSpec((1, S5_WIDTH), lambda i: (0, 0)), sq],
        out_specs=[row, sq, pl.BlockSpec((SUBLANES, S5_WIDTH), lambda i: (0, 0))],
        out_shape=[_sds((n_rows, S5_WIDTH)), _sds((S5_WIDTH, S5_WIDTH)), _sds((SUBLANES, S5_WIDTH))],
        compiler_params=_params(("arbitrary",)),
    )(d_ycat, z_all, y0, y1, d_skip, w_glu)


CONV_HALF = CONV_K // 2


def _conv_block(n_rows):
    blk = min(1024, n_rows)
    assert blk >= CONV_HALF * GRID_W and n_rows % blk == 0
    return blk


def _conv_gate(z_all, n_rows):
    blk = _conv_block(n_rows)
    nb = n_rows // blk

    def body(v_ref, g_ref, o_ref):
        i = pl.program_id(0)
        inside = jnp.logical_and(i >= 1, i <= nb)

        @pl.when(inside)
        def _():
            o_ref[...] = v_ref[...] * _sigmoid(g_ref[...])

        @pl.when(jnp.logical_not(inside))
        def _():
            o_ref[...] = jnp.zeros_like(o_ref)

    src = lambda col: pl.BlockSpec((blk, CONV_WIDTH), lambda i: (jnp.clip(i - 1, 0, nb - 1), col))
    return pl.pallas_call(
        body, name="conv_gate", grid=(nb + 2,), in_specs=[src(1), src(2)],
        out_specs=pl.BlockSpec((blk, CONV_WIDTH), lambda i: (i, 0)),
        out_shape=_sds(((nb + 2) * blk, CONV_WIDTH)), compiler_params=_params(("parallel",)),
    )(z_all, z_all)


def _stream_padded(pad_ref, buf, sems, blk, n_blocks):
    i = pl.program_id(0)

    def copy(b):
        rows = pl.ds(pl.multiple_of(b * blk, blk), blk)
        return pltpu.make_async_copy(pad_ref.at[rows, :], buf.at[rows, :], sems.at[b])

    @pl.when(i == 0)
    def _():
        for b in range(n_blocks):
            copy(b).start()
        copy(0).wait()
        copy(1).wait()

    copy(i + 2).wait()
    return pl.multiple_of(i * blk, blk)


def _conv_fwd(hh_pad, w, b, ln_g, ln_b, ycat, n_rows):
    blk = _conv_block(n_rows)
    nblk = n_rows // blk + 2

    def body(hh_ref, w_ref, b_ref, g_ref, lb_ref, ycat_ref, hc_ref, y_ref, win, sems):
        base = _stream_padded(hh_ref, win, sems, blk, nblk)

        def tile(t, _):
            r0 = pl.multiple_of(t * CONV_ROWS, CONV_ROWS)
            acc = jnp.zeros((CONV_ROWS, CONV_WIDTH), F32)
            for k in range(CONV_K):
                acc = acc + w_ref[k:k + 1, :] * win[pl.ds(base + r0 + blk + (k - CONV_HALF) * GRID_W, CONV_ROWS), :]
            hc = acc + b_ref[...]
            hc_ref[pl.ds(r0, CONV_ROWS), :] = hc
            mu = jnp.mean(hc, axis=-1, keepdims=True)
            xc = hc - mu
            ln = xc * lax.rsqrt(jnp.mean(xc * xc, axis=-1, keepdims=True) + EPS_LN) * g_ref[...] + lb_ref[...]
            y_ref[pl.ds(r0, CONV_ROWS), :] = _silu(ln).astype(y_ref.dtype)
            return 0

        lax.fori_loop(0, blk // CONV_ROWS, tile, 0)

    vec = pl.BlockSpec((1, CONV_WIDTH), lambda i: (0, 0))
    row = pl.BlockSpec((blk, CONV_WIDTH), lambda i: (i, 0))
    return pl.pallas_call(
        body, name="conv_fwd", grid=(n_rows // blk,),
        in_specs=[ANY, pl.BlockSpec((CONV_K, CONV_WIDTH), lambda i: (0, 0)), vec, vec, vec, ANY],
        out_specs=[row, pl.BlockSpec((blk, CONV_WIDTH), lambda i: (i, 1))],
        out_shape=[_sds((n_rows, CONV_WIDTH)), _sds(ycat.shape, ycat.dtype)], input_output_aliases={5: 1},
        scratch_shapes=[pltpu.VMEM((nblk * blk, CONV_WIDTH), F32), pltpu.SemaphoreType.DMA((nblk,))],
        compiler_params=_params(("arbitrary",)),
    )(hh_pad, w, b, ln_g, ln_b, ycat)


def _conv_bwd_norm(d_ycat, hc, ln_g, ln_b, n_rows):
    blk = _conv_block(n_rows)
    nb = n_rows // blk

    def body(dy_ref, hc_ref, g_ref, lb_ref, o_ref, sums):
        i = pl.program_id(0)

        @pl.when(i == 0)
        def _():
            sums[...] = jnp.zeros_like(sums)

        inside = jnp.logical_and(i >= 1, i <= nb)

        @pl.when(inside)
        def _():
            hcv = hc_ref[...]
            mu = jnp.mean(hcv, axis=-1, keepdims=True)
            xc = hcv - mu
            rstd = lax.rsqrt(jnp.mean(xc * xc, axis=-1, keepdims=True) + EPS_LN)
            xh = xc * rstd
            g = g_ref[...]
            dln = dy_ref[...] * _dsilu(xh * g + lb_ref[...])
            dxh = dln * g
            dhc = rstd * (dxh - jnp.mean(dxh, axis=-1, keepdims=True) - xh * jnp.mean(dxh * xh, axis=-1, keepdims=True))
            o_ref[...] = dhc
            sums[0] += _fold8(dhc)
            sums[1] += _fold8(dln * xh)
            sums[2] += _fold8(dln)

        @pl.when(jnp.logical_not(inside))
        def _():
            o_ref[...] = jnp.zeros_like(o_ref)

    vec = pl.BlockSpec((1, CONV_WIDTH), lambda i: (0, 0))
    return pl.pallas_call(
        body, name="conv_bwd_norm", grid=(nb + 2,),
        in_specs=[pl.BlockSpec((blk, CONV_WIDTH), lambda i: (jnp.clip(i - 1, 0, nb - 1), 1)),
                  pl.BlockSpec((blk, CONV_WIDTH), lambda i: (jnp.clip(i - 1, 0, nb - 1), 0)), vec, vec],
        out_specs=[pl.BlockSpec((blk, CONV_WIDTH), lambda i: (i, 0)),
                   pl.BlockSpec((3, SUBLANES, CONV_WIDTH), lambda i: (0, 0, 0))],
        out_shape=[_sds(((nb + 2) * blk, CONV_WIDTH)), _sds((3, SUBLANES, CONV_WIDTH))],
        compiler_params=_params(("arbitrary",)),
    )(d_ycat, hc, ln_g, ln_b)


def _conv_bwd_taps(dhc_pad, hh_pad, z_all, w, n_rows):
    blk = _conv_block(n_rows)
    nblk = n_rows // blk + 2

    def body(dhc_ref, hh_ref, v_ref, g_ref, w_ref, dv_ref, dg_ref, dw_ref, dwin, hwin, dsems, hsems):
        @pl.when(pl.program_id(0) == 0)
        def _():
            dw_ref[...] = jnp.zeros_like(dw_ref)

        base = _stream_padded(dhc_ref, dwin, dsems, blk, nblk)
        _stream_padded(hh_ref, hwin, hsems, blk, nblk)

        def tile(t, _):
            r0 = pl.multiple_of(t * CONV_BWD_ROWS, CONV_BWD_ROWS) + base
            dh = dwin[pl.ds(r0 + blk, CONV_BWD_ROWS), :]
            acc = jnp.zeros((CONV_BWD_ROWS, CONV_WIDTH), F32)
            for k in range(CONV_K):
                off = (k - CONV_HALF) * GRID_W
                acc = acc + w_ref[k:k + 1, :] * dwin[pl.ds(r0 + blk - off, CONV_BWD_ROWS), :]
                dw_ref[k] += _fold8(dh * hwin[pl.ds(r0 + blk + off, CONV_BWD_ROWS), :])
            rs = pl.ds(pl.multiple_of(t * CONV_BWD_ROWS, CONV_BWD_ROWS), CONV_BWD_ROWS)
            sg = _sigmoid(g_ref[rs, :])
            vv = v_ref[rs, :]
            dv_ref[rs, :] = acc * sg
            dg_ref[rs, :] = acc * vv * sg * (1.0 - sg)
            return 0

        lax.fori_loop(0, blk // CONV_BWD_ROWS, tile, 0)

    row = pl.BlockSpec((blk, CONV_WIDTH), lambda i: (i, 0))
    return pl.pallas_call(
        body, name="conv_bwd_taps", grid=(n_rows // blk,),
        in_specs=[ANY, ANY,
            pl.BlockSpec((blk, CONV_WIDTH), lambda i: (i, 1)), pl.BlockSpec((blk, CONV_WIDTH), lambda i: (i, 2)),
            pl.BlockSpec((CONV_K, CONV_WIDTH), lambda i: (0, 0))],
        out_specs=[row, row, pl.BlockSpec((CONV_K, SUBLANES, CONV_WIDTH), lambda i: (0, 0, 0))],
        out_shape=[_sds((n_rows, CONV_WIDTH)), _sds((n_rows, CONV_WIDTH)), _sds((CONV_K, SUBLANES, CONV_WIDTH))],
        scratch_shapes=[pltpu.VMEM((nblk * blk, CONV_WIDTH), F32), pltpu.VMEM((nblk * blk, CONV_WIDTH), F32),
                        pltpu.SemaphoreType.DMA((nblk,)), pltpu.SemaphoreType.DMA((nblk,))],
        compiler_params=_params(("arbitrary",)),
    )(dhc_pad, hh_pad, z_all, z_all, w)


def _dz_assemble(du0, du1, dy, d_skip, dv, dgate, n_lat):
    rows = du0.shape[0]
    nb = rows // ROW_BLOCK

    w = S5_WIDTH

    def body(a_ref, b_ref, dy_ref, d_ref, dv_ref, dg_ref, o_ref):
        lat = pl.program_id(0) < n_lat

        @pl.when(lat)
        def _():
            o_ref[:, 0:w] = (a_ref[...] + b_ref[...] + dy_ref[...] * d_ref[...]).astype(o_ref.dtype)
            o_ref[:, w:2 * w] = dv_ref[...].astype(o_ref.dtype)
            o_ref[:, 2 * w:3 * w] = dg_ref[...].astype(o_ref.dtype)

        @pl.when(jnp.logical_not(lat))
        def _():
            o_ref[:, 0:w] = (a_ref[...] + b_ref[...]).astype(o_ref.dtype)
            o_ref[:, w:3 * w] = jnp.zeros((ROW_BLOCK, 2 * w), o_ref.dtype)

    all_rows = pl.BlockSpec((ROW_BLOCK, w), lambda i: (i, 0))
    lat_rows = pl.BlockSpec((ROW_BLOCK, w), lambda i: (jnp.minimum(i, n_lat - 1), 0))
    return pl.pallas_call(
        body, name="dz_assemble", grid=(nb,),
        in_specs=[all_rows, all_rows, lat_rows, pl.BlockSpec((1, w), lambda i: (0, 0)), lat_rows, lat_rows],
        out_specs=pl.BlockSpec((ROW_BLOCK, IN_COLS), lambda i: (i, 0)),
        out_shape=_sds((rows, IN_COLS), BF16), compiler_params=_params(("parallel",)),
    )(du0, du1, dy, d_skip, dv, dgate)


def _sum_parts(parts):
    _, r, c = parts.shape

    def body(p_ref, o_ref):
        acc = p_ref[0]
        for q in range(1, NDEV):
            acc = acc + p_ref[q]
        o_ref[...] = acc

    return pl.pallas_call(body, name="sum_parts", out_shape=_sds((r, c)), compiler_params=_params())(parts)


def _row_tile(r, c):
    best = r
    for t in (1024, 512, 256, 128, 64, 32, 16, 8):
        if r % t == 0 and t * c <= 128 * 1024:
            return t
    return best


def _adamw(name, w, gparts, m, v):
    r, c = w.shape
    np_ = gparts.shape[0]
    tr = _row_tile(r, c)

    def body(w_ref, g_ref, m_ref, v_ref, go_ref, d_ref, mo_ref, vo_ref):
        g = g_ref[0].astype(F32)
        for q in range(1, np_):
            g = g + g_ref[q].astype(F32)
        m2 = ADAM_B1 * m_ref[...] + (1.0 - ADAM_B1) * g
        v2 = ADAM_B2 * v_ref[...] + (1.0 - ADAM_B2) * jnp.square(g)
        m_hat = m2 / (1.0 - ADAM_B1 ** ADAM_STEP)
        v_hat = v2 / (1.0 - ADAM_B2 ** ADAM_STEP)
        go_ref[...] = g
        d_ref[...] = -ADAM_LR * (m_hat / (jnp.sqrt(v_hat) + ADAM_EPS) + ADAM_WD * w_ref[...])
        mo_ref[...] = m2
        vo_ref[...] = v2

    row = pl.BlockSpec((tr, c), lambda i: (i, 0))
    return pl.pallas_call(
        body, name=name, grid=(r // tr,),
        in_specs=[row, pl.BlockSpec((np_, tr, c), lambda i: (0, i, 0)), row, row],
        out_specs=[row] * 4, out_shape=[_sds((r, c))] * 4, compiler_params=_params(("parallel",)),
    )(w, gparts, m, v)


def _adamw_native(name, w, g, m, v):
    def body(w_ref, g_ref, m_ref, v_ref, d_ref, mo_ref, vo_ref):
        gv = g_ref[...]
        m2 = ADAM_B1 * m_ref[...] + (1.0 - ADAM_B1) * gv
        v2 = ADAM_B2 * v_ref[...] + (1.0 - ADAM_B2) * jnp.square(gv)
        m_hat = m2 / (1.0 - ADAM_B1 ** ADAM_STEP)
        v_hat = v2 / (1.0 - ADAM_B2 ** ADAM_STEP)
        d_ref[...] = -ADAM_LR * (m_hat / (jnp.sqrt(v_hat) + ADAM_EPS) + ADAM_WD * w_ref[...])
        mo_ref[...] = m2
        vo_ref[...] = v2

    return pl.pallas_call(body, name=name, out_shape=[_sds(w.shape)] * 3, compiler_params=_params())(w, g, m, v)


SMALL = ["c_ctx", "ada_b", "norm1_g", "s5_lam_re", "s5_lam_im", "s5_log_dt", "s5_d", "conv_b", "conv_ln_g", "conv_ln_b",
         "norm2_g", "final_g"]
SMALL_PACKED_ROWS = 24


def _pack_rows(parts, rows):
    flat = jnp.concatenate([p.reshape(-1).astype(F32) for p in parts])
    return jnp.pad(flat, (0, rows * D_MODEL - flat.shape[0])).reshape(rows, D_MODEL)


def _unpack_rows(packed, shapes):
    flat = packed.reshape(-1)
    out, off = [], 0
    for shape in shapes:
        size = 1
        for s in shape:
            size *= s
        out.append(flat[off:off + size].reshape(shape))
        off += size
    return out


def kernel(x, c, ctx, c_ctx, ada_w, ada_b, norm1_g, w_in, s5_lam_re, s5_lam_im, s5_log_dt, s5_b_re, s5_b_im, s5_c_re, s5_c_im, s5_d, s5_w_glu, conv_w, conv_b, conv_ln_g, conv_ln_b, w_out, norm2_g, mlp_w1, mlp_w2, final_g, loss_target, m_c_ctx, m_ada_w, m_ada_b, m_norm1_g, m_w_in, m_s5_lam_re, m_s5_lam_im, m_s5_log_dt, m_s5_b_re, m_s5_b_im, m_s5_c_re, m_s5_c_im, m_s5_d, m_s5_w_glu, m_conv_w, m_conv_b, m_conv_ln_g, m_conv_ln_b, m_w_out, m_norm2_g, m_mlp_w1, m_mlp_w2, m_final_g, v_c_ctx, v_ada_w, v_ada_b, v_norm1_g, v_w_in, v_s5_lam_re, v_s5_lam_im, v_s5_log_dt, v_s5_b_re, v_s5_b_im, v_s5_c_re, v_s5_c_im, v_s5_d, v_s5_w_glu, v_conv_w, v_conv_b, v_conv_ln_g, v_conv_ln_b, v_w_out, v_norm2_g, v_mlp_w1, v_mlp_w2, v_final_g):
    weights = dict(c_ctx=c_ctx, ada_w=ada_w, ada_b=ada_b, norm1_g=norm1_g, w_in=w_in, s5_lam_re=s5_lam_re, s5_lam_im=s5_lam_im, s5_log_dt=s5_log_dt, s5_b_re=s5_b_re, s5_b_im=s5_b_im, s5_c_re=s5_c_re, s5_c_im=s5_c_im, s5_d=s5_d, s5_w_glu=s5_w_glu, conv_w=conv_w, conv_b=conv_b, conv_ln_g=conv_ln_g, conv_ln_b=conv_ln_b, w_out=w_out, norm2_g=norm2_g, mlp_w1=mlp_w1, mlp_w2=mlp_w2, final_g=final_g)
    mom1 = dict(c_ctx=m_c_ctx, ada_w=m_ada_w, ada_b=m_ada_b, norm1_g=m_norm1_g, w_in=m_w_in, s5_lam_re=m_s5_lam_re, s5_lam_im=m_s5_lam_im, s5_log_dt=m_s5_log_dt, s5_b_re=m_s5_b_re, s5_b_im=m_s5_b_im, s5_c_re=m_s5_c_re, s5_c_im=m_s5_c_im, s5_d=m_s5_d, s5_w_glu=m_s5_w_glu, conv_w=m_conv_w, conv_b=m_conv_b, conv_ln_g=m_conv_ln_g, conv_ln_b=m_conv_ln_b, w_out=m_w_out, norm2_g=m_norm2_g, mlp_w1=m_mlp_w1, mlp_w2=m_mlp_w2, final_g=m_final_g)
    mom2 = dict(c_ctx=v_c_ctx, ada_w=v_ada_w, ada_b=v_ada_b, norm1_g=v_norm1_g, w_in=v_w_in, s5_lam_re=v_s5_lam_re, s5_lam_im=v_s5_lam_im, s5_log_dt=v_s5_log_dt, s5_b_re=v_s5_b_re, s5_b_im=v_s5_b_im, s5_c_re=v_s5_c_re, s5_c_im=v_s5_c_im, s5_d=v_s5_d, s5_w_glu=v_s5_w_glu, conv_w=v_conv_w, conv_b=v_conv_b, conv_ln_g=v_conv_ln_g, conv_ln_b=v_conv_ln_b, w_out=v_w_out, norm2_g=v_norm2_g, mlp_w1=v_mlp_w1, mlp_w2=v_mlp_w2, final_g=v_final_g)
    order = list(weights)

    me = 4 * lax.axis_index("x") + 2 * lax.axis_index("y") + lax.axis_index("c")
    xs, cs, tgt = x[0], ctx[0], loss_target[0]
    n_lat_rows, n_ctx_rows = xs.shape[0], cs.shape[0]
    n_rows = n_lat_rows + n_ctx_rows
    n_lat = n_lat_rows // ROW_BLOCK
    ada_cols = ada_w.shape[2]

    (c_all,), _ = _exchange("gather_c", [c], [True])
    c_all = c_all.reshape(NDEV, D_MODEL)

    cond_fwd = jnp.concatenate([c_all, c_ctx[None], jnp.zeros((7, D_MODEL), F32)])
    ada_b_loc = lax.dynamic_slice(ada_b, (0, me * ada_cols), (1, ada_cols))
    (mod_g,), mod_token = _exchange("gather_mod", [_ada_fwd(cond_fwd, ada_w[0], ada_b_loc)], [True])
    weight_groups, weights_token = _exchange_start_groups("gather_weights_start", [
        ([w_in[0].astype(BF16)], [True]),
        ([s5_w_glu[0].astype(BF16), conv_w[0] + mod_token[0:1, 0:1], w_out[0].astype(BF16)], [True] * 3),
        ([mlp_w1[0].astype(BF16), mlp_w2[0].astype(BF16)], [True] * 2)])
    (wi_send, wi_recv, wi_src, wi_land), (mixer_send, mixer_recv, mixer_src, mixer_land), \
        (mlpw_send, mlpw_recv, mlpw_src, mlpw_land) = weight_groups
    mod_rows = jnp.transpose(mod_g, (1, 0, 2)).reshape(16, 6 * D_MODEL) + weights_token[0:1, 0:1]
    mod = lax.dynamic_slice(mod_rows, (me, 0), (1, 6 * D_MODEL)).reshape(6, D_MODEL)
    modc = mod_rows[8, :2 * D_MODEL].reshape(2, D_MODEL)
    sh1, sc1, g1, sh2, sc2, g2 = [mod[i:i + 1] for i in range(6)]

    lam_re, lam_im = s5_lam_re[0].reshape(2, 1, NSTATE), s5_lam_im[0].reshape(2, 1, NSTATE)
    ldt = jnp.repeat(s5_log_dt[0], S5_STATE, axis=-1).reshape(2, 1, NSTATE)
    bt_re = jnp.transpose(s5_b_re[0], (0, 3, 1, 2)).reshape(2, S5_GROUP, NSTATE)
    bt_im = jnp.transpose(s5_b_im[0], (0, 3, 1, 2)).reshape(2, S5_GROUP, NSTATE)
    groups_per_block = S5_GROUPS // S5_BLOCKS
    ct_re = jnp.tile(s5_c_re[0].reshape(2, S5_WIDTH, S5_STATE), (1, 1, groups_per_block))
    ct_im = jnp.tile(s5_c_im[0].reshape(2, S5_WIDTH, S5_STATE), (1, 1, groups_per_block))
    d_skip = s5_d[0].reshape(1, S5_WIDTH)
    perm = _segment_permutation()
    perm_t = perm.T
    disc = [_s5_discretise(f"s5_disc{d}", d == 0, lam_re[d], lam_im[d], ldt[d], bt_re[d], bt_im[d], ct_re[d], ct_im[d])
            for d in range(2)]

    a_all = _prenorm("prenorm1", xs, cs, norm1_g, jnp.stack([mod[0:2], modc]))
    before_w_in = a_all[0:SUBLANES, 0:LANES].astype(F32) + disc[0][0][0:SUBLANES, 0:LANES] + disc[1][0][0:SUBLANES, 0:LANES]
    wi_own, wi_landed = _exchange_wait("gather_w_in_wait", wi_send, wi_recv, wi_src, wi_land, [True], before_w_in)
    w_in_full = jnp.transpose(_with_own(wi_landed[0], wi_own[0], me), (1, 0, 2)).reshape(D_MODEL, IN_COLS)
    tm_all = 1088 if n_rows % 1088 == 0 else ROW_BLOCK
    (z_all,) = _matmul("in_proj", a_all, w_in_full, "nn", (n_rows, IN_COLS, D_MODEL), (tm_all, IN_COLS, D_MODEL),
                       [((n_rows, IN_COLS), F32)])

    states, y_dir = [], []
    for d in range(2):
        _, tab, _, bmat, cmat = disc[d]
        s, yd = _s5_scan_fwd(f"s5_scan_fwd{d}", d == 0, z_all, bmat, cmat, tab, perm, perm_t)
        states.append(s)
        y_dir.append(yd)
    mixer_own, mixer_landed = _exchange_wait("gather_mixer_wait", mixer_send, mixer_recv, mixer_src, mixer_land,
                                             [True] * 3, y_dir[1])
    glu_g, conv_w_g, w_out_g = [_with_own(l, o, me) for l, o in zip(mixer_landed, mixer_own)]
    glu_full = glu_g.reshape(S5_WIDTH, S5_WIDTH)
    conv_w_full = jnp.transpose(conv_w_g, (1, 0, 2)).reshape(CONV_K, CONV_WIDTH)
    w_out_full = w_out_g.reshape(D_MODEL, D_MODEL)
    ycat = _glu_fwd(z_all, y_dir[0], y_dir[1], d_skip, glu_full, n_lat_rows)

    hh_pad = _conv_gate(z_all, n_lat_rows)
    hc, ycat = _conv_fwd(hh_pad, conv_w_full, conv_b, conv_ln_g, conv_ln_b, ycat, n_lat_rows)

    tm = min(1024, n_lat_rows)
    tm_e = min(512, n_lat_rows)
    w1_cols = D_FF // NDEV
    row_vec = lambda tn: pl.BlockSpec((1, tn), lambda i, j, k: (0, j))
    out_tile = lambda t_m, t_n: pl.BlockSpec((t_m, t_n), lambda i, j, k: (i, j))
    full_rows = ((n_lat_rows, D_MODEL), F32)
    sums = ((n_lat_rows // tm_e, SUBLANES, D_MODEL), F32)
    sums_spec = pl.BlockSpec((None, SUBLANES, D_MODEL), lambda i, j, k: (i, 0, 0))
    vec = lambda v: (v, row_vec(D_MODEL))
    transposed_tile = lambda t_m, t_n: pl.BlockSpec((t_n, t_m), lambda i, j, k: (j, i))
    mix, h1, a2, a2_t = _matmul(
        "out_proj", ycat, w_out_full, "nn", (n_lat_rows, D_MODEL, D_MODEL), (tm_e, D_MODEL, D_MODEL),
        [full_rows, full_rows, ((n_lat_rows, D_MODEL), BF16), ((D_MODEL, n_lat_rows), BF16)],
        epi=_epi_residual_prenorm,
        epi_extra=[(xs, out_tile(tm_e, D_MODEL)), vec(g1), vec(norm2_g), vec(sc2), vec(sh2)],
        out_specs=[out_tile(tm_e, D_MODEL)] * 3 + [transposed_tile(tm_e, D_MODEL)])
    mlpw_own, mlpw_landed = _exchange_wait("gather_mlp_wait", mlpw_send, mlpw_recv, mlpw_src, mlpw_land, [True] * 2, a2)
    w1_g, w2_g = [_with_own(l, o, me) for l, o in zip(mlpw_landed, mlpw_own)]
    w2_full = w2_g.reshape(D_FF, D_MODEL)
    tm_up = min(2048, n_lat_rows)
    f, f_t = _matmul("mlp_up", a2, w1_g, "nn", (n_lat_rows, D_FF, D_MODEL), (tm_up, w1_cols, D_MODEL),
                     [((n_lat_rows, D_FF), BF16), ((D_FF, n_lat_rows), BF16)], epi=lambda acc: (acc, acc.T),
                     b_spec=pl.BlockSpec((None, D_MODEL, w1_cols), lambda i, j, k: (j, 0, 0)),
                     out_specs=[out_tile(tm_up, w1_cols), transposed_tile(tm_up, w1_cols)])
    sq_relu = lambda t: jnp.square(jnp.maximum(t, 0.0))
    mlp_out, d_h2, dm2, err_sums, d_final_g8 = _matmul(
        "mlp_down", f, w2_full, "nn", (n_lat_rows, D_MODEL, D_FF), (tm_e, D_MODEL, 2048),
        [full_rows, full_rows, ((n_lat_rows, D_MODEL), BF16), sums, sums], a_fn=sq_relu, epi=_epi_residual_loss,
        epi_extra=[(h1, out_tile(tm_e, D_MODEL)), vec(g2), (tgt, out_tile(tm_e, D_MODEL)), vec(final_g[None])],
        out_specs=[out_tile(tm_e, D_MODEL)] * 3 + [sums_spec] * 2)

    (d_f,) = _matmul("mlp_down_dx", dm2, w2_full, "nt", (n_lat_rows, D_FF, D_MODEL), (tm_up, 512, D_MODEL),
                     [((n_lat_rows, D_FF), BF16)],
                     epi=lambda acc, ft: (acc * 2.0 * jnp.maximum(ft.astype(F32), 0.0),),
                     epi_extra=[(f, out_tile(tm_up, 512))])
    tk_dw = min(2048, n_lat_rows)
    (g_w2,) = _matmul("mlp_down_dw", f_t, dm2, "nn", (D_FF, D_MODEL, n_lat_rows), (1024, D_MODEL, tk_dw),
                      [((D_FF, D_MODEL), F32)], a_fn=sq_relu)
    (g_w1,) = _matmul("mlp_up_dw", a2_t, d_f, "nn", (D_MODEL, D_FF, n_lat_rows), (D_MODEL, w1_cols, n_lat_rows),
                      [((NDEV, D_MODEL, w1_cols), F32)],
                      out_specs=[pl.BlockSpec((None, D_MODEL, w1_cols), lambda i, j, k: (j, 0, 0))])
    mlp_send, mlp_recv, mlp_src, mlp_land, mlp_token = _exchange_start(
        "scatter_mlp_start", [g_w1, g_w2.reshape(NDEV, D_FF // NDEV, D_MODEL)], [False] * 2)
    d_h1, dm1, *sums2 = _matmul(
        "mlp_up_dx", d_f, w1_g, "nt", (n_lat_rows, D_MODEL, D_FF), (tm_e, D_MODEL, 4 * w1_cols),
        [full_rows, ((n_lat_rows, D_MODEL), BF16)] + [sums] * 4, epi=_epi_norm_bwd,
        epi_extra=[(h1, out_tile(tm_e, D_MODEL)), (d_h2, out_tile(tm_e, D_MODEL)), (mlp_out, out_tile(tm_e, D_MODEL)),
                   vec(norm2_g), vec(sc2 + mlp_token[0:1, 0:1]), vec(g1)],
        b_spec=pl.BlockSpec((4, D_MODEL, w1_cols), lambda i, j, k: (k, 0, 0)), b_slabs=4,
        out_specs=[out_tile(tm_e, D_MODEL)] * 2 + [sums_spec] * 4)

    (d_ycat,) = _matmul("out_proj_dx", dm1, w_out_full, "nt", (n_lat_rows, D_MODEL, D_MODEL), (tm, D_MODEL, D_MODEL),
                        [((n_lat_rows, D_MODEL), F32)])
    (g_w_out,) = _matmul("out_proj_dw", ycat, dm1, "tn", (D_MODEL, D_MODEL, n_lat_rows), (D_MODEL, D_MODEL, 512),
                         [((D_MODEL, D_MODEL), F32)])

    dy, g_glu, dd8 = _glu_bwd(d_ycat, z_all, y_dir[0], y_dir[1], d_skip, glu_full, n_lat_rows)
    proj_send, proj_recv, proj_src, proj_land, proj_token = _exchange_start(
        "scatter_proj_start",
        [g_w_out.reshape(NDEV, D_MODEL // NDEV, D_MODEL), g_glu.reshape(NDEV, S5_WIDTH // NDEV, S5_WIDTH)], [False] * 2)
    perm = perm + proj_token[0:1, 0:1].astype(BF16)
    du, g_lam_re, g_lam_im, g_ldt, g_bt, g_cdiag = [], [], [], [], [], []
    for d in range(2):
        _, _, adj, bmat, cmat = disc[d]
        du_d, d_bdiag, d_cdiag, d_abar8 = _s5_scan_bwd(f"s5_scan_bwd{d}", d == 0, dy, z_all, states[d], bmat, cmat, adj,
                                                       perm, perm_t)
        du.append(du_d)
        d_bbar = jnp.transpose(d_bdiag.reshape(S5_BLOCKS, S5_GROUP, 2, NSTATE // S5_BLOCKS), (2, 1, 0, 3)).reshape(
            2 * S5_GROUP, NSTATE)
        d_lam8, d_bt = _s5_discretise_bwd(f"s5_disc_bwd{d}", lam_re[d], lam_im[d], ldt[d], bt_re[d], bt_im[d], d_abar8, d_bbar)
        g_lam_re.append(d_lam8[0].reshape(S5_GROUPS, S5_STATE))
        g_lam_im.append(d_lam8[1].reshape(S5_GROUPS, S5_STATE))
        g_ldt.append(d_lam8[2].reshape(S5_GROUPS, S5_STATE).sum(axis=-1))
        g_bt.append(d_bt)
        g_cdiag.append(d_cdiag)

    dhc_pad, conv_sums = _conv_bwd_norm(d_ycat, hc, conv_ln_g, conv_ln_b, n_lat_rows)
    d_v, d_gate, g_conv_w8 = _conv_bwd_taps(dhc_pad, hh_pad, z_all, conv_w_full, n_lat_rows)

    dz_all = _dz_assemble(du[0], du[1], dy, d_skip, d_v, d_gate, n_lat)
    (g_w_in_full,) = _matmul("in_proj_dw", a_all, dz_all, "tn", (D_MODEL, IN_COLS, n_rows), (D_MODEL, IN_COLS, tm_all),
                             [((D_MODEL, IN_COLS), F32)])
    g_w_in_parts = jnp.transpose(g_w_in_full.reshape(D_MODEL, NDEV, IN_COLS // NDEV), (1, 0, 2)).astype(BF16)
    win_send, win_recv, win_src, win_land, win_token = _exchange_start("scatter_w_in_start", [g_w_in_parts], [False])
    w_in_late = w_in_full + win_token[0:1, 0:1].astype(BF16)
    grad_x, *sums1 = _matmul(
        "in_proj_dx", dz_all, w_in_late, "nt", (n_lat_rows, D_MODEL, IN_COLS), (tm_e, D_MODEL, IN_COLS),
        [full_rows] + [sums] * 4, epi=_epi_norm_bwd,
        epi_extra=[(xs, out_tile(tm_e, D_MODEL)), (d_h1, out_tile(tm_e, D_MODEL)), (mix, out_tile(tm_e, D_MODEL)),
                   vec(norm1_g), vec(sc1)],
        out_specs=[out_tile(tm_e, D_MODEL)] + [sums_spec] * 4)
    (d_a_ctx,) = _matmul("in_proj_dx_ctx", dz_all, w_in_late, "nt", (n_ctx_rows, D_MODEL, IN_COLS),
                         (ROW_BLOCK, D_MODEL, IN_COLS), [((n_ctx_rows, D_MODEL), F32)],
                         a_spec=pl.BlockSpec((ROW_BLOCK, IN_COLS), lambda i, j, k: (i + n_lat, 0)))
    (sums1c,) = _norm_bwd("norm1_bwd_ctx", cs, d_a_ctx, 0, norm1_g, modc[1:2])

    s1, s1c, s2 = [p.sum(axis=(0, 1)) for p in sums1], sums1c.sum(axis=1), [p.sum(axis=(0, 1)) for p in sums2]
    d_mod = jnp.concatenate([s1[0], s1[1], s1[3], s2[0], s2[1], s2[3]])
    d_modc = jnp.concatenate([s1c[0], s1c[1], jnp.zeros((4 * D_MODEL,), F32)])
    (dmod_g,), _ = _exchange("gather_dmod", [jnp.stack([d_mod, d_modc])], [True])
    dmod16 = jnp.concatenate([dmod_g[:, 0], dmod_g[:, 1]])
    dmod16_loc = lax.dynamic_slice(dmod16, (0, me * ada_cols), (16, ada_cols))
    cond_bwd = jnp.concatenate([c_all, jnp.broadcast_to(c_ctx[None], (NDEV, D_MODEL))])
    g_ada_w, g_c_ctx8 = _ada_bwd(cond_bwd, dmod16_loc, ada_w[0], c_ctx[None])

    small_parts = dict(
        c_ctx=g_c_ctx8[0], ada_b=d_mod + d_modc, norm1_g=s1[2] + s1c[2],
        s5_lam_re=jnp.stack(g_lam_re), s5_lam_im=jnp.stack(g_lam_im), s5_log_dt=jnp.stack(g_ldt),
        s5_d=dd8.sum(axis=0), conv_b=conv_sums[0].sum(axis=0), conv_ln_g=conv_sums[1].sum(axis=0),
        conv_ln_b=conv_sums[2].sum(axis=0), norm2_g=s2[2], final_g=d_final_g8.sum(axis=(0, 1)))
    reduced_shapes = [(SMALL_PACKED_ROWS, D_MODEL), (2, 2 * S5_GROUP, NSTATE), (2,) + _S5_DIAG, (1,)]
    small_g = _pack_rows(
        [_pack_rows([small_parts[n] for n in SMALL], SMALL_PACKED_ROWS), jnp.stack(g_bt), jnp.stack(g_cdiag),
         (0.5 / D_MODEL * jnp.sum(err_sums)).reshape(1)], SMALL_ROWS).reshape(NDEV, SMALL_ROWS // NDEV, D_MODEL)
    g_conv_w_parts = jnp.transpose(g_conv_w8.sum(axis=1).reshape(CONV_K, NDEV, CONV_WIDTH // NDEV), (1, 0, 2))

    res = {}

    def own_chunk(src):
        return lax.dynamic_index_in_dim(src, me, 0, keepdims=False)

    def adamw_big(name, parts):
        outs = _adamw("adamw_" + name, weights[name][0], parts, mom1[name][0], mom2[name][0])
        res[name] = [o[None] for o in outs]
        return outs[0]

    sm_send, sm_recv, sm_src, sm_land, sm_token = _exchange_start("scatter_small_start", [g_conv_w_parts, small_g],
                                                                  [False] * 2)
    mlp_src, mlp_landed = _exchange_wait("scatter_mlp_wait", mlp_send, mlp_recv, mlp_src, mlp_land, [False] * 2, sm_token)
    p_w1, p_w2 = [_with_own(l, own_chunk(s), me) for l, s in zip(mlp_landed, mlp_src)]
    adamw_big("ada_w", g_ada_w[None])
    adamw_big("mlp_w1", p_w1)
    done = adamw_big("mlp_w2", p_w2)
    sm_src, sm_landed = _exchange_wait("scatter_small_wait", sm_send, sm_recv, sm_src, sm_land, [False] * 2, done)
    p_conv_w, p_small = [_with_own(l, own_chunk(s), me) for l, s in zip(sm_landed, sm_src)]
    ga_send, ga_recv, ga_src, ga_land, ga_token = _exchange_start("gather_small_start", [_sum_parts(p_small)], [True])
    proj_src, proj_landed = _exchange_wait("scatter_proj_wait", proj_send, proj_recv, proj_src, proj_land, [False] * 2,
                                           ga_token)
    p_w_out, p_glu = [_with_own(l, own_chunk(s), me) for l, s in zip(proj_landed, proj_src)]
    adamw_big("w_out", p_w_out)
    done = adamw_big("s5_w_glu", p_glu)
    win_src, win_landed = _exchange_wait("scatter_w_in_wait", win_send, win_recv, win_src, win_land, [False], done)
    adamw_big("w_in", _with_own(win_landed[0], own_chunk(win_src[0]), me))
    done = adamw_big("conv_w", p_conv_w)
    ga_own, ga_landed = _exchange_wait("gather_small_wait", ga_send, ga_recv, ga_src, ga_land, [True], done)
    small_all = _with_own(ga_landed[0], ga_own[0], me).reshape(1, SMALL_ROWS, D_MODEL)
    _, r_bt, r_cdiag, loss = _unpack_rows(small_all, reduced_shapes)
    loss = loss.reshape(())
    pack = lambda src: _pack_rows([src[n] for n in SMALL], SMALL_PACKED_ROWS)
    outs = _adamw("adamw_small", pack(weights), small_all, pack(mom1), pack(mom2))
    unpacked = [_unpack_rows(o, [weights[n].shape for n in SMALL]) for o in outs]
    for i, name in enumerate(SMALL):
        res[name] = [u[i] for u in unpacked]
    to_ghp = lambda t: jnp.transpose(t.reshape(2, S5_GROUP, S5_GROUPS, S5_STATE), (0, 2, 1, 3))[None]
    r_c = jnp.transpose(r_cdiag.reshape(2, S5_BLOCKS, S5_GROUP, 2, groups_per_block, S5_STATE), (3, 0, 1, 4, 2, 5)).reshape(
        2, 1, 2, S5_GROUPS, S5_GROUP, S5_STATE)
    swap = lambda t: jnp.swapaxes(t, -1, -2)
    for name, grad in (("s5_b_re", to_ghp(r_bt[:, :S5_GROUP])), ("s5_b_im", to_ghp(r_bt[:, S5_GROUP:]))):
        outs = _adamw_native("adamw_" + name, swap(weights[name]), grad, swap(mom1[name]), swap(mom2[name]))
        res[name] = [swap(grad), *[swap(o) for o in outs]]
    for name, grad in (("s5_c_re", r_c[0]), ("s5_c_im", -r_c[1])):
        res[name] = [grad, *_adamw_native("adamw_" + name, weights[name], grad, mom1[name], mom2[name])]

    return (loss, grad_x[None], *[res[n][0] for n in order], *[res[n][1] for n in order],
            *[res[n][2] for n in order], *[res[n][3] for n in order])
```

```python
import jax
import jax.numpy as jnp
from jax import lax
from jax.experimental import pallas as pl
from jax.experimental.pallas import tpu as pltpu

F32 = jnp.float32
BF16 = jnp.bfloat16
MESH = pl.DeviceIdType.MESH
ANY = pl.BlockSpec(memory_space=pl.ANY)

NDEV = 8
D_MODEL = 1024
GRID_W = 64
S5_WIDTH = 512
S5_GROUP = 16
S5_GROUPS = 32
S5_STATE = 64
NSTATE = S5_GROUPS * S5_STATE
CONV_WIDTH = 512
CONV_K = 31
IN_COLS = S5_WIDTH + 2 * CONV_WIDTH
D_FF = 4 * D_MODEL
EPS_RMS = 1e-6
EPS_LN = 1e-5
ADAM_LR = 0.001
ADAM_B1 = 0.9
ADAM_B2 = 0.999
ADAM_EPS = 1e-08
ADAM_WD = 0.01
ADAM_STEP = 10

SUBLANES = 8
LANES = 128
ROW_BLOCK = 256
SCAN_LANES = 512
SCAN_UNROLL = 32
SEGMENTS = SUBLANES
STEPS = ROW_BLOCK // SEGMENTS
S5_BLOCKS = 4
S5_BLOCK_WIDTH = S5_WIDTH // S5_BLOCKS
CONV_ROWS = 64
CONV_BWD_ROWS = 32
VMEM_LIMIT = 48 * 1024 * 1024
SMALL_ROWS = 320


def _params(sem=None):
    kw = dict(vmem_limit_bytes=VMEM_LIMIT)
    if sem is not None:
        kw["dimension_semantics"] = sem
    return pltpu.CompilerParams(**kw)


def _sds(shape, dtype=F32):
    return jax.ShapeDtypeStruct(tuple(shape), dtype)


def _fold8(x):
    return x.reshape(x.shape[0] // SUBLANES, SUBLANES, x.shape[1]).sum(axis=0)


def _sigmoid(x):
    return 1.0 / (1.0 + jnp.exp(-x))


def _silu(x):
    return x * _sigmoid(x)


def _dsilu(x):
    s = _sigmoid(x)
    return s * (1.0 + x * (1.0 - s))


_GELU_C = 0.7978845608028654


def _gelu(x):
    return 0.5 * x * (1.0 + jnp.tanh(_GELU_C * (x + 0.044715 * x * x * x)))


def _dgelu(x):
    t = jnp.tanh(_GELU_C * (x + 0.044715 * x * x * x))
    return 0.5 * (1.0 + t) + 0.5 * x * (1.0 - t * t) * _GELU_C * (1.0 + 3.0 * 0.044715 * x * x)


def _rms(x):
    rstd = lax.rsqrt(jnp.mean(x * x, axis=-1, keepdims=True) + EPS_RMS)
    return x * rstd, rstd


def _epi_residual_prenorm(acc, res, gate, gain, scale, shift):
    h = res + gate * acc
    xh, _ = _rms(h)
    a = (xh * gain) * (1.0 + scale) + shift
    return acc, h, a, a.T


def _epi_residual_loss(acc, res, gate, target, gain):
    h = res + gate * acc
    xh, rstd = _rms(h)
    err = xh * gain - target
    dy = err * (1.0 / h.shape[-1])
    dxh = dy * gain
    dh = rstd * (dxh - xh * jnp.mean(dxh * xh, axis=-1, keepdims=True))
    return acc, dh, dh * gate, _fold8(err * err), _fold8(dy * xh)


def _epi_norm_bwd(d_act, x, res, aux, gain, scale, gate=None):
    xh, rstd = _rms(x)
    dn = d_act * (1.0 + scale)
    dxh = dn * gain
    dx = res + rstd * (dxh - xh * jnp.mean(dxh * xh, axis=-1, keepdims=True))
    sums = (_fold8(d_act), _fold8(d_act * (xh * gain)), _fold8(dn * xh), _fold8(res * aux))
    return (dx, *sums) if gate is None else (dx, dx * gate, *sums)


def _dot(a, b, mode):
    dims = {"nn": (((1,), (0,)), ((), ())), "nt": (((1,), (1,)), ((), ())), "tn": (((0,), (0,)), ((), ()))}[mode]
    return lax.dot_general(a, b, dims, preferred_element_type=F32)


def _peers(x, y, c):
    out = []
    for k in range(1, NDEV):
        px = 1 - x if k & 4 else x
        py = 1 - y if k & 2 else y
        pc = 1 - c if k & 1 else c
        out.append(((px, py, pc), 4 * px + 2 * py + pc))
    return out


def _exchange_copies(src, land, send_sems, recv_sems, gather):
    x, y, c = lax.axis_index("x"), lax.axis_index("y"), lax.axis_index("c")
    me = 4 * x + 2 * y + c
    out = []
    for a in range(len(src)):
        for k, (peer, plin) in enumerate(_peers(x, y, c)):
            chunk = src[a] if gather[a] else src[a].at[plin]
            sems = dict(send_sem=send_sems.at[a * (NDEV - 1) + k], recv_sem=recv_sems.at[a * (NDEV - 1) + k],
                        device_id=peer, device_id_type=MESH)
            out.append((pltpu.make_async_remote_copy(src_ref=chunk, dst_ref=land[a].at[me], **sems),
                        pltpu.make_async_remote_copy(src_ref=chunk, dst_ref=land[a].at[plin], **sems)))
    return out


def _exchange(name, srcs, gather):
    n = len(srcs)
    outs = [_sds(((NDEV,) + s.shape) if g else s.shape, s.dtype) for s, g in zip(srcs, gather)]

    def body(*refs):
        src, dst, token = refs[:n], refs[n:2 * n], refs[2 * n]
        send_sems, recv_sems, local_sems = refs[2 * n + 1:]
        me = 4 * lax.axis_index("x") + 2 * lax.axis_index("y") + lax.axis_index("c")
        local = [pltpu.make_async_copy(src[a] if gather[a] else src[a].at[me], dst[a].at[me], local_sems.at[a])
                 for a in range(n)]
        for copy in local:
            copy.start()
        copies = _exchange_copies(src, dst, send_sems, recv_sems, gather)
        for copy, _ in copies:
            copy.start()
        token[...] = jnp.zeros_like(token)
        for copy, landing in copies:
            copy.wait_send()
            landing.wait_recv()
        for copy in local:
            copy.wait()

    nsem = n * (NDEV - 1)
    out = pl.pallas_call(
        body, name=name, out_shape=outs + [_sds((SUBLANES, LANES))], in_specs=[ANY] * n,
        out_specs=[ANY] * n + [pl.BlockSpec(memory_space=pltpu.VMEM)],
        scratch_shapes=[pltpu.SemaphoreType.DMA((nsem,)), pltpu.SemaphoreType.DMA((nsem,)), pltpu.SemaphoreType.DMA((n,))],
    )(*srcs)
    return out[:n], out[n]


HBM = pl.BlockSpec(memory_space=pltpu.HBM)
SEM = pl.BlockSpec(memory_space=pltpu.SEMAPHORE)
EFFECT = pltpu.SideEffectType.DATAFLOW_SIDE_EFFECTING


def _exchange_start_groups(name, groups):
    srcs = [s for g_srcs, _ in groups for s in g_srcs]
    gathers = [g for _, g_gather in groups for g in g_gather]
    lands = [lax.empty(((NDEV,) + s.shape) if g else s.shape, s.dtype) for s, g in zip(srcs, gathers)]
    n, ng = len(srcs), len(groups)

    def body(*refs):
        src, land = refs[:n], refs[n:2 * n]
        sems = refs[2 * n:2 * n + 2 * ng]
        token = refs[-1]
        first = 0
        for g, (g_srcs, g_gather) in enumerate(groups):
            last = first + len(g_srcs)
            for copy, _ in _exchange_copies(src[first:last], land[first:last], sems[2 * g], sems[2 * g + 1], g_gather):
                copy.start()
            first = last
        token[...] = jnp.zeros_like(token)

    hbm = lambda v: pltpu.HBM(v.shape, v.dtype)
    sem_shapes = []
    for g_srcs, _ in groups:
        sem_shapes += [pltpu.SemaphoreType.DMA((len(g_srcs) * (NDEV - 1),))] * 2
    out = pl.pallas_call(
        body, name=name,
        out_shape=(*sem_shapes, *[hbm(v) for v in srcs], *[hbm(v) for v in lands], _sds((SUBLANES, LANES))),
        in_specs=[HBM] * (2 * n),
        out_specs=(*([SEM] * (2 * ng)), *([HBM] * (2 * n)), pl.BlockSpec(memory_space=pltpu.VMEM)),
        input_output_aliases={i: 2 * ng + i for i in range(2 * n)},
        compiler_params=pltpu.CompilerParams(has_side_effects=EFFECT),
    )(*[pltpu.with_memory_space_constraint(v, pltpu.HBM) for v in srcs + lands])
    src_out, land_out = out[2 * ng:2 * ng + n], out[2 * ng + n:2 * ng + 2 * n]
    result, first = [], 0
    for g, (g_srcs, _) in enumerate(groups):
        last = first + len(g_srcs)
        result.append((out[2 * g], out[2 * g + 1], src_out[first:last], land_out[first:last]))
        first = last
    return result, out[-1]


def _exchange_start(name, srcs, gather):
    (group,), token = _exchange_start_groups(name, [(srcs, gather)])
    return (*group, token)


def _exchange_wait(name, send_sems, recv_sems, srcs, lands, gather, after):
    n = len(srcs)

    def body(*refs):
        src, land = refs[:n], refs[n:2 * n]
        send_ref, recv_ref = refs[2 * n], refs[2 * n + 1]
        for copy, landing in _exchange_copies(src, land, send_ref, recv_ref, gather):
            copy.wait_send()
            landing.wait_recv()

    hbm = lambda v: pltpu.HBM(v.shape, v.dtype)
    out = pl.pallas_call(
        body, name=name, out_shape=[hbm(v) for v in list(srcs) + list(lands)],
        in_specs=[HBM] * (2 * n) + [SEM, SEM, ANY], out_specs=[HBM] * (2 * n),
        input_output_aliases={i: i for i in range(2 * n)},
        compiler_params=pltpu.CompilerParams(has_side_effects=EFFECT),
    )(*srcs, *lands, send_sems, recv_sems, after)
    return out[:n], out[n:]


def _with_own(landed, own, me):
    return lax.dynamic_update_slice(landed, own[None], (me,) + (0,) * own.ndim)


def _matmul(name, a, b, mode, mnk, tiles, outs, a_spec=None, b_spec=None, a_fn=None, a_extra=(),
            epi=None, epi_extra=(), out_specs=None, b_slabs=1):
    m_, n_, k_ = mnk
    tm, tn, tk = tiles
    nk = k_ // tk
    if a_spec is None:
        a_spec = (pl.BlockSpec((tk, tm), lambda i, j, k: (k, i)) if mode == "tn"
                  else pl.BlockSpec((tm, tk), lambda i, j, k: (i, k)))
    if b_spec is None:
        b_spec = (pl.BlockSpec((tn, tk), lambda i, j, k: (j, k)) if mode == "nt"
                  else pl.BlockSpec((tk, tn), lambda i, j, k: (k, j)))
    if out_specs is None:
        out_specs = [pl.BlockSpec((tm, tn), lambda i, j, k: (i, j)) for _ in outs]
    na, ne, no = len(a_extra), len(epi_extra), len(outs)

    def body(*refs):
        a_ref, b_ref = refs[0], refs[1]
        ax = refs[2:2 + na]
        ex = refs[2 + na:2 + na + ne]
        o = refs[2 + na + ne:2 + na + ne + no]

        def finish(res):
            res = epi(res, *[r[...] for r in ex]) if epi is not None else (res,)
            for ref, val in zip(o, res):
                ref[...] = val.astype(ref.dtype)

        at = a_ref[...]
        if a_fn is not None:
            at = a_fn(at, *[r[...] for r in ax])
        at = at.astype(BF16)
        if b_slabs == 1:
            part = _dot(at, b_ref[...].astype(BF16), mode)
        else:
            ks = tk // b_slabs
            part = _dot(at[:, 0:ks], b_ref[0].astype(BF16), mode)
            for s in range(1, b_slabs):
                part = part + _dot(at[:, s * ks:(s + 1) * ks], b_ref[s].astype(BF16), mode)
        if nk == 1:
            finish(part)
            return
        acc = refs[-1]
        k = pl.program_id(2)

        @pl.when(k == 0)
        def _():
            acc[...] = part

        @pl.when(k > 0)
        def _():
            acc[...] += part

        @pl.when(k == nk - 1)
        def _():
            finish(acc[...])

    return pl.pallas_call(
        body, name=name, grid=(m_ // tm, n_ // tn, nk),
        in_specs=[a_spec, b_spec] + [s for _, s in a_extra] + [s for _, s in epi_extra],
        out_specs=out_specs, out_shape=[_sds(s, d) for s, d in outs],
        scratch_shapes=[pltpu.VMEM((tm, tn), F32)] if nk > 1 else [],
        compiler_params=_params(("parallel", "parallel", "arbitrary")),
    )(a, b, *[x for x, _ in a_extra], *[x for x, _ in epi_extra])


def _prenorm(name, x, ctx, gain, shsc):
    n_lat = x.shape[0] // ROW_BLOCK
    n_ctx = 0 if ctx is None else ctx.shape[0] // ROW_BLOCK
    d = x.shape[1]

    def norm(src, g_ref, m_ref, o_ref):
        xv = src[...]
        xh = xv * lax.rsqrt(jnp.mean(xv * xv, axis=-1, keepdims=True) + EPS_RMS)
        o_ref[...] = ((xh * g_ref[...]) * (1.0 + m_ref[1:2, :]) + m_ref[0:1, :]).astype(o_ref.dtype)

    def body(*refs):
        if ctx is None:
            x_ref, g_ref, m_ref, o_ref = refs
            norm(x_ref, g_ref, m_ref, o_ref)
        else:
            x_ref, c_ref, g_ref, m_ref, o_ref = refs
            i = pl.program_id(0)

            @pl.when(i < n_lat)
            def _():
                norm(x_ref, g_ref, m_ref, o_ref)

            @pl.when(i >= n_lat)
            def _():
                norm(c_ref, g_ref, m_ref, o_ref)

    in_specs = [pl.BlockSpec((ROW_BLOCK, d), lambda i: (jnp.minimum(i, n_lat - 1), 0))]
    args = [x]
    if ctx is not None:
        in_specs.append(pl.BlockSpec((ROW_BLOCK, d), lambda i: (jnp.maximum(i - n_lat, 0), 0)))
        args.append(ctx)
    in_specs += [pl.BlockSpec((1, d), lambda i: (0, 0)),
                 pl.BlockSpec((None, 2, d), lambda i: (jnp.minimum(i // n_lat, 1), 0, 0))]
    args += [gain, shsc]
    return pl.pallas_call(
        body, name=name, grid=(n_lat + n_ctx,), in_specs=in_specs,
        out_specs=pl.BlockSpec((ROW_BLOCK, d), lambda i: (i, 0)),
        out_shape=_sds(((n_lat + n_ctx) * ROW_BLOCK, d), BF16),
        compiler_params=_params(("parallel",)),
    )(*args)


def _norm_bwd(name, x, d_act, d_act_row0, gain, scale, res=None, aux=None):
    rows, d = x.shape
    nb = rows // ROW_BLOCK
    has_res = res is not None

    def body(*refs):
        if has_res:
            x_ref, da_ref, g_ref, sc_ref, r_ref, aux_ref, dx_ref, sums = refs
        else:
            x_ref, da_ref, g_ref, sc_ref, sums = refs
        i = pl.program_id(0)

        @pl.when(i == 0)
        def _():
            sums[...] = jnp.zeros_like(sums)

        xv, da = x_ref[...], da_ref[...]
        rstd = lax.rsqrt(jnp.mean(xv * xv, axis=-1, keepdims=True) + EPS_RMS)
        xh = xv * rstd
        g = g_ref[...]
        dn = da * (1.0 + sc_ref[...])
        sums[0] += _fold8(da)
        sums[1] += _fold8(da * (xh * g))
        sums[2] += _fold8(dn * xh)
        if has_res:
            dxh = dn * g
            dx = rstd * (dxh - xh * jnp.mean(dxh * xh, axis=-1, keepdims=True))
            rv = r_ref[...]
            dx_ref[...] = rv + dx
            sums[3] += _fold8(rv * aux_ref[...])

    row = lambda i: (i, 0)
    vec = pl.BlockSpec((1, d), lambda i: (0, 0))
    in_specs = [pl.BlockSpec((ROW_BLOCK, d), row), pl.BlockSpec((ROW_BLOCK, d), lambda i: (i + d_act_row0, 0)), vec, vec]
    args = [x, d_act, gain, scale]
    out_shape = [_sds((4, SUBLANES, d))]
    out_specs = [pl.BlockSpec((4, SUBLANES, d), lambda i: (0, 0, 0))]
    if has_res:
        in_specs += [pl.BlockSpec((ROW_BLOCK, d), row), pl.BlockSpec((ROW_BLOCK, d), row)]
        args += [res, aux]
        out_shape = [_sds((rows, d))] + out_shape
        out_specs = [pl.BlockSpec((ROW_BLOCK, d), row)] + out_specs
    return pl.pallas_call(
        body, name=name, grid=(nb,), in_specs=in_specs, out_specs=out_specs, out_shape=out_shape,
        compiler_params=_params(("arbitrary",)),
    )(*args)


def _ada_fwd(cond16, ada_w_loc, ada_b_loc):
    cols = ada_w_loc.shape[1]

    def body(c_ref, w_ref, b_ref, o_ref):
        s = _silu(c_ref[...]).astype(BF16)
        o_ref[...] = _dot(s, w_ref[...].astype(BF16), "nn") + b_ref[...]

    return pl.pallas_call(body, name="ada_fwd", out_shape=_sds((16, cols)), compiler_params=_params())(
        cond16, ada_w_loc, ada_b_loc)


def _ada_bwd(cond16, dmod16, ada_w_loc, c_ctx_row):
    k_, cols = ada_w_loc.shape

    def body(c_ref, dm_ref, w_ref, cc_ref, gw_ref, gc_ref):
        s = _silu(c_ref[...]).astype(BF16)
        dm = dm_ref[...]
        gw_ref[...] = _dot(s, dm.astype(BF16), "tn")
        dmc = jnp.sum(dm[8:16, :], axis=0, keepdims=True)
        dmc8 = jnp.broadcast_to(dmc, (SUBLANES, cols)).astype(BF16)
        ds = _dot(dmc8, w_ref[...].astype(BF16), "nt")
        row = lax.broadcasted_iota(jnp.int32, ds.shape, 0)
        gc_ref[...] = jnp.where(row == 0, ds * _dsilu(cc_ref[...]), 0.0)

    return pl.pallas_call(body, name="ada_bwd", out_shape=[_sds((k_, cols)), _sds((SUBLANES, k_))],
                          compiler_params=_params())(cond16, dmod16, ada_w_loc, c_ctx_row)


def _cmul(a, b):
    return a[0] * b[0] - a[1] * b[1], a[0] * b[1] + a[1] * b[0]


def _disc(lam_re, lam_im, ldt):
    dt = jnp.exp(ldt)
    mag = jnp.exp(lam_re * dt)
    th = lam_im * dt
    a_re, a_im = mag * jnp.cos(th), mag * jnp.sin(th)
    den = lam_re * lam_re + lam_im * lam_im
    n_re = a_re - 1.0
    f_re = (n_re * lam_re + a_im * lam_im) / den
    f_im = (a_im * lam_re - n_re * lam_im) / den
    return dt, mag, th, a_re, a_im, den, n_re, f_re, f_im


def _block_diag_mask(shape):
    row = lax.broadcasted_iota(jnp.int32, shape, 0)
    col = lax.broadcasted_iota(jnp.int32, shape, 1)
    return lax.shift_right_logical(row, 4) == lax.shift_right_logical(col, 6)


TAB_A = 0
TAB_BIG = 1
TAB_SEG = 4
TAB_PW = 5
TAB_ROWS = TAB_PW + STEPS


def _s5_discretise(name, ascending, lam_re, lam_im, ldt, bt_re, bt_im, ct_re, ct_im):
    def write_tables(ref, pw, big, asc, sign):
        row = lax.broadcasted_iota(jnp.int32, (SUBLANES, NSTATE), 0)
        full = lambda v: jnp.broadcast_to(v, (SUBLANES, NSTATE))

        def put(t, p):
            ref[0, t] = full(p[0])
            ref[1, t] = full(sign * p[1])

        put(TAB_A, pw[0])
        for t in range(3):
            put(TAB_BIG + t, big[t])
        seg = [big[0]]
        for _ in range(SEGMENTS - 1):
            seg.append(_cmul(seg[-1], big[0]))
        seg_re = jnp.zeros((SUBLANES, NSTATE), F32)
        seg_im = jnp.zeros((SUBLANES, NSTATE), F32)
        for r in range(SEGMENTS):
            p = seg[r] if asc else seg[SEGMENTS - 1 - r]
            seg_re = jnp.where(row == r, p[0], seg_re)
            seg_im = jnp.where(row == r, sign * p[1], seg_im)
        ref[0, TAB_SEG] = seg_re
        ref[1, TAB_SEG] = seg_im
        for k in range(STEPS):
            put(TAB_PW + k, pw[k])

    def body(lr_ref, li_ref, ldt_ref, br_ref, bi_ref, cr_ref, ci_ref, bb_ref, tab_ref, adj_ref, bm_ref, cm_ref):
        _, _, _, a_re, a_im, _, _, f_re, f_im = _disc(lr_ref[...], li_ref[...], ldt_ref[...])
        bre, bim = br_ref[...], bi_ref[...]
        bb_re = f_re * bre - f_im * bim
        bb_im = f_re * bim + f_im * bre
        bb_ref[0:S5_GROUP, :] = bb_re
        bb_ref[S5_GROUP:2 * S5_GROUP, :] = bb_im
        pw = [(a_re, a_im)]
        for _ in range(STEPS - 1):
            pw.append(_cmul(pw[-1], (a_re, a_im)))
        big = [pw[STEPS - 1]]
        for _ in range(2):
            big.append(_cmul(big[-1], big[-1]))
        write_tables(tab_ref, pw, big, ascending, 1.0)
        write_tables(adj_ref, pw, big, not ascending, -1.0)
        half = NSTATE // S5_BLOCKS
        mask = _block_diag_mask((S5_BLOCK_WIDTH, half))
        tile = lambda v: jnp.broadcast_to(v[None], (S5_BLOCK_WIDTH // S5_GROUP, S5_GROUP, half)).reshape(S5_BLOCK_WIDTH, half)
        for c in range(S5_BLOCKS):
            cols = slice(c * half, (c + 1) * half)
            rows = slice(c * S5_BLOCK_WIDTH, (c + 1) * S5_BLOCK_WIDTH)
            bm_ref[c, :, 0:half] = jnp.where(mask, tile(bb_re[:, cols]), 0.0).astype(BF16)
            bm_ref[c, :, half:2 * half] = jnp.where(mask, tile(bb_im[:, cols]), 0.0).astype(BF16)
            cm_ref[c, :, 0:half] = jnp.where(mask, cr_ref[rows, :], 0.0).astype(BF16)
            cm_ref[c, :, half:2 * half] = jnp.where(mask, -ci_ref[rows, :], 0.0).astype(BF16)

    blocked = _sds((S5_BLOCKS, S5_BLOCK_WIDTH, 2 * NSTATE // S5_BLOCKS), BF16)
    return pl.pallas_call(
        body, name=name,
        out_shape=[_sds((2 * S5_GROUP, NSTATE)), _sds((2, TAB_ROWS, SUBLANES, NSTATE)),
                   _sds((2, TAB_ROWS, SUBLANES, NSTATE)), blocked, blocked],
        compiler_params=_params(),
    )(lam_re, lam_im, ldt, bt_re, bt_im, ct_re, ct_im)


def _s5_discretise_bwd(name, lam_re, lam_im, ldt, bt_re, bt_im, d_abar8, d_bbar):
    def body(lr_ref, li_ref, ldt_ref, br_ref, bi_ref, da_ref, db_ref, dl_ref, dbt_ref):
        lam_re, lam_im = lr_ref[...], li_ref[...]
        dt, mag, _, a_re, a_im, den, n_re, f_re, f_im = _disc(lam_re, lam_im, ldt_ref[...])
        bre, bim = br_ref[...], bi_ref[...]
        dbr, dbi = db_ref[0:S5_GROUP, :], db_ref[S5_GROUP:2 * S5_GROUP, :]
        dbt_ref[0:S5_GROUP, :] = f_re * dbr + f_im * dbi
        dbt_ref[S5_GROUP:2 * S5_GROUP, :] = f_re * dbi - f_im * dbr
        df_re = jnp.sum(bre * dbr + bim * dbi, axis=0, keepdims=True)
        df_im = jnp.sum(bre * dbi - bim * dbr, axis=0, keepdims=True)
        da = da_ref[...]
        da_re = jnp.sum(da[:, 0:NSTATE], axis=0, keepdims=True)
        da_im = jnp.sum(da[:, NSTATE:2 * NSTATE], axis=0, keepdims=True)
        da_re = da_re + (df_re * lam_re - df_im * lam_im) / den
        da_im = da_im + (df_re * lam_im + df_im * lam_re) / den
        ff = (f_re * df_re + f_im * df_im) * 2.0 / den
        d_lr = (df_re * n_re + df_im * a_im) / den - ff * lam_re
        d_li = (df_re * a_im - df_im * n_re) / den - ff * lam_im
        d_mag = (da_re * a_re + da_im * a_im) / mag
        d_th = da_im * a_re - da_re * a_im
        d_lr = d_lr + d_mag * mag * dt
        d_li = d_li + d_th * dt
        d_ldt = (d_mag * mag * lam_re + d_th * lam_im) * dt
        row = lax.broadcasted_iota(jnp.int32, (SUBLANES, NSTATE), 0)
        dl_ref[...] = jnp.where(row == 0, d_lr, jnp.where(row == 1, d_li, jnp.where(row == 2, d_ldt, 0.0)))

    return pl.pallas_call(
        body, name=name, out_shape=[_sds((SUBLANES, NSTATE)), _sds((2 * S5_GROUP, NSTATE))],
        compiler_params=_params(),
    )(lam_re, lam_im, ldt, bt_re, bt_im, d_abar8, d_bbar)


def _segment_permutation(reverse_time):
    rho = jnp.arange(ROW_BLOCK)
    src = STEPS * (rho % SEGMENTS) + rho // SEGMENTS
    if reverse_time:
        src = ROW_BLOCK - 1 - src
    return (src[:, None] == jnp.arange(ROW_BLOCK)[None, :]).astype(BF16)


def _permute_rows(perm_ref, v):
    return _dot(perm_ref[...], v, "nn").astype(BF16)


def _unpermute_rows(perm_t_ref, v):
    hi = v.astype(BF16)
    lo = (v - hi.astype(F32)).astype(BF16)
    return _dot(perm_t_ref[...], hi, "nn") + _dot(perm_t_ref[...], lo, "nn")


def _unrolled_loop(step, init):
    def trip(o, state):
        for u in range(SCAN_UNROLL):
            state = step(o * SCAN_UNROLL + u, state)
        return state

    if SCAN_UNROLL == STEPS:
        return trip(0, init)
    return lax.fori_loop(0, STEPS // SCAN_UNROLL, trip, init)


def _scan_chunk(x_ref, out_ref, tab_ref, carry_re, carry_im, ascending, pair_ref=None, acc_ref=None, lane_chunks=None):
    w = SCAN_LANES
    half = NSTATE // S5_BLOCKS
    row = lax.broadcasted_iota(jnp.int32, (SUBLANES, w), 0)
    last = (SEGMENTS - 1) if ascending else 0

    def from_previous_segment(v, k, fill):
        if ascending:
            return jnp.where(row >= k, pltpu.roll(v, k, 0), fill)
        return jnp.where(row < SEGMENTS - k, pltpu.roll(v, SEGMENTS - k, 0), fill)

    def tile_rows(k):
        return pl.ds(pl.multiple_of((k if ascending else STEPS - 1 - k) * SUBLANES, SUBLANES), SUBLANES)

    for j in (range(NSTATE // w) if lane_chunks is None else lane_chunks):
        n_l = pl.ds(j * w, w)
        lane0 = (j * w // half) * 2 * half + (j * w) % half
        re_l, im_l = pl.ds(lane0, w), pl.ds(lane0 + half, w)
        tab = lambda t, n_l=n_l: (tab_ref[0, t, :, n_l], tab_ref[1, t, :, n_l])
        a_re, a_im = tab(TAB_A)

        def local_step(k, h):
            rs = tile_rows(k)
            h_re = a_re * h[0] - a_im * h[1] + x_ref[rs, re_l]
            h_im = a_re * h[1] + a_im * h[0] + x_ref[rs, im_l]
            out_ref[rs, re_l] = h_re
            out_ref[rs, im_l] = h_im
            return h_re, h_im

        zero = jnp.zeros((SUBLANES, w), F32)
        end_re, end_im = _unrolled_loop(local_step, (zero, zero))
        for t, k in ((TAB_BIG, 1), (TAB_BIG + 1, 2), (TAB_BIG + 2, 4)):
            p_re, p_im = tab(t)
            s_re, s_im = from_previous_segment(end_re, k, 0.0), from_previous_segment(end_im, k, 0.0)
            end_re, end_im = end_re + (p_re * s_re - p_im * s_im), end_im + (p_re * s_im + p_im * s_re)
        c0_re, c0_im = carry_re[:, n_l], carry_im[:, n_l]
        p_re, p_im = tab(TAB_SEG)
        end_re = end_re + (p_re * c0_re - p_im * c0_im)
        end_im = end_im + (p_re * c0_im + p_im * c0_re)
        carry_re[:, n_l] = jnp.broadcast_to(end_re[last:last + 1, :], end_re.shape)
        carry_im[:, n_l] = jnp.broadcast_to(end_im[last:last + 1, :], end_im.shape)
        in_re = from_previous_segment(end_re, 1, c0_re)
        in_im = from_previous_segment(end_im, 1, c0_im)

        def carry_step(k, st):
            rs = tile_rows(k)
            p_re, p_im = tab_ref[0, TAB_PW + k, :, n_l], tab_ref[1, TAB_PW + k, :, n_l]
            o_re = out_ref[rs, re_l] + (p_re * in_re - p_im * in_im)
            o_im = out_ref[rs, im_l] + (p_re * in_im + p_im * in_re)
            out_ref[rs, re_l] = o_re
            out_ref[rs, im_l] = o_im
            if pair_ref is None:
                return st
            s_re, s_im = pair_ref[rs, re_l], pair_ref[rs, im_l]
            return (o_re, o_im, st[2] + (st[0] * s_re + st[1] * s_im), st[3] + (st[1] * s_re - st[0] * s_im))

        if pair_ref is None:
            _unrolled_loop(carry_step, 0)
        else:
            fin = _unrolled_loop(carry_step, (in_re, in_im, zero, zero))
            acc_ref[:, n_l] += fin[2]
            acc_ref[:, pl.ds(NSTATE + j * w, w)] += fin[3]


def _scan_block_index(i, n_lat, ctx_first_then_ascending):
    if ctx_first_then_ascending:
        return jnp.where(i == 0, n_lat, i - 1)
    return jnp.where(i == 0, n_lat, n_lat - i)


def _full_spec(shape):
    return pl.BlockSpec(shape, lambda i: (0,) * len(shape))


_S5_BLOCKED = (S5_BLOCKS, S5_BLOCK_WIDTH, 2 * NSTATE // S5_BLOCKS)
_S5_TABLES = (2, TAB_ROWS, SUBLANES, NSTATE)
_S5_DIAG = (S5_BLOCKS, S5_GROUP, 2 * NSTATE // S5_BLOCKS)


def _s5_scan_fwd(name, ascending, z_all, bmat, cmat, tab, perm, perm_t):
    rows = z_all.shape[0]
    nb = rows // ROW_BLOCK
    n_lat = nb - 1
    bw, sw = S5_BLOCK_WIDTH, 2 * NSTATE // S5_BLOCKS

    def body(u_ref, bm_ref, cm_ref, tab_ref, p_ref, pt_ref, s_ref, y_ref, bu, yp, carry_re, carry_im):
        @pl.when(pl.program_id(0) == 0)
        def _():
            carry_re[...] = jnp.zeros_like(carry_re)
            carry_im[...] = jnp.zeros_like(carry_im)

        up = _permute_rows(p_ref, u_ref[...].astype(BF16))
        for c in range(S5_BLOCKS):
            bu[:, c * sw:(c + 1) * sw] = _dot(up[:, c * bw:(c + 1) * bw], bm_ref[c], "nn")
        _scan_chunk(bu, s_ref, tab_ref, carry_re, carry_im, False)
        for c in range(S5_BLOCKS):
            yp[:, c * bw:(c + 1) * bw] = _dot(s_ref[:, c * sw:(c + 1) * sw].astype(BF16), cm_ref[c], "nt")
        y_ref[...] = _unpermute_rows(pt_ref, yp[...])

    blk = lambda i: (_scan_block_index(i, n_lat, ascending), 0)
    return pl.pallas_call(
        body, name=name, grid=(nb,),
        in_specs=[pl.BlockSpec((ROW_BLOCK, S5_WIDTH), blk), _full_spec(_S5_BLOCKED), _full_spec(_S5_BLOCKED),
                  _full_spec(_S5_TABLES), _full_spec((ROW_BLOCK, ROW_BLOCK)), _full_spec((ROW_BLOCK, ROW_BLOCK))],
        out_specs=[pl.BlockSpec((ROW_BLOCK, 2 * NSTATE), blk), pl.BlockSpec((ROW_BLOCK, S5_WIDTH), blk)],
        out_shape=[_sds((rows, 2 * NSTATE)), _sds((rows, S5_WIDTH))],
        scratch_shapes=[pltpu.VMEM((ROW_BLOCK, 2 * NSTATE), F32), pltpu.VMEM((ROW_BLOCK, S5_WIDTH), F32),
                        pltpu.VMEM((SUBLANES, NSTATE), F32), pltpu.VMEM((SUBLANES, NSTATE), F32)],
        compiler_params=_params(("arbitrary",)),
    )(z_all, bmat, cmat, tab, perm, perm_t)


def _s5_scan_bwd(name, ascending, dy, z_all, states, bmat, cmat, adj, perm, perm_t):
    rows = states.shape[0]
    nb = rows // ROW_BLOCK
    n_lat = nb - 1
    bw, sw = S5_BLOCK_WIDTH, 2 * NSTATE // S5_BLOCKS

    def block_index(i):
        if ascending:
            return jnp.where(i == nb - 1, n_lat, n_lat - 1 - i)
        return jnp.where(i == nb - 1, n_lat, i)

    def body(dy_ref, u_ref, s_ref, bm_ref, cm_ref, adj_ref, p_ref, pt_ref, du_ref, db_ref, dc_ref, da_ref,
             g, dup, db_acc, dc_acc, carry_re, carry_im):
        i = pl.program_id(0)

        @pl.when(i == 0)
        def _():
            carry_re[...] = jnp.zeros_like(carry_re)
            carry_im[...] = jnp.zeros_like(carry_im)
            da_ref[...] = jnp.zeros_like(da_ref)
            db_acc[...] = jnp.zeros_like(db_acc)
            dc_acc[...] = jnp.zeros_like(dc_acc)

        has_dy = (i < nb - 1).astype(F32)
        dyp = _permute_rows(p_ref, (dy_ref[...] * has_dy).astype(BF16))
        up = _permute_rows(p_ref, u_ref[...].astype(BF16))
        for c in range(S5_BLOCKS):
            g[:, c * sw:(c + 1) * sw] = _dot(dyp[:, c * bw:(c + 1) * bw], cm_ref[c], "nn")
            dc_acc[c] += _dot(dyp[:, c * bw:(c + 1) * bw], s_ref[:, c * sw:(c + 1) * sw].astype(BF16), "tn")
            _scan_chunk(g, g, adj_ref, carry_re, carry_im, True, pair_ref=s_ref, acc_ref=da_ref, lane_chunks=[c])
            gc = g[:, c * sw:(c + 1) * sw].astype(BF16)
            dup[:, c * bw:(c + 1) * bw] = _dot(gc, bm_ref[c], "nt")
            db_acc[c] += _dot(up[:, c * bw:(c + 1) * bw], gc, "tn")
        du_ref[...] = _unpermute_rows(pt_ref, dup[...])

        @pl.when(i == nb - 1)
        def _():
            mask = _block_diag_mask((bw, sw // 2))
            for acc, out in ((db_acc, db_ref), (dc_acc, dc_ref)):
                for c in range(S5_BLOCKS):
                    for part in range(2):
                        cols = slice(part * (sw // 2), (part + 1) * (sw // 2))
                        kept = jnp.where(mask, acc[c, :, cols], 0.0)
                        out[c, :, cols] = kept.reshape(bw // S5_GROUP, S5_GROUP, sw // 2).sum(axis=0)

    blk = lambda i: (block_index(i), 0)
    return pl.pallas_call(
        body, name=name, grid=(nb,),
        in_specs=[pl.BlockSpec((ROW_BLOCK, S5_WIDTH), lambda i: (jnp.minimum(block_index(i), n_lat - 1), 0)),
                  pl.BlockSpec((ROW_BLOCK, S5_WIDTH), blk), pl.BlockSpec((ROW_BLOCK, 2 * NSTATE), blk),
                  _full_spec(_S5_BLOCKED), _full_spec(_S5_BLOCKED), _full_spec(_S5_TABLES),
                  _full_spec((ROW_BLOCK, ROW_BLOCK)), _full_spec((ROW_BLOCK, ROW_BLOCK))],
        out_specs=[pl.BlockSpec((ROW_BLOCK, S5_WIDTH), blk), _full_spec(_S5_DIAG), _full_spec(_S5_DIAG),
                   _full_spec((SUBLANES, 2 * NSTATE))],
        out_shape=[_sds((rows, S5_WIDTH)), _sds(_S5_DIAG), _sds(_S5_DIAG), _sds((SUBLANES, 2 * NSTATE))],
        scratch_shapes=[pltpu.VMEM((ROW_BLOCK, 2 * NSTATE), F32), pltpu.VMEM((ROW_BLOCK, S5_WIDTH), F32),
                        pltpu.VMEM(_S5_BLOCKED, F32), pltpu.VMEM(_S5_BLOCKED, F32),
                        pltpu.VMEM((SUBLANES, NSTATE), F32), pltpu.VMEM((SUBLANES, NSTATE), F32)],
        compiler_params=_params(("arbitrary",)),
    )(dy, z_all, states, bmat, cmat, adj, perm, perm_t)


def _glu_fwd(z_all, y0, y1, d_skip, w_glu, n_rows):
    def body(u_ref, y0_ref, y1_ref, d_ref, w_ref, o_ref):
        y = d_ref[...] * u_ref[...] + y0_ref[...] + y1_ref[...]
        g = _gelu(y)
        t = _dot(g.astype(BF16), w_ref[...], "nn")
        o_ref[...] = (g * _sigmoid(t)).astype(o_ref.dtype)

    row = pl.BlockSpec((ROW_BLOCK, S5_WIDTH), lambda i: (i, 0))
    return pl.pallas_call(
        body, name="glu_fwd", grid=(n_rows // ROW_BLOCK,),
        in_specs=[row, row, row, pl.BlockSpec((1, S5_WIDTH), lambda i: (0, 0)),
                  pl.BlockSpec((S5_WIDTH, S5_WIDTH), lambda i: (0, 0))],
        out_specs=row, out_shape=_sds((n_rows, S5_WIDTH + CONV_WIDTH), BF16), compiler_params=_params(("parallel",)),
    )(z_all, y0, y1, d_skip, w_glu)


def _glu_bwd(d_ycat, z_all, y0, y1, d_skip, w_glu, n_rows):
    def body(do_ref, u_ref, y0_ref, y1_ref, d_ref, w_ref, dy_ref, dw_ref, dd_ref):
        @pl.when(pl.program_id(0) == 0)
        def _():
            dw_ref[...] = jnp.zeros_like(dw_ref)
            dd_ref[...] = jnp.zeros_like(dd_ref)

        u = u_ref[...]
        y = d_ref[...] * u + y0_ref[...] + y1_ref[...]
        g = _gelu(y)
        gb = g.astype(BF16)
        w = w_ref[...]
        sg = _sigmoid(_dot(gb, w, "nn"))
        do = do_ref[...]
        dt = do * g * sg * (1.0 - sg)
        dtb = dt.astype(BF16)
        dg = do * sg + _dot(dtb, w, "nt")
        dy = dg * _dgelu(y)
        dy_ref[...] = dy
        dw_ref[...] += _dot(gb, dtb, "tn")
        dd_ref[...] += _fold8(dy * u)

    row = pl.BlockSpec((ROW_BLOCK, S5_WIDTH), lambda i: (i, 0))
    sq = pl.BlockSpec((S5_WIDTH, S5_WIDTH), lambda i: (0, 0))
    return pl.pallas_call(
        body, name="glu_bwd", grid=(n_rows // ROW_BLOCK,),
        in_specs=[row, row, row, row, pl.BlockSpec((1, S5_WIDTH), lambda i: (0, 0)), sq],
        out_specs=[row, sq, pl.BlockSpec((SUBLANES, S5_WIDTH), lambda i: (0, 0))],
        out_shape=[_sds((n_rows, S5_WIDTH)), _sds((S5_WIDTH, S5_WIDTH)), _sds((SUBLANES, S5_WIDTH))],
        compiler_params=_params(("arbitrary",)),
    )(d_ycat, z_all, y0, y1, d_skip, w_glu)


CONV_HALF = CONV_K // 2


def _conv_block(n_rows):
    blk = min(1024, n_rows)
    assert blk >= CONV_HALF * GRID_W and n_rows % blk == 0
    return blk


def _conv_gate(z_all, n_rows):
    blk = _conv_block(n_rows)
    nb = n_rows // blk

    def body(v_ref, g_ref, o_ref):
        i = pl.program_id(0)
        inside = jnp.logical_and(i >= 1, i <= nb)

        @pl.when(inside)
        def _():
            o_ref[...] = v_ref[...] * _sigmoid(g_ref[...])

        @pl.when(jnp.logical_not(inside))
        def _():
            o_ref[...] = jnp.zeros_like(o_ref)

    src = lambda col: pl.BlockSpec((blk, CONV_WIDTH), lambda i: (jnp.clip(i - 1, 0, nb - 1), col))
    return pl.pallas_call(
        body, name="conv_gate", grid=(nb + 2,), in_specs=[src(1), src(2)],
        out_specs=pl.BlockSpec((blk, CONV_WIDTH), lambda i: (i, 0)),
        out_shape=_sds(((nb + 2) * blk, CONV_WIDTH)), compiler_params=_params(("parallel",)),
    )(z_all, z_all)


def _stream_padded(pad_ref, buf, sems, blk, n_blocks):
    i = pl.program_id(0)

    def copy(b):
        rows = pl.ds(pl.multiple_of(b * blk, blk), blk)
        return pltpu.make_async_copy(pad_ref.at[rows, :], buf.at[rows, :], sems.at[b])

    @pl.when(i == 0)
    def _():
        for b in range(n_blocks):
            copy(b).start()
        copy(0).wait()
        copy(1).wait()

    copy(i + 2).wait()
    return pl.multiple_of(i * blk, blk)


def _conv_fwd(hh_pad, w, b, ln_g, ln_b, ycat, n_rows):
    blk = _conv_block(n_rows)
    nblk = n_rows // blk + 2

    def body(hh_ref, w_ref, b_ref, g_ref, lb_ref, ycat_ref, hc_ref, y_ref, win, sems):
        base = _stream_padded(hh_ref, win, sems, blk, nblk)

        def tile(t, _):
            r0 = pl.multiple_of(t * CONV_ROWS, CONV_ROWS)
            acc = jnp.zeros((CONV_ROWS, CONV_WIDTH), F32)
            for k in range(CONV_K):
                acc = acc + w_ref[k:k + 1, :] * win[pl.ds(base + r0 + blk + (k - CONV_HALF) * GRID_W, CONV_ROWS), :]
            hc = acc + b_ref[...]
            hc_ref[pl.ds(r0, CONV_ROWS), :] = hc
            mu = jnp.mean(hc, axis=-1, keepdims=True)
            xc = hc - mu
            ln = xc * lax.rsqrt(jnp.mean(xc * xc, axis=-1, keepdims=True) + EPS_LN) * g_ref[...] + lb_ref[...]
            y_ref[pl.ds(r0, CONV_ROWS), :] = _silu(ln).astype(y_ref.dtype)
            return 0

        lax.fori_loop(0, blk // CONV_ROWS, tile, 0)

    vec = pl.BlockSpec((1, CONV_WIDTH), lambda i: (0, 0))
    row = pl.BlockSpec((blk, CONV_WIDTH), lambda i: (i, 0))
    return pl.pallas_call(
        body, name="conv_fwd", grid=(n_rows // blk,),
        in_specs=[ANY, pl.BlockSpec((CONV_K, CONV_WIDTH), lambda i: (0, 0)), vec, vec, vec, ANY],
        out_specs=[row, pl.BlockSpec((blk, CONV_WIDTH), lambda i: (i, 1))],
        out_shape=[_sds((n_rows, CONV_WIDTH)), _sds(ycat.shape, ycat.dtype)], input_output_aliases={5: 1},
        scratch_shapes=[pltpu.VMEM((nblk * blk, CONV_WIDTH), F32), pltpu.SemaphoreType.DMA((nblk,))],
        compiler_params=_params(("arbitrary",)),
    )(hh_pad, w, b, ln_g, ln_b, ycat)


def _conv_bwd_norm(d_ycat, hc, ln_g, ln_b, n_rows):
    blk = _conv_block(n_rows)
    nb = n_rows // blk

    def body(dy_ref, hc_ref, g_ref, lb_ref, o_ref, sums):
        i = pl.program_id(0)

        @pl.when(i == 0)
        def _():
            sums[...] = jnp.zeros_like(sums)

        inside = jnp.logical_and(i >= 1, i <= nb)

        @pl.when(inside)
        def _():
            hcv = hc_ref[...]
            mu = jnp.mean(hcv, axis=-1, keepdims=True)
            xc = hcv - mu
            rstd = lax.rsqrt(jnp.mean(xc * xc, axis=-1, keepdims=True) + EPS_LN)
            xh = xc * rstd
            g = g_ref[...]
            dln = dy_ref[...] * _dsilu(xh * g + lb_ref[...])
            dxh = dln * g
            dhc = rstd * (dxh - jnp.mean(dxh, axis=-1, keepdims=True) - xh * jnp.mean(dxh * xh, axis=-1, keepdims=True))
            o_ref[...] = dhc
            sums[0] += _fold8(dhc)
            sums[1] += _fold8(dln * xh)
            sums[2] += _fold8(dln)

        @pl.when(jnp.logical_not(inside))
        def _():
            o_ref[...] = jnp.zeros_like(o_ref)

    vec = pl.BlockSpec((1, CONV_WIDTH), lambda i: (0, 0))
    return pl.pallas_call(
        body, name="conv_bwd_norm", grid=(nb + 2,),
        in_specs=[pl.BlockSpec((blk, CONV_WIDTH), lambda i: (jnp.clip(i - 1, 0, nb - 1), 1)),
                  pl.BlockSpec((blk, CONV_WIDTH), lambda i: (jnp.clip(i - 1, 0, nb - 1), 0)), vec, vec],
        out_specs=[pl.BlockSpec((blk, CONV_WIDTH), lambda i: (i, 0)),
                   pl.BlockSpec((3, SUBLANES, CONV_WIDTH), lambda i: (0, 0, 0))],
        out_shape=[_sds(((nb + 2) * blk, CONV_WIDTH)), _sds((3, SUBLANES, CONV_WIDTH))],
        compiler_params=_params(("arbitrary",)),
    )(d_ycat, hc, ln_g, ln_b)


def _conv_bwd_taps(dhc_pad, hh_pad, z_all, w, n_rows):
    blk = _conv_block(n_rows)
    nblk = n_rows // blk + 2

    def body(dhc_ref, hh_ref, v_ref, g_ref, w_ref, dv_ref, dg_ref, dw_ref, dwin, hwin, dsems, hsems):
        @pl.when(pl.program_id(0) == 0)
        def _():
            dw_ref[...] = jnp.zeros_like(dw_ref)

        base = _stream_padded(dhc_ref, dwin, dsems, blk, nblk)
        _stream_padded(hh_ref, hwin, hsems, blk, nblk)

        def tile(t, _):
            r0 = pl.multiple_of(t * CONV_BWD_ROWS, CONV_BWD_ROWS) + base
            dh = dwin[pl.ds(r0 + blk, CONV_BWD_ROWS), :]
            acc = jnp.zeros((CONV_BWD_ROWS, CONV_WIDTH), F32)
            for k in range(CONV_K):
                off = (k - CONV_HALF) * GRID_W
                acc = acc + w_ref[k:k + 1, :] * dwin[pl.ds(r0 + blk - off, CONV_BWD_ROWS), :]
                dw_ref[k] += _fold8(dh * hwin[pl.ds(r0 + blk + off, CONV_BWD_ROWS), :])
            rs = pl.ds(pl.multiple_of(t * CONV_BWD_ROWS, CONV_BWD_ROWS), CONV_BWD_ROWS)
            sg = _sigmoid(g_ref[rs, :])
            vv = v_ref[rs, :]
            dv_ref[rs, :] = acc * sg
            dg_ref[rs, :] = acc * vv * sg * (1.0 - sg)
            return 0

        lax.fori_loop(0, blk // CONV_BWD_ROWS, tile, 0)

    row = pl.BlockSpec((blk, CONV_WIDTH), lambda i: (i, 0))
    return pl.pallas_call(
        body, name="conv_bwd_taps", grid=(n_rows // blk,),
        in_specs=[ANY, ANY,
            pl.BlockSpec((blk, CONV_WIDTH), lambda i: (i, 1)), pl.BlockSpec((blk, CONV_WIDTH), lambda i: (i, 2)),
            pl.BlockSpec((CONV_K, CONV_WIDTH), lambda i: (0, 0))],
        out_specs=[row, row, pl.BlockSpec((CONV_K, SUBLANES, CONV_WIDTH), lambda i: (0, 0, 0))],
        out_shape=[_sds((n_rows, CONV_WIDTH)), _sds((n_rows, CONV_WIDTH)), _sds((CONV_K, SUBLANES, CONV_WIDTH))],
        scratch_shapes=[pltpu.VMEM((nblk * blk, CONV_WIDTH), F32), pltpu.VMEM((nblk * blk, CONV_WIDTH), F32),
                        pltpu.SemaphoreType.DMA((nblk,)), pltpu.SemaphoreType.DMA((nblk,))],
        compiler_params=_params(("arbitrary",)),
    )(dhc_pad, hh_pad, z_all, z_all, w)


def _dz_assemble(du0, du1, dy, d_skip, dv, dgate, n_lat):
    rows = du0.shape[0]
    nb = rows // ROW_BLOCK

    w = S5_WIDTH

    def body(a_ref, b_ref, dy_ref, d_ref, dv_ref, dg_ref, o_ref):
        lat = pl.program_id(0) < n_lat

        @pl.when(lat)
        def _():
            o_ref[:, 0:w] = (a_ref[...] + b_ref[...] + dy_ref[...] * d_ref[...]).astype(o_ref.dtype)
            o_ref[:, w:2 * w] = dv_ref[...].astype(o_ref.dtype)
            o_ref[:, 2 * w:3 * w] = dg_ref[...].astype(o_ref.dtype)

        @pl.when(jnp.logical_not(lat))
        def _():
            o_ref[:, 0:w] = (a_ref[...] + b_ref[...]).astype(o_ref.dtype)
            o_ref[:, w:3 * w] = jnp.zeros((ROW_BLOCK, 2 * w), o_ref.dtype)

    all_rows = pl.BlockSpec((ROW_BLOCK, w), lambda i: (i, 0))
    lat_rows = pl.BlockSpec((ROW_BLOCK, w), lambda i: (jnp.minimum(i, n_lat - 1), 0))
    return pl.pallas_call(
        body, name="dz_assemble", grid=(nb,),
        in_specs=[all_rows, all_rows, lat_rows, pl.BlockSpec((1, w), lambda i: (0, 0)), lat_rows, lat_rows],
        out_specs=pl.BlockSpec((ROW_BLOCK, IN_COLS), lambda i: (i, 0)),
        out_shape=_sds((rows, IN_COLS), BF16), compiler_params=_params(("parallel",)),
    )(du0, du1, dy, d_skip, dv, dgate)


def _sum_parts(parts):
    _, r, c = parts.shape

    def body(p_ref, o_ref):
        acc = p_ref[0]
        for q in range(1, NDEV):
            acc = acc + p_ref[q]
        o_ref[...] = acc

    return pl.pallas_call(body, name="sum_parts", out_shape=_sds((r, c)), compiler_params=_params())(parts)


def _row_tile(r, c):
    best = r
    for t in (1024, 512, 256, 128, 64, 32, 16, 8):
        if r % t == 0 and t * c <= 128 * 1024:
            return t
    return best


def _adamw(name, w, gparts, m, v):
    r, c = w.shape
    np_ = gparts.shape[0]
    tr = _row_tile(r, c)

    def body(w_ref, g_ref, m_ref, v_ref, go_ref, d_ref, mo_ref, vo_ref):
        g = g_ref[0].astype(F32)
        for q in range(1, np_):
            g = g + g_ref[q].astype(F32)
        m2 = ADAM_B1 * m_ref[...] + (1.0 - ADAM_B1) * g
        v2 = ADAM_B2 * v_ref[...] + (1.0 - ADAM_B2) * jnp.square(g)
        m_hat = m2 / (1.0 - ADAM_B1 ** ADAM_STEP)
        v_hat = v2 / (1.0 - ADAM_B2 ** ADAM_STEP)
        go_ref[...] = g
        d_ref[...] = -ADAM_LR * (m_hat / (jnp.sqrt(v_hat) + ADAM_EPS) + ADAM_WD * w_ref[...])
        mo_ref[...] = m2
        vo_ref[...] = v2

    row = pl.BlockSpec((tr, c), lambda i: (i, 0))
    return pl.pallas_call(
        body, name=name, grid=(r // tr,),
        in_specs=[row, pl.BlockSpec((np_, tr, c), lambda i: (0, i, 0)), row, row],
        out_specs=[row] * 4, out_shape=[_sds((r, c))] * 4, compiler_params=_params(("parallel",)),
    )(w, gparts, m, v)


def _adamw_native(name, w, g, m, v):
    def body(w_ref, g_ref, m_ref, v_ref, d_ref, mo_ref, vo_ref):
        gv = g_ref[...]
        m2 = ADAM_B1 * m_ref[...] + (1.0 - ADAM_B1) * gv
        v2 = ADAM_B2 * v_ref[...] + (1.0 - ADAM_B2) * jnp.square(gv)
        m_hat = m2 / (1.0 - ADAM_B1 ** ADAM_STEP)
        v_hat = v2 / (1.0 - ADAM_B2 ** ADAM_STEP)
        d_ref[...] = -ADAM_LR * (m_hat / (jnp.sqrt(v_hat) + ADAM_EPS) + ADAM_WD * w_ref[...])
        mo_ref[...] = m2
        vo_ref[...] = v2

    return pl.pallas_call(body, name=name, out_shape=[_sds(w.shape)] * 3, compiler_params=_params())(w, g, m, v)


SMALL = ["c_ctx", "ada_b", "norm1_g", "s5_lam_re", "s5_lam_im", "s5_log_dt", "s5_d", "conv_b", "conv_ln_g", "conv_ln_b",
         "norm2_g", "final_g"]
SMALL_PACKED_ROWS = 24


def _pack_rows(parts, rows):
    flat = jnp.concatenate([p.reshape(-1).astype(F32) for p in parts])
    return jnp.pad(flat, (0, rows * D_MODEL - flat.shape[0])).reshape(rows, D_MODEL)


def _unpack_rows(packed, shapes):
    flat = packed.reshape(-1)
    out, off = [], 0
    for shape in shapes:
        size = 1
        for s in shape:
            size *= s
        out.append(flat[off:off + size].reshape(shape))
        off += size
    return out


def kernel(x, c, ctx, c_ctx, ada_w, ada_b, norm1_g, w_in, s5_lam_re, s5_lam_im, s5_log_dt, s5_b_re, s5_b_im, s5_c_re, s5_c_im, s5_d, s5_w_glu, conv_w, conv_b, conv_ln_g, conv_ln_b, w_out, norm2_g, mlp_w1, mlp_w2, final_g, loss_target, m_c_ctx, m_ada_w, m_ada_b, m_norm1_g, m_w_in, m_s5_lam_re, m_s5_lam_im, m_s5_log_dt, m_s5_b_re, m_s5_b_im, m_s5_c_re, m_s5_c_im, m_s5_d, m_s5_w_glu, m_conv_w, m_conv_b, m_conv_ln_g, m_conv_ln_b, m_w_out, m_norm2_g, m_mlp_w1, m_mlp_w2, m_final_g, v_c_ctx, v_ada_w, v_ada_b, v_norm1_g, v_w_in, v_s5_lam_re, v_s5_lam_im, v_s5_log_dt, v_s5_b_re, v_s5_b_im, v_s5_c_re, v_s5_c_im, v_s5_d, v_s5_w_glu, v_conv_w, v_conv_b, v_conv_ln_g, v_conv_ln_b, v_w_out, v_norm2_g, v_mlp_w1, v_mlp_w2, v_final_g):
    weights = dict(c_ctx=c_ctx, ada_w=ada_w, ada_b=ada_b, norm1_g=norm1_g, w_in=w_in, s5_lam_re=s5_lam_re, s5_lam_im=s5_lam_im, s5_log_dt=s5_log_dt, s5_b_re=s5_b_re, s5_b_im=s5_b_im, s5_c_re=s5_c_re, s5_c_im=s5_c_im, s5_d=s5_d, s5_w_glu=s5_w_glu, conv_w=conv_w, conv_b=conv_b, conv_ln_g=conv_ln_g, conv_ln_b=conv_ln_b, w_out=w_out, norm2_g=norm2_g, mlp_w1=mlp_w1, mlp_w2=mlp_w2, final_g=final_g)
    mom1 = dict(c_ctx=m_c_ctx, ada_w=m_ada_w, ada_b=m_ada_b, norm1_g=m_norm1_g, w_in=m_w_in, s5_lam_re=m_s5_lam_re, s5_lam_im=m_s5_lam_im, s5_log_dt=m_s5_log_dt, s5_b_re=m_s5_b_re, s5_b_im=m_s5_b_im, s5_c_re=m_s5_c_re, s5_c_im=m_s5_c_im, s5_d=m_s5_d, s5_w_glu=m_s5_w_glu, conv_w=m_conv_w, conv_b=m_conv_b, conv_ln_g=m_conv_ln_g, conv_ln_b=m_conv_ln_b, w_out=m_w_out, norm2_g=m_norm2_g, mlp_w1=m_mlp_w1, mlp_w2=m_mlp_w2, final_g=m_final_g)
    mom2 = dict(c_ctx=v_c_ctx, ada_w=v_ada_w, ada_b=v_ada_b, norm1_g=v_norm1_g, w_in=v_w_in, s5_lam_re=v_s5_lam_re, s5_lam_im=v_s5_lam_im, s5_log_dt=v_s5_log_dt, s5_b_re=v_s5_b_re, s5_b_im=v_s5_b_im, s5_c_re=v_s5_c_re, s5_c_im=v_s5_c_im, s5_d=v_s5_d, s5_w_glu=v_s5_w_glu, conv_w=v_conv_w, conv_b=v_conv_b, conv_ln_g=v_conv_ln_g, conv_ln_b=v_conv_ln_b, w_out=v_w_out, norm2_g=v_norm2_g, mlp_w1=v_mlp_w1, mlp_w2=v_mlp_w2, final_g=v_final_g)
    order = list(weights)

    me = 4 * lax.axis_index("x") + 2 * lax.axis_index("y") + lax.axis_index("c")
    xs, cs, tgt = x[0], ctx[0], loss_target[0]
    n_lat_rows, n_ctx_rows = xs.shape[0], cs.shape[0]
    n_rows = n_lat_rows + n_ctx_rows
    n_lat = n_lat_rows // ROW_BLOCK
    ada_cols = ada_w.shape[2]

    (c_all,), _ = _exchange("gather_c", [c], [True])
    c_all = c_all.reshape(NDEV, D_MODEL)

    cond_fwd = jnp.concatenate([c_all, c_ctx[None], jnp.zeros((7, D_MODEL), F32)])
    ada_b_loc = lax.dynamic_slice(ada_b, (0, me * ada_cols), (1, ada_cols))
    (mod_g,), mod_token = _exchange("gather_mod", [_ada_fwd(cond_fwd, ada_w[0], ada_b_loc)], [True])
    weight_groups, weights_token = _exchange_start_groups("gather_weights_start", [
        ([w_in[0].astype(BF16)], [True]),
        ([s5_w_glu[0].astype(BF16), conv_w[0] + mod_token[0:1, 0:1], w_out[0].astype(BF16)], [True] * 3),
        ([mlp_w1[0].astype(BF16), mlp_w2[0].astype(BF16)], [True] * 2)])
    (wi_send, wi_recv, wi_src, wi_land), (mixer_send, mixer_recv, mixer_src, mixer_land), \
        (mlpw_send, mlpw_recv, mlpw_src, mlpw_land) = weight_groups
    mod_rows = jnp.transpose(mod_g, (1, 0, 2)).reshape(16, 6 * D_MODEL) + weights_token[0:1, 0:1]
    mod = lax.dynamic_slice(mod_rows, (me, 0), (1, 6 * D_MODEL)).reshape(6, D_MODEL)
    modc = mod_rows[8, :2 * D_MODEL].reshape(2, D_MODEL)
    sh1, sc1, g1, sh2, sc2, g2 = [mod[i:i + 1] for i in range(6)]

    lam_re, lam_im = s5_lam_re[0].reshape(2, 1, NSTATE), s5_lam_im[0].reshape(2, 1, NSTATE)
    ldt = jnp.repeat(s5_log_dt[0], S5_STATE, axis=-1).reshape(2, 1, NSTATE)
    bt_re = jnp.transpose(s5_b_re[0], (0, 3, 1, 2)).reshape(2, S5_GROUP, NSTATE)
    bt_im = jnp.transpose(s5_b_im[0], (0, 3, 1, 2)).reshape(2, S5_GROUP, NSTATE)
    groups_per_block = S5_GROUPS // S5_BLOCKS
    ct_re = jnp.tile(s5_c_re[0].reshape(2, S5_WIDTH, S5_STATE), (1, 1, groups_per_block))
    ct_im = jnp.tile(s5_c_im[0].reshape(2, S5_WIDTH, S5_STATE), (1, 1, groups_per_block))
    d_skip = s5_d[0].reshape(1, S5_WIDTH)
    perms = [_segment_permutation(reverse_time=(d == 0)) for d in range(2)]
    perms_t = [p.T for p in perms]
    disc = [_s5_discretise(f"s5_disc{d}", False, lam_re[d], lam_im[d], ldt[d], bt_re[d], bt_im[d], ct_re[d], ct_im[d])
            for d in range(2)]

    a_all = _prenorm("prenorm1", xs, cs, norm1_g, jnp.stack([mod[0:2], modc]))
    before_w_in = a_all[0:SUBLANES, 0:LANES].astype(F32) + disc[0][0][0:SUBLANES, 0:LANES] + disc[1][0][0:SUBLANES, 0:LANES]
    wi_own, wi_landed = _exchange_wait("gather_w_in_wait", wi_send, wi_recv, wi_src, wi_land, [True], before_w_in)
    w_in_full = jnp.transpose(_with_own(wi_landed[0], wi_own[0], me), (1, 0, 2)).reshape(D_MODEL, IN_COLS)
    tm_all = 1088 if n_rows % 1088 == 0 else ROW_BLOCK
    (z_all,) = _matmul("in_proj", a_all, w_in_full, "nn", (n_rows, IN_COLS, D_MODEL), (tm_all, IN_COLS, D_MODEL),
                       [((n_rows, IN_COLS), F32)])

    states, y_dir = [], []
    for d in range(2):
        _, tab, _, bmat, cmat = disc[d]
        s, yd = _s5_scan_fwd(f"s5_scan_fwd{d}", d == 0, z_all, bmat, cmat, tab, perms[d], perms_t[d])
        states.append(s)
        y_dir.append(yd)
    mixer_own, mixer_landed = _exchange_wait("gather_mixer_wait", mixer_send, mixer_recv, mixer_src, mixer_land,
                                             [True] * 3, y_dir[1])
    glu_g, conv_w_g, w_out_g = [_with_own(l, o, me) for l, o in zip(mixer_landed, mixer_own)]
    glu_full = glu_g.reshape(S5_WIDTH, S5_WIDTH)
    conv_w_full = jnp.transpose(conv_w_g, (1, 0, 2)).reshape(CONV_K, CONV_WIDTH)
    w_out_full = w_out_g.reshape(D_MODEL, D_MODEL)
    ycat = _glu_fwd(z_all, y_dir[0], y_dir[1], d_skip, glu_full, n_lat_rows)

    hh_pad = _conv_gate(z_all, n_lat_rows)
    hc, ycat = _conv_fwd(hh_pad, conv_w_full, conv_b, conv_ln_g, conv_ln_b, ycat, n_lat_rows)

    tm = min(1024, n_lat_rows)
    tm_e = min(512, n_lat_rows)
    w1_cols = D_FF // NDEV
    row_vec = lambda tn: pl.BlockSpec((1, tn), lambda i, j, k: (0, j))
    out_tile = lambda t_m, t_n: pl.BlockSpec((t_m, t_n), lambda i, j, k: (i, j))
    full_rows = ((n_lat_rows, D_MODEL), F32)
    sums = ((n_lat_rows // tm_e, SUBLANES, D_MODEL), F32)
    sums_spec = pl.BlockSpec((None, SUBLANES, D_MODEL), lambda i, j, k: (i, 0, 0))
    vec = lambda v: (v, row_vec(D_MODEL))
    transposed_tile = lambda t_m, t_n: pl.BlockSpec((t_n, t_m), lambda i, j, k: (j, i))
    mix, h1, a2, a2_t = _matmul(
        "out_proj", ycat, w_out_full, "nn", (n_lat_rows, D_MODEL, D_MODEL), (tm_e, D_MODEL, D_MODEL),
        [full_rows, full_rows, ((n_lat_rows, D_MODEL), BF16), ((D_MODEL, n_lat_rows), BF16)],
        epi=_epi_residual_prenorm,
        epi_extra=[(xs, out_tile(tm_e, D_MODEL)), vec(g1), vec(norm2_g), vec(sc2), vec(sh2)],
        out_specs=[out_tile(tm_e, D_MODEL)] * 3 + [transposed_tile(tm_e, D_MODEL)])
    mlpw_own, mlpw_landed = _exchange_wait("gather_mlp_wait", mlpw_send, mlpw_recv, mlpw_src, mlpw_land, [True] * 2, a2)
    w1_g, w2_g = [_with_own(l, o, me) for l, o in zip(mlpw_landed, mlpw_own)]
    w2_full = w2_g.reshape(D_FF, D_MODEL)
    tm_up = min(2048, n_lat_rows)
    f, f_t = _matmul("mlp_up", a2, w1_g, "nn", (n_lat_rows, D_FF, D_MODEL), (tm_up, w1_cols, D_MODEL),
                     [((n_lat_rows, D_FF), BF16), ((D_FF, n_lat_rows), BF16)], epi=lambda acc: (acc, acc.T),
                     b_spec=pl.BlockSpec((None, D_MODEL, w1_cols), lambda i, j, k: (j, 0, 0)),
                     out_specs=[out_tile(tm_up, w1_cols), transposed_tile(tm_up, w1_cols)])
    sq_relu = lambda t: jnp.square(jnp.maximum(t, 0.0))
    mlp_out, d_h2, dm2, err_sums, d_final_g8 = _matmul(
        "mlp_down", f, w2_full, "nn", (n_lat_rows, D_MODEL, D_FF), (tm_e, D_MODEL, 2048),
        [full_rows, full_rows, ((n_lat_rows, D_MODEL), BF16), sums, sums], a_fn=sq_relu, epi=_epi_residual_loss,
        epi_extra=[(h1, out_tile(tm_e, D_MODEL)), vec(g2), (tgt, out_tile(tm_e, D_MODEL)), vec(final_g[None])],
        out_specs=[out_tile(tm_e, D_MODEL)] * 3 + [sums_spec] * 2)

    (d_f,) = _matmul("mlp_down_dx", dm2, w2_full, "nt", (n_lat_rows, D_FF, D_MODEL), (tm_up, 512, D_MODEL),
                     [((n_lat_rows, D_FF), BF16)],
                     epi=lambda acc, ft: (acc * 2.0 * jnp.maximum(ft.astype(F32), 0.0),),
                     epi_extra=[(f, out_tile(tm_up, 512))])
    tk_dw = min(2048, n_lat_rows)
    (g_w2,) = _matmul("mlp_down_dw", f_t, dm2, "nn", (D_FF, D_MODEL, n_lat_rows), (1024, D_MODEL, tk_dw),
                      [((D_FF, D_MODEL), F32)], a_fn=sq_relu)
    (g_w1,) = _matmul("mlp_up_dw", a2_t, d_f, "nn", (D_MODEL, D_FF, n_lat_rows), (D_MODEL, w1_cols, n_lat_rows),
                      [((NDEV, D_MODEL, w1_cols), F32)],
                      out_specs=[pl.BlockSpec((None, D_MODEL, w1_cols), lambda i, j, k: (j, 0, 0))])
    mlp_send, mlp_recv, mlp_src, mlp_land, mlp_token = _exchange_start(
        "scatter_mlp_start", [g_w1, g_w2.reshape(NDEV, D_FF // NDEV, D_MODEL)], [False] * 2)
    d_h1, dm1, *sums2 = _matmul(
        "mlp_up_dx", d_f, w1_g, "nt", (n_lat_rows, D_MODEL, D_FF), (tm_e, D_MODEL, 4 * w1_cols),
        [full_rows, ((n_lat_rows, D_MODEL), BF16)] + [sums] * 4, epi=_epi_norm_bwd,
        epi_extra=[(h1, out_tile(tm_e, D_MODEL)), (d_h2, out_tile(tm_e, D_MODEL)), (mlp_out, out_tile(tm_e, D_MODEL)),
                   vec(norm2_g), vec(sc2 + mlp_token[0:1, 0:1]), vec(g1)],
        b_spec=pl.BlockSpec((4, D_MODEL, w1_cols), lambda i, j, k: (k, 0, 0)), b_slabs=4,
        out_specs=[out_tile(tm_e, D_MODEL)] * 2 + [sums_spec] * 4)

    (d_ycat,) = _matmul("out_proj_dx", dm1, w_out_full, "nt", (n_lat_rows, D_MODEL, D_MODEL), (tm, D_MODEL, D_MODEL),
                        [((n_lat_rows, D_MODEL), F32)])
    (g_w_out,) = _matmul("out_proj_dw", ycat, dm1, "tn", (D_MODEL, D_MODEL, n_lat_rows), (D_MODEL, D_MODEL, 512),
                         [((D_MODEL, D_MODEL), F32)])

    dy, g_glu, dd8 = _glu_bwd(d_ycat, z_all, y_dir[0], y_dir[1], d_skip, glu_full, n_lat_rows)
    proj_send, proj_recv, proj_src, proj_land, proj_token = _exchange_start(
        "scatter_proj_start",
        [g_w_out.reshape(NDEV, D_MODEL // NDEV, D_MODEL), g_glu.reshape(NDEV, S5_WIDTH // NDEV, S5_WIDTH)], [False] * 2)
    perms = [p + proj_token[0:1, 0:1].astype(BF16) for p in perms]
    du, g_lam_re, g_lam_im, g_ldt, g_bt, g_cdiag = [], [], [], [], [], []
    for d in range(2):
        _, _, adj, bmat, cmat = disc[d]
        du_d, d_bdiag, d_cdiag, d_abar8 = _s5_scan_bwd(f"s5_scan_bwd{d}", d == 0, dy, z_all, states[d], bmat, cmat, adj,
                                                       perms[d], perms_t[d])
        du.append(du_d)
        d_bbar = jnp.transpose(d_bdiag.reshape(S5_BLOCKS, S5_GROUP, 2, NSTATE // S5_BLOCKS), (2, 1, 0, 3)).reshape(
            2 * S5_GROUP, NSTATE)
        d_lam8, d_bt = _s5_discretise_bwd(f"s5_disc_bwd{d}", lam_re[d], lam_im[d], ldt[d], bt_re[d], bt_im[d], d_abar8, d_bbar)
        g_lam_re.append(d_lam8[0].reshape(S5_GROUPS, S5_STATE))
        g_lam_im.append(d_lam8[1].reshape(S5_GROUPS, S5_STATE))
        g_ldt.append(d_lam8[2].reshape(S5_GROUPS, S5_STATE).sum(axis=-1))
        g_bt.append(d_bt)
        g_cdiag.append(d_cdiag)

    dhc_pad, conv_sums = _conv_bwd_norm(d_ycat, hc, conv_ln_g, conv_ln_b, n_lat_rows)
    d_v, d_gate, g_conv_w8 = _conv_bwd_taps(dhc_pad, hh_pad, z_all, conv_w_full, n_lat_rows)

    dz_all = _dz_assemble(du[0], du[1], dy, d_skip, d_v, d_gate, n_lat)
    (g_w_in_full,) = _matmul("in_proj_dw", a_all, dz_all, "tn", (D_MODEL, IN_COLS, n_rows), (D_MODEL, IN_COLS, tm_all),
                             [((D_MODEL, IN_COLS), F32)])
    g_w_in_parts = jnp.transpose(g_w_in_full.reshape(D_MODEL, NDEV, IN_COLS // NDEV), (1, 0, 2)).astype(BF16)
    win_send, win_recv, win_src, win_land, win_token = _exchange_start("scatter_w_in_start", [g_w_in_parts], [False])
    w_in_late = w_in_full + win_token[0:1, 0:1].astype(BF16)
    grad_x, *sums1 = _matmul(
        "in_proj_dx", dz_all, w_in_late, "nt", (n_lat_rows, D_MODEL, IN_COLS), (tm_e, D_MODEL, IN_COLS),
        [full_rows] + [sums] * 4, epi=_epi_norm_bwd,
        epi_extra=[(xs, out_tile(tm_e, D_MODEL)), (d_h1, out_tile(tm_e, D_MODEL)), (mix, out_tile(tm_e, D_MODEL)),
                   vec(norm1_g), vec(sc1)],
        out_specs=[out_tile(tm_e, D_MODEL)] + [sums_spec] * 4)
    (d_a_ctx,) = _matmul("in_proj_dx_ctx", dz_all, w_in_late, "nt", (n_ctx_rows, D_MODEL, IN_COLS),
                         (ROW_BLOCK, D_MODEL, IN_COLS), [((n_ctx_rows, D_MODEL), F32)],
                         a_spec=pl.BlockSpec((ROW_BLOCK, IN_COLS), lambda i, j, k: (i + n_lat, 0)))
    (sums1c,) = _norm_bwd("norm1_bwd_ctx", cs, d_a_ctx, 0, norm1_g, modc[1:2])

    s1, s1c, s2 = [p.sum(axis=(0, 1)) for p in sums1], sums1c.sum(axis=1), [p.sum(axis=(0, 1)) for p in sums2]
    d_mod = jnp.concatenate([s1[0], s1[1], s1[3], s2[0], s2[1], s2[3]])
    d_modc = jnp.concatenate([s1c[0], s1c[1], jnp.zeros((4 * D_MODEL,), F32)])
    (dmod_g,), _ = _exchange("gather_dmod", [jnp.stack([d_mod, d_modc])], [True])
    dmod16 = jnp.concatenate([dmod_g[:, 0], dmod_g[:, 1]])
    dmod16_loc = lax.dynamic_slice(dmod16, (0, me * ada_cols), (16, ada_cols))
    cond_bwd = jnp.concatenate([c_all, jnp.broadcast_to(c_ctx[None], (NDEV, D_MODEL))])
    g_ada_w, g_c_ctx8 = _ada_bwd(cond_bwd, dmod16_loc, ada_w[0], c_ctx[None])

    small_parts = dict(
        c_ctx=g_c_ctx8[0], ada_b=d_mod + d_modc, norm1_g=s1[2] + s1c[2],
        s5_lam_re=jnp.stack(g_lam_re), s5_lam_im=jnp.stack(g_lam_im), s5_log_dt=jnp.stack(g_ldt),
        s5_d=dd8.sum(axis=0), conv_b=conv_sums[0].sum(axis=0), conv_ln_g=conv_sums[1].sum(axis=0),
        conv_ln_b=conv_sums[2].sum(axis=0), norm2_g=s2[2], final_g=d_final_g8.sum(axis=(0, 1)))
    reduced_shapes = [(SMALL_PACKED_ROWS, D_MODEL), (2, 2 * S5_GROUP, NSTATE), (2,) + _S5_DIAG, (1,)]
    small_g = _pack_rows(
        [_pack_rows([small_parts[n] for n in SMALL], SMALL_PACKED_ROWS), jnp.stack(g_bt), jnp.stack(g_cdiag),
         (0.5 / D_MODEL * jnp.sum(err_sums)).reshape(1)], SMALL_ROWS).reshape(NDEV, SMALL_ROWS // NDEV, D_MODEL)
    g_conv_w_parts = jnp.transpose(g_conv_w8.sum(axis=1).reshape(CONV_K, NDEV, CONV_WIDTH // NDEV), (1, 0, 2))

    res = {}

    def own_chunk(src):
        return lax.dynamic_index_in_dim(src, me, 0, keepdims=False)

    def adamw_big(name, parts):
        outs = _adamw("adamw_" + name, weights[name][0], parts, mom1[name][0], mom2[name][0])
        res[name] = [o[None] for o in outs]
        return outs[0]

    sm_send, sm_recv, sm_src, sm_land, sm_token = _exchange_start("scatter_small_start", [g_conv_w_parts, small_g],
                                                                  [False] * 2)
    mlp_src, mlp_landed = _exchange_wait("scatter_mlp_wait", mlp_send, mlp_recv, mlp_src, mlp_land, [False] * 2, sm_token)
    p_w1, p_w2 = [_with_own(l, own_chunk(s), me) for l, s in zip(mlp_landed, mlp_src)]
    adamw_big("ada_w", g_ada_w[None])
    adamw_big("mlp_w1", p_w1)
    done = adamw_big("mlp_w2", p_w2)
    sm_src, sm_landed = _exchange_wait("scatter_small_wait", sm_send, sm_recv, sm_src, sm_land, [False] * 2, done)
    p_conv_w, p_small = [_with_own(l, own_chunk(s), me) for l, s in zip(sm_landed, sm_src)]
    ga_send, ga_recv, ga_src, ga_land, ga_token = _exchange_start("gather_small_start", [_sum_parts(p_small)], [True])
    proj_src, proj_landed = _exchange_wait("scatter_proj_wait", proj_send, proj_recv, proj_src, proj_land, [False] * 2,
                                           ga_token)
    p_w_out, p_glu = [_with_own(l, own_chunk(s), me) for l, s in zip(proj_landed, proj_src)]
    adamw_big("w_out", p_w_out)
    done = adamw_big("s5_w_glu", p_glu)
    win_src, win_landed = _exchange_wait("scatter_w_in_wait", win_send, win_recv, win_src, win_land, [False], done)
    adamw_big("w_in", _with_own(win_landed[0], own_chunk(win_src[0]), me))
    done = adamw_big("conv_w", p_conv_w)
    ga_own, ga_landed = _exchange_wait("gather_small_wait", ga_send, ga_recv, ga_src, ga_land, [True], done)
    small_all = _with_own(ga_landed[0], ga_own[0], me).reshape(1, SMALL_ROWS, D_MODEL)
    _, r_bt, r_cdiag, loss = _unpack_rows(small_all, reduced_shapes)
    loss = loss.reshape(())
    pack = lambda src: _pack_rows([src[n] for n in SMALL], SMALL_PACKED_ROWS)
    outs = _adamw("adamw_small", pack(weights), small_all, pack(mom1), pack(mom2))
    unpacked = [_unpack_rows(o, [weights[n].shape for n in SMALL]) for o in outs]
    for i, name in enumerate(SMALL):
        res[name] = [u[i] for u in unpacked]
    to_ghp = lambda t: jnp.transpose(t.reshape(2, S5_GROUP, S5_GROUPS, S5_STATE), (0, 2, 1, 3))[None]
    r_c = jnp.transpose(r_cdiag.reshape(2, S5_BLOCKS, S5_GROUP, 2, groups_per_block, S5_STATE), (3, 0, 1, 4, 2, 5)).reshape(
        2, 1, 2, S5_GROUPS, S5_GROUP, S5_STATE)
    swap = lambda t: jnp.swapaxes(t, -1, -2)
    for name, grad in (("s5_b_re", to_ghp(r_bt[:, :S5_GROUP])), ("s5_b_im", to_ghp(r_bt[:, S5_GROUP:]))):
        outs = _adamw_native("adamw_" + name, swap(weights[name]), grad, swap(mom1[name]), swap(mom2[name]))
        res[name] = [swap(grad), *[swap(o) for o in outs]]
    for name, grad in (("s5_c_re", r_c[0]), ("s5_c_im", -r_c[1])):
        res[name] = [grad, *_adamw_native("adamw_" + name, weights[name], grad, mom1[name], mom2[name])]

    return (loss, grad_x[None], *[res[n][0] for n in order], *[res[n][1] for n in order],
            *[res[n][2] for n in order], *[res[n][3] for n in order])
```

```python
import jax
import jax.numpy as jnp
from jax import lax
from jax.experimental import pallas as pl
from jax.experimental.pallas import tpu as pltpu

F32 = jnp.float32
BF16 = jnp.bfloat16
MESH = pl.DeviceIdType.MESH
ANY = pl.BlockSpec(memory_space=pl.ANY)

NDEV = 8
D_MODEL = 1024
GRID_W = 64
S5_WIDTH = 512
S5_GROUP = 16
S5_GROUPS = 32
S5_STATE = 64
NSTATE = S5_GROUPS * S5_STATE
CONV_WIDTH = 512
CONV_K = 31
IN_COLS = S5_WIDTH + 2 * CONV_WIDTH
D_FF = 4 * D_MODEL
EPS_RMS = 1e-6
EPS_LN = 1e-5
ADAM_LR = 0.001
ADAM_B1 = 0.9
ADAM_B2 = 0.999
ADAM_EPS = 1e-08
ADAM_WD = 0.01
ADAM_STEP = 10

SUBLANES = 8
LANES = 128
ROW_BLOCK = 256
SCAN_LANES = 512
SCAN_UNROLL = 32
SEGMENTS = SUBLANES
STEPS = ROW_BLOCK // SEGMENTS
S5_BLOCKS = 4
S5_BLOCK_WIDTH = S5_WIDTH // S5_BLOCKS
CONV_ROWS = 64
CONV_BWD_ROWS = 32
VMEM_LIMIT = 48 * 1024 * 1024
SMALL_ROWS = 320


def _params(sem=None):
    kw = dict(vmem_limit_bytes=VMEM_LIMIT)
    if sem is not None:
        kw["dimension_semantics"] = sem
    return pltpu.CompilerParams(**kw)


def _sds(shape, dtype=F32):
    return jax.ShapeDtypeStruct(tuple(shape), dtype)


def _fold8(x):
    return x.reshape(x.shape[0] // SUBLANES, SUBLANES, x.shape[1]).sum(axis=0)


def _sigmoid(x):
    return 1.0 / (1.0 + jnp.exp(-x))


def _silu(x):
    return x * _sigmoid(x)


def _dsilu(x):
    s = _sigmoid(x)
    return s * (1.0 + x * (1.0 - s))


_GELU_C = 0.7978845608028654


def _gelu(x):
    return 0.5 * x * (1.0 + jnp.tanh(_GELU_C * (x + 0.044715 * x * x * x)))


def _dgelu(x):
    t = jnp.tanh(_GELU_C * (x + 0.044715 * x * x * x))
    return 0.5 * (1.0 + t) + 0.5 * x * (1.0 - t * t) * _GELU_C * (1.0 + 3.0 * 0.044715 * x * x)


def _rms(x):
    rstd = lax.rsqrt(jnp.mean(x * x, axis=-1, keepdims=True) + EPS_RMS)
    return x * rstd, rstd


def _epi_residual_prenorm(acc, res, gate, gain, scale, shift):
    h = res + gate * acc
    xh, _ = _rms(h)
    a = (xh * gain) * (1.0 + scale) + shift
    return acc, h, a, a.T


def _epi_residual_loss(acc, res, gate, target, gain):
    h = res + gate * acc
    xh, rstd = _rms(h)
    err = xh * gain - target
    dy = err * (1.0 / h.shape[-1])
    dxh = dy * gain
    dh = rstd * (dxh - xh * jnp.mean(dxh * xh, axis=-1, keepdims=True))
    return acc, dh, dh * gate, _fold8(err * err), _fold8(dy * xh)


def _epi_norm_bwd(d_act, x, res, aux, gain, scale, gate=None):
    xh, rstd = _rms(x)
    dn = d_act * (1.0 + scale)
    dxh = dn * gain
    dx = res + rstd * (dxh - xh * jnp.mean(dxh * xh, axis=-1, keepdims=True))
    sums = (_fold8(d_act), _fold8(d_act * (xh * gain)), _fold8(dn * xh), _fold8(res * aux))
    return (dx, *sums) if gate is None else (dx, dx * gate, *sums)


def _dot(a, b, mode):
    dims = {"nn": (((1,), (0,)), ((), ())), "nt": (((1,), (1,)), ((), ())), "tn": (((0,), (0,)), ((), ()))}[mode]
    return lax.dot_general(a, b, dims, preferred_element_type=F32)


def _peers(x, y, c):
    out = []
    for k in range(1, NDEV):
        px = 1 - x if k & 4 else x
        py = 1 - y if k & 2 else y
        pc = 1 - c if k & 1 else c
        out.append(((px, py, pc), 4 * px + 2 * py + pc))
    return out


def _exchange_copies(src, land, send_sems, recv_sems, gather):
    x, y, c = lax.axis_index("x"), lax.axis_index("y"), lax.axis_index("c")
    me = 4 * x + 2 * y + c
    out = []
    for a in range(len(src)):
        for k, (peer, plin) in enumerate(_peers(x, y, c)):
            chunk = src[a] if gather[a] else src[a].at[plin]
            sems = dict(send_sem=send_sems.at[a * (NDEV - 1) + k], recv_sem=recv_sems.at[a * (NDEV - 1) + k],
                        device_id=peer, device_id_type=MESH)
            out.append((pltpu.make_async_remote_copy(src_ref=chunk, dst_ref=land[a].at[me], **sems),
                        pltpu.make_async_remote_copy(src_ref=chunk, dst_ref=land[a].at[plin], **sems)))
    return out


def _exchange(name, srcs, gather):
    n = len(srcs)
    outs = [_sds(((NDEV,) + s.shape) if g else s.shape, s.dtype) for s, g in zip(srcs, gather)]

    def body(*refs):
        src, dst, token = refs[:n], refs[n:2 * n], refs[2 * n]
        send_sems, recv_sems, local_sems = refs[2 * n + 1:]
        me = 4 * lax.axis_index("x") + 2 * lax.axis_index("y") + lax.axis_index("c")
        local = [pltpu.make_async_copy(src[a] if gather[a] else src[a].at[me], dst[a].at[me], local_sems.at[a])
                 for a in range(n)]
        for copy in local:
            copy.start()
        copies = _exchange_copies(src, dst, send_sems, recv_sems, gather)
        for copy, _ in copies:
            copy.start()
        token[...] = jnp.zeros_like(token)
        for copy, landing in copies:
            copy.wait_send()
            landing.wait_recv()
        for copy in local:
            copy.wait()

    nsem = n * (NDEV - 1)
    out = pl.pallas_call(
        body, name=name, out_shape=outs + [_sds((SUBLANES, LANES))], in_specs=[ANY] * n,
        out_specs=[ANY] * n + [pl.BlockSpec(memory_space=pltpu.VMEM)],
        scratch_shapes=[pltpu.SemaphoreType.DMA((nsem,)), pltpu.SemaphoreType.DMA((nsem,)), pltpu.SemaphoreType.DMA((n,))],
    )(*srcs)
    return out[:n], out[n]


HBM = pl.BlockSpec(memory_space=pltpu.HBM)
SEM = pl.BlockSpec(memory_space=pltpu.SEMAPHORE)
EFFECT = pltpu.SideEffectType.DATAFLOW_SIDE_EFFECTING


def _exchange_start_groups(name, groups):
    srcs = [s for g_srcs, _ in groups for s in g_srcs]
    gathers = [g for _, g_gather in groups for g in g_gather]
    lands = [lax.empty(((NDEV,) + s.shape) if g else s.shape, s.dtype) for s, g in zip(srcs, gathers)]
    n, ng = len(srcs), len(groups)

    def body(*refs):
        src, land = refs[:n], refs[n:2 * n]
        sems = refs[2 * n:2 * n + 2 * ng]
        token = refs[-1]
        first = 0
        for g, (g_srcs, g_gather) in enumerate(groups):
            last = first + len(g_srcs)
            for copy, _ in _exchange_copies(src[first:last], land[first:last], sems[2 * g], sems[2 * g + 1], g_gather):
                copy.start()
            first = last
        token[...] = jnp.zeros_like(token)

    hbm = lambda v: pltpu.HBM(v.shape, v.dtype)
    sem_shapes = []
    for g_srcs, _ in groups:
        sem_shapes += [pltpu.SemaphoreType.DMA((len(g_srcs) * (NDEV - 1),))] * 2
    out = pl.pallas_call(
        body, name=name,
        out_shape=(*sem_shapes, *[hbm(v) for v in srcs], *[hbm(v) for v in lands], _sds((SUBLANES, LANES))),
        in_specs=[HBM] * (2 * n),
        out_specs=(*([SEM] * (2 * ng)), *([HBM] * (2 * n)), pl.BlockSpec(memory_space=pltpu.VMEM)),
        input_output_aliases={i: 2 * ng + i for i in range(2 * n)},
        compiler_params=pltpu.CompilerParams(has_side_effects=EFFECT),
    )(*[pltpu.with_memory_space_constraint(v, pltpu.HBM) for v in srcs + lands])
    src_out, land_out = out[2 * ng:2 * ng + n], out[2 * ng + n:2 * ng + 2 * n]
    result, first = [], 0
    for g, (g_srcs, _) in enumerate(groups):
        last = first + len(g_srcs)
        result.append((out[2 * g], out[2 * g + 1], src_out[first:last], land_out[first:last]))
        first = last
    return result, out[-1]


def _exchange_start(name, srcs, gather):
    (group,), token = _exchange_start_groups(name, [(srcs, gather)])
    return (*group, token)


def _exchange_wait(name, send_sems, recv_sems, srcs, lands, gather, after):
    n = len(srcs)

    def body(*refs):
        src, land = refs[:n], refs[n:2 * n]
        send_ref, recv_ref = refs[2 * n], refs[2 * n + 1]
        for copy, landing in _exchange_copies(src, land, send_ref, recv_ref, gather):
            copy.wait_send()
            landing.wait_recv()

    hbm = lambda v: pltpu.HBM(v.shape, v.dtype)
    out = pl.pallas_call(
        body, name=name, out_shape=[hbm(v) for v in list(srcs) + list(lands)],
        in_specs=[HBM] * (2 * n) + [SEM, SEM, ANY], out_specs=[HBM] * (2 * n),
        input_output_aliases={i: i for i in range(2 * n)},
        compiler_params=pltpu.CompilerParams(has_side_effects=EFFECT),
    )(*srcs, *lands, send_sems, recv_sems, after)
    return out[:n], out[n:]


def _with_own(landed, own, me):
    return lax.dynamic_update_slice(landed, own[None], (me,) + (0,) * own.ndim)


def _matmul(name, a, b, mode, mnk, tiles, outs, a_spec=None, b_spec=None, a_fn=None, a_extra=(),
            epi=None, epi_extra=(), out_specs=None, b_slabs=1):
    m_, n_, k_ = mnk
    tm, tn, tk = tiles
    nk = k_ // tk
    if a_spec is None:
        a_spec = (pl.BlockSpec((tk, tm), lambda i, j, k: (k, i)) if mode == "tn"
                  else pl.BlockSpec((tm, tk), lambda i, j, k: (i, k)))
    if b_spec is None:
        b_spec = (pl.BlockSpec((tn, tk), lambda i, j, k: (j, k)) if mode == "nt"
                  else pl.BlockSpec((tk, tn), lambda i, j, k: (k, j)))
    if out_specs is None:
        out_specs = [pl.BlockSpec((tm, tn), lambda i, j, k: (i, j)) for _ in outs]
    na, ne, no = len(a_extra), len(epi_extra), len(outs)

    def body(*refs):
        a_ref, b_ref = refs[0], refs[1]
        ax = refs[2:2 + na]
        ex = refs[2 + na:2 + na + ne]
        o = refs[2 + na + ne:2 + na + ne + no]

        def finish(res):
            res = epi(res, *[r[...] for r in ex]) if epi is not None else (res,)
            for ref, val in zip(o, res):
                ref[...] = val.astype(ref.dtype)

        at = a_ref[...]
        if a_fn is not None:
            at = a_fn(at, *[r[...] for r in ax])
        at = at.astype(BF16)
        if b_slabs == 1:
            part = _dot(at, b_ref[...].astype(BF16), mode)
        else:
            ks = tk // b_slabs
            part = _dot(at[:, 0:ks], b_ref[0].astype(BF16), mode)
            for s in range(1, b_slabs):
                part = part + _dot(at[:, s * ks:(s + 1) * ks], b_ref[s].astype(BF16), mode)
        if nk == 1:
            finish(part)
            return
        acc = refs[-1]
        k = pl.program_id(2)

        @pl.when(k == 0)
        def _():
            acc[...] = part

        @pl.when(k > 0)
        def _():
            acc[...] += part

        @pl.when(k == nk - 1)
        def _():
            finish(acc[...])

    return pl.pallas_call(
        body, name=name, grid=(m_ // tm, n_ // tn, nk),
        in_specs=[a_spec, b_spec] + [s for _, s in a_extra] + [s for _, s in epi_extra],
        out_specs=out_specs, out_shape=[_sds(s, d) for s, d in outs],
        scratch_shapes=[pltpu.VMEM((tm, tn), F32)] if nk > 1 else [],
        compiler_params=_params(("parallel", "parallel", "arbitrary")),
    )(a, b, *[x for x, _ in a_extra], *[x for x, _ in epi_extra])


def _prenorm(name, x, ctx, gain, shsc):
    n_lat = x.shape[0] // ROW_BLOCK
    n_ctx = 0 if ctx is None else ctx.shape[0] // ROW_BLOCK
    d = x.shape[1]

    def norm(src, g_ref, m_ref, o_ref):
        xv = src[...]
        xh = xv * lax.rsqrt(jnp.mean(xv * xv, axis=-1, keepdims=True) + EPS_RMS)
        o_ref[...] = ((xh * g_ref[...]) * (1.0 + m_ref[1:2, :]) + m_ref[0:1, :]).astype(o_ref.dtype)

    def body(*refs):
        if ctx is None:
            x_ref, g_ref, m_ref, o_ref = refs
            norm(x_ref, g_ref, m_ref, o_ref)
        else:
            x_ref, c_ref, g_ref, m_ref, o_ref = refs
            i = pl.program_id(0)

            @pl.when(i < n_lat)
            def _():
                norm(x_ref, g_ref, m_ref, o_ref)

            @pl.when(i >= n_lat)
            def _():
                norm(c_ref, g_ref, m_ref, o_ref)

    in_specs = [pl.BlockSpec((ROW_BLOCK, d), lambda i: (jnp.minimum(i, n_lat - 1), 0))]
    args = [x]
    if ctx is not None:
        in_specs.append(pl.BlockSpec((ROW_BLOCK, d), lambda i: (jnp.maximum(i - n_lat, 0), 0)))
        args.append(ctx)
    in_specs += [pl.BlockSpec((1, d), lambda i: (0, 0)),
                 pl.BlockSpec((None, 2, d), lambda i: (jnp.minimum(i // n_lat, 1), 0, 0))]
    args += [gain, shsc]
    return pl.pallas_call(
        body, name=name, grid=(n_lat + n_ctx,), in_specs=in_specs,
        out_specs=pl.BlockSpec((ROW_BLOCK, d), lambda i: (i, 0)),
        out_shape=_sds(((n_lat + n_ctx) * ROW_BLOCK, d), BF16),
        compiler_params=_params(("parallel",)),
    )(*args)


def _norm_bwd(name, x, d_act, d_act_row0, gain, scale, res=None, aux=None):
    rows, d = x.shape
    nb = rows // ROW_BLOCK
    has_res = res is not None

    def body(*refs):
        if has_res:
            x_ref, da_ref, g_ref, sc_ref, r_ref, aux_ref, dx_ref, sums = refs
        else:
            x_ref, da_ref, g_ref, sc_ref, sums = refs
        i = pl.program_id(0)

        @pl.when(i == 0)
        def _():
            sums[...] = jnp.zeros_like(sums)

        xv, da = x_ref[...], da_ref[...]
        rstd = lax.rsqrt(jnp.mean(xv * xv, axis=-1, keepdims=True) + EPS_RMS)
        xh = xv * rstd
        g = g_ref[...]
        dn = da * (1.0 + sc_ref[...])
        sums[0] += _fold8(da)
        sums[1] += _fold8(da * (xh * g))
        sums[2] += _fold8(dn * xh)
        if has_res:
            dxh = dn * g
            dx = rstd * (dxh - xh * jnp.mean(dxh * xh, axis=-1, keepdims=True))
            rv = r_ref[...]
            dx_ref[...] = rv + dx
            sums[3] += _fold8(rv * aux_ref[...])

    row = lambda i: (i, 0)
    vec = pl.BlockSpec((1, d), lambda i: (0, 0))
    in_specs = [pl.BlockSpec((ROW_BLOCK, d), row), pl.BlockSpec((ROW_BLOCK, d), lambda i: (i + d_act_row0, 0)), vec, vec]
    args = [x, d_act, gain, scale]
    out_shape = [_sds((4, SUBLANES, d))]
    out_specs = [pl.BlockSpec((4, SUBLANES, d), lambda i: (0, 0, 0))]
    if has_res:
        in_specs += [pl.BlockSpec((ROW_BLOCK, d), row), pl.BlockSpec((ROW_BLOCK, d), row)]
        args += [res, aux]
        out_shape = [_sds((rows, d))] + out_shape
        out_specs = [pl.BlockSpec((ROW_BLOCK, d), row)] + out_specs
    return pl.pallas_call(
        body, name=name, grid=(nb,), in_specs=in_specs, out_specs=out_specs, out_shape=out_shape,
        compiler_params=_params(("arbitrary",)),
    )(*args)


def _ada_fwd(cond16, ada_w_loc, ada_b_loc):
    cols = ada_w_loc.shape[1]

    def body(c_ref, w_ref, b_ref, o_ref):
        s = _silu(c_ref[...]).astype(BF16)
        o_ref[...] = _dot(s, w_ref[...].astype(BF16), "nn") + b_ref[...]

    return pl.pallas_call(body, name="ada_fwd", out_shape=_sds((16, cols)), compiler_params=_params())(
        cond16, ada_w_loc, ada_b_loc)


def _ada_bwd(cond16, dmod16, ada_w_loc, c_ctx_row):
    k_, cols = ada_w_loc.shape

    def body(c_ref, dm_ref, w_ref, cc_ref, gw_ref, gc_ref):
        s = _silu(c_ref[...]).astype(BF16)
        dm = dm_ref[...]
        gw_ref[...] = _dot(s, dm.astype(BF16), "tn")
        dmc = jnp.sum(dm[8:16, :], axis=0, keepdims=True)
        dmc8 = jnp.broadcast_to(dmc, (SUBLANES, cols)).astype(BF16)
        ds = _dot(dmc8, w_ref[...].astype(BF16), "nt")
        row = lax.broadcasted_iota(jnp.int32, ds.shape, 0)
        gc_ref[...] = jnp.where(row == 0, ds * _dsilu(cc_ref[...]), 0.0)

    return pl.pallas_call(body, name="ada_bwd", out_shape=[_sds((k_, cols)), _sds((SUBLANES, k_))],
                          compiler_params=_params())(cond16, dmod16, ada_w_loc, c_ctx_row)


def _cmul(a, b):
    return a[0] * b[0] - a[1] * b[1], a[0] * b[1] + a[1] * b[0]


def _disc(lam_re, lam_im, ldt):
    dt = jnp.exp(ldt)
    mag = jnp.exp(lam_re * dt)
    th = lam_im * dt
    a_re, a_im = mag * jnp.cos(th), mag * jnp.sin(th)
    den = lam_re * lam_re + lam_im * lam_im
    n_re = a_re - 1.0
    f_re = (n_re * lam_re + a_im * lam_im) / den
    f_im = (a_im * lam_re - n_re * lam_im) / den
    return dt, mag, th, a_re, a_im, den, n_re, f_re, f_im


def _block_diag_mask(shape):
    row = lax.broadcasted_iota(jnp.int32, shape, 0)
    col = lax.broadcasted_iota(jnp.int32, shape, 1)
    return lax.shift_right_logical(row, 4) == lax.shift_right_logical(col, 6)


TAB_A = 0
TAB_BIG = 1
TAB_SEG = 4
TAB_PW = 5
TAB_ROWS = TAB_PW + STEPS


def _s5_discretise(name, ascending, lam_re, lam_im, ldt, bt_re, bt_im, ct_re, ct_im):
    def write_tables(ref, pw, big, asc, sign):
        row = lax.broadcasted_iota(jnp.int32, (SUBLANES, NSTATE), 0)
        full = lambda v: jnp.broadcast_to(v, (SUBLANES, NSTATE))

        def put(t, p):
            ref[0, t] = full(p[0])
            ref[1, t] = full(sign * p[1])

        put(TAB_A, pw[0])
        for t in range(3):
            put(TAB_BIG + t, big[t])
        seg = [big[0]]
        for _ in range(SEGMENTS - 1):
            seg.append(_cmul(seg[-1], big[0]))
        seg_re = jnp.zeros((SUBLANES, NSTATE), F32)
        seg_im = jnp.zeros((SUBLANES, NSTATE), F32)
        for r in range(SEGMENTS):
            p = seg[r] if asc else seg[SEGMENTS - 1 - r]
            seg_re = jnp.where(row == r, p[0], seg_re)
            seg_im = jnp.where(row == r, sign * p[1], seg_im)
        ref[0, TAB_SEG] = seg_re
        ref[1, TAB_SEG] = seg_im
        for k in range(STEPS):
            put(TAB_PW + k, pw[k])

    def body(lr_ref, li_ref, ldt_ref, br_ref, bi_ref, cr_ref, ci_ref, bb_ref, tab_ref, adj_ref, bm_ref, cm_ref):
        _, _, _, a_re, a_im, _, _, f_re, f_im = _disc(lr_ref[...], li_ref[...], ldt_ref[...])
        bre, bim = br_ref[...], bi_ref[...]
        bb_re = f_re * bre - f_im * bim
        bb_im = f_re * bim + f_im * bre
        bb_ref[0:S5_GROUP, :] = bb_re
        bb_ref[S5_GROUP:2 * S5_GROUP, :] = bb_im
        pw = [(a_re, a_im)]
        for _ in range(STEPS - 1):
            pw.append(_cmul(pw[-1], (a_re, a_im)))
        big = [pw[STEPS - 1]]
        for _ in range(2):
            big.append(_cmul(big[-1], big[-1]))
        write_tables(tab_ref, pw, big, ascending, 1.0)
        write_tables(adj_ref, pw, big, not ascending, -1.0)
        half = NSTATE // S5_BLOCKS
        mask = _block_diag_mask((S5_BLOCK_WIDTH, half))
        tile = lambda v: jnp.broadcast_to(v[None], (S5_BLOCK_WIDTH // S5_GROUP, S5_GROUP, half)).reshape(S5_BLOCK_WIDTH, half)
        for c in range(S5_BLOCKS):
            cols = slice(c * half, (c + 1) * half)
            rows = slice(c * S5_BLOCK_WIDTH, (c + 1) * S5_BLOCK_WIDTH)
            bm_ref[c, :, 0:half] = jnp.where(mask, tile(bb_re[:, cols]), 0.0).astype(BF16)
            bm_ref[c, :, half:2 * half] = jnp.where(mask, tile(bb_im[:, cols]), 0.0).astype(BF16)
            cm_ref[c, :, 0:half] = jnp.where(mask, cr_ref[rows, :], 0.0).astype(BF16)
            cm_ref[c, :, half:2 * half] = jnp.where(mask, -ci_ref[rows, :], 0.0).astype(BF16)

    blocked = _sds((S5_BLOCKS, S5_BLOCK_WIDTH, 2 * NSTATE // S5_BLOCKS), BF16)
    return pl.pallas_call(
        body, name=name,
        out_shape=[_sds((2 * S5_GROUP, NSTATE)), _sds((2, TAB_ROWS, SUBLANES, NSTATE)),
                   _sds((2, TAB_ROWS, SUBLANES, NSTATE)), blocked, blocked],
        compiler_params=_params(),
    )(lam_re, lam_im, ldt, bt_re, bt_im, ct_re, ct_im)


def _s5_discretise_bwd(name, lam_re, lam_im, ldt, bt_re, bt_im, d_abar8, d_bbar):
    def body(lr_ref, li_ref, ldt_ref, br_ref, bi_ref, da_ref, db_ref, dl_ref, dbt_ref):
        lam_re, lam_im = lr_ref[...], li_ref[...]
        dt, _, _, a_re, a_im, den, n_re, f_re, f_im = _disc(lam_re, lam_im, ldt_ref[...])
        bre, bim = br_ref[...], bi_ref[...]
        dbr, dbi = db_ref[0:S5_GROUP, :], db_ref[S5_GROUP:2 * S5_GROUP, :]
        dbt_ref[0:S5_GROUP, :] = f_re * dbr + f_im * dbi
        dbt_ref[S5_GROUP:2 * S5_GROUP, :] = f_re * dbi - f_im * dbr
        df_re = jnp.sum(bre * dbr + bim * dbi, axis=0, keepdims=True)
        df_im = jnp.sum(bre * dbi - bim * dbr, axis=0, keepdims=True)
        da = da_ref[...]
        da_re = jnp.sum(da[:, 0:NSTATE], axis=0, keepdims=True)
        da_im = jnp.sum(da[:, NSTATE:2 * NSTATE], axis=0, keepdims=True)
        da_re = da_re + (df_re * lam_re - df_im * lam_im) / den
        da_im = da_im + (df_re * lam_im + df_im * lam_re) / den
        ff = (f_re * df_re + f_im * df_im) * 2.0 / den
        d_lr = (df_re * n_re + df_im * a_im) / den - ff * lam_re
        d_li = (df_re * a_im - df_im * n_re) / den - ff * lam_im
        d_mag_mag = da_re * a_re + da_im * a_im
        d_th = da_im * a_re - da_re * a_im
        d_lr = d_lr + d_mag_mag * dt
        d_li = d_li + d_th * dt
        d_ldt = (d_mag_mag * lam_re + d_th * lam_im) * dt
        row = lax.broadcasted_iota(jnp.int32, (SUBLANES, NSTATE), 0)
        dl_ref[...] = jnp.where(row == 0, d_lr, jnp.where(row == 1, d_li, jnp.where(row == 2, d_ldt, 0.0)))

    return pl.pallas_call(
        body, name=name, out_shape=[_sds((SUBLANES, NSTATE)), _sds((2 * S5_GROUP, NSTATE))],
        compiler_params=_params(),
    )(lam_re, lam_im, ldt, bt_re, bt_im, d_abar8, d_bbar)


def _segment_permutation(reverse_time):
    rho = jnp.arange(ROW_BLOCK)
    src = STEPS * (rho % SEGMENTS) + rho // SEGMENTS
    if reverse_time:
        src = ROW_BLOCK - 1 - src
    return (src[:, None] == jnp.arange(ROW_BLOCK)[None, :]).astype(BF16)


def _permute_rows(perm_ref, v):
    return _dot(perm_ref[...], v, "nn").astype(BF16)


def _unpermute_rows(perm_t_ref, v):
    hi = v.astype(BF16)
    lo = (v - hi.astype(F32)).astype(BF16)
    return _dot(perm_t_ref[...], hi, "nn") + _dot(perm_t_ref[...], lo, "nn")


def _unrolled_loop(step, init):
    def trip(o, state):
        for u in range(SCAN_UNROLL):
            state = step(o * SCAN_UNROLL + u, state)
        return state

    if SCAN_UNROLL == STEPS:
        return trip(0, init)
    return lax.fori_loop(0, STEPS // SCAN_UNROLL, trip, init)


def _scan_chunk(x_ref, out_ref, tab_ref, carry_re, carry_im, ascending, pair_ref=None, acc_ref=None, lane_chunks=None):
    w = SCAN_LANES
    half = NSTATE // S5_BLOCKS
    row = lax.broadcasted_iota(jnp.int32, (SUBLANES, w), 0)
    last = (SEGMENTS - 1) if ascending else 0

    def from_previous_segment(v, k, fill):
        if ascending:
            return jnp.where(row >= k, pltpu.roll(v, k, 0), fill)
        return jnp.where(row < SEGMENTS - k, pltpu.roll(v, SEGMENTS - k, 0), fill)

    def tile_rows(k):
        return pl.ds(pl.multiple_of((k if ascending else STEPS - 1 - k) * SUBLANES, SUBLANES), SUBLANES)

    for j in (range(NSTATE // w) if lane_chunks is None else lane_chunks):
        n_l = pl.ds(j * w, w)
        lane0 = (j * w // half) * 2 * half + (j * w) % half
        re_l, im_l = pl.ds(lane0, w), pl.ds(lane0 + half, w)
        tab = lambda t, n_l=n_l: (tab_ref[0, t, :, n_l], tab_ref[1, t, :, n_l])
        a_re, a_im = tab(TAB_A)

        def local_step(k, h):
            rs = tile_rows(k)
            h_re = a_re * h[0] - a_im * h[1] + x_ref[rs, re_l]
            h_im = a_re * h[1] + a_im * h[0] + x_ref[rs, im_l]
            out_ref[rs, re_l] = h_re
            out_ref[rs, im_l] = h_im
            return h_re, h_im

        zero = jnp.zeros((SUBLANES, w), F32)
        end_re, end_im = _unrolled_loop(local_step, (zero, zero))
        for t, k in ((TAB_BIG, 1), (TAB_BIG + 1, 2), (TAB_BIG + 2, 4)):
            p_re, p_im = tab(t)
            s_re, s_im = from_previous_segment(end_re, k, 0.0), from_previous_segment(end_im, k, 0.0)
            end_re, end_im = end_re + (p_re * s_re - p_im * s_im), end_im + (p_re * s_im + p_im * s_re)
        c0_re, c0_im = carry_re[:, n_l], carry_im[:, n_l]
        p_re, p_im = tab(TAB_SEG)
        end_re = end_re + (p_re * c0_re - p_im * c0_im)
        end_im = end_im + (p_re * c0_im + p_im * c0_re)
        carry_re[:, n_l] = jnp.broadcast_to(end_re[last:last + 1, :], end_re.shape)
        carry_im[:, n_l] = jnp.broadcast_to(end_im[last:last + 1, :], end_im.shape)
        in_re = from_previous_segment(end_re, 1, c0_re)
        in_im = from_previous_segment(end_im, 1, c0_im)

        def carry_step(k, st):
            rs = tile_rows(k)
            p_re, p_im = tab_ref[0, TAB_PW + k, :, n_l], tab_ref[1, TAB_PW + k, :, n_l]
            o_re = out_ref[rs, re_l] + (p_re * in_re - p_im * in_im)
            o_im = out_ref[rs, im_l] + (p_re * in_im + p_im * in_re)
            out_ref[rs, re_l] = o_re
            out_ref[rs, im_l] = o_im
            if pair_ref is None:
                return st
            s_re, s_im = pair_ref[rs, re_l], pair_ref[rs, im_l]
            return (o_re, o_im, st[2] + (st[0] * s_re + st[1] * s_im), st[3] + (st[1] * s_re - st[0] * s_im))

        if pair_ref is None:
            _unrolled_loop(carry_step, 0)
        else:
            fin = _unrolled_loop(carry_step, (in_re, in_im, zero, zero))
            acc_ref[:, n_l] += fin[2]
            acc_ref[:, pl.ds(NSTATE + j * w, w)] += fin[3]


def _scan_block_index(i, n_lat, ctx_first_then_ascending):
    if ctx_first_then_ascending:
        return jnp.where(i == 0, n_lat, i - 1)
    return jnp.where(i == 0, n_lat, n_lat - i)


def _full_spec(shape):
    return pl.BlockSpec(shape, lambda i: (0,) * len(shape))


_S5_BLOCKED = (S5_BLOCKS, S5_BLOCK_WIDTH, 2 * NSTATE // S5_BLOCKS)
_S5_TABLES = (2, TAB_ROWS, SUBLANES, NSTATE)
_S5_DIAG = (S5_BLOCKS, S5_GROUP, 2 * NSTATE // S5_BLOCKS)


def _s5_scan_fwd(name, ascending, z_all, bmat, cmat, tab, perm, perm_t):
    rows = z_all.shape[0]
    nb = rows // ROW_BLOCK
    n_lat = nb - 1
    bw, sw = S5_BLOCK_WIDTH, 2 * NSTATE // S5_BLOCKS

    def body(u_ref, bm_ref, cm_ref, tab_ref, p_ref, pt_ref, s_ref, y_ref, bu, yp, carry_re, carry_im):
        @pl.when(pl.program_id(0) == 0)
        def _():
            carry_re[...] = jnp.zeros_like(carry_re)
            carry_im[...] = jnp.zeros_like(carry_im)

        up = _permute_rows(p_ref, u_ref[...].astype(BF16))
        for c in range(S5_BLOCKS):
            bu[:, c * sw:(c + 1) * sw] = _dot(up[:, c * bw:(c + 1) * bw], bm_ref[c], "nn")
        _scan_chunk(bu, s_ref, tab_ref, carry_re, carry_im, False)
        for c in range(S5_BLOCKS):
            yp[:, c * bw:(c + 1) * bw] = _dot(s_ref[:, c * sw:(c + 1) * sw].astype(BF16), cm_ref[c], "nt")
        y_ref[...] = _unpermute_rows(pt_ref, yp[...])

    blk = lambda i: (_scan_block_index(i, n_lat, ascending), 0)
    return pl.pallas_call(
        body, name=name, grid=(nb,),
        in_specs=[pl.BlockSpec((ROW_BLOCK, S5_WIDTH), blk), _full_spec(_S5_BLOCKED), _full_spec(_S5_BLOCKED),
                  _full_spec(_S5_TABLES), _full_spec((ROW_BLOCK, ROW_BLOCK)), _full_spec((ROW_BLOCK, ROW_BLOCK))],
        out_specs=[pl.BlockSpec((ROW_BLOCK, 2 * NSTATE), blk), pl.BlockSpec((ROW_BLOCK, S5_WIDTH), blk)],
        out_shape=[_sds((rows, 2 * NSTATE)), _sds((rows, S5_WIDTH))],
        scratch_shapes=[pltpu.VMEM((ROW_BLOCK, 2 * NSTATE), F32), pltpu.VMEM((ROW_BLOCK, S5_WIDTH), F32),
                        pltpu.VMEM((SUBLANES, NSTATE), F32), pltpu.VMEM((SUBLANES, NSTATE), F32)],
        compiler_params=_params(("arbitrary",)),
    )(z_all, bmat, cmat, tab, perm, perm_t)


def _s5_scan_bwd(name, ascending, dy, z_all, states, bmat, cmat, adj, perm, perm_t):
    rows = states.shape[0]
    nb = rows // ROW_BLOCK
    n_lat = nb - 1
    bw, sw = S5_BLOCK_WIDTH, 2 * NSTATE // S5_BLOCKS

    def block_index(i):
        if ascending:
            return jnp.where(i == nb - 1, n_lat, n_lat - 1 - i)
        return jnp.where(i == nb - 1, n_lat, i)

    def body(dy_ref, u_ref, s_ref, bm_ref, cm_ref, adj_ref, p_ref, pt_ref, du_ref, db_ref, dc_ref, da_ref,
             g, dup, db_acc, dc_acc, carry_re, carry_im):
        i = pl.program_id(0)

        @pl.when(i == 0)
        def _():
            carry_re[...] = jnp.zeros_like(carry_re)
            carry_im[...] = jnp.zeros_like(carry_im)
            da_ref[...] = jnp.zeros_like(da_ref)
            db_acc[...] = jnp.zeros_like(db_acc)
            dc_acc[...] = jnp.zeros_like(dc_acc)

        has_dy = (i < nb - 1).astype(F32)
        dyp = _permute_rows(p_ref, (dy_ref[...] * has_dy).astype(BF16))
        up = _permute_rows(p_ref, u_ref[...].astype(BF16))
        for c in range(S5_BLOCKS):
            g[:, c * sw:(c + 1) * sw] = _dot(dyp[:, c * bw:(c + 1) * bw], cm_ref[c], "nn")
            dc_acc[c] += _dot(dyp[:, c * bw:(c + 1) * bw], s_ref[:, c * sw:(c + 1) * sw].astype(BF16), "tn")
            _scan_chunk(g, g, adj_ref, carry_re, carry_im, True, pair_ref=s_ref, acc_ref=da_ref, lane_chunks=[c])
            gc = g[:, c * sw:(c + 1) * sw].astype(BF16)
            dup[:, c * bw:(c + 1) * bw] = _dot(gc, bm_ref[c], "nt")
            db_acc[c] += _dot(up[:, c * bw:(c + 1) * bw], gc, "tn")
        du_ref[...] = _unpermute_rows(pt_ref, dup[...])

        @pl.when(i == nb - 1)
        def _():
            mask = _block_diag_mask((bw, sw // 2))
            for acc, out in ((db_acc, db_ref), (dc_acc, dc_ref)):
                for c in range(S5_BLOCKS):
                    for part in range(2):
                        cols = slice(part * (sw // 2), (part + 1) * (sw // 2))
                        kept = jnp.where(mask, acc[c, :, cols], 0.0)
                        out[c, :, cols] = kept.reshape(bw // S5_GROUP, S5_GROUP, sw // 2).sum(axis=0)

    blk = lambda i: (block_index(i), 0)
    return pl.pallas_call(
        body, name=name, grid=(nb,),
        in_specs=[pl.BlockSpec((ROW_BLOCK, S5_WIDTH), lambda i: (jnp.minimum(block_index(i), n_lat - 1), 0)),
                  pl.BlockSpec((ROW_BLOCK, S5_WIDTH), blk), pl.BlockSpec((ROW_BLOCK, 2 * NSTATE), blk),
                  _full_spec(_S5_BLOCKED), _full_spec(_S5_BLOCKED), _full_spec(_S5_TABLES),
                  _full_spec((ROW_BLOCK, ROW_BLOCK)), _full_spec((ROW_BLOCK, ROW_BLOCK))],
        out_specs=[pl.BlockSpec((ROW_BLOCK, S5_WIDTH), blk), _full_spec(_S5_DIAG), _full_spec(_S5_DIAG),
                   _full_spec((SUBLANES, 2 * NSTATE))],
        out_shape=[_sds((rows, S5_WIDTH)), _sds(_S5_DIAG), _sds(_S5_DIAG), _sds((SUBLANES, 2 * NSTATE))],
        scratch_shapes=[pltpu.VMEM((ROW_BLOCK, 2 * NSTATE), F32), pltpu.VMEM((ROW_BLOCK, S5_WIDTH), F32),
                        pltpu.VMEM(_S5_BLOCKED, F32), pltpu.VMEM(_S5_BLOCKED, F32),
                        pltpu.VMEM((SUBLANES, NSTATE), F32), pltpu.VMEM((SUBLANES, NSTATE), F32)],
        compiler_params=_params(("arbitrary",)),
    )(dy, z_all, states, bmat, cmat, adj, perm, perm_t)


def _latent_row_tile(n_rows):
    return 512 if n_rows % 512 == 0 else ROW_BLOCK


def _glu_fwd(z_all, y0, y1, d_skip, w_glu, n_rows):
    def body(u_ref, y0_ref, y1_ref, d_ref, w_ref, o_ref):
        y = d_ref[...] * u_ref[...] + y0_ref[...] + y1_ref[...]
        g = _gelu(y)
        t = _dot(g.astype(BF16), w_ref[...], "nn")
        o_ref[...] = (g * _sigmoid(t)).astype(o_ref.dtype)

    rows = _latent_row_tile(n_rows)
    row = pl.BlockSpec((rows, S5_WIDTH), lambda i: (i, 0))
    return pl.pallas_call(
        body, name="glu_fwd", grid=(n_rows // rows,),
        in_specs=[row, row, row, pl.BlockSpec((1, S5_WIDTH), lambda i: (0, 0)),
                  pl.BlockSpec((S5_WIDTH, S5_WIDTH), lambda i: (0, 0))],
        out_specs=row, out_shape=_sds((n_rows, S5_WIDTH + CONV_WIDTH), BF16), compiler_params=_params(("parallel",)),
    )(z_all, y0, y1, d_skip, w_glu)


def _glu_bwd(d_ycat, z_all, y0, y1, d_skip, w_glu, n_rows):
    def body(do_ref, u_ref, y0_ref, y1_ref, d_ref, w_ref, dy_ref, dw_ref, dd_ref):
        @pl.when(pl.program_id(0) == 0)
        def _():
            dw_ref[...] = jnp.zeros_like(dw_ref)
            dd_ref[...] = jnp.zeros_like(dd_ref)

        u = u_ref[...]
        y = d_ref[...] * u + y0_ref[...] + y1_ref[...]
        g = _gelu(y)
        gb = g.astype(BF16)
        w = w_ref[...]
        sg = _sigmoid(_dot(gb, w, "nn"))
        do = do_ref[...]
        dt = do * g * sg * (1.0 - sg)
        dtb = dt.astype(BF16)
        dg = do * sg + _dot(dtb, w, "nt")
        dy = dg * _dgelu(y)
        dy_ref[...] = dy
        dw_ref[...] += _dot(gb, dtb, "tn")
        dd_ref[...] += _fold8(dy * u)

    rows = _latent_row_tile(n_rows)
    row = pl.BlockSpec((rows, S5_WIDTH), lambda i: (i, 0))
    sq = pl.BlockSpec((S5_WIDTH, S5_WIDTH), lambda i: (0, 0))
    return pl.pallas_call(
        body, name="glu_bwd", grid=(n_rows // rows,),
        in_specs=[row, row, row, row, pl.BlockSpec((1, S5_WIDTH), lambda i: (0, 0)), sq],
        out_specs=[row, sq, pl.BlockSpec((SUBLANES, S5_WIDTH), lambda i: (0, 0))],
        out_shape=[_sds((n_rows, S5_WIDTH)), _sds((S5_WIDTH, S5_WIDTH)), _sds((SUBLANES, S5_WIDTH))],
        compiler_params=_params(("arbitrary",)),
    )(d_ycat, z_all, y0, y1, d_skip, w_glu)


CONV_HALF = CONV_K // 2


def _conv_block(n_rows):
    blk = min(1024, n_rows)
    assert blk >= CONV_HALF * GRID_W and n_rows % blk == 0
    return blk


def _conv_gate(z_all, n_rows):
    blk = _conv_block(n_rows)
    nb = n_rows // blk

    def body(v_ref, g_ref, o_ref):
        i = pl.program_id(0)
        inside = jnp.logical_and(i >= 1, i <= nb)

        @pl.when(inside)
        def _():
            o_ref[...] = v_ref[...] * _sigmoid(g_ref[...])

        @pl.when(jnp.logical_not(inside))
        def _():
            o_ref[...] = jnp.zeros_like(o_ref)

    src = lambda col: pl.BlockSpec((blk, CONV_WIDTH), lambda i: (jnp.clip(i - 1, 0, nb - 1), col))
    return pl.pallas_call(
        body, name="conv_gate", grid=(nb + 2,), in_specs=[src(1), src(2)],
        out_specs=pl.BlockSpec((blk, CONV_WIDTH), lambda i: (i, 0)),
        out_shape=_sds(((nb + 2) * blk, CONV_WIDTH)), compiler_params=_params(("parallel",)),
    )(z_all, z_all)


def _stream_padded(pad_ref, buf, sems, blk, n_blocks):
    i = pl.program_id(0)

    def copy(b):
        rows = pl.ds(pl.multiple_of(b * blk, blk), blk)
        return pltpu.make_async_copy(pad_ref.at[rows, :], buf.at[rows, :], sems.at[b])

    @pl.when(i == 0)
    def _():
        for b in range(n_blocks):
            copy(b).start()
        copy(0).wait()
        copy(1).wait()

    copy(i + 2).wait()
    return pl.multiple_of(i * blk, blk)


def _conv_fwd(hh_pad, w, b, ln_g, ln_b, ycat, n_rows):
    blk = _conv_block(n_rows)
    nblk = n_rows // blk + 2

    def body(hh_ref, w_ref, b_ref, g_ref, lb_ref, ycat_ref, hc_ref, y_ref, win, sems):
        base = _stream_padded(hh_ref, win, sems, blk, nblk)

        def tile(t, _):
            r0 = pl.multiple_of(t * CONV_ROWS, CONV_ROWS)
            acc = jnp.zeros((CONV_ROWS, CONV_WIDTH), F32)
            for k in range(CONV_K):
                acc = acc + w_ref[k:k + 1, :] * win[pl.ds(base + r0 + blk + (k - CONV_HALF) * GRID_W, CONV_ROWS), :]
            hc = acc + b_ref[...]
            hc_ref[pl.ds(r0, CONV_ROWS), :] = hc
            mu = jnp.mean(hc, axis=-1, keepdims=True)
            xc = hc - mu
            ln = xc * lax.rsqrt(jnp.mean(xc * xc, axis=-1, keepdims=True) + EPS_LN) * g_ref[...] + lb_ref[...]
            y_ref[pl.ds(r0, CONV_ROWS), :] = _silu(ln).astype(y_ref.dtype)
            return 0

        lax.fori_loop(0, blk // CONV_ROWS, tile, 0)

    vec = pl.BlockSpec((1, CONV_WIDTH), lambda i: (0, 0))
    row = pl.BlockSpec((blk, CONV_WIDTH), lambda i: (i, 0))
    return pl.pallas_call(
        body, name="conv_fwd", grid=(n_rows // blk,),
        in_specs=[ANY, pl.BlockSpec((CONV_K, CONV_WIDTH), lambda i: (0, 0)), vec, vec, vec, ANY],
        out_specs=[row, pl.BlockSpec((blk, CONV_WIDTH), lambda i: (i, 1))],
        out_shape=[_sds((n_rows, CONV_WIDTH)), _sds(ycat.shape, ycat.dtype)], input_output_aliases={5: 1},
        scratch_shapes=[pltpu.VMEM((nblk * blk, CONV_WIDTH), F32), pltpu.SemaphoreType.DMA((nblk,))],
        compiler_params=_params(("arbitrary",)),
    )(hh_pad, w, b, ln_g, ln_b, ycat)


def _conv_bwd_norm(d_ycat, hc, ln_g, ln_b, n_rows):
    blk = _conv_block(n_rows)
    nb = n_rows // blk

    def body(dy_ref, hc_ref, g_ref, lb_ref, o_ref, sums):
        i = pl.program_id(0)

        @pl.when(i == 0)
        def _():
            sums[...] = jnp.zeros_like(sums)

        inside = jnp.logical_and(i >= 1, i <= nb)

        @pl.when(inside)
        def _():
            hcv = hc_ref[...]
            mu = jnp.mean(hcv, axis=-1, keepdims=True)
            xc = hcv - mu
            rstd = lax.rsqrt(jnp.mean(xc * xc, axis=-1, keepdims=True) + EPS_LN)
            xh = xc * rstd
            g = g_ref[...]
            dln = dy_ref[...] * _dsilu(xh * g + lb_ref[...])
            dxh = dln * g
            dhc = rstd * (dxh - jnp.mean(dxh, axis=-1, keepdims=True) - xh * jnp.mean(dxh * xh, axis=-1, keepdims=True))
            o_ref[...] = dhc
            sums[0] += _fold8(dhc)
            sums[1] += _fold8(dln * xh)
            sums[2] += _fold8(dln)

        @pl.when(jnp.logical_not(inside))
        def _():
            o_ref[...] = jnp.zeros_like(o_ref)

    vec = pl.BlockSpec((1, CONV_WIDTH), lambda i: (0, 0))
    return pl.pallas_call(
        body, name="conv_bwd_norm", grid=(nb + 2,),
        in_specs=[pl.BlockSpec((blk, CONV_WIDTH), lambda i: (jnp.clip(i - 1, 0, nb - 1), 1)),
                  pl.BlockSpec((blk, CONV_WIDTH), lambda i: (jnp.clip(i - 1, 0, nb - 1), 0)), vec, vec],
        out_specs=[pl.BlockSpec((blk, CONV_WIDTH), lambda i: (i, 0)),
                   pl.BlockSpec((3, SUBLANES, CONV_WIDTH), lambda i: (0, 0, 0))],
        out_shape=[_sds(((nb + 2) * blk, CONV_WIDTH)), _sds((3, SUBLANES, CONV_WIDTH))],
        compiler_params=_params(("arbitrary",)),
    )(d_ycat, hc, ln_g, ln_b)


def _conv_bwd_taps(dhc_pad, hh_pad, z_all, w, n_rows):
    blk = _conv_block(n_rows)
    nblk = n_rows // blk + 2

    def body(dhc_ref, hh_ref, v_ref, g_ref, w_ref, dv_ref, dg_ref, dw_ref, dwin, hwin, dsems, hsems):
        @pl.when(pl.program_id(0) == 0)
        def _():
            dw_ref[...] = jnp.zeros_like(dw_ref)

        base = _stream_padded(dhc_ref, dwin, dsems, blk, nblk)
        _stream_padded(hh_ref, hwin, hsems, blk, nblk)

        def tile(t, _):
            r0 = pl.multiple_of(t * CONV_BWD_ROWS, CONV_BWD_ROWS) + base
            dh = dwin[pl.ds(r0 + blk, CONV_BWD_ROWS), :]
            acc = jnp.zeros((CONV_BWD_ROWS, CONV_WIDTH), F32)
            for k in range(CONV_K):
                off = (k - CONV_HALF) * GRID_W
                acc = acc + w_ref[k:k + 1, :] * dwin[pl.ds(r0 + blk - off, CONV_BWD_ROWS), :]
                dw_ref[k] += _fold8(dh * hwin[pl.ds(r0 + blk + off, CONV_BWD_ROWS), :])
            rs = pl.ds(pl.multiple_of(t * CONV_BWD_ROWS, CONV_BWD_ROWS), CONV_BWD_ROWS)
            sg = _sigmoid(g_ref[rs, :])
            vv = v_ref[rs, :]
            dv_ref[rs, :] = acc * sg
            dg_ref[rs, :] = acc * vv * sg * (1.0 - sg)
            return 0

        lax.fori_loop(0, blk // CONV_BWD_ROWS, tile, 0)

    row = pl.BlockSpec((blk, CONV_WIDTH), lambda i: (i, 0))
    return pl.pallas_call(
        body, name="conv_bwd_taps", grid=(n_rows // blk,),
        in_specs=[ANY, ANY,
            pl.BlockSpec((blk, CONV_WIDTH), lambda i: (i, 1)), pl.BlockSpec((blk, CONV_WIDTH), lambda i: (i, 2)),
            pl.BlockSpec((CONV_K, CONV_WIDTH), lambda i: (0, 0))],
        out_specs=[row, row, pl.BlockSpec((CONV_K, SUBLANES, CONV_WIDTH), lambda i: (0, 0, 0))],
        out_shape=[_sds((n_rows, CONV_WIDTH)), _sds((n_rows, CONV_WIDTH)), _sds((CONV_K, SUBLANES, CONV_WIDTH))],
        scratch_shapes=[pltpu.VMEM((nblk * blk, CONV_WIDTH), F32), pltpu.VMEM((nblk * blk, CONV_WIDTH), F32),
                        pltpu.SemaphoreType.DMA((nblk,)), pltpu.SemaphoreType.DMA((nblk,))],
        compiler_params=_params(("arbitrary",)),
    )(dhc_pad, hh_pad, z_all, z_all, w)


def _dz_assemble(du0, du1, dy, d_skip, dv, dgate, n_lat):
    rows = du0.shape[0]
    nb = rows // ROW_BLOCK

    w = S5_WIDTH

    def body(a_ref, b_ref, dy_ref, d_ref, dv_ref, dg_ref, o_ref):
        lat = pl.program_id(0) < n_lat

        @pl.when(lat)
        def _():
            o_ref[:, 0:w] = (a_ref[...] + b_ref[...] + dy_ref[...] * d_ref[...]).astype(o_ref.dtype)
            o_ref[:, w:2 * w] = dv_ref[...].astype(o_ref.dtype)
            o_ref[:, 2 * w:3 * w] = dg_ref[...].astype(o_ref.dtype)

        @pl.when(jnp.logical_not(lat))
        def _():
            o_ref[:, 0:w] = (a_ref[...] + b_ref[...]).astype(o_ref.dtype)
            o_ref[:, w:3 * w] = jnp.zeros((ROW_BLOCK, 2 * w), o_ref.dtype)

    all_rows = pl.BlockSpec((ROW_BLOCK, w), lambda i: (i, 0))
    lat_rows = pl.BlockSpec((ROW_BLOCK, w), lambda i: (jnp.minimum(i, n_lat - 1), 0))
    return pl.pallas_call(
        body, name="dz_assemble", grid=(nb,),
        in_specs=[all_rows, all_rows, lat_rows, pl.BlockSpec((1, w), lambda i: (0, 0)), lat_rows, lat_rows],
        out_specs=pl.BlockSpec((ROW_BLOCK, IN_COLS), lambda i: (i, 0)),
        out_shape=_sds((rows, IN_COLS), BF16), compiler_params=_params(("parallel",)),
    )(du0, du1, dy, d_skip, dv, dgate)


def _sum_parts(parts):
    _, r, c = parts.shape

    def body(p_ref, o_ref):
        acc = p_ref[0]
        for q in range(1, NDEV):
            acc = acc + p_ref[q]
        o_ref[...] = acc

    return pl.pallas_call(body, name="sum_parts", out_shape=_sds((r, c)), compiler_params=_params())(parts)


def _row_tile(r, c):
    best = r
    for t in (1024, 512, 256, 128, 64, 32, 16, 8):
        if r % t == 0 and t * c <= 128 * 1024:
            return t
    return best


def _adamw(name, w, gparts, m, v):
    r, c = w.shape
    np_ = gparts.shape[0]
    tr = _row_tile(r, c)

    def body(w_ref, g_ref, m_ref, v_ref, go_ref, d_ref, mo_ref, vo_ref):
        g = g_ref[0].astype(F32)
        for q in range(1, np_):
            g = g + g_ref[q].astype(F32)
        m2 = ADAM_B1 * m_ref[...] + (1.0 - ADAM_B1) * g
        v2 = ADAM_B2 * v_ref[...] + (1.0 - ADAM_B2) * jnp.square(g)
        m_hat = m2 / (1.0 - ADAM_B1 ** ADAM_STEP)
        v_hat = v2 / (1.0 - ADAM_B2 ** ADAM_STEP)
        go_ref[...] = g
        d_ref[...] = -ADAM_LR * (m_hat / (jnp.sqrt(v_hat) + ADAM_EPS) + ADAM_WD * w_ref[...])
        mo_ref[...] = m2
        vo_ref[...] = v2

    row = pl.BlockSpec((tr, c), lambda i: (i, 0))
    return pl.pallas_call(
        body, name=name, grid=(r // tr,),
        in_specs=[row, pl.BlockSpec((np_, tr, c), lambda i: (0, i, 0)), row, row],
        out_specs=[row] * 4, out_shape=[_sds((r, c))] * 4, compiler_params=_params(("parallel",)),
    )(w, gparts, m, v)


def _adamw_native(name, w, g, m, v):
    def body(w_ref, g_ref, m_ref, v_ref, d_ref, mo_ref, vo_ref):
        gv = g_ref[...]
        m2 = ADAM_B1 * m_ref[...] + (1.0 - ADAM_B1) * gv
        v2 = ADAM_B2 * v_ref[...] + (1.0 - ADAM_B2) * jnp.square(gv)
        m_hat = m2 / (1.0 - ADAM_B1 ** ADAM_STEP)
        v_hat = v2 / (1.0 - ADAM_B2 ** ADAM_STEP)
        d_ref[...] = -ADAM_LR * (m_hat / (jnp.sqrt(v_hat) + ADAM_EPS) + ADAM_WD * w_ref[...])
        mo_ref[...] = m2
        vo_ref[...] = v2

    return pl.pallas_call(body, name=name, out_shape=[_sds(w.shape)] * 3, compiler_params=_params())(w, g, m, v)


SMALL = ["c_ctx", "ada_b", "norm1_g", "s5_lam_re", "s5_lam_im", "s5_log_dt", "s5_d", "conv_b", "conv_ln_g", "conv_ln_b",
         "norm2_g", "final_g"]
SMALL_PACKED_ROWS = 24


def _pack_rows(parts, rows):
    flat = jnp.concatenate([p.reshape(-1).astype(F32) for p in parts])
    return jnp.pad(flat, (0, rows * D_MODEL - flat.shape[0])).reshape(rows, D_MODEL)


def _unpack_rows(packed, shapes):
    flat = packed.reshape(-1)
    out, off = [], 0
    for shape in shapes:
        size = 1
        for s in shape:
            size *= s
        out.append(flat[off:off + size].reshape(shape))
        off += size
    return out


def kernel(x, c, ctx, c_ctx, ada_w, ada_b, norm1_g, w_in, s5_lam_re, s5_lam_im, s5_log_dt, s5_b_re, s5_b_im, s5_c_re, s5_c_im, s5_d, s5_w_glu, conv_w, conv_b, conv_ln_g, conv_ln_b, w_out, norm2_g, mlp_w1, mlp_w2, final_g, loss_target, m_c_ctx, m_ada_w, m_ada_b, m_norm1_g, m_w_in, m_s5_lam_re, m_s5_lam_im, m_s5_log_dt, m_s5_b_re, m_s5_b_im, m_s5_c_re, m_s5_c_im, m_s5_d, m_s5_w_glu, m_conv_w, m_conv_b, m_conv_ln_g, m_conv_ln_b, m_w_out, m_norm2_g, m_mlp_w1, m_mlp_w2, m_final_g, v_c_ctx, v_ada_w, v_ada_b, v_norm1_g, v_w_in, v_s5_lam_re, v_s5_lam_im, v_s5_log_dt, v_s5_b_re, v_s5_b_im, v_s5_c_re, v_s5_c_im, v_s5_d, v_s5_w_glu, v_conv_w, v_conv_b, v_conv_ln_g, v_conv_ln_b, v_w_out, v_norm2_g, v_mlp_w1, v_mlp_w2, v_final_g):
    weights = dict(c_ctx=c_ctx, ada_w=ada_w, ada_b=ada_b, norm1_g=norm1_g, w_in=w_in, s5_lam_re=s5_lam_re, s5_lam_im=s5_lam_im, s5_log_dt=s5_log_dt, s5_b_re=s5_b_re, s5_b_im=s5_b_im, s5_c_re=s5_c_re, s5_c_im=s5_c_im, s5_d=s5_d, s5_w_glu=s5_w_glu, conv_w=conv_w, conv_b=conv_b, conv_ln_g=conv_ln_g, conv_ln_b=conv_ln_b, w_out=w_out, norm2_g=norm2_g, mlp_w1=mlp_w1, mlp_w2=mlp_w2, final_g=final_g)
    mom1 = dict(c_ctx=m_c_ctx, ada_w=m_ada_w, ada_b=m_ada_b, norm1_g=m_norm1_g, w_in=m_w_in, s5_lam_re=m_s5_lam_re, s5_lam_im=m_s5_lam_im, s5_log_dt=m_s5_log_dt, s5_b_re=m_s5_b_re, s5_b_im=m_s5_b_im, s5_c_re=m_s5_c_re, s5_c_im=m_s5_c_im, s5_d=m_s5_d, s5_w_glu=m_s5_w_glu, conv_w=m_conv_w, conv_b=m_conv_b, conv_ln_g=m_conv_ln_g, conv_ln_b=m_conv_ln_b, w_out=m_w_out, norm2_g=m_norm2_g, mlp_w1=m_mlp_w1, mlp_w2=m_mlp_w2, final_g=m_final_g)
    mom2 = dict(c_ctx=v_c_ctx, ada_w=v_ada_w, ada_b=v_ada_b, norm1_g=v_norm1_g, w_in=v_w_in, s5_lam_re=v_s5_lam_re, s5_lam_im=v_s5_lam_im, s5_log_dt=v_s5_log_dt, s5_b_re=v_s5_b_re, s5_b_im=v_s5_b_im, s5_c_re=v_s5_c_re, s5_c_im=v_s5_c_im, s5_d=v_s5_d, s5_w_glu=v_s5_w_glu, conv_w=v_conv_w, conv_b=v_conv_b, conv_ln_g=v_conv_ln_g, conv_ln_b=v_conv_ln_b, w_out=v_w_out, norm2_g=v_norm2_g, mlp_w1=v_mlp_w1, mlp_w2=v_mlp_w2, final_g=v_final_g)
    order = list(weights)

    me = 4 * lax.axis_index("x") + 2 * lax.axis_index("y") + lax.axis_index("c")
    xs, cs, tgt = x[0], ctx[0], loss_target[0]
    n_lat_rows, n_ctx_rows = xs.shape[0], cs.shape[0]
    n_rows = n_lat_rows + n_ctx_rows
    n_lat = n_lat_rows // ROW_BLOCK
    ada_cols = ada_w.shape[2]

    (c_all,), _ = _exchange("gather_c", [c], [True])
    c_all = c_all.reshape(NDEV, D_MODEL)

    cond_fwd = jnp.concatenate([c_all, c_ctx[None], jnp.zeros((7, D_MODEL), F32)])
    ada_b_loc = lax.dynamic_slice(ada_b, (0, me * ada_cols), (1, ada_cols))
    (mod_g,), mod_token = _exchange("gather_mod", [_ada_fwd(cond_fwd, ada_w[0], ada_b_loc)], [True])
    weight_groups, weights_token = _exchange_start_groups("gather_weights_start", [
        ([w_in[0].astype(BF16)], [True]),
        ([s5_w_glu[0].astype(BF16), conv_w[0] + mod_token[0:1, 0:1], w_out[0].astype(BF16)], [True] * 3),
        ([mlp_w1[0].astype(BF16), mlp_w2[0].astype(BF16)], [True] * 2)])
    (wi_send, wi_recv, wi_src, wi_land), (mixer_send, mixer_recv, mixer_src, mixer_land), \
        (mlpw_send, mlpw_recv, mlpw_src, mlpw_land) = weight_groups
    mod_rows = jnp.transpose(mod_g, (1, 0, 2)).reshape(16, 6 * D_MODEL) + weights_token[0:1, 0:1]
    mod = lax.dynamic_slice(mod_rows, (me, 0), (1, 6 * D_MODEL)).reshape(6, D_MODEL)
    modc = mod_rows[8, :2 * D_MODEL].reshape(2, D_MODEL)
    sh1, sc1, g1, sh2, sc2, g2 = [mod[i:i + 1] for i in range(6)]

    lam_re, lam_im = s5_lam_re[0].reshape(2, 1, NSTATE), s5_lam_im[0].reshape(2, 1, NSTATE)
    ldt = jnp.repeat(s5_log_dt[0], S5_STATE, axis=-1).reshape(2, 1, NSTATE)
    bt_re = jnp.transpose(s5_b_re[0], (0, 3, 1, 2)).reshape(2, S5_GROUP, NSTATE)
    bt_im = jnp.transpose(s5_b_im[0], (0, 3, 1, 2)).reshape(2, S5_GROUP, NSTATE)
    groups_per_block = S5_GROUPS // S5_BLOCKS
    ct_re = jnp.tile(s5_c_re[0].reshape(2, S5_WIDTH, S5_STATE), (1, 1, groups_per_block))
    ct_im = jnp.tile(s5_c_im[0].reshape(2, S5_WIDTH, S5_STATE), (1, 1, groups_per_block))
    d_skip = s5_d[0].reshape(1, S5_WIDTH)
    perms = [_segment_permutation(reverse_time=(d == 0)) for d in range(2)]
    perms_t = [p.T for p in perms]
    disc = [_s5_discretise(f"s5_disc{d}", False, lam_re[d], lam_im[d], ldt[d], bt_re[d], bt_im[d], ct_re[d], ct_im[d])
            for d in range(2)]

    a_all = _prenorm("prenorm1", xs, cs, norm1_g, jnp.stack([mod[0:2], modc]))
    before_w_in = a_all[0:SUBLANES, 0:LANES].astype(F32) + disc[0][0][0:SUBLANES, 0:LANES] + disc[1][0][0:SUBLANES, 0:LANES]
    wi_own, wi_landed = _exchange_wait("gather_w_in_wait", wi_send, wi_recv, wi_src, wi_land, [True], before_w_in)
    w_in_full = jnp.transpose(_with_own(wi_landed[0], wi_own[0], me), (1, 0, 2)).reshape(D_MODEL, IN_COLS)
    tm_all = 1088 if n_rows % 1088 == 0 else ROW_BLOCK
    (z_all,) = _matmul("in_proj", a_all, w_in_full, "nn", (n_rows, IN_COLS, D_MODEL), (tm_all, IN_COLS, D_MODEL),
                       [((n_rows, IN_COLS), F32)])

    states, y_dir = [], []
    for d in range(2):
        _, tab, _, bmat, cmat = disc[d]
        s, yd = _s5_scan_fwd(f"s5_scan_fwd{d}", d == 0, z_all, bmat, cmat, tab, perms[d], perms_t[d])
        states.append(s)
        y_dir.append(yd)
    mixer_own, mixer_landed = _exchange_wait("gather_mixer_wait", mixer_send, mixer_recv, mixer_src, mixer_land,
                                             [True] * 3, y_dir[1])
    glu_g, conv_w_g, w_out_g = [_with_own(l, o, me) for l, o in zip(mixer_landed, mixer_own)]
    glu_full = glu_g.reshape(S5_WIDTH, S5_WIDTH)
    conv_w_full = jnp.transpose(conv_w_g, (1, 0, 2)).reshape(CONV_K, CONV_WIDTH)
    w_out_full = w_out_g.reshape(D_MODEL, D_MODEL)
    ycat = _glu_fwd(z_all, y_dir[0], y_dir[1], d_skip, glu_full, n_lat_rows)

    hh_pad = _conv_gate(z_all, n_lat_rows)
    hc, ycat = _conv_fwd(hh_pad, conv_w_full, conv_b, conv_ln_g, conv_ln_b, ycat, n_lat_rows)

    tm = min(1024, n_lat_rows)
    tm_e = min(512, n_lat_rows)
    w1_cols = D_FF // NDEV
    row_vec = lambda tn: pl.BlockSpec((1, tn), lambda i, j, k: (0, j))
    out_tile = lambda t_m, t_n: pl.BlockSpec((t_m, t_n), lambda i, j, k: (i, j))
    full_rows = ((n_lat_rows, D_MODEL), F32)
    sums = ((n_lat_rows // tm_e, SUBLANES, D_MODEL), F32)
    sums_spec = pl.BlockSpec((None, SUBLANES, D_MODEL), lambda i, j, k: (i, 0, 0))
    vec = lambda v: (v, row_vec(D_MODEL))
    transposed_tile = lambda t_m, t_n: pl.BlockSpec((t_n, t_m), lambda i, j, k: (j, i))
    mix, h1, a2, a2_t = _matmul(
        "out_proj", ycat, w_out_full, "nn", (n_lat_rows, D_MODEL, D_MODEL), (tm_e, D_MODEL, D_MODEL),
        [full_rows, full_rows, ((n_lat_rows, D_MODEL), BF16), ((D_MODEL, n_lat_rows), BF16)],
        epi=_epi_residual_prenorm,
        epi_extra=[(xs, out_tile(tm_e, D_MODEL)), vec(g1), vec(norm2_g), vec(sc2), vec(sh2)],
        out_specs=[out_tile(tm_e, D_MODEL)] * 3 + [transposed_tile(tm_e, D_MODEL)])
    mlpw_own, mlpw_landed = _exchange_wait("gather_mlp_wait", mlpw_send, mlpw_recv, mlpw_src, mlpw_land, [True] * 2, a2)
    w1_g, w2_g = [_with_own(l, o, me) for l, o in zip(mlpw_landed, mlpw_own)]
    w2_full = w2_g.reshape(D_FF, D_MODEL)
    tm_up = min(2048, n_lat_rows)
    f, f_t = _matmul("mlp_up", a2, w1_g, "nn", (n_lat_rows, D_FF, D_MODEL), (tm_up, w1_cols, D_MODEL),
                     [((n_lat_rows, D_FF), BF16), ((D_FF, n_lat_rows), BF16)], epi=lambda acc: (acc, acc.T),
                     b_spec=pl.BlockSpec((None, D_MODEL, w1_cols), lambda i, j, k: (j, 0, 0)),
                     out_specs=[out_tile(tm_up, w1_cols), transposed_tile(tm_up, w1_cols)])
    sq_relu = lambda t: jnp.square(jnp.maximum(t, 0.0))
    mlp_out, d_h2, dm2, err_sums, d_final_g8 = _matmul(
        "mlp_down", f, w2_full, "nn", (n_lat_rows, D_MODEL, D_FF), (tm_e, D_MODEL, 2048),
        [full_rows, full_rows, ((n_lat_rows, D_MODEL), BF16), sums, sums], a_fn=sq_relu, epi=_epi_residual_loss,
        epi_extra=[(h1, out_tile(tm_e, D_MODEL)), vec(g2), (tgt, out_tile(tm_e, D_MODEL)), vec(final_g[None])],
        out_specs=[out_tile(tm_e, D_MODEL)] * 3 + [sums_spec] * 2)

    (d_f,) = _matmul("mlp_down_dx", dm2, w2_full, "nt", (n_lat_rows, D_FF, D_MODEL), (tm_up, 512, D_MODEL),
                     [((n_lat_rows, D_FF), BF16)],
                     epi=lambda acc, ft: (acc * 2.0 * jnp.maximum(ft.astype(F32), 0.0),),
                     epi_extra=[(f, out_tile(tm_up, 512))])
    tk_dw = min(2048, n_lat_rows)
    (g_w2,) = _matmul("mlp_down_dw", f_t, dm2, "nn", (D_FF, D_MODEL, n_lat_rows), (1024, D_MODEL, tk_dw),
                      [((D_FF, D_MODEL), F32)], a_fn=sq_relu)
    (g_w1,) = _matmul("mlp_up_dw", a2_t, d_f, "nn", (D_MODEL, D_FF, n_lat_rows), (D_MODEL, w1_cols, n_lat_rows),
                      [((NDEV, D_MODEL, w1_cols), F32)],
                      out_specs=[pl.BlockSpec((None, D_MODEL, w1_cols), lambda i, j, k: (j, 0, 0))])
    mlp_send, mlp_recv, mlp_src, mlp_land, mlp_token = _exchange_start(
        "scatter_mlp_start", [g_w1, g_w2.reshape(NDEV, D_FF // NDEV, D_MODEL)], [False] * 2)
    d_h1, dm1, *sums2 = _matmul(
        "mlp_up_dx", d_f, w1_g, "nt", (n_lat_rows, D_MODEL, D_FF), (tm_e, D_MODEL, 4 * w1_cols),
        [full_rows, ((n_lat_rows, D_MODEL), BF16)] + [sums] * 4, epi=_epi_norm_bwd,
        epi_extra=[(h1, out_tile(tm_e, D_MODEL)), (d_h2, out_tile(tm_e, D_MODEL)), (mlp_out, out_tile(tm_e, D_MODEL)),
                   vec(norm2_g), vec(sc2 + mlp_token[0:1, 0:1]), vec(g1)],
        b_spec=pl.BlockSpec((4, D_MODEL, w1_cols), lambda i, j, k: (k, 0, 0)), b_slabs=4,
        out_specs=[out_tile(tm_e, D_MODEL)] * 2 + [sums_spec] * 4)

    (d_ycat,) = _matmul("out_proj_dx", dm1, w_out_full, "nt", (n_lat_rows, D_MODEL, D_MODEL), (tm, D_MODEL, D_MODEL),
                        [((n_lat_rows, D_MODEL), F32)])
    (g_w_out,) = _matmul("out_proj_dw", ycat, dm1, "tn", (D_MODEL, D_MODEL, n_lat_rows), (D_MODEL, D_MODEL, 512),
                         [((D_MODEL, D_MODEL), F32)])

    dy, g_glu, dd8 = _glu_bwd(d_ycat, z_all, y_dir[0], y_dir[1], d_skip, glu_full, n_lat_rows)
    proj_send, proj_recv, proj_src, proj_land, proj_token = _exchange_start(
        "scatter_proj_start",
        [g_w_out.reshape(NDEV, D_MODEL // NDEV, D_MODEL), g_glu.reshape(NDEV, S5_WIDTH // NDEV, S5_WIDTH)], [False] * 2)
    perms = [p + proj_token[0:1, 0:1].astype(BF16) for p in perms]
    du, g_lam_re, g_lam_im, g_ldt, g_bt, g_cdiag = [], [], [], [], [], []
    for d in range(2):
        _, _, adj, bmat, cmat = disc[d]
        du_d, d_bdiag, d_cdiag, d_abar8 = _s5_scan_bwd(f"s5_scan_bwd{d}", d == 0, dy, z_all, states[d], bmat, cmat, adj,
                                                       perms[d], perms_t[d])
        du.append(du_d)
        d_bbar = jnp.transpose(d_bdiag.reshape(S5_BLOCKS, S5_GROUP, 2, NSTATE // S5_BLOCKS), (2, 1, 0, 3)).reshape(
            2 * S5_GROUP, NSTATE)
        d_lam8, d_bt = _s5_discretise_bwd(f"s5_disc_bwd{d}", lam_re[d], lam_im[d], ldt[d], bt_re[d], bt_im[d], d_abar8, d_bbar)
        g_lam_re.append(d_lam8[0].reshape(S5_GROUPS, S5_STATE))
        g_lam_im.append(d_lam8[1].reshape(S5_GROUPS, S5_STATE))
        g_ldt.append(d_lam8[2].reshape(S5_GROUPS, S5_STATE).sum(axis=-1))
        g_bt.append(d_bt)
        g_cdiag.append(d_cdiag)

    dhc_pad, conv_sums = _conv_bwd_norm(d_ycat, hc, conv_ln_g, conv_ln_b, n_lat_rows)
    d_v, d_gate, g_conv_w8 = _conv_bwd_taps(dhc_pad, hh_pad, z_all, conv_w_full, n_lat_rows)

    dz_all = _dz_assemble(du[0], du[1], dy, d_skip, d_v, d_gate, n_lat)
    (g_w_in_full,) = _matmul("in_proj_dw", a_all, dz_all, "tn", (D_MODEL, IN_COLS, n_rows), (D_MODEL, IN_COLS, tm_all),
                             [((D_MODEL, IN_COLS), F32)])
    g_w_in_parts = jnp.transpose(g_w_in_full.reshape(D_MODEL, NDEV, IN_COLS // NDEV), (1, 0, 2)).astype(BF16)
    win_send, win_recv, win_src, win_land, win_token = _exchange_start("scatter_w_in_start", [g_w_in_parts], [False])
    w_in_late = w_in_full + win_token[0:1, 0:1].astype(BF16)
    grad_x, *sums1 = _matmul(
        "in_proj_dx", dz_all, w_in_late, "nt", (n_lat_rows, D_MODEL, IN_COLS), (tm_e, D_MODEL, IN_COLS),
        [full_rows] + [sums] * 4, epi=_epi_norm_bwd,
        epi_extra=[(xs, out_tile(tm_e, D_MODEL)), (d_h1, out_tile(tm_e, D_MODEL)), (mix, out_tile(tm_e, D_MODEL)),
                   vec(norm1_g), vec(sc1)],
        out_specs=[out_tile(tm_e, D_MODEL)] + [sums_spec] * 4)
    (d_a_ctx,) = _matmul("in_proj_dx_ctx", dz_all, w_in_late, "nt", (n_ctx_rows, D_MODEL, IN_COLS),
                         (ROW_BLOCK, D_MODEL, IN_COLS), [((n_ctx_rows, D_MODEL), F32)],
                         a_spec=pl.BlockSpec((ROW_BLOCK, IN_COLS), lambda i, j, k: (i + n_lat, 0)))
    (sums1c,) = _norm_bwd("norm1_bwd_ctx", cs, d_a_ctx, 0, norm1_g, modc[1:2])

    s1, s1c, s2 = [p.sum(axis=(0, 1)) for p in sums1], sums1c.sum(axis=1), [p.sum(axis=(0, 1)) for p in sums2]
    d_mod = jnp.concatenate([s1[0], s1[1], s1[3], s2[0], s2[1], s2[3]])
    d_modc = jnp.concatenate([s1c[0], s1c[1], jnp.zeros((4 * D_MODEL,), F32)])
    (dmod_g,), _ = _exchange("gather_dmod", [jnp.stack([d_mod, d_modc])], [True])
    dmod16 = jnp.concatenate([dmod_g[:, 0], dmod_g[:, 1]])
    dmod16_loc = lax.dynamic_slice(dmod16, (0, me * ada_cols), (16, ada_cols))
    cond_bwd = jnp.concatenate([c_all, jnp.broadcast_to(c_ctx[None], (NDEV, D_MODEL))])
    g_ada_w, g_c_ctx8 = _ada_bwd(cond_bwd, dmod16_loc, ada_w[0], c_ctx[None])

    small_parts = dict(
        c_ctx=g_c_ctx8[0], ada_b=d_mod + d_modc, norm1_g=s1[2] + s1c[2],
        s5_lam_re=jnp.stack(g_lam_re), s5_lam_im=jnp.stack(g_lam_im), s5_log_dt=jnp.stack(g_ldt),
        s5_d=dd8.sum(axis=0), conv_b=conv_sums[0].sum(axis=0), conv_ln_g=conv_sums[1].sum(axis=0),
        conv_ln_b=conv_sums[2].sum(axis=0), norm2_g=s2[2], final_g=d_final_g8.sum(axis=(0, 1)))
    reduced_shapes = [(SMALL_PACKED_ROWS, D_MODEL), (2, 2 * S5_GROUP, NSTATE), (2,) + _S5_DIAG, (1,)]
    small_g = _pack_rows(
        [_pack_rows([small_parts[n] for n in SMALL], SMALL_PACKED_ROWS), jnp.stack(g_bt), jnp.stack(g_cdiag),
         (0.5 / D_MODEL * jnp.sum(err_sums)).reshape(1)], SMALL_ROWS).reshape(NDEV, SMALL_ROWS // NDEV, D_MODEL)
    g_conv_w_parts = jnp.transpose(g_conv_w8.sum(axis=1).reshape(CONV_K, NDEV, CONV_WIDTH // NDEV), (1, 0, 2))

    res = {}

    def own_chunk(src):
        return lax.dynamic_index_in_dim(src, me, 0, keepdims=False)

    def adamw_big(name, parts):
        outs = _adamw("adamw_" + name, weights[name][0], parts, mom1[name][0], mom2[name][0])
        res[name] = [o[None] for o in outs]
        return outs[0]

    sm_send, sm_recv, sm_src, sm_land, sm_token = _exchange_start("scatter_small_start", [g_conv_w_parts, small_g],
                                                                  [False] * 2)
    mlp_src, mlp_landed = _exchange_wait("scatter_mlp_wait", mlp_send, mlp_recv, mlp_src, mlp_land, [False] * 2, sm_token)
    p_w1, p_w2 = [_with_own(l, own_chunk(s), me) for l, s in zip(mlp_landed, mlp_src)]
    adamw_big("ada_w", g_ada_w[None])
    adamw_big("mlp_w1", p_w1)
    done = adamw_big("mlp_w2", p_w2)
    sm_src, sm_landed = _exchange_wait("scatter_small_wait", sm_send, sm_recv, sm_src, sm_land, [False] * 2, done)
    p_conv_w, p_small = [_with_own(l, own_chunk(s), me) for l, s in zip(sm_landed, sm_src)]
    ga_send, ga_recv, ga_src, ga_land, ga_token = _exchange_start("gather_small_start", [_sum_parts(p_small)], [True])
    proj_src, proj_landed = _exchange_wait("scatter_proj_wait", proj_send, proj_recv, proj_src, proj_land, [False] * 2,
                                           ga_token)
    p_w_out, p_glu = [_with_own(l, own_chunk(s), me) for l, s in zip(proj_landed, proj_src)]
    adamw_big("w_out", p_w_out)
    done = adamw_big("s5_w_glu", p_glu)
    win_src, win_landed = _exchange_wait("scatter_w_in_wait", win_send, win_recv, win_src, win_land, [False], done)
    adamw_big("w_in", _with_own(win_landed[0], own_chunk(win_src[0]), me))
    done = adamw_big("conv_w", p_conv_w)
    ga_own, ga_landed = _exchange_wait("gather_small_wait", ga_send, ga_recv, ga_src, ga_land, [True], done)
    small_all = _with_own(ga_landed[0], ga_own[0], me).reshape(1, SMALL_ROWS, D_MODEL)
    _, r_bt, r_cdiag, loss = _unpack_rows(small_all, reduced_shapes)
    loss = loss.reshape(())
    pack = lambda src: _pack_rows([src[n] for n in SMALL], SMALL_PACKED_ROWS)
    outs = _adamw("adamw_small", pack(weights), small_all, pack(mom1), pack(mom2))
    unpacked = [_unpack_rows(o, [weights[n].shape for n in SMALL]) for o in outs]
    for i, name in enumerate(SMALL):
        res[name] = [u[i] for u in unpacked]
    to_ghp = lambda t: jnp.transpose(t.reshape(2, S5_GROUP, S5_GROUPS, S5_STATE), (0, 2, 1, 3))[None]
    r_c = jnp.transpose(r_cdiag.reshape(2, S5_BLOCKS, S5_GROUP, 2, groups_per_block, S5_STATE), (3, 0, 1, 4, 2, 5)).reshape(
        2, 1, 2, S5_GROUPS, S5_GROUP, S5_STATE)
    swap = lambda t: jnp.swapaxes(t, -1, -2)
    for name, grad in (("s5_b_re", to_ghp(r_bt[:, :S5_GROUP])), ("s5_b_im", to_ghp(r_bt[:, S5_GROUP:]))):
        outs = _adamw_native("adamw_" + name, swap(weights[name]), grad, swap(mom1[name]), swap(mom2[name]))
        res[name] = [swap(grad), *[swap(o) for o in outs]]
    for name, grad in (("s5_c_re", r_c[0]), ("s5_c_im", -r_c[1])):
        res[name] = [grad, *_adamw_native("adamw_" + name, weights[name], grad, mom1[name], mom2[name])]

    return (loss, grad_x[None], *[res[n][0] for n in order], *[res[n][1] for n in order],
            *[res[n][2] for n in order], *[res[n][3] for n in order])
```

```python
import jax
import jax.numpy as jnp
from jax import lax
from jax.experimental import pallas as pl
from jax.experimental.pallas import tpu as pltpu

F32 = jnp.float32
BF16 = jnp.bfloat16
MESH = pl.DeviceIdType.MESH
ANY = pl.BlockSpec(memory_space=pl.ANY)

NDEV = 8
D_MODEL = 1024
GRID_W = 64
S5_WIDTH = 512
S5_GROUP = 16
S5_GROUPS = 32
S5_STATE = 64
NSTATE = S5_GROUPS * S5_STATE
CONV_WIDTH = 512
CONV_K = 31
IN_COLS = S5_WIDTH + 2 * CONV_WIDTH
D_FF = 4 * D_MODEL
EPS_RMS = 1e-6
EPS_LN = 1e-5
ADAM_LR = 0.001
ADAM_B1 = 0.9
ADAM_B2 = 0.999
ADAM_EPS = 1e-08
ADAM_WD = 0.01
ADAM_STEP = 10

SUBLANES = 8
LANES = 128
ROW_BLOCK = 256
SCAN_LANES = 512
SCAN_UNROLL = 32
SEGMENTS = SUBLANES
STEPS = ROW_BLOCK // SEGMENTS
S5_BLOCKS = 4
S5_BLOCK_WIDTH = S5_WIDTH // S5_BLOCKS
CONV_ROWS = 64
CONV_BWD_ROWS = 32
VMEM_LIMIT = 48 * 1024 * 1024
SMALL_ROWS = 320


def _params(sem=None):
    kw = dict(vmem_limit_bytes=VMEM_LIMIT)
    if sem is not None:
        kw["dimension_semantics"] = sem
    return pltpu.CompilerParams(**kw)


def _sds(shape, dtype=F32):
    return jax.ShapeDtypeStruct(tuple(shape), dtype)


def _fold8(x):
    return x.reshape(x.shape[0] // SUBLANES, SUBLANES, x.shape[1]).sum(axis=0)


def _sigmoid(x):
    return 1.0 / (1.0 + jnp.exp(-x))


def _silu(x):
    return x * _sigmoid(x)


def _dsilu(x):
    s = _sigmoid(x)
    return s * (1.0 + x * (1.0 - s))


_GELU_C = 0.7978845608028654


def _gelu(x):
    return 0.5 * x * (1.0 + jnp.tanh(_GELU_C * (x + 0.044715 * x * x * x)))


def _dgelu(x):
    t = jnp.tanh(_GELU_C * (x + 0.044715 * x * x * x))
    return 0.5 * (1.0 + t) + 0.5 * x * (1.0 - t * t) * _GELU_C * (1.0 + 3.0 * 0.044715 * x * x)


def _rms(x):
    rstd = lax.rsqrt(jnp.mean(x * x, axis=-1, keepdims=True) + EPS_RMS)
    return x * rstd, rstd


def _epi_residual_prenorm(acc, res, gate, gain, scale, shift):
    h = res + gate * acc
    xh, _ = _rms(h)
    a = (xh * gain) * (1.0 + scale) + shift
    return acc, h, a, a.T


def _epi_residual_loss(acc, res, gate, target, gain):
    h = res + gate * acc
    xh, rstd = _rms(h)
    err = xh * gain - target
    dy = err * (1.0 / h.shape[-1])
    dxh = dy * gain
    dh = rstd * (dxh - xh * jnp.mean(dxh * xh, axis=-1, keepdims=True))
    return acc, dh, dh * gate, _fold8(err * err), _fold8(dy * xh)


def _epi_norm_bwd(d_act, x, res, aux, gain, scale, gate=None):
    xh, rstd = _rms(x)
    dn = d_act * (1.0 + scale)
    dxh = dn * gain
    dx = res + rstd * (dxh - xh * jnp.mean(dxh * xh, axis=-1, keepdims=True))
    sums = (_fold8(d_act), _fold8(d_act * (xh * gain)), _fold8(dn * xh), _fold8(res * aux))
    return (dx, *sums) if gate is None else (dx, dx * gate, *sums)


def _dot(a, b, mode):
    dims = {"nn": (((1,), (0,)), ((), ())), "nt": (((1,), (1,)), ((), ())), "tn": (((0,), (0,)), ((), ()))}[mode]
    return lax.dot_general(a, b, dims, preferred_element_type=F32)


def _peers(x, y, c):
    out = []
    for k in range(1, NDEV):
        px = 1 - x if k & 4 else x
        py = 1 - y if k & 2 else y
        pc = 1 - c if k & 1 else c
        out.append(((px, py, pc), 4 * px + 2 * py + pc))
    return out


def _exchange_copies(src, land, send_sems, recv_sems, gather):
    x, y, c = lax.axis_index("x"), lax.axis_index("y"), lax.axis_index("c")
    me = 4 * x + 2 * y + c
    out = []
    for a in range(len(src)):
        for k, (peer, plin) in enumerate(_peers(x, y, c)):
            chunk = src[a] if gather[a] else src[a].at[plin]
            sems = dict(send_sem=send_sems.at[a * (NDEV - 1) + k], recv_sem=recv_sems.at[a * (NDEV - 1) + k],
                        device_id=peer, device_id_type=MESH)
            out.append((pltpu.make_async_remote_copy(src_ref=chunk, dst_ref=land[a].at[me], **sems),
                        pltpu.make_async_remote_copy(src_ref=chunk, dst_ref=land[a].at[plin], **sems)))
    return out


def _exchange(name, srcs, gather):
    n = len(srcs)
    outs = [_sds(((NDEV,) + s.shape) if g else s.shape, s.dtype) for s, g in zip(srcs, gather)]

    def body(*refs):
        src, dst, token = refs[:n], refs[n:2 * n], refs[2 * n]
        send_sems, recv_sems, local_sems = refs[2 * n + 1:]
        me = 4 * lax.axis_index("x") + 2 * lax.axis_index("y") + lax.axis_index("c")
        local = [pltpu.make_async_copy(src[a] if gather[a] else src[a].at[me], dst[a].at[me], local_sems.at[a])
                 for a in range(n)]
        for copy in local:
            copy.start()
        copies = _exchange_copies(src, dst, send_sems, recv_sems, gather)
        for copy, _ in copies:
            copy.start()
        token[...] = jnp.zeros_like(token)
        for copy, landing in copies:
            copy.wait_send()
            landing.wait_recv()
        for copy in local:
            copy.wait()

    nsem = n * (NDEV - 1)
    out = pl.pallas_call(
        body, name=name, out_shape=outs + [_sds((SUBLANES, LANES))], in_specs=[ANY] * n,
        out_specs=[ANY] * n + [pl.BlockSpec(memory_space=pltpu.VMEM)],
        scratch_shapes=[pltpu.SemaphoreType.DMA((nsem,)), pltpu.SemaphoreType.DMA((nsem,)), pltpu.SemaphoreType.DMA((n,))],
    )(*srcs)
    return out[:n], out[n]


HBM = pl.BlockSpec(memory_space=pltpu.HBM)
SEM = pl.BlockSpec(memory_space=pltpu.SEMAPHORE)
EFFECT = pltpu.SideEffectType.DATAFLOW_SIDE_EFFECTING


def _exchange_start_groups(name, groups):
    srcs = [s for g_srcs, _ in groups for s in g_srcs]
    gathers = [g for _, g_gather in groups for g in g_gather]
    lands = [lax.empty(((NDEV,) + s.shape) if g else s.shape, s.dtype) for s, g in zip(srcs, gathers)]
    n, ng = len(srcs), len(groups)

    def body(*refs):
        src, land = refs[:n], refs[n:2 * n]
        sems = refs[2 * n:2 * n + 2 * ng]
        token = refs[-1]
        first = 0
        for g, (g_srcs, g_gather) in enumerate(groups):
            last = first + len(g_srcs)
            for copy, _ in _exchange_copies(src[first:last], land[first:last], sems[2 * g], sems[2 * g + 1], g_gather):
                copy.start()
            first = last
        token[...] = jnp.zeros_like(token)

    hbm = lambda v: pltpu.HBM(v.shape, v.dtype)
    sem_shapes = []
    for g_srcs, _ in groups:
        sem_shapes += [pltpu.SemaphoreType.DMA((len(g_srcs) * (NDEV - 1),))] * 2
    out = pl.pallas_call(
        body, name=name,
        out_shape=(*sem_shapes, *[hbm(v) for v in srcs], *[hbm(v) for v in lands], _sds((SUBLANES, LANES))),
        in_specs=[HBM] * (2 * n),
        out_specs=(*([SEM] * (2 * ng)), *([HBM] * (2 * n)), pl.BlockSpec(memory_space=pltpu.VMEM)),
        input_output_aliases={i: 2 * ng + i for i in range(2 * n)},
        compiler_params=pltpu.CompilerParams(has_side_effects=EFFECT),
    )(*[pltpu.with_memory_space_constraint(v, pltpu.HBM) for v in srcs + lands])
    src_out, land_out = out[2 * ng:2 * ng + n], out[2 * ng + n:2 * ng + 2 * n]
    result, first = [], 0
    for g, (g_srcs, _) in enumerate(groups):
        last = first + len(g_srcs)
        result.append((out[2 * g], out[2 * g + 1], src_out[first:last], land_out[first:last]))
        first = last
    return result, out[-1]


def _exchange_start(name, srcs, gather):
    (group,), token = _exchange_start_groups(name, [(srcs, gather)])
    return (*group, token)


def _exchange_wait(name, send_sems, recv_sems, srcs, lands, gather, after):
    n = len(srcs)

    def body(*refs):
        src, land = refs[:n], refs[n:2 * n]
        send_ref, recv_ref = refs[2 * n], refs[2 * n + 1]
        for copy, landing in _exchange_copies(src, land, send_ref, recv_ref, gather):
            copy.wait_send()
            landing.wait_recv()

    hbm = lambda v: pltpu.HBM(v.shape, v.dtype)
    out = pl.pallas_call(
        body, name=name, out_shape=[hbm(v) for v in list(srcs) + list(lands)],
        in_specs=[HBM] * (2 * n) + [SEM, SEM, ANY], out_specs=[HBM] * (2 * n),
        input_output_aliases={i: i for i in range(2 * n)},
        compiler_params=pltpu.CompilerParams(has_side_effects=EFFECT),
    )(*srcs, *lands, send_sems, recv_sems, after)
    return out[:n], out[n:]


def _with_own(landed, own, me):
    return lax.dynamic_update_slice(landed, own[None], (me,) + (0,) * own.ndim)


def _matmul(name, a, b, mode, mnk, tiles, outs, a_spec=None, b_spec=None, a_fn=None, a_extra=(),
            epi=None, epi_extra=(), out_specs=None, b_slabs=1):
    m_, n_, k_ = mnk
    tm, tn, tk = tiles
    nk = k_ // tk
    if a_spec is None:
        a_spec = (pl.BlockSpec((tk, tm), lambda i, j, k: (k, i)) if mode == "tn"
                  else pl.BlockSpec((tm, tk), lambda i, j, k: (i, k)))
    if b_spec is None:
        b_spec = (pl.BlockSpec((tn, tk), lambda i, j, k: (j, k)) if mode == "nt"
                  else pl.BlockSpec((tk, tn), lambda i, j, k: (k, j)))
    if out_specs is None:
        out_specs = [pl.BlockSpec((tm, tn), lambda i, j, k: (i, j)) for _ in outs]
    na, ne, no = len(a_extra), len(epi_extra), len(outs)

    def body(*refs):
        a_ref, b_ref = refs[0], refs[1]
        ax = refs[2:2 + na]
        ex = refs[2 + na:2 + na + ne]
        o = refs[2 + na + ne:2 + na + ne + no]

        def finish(res):
            res = epi(res, *[r[...] for r in ex]) if epi is not None else (res,)
            for ref, val in zip(o, res):
                ref[...] = val.astype(ref.dtype)

        at = a_ref[...]
        if a_fn is not None:
            at = a_fn(at, *[r[...] for r in ax])
        at = at.astype(BF16)
        if b_slabs == 1:
            part = _dot(at, b_ref[...].astype(BF16), mode)
        else:
            ks = tk // b_slabs
            part = _dot(at[:, 0:ks], b_ref[0].astype(BF16), mode)
            for s in range(1, b_slabs):
                part = part + _dot(at[:, s * ks:(s + 1) * ks], b_ref[s].astype(BF16), mode)
        if nk == 1:
            finish(part)
            return
        acc = refs[-1]
        k = pl.program_id(2)

        @pl.when(k == 0)
        def _():
            acc[...] = part

        @pl.when(k > 0)
        def _():
            acc[...] += part

        @pl.when(k == nk - 1)
        def _():
            finish(acc[...])

    return pl.pallas_call(
        body, name=name, grid=(m_ // tm, n_ // tn, nk),
        in_specs=[a_spec, b_spec] + [s for _, s in a_extra] + [s for _, s in epi_extra],
        out_specs=out_specs, out_shape=[_sds(s, d) for s, d in outs],
        scratch_shapes=[pltpu.VMEM((tm, tn), F32)] if nk > 1 else [],
        compiler_params=_params(("parallel", "parallel", "arbitrary")),
    )(a, b, *[x for x, _ in a_extra], *[x for x, _ in epi_extra])


def _prenorm(name, x, ctx, gain, shsc):
    n_lat = x.shape[0] // ROW_BLOCK
    n_ctx = 0 if ctx is None else ctx.shape[0] // ROW_BLOCK
    d = x.shape[1]

    def norm(src, g_ref, m_ref, o_ref):
        xv = src[...]
        xh = xv * lax.rsqrt(jnp.mean(xv * xv, axis=-1, keepdims=True) + EPS_RMS)
        o_ref[...] = ((xh * g_ref[...]) * (1.0 + m_ref[1:2, :]) + m_ref[0:1, :]).astype(o_ref.dtype)

    def body(*refs):
        if ctx is None:
            x_ref, g_ref, m_ref, o_ref = refs
            norm(x_ref, g_ref, m_ref, o_ref)
        else:
            x_ref, c_ref, g_ref, m_ref, o_ref = refs
            i = pl.program_id(0)

            @pl.when(i < n_lat)
            def _():
                norm(x_ref, g_ref, m_ref, o_ref)

            @pl.when(i >= n_lat)
            def _():
                norm(c_ref, g_ref, m_ref, o_ref)

    in_specs = [pl.BlockSpec((ROW_BLOCK, d), lambda i: (jnp.minimum(i, n_lat - 1), 0))]
    args = [x]
    if ctx is not None:
        in_specs.append(pl.BlockSpec((ROW_BLOCK, d), lambda i: (jnp.maximum(i - n_lat, 0), 0)))
        args.append(ctx)
    in_specs += [pl.BlockSpec((1, d), lambda i: (0, 0)),
                 pl.BlockSpec((None, 2, d), lambda i: (jnp.minimum(i // n_lat, 1), 0, 0))]
    args += [gain, shsc]
    return pl.pallas_call(
        body, name=name, grid=(n_lat + n_ctx,), in_specs=in_specs,
        out_specs=pl.BlockSpec((ROW_BLOCK, d), lambda i: (i, 0)),
        out_shape=_sds(((n_lat + n_ctx) * ROW_BLOCK, d), BF16),
        compiler_params=_params(("parallel",)),
    )(*args)


def _norm_bwd(name, x, d_act, d_act_row0, gain, scale, res=None, aux=None):
    rows, d = x.shape
    nb = rows // ROW_BLOCK
    has_res = res is not None

    def body(*refs):
        if has_res:
            x_ref, da_ref, g_ref, sc_ref, r_ref, aux_ref, dx_ref, sums = refs
        else:
            x_ref, da_ref, g_ref, sc_ref, sums = refs
        i = pl.program_id(0)

        @pl.when(i == 0)
        def _():
            sums[...] = jnp.zeros_like(sums)

        xv, da = x_ref[...], da_ref[...]
        rstd = lax.rsqrt(jnp.mean(xv * xv, axis=-1, keepdims=True) + EPS_RMS)
        xh = xv * rstd
        g = g_ref[...]
        dn = da * (1.0 + sc_ref[...])
        sums[0] += _fold8(da)
        sums[1] += _fold8(da * (xh * g))
        sums[2] += _fold8(dn * xh)
        if has_res:
            dxh = dn * g
            dx = rstd * (dxh - xh * jnp.mean(dxh * xh, axis=-1, keepdims=True))
            rv = r_ref[...]
            dx_ref[...] = rv + dx
            sums[3] += _fold8(rv * aux_ref[...])

    row = lambda i: (i, 0)
    vec = pl.BlockSpec((1, d), lambda i: (0, 0))
    in_specs = [pl.BlockSpec((ROW_BLOCK, d), row), pl.BlockSpec((ROW_BLOCK, d), lambda i: (i + d_act_row0, 0)), vec, vec]
    args = [x, d_act, gain, scale]
    out_shape = [_sds((4, SUBLANES, d))]
    out_specs = [pl.BlockSpec((4, SUBLANES, d), lambda i: (0, 0, 0))]
    if has_res:
        in_specs += [pl.BlockSpec((ROW_BLOCK, d), row), pl.BlockSpec((ROW_BLOCK, d), row)]
        args += [res, aux]
        out_shape = [_sds((rows, d))] + out_shape
        out_specs = [pl.BlockSpec((ROW_BLOCK, d), row)] + out_specs
    return pl.pallas_call(
        body, name=name, grid=(nb,), in_specs=in_specs, out_specs=out_specs, out_shape=out_shape,
        compiler_params=_params(("arbitrary",)),
    )(*args)


def _ada_fwd(cond16, ada_w_loc, ada_b_loc):
    cols = ada_w_loc.shape[1]

    def body(c_ref, w_ref, b_ref, o_ref):
        s = _silu(c_ref[...]).astype(BF16)
        o_ref[...] = _dot(s, w_ref[...].astype(BF16), "nn") + b_ref[...]

    return pl.pallas_call(body, name="ada_fwd", out_shape=_sds((16, cols)), compiler_params=_params())(
        cond16, ada_w_loc, ada_b_loc)


def _ada_bwd(cond16, dmod16, ada_w_loc, c_ctx_row):
    k_, cols = ada_w_loc.shape

    def body(c_ref, dm_ref, w_ref, cc_ref, gw_ref, gc_ref):
        s = _silu(c_ref[...]).astype(BF16)
        dm = dm_ref[...]
        gw_ref[...] = _dot(s, dm.astype(BF16), "tn")
        dmc = jnp.sum(dm[8:16, :], axis=0, keepdims=True)
        dmc8 = jnp.broadcast_to(dmc, (SUBLANES, cols)).astype(BF16)
        ds = _dot(dmc8, w_ref[...].astype(BF16), "nt")
        row = lax.broadcasted_iota(jnp.int32, ds.shape, 0)
        gc_ref[...] = jnp.where(row == 0, ds * _dsilu(cc_ref[...]), 0.0)

    return pl.pallas_call(body, name="ada_bwd", out_shape=[_sds((k_, cols)), _sds((SUBLANES, k_))],
                          compiler_params=_params())(cond16, dmod16, ada_w_loc, c_ctx_row)


def _cmul(a, b):
    return a[0] * b[0] - a[1] * b[1], a[0] * b[1] + a[1] * b[0]


def _disc(lam_re, lam_im, ldt):
    dt = jnp.exp(ldt)
    mag = jnp.exp(lam_re * dt)
    th = lam_im * dt
    a_re, a_im = mag * jnp.cos(th), mag * jnp.sin(th)
    den = lam_re * lam_re + lam_im * lam_im
    n_re = a_re - 1.0
    f_re = (n_re * lam_re + a_im * lam_im) / den
    f_im = (a_im * lam_re - n_re * lam_im) / den
    return dt, mag, th, a_re, a_im, den, n_re, f_re, f_im


def _block_diag_mask(shape):
    row = lax.broadcasted_iota(jnp.int32, shape, 0)
    col = lax.broadcasted_iota(jnp.int32, shape, 1)
    return lax.shift_right_logical(row, 4) == lax.shift_right_logical(col, 6)


TAB_A = 0
TAB_BIG = 1
TAB_SEG = 4
TAB_PW = 5
TAB_ROWS = TAB_PW + STEPS


def _s5_discretise(name, ascending, lam_re, lam_im, ldt, bt_re, bt_im, ct_re, ct_im):
    def write_tables(ref, pw, big, asc, sign):
        row = lax.broadcasted_iota(jnp.int32, (SUBLANES, NSTATE), 0)
        full = lambda v: jnp.broadcast_to(v, (SUBLANES, NSTATE))

        def put(t, p):
            ref[0, t] = full(p[0])
            ref[1, t] = full(sign * p[1])

        put(TAB_A, pw[0])
        for t in range(3):
            put(TAB_BIG + t, big[t])
        seg = [big[0]]
        for _ in range(SEGMENTS - 1):
            seg.append(_cmul(seg[-1], big[0]))
        seg_re = jnp.zeros((SUBLANES, NSTATE), F32)
        seg_im = jnp.zeros((SUBLANES, NSTATE), F32)
        for r in range(SEGMENTS):
            p = seg[r] if asc else seg[SEGMENTS - 1 - r]
            seg_re = jnp.where(row == r, p[0], seg_re)
            seg_im = jnp.where(row == r, sign * p[1], seg_im)
        ref[0, TAB_SEG] = seg_re
        ref[1, TAB_SEG] = seg_im
        for k in range(STEPS):
            put(TAB_PW + k, pw[k])

    def body(lr_ref, li_ref, ldt_ref, br_ref, bi_ref, cr_ref, ci_ref, bb_ref, tab_ref, adj_ref, bm_ref, cm_ref):
        _, _, _, a_re, a_im, _, _, f_re, f_im = _disc(lr_ref[...], li_ref[...], ldt_ref[...])
        bre, bim = br_ref[...], bi_ref[...]
        bb_re = f_re * bre - f_im * bim
        bb_im = f_re * bim + f_im * bre
        bb_ref[0:S5_GROUP, :] = bb_re
        bb_ref[S5_GROUP:2 * S5_GROUP, :] = bb_im
        pw = [(a_re, a_im)]
        for _ in range(STEPS - 1):
            pw.append(_cmul(pw[-1], (a_re, a_im)))
        big = [pw[STEPS - 1]]
        for _ in range(2):
            big.append(_cmul(big[-1], big[-1]))
        write_tables(tab_ref, pw, big, ascending, 1.0)
        write_tables(adj_ref, pw, big, not ascending, -1.0)
        half = NSTATE // S5_BLOCKS
        mask = _block_diag_mask((S5_BLOCK_WIDTH, half))
        tile = lambda v: jnp.broadcast_to(v[None], (S5_BLOCK_WIDTH // S5_GROUP, S5_GROUP, half)).reshape(S5_BLOCK_WIDTH, half)
        for c in range(S5_BLOCKS):
            cols = slice(c * half, (c + 1) * half)
            rows = slice(c * S5_BLOCK_WIDTH, (c + 1) * S5_BLOCK_WIDTH)
            bm_ref[c, :, 0:half] = jnp.where(mask, tile(bb_re[:, cols]), 0.0).astype(BF16)
            bm_ref[c, :, half:2 * half] = jnp.where(mask, tile(bb_im[:, cols]), 0.0).astype(BF16)
            cm_ref[c, :, 0:half] = jnp.where(mask, cr_ref[rows, :], 0.0).astype(BF16)
            cm_ref[c, :, half:2 * half] = jnp.where(mask, -ci_ref[rows, :], 0.0).astype(BF16)

    blocked = _sds((S5_BLOCKS, S5_BLOCK_WIDTH, 2 * NSTATE // S5_BLOCKS), BF16)
    return pl.pallas_call(
        body, name=name,
        out_shape=[_sds((2 * S5_GROUP, NSTATE)), _sds((2, TAB_ROWS, SUBLANES, NSTATE)),
                   _sds((2, TAB_ROWS, SUBLANES, NSTATE)), blocked, blocked],
        compiler_params=_params(),
    )(lam_re, lam_im, ldt, bt_re, bt_im, ct_re, ct_im)


def _s5_discretise_bwd(name, lam_re, lam_im, ldt, bt_re, bt_im, d_abar8, d_bbar):
    def body(lr_ref, li_ref, ldt_ref, br_ref, bi_ref, da_ref, db_ref, dl_ref, dbt_ref):
        lam_re, lam_im = lr_ref[...], li_ref[...]
        dt, _, _, a_re, a_im, den, n_re, f_re, f_im = _disc(lam_re, lam_im, ldt_ref[...])
        bre, bim = br_ref[...], bi_ref[...]
        dbr, dbi = db_ref[0:S5_GROUP, :], db_ref[S5_GROUP:2 * S5_GROUP, :]
        dbt_ref[0:S5_GROUP, :] = f_re * dbr + f_im * dbi
        dbt_ref[S5_GROUP:2 * S5_GROUP, :] = f_re * dbi - f_im * dbr
        df_re = jnp.sum(bre * dbr + bim * dbi, axis=0, keepdims=True)
        df_im = jnp.sum(bre * dbi - bim * dbr, axis=0, keepdims=True)
        da = da_ref[...]
        da_re = jnp.sum(da[:, 0:NSTATE], axis=0, keepdims=True)
        da_im = jnp.sum(da[:, NSTATE:2 * NSTATE], axis=0, keepdims=True)
        da_re = da_re + (df_re * lam_re - df_im * lam_im) / den
        da_im = da_im + (df_re * lam_im + df_im * lam_re) / den
        ff = (f_re * df_re + f_im * df_im) * 2.0 / den
        d_lr = (df_re * n_re + df_im * a_im) / den - ff * lam_re
        d_li = (df_re * a_im - df_im * n_re) / den - ff * lam_im
        d_mag_mag = da_re * a_re + da_im * a_im
        d_th = da_im * a_re - da_re * a_im
        d_lr = d_lr + d_mag_mag * dt
        d_li = d_li + d_th * dt
        d_ldt = (d_mag_mag * lam_re + d_th * lam_im) * dt
        row = lax.broadcasted_iota(jnp.int32, (SUBLANES, NSTATE), 0)
        dl_ref[...] = jnp.where(row == 0, d_lr, jnp.where(row == 1, d_li, jnp.where(row == 2, d_ldt, 0.0)))

    return pl.pallas_call(
        body, name=name, out_shape=[_sds((SUBLANES, NSTATE)), _sds((2 * S5_GROUP, NSTATE))],
        compiler_params=_params(),
    )(lam_re, lam_im, ldt, bt_re, bt_im, d_abar8, d_bbar)


def _segment_permutation(reverse_time):
    rho = jnp.arange(ROW_BLOCK)
    src = STEPS * (rho % SEGMENTS) + rho // SEGMENTS
    if reverse_time:
        src = ROW_BLOCK - 1 - src
    return (src[:, None] == jnp.arange(ROW_BLOCK)[None, :]).astype(BF16)


def _permute_rows(perm_ref, v):
    return _dot(perm_ref[...], v, "nn").astype(BF16)


def _unpermute_rows(perm_t_ref, v):
    hi = v.astype(BF16)
    lo = (v - hi.astype(F32)).astype(BF16)
    return _dot(perm_t_ref[...], hi, "nn") + _dot(perm_t_ref[...], lo, "nn")


def _unrolled_loop(step, init):
    def trip(o, state):
        for u in range(SCAN_UNROLL):
            state = step(o * SCAN_UNROLL + u, state)
        return state

    if SCAN_UNROLL == STEPS:
        return trip(0, init)
    return lax.fori_loop(0, STEPS // SCAN_UNROLL, trip, init)


def _scan_chunk(x_ref, out_ref, tab_ref, carry_re, carry_im, ascending, pair_ref=None, acc_ref=None, lane_chunks=None):
    w = SCAN_LANES
    half = NSTATE // S5_BLOCKS
    row = lax.broadcasted_iota(jnp.int32, (SUBLANES, w), 0)
    last = (SEGMENTS - 1) if ascending else 0

    def from_previous_segment(v, k, fill):
        if ascending:
            return jnp.where(row >= k, pltpu.roll(v, k, 0), fill)
        return jnp.where(row < SEGMENTS - k, pltpu.roll(v, SEGMENTS - k, 0), fill)

    def tile_rows(k):
        return pl.ds(pl.multiple_of((k if ascending else STEPS - 1 - k) * SUBLANES, SUBLANES), SUBLANES)

    for j in (range(NSTATE // w) if lane_chunks is None else lane_chunks):
        n_l = pl.ds(j * w, w)
        lane0 = (j * w // half) * 2 * half + (j * w) % half
        re_l, im_l = pl.ds(lane0, w), pl.ds(lane0 + half, w)
        tab = lambda t, n_l=n_l: (tab_ref[0, t, :, n_l], tab_ref[1, t, :, n_l])
        a_re, a_im = tab(TAB_A)

        def local_step(k, h):
            rs = tile_rows(k)
            h_re = a_re * h[0] - a_im * h[1] + x_ref[rs, re_l]
            h_im = a_re * h[1] + a_im * h[0] + x_ref[rs, im_l]
            out_ref[rs, re_l] = h_re
            out_ref[rs, im_l] = h_im
            return h_re, h_im

        zero = jnp.zeros((SUBLANES, w), F32)
        end_re, end_im = _unrolled_loop(local_step, (zero, zero))
        for t, k in ((TAB_BIG, 1), (TAB_BIG + 1, 2), (TAB_BIG + 2, 4)):
            p_re, p_im = tab(t)
            s_re, s_im = from_previous_segment(end_re, k, 0.0), from_previous_segment(end_im, k, 0.0)
            end_re, end_im = end_re + (p_re * s_re - p_im * s_im), end_im + (p_re * s_im + p_im * s_re)
        c0_re, c0_im = carry_re[:, n_l], carry_im[:, n_l]
        p_re, p_im = tab(TAB_SEG)
        end_re = end_re + (p_re * c0_re - p_im * c0_im)
        end_im = end_im + (p_re * c0_im + p_im * c0_re)
        carry_re[:, n_l] = jnp.broadcast_to(end_re[last:last + 1, :], end_re.shape)
        carry_im[:, n_l] = jnp.broadcast_to(end_im[last:last + 1, :], end_im.shape)
        in_re = from_previous_segment(end_re, 1, c0_re)
        in_im = from_previous_segment(end_im, 1, c0_im)

        def carry_step(k, st):
            rs = tile_rows(k)
            p_re, p_im = tab_ref[0, TAB_PW + k, :, n_l], tab_ref[1, TAB_PW + k, :, n_l]
            o_re = out_ref[rs, re_l] + (p_re * in_re - p_im * in_im)
            o_im = out_ref[rs, im_l] + (p_re * in_im + p_im * in_re)
            out_ref[rs, re_l] = o_re
            out_ref[rs, im_l] = o_im
            if pair_ref is None:
                return st
            s_re, s_im = pair_ref[rs, re_l], pair_ref[rs, im_l]
            return (o_re, o_im, st[2] + (st[0] * s_re + st[1] * s_im), st[3] + (st[1] * s_re - st[0] * s_im))

        if pair_ref is None:
            _unrolled_loop(carry_step, 0)
        else:
            fin = _unrolled_loop(carry_step, (in_re, in_im, zero, zero))
            acc_ref[:, n_l] += fin[2]
            acc_ref[:, pl.ds(NSTATE + j * w, w)] += fin[3]


def _scan_block_index(i, n_lat, ctx_first_then_ascending):
    if ctx_first_then_ascending:
        return jnp.where(i == 0, n_lat, i - 1)
    return jnp.where(i == 0, n_lat, n_lat - i)


def _full_spec(shape):
    return pl.BlockSpec(shape, lambda i: (0,) * len(shape))


_S5_BLOCKED = (S5_BLOCKS, S5_BLOCK_WIDTH, 2 * NSTATE // S5_BLOCKS)
_S5_TABLES = (2, TAB_ROWS, SUBLANES, NSTATE)
_S5_DIAG = (S5_BLOCKS, S5_GROUP, 2 * NSTATE // S5_BLOCKS)


def _s5_scan_fwd(name, ascending, z_all, bmat, cmat, tab, perm, perm_t):
    rows = z_all.shape[0]
    nb = rows // ROW_BLOCK
    n_lat = nb - 1
    bw, sw = S5_BLOCK_WIDTH, 2 * NSTATE // S5_BLOCKS

    def body(u_ref, bm_ref, cm_ref, tab_ref, p_ref, pt_ref, s_ref, y_ref, bu, yp, carry_re, carry_im):
        @pl.when(pl.program_id(0) == 0)
        def _():
            carry_re[...] = jnp.zeros_like(carry_re)
            carry_im[...] = jnp.zeros_like(carry_im)

        up = _permute_rows(p_ref, u_ref[...].astype(BF16))
        for c in range(S5_BLOCKS):
            bu[:, c * sw:(c + 1) * sw] = _dot(up[:, c * bw:(c + 1) * bw], bm_ref[c], "nn")
        _scan_chunk(bu, s_ref, tab_ref, carry_re, carry_im, False)
        for c in range(S5_BLOCKS):
            yp[:, c * bw:(c + 1) * bw] = _dot(s_ref[:, c * sw:(c + 1) * sw].astype(BF16), cm_ref[c], "nt")
        y_ref[...] = _unpermute_rows(pt_ref, yp[...])

    blk = lambda i: (_scan_block_index(i, n_lat, ascending), 0)
    return pl.pallas_call(
        body, name=name, grid=(nb,),
        in_specs=[pl.BlockSpec((ROW_BLOCK, S5_WIDTH), blk), _full_spec(_S5_BLOCKED), _full_spec(_S5_BLOCKED),
                  _full_spec(_S5_TABLES), _full_spec((ROW_BLOCK, ROW_BLOCK)), _full_spec((ROW_BLOCK, ROW_BLOCK))],
        out_specs=[pl.BlockSpec((ROW_BLOCK, 2 * NSTATE), blk), pl.BlockSpec((ROW_BLOCK, S5_WIDTH), blk)],
        out_shape=[_sds((rows, 2 * NSTATE)), _sds((rows, S5_WIDTH))],
        scratch_shapes=[pltpu.VMEM((ROW_BLOCK, 2 * NSTATE), F32), pltpu.VMEM((ROW_BLOCK, S5_WIDTH), F32),
                        pltpu.VMEM((SUBLANES, NSTATE), F32), pltpu.VMEM((SUBLANES, NSTATE), F32)],
        compiler_params=_params(("arbitrary",)),
    )(z_all, bmat, cmat, tab, perm, perm_t)


def _s5_scan_bwd(name, ascending, dy, z_all, states, bmat, cmat, adj, perm, perm_t, du_other=None, d_skip=None):
    rows = states.shape[0]
    nb = rows // ROW_BLOCK
    n_lat = nb - 1
    bw, sw = S5_BLOCK_WIDTH, 2 * NSTATE // S5_BLOCKS
    finish = du_other is not None

    def block_index(i):
        if ascending:
            return jnp.where(i == nb - 1, n_lat, n_lat - 1 - i)
        return jnp.where(i == nb - 1, n_lat, i)

    def body(*refs):
        dy_ref, u_ref, s_ref, bm_ref, cm_ref, adj_ref, p_ref, pt_ref = refs[:8]
        extra = refs[8:10] if finish else ()
        du_ref, db_ref, dc_ref, da_ref, g, dup, db_acc, dc_acc, carry_re, carry_im = refs[8 + len(extra):]
        i = pl.program_id(0)

        @pl.when(i == 0)
        def _():
            carry_re[...] = jnp.zeros_like(carry_re)
            carry_im[...] = jnp.zeros_like(carry_im)
            da_ref[...] = jnp.zeros_like(da_ref)
            db_acc[...] = jnp.zeros_like(db_acc)
            dc_acc[...] = jnp.zeros_like(dc_acc)

        has_dy = (i < nb - 1).astype(F32)
        dyp = _permute_rows(p_ref, (dy_ref[...] * has_dy).astype(BF16))
        up = _permute_rows(p_ref, u_ref[...].astype(BF16))
        for c in range(S5_BLOCKS):
            g[:, c * sw:(c + 1) * sw] = _dot(dyp[:, c * bw:(c + 1) * bw], cm_ref[c], "nn")
            dc_acc[c] += _dot(dyp[:, c * bw:(c + 1) * bw], s_ref[:, c * sw:(c + 1) * sw].astype(BF16), "tn")
            _scan_chunk(g, g, adj_ref, carry_re, carry_im, True, pair_ref=s_ref, acc_ref=da_ref, lane_chunks=[c])
            gc = g[:, c * sw:(c + 1) * sw].astype(BF16)
            dup[:, c * bw:(c + 1) * bw] = _dot(gc, bm_ref[c], "nt")
            db_acc[c] += _dot(up[:, c * bw:(c + 1) * bw], gc, "tn")
        du = _unpermute_rows(pt_ref, dup[...])
        if finish:
            du = du + extra[0][...] + (dy_ref[...] * has_dy) * extra[1][...]
        du_ref[...] = du.astype(du_ref.dtype)

        @pl.when(i == nb - 1)
        def _():
            mask = _block_diag_mask((bw, sw // 2))
            for acc, out in ((db_acc, db_ref), (dc_acc, dc_ref)):
                for c in range(S5_BLOCKS):
                    for part in range(2):
                        cols = slice(part * (sw // 2), (part + 1) * (sw // 2))
                        kept = jnp.where(mask, acc[c, :, cols], 0.0)
                        out[c, :, cols] = kept.reshape(bw // S5_GROUP, S5_GROUP, sw // 2).sum(axis=0)

    blk = lambda i: (block_index(i), 0)
    in_specs = [pl.BlockSpec((ROW_BLOCK, S5_WIDTH), lambda i: (jnp.minimum(block_index(i), n_lat - 1), 0)),
                pl.BlockSpec((ROW_BLOCK, S5_WIDTH), blk), pl.BlockSpec((ROW_BLOCK, 2 * NSTATE), blk),
                _full_spec(_S5_BLOCKED), _full_spec(_S5_BLOCKED), _full_spec(_S5_TABLES),
                _full_spec((ROW_BLOCK, ROW_BLOCK)), _full_spec((ROW_BLOCK, ROW_BLOCK))]
    args = [dy, z_all, states, bmat, cmat, adj, perm, perm_t]
    if finish:
        in_specs += [pl.BlockSpec((ROW_BLOCK, S5_WIDTH), blk), _full_spec((1, S5_WIDTH))]
        args += [du_other, d_skip]
    return pl.pallas_call(
        body, name=name, grid=(nb,), in_specs=in_specs,
        out_specs=[pl.BlockSpec((ROW_BLOCK, S5_WIDTH), blk), _full_spec(_S5_DIAG), _full_spec(_S5_DIAG),
                   _full_spec((SUBLANES, 2 * NSTATE))],
        out_shape=[_sds((rows, S5_WIDTH), BF16 if finish else F32), _sds(_S5_DIAG), _sds(_S5_DIAG),
                   _sds((SUBLANES, 2 * NSTATE))],
        scratch_shapes=[pltpu.VMEM((ROW_BLOCK, 2 * NSTATE), F32), pltpu.VMEM((ROW_BLOCK, S5_WIDTH), F32),
                        pltpu.VMEM(_S5_BLOCKED, F32), pltpu.VMEM(_S5_BLOCKED, F32),
                        pltpu.VMEM((SUBLANES, NSTATE), F32), pltpu.VMEM((SUBLANES, NSTATE), F32)],
        compiler_params=_params(("arbitrary",)),
    )(*args)


def _latent_row_tile(n_rows):
    return 512 if n_rows % 512 == 0 else ROW_BLOCK


def _glu_fwd(z_all, y0, y1, d_skip, w_glu, n_rows):
    def body(u_ref, y0_ref, y1_ref, d_ref, w_ref, o_ref):
        y = d_ref[...] * u_ref[...] + y0_ref[...] + y1_ref[...]
        g = _gelu(y)
        t = _dot(g.astype(BF16), w_ref[...], "nn")
        o_ref[...] = (g * _sigmoid(t)).astype(o_ref.dtype)

    rows = _latent_row_tile(n_rows)
    row = pl.BlockSpec((rows, S5_WIDTH), lambda i: (i, 0))
    return pl.pallas_call(
        body, name="glu_fwd", grid=(n_rows // rows,),
        in_specs=[row, row, row, pl.BlockSpec((1, S5_WIDTH), lambda i: (0, 0)),
                  pl.BlockSpec((S5_WIDTH, S5_WIDTH), lambda i: (0, 0))],
        out_specs=row, out_shape=_sds((n_rows, S5_WIDTH + CONV_WIDTH), BF16), compiler_params=_params(("parallel",)),
    )(z_all, y0, y1, d_skip, w_glu)


def _glu_bwd(d_ycat, z_all, y0, y1, d_skip, w_glu, n_rows):
    def body(do_ref, u_ref, y0_ref, y1_ref, d_ref, w_ref, dy_ref, dw_ref, dd_ref):
        @pl.when(pl.program_id(0) == 0)
        def _():
            dw_ref[...] = jnp.zeros_like(dw_ref)
            dd_ref[...] = jnp.zeros_like(dd_ref)

        u = u_ref[...]
        y = d_ref[...] * u + y0_ref[...] + y1_ref[...]
        g = _gelu(y)
        gb = g.astype(BF16)
        w = w_ref[...]
        sg = _sigmoid(_dot(gb, w, "nn"))
        do = do_ref[...]
        dt = do * g * sg * (1.0 - sg)
        dtb = dt.astype(BF16)
        dg = do * sg + _dot(dtb, w, "nt")
        dy = dg * _dgelu(y)
        dy_ref[...] = dy
        dw_ref[...] += _dot(gb, dtb, "tn")
        dd_ref[...] += _fold8(dy * u)

    rows = _latent_row_tile(n_rows)
    row = pl.BlockSpec((rows, S5_WIDTH), lambda i: (i, 0))
    sq = pl.BlockSpec((S5_WIDTH, S5_WIDTH), lambda i: (0, 0))
    return pl.pallas_call(
        body, name="glu_bwd", grid=(n_rows // rows,),
        in_specs=[row, row, row, row, pl.BlockSpec((1, S5_WIDTH), lambda i: (0, 0)), sq],
        out_specs=[row, sq, pl.BlockSpec((SUBLANES, S5_WIDTH), lambda i: (0, 0))],
        out_shape=[_sds((n_rows, S5_WIDTH)), _sds((S5_WIDTH, S5_WIDTH)), _sds((SUBLANES, S5_WIDTH))],
        compiler_params=_params(("arbitrary",)),
    )(d_ycat, z_all, y0, y1, d_skip, w_glu)


CONV_HALF = CONV_K // 2


def _conv_block(n_rows):
    blk = min(1024, n_rows)
    assert blk >= CONV_HALF * GRID_W and n_rows % blk == 0
    return blk


def _conv_gate(z_all, n_rows):
    blk = _conv_block(n_rows)
    nb = n_rows // blk

    def body(v_ref, g_ref, o_ref):
        i = pl.program_id(0)
        inside = jnp.logical_and(i >= 1, i <= nb)

        @pl.when(inside)
        def _():
            o_ref[...] = v_ref[...] * _sigmoid(g_ref[...])

        @pl.when(jnp.logical_not(inside))
        def _():
            o_ref[...] = jnp.zeros_like(o_ref)

    src = lambda col: pl.BlockSpec((blk, CONV_WIDTH), lambda i: (jnp.clip(i - 1, 0, nb - 1), col))
    return pl.pallas_call(
        body, name="conv_gate", grid=(nb + 2,), in_specs=[src(1), src(2)],
        out_specs=pl.BlockSpec((blk, CONV_WIDTH), lambda i: (i, 0)),
        out_shape=_sds(((nb + 2) * blk, CONV_WIDTH)), compiler_params=_params(("parallel",)),
    )(z_all, z_all)


def _stream_padded(pad_ref, buf, sems, blk, n_blocks):
    i = pl.program_id(0)

    def copy(b):
        rows = pl.ds(pl.multiple_of(b * blk, blk), blk)
        return pltpu.make_async_copy(pad_ref.at[rows, :], buf.at[rows, :], sems.at[b])

    @pl.when(i == 0)
    def _():
        for b in range(n_blocks):
            copy(b).start()
        copy(0).wait()
        copy(1).wait()

    copy(i + 2).wait()
    return pl.multiple_of(i * blk, blk)


def _conv_fwd(hh_pad, w, b, ln_g, ln_b, ycat, n_rows):
    blk = _conv_block(n_rows)
    nblk = n_rows // blk + 2

    def body(hh_ref, w_ref, b_ref, g_ref, lb_ref, ycat_ref, hc_ref, y_ref, win, sems):
        base = _stream_padded(hh_ref, win, sems, blk, nblk)

        def tile(t, _):
            r0 = pl.multiple_of(t * CONV_ROWS, CONV_ROWS)
            acc = jnp.zeros((CONV_ROWS, CONV_WIDTH), F32)
            for k in range(CONV_K):
                acc = acc + w_ref[k:k + 1, :] * win[pl.ds(base + r0 + blk + (k - CONV_HALF) * GRID_W, CONV_ROWS), :]
            hc = acc + b_ref[...]
            hc_ref[pl.ds(r0, CONV_ROWS), :] = hc
            mu = jnp.mean(hc, axis=-1, keepdims=True)
            xc = hc - mu
            ln = xc * lax.rsqrt(jnp.mean(xc * xc, axis=-1, keepdims=True) + EPS_LN) * g_ref[...] + lb_ref[...]
            y_ref[pl.ds(r0, CONV_ROWS), :] = _silu(ln).astype(y_ref.dtype)
            return 0

        lax.fori_loop(0, blk // CONV_ROWS, tile, 0)

    vec = pl.BlockSpec((1, CONV_WIDTH), lambda i: (0, 0))
    row = pl.BlockSpec((blk, CONV_WIDTH), lambda i: (i, 0))
    return pl.pallas_call(
        body, name="conv_fwd", grid=(n_rows // blk,),
        in_specs=[ANY, pl.BlockSpec((CONV_K, CONV_WIDTH), lambda i: (0, 0)), vec, vec, vec, ANY],
        out_specs=[row, pl.BlockSpec((blk, CONV_WIDTH), lambda i: (i, 1))],
        out_shape=[_sds((n_rows, CONV_WIDTH)), _sds(ycat.shape, ycat.dtype)], input_output_aliases={5: 1},
        scratch_shapes=[pltpu.VMEM((nblk * blk, CONV_WIDTH), F32), pltpu.SemaphoreType.DMA((nblk,))],
        compiler_params=_params(("arbitrary",)),
    )(hh_pad, w, b, ln_g, ln_b, ycat)


def _conv_bwd_norm(d_ycat, hc, ln_g, ln_b, n_rows):
    blk = _conv_block(n_rows)
    nb = n_rows // blk

    def body(dy_ref, hc_ref, g_ref, lb_ref, o_ref, sums):
        i = pl.program_id(0)

        @pl.when(i == 0)
        def _():
            sums[...] = jnp.zeros_like(sums)

        inside = jnp.logical_and(i >= 1, i <= nb)

        @pl.when(inside)
        def _():
            hcv = hc_ref[...]
            mu = jnp.mean(hcv, axis=-1, keepdims=True)
            xc = hcv - mu
            rstd = lax.rsqrt(jnp.mean(xc * xc, axis=-1, keepdims=True) + EPS_LN)
            xh = xc * rstd
            g = g_ref[...]
            dln = dy_ref[...] * _dsilu(xh * g + lb_ref[...])
            dxh = dln * g
            dhc = rstd * (dxh - jnp.mean(dxh, axis=-1, keepdims=True) - xh * jnp.mean(dxh * xh, axis=-1, keepdims=True))
            o_ref[...] = dhc
            sums[0] += _fold8(dhc)
            sums[1] += _fold8(dln * xh)
            sums[2] += _fold8(dln)

        @pl.when(jnp.logical_not(inside))
        def _():
            o_ref[...] = jnp.zeros_like(o_ref)

    vec = pl.BlockSpec((1, CONV_WIDTH), lambda i: (0, 0))
    return pl.pallas_call(
        body, name="conv_bwd_norm", grid=(nb + 2,),
        in_specs=[pl.BlockSpec((blk, CONV_WIDTH), lambda i: (jnp.clip(i - 1, 0, nb - 1), 1)),
                  pl.BlockSpec((blk, CONV_WIDTH), lambda i: (jnp.clip(i - 1, 0, nb - 1), 0)), vec, vec],
        out_specs=[pl.BlockSpec((blk, CONV_WIDTH), lambda i: (i, 0)),
                   pl.BlockSpec((3, SUBLANES, CONV_WIDTH), lambda i: (0, 0, 0))],
        out_shape=[_sds(((nb + 2) * blk, CONV_WIDTH)), _sds((3, SUBLANES, CONV_WIDTH))],
        compiler_params=_params(("arbitrary",)),
    )(d_ycat, hc, ln_g, ln_b)


def _conv_bwd_taps(dhc_pad, hh_pad, z_all, w, n_rows):
    blk = _conv_block(n_rows)
    nblk = n_rows // blk + 2

    def body(dhc_ref, hh_ref, v_ref, g_ref, w_ref, dv_ref, dg_ref, dw_ref, dwin, hwin, dsems, hsems):
        @pl.when(pl.program_id(0) == 0)
        def _():
            dw_ref[...] = jnp.zeros_like(dw_ref)

        base = _stream_padded(dhc_ref, dwin, dsems, blk, nblk)
        _stream_padded(hh_ref, hwin, hsems, blk, nblk)

        def tile(t, _):
            r0 = pl.multiple_of(t * CONV_BWD_ROWS, CONV_BWD_ROWS) + base
            dh = dwin[pl.ds(r0 + blk, CONV_BWD_ROWS), :]
            acc = jnp.zeros((CONV_BWD_ROWS, CONV_WIDTH), F32)
            for k in range(CONV_K):
                off = (k - CONV_HALF) * GRID_W
                acc = acc + w_ref[k:k + 1, :] * dwin[pl.ds(r0 + blk - off, CONV_BWD_ROWS), :]
                dw_ref[k] += _fold8(dh * hwin[pl.ds(r0 + blk + off, CONV_BWD_ROWS), :])
            rs = pl.ds(pl.multiple_of(t * CONV_BWD_ROWS, CONV_BWD_ROWS), CONV_BWD_ROWS)
            sg = _sigmoid(g_ref[rs, :])
            vv = v_ref[rs, :]
            dv_ref[rs, :] = (acc * sg).astype(dv_ref.dtype)
            dg_ref[rs, :] = (acc * vv * sg * (1.0 - sg)).astype(dg_ref.dtype)
            return 0

        lax.fori_loop(0, blk // CONV_BWD_ROWS, tile, 0)

    row = pl.BlockSpec((blk, CONV_WIDTH), lambda i: (i, 0))
    return pl.pallas_call(
        body, name="conv_bwd_taps", grid=(n_rows // blk,),
        in_specs=[ANY, ANY,
            pl.BlockSpec((blk, CONV_WIDTH), lambda i: (i, 1)), pl.BlockSpec((blk, CONV_WIDTH), lambda i: (i, 2)),
            pl.BlockSpec((CONV_K, CONV_WIDTH), lambda i: (0, 0))],
        out_specs=[row, row, pl.BlockSpec((CONV_K, SUBLANES, CONV_WIDTH), lambda i: (0, 0, 0))],
        out_shape=[_sds((n_rows, CONV_WIDTH), BF16), _sds((n_rows, CONV_WIDTH), BF16),
                   _sds((CONV_K, SUBLANES, CONV_WIDTH))],
        scratch_shapes=[pltpu.VMEM((nblk * blk, CONV_WIDTH), F32), pltpu.VMEM((nblk * blk, CONV_WIDTH), F32),
                        pltpu.SemaphoreType.DMA((nblk,)), pltpu.SemaphoreType.DMA((nblk,))],
        compiler_params=_params(("arbitrary",)),
    )(dhc_pad, hh_pad, z_all, z_all, w)


def _sum_parts(parts):
    _, r, c = parts.shape

    def body(p_ref, o_ref):
        acc = p_ref[0]
        for q in range(1, NDEV):
            acc = acc + p_ref[q]
        o_ref[...] = acc

    return pl.pallas_call(body, name="sum_parts", out_shape=_sds((r, c)), compiler_params=_params())(parts)


def _row_tile(r, c):
    best = r
    for t in (1024, 512, 256, 128, 64, 32, 16, 8):
        if r % t == 0 and t * c <= 128 * 1024:
            return t
    return best


def _adamw(name, w, gparts, m, v):
    r, c = w.shape
    np_ = gparts.shape[0]
    tr = _row_tile(r, c)

    def body(w_ref, g_ref, m_ref, v_ref, go_ref, d_ref, mo_ref, vo_ref):
        g = g_ref[0].astype(F32)
        for q in range(1, np_):
            g = g + g_ref[q].astype(F32)
        m2 = ADAM_B1 * m_ref[...] + (1.0 - ADAM_B1) * g
        v2 = ADAM_B2 * v_ref[...] + (1.0 - ADAM_B2) * jnp.square(g)
        m_hat = m2 / (1.0 - ADAM_B1 ** ADAM_STEP)
        v_hat = v2 / (1.0 - ADAM_B2 ** ADAM_STEP)
        go_ref[...] = g
        d_ref[...] = -ADAM_LR * (m_hat / (jnp.sqrt(v_hat) + ADAM_EPS) + ADAM_WD * w_ref[...])
        mo_ref[...] = m2
        vo_ref[...] = v2

    row = pl.BlockSpec((tr, c), lambda i: (i, 0))
    return pl.pallas_call(
        body, name=name, grid=(r // tr,),
        in_specs=[row, pl.BlockSpec((np_, tr, c), lambda i: (0, i, 0)), row, row],
        out_specs=[row] * 4, out_shape=[_sds((r, c))] * 4, compiler_params=_params(("parallel",)),
    )(w, gparts, m, v)


def _adamw_native(name, w, g, m, v):
    def body(w_ref, g_ref, m_ref, v_ref, d_ref, mo_ref, vo_ref):
        gv = g_ref[...]
        m2 = ADAM_B1 * m_ref[...] + (1.0 - ADAM_B1) * gv
        v2 = ADAM_B2 * v_ref[...] + (1.0 - ADAM_B2) * jnp.square(gv)
        m_hat = m2 / (1.0 - ADAM_B1 ** ADAM_STEP)
        v_hat = v2 / (1.0 - ADAM_B2 ** ADAM_STEP)
        d_ref[...] = -ADAM_LR * (m_hat / (jnp.sqrt(v_hat) + ADAM_EPS) + ADAM_WD * w_ref[...])
        mo_ref[...] = m2
        vo_ref[...] = v2

    return pl.pallas_call(body, name=name, out_shape=[_sds(w.shape)] * 3, compiler_params=_params())(w, g, m, v)


SMALL = ["c_ctx", "ada_b", "norm1_g", "s5_lam_re", "s5_lam_im", "s5_log_dt", "s5_d", "conv_b", "conv_ln_g", "conv_ln_b",
         "norm2_g", "final_g"]
SMALL_PACKED_ROWS = 24


def _pack_rows(parts, rows):
    flat = jnp.concatenate([p.reshape(-1).astype(F32) for p in parts])
    return jnp.pad(flat, (0, rows * D_MODEL - flat.shape[0])).reshape(rows, D_MODEL)


def _unpack_rows(packed, shapes):
    flat = packed.reshape(-1)
    out, off = [], 0
    for shape in shapes:
        size = 1
        for s in shape:
            size *= s
        out.append(flat[off:off + size].reshape(shape))
        off += size
    return out


def kernel(x, c, ctx, c_ctx, ada_w, ada_b, norm1_g, w_in, s5_lam_re, s5_lam_im, s5_log_dt, s5_b_re, s5_b_im, s5_c_re, s5_c_im, s5_d, s5_w_glu, conv_w, conv_b, conv_ln_g, conv_ln_b, w_out, norm2_g, mlp_w1, mlp_w2, final_g, loss_target, m_c_ctx, m_ada_w, m_ada_b, m_norm1_g, m_w_in, m_s5_lam_re, m_s5_lam_im, m_s5_log_dt, m_s5_b_re, m_s5_b_im, m_s5_c_re, m_s5_c_im, m_s5_d, m_s5_w_glu, m_conv_w, m_conv_b, m_conv_ln_g, m_conv_ln_b, m_w_out, m_norm2_g, m_mlp_w1, m_mlp_w2, m_final_g, v_c_ctx, v_ada_w, v_ada_b, v_norm1_g, v_w_in, v_s5_lam_re, v_s5_lam_im, v_s5_log_dt, v_s5_b_re, v_s5_b_im, v_s5_c_re, v_s5_c_im, v_s5_d, v_s5_w_glu, v_conv_w, v_conv_b, v_conv_ln_g, v_conv_ln_b, v_w_out, v_norm2_g, v_mlp_w1, v_mlp_w2, v_final_g):
    weights = dict(c_ctx=c_ctx, ada_w=ada_w, ada_b=ada_b, norm1_g=norm1_g, w_in=w_in, s5_lam_re=s5_lam_re, s5_lam_im=s5_lam_im, s5_log_dt=s5_log_dt, s5_b_re=s5_b_re, s5_b_im=s5_b_im, s5_c_re=s5_c_re, s5_c_im=s5_c_im, s5_d=s5_d, s5_w_glu=s5_w_glu, conv_w=conv_w, conv_b=conv_b, conv_ln_g=conv_ln_g, conv_ln_b=conv_ln_b, w_out=w_out, norm2_g=norm2_g, mlp_w1=mlp_w1, mlp_w2=mlp_w2, final_g=final_g)
    mom1 = dict(c_ctx=m_c_ctx, ada_w=m_ada_w, ada_b=m_ada_b, norm1_g=m_norm1_g, w_in=m_w_in, s5_lam_re=m_s5_lam_re, s5_lam_im=m_s5_lam_im, s5_log_dt=m_s5_log_dt, s5_b_re=m_s5_b_re, s5_b_im=m_s5_b_im, s5_c_re=m_s5_c_re, s5_c_im=m_s5_c_im, s5_d=m_s5_d, s5_w_glu=m_s5_w_glu, conv_w=m_conv_w, conv_b=m_conv_b, conv_ln_g=m_conv_ln_g, conv_ln_b=m_conv_ln_b, w_out=m_w_out, norm2_g=m_norm2_g, mlp_w1=m_mlp_w1, mlp_w2=m_mlp_w2, final_g=m_final_g)
    mom2 = dict(c_ctx=v_c_ctx, ada_w=v_ada_w, ada_b=v_ada_b, norm1_g=v_norm1_g, w_in=v_w_in, s5_lam_re=v_s5_lam_re, s5_lam_im=v_s5_lam_im, s5_log_dt=v_s5_log_dt, s5_b_re=v_s5_b_re, s5_b_im=v_s5_b_im, s5_c_re=v_s5_c_re, s5_c_im=v_s5_c_im, s5_d=v_s5_d, s5_w_glu=v_s5_w_glu, conv_w=v_conv_w, conv_b=v_conv_b, conv_ln_g=v_conv_ln_g, conv_ln_b=v_conv_ln_b, w_out=v_w_out, norm2_g=v_norm2_g, mlp_w1=v_mlp_w1, mlp_w2=v_mlp_w2, final_g=v_final_g)
    order = list(weights)

    me = 4 * lax.axis_index("x") + 2 * lax.axis_index("y") + lax.axis_index("c")
    xs, cs, tgt = x[0], ctx[0], loss_target[0]
    n_lat_rows, n_ctx_rows = xs.shape[0], cs.shape[0]
    n_rows = n_lat_rows + n_ctx_rows
    n_lat = n_lat_rows // ROW_BLOCK
    ada_cols = ada_w.shape[2]

    (c_all,), _ = _exchange("gather_c", [c], [True])
    c_all = c_all.reshape(NDEV, D_MODEL)

    cond_fwd = jnp.concatenate([c_all, c_ctx[None], jnp.zeros((7, D_MODEL), F32)])
    ada_b_loc = lax.dynamic_slice(ada_b, (0, me * ada_cols), (1, ada_cols))
    (mod_g,), mod_token = _exchange("gather_mod", [_ada_fwd(cond_fwd, ada_w[0], ada_b_loc)], [True])
    weight_groups, weights_token = _exchange_start_groups("gather_weights_start", [
        ([w_in[0].astype(BF16)], [True]),
        ([s5_w_glu[0].astype(BF16), conv_w[0] + mod_token[0:1, 0:1], w_out[0].astype(BF16)], [True] * 3),
        ([mlp_w1[0].astype(BF16), mlp_w2[0].astype(BF16)], [True] * 2)])
    (wi_send, wi_recv, wi_src, wi_land), (mixer_send, mixer_recv, mixer_src, mixer_land), \
        (mlpw_send, mlpw_recv, mlpw_src, mlpw_land) = weight_groups
    mod_rows = jnp.transpose(mod_g, (1, 0, 2)).reshape(16, 6 * D_MODEL) + weights_token[0:1, 0:1]
    mod = lax.dynamic_slice(mod_rows, (me, 0), (1, 6 * D_MODEL)).reshape(6, D_MODEL)
    modc = mod_rows[8, :2 * D_MODEL].reshape(2, D_MODEL)
    sh1, sc1, g1, sh2, sc2, g2 = [mod[i:i + 1] for i in range(6)]

    lam_re, lam_im = s5_lam_re[0].reshape(2, 1, NSTATE), s5_lam_im[0].reshape(2, 1, NSTATE)
    ldt = jnp.repeat(s5_log_dt[0], S5_STATE, axis=-1).reshape(2, 1, NSTATE)
    bt_re = jnp.transpose(s5_b_re[0], (0, 3, 1, 2)).reshape(2, S5_GROUP, NSTATE)
    bt_im = jnp.transpose(s5_b_im[0], (0, 3, 1, 2)).reshape(2, S5_GROUP, NSTATE)
    groups_per_block = S5_GROUPS // S5_BLOCKS
    ct_re = jnp.tile(s5_c_re[0].reshape(2, S5_WIDTH, S5_STATE), (1, 1, groups_per_block))
    ct_im = jnp.tile(s5_c_im[0].reshape(2, S5_WIDTH, S5_STATE), (1, 1, groups_per_block))
    d_skip = s5_d[0].reshape(1, S5_WIDTH)
    perms = [_segment_permutation(reverse_time=(d == 0)) for d in range(2)]
    perms_t = [p.T for p in perms]
    disc = [_s5_discretise(f"s5_disc{d}", False, lam_re[d], lam_im[d], ldt[d], bt_re[d], bt_im[d], ct_re[d], ct_im[d])
            for d in range(2)]

    a_all = _prenorm("prenorm1", xs, cs, norm1_g, jnp.stack([mod[0:2], modc]))
    before_w_in = a_all[0:SUBLANES, 0:LANES].astype(F32) + disc[0][0][0:SUBLANES, 0:LANES] + disc[1][0][0:SUBLANES, 0:LANES]
    wi_own, wi_landed = _exchange_wait("gather_w_in_wait", wi_send, wi_recv, wi_src, wi_land, [True], before_w_in)
    w_in_full = jnp.transpose(_with_own(wi_landed[0], wi_own[0], me), (1, 0, 2)).reshape(D_MODEL, IN_COLS)
    tm_all = 1088 if n_rows % 1088 == 0 else ROW_BLOCK
    (z_all,) = _matmul("in_proj", a_all, w_in_full, "nn", (n_rows, IN_COLS, D_MODEL), (tm_all, IN_COLS, D_MODEL),
                       [((n_rows, IN_COLS), F32)])

    states, y_dir = [], []
    for d in range(2):
        _, tab, _, bmat, cmat = disc[d]
        s, yd = _s5_scan_fwd(f"s5_scan_fwd{d}", d == 0, z_all, bmat, cmat, tab, perms[d], perms_t[d])
        states.append(s)
        y_dir.append(yd)
    mixer_own, mixer_landed = _exchange_wait("gather_mixer_wait", mixer_send, mixer_recv, mixer_src, mixer_land,
                                             [True] * 3, y_dir[1])
    glu_g, conv_w_g, w_out_g = [_with_own(l, o, me) for l, o in zip(mixer_landed, mixer_own)]
    glu_full = glu_g.reshape(S5_WIDTH, S5_WIDTH)
    conv_w_full = jnp.transpose(conv_w_g, (1, 0, 2)).reshape(CONV_K, CONV_WIDTH)
    w_out_full = w_out_g.reshape(D_MODEL, D_MODEL)
    ycat = _glu_fwd(z_all, y_dir[0], y_dir[1], d_skip, glu_full, n_lat_rows)

    hh_pad = _conv_gate(z_all, n_lat_rows)
    hc, ycat = _conv_fwd(hh_pad, conv_w_full, conv_b, conv_ln_g, conv_ln_b, ycat, n_lat_rows)

    tm = min(1024, n_lat_rows)
    tm_e = min(512, n_lat_rows)
    w1_cols = D_FF // NDEV
    row_vec = lambda tn: pl.BlockSpec((1, tn), lambda i, j, k: (0, j))
    out_tile = lambda t_m, t_n: pl.BlockSpec((t_m, t_n), lambda i, j, k: (i, j))
    full_rows = ((n_lat_rows, D_MODEL), F32)
    sums = ((n_lat_rows // tm_e, SUBLANES, D_MODEL), F32)
    sums_spec = pl.BlockSpec((None, SUBLANES, D_MODEL), lambda i, j, k: (i, 0, 0))
    vec = lambda v: (v, row_vec(D_MODEL))
    transposed_tile = lambda t_m, t_n: pl.BlockSpec((t_n, t_m), lambda i, j, k: (j, i))
    mix, h1, a2, a2_t = _matmul(
        "out_proj", ycat, w_out_full, "nn", (n_lat_rows, D_MODEL, D_MODEL), (tm_e, D_MODEL, D_MODEL),
        [full_rows, full_rows, ((n_lat_rows, D_MODEL), BF16), ((D_MODEL, n_lat_rows), BF16)],
        epi=_epi_residual_prenorm,
        epi_extra=[(xs, out_tile(tm_e, D_MODEL)), vec(g1), vec(norm2_g), vec(sc2), vec(sh2)],
        out_specs=[out_tile(tm_e, D_MODEL)] * 3 + [transposed_tile(tm_e, D_MODEL)])
    mlpw_own, mlpw_landed = _exchange_wait("gather_mlp_wait", mlpw_send, mlpw_recv, mlpw_src, mlpw_land, [True] * 2, a2)
    w1_g, w2_g = [_with_own(l, o, me) for l, o in zip(mlpw_landed, mlpw_own)]
    w2_full = w2_g.reshape(D_FF, D_MODEL)
    tm_up = min(2048, n_lat_rows)
    f, f_t = _matmul("mlp_up", a2, w1_g, "nn", (n_lat_rows, D_FF, D_MODEL), (tm_up, w1_cols, D_MODEL),
                     [((n_lat_rows, D_FF), BF16), ((D_FF, n_lat_rows), BF16)], epi=lambda acc: (acc, acc.T),
                     b_spec=pl.BlockSpec((None, D_MODEL, w1_cols), lambda i, j, k: (j, 0, 0)),
                     out_specs=[out_tile(tm_up, w1_cols), transposed_tile(tm_up, w1_cols)])
    sq_relu = lambda t: jnp.square(jnp.maximum(t, 0.0))
    mlp_out, d_h2, dm2, err_sums, d_final_g8 = _matmul(
        "mlp_down", f, w2_full, "nn", (n_lat_rows, D_MODEL, D_FF), (tm_e, D_MODEL, 2048),
        [full_rows, full_rows, ((n_lat_rows, D_MODEL), BF16), sums, sums], a_fn=sq_relu, epi=_epi_residual_loss,
        epi_extra=[(h1, out_tile(tm_e, D_MODEL)), vec(g2), (tgt, out_tile(tm_e, D_MODEL)), vec(final_g[None])],
        out_specs=[out_tile(tm_e, D_MODEL)] * 3 + [sums_spec] * 2)

    (d_f,) = _matmul("mlp_down_dx", dm2, w2_full, "nt", (n_lat_rows, D_FF, D_MODEL), (tm_up, 512, D_MODEL),
                     [((n_lat_rows, D_FF), BF16)],
                     epi=lambda acc, ft: (acc * 2.0 * jnp.maximum(ft.astype(F32), 0.0),),
                     epi_extra=[(f, out_tile(tm_up, 512))])
    tk_dw = min(2048, n_lat_rows)
    (g_w2,) = _matmul("mlp_down_dw", f_t, dm2, "nn", (D_FF, D_MODEL, n_lat_rows), (1024, D_MODEL, tk_dw),
                      [((D_FF, D_MODEL), F32)], a_fn=sq_relu)
    (g_w1,) = _matmul("mlp_up_dw", a2_t, d_f, "nn", (D_MODEL, D_FF, n_lat_rows), (D_MODEL, w1_cols, n_lat_rows),
                      [((NDEV, D_MODEL, w1_cols), F32)],
                      out_specs=[pl.BlockSpec((None, D_MODEL, w1_cols), lambda i, j, k: (j, 0, 0))])
    mlp_send, mlp_recv, mlp_src, mlp_land, mlp_token = _exchange_start(
        "scatter_mlp_start", [g_w1, g_w2.reshape(NDEV, D_FF // NDEV, D_MODEL)], [False] * 2)
    d_h1, dm1, *sums2 = _matmul(
        "mlp_up_dx", d_f, w1_g, "nt", (n_lat_rows, D_MODEL, D_FF), (tm_e, D_MODEL, 4 * w1_cols),
        [full_rows, ((n_lat_rows, D_MODEL), BF16)] + [sums] * 4, epi=_epi_norm_bwd,
        epi_extra=[(h1, out_tile(tm_e, D_MODEL)), (d_h2, out_tile(tm_e, D_MODEL)), (mlp_out, out_tile(tm_e, D_MODEL)),
                   vec(norm2_g), vec(sc2 + mlp_token[0:1, 0:1]), vec(g1)],
        b_spec=pl.BlockSpec((4, D_MODEL, w1_cols), lambda i, j, k: (k, 0, 0)), b_slabs=4,
        out_specs=[out_tile(tm_e, D_MODEL)] * 2 + [sums_spec] * 4)

    (d_ycat,) = _matmul("out_proj_dx", dm1, w_out_full, "nt", (n_lat_rows, D_MODEL, D_MODEL), (tm, D_MODEL, D_MODEL),
                        [((n_lat_rows, D_MODEL), F32)])
    (g_w_out,) = _matmul("out_proj_dw", ycat, dm1, "tn", (D_MODEL, D_MODEL, n_lat_rows), (D_MODEL, D_MODEL, 512),
                         [((D_MODEL, D_MODEL), F32)])

    dy, g_glu, dd8 = _glu_bwd(d_ycat, z_all, y_dir[0], y_dir[1], d_skip, glu_full, n_lat_rows)
    proj_send, proj_recv, proj_src, proj_land, proj_token = _exchange_start(
        "scatter_proj_start",
        [g_w_out.reshape(NDEV, D_MODEL // NDEV, D_MODEL), g_glu.reshape(NDEV, S5_WIDTH // NDEV, S5_WIDTH)], [False] * 2)
    perms = [p + proj_token[0:1, 0:1].astype(BF16) for p in perms]
    du, g_lam_re, g_lam_im, g_ldt, g_bt, g_cdiag = None, [], [], [], [], []
    for d in range(2):
        _, _, adj, bmat, cmat = disc[d]
        du, d_bdiag, d_cdiag, d_abar8 = _s5_scan_bwd(f"s5_scan_bwd{d}", d == 0, dy, z_all, states[d], bmat, cmat, adj,
                                                     perms[d], perms_t[d], du_other=du, d_skip=d_skip if d else None)
        d_bbar = jnp.transpose(d_bdiag.reshape(S5_BLOCKS, S5_GROUP, 2, NSTATE // S5_BLOCKS), (2, 1, 0, 3)).reshape(
            2 * S5_GROUP, NSTATE)
        d_lam8, d_bt = _s5_discretise_bwd(f"s5_disc_bwd{d}", lam_re[d], lam_im[d], ldt[d], bt_re[d], bt_im[d], d_abar8, d_bbar)
        g_lam_re.append(d_lam8[0].reshape(S5_GROUPS, S5_STATE))
        g_lam_im.append(d_lam8[1].reshape(S5_GROUPS, S5_STATE))
        g_ldt.append(d_lam8[2].reshape(S5_GROUPS, S5_STATE).sum(axis=-1))
        g_bt.append(d_bt)
        g_cdiag.append(d_cdiag)

    dhc_pad, conv_sums = _conv_bwd_norm(d_ycat, hc, conv_ln_g, conv_ln_b, n_lat_rows)
    d_v, d_gate, g_conv_w8 = _conv_bwd_taps(dhc_pad, hh_pad, z_all, conv_w_full, n_lat_rows)

    no_ctx = jnp.zeros((n_ctx_rows, CONV_WIDTH), BF16)
    dz_all = jnp.concatenate([du, jnp.concatenate([d_v, no_ctx]), jnp.concatenate([d_gate, no_ctx])], axis=1)
    (g_w_in_full,) = _matmul("in_proj_dw", a_all, dz_all, "tn", (D_MODEL, IN_COLS, n_rows), (D_MODEL, IN_COLS, tm_all),
                             [((D_MODEL, IN_COLS), F32)])
    g_w_in_parts = jnp.transpose(g_w_in_full.reshape(D_MODEL, NDEV, IN_COLS // NDEV), (1, 0, 2)).astype(BF16)
    win_send, win_recv, win_src, win_land, win_token = _exchange_start("scatter_w_in_start", [g_w_in_parts], [False])
    w_in_late = w_in_full + win_token[0:1, 0:1].astype(BF16)
    grad_x, *sums1 = _matmul(
        "in_proj_dx", dz_all, w_in_late, "nt", (n_lat_rows, D_MODEL, IN_COLS), (tm_e, D_MODEL, IN_COLS),
        [full_rows] + [sums] * 4, epi=_epi_norm_bwd,
        epi_extra=[(xs, out_tile(tm_e, D_MODEL)), (d_h1, out_tile(tm_e, D_MODEL)), (mix, out_tile(tm_e, D_MODEL)),
                   vec(norm1_g), vec(sc1)],
        out_specs=[out_tile(tm_e, D_MODEL)] + [sums_spec] * 4)
    (d_a_ctx,) = _matmul("in_proj_dx_ctx", dz_all, w_in_late, "nt", (n_ctx_rows, D_MODEL, IN_COLS),
                         (ROW_BLOCK, D_MODEL, IN_COLS), [((n_ctx_rows, D_MODEL), F32)],
                         a_spec=pl.BlockSpec((ROW_BLOCK, IN_COLS), lambda i, j, k: (i + n_lat, 0)))
    (sums1c,) = _norm_bwd("norm1_bwd_ctx", cs, d_a_ctx, 0, norm1_g, modc[1:2])

    s1, s1c, s2 = [p.sum(axis=(0, 1)) for p in sums1], sums1c.sum(axis=1), [p.sum(axis=(0, 1)) for p in sums2]
    d_mod = jnp.concatenate([s1[0], s1[1], s1[3], s2[0], s2[1], s2[3]])
    d_modc = jnp.concatenate([s1c[0], s1c[1], jnp.zeros((4 * D_MODEL,), F32)])
    (dmod_g,), _ = _exchange("gather_dmod", [jnp.stack([d_mod, d_modc])], [True])
    dmod16 = jnp.concatenate([dmod_g[:, 0], dmod_g[:, 1]])
    dmod16_loc = lax.dynamic_slice(dmod16, (0, me * ada_cols), (16, ada_cols))
    cond_bwd = jnp.concatenate([c_all, jnp.broadcast_to(c_ctx[None], (NDEV, D_MODEL))])
    g_ada_w, g_c_ctx8 = _ada_bwd(cond_bwd, dmod16_loc, ada_w[0], c_ctx[None])

    small_parts = dict(
        c_ctx=g_c_ctx8[0], ada_b=d_mod + d_modc, norm1_g=s1[2] + s1c[2],
        s5_lam_re=jnp.stack(g_lam_re), s5_lam_im=jnp.stack(g_lam_im), s5_log_dt=jnp.stack(g_ldt),
        s5_d=dd8.sum(axis=0), conv_b=conv_sums[0].sum(axis=0), conv_ln_g=conv_sums[1].sum(axis=0),
        conv_ln_b=conv_sums[2].sum(axis=0), norm2_g=s2[2], final_g=d_final_g8.sum(axis=(0, 1)))
    reduced_shapes = [(SMALL_PACKED_ROWS, D_MODEL), (2, 2 * S5_GROUP, NSTATE), (2,) + _S5_DIAG, (1,)]
    small_g = _pack_rows(
        [_pack_rows([small_parts[n] for n in SMALL], SMALL_PACKED_ROWS), jnp.stack(g_bt), jnp.stack(g_cdiag),
         (0.5 / D_MODEL * jnp.sum(err_sums)).reshape(1)], SMALL_ROWS).reshape(NDEV, SMALL_ROWS // NDEV, D_MODEL)
    g_conv_w_parts = jnp.transpose(g_conv_w8.sum(axis=1).reshape(CONV_K, NDEV, CONV_WIDTH // NDEV), (1, 0, 2))

    res = {}

    def own_chunk(src):
        return lax.dynamic_index_in_dim(src, me, 0, keepdims=False)

    def adamw_big(name, parts):
        outs = _adamw("adamw_" + name, weights[name][0], parts, mom1[name][0], mom2[name][0])
        res[name] = [o[None] for o in outs]
        return outs[0]

    sm_send, sm_recv, sm_src, sm_land, sm_token = _exchange_start("scatter_small_start", [g_conv_w_parts, small_g],
                                                                  [False] * 2)
    mlp_src, mlp_landed = _exchange_wait("scatter_mlp_wait", mlp_send, mlp_recv, mlp_src, mlp_land, [False] * 2, sm_token)
    p_w1, p_w2 = [_with_own(l, own_chunk(s), me) for l, s in zip(mlp_landed, mlp_src)]
    adamw_big("ada_w", g_ada_w[None])
    adamw_big("mlp_w1", p_w1)
    done = adamw_big("mlp_w2", p_w2)
    sm_src, sm_landed = _exchange_wait("scatter_small_wait", sm_send, sm_recv, sm_src, sm_land, [False] * 2, done)
    p_conv_w, p_small = [_with_own(l, own_chunk(s), me) for l, s in zip(sm_landed, sm_src)]
    ga_send, ga_recv, ga_src, ga_land, ga_token = _exchange_start("gather_small_start", [_sum_parts(p_small)], [True])
    proj_src, proj_landed = _exchange_wait("scatter_proj_wait", proj_send, proj_recv, proj_src, proj_land, [False] * 2,
                                           ga_token)
    p_w_out, p_glu = [_with_own(l, own_chunk(s), me) for l, s in zip(proj_landed, proj_src)]
    adamw_big("w_out", p_w_out)
    done = adamw_big("s5_w_glu", p_glu)
    win_src, win_landed = _exchange_wait("scatter_w_in_wait", win_send, win_recv, win_src, win_land, [False], done)
    adamw_big("w_in", _with_own(win_landed[0], own_chunk(win_src[0]), me))
    done = adamw_big("conv_w", p_conv_w)
    ga_own, ga_landed = _exchange_wait("gather_small_wait", ga_send, ga_recv, ga_src, ga_land, [True], done)
    small_all = _with_own(ga_landed[0], ga_own[0], me).reshape(1, SMALL_ROWS, D_MODEL)
    _, r_bt, r_cdiag, loss = _unpack_rows(small_all, reduced_shapes)
    loss = loss.reshape(())
    pack = lambda src: _pack_rows([src[n] for n in SMALL], SMALL_PACKED_ROWS)
    outs = _adamw("adamw_small", pack(weights), small_all, pack(mom1), pack(mom2))
    unpacked = [_unpack_rows(o, [weights[n].shape for n in SMALL]) for o in outs]
    for i, name in enumerate(SMALL):
        res[name] = [u[i] for u in unpacked]
    to_ghp = lambda t: jnp.transpose(t.reshape(2, S5_GROUP, S5_GROUPS, S5_STATE), (0, 2, 1, 3))[None]
    r_c = jnp.transpose(r_cdiag.reshape(2, S5_BLOCKS, S5_GROUP, 2, groups_per_block, S5_STATE), (3, 0, 1, 4, 2, 5)).reshape(
        2, 1, 2, S5_GROUPS, S5_GROUP, S5_STATE)
    swap = lambda t: jnp.swapaxes(t, -1, -2)
    for name, grad in (("s5_b_re", to_ghp(r_bt[:, :S5_GROUP])), ("s5_b_im", to_ghp(r_bt[:, S5_GROUP:]))):
        outs = _adamw_native("adamw_" + name, swap(weights[name]), grad, swap(mom1[name]), swap(mom2[name]))
        res[name] = [swap(grad), *[swap(o) for o in outs]]
    for name, grad in (("s5_c_re", r_c[0]), ("s5_c_im", -r_c[1])):
        res[name] = [grad, *_adamw_native("adamw_" + name, weights[name], grad, mom1[name], mom2[name])]

    return (loss, grad_x[None], *[res[n][0] for n in order], *[res[n][1] for n in order],
            *[res[n][2] for n in order], *[res[n][3] for n in order])
```

```python
import jax
import jax.numpy as jnp
from jax import lax
from jax.experimental import pallas as pl
from jax.experimental.pallas import tpu as pltpu

F32 = jnp.float32
BF16 = jnp.bfloat16
MESH = pl.DeviceIdType.MESH
ANY = pl.BlockSpec(memory_space=pl.ANY)

NDEV = 8
D_MODEL = 1024
GRID_W = 64
S5_WIDTH = 512
S5_GROUP = 16
S5_GROUPS = 32
S5_STATE = 64
NSTATE = S5_GROUPS * S5_STATE
CONV_WIDTH = 512
CONV_K = 31
IN_COLS = S5_WIDTH + 2 * CONV_WIDTH
D_FF = 4 * D_MODEL
EPS_RMS = 1e-6
EPS_LN = 1e-5
ADAM_LR = 0.001
ADAM_B1 = 0.9
ADAM_B2 = 0.999
ADAM_EPS = 1e-08
ADAM_WD = 0.01
ADAM_STEP = 10

SUBLANES = 8
LANES = 128
ROW_BLOCK = 256
SCAN_LANES = 512
SCAN_UNROLL = 32
SEGMENTS = SUBLANES
STEPS = ROW_BLOCK // SEGMENTS
S5_BLOCKS = 4
S5_BLOCK_WIDTH = S5_WIDTH // S5_BLOCKS
CONV_ROWS = 64
CONV_BWD_ROWS = 32
VMEM_LIMIT = 48 * 1024 * 1024
SMALL_ROWS = 320


def _params(sem=None):
    kw = dict(vmem_limit_bytes=VMEM_LIMIT)
    if sem is not None:
        kw["dimension_semantics"] = sem
    return pltpu.CompilerParams(**kw)


def _sds(shape, dtype=F32):
    return jax.ShapeDtypeStruct(tuple(shape), dtype)


def _fold8(x):
    return x.reshape(x.shape[0] // SUBLANES, SUBLANES, x.shape[1]).sum(axis=0)


def _sigmoid(x):
    return 1.0 / (1.0 + jnp.exp(-x))


def _silu(x):
    return x * _sigmoid(x)


def _dsilu(x):
    s = _sigmoid(x)
    return s * (1.0 + x * (1.0 - s))


_GELU_C = 0.7978845608028654


def _gelu(x):
    return 0.5 * x * (1.0 + jnp.tanh(_GELU_C * (x + 0.044715 * x * x * x)))


def _dgelu(x):
    t = jnp.tanh(_GELU_C * (x + 0.044715 * x * x * x))
    return 0.5 * (1.0 + t) + 0.5 * x * (1.0 - t * t) * _GELU_C * (1.0 + 3.0 * 0.044715 * x * x)


def _rms(x):
    rstd = lax.rsqrt(jnp.mean(x * x, axis=-1, keepdims=True) + EPS_RMS)
    return x * rstd, rstd


def _epi_residual_prenorm(acc, res, gate, gain, scale, shift):
    h = res + gate * acc
    xh, _ = _rms(h)
    a = (xh * gain) * (1.0 + scale) + shift
    return acc, h, a, a.T


def _epi_residual_loss(acc, res, gate, target, gain):
    h = res + gate * acc
    xh, rstd = _rms(h)
    err = xh * gain - target
    dy = err * (1.0 / h.shape[-1])
    dxh = dy * gain
    dh = rstd * (dxh - xh * jnp.mean(dxh * xh, axis=-1, keepdims=True))
    return acc, dh, dh * gate, _fold8(err * err), _fold8(dy * xh)


def _epi_norm_bwd(d_act, x, res, aux, gain, scale, gate=None):
    xh, rstd = _rms(x)
    dn = d_act * (1.0 + scale)
    dxh = dn * gain
    dx = res + rstd * (dxh - xh * jnp.mean(dxh * xh, axis=-1, keepdims=True))
    sums = (_fold8(d_act), _fold8(d_act * (xh * gain)), _fold8(dn * xh), _fold8(res * aux))
    return (dx, *sums) if gate is None else (dx, dx * gate, *sums)


def _dot(a, b, mode):
    dims = {"nn": (((1,), (0,)), ((), ())), "nt": (((1,), (1,)), ((), ())), "tn": (((0,), (0,)), ((), ()))}[mode]
    return lax.dot_general(a, b, dims, preferred_element_type=F32)


def _peers(x, y, c):
    out = []
    for k in range(1, NDEV):
        px = 1 - x if k & 4 else x
        py = 1 - y if k & 2 else y
        pc = 1 - c if k & 1 else c
        out.append(((px, py, pc), 4 * px + 2 * py + pc))
    return out


def _exchange_copies(src, land, send_sems, recv_sems, gather):
    x, y, c = lax.axis_index("x"), lax.axis_index("y"), lax.axis_index("c")
    me = 4 * x + 2 * y + c
    out = []
    for a in range(len(src)):
        for k, (peer, plin) in enumerate(_peers(x, y, c)):
            chunk = src[a] if gather[a] else src[a].at[plin]
            sems = dict(send_sem=send_sems.at[a * (NDEV - 1) + k], recv_sem=recv_sems.at[a * (NDEV - 1) + k],
                        device_id=peer, device_id_type=MESH)
            out.append((pltpu.make_async_remote_copy(src_ref=chunk, dst_ref=land[a].at[me], **sems),
                        pltpu.make_async_remote_copy(src_ref=chunk, dst_ref=land[a].at[plin], **sems)))
    return out


def _exchange(name, srcs, gather):
    n = len(srcs)
    outs = [_sds(((NDEV,) + s.shape) if g else s.shape, s.dtype) for s, g in zip(srcs, gather)]

    def body(*refs):
        src, dst, token = refs[:n], refs[n:2 * n], refs[2 * n]
        send_sems, recv_sems, local_sems = refs[2 * n + 1:]
        me = 4 * lax.axis_index("x") + 2 * lax.axis_index("y") + lax.axis_index("c")
        local = [pltpu.make_async_copy(src[a] if gather[a] else src[a].at[me], dst[a].at[me], local_sems.at[a])
                 for a in range(n)]
        for copy in local:
            copy.start()
        copies = _exchange_copies(src, dst, send_sems, recv_sems, gather)
        for copy, _ in copies:
            copy.start()
        token[...] = jnp.zeros_like(token)
        for copy, landing in copies:
            copy.wait_send()
            landing.wait_recv()
        for copy in local:
            copy.wait()

    nsem = n * (NDEV - 1)
    out = pl.pallas_call(
        body, name=name, out_shape=outs + [_sds((SUBLANES, LANES))], in_specs=[ANY] * n,
        out_specs=[ANY] * n + [pl.BlockSpec(memory_space=pltpu.VMEM)],
        scratch_shapes=[pltpu.SemaphoreType.DMA((nsem,)), pltpu.SemaphoreType.DMA((nsem,)), pltpu.SemaphoreType.DMA((n,))],
    )(*srcs)
    return out[:n], out[n]


HBM = pl.BlockSpec(memory_space=pltpu.HBM)
SEM = pl.BlockSpec(memory_space=pltpu.SEMAPHORE)
EFFECT = pltpu.SideEffectType.DATAFLOW_SIDE_EFFECTING


def _exchange_start_groups(name, groups):
    srcs = [s for g_srcs, _ in groups for s in g_srcs]
    gathers = [g for _, g_gather in groups for g in g_gather]
    lands = [lax.empty(((NDEV,) + s.shape) if g else s.shape, s.dtype) for s, g in zip(srcs, gathers)]
    n, ng = len(srcs), len(groups)

    def body(*refs):
        src, land = refs[:n], refs[n:2 * n]
        sems = refs[2 * n:2 * n + 2 * ng]
        token = refs[-1]
        first = 0
        for g, (g_srcs, g_gather) in enumerate(groups):
            last = first + len(g_srcs)
            for copy, _ in _exchange_copies(src[first:last], land[first:last], sems[2 * g], sems[2 * g + 1], g_gather):
                copy.start()
            first = last
        token[...] = jnp.zeros_like(token)

    hbm = lambda v: pltpu.HBM(v.shape, v.dtype)
    sem_shapes = []
    for g_srcs, _ in groups:
        sem_shapes += [pltpu.SemaphoreType.DMA((len(g_srcs) * (NDEV - 1),))] * 2
    out = pl.pallas_call(
        body, name=name,
        out_shape=(*sem_shapes, *[hbm(v) for v in srcs], *[hbm(v) for v in lands], _sds((SUBLANES, LANES))),
        in_specs=[HBM] * (2 * n),
        out_specs=(*([SEM] * (2 * ng)), *([HBM] * (2 * n)), pl.BlockSpec(memory_space=pltpu.VMEM)),
        input_output_aliases={i: 2 * ng + i for i in range(2 * n)},
        compiler_params=pltpu.CompilerParams(has_side_effects=EFFECT),
    )(*[pltpu.with_memory_space_constraint(v, pltpu.HBM) for v in srcs + lands])
    src_out, land_out = out[2 * ng:2 * ng + n], out[2 * ng + n:2 * ng + 2 * n]
    result, first = [], 0
    for g, (g_srcs, _) in enumerate(groups):
        last = first + len(g_srcs)
        result.append((out[2 * g], out[2 * g + 1], src_out[first:last], land_out[first:last]))
        first = last
    return result, out[-1]


def _exchange_start(name, srcs, gather):
    (group,), token = _exchange_start_groups(name, [(srcs, gather)])
    return (*group, token)


def _exchange_wait(name, send_sems, recv_sems, srcs, lands, gather, after):
    n = len(srcs)

    def body(*refs):
        src, land = refs[:n], refs[n:2 * n]
        send_ref, recv_ref = refs[2 * n], refs[2 * n + 1]
        for copy, landing in _exchange_copies(src, land, send_ref, recv_ref, gather):
            copy.wait_send()
            landing.wait_recv()

    hbm = lambda v: pltpu.HBM(v.shape, v.dtype)
    out = pl.pallas_call(
        body, name=name, out_shape=[hbm(v) for v in list(srcs) + list(lands)],
        in_specs=[HBM] * (2 * n) + [SEM, SEM, ANY], out_specs=[HBM] * (2 * n),
        input_output_aliases={i: i for i in range(2 * n)},
        compiler_params=pltpu.CompilerParams(has_side_effects=EFFECT),
    )(*srcs, *lands, send_sems, recv_sems, after)
    return out[:n], out[n:]


def _with_own(landed, own, me):
    return lax.dynamic_update_slice(landed, own[None], (me,) + (0,) * own.ndim)


def _matmul(name, a, b, mode, mnk, tiles, outs, a_spec=None, b_spec=None, a_fn=None, a_extra=(),
            epi=None, epi_extra=(), out_specs=None, b_slabs=1):
    m_, n_, k_ = mnk
    tm, tn, tk = tiles
    nk = k_ // tk
    if a_spec is None:
        a_spec = (pl.BlockSpec((tk, tm), lambda i, j, k: (k, i)) if mode == "tn"
                  else pl.BlockSpec((tm, tk), lambda i, j, k: (i, k)))
    if b_spec is None:
        b_spec = (pl.BlockSpec((tn, tk), lambda i, j, k: (j, k)) if mode == "nt"
                  else pl.BlockSpec((tk, tn), lambda i, j, k: (k, j)))
    if out_specs is None:
        out_specs = [pl.BlockSpec((tm, tn), lambda i, j, k: (i, j)) for _ in outs]
    na, ne, no = len(a_extra), len(epi_extra), len(outs)

    def body(*refs):
        a_ref, b_ref = refs[0], refs[1]
        ax = refs[2:2 + na]
        ex = refs[2 + na:2 + na + ne]
        o = refs[2 + na + ne:2 + na + ne + no]

        def finish(res):
            res = epi(res, *[r[...] for r in ex]) if epi is not None else (res,)
            for ref, val in zip(o, res):
                ref[...] = val.astype(ref.dtype)

        at = a_ref[...]
        if a_fn is not None:
            at = a_fn(at, *[r[...] for r in ax])
        at = at.astype(BF16)
        if b_slabs == 1:
            part = _dot(at, b_ref[...].astype(BF16), mode)
        else:
            ks = tk // b_slabs
            part = _dot(at[:, 0:ks], b_ref[0].astype(BF16), mode)
            for s in range(1, b_slabs):
                part = part + _dot(at[:, s * ks:(s + 1) * ks], b_ref[s].astype(BF16), mode)
        if nk == 1:
            finish(part)
            return
        acc = refs[-1]
        k = pl.program_id(2)

        @pl.when(k == 0)
        def _():
            acc[...] = part

        @pl.when(k > 0)
        def _():
            acc[...] += part

        @pl.when(k == nk - 1)
        def _():
            finish(acc[...])

    return pl.pallas_call(
        body, name=name, grid=(m_ // tm, n_ // tn, nk),
        in_specs=[a_spec, b_spec] + [s for _, s in a_extra] + [s for _, s in epi_extra],
        out_specs=out_specs, out_shape=[_sds(s, d) for s, d in outs],
        scratch_shapes=[pltpu.VMEM((tm, tn), F32)] if nk > 1 else [],
        compiler_params=_params(("parallel", "parallel", "arbitrary")),
    )(a, b, *[x for x, _ in a_extra], *[x for x, _ in epi_extra])


def _prenorm(name, x, ctx, gain, shsc):
    n_lat = x.shape[0] // ROW_BLOCK
    n_ctx = 0 if ctx is None else ctx.shape[0] // ROW_BLOCK
    d = x.shape[1]

    def norm(src, g_ref, m_ref, o_ref):
        xv = src[...]
        xh = xv * lax.rsqrt(jnp.mean(xv * xv, axis=-1, keepdims=True) + EPS_RMS)
        o_ref[...] = ((xh * g_ref[...]) * (1.0 + m_ref[1:2, :]) + m_ref[0:1, :]).astype(o_ref.dtype)

    def body(*refs):
        if ctx is None:
            x_ref, g_ref, m_ref, o_ref = refs
            norm(x_ref, g_ref, m_ref, o_ref)
        else:
            x_ref, c_ref, g_ref, m_ref, o_ref = refs
            i = pl.program_id(0)

            @pl.when(i < n_lat)
            def _():
                norm(x_ref, g_ref, m_ref, o_ref)

            @pl.when(i >= n_lat)
            def _():
                norm(c_ref, g_ref, m_ref, o_ref)

    in_specs = [pl.BlockSpec((ROW_BLOCK, d), lambda i: (jnp.minimum(i, n_lat - 1), 0))]
    args = [x]
    if ctx is not None:
        in_specs.append(pl.BlockSpec((ROW_BLOCK, d), lambda i: (jnp.maximum(i - n_lat, 0), 0)))
        args.append(ctx)
    in_specs += [pl.BlockSpec((1, d), lambda i: (0, 0)),
                 pl.BlockSpec((None, 2, d), lambda i: (jnp.minimum(i // n_lat, 1), 0, 0))]
    args += [gain, shsc]
    return pl.pallas_call(
        body, name=name, grid=(n_lat + n_ctx,), in_specs=in_specs,
        out_specs=pl.BlockSpec((ROW_BLOCK, d), lambda i: (i, 0)),
        out_shape=_sds(((n_lat + n_ctx) * ROW_BLOCK, d), BF16),
        compiler_params=_params(("parallel",)),
    )(*args)


def _norm_bwd(name, x, d_act, d_act_row0, gain, scale, res=None, aux=None):
    rows, d = x.shape
    nb = rows // ROW_BLOCK
    has_res = res is not None

    def body(*refs):
        if has_res:
            x_ref, da_ref, g_ref, sc_ref, r_ref, aux_ref, dx_ref, sums = refs
        else:
            x_ref, da_ref, g_ref, sc_ref, sums = refs
        i = pl.program_id(0)

        @pl.when(i == 0)
        def _():
            sums[...] = jnp.zeros_like(sums)

        xv, da = x_ref[...], da_ref[...]
        rstd = lax.rsqrt(jnp.mean(xv * xv, axis=-1, keepdims=True) + EPS_RMS)
        xh = xv * rstd
        g = g_ref[...]
        dn = da * (1.0 + sc_ref[...])
        sums[0] += _fold8(da)
        sums[1] += _fold8(da * (xh * g))
        sums[2] += _fold8(dn * xh)
        if has_res:
            dxh = dn * g
            dx = rstd * (dxh - xh * jnp.mean(dxh * xh, axis=-1, keepdims=True))
            rv = r_ref[...]
            dx_ref[...] = rv + dx
            sums[3] += _fold8(rv * aux_ref[...])

    row = lambda i: (i, 0)
    vec = pl.BlockSpec((1, d), lambda i: (0, 0))
    in_specs = [pl.BlockSpec((ROW_BLOCK, d), row), pl.BlockSpec((ROW_BLOCK, d), lambda i: (i + d_act_row0, 0)), vec, vec]
    args = [x, d_act, gain, scale]
    out_shape = [_sds((4, SUBLANES, d))]
    out_specs = [pl.BlockSpec((4, SUBLANES, d), lambda i: (0, 0, 0))]
    if has_res:
        in_specs += [pl.BlockSpec((ROW_BLOCK, d), row), pl.BlockSpec((ROW_BLOCK, d), row)]
        args += [res, aux]
        out_shape = [_sds((rows, d))] + out_shape
        out_specs = [pl.BlockSpec((ROW_BLOCK, d), row)] + out_specs
    return pl.pallas_call(
        body, name=name, grid=(nb,), in_specs=in_specs, out_specs=out_specs, out_shape=out_shape,
        compiler_params=_params(("arbitrary",)),
    )(*args)


def _ada_fwd(cond16, ada_w_loc, ada_b_loc):
    cols = ada_w_loc.shape[1]

    def body(c_ref, w_ref, b_ref, o_ref):
        s = _silu(c_ref[...]).astype(BF16)
        o_ref[...] = _dot(s, w_ref[...].astype(BF16), "nn") + b_ref[...]

    return pl.pallas_call(body, name="ada_fwd", out_shape=_sds((16, cols)), compiler_params=_params())(
        cond16, ada_w_loc, ada_b_loc)


def _ada_bwd(cond16, dmod16, ada_w_loc, c_ctx_row):
    k_, cols = ada_w_loc.shape

    def body(c_ref, dm_ref, w_ref, cc_ref, gw_ref, gc_ref):
        s = _silu(c_ref[...]).astype(BF16)
        dm = dm_ref[...]
        gw_ref[...] = _dot(s, dm.astype(BF16), "tn")
        dmc = jnp.sum(dm[8:16, :], axis=0, keepdims=True)
        dmc8 = jnp.broadcast_to(dmc, (SUBLANES, cols)).astype(BF16)
        ds = _dot(dmc8, w_ref[...].astype(BF16), "nt")
        row = lax.broadcasted_iota(jnp.int32, ds.shape, 0)
        gc_ref[...] = jnp.where(row == 0, ds * _dsilu(cc_ref[...]), 0.0)

    return pl.pallas_call(body, name="ada_bwd", out_shape=[_sds((k_, cols)), _sds((SUBLANES, k_))],
                          compiler_params=_params())(cond16, dmod16, ada_w_loc, c_ctx_row)


def _cmul(a, b):
    return a[0] * b[0] - a[1] * b[1], a[0] * b[1] + a[1] * b[0]


def _disc(lam_re, lam_im, ldt):
    dt = jnp.exp(ldt)
    mag = jnp.exp(lam_re * dt)
    th = lam_im * dt
    a_re, a_im = mag * jnp.cos(th), mag * jnp.sin(th)
    den = lam_re * lam_re + lam_im * lam_im
    n_re = a_re - 1.0
    f_re = (n_re * lam_re + a_im * lam_im) / den
    f_im = (a_im * lam_re - n_re * lam_im) / den
    return dt, mag, th, a_re, a_im, den, n_re, f_re, f_im


def _block_diag_mask(shape):
    row = lax.broadcasted_iota(jnp.int32, shape, 0)
    col = lax.broadcasted_iota(jnp.int32, shape, 1)
    return lax.shift_right_logical(row, 4) == lax.shift_right_logical(col, 6)


TAB_A = 0
TAB_BIG = 1
TAB_SEG = 4
TAB_PW = 5
TAB_ROWS = TAB_PW + STEPS


def _s5_discretise(name, ascending, lam_re, lam_im, ldt, bt_re, bt_im, ct_re, ct_im):
    def write_tables(ref, pw, big, asc, sign):
        row = lax.broadcasted_iota(jnp.int32, (SUBLANES, NSTATE), 0)
        full = lambda v: jnp.broadcast_to(v, (SUBLANES, NSTATE))

        def put(t, p):
            ref[0, t] = full(p[0])
            ref[1, t] = full(sign * p[1])

        put(TAB_A, pw[0])
        for t in range(3):
            put(TAB_BIG + t, big[t])
        seg = [big[0]]
        for _ in range(SEGMENTS - 1):
            seg.append(_cmul(seg[-1], big[0]))
        seg_re = jnp.zeros((SUBLANES, NSTATE), F32)
        seg_im = jnp.zeros((SUBLANES, NSTATE), F32)
        for r in range(SEGMENTS):
            p = seg[r] if asc else seg[SEGMENTS - 1 - r]
            seg_re = jnp.where(row == r, p[0], seg_re)
            seg_im = jnp.where(row == r, sign * p[1], seg_im)
        ref[0, TAB_SEG] = seg_re
        ref[1, TAB_SEG] = seg_im
        for k in range(STEPS):
            put(TAB_PW + k, pw[k])

    def body(lr_ref, li_ref, ldt_ref, br_ref, bi_ref, cr_ref, ci_ref, bb_ref, tab_ref, adj_ref, bm_ref, cm_ref):
        _, _, _, a_re, a_im, _, _, f_re, f_im = _disc(lr_ref[...], li_ref[...], ldt_ref[...])
        bre, bim = br_ref[...], bi_ref[...]
        bb_re = f_re * bre - f_im * bim
        bb_im = f_re * bim + f_im * bre
        bb_ref[0:S5_GROUP, :] = bb_re
        bb_ref[S5_GROUP:2 * S5_GROUP, :] = bb_im
        pw = [(a_re, a_im)]
        for _ in range(STEPS - 1):
            pw.append(_cmul(pw[-1], (a_re, a_im)))
        big = [pw[STEPS - 1]]
        for _ in range(2):
            big.append(_cmul(big[-1], big[-1]))
        write_tables(tab_ref, pw, big, ascending, 1.0)
        write_tables(adj_ref, pw, big, not ascending, -1.0)
        half = NSTATE // S5_BLOCKS
        mask = _block_diag_mask((S5_BLOCK_WIDTH, half))
        tile = lambda v: jnp.broadcast_to(v[None], (S5_BLOCK_WIDTH // S5_GROUP, S5_GROUP, half)).reshape(S5_BLOCK_WIDTH, half)
        for c in range(S5_BLOCKS):
            cols = slice(c * half, (c + 1) * half)
            rows = slice(c * S5_BLOCK_WIDTH, (c + 1) * S5_BLOCK_WIDTH)
            bm_ref[c, :, 0:half] = jnp.where(mask, tile(bb_re[:, cols]), 0.0).astype(BF16)
            bm_ref[c, :, half:2 * half] = jnp.where(mask, tile(bb_im[:, cols]), 0.0).astype(BF16)
            cm_ref[c, :, 0:half] = jnp.where(mask, cr_ref[rows, :], 0.0).astype(BF16)
            cm_ref[c, :, half:2 * half] = jnp.where(mask, -ci_ref[rows, :], 0.0).astype(BF16)

    blocked = _sds((S5_BLOCKS, S5_BLOCK_WIDTH, 2 * NSTATE // S5_BLOCKS), BF16)
    return pl.pallas_call(
        body, name=name,
        out_shape=[_sds((2 * S5_GROUP, NSTATE)), _sds((2, TAB_ROWS, SUBLANES, NSTATE)),
                   _sds((2, TAB_ROWS, SUBLANES, NSTATE)), blocked, blocked],
        compiler_params=_params(),
    )(lam_re, lam_im, ldt, bt_re, bt_im, ct_re, ct_im)


def _s5_discretise_bwd(name, lam_re, lam_im, ldt, bt_re, bt_im, d_abar8, d_bbar):
    def body(lr_ref, li_ref, ldt_ref, br_ref, bi_ref, da_ref, db_ref, dl_ref, dbt_ref):
        lam_re, lam_im = lr_ref[...], li_ref[...]
        dt, _, _, a_re, a_im, den, n_re, f_re, f_im = _disc(lam_re, lam_im, ldt_ref[...])
        bre, bim = br_ref[...], bi_ref[...]
        dbr, dbi = db_ref[0:S5_GROUP, :], db_ref[S5_GROUP:2 * S5_GROUP, :]
        dbt_ref[0:S5_GROUP, :] = f_re * dbr + f_im * dbi
        dbt_ref[S5_GROUP:2 * S5_GROUP, :] = f_re * dbi - f_im * dbr
        df_re = jnp.sum(bre * dbr + bim * dbi, axis=0, keepdims=True)
        df_im = jnp.sum(bre * dbi - bim * dbr, axis=0, keepdims=True)
        da = da_ref[...]
        da_re = jnp.sum(da[:, 0:NSTATE], axis=0, keepdims=True)
        da_im = jnp.sum(da[:, NSTATE:2 * NSTATE], axis=0, keepdims=True)
        da_re = da_re + (df_re * lam_re - df_im * lam_im) / den
        da_im = da_im + (df_re * lam_im + df_im * lam_re) / den
        ff = (f_re * df_re + f_im * df_im) * 2.0 / den
        d_lr = (df_re * n_re + df_im * a_im) / den - ff * lam_re
        d_li = (df_re * a_im - df_im * n_re) / den - ff * lam_im
        d_mag_mag = da_re * a_re + da_im * a_im
        d_th = da_im * a_re - da_re * a_im
        d_lr = d_lr + d_mag_mag * dt
        d_li = d_li + d_th * dt
        d_ldt = (d_mag_mag * lam_re + d_th * lam_im) * dt
        row = lax.broadcasted_iota(jnp.int32, (SUBLANES, NSTATE), 0)
        dl_ref[...] = jnp.where(row == 0, d_lr, jnp.where(row == 1, d_li, jnp.where(row == 2, d_ldt, 0.0)))

    return pl.pallas_call(
        body, name=name, out_shape=[_sds((SUBLANES, NSTATE)), _sds((2 * S5_GROUP, NSTATE))],
        compiler_params=_params(),
    )(lam_re, lam_im, ldt, bt_re, bt_im, d_abar8, d_bbar)


def _segment_permutation(reverse_time):
    rho = jnp.arange(ROW_BLOCK)
    src = STEPS * (rho % SEGMENTS) + rho // SEGMENTS
    if reverse_time:
        src = ROW_BLOCK - 1 - src
    return (src[:, None] == jnp.arange(ROW_BLOCK)[None, :]).astype(BF16)


def _permute_rows(perm_ref, v):
    return _dot(perm_ref[...], v, "nn").astype(BF16)


def _unpermute_rows(perm_t_ref, v):
    hi = v.astype(BF16)
    lo = (v - hi.astype(F32)).astype(BF16)
    return _dot(perm_t_ref[...], hi, "nn") + _dot(perm_t_ref[...], lo, "nn")


def _unrolled_loop(step, init):
    def trip(o, state):
        for u in range(SCAN_UNROLL):
            state = step(o * SCAN_UNROLL + u, state)
        return state

    if SCAN_UNROLL == STEPS:
        return trip(0, init)
    return lax.fori_loop(0, STEPS // SCAN_UNROLL, trip, init)


def _scan_chunk(x_ref, out_ref, tab_ref, carry_re, carry_im, ascending, pair_ref=None, acc_ref=None, lane_chunks=None):
    w = SCAN_LANES
    half = NSTATE // S5_BLOCKS
    row = lax.broadcasted_iota(jnp.int32, (SUBLANES, w), 0)
    last = (SEGMENTS - 1) if ascending else 0

    def from_previous_segment(v, k, fill):
        if ascending:
            return jnp.where(row >= k, pltpu.roll(v, k, 0), fill)
        return jnp.where(row < SEGMENTS - k, pltpu.roll(v, SEGMENTS - k, 0), fill)

    def tile_rows(k):
        return pl.ds(pl.multiple_of((k if ascending else STEPS - 1 - k) * SUBLANES, SUBLANES), SUBLANES)

    for j in (range(NSTATE // w) if lane_chunks is None else lane_chunks):
        n_l = pl.ds(j * w, w)
        lane0 = (j * w // half) * 2 * half + (j * w) % half
        re_l, im_l = pl.ds(lane0, w), pl.ds(lane0 + half, w)
        tab = lambda t, n_l=n_l: (tab_ref[0, t, :, n_l], tab_ref[1, t, :, n_l])
        a_re, a_im = tab(TAB_A)

        def local_step(k, h):
            rs = tile_rows(k)
            h_re = a_re * h[0] - a_im * h[1] + x_ref[rs, re_l]
            h_im = a_re * h[1] + a_im * h[0] + x_ref[rs, im_l]
            out_ref[rs, re_l] = h_re
            out_ref[rs, im_l] = h_im
            return h_re, h_im

        zero = jnp.zeros((SUBLANES, w), F32)
        end_re, end_im = _unrolled_loop(local_step, (zero, zero))
        for t, k in ((TAB_BIG, 1), (TAB_BIG + 1, 2), (TAB_BIG + 2, 4)):
            p_re, p_im = tab(t)
            s_re, s_im = from_previous_segment(end_re, k, 0.0), from_previous_segment(end_im, k, 0.0)
            end_re, end_im = end_re + (p_re * s_re - p_im * s_im), end_im + (p_re * s_im + p_im * s_re)
        c0_re, c0_im = carry_re[:, n_l], carry_im[:, n_l]
        p_re, p_im = tab(TAB_SEG)
        end_re = end_re + (p_re * c0_re - p_im * c0_im)
        end_im = end_im + (p_re * c0_im + p_im * c0_re)
        carry_re[:, n_l] = jnp.broadcast_to(end_re[last:last + 1, :], end_re.shape)
        carry_im[:, n_l] = jnp.broadcast_to(end_im[last:last + 1, :], end_im.shape)
        in_re = from_previous_segment(end_re, 1, c0_re)
        in_im = from_previous_segment(end_im, 1, c0_im)

        def carry_step(k, st):
            rs = tile_rows(k)
            p_re, p_im = tab_ref[0, TAB_PW + k, :, n_l], tab_ref[1, TAB_PW + k, :, n_l]
            o_re = out_ref[rs, re_l] + (p_re * in_re - p_im * in_im)
            o_im = out_ref[rs, im_l] + (p_re * in_im + p_im * in_re)
            out_ref[rs, re_l] = o_re
            out_ref[rs, im_l] = o_im
            if pair_ref is None:
                return st
            s_re, s_im = pair_ref[rs, re_l], pair_ref[rs, im_l]
            return (o_re, o_im, st[2] + (st[0] * s_re + st[1] * s_im), st[3] + (st[1] * s_re - st[0] * s_im))

        if pair_ref is None:
            _unrolled_loop(carry_step, 0)
        else:
            fin = _unrolled_loop(carry_step, (in_re, in_im, zero, zero))
            acc_ref[:, n_l] += fin[2]
            acc_ref[:, pl.ds(NSTATE + j * w, w)] += fin[3]


def _scan_block_index(i, n_lat, ctx_first_then_ascending):
    if ctx_first_then_ascending:
        return jnp.where(i == 0, n_lat, i - 1)
    return jnp.where(i == 0, n_lat, n_lat - i)


def _full_spec(shape):
    return pl.BlockSpec(shape, lambda i: (0,) * len(shape))


_S5_BLOCKED = (S5_BLOCKS, S5_BLOCK_WIDTH, 2 * NSTATE // S5_BLOCKS)
_S5_TABLES = (2, TAB_ROWS, SUBLANES, NSTATE)
_S5_DIAG = (S5_BLOCKS, S5_GROUP, 2 * NSTATE // S5_BLOCKS)


def _s5_scan_fwd(name, ascending, z_all, bmat, cmat, tab, perm, perm_t, y_other=None, d_skip=None, w_glu=None):
    rows = z_all.shape[0]
    nb = rows // ROW_BLOCK
    n_lat = nb - 1
    bw, sw = S5_BLOCK_WIDTH, 2 * NSTATE // S5_BLOCKS
    gated = y_other is not None

    def body(*refs):
        u_ref, bm_ref, cm_ref, tab_ref, p_ref, pt_ref = refs[:6]
        extra = refs[6:9] if gated else ()
        s_ref, y_ref = refs[6 + len(extra):8 + len(extra)]
        bu, yp, carry_re, carry_im = refs[-4:]

        @pl.when(pl.program_id(0) == 0)
        def _():
            carry_re[...] = jnp.zeros_like(carry_re)
            carry_im[...] = jnp.zeros_like(carry_im)

        up = _permute_rows(p_ref, u_ref[...].astype(BF16))
        for c in range(S5_BLOCKS):
            bu[:, c * sw:(c + 1) * sw] = _dot(up[:, c * bw:(c + 1) * bw], bm_ref[c], "nn")
        _scan_chunk(bu, s_ref, tab_ref, carry_re, carry_im, False)
        for c in range(S5_BLOCKS):
            yp[:, c * bw:(c + 1) * bw] = _dot(s_ref[:, c * sw:(c + 1) * sw].astype(BF16), cm_ref[c], "nt")
        y = _unpermute_rows(pt_ref, yp[...])
        y_ref[...] = y
        if gated:
            y_other_ref, d_ref, w_ref = extra
            gel = _gelu(d_ref[...] * u_ref[...] + y_other_ref[...] + y)
            refs[8 + len(extra)][...] = (gel * _sigmoid(_dot(gel.astype(BF16), w_ref[...], "nn"))).astype(BF16)

    blk = lambda i: (_scan_block_index(i, n_lat, ascending), 0)
    in_specs = [pl.BlockSpec((ROW_BLOCK, S5_WIDTH), blk), _full_spec(_S5_BLOCKED), _full_spec(_S5_BLOCKED),
                _full_spec(_S5_TABLES), _full_spec((ROW_BLOCK, ROW_BLOCK)), _full_spec((ROW_BLOCK, ROW_BLOCK))]
    args = [z_all, bmat, cmat, tab, perm, perm_t]
    out_specs = [pl.BlockSpec((ROW_BLOCK, 2 * NSTATE), blk), pl.BlockSpec((ROW_BLOCK, S5_WIDTH), blk)]
    out_shape = [_sds((rows, 2 * NSTATE)), _sds((rows, S5_WIDTH))]
    if gated:
        in_specs += [pl.BlockSpec((ROW_BLOCK, S5_WIDTH), blk), _full_spec((1, S5_WIDTH)), _full_spec((S5_WIDTH, S5_WIDTH))]
        args += [y_other, d_skip, w_glu]
        out_specs.append(pl.BlockSpec((ROW_BLOCK, S5_WIDTH),
                                      lambda i: (jnp.minimum(_scan_block_index(i, n_lat, ascending), n_lat - 1), 0)))
        out_shape.append(_sds((n_lat * ROW_BLOCK, S5_WIDTH + CONV_WIDTH), BF16))
    return pl.pallas_call(
        body, name=name, grid=(nb,), in_specs=in_specs, out_specs=out_specs, out_shape=out_shape,
        scratch_shapes=[pltpu.VMEM((ROW_BLOCK, 2 * NSTATE), F32), pltpu.VMEM((ROW_BLOCK, S5_WIDTH), F32),
                        pltpu.VMEM((SUBLANES, NSTATE), F32), pltpu.VMEM((SUBLANES, NSTATE), F32)],
        compiler_params=_params(("arbitrary",)),
    )(*args)


def _s5_scan_bwd(name, ascending, dy, z_all, states, bmat, cmat, adj, perm, perm_t, du_other=None, d_skip=None):
    rows = states.shape[0]
    nb = rows // ROW_BLOCK
    n_lat = nb - 1
    bw, sw = S5_BLOCK_WIDTH, 2 * NSTATE // S5_BLOCKS
    finish = du_other is not None

    def block_index(i):
        if ascending:
            return jnp.where(i == nb - 1, n_lat, n_lat - 1 - i)
        return jnp.where(i == nb - 1, n_lat, i)

    def body(*refs):
        dy_ref, u_ref, s_ref, bm_ref, cm_ref, adj_ref, p_ref, pt_ref = refs[:8]
        extra = refs[8:10] if finish else ()
        du_ref, db_ref, dc_ref, da_ref, g, dup, db_acc, dc_acc, carry_re, carry_im = refs[8 + len(extra):]
        i = pl.program_id(0)

        @pl.when(i == 0)
        def _():
            carry_re[...] = jnp.zeros_like(carry_re)
            carry_im[...] = jnp.zeros_like(carry_im)
            da_ref[...] = jnp.zeros_like(da_ref)
            db_acc[...] = jnp.zeros_like(db_acc)
            dc_acc[...] = jnp.zeros_like(dc_acc)

        has_dy = (i < nb - 1).astype(F32)
        dyp = _permute_rows(p_ref, (dy_ref[...] * has_dy).astype(BF16))
        up = _permute_rows(p_ref, u_ref[...].astype(BF16))
        for c in range(S5_BLOCKS):
            g[:, c * sw:(c + 1) * sw] = _dot(dyp[:, c * bw:(c + 1) * bw], cm_ref[c], "nn")
            dc_acc[c] += _dot(dyp[:, c * bw:(c + 1) * bw], s_ref[:, c * sw:(c + 1) * sw].astype(BF16), "tn")
            _scan_chunk(g, g, adj_ref, carry_re, carry_im, True, pair_ref=s_ref, acc_ref=da_ref, lane_chunks=[c])
            gc = g[:, c * sw:(c + 1) * sw].astype(BF16)
            dup[:, c * bw:(c + 1) * bw] = _dot(gc, bm_ref[c], "nt")
            db_acc[c] += _dot(up[:, c * bw:(c + 1) * bw], gc, "tn")
        du = _unpermute_rows(pt_ref, dup[...])
        if finish:
            du = du + extra[0][...] + (dy_ref[...] * has_dy) * extra[1][...]
        du_ref[...] = du.astype(du_ref.dtype)

        @pl.when(i == nb - 1)
        def _():
            mask = _block_diag_mask((bw, sw // 2))
            for acc, out in ((db_acc, db_ref), (dc_acc, dc_ref)):
                for c in range(S5_BLOCKS):
                    for part in range(2):
                        cols = slice(part * (sw // 2), (part + 1) * (sw // 2))
                        kept = jnp.where(mask, acc[c, :, cols], 0.0)
                        out[c, :, cols] = kept.reshape(bw // S5_GROUP, S5_GROUP, sw // 2).sum(axis=0)

    blk = lambda i: (block_index(i), 0)
    in_specs = [pl.BlockSpec((ROW_BLOCK, S5_WIDTH), lambda i: (jnp.minimum(block_index(i), n_lat - 1), 0)),
                pl.BlockSpec((ROW_BLOCK, S5_WIDTH), blk), pl.BlockSpec((ROW_BLOCK, 2 * NSTATE), blk),
                _full_spec(_S5_BLOCKED), _full_spec(_S5_BLOCKED), _full_spec(_S5_TABLES),
                _full_spec((ROW_BLOCK, ROW_BLOCK)), _full_spec((ROW_BLOCK, ROW_BLOCK))]
    args = [dy, z_all, states, bmat, cmat, adj, perm, perm_t]
    if finish:
        in_specs += [pl.BlockSpec((ROW_BLOCK, S5_WIDTH), blk), _full_spec((1, S5_WIDTH))]
        args += [du_other, d_skip]
    return pl.pallas_call(
        body, name=name, grid=(nb,), in_specs=in_specs,
        out_specs=[pl.BlockSpec((ROW_BLOCK, S5_WIDTH), blk), _full_spec(_S5_DIAG), _full_spec(_S5_DIAG),
                   _full_spec((SUBLANES, 2 * NSTATE))],
        out_shape=[_sds((rows, S5_WIDTH), BF16 if finish else F32), _sds(_S5_DIAG), _sds(_S5_DIAG),
                   _sds((SUBLANES, 2 * NSTATE))],
        scratch_shapes=[pltpu.VMEM((ROW_BLOCK, 2 * NSTATE), F32), pltpu.VMEM((ROW_BLOCK, S5_WIDTH), F32),
                        pltpu.VMEM(_S5_BLOCKED, F32), pltpu.VMEM(_S5_BLOCKED, F32),
                        pltpu.VMEM((SUBLANES, NSTATE), F32), pltpu.VMEM((SUBLANES, NSTATE), F32)],
        compiler_params=_params(("arbitrary",)),
    )(*args)


def _latent_row_tile(n_rows):
    return 512 if n_rows % 512 == 0 else ROW_BLOCK


def _glu_bwd(d_ycat, z_all, y0, y1, d_skip, w_glu, n_rows):
    def body(do_ref, u_ref, y0_ref, y1_ref, d_ref, w_ref, dy_ref, dw_ref, dd_ref):
        @pl.when(pl.program_id(0) == 0)
        def _():
            dw_ref[...] = jnp.zeros_like(dw_ref)
            dd_ref[...] = jnp.zeros_like(dd_ref)

        u = u_ref[...]
        y = d_ref[...] * u + y0_ref[...] + y1_ref[...]
        g = _gelu(y)
        gb = g.astype(BF16)
        w = w_ref[...]
        sg = _sigmoid(_dot(gb, w, "nn"))
        do = do_ref[...]
        dt = do * g * sg * (1.0 - sg)
        dtb = dt.astype(BF16)
        dg = do * sg + _dot(dtb, w, "nt")
        dy = dg * _dgelu(y)
        dy_ref[...] = dy
        dw_ref[...] += _dot(gb, dtb, "tn")
        dd_ref[...] += _fold8(dy * u)

    rows = _latent_row_tile(n_rows)
    row = pl.BlockSpec((rows, S5_WIDTH), lambda i: (i, 0))
    sq = pl.BlockSpec((S5_WIDTH, S5_WIDTH), lambda i: (0, 0))
    return pl.pallas_call(
        body, name="glu_bwd", grid=(n_rows // rows,),
        in_specs=[row, row, row, row, pl.BlockSpec((1, S5_WIDTH), lambda i: (0, 0)), sq],
        out_specs=[row, sq, pl.BlockSpec((SUBLANES, S5_WIDTH), lambda i: (0, 0))],
        out_shape=[_sds((n_rows, S5_WIDTH)), _sds((S5_WIDTH, S5_WIDTH)), _sds((SUBLANES, S5_WIDTH))],
        compiler_params=_params(("arbitrary",)),
    )(d_ycat, z_all, y0, y1, d_skip, w_glu)


CONV_HALF = CONV_K // 2


def _conv_block(n_rows):
    blk = min(1024, n_rows)
    assert blk >= CONV_HALF * GRID_W and n_rows % blk == 0
    return blk


def _conv_gate(z_all, n_rows):
    blk = _conv_block(n_rows)
    nb = n_rows // blk

    def body(v_ref, g_ref, o_ref):
        i = pl.program_id(0)
        inside = jnp.logical_and(i >= 1, i <= nb)

        @pl.when(inside)
        def _():
            o_ref[...] = v_ref[...] * _sigmoid(g_ref[...])

        @pl.when(jnp.logical_not(inside))
        def _():
            o_ref[...] = jnp.zeros_like(o_ref)

    src = lambda col: pl.BlockSpec((blk, CONV_WIDTH), lambda i: (jnp.clip(i - 1, 0, nb - 1), col))
    return pl.pallas_call(
        body, name="conv_gate", grid=(nb + 2,), in_specs=[src(1), src(2)],
        out_specs=pl.BlockSpec((blk, CONV_WIDTH), lambda i: (i, 0)),
        out_shape=_sds(((nb + 2) * blk, CONV_WIDTH)), compiler_params=_params(("parallel",)),
    )(z_all, z_all)


def _stream_padded(pad_ref, buf, sems, blk, n_blocks):
    i = pl.program_id(0)

    def copy(b):
        rows = pl.ds(pl.multiple_of(b * blk, blk), blk)
        return pltpu.make_async_copy(pad_ref.at[rows, :], buf.at[rows, :], sems.at[b])

    @pl.when(i == 0)
    def _():
        for b in range(n_blocks):
            copy(b).start()
        copy(0).wait()
        copy(1).wait()

    copy(i + 2).wait()
    return pl.multiple_of(i * blk, blk)


def _conv_fwd(hh_pad, w, b, ln_g, ln_b, ycat, n_rows):
    blk = _conv_block(n_rows)
    nblk = n_rows // blk + 2

    def body(hh_ref, w_ref, b_ref, g_ref, lb_ref, ycat_ref, hc_ref, y_ref, win, sems):
        base = _stream_padded(hh_ref, win, sems, blk, nblk)

        def tile(t, _):
            r0 = pl.multiple_of(t * CONV_ROWS, CONV_ROWS)
            acc = jnp.zeros((CONV_ROWS, CONV_WIDTH), F32)
            for k in range(CONV_K):
                acc = acc + w_ref[k:k + 1, :] * win[pl.ds(base + r0 + blk + (k - CONV_HALF) * GRID_W, CONV_ROWS), :]
            hc = acc + b_ref[...]
            hc_ref[pl.ds(r0, CONV_ROWS), :] = hc
            mu = jnp.mean(hc, axis=-1, keepdims=True)
            xc = hc - mu
            ln = xc * lax.rsqrt(jnp.mean(xc * xc, axis=-1, keepdims=True) + EPS_LN) * g_ref[...] + lb_ref[...]
            y_ref[pl.ds(r0, CONV_ROWS), :] = _silu(ln).astype(y_ref.dtype)
            return 0

        lax.fori_loop(0, blk // CONV_ROWS, tile, 0)

    vec = pl.BlockSpec((1, CONV_WIDTH), lambda i: (0, 0))
    row = pl.BlockSpec((blk, CONV_WIDTH), lambda i: (i, 0))
    return pl.pallas_call(
        body, name="conv_fwd", grid=(n_rows // blk,),
        in_specs=[ANY, pl.BlockSpec((CONV_K, CONV_WIDTH), lambda i: (0, 0)), vec, vec, vec, ANY],
        out_specs=[row, pl.BlockSpec((blk, CONV_WIDTH), lambda i: (i, 1))],
        out_shape=[_sds((n_rows, CONV_WIDTH)), _sds(ycat.shape, ycat.dtype)], input_output_aliases={5: 1},
        scratch_shapes=[pltpu.VMEM((nblk * blk, CONV_WIDTH), F32), pltpu.SemaphoreType.DMA((nblk,))],
        compiler_params=_params(("arbitrary",)),
    )(hh_pad, w, b, ln_g, ln_b, ycat)


def _conv_bwd_norm(d_ycat, hc, ln_g, ln_b, n_rows):
    blk = _conv_block(n_rows)
    nb = n_rows // blk

    def body(dy_ref, hc_ref, g_ref, lb_ref, o_ref, sums):
        i = pl.program_id(0)

        @pl.when(i == 0)
        def _():
            sums[...] = jnp.zeros_like(sums)

        inside = jnp.logical_and(i >= 1, i <= nb)

        @pl.when(inside)
        def _():
            hcv = hc_ref[...]
            mu = jnp.mean(hcv, axis=-1, keepdims=True)
            xc = hcv - mu
            rstd = lax.rsqrt(jnp.mean(xc * xc, axis=-1, keepdims=True) + EPS_LN)
            xh = xc * rstd
            g = g_ref[...]
            dln = dy_ref[...] * _dsilu(xh * g + lb_ref[...])
            dxh = dln * g
            dhc = rstd * (dxh - jnp.mean(dxh, axis=-1, keepdims=True) - xh * jnp.mean(dxh * xh, axis=-1, keepdims=True))
            o_ref[...] = dhc
            sums[0] += _fold8(dhc)
            sums[1] += _fold8(dln * xh)
            sums[2] += _fold8(dln)

        @pl.when(jnp.logical_not(inside))
        def _():
            o_ref[...] = jnp.zeros_like(o_ref)

    vec = pl.BlockSpec((1, CONV_WIDTH), lambda i: (0, 0))
    return pl.pallas_call(
        body, name="conv_bwd_norm", grid=(nb + 2,),
        in_specs=[pl.BlockSpec((blk, CONV_WIDTH), lambda i: (jnp.clip(i - 1, 0, nb - 1), 1)),
                  pl.BlockSpec((blk, CONV_WIDTH), lambda i: (jnp.clip(i - 1, 0, nb - 1), 0)), vec, vec],
        out_specs=[pl.BlockSpec((blk, CONV_WIDTH), lambda i: (i, 0)),
                   pl.BlockSpec((3, SUBLANES, CONV_WIDTH), lambda i: (0, 0, 0))],
        out_shape=[_sds(((nb + 2) * blk, CONV_WIDTH)), _sds((3, SUBLANES, CONV_WIDTH))],
        compiler_params=_params(("arbitrary",)),
    )(d_ycat, hc, ln_g, ln_b)


def _conv_bwd_taps(dhc_pad, hh_pad, z_all, w, n_rows):
    blk = _conv_block(n_rows)
    nblk = n_rows // blk + 2

    def body(dhc_ref, hh_ref, v_ref, g_ref, w_ref, dv_ref, dg_ref, dw_ref, dwin, hwin, dsems, hsems):
        @pl.when(pl.program_id(0) == 0)
        def _():
            dw_ref[...] = jnp.zeros_like(dw_ref)

        base = _stream_padded(dhc_ref, dwin, dsems, blk, nblk)
        _stream_padded(hh_ref, hwin, hsems, blk, nblk)

        def tile(t, _):
            r0 = pl.multiple_of(t * CONV_BWD_ROWS, CONV_BWD_ROWS) + base
            dh = dwin[pl.ds(r0 + blk, CONV_BWD_ROWS), :]
            acc = jnp.zeros((CONV_BWD_ROWS, CONV_WIDTH), F32)
            for k in range(CONV_K):
                off = (k - CONV_HALF) * GRID_W
                acc = acc + w_ref[k:k + 1, :] * dwin[pl.ds(r0 + blk - off, CONV_BWD_ROWS), :]
                dw_ref[k] += _fold8(dh * hwin[pl.ds(r0 + blk + off, CONV_BWD_ROWS), :])
            rs = pl.ds(pl.multiple_of(t * CONV_BWD_ROWS, CONV_BWD_ROWS), CONV_BWD_ROWS)
            sg = _sigmoid(g_ref[rs, :])
            vv = v_ref[rs, :]
            dv_ref[rs, :] = (acc * sg).astype(dv_ref.dtype)
            dg_ref[rs, :] = (acc * vv * sg * (1.0 - sg)).astype(dg_ref.dtype)
            return 0

        lax.fori_loop(0, blk // CONV_BWD_ROWS, tile, 0)

    row = pl.BlockSpec((blk, CONV_WIDTH), lambda i: (i, 0))
    return pl.pallas_call(
        body, name="conv_bwd_taps", grid=(n_rows // blk,),
        in_specs=[ANY, ANY,
            pl.BlockSpec((blk, CONV_WIDTH), lambda i: (i, 1)), pl.BlockSpec((blk, CONV_WIDTH), lambda i: (i, 2)),
            pl.BlockSpec((CONV_K, CONV_WIDTH), lambda i: (0, 0))],
        out_specs=[row, row, pl.BlockSpec((CONV_K, SUBLANES, CONV_WIDTH), lambda i: (0, 0, 0))],
        out_shape=[_sds((n_rows, CONV_WIDTH), BF16), _sds((n_rows, CONV_WIDTH), BF16),
                   _sds((CONV_K, SUBLANES, CONV_WIDTH))],
        scratch_shapes=[pltpu.VMEM((nblk * blk, CONV_WIDTH), F32), pltpu.VMEM((nblk * blk, CONV_WIDTH), F32),
                        pltpu.SemaphoreType.DMA((nblk,)), pltpu.SemaphoreType.DMA((nblk,))],
        compiler_params=_params(("arbitrary",)),
    )(dhc_pad, hh_pad, z_all, z_all, w)


def _sum_parts(parts):
    _, r, c = parts.shape

    def body(p_ref, o_ref):
        acc = p_ref[0]
        for q in range(1, NDEV):
            acc = acc + p_ref[q]
        o_ref[...] = acc

    return pl.pallas_call(body, name="sum_parts", out_shape=_sds((r, c)), compiler_params=_params())(parts)


def _row_tile(r, c):
    best = r
    for t in (1024, 512, 256, 128, 64, 32, 16, 8):
        if r % t == 0 and t * c <= 128 * 1024:
            return t
    return best


def _adamw(name, w, gparts, m, v):
    r, c = w.shape
    np_ = gparts.shape[0]
    tr = _row_tile(r, c)

    def body(w_ref, g_ref, m_ref, v_ref, go_ref, d_ref, mo_ref, vo_ref):
        g = g_ref[0].astype(F32)
        for q in range(1, np_):
            g = g + g_ref[q].astype(F32)
        m2 = ADAM_B1 * m_ref[...] + (1.0 - ADAM_B1) * g
        v2 = ADAM_B2 * v_ref[...] + (1.0 - ADAM_B2) * jnp.square(g)
        m_hat = m2 / (1.0 - ADAM_B1 ** ADAM_STEP)
        v_hat = v2 / (1.0 - ADAM_B2 ** ADAM_STEP)
        go_ref[...] = g
        d_ref[...] = -ADAM_LR * (m_hat / (jnp.sqrt(v_hat) + ADAM_EPS) + ADAM_WD * w_ref[...])
        mo_ref[...] = m2
        vo_ref[...] = v2

    row = pl.BlockSpec((tr, c), lambda i: (i, 0))
    return pl.pallas_call(
        body, name=name, grid=(r // tr,),
        in_specs=[row, pl.BlockSpec((np_, tr, c), lambda i: (0, i, 0)), row, row],
        out_specs=[row] * 4, out_shape=[_sds((r, c))] * 4, compiler_params=_params(("parallel",)),
    )(w, gparts, m, v)


def _adamw_native(name, w, g, m, v):
    def body(w_ref, g_ref, m_ref, v_ref, d_ref, mo_ref, vo_ref):
        gv = g_ref[...]
        m2 = ADAM_B1 * m_ref[...] + (1.0 - ADAM_B1) * gv
        v2 = ADAM_B2 * v_ref[...] + (1.0 - ADAM_B2) * jnp.square(gv)
        m_hat = m2 / (1.0 - ADAM_B1 ** ADAM_STEP)
        v_hat = v2 / (1.0 - ADAM_B2 ** ADAM_STEP)
        d_ref[...] = -ADAM_LR * (m_hat / (jnp.sqrt(v_hat) + ADAM_EPS) + ADAM_WD * w_ref[...])
        mo_ref[...] = m2
        vo_ref[...] = v2

    return pl.pallas_call(body, name=name, out_shape=[_sds(w.shape)] * 3, compiler_params=_params())(w, g, m, v)


SMALL = ["c_ctx", "ada_b", "norm1_g", "s5_lam_re", "s5_lam_im", "s5_log_dt", "s5_d", "conv_b", "conv_ln_g", "conv_ln_b",
         "norm2_g", "final_g"]
SMALL_PACKED_ROWS = 24


def _pack_rows(parts, rows):
    flat = jnp.concatenate([p.reshape(-1).astype(F32) for p in parts])
    return jnp.pad(flat, (0, rows * D_MODEL - flat.shape[0])).reshape(rows, D_MODEL)


def _unpack_rows(packed, shapes):
    flat = packed.reshape(-1)
    out, off = [], 0
    for shape in shapes:
        size = 1
        for s in shape:
            size *= s
        out.append(flat[off:off + size].reshape(shape))
        off += size
    return out


def kernel(x, c, ctx, c_ctx, ada_w, ada_b, norm1_g, w_in, s5_lam_re, s5_lam_im, s5_log_dt, s5_b_re, s5_b_im, s5_c_re, s5_c_im, s5_d, s5_w_glu, conv_w, conv_b, conv_ln_g, conv_ln_b, w_out, norm2_g, mlp_w1, mlp_w2, final_g, loss_target, m_c_ctx, m_ada_w, m_ada_b, m_norm1_g, m_w_in, m_s5_lam_re, m_s5_lam_im, m_s5_log_dt, m_s5_b_re, m_s5_b_im, m_s5_c_re, m_s5_c_im, m_s5_d, m_s5_w_glu, m_conv_w, m_conv_b, m_conv_ln_g, m_conv_ln_b, m_w_out, m_norm2_g, m_mlp_w1, m_mlp_w2, m_final_g, v_c_ctx, v_ada_w, v_ada_b, v_norm1_g, v_w_in, v_s5_lam_re, v_s5_lam_im, v_s5_log_dt, v_s5_b_re, v_s5_b_im, v_s5_c_re, v_s5_c_im, v_s5_d, v_s5_w_glu, v_conv_w, v_conv_b, v_conv_ln_g, v_conv_ln_b, v_w_out, v_norm2_g, v_mlp_w1, v_mlp_w2, v_final_g):
    weights = dict(c_ctx=c_ctx, ada_w=ada_w, ada_b=ada_b, norm1_g=norm1_g, w_in=w_in, s5_lam_re=s5_lam_re, s5_lam_im=s5_lam_im, s5_log_dt=s5_log_dt, s5_b_re=s5_b_re, s5_b_im=s5_b_im, s5_c_re=s5_c_re, s5_c_im=s5_c_im, s5_d=s5_d, s5_w_glu=s5_w_glu, conv_w=conv_w, conv_b=conv_b, conv_ln_g=conv_ln_g, conv_ln_b=conv_ln_b, w_out=w_out, norm2_g=norm2_g, mlp_w1=mlp_w1, mlp_w2=mlp_w2, final_g=final_g)
    mom1 = dict(c_ctx=m_c_ctx, ada_w=m_ada_w, ada_b=m_ada_b, norm1_g=m_norm1_g, w_in=m_w_in, s5_lam_re=m_s5_lam_re, s5_lam_im=m_s5_lam_im, s5_log_dt=m_s5_log_dt, s5_b_re=m_s5_b_re, s5_b_im=m_s5_b_im, s5_c_re=m_s5_c_re, s5_c_im=m_s5_c_im, s5_d=m_s5_d, s5_w_glu=m_s5_w_glu, conv_w=m_conv_w, conv_b=m_conv_b, conv_ln_g=m_conv_ln_g, conv_ln_b=m_conv_ln_b, w_out=m_w_out, norm2_g=m_norm2_g, mlp_w1=m_mlp_w1, mlp_w2=m_mlp_w2, final_g=m_final_g)
    mom2 = dict(c_ctx=v_c_ctx, ada_w=v_ada_w, ada_b=v_ada_b, norm1_g=v_norm1_g, w_in=v_w_in, s5_lam_re=v_s5_lam_re, s5_lam_im=v_s5_lam_im, s5_log_dt=v_s5_log_dt, s5_b_re=v_s5_b_re, s5_b_im=v_s5_b_im, s5_c_re=v_s5_c_re, s5_c_im=v_s5_c_im, s5_d=v_s5_d, s5_w_glu=v_s5_w_glu, conv_w=v_conv_w, conv_b=v_conv_b, conv_ln_g=v_conv_ln_g, conv_ln_b=v_conv_ln_b, w_out=v_w_out, norm2_g=v_norm2_g, mlp_w1=v_mlp_w1, mlp_w2=v_mlp_w2, final_g=v_final_g)
    order = list(weights)

    me = 4 * lax.axis_index("x") + 2 * lax.axis_index("y") + lax.axis_index("c")
    xs, cs, tgt = x[0], ctx[0], loss_target[0]
    n_lat_rows, n_ctx_rows = xs.shape[0], cs.shape[0]
    n_rows = n_lat_rows + n_ctx_rows
    n_lat = n_lat_rows // ROW_BLOCK
    ada_cols = ada_w.shape[2]

    (c_all,), _ = _exchange("gather_c", [c], [True])
    c_all = c_all.reshape(NDEV, D_MODEL)

    cond_fwd = jnp.concatenate([c_all, c_ctx[None], jnp.zeros((7, D_MODEL), F32)])
    ada_b_loc = lax.dynamic_slice(ada_b, (0, me * ada_cols), (1, ada_cols))
    (mod_g,), mod_token = _exchange("gather_mod", [_ada_fwd(cond_fwd, ada_w[0], ada_b_loc)], [True])
    weight_groups, weights_token = _exchange_start_groups("gather_weights_start", [
        ([w_in[0].astype(BF16)], [True]),
        ([s5_w_glu[0].astype(BF16), conv_w[0] + mod_token[0:1, 0:1], w_out[0].astype(BF16)], [True] * 3),
        ([mlp_w1[0].astype(BF16), mlp_w2[0].astype(BF16)], [True] * 2)])
    (wi_send, wi_recv, wi_src, wi_land), (mixer_send, mixer_recv, mixer_src, mixer_land), \
        (mlpw_send, mlpw_recv, mlpw_src, mlpw_land) = weight_groups
    mod_rows = jnp.transpose(mod_g, (1, 0, 2)).reshape(16, 6 * D_MODEL) + weights_token[0:1, 0:1]
    mod = lax.dynamic_slice(mod_rows, (me, 0), (1, 6 * D_MODEL)).reshape(6, D_MODEL)
    modc = mod_rows[8, :2 * D_MODEL].reshape(2, D_MODEL)
    sh1, sc1, g1, sh2, sc2, g2 = [mod[i:i + 1] for i in range(6)]

    lam_re, lam_im = s5_lam_re[0].reshape(2, 1, NSTATE), s5_lam_im[0].reshape(2, 1, NSTATE)
    ldt = jnp.repeat(s5_log_dt[0], S5_STATE, axis=-1).reshape(2, 1, NSTATE)
    bt_re = jnp.transpose(s5_b_re[0], (0, 3, 1, 2)).reshape(2, S5_GROUP, NSTATE)
    bt_im = jnp.transpose(s5_b_im[0], (0, 3, 1, 2)).reshape(2, S5_GROUP, NSTATE)
    groups_per_block = S5_GROUPS // S5_BLOCKS
    ct_re = jnp.tile(s5_c_re[0].reshape(2, S5_WIDTH, S5_STATE), (1, 1, groups_per_block))
    ct_im = jnp.tile(s5_c_im[0].reshape(2, S5_WIDTH, S5_STATE), (1, 1, groups_per_block))
    d_skip = s5_d[0].reshape(1, S5_WIDTH)
    perms = [_segment_permutation(reverse_time=(d == 0)) for d in range(2)]
    perms_t = [p.T for p in perms]
    disc = [_s5_discretise(f"s5_disc{d}", False, lam_re[d], lam_im[d], ldt[d], bt_re[d], bt_im[d], ct_re[d], ct_im[d])
            for d in range(2)]

    a_all = _prenorm("prenorm1", xs, cs, norm1_g, jnp.stack([mod[0:2], modc]))
    before_w_in = a_all[0:SUBLANES, 0:LANES].astype(F32) + disc[0][0][0:SUBLANES, 0:LANES] + disc[1][0][0:SUBLANES, 0:LANES]
    wi_own, wi_landed = _exchange_wait("gather_w_in_wait", wi_send, wi_recv, wi_src, wi_land, [True], before_w_in)
    w_in_full = jnp.transpose(_with_own(wi_landed[0], wi_own[0], me), (1, 0, 2)).reshape(D_MODEL, IN_COLS)
    tm_all = 1088 if n_rows % 1088 == 0 else ROW_BLOCK
    (z_all,) = _matmul("in_proj", a_all, w_in_full, "nn", (n_rows, IN_COLS, D_MODEL), (tm_all, IN_COLS, D_MODEL),
                       [((n_rows, IN_COLS), F32)])

    _, tab, _, bmat, cmat = disc[0]
    s0, y0 = _s5_scan_fwd("s5_scan_fwd0", True, z_all, bmat, cmat, tab, perms[0], perms_t[0])
    mixer_own, mixer_landed = _exchange_wait("gather_mixer_wait", mixer_send, mixer_recv, mixer_src, mixer_land,
                                             [True] * 3, y0)
    glu_g, conv_w_g, w_out_g = [_with_own(l, o, me) for l, o in zip(mixer_landed, mixer_own)]
    glu_full = glu_g.reshape(S5_WIDTH, S5_WIDTH)
    conv_w_full = jnp.transpose(conv_w_g, (1, 0, 2)).reshape(CONV_K, CONV_WIDTH)
    w_out_full = w_out_g.reshape(D_MODEL, D_MODEL)
    _, tab, _, bmat, cmat = disc[1]
    s1, y1, ycat = _s5_scan_fwd("s5_scan_fwd1", False, z_all, bmat, cmat, tab, perms[1], perms_t[1],
                                y_other=y0, d_skip=d_skip, w_glu=glu_full)
    states, y_dir = [s0, s1], [y0, y1]

    hh_pad = _conv_gate(z_all, n_lat_rows)
    hc, ycat = _conv_fwd(hh_pad, conv_w_full, conv_b, conv_ln_g, conv_ln_b, ycat, n_lat_rows)

    tm = min(1024, n_lat_rows)
    tm_e = min(512, n_lat_rows)
    w1_cols = D_FF // NDEV
    row_vec = lambda tn: pl.BlockSpec((1, tn), lambda i, j, k: (0, j))
    out_tile = lambda t_m, t_n: pl.BlockSpec((t_m, t_n), lambda i, j, k: (i, j))
    full_rows = ((n_lat_rows, D_MODEL), F32)
    sums = ((n_lat_rows // tm_e, SUBLANES, D_MODEL), F32)
    sums_spec = pl.BlockSpec((None, SUBLANES, D_MODEL), lambda i, j, k: (i, 0, 0))
    vec = lambda v: (v, row_vec(D_MODEL))
    transposed_tile = lambda t_m, t_n: pl.BlockSpec((t_n, t_m), lambda i, j, k: (j, i))
    mix, h1, a2, a2_t = _matmul(
        "out_proj", ycat, w_out_full, "nn", (n_lat_rows, D_MODEL, D_MODEL), (tm_e, D_MODEL, D_MODEL),
        [full_rows, full_rows, ((n_lat_rows, D_MODEL), BF16), ((D_MODEL, n_lat_rows), BF16)],
        epi=_epi_residual_prenorm,
        epi_extra=[(xs, out_tile(tm_e, D_MODEL)), vec(g1), vec(norm2_g), vec(sc2), vec(sh2)],
        out_specs=[out_tile(tm_e, D_MODEL)] * 3 + [transposed_tile(tm_e, D_MODEL)])
    mlpw_own, mlpw_landed = _exchange_wait("gather_mlp_wait", mlpw_send, mlpw_recv, mlpw_src, mlpw_land, [True] * 2, a2)
    w1_g, w2_g = [_with_own(l, o, me) for l, o in zip(mlpw_landed, mlpw_own)]
    w2_full = w2_g.reshape(D_FF, D_MODEL)
    tm_up = min(2048, n_lat_rows)
    f, f_t = _matmul("mlp_up", a2, w1_g, "nn", (n_lat_rows, D_FF, D_MODEL), (tm_up, w1_cols, D_MODEL),
                     [((n_lat_rows, D_FF), BF16), ((D_FF, n_lat_rows), BF16)], epi=lambda acc: (acc, acc.T),
                     b_spec=pl.BlockSpec((None, D_MODEL, w1_cols), lambda i, j, k: (j, 0, 0)),
                     out_specs=[out_tile(tm_up, w1_cols), transposed_tile(tm_up, w1_cols)])
    sq_relu = lambda t: jnp.square(jnp.maximum(t, 0.0))
    mlp_out, d_h2, dm2, err_sums, d_final_g8 = _matmul(
        "mlp_down", f, w2_full, "nn", (n_lat_rows, D_MODEL, D_FF), (tm_e, D_MODEL, 2048),
        [full_rows, full_rows, ((n_lat_rows, D_MODEL), BF16), sums, sums], a_fn=sq_relu, epi=_epi_residual_loss,
        epi_extra=[(h1, out_tile(tm_e, D_MODEL)), vec(g2), (tgt, out_tile(tm_e, D_MODEL)), vec(final_g[None])],
        out_specs=[out_tile(tm_e, D_MODEL)] * 3 + [sums_spec] * 2)

    (d_f,) = _matmul("mlp_down_dx", dm2, w2_full, "nt", (n_lat_rows, D_FF, D_MODEL), (tm_up, 512, D_MODEL),
                     [((n_lat_rows, D_FF), BF16)],
                     epi=lambda acc, ft: (acc * 2.0 * jnp.maximum(ft.astype(F32), 0.0),),
                     epi_extra=[(f, out_tile(tm_up, 512))])
    tk_dw = min(2048, n_lat_rows)
    (g_w2,) = _matmul("mlp_down_dw", f_t, dm2, "nn", (D_FF, D_MODEL, n_lat_rows), (1024, D_MODEL, tk_dw),
                      [((D_FF, D_MODEL), F32)], a_fn=sq_relu)
    (g_w1,) = _matmul("mlp_up_dw", a2_t, d_f, "nn", (D_MODEL, D_FF, n_lat_rows), (D_MODEL, w1_cols, n_lat_rows),
                      [((NDEV, D_MODEL, w1_cols), F32)],
                      out_specs=[pl.BlockSpec((None, D_MODEL, w1_cols), lambda i, j, k: (j, 0, 0))])
    mlp_send, mlp_recv, mlp_src, mlp_land, mlp_token = _exchange_start(
        "scatter_mlp_start", [g_w1, g_w2.reshape(NDEV, D_FF // NDEV, D_MODEL)], [False] * 2)
    d_h1, dm1, *sums2 = _matmul(
        "mlp_up_dx", d_f, w1_g, "nt", (n_lat_rows, D_MODEL, D_FF), (tm_e, D_MODEL, 4 * w1_cols),
        [full_rows, ((n_lat_rows, D_MODEL), BF16)] + [sums] * 4, epi=_epi_norm_bwd,
        epi_extra=[(h1, out_tile(tm_e, D_MODEL)), (d_h2, out_tile(tm_e, D_MODEL)), (mlp_out, out_tile(tm_e, D_MODEL)),
                   vec(norm2_g), vec(sc2 + mlp_token[0:1, 0:1]), vec(g1)],
        b_spec=pl.BlockSpec((4, D_MODEL, w1_cols), lambda i, j, k: (k, 0, 0)), b_slabs=4,
        out_specs=[out_tile(tm_e, D_MODEL)] * 2 + [sums_spec] * 4)

    (d_ycat,) = _matmul("out_proj_dx", dm1, w_out_full, "nt", (n_lat_rows, D_MODEL, D_MODEL), (tm, D_MODEL, D_MODEL),
                        [((n_lat_rows, D_MODEL), F32)])
    (g_w_out,) = _matmul("out_proj_dw", ycat, dm1, "tn", (D_MODEL, D_MODEL, n_lat_rows), (D_MODEL, D_MODEL, 512),
                         [((D_MODEL, D_MODEL), F32)])

    dy, g_glu, dd8 = _glu_bwd(d_ycat, z_all, y_dir[0], y_dir[1], d_skip, glu_full, n_lat_rows)
    proj_send, proj_recv, proj_src, proj_land, proj_token = _exchange_start(
        "scatter_proj_start",
        [g_w_out.reshape(NDEV, D_MODEL // NDEV, D_MODEL), g_glu.reshape(NDEV, S5_WIDTH // NDEV, S5_WIDTH)], [False] * 2)
    perms = [p + proj_token[0:1, 0:1].astype(BF16) for p in perms]
    du, g_lam_re, g_lam_im, g_ldt, g_bt, g_cdiag = None, [], [], [], [], []
    for d in range(2):
        _, _, adj, bmat, cmat = disc[d]
        du, d_bdiag, d_cdiag, d_abar8 = _s5_scan_bwd(f"s5_scan_bwd{d}", d == 0, dy, z_all, states[d], bmat, cmat, adj,
                                                     perms[d], perms_t[d], du_other=du, d_skip=d_skip if d else None)
        d_bbar = jnp.transpose(d_bdiag.reshape(S5_BLOCKS, S5_GROUP, 2, NSTATE // S5_BLOCKS), (2, 1, 0, 3)).reshape(
            2 * S5_GROUP, NSTATE)
        d_lam8, d_bt = _s5_discretise_bwd(f"s5_disc_bwd{d}", lam_re[d], lam_im[d], ldt[d], bt_re[d], bt_im[d], d_abar8, d_bbar)
        g_lam_re.append(d_lam8[0].reshape(S5_GROUPS, S5_STATE))
        g_lam_im.append(d_lam8[1].reshape(S5_GROUPS, S5_STATE))
        g_ldt.append(d_lam8[2].reshape(S5_GROUPS, S5_STATE).sum(axis=-1))
        g_bt.append(d_bt)
        g_cdiag.append(d_cdiag)

    dhc_pad, conv_sums = _conv_bwd_norm(d_ycat, hc, conv_ln_g, conv_ln_b, n_lat_rows)
    d_v, d_gate, g_conv_w8 = _conv_bwd_taps(dhc_pad, hh_pad, z_all, conv_w_full, n_lat_rows)

    no_ctx = jnp.zeros((n_ctx_rows, CONV_WIDTH), BF16)
    dz_all = jnp.concatenate([du, jnp.concatenate([d_v, no_ctx]), jnp.concatenate([d_gate, no_ctx])], axis=1)
    (g_w_in_full,) = _matmul("in_proj_dw", a_all, dz_all, "tn", (D_MODEL, IN_COLS, n_rows), (D_MODEL, IN_COLS, tm_all),
                             [((D_MODEL, IN_COLS), F32)])
    g_w_in_parts = jnp.transpose(g_w_in_full.reshape(D_MODEL, NDEV, IN_COLS // NDEV), (1, 0, 2)).astype(BF16)
    win_send, win_recv, win_src, win_land, win_token = _exchange_start("scatter_w_in_start", [g_w_in_parts], [False])
    w_in_late = w_in_full + win_token[0:1, 0:1].astype(BF16)
    grad_x, *sums1 = _matmul(
        "in_proj_dx", dz_all, w_in_late, "nt", (n_lat_rows, D_MODEL, IN_COLS), (tm_e, D_MODEL, IN_COLS),
        [full_rows] + [sums] * 4, epi=_epi_norm_bwd,
        epi_extra=[(xs, out_tile(tm_e, D_MODEL)), (d_h1, out_tile(tm_e, D_MODEL)), (mix, out_tile(tm_e, D_MODEL)),
                   vec(norm1_g), vec(sc1)],
        out_specs=[out_tile(tm_e, D_MODEL)] + [sums_spec] * 4)
    (d_a_ctx,) = _matmul("in_proj_dx_ctx", dz_all, w_in_late, "nt", (n_ctx_rows, D_MODEL, IN_COLS),
                         (ROW_BLOCK, D_MODEL, IN_COLS), [((n_ctx_rows, D_MODEL), F32)],
                         a_spec=pl.BlockSpec((ROW_BLOCK, IN_COLS), lambda i, j, k: (i + n_lat, 0)))
    (sums1c,) = _norm_bwd("norm1_bwd_ctx", cs, d_a_ctx, 0, norm1_g, modc[1:2])

    s1, s1c, s2 = [p.sum(axis=(0, 1)) for p in sums1], sums1c.sum(axis=1), [p.sum(axis=(0, 1)) for p in sums2]
    d_mod = jnp.concatenate([s1[0], s1[1], s1[3], s2[0], s2[1], s2[3]])
    d_modc = jnp.concatenate([s1c[0], s1c[1], jnp.zeros((4 * D_MODEL,), F32)])
    (dmod_g,), _ = _exchange("gather_dmod", [jnp.stack([d_mod, d_modc])], [True])
    dmod16 = jnp.concatenate([dmod_g[:, 0], dmod_g[:, 1]])
    dmod16_loc = lax.dynamic_slice(dmod16, (0, me * ada_cols), (16, ada_cols))
    cond_bwd = jnp.concatenate([c_all, jnp.broadcast_to(c_ctx[None], (NDEV, D_MODEL))])
    g_ada_w, g_c_ctx8 = _ada_bwd(cond_bwd, dmod16_loc, ada_w[0], c_ctx[None])

    small_parts = dict(
        c_ctx=g_c_ctx8[0], ada_b=d_mod + d_modc, norm1_g=s1[2] + s1c[2],
        s5_lam_re=jnp.stack(g_lam_re), s5_lam_im=jnp.stack(g_lam_im), s5_log_dt=jnp.stack(g_ldt),
        s5_d=dd8.sum(axis=0), conv_b=conv_sums[0].sum(axis=0), conv_ln_g=conv_sums[1].sum(axis=0),
        conv_ln_b=conv_sums[2].sum(axis=0), norm2_g=s2[2], final_g=d_final_g8.sum(axis=(0, 1)))
    reduced_shapes = [(SMALL_PACKED_ROWS, D_MODEL), (2, 2 * S5_GROUP, NSTATE), (2,) + _S5_DIAG, (1,)]
    small_g = _pack_rows(
        [_pack_rows([small_parts[n] for n in SMALL], SMALL_PACKED_ROWS), jnp.stack(g_bt), jnp.stack(g_cdiag),
         (0.5 / D_MODEL * jnp.sum(err_sums)).reshape(1)], SMALL_ROWS).reshape(NDEV, SMALL_ROWS // NDEV, D_MODEL)
    g_conv_w_parts = jnp.transpose(g_conv_w8.sum(axis=1).reshape(CONV_K, NDEV, CONV_WIDTH // NDEV), (1, 0, 2))

    res = {}

    def own_chunk(src):
        return lax.dynamic_index_in_dim(src, me, 0, keepdims=False)

    def adamw_big(name, parts):
        outs = _adamw("adamw_" + name, weights[name][0], parts, mom1[name][0], mom2[name][0])
        res[name] = [o[None] for o in outs]
        return outs[0]

    sm_send, sm_recv, sm_src, sm_land, sm_token = _exchange_start("scatter_small_start", [g_conv_w_parts, small_g],
                                                                  [False] * 2)
    mlp_src, mlp_landed = _exchange_wait("scatter_mlp_wait", mlp_send, mlp_recv, mlp_src, mlp_land, [False] * 2, sm_token)
    p_w1, p_w2 = [_with_own(l, own_chunk(s), me) for l, s in zip(mlp_landed, mlp_src)]
    adamw_big("ada_w", g_ada_w[None])
    adamw_big("mlp_w1", p_w1)
    done = adamw_big("mlp_w2", p_w2)
    sm_src, sm_landed = _exchange_wait("scatter_small_wait", sm_send, sm_recv, sm_src, sm_land, [False] * 2, done)
    p_conv_w, p_small = [_with_own(l, own_chunk(s), me) for l, s in zip(sm_landed, sm_src)]
    ga_send, ga_recv, ga_src, ga_land, ga_token = _exchange_start("gather_small_start", [_sum_parts(p_small)], [True])
    proj_src, proj_landed = _exchange_wait("scatter_proj_wait", proj_send, proj_recv, proj_src, proj_land, [False] * 2,
                                           ga_token)
    p_w_out, p_glu = [_with_own(l, own_chunk(s), me) for l, s in zip(proj_landed, proj_src)]
    adamw_big("w_out", p_w_out)
    done = adamw_big("s5_w_glu", p_glu)
    win_src, win_landed = _exchange_wait("scatter_w_in_wait", win_send, win_recv, win_src, win_land, [False], done)
    adamw_big("w_in", _with_own(win_landed[0], own_chunk(win_src[0]), me))
    done = adamw_big("conv_w", p_conv_w)
    ga_own, ga_landed = _exchange_wait("gather_small_wait", ga_send, ga_recv, ga_src, ga_land, [True], done)
    small_all = _with_own(ga_landed[0], ga_own[0], me).reshape(1, SMALL_ROWS, D_MODEL)
    _, r_bt, r_cdiag, loss = _unpack_rows(small_all, reduced_shapes)
    loss = loss.reshape(())
    pack = lambda src: _pack_rows([src[n] for n in SMALL], SMALL_PACKED_ROWS)
    outs = _adamw("adamw_small", pack(weights), small_all, pack(mom1), pack(mom2))
    unpacked = [_unpack_rows(o, [weights[n].shape for n in SMALL]) for o in outs]
    for i, name in enumerate(SMALL):
        res[name] = [u[i] for u in unpacked]
    to_ghp = lambda t: jnp.transpose(t.reshape(2, S5_GROUP, S5_GROUPS, S5_STATE), (0, 2, 1, 3))[None]
    r_c = jnp.transpose(r_cdiag.reshape(2, S5_BLOCKS, S5_GROUP, 2, groups_per_block, S5_STATE), (3, 0, 1, 4, 2, 5)).reshape(
        2, 1, 2, S5_GROUPS, S5_GROUP, S5_STATE)
    swap = lambda t: jnp.swapaxes(t, -1, -2)
    for name, grad in (("s5_b_re", to_ghp(r_bt[:, :S5_GROUP])), ("s5_b_im", to_ghp(r_bt[:, S5_GROUP:]))):
        outs = _adamw_native("adamw_" + name, swap(weights[name]), grad, swap(mom1[name]), swap(mom2[name]))
        res[name] = [swap(grad), *[swap(o) for o in outs]]
    for name, grad in (("s5_c_re", r_c[0]), ("s5_c_im", -r_c[1])):
        res[name] = [grad, *_adamw_native("adamw_" + name, weights[name], grad, mom1[name], mom2[name])]

    return (loss, grad_x[None], *[res[n][0] for n in order], *[res[n][1] for n in order],
            *[res[n][2] for n in order], *[res[n][3] for n in order])
```

```python
import jax
import jax.numpy as jnp
from jax import lax
from jax.experimental import pallas as pl
from jax.experimental.pallas import tpu as pltpu

F32 = jnp.float32
BF16 = jnp.bfloat16
MESH = pl.DeviceIdType.MESH
ANY = pl.BlockSpec(memory_space=pl.ANY)

NDEV = 8
D_MODEL = 1024
GRID_W = 64
S5_WIDTH = 512
S5_GROUP = 16
S5_GROUPS = 32
S5_STATE = 64
NSTATE = S5_GROUPS * S5_STATE
CONV_WIDTH = 512
CONV_K = 31
IN_COLS = S5_WIDTH + 2 * CONV_WIDTH
D_FF = 4 * D_MODEL
EPS_RMS = 1e-6
EPS_LN = 1e-5
ADAM_LR = 0.001
ADAM_B1 = 0.9
ADAM_B2 = 0.999
ADAM_EPS = 1e-08
ADAM_WD = 0.01
ADAM_STEP = 10

SUBLANES = 8
LANES = 128
ROW_BLOCK = 256
SCAN_LANES = 512
SCAN_UNROLL = 32
SEGMENTS = SUBLANES
STEPS = ROW_BLOCK // SEGMENTS
S5_BLOCKS = 4
S5_BLOCK_WIDTH = S5_WIDTH // S5_BLOCKS
CONV_ROWS = 64
CONV_BWD_ROWS = 32
VMEM_LIMIT = 48 * 1024 * 1024
SMALL_ROWS = 320


def _params(sem=None):
    kw = dict(vmem_limit_bytes=VMEM_LIMIT)
    if sem is not None:
        kw["dimension_semantics"] = sem
    return pltpu.CompilerParams(**kw)


def _sds(shape, dtype=F32):
    return jax.ShapeDtypeStruct(tuple(shape), dtype)


def _fold8(x):
    return x.reshape(x.shape[0] // SUBLANES, SUBLANES, x.shape[1]).sum(axis=0)


def _sigmoid(x):
    return 1.0 / (1.0 + jnp.exp(-x))


def _silu(x):
    return x * _sigmoid(x)


def _dsilu(x):
    s = _sigmoid(x)
    return s * (1.0 + x * (1.0 - s))


_GELU_C = 0.7978845608028654


def _gelu(x):
    return 0.5 * x * (1.0 + jnp.tanh(_GELU_C * (x + 0.044715 * x * x * x)))


def _dgelu(x):
    t = jnp.tanh(_GELU_C * (x + 0.044715 * x * x * x))
    return 0.5 * (1.0 + t) + 0.5 * x * (1.0 - t * t) * _GELU_C * (1.0 + 3.0 * 0.044715 * x * x)


def _rms(x):
    rstd = lax.rsqrt(jnp.mean(x * x, axis=-1, keepdims=True) + EPS_RMS)
    return x * rstd, rstd


def _epi_residual_prenorm(acc, res, gate, gain, scale, shift):
    h = res + gate * acc
    xh, _ = _rms(h)
    a = (xh * gain) * (1.0 + scale) + shift
    return acc, h, a, a.T


def _epi_residual_loss(acc, res, gate, target, gain):
    h = res + gate * acc
    xh, rstd = _rms(h)
    err = xh * gain - target
    dy = err * (1.0 / h.shape[-1])
    dxh = dy * gain
    dh = rstd * (dxh - xh * jnp.mean(dxh * xh, axis=-1, keepdims=True))
    return acc, dh, dh * gate, _fold8(err * err), _fold8(dy * xh)


def _epi_norm_bwd(d_act, x, res, aux, gain, scale, gate=None):
    xh, rstd = _rms(x)
    dn = d_act * (1.0 + scale)
    dxh = dn * gain
    dx = res + rstd * (dxh - xh * jnp.mean(dxh * xh, axis=-1, keepdims=True))
    sums = (_fold8(d_act), _fold8(d_act * (xh * gain)), _fold8(dn * xh), _fold8(res * aux))
    return (dx, *sums) if gate is None else (dx, dx * gate, *sums)


def _dot(a, b, mode):
    dims = {"nn": (((1,), (0,)), ((), ())), "nt": (((1,), (1,)), ((), ())), "tn": (((0,), (0,)), ((), ()))}[mode]
    return lax.dot_general(a, b, dims, preferred_element_type=F32)


def _peers(x, y, c):
    out = []
    for k in range(1, NDEV):
        px = 1 - x if k & 4 else x
        py = 1 - y if k & 2 else y
        pc = 1 - c if k & 1 else c
        out.append(((px, py, pc), 4 * px + 2 * py + pc))
    return out


def _exchange_copies(src, land, send_sems, recv_sems, gather):
    x, y, c = lax.axis_index("x"), lax.axis_index("y"), lax.axis_index("c")
    me = 4 * x + 2 * y + c
    out = []
    for a in range(len(src)):
        for k, (peer, plin) in enumerate(_peers(x, y, c)):
            chunk = src[a] if gather[a] else src[a].at[plin]
            sems = dict(send_sem=send_sems.at[a * (NDEV - 1) + k], recv_sem=recv_sems.at[a * (NDEV - 1) + k],
                        device_id=peer, device_id_type=MESH)
            out.append((pltpu.make_async_remote_copy(src_ref=chunk, dst_ref=land[a].at[me], **sems),
                        pltpu.make_async_remote_copy(src_ref=chunk, dst_ref=land[a].at[plin], **sems)))
    return out


def _exchange(name, srcs, gather):
    n = len(srcs)
    outs = [_sds(((NDEV,) + s.shape) if g else s.shape, s.dtype) for s, g in zip(srcs, gather)]

    def body(*refs):
        src, dst, token = refs[:n], refs[n:2 * n], refs[2 * n]
        send_sems, recv_sems, local_sems = refs[2 * n + 1:]
        me = 4 * lax.axis_index("x") + 2 * lax.axis_index("y") + lax.axis_index("c")
        local = [pltpu.make_async_copy(src[a] if gather[a] else src[a].at[me], dst[a].at[me], local_sems.at[a])
                 for a in range(n)]
        for copy in local:
            copy.start()
        copies = _exchange_copies(src, dst, send_sems, recv_sems, gather)
        for copy, _ in copies:
            copy.start()
        token[...] = jnp.zeros_like(token)
        for copy, landing in copies:
            copy.wait_send()
            landing.wait_recv()
        for copy in local:
            copy.wait()

    nsem = n * (NDEV - 1)
    out = pl.pallas_call(
        body, name=name, out_shape=outs + [_sds((SUBLANES, LANES))], in_specs=[ANY] * n,
        out_specs=[ANY] * n + [pl.BlockSpec(memory_space=pltpu.VMEM)],
        scratch_shapes=[pltpu.SemaphoreType.DMA((nsem,)), pltpu.SemaphoreType.DMA((nsem,)), pltpu.SemaphoreType.DMA((n,))],
    )(*srcs)
    return out[:n], out[n]


HBM = pl.BlockSpec(memory_space=pltpu.HBM)
SEM = pl.BlockSpec(memory_space=pltpu.SEMAPHORE)
EFFECT = pltpu.SideEffectType.DATAFLOW_SIDE_EFFECTING


def _exchange_start_groups(name, groups):
    srcs = [s for g_srcs, _ in groups for s in g_srcs]
    gathers = [g for _, g_gather in groups for g in g_gather]
    lands = [lax.empty(((NDEV,) + s.shape) if g else s.shape, s.dtype) for s, g in zip(srcs, gathers)]
    n, ng = len(srcs), len(groups)

    def body(*refs):
        src, land = refs[:n], refs[n:2 * n]
        sems = refs[2 * n:2 * n + 2 * ng]
        token = refs[-1]
        first = 0
        for g, (g_srcs, g_gather) in enumerate(groups):
            last = first + len(g_srcs)
            for copy, _ in _exchange_copies(src[first:last], land[first:last], sems[2 * g], sems[2 * g + 1], g_gather):
                copy.start()
            first = last
        token[...] = jnp.zeros_like(token)

    hbm = lambda v: pltpu.HBM(v.shape, v.dtype)
    sem_shapes = []
    for g_srcs, _ in groups:
        sem_shapes += [pltpu.SemaphoreType.DMA((len(g_srcs) * (NDEV - 1),))] * 2
    out = pl.pallas_call(
        body, name=name,
        out_shape=(*sem_shapes, *[hbm(v) for v in srcs], *[hbm(v) for v in lands], _sds((SUBLANES, LANES))),
        in_specs=[HBM] * (2 * n),
        out_specs=(*([SEM] * (2 * ng)), *([HBM] * (2 * n)), pl.BlockSpec(memory_space=pltpu.VMEM)),
        input_output_aliases={i: 2 * ng + i for i in range(2 * n)},
        compiler_params=pltpu.CompilerParams(has_side_effects=EFFECT),
    )(*[pltpu.with_memory_space_constraint(v, pltpu.HBM) for v in srcs + lands])
    src_out, land_out = out[2 * ng:2 * ng + n], out[2 * ng + n:2 * ng + 2 * n]
    result, first = [], 0
    for g, (g_srcs, _) in enumerate(groups):
        last = first + len(g_srcs)
        result.append((out[2 * g], out[2 * g + 1], src_out[first:last], land_out[first:last]))
        first = last
    return result, out[-1]


def _exchange_start(name, srcs, gather):
    (group,), token = _exchange_start_groups(name, [(srcs, gather)])
    return (*group, token)


def _exchange_wait(name, send_sems, recv_sems, srcs, lands, gather, after):
    n = len(srcs)

    def body(*refs):
        src, land = refs[:n], refs[n:2 * n]
        send_ref, recv_ref = refs[2 * n], refs[2 * n + 1]
        for copy, landing in _exchange_copies(src, land, send_ref, recv_ref, gather):
            copy.wait_send()
            landing.wait_recv()

    hbm = lambda v: pltpu.HBM(v.shape, v.dtype)
    out = pl.pallas_call(
        body, name=name, out_shape=[hbm(v) for v in list(srcs) + list(lands)],
        in_specs=[HBM] * (2 * n) + [SEM, SEM, ANY], out_specs=[HBM] * (2 * n),
        input_output_aliases={i: i for i in range(2 * n)},
        compiler_params=pltpu.CompilerParams(has_side_effects=EFFECT),
    )(*srcs, *lands, send_sems, recv_sems, after)
    return out[:n], out[n:]


def _with_own(landed, own, me):
    return lax.dynamic_update_slice(landed, own[None], (me,) + (0,) * own.ndim)


def _matmul(name, a, b, mode, mnk, tiles, outs, a_spec=None, b_spec=None, a_fn=None, a_extra=(),
            epi=None, epi_extra=(), out_specs=None, b_slabs=1):
    m_, n_, k_ = mnk
    tm, tn, tk = tiles
    nk = k_ // tk
    if a_spec is None:
        a_spec = (pl.BlockSpec((tk, tm), lambda i, j, k: (k, i)) if mode == "tn"
                  else pl.BlockSpec((tm, tk), lambda i, j, k: (i, k)))
    if b_spec is None:
        b_spec = (pl.BlockSpec((tn, tk), lambda i, j, k: (j, k)) if mode == "nt"
                  else pl.BlockSpec((tk, tn), lambda i, j, k: (k, j)))
    if out_specs is None:
        out_specs = [pl.BlockSpec((tm, tn), lambda i, j, k: (i, j)) for _ in outs]
    na, ne, no = len(a_extra), len(epi_extra), len(outs)

    def body(*refs):
        a_ref, b_ref = refs[0], refs[1]
        ax = refs[2:2 + na]
        ex = refs[2 + na:2 + na + ne]
        o = refs[2 + na + ne:2 + na + ne + no]

        def finish(res):
            res = epi(res, *[r[...] for r in ex]) if epi is not None else (res,)
            for ref, val in zip(o, res):
                ref[...] = val.astype(ref.dtype)

        at = a_ref[...]
        if a_fn is not None:
            at = a_fn(at, *[r[...] for r in ax])
        at = at.astype(BF16)
        if b_slabs == 1:
            part = _dot(at, b_ref[...].astype(BF16), mode)
        else:
            ks = tk // b_slabs
            part = _dot(at[:, 0:ks], b_ref[0].astype(BF16), mode)
            for s in range(1, b_slabs):
                part = part + _dot(at[:, s * ks:(s + 1) * ks], b_ref[s].astype(BF16), mode)
        if nk == 1:
            finish(part)
            return
        acc = refs[-1]
        k = pl.program_id(2)

        @pl.when(k == 0)
        def _():
            acc[...] = part

        @pl.when(k > 0)
        def _():
            acc[...] += part

        @pl.when(k == nk - 1)
        def _():
            finish(acc[...])

    return pl.pallas_call(
        body, name=name, grid=(m_ // tm, n_ // tn, nk),
        in_specs=[a_spec, b_spec] + [s for _, s in a_extra] + [s for _, s in epi_extra],
        out_specs=out_specs, out_shape=[_sds(s, d) for s, d in outs],
        scratch_shapes=[pltpu.VMEM((tm, tn), F32)] if nk > 1 else [],
        compiler_params=_params(("parallel", "parallel", "arbitrary")),
    )(a, b, *[x for x, _ in a_extra], *[x for x, _ in epi_extra])


def _prenorm(name, x, ctx, gain, shsc):
    n_lat = x.shape[0] // ROW_BLOCK
    n_ctx = 0 if ctx is None else ctx.shape[0] // ROW_BLOCK
    d = x.shape[1]

    def norm(src, g_ref, m_ref, o_ref):
        xv = src[...]
        xh = xv * lax.rsqrt(jnp.mean(xv * xv, axis=-1, keepdims=True) + EPS_RMS)
        o_ref[...] = ((xh * g_ref[...]) * (1.0 + m_ref[1:2, :]) + m_ref[0:1, :]).astype(o_ref.dtype)

    def body(*refs):
        if ctx is None:
            x_ref, g_ref, m_ref, o_ref = refs
            norm(x_ref, g_ref, m_ref, o_ref)
        else:
            x_ref, c_ref, g_ref, m_ref, o_ref = refs
            i = pl.program_id(0)

            @pl.when(i < n_lat)
            def _():
                norm(x_ref, g_ref, m_ref, o_ref)

            @pl.when(i >= n_lat)
            def _():
                norm(c_ref, g_ref, m_ref, o_ref)

    in_specs = [pl.BlockSpec((ROW_BLOCK, d), lambda i: (jnp.minimum(i, n_lat - 1), 0))]
    args = [x]
    if ctx is not None:
        in_specs.append(pl.BlockSpec((ROW_BLOCK, d), lambda i: (jnp.maximum(i - n_lat, 0), 0)))
        args.append(ctx)
    in_specs += [pl.BlockSpec((1, d), lambda i: (0, 0)),
                 pl.BlockSpec((None, 2, d), lambda i: (jnp.minimum(i // n_lat, 1), 0, 0))]
    args += [gain, shsc]
    return pl.pallas_call(
        body, name=name, grid=(n_lat + n_ctx,), in_specs=in_specs,
        out_specs=pl.BlockSpec((ROW_BLOCK, d), lambda i: (i, 0)),
        out_shape=_sds(((n_lat + n_ctx) * ROW_BLOCK, d), BF16),
        compiler_params=_params(("parallel",)),
    )(*args)


def _norm_bwd(name, x, d_act, d_act_row0, gain, scale, res=None, aux=None):
    rows, d = x.shape
    nb = rows // ROW_BLOCK
    has_res = res is not None

    def body(*refs):
        if has_res:
            x_ref, da_ref, g_ref, sc_ref, r_ref, aux_ref, dx_ref, sums = refs
        else:
            x_ref, da_ref, g_ref, sc_ref, sums = refs
        i = pl.program_id(0)

        @pl.when(i == 0)
        def _():
            sums[...] = jnp.zeros_like(sums)

        xv, da = x_ref[...], da_ref[...]
        rstd = lax.rsqrt(jnp.mean(xv * xv, axis=-1, keepdims=True) + EPS_RMS)
        xh = xv * rstd
        g = g_ref[...]
        dn = da * (1.0 + sc_ref[...])
        sums[0] += _fold8(da)
        sums[1] += _fold8(da * (xh * g))
        sums[2] += _fold8(dn * xh)
        if has_res:
            dxh = dn * g
            dx = rstd * (dxh - xh * jnp.mean(dxh * xh, axis=-1, keepdims=True))
            rv = r_ref[...]
            dx_ref[...] = rv + dx
            sums[3] += _fold8(rv * aux_ref[...])

    row = lambda i: (i, 0)
    vec = pl.BlockSpec((1, d), lambda i: (0, 0))
    in_specs = [pl.BlockSpec((ROW_BLOCK, d), row), pl.BlockSpec((ROW_BLOCK, d), lambda i: (i + d_act_row0, 0)), vec, vec]
    args = [x, d_act, gain, scale]
    out_shape = [_sds((4, SUBLANES, d))]
    out_specs = [pl.BlockSpec((4, SUBLANES, d), lambda i: (0, 0, 0))]
    if has_res:
        in_specs += [pl.BlockSpec((ROW_BLOCK, d), row), pl.BlockSpec((ROW_BLOCK, d), row)]
        args += [res, aux]
        out_shape = [_sds((rows, d))] + out_shape
        out_specs = [pl.BlockSpec((ROW_BLOCK, d), row)] + out_specs
    return pl.pallas_call(
        body, name=name, grid=(nb,), in_specs=in_specs, out_specs=out_specs, out_shape=out_shape,
        compiler_params=_params(("arbitrary",)),
    )(*args)


def _ada_fwd(cond16, ada_w_loc, ada_b_loc):
    cols = ada_w_loc.shape[1]

    def body(c_ref, w_ref, b_ref, o_ref):
        s = _silu(c_ref[...]).astype(BF16)
        o_ref[...] = _dot(s, w_ref[...].astype(BF16), "nn") + b_ref[...]

    return pl.pallas_call(body, name="ada_fwd", out_shape=_sds((16, cols)), compiler_params=_params())(
        cond16, ada_w_loc, ada_b_loc)


def _ada_bwd(cond16, dmod16, ada_w_loc, c_ctx_row):
    k_, cols = ada_w_loc.shape

    def body(c_ref, dm_ref, w_ref, cc_ref, gw_ref, gc_ref):
        s = _silu(c_ref[...]).astype(BF16)
        dm = dm_ref[...]
        gw_ref[...] = _dot(s, dm.astype(BF16), "tn")
        dmc = jnp.sum(dm[8:16, :], axis=0, keepdims=True)
        dmc8 = jnp.broadcast_to(dmc, (SUBLANES, cols)).astype(BF16)
        ds = _dot(dmc8, w_ref[...].astype(BF16), "nt")
        row = lax.broadcasted_iota(jnp.int32, ds.shape, 0)
        gc_ref[...] = jnp.where(row == 0, ds * _dsilu(cc_ref[...]), 0.0)

    return pl.pallas_call(body, name="ada_bwd", out_shape=[_sds((k_, cols)), _sds((SUBLANES, k_))],
                          compiler_params=_params())(cond16, dmod16, ada_w_loc, c_ctx_row)


def _cmul(a, b):
    return a[0] * b[0] - a[1] * b[1], a[0] * b[1] + a[1] * b[0]


def _disc(lam_re, lam_im, ldt):
    dt = jnp.exp(ldt)
    mag = jnp.exp(lam_re * dt)
    th = lam_im * dt
    a_re, a_im = mag * jnp.cos(th), mag * jnp.sin(th)
    den = lam_re * lam_re + lam_im * lam_im
    n_re = a_re - 1.0
    f_re = (n_re * lam_re + a_im * lam_im) / den
    f_im = (a_im * lam_re - n_re * lam_im) / den
    return dt, mag, th, a_re, a_im, den, n_re, f_re, f_im


def _block_diag_mask(shape):
    row = lax.broadcasted_iota(jnp.int32, shape, 0)
    col = lax.broadcasted_iota(jnp.int32, shape, 1)
    return lax.shift_right_logical(row, 4) == lax.shift_right_logical(col, 6)


TAB_A = 0
TAB_BIG = 1
TAB_SEG = 4
TAB_PW = 5
TAB_ROWS = TAB_PW + STEPS


def _s5_discretise(name, ascending, lam_re, lam_im, ldt, bt_re, bt_im, ct_re, ct_im):
    def write_tables(ref, pw, big, asc, sign):
        row = lax.broadcasted_iota(jnp.int32, (SUBLANES, NSTATE), 0)
        full = lambda v: jnp.broadcast_to(v, (SUBLANES, NSTATE))

        def put(t, p):
            ref[0, t] = full(p[0])
            ref[1, t] = full(sign * p[1])

        put(TAB_A, pw[0])
        for t in range(3):
            put(TAB_BIG + t, big[t])
        seg = [big[0]]
        for _ in range(SEGMENTS - 1):
            seg.append(_cmul(seg[-1], big[0]))
        seg_re = jnp.zeros((SUBLANES, NSTATE), F32)
        seg_im = jnp.zeros((SUBLANES, NSTATE), F32)
        for r in range(SEGMENTS):
            p = seg[r] if asc else seg[SEGMENTS - 1 - r]
            seg_re = jnp.where(row == r, p[0], seg_re)
            seg_im = jnp.where(row == r, sign * p[1], seg_im)
        ref[0, TAB_SEG] = seg_re
        ref[1, TAB_SEG] = seg_im
        for k in range(STEPS):
            put(TAB_PW + k, pw[k])

    def body(lr_ref, li_ref, ldt_ref, br_ref, bi_ref, cr_ref, ci_ref, bb_ref, tab_ref, adj_ref, bm_ref, cm_ref):
        _, _, _, a_re, a_im, _, _, f_re, f_im = _disc(lr_ref[...], li_ref[...], ldt_ref[...])
        bre, bim = br_ref[...], bi_ref[...]
        bb_re = f_re * bre - f_im * bim
        bb_im = f_re * bim + f_im * bre
        bb_ref[0:S5_GROUP, :] = bb_re
        bb_ref[S5_GROUP:2 * S5_GROUP, :] = bb_im
        pw = [(a_re, a_im)]
        for _ in range(STEPS - 1):
            pw.append(_cmul(pw[-1], (a_re, a_im)))
        big = [pw[STEPS - 1]]
        for _ in range(2):
            big.append(_cmul(big[-1], big[-1]))
        write_tables(tab_ref, pw, big, ascending, 1.0)
        write_tables(adj_ref, pw, big, not ascending, -1.0)
        half = NSTATE // S5_BLOCKS
        mask = _block_diag_mask((S5_BLOCK_WIDTH, half))
        tile = lambda v: jnp.broadcast_to(v[None], (S5_BLOCK_WIDTH // S5_GROUP, S5_GROUP, half)).reshape(S5_BLOCK_WIDTH, half)
        for c in range(S5_BLOCKS):
            cols = slice(c * half, (c + 1) * half)
            rows = slice(c * S5_BLOCK_WIDTH, (c + 1) * S5_BLOCK_WIDTH)
            bm_ref[c, :, 0:half] = jnp.where(mask, tile(bb_re[:, cols]), 0.0).astype(BF16)
            bm_ref[c, :, half:2 * half] = jnp.where(mask, tile(bb_im[:, cols]), 0.0).astype(BF16)
            cm_ref[c, :, 0:half] = jnp.where(mask, cr_ref[rows, :], 0.0).astype(BF16)
            cm_ref[c, :, half:2 * half] = jnp.where(mask, -ci_ref[rows, :], 0.0).astype(BF16)

    blocked = _sds((S5_BLOCKS, S5_BLOCK_WIDTH, 2 * NSTATE // S5_BLOCKS), BF16)
    return pl.pallas_call(
        body, name=name,
        out_shape=[_sds((2 * S5_GROUP, NSTATE)), _sds((2, TAB_ROWS, SUBLANES, NSTATE)),
                   _sds((2, TAB_ROWS, SUBLANES, NSTATE)), blocked, blocked],
        compiler_params=_params(),
    )(lam_re, lam_im, ldt, bt_re, bt_im, ct_re, ct_im)


def _s5_discretise_bwd(name, lam_re, lam_im, ldt, bt_re, bt_im, d_abar8, d_bbar):
    def body(lr_ref, li_ref, ldt_ref, br_ref, bi_ref, da_ref, db_ref, dl_ref, dbt_ref):
        lam_re, lam_im = lr_ref[...], li_ref[...]
        dt, _, _, a_re, a_im, den, n_re, f_re, f_im = _disc(lam_re, lam_im, ldt_ref[...])
        bre, bim = br_ref[...], bi_ref[...]
        dbr, dbi = db_ref[0:S5_GROUP, :], db_ref[S5_GROUP:2 * S5_GROUP, :]
        dbt_ref[0:S5_GROUP, :] = f_re * dbr + f_im * dbi
        dbt_ref[S5_GROUP:2 * S5_GROUP, :] = f_re * dbi - f_im * dbr
        df_re = jnp.sum(bre * dbr + bim * dbi, axis=0, keepdims=True)
        df_im = jnp.sum(bre * dbi - bim * dbr, axis=0, keepdims=True)
        da = da_ref[...]
        da_re = jnp.sum(da[:, 0:NSTATE], axis=0, keepdims=True)
        da_im = jnp.sum(da[:, NSTATE:2 * NSTATE], axis=0, keepdims=True)
        da_re = da_re + (df_re * lam_re - df_im * lam_im) / den
        da_im = da_im + (df_re * lam_im + df_im * lam_re) / den
        ff = (f_re * df_re + f_im * df_im) * 2.0 / den
        d_lr = (df_re * n_re + df_im * a_im) / den - ff * lam_re
        d_li = (df_re * a_im - df_im * n_re) / den - ff * lam_im
        d_mag_mag = da_re * a_re + da_im * a_im
        d_th = da_im * a_re - da_re * a_im
        d_lr = d_lr + d_mag_mag * dt
        d_li = d_li + d_th * dt
        d_ldt = (d_mag_mag * lam_re + d_th * lam_im) * dt
        row = lax.broadcasted_iota(jnp.int32, (SUBLANES, NSTATE), 0)
        dl_ref[...] = jnp.where(row == 0, d_lr, jnp.where(row == 1, d_li, jnp.where(row == 2, d_ldt, 0.0)))

    return pl.pallas_call(
        body, name=name, out_shape=[_sds((SUBLANES, NSTATE)), _sds((2 * S5_GROUP, NSTATE))],
        compiler_params=_params(),
    )(lam_re, lam_im, ldt, bt_re, bt_im, d_abar8, d_bbar)


def _segment_permutation(reverse_time):
    rho = jnp.arange(ROW_BLOCK)
    src = STEPS * (rho % SEGMENTS) + rho // SEGMENTS
    if reverse_time:
        src = ROW_BLOCK - 1 - src
    return (src[:, None] == jnp.arange(ROW_BLOCK)[None, :]).astype(BF16)


def _permute_rows(perm_ref, v):
    return _dot(perm_ref[...], v, "nn").astype(BF16)


def _unpermute_rows(perm_t_ref, v):
    hi = v.astype(BF16)
    lo = (v - hi.astype(F32)).astype(BF16)
    return _dot(perm_t_ref[...], hi, "nn") + _dot(perm_t_ref[...], lo, "nn")


def _unrolled_loop(step, init):
    def trip(o, state):
        for u in range(SCAN_UNROLL):
            state = step(o * SCAN_UNROLL + u, state)
        return state

    if SCAN_UNROLL == STEPS:
        return trip(0, init)
    return lax.fori_loop(0, STEPS // SCAN_UNROLL, trip, init)


def _scan_chunk(x_ref, out_ref, tab_ref, carry_re, carry_im, ascending, pair_ref=None, acc_ref=None, lane_chunks=None):
    w = SCAN_LANES
    half = NSTATE // S5_BLOCKS
    row = lax.broadcasted_iota(jnp.int32, (SUBLANES, w), 0)
    last = (SEGMENTS - 1) if ascending else 0

    def from_previous_segment(v, k, fill):
        if ascending:
            return jnp.where(row >= k, pltpu.roll(v, k, 0), fill)
        return jnp.where(row < SEGMENTS - k, pltpu.roll(v, SEGMENTS - k, 0), fill)

    def tile_rows(k):
        return pl.ds(pl.multiple_of((k if ascending else STEPS - 1 - k) * SUBLANES, SUBLANES), SUBLANES)

    for j in (range(NSTATE // w) if lane_chunks is None else lane_chunks):
        n_l = pl.ds(j * w, w)
        lane0 = (j * w // half) * 2 * half + (j * w) % half
        re_l, im_l = pl.ds(lane0, w), pl.ds(lane0 + half, w)
        tab = lambda t, n_l=n_l: (tab_ref[0, t, :, n_l], tab_ref[1, t, :, n_l])
        a_re, a_im = tab(TAB_A)

        def local_step(k, h):
            rs = tile_rows(k)
            h_re = a_re * h[0] - a_im * h[1] + x_ref[rs, re_l]
            h_im = a_re * h[1] + a_im * h[0] + x_ref[rs, im_l]
            out_ref[rs, re_l] = h_re
            out_ref[rs, im_l] = h_im
            return h_re, h_im

        zero = jnp.zeros((SUBLANES, w), F32)
        end_re, end_im = _unrolled_loop(local_step, (zero, zero))
        for t, k in ((TAB_BIG, 1), (TAB_BIG + 1, 2), (TAB_BIG + 2, 4)):
            p_re, p_im = tab(t)
            s_re, s_im = from_previous_segment(end_re, k, 0.0), from_previous_segment(end_im, k, 0.0)
            end_re, end_im = end_re + (p_re * s_re - p_im * s_im), end_im + (p_re * s_im + p_im * s_re)
        c0_re, c0_im = carry_re[:, n_l], carry_im[:, n_l]
        p_re, p_im = tab(TAB_SEG)
        end_re = end_re + (p_re * c0_re - p_im * c0_im)
        end_im = end_im + (p_re * c0_im + p_im * c0_re)
        carry_re[:, n_l] = jnp.broadcast_to(end_re[last:last + 1, :], end_re.shape)
        carry_im[:, n_l] = jnp.broadcast_to(end_im[last:last + 1, :], end_im.shape)
        in_re = from_previous_segment(end_re, 1, c0_re)
        in_im = from_previous_segment(end_im, 1, c0_im)

        def carry_step(k, st):
            rs = tile_rows(k)
            p_re, p_im = tab_ref[0, TAB_PW + k, :, n_l], tab_ref[1, TAB_PW + k, :, n_l]
            o_re = out_ref[rs, re_l] + (p_re * in_re - p_im * in_im)
            o_im = out_ref[rs, im_l] + (p_re * in_im + p_im * in_re)
            out_ref[rs, re_l] = o_re
            out_ref[rs, im_l] = o_im
            if pair_ref is None:
                return st
            s_re, s_im = pair_ref[rs, re_l], pair_ref[rs, im_l]
            return (o_re, o_im, st[2] + (st[0] * s_re + st[1] * s_im), st[3] + (st[1] * s_re - st[0] * s_im))

        if pair_ref is None:
            _unrolled_loop(carry_step, 0)
        else:
            fin = _unrolled_loop(carry_step, (in_re, in_im, zero, zero))
            acc_ref[:, n_l] += fin[2]
            acc_ref[:, pl.ds(NSTATE + j * w, w)] += fin[3]


def _scan_block_index(i, n_lat, ctx_first_then_ascending):
    if ctx_first_then_ascending:
        return jnp.where(i == 0, n_lat, i - 1)
    return jnp.where(i == 0, n_lat, n_lat - i)


def _full_spec(shape):
    return pl.BlockSpec(shape, lambda i: (0,) * len(shape))


_S5_BLOCKED = (S5_BLOCKS, S5_BLOCK_WIDTH, 2 * NSTATE // S5_BLOCKS)
_S5_TABLES = (2, TAB_ROWS, SUBLANES, NSTATE)
_S5_DIAG = (S5_BLOCKS, S5_GROUP, 2 * NSTATE // S5_BLOCKS)


def _s5_scan_fwd(name, ascending, z_all, bmat, cmat, tab, perm, perm_t, y_other=None, d_skip=None, w_glu=None):
    rows = z_all.shape[0]
    nb = rows // ROW_BLOCK
    n_lat = nb - 1
    bw, sw = S5_BLOCK_WIDTH, 2 * NSTATE // S5_BLOCKS
    gated = y_other is not None

    def body(*refs):
        u_ref, bm_ref, cm_ref, tab_ref, p_ref, pt_ref = refs[:6]
        extra = refs[6:9] if gated else ()
        s_ref, y_ref = refs[6 + len(extra):8 + len(extra)]
        bu, yp, carry_re, carry_im = refs[-4:]

        @pl.when(pl.program_id(0) == 0)
        def _():
            carry_re[...] = jnp.zeros_like(carry_re)
            carry_im[...] = jnp.zeros_like(carry_im)

        up = _permute_rows(p_ref, u_ref[...].astype(BF16))
        for c in range(S5_BLOCKS):
            bu[:, c * sw:(c + 1) * sw] = _dot(up[:, c * bw:(c + 1) * bw], bm_ref[c], "nn")
        _scan_chunk(bu, s_ref, tab_ref, carry_re, carry_im, False)
        for c in range(S5_BLOCKS):
            yp[:, c * bw:(c + 1) * bw] = _dot(s_ref[:, c * sw:(c + 1) * sw].astype(BF16), cm_ref[c], "nt")
        y = _unpermute_rows(pt_ref, yp[...])
        y_ref[...] = y
        if gated:
            y_other_ref, d_ref, w_ref = extra
            gel = _gelu(d_ref[...] * u_ref[...] + y_other_ref[...] + y)
            refs[8 + len(extra)][...] = (gel * _sigmoid(_dot(gel.astype(BF16), w_ref[...], "nn"))).astype(BF16)

    blk = lambda i: (_scan_block_index(i, n_lat, ascending), 0)
    in_specs = [pl.BlockSpec((ROW_BLOCK, S5_WIDTH), blk), _full_spec(_S5_BLOCKED), _full_spec(_S5_BLOCKED),
                _full_spec(_S5_TABLES), _full_spec((ROW_BLOCK, ROW_BLOCK)), _full_spec((ROW_BLOCK, ROW_BLOCK))]
    args = [z_all, bmat, cmat, tab, perm, perm_t]
    out_specs = [pl.BlockSpec((ROW_BLOCK, 2 * NSTATE), blk), pl.BlockSpec((ROW_BLOCK, S5_WIDTH), blk)]
    out_shape = [_sds((rows, 2 * NSTATE)), _sds((rows, S5_WIDTH))]
    if gated:
        in_specs += [pl.BlockSpec((ROW_BLOCK, S5_WIDTH), blk), _full_spec((1, S5_WIDTH)), _full_spec((S5_WIDTH, S5_WIDTH))]
        args += [y_other, d_skip, w_glu]
        out_specs.append(pl.BlockSpec((ROW_BLOCK, S5_WIDTH),
                                      lambda i: (jnp.minimum(_scan_block_index(i, n_lat, ascending), n_lat - 1), 0)))
        out_shape.append(_sds((n_lat * ROW_BLOCK, S5_WIDTH + CONV_WIDTH), BF16))
    return pl.pallas_call(
        body, name=name, grid=(nb,), in_specs=in_specs, out_specs=out_specs, out_shape=out_shape,
        scratch_shapes=[pltpu.VMEM((ROW_BLOCK, 2 * NSTATE), F32), pltpu.VMEM((ROW_BLOCK, S5_WIDTH), F32),
                        pltpu.VMEM((SUBLANES, NSTATE), F32), pltpu.VMEM((SUBLANES, NSTATE), F32)],
        compiler_params=_params(("arbitrary",)),
    )(*args)


def _s5_scan_bwd(name, ascending, dy, z_all, states, bmat, cmat, adj, perm, perm_t, du_other=None, d_skip=None):
    rows = states.shape[0]
    nb = rows // ROW_BLOCK
    n_lat = nb - 1
    bw, sw = S5_BLOCK_WIDTH, 2 * NSTATE // S5_BLOCKS
    finish = du_other is not None

    def block_index(i):
        if ascending:
            return jnp.where(i == nb - 1, n_lat, n_lat - 1 - i)
        return jnp.where(i == nb - 1, n_lat, i)

    def body(*refs):
        dy_ref, u_ref, s_ref, bm_ref, cm_ref, adj_ref, p_ref, pt_ref = refs[:8]
        extra = refs[8:10] if finish else ()
        du_ref, db_ref, dc_ref, da_ref, g, dup, db_acc, dc_acc, carry_re, carry_im = refs[8 + len(extra):]
        i = pl.program_id(0)

        @pl.when(i == 0)
        def _():
            carry_re[...] = jnp.zeros_like(carry_re)
            carry_im[...] = jnp.zeros_like(carry_im)
            da_ref[...] = jnp.zeros_like(da_ref)
            db_acc[...] = jnp.zeros_like(db_acc)
            dc_acc[...] = jnp.zeros_like(dc_acc)

        has_dy = (i < nb - 1).astype(F32)
        dyp = _permute_rows(p_ref, (dy_ref[...] * has_dy).astype(BF16))
        up = _permute_rows(p_ref, u_ref[...].astype(BF16))
        for c in range(S5_BLOCKS):
            g[:, c * sw:(c + 1) * sw] = _dot(dyp[:, c * bw:(c + 1) * bw], cm_ref[c], "nn")
            dc_acc[c] += _dot(dyp[:, c * bw:(c + 1) * bw], s_ref[:, c * sw:(c + 1) * sw].astype(BF16), "tn")
            _scan_chunk(g, g, adj_ref, carry_re, carry_im, True, pair_ref=s_ref, acc_ref=da_ref, lane_chunks=[c])
            gc = g[:, c * sw:(c + 1) * sw].astype(BF16)
            dup[:, c * bw:(c + 1) * bw] = _dot(gc, bm_ref[c], "nt")
            db_acc[c] += _dot(up[:, c * bw:(c + 1) * bw], gc, "tn")
        du = _unpermute_rows(pt_ref, dup[...])
        if finish:
            du = du + extra[0][...] + (dy_ref[...] * has_dy) * extra[1][...]
        du_ref[...] = du.astype(du_ref.dtype)

        @pl.when(i == nb - 1)
        def _():
            mask = _block_diag_mask((bw, sw // 2))
            for acc, out in ((db_acc, db_ref), (dc_acc, dc_ref)):
                for c in range(S5_BLOCKS):
                    for part in range(2):
                        cols = slice(part * (sw // 2), (part + 1) * (sw // 2))
                        kept = jnp.where(mask, acc[c, :, cols], 0.0)
                        out[c, :, cols] = kept.reshape(bw // S5_GROUP, S5_GROUP, sw // 2).sum(axis=0)

    blk = lambda i: (block_index(i), 0)
    in_specs = [pl.BlockSpec((ROW_BLOCK, S5_WIDTH), lambda i: (jnp.minimum(block_index(i), n_lat - 1), 0)),
                pl.BlockSpec((ROW_BLOCK, S5_WIDTH), blk), pl.BlockSpec((ROW_BLOCK, 2 * NSTATE), blk),
                _full_spec(_S5_BLOCKED), _full_spec(_S5_BLOCKED), _full_spec(_S5_TABLES),
                _full_spec((ROW_BLOCK, ROW_BLOCK)), _full_spec((ROW_BLOCK, ROW_BLOCK))]
    args = [dy, z_all, states, bmat, cmat, adj, perm, perm_t]
    if finish:
        in_specs += [pl.BlockSpec((ROW_BLOCK, S5_WIDTH), blk), _full_spec((1, S5_WIDTH))]
        args += [du_other, d_skip]
    return pl.pallas_call(
        body, name=name, grid=(nb,), in_specs=in_specs,
        out_specs=[pl.BlockSpec((ROW_BLOCK, S5_WIDTH), blk), _full_spec(_S5_DIAG), _full_spec(_S5_DIAG),
                   _full_spec((SUBLANES, 2 * NSTATE))],
        out_shape=[_sds((rows, S5_WIDTH), BF16 if finish else F32), _sds(_S5_DIAG), _sds(_S5_DIAG),
                   _sds((SUBLANES, 2 * NSTATE))],
        scratch_shapes=[pltpu.VMEM((ROW_BLOCK, 2 * NSTATE), F32), pltpu.VMEM((ROW_BLOCK, S5_WIDTH), F32),
                        pltpu.VMEM(_S5_BLOCKED, F32), pltpu.VMEM(_S5_BLOCKED, F32),
                        pltpu.VMEM((SUBLANES, NSTATE), F32), pltpu.VMEM((SUBLANES, NSTATE), F32)],
        compiler_params=_params(("arbitrary",)),
    )(*args)


def _latent_row_tile(n_rows):
    return 512 if n_rows % 512 == 0 else ROW_BLOCK


def _glu_bwd(d_ycat, z_all, y0, y1, d_skip, w_glu, n_rows):
    def body(do_ref, u_ref, y0_ref, y1_ref, d_ref, w_ref, dy_ref, dw_ref, dd_ref):
        @pl.when(pl.program_id(0) == 0)
        def _():
            dw_ref[...] = jnp.zeros_like(dw_ref)
            dd_ref[...] = jnp.zeros_like(dd_ref)

        u = u_ref[...]
        y = d_ref[...] * u + y0_ref[...] + y1_ref[...]
        g = _gelu(y)
        gb = g.astype(BF16)
        w = w_ref[...]
        sg = _sigmoid(_dot(gb, w, "nn"))
        do = do_ref[...]
        dt = do * g * sg * (1.0 - sg)
        dtb = dt.astype(BF16)
        dg = do * sg + _dot(dtb, w, "nt")
        dy = dg * _dgelu(y)
        dy_ref[...] = dy
        dw_ref[...] += _dot(gb, dtb, "tn")
        dd_ref[...] += _fold8(dy * u)

    rows = _latent_row_tile(n_rows)
    row = pl.BlockSpec((rows, S5_WIDTH), lambda i: (i, 0))
    sq = pl.BlockSpec((S5_WIDTH, S5_WIDTH), lambda i: (0, 0))
    return pl.pallas_call(
        body, name="glu_bwd", grid=(n_rows // rows,),
        in_specs=[row, row, row, row, pl.BlockSpec((1, S5_WIDTH), lambda i: (0, 0)), sq],
        out_specs=[row, sq, pl.BlockSpec((SUBLANES, S5_WIDTH), lambda i: (0, 0))],
        out_shape=[_sds((n_rows, S5_WIDTH)), _sds((S5_WIDTH, S5_WIDTH)), _sds((SUBLANES, S5_WIDTH))],
        compiler_params=_params(("arbitrary",)),
    )(d_ycat, z_all, y0, y1, d_skip, w_glu)


CONV_HALF = CONV_K // 2


def _conv_block(n_rows):
    blk = min(1024, n_rows)
    assert blk >= CONV_HALF * GRID_W and n_rows % blk == 0
    return blk


def _conv_gate(z_all, n_rows):
    blk = _conv_block(n_rows)
    nb = n_rows // blk

    def body(v_ref, g_ref, o_ref):
        i = pl.program_id(0)
        inside = jnp.logical_and(i >= 1, i <= nb)

        @pl.when(inside)
        def _():
            o_ref[...] = v_ref[...] * _sigmoid(g_ref[...])

        @pl.when(jnp.logical_not(inside))
        def _():
            o_ref[...] = jnp.zeros_like(o_ref)

    src = lambda col: pl.BlockSpec((blk, CONV_WIDTH), lambda i: (jnp.clip(i - 1, 0, nb - 1), col))
    return pl.pallas_call(
        body, name="conv_gate", grid=(nb + 2,), in_specs=[src(1), src(2)],
        out_specs=pl.BlockSpec((blk, CONV_WIDTH), lambda i: (i, 0)),
        out_shape=_sds(((nb + 2) * blk, CONV_WIDTH)), compiler_params=_params(("parallel",)),
    )(z_all, z_all)


def _stream_padded(pad_ref, buf, sems, blk, n_blocks):
    i = pl.program_id(0)

    def copy(b):
        rows = pl.ds(pl.multiple_of(b * blk, blk), blk)
        return pltpu.make_async_copy(pad_ref.at[rows, :], buf.at[rows, :], sems.at[b])

    @pl.when(i == 0)
    def _():
        for b in range(n_blocks):
            copy(b).start()
        copy(0).wait()
        copy(1).wait()

    copy(i + 2).wait()
    return pl.multiple_of(i * blk, blk)


def _conv_fwd(hh_pad, w, b, ln_g, ln_b, ycat, n_rows):
    blk = _conv_block(n_rows)
    nblk = n_rows // blk + 2

    def body(hh_ref, w_ref, b_ref, g_ref, lb_ref, ycat_ref, hc_ref, y_ref, win, sems):
        base = _stream_padded(hh_ref, win, sems, blk, nblk)

        def tile(t, _):
            r0 = pl.multiple_of(t * CONV_ROWS, CONV_ROWS)
            acc = jnp.zeros((CONV_ROWS, CONV_WIDTH), F32)
            for k in range(CONV_K):
                acc = acc + w_ref[k:k + 1, :] * win[pl.ds(base + r0 + blk + (k - CONV_HALF) * GRID_W, CONV_ROWS), :]
            hc = acc + b_ref[...]
            hc_ref[pl.ds(r0, CONV_ROWS), :] = hc
            mu = jnp.mean(hc, axis=-1, keepdims=True)
            xc = hc - mu
            ln = xc * lax.rsqrt(jnp.mean(xc * xc, axis=-1, keepdims=True) + EPS_LN) * g_ref[...] + lb_ref[...]
            y_ref[pl.ds(r0, CONV_ROWS), :] = _silu(ln).astype(y_ref.dtype)
            return 0

        lax.fori_loop(0, blk // CONV_ROWS, tile, 0)

    vec = pl.BlockSpec((1, CONV_WIDTH), lambda i: (0, 0))
    row = pl.BlockSpec((blk, CONV_WIDTH), lambda i: (i, 0))
    return pl.pallas_call(
        body, name="conv_fwd", grid=(n_rows // blk,),
        in_specs=[ANY, pl.BlockSpec((CONV_K, CONV_WIDTH), lambda i: (0, 0)), vec, vec, vec, ANY],
        out_specs=[row, pl.BlockSpec((blk, CONV_WIDTH), lambda i: (i, 1))],
        out_shape=[_sds((n_rows, CONV_WIDTH)), _sds(ycat.shape, ycat.dtype)], input_output_aliases={5: 1},
        scratch_shapes=[pltpu.VMEM((nblk * blk, CONV_WIDTH), F32), pltpu.SemaphoreType.DMA((nblk,))],
        compiler_params=_params(("arbitrary",)),
    )(hh_pad, w, b, ln_g, ln_b, ycat)


def _conv_bwd_norm(d_ycat, hc, ln_g, ln_b, n_rows):
    blk = _conv_block(n_rows)
    nb = n_rows // blk

    def body(dy_ref, hc_ref, g_ref, lb_ref, o_ref, sums):
        i = pl.program_id(0)

        @pl.when(i == 0)
        def _():
            sums[...] = jnp.zeros_like(sums)

        inside = jnp.logical_and(i >= 1, i <= nb)

        @pl.when(inside)
        def _():
            hcv = hc_ref[...]
            mu = jnp.mean(hcv, axis=-1, keepdims=True)
            xc = hcv - mu
            rstd = lax.rsqrt(jnp.mean(xc * xc, axis=-1, keepdims=True) + EPS_LN)
            xh = xc * rstd
            g = g_ref[...]
            dln = dy_ref[...] * _dsilu(xh * g + lb_ref[...])
            dxh = dln * g
            dhc = rstd * (dxh - jnp.mean(dxh, axis=-1, keepdims=True) - xh * jnp.mean(dxh * xh, axis=-1, keepdims=True))
            o_ref[...] = dhc
            sums[0] += _fold8(dhc)
            sums[1] += _fold8(dln * xh)
            sums[2] += _fold8(dln)

        @pl.when(jnp.logical_not(inside))
        def _():
            o_ref[...] = jnp.zeros_like(o_ref)

    vec = pl.BlockSpec((1, CONV_WIDTH), lambda i: (0, 0))
    return pl.pallas_call(
        body, name="conv_bwd_norm", grid=(nb + 2,),
        in_specs=[pl.BlockSpec((blk, CONV_WIDTH), lambda i: (jnp.clip(i - 1, 0, nb - 1), 1)),
                  pl.BlockSpec((blk, CONV_WIDTH), lambda i: (jnp.clip(i - 1, 0, nb - 1), 0)), vec, vec],
        out_specs=[pl.BlockSpec((blk, CONV_WIDTH), lambda i: (i, 0)),
                   pl.BlockSpec((3, SUBLANES, CONV_WIDTH), lambda i: (0, 0, 0))],
        out_shape=[_sds(((nb + 2) * blk, CONV_WIDTH)), _sds((3, SUBLANES, CONV_WIDTH))],
        compiler_params=_params(("arbitrary",)),
    )(d_ycat, hc, ln_g, ln_b)


def _conv_bwd_taps(dhc_pad, hh_pad, z_all, w, n_rows):
    blk = _conv_block(n_rows)
    nblk = n_rows // blk + 2

    def body(dhc_ref, hh_ref, v_ref, g_ref, w_ref, dv_ref, dg_ref, dw_ref, dwin, hwin, dsems, hsems):
        @pl.when(pl.program_id(0) == 0)
        def _():
            dw_ref[...] = jnp.zeros_like(dw_ref)

        base = _stream_padded(dhc_ref, dwin, dsems, blk, nblk)
        _stream_padded(hh_ref, hwin, hsems, blk, nblk)

        def tile(t, _):
            r0 = pl.multiple_of(t * CONV_BWD_ROWS, CONV_BWD_ROWS) + base
            dh = dwin[pl.ds(r0 + blk, CONV_BWD_ROWS), :]
            acc = jnp.zeros((CONV_BWD_ROWS, CONV_WIDTH), F32)
            for k in range(CONV_K):
                off = (k - CONV_HALF) * GRID_W
                acc = acc + w_ref[k:k + 1, :] * dwin[pl.ds(r0 + blk - off, CONV_BWD_ROWS), :]
                dw_ref[k] += _fold8(dh * hwin[pl.ds(r0 + blk + off, CONV_BWD_ROWS), :])
            rs = pl.ds(pl.multiple_of(t * CONV_BWD_ROWS, CONV_BWD_ROWS), CONV_BWD_ROWS)
            sg = _sigmoid(g_ref[rs, :])
            vv = v_ref[rs, :]
            dv_ref[rs, :] = (acc * sg).astype(dv_ref.dtype)
            dg_ref[rs, :] = (acc * vv * sg * (1.0 - sg)).astype(dg_ref.dtype)
            return 0

        lax.fori_loop(0, blk // CONV_BWD_ROWS, tile, 0)

    row = pl.BlockSpec((blk, CONV_WIDTH), lambda i: (i, 0))
    return pl.pallas_call(
        body, name="conv_bwd_taps", grid=(n_rows // blk,),
        in_specs=[ANY, ANY,
            pl.BlockSpec((blk, CONV_WIDTH), lambda i: (i, 1)), pl.BlockSpec((blk, CONV_WIDTH), lambda i: (i, 2)),
            pl.BlockSpec((CONV_K, CONV_WIDTH), lambda i: (0, 0))],
        out_specs=[row, row, pl.BlockSpec((CONV_K, SUBLANES, CONV_WIDTH), lambda i: (0, 0, 0))],
        out_shape=[_sds((n_rows, CONV_WIDTH), BF16), _sds((n_rows, CONV_WIDTH), BF16),
                   _sds((CONV_K, SUBLANES, CONV_WIDTH))],
        scratch_shapes=[pltpu.VMEM((nblk * blk, CONV_WIDTH), F32), pltpu.VMEM((nblk * blk, CONV_WIDTH), F32),
                        pltpu.SemaphoreType.DMA((nblk,)), pltpu.SemaphoreType.DMA((nblk,))],
        compiler_params=_params(("arbitrary",)),
    )(dhc_pad, hh_pad, z_all, z_all, w)


def _sum_parts(parts):
    _, r, c = parts.shape

    def body(p_ref, o_ref):
        acc = p_ref[0]
        for q in range(1, NDEV):
            acc = acc + p_ref[q]
        o_ref[...] = acc

    return pl.pallas_call(body, name="sum_parts", out_shape=_sds((r, c)), compiler_params=_params())(parts)


def _row_tile(r, c):
    best = r
    for t in (1024, 512, 256, 128, 64, 32, 16, 8):
        if r % t == 0 and t * c <= 128 * 1024:
            return t
    return best


def _adamw(name, w, gparts, m, v):
    r, c = w.shape
    np_ = gparts.shape[0]
    tr = _row_tile(r, c)

    def body(w_ref, g_ref, m_ref, v_ref, go_ref, d_ref, mo_ref, vo_ref):
        g = g_ref[0].astype(F32)
        for q in range(1, np_):
            g = g + g_ref[q].astype(F32)
        m2 = ADAM_B1 * m_ref[...] + (1.0 - ADAM_B1) * g
        v2 = ADAM_B2 * v_ref[...] + (1.0 - ADAM_B2) * jnp.square(g)
        m_hat = m2 / (1.0 - ADAM_B1 ** ADAM_STEP)
        v_hat = v2 / (1.0 - ADAM_B2 ** ADAM_STEP)
        go_ref[...] = g
        d_ref[...] = -ADAM_LR * (m_hat / (jnp.sqrt(v_hat) + ADAM_EPS) + ADAM_WD * w_ref[...])
        mo_ref[...] = m2
        vo_ref[...] = v2

    row = pl.BlockSpec((tr, c), lambda i: (i, 0))
    return pl.pallas_call(
        body, name=name, grid=(r // tr,),
        in_specs=[row, pl.BlockSpec((np_, tr, c), lambda i: (0, i, 0)), row, row],
        out_specs=[row] * 4, out_shape=[_sds((r, c))] * 4, compiler_params=_params(("parallel",)),
    )(w, gparts, m, v)


def _adamw_native(name, w, g, m, v):
    def body(w_ref, g_ref, m_ref, v_ref, d_ref, mo_ref, vo_ref):
        gv = g_ref[...]
        m2 = ADAM_B1 * m_ref[...] + (1.0 - ADAM_B1) * gv
        v2 = ADAM_B2 * v_ref[...] + (1.0 - ADAM_B2) * jnp.square(gv)
        m_hat = m2 / (1.0 - ADAM_B1 ** ADAM_STEP)
        v_hat = v2 / (1.0 - ADAM_B2 ** ADAM_STEP)
        d_ref[...] = -ADAM_LR * (m_hat / (jnp.sqrt(v_hat) + ADAM_EPS) + ADAM_WD * w_ref[...])
        mo_ref[...] = m2
        vo_ref[...] = v2

    return pl.pallas_call(body, name=name, out_shape=[_sds(w.shape)] * 3, compiler_params=_params())(w, g, m, v)


SMALL = ["c_ctx", "ada_b", "norm1_g", "s5_lam_re", "s5_lam_im", "s5_log_dt", "s5_d", "conv_b", "conv_ln_g", "conv_ln_b",
         "norm2_g", "final_g"]
SMALL_PACKED_ROWS = 24


def _pack_rows(parts, rows):
    flat = jnp.concatenate([p.reshape(-1).astype(F32) for p in parts])
    return jnp.pad(flat, (0, rows * D_MODEL - flat.shape[0])).reshape(rows, D_MODEL)


def _unpack_rows(packed, shapes):
    flat = packed.reshape(-1)
    out, off = [], 0
    for shape in shapes:
        size = 1
        for s in shape:
            size *= s
        out.append(flat[off:off + size].reshape(shape))
        off += size
    return out


def kernel(x, c, ctx, c_ctx, ada_w, ada_b, norm1_g, w_in, s5_lam_re, s5_lam_im, s5_log_dt, s5_b_re, s5_b_im, s5_c_re, s5_c_im, s5_d, s5_w_glu, conv_w, conv_b, conv_ln_g, conv_ln_b, w_out, norm2_g, mlp_w1, mlp_w2, final_g, loss_target, m_c_ctx, m_ada_w, m_ada_b, m_norm1_g, m_w_in, m_s5_lam_re, m_s5_lam_im, m_s5_log_dt, m_s5_b_re, m_s5_b_im, m_s5_c_re, m_s5_c_im, m_s5_d, m_s5_w_glu, m_conv_w, m_conv_b, m_conv_ln_g, m_conv_ln_b, m_w_out, m_norm2_g, m_mlp_w1, m_mlp_w2, m_final_g, v_c_ctx, v_ada_w, v_ada_b, v_norm1_g, v_w_in, v_s5_lam_re, v_s5_lam_im, v_s5_log_dt, v_s5_b_re, v_s5_b_im, v_s5_c_re, v_s5_c_im, v_s5_d, v_s5_w_glu, v_conv_w, v_conv_b, v_conv_ln_g, v_conv_ln_b, v_w_out, v_norm2_g, v_mlp_w1, v_mlp_w2, v_final_g):
    weights = dict(c_ctx=c_ctx, ada_w=ada_w, ada_b=ada_b, norm1_g=norm1_g, w_in=w_in, s5_lam_re=s5_lam_re, s5_lam_im=s5_lam_im, s5_log_dt=s5_log_dt, s5_b_re=s5_b_re, s5_b_im=s5_b_im, s5_c_re=s5_c_re, s5_c_im=s5_c_im, s5_d=s5_d, s5_w_glu=s5_w_glu, conv_w=conv_w, conv_b=conv_b, conv_ln_g=conv_ln_g, conv_ln_b=conv_ln_b, w_out=w_out, norm2_g=norm2_g, mlp_w1=mlp_w1, mlp_w2=mlp_w2, final_g=final_g)
    mom1 = dict(c_ctx=m_c_ctx, ada_w=m_ada_w, ada_b=m_ada_b, norm1_g=m_norm1_g, w_in=m_w_in, s5_lam_re=m_s5_lam_re, s5_lam_im=m_s5_lam_im, s5_log_dt=m_s5_log_dt, s5_b_re=m_s5_b_re, s5_b_im=m_s5_b_im, s5_c_re=m_s5_c_re, s5_c_im=m_s5_c_im, s5_d=m_s5_d, s5_w_glu=m_s5_w_glu, conv_w=m_conv_w, conv_b=m_conv_b, conv_ln_g=m_conv_ln_g, conv_ln_b=m_conv_ln_b, w_out=m_w_out, norm2_g=m_norm2_g, mlp_w1=m_mlp_w1, mlp_w2=m_mlp_w2, final_g=m_final_g)
    mom2 = dict(c_ctx=v_c_ctx, ada_w=v_ada_w, ada_b=v_ada_b, norm1_g=v_norm1_g, w_in=v_w_in, s5_lam_re=v_s5_lam_re, s5_lam_im=v_s5_lam_im, s5_log_dt=v_s5_log_dt, s5_b_re=v_s5_b_re, s5_b_im=v_s5_b_im, s5_c_re=v_s5_c_re, s5_c_im=v_s5_c_im, s5_d=v_s5_d, s5_w_glu=v_s5_w_glu, conv_w=v_conv_w, conv_b=v_conv_b, conv_ln_g=v_conv_ln_g, conv_ln_b=v_conv_ln_b, w_out=v_w_out, norm2_g=v_norm2_g, mlp_w1=v_mlp_w1, mlp_w2=v_mlp_w2, final_g=v_final_g)
    order = list(weights)

    me = 4 * lax.axis_index("x") + 2 * lax.axis_index("y") + lax.axis_index("c")
    xs, cs, tgt = x[0], ctx[0], loss_target[0]
    n_lat_rows, n_ctx_rows = xs.shape[0], cs.shape[0]
    n_rows = n_lat_rows + n_ctx_rows
    n_lat = n_lat_rows // ROW_BLOCK
    ada_cols = ada_w.shape[2]

    (c_all,), _ = _exchange("gather_c", [c], [True])
    c_all = c_all.reshape(NDEV, D_MODEL)

    cond_fwd = jnp.concatenate([c_all, c_ctx[None], jnp.zeros((7, D_MODEL), F32)])
    ada_b_loc = lax.dynamic_slice(ada_b, (0, me * ada_cols), (1, ada_cols))
    (mod_g,), mod_token = _exchange("gather_mod", [_ada_fwd(cond_fwd, ada_w[0], ada_b_loc)], [True])
    weight_groups, weights_token = _exchange_start_groups("gather_weights_start", [
        ([w_in[0].astype(BF16)], [True]),
        ([s5_w_glu[0].astype(BF16), conv_w[0] + mod_token[0:1, 0:1], w_out[0].astype(BF16)], [True] * 3),
        ([mlp_w1[0].astype(BF16), mlp_w2[0].astype(BF16)], [True] * 2)])
    (wi_send, wi_recv, wi_src, wi_land), (mixer_send, mixer_recv, mixer_src, mixer_land), \
        (mlpw_send, mlpw_recv, mlpw_src, mlpw_land) = weight_groups
    mod_rows = jnp.transpose(mod_g, (1, 0, 2)).reshape(16, 6 * D_MODEL) + weights_token[0:1, 0:1]
    mod = lax.dynamic_slice(mod_rows, (me, 0), (1, 6 * D_MODEL)).reshape(6, D_MODEL)
    modc = mod_rows[8, :2 * D_MODEL].reshape(2, D_MODEL)
    sh1, sc1, g1, sh2, sc2, g2 = [mod[i:i + 1] for i in range(6)]

    lam_re, lam_im = s5_lam_re[0].reshape(2, 1, NSTATE), s5_lam_im[0].reshape(2, 1, NSTATE)
    ldt = jnp.repeat(s5_log_dt[0], S5_STATE, axis=-1).reshape(2, 1, NSTATE)
    bt_re = jnp.transpose(s5_b_re[0], (0, 3, 1, 2)).reshape(2, S5_GROUP, NSTATE)
    bt_im = jnp.transpose(s5_b_im[0], (0, 3, 1, 2)).reshape(2, S5_GROUP, NSTATE)
    groups_per_block = S5_GROUPS // S5_BLOCKS
    ct_re = jnp.tile(s5_c_re[0].reshape(2, S5_WIDTH, S5_STATE), (1, 1, groups_per_block))
    ct_im = jnp.tile(s5_c_im[0].reshape(2, S5_WIDTH, S5_STATE), (1, 1, groups_per_block))
    d_skip = s5_d[0].reshape(1, S5_WIDTH)
    perms = [_segment_permutation(reverse_time=(d == 0)) for d in range(2)]
    perms_t = [p.T for p in perms]
    disc = [_s5_discretise(f"s5_disc{d}", False, lam_re[d], lam_im[d], ldt[d], bt_re[d], bt_im[d], ct_re[d], ct_im[d])
            for d in range(2)]

    a_all = _prenorm("prenorm1", xs, cs, norm1_g, jnp.stack([mod[0:2], modc]))
    before_w_in = a_all[0:SUBLANES, 0:LANES].astype(F32) + disc[0][0][0:SUBLANES, 0:LANES] + disc[1][0][0:SUBLANES, 0:LANES]
    wi_own, wi_landed = _exchange_wait("gather_w_in_wait", wi_send, wi_recv, wi_src, wi_land, [True], before_w_in)
    w_in_full = jnp.transpose(_with_own(wi_landed[0], wi_own[0], me), (1, 0, 2)).reshape(D_MODEL, IN_COLS)
    tm_all = 1088 if n_rows % 1088 == 0 else ROW_BLOCK
    (z_all,) = _matmul("in_proj", a_all, w_in_full, "nn", (n_rows, IN_COLS, D_MODEL), (tm_all, IN_COLS, D_MODEL),
                       [((n_rows, IN_COLS), F32)])

    _, tab, _, bmat, cmat = disc[0]
    s0, y0 = _s5_scan_fwd("s5_scan_fwd0", True, z_all, bmat, cmat, tab, perms[0], perms_t[0])
    mixer_own, mixer_landed = _exchange_wait("gather_mixer_wait", mixer_send, mixer_recv, mixer_src, mixer_land,
                                             [True] * 3, y0)
    glu_g, conv_w_g, w_out_g = [_with_own(l, o, me) for l, o in zip(mixer_landed, mixer_own)]
    glu_full = glu_g.reshape(S5_WIDTH, S5_WIDTH)
    conv_w_full = jnp.transpose(conv_w_g, (1, 0, 2)).reshape(CONV_K, CONV_WIDTH)
    w_out_full = w_out_g.reshape(D_MODEL, D_MODEL)
    _, tab, _, bmat, cmat = disc[1]
    s1, y1, ycat = _s5_scan_fwd("s5_scan_fwd1", False, z_all, bmat, cmat, tab, perms[1], perms_t[1],
                                y_other=y0, d_skip=d_skip, w_glu=glu_full)
    states, y_dir = [s0, s1], [y0, y1]

    hh_pad = _conv_gate(z_all, n_lat_rows)
    hc, ycat = _conv_fwd(hh_pad, conv_w_full, conv_b, conv_ln_g, conv_ln_b, ycat, n_lat_rows)

    tm = min(1024, n_lat_rows)
    tm_e = min(512, n_lat_rows)
    w1_cols = D_FF // NDEV
    row_vec = lambda tn: pl.BlockSpec((1, tn), lambda i, j, k: (0, j))
    out_tile = lambda t_m, t_n: pl.BlockSpec((t_m, t_n), lambda i, j, k: (i, j))
    full_rows = ((n_lat_rows, D_MODEL), F32)
    sums = ((n_lat_rows // tm_e, SUBLANES, D_MODEL), F32)
    sums_spec = pl.BlockSpec((None, SUBLANES, D_MODEL), lambda i, j, k: (i, 0, 0))
    vec = lambda v: (v, row_vec(D_MODEL))
    transposed_tile = lambda t_m, t_n: pl.BlockSpec((t_n, t_m), lambda i, j, k: (j, i))
    mix, h1, a2, a2_t = _matmul(
        "out_proj", ycat, w_out_full, "nn", (n_lat_rows, D_MODEL, D_MODEL), (tm_e, D_MODEL, D_MODEL),
        [full_rows, full_rows, ((n_lat_rows, D_MODEL), BF16), ((D_MODEL, n_lat_rows), BF16)],
        epi=_epi_residual_prenorm,
        epi_extra=[(xs, out_tile(tm_e, D_MODEL)), vec(g1), vec(norm2_g), vec(sc2), vec(sh2)],
        out_specs=[out_tile(tm_e, D_MODEL)] * 3 + [transposed_tile(tm_e, D_MODEL)])
    mlpw_own, mlpw_landed = _exchange_wait("gather_mlp_wait", mlpw_send, mlpw_recv, mlpw_src, mlpw_land, [True] * 2, a2)
    w1_g, w2_g = [_with_own(l, o, me) for l, o in zip(mlpw_landed, mlpw_own)]
    w2_full = w2_g.reshape(D_FF, D_MODEL)
    tm_up = min(2048, n_lat_rows)
    f, f_t = _matmul("mlp_up", a2, w1_g, "nn", (n_lat_rows, D_FF, D_MODEL), (tm_up, w1_cols, D_MODEL),
                     [((n_lat_rows, D_FF), BF16), ((D_FF, n_lat_rows), BF16)], epi=lambda acc: (acc, acc.T),
                     b_spec=pl.BlockSpec((None, D_MODEL, w1_cols), lambda i, j, k: (j, 0, 0)),
                     out_specs=[out_tile(tm_up, w1_cols), transposed_tile(tm_up, w1_cols)])
    sq_relu = lambda t: jnp.square(jnp.maximum(t, 0.0))
    mlp_out, d_h2, dm2, err_sums, d_final_g8 = _matmul(
        "mlp_down", f, w2_full, "nn", (n_lat_rows, D_MODEL, D_FF), (tm_e, D_MODEL, 2048),
        [full_rows, full_rows, ((n_lat_rows, D_MODEL), BF16), sums, sums], a_fn=sq_relu, epi=_epi_residual_loss,
        epi_extra=[(h1, out_tile(tm_e, D_MODEL)), vec(g2), (tgt, out_tile(tm_e, D_MODEL)), vec(final_g[None])],
        out_specs=[out_tile(tm_e, D_MODEL)] * 3 + [sums_spec] * 2)

    (d_f,) = _matmul("mlp_down_dx", dm2, w2_full, "nt", (n_lat_rows, D_FF, D_MODEL), (tm_up, 1024, D_MODEL),
                     [((n_lat_rows, D_FF), BF16)],
                     epi=lambda acc, ft: (acc * 2.0 * jnp.maximum(ft.astype(F32), 0.0),),
                     epi_extra=[(f, out_tile(tm_up, 1024))])
    tk_dw = min(2048, n_lat_rows)
    (g_w2,) = _matmul("mlp_down_dw", f_t, dm2, "nn", (D_FF, D_MODEL, n_lat_rows), (1024, D_MODEL, tk_dw),
                      [((D_FF, D_MODEL), F32)], a_fn=sq_relu)
    (g_w1,) = _matmul("mlp_up_dw", a2_t, d_f, "nn", (D_MODEL, D_FF, n_lat_rows), (D_MODEL, w1_cols, n_lat_rows),
                      [((NDEV, D_MODEL, w1_cols), F32)],
                      out_specs=[pl.BlockSpec((None, D_MODEL, w1_cols), lambda i, j, k: (j, 0, 0))])
    mlp_send, mlp_recv, mlp_src, mlp_land, mlp_token = _exchange_start(
        "scatter_mlp_start", [g_w1, g_w2.reshape(NDEV, D_FF // NDEV, D_MODEL)], [False] * 2)
    d_h1, dm1, *sums2 = _matmul(
        "mlp_up_dx", d_f, w1_g, "nt", (n_lat_rows, D_MODEL, D_FF), (tm_e, D_MODEL, 4 * w1_cols),
        [full_rows, ((n_lat_rows, D_MODEL), BF16)] + [sums] * 4, epi=_epi_norm_bwd,
        epi_extra=[(h1, out_tile(tm_e, D_MODEL)), (d_h2, out_tile(tm_e, D_MODEL)), (mlp_out, out_tile(tm_e, D_MODEL)),
                   vec(norm2_g), vec(sc2 + mlp_token[0:1, 0:1]), vec(g1)],
        b_spec=pl.BlockSpec((4, D_MODEL, w1_cols), lambda i, j, k: (k, 0, 0)), b_slabs=4,
        out_specs=[out_tile(tm_e, D_MODEL)] * 2 + [sums_spec] * 4)

    (d_ycat,) = _matmul("out_proj_dx", dm1, w_out_full, "nt", (n_lat_rows, D_MODEL, D_MODEL), (tm, D_MODEL, D_MODEL),
                        [((n_lat_rows, D_MODEL), F32)])
    (g_w_out,) = _matmul("out_proj_dw", ycat, dm1, "tn", (D_MODEL, D_MODEL, n_lat_rows), (D_MODEL, D_MODEL, 512),
                         [((D_MODEL, D_MODEL), F32)])

    dy, g_glu, dd8 = _glu_bwd(d_ycat, z_all, y_dir[0], y_dir[1], d_skip, glu_full, n_lat_rows)
    proj_send, proj_recv, proj_src, proj_land, proj_token = _exchange_start(
        "scatter_proj_start",
        [g_w_out.reshape(NDEV, D_MODEL // NDEV, D_MODEL), g_glu.reshape(NDEV, S5_WIDTH // NDEV, S5_WIDTH)], [False] * 2)
    perms = [p + proj_token[0:1, 0:1].astype(BF16) for p in perms]
    du, g_lam_re, g_lam_im, g_ldt, g_bt, g_cdiag = None, [], [], [], [], []
    for d in range(2):
        _, _, adj, bmat, cmat = disc[d]
        du, d_bdiag, d_cdiag, d_abar8 = _s5_scan_bwd(f"s5_scan_bwd{d}", d == 0, dy, z_all, states[d], bmat, cmat, adj,
                                                     perms[d], perms_t[d], du_other=du, d_skip=d_skip if d else None)
        d_bbar = jnp.transpose(d_bdiag.reshape(S5_BLOCKS, S5_GROUP, 2, NSTATE // S5_BLOCKS), (2, 1, 0, 3)).reshape(
            2 * S5_GROUP, NSTATE)
        d_lam8, d_bt = _s5_discretise_bwd(f"s5_disc_bwd{d}", lam_re[d], lam_im[d], ldt[d], bt_re[d], bt_im[d], d_abar8, d_bbar)
        g_lam_re.append(d_lam8[0].reshape(S5_GROUPS, S5_STATE))
        g_lam_im.append(d_lam8[1].reshape(S5_GROUPS, S5_STATE))
        g_ldt.append(d_lam8[2].reshape(S5_GROUPS, S5_STATE).sum(axis=-1))
        g_bt.append(d_bt)
        g_cdiag.append(d_cdiag)

    dhc_pad, conv_sums = _conv_bwd_norm(d_ycat, hc, conv_ln_g, conv_ln_b, n_lat_rows)
    d_v, d_gate, g_conv_w8 = _conv_bwd_taps(dhc_pad, hh_pad, z_all, conv_w_full, n_lat_rows)

    no_ctx = jnp.zeros((n_ctx_rows, CONV_WIDTH), BF16)
    dz_all = jnp.concatenate([du, jnp.concatenate([d_v, no_ctx]), jnp.concatenate([d_gate, no_ctx])], axis=1)
    (g_w_in_full,) = _matmul("in_proj_dw", a_all, dz_all, "tn", (D_MODEL, IN_COLS, n_rows), (D_MODEL, IN_COLS, tm_all),
                             [((D_MODEL, IN_COLS), F32)])
    g_w_in_parts = jnp.transpose(g_w_in_full.reshape(D_MODEL, NDEV, IN_COLS // NDEV), (1, 0, 2)).astype(BF16)
    win_send, win_recv, win_src, win_land, win_token = _exchange_start("scatter_w_in_start", [g_w_in_parts], [False])
    w_in_late = w_in_full + win_token[0:1, 0:1].astype(BF16)
    grad_x, *sums1 = _matmul(
        "in_proj_dx", dz_all, w_in_late, "nt", (n_lat_rows, D_MODEL, IN_COLS), (tm_e, D_MODEL, IN_COLS),
        [full_rows] + [sums] * 4, epi=_epi_norm_bwd,
        epi_extra=[(xs, out_tile(tm_e, D_MODEL)), (d_h1, out_tile(tm_e, D_MODEL)), (mix, out_tile(tm_e, D_MODEL)),
                   vec(norm1_g), vec(sc1)],
        out_specs=[out_tile(tm_e, D_MODEL)] + [sums_spec] * 4)
    (d_a_ctx,) = _matmul("in_proj_dx_ctx", dz_all, w_in_late, "nt", (n_ctx_rows, D_MODEL, IN_COLS),
                         (ROW_BLOCK, D_MODEL, IN_COLS), [((n_ctx_rows, D_MODEL), F32)],
                         a_spec=pl.BlockSpec((ROW_BLOCK, IN_COLS), lambda i, j, k: (i + n_lat, 0)))
    (sums1c,) = _norm_bwd("norm1_bwd_ctx", cs, d_a_ctx, 0, norm1_g, modc[1:2])

    s1, s1c, s2 = [p.sum(axis=(0, 1)) for p in sums1], sums1c.sum(axis=1), [p.sum(axis=(0, 1)) for p in sums2]
    d_mod = jnp.concatenate([s1[0], s1[1], s1[3], s2[0], s2[1], s2[3]])
    d_modc = jnp.concatenate([s1c[0], s1c[1], jnp.zeros((4 * D_MODEL,), F32)])
    (dmod_g,), _ = _exchange("gather_dmod", [jnp.stack([d_mod, d_modc])], [True])
    dmod16 = jnp.concatenate([dmod_g[:, 0], dmod_g[:, 1]])
    dmod16_loc = lax.dynamic_slice(dmod16, (0, me * ada_cols), (16, ada_cols))
    cond_bwd = jnp.concatenate([c_all, jnp.broadcast_to(c_ctx[None], (NDEV, D_MODEL))])
    g_ada_w, g_c_ctx8 = _ada_bwd(cond_bwd, dmod16_loc, ada_w[0], c_ctx[None])

    small_parts = dict(
        c_ctx=g_c_ctx8[0], ada_b=d_mod + d_modc, norm1_g=s1[2] + s1c[2],
        s5_lam_re=jnp.stack(g_lam_re), s5_lam_im=jnp.stack(g_lam_im), s5_log_dt=jnp.stack(g_ldt),
        s5_d=dd8.sum(axis=0), conv_b=conv_sums[0].sum(axis=0), conv_ln_g=conv_sums[1].sum(axis=0),
        conv_ln_b=conv_sums[2].sum(axis=0), norm2_g=s2[2], final_g=d_final_g8.sum(axis=(0, 1)))
    reduced_shapes = [(SMALL_PACKED_ROWS, D_MODEL), (2, 2 * S5_GROUP, NSTATE), (2,) + _S5_DIAG, (1,)]
    small_g = _pack_rows(
        [_pack_rows([small_parts[n] for n in SMALL], SMALL_PACKED_ROWS), jnp.stack(g_bt), jnp.stack(g_cdiag),
         (0.5 / D_MODEL * jnp.sum(err_sums)).reshape(1)], SMALL_ROWS).reshape(NDEV, SMALL_ROWS // NDEV, D_MODEL)
    g_conv_w_parts = jnp.transpose(g_conv_w8.sum(axis=1).reshape(CONV_K, NDEV, CONV_WIDTH // NDEV), (1, 0, 2))

    res = {}

    def own_chunk(src):
        return lax.dynamic_index_in_dim(src, me, 0, keepdims=False)

    def adamw_big(name, parts):
        outs = _adamw("adamw_" + name, weights[name][0], parts, mom1[name][0], mom2[name][0])
        res[name] = [o[None] for o in outs]
        return outs[0]

    sm_send, sm_recv, sm_src, sm_land, sm_token = _exchange_start("scatter_small_start", [g_conv_w_parts, small_g],
                                                                  [False] * 2)
    mlp_src, mlp_landed = _exchange_wait("scatter_mlp_wait", mlp_send, mlp_recv, mlp_src, mlp_land, [False] * 2, sm_token)
    p_w1, p_w2 = [_with_own(l, own_chunk(s), me) for l, s in zip(mlp_landed, mlp_src)]
    adamw_big("ada_w", g_ada_w[None])
    adamw_big("mlp_w1", p_w1)
    done = adamw_big("mlp_w2", p_w2)
    sm_src, sm_landed = _exchange_wait("scatter_small_wait", sm_send, sm_recv, sm_src, sm_land, [False] * 2, done)
    p_conv_w, p_small = [_with_own(l, own_chunk(s), me) for l, s in zip(sm_landed, sm_src)]
    ga_send, ga_recv, ga_src, ga_land, ga_token = _exchange_start("gather_small_start", [_sum_parts(p_small)], [True])
    proj_src, proj_landed = _exchange_wait("scatter_proj_wait", proj_send, proj_recv, proj_src, proj_land, [False] * 2,
                                           ga_token)
    p_w_out, p_glu = [_with_own(l, own_chunk(s), me) for l, s in zip(proj_landed, proj_src)]
    adamw_big("w_out", p_w_out)
    done = adamw_big("s5_w_glu", p_glu)
    win_src, win_landed = _exchange_wait("scatter_w_in_wait", win_send, win_recv, win_src, win_land, [False], done)
    adamw_big("w_in", _with_own(win_landed[0], own_chunk(win_src[0]), me))
    done = adamw_big("conv_w", p_conv_w)
    ga_own, ga_landed = _exchange_wait("gather_small_wait", ga_send, ga_recv, ga_src, ga_land, [True], done)
    small_all = _with_own(ga_landed[0], ga_own[0], me).reshape(1, SMALL_ROWS, D_MODEL)
    _, r_bt, r_cdiag, loss = _unpack_rows(small_all, reduced_shapes)
    loss = loss.reshape(())
    pack = lambda src: _pack_rows([src[n] for n in SMALL], SMALL_PACKED_ROWS)
    outs = _adamw("adamw_small", pack(weights), small_all, pack(mom1), pack(mom2))
    unpacked = [_unpack_rows(o, [weights[n].shape for n in SMALL]) for o in outs]
    for i, name in enumerate(SMALL):
        res[name] = [u[i] for u in unpacked]
    to_ghp = lambda t: jnp.transpose(t.reshape(2, S5_GROUP, S5_GROUPS, S5_STATE), (0, 2, 1, 3))[None]
    r_c = jnp.transpose(r_cdiag.reshape(2, S5_BLOCKS, S5_GROUP, 2, groups_per_block, S5_STATE), (3, 0, 1, 4, 2, 5)).reshape(
        2, 1, 2, S5_GROUPS, S5_GROUP, S5_STATE)
    swap = lambda t: jnp.swapaxes(t, -1, -2)
    for name, grad in (("s5_b_re", to_ghp(r_bt[:, :S5_GROUP])), ("s5_b_im", to_ghp(r_bt[:, S5_GROUP:]))):
        outs = _adamw_native("adamw_" + name, swap(weights[name]), grad, swap(mom1[name]), swap(mom2[name]))
        res[name] = [swap(grad), *[swap(o) for o in outs]]
    for name, grad in (("s5_c_re", r_c[0]), ("s5_c_im", -r_c[1])):
        res[name] = [grad, *_adamw_native("adamw_" + name, weights[name], grad, mom1[name], mom2[name])]

    return (loss, grad_x[None], *[res[n][0] for n in order], *[res[n][1] for n in order],
            *[res[n][2] for n in order], *[res[n][3] for n in order])
```

```python
import jax
import jax.numpy as jnp
from jax import lax
from jax.experimental import pallas as pl
from jax.experimental.pallas import tpu as pltpu

F32 = jnp.float32
BF16 = jnp.bfloat16
MESH = pl.DeviceIdType.MESH
ANY = pl.BlockSpec(memory_space=pl.ANY)

NDEV = 8
D_MODEL = 1024
GRID_W = 64
S5_WIDTH = 512
S5_GROUP = 16
S5_GROUPS = 32
S5_STATE = 64
NSTATE = S5_GROUPS * S5_STATE
CONV_WIDTH = 512
CONV_K = 31
IN_COLS = S5_WIDTH + 2 * CONV_WIDTH
D_FF = 4 * D_MODEL
EPS_RMS = 1e-6
EPS_LN = 1e-5
ADAM_LR = 0.001
ADAM_B1 = 0.9
ADAM_B2 = 0.999
ADAM_EPS = 1e-08
ADAM_WD = 0.01
ADAM_STEP = 10

SUBLANES = 8
LANES = 128
ROW_BLOCK = 256
SCAN_LANES = 512
SCAN_UNROLL = 32
SEGMENTS = SUBLANES
STEPS = ROW_BLOCK // SEGMENTS
S5_BLOCKS = 4
S5_BLOCK_WIDTH = S5_WIDTH // S5_BLOCKS
CONV_ROWS = 64
CONV_BWD_ROWS = 32
VMEM_LIMIT = 48 * 1024 * 1024
SMALL_ROWS = 320


def _params(sem=None):
    kw = dict(vmem_limit_bytes=VMEM_LIMIT)
    if sem is not None:
        kw["dimension_semantics"] = sem
    return pltpu.CompilerParams(**kw)


def _sds(shape, dtype=F32):
    return jax.ShapeDtypeStruct(tuple(shape), dtype)


def _fold8(x):
    return x.reshape(x.shape[0] // SUBLANES, SUBLANES, x.shape[1]).sum(axis=0)


def _sigmoid(x):
    return 1.0 / (1.0 + jnp.exp(-x))


def _silu(x):
    return x * _sigmoid(x)


def _dsilu(x):
    s = _sigmoid(x)
    return s * (1.0 + x * (1.0 - s))


_GELU_C = 0.7978845608028654


def _gelu(x):
    return 0.5 * x * (1.0 + jnp.tanh(_GELU_C * (x + 0.044715 * x * x * x)))


def _dgelu(x):
    t = jnp.tanh(_GELU_C * (x + 0.044715 * x * x * x))
    return 0.5 * (1.0 + t) + 0.5 * x * (1.0 - t * t) * _GELU_C * (1.0 + 3.0 * 0.044715 * x * x)


def _rms(x):
    rstd = lax.rsqrt(jnp.mean(x * x, axis=-1, keepdims=True) + EPS_RMS)
    return x * rstd, rstd


def _epi_residual_prenorm(acc, res, gate, gain, scale, shift):
    h = res + gate * acc
    xh, _ = _rms(h)
    a = (xh * gain) * (1.0 + scale) + shift
    return acc, h, a, a.T


def _epi_residual_loss(acc, res, gate, target, gain):
    h = res + gate * acc
    xh, rstd = _rms(h)
    err = xh * gain - target
    dy = err * (1.0 / h.shape[-1])
    dxh = dy * gain
    dh = rstd * (dxh - xh * jnp.mean(dxh * xh, axis=-1, keepdims=True))
    return acc, dh, dh * gate, _fold8(err * err), _fold8(dy * xh)


def _epi_norm_bwd(d_act, x, res, aux, gain, scale, gate=None):
    xh, rstd = _rms(x)
    dn = d_act * (1.0 + scale)
    dxh = dn * gain
    dx = res + rstd * (dxh - xh * jnp.mean(dxh * xh, axis=-1, keepdims=True))
    sums = (_fold8(d_act), _fold8(d_act * (xh * gain)), _fold8(dn * xh), _fold8(res * aux))
    return (dx, *sums) if gate is None else (dx, dx * gate, *sums)


def _dot(a, b, mode):
    dims = {"nn": (((1,), (0,)), ((), ())), "nt": (((1,), (1,)), ((), ())), "tn": (((0,), (0,)), ((), ()))}[mode]
    return lax.dot_general(a, b, dims, preferred_element_type=F32)


def _peers(x, y, c):
    out = []
    for k in range(1, NDEV):
        px = 1 - x if k & 4 else x
        py = 1 - y if k & 2 else y
        pc = 1 - c if k & 1 else c
        out.append(((px, py, pc), 4 * px + 2 * py + pc))
    return out


def _exchange_copies(src, land, send_sems, recv_sems, gather):
    x, y, c = lax.axis_index("x"), lax.axis_index("y"), lax.axis_index("c")
    me = 4 * x + 2 * y + c
    out = []
    for a in range(len(src)):
        for k, (peer, plin) in enumerate(_peers(x, y, c)):
            chunk = src[a] if gather[a] else src[a].at[plin]
            sems = dict(send_sem=send_sems.at[a * (NDEV - 1) + k], recv_sem=recv_sems.at[a * (NDEV - 1) + k],
                        device_id=peer, device_id_type=MESH)
            out.append((pltpu.make_async_remote_copy(src_ref=chunk, dst_ref=land[a].at[me], **sems),
                        pltpu.make_async_remote_copy(src_ref=chunk, dst_ref=land[a].at[plin], **sems)))
    return out


def _exchange(name, srcs, gather):
    n = len(srcs)
    outs = [_sds(((NDEV,) + s.shape) if g else s.shape, s.dtype) for s, g in zip(srcs, gather)]

    def body(*refs):
        src, dst, token = refs[:n], refs[n:2 * n], refs[2 * n]
        send_sems, recv_sems, local_sems = refs[2 * n + 1:]
        me = 4 * lax.axis_index("x") + 2 * lax.axis_index("y") + lax.axis_index("c")
        local = [pltpu.make_async_copy(src[a] if gather[a] else src[a].at[me], dst[a].at[me], local_sems.at[a])
                 for a in range(n)]
        for copy in local:
            copy.start()
        copies = _exchange_copies(src, dst, send_sems, recv_sems, gather)
        for copy, _ in copies:
            copy.start()
        token[...] = jnp.zeros_like(token)
        for copy, landing in copies:
            copy.wait_send()
            landing.wait_recv()
        for copy in local:
            copy.wait()

    nsem = n * (NDEV - 1)
    out = pl.pallas_call(
        body, name=name, out_shape=outs + [_sds((SUBLANES, LANES))], in_specs=[ANY] * n,
        out_specs=[ANY] * n + [pl.BlockSpec(memory_space=pltpu.VMEM)],
        scratch_shapes=[pltpu.SemaphoreType.DMA((nsem,)), pltpu.SemaphoreType.DMA((nsem,)), pltpu.SemaphoreType.DMA((n,))],
    )(*srcs)
    return out[:n], out[n]


HBM = pl.BlockSpec(memory_space=pltpu.HBM)
SEM = pl.BlockSpec(memory_space=pltpu.SEMAPHORE)
EFFECT = pltpu.SideEffectType.DATAFLOW_SIDE_EFFECTING


def _own_chunk_copies(src, land, local_sems, gather):
    me = 4 * lax.axis_index("x") + 2 * lax.axis_index("y") + lax.axis_index("c")
    return [pltpu.make_async_copy(src[a] if gather[a] else src[a].at[me], land[a].at[me], local_sems.at[a])
            for a in range(len(src))]


def _exchange_start_groups(name, groups):
    srcs = [s for g_srcs, _ in groups for s in g_srcs]
    gathers = [g for _, g_gather in groups for g in g_gather]
    lands = [lax.empty(((NDEV,) + s.shape) if g else s.shape, s.dtype) for s, g in zip(srcs, gathers)]
    n, ng = len(srcs), len(groups)

    def body(*refs):
        src, land = refs[:n], refs[n:2 * n]
        sems = refs[2 * n:2 * n + 3 * ng]
        token = refs[-1]
        first = 0
        for g, (g_srcs, g_gather) in enumerate(groups):
            last = first + len(g_srcs)
            send, recv, local = sems[3 * g:3 * g + 3]
            for copy, _ in _exchange_copies(src[first:last], land[first:last], send, recv, g_gather):
                copy.start()
            for copy in _own_chunk_copies(src[first:last], land[first:last], local, g_gather):
                copy.start()
            first = last
        token[...] = jnp.zeros_like(token)

    hbm = lambda v: pltpu.HBM(v.shape, v.dtype)
    sem_shapes = []
    for g_srcs, _ in groups:
        sem_shapes += [pltpu.SemaphoreType.DMA((len(g_srcs) * (NDEV - 1),))] * 2 + [pltpu.SemaphoreType.DMA((len(g_srcs),))]
    out = pl.pallas_call(
        body, name=name,
        out_shape=(*sem_shapes, *[hbm(v) for v in srcs], *[hbm(v) for v in lands], _sds((SUBLANES, LANES))),
        in_specs=[HBM] * (2 * n),
        out_specs=(*([SEM] * (3 * ng)), *([HBM] * (2 * n)), pl.BlockSpec(memory_space=pltpu.VMEM)),
        input_output_aliases={i: 3 * ng + i for i in range(2 * n)},
        compiler_params=pltpu.CompilerParams(has_side_effects=EFFECT),
    )(*[pltpu.with_memory_space_constraint(v, pltpu.HBM) for v in srcs + lands])
    src_out, land_out = out[3 * ng:3 * ng + n], out[3 * ng + n:3 * ng + 2 * n]
    result, first = [], 0
    for g, (g_srcs, _) in enumerate(groups):
        last = first + len(g_srcs)
        result.append(((out[3 * g], out[3 * g + 2]), out[3 * g + 1], src_out[first:last], land_out[first:last]))
        first = last
    return result, out[-1]


def _exchange_start(name, srcs, gather):
    (group,), token = _exchange_start_groups(name, [(srcs, gather)])
    return (*group, token)


def _exchange_wait(name, send_sems, recv_sems, srcs, lands, gather, after):
    n = len(srcs)

    def body(*refs):
        src, land = refs[:n], refs[n:2 * n]
        send_ref, local_ref, recv_ref = refs[2 * n:2 * n + 3]
        for copy, landing in _exchange_copies(src, land, send_ref, recv_ref, gather):
            copy.wait_send()
            landing.wait_recv()
        for copy in _own_chunk_copies(src, land, local_ref, gather):
            copy.wait()

    hbm = lambda v: pltpu.HBM(v.shape, v.dtype)
    out = pl.pallas_call(
        body, name=name, out_shape=[hbm(v) for v in list(srcs) + list(lands)],
        in_specs=[HBM] * (2 * n) + [SEM, SEM, SEM, ANY], out_specs=[HBM] * (2 * n),
        input_output_aliases={i: i for i in range(2 * n)},
        compiler_params=pltpu.CompilerParams(has_side_effects=EFFECT),
    )(*srcs, *lands, send_sems[0], send_sems[1], recv_sems, after)
    return out[:n], out[n:]


def _matmul(name, a, b, mode, mnk, tiles, outs, a_spec=None, b_spec=None, a_fn=None, a_extra=(),
            epi=None, epi_extra=(), out_specs=None, b_slabs=1):
    m_, n_, k_ = mnk
    tm, tn, tk = tiles
    nk = k_ // tk
    if a_spec is None:
        a_spec = (pl.BlockSpec((tk, tm), lambda i, j, k: (k, i)) if mode == "tn"
                  else pl.BlockSpec((tm, tk), lambda i, j, k: (i, k)))
    if b_spec is None:
        b_spec = (pl.BlockSpec((tn, tk), lambda i, j, k: (j, k)) if mode == "nt"
                  else pl.BlockSpec((tk, tn), lambda i, j, k: (k, j)))
    if out_specs is None:
        out_specs = [pl.BlockSpec((tm, tn), lambda i, j, k: (i, j)) for _ in outs]
    na, ne, no = len(a_extra), len(epi_extra), len(outs)

    def body(*refs):
        a_ref, b_ref = refs[0], refs[1]
        ax = refs[2:2 + na]
        ex = refs[2 + na:2 + na + ne]
        o = refs[2 + na + ne:2 + na + ne + no]

        def finish(res):
            res = epi(res, *[r[...] for r in ex]) if epi is not None else (res,)
            for ref, val in zip(o, res):
                ref[...] = val.astype(ref.dtype)

        at = a_ref[...]
        if a_fn is not None:
            at = a_fn(at, *[r[...] for r in ax])
        at = at.astype(BF16)
        if b_slabs == 1:
            part = _dot(at, b_ref[...].astype(BF16), mode)
        else:
            ks = tk // b_slabs
            part = _dot(at[:, 0:ks], b_ref[0].astype(BF16), mode)
            for s in range(1, b_slabs):
                part = part + _dot(at[:, s * ks:(s + 1) * ks], b_ref[s].astype(BF16), mode)
        if nk == 1:
            finish(part)
            return
        acc = refs[-1]
        k = pl.program_id(2)

        @pl.when(k == 0)
        def _():
            acc[...] = part

        @pl.when(k > 0)
        def _():
            acc[...] += part

        @pl.when(k == nk - 1)
        def _():
            finish(acc[...])

    return pl.pallas_call(
        body, name=name, grid=(m_ // tm, n_ // tn, nk),
        in_specs=[a_spec, b_spec] + [s for _, s in a_extra] + [s for _, s in epi_extra],
        out_specs=out_specs, out_shape=[_sds(s, d) for s, d in outs],
        scratch_shapes=[pltpu.VMEM((tm, tn), F32)] if nk > 1 else [],
        compiler_params=_params(("parallel", "parallel", "arbitrary")),
    )(a, b, *[x for x, _ in a_extra], *[x for x, _ in epi_extra])


def _prenorm(name, x, ctx, gain, shsc):
    n_lat = x.shape[0] // ROW_BLOCK
    n_ctx = 0 if ctx is None else ctx.shape[0] // ROW_BLOCK
    d = x.shape[1]

    def norm(src, g_ref, m_ref, o_ref):
        xv = src[...]
        xh = xv * lax.rsqrt(jnp.mean(xv * xv, axis=-1, keepdims=True) + EPS_RMS)
        o_ref[...] = ((xh * g_ref[...]) * (1.0 + m_ref[1:2, :]) + m_ref[0:1, :]).astype(o_ref.dtype)

    def body(*refs):
        if ctx is None:
            x_ref, g_ref, m_ref, o_ref = refs
            norm(x_ref, g_ref, m_ref, o_ref)
        else:
            x_ref, c_ref, g_ref, m_ref, o_ref = refs
            i = pl.program_id(0)

            @pl.when(i < n_lat)
            def _():
                norm(x_ref, g_ref, m_ref, o_ref)

            @pl.when(i >= n_lat)
            def _():
                norm(c_ref, g_ref, m_ref, o_ref)

    in_specs = [pl.BlockSpec((ROW_BLOCK, d), lambda i: (jnp.minimum(i, n_lat - 1), 0))]
    args = [x]
    if ctx is not None:
        in_specs.append(pl.BlockSpec((ROW_BLOCK, d), lambda i: (jnp.maximum(i - n_lat, 0), 0)))
        args.append(ctx)
    in_specs += [pl.BlockSpec((1, d), lambda i: (0, 0)),
                 pl.BlockSpec((None, 2, d), lambda i: (jnp.minimum(i // n_lat, 1), 0, 0))]
    args += [gain, shsc]
    return pl.pallas_call(
        body, name=name, grid=(n_lat + n_ctx,), in_specs=in_specs,
        out_specs=pl.BlockSpec((ROW_BLOCK, d), lambda i: (i, 0)),
        out_shape=_sds(((n_lat + n_ctx) * ROW_BLOCK, d), BF16),
        compiler_params=_params(("parallel",)),
    )(*args)


def _norm_bwd(name, x, d_act, d_act_row0, gain, scale, res=None, aux=None):
    rows, d = x.shape
    nb = rows // ROW_BLOCK
    has_res = res is not None

    def body(*refs):
        if has_res:
            x_ref, da_ref, g_ref, sc_ref, r_ref, aux_ref, dx_ref, sums = refs
        else:
            x_ref, da_ref, g_ref, sc_ref, sums = refs
        i = pl.program_id(0)

        @pl.when(i == 0)
        def _():
            sums[...] = jnp.zeros_like(sums)

        xv, da = x_ref[...], da_ref[...]
        rstd = lax.rsqrt(jnp.mean(xv * xv, axis=-1, keepdims=True) + EPS_RMS)
        xh = xv * rstd
        g = g_ref[...]
        dn = da * (1.0 + sc_ref[...])
        sums[0] += _fold8(da)
        sums[1] += _fold8(da * (xh * g))
        sums[2] += _fold8(dn * xh)
        if has_res:
            dxh = dn * g
            dx = rstd * (dxh - xh * jnp.mean(dxh * xh, axis=-1, keepdims=True))
            rv = r_ref[...]
            dx_ref[...] = rv + dx
            sums[3] += _fold8(rv * aux_ref[...])

    row = lambda i: (i, 0)
    vec = pl.BlockSpec((1, d), lambda i: (0, 0))
    in_specs = [pl.BlockSpec((ROW_BLOCK, d), row), pl.BlockSpec((ROW_BLOCK, d), lambda i: (i + d_act_row0, 0)), vec, vec]
    args = [x, d_act, gain, scale]
    out_shape = [_sds((4, SUBLANES, d))]
    out_specs = [pl.BlockSpec((4, SUBLANES, d), lambda i: (0, 0, 0))]
    if has_res:
        in_specs += [pl.BlockSpec((ROW_BLOCK, d), row), pl.BlockSpec((ROW_BLOCK, d), row)]
        args += [res, aux]
        out_shape = [_sds((rows, d))] + out_shape
        out_specs = [pl.BlockSpec((ROW_BLOCK, d), row)] + out_specs
    return pl.pallas_call(
        body, name=name, grid=(nb,), in_specs=in_specs, out_specs=out_specs, out_shape=out_shape,
        compiler_params=_params(("arbitrary",)),
    )(*args)


def _ada_fwd(cond16, ada_w_loc, ada_b_loc):
    cols = ada_w_loc.shape[1]

    def body(c_ref, w_ref, b_ref, o_ref):
        s = _silu(c_ref[...]).astype(BF16)
        o_ref[...] = _dot(s, w_ref[...].astype(BF16), "nn") + b_ref[...]

    return pl.pallas_call(body, name="ada_fwd", out_shape=_sds((16, cols)), compiler_params=_params())(
        cond16, ada_w_loc, ada_b_loc)


def _ada_bwd(cond16, dmod16, ada_w_loc, c_ctx_row):
    k_, cols = ada_w_loc.shape

    def body(c_ref, dm_ref, w_ref, cc_ref, gw_ref, gc_ref):
        s = _silu(c_ref[...]).astype(BF16)
        dm = dm_ref[...]
        gw_ref[...] = _dot(s, dm.astype(BF16), "tn")
        dmc = jnp.sum(dm[8:16, :], axis=0, keepdims=True)
        dmc8 = jnp.broadcast_to(dmc, (SUBLANES, cols)).astype(BF16)
        ds = _dot(dmc8, w_ref[...].astype(BF16), "nt")
        row = lax.broadcasted_iota(jnp.int32, ds.shape, 0)
        gc_ref[...] = jnp.where(row == 0, ds * _dsilu(cc_ref[...]), 0.0)

    return pl.pallas_call(body, name="ada_bwd", out_shape=[_sds((k_, cols)), _sds((SUBLANES, k_))],
                          compiler_params=_params())(cond16, dmod16, ada_w_loc, c_ctx_row)


def _cmul(a, b):
    return a[0] * b[0] - a[1] * b[1], a[0] * b[1] + a[1] * b[0]


def _disc(lam_re, lam_im, ldt):
    dt = jnp.exp(ldt)
    mag = jnp.exp(lam_re * dt)
    th = lam_im * dt
    a_re, a_im = mag * jnp.cos(th), mag * jnp.sin(th)
    den = lam_re * lam_re + lam_im * lam_im
    n_re = a_re - 1.0
    f_re = (n_re * lam_re + a_im * lam_im) / den
    f_im = (a_im * lam_re - n_re * lam_im) / den
    return dt, mag, th, a_re, a_im, den, n_re, f_re, f_im


def _block_diag_mask(shape):
    row = lax.broadcasted_iota(jnp.int32, shape, 0)
    col = lax.broadcasted_iota(jnp.int32, shape, 1)
    return lax.shift_right_logical(row, 4) == lax.shift_right_logical(col, 6)


TAB_A = 0
TAB_BIG = 1
TAB_SEG = 4
TAB_PW = 5
TAB_ROWS = TAB_PW + STEPS


def _s5_discretise(name, ascending, lam_re, lam_im, ldt, bt_re, bt_im, ct_re, ct_im):
    def write_tables(ref, pw, big, asc, sign):
        row = lax.broadcasted_iota(jnp.int32, (SUBLANES, NSTATE), 0)
        full = lambda v: jnp.broadcast_to(v, (SUBLANES, NSTATE))

        def put(t, p):
            ref[0, t] = full(p[0])
            ref[1, t] = full(sign * p[1])

        put(TAB_A, pw[0])
        for t in range(3):
            put(TAB_BIG + t, big[t])
        seg = [big[0]]
        for _ in range(SEGMENTS - 1):
            seg.append(_cmul(seg[-1], big[0]))
        seg_re = jnp.zeros((SUBLANES, NSTATE), F32)
        seg_im = jnp.zeros((SUBLANES, NSTATE), F32)
        for r in range(SEGMENTS):
            p = seg[r] if asc else seg[SEGMENTS - 1 - r]
            seg_re = jnp.where(row == r, p[0], seg_re)
            seg_im = jnp.where(row == r, sign * p[1], seg_im)
        ref[0, TAB_SEG] = seg_re
        ref[1, TAB_SEG] = seg_im
        for k in range(STEPS):
            put(TAB_PW + k, pw[k])

    def body(lr_ref, li_ref, ldt_ref, br_ref, bi_ref, cr_ref, ci_ref, bb_ref, tab_ref, adj_ref, bm_ref, cm_ref):
        _, _, _, a_re, a_im, _, _, f_re, f_im = _disc(lr_ref[...], li_ref[...], ldt_ref[...])
        bre, bim = br_ref[...], bi_ref[...]
        bb_re = f_re * bre - f_im * bim
        bb_im = f_re * bim + f_im * bre
        bb_ref[0:S5_GROUP, :] = bb_re
        bb_ref[S5_GROUP:2 * S5_GROUP, :] = bb_im
        pw = [(a_re, a_im)]
        for _ in range(STEPS - 1):
            pw.append(_cmul(pw[-1], (a_re, a_im)))
        big = [pw[STEPS - 1]]
        for _ in range(2):
            big.append(_cmul(big[-1], big[-1]))
        write_tables(tab_ref, pw, big, ascending, 1.0)
        write_tables(adj_ref, pw, big, not ascending, -1.0)
        half = NSTATE // S5_BLOCKS
        mask = _block_diag_mask((S5_BLOCK_WIDTH, half))
        tile = lambda v: jnp.broadcast_to(v[None], (S5_BLOCK_WIDTH // S5_GROUP, S5_GROUP, half)).reshape(S5_BLOCK_WIDTH, half)
        for c in range(S5_BLOCKS):
            cols = slice(c * half, (c + 1) * half)
            rows = slice(c * S5_BLOCK_WIDTH, (c + 1) * S5_BLOCK_WIDTH)
            bm_ref[c, :, 0:half] = jnp.where(mask, tile(bb_re[:, cols]), 0.0).astype(BF16)
            bm_ref[c, :, half:2 * half] = jnp.where(mask, tile(bb_im[:, cols]), 0.0).astype(BF16)
            cm_ref[c, :, 0:half] = jnp.where(mask, cr_ref[rows, :], 0.0).astype(BF16)
            cm_ref[c, :, half:2 * half] = jnp.where(mask, -ci_ref[rows, :], 0.0).astype(BF16)

    blocked = _sds((S5_BLOCKS, S5_BLOCK_WIDTH, 2 * NSTATE // S5_BLOCKS), BF16)
    return pl.pallas_call(
        body, name=name,
        out_shape=[_sds((2 * S5_GROUP, NSTATE)), _sds((2, TAB_ROWS, SUBLANES, NSTATE)),
                   _sds((2, TAB_ROWS, SUBLANES, NSTATE)), blocked, blocked],
        compiler_params=_params(),
    )(lam_re, lam_im, ldt, bt_re, bt_im, ct_re, ct_im)


def _s5_discretise_bwd(name, lam_re, lam_im, ldt, bt_re, bt_im, d_abar8, d_bbar):
    def body(lr_ref, li_ref, ldt_ref, br_ref, bi_ref, da_ref, db_ref, dl_ref, dbt_ref):
        lam_re, lam_im = lr_ref[...], li_ref[...]
        dt, _, _, a_re, a_im, den, n_re, f_re, f_im = _disc(lam_re, lam_im, ldt_ref[...])
        bre, bim = br_ref[...], bi_ref[...]
        dbr, dbi = db_ref[0:S5_GROUP, :], db_ref[S5_GROUP:2 * S5_GROUP, :]
        dbt_ref[0:S5_GROUP, :] = f_re * dbr + f_im * dbi
        dbt_ref[S5_GROUP:2 * S5_GROUP, :] = f_re * dbi - f_im * dbr
        df_re = jnp.sum(bre * dbr + bim * dbi, axis=0, keepdims=True)
        df_im = jnp.sum(bre * dbi - bim * dbr, axis=0, keepdims=True)
        da = da_ref[...]
        da_re = jnp.sum(da[:, 0:NSTATE], axis=0, keepdims=True)
        da_im = jnp.sum(da[:, NSTATE:2 * NSTATE], axis=0, keepdims=True)
        da_re = da_re + (df_re * lam_re - df_im * lam_im) / den
        da_im = da_im + (df_re * lam_im + df_im * lam_re) / den
        ff = (f_re * df_re + f_im * df_im) * 2.0 / den
        d_lr = (df_re * n_re + df_im * a_im) / den - ff * lam_re
        d_li = (df_re * a_im - df_im * n_re) / den - ff * lam_im
        d_mag_mag = da_re * a_re + da_im * a_im
        d_th = da_im * a_re - da_re * a_im
        d_lr = d_lr + d_mag_mag * dt
        d_li = d_li + d_th * dt
        d_ldt = (d_mag_mag * lam_re + d_th * lam_im) * dt
        row = lax.broadcasted_iota(jnp.int32, (SUBLANES, NSTATE), 0)
        dl_ref[...] = jnp.where(row == 0, d_lr, jnp.where(row == 1, d_li, jnp.where(row == 2, d_ldt, 0.0)))

    return pl.pallas_call(
        body, name=name, out_shape=[_sds((SUBLANES, NSTATE)), _sds((2 * S5_GROUP, NSTATE))],
        compiler_params=_params(),
    )(lam_re, lam_im, ldt, bt_re, bt_im, d_abar8, d_bbar)


def _segment_permutation(reverse_time):
    rho = jnp.arange(ROW_BLOCK)
    src = STEPS * (rho % SEGMENTS) + rho // SEGMENTS
    if reverse_time:
        src = ROW_BLOCK - 1 - src
    return (src[:, None] == jnp.arange(ROW_BLOCK)[None, :]).astype(BF16)


def _permute_rows(perm_ref, v):
    return _dot(perm_ref[...], v, "nn").astype(BF16)


def _unpermute_rows(perm_t_ref, v):
    hi = v.astype(BF16)
    lo = (v - hi.astype(F32)).astype(BF16)
    return _dot(perm_t_ref[...], hi, "nn") + _dot(perm_t_ref[...], lo, "nn")


def _unrolled_loop(step, init):
    def trip(o, state):
        for u in range(SCAN_UNROLL):
            state = step(o * SCAN_UNROLL + u, state)
        return state

    if SCAN_UNROLL == STEPS:
        return trip(0, init)
    return lax.fori_loop(0, STEPS // SCAN_UNROLL, trip, init)


def _scan_chunk(x_ref, out_ref, tab_ref, carry_re, carry_im, ascending, pair_ref=None, acc_ref=None, lane_chunks=None):
    w = SCAN_LANES
    half = NSTATE // S5_BLOCKS
    row = lax.broadcasted_iota(jnp.int32, (SUBLANES, w), 0)
    last = (SEGMENTS - 1) if ascending else 0

    def from_previous_segment(v, k, fill):
        if ascending:
            return jnp.where(row >= k, pltpu.roll(v, k, 0), fill)
        return jnp.where(row < SEGMENTS - k, pltpu.roll(v, SEGMENTS - k, 0), fill)

    def tile_rows(k):
        return pl.ds(pl.multiple_of((k if ascending else STEPS - 1 - k) * SUBLANES, SUBLANES), SUBLANES)

    for j in (range(NSTATE // w) if lane_chunks is None else lane_chunks):
        n_l = pl.ds(j * w, w)
        lane0 = (j * w // half) * 2 * half + (j * w) % half
        re_l, im_l = pl.ds(lane0, w), pl.ds(lane0 + half, w)
        tab = lambda t, n_l=n_l: (tab_ref[0, t, :, n_l], tab_ref[1, t, :, n_l])
        a_re, a_im = tab(TAB_A)

        def local_step(k, h):
            rs = tile_rows(k)
            h_re = a_re * h[0] - a_im * h[1] + x_ref[rs, re_l]
            h_im = a_re * h[1] + a_im * h[0] + x_ref[rs, im_l]
            out_ref[rs, re_l] = h_re
            out_ref[rs, im_l] = h_im
            return h_re, h_im

        zero = jnp.zeros((SUBLANES, w), F32)
        end_re, end_im = _unrolled_loop(local_step, (zero, zero))
        for t, k in ((TAB_BIG, 1), (TAB_BIG + 1, 2), (TAB_BIG + 2, 4)):
            p_re, p_im = tab(t)
            s_re, s_im = from_previous_segment(end_re, k, 0.0), from_previous_segment(end_im, k, 0.0)
            end_re, end_im = end_re + (p_re * s_re - p_im * s_im), end_im + (p_re * s_im + p_im * s_re)
        c0_re, c0_im = carry_re[:, n_l], carry_im[:, n_l]
        p_re, p_im = tab(TAB_SEG)
        end_re = end_re + (p_re * c0_re - p_im * c0_im)
        end_im = end_im + (p_re * c0_im + p_im * c0_re)
        carry_re[:, n_l] = jnp.broadcast_to(end_re[last:last + 1, :], end_re.shape)
        carry_im[:, n_l] = jnp.broadcast_to(end_im[last:last + 1, :], end_im.shape)
        in_re = from_previous_segment(end_re, 1, c0_re)
        in_im = from_previous_segment(end_im, 1, c0_im)

        def carry_step(k, st):
            rs = tile_rows(k)
            p_re, p_im = tab_ref[0, TAB_PW + k, :, n_l], tab_ref[1, TAB_PW + k, :, n_l]
            o_re = out_ref[rs, re_l] + (p_re * in_re - p_im * in_im)
            o_im = out_ref[rs, im_l] + (p_re * in_im + p_im * in_re)
            out_ref[rs, re_l] = o_re
            out_ref[rs, im_l] = o_im
            if pair_ref is None:
                return st
            s_re, s_im = pair_ref[rs, re_l], pair_ref[rs, im_l]
            return (o_re, o_im, st[2] + (st[0] * s_re + st[1] * s_im), st[3] + (st[1] * s_re - st[0] * s_im))

        if pair_ref is None:
            _unrolled_loop(carry_step, 0)
        else:
            fin = _unrolled_loop(carry_step, (in_re, in_im, zero, zero))
            acc_ref[:, n_l] += fin[2]
            acc_ref[:, pl.ds(NSTATE + j * w, w)] += fin[3]


def _scan_block_index(i, n_lat, ctx_first_then_ascending):
    if ctx_first_then_ascending:
        return jnp.where(i == 0, n_lat, i - 1)
    return jnp.where(i == 0, n_lat, n_lat - i)


def _full_spec(shape):
    return pl.BlockSpec(shape, lambda i: (0,) * len(shape))


_S5_BLOCKED = (S5_BLOCKS, S5_BLOCK_WIDTH, 2 * NSTATE // S5_BLOCKS)
_S5_TABLES = (2, TAB_ROWS, SUBLANES, NSTATE)
_S5_DIAG = (S5_BLOCKS, S5_GROUP, 2 * NSTATE // S5_BLOCKS)


def _s5_scan_fwd(name, ascending, z_all, bmat, cmat, tab, perm, perm_t, y_other=None, d_skip=None, w_glu=None):
    rows = z_all.shape[0]
    nb = rows // ROW_BLOCK
    n_lat = nb - 1
    bw, sw = S5_BLOCK_WIDTH, 2 * NSTATE // S5_BLOCKS
    gated = y_other is not None

    def body(*refs):
        u_ref, bm_ref, cm_ref, tab_ref, p_ref, pt_ref = refs[:6]
        extra = refs[6:9] if gated else ()
        s_ref, y_ref = refs[6 + len(extra):8 + len(extra)]
        bu, yp, carry_re, carry_im = refs[-4:]

        @pl.when(pl.program_id(0) == 0)
        def _():
            carry_re[...] = jnp.zeros_like(carry_re)
            carry_im[...] = jnp.zeros_like(carry_im)

        up = _permute_rows(p_ref, u_ref[...].astype(BF16))
        for c in range(S5_BLOCKS):
            bu[:, c * sw:(c + 1) * sw] = _dot(up[:, c * bw:(c + 1) * bw], bm_ref[c], "nn")
        _scan_chunk(bu, s_ref, tab_ref, carry_re, carry_im, False)
        for c in range(S5_BLOCKS):
            yp[:, c * bw:(c + 1) * bw] = _dot(s_ref[:, c * sw:(c + 1) * sw].astype(BF16), cm_ref[c], "nt")
        y = _unpermute_rows(pt_ref, yp[...])
        y_ref[...] = y
        if gated:
            y_other_ref, d_ref, w_ref = extra
            gel = _gelu(d_ref[...] * u_ref[...] + y_other_ref[...] + y)
            refs[8 + len(extra)][...] = (gel * _sigmoid(_dot(gel.astype(BF16), w_ref[...], "nn"))).astype(BF16)

    blk = lambda i: (_scan_block_index(i, n_lat, ascending), 0)
    in_specs = [pl.BlockSpec((ROW_BLOCK, S5_WIDTH), blk), _full_spec(_S5_BLOCKED), _full_spec(_S5_BLOCKED),
                _full_spec(_S5_TABLES), _full_spec((ROW_BLOCK, ROW_BLOCK)), _full_spec((ROW_BLOCK, ROW_BLOCK))]
    args = [z_all, bmat, cmat, tab, perm, perm_t]
    out_specs = [pl.BlockSpec((ROW_BLOCK, 2 * NSTATE), blk), pl.BlockSpec((ROW_BLOCK, S5_WIDTH), blk)]
    out_shape = [_sds((rows, 2 * NSTATE)), _sds((rows, S5_WIDTH))]
    if gated:
        in_specs += [pl.BlockSpec((ROW_BLOCK, S5_WIDTH), blk), _full_spec((1, S5_WIDTH)), _full_spec((S5_WIDTH, S5_WIDTH))]
        args += [y_other, d_skip, w_glu]
        out_specs.append(pl.BlockSpec((ROW_BLOCK, S5_WIDTH),
                                      lambda i: (jnp.minimum(_scan_block_index(i, n_lat, ascending), n_lat - 1), 0)))
        out_shape.append(_sds((n_lat * ROW_BLOCK, S5_WIDTH + CONV_WIDTH), BF16))
    return pl.pallas_call(
        body, name=name, grid=(nb,), in_specs=in_specs, out_specs=out_specs, out_shape=out_shape,
        scratch_shapes=[pltpu.VMEM((ROW_BLOCK, 2 * NSTATE), F32), pltpu.VMEM((ROW_BLOCK, S5_WIDTH), F32),
                        pltpu.VMEM((SUBLANES, NSTATE), F32), pltpu.VMEM((SUBLANES, NSTATE), F32)],
        compiler_params=_params(("arbitrary",)),
    )(*args)


def _s5_scan_bwd(name, ascending, dy, z_all, states, bmat, cmat, adj, perm, perm_t, du_other=None, d_skip=None):
    rows = states.shape[0]
    nb = rows // ROW_BLOCK
    n_lat = nb - 1
    bw, sw = S5_BLOCK_WIDTH, 2 * NSTATE // S5_BLOCKS
    finish = du_other is not None

    def block_index(i):
        if ascending:
            return jnp.where(i == nb - 1, n_lat, n_lat - 1 - i)
        return jnp.where(i == nb - 1, n_lat, i)

    def body(*refs):
        dy_ref, u_ref, s_ref, bm_ref, cm_ref, adj_ref, p_ref, pt_ref = refs[:8]
        extra = refs[8:10] if finish else ()
        du_ref, db_ref, dc_ref, da_ref, g, dup, db_acc, dc_acc, carry_re, carry_im = refs[8 + len(extra):]
        i = pl.program_id(0)

        @pl.when(i == 0)
        def _():
            carry_re[...] = jnp.zeros_like(carry_re)
            carry_im[...] = jnp.zeros_like(carry_im)
            da_ref[...] = jnp.zeros_like(da_ref)
            db_acc[...] = jnp.zeros_like(db_acc)
            dc_acc[...] = jnp.zeros_like(dc_acc)

        has_dy = (i < nb - 1).astype(F32)
        dyp = _permute_rows(p_ref, (dy_ref[...] * has_dy).astype(BF16))
        up = _permute_rows(p_ref, u_ref[...].astype(BF16))
        for c in range(S5_BLOCKS):
            g[:, c * sw:(c + 1) * sw] = _dot(dyp[:, c * bw:(c + 1) * bw], cm_ref[c], "nn")
            dc_acc[c] += _dot(dyp[:, c * bw:(c + 1) * bw], s_ref[:, c * sw:(c + 1) * sw].astype(BF16), "tn")
            _scan_chunk(g, g, adj_ref, carry_re, carry_im, True, pair_ref=s_ref, acc_ref=da_ref, lane_chunks=[c])
            gc = g[:, c * sw:(c + 1) * sw].astype(BF16)
            dup[:, c * bw:(c + 1) * bw] = _dot(gc, bm_ref[c], "nt")
            db_acc[c] += _dot(up[:, c * bw:(c + 1) * bw], gc, "tn")
        du = _unpermute_rows(pt_ref, dup[...])
        if finish:
            du = du + extra[0][...] + (dy_ref[...] * has_dy) * extra[1][...]
        du_ref[...] = du.astype(du_ref.dtype)

        @pl.when(i == nb - 1)
        def _():
            mask = _block_diag_mask((bw, sw // 2))
            for acc, out in ((db_acc, db_ref), (dc_acc, dc_ref)):
                for c in range(S5_BLOCKS):
                    for part in range(2):
                        cols = slice(part * (sw // 2), (part + 1) * (sw // 2))
                        kept = jnp.where(mask, acc[c, :, cols], 0.0)
                        out[c, :, cols] = kept.reshape(bw // S5_GROUP, S5_GROUP, sw // 2).sum(axis=0)

    blk = lambda i: (block_index(i), 0)
    in_specs = [pl.BlockSpec((ROW_BLOCK, S5_WIDTH), lambda i: (jnp.minimum(block_index(i), n_lat - 1), 0)),
                pl.BlockSpec((ROW_BLOCK, S5_WIDTH), blk), pl.BlockSpec((ROW_BLOCK, 2 * NSTATE), blk),
                _full_spec(_S5_BLOCKED), _full_spec(_S5_BLOCKED), _full_spec(_S5_TABLES),
                _full_spec((ROW_BLOCK, ROW_BLOCK)), _full_spec((ROW_BLOCK, ROW_BLOCK))]
    args = [dy, z_all, states, bmat, cmat, adj, perm, perm_t]
    if finish:
        in_specs += [pl.BlockSpec((ROW_BLOCK, S5_WIDTH), blk), _full_spec((1, S5_WIDTH))]
        args += [du_other, d_skip]
    return pl.pallas_call(
        body, name=name, grid=(nb,), in_specs=in_specs,
        out_specs=[pl.BlockSpec((ROW_BLOCK, S5_WIDTH), blk), _full_spec(_S5_DIAG), _full_spec(_S5_DIAG),
                   _full_spec((SUBLANES, 2 * NSTATE))],
        out_shape=[_sds((rows, S5_WIDTH), BF16 if finish else F32), _sds(_S5_DIAG), _sds(_S5_DIAG),
                   _sds((SUBLANES, 2 * NSTATE))],
        scratch_shapes=[pltpu.VMEM((ROW_BLOCK, 2 * NSTATE), F32), pltpu.VMEM((ROW_BLOCK, S5_WIDTH), F32),
                        pltpu.VMEM(_S5_BLOCKED, F32), pltpu.VMEM(_S5_BLOCKED, F32),
                        pltpu.VMEM((SUBLANES, NSTATE), F32), pltpu.VMEM((SUBLANES, NSTATE), F32)],
        compiler_params=_params(("arbitrary",)),
    )(*args)


def _latent_row_tile(n_rows):
    return 512 if n_rows % 512 == 0 else ROW_BLOCK


def _glu_bwd(d_ycat, z_all, y0, y1, d_skip, w_glu, n_rows):
    def body(do_ref, u_ref, y0_ref, y1_ref, d_ref, w_ref, dy_ref, dw_ref, dd_ref):
        @pl.when(pl.program_id(0) == 0)
        def _():
            dw_ref[...] = jnp.zeros_like(dw_ref)
            dd_ref[...] = jnp.zeros_like(dd_ref)

        u = u_ref[...]
        y = d_ref[...] * u + y0_ref[...] + y1_ref[...]
        g = _gelu(y)
        gb = g.astype(BF16)
        w = w_ref[...]
        sg = _sigmoid(_dot(gb, w, "nn"))
        do = do_ref[...]
        dt = do * g * sg * (1.0 - sg)
        dtb = dt.astype(BF16)
        dg = do * sg + _dot(dtb, w, "nt")
        dy = dg * _dgelu(y)
        dy_ref[...] = dy
        dw_ref[...] += _dot(gb, dtb, "tn")
        dd_ref[...] += _fold8(dy * u)

    rows = _latent_row_tile(n_rows)
    row = pl.BlockSpec((rows, S5_WIDTH), lambda i: (i, 0))
    sq = pl.BlockSpec((S5_WIDTH, S5_WIDTH), lambda i: (0, 0))
    return pl.pallas_call(
        body, name="glu_bwd", grid=(n_rows // rows,),
        in_specs=[row, row, row, row, pl.BlockSpec((1, S5_WIDTH), lambda i: (0, 0)), sq],
        out_specs=[row, sq, pl.BlockSpec((SUBLANES, S5_WIDTH), lambda i: (0, 0))],
        out_shape=[_sds((n_rows, S5_WIDTH)), _sds((S5_WIDTH, S5_WIDTH)), _sds((SUBLANES, S5_WIDTH))],
        compiler_params=_params(("arbitrary",)),
    )(d_ycat, z_all, y0, y1, d_skip, w_glu)


CONV_HALF = CONV_K // 2


def _conv_block(n_rows):
    blk = min(1024, n_rows)
    assert blk >= CONV_HALF * GRID_W and n_rows % blk == 0
    return blk


def _conv_gate(z_all, n_rows):
    blk = _conv_block(n_rows)
    nb = n_rows // blk

    def body(v_ref, g_ref, o_ref):
        i = pl.program_id(0)
        inside = jnp.logical_and(i >= 1, i <= nb)

        @pl.when(inside)
        def _():
            o_ref[...] = v_ref[...] * _sigmoid(g_ref[...])

        @pl.when(jnp.logical_not(inside))
        def _():
            o_ref[...] = jnp.zeros_like(o_ref)

    src = lambda col: pl.BlockSpec((blk, CONV_WIDTH), lambda i: (jnp.clip(i - 1, 0, nb - 1), col))
    return pl.pallas_call(
        body, name="conv_gate", grid=(nb + 2,), in_specs=[src(1), src(2)],
        out_specs=pl.BlockSpec((blk, CONV_WIDTH), lambda i: (i, 0)),
        out_shape=_sds(((nb + 2) * blk, CONV_WIDTH)), compiler_params=_params(("parallel",)),
    )(z_all, z_all)


def _stream_padded(pad_ref, buf, sems, blk, n_blocks):
    i = pl.program_id(0)

    def copy(b):
        rows = pl.ds(pl.multiple_of(b * blk, blk), blk)
        return pltpu.make_async_copy(pad_ref.at[rows, :], buf.at[rows, :], sems.at[b])

    @pl.when(i == 0)
    def _():
        for b in range(n_blocks):
            copy(b).start()
        copy(0).wait()
        copy(1).wait()

    copy(i + 2).wait()
    return pl.multiple_of(i * blk, blk)


def _conv_fwd(hh_pad, w, b, ln_g, ln_b, ycat, n_rows):
    blk = _conv_block(n_rows)
    nblk = n_rows // blk + 2

    def body(hh_ref, w_ref, b_ref, g_ref, lb_ref, ycat_ref, hc_ref, y_ref, win, sems):
        base = _stream_padded(hh_ref, win, sems, blk, nblk)

        def tile(t, _):
            r0 = pl.multiple_of(t * CONV_ROWS, CONV_ROWS)
            acc = jnp.zeros((CONV_ROWS, CONV_WIDTH), F32)
            for k in range(CONV_K):
                acc = acc + w_ref[k:k + 1, :] * win[pl.ds(base + r0 + blk + (k - CONV_HALF) * GRID_W, CONV_ROWS), :]
            hc = acc + b_ref[...]
            hc_ref[pl.ds(r0, CONV_ROWS), :] = hc
            mu = jnp.mean(hc, axis=-1, keepdims=True)
            xc = hc - mu
            ln = xc * lax.rsqrt(jnp.mean(xc * xc, axis=-1, keepdims=True) + EPS_LN) * g_ref[...] + lb_ref[...]
            y_ref[pl.ds(r0, CONV_ROWS), :] = _silu(ln).astype(y_ref.dtype)
            return 0

        lax.fori_loop(0, blk // CONV_ROWS, tile, 0)

    vec = pl.BlockSpec((1, CONV_WIDTH), lambda i: (0, 0))
    row = pl.BlockSpec((blk, CONV_WIDTH), lambda i: (i, 0))
    return pl.pallas_call(
        body, name="conv_fwd", grid=(n_rows // blk,),
        in_specs=[ANY, pl.BlockSpec((CONV_K, CONV_WIDTH), lambda i: (0, 0)), vec, vec, vec, ANY],
        out_specs=[row, pl.BlockSpec((blk, CONV_WIDTH), lambda i: (i, 1))],
        out_shape=[_sds((n_rows, CONV_WIDTH)), _sds(ycat.shape, ycat.dtype)], input_output_aliases={5: 1},
        scratch_shapes=[pltpu.VMEM((nblk * blk, CONV_WIDTH), F32), pltpu.SemaphoreType.DMA((nblk,))],
        compiler_params=_params(("arbitrary",)),
    )(hh_pad, w, b, ln_g, ln_b, ycat)


def _conv_bwd_norm(d_ycat, hc, ln_g, ln_b, n_rows):
    blk = _conv_block(n_rows)
    nb = n_rows // blk

    def body(dy_ref, hc_ref, g_ref, lb_ref, o_ref, sums):
        i = pl.program_id(0)

        @pl.when(i == 0)
        def _():
            sums[...] = jnp.zeros_like(sums)

        inside = jnp.logical_and(i >= 1, i <= nb)

        @pl.when(inside)
        def _():
            hcv = hc_ref[...]
            mu = jnp.mean(hcv, axis=-1, keepdims=True)
            xc = hcv - mu
            rstd = lax.rsqrt(jnp.mean(xc * xc, axis=-1, keepdims=True) + EPS_LN)
            xh = xc * rstd
            g = g_ref[...]
            dln = dy_ref[...] * _dsilu(xh * g + lb_ref[...])
            dxh = dln * g
            dhc = rstd * (dxh - jnp.mean(dxh, axis=-1, keepdims=True) - xh * jnp.mean(dxh * xh, axis=-1, keepdims=True))
            o_ref[...] = dhc
            sums[0] += _fold8(dhc)
            sums[1] += _fold8(dln * xh)
            sums[2] += _fold8(dln)

        @pl.when(jnp.logical_not(inside))
        def _():
            o_ref[...] = jnp.zeros_like(o_ref)

    vec = pl.BlockSpec((1, CONV_WIDTH), lambda i: (0, 0))
    return pl.pallas_call(
        body, name="conv_bwd_norm", grid=(nb + 2,),
        in_specs=[pl.BlockSpec((blk, CONV_WIDTH), lambda i: (jnp.clip(i - 1, 0, nb - 1), 1)),
                  pl.BlockSpec((blk, CONV_WIDTH), lambda i: (jnp.clip(i - 1, 0, nb - 1), 0)), vec, vec],
        out_specs=[pl.BlockSpec((blk, CONV_WIDTH), lambda i: (i, 0)),
                   pl.BlockSpec((3, SUBLANES, CONV_WIDTH), lambda i: (0, 0, 0))],
        out_shape=[_sds(((nb + 2) * blk, CONV_WIDTH)), _sds((3, SUBLANES, CONV_WIDTH))],
        compiler_params=_params(("arbitrary",)),
    )(d_ycat, hc, ln_g, ln_b)


def _conv_bwd_taps(dhc_pad, hh_pad, z_all, w, n_rows):
    blk = _conv_block(n_rows)
    nblk = n_rows // blk + 2

    def body(dhc_ref, hh_ref, v_ref, g_ref, w_ref, dv_ref, dg_ref, dw_ref, dwin, hwin, dsems, hsems):
        @pl.when(pl.program_id(0) == 0)
        def _():
            dw_ref[...] = jnp.zeros_like(dw_ref)

        base = _stream_padded(dhc_ref, dwin, dsems, blk, nblk)
        _stream_padded(hh_ref, hwin, hsems, blk, nblk)

        def tile(t, _):
            r0 = pl.multiple_of(t * CONV_BWD_ROWS, CONV_BWD_ROWS) + base
            dh = dwin[pl.ds(r0 + blk, CONV_BWD_ROWS), :]
            acc = jnp.zeros((CONV_BWD_ROWS, CONV_WIDTH), F32)
            for k in range(CONV_K):
                off = (k - CONV_HALF) * GRID_W
                acc = acc + w_ref[k:k + 1, :] * dwin[pl.ds(r0 + blk - off, CONV_BWD_ROWS), :]
                dw_ref[k] += _fold8(dh * hwin[pl.ds(r0 + blk + off, CONV_BWD_ROWS), :])
            rs = pl.ds(pl.multiple_of(t * CONV_BWD_ROWS, CONV_BWD_ROWS), CONV_BWD_ROWS)
            sg = _sigmoid(g_ref[rs, :])
            vv = v_ref[rs, :]
            dv_ref[rs, :] = (acc * sg).astype(dv_ref.dtype)
            dg_ref[rs, :] = (acc * vv * sg * (1.0 - sg)).astype(dg_ref.dtype)
            return 0

        lax.fori_loop(0, blk // CONV_BWD_ROWS, tile, 0)

    row = pl.BlockSpec((blk, CONV_WIDTH), lambda i: (i, 0))
    return pl.pallas_call(
        body, name="conv_bwd_taps", grid=(n_rows // blk,),
        in_specs=[ANY, ANY,
            pl.BlockSpec((blk, CONV_WIDTH), lambda i: (i, 1)), pl.BlockSpec((blk, CONV_WIDTH), lambda i: (i, 2)),
            pl.BlockSpec((CONV_K, CONV_WIDTH), lambda i: (0, 0))],
        out_specs=[row, row, pl.BlockSpec((CONV_K, SUBLANES, CONV_WIDTH), lambda i: (0, 0, 0))],
        out_shape=[_sds((n_rows, CONV_WIDTH), BF16), _sds((n_rows, CONV_WIDTH), BF16),
                   _sds((CONV_K, SUBLANES, CONV_WIDTH))],
        scratch_shapes=[pltpu.VMEM((nblk * blk, CONV_WIDTH), F32), pltpu.VMEM((nblk * blk, CONV_WIDTH), F32),
                        pltpu.SemaphoreType.DMA((nblk,)), pltpu.SemaphoreType.DMA((nblk,))],
        compiler_params=_params(("arbitrary",)),
    )(dhc_pad, hh_pad, z_all, z_all, w)


def _sum_parts(parts):
    _, r, c = parts.shape

    def body(p_ref, o_ref):
        acc = p_ref[0]
        for q in range(1, NDEV):
            acc = acc + p_ref[q]
        o_ref[...] = acc

    return pl.pallas_call(body, name="sum_parts", out_shape=_sds((r, c)), compiler_params=_params())(parts)


def _row_tile(r, c):
    best = r
    for t in (1024, 512, 256, 128, 64, 32, 16, 8):
        if r % t == 0 and t * c <= 128 * 1024:
            return t
    return best


def _adamw(name, w, gparts, m, v):
    r, c = w.shape
    np_ = gparts.shape[0]
    tr = _row_tile(r, c)

    def body(w_ref, g_ref, m_ref, v_ref, go_ref, d_ref, mo_ref, vo_ref):
        g = g_ref[0].astype(F32)
        for q in range(1, np_):
            g = g + g_ref[q].astype(F32)
        m2 = ADAM_B1 * m_ref[...] + (1.0 - ADAM_B1) * g
        v2 = ADAM_B2 * v_ref[...] + (1.0 - ADAM_B2) * jnp.square(g)
        m_hat = m2 / (1.0 - ADAM_B1 ** ADAM_STEP)
        v_hat = v2 / (1.0 - ADAM_B2 ** ADAM_STEP)
        go_ref[...] = g
        d_ref[...] = -ADAM_LR * (m_hat / (jnp.sqrt(v_hat) + ADAM_EPS) + ADAM_WD * w_ref[...])
        mo_ref[...] = m2
        vo_ref[...] = v2

    row = pl.BlockSpec((tr, c), lambda i: (i, 0))
    return pl.pallas_call(
        body, name=name, grid=(r // tr,),
        in_specs=[row, pl.BlockSpec((np_, tr, c), lambda i: (0, i, 0)), row, row],
        out_specs=[row] * 4, out_shape=[_sds((r, c))] * 4, compiler_params=_params(("parallel",)),
    )(w, gparts, m, v)


def _adamw_native(name, w, g, m, v):
    def body(w_ref, g_ref, m_ref, v_ref, d_ref, mo_ref, vo_ref):
        gv = g_ref[...]
        m2 = ADAM_B1 * m_ref[...] + (1.0 - ADAM_B1) * gv
        v2 = ADAM_B2 * v_ref[...] + (1.0 - ADAM_B2) * jnp.square(gv)
        m_hat = m2 / (1.0 - ADAM_B1 ** ADAM_STEP)
        v_hat = v2 / (1.0 - ADAM_B2 ** ADAM_STEP)
        d_ref[...] = -ADAM_LR * (m_hat / (jnp.sqrt(v_hat) + ADAM_EPS) + ADAM_WD * w_ref[...])
        mo_ref[...] = m2
        vo_ref[...] = v2

    return pl.pallas_call(body, name=name, out_shape=[_sds(w.shape)] * 3, compiler_params=_params())(w, g, m, v)


SMALL = ["c_ctx", "ada_b", "norm1_g", "s5_lam_re", "s5_lam_im", "s5_log_dt", "s5_d", "conv_b", "conv_ln_g", "conv_ln_b",
         "norm2_g", "final_g"]
SMALL_PACKED_ROWS = 24


def _pack_rows(parts, rows):
    flat = jnp.concatenate([p.reshape(-1).astype(F32) for p in parts])
    return jnp.pad(flat, (0, rows * D_MODEL - flat.shape[0])).reshape(rows, D_MODEL)


def _unpack_rows(packed, shapes):
    flat = packed.reshape(-1)
    out, off = [], 0
    for shape in shapes:
        size = 1
        for s in shape:
            size *= s
        out.append(flat[off:off + size].reshape(shape))
        off += size
    return out


def kernel(x, c, ctx, c_ctx, ada_w, ada_b, norm1_g, w_in, s5_lam_re, s5_lam_im, s5_log_dt, s5_b_re, s5_b_im, s5_c_re, s5_c_im, s5_d, s5_w_glu, conv_w, conv_b, conv_ln_g, conv_ln_b, w_out, norm2_g, mlp_w1, mlp_w2, final_g, loss_target, m_c_ctx, m_ada_w, m_ada_b, m_norm1_g, m_w_in, m_s5_lam_re, m_s5_lam_im, m_s5_log_dt, m_s5_b_re, m_s5_b_im, m_s5_c_re, m_s5_c_im, m_s5_d, m_s5_w_glu, m_conv_w, m_conv_b, m_conv_ln_g, m_conv_ln_b, m_w_out, m_norm2_g, m_mlp_w1, m_mlp_w2, m_final_g, v_c_ctx, v_ada_w, v_ada_b, v_norm1_g, v_w_in, v_s5_lam_re, v_s5_lam_im, v_s5_log_dt, v_s5_b_re, v_s5_b_im, v_s5_c_re, v_s5_c_im, v_s5_d, v_s5_w_glu, v_conv_w, v_conv_b, v_conv_ln_g, v_conv_ln_b, v_w_out, v_norm2_g, v_mlp_w1, v_mlp_w2, v_final_g):
    weights = dict(c_ctx=c_ctx, ada_w=ada_w, ada_b=ada_b, norm1_g=norm1_g, w_in=w_in, s5_lam_re=s5_lam_re, s5_lam_im=s5_lam_im, s5_log_dt=s5_log_dt, s5_b_re=s5_b_re, s5_b_im=s5_b_im, s5_c_re=s5_c_re, s5_c_im=s5_c_im, s5_d=s5_d, s5_w_glu=s5_w_glu, conv_w=conv_w, conv_b=conv_b, conv_ln_g=conv_ln_g, conv_ln_b=conv_ln_b, w_out=w_out, norm2_g=norm2_g, mlp_w1=mlp_w1, mlp_w2=mlp_w2, final_g=final_g)
    mom1 = dict(c_ctx=m_c_ctx, ada_w=m_ada_w, ada_b=m_ada_b, norm1_g=m_norm1_g, w_in=m_w_in, s5_lam_re=m_s5_lam_re, s5_lam_im=m_s5_lam_im, s5_log_dt=m_s5_log_dt, s5_b_re=m_s5_b_re, s5_b_im=m_s5_b_im, s5_c_re=m_s5_c_re, s5_c_im=m_s5_c_im, s5_d=m_s5_d, s5_w_glu=m_s5_w_glu, conv_w=m_conv_w, conv_b=m_conv_b, conv_ln_g=m_conv_ln_g, conv_ln_b=m_conv_ln_b, w_out=m_w_out, norm2_g=m_norm2_g, mlp_w1=m_mlp_w1, mlp_w2=m_mlp_w2, final_g=m_final_g)
    mom2 = dict(c_ctx=v_c_ctx, ada_w=v_ada_w, ada_b=v_ada_b, norm1_g=v_norm1_g, w_in=v_w_in, s5_lam_re=v_s5_lam_re, s5_lam_im=v_s5_lam_im, s5_log_dt=v_s5_log_dt, s5_b_re=v_s5_b_re, s5_b_im=v_s5_b_im, s5_c_re=v_s5_c_re, s5_c_im=v_s5_c_im, s5_d=v_s5_d, s5_w_glu=v_s5_w_glu, conv_w=v_conv_w, conv_b=v_conv_b, conv_ln_g=v_conv_ln_g, conv_ln_b=v_conv_ln_b, w_out=v_w_out, norm2_g=v_norm2_g, mlp_w1=v_mlp_w1, mlp_w2=v_mlp_w2, final_g=v_final_g)
    order = list(weights)

    me = 4 * lax.axis_index("x") + 2 * lax.axis_index("y") + lax.axis_index("c")
    xs, cs, tgt = x[0], ctx[0], loss_target[0]
    n_lat_rows, n_ctx_rows = xs.shape[0], cs.shape[0]
    n_rows = n_lat_rows + n_ctx_rows
    n_lat = n_lat_rows // ROW_BLOCK
    ada_cols = ada_w.shape[2]

    (c_all,), _ = _exchange("gather_c", [c], [True])
    c_all = c_all.reshape(NDEV, D_MODEL)

    cond_fwd = jnp.concatenate([c_all, c_ctx[None], jnp.zeros((7, D_MODEL), F32)])
    ada_b_loc = lax.dynamic_slice(ada_b, (0, me * ada_cols), (1, ada_cols))
    (mod_g,), mod_token = _exchange("gather_mod", [_ada_fwd(cond_fwd, ada_w[0], ada_b_loc)], [True])
    weight_groups, weights_token = _exchange_start_groups("gather_weights_start", [
        ([w_in[0].astype(BF16)], [True]),
        ([s5_w_glu[0].astype(BF16), conv_w[0] + mod_token[0:1, 0:1], w_out[0].astype(BF16)], [True] * 3),
        ([mlp_w1[0].astype(BF16), mlp_w2[0].astype(BF16)], [True] * 2)])
    (wi_send, wi_recv, wi_src, wi_land), (mixer_send, mixer_recv, mixer_src, mixer_land), \
        (mlpw_send, mlpw_recv, mlpw_src, mlpw_land) = weight_groups
    mod_rows = jnp.transpose(mod_g, (1, 0, 2)).reshape(16, 6 * D_MODEL) + weights_token[0:1, 0:1]
    mod = lax.dynamic_slice(mod_rows, (me, 0), (1, 6 * D_MODEL)).reshape(6, D_MODEL)
    modc = mod_rows[8, :2 * D_MODEL].reshape(2, D_MODEL)
    sh1, sc1, g1, sh2, sc2, g2 = [mod[i:i + 1] for i in range(6)]

    lam_re, lam_im = s5_lam_re[0].reshape(2, 1, NSTATE), s5_lam_im[0].reshape(2, 1, NSTATE)
    ldt = jnp.repeat(s5_log_dt[0], S5_STATE, axis=-1).reshape(2, 1, NSTATE)
    bt_re = jnp.transpose(s5_b_re[0], (0, 3, 1, 2)).reshape(2, S5_GROUP, NSTATE)
    bt_im = jnp.transpose(s5_b_im[0], (0, 3, 1, 2)).reshape(2, S5_GROUP, NSTATE)
    groups_per_block = S5_GROUPS // S5_BLOCKS
    ct_re = jnp.tile(s5_c_re[0].reshape(2, S5_WIDTH, S5_STATE), (1, 1, groups_per_block))
    ct_im = jnp.tile(s5_c_im[0].reshape(2, S5_WIDTH, S5_STATE), (1, 1, groups_per_block))
    d_skip = s5_d[0].reshape(1, S5_WIDTH)
    perms = [_segment_permutation(reverse_time=(d == 0)) for d in range(2)]
    perms_t = [p.T for p in perms]
    disc = [_s5_discretise(f"s5_disc{d}", False, lam_re[d], lam_im[d], ldt[d], bt_re[d], bt_im[d], ct_re[d], ct_im[d])
            for d in range(2)]

    a_all = _prenorm("prenorm1", xs, cs, norm1_g, jnp.stack([mod[0:2], modc]))
    before_w_in = a_all[0:SUBLANES, 0:LANES].astype(F32) + disc[0][0][0:SUBLANES, 0:LANES] + disc[1][0][0:SUBLANES, 0:LANES]
    wi_own, wi_landed = _exchange_wait("gather_w_in_wait", wi_send, wi_recv, wi_src, wi_land, [True], before_w_in)
    w_in_full = jnp.transpose(wi_landed[0], (1, 0, 2)).reshape(D_MODEL, IN_COLS)
    tm_all = 1088 if n_rows % 1088 == 0 else ROW_BLOCK
    (z_all,) = _matmul("in_proj", a_all, w_in_full, "nn", (n_rows, IN_COLS, D_MODEL), (tm_all, IN_COLS, D_MODEL),
                       [((n_rows, IN_COLS), F32)])

    _, tab, _, bmat, cmat = disc[0]
    s0, y0 = _s5_scan_fwd("s5_scan_fwd0", True, z_all, bmat, cmat, tab, perms[0], perms_t[0])
    mixer_own, mixer_landed = _exchange_wait("gather_mixer_wait", mixer_send, mixer_recv, mixer_src, mixer_land,
                                             [True] * 3, y0)
    glu_g, conv_w_g, w_out_g = mixer_landed
    glu_full = glu_g.reshape(S5_WIDTH, S5_WIDTH)
    conv_w_full = jnp.transpose(conv_w_g, (1, 0, 2)).reshape(CONV_K, CONV_WIDTH)
    w_out_full = w_out_g.reshape(D_MODEL, D_MODEL)
    _, tab, _, bmat, cmat = disc[1]
    s1, y1, ycat = _s5_scan_fwd("s5_scan_fwd1", False, z_all, bmat, cmat, tab, perms[1], perms_t[1],
                                y_other=y0, d_skip=d_skip, w_glu=glu_full)
    states, y_dir = [s0, s1], [y0, y1]

    hh_pad = _conv_gate(z_all, n_lat_rows)
    hc, ycat = _conv_fwd(hh_pad, conv_w_full, conv_b, conv_ln_g, conv_ln_b, ycat, n_lat_rows)

    tm = min(1024, n_lat_rows)
    tm_e = min(512, n_lat_rows)
    w1_cols = D_FF // NDEV
    row_vec = lambda tn: pl.BlockSpec((1, tn), lambda i, j, k: (0, j))
    out_tile = lambda t_m, t_n: pl.BlockSpec((t_m, t_n), lambda i, j, k: (i, j))
    full_rows = ((n_lat_rows, D_MODEL), F32)
    sums = ((n_lat_rows // tm_e, SUBLANES, D_MODEL), F32)
    sums_spec = pl.BlockSpec((None, SUBLANES, D_MODEL), lambda i, j, k: (i, 0, 0))
    vec = lambda v: (v, row_vec(D_MODEL))
    transposed_tile = lambda t_m, t_n: pl.BlockSpec((t_n, t_m), lambda i, j, k: (j, i))
    mix, h1, a2, a2_t = _matmul(
        "out_proj", ycat, w_out_full, "nn", (n_lat_rows, D_MODEL, D_MODEL), (tm_e, D_MODEL, D_MODEL),
        [full_rows, full_rows, ((n_lat_rows, D_MODEL), BF16), ((D_MODEL, n_lat_rows), BF16)],
        epi=_epi_residual_prenorm,
        epi_extra=[(xs, out_tile(tm_e, D_MODEL)), vec(g1), vec(norm2_g), vec(sc2), vec(sh2)],
        out_specs=[out_tile(tm_e, D_MODEL)] * 3 + [transposed_tile(tm_e, D_MODEL)])
    mlpw_own, mlpw_landed = _exchange_wait("gather_mlp_wait", mlpw_send, mlpw_recv, mlpw_src, mlpw_land, [True] * 2, a2)
    w1_g, w2_g = mlpw_landed
    w2_full = w2_g.reshape(D_FF, D_MODEL)
    tm_up = min(2048, n_lat_rows)
    f, f_t = _matmul("mlp_up", a2, w1_g, "nn", (n_lat_rows, D_FF, D_MODEL), (tm_up, w1_cols, D_MODEL),
                     [((n_lat_rows, D_FF), BF16), ((D_FF, n_lat_rows), BF16)], epi=lambda acc: (acc, acc.T),
                     b_spec=pl.BlockSpec((None, D_MODEL, w1_cols), lambda i, j, k: (j, 0, 0)),
                     out_specs=[out_tile(tm_up, w1_cols), transposed_tile(tm_up, w1_cols)])
    sq_relu = lambda t: jnp.square(jnp.maximum(t, 0.0))
    mlp_out, d_h2, dm2, err_sums, d_final_g8 = _matmul(
        "mlp_down", f, w2_full, "nn", (n_lat_rows, D_MODEL, D_FF), (tm_e, D_MODEL, 2048),
        [full_rows, full_rows, ((n_lat_rows, D_MODEL), BF16), sums, sums], a_fn=sq_relu, epi=_epi_residual_loss,
        epi_extra=[(h1, out_tile(tm_e, D_MODEL)), vec(g2), (tgt, out_tile(tm_e, D_MODEL)), vec(final_g[None])],
        out_specs=[out_tile(tm_e, D_MODEL)] * 3 + [sums_spec] * 2)

    (d_f,) = _matmul("mlp_down_dx", dm2, w2_full, "nt", (n_lat_rows, D_FF, D_MODEL), (tm_up, 1024, D_MODEL),
                     [((n_lat_rows, D_FF), BF16)],
                     epi=lambda acc, ft: (acc * 2.0 * jnp.maximum(ft.astype(F32), 0.0),),
                     epi_extra=[(f, out_tile(tm_up, 1024))])
    tk_dw = min(2048, n_lat_rows)
    (g_w2,) = _matmul("mlp_down_dw", f_t, dm2, "nn", (D_FF, D_MODEL, n_lat_rows), (1024, D_MODEL, tk_dw),
                      [((D_FF, D_MODEL), F32)], a_fn=sq_relu)
    (g_w1,) = _matmul("mlp_up_dw", a2_t, d_f, "nn", (D_MODEL, D_FF, n_lat_rows), (D_MODEL, w1_cols, n_lat_rows),
                      [((NDEV, D_MODEL, w1_cols), F32)],
                      out_specs=[pl.BlockSpec((None, D_MODEL, w1_cols), lambda i, j, k: (j, 0, 0))])
    mlp_send, mlp_recv, mlp_src, mlp_land, mlp_token = _exchange_start(
        "scatter_mlp_start", [g_w1, g_w2.reshape(NDEV, D_FF // NDEV, D_MODEL)], [False] * 2)
    d_h1, dm1, *sums2 = _matmul(
        "mlp_up_dx", d_f, w1_g, "nt", (n_lat_rows, D_MODEL, D_FF), (tm_e, D_MODEL, 4 * w1_cols),
        [full_rows, ((n_lat_rows, D_MODEL), BF16)] + [sums] * 4, epi=_epi_norm_bwd,
        epi_extra=[(h1, out_tile(tm_e, D_MODEL)), (d_h2, out_tile(tm_e, D_MODEL)), (mlp_out, out_tile(tm_e, D_MODEL)),
                   vec(norm2_g), vec(sc2 + mlp_token[0:1, 0:1]), vec(g1)],
        b_spec=pl.BlockSpec((4, D_MODEL, w1_cols), lambda i, j, k: (k, 0, 0)), b_slabs=4,
        out_specs=[out_tile(tm_e, D_MODEL)] * 2 + [sums_spec] * 4)

    (d_ycat,) = _matmul("out_proj_dx", dm1, w_out_full, "nt", (n_lat_rows, D_MODEL, D_MODEL), (tm, D_MODEL, D_MODEL),
                        [((n_lat_rows, D_MODEL), F32)])
    (g_w_out,) = _matmul("out_proj_dw", ycat, dm1, "tn", (D_MODEL, D_MODEL, n_lat_rows), (D_MODEL, D_MODEL, 512),
                         [((D_MODEL, D_MODEL), F32)])

    dy, g_glu, dd8 = _glu_bwd(d_ycat, z_all, y_dir[0], y_dir[1], d_skip, glu_full, n_lat_rows)
    proj_send, proj_recv, proj_src, proj_land, proj_token = _exchange_start(
        "scatter_proj_start",
        [g_w_out.reshape(NDEV, D_MODEL // NDEV, D_MODEL), g_glu.reshape(NDEV, S5_WIDTH // NDEV, S5_WIDTH)], [False] * 2)
    perms = [p + proj_token[0:1, 0:1].astype(BF16) for p in perms]
    du, g_lam_re, g_lam_im, g_ldt, g_bt, g_cdiag = None, [], [], [], [], []
    for d in range(2):
        _, _, adj, bmat, cmat = disc[d]
        du, d_bdiag, d_cdiag, d_abar8 = _s5_scan_bwd(f"s5_scan_bwd{d}", d == 0, dy, z_all, states[d], bmat, cmat, adj,
                                                     perms[d], perms_t[d], du_other=du, d_skip=d_skip if d else None)
        d_bbar = jnp.transpose(d_bdiag.reshape(S5_BLOCKS, S5_GROUP, 2, NSTATE // S5_BLOCKS), (2, 1, 0, 3)).reshape(
            2 * S5_GROUP, NSTATE)
        d_lam8, d_bt = _s5_discretise_bwd(f"s5_disc_bwd{d}", lam_re[d], lam_im[d], ldt[d], bt_re[d], bt_im[d], d_abar8, d_bbar)
        g_lam_re.append(d_lam8[0].reshape(S5_GROUPS, S5_STATE))
        g_lam_im.append(d_lam8[1].reshape(S5_GROUPS, S5_STATE))
        g_ldt.append(d_lam8[2].reshape(S5_GROUPS, S5_STATE).sum(axis=-1))
        g_bt.append(d_bt)
        g_cdiag.append(d_cdiag)

    dhc_pad, conv_sums = _conv_bwd_norm(d_ycat, hc, conv_ln_g, conv_ln_b, n_lat_rows)
    d_v, d_gate, g_conv_w8 = _conv_bwd_taps(dhc_pad, hh_pad, z_all, conv_w_full, n_lat_rows)

    no_ctx = jnp.zeros((n_ctx_rows, CONV_WIDTH), BF16)
    dz_all = jnp.concatenate([du, jnp.concatenate([d_v, no_ctx]), jnp.concatenate([d_gate, no_ctx])], axis=1)
    (g_w_in_full,) = _matmul("in_proj_dw", a_all, dz_all, "tn", (D_MODEL, IN_COLS, n_rows), (D_MODEL, IN_COLS, tm_all),
                             [((D_MODEL, IN_COLS), F32)])
    g_w_in_parts = jnp.transpose(g_w_in_full.reshape(D_MODEL, NDEV, IN_COLS // NDEV), (1, 0, 2)).astype(BF16)
    win_send, win_recv, win_src, win_land, win_token = _exchange_start("scatter_w_in_start", [g_w_in_parts], [False])
    w_in_late = w_in_full + win_token[0:1, 0:1].astype(BF16)
    grad_x, *sums1 = _matmul(
        "in_proj_dx", dz_all, w_in_late, "nt", (n_lat_rows, D_MODEL, IN_COLS), (tm_e, D_MODEL, IN_COLS),
        [full_rows] + [sums] * 4, epi=_epi_norm_bwd,
        epi_extra=[(xs, out_tile(tm_e, D_MODEL)), (d_h1, out_tile(tm_e, D_MODEL)), (mix, out_tile(tm_e, D_MODEL)),
                   vec(norm1_g), vec(sc1)],
        out_specs=[out_tile(tm_e, D_MODEL)] + [sums_spec] * 4)
    (d_a_ctx,) = _matmul("in_proj_dx_ctx", dz_all, w_in_late, "nt", (n_ctx_rows, D_MODEL, IN_COLS),
                         (ROW_BLOCK, D_MODEL, IN_COLS), [((n_ctx_rows, D_MODEL), F32)],
                         a_spec=pl.BlockSpec((ROW_BLOCK, IN_COLS), lambda i, j, k: (i + n_lat, 0)))
    (sums1c,) = _norm_bwd("norm1_bwd_ctx", cs, d_a_ctx, 0, norm1_g, modc[1:2])

    s1, s1c, s2 = [p.sum(axis=(0, 1)) for p in sums1], sums1c.sum(axis=1), [p.sum(axis=(0, 1)) for p in sums2]
    d_mod = jnp.concatenate([s1[0], s1[1], s1[3], s2[0], s2[1], s2[3]])
    d_modc = jnp.concatenate([s1c[0], s1c[1], jnp.zeros((4 * D_MODEL,), F32)])
    (dmod_g,), _ = _exchange("gather_dmod", [jnp.stack([d_mod, d_modc])], [True])
    dmod16 = jnp.concatenate([dmod_g[:, 0], dmod_g[:, 1]])
    dmod16_loc = lax.dynamic_slice(dmod16, (0, me * ada_cols), (16, ada_cols))
    cond_bwd = jnp.concatenate([c_all, jnp.broadcast_to(c_ctx[None], (NDEV, D_MODEL))])
    g_ada_w, g_c_ctx8 = _ada_bwd(cond_bwd, dmod16_loc, ada_w[0], c_ctx[None])

    small_parts = dict(
        c_ctx=g_c_ctx8[0], ada_b=d_mod + d_modc, norm1_g=s1[2] + s1c[2],
        s5_lam_re=jnp.stack(g_lam_re), s5_lam_im=jnp.stack(g_lam_im), s5_log_dt=jnp.stack(g_ldt),
        s5_d=dd8.sum(axis=0), conv_b=conv_sums[0].sum(axis=0), conv_ln_g=conv_sums[1].sum(axis=0),
        conv_ln_b=conv_sums[2].sum(axis=0), norm2_g=s2[2], final_g=d_final_g8.sum(axis=(0, 1)))
    reduced_shapes = [(SMALL_PACKED_ROWS, D_MODEL), (2, 2 * S5_GROUP, NSTATE), (2,) + _S5_DIAG, (1,)]
    small_g = _pack_rows(
        [_pack_rows([small_parts[n] for n in SMALL], SMALL_PACKED_ROWS), jnp.stack(g_bt), jnp.stack(g_cdiag),
         (0.5 / D_MODEL * jnp.sum(err_sums)).reshape(1)], SMALL_ROWS).reshape(NDEV, SMALL_ROWS // NDEV, D_MODEL)
    g_conv_w_parts = jnp.transpose(g_conv_w8.sum(axis=1).reshape(CONV_K, NDEV, CONV_WIDTH // NDEV), (1, 0, 2))

    res = {}

    def adamw_big(name, parts):
        outs = _adamw("adamw_" + name, weights[name][0], parts, mom1[name][0], mom2[name][0])
        res[name] = [o[None] for o in outs]
        return outs[0]

    sm_send, sm_recv, sm_src, sm_land, sm_token = _exchange_start("scatter_small_start", [g_conv_w_parts, small_g],
                                                                  [False] * 2)
    _, (p_w1, p_w2) = _exchange_wait("scatter_mlp_wait", mlp_send, mlp_recv, mlp_src, mlp_land, [False] * 2, sm_token)
    adamw_big("ada_w", g_ada_w[None])
    adamw_big("mlp_w1", p_w1)
    done = adamw_big("mlp_w2", p_w2)
    _, (p_conv_w, p_small) = _exchange_wait("scatter_small_wait", sm_send, sm_recv, sm_src, sm_land, [False] * 2, done)
    ga_send, ga_recv, ga_src, ga_land, ga_token = _exchange_start("gather_small_start", [_sum_parts(p_small)], [True])
    _, (p_w_out, p_glu) = _exchange_wait("scatter_proj_wait", proj_send, proj_recv, proj_src, proj_land, [False] * 2,
                                         ga_token)
    adamw_big("w_out", p_w_out)
    done = adamw_big("s5_w_glu", p_glu)
    _, (p_w_in,) = _exchange_wait("scatter_w_in_wait", win_send, win_recv, win_src, win_land, [False], done)
    adamw_big("w_in", p_w_in)
    done = adamw_big("conv_w", p_conv_w)
    _, (small_all,) = _exchange_wait("gather_small_wait", ga_send, ga_recv, ga_src, ga_land, [True], done)
    small_all = small_all.reshape(1, SMALL_ROWS, D_MODEL)
    _, r_bt, r_cdiag, loss = _unpack_rows(small_all, reduced_shapes)
    loss = loss.reshape(())
    pack = lambda src: _pack_rows([src[n] for n in SMALL], SMALL_PACKED_ROWS)
    outs = _adamw("adamw_small", pack(weights), small_all, pack(mom1), pack(mom2))
    unpacked = [_unpack_rows(o, [weights[n].shape for n in SMALL]) for o in outs]
    for i, name in enumerate(SMALL):
        res[name] = [u[i] for u in unpacked]
    to_ghp = lambda t: jnp.transpose(t.reshape(2, S5_GROUP, S5_GROUPS, S5_STATE), (0, 2, 1, 3))[None]
    r_c = jnp.transpose(r_cdiag.reshape(2, S5_BLOCKS, S5_GROUP, 2, groups_per_block, S5_STATE), (3, 0, 1, 4, 2, 5)).reshape(
        2, 1, 2, S5_GROUPS, S5_GROUP, S5_STATE)
    swap = lambda t: jnp.swapaxes(t, -1, -2)
    for name, grad in (("s5_b_re", to_ghp(r_bt[:, :S5_GROUP])), ("s5_b_im", to_ghp(r_bt[:, S5_GROUP:]))):
        outs = _adamw_native("adamw_" + name, swap(weights[name]), grad, swap(mom1[name]), swap(mom2[name]))
        res[name] = [swap(grad), *[swap(o) for o in outs]]
    for name, grad in (("s5_c_re", r_c[0]), ("s5_c_im", -r_c[1])):
        res[name] = [grad, *_adamw_native("adamw_" + name, weights[name], grad, mom1[name], mom2[name])]

    return (loss, grad_x[None], *[res[n][0] for n in order], *[res[n][1] for n in order],
            *[res[n][2] for n in order], *[res[n][3] for n in order])
```

```python
import jax
import jax.numpy as jnp
from jax import lax
from jax.experimental import pallas as pl
from jax.experimental.pallas import tpu as pltpu

F32 = jnp.float32
BF16 = jnp.bfloat16
MESH = pl.DeviceIdType.MESH
ANY = pl.BlockSpec(memory_space=pl.ANY)

NDEV = 8
D_MODEL = 1024
GRID_W = 64
S5_WIDTH = 512
S5_GROUP = 16
S5_GROUPS = 32
S5_STATE = 64
NSTATE = S5_GROUPS * S5_STATE
CONV_WIDTH = 512
CONV_K = 31
IN_COLS = S5_WIDTH + 2 * CONV_WIDTH
D_FF = 4 * D_MODEL
EPS_RMS = 1e-6
EPS_LN = 1e-5
ADAM_LR = 0.001
ADAM_B1 = 0.9
ADAM_B2 = 0.999
ADAM_EPS = 1e-08
ADAM_WD = 0.01
ADAM_STEP = 10

SUBLANES = 8
LANES = 128
ROW_BLOCK = 256
SCAN_LANES = 512
SCAN_UNROLL = 32
SEGMENTS = SUBLANES
STEPS = ROW_BLOCK // SEGMENTS
S5_BLOCKS = 4
S5_BLOCK_WIDTH = S5_WIDTH // S5_BLOCKS
CONV_ROWS = 64
CONV_BWD_ROWS = 32
VMEM_LIMIT = 48 * 1024 * 1024
SMALL_ROWS = 320


def _params(sem=None):
    kw = dict(vmem_limit_bytes=VMEM_LIMIT)
    if sem is not None:
        kw["dimension_semantics"] = sem
    return pltpu.CompilerParams(**kw)


def _sds(shape, dtype=F32):
    return jax.ShapeDtypeStruct(tuple(shape), dtype)


def _fold8(x):
    return x.reshape(x.shape[0] // SUBLANES, SUBLANES, x.shape[1]).sum(axis=0)


def _sigmoid(x):
    return 1.0 / (1.0 + jnp.exp(-x))


def _silu(x):
    return x * _sigmoid(x)


def _dsilu(x):
    s = _sigmoid(x)
    return s * (1.0 + x * (1.0 - s))


_GELU_C = 0.7978845608028654


def _gelu(x):
    return 0.5 * x * (1.0 + jnp.tanh(_GELU_C * (x + 0.044715 * x * x * x)))


def _dgelu(x):
    t = jnp.tanh(_GELU_C * (x + 0.044715 * x * x * x))
    return 0.5 * (1.0 + t) + 0.5 * x * (1.0 - t * t) * _GELU_C * (1.0 + 3.0 * 0.044715 * x * x)


def _rms(x):
    rstd = lax.rsqrt(jnp.mean(x * x, axis=-1, keepdims=True) + EPS_RMS)
    return x * rstd, rstd


def _epi_residual_prenorm(acc, res, gate, gain, scale, shift):
    h = res + gate * acc
    xh, _ = _rms(h)
    a = (xh * gain) * (1.0 + scale) + shift
    return acc, h, a, a.T


def _epi_residual_loss(acc, res, gate, target, gain):
    h = res + gate * acc
    xh, rstd = _rms(h)
    err = xh * gain - target
    dy = err * (1.0 / h.shape[-1])
    dxh = dy * gain
    dh = rstd * (dxh - xh * jnp.mean(dxh * xh, axis=-1, keepdims=True))
    return acc, dh, dh * gate, _fold8(err * err), _fold8(dy * xh)


def _epi_norm_bwd(d_act, x, res, aux, gain, scale, gate=None):
    xh, rstd = _rms(x)
    dn = d_act * (1.0 + scale)
    dxh = dn * gain
    dx = res + rstd * (dxh - xh * jnp.mean(dxh * xh, axis=-1, keepdims=True))
    sums = (_fold8(d_act), _fold8(d_act * (xh * gain)), _fold8(dn * xh), _fold8(res * aux))
    return (dx, *sums) if gate is None else (dx, dx * gate, *sums)


def _dot(a, b, mode):
    dims = {"nn": (((1,), (0,)), ((), ())), "nt": (((1,), (1,)), ((), ())), "tn": (((0,), (0,)), ((), ()))}[mode]
    return lax.dot_general(a, b, dims, preferred_element_type=F32)


def _peers(x, y, c):
    out = []
    for k in range(1, NDEV):
        px = 1 - x if k & 4 else x
        py = 1 - y if k & 2 else y
        pc = 1 - c if k & 1 else c
        out.append(((px, py, pc), 4 * px + 2 * py + pc))
    return out


def _exchange_copies(src, land, send_sems, recv_sems, gather):
    x, y, c = lax.axis_index("x"), lax.axis_index("y"), lax.axis_index("c")
    me = 4 * x + 2 * y + c
    out = []
    for a in range(len(src)):
        for k, (peer, plin) in enumerate(_peers(x, y, c)):
            chunk = src[a] if gather[a] else src[a].at[plin]
            sems = dict(send_sem=send_sems.at[a * (NDEV - 1) + k], recv_sem=recv_sems.at[a * (NDEV - 1) + k],
                        device_id=peer, device_id_type=MESH)
            out.append((pltpu.make_async_remote_copy(src_ref=chunk, dst_ref=land[a].at[me], **sems),
                        pltpu.make_async_remote_copy(src_ref=chunk, dst_ref=land[a].at[plin], **sems)))
    return out


def _exchange(name, srcs, gather):
    n = len(srcs)
    outs = [_sds(((NDEV,) + s.shape) if g else s.shape, s.dtype) for s, g in zip(srcs, gather)]

    def body(*refs):
        src, dst, token = refs[:n], refs[n:2 * n], refs[2 * n]
        send_sems, recv_sems, local_sems = refs[2 * n + 1:]
        me = 4 * lax.axis_index("x") + 2 * lax.axis_index("y") + lax.axis_index("c")
        local = [pltpu.make_async_copy(src[a] if gather[a] else src[a].at[me], dst[a].at[me], local_sems.at[a])
                 for a in range(n)]
        for copy in local:
            copy.start()
        copies = _exchange_copies(src, dst, send_sems, recv_sems, gather)
        for copy, _ in copies:
            copy.start()
        token[...] = jnp.zeros_like(token)
        for copy, landing in copies:
            copy.wait_send()
            landing.wait_recv()
        for copy in local:
            copy.wait()

    nsem = n * (NDEV - 1)
    out = pl.pallas_call(
        body, name=name, out_shape=outs + [_sds((SUBLANES, LANES))], in_specs=[ANY] * n,
        out_specs=[ANY] * n + [pl.BlockSpec(memory_space=pltpu.VMEM)],
        scratch_shapes=[pltpu.SemaphoreType.DMA((nsem,)), pltpu.SemaphoreType.DMA((nsem,)), pltpu.SemaphoreType.DMA((n,))],
    )(*srcs)
    return out[:n], out[n]


HBM = pl.BlockSpec(memory_space=pltpu.HBM)
SEM = pl.BlockSpec(memory_space=pltpu.SEMAPHORE)
EFFECT = pltpu.SideEffectType.DATAFLOW_SIDE_EFFECTING


def _own_chunk_copies(src, land, local_sems, gather):
    me = 4 * lax.axis_index("x") + 2 * lax.axis_index("y") + lax.axis_index("c")
    return [pltpu.make_async_copy(src[a] if gather[a] else src[a].at[me], land[a].at[me], local_sems.at[a])
            for a in range(len(src))]


def _exchange_start_groups(name, groups):
    srcs = [s for g_srcs, _ in groups for s in g_srcs]
    gathers = [g for _, g_gather in groups for g in g_gather]
    lands = [lax.empty(((NDEV,) + s.shape) if g else s.shape, s.dtype) for s, g in zip(srcs, gathers)]
    n, ng = len(srcs), len(groups)

    def body(*refs):
        src, land = refs[:n], refs[n:2 * n]
        sems = refs[2 * n:2 * n + 3 * ng]
        token = refs[-1]
        first = 0
        for g, (g_srcs, g_gather) in enumerate(groups):
            last = first + len(g_srcs)
            send, recv, local = sems[3 * g:3 * g + 3]
            for copy, _ in _exchange_copies(src[first:last], land[first:last], send, recv, g_gather):
                copy.start()
            for copy in _own_chunk_copies(src[first:last], land[first:last], local, g_gather):
                copy.start()
            first = last
        token[...] = jnp.zeros_like(token)

    hbm = lambda v: pltpu.HBM(v.shape, v.dtype)
    sem_shapes = []
    for g_srcs, _ in groups:
        sem_shapes += [pltpu.SemaphoreType.DMA((len(g_srcs) * (NDEV - 1),))] * 2 + [pltpu.SemaphoreType.DMA((len(g_srcs),))]
    out = pl.pallas_call(
        body, name=name,
        out_shape=(*sem_shapes, *[hbm(v) for v in srcs], *[hbm(v) for v in lands], _sds((SUBLANES, LANES))),
        in_specs=[HBM] * (2 * n),
        out_specs=(*([SEM] * (3 * ng)), *([HBM] * (2 * n)), pl.BlockSpec(memory_space=pltpu.VMEM)),
        input_output_aliases={i: 3 * ng + i for i in range(2 * n)},
        compiler_params=pltpu.CompilerParams(has_side_effects=EFFECT),
    )(*[pltpu.with_memory_space_constraint(v, pltpu.HBM) for v in srcs + lands])
    src_out, land_out = out[3 * ng:3 * ng + n], out[3 * ng + n:3 * ng + 2 * n]
    result, first = [], 0
    for g, (g_srcs, _) in enumerate(groups):
        last = first + len(g_srcs)
        result.append(((out[3 * g], out[3 * g + 2]), out[3 * g + 1], src_out[first:last], land_out[first:last]))
        first = last
    return result, out[-1]


def _exchange_start(name, srcs, gather):
    (group,), token = _exchange_start_groups(name, [(srcs, gather)])
    return (*group, token)


def _exchange_wait(name, send_sems, recv_sems, srcs, lands, gather, after):
    n = len(srcs)

    def body(*refs):
        src, land = refs[:n], refs[n:2 * n]
        send_ref, local_ref, recv_ref = refs[2 * n:2 * n + 3]
        for copy, landing in _exchange_copies(src, land, send_ref, recv_ref, gather):
            copy.wait_send()
            landing.wait_recv()
        for copy in _own_chunk_copies(src, land, local_ref, gather):
            copy.wait()

    hbm = lambda v: pltpu.HBM(v.shape, v.dtype)
    out = pl.pallas_call(
        body, name=name, out_shape=[hbm(v) for v in list(srcs) + list(lands)],
        in_specs=[HBM] * (2 * n) + [SEM, SEM, SEM, ANY], out_specs=[HBM] * (2 * n),
        input_output_aliases={i: i for i in range(2 * n)},
        compiler_params=pltpu.CompilerParams(has_side_effects=EFFECT),
    )(*srcs, *lands, send_sems[0], send_sems[1], recv_sems, after)
    return out[:n], out[n:]


def _matmul(name, a, b, mode, mnk, tiles, outs, a_spec=None, b_spec=None, a_fn=None, a_extra=(),
            epi=None, epi_extra=(), out_specs=None, b_slabs=1):
    m_, n_, k_ = mnk
    tm, tn, tk = tiles
    nk = k_ // tk
    if a_spec is None:
        a_spec = (pl.BlockSpec((tk, tm), lambda i, j, k: (k, i)) if mode == "tn"
                  else pl.BlockSpec((tm, tk), lambda i, j, k: (i, k)))
    if b_spec is None:
        b_spec = (pl.BlockSpec((tn, tk), lambda i, j, k: (j, k)) if mode == "nt"
                  else pl.BlockSpec((tk, tn), lambda i, j, k: (k, j)))
    if out_specs is None:
        out_specs = [pl.BlockSpec((tm, tn), lambda i, j, k: (i, j)) for _ in outs]
    na, ne, no = len(a_extra), len(epi_extra), len(outs)

    def body(*refs):
        a_ref, b_ref = refs[0], refs[1]
        ax = refs[2:2 + na]
        ex = refs[2 + na:2 + na + ne]
        o = refs[2 + na + ne:2 + na + ne + no]

        def finish(res):
            res = epi(res, *[r[...] for r in ex]) if epi is not None else (res,)
            for ref, val in zip(o, res):
                ref[...] = val.astype(ref.dtype)

        at = a_ref[...]
        if a_fn is not None:
            at = a_fn(at, *[r[...] for r in ax])
        at = at.astype(BF16)
        if b_slabs == 1:
            part = _dot(at, b_ref[...].astype(BF16), mode)
        else:
            ks = tk // b_slabs
            part = _dot(at[:, 0:ks], b_ref[0].astype(BF16), mode)
            for s in range(1, b_slabs):
                part = part + _dot(at[:, s * ks:(s + 1) * ks], b_ref[s].astype(BF16), mode)
        if nk == 1:
            finish(part)
            return
        acc = refs[-1]
        k = pl.program_id(2)

        @pl.when(k == 0)
        def _():
            acc[...] = part

        @pl.when(k > 0)
        def _():
            acc[...] += part

        @pl.when(k == nk - 1)
        def _():
            finish(acc[...])

    return pl.pallas_call(
        body, name=name, grid=(m_ // tm, n_ // tn, nk),
        in_specs=[a_spec, b_spec] + [s for _, s in a_extra] + [s for _, s in epi_extra],
        out_specs=out_specs, out_shape=[_sds(s, d) for s, d in outs],
        scratch_shapes=[pltpu.VMEM((tm, tn), F32)] if nk > 1 else [],
        compiler_params=_params(("parallel", "parallel", "arbitrary")),
    )(a, b, *[x for x, _ in a_extra], *[x for x, _ in epi_extra])


def _prenorm(name, x, ctx, gain, shsc):
    n_lat = x.shape[0] // ROW_BLOCK
    n_ctx = 0 if ctx is None else ctx.shape[0] // ROW_BLOCK
    d = x.shape[1]

    def norm(src, g_ref, m_ref, o_ref):
        xv = src[...]
        xh = xv * lax.rsqrt(jnp.mean(xv * xv, axis=-1, keepdims=True) + EPS_RMS)
        o_ref[...] = ((xh * g_ref[...]) * (1.0 + m_ref[1:2, :]) + m_ref[0:1, :]).astype(o_ref.dtype)

    def body(*refs):
        if ctx is None:
            x_ref, g_ref, m_ref, o_ref = refs
            norm(x_ref, g_ref, m_ref, o_ref)
        else:
            x_ref, c_ref, g_ref, m_ref, o_ref = refs
            i = pl.program_id(0)

            @pl.when(i < n_lat)
            def _():
                norm(x_ref, g_ref, m_ref, o_ref)

            @pl.when(i >= n_lat)
            def _():
                norm(c_ref, g_ref, m_ref, o_ref)

    in_specs = [pl.BlockSpec((ROW_BLOCK, d), lambda i: (jnp.minimum(i, n_lat - 1), 0))]
    args = [x]
    if ctx is not None:
        in_specs.append(pl.BlockSpec((ROW_BLOCK, d), lambda i: (jnp.maximum(i - n_lat, 0), 0)))
        args.append(ctx)
    in_specs += [pl.BlockSpec((1, d), lambda i: (0, 0)),
                 pl.BlockSpec((None, 2, d), lambda i: (jnp.minimum(i // n_lat, 1), 0, 0))]
    args += [gain, shsc]
    return pl.pallas_call(
        body, name=name, grid=(n_lat + n_ctx,), in_specs=in_specs,
        out_specs=pl.BlockSpec((ROW_BLOCK, d), lambda i: (i, 0)),
        out_shape=_sds(((n_lat + n_ctx) * ROW_BLOCK, d), BF16),
        compiler_params=_params(("parallel",)),
    )(*args)


def _norm_bwd(name, x, d_act, d_act_row0, gain, scale, res=None, aux=None):
    rows, d = x.shape
    nb = rows // ROW_BLOCK
    has_res = res is not None

    def body(*refs):
        if has_res:
            x_ref, da_ref, g_ref, sc_ref, r_ref, aux_ref, dx_ref, sums = refs
        else:
            x_ref, da_ref, g_ref, sc_ref, sums = refs
        i = pl.program_id(0)

        @pl.when(i == 0)
        def _():
            sums[...] = jnp.zeros_like(sums)

        xv, da = x_ref[...], da_ref[...]
        rstd = lax.rsqrt(jnp.mean(xv * xv, axis=-1, keepdims=True) + EPS_RMS)
        xh = xv * rstd
        g = g_ref[...]
        dn = da * (1.0 + sc_ref[...])
        sums[0] += _fold8(da)
        sums[1] += _fold8(da * (xh * g))
        sums[2] += _fold8(dn * xh)
        if has_res:
            dxh = dn * g
            dx = rstd * (dxh - xh * jnp.mean(dxh * xh, axis=-1, keepdims=True))
            rv = r_ref[...]
            dx_ref[...] = rv + dx
            sums[3] += _fold8(rv * aux_ref[...])

    row = lambda i: (i, 0)
    vec = pl.BlockSpec((1, d), lambda i: (0, 0))
    in_specs = [pl.BlockSpec((ROW_BLOCK, d), row), pl.BlockSpec((ROW_BLOCK, d), lambda i: (i + d_act_row0, 0)), vec, vec]
    args = [x, d_act, gain, scale]
    out_shape = [_sds((4, SUBLANES, d))]
    out_specs = [pl.BlockSpec((4, SUBLANES, d), lambda i: (0, 0, 0))]
    if has_res:
        in_specs += [pl.BlockSpec((ROW_BLOCK, d), row), pl.BlockSpec((ROW_BLOCK, d), row)]
        args += [res, aux]
        out_shape = [_sds((rows, d))] + out_shape
        out_specs = [pl.BlockSpec((ROW_BLOCK, d), row)] + out_specs
    return pl.pallas_call(
        body, name=name, grid=(nb,), in_specs=in_specs, out_specs=out_specs, out_shape=out_shape,
        compiler_params=_params(("arbitrary",)),
    )(*args)


def _ada_fwd(cond16, ada_w_loc, ada_b_loc):
    cols = ada_w_loc.shape[1]

    def body(c_ref, w_ref, b_ref, o_ref):
        s = _silu(c_ref[...]).astype(BF16)
        o_ref[...] = _dot(s, w_ref[...].astype(BF16), "nn") + b_ref[...]

    return pl.pallas_call(body, name="ada_fwd", out_shape=_sds((16, cols)), compiler_params=_params())(
        cond16, ada_w_loc, ada_b_loc)


def _ada_bwd(cond16, dmod16, ada_w_loc, c_ctx_row):
    k_, cols = ada_w_loc.shape

    def body(c_ref, dm_ref, w_ref, cc_ref, gw_ref, gc_ref):
        s = _silu(c_ref[...]).astype(BF16)
        dm = dm_ref[...]
        gw_ref[...] = _dot(s, dm.astype(BF16), "tn")
        dmc = jnp.sum(dm[8:16, :], axis=0, keepdims=True)
        dmc8 = jnp.broadcast_to(dmc, (SUBLANES, cols)).astype(BF16)
        ds = _dot(dmc8, w_ref[...].astype(BF16), "nt")
        row = lax.broadcasted_iota(jnp.int32, ds.shape, 0)
        gc_ref[...] = jnp.where(row == 0, ds * _dsilu(cc_ref[...]), 0.0)

    return pl.pallas_call(body, name="ada_bwd", out_shape=[_sds((k_, cols)), _sds((SUBLANES, k_))],
                          compiler_params=_params())(cond16, dmod16, ada_w_loc, c_ctx_row)


def _cmul(a, b):
    return a[0] * b[0] - a[1] * b[1], a[0] * b[1] + a[1] * b[0]


def _disc(lam_re, lam_im, ldt):
    dt = jnp.exp(ldt)
    mag = jnp.exp(lam_re * dt)
    th = lam_im * dt
    a_re, a_im = mag * jnp.cos(th), mag * jnp.sin(th)
    den = lam_re * lam_re + lam_im * lam_im
    n_re = a_re - 1.0
    f_re = (n_re * lam_re + a_im * lam_im) / den
    f_im = (a_im * lam_re - n_re * lam_im) / den
    return dt, mag, th, a_re, a_im, den, n_re, f_re, f_im


def _block_diag_mask(shape):
    row = lax.broadcasted_iota(jnp.int32, shape, 0)
    col = lax.broadcasted_iota(jnp.int32, shape, 1)
    return lax.shift_right_logical(row, 4) == lax.shift_right_logical(col, 6)


TAB_A = 0
TAB_BIG = 1
TAB_SEG = 4
TAB_PW = 5
TAB_ROWS = TAB_PW + STEPS


def _s5_discretise(name, ascending, lam_re, lam_im, ldt, bt_re, bt_im, ct_re, ct_im):
    def write_tables(ref, pw, big, asc, sign):
        row = lax.broadcasted_iota(jnp.int32, (SUBLANES, NSTATE), 0)
        full = lambda v: jnp.broadcast_to(v, (SUBLANES, NSTATE))

        def put(t, p):
            ref[0, t] = full(p[0])
            ref[1, t] = full(sign * p[1])

        put(TAB_A, pw[0])
        for t in range(3):
            put(TAB_BIG + t, big[t])
        seg = [big[0]]
        for _ in range(SEGMENTS - 1):
            seg.append(_cmul(seg[-1], big[0]))
        seg_re = jnp.zeros((SUBLANES, NSTATE), F32)
        seg_im = jnp.zeros((SUBLANES, NSTATE), F32)
        for r in range(SEGMENTS):
            p = seg[r] if asc else seg[SEGMENTS - 1 - r]
            seg_re = jnp.where(row == r, p[0], seg_re)
            seg_im = jnp.where(row == r, sign * p[1], seg_im)
        ref[0, TAB_SEG] = seg_re
        ref[1, TAB_SEG] = seg_im
        for k in range(STEPS):
            put(TAB_PW + k, pw[k])

    def body(lr_ref, li_ref, ldt_ref, br_ref, bi_ref, cr_ref, ci_ref, bb_ref, tab_ref, adj_ref, bm_ref, cm_ref):
        _, _, _, a_re, a_im, _, _, f_re, f_im = _disc(lr_ref[...], li_ref[...], ldt_ref[...])
        bre, bim = br_ref[...], bi_ref[...]
        bb_re = f_re * bre - f_im * bim
        bb_im = f_re * bim + f_im * bre
        bb_ref[0:S5_GROUP, :] = bb_re
        bb_ref[S5_GROUP:2 * S5_GROUP, :] = bb_im
        pw = [(a_re, a_im)]
        for _ in range(STEPS - 1):
            pw.append(_cmul(pw[-1], (a_re, a_im)))
        big = [pw[STEPS - 1]]
        for _ in range(2):
            big.append(_cmul(big[-1], big[-1]))
        write_tables(tab_ref, pw, big, ascending, 1.0)
        write_tables(adj_ref, pw, big, not ascending, -1.0)
        half = NSTATE // S5_BLOCKS
        mask = _block_diag_mask((S5_BLOCK_WIDTH, half))
        tile = lambda v: jnp.broadcast_to(v[None], (S5_BLOCK_WIDTH // S5_GROUP, S5_GROUP, half)).reshape(S5_BLOCK_WIDTH, half)
        for c in range(S5_BLOCKS):
            cols = slice(c * half, (c + 1) * half)
            rows = slice(c * S5_BLOCK_WIDTH, (c + 1) * S5_BLOCK_WIDTH)
            bm_ref[c, :, 0:half] = jnp.where(mask, tile(bb_re[:, cols]), 0.0).astype(BF16)
            bm_ref[c, :, half:2 * half] = jnp.where(mask, tile(bb_im[:, cols]), 0.0).astype(BF16)
            cm_ref[c, :, 0:half] = jnp.where(mask, cr_ref[rows, :], 0.0).astype(BF16)
            cm_ref[c, :, half:2 * half] = jnp.where(mask, -ci_ref[rows, :], 0.0).astype(BF16)

    blocked = _sds((S5_BLOCKS, S5_BLOCK_WIDTH, 2 * NSTATE // S5_BLOCKS), BF16)
    return pl.pallas_call(
        body, name=name,
        out_shape=[_sds((2 * S5_GROUP, NSTATE)), _sds((2, TAB_ROWS, SUBLANES, NSTATE)),
                   _sds((2, TAB_ROWS, SUBLANES, NSTATE)), blocked, blocked],
        compiler_params=_params(),
    )(lam_re, lam_im, ldt, bt_re, bt_im, ct_re, ct_im)


def _s5_discretise_bwd(name, lam_re, lam_im, ldt, bt_re, bt_im, d_abar8, d_bbar):
    def body(lr_ref, li_ref, ldt_ref, br_ref, bi_ref, da_ref, db_ref, dl_ref, dbt_ref):
        lam_re, lam_im = lr_ref[...], li_ref[...]
        dt, _, _, a_re, a_im, den, n_re, f_re, f_im = _disc(lam_re, lam_im, ldt_ref[...])
        bre, bim = br_ref[...], bi_ref[...]
        dbr, dbi = db_ref[0:S5_GROUP, :], db_ref[S5_GROUP:2 * S5_GROUP, :]
        dbt_ref[0:S5_GROUP, :] = f_re * dbr + f_im * dbi
        dbt_ref[S5_GROUP:2 * S5_GROUP, :] = f_re * dbi - f_im * dbr
        df_re = jnp.sum(bre * dbr + bim * dbi, axis=0, keepdims=True)
        df_im = jnp.sum(bre * dbi - bim * dbr, axis=0, keepdims=True)
        da = da_ref[...]
        da_re = jnp.sum(da[:, 0:NSTATE], axis=0, keepdims=True)
        da_im = jnp.sum(da[:, NSTATE:2 * NSTATE], axis=0, keepdims=True)
        da_re = da_re + (df_re * lam_re - df_im * lam_im) / den
        da_im = da_im + (df_re * lam_im + df_im * lam_re) / den
        ff = (f_re * df_re + f_im * df_im) * 2.0 / den
        d_lr = (df_re * n_re + df_im * a_im) / den - ff * lam_re
        d_li = (df_re * a_im - df_im * n_re) / den - ff * lam_im
        d_mag_mag = da_re * a_re + da_im * a_im
        d_th = da_im * a_re - da_re * a_im
        d_lr = d_lr + d_mag_mag * dt
        d_li = d_li + d_th * dt
        d_ldt = (d_mag_mag * lam_re + d_th * lam_im) * dt
        row = lax.broadcasted_iota(jnp.int32, (SUBLANES, NSTATE), 0)
        dl_ref[...] = jnp.where(row == 0, d_lr, jnp.where(row == 1, d_li, jnp.where(row == 2, d_ldt, 0.0)))

    return pl.pallas_call(
        body, name=name, out_shape=[_sds((SUBLANES, NSTATE)), _sds((2 * S5_GROUP, NSTATE))],
        compiler_params=_params(),
    )(lam_re, lam_im, ldt, bt_re, bt_im, d_abar8, d_bbar)


def _segment_permutation(reverse_time):
    rho = jnp.arange(ROW_BLOCK)
    src = STEPS * (rho % SEGMENTS) + rho // SEGMENTS
    if reverse_time:
        src = ROW_BLOCK - 1 - src
    return (src[:, None] == jnp.arange(ROW_BLOCK)[None, :]).astype(BF16)


def _permute_rows(perm_ref, v):
    return _dot(perm_ref[...], v, "nn").astype(BF16)


def _unpermute_rows(perm_t_ref, v):
    hi = v.astype(BF16)
    lo = (v - hi.astype(F32)).astype(BF16)
    return _dot(perm_t_ref[...], hi, "nn") + _dot(perm_t_ref[...], lo, "nn")


def _unrolled_loop(step, init):
    def trip(o, state):
        for u in range(SCAN_UNROLL):
            state = step(o * SCAN_UNROLL + u, state)
        return state

    if SCAN_UNROLL == STEPS:
        return trip(0, init)
    return lax.fori_loop(0, STEPS // SCAN_UNROLL, trip, init)


def _scan_chunk(x_ref, out_ref, tab_ref, carry_re, carry_im, ascending, pair_ref=None, acc_ref=None, lane_chunks=None):
    w = SCAN_LANES
    half = NSTATE // S5_BLOCKS
    row = lax.broadcasted_iota(jnp.int32, (SUBLANES, w), 0)
    last = (SEGMENTS - 1) if ascending else 0

    def from_previous_segment(v, k, fill):
        if ascending:
            return jnp.where(row >= k, pltpu.roll(v, k, 0), fill)
        return jnp.where(row < SEGMENTS - k, pltpu.roll(v, SEGMENTS - k, 0), fill)

    def tile_rows(k):
        return pl.ds(pl.multiple_of((k if ascending else STEPS - 1 - k) * SUBLANES, SUBLANES), SUBLANES)

    for j in (range(NSTATE // w) if lane_chunks is None else lane_chunks):
        n_l = pl.ds(j * w, w)
        lane0 = (j * w // half) * 2 * half + (j * w) % half
        re_l, im_l = pl.ds(lane0, w), pl.ds(lane0 + half, w)
        tab = lambda t, n_l=n_l: (tab_ref[0, t, :, n_l], tab_ref[1, t, :, n_l])
        a_re, a_im = tab(TAB_A)

        def local_step(k, h):
            rs = tile_rows(k)
            h_re = a_re * h[0] - a_im * h[1] + x_ref[rs, re_l]
            h_im = a_re * h[1] + a_im * h[0] + x_ref[rs, im_l]
            out_ref[rs, re_l] = h_re
            out_ref[rs, im_l] = h_im
            return h_re, h_im

        zero = jnp.zeros((SUBLANES, w), F32)
        end_re, end_im = _unrolled_loop(local_step, (zero, zero))
        for t, k in ((TAB_BIG, 1), (TAB_BIG + 1, 2), (TAB_BIG + 2, 4)):
            p_re, p_im = tab(t)
            s_re, s_im = from_previous_segment(end_re, k, 0.0), from_previous_segment(end_im, k, 0.0)
            end_re, end_im = end_re + (p_re * s_re - p_im * s_im), end_im + (p_re * s_im + p_im * s_re)
        c0_re, c0_im = carry_re[:, n_l], carry_im[:, n_l]
        p_re, p_im = tab(TAB_SEG)
        end_re = end_re + (p_re * c0_re - p_im * c0_im)
        end_im = end_im + (p_re * c0_im + p_im * c0_re)
        carry_re[:, n_l] = jnp.broadcast_to(end_re[last:last + 1, :], end_re.shape)
        carry_im[:, n_l] = jnp.broadcast_to(end_im[last:last + 1, :], end_im.shape)
        in_re = from_previous_segment(end_re, 1, c0_re)
        in_im = from_previous_segment(end_im, 1, c0_im)

        def carry_step(k, st):
            rs = tile_rows(k)
            p_re, p_im = tab_ref[0, TAB_PW + k, :, n_l], tab_ref[1, TAB_PW + k, :, n_l]
            o_re = out_ref[rs, re_l] + (p_re * in_re - p_im * in_im)
            o_im = out_ref[rs, im_l] + (p_re * in_im + p_im * in_re)
            out_ref[rs, re_l] = o_re
            out_ref[rs, im_l] = o_im
            if pair_ref is None:
                return st
            s_re, s_im = pair_ref[rs, re_l], pair_ref[rs, im_l]
            return (o_re, o_im, st[2] + (st[0] * s_re + st[1] * s_im), st[3] + (st[1] * s_re - st[0] * s_im))

        if pair_ref is None:
            _unrolled_loop(carry_step, 0)
        else:
            fin = _unrolled_loop(carry_step, (in_re, in_im, zero, zero))
            acc_ref[:, n_l] += fin[2]
            acc_ref[:, pl.ds(NSTATE + j * w, w)] += fin[3]


def _scan_block_index(i, n_lat, ctx_first_then_ascending):
    if ctx_first_then_ascending:
        return jnp.where(i == 0, n_lat, i - 1)
    return jnp.where(i == 0, n_lat, n_lat - i)


def _full_spec(shape):
    return pl.BlockSpec(shape, lambda i: (0,) * len(shape))


_S5_BLOCKED = (S5_BLOCKS, S5_BLOCK_WIDTH, 2 * NSTATE // S5_BLOCKS)
_S5_TABLES = (2, TAB_ROWS, SUBLANES, NSTATE)
_S5_DIAG = (S5_BLOCKS, S5_GROUP, 2 * NSTATE // S5_BLOCKS)


def _s5_scan_fwd(name, ascending, z_all, bmat, cmat, tab, perm, perm_t, y_other=None, d_skip=None, w_glu=None):
    rows = z_all.shape[0]
    nb = rows // ROW_BLOCK
    n_lat = nb - 1
    bw, sw = S5_BLOCK_WIDTH, 2 * NSTATE // S5_BLOCKS
    gated = y_other is not None

    def body(*refs):
        u_ref, bm_ref, cm_ref, tab_ref, p_ref, pt_ref = refs[:6]
        extra = refs[6:9] if gated else ()
        s_ref, y_ref = refs[6 + len(extra):8 + len(extra)]
        bu, yp, carry_re, carry_im = refs[-4:]

        @pl.when(pl.program_id(0) == 0)
        def _():
            carry_re[...] = jnp.zeros_like(carry_re)
            carry_im[...] = jnp.zeros_like(carry_im)

        up = _permute_rows(p_ref, u_ref[...].astype(BF16))
        for c in range(S5_BLOCKS):
            bu[:, c * sw:(c + 1) * sw] = _dot(up[:, c * bw:(c + 1) * bw], bm_ref[c], "nn")
        _scan_chunk(bu, s_ref, tab_ref, carry_re, carry_im, False)
        for c in range(S5_BLOCKS):
            yp[:, c * bw:(c + 1) * bw] = _dot(s_ref[:, c * sw:(c + 1) * sw].astype(BF16), cm_ref[c], "nt")
        y = _unpermute_rows(pt_ref, yp[...])
        y_ref[...] = y
        if gated:
            y_other_ref, d_ref, w_ref = extra
            gel = _gelu(d_ref[...] * u_ref[...] + y_other_ref[...] + y)
            refs[8 + len(extra)][...] = (gel * _sigmoid(_dot(gel.astype(BF16), w_ref[...], "nn"))).astype(BF16)

    blk = lambda i: (_scan_block_index(i, n_lat, ascending), 0)
    in_specs = [pl.BlockSpec((ROW_BLOCK, S5_WIDTH), blk), _full_spec(_S5_BLOCKED), _full_spec(_S5_BLOCKED),
                _full_spec(_S5_TABLES), _full_spec((ROW_BLOCK, ROW_BLOCK)), _full_spec((ROW_BLOCK, ROW_BLOCK))]
    args = [z_all, bmat, cmat, tab, perm, perm_t]
    out_specs = [pl.BlockSpec((ROW_BLOCK, 2 * NSTATE), blk), pl.BlockSpec((ROW_BLOCK, S5_WIDTH), blk)]
    out_shape = [_sds((rows, 2 * NSTATE)), _sds((rows, S5_WIDTH))]
    if gated:
        in_specs += [pl.BlockSpec((ROW_BLOCK, S5_WIDTH), blk), _full_spec((1, S5_WIDTH)), _full_spec((S5_WIDTH, S5_WIDTH))]
        args += [y_other, d_skip, w_glu]
        out_specs.append(pl.BlockSpec((ROW_BLOCK, S5_WIDTH),
                                      lambda i: (jnp.minimum(_scan_block_index(i, n_lat, ascending), n_lat - 1), 0)))
        out_shape.append(_sds((n_lat * ROW_BLOCK, S5_WIDTH + CONV_WIDTH), BF16))
    return pl.pallas_call(
        body, name=name, grid=(nb,), in_specs=in_specs, out_specs=out_specs, out_shape=out_shape,
        scratch_shapes=[pltpu.VMEM((ROW_BLOCK, 2 * NSTATE), F32), pltpu.VMEM((ROW_BLOCK, S5_WIDTH), F32),
                        pltpu.VMEM((SUBLANES, NSTATE), F32), pltpu.VMEM((SUBLANES, NSTATE), F32)],
        compiler_params=_params(("arbitrary",)),
    )(*args)


def _s5_scan_bwd(name, ascending, dy, z_all, states, bmat, cmat, adj, perm, perm_t, du_other=None, d_skip=None):
    rows = states.shape[0]
    nb = rows // ROW_BLOCK
    n_lat = nb - 1
    bw, sw = S5_BLOCK_WIDTH, 2 * NSTATE // S5_BLOCKS
    finish = du_other is not None

    def block_index(i):
        if ascending:
            return jnp.where(i == nb - 1, n_lat, n_lat - 1 - i)
        return jnp.where(i == nb - 1, n_lat, i)

    def body(*refs):
        dy_ref, u_ref, s_ref, bm_ref, cm_ref, adj_ref, p_ref, pt_ref = refs[:8]
        extra = refs[8:10] if finish else ()
        du_ref, db_ref, dc_ref, da_ref, g, dup, db_acc, dc_acc, carry_re, carry_im = refs[8 + len(extra):]
        i = pl.program_id(0)

        @pl.when(i == 0)
        def _():
            carry_re[...] = jnp.zeros_like(carry_re)
            carry_im[...] = jnp.zeros_like(carry_im)
            da_ref[...] = jnp.zeros_like(da_ref)
            db_acc[...] = jnp.zeros_like(db_acc)
            dc_acc[...] = jnp.zeros_like(dc_acc)

        has_dy = (i < nb - 1).astype(F32)
        dyp = _permute_rows(p_ref, (dy_ref[...] * has_dy).astype(BF16))
        up = _permute_rows(p_ref, u_ref[...].astype(BF16))
        for c in range(S5_BLOCKS):
            g[:, c * sw:(c + 1) * sw] = _dot(dyp[:, c * bw:(c + 1) * bw], cm_ref[c], "nn")
            dc_acc[c] += _dot(dyp[:, c * bw:(c + 1) * bw], s_ref[:, c * sw:(c + 1) * sw].astype(BF16), "tn")
            _scan_chunk(g, g, adj_ref, carry_re, carry_im, True, pair_ref=s_ref, acc_ref=da_ref, lane_chunks=[c])
            gc = g[:, c * sw:(c + 1) * sw].astype(BF16)
            dup[:, c * bw:(c + 1) * bw] = _dot(gc, bm_ref[c], "nt")
            db_acc[c] += _dot(up[:, c * bw:(c + 1) * bw], gc, "tn")
        du = _unpermute_rows(pt_ref, dup[...])
        if finish:
            du = du + extra[0][...] + (dy_ref[...] * has_dy) * extra[1][...]
        du_ref[...] = du.astype(du_ref.dtype)

        @pl.when(i == nb - 1)
        def _():
            mask = _block_diag_mask((bw, sw // 2))
            for acc, out in ((db_acc, db_ref), (dc_acc, dc_ref)):
                for c in range(S5_BLOCKS):
                    for part in range(2):
                        cols = slice(part * (sw // 2), (part + 1) * (sw // 2))
                        kept = jnp.where(mask, acc[c, :, cols], 0.0)
                        out[c, :, cols] = kept.reshape(bw // S5_GROUP, S5_GROUP, sw // 2).sum(axis=0)

    blk = lambda i: (block_index(i), 0)
    in_specs = [pl.BlockSpec((ROW_BLOCK, S5_WIDTH), lambda i: (jnp.minimum(block_index(i), n_lat - 1), 0)),
                pl.BlockSpec((ROW_BLOCK, S5_WIDTH), blk), pl.BlockSpec((ROW_BLOCK, 2 * NSTATE), blk),
                _full_spec(_S5_BLOCKED), _full_spec(_S5_BLOCKED), _full_spec(_S5_TABLES),
                _full_spec((ROW_BLOCK, ROW_BLOCK)), _full_spec((ROW_BLOCK, ROW_BLOCK))]
    args = [dy, z_all, states, bmat, cmat, adj, perm, perm_t]
    if finish:
        in_specs += [pl.BlockSpec((ROW_BLOCK, S5_WIDTH), blk), _full_spec((1, S5_WIDTH))]
        args += [du_other, d_skip]
    return pl.pallas_call(
        body, name=name, grid=(nb,), in_specs=in_specs,
        out_specs=[pl.BlockSpec((ROW_BLOCK, S5_WIDTH), blk), _full_spec(_S5_DIAG), _full_spec(_S5_DIAG),
                   _full_spec((SUBLANES, 2 * NSTATE))],
        out_shape=[_sds((rows, S5_WIDTH), BF16 if finish else F32), _sds(_S5_DIAG), _sds(_S5_DIAG),
                   _sds((SUBLANES, 2 * NSTATE))],
        scratch_shapes=[pltpu.VMEM((ROW_BLOCK, 2 * NSTATE), F32), pltpu.VMEM((ROW_BLOCK, S5_WIDTH), F32),
                        pltpu.VMEM(_S5_BLOCKED, F32), pltpu.VMEM(_S5_BLOCKED, F32),
                        pltpu.VMEM((SUBLANES, NSTATE), F32), pltpu.VMEM((SUBLANES, NSTATE), F32)],
        compiler_params=_params(("arbitrary",)),
    )(*args)


def _latent_row_tile(n_rows):
    return 512 if n_rows % 512 == 0 else ROW_BLOCK


def _glu_bwd(d_ycat, z_all, y0, y1, d_skip, w_glu, n_rows):
    def body(do_ref, u_ref, y0_ref, y1_ref, d_ref, w_ref, dy_ref, dw_ref, dd_ref):
        @pl.when(pl.program_id(0) == 0)
        def _():
            dw_ref[...] = jnp.zeros_like(dw_ref)
            dd_ref[...] = jnp.zeros_like(dd_ref)

        u = u_ref[...]
        y = d_ref[...] * u + y0_ref[...] + y1_ref[...]
        g = _gelu(y)
        gb = g.astype(BF16)
        w = w_ref[...]
        sg = _sigmoid(_dot(gb, w, "nn"))
        do = do_ref[...]
        dt = do * g * sg * (1.0 - sg)
        dtb = dt.astype(BF16)
        dg = do * sg + _dot(dtb, w, "nt")
        dy = dg * _dgelu(y)
        dy_ref[...] = dy
        dw_ref[...] += _dot(gb, dtb, "tn")
        dd_ref[...] += _fold8(dy * u)

    rows = _latent_row_tile(n_rows)
    row = pl.BlockSpec((rows, S5_WIDTH), lambda i: (i, 0))
    sq = pl.BlockSpec((S5_WIDTH, S5_WIDTH), lambda i: (0, 0))
    return pl.pallas_call(
        body, name="glu_bwd", grid=(n_rows // rows,),
        in_specs=[row, row, row, row, pl.BlockSpec((1, S5_WIDTH), lambda i: (0, 0)), sq],
        out_specs=[row, sq, pl.BlockSpec((SUBLANES, S5_WIDTH), lambda i: (0, 0))],
        out_shape=[_sds((n_rows, S5_WIDTH)), _sds((S5_WIDTH, S5_WIDTH)), _sds((SUBLANES, S5_WIDTH))],
        compiler_params=_params(("arbitrary",)),
    )(d_ycat, z_all, y0, y1, d_skip, w_glu)


CONV_HALF = CONV_K // 2


def _conv_block(n_rows):
    blk = min(1024, n_rows)
    assert blk >= CONV_HALF * GRID_W and n_rows % blk == 0
    return blk


def _conv_gate(z_all, n_rows):
    blk = _conv_block(n_rows)
    nb = n_rows // blk

    def body(v_ref, g_ref, o_ref):
        i = pl.program_id(0)
        inside = jnp.logical_and(i >= 1, i <= nb)

        @pl.when(inside)
        def _():
            o_ref[...] = v_ref[...] * _sigmoid(g_ref[...])

        @pl.when(jnp.logical_not(inside))
        def _():
            o_ref[...] = jnp.zeros_like(o_ref)

    src = lambda col: pl.BlockSpec((blk, CONV_WIDTH), lambda i: (jnp.clip(i - 1, 0, nb - 1), col))
    return pl.pallas_call(
        body, name="conv_gate", grid=(nb + 2,), in_specs=[src(1), src(2)],
        out_specs=pl.BlockSpec((blk, CONV_WIDTH), lambda i: (i, 0)),
        out_shape=_sds(((nb + 2) * blk, CONV_WIDTH)), compiler_params=_params(("parallel",)),
    )(z_all, z_all)


def _stream_padded(pad_ref, buf, sems, blk, n_blocks):
    i = pl.program_id(0)

    def copy(b):
        rows = pl.ds(pl.multiple_of(b * blk, blk), blk)
        return pltpu.make_async_copy(pad_ref.at[rows, :], buf.at[rows, :], sems.at[b])

    @pl.when(i == 0)
    def _():
        for b in range(n_blocks):
            copy(b).start()
        copy(0).wait()
        copy(1).wait()

    copy(i + 2).wait()
    return pl.multiple_of(i * blk, blk)


def _conv_fwd(hh_pad, w, b, ln_g, ln_b, ycat, n_rows):
    blk = _conv_block(n_rows)
    nblk = n_rows // blk + 2

    def body(hh_ref, w_ref, b_ref, g_ref, lb_ref, ycat_ref, hc_ref, y_ref, win, sems):
        base = _stream_padded(hh_ref, win, sems, blk, nblk)

        def tile(t, _):
            r0 = pl.multiple_of(t * CONV_ROWS, CONV_ROWS)
            acc = jnp.zeros((CONV_ROWS, CONV_WIDTH), F32)
            for k in range(CONV_K):
                acc = acc + w_ref[k:k + 1, :] * win[pl.ds(base + r0 + blk + (k - CONV_HALF) * GRID_W, CONV_ROWS), :]
            hc = acc + b_ref[...]
            hc_ref[pl.ds(r0, CONV_ROWS), :] = hc
            mu = jnp.mean(hc, axis=-1, keepdims=True)
            xc = hc - mu
            ln = xc * lax.rsqrt(jnp.mean(xc * xc, axis=-1, keepdims=True) + EPS_LN) * g_ref[...] + lb_ref[...]
            y_ref[pl.ds(r0, CONV_ROWS), :] = _silu(ln).astype(y_ref.dtype)
            return 0

        lax.fori_loop(0, blk // CONV_ROWS, tile, 0)

    vec = pl.BlockSpec((1, CONV_WIDTH), lambda i: (0, 0))
    row = pl.BlockSpec((blk, CONV_WIDTH), lambda i: (i, 0))
    return pl.pallas_call(
        body, name="conv_fwd", grid=(n_rows // blk,),
        in_specs=[ANY, pl.BlockSpec((CONV_K, CONV_WIDTH), lambda i: (0, 0)), vec, vec, vec, ANY],
        out_specs=[row, pl.BlockSpec((blk, CONV_WIDTH), lambda i: (i, 1))],
        out_shape=[_sds((n_rows, CONV_WIDTH)), _sds(ycat.shape, ycat.dtype)], input_output_aliases={5: 1},
        scratch_shapes=[pltpu.VMEM((nblk * blk, CONV_WIDTH), F32), pltpu.SemaphoreType.DMA((nblk,))],
        compiler_params=_params(("arbitrary",)),
    )(hh_pad, w, b, ln_g, ln_b, ycat)


def _conv_bwd_norm(d_ycat, hc, ln_g, ln_b, n_rows):
    blk = _conv_block(n_rows)
    nb = n_rows // blk

    def body(dy_ref, hc_ref, g_ref, lb_ref, o_ref, sums):
        i = pl.program_id(0)

        @pl.when(i == 0)
        def _():
            sums[...] = jnp.zeros_like(sums)

        inside = jnp.logical_and(i >= 1, i <= nb)

        @pl.when(inside)
        def _():
            hcv = hc_ref[...]
            mu = jnp.mean(hcv, axis=-1, keepdims=True)
            xc = hcv - mu
            rstd = lax.rsqrt(jnp.mean(xc * xc, axis=-1, keepdims=True) + EPS_LN)
            xh = xc * rstd
            g = g_ref[...]
            dln = dy_ref[...] * _dsilu(xh * g + lb_ref[...])
            dxh = dln * g
            dhc = rstd * (dxh - jnp.mean(dxh, axis=-1, keepdims=True) - xh * jnp.mean(dxh * xh, axis=-1, keepdims=True))
            o_ref[...] = dhc
            sums[0] += _fold8(dhc)
            sums[1] += _fold8(dln * xh)
            sums[2] += _fold8(dln)

        @pl.when(jnp.logical_not(inside))
        def _():
            o_ref[...] = jnp.zeros_like(o_ref)

    vec = pl.BlockSpec((1, CONV_WIDTH), lambda i: (0, 0))
    return pl.pallas_call(
        body, name="conv_bwd_norm", grid=(nb + 2,),
        in_specs=[pl.BlockSpec((blk, CONV_WIDTH), lambda i: (jnp.clip(i - 1, 0, nb - 1), 1)),
                  pl.BlockSpec((blk, CONV_WIDTH), lambda i: (jnp.clip(i - 1, 0, nb - 1), 0)), vec, vec],
        out_specs=[pl.BlockSpec((blk, CONV_WIDTH), lambda i: (i, 0)),
                   pl.BlockSpec((3, SUBLANES, CONV_WIDTH), lambda i: (0, 0, 0))],
        out_shape=[_sds(((nb + 2) * blk, CONV_WIDTH)), _sds((3, SUBLANES, CONV_WIDTH))],
        compiler_params=_params(("arbitrary",)),
    )(d_ycat, hc, ln_g, ln_b)


def _conv_bwd_taps(dhc_pad, hh_pad, z_all, w, n_rows):
    blk = _conv_block(n_rows)
    nblk = n_rows // blk + 2

    def body(dhc_ref, hh_ref, v_ref, g_ref, w_ref, dv_ref, dg_ref, dw_ref, dwin, hwin, dsems, hsems):
        @pl.when(pl.program_id(0) == 0)
        def _():
            dw_ref[...] = jnp.zeros_like(dw_ref)

        base = _stream_padded(dhc_ref, dwin, dsems, blk, nblk)
        _stream_padded(hh_ref, hwin, hsems, blk, nblk)

        def tile(t, _):
            r0 = pl.multiple_of(t * CONV_BWD_ROWS, CONV_BWD_ROWS) + base
            dh = dwin[pl.ds(r0 + blk, CONV_BWD_ROWS), :]
            acc = jnp.zeros((CONV_BWD_ROWS, CONV_WIDTH), F32)
            for k in range(CONV_K):
                off = (k - CONV_HALF) * GRID_W
                acc = acc + w_ref[k:k + 1, :] * dwin[pl.ds(r0 + blk - off, CONV_BWD_ROWS), :]
                dw_ref[k] += _fold8(dh * hwin[pl.ds(r0 + blk + off, CONV_BWD_ROWS), :])
            rs = pl.ds(pl.multiple_of(t * CONV_BWD_ROWS, CONV_BWD_ROWS), CONV_BWD_ROWS)
            sg = _sigmoid(g_ref[rs, :])
            vv = v_ref[rs, :]
            dv_ref[rs, :] = (acc * sg).astype(dv_ref.dtype)
            dg_ref[rs, :] = (acc * vv * sg * (1.0 - sg)).astype(dg_ref.dtype)
            return 0

        lax.fori_loop(0, blk // CONV_BWD_ROWS, tile, 0)

    row = pl.BlockSpec((blk, CONV_WIDTH), lambda i: (i, 0))
    return pl.pallas_call(
        body, name="conv_bwd_taps", grid=(n_rows // blk,),
        in_specs=[ANY, ANY,
            pl.BlockSpec((blk, CONV_WIDTH), lambda i: (i, 1)), pl.BlockSpec((blk, CONV_WIDTH), lambda i: (i, 2)),
            pl.BlockSpec((CONV_K, CONV_WIDTH), lambda i: (0, 0))],
        out_specs=[row, row, pl.BlockSpec((CONV_K, SUBLANES, CONV_WIDTH), lambda i: (0, 0, 0))],
        out_shape=[_sds((n_rows, CONV_WIDTH), BF16), _sds((n_rows, CONV_WIDTH), BF16),
                   _sds((CONV_K, SUBLANES, CONV_WIDTH))],
        scratch_shapes=[pltpu.VMEM((nblk * blk, CONV_WIDTH), F32), pltpu.VMEM((nblk * blk, CONV_WIDTH), F32),
                        pltpu.SemaphoreType.DMA((nblk,)), pltpu.SemaphoreType.DMA((nblk,))],
        compiler_params=_params(("arbitrary",)),
    )(dhc_pad, hh_pad, z_all, z_all, w)


def _sum_parts(parts):
    _, r, c = parts.shape

    def body(p_ref, o_ref):
        acc = p_ref[0]
        for q in range(1, NDEV):
            acc = acc + p_ref[q]
        o_ref[...] = acc

    return pl.pallas_call(body, name="sum_parts", out_shape=_sds((r, c)), compiler_params=_params())(parts)


def _row_tile(r, c):
    best = r
    for t in (1024, 512, 256, 128, 64, 32, 16, 8):
        if r % t == 0 and t * c <= 128 * 1024:
            return t
    return best


def _adamw(name, w, gparts, m, v):
    r, c = w.shape
    np_ = gparts.shape[0]
    tr = _row_tile(r, c)

    def body(w_ref, g_ref, m_ref, v_ref, go_ref, d_ref, mo_ref, vo_ref):
        g = g_ref[0].astype(F32)
        for q in range(1, np_):
            g = g + g_ref[q].astype(F32)
        m2 = ADAM_B1 * m_ref[...] + (1.0 - ADAM_B1) * g
        v2 = ADAM_B2 * v_ref[...] + (1.0 - ADAM_B2) * jnp.square(g)
        m_hat = m2 / (1.0 - ADAM_B1 ** ADAM_STEP)
        v_hat = v2 / (1.0 - ADAM_B2 ** ADAM_STEP)
        go_ref[...] = g
        d_ref[...] = -ADAM_LR * (m_hat / (jnp.sqrt(v_hat) + ADAM_EPS) + ADAM_WD * w_ref[...])
        mo_ref[...] = m2
        vo_ref[...] = v2

    row = pl.BlockSpec((tr, c), lambda i: (i, 0))
    return pl.pallas_call(
        body, name=name, grid=(r // tr,),
        in_specs=[row, pl.BlockSpec((np_, tr, c), lambda i: (0, i, 0)), row, row],
        out_specs=[row] * 4, out_shape=[_sds((r, c))] * 4, compiler_params=_params(("parallel",)),
    )(w, gparts, m, v)


def _adamw_native(name, w, g, m, v):
    def body(w_ref, g_ref, m_ref, v_ref, d_ref, mo_ref, vo_ref):
        gv = g_ref[...]
        m2 = ADAM_B1 * m_ref[...] + (1.0 - ADAM_B1) * gv
        v2 = ADAM_B2 * v_ref[...] + (1.0 - ADAM_B2) * jnp.square(gv)
        m_hat = m2 / (1.0 - ADAM_B1 ** ADAM_STEP)
        v_hat = v2 / (1.0 - ADAM_B2 ** ADAM_STEP)
        d_ref[...] = -ADAM_LR * (m_hat / (jnp.sqrt(v_hat) + ADAM_EPS) + ADAM_WD * w_ref[...])
        mo_ref[...] = m2
        vo_ref[...] = v2

    return pl.pallas_call(body, name=name, out_shape=[_sds(w.shape)] * 3, compiler_params=_params())(w, g, m, v)


SMALL = ["c_ctx", "ada_b", "norm1_g", "s5_lam_re", "s5_lam_im", "s5_log_dt", "s5_d", "conv_b", "conv_ln_g", "conv_ln_b",
         "norm2_g", "final_g"]
SMALL_PACKED_ROWS = 24


def _pack_rows(parts, rows):
    flat = jnp.concatenate([p.reshape(-1).astype(F32) for p in parts])
    return jnp.pad(flat, (0, rows * D_MODEL - flat.shape[0])).reshape(rows, D_MODEL)


def _unpack_rows(packed, shapes):
    flat = packed.reshape(-1)
    out, off = [], 0
    for shape in shapes:
        size = 1
        for s in shape:
            size *= s
        out.append(flat[off:off + size].reshape(shape))
        off += size
    return out


def kernel(x, c, ctx, c_ctx, ada_w, ada_b, norm1_g, w_in, s5_lam_re, s5_lam_im, s5_log_dt, s5_b_re, s5_b_im, s5_c_re, s5_c_im, s5_d, s5_w_glu, conv_w, conv_b, conv_ln_g, conv_ln_b, w_out, norm2_g, mlp_w1, mlp_w2, final_g, loss_target, m_c_ctx, m_ada_w, m_ada_b, m_norm1_g, m_w_in, m_s5_lam_re, m_s5_lam_im, m_s5_log_dt, m_s5_b_re, m_s5_b_im, m_s5_c_re, m_s5_c_im, m_s5_d, m_s5_w_glu, m_conv_w, m_conv_b, m_conv_ln_g, m_conv_ln_b, m_w_out, m_norm2_g, m_mlp_w1, m_mlp_w2, m_final_g, v_c_ctx, v_ada_w, v_ada_b, v_norm1_g, v_w_in, v_s5_lam_re, v_s5_lam_im, v_s5_log_dt, v_s5_b_re, v_s5_b_im, v_s5_c_re, v_s5_c_im, v_s5_d, v_s5_w_glu, v_conv_w, v_conv_b, v_conv_ln_g, v_conv_ln_b, v_w_out, v_norm2_g, v_mlp_w1, v_mlp_w2, v_final_g):
    weights = dict(c_ctx=c_ctx, ada_w=ada_w, ada_b=ada_b, norm1_g=norm1_g, w_in=w_in, s5_lam_re=s5_lam_re, s5_lam_im=s5_lam_im, s5_log_dt=s5_log_dt, s5_b_re=s5_b_re, s5_b_im=s5_b_im, s5_c_re=s5_c_re, s5_c_im=s5_c_im, s5_d=s5_d, s5_w_glu=s5_w_glu, conv_w=conv_w, conv_b=conv_b, conv_ln_g=conv_ln_g, conv_ln_b=conv_ln_b, w_out=w_out, norm2_g=norm2_g, mlp_w1=mlp_w1, mlp_w2=mlp_w2, final_g=final_g)
    mom1 = dict(c_ctx=m_c_ctx, ada_w=m_ada_w, ada_b=m_ada_b, norm1_g=m_norm1_g, w_in=m_w_in, s5_lam_re=m_s5_lam_re, s5_lam_im=m_s5_lam_im, s5_log_dt=m_s5_log_dt, s5_b_re=m_s5_b_re, s5_b_im=m_s5_b_im, s5_c_re=m_s5_c_re, s5_c_im=m_s5_c_im, s5_d=m_s5_d, s5_w_glu=m_s5_w_glu, conv_w=m_conv_w, conv_b=m_conv_b, conv_ln_g=m_conv_ln_g, conv_ln_b=m_conv_ln_b, w_out=m_w_out, norm2_g=m_norm2_g, mlp_w1=m_mlp_w1, mlp_w2=m_mlp_w2, final_g=m_final_g)
    mom2 = dict(c_ctx=v_c_ctx, ada_w=v_ada_w, ada_b=v_ada_b, norm1_g=v_norm1_g, w_in=v_w_in, s5_lam_re=v_s5_lam_re, s5_lam_im=v_s5_lam_im, s5_log_dt=v_s5_log_dt, s5_b_re=v_s5_b_re, s5_b_im=v_s5_b_im, s5_c_re=v_s5_c_re, s5_c_im=v_s5_c_im, s5_d=v_s5_d, s5_w_glu=v_s5_w_glu, conv_w=v_conv_w, conv_b=v_conv_b, conv_ln_g=v_conv_ln_g, conv_ln_b=v_conv_ln_b, w_out=v_w_out, norm2_g=v_norm2_g, mlp_w1=v_mlp_w1, mlp_w2=v_mlp_w2, final_g=v_final_g)
    order = list(weights)

    me = 4 * lax.axis_index("x") + 2 * lax.axis_index("y") + lax.axis_index("c")
    xs, cs, tgt = x[0], ctx[0], loss_target[0]
    n_lat_rows, n_ctx_rows = xs.shape[0], cs.shape[0]
    n_rows = n_lat_rows + n_ctx_rows
    n_lat = n_lat_rows // ROW_BLOCK
    ada_cols = ada_w.shape[2]

    (c_all,), _ = _exchange("gather_c", [c], [True])
    c_all = c_all.reshape(NDEV, D_MODEL)

    cond_fwd = jnp.concatenate([c_all, c_ctx[None], jnp.zeros((7, D_MODEL), F32)])
    ada_b_loc = lax.dynamic_slice(ada_b, (0, me * ada_cols), (1, ada_cols))
    (mod_g,), mod_token = _exchange("gather_mod", [_ada_fwd(cond_fwd, ada_w[0], ada_b_loc)], [True])
    weight_groups, weights_token = _exchange_start_groups("gather_weights_start", [
        ([w_in[0].astype(BF16)], [True]),
        ([s5_w_glu[0].astype(BF16), conv_w[0] + mod_token[0:1, 0:1], w_out[0].astype(BF16)], [True] * 3),
        ([mlp_w1[0].astype(BF16), mlp_w2[0].astype(BF16)], [True] * 2)])
    (wi_send, wi_recv, wi_src, wi_land), (mixer_send, mixer_recv, mixer_src, mixer_land), \
        (mlpw_send, mlpw_recv, mlpw_src, mlpw_land) = weight_groups
    mod_rows = jnp.transpose(mod_g, (1, 0, 2)).reshape(16, 6 * D_MODEL) + weights_token[0:1, 0:1]
    mod = lax.dynamic_slice(mod_rows, (me, 0), (1, 6 * D_MODEL)).reshape(6, D_MODEL)
    modc = mod_rows[8, :2 * D_MODEL].reshape(2, D_MODEL)
    sh1, sc1, g1, sh2, sc2, g2 = [mod[i:i + 1] for i in range(6)]

    lam_re, lam_im = s5_lam_re[0].reshape(2, 1, NSTATE), s5_lam_im[0].reshape(2, 1, NSTATE)
    ldt = jnp.repeat(s5_log_dt[0], S5_STATE, axis=-1).reshape(2, 1, NSTATE)
    bt_re = jnp.transpose(s5_b_re[0], (0, 3, 1, 2)).reshape(2, S5_GROUP, NSTATE)
    bt_im = jnp.transpose(s5_b_im[0], (0, 3, 1, 2)).reshape(2, S5_GROUP, NSTATE)
    groups_per_block = S5_GROUPS // S5_BLOCKS
    ct_re = jnp.tile(s5_c_re[0].reshape(2, S5_WIDTH, S5_STATE), (1, 1, groups_per_block))
    ct_im = jnp.tile(s5_c_im[0].reshape(2, S5_WIDTH, S5_STATE), (1, 1, groups_per_block))
    d_skip = s5_d[0].reshape(1, S5_WIDTH)
    perms = [_segment_permutation(reverse_time=(d == 0)) for d in range(2)]
    perms_t = [p.T for p in perms]
    disc = [_s5_discretise(f"s5_disc{d}", False, lam_re[d], lam_im[d], ldt[d], bt_re[d], bt_im[d], ct_re[d], ct_im[d])
            for d in range(2)]

    a_all = _prenorm("prenorm1", xs, cs, norm1_g, jnp.stack([mod[0:2], modc]))
    before_w_in = a_all[0:SUBLANES, 0:LANES].astype(F32) + disc[0][0][0:SUBLANES, 0:LANES] + disc[1][0][0:SUBLANES, 0:LANES]
    wi_own, wi_landed = _exchange_wait("gather_w_in_wait", wi_send, wi_recv, wi_src, wi_land, [True], before_w_in)
    w_in_full = jnp.transpose(wi_landed[0], (1, 0, 2)).reshape(D_MODEL, IN_COLS)
    tm_all = 1088 if n_rows % 1088 == 0 else ROW_BLOCK
    (z_all,) = _matmul("in_proj", a_all, w_in_full, "nn", (n_rows, IN_COLS, D_MODEL), (tm_all, IN_COLS, D_MODEL),
                       [((n_rows, IN_COLS), F32)])

    _, tab, _, bmat, cmat = disc[0]
    s0, y0 = _s5_scan_fwd("s5_scan_fwd0", True, z_all, bmat, cmat, tab, perms[0], perms_t[0])
    mixer_own, mixer_landed = _exchange_wait("gather_mixer_wait", mixer_send, mixer_recv, mixer_src, mixer_land,
                                             [True] * 3, y0)
    glu_g, conv_w_g, w_out_g = mixer_landed
    glu_full = glu_g.reshape(S5_WIDTH, S5_WIDTH)
    conv_w_full = jnp.transpose(conv_w_g, (1, 0, 2)).reshape(CONV_K, CONV_WIDTH)
    w_out_full = w_out_g.reshape(D_MODEL, D_MODEL)
    _, tab, _, bmat, cmat = disc[1]
    s1, y1, ycat = _s5_scan_fwd("s5_scan_fwd1", False, z_all, bmat, cmat, tab, perms[1], perms_t[1],
                                y_other=y0, d_skip=d_skip, w_glu=glu_full)
    states, y_dir = [s0, s1], [y0, y1]

    hh_pad = _conv_gate(z_all, n_lat_rows)
    hc, ycat = _conv_fwd(hh_pad, conv_w_full, conv_b, conv_ln_g, conv_ln_b, ycat, n_lat_rows)

    tm = min(1024, n_lat_rows)
    tm_e = min(512, n_lat_rows)
    w1_cols = D_FF // NDEV
    row_vec = lambda tn: pl.BlockSpec((1, tn), lambda i, j, k: (0, j))
    out_tile = lambda t_m, t_n: pl.BlockSpec((t_m, t_n), lambda i, j, k: (i, j))
    full_rows = ((n_lat_rows, D_MODEL), F32)
    sums = ((n_lat_rows // tm_e, SUBLANES, D_MODEL), F32)
    sums_spec = pl.BlockSpec((None, SUBLANES, D_MODEL), lambda i, j, k: (i, 0, 0))
    vec = lambda v: (v, row_vec(D_MODEL))
    transposed_tile = lambda t_m, t_n: pl.BlockSpec((t_n, t_m), lambda i, j, k: (j, i))
    mix, h1, a2, a2_t = _matmul(
        "out_proj", ycat, w_out_full, "nn", (n_lat_rows, D_MODEL, D_MODEL), (tm_e, D_MODEL, D_MODEL),
        [full_rows, full_rows, ((n_lat_rows, D_MODEL), BF16), ((D_MODEL, n_lat_rows), BF16)],
        epi=_epi_residual_prenorm,
        epi_extra=[(xs, out_tile(tm_e, D_MODEL)), vec(g1), vec(norm2_g), vec(sc2), vec(sh2)],
        out_specs=[out_tile(tm_e, D_MODEL)] * 3 + [transposed_tile(tm_e, D_MODEL)])
    mlpw_own, mlpw_landed = _exchange_wait("gather_mlp_wait", mlpw_send, mlpw_recv, mlpw_src, mlpw_land, [True] * 2, a2)
    w1_g, w2_g = mlpw_landed
    w2_full = w2_g.reshape(D_FF, D_MODEL)
    tm_up = min(2048, n_lat_rows)
    f, f_t = _matmul("mlp_up", a2, w1_g, "nn", (n_lat_rows, D_FF, D_MODEL), (tm_up, w1_cols, D_MODEL),
                     [((n_lat_rows, D_FF), BF16), ((D_FF, n_lat_rows), BF16)], epi=lambda acc: (acc, acc.T),
                     b_spec=pl.BlockSpec((None, D_MODEL, w1_cols), lambda i, j, k: (j, 0, 0)),
                     out_specs=[out_tile(tm_up, w1_cols), transposed_tile(tm_up, w1_cols)])
    sq_relu = lambda t: jnp.square(jnp.maximum(t, 0.0))
    mlp_out, d_h2, dm2, err_sums, d_final_g8 = _matmul(
        "mlp_down", f, w2_full, "nn", (n_lat_rows, D_MODEL, D_FF), (tm_e, D_MODEL, 2048),
        [full_rows, full_rows, ((n_lat_rows, D_MODEL), BF16), sums, sums], a_fn=sq_relu, epi=_epi_residual_loss,
        epi_extra=[(h1, out_tile(tm_e, D_MODEL)), vec(g2), (tgt, out_tile(tm_e, D_MODEL)), vec(final_g[None])],
        out_specs=[out_tile(tm_e, D_MODEL)] * 3 + [sums_spec] * 2)

    (d_f,) = _matmul("mlp_down_dx", dm2, w2_full, "nt", (n_lat_rows, D_FF, D_MODEL), (tm_up, 1024, D_MODEL),
                     [((n_lat_rows, D_FF), BF16)],
                     epi=lambda acc, ft: (acc * 2.0 * jnp.maximum(ft.astype(F32), 0.0),),
                     epi_extra=[(f, out_tile(tm_up, 1024))])
    tk_dw = min(2048, n_lat_rows)
    (g_w2,) = _matmul("mlp_down_dw", f_t, dm2, "nn", (D_FF, D_MODEL, n_lat_rows), (1024, D_MODEL, tk_dw),
                      [((D_FF, D_MODEL), F32)], a_fn=sq_relu)
    (g_w1,) = _matmul("mlp_up_dw", a2_t, d_f, "nn", (D_MODEL, D_FF, n_lat_rows), (D_MODEL, w1_cols, n_lat_rows),
                      [((NDEV, D_MODEL, w1_cols), F32)],
                      out_specs=[pl.BlockSpec((None, D_MODEL, w1_cols), lambda i, j, k: (j, 0, 0))])
    mlp_send, mlp_recv, mlp_src, mlp_land, mlp_token = _exchange_start(
        "scatter_mlp_start", [g_w1, g_w2.reshape(NDEV, D_FF // NDEV, D_MODEL)], [False] * 2)
    d_h1, dm1, *sums2 = _matmul(
        "mlp_up_dx", d_f, w1_g, "nt", (n_lat_rows, D_MODEL, D_FF), (tm_e, D_MODEL, 4 * w1_cols),
        [full_rows, ((n_lat_rows, D_MODEL), BF16)] + [sums] * 4, epi=_epi_norm_bwd,
        epi_extra=[(h1, out_tile(tm_e, D_MODEL)), (d_h2, out_tile(tm_e, D_MODEL)), (mlp_out, out_tile(tm_e, D_MODEL)),
                   vec(norm2_g), vec(sc2 + mlp_token[0:1, 0:1]), vec(g1)],
        b_spec=pl.BlockSpec((4, D_MODEL, w1_cols), lambda i, j, k: (k, 0, 0)), b_slabs=4,
        out_specs=[out_tile(tm_e, D_MODEL)] * 2 + [sums_spec] * 4)

    (d_ycat,) = _matmul("out_proj_dx", dm1, w_out_full, "nt", (n_lat_rows, D_MODEL, D_MODEL), (tm, D_MODEL, D_MODEL),
                        [((n_lat_rows, D_MODEL), F32)])
    (g_w_out,) = _matmul("out_proj_dw", ycat, dm1, "tn", (D_MODEL, D_MODEL, n_lat_rows), (D_MODEL, D_MODEL, 512),
                         [((D_MODEL, D_MODEL), F32)])

    dy, g_glu, dd8 = _glu_bwd(d_ycat, z_all, y_dir[0], y_dir[1], d_skip, glu_full, n_lat_rows)
    proj_send, proj_recv, proj_src, proj_land, proj_token = _exchange_start(
        "scatter_proj_start",
        [g_w_out.reshape(NDEV, D_MODEL // NDEV, D_MODEL), g_glu.reshape(NDEV, S5_WIDTH // NDEV, S5_WIDTH)], [False] * 2)
    perms = [p + proj_token[0:1, 0:1].astype(BF16) for p in perms]
    du, g_lam_re, g_lam_im, g_ldt, g_bt, g_cdiag = None, [], [], [], [], []
    for d in range(2):
        _, _, adj, bmat, cmat = disc[d]
        du, d_bdiag, d_cdiag, d_abar8 = _s5_scan_bwd(f"s5_scan_bwd{d}", d == 0, dy, z_all, states[d], bmat, cmat, adj,
                                                     perms[d], perms_t[d], du_other=du, d_skip=d_skip if d else None)
        d_bbar = jnp.transpose(d_bdiag.reshape(S5_BLOCKS, S5_GROUP, 2, NSTATE // S5_BLOCKS), (2, 1, 0, 3)).reshape(
            2 * S5_GROUP, NSTATE)
        d_lam8, d_bt = _s5_discretise_bwd(f"s5_disc_bwd{d}", lam_re[d], lam_im[d], ldt[d], bt_re[d], bt_im[d], d_abar8, d_bbar)
        g_lam_re.append(d_lam8[0].reshape(S5_GROUPS, S5_STATE))
        g_lam_im.append(d_lam8[1].reshape(S5_GROUPS, S5_STATE))
        g_ldt.append(d_lam8[2].reshape(S5_GROUPS, S5_STATE).sum(axis=-1))
        g_bt.append(d_bt)
        g_cdiag.append(d_cdiag)

    dhc_pad, conv_sums = _conv_bwd_norm(d_ycat, hc, conv_ln_g, conv_ln_b, n_lat_rows)
    d_v, d_gate, g_conv_w8 = _conv_bwd_taps(dhc_pad, hh_pad, z_all, conv_w_full, n_lat_rows)

    no_ctx = jnp.zeros((n_ctx_rows, CONV_WIDTH), BF16)
    dz_all = jnp.concatenate([du, jnp.concatenate([d_v, no_ctx]), jnp.concatenate([d_gate, no_ctx])], axis=1)
    (g_w_in_full,) = _matmul("in_proj_dw", a_all, dz_all, "tn", (D_MODEL, IN_COLS, n_rows), (D_MODEL, IN_COLS, tm_all),
                             [((D_MODEL, IN_COLS), BF16)])
    g_w_in_parts = jnp.transpose(g_w_in_full.reshape(D_MODEL, NDEV, IN_COLS // NDEV), (1, 0, 2))
    win_send, win_recv, win_src, win_land, win_token = _exchange_start("scatter_w_in_start", [g_w_in_parts], [False])
    w_in_late = w_in_full + win_token[0:1, 0:1].astype(BF16)
    grad_x, *sums1 = _matmul(
        "in_proj_dx", dz_all, w_in_late, "nt", (n_lat_rows, D_MODEL, IN_COLS), (tm_e, D_MODEL, IN_COLS),
        [full_rows] + [sums] * 4, epi=_epi_norm_bwd,
        epi_extra=[(xs, out_tile(tm_e, D_MODEL)), (d_h1, out_tile(tm_e, D_MODEL)), (mix, out_tile(tm_e, D_MODEL)),
                   vec(norm1_g), vec(sc1)],
        out_specs=[out_tile(tm_e, D_MODEL)] + [sums_spec] * 4)
    (d_a_ctx,) = _matmul("in_proj_dx_ctx", dz_all, w_in_late, "nt", (n_ctx_rows, D_MODEL, IN_COLS),
                         (ROW_BLOCK, D_MODEL, IN_COLS), [((n_ctx_rows, D_MODEL), F32)],
                         a_spec=pl.BlockSpec((ROW_BLOCK, IN_COLS), lambda i, j, k: (i + n_lat, 0)))
    (sums1c,) = _norm_bwd("norm1_bwd_ctx", cs, d_a_ctx, 0, norm1_g, modc[1:2])

    s1, s1c, s2 = [p.sum(axis=(0, 1)) for p in sums1], sums1c.sum(axis=1), [p.sum(axis=(0, 1)) for p in sums2]
    d_mod = jnp.concatenate([s1[0], s1[1], s1[3], s2[0], s2[1], s2[3]])
    d_modc = jnp.concatenate([s1c[0], s1c[1], jnp.zeros((4 * D_MODEL,), F32)])
    (dmod_g,), _ = _exchange("gather_dmod", [jnp.stack([d_mod, d_modc])], [True])
    dmod16 = jnp.concatenate([dmod_g[:, 0], dmod_g[:, 1]])
    dmod16_loc = lax.dynamic_slice(dmod16, (0, me * ada_cols), (16, ada_cols))
    cond_bwd = jnp.concatenate([c_all, jnp.broadcast_to(c_ctx[None], (NDEV, D_MODEL))])
    g_ada_w, g_c_ctx8 = _ada_bwd(cond_bwd, dmod16_loc, ada_w[0], c_ctx[None])

    small_parts = dict(
        c_ctx=g_c_ctx8[0], ada_b=d_mod + d_modc, norm1_g=s1[2] + s1c[2],
        s5_lam_re=jnp.stack(g_lam_re), s5_lam_im=jnp.stack(g_lam_im), s5_log_dt=jnp.stack(g_ldt),
        s5_d=dd8.sum(axis=0), conv_b=conv_sums[0].sum(axis=0), conv_ln_g=conv_sums[1].sum(axis=0),
        conv_ln_b=conv_sums[2].sum(axis=0), norm2_g=s2[2], final_g=d_final_g8.sum(axis=(0, 1)))
    reduced_shapes = [(SMALL_PACKED_ROWS, D_MODEL), (2, 2 * S5_GROUP, NSTATE), (2,) + _S5_DIAG, (1,)]
    small_g = _pack_rows(
        [_pack_rows([small_parts[n] for n in SMALL], SMALL_PACKED_ROWS), jnp.stack(g_bt), jnp.stack(g_cdiag),
         (0.5 / D_MODEL * jnp.sum(err_sums)).reshape(1)], SMALL_ROWS).reshape(NDEV, SMALL_ROWS // NDEV, D_MODEL)
    g_conv_w_parts = jnp.transpose(g_conv_w8.sum(axis=1).reshape(CONV_K, NDEV, CONV_WIDTH // NDEV), (1, 0, 2))

    res = {}

    def adamw_big(name, parts):
        outs = _adamw("adamw_" + name, weights[name][0], parts, mom1[name][0], mom2[name][0])
        res[name] = [o[None] for o in outs]
        return outs[0]

    sm_send, sm_recv, sm_src, sm_land, sm_token = _exchange_start("scatter_small_start", [g_conv_w_parts, small_g],
                                                                  [False] * 2)
    _, (p_w1, p_w2) = _exchange_wait("scatter_mlp_wait", mlp_send, mlp_recv, mlp_src, mlp_land, [False] * 2, sm_token)
    adamw_big("ada_w", g_ada_w[None])
    adamw_big("mlp_w1", p_w1)
    done = adamw_big("mlp_w2", p_w2)
    _, (p_conv_w, p_small) = _exchange_wait("scatter_small_wait", sm_send, sm_recv, sm_src, sm_land, [False] * 2, done)
    ga_send, ga_recv, ga_src, ga_land, ga_token = _exchange_start("gather_small_start", [_sum_parts(p_small)], [True])
    _, (p_w_out, p_glu) = _exchange_wait("scatter_proj_wait", proj_send, proj_recv, proj_src, proj_land, [False] * 2,
                                         ga_token)
    adamw_big("w_out", p_w_out)
    done = adamw_big("s5_w_glu", p_glu)
    _, (p_w_in,) = _exchange_wait("scatter_w_in_wait", win_send, win_recv, win_src, win_land, [False], done)
    adamw_big("w_in", p_w_in)
    done = adamw_big("conv_w", p_conv_w)
    _, (small_all,) = _exchange_wait("gather_small_wait", ga_send, ga_recv, ga_src, ga_land, [True], done)
    small_all = small_all.reshape(1, SMALL_ROWS, D_MODEL)
    _, r_bt, r_cdiag, loss = _unpack_rows(small_all, reduced_shapes)
    loss = loss.reshape(())
    pack = lambda src: _pack_rows([src[n] for n in SMALL], SMALL_PACKED_ROWS)
    outs = _adamw("adamw_small", pack(weights), small_all, pack(mom1), pack(mom2))
    unpacked = [_unpack_rows(o, [weights[n].shape for n in SMALL]) for o in outs]
    for i, name in enumerate(SMALL):
        res[name] = [u[i] for u in unpacked]
    to_ghp = lambda t: jnp.transpose(t.reshape(2, S5_GROUP, S5_GROUPS, S5_STATE), (0, 2, 1, 3))[None]
    r_c = jnp.transpose(r_cdiag.reshape(2, S5_BLOCKS, S5_GROUP, 2, groups_per_block, S5_STATE), (3, 0, 1, 4, 2, 5)).reshape(
        2, 1, 2, S5_GROUPS, S5_GROUP, S5_STATE)
    swap = lambda t: jnp.swapaxes(t, -1, -2)
    for name, grad in (("s5_b_re", to_ghp(r_bt[:, :S5_GROUP])), ("s5_b_im", to_ghp(r_bt[:, S5_GROUP:]))):
        outs = _adamw_native("adamw_" + name, swap(weights[name]), grad, swap(mom1[name]), swap(mom2[name]))
        res[name] = [swap(grad), *[swap(o) for o in outs]]
    for name, grad in (("s5_c_re", r_c[0]), ("s5_c_im", -r_c[1])):
        res[name] = [grad, *_adamw_native("adamw_" + name, weights[name], grad, mom1[name], mom2[name])]

    return (loss, grad_x[None], *[res[n][0] for n in order], *[res[n][1] for n in order],
            *[res[n][2] for n in order], *[res[n][3] for n in order])
```

```python
import jax
import jax.numpy as jnp
from jax import lax
from jax.experimental import pallas as pl
from jax.experimental.pallas import tpu as pltpu

F32 = jnp.float32
BF16 = jnp.bfloat16
MESH = pl.DeviceIdType.MESH
ANY = pl.BlockSpec(memory_space=pl.ANY)

NDEV = 8
D_MODEL = 1024
GRID_W = 64
S5_WIDTH = 512
S5_GROUP = 16
S5_GROUPS = 32
S5_STATE = 64
NSTATE = S5_GROUPS * S5_STATE
CONV_WIDTH = 512
CONV_K = 31
IN_COLS = S5_WIDTH + 2 * CONV_WIDTH
D_FF = 4 * D_MODEL
EPS_RMS = 1e-6
EPS_LN = 1e-5
ADAM_LR = 0.001
ADAM_B1 = 0.9
ADAM_B2 = 0.999
ADAM_EPS = 1e-08
ADAM_WD = 0.01
ADAM_STEP = 10

SUBLANES = 8
LANES = 128
ROW_BLOCK = 256
SCAN_LANES = 512
SCAN_UNROLL = 32
SEGMENTS = SUBLANES
STEPS = ROW_BLOCK // SEGMENTS
S5_BLOCKS = 4
S5_BLOCK_WIDTH = S5_WIDTH // S5_BLOCKS
CONV_ROWS = 64
CONV_BWD_ROWS = 32
VMEM_LIMIT = 48 * 1024 * 1024
SMALL_ROWS = 320


def _params(sem=None):
    kw = dict(vmem_limit_bytes=VMEM_LIMIT)
    if sem is not None:
        kw["dimension_semantics"] = sem
    return pltpu.CompilerParams(**kw)


def _sds(shape, dtype=F32):
    return jax.ShapeDtypeStruct(tuple(shape), dtype)


def _fold8(x):
    return x.reshape(x.shape[0] // SUBLANES, SUBLANES, x.shape[1]).sum(axis=0)


def _sigmoid(x):
    return 1.0 / (1.0 + jnp.exp(-x))


def _silu(x):
    return x * _sigmoid(x)


def _dsilu(x):
    s = _sigmoid(x)
    return s * (1.0 + x * (1.0 - s))


_GELU_C = 0.7978845608028654


def _gelu(x):
    return 0.5 * x * (1.0 + jnp.tanh(_GELU_C * (x + 0.044715 * x * x * x)))


def _dgelu(x):
    t = jnp.tanh(_GELU_C * (x + 0.044715 * x * x * x))
    return 0.5 * (1.0 + t) + 0.5 * x * (1.0 - t * t) * _GELU_C * (1.0 + 3.0 * 0.044715 * x * x)


def _rms(x):
    rstd = lax.rsqrt(jnp.mean(x * x, axis=-1, keepdims=True) + EPS_RMS)
    return x * rstd, rstd


def _epi_residual_prenorm(acc, res, gate, gain, scale, shift):
    h = res + gate * acc
    xh, _ = _rms(h)
    a = (xh * gain) * (1.0 + scale) + shift
    return acc, h, a, a.T


def _epi_residual_loss(acc, res, gate, target, gain):
    h = res + gate * acc
    xh, rstd = _rms(h)
    err = xh * gain - target
    dy = err * (1.0 / h.shape[-1])
    dxh = dy * gain
    dh = rstd * (dxh - xh * jnp.mean(dxh * xh, axis=-1, keepdims=True))
    return acc, dh, dh * gate, _fold8(err * err), _fold8(dy * xh)


def _epi_norm_bwd(d_act, x, res, aux, gain, scale, gate=None):
    xh, rstd = _rms(x)
    dn = d_act * (1.0 + scale)
    dxh = dn * gain
    dx = res + rstd * (dxh - xh * jnp.mean(dxh * xh, axis=-1, keepdims=True))
    sums = (_fold8(d_act), _fold8(d_act * (xh * gain)), _fold8(dn * xh), _fold8(res * aux))
    return (dx, *sums) if gate is None else (dx, dx * gate, *sums)


def _dot(a, b, mode):
    dims = {"nn": (((1,), (0,)), ((), ())), "nt": (((1,), (1,)), ((), ())), "tn": (((0,), (0,)), ((), ()))}[mode]
    return lax.dot_general(a, b, dims, preferred_element_type=F32)


def _peers(x, y, c):
    out = []
    for k in range(1, NDEV):
        px = 1 - x if k & 4 else x
        py = 1 - y if k & 2 else y
        pc = 1 - c if k & 1 else c
        out.append(((px, py, pc), 4 * px + 2 * py + pc))
    return out


def _exchange_copies(src, land, send_sems, recv_sems, gather):
    x, y, c = lax.axis_index("x"), lax.axis_index("y"), lax.axis_index("c")
    me = 4 * x + 2 * y + c
    out = []
    for a in range(len(src)):
        for k, (peer, plin) in enumerate(_peers(x, y, c)):
            chunk = src[a] if gather[a] else src[a].at[plin]
            sems = dict(send_sem=send_sems.at[a * (NDEV - 1) + k], recv_sem=recv_sems.at[a * (NDEV - 1) + k],
                        device_id=peer, device_id_type=MESH)
            out.append((pltpu.make_async_remote_copy(src_ref=chunk, dst_ref=land[a].at[me], **sems),
                        pltpu.make_async_remote_copy(src_ref=chunk, dst_ref=land[a].at[plin], **sems)))
    return out


def _exchange(name, srcs, gather):
    n = len(srcs)
    outs = [_sds(((NDEV,) + s.shape) if g else s.shape, s.dtype) for s, g in zip(srcs, gather)]

    def body(*refs):
        src, dst, token = refs[:n], refs[n:2 * n], refs[2 * n]
        send_sems, recv_sems, local_sems = refs[2 * n + 1:]
        me = 4 * lax.axis_index("x") + 2 * lax.axis_index("y") + lax.axis_index("c")
        local = [pltpu.make_async_copy(src[a] if gather[a] else src[a].at[me], dst[a].at[me], local_sems.at[a])
                 for a in range(n)]
        for copy in local:
            copy.start()
        copies = _exchange_copies(src, dst, send_sems, recv_sems, gather)
        for copy, _ in copies:
            copy.start()
        token[...] = jnp.zeros_like(token)
        for copy, landing in copies:
            copy.wait_send()
            landing.wait_recv()
        for copy in local:
            copy.wait()

    nsem = n * (NDEV - 1)
    out = pl.pallas_call(
        body, name=name, out_shape=outs + [_sds((SUBLANES, LANES))], in_specs=[ANY] * n,
        out_specs=[ANY] * n + [pl.BlockSpec(memory_space=pltpu.VMEM)],
        scratch_shapes=[pltpu.SemaphoreType.DMA((nsem,)), pltpu.SemaphoreType.DMA((nsem,)), pltpu.SemaphoreType.DMA((n,))],
    )(*srcs)
    return out[:n], out[n]


HBM = pl.BlockSpec(memory_space=pltpu.HBM)
SEM = pl.BlockSpec(memory_space=pltpu.SEMAPHORE)
EFFECT = pltpu.SideEffectType.DATAFLOW_SIDE_EFFECTING


def _own_chunk_copies(src, land, local_sems, gather):
    me = 4 * lax.axis_index("x") + 2 * lax.axis_index("y") + lax.axis_index("c")
    return [pltpu.make_async_copy(src[a] if gather[a] else src[a].at[me], land[a].at[me], local_sems.at[a])
            for a in range(len(src))]


def _exchange_start_groups(name, groups):
    srcs = [s for g_srcs, _ in groups for s in g_srcs]
    gathers = [g for _, g_gather in groups for g in g_gather]
    lands = [lax.empty(((NDEV,) + s.shape) if g else s.shape, s.dtype) for s, g in zip(srcs, gathers)]
    n, ng = len(srcs), len(groups)

    def body(*refs):
        src, land = refs[:n], refs[n:2 * n]
        sems = refs[2 * n:2 * n + 3 * ng]
        token = refs[-1]
        first = 0
        for g, (g_srcs, g_gather) in enumerate(groups):
            last = first + len(g_srcs)
            send, recv, local = sems[3 * g:3 * g + 3]
            for copy, _ in _exchange_copies(src[first:last], land[first:last], send, recv, g_gather):
                copy.start()
            for copy in _own_chunk_copies(src[first:last], land[first:last], local, g_gather):
                copy.start()
            first = last
        token[...] = jnp.zeros_like(token)

    hbm = lambda v: pltpu.HBM(v.shape, v.dtype)
    sem_shapes = []
    for g_srcs, _ in groups:
        sem_shapes += [pltpu.SemaphoreType.DMA((len(g_srcs) * (NDEV - 1),))] * 2 + [pltpu.SemaphoreType.DMA((len(g_srcs),))]
    out = pl.pallas_call(
        body, name=name,
        out_shape=(*sem_shapes, *[hbm(v) for v in srcs], *[hbm(v) for v in lands], _sds((SUBLANES, LANES))),
        in_specs=[HBM] * (2 * n),
        out_specs=(*([SEM] * (3 * ng)), *([HBM] * (2 * n)), pl.BlockSpec(memory_space=pltpu.VMEM)),
        input_output_aliases={i: 3 * ng + i for i in range(2 * n)},
        compiler_params=pltpu.CompilerParams(has_side_effects=EFFECT),
    )(*[pltpu.with_memory_space_constraint(v, pltpu.HBM) for v in srcs + lands])
    src_out, land_out = out[3 * ng:3 * ng + n], out[3 * ng + n:3 * ng + 2 * n]
    result, first = [], 0
    for g, (g_srcs, _) in enumerate(groups):
        last = first + len(g_srcs)
        result.append(((out[3 * g], out[3 * g + 2]), out[3 * g + 1], src_out[first:last], land_out[first:last]))
        first = last
    return result, out[-1]


def _exchange_start(name, srcs, gather):
    (group,), token = _exchange_start_groups(name, [(srcs, gather)])
    return (*group, token)


def _exchange_wait(name, send_sems, recv_sems, srcs, lands, gather, after):
    n = len(srcs)

    def body(*refs):
        src, land = refs[:n], refs[n:2 * n]
        send_ref, local_ref, recv_ref = refs[2 * n:2 * n + 3]
        for copy, landing in _exchange_copies(src, land, send_ref, recv_ref, gather):
            copy.wait_send()
            landing.wait_recv()
        for copy in _own_chunk_copies(src, land, local_ref, gather):
            copy.wait()

    hbm = lambda v: pltpu.HBM(v.shape, v.dtype)
    out = pl.pallas_call(
        body, name=name, out_shape=[hbm(v) for v in list(srcs) + list(lands)],
        in_specs=[HBM] * (2 * n) + [SEM, SEM, SEM, ANY], out_specs=[HBM] * (2 * n),
        input_output_aliases={i: i for i in range(2 * n)},
        compiler_params=pltpu.CompilerParams(has_side_effects=EFFECT),
    )(*srcs, *lands, send_sems[0], send_sems[1], recv_sems, after)
    return out[:n], out[n:]


def _matmul(name, a, b, mode, mnk, tiles, outs, a_spec=None, b_spec=None, a_fn=None, a_extra=(),
            epi=None, epi_extra=(), out_specs=None, b_slabs=1):
    m_, n_, k_ = mnk
    tm, tn, tk = tiles
    nk = k_ // tk
    if a_spec is None:
        a_spec = (pl.BlockSpec((tk, tm), lambda i, j, k: (k, i)) if mode == "tn"
                  else pl.BlockSpec((tm, tk), lambda i, j, k: (i, k)))
    if b_spec is None:
        b_spec = (pl.BlockSpec((tn, tk), lambda i, j, k: (j, k)) if mode == "nt"
                  else pl.BlockSpec((tk, tn), lambda i, j, k: (k, j)))
    if out_specs is None:
        out_specs = [pl.BlockSpec((tm, tn), lambda i, j, k: (i, j)) for _ in outs]
    na, ne, no = len(a_extra), len(epi_extra), len(outs)

    def body(*refs):
        a_ref, b_ref = refs[0], refs[1]
        ax = refs[2:2 + na]
        ex = refs[2 + na:2 + na + ne]
        o = refs[2 + na + ne:2 + na + ne + no]

        def finish(res):
            res = epi(res, *[r[...] for r in ex]) if epi is not None else (res,)
            for ref, val in zip(o, res):
                ref[...] = val.astype(ref.dtype)

        at = a_ref[...]
        if a_fn is not None:
            at = a_fn(at, *[r[...] for r in ax])
        at = at.astype(BF16)
        if b_slabs == 1:
            part = _dot(at, b_ref[...].astype(BF16), mode)
        else:
            ks = tk // b_slabs
            part = _dot(at[:, 0:ks], b_ref[0].astype(BF16), mode)
            for s in range(1, b_slabs):
                part = part + _dot(at[:, s * ks:(s + 1) * ks], b_ref[s].astype(BF16), mode)
        if nk == 1:
            finish(part)
            return
        acc = refs[-1]
        k = pl.program_id(2)

        @pl.when(k == 0)
        def _():
            acc[...] = part

        @pl.when(k > 0)
        def _():
            acc[...] += part

        @pl.when(k == nk - 1)
        def _():
            finish(acc[...])

    return pl.pallas_call(
        body, name=name, grid=(m_ // tm, n_ // tn, nk),
        in_specs=[a_spec, b_spec] + [s for _, s in a_extra] + [s for _, s in epi_extra],
        out_specs=out_specs, out_shape=[_sds(s, d) for s, d in outs],
        scratch_shapes=[pltpu.VMEM((tm, tn), F32)] if nk > 1 else [],
        compiler_params=_params(("parallel", "parallel", "arbitrary")),
    )(a, b, *[x for x, _ in a_extra], *[x for x, _ in epi_extra])


def _prenorm(name, x, ctx, gain, shsc):
    n_lat, n_ctx = x.shape[0] // ROW_BLOCK, ctx.shape[0] // ROW_BLOCK
    rows, d = (n_lat + n_ctx) * ROW_BLOCK, x.shape[1]

    def norm(src, g_ref, m_ref, o_ref, t_ref):
        xv = src[...]
        xh = xv * lax.rsqrt(jnp.mean(xv * xv, axis=-1, keepdims=True) + EPS_RMS)
        a = (xh * g_ref[...]) * (1.0 + m_ref[1:2, :]) + m_ref[0:1, :]
        o_ref[...] = a.astype(o_ref.dtype)
        t_ref[...] = a.T.astype(t_ref.dtype)

    def body(x_ref, c_ref, g_ref, m_ref, o_ref, t_ref):
        i = pl.program_id(0)

        @pl.when(i < n_lat)
        def _():
            norm(x_ref, g_ref, m_ref, o_ref, t_ref)

        @pl.when(i >= n_lat)
        def _():
            norm(c_ref, g_ref, m_ref, o_ref, t_ref)

    return pl.pallas_call(
        body, name=name, grid=(n_lat + n_ctx,),
        in_specs=[pl.BlockSpec((ROW_BLOCK, d), lambda i: (jnp.minimum(i, n_lat - 1), 0)),
                  pl.BlockSpec((ROW_BLOCK, d), lambda i: (jnp.maximum(i - n_lat, 0), 0)),
                  pl.BlockSpec((1, d), lambda i: (0, 0)),
                  pl.BlockSpec((None, 2, d), lambda i: (jnp.minimum(i // n_lat, 1), 0, 0))],
        out_specs=[pl.BlockSpec((ROW_BLOCK, d), lambda i: (i, 0)), pl.BlockSpec((d, ROW_BLOCK), lambda i: (0, i))],
        out_shape=[_sds((rows, d), BF16), _sds((d, rows), BF16)],
        compiler_params=_params(("parallel",)),
    )(x, ctx, gain, shsc)


def _norm_bwd(name, x, d_act, d_act_row0, gain, scale, res=None, aux=None):
    rows, d = x.shape
    nb = rows // ROW_BLOCK
    has_res = res is not None

    def body(*refs):
        if has_res:
            x_ref, da_ref, g_ref, sc_ref, r_ref, aux_ref, dx_ref, sums = refs
        else:
            x_ref, da_ref, g_ref, sc_ref, sums = refs
        i = pl.program_id(0)

        @pl.when(i == 0)
        def _():
            sums[...] = jnp.zeros_like(sums)

        xv, da = x_ref[...], da_ref[...]
        rstd = lax.rsqrt(jnp.mean(xv * xv, axis=-1, keepdims=True) + EPS_RMS)
        xh = xv * rstd
        g = g_ref[...]
        dn = da * (1.0 + sc_ref[...])
        sums[0] += _fold8(da)
        sums[1] += _fold8(da * (xh * g))
        sums[2] += _fold8(dn * xh)
        if has_res:
            dxh = dn * g
            dx = rstd * (dxh - xh * jnp.mean(dxh * xh, axis=-1, keepdims=True))
            rv = r_ref[...]
            dx_ref[...] = rv + dx
            sums[3] += _fold8(rv * aux_ref[...])

    row = lambda i: (i, 0)
    vec = pl.BlockSpec((1, d), lambda i: (0, 0))
    in_specs = [pl.BlockSpec((ROW_BLOCK, d), row), pl.BlockSpec((ROW_BLOCK, d), lambda i: (i + d_act_row0, 0)), vec, vec]
    args = [x, d_act, gain, scale]
    out_shape = [_sds((4, SUBLANES, d))]
    out_specs = [pl.BlockSpec((4, SUBLANES, d), lambda i: (0, 0, 0))]
    if has_res:
        in_specs += [pl.BlockSpec((ROW_BLOCK, d), row), pl.BlockSpec((ROW_BLOCK, d), row)]
        args += [res, aux]
        out_shape = [_sds((rows, d))] + out_shape
        out_specs = [pl.BlockSpec((ROW_BLOCK, d), row)] + out_specs
    return pl.pallas_call(
        body, name=name, grid=(nb,), in_specs=in_specs, out_specs=out_specs, out_shape=out_shape,
        compiler_params=_params(("arbitrary",)),
    )(*args)


def _ada_fwd(cond16, ada_w_loc, ada_b_loc):
    cols = ada_w_loc.shape[1]

    def body(c_ref, w_ref, b_ref, o_ref):
        s = _silu(c_ref[...]).astype(BF16)
        o_ref[...] = _dot(s, w_ref[...].astype(BF16), "nn") + b_ref[...]

    return pl.pallas_call(body, name="ada_fwd", out_shape=_sds((16, cols)), compiler_params=_params())(
        cond16, ada_w_loc, ada_b_loc)


def _ada_bwd(cond16, dmod16, ada_w_loc, c_ctx_row):
    k_, cols = ada_w_loc.shape

    def body(c_ref, dm_ref, w_ref, cc_ref, gw_ref, gc_ref):
        s = _silu(c_ref[...]).astype(BF16)
        dm = dm_ref[...]
        gw_ref[...] = _dot(s, dm.astype(BF16), "tn")
        dmc = jnp.sum(dm[8:16, :], axis=0, keepdims=True)
        dmc8 = jnp.broadcast_to(dmc, (SUBLANES, cols)).astype(BF16)
        ds = _dot(dmc8, w_ref[...].astype(BF16), "nt")
        row = lax.broadcasted_iota(jnp.int32, ds.shape, 0)
        gc_ref[...] = jnp.where(row == 0, ds * _dsilu(cc_ref[...]), 0.0)

    return pl.pallas_call(body, name="ada_bwd", out_shape=[_sds((k_, cols)), _sds((SUBLANES, k_))],
                          compiler_params=_params())(cond16, dmod16, ada_w_loc, c_ctx_row)


def _cmul(a, b):
    return a[0] * b[0] - a[1] * b[1], a[0] * b[1] + a[1] * b[0]


def _disc(lam_re, lam_im, ldt):
    dt = jnp.exp(ldt)
    mag = jnp.exp(lam_re * dt)
    th = lam_im * dt
    a_re, a_im = mag * jnp.cos(th), mag * jnp.sin(th)
    den = lam_re * lam_re + lam_im * lam_im
    n_re = a_re - 1.0
    f_re = (n_re * lam_re + a_im * lam_im) / den
    f_im = (a_im * lam_re - n_re * lam_im) / den
    return dt, mag, th, a_re, a_im, den, n_re, f_re, f_im


def _block_diag_mask(shape):
    row = lax.broadcasted_iota(jnp.int32, shape, 0)
    col = lax.broadcasted_iota(jnp.int32, shape, 1)
    return lax.shift_right_logical(row, 4) == lax.shift_right_logical(col, 6)


TAB_A = 0
TAB_BIG = 1
TAB_SEG = 4
TAB_PW = 5
TAB_ROWS = TAB_PW + STEPS


def _s5_discretise(name, ascending, lam_re, lam_im, ldt, bt_re, bt_im, ct_re, ct_im):
    def write_tables(ref, pw, big, asc, sign):
        row = lax.broadcasted_iota(jnp.int32, (SUBLANES, NSTATE), 0)
        full = lambda v: jnp.broadcast_to(v, (SUBLANES, NSTATE))

        def put(t, p):
            ref[0, t] = full(p[0])
            ref[1, t] = full(sign * p[1])

        put(TAB_A, pw[0])
        for t in range(3):
            put(TAB_BIG + t, big[t])
        seg = [big[0]]
        for _ in range(SEGMENTS - 1):
            seg.append(_cmul(seg[-1], big[0]))
        seg_re = jnp.zeros((SUBLANES, NSTATE), F32)
        seg_im = jnp.zeros((SUBLANES, NSTATE), F32)
        for r in range(SEGMENTS):
            p = seg[r] if asc else seg[SEGMENTS - 1 - r]
            seg_re = jnp.where(row == r, p[0], seg_re)
            seg_im = jnp.where(row == r, sign * p[1], seg_im)
        ref[0, TAB_SEG] = seg_re
        ref[1, TAB_SEG] = seg_im
        for k in range(STEPS):
            put(TAB_PW + k, pw[k])

    def body(lr_ref, li_ref, ldt_ref, br_ref, bi_ref, cr_ref, ci_ref, bb_ref, tab_ref, adj_ref, bm_ref, cm_ref):
        _, _, _, a_re, a_im, _, _, f_re, f_im = _disc(lr_ref[...], li_ref[...], ldt_ref[...])
        bre, bim = br_ref[...], bi_ref[...]
        bb_re = f_re * bre - f_im * bim
        bb_im = f_re * bim + f_im * bre
        bb_ref[0:S5_GROUP, :] = bb_re
        bb_ref[S5_GROUP:2 * S5_GROUP, :] = bb_im
        pw = [(a_re, a_im)]
        for _ in range(STEPS - 1):
            pw.append(_cmul(pw[-1], (a_re, a_im)))
        big = [pw[STEPS - 1]]
        for _ in range(2):
            big.append(_cmul(big[-1], big[-1]))
        write_tables(tab_ref, pw, big, ascending, 1.0)
        write_tables(adj_ref, pw, big, not ascending, -1.0)
        half = NSTATE // S5_BLOCKS
        mask = _block_diag_mask((S5_BLOCK_WIDTH, half))
        tile = lambda v: jnp.broadcast_to(v[None], (S5_BLOCK_WIDTH // S5_GROUP, S5_GROUP, half)).reshape(S5_BLOCK_WIDTH, half)
        for c in range(S5_BLOCKS):
            cols = slice(c * half, (c + 1) * half)
            rows = slice(c * S5_BLOCK_WIDTH, (c + 1) * S5_BLOCK_WIDTH)
            bm_ref[c, :, 0:half] = jnp.where(mask, tile(bb_re[:, cols]), 0.0).astype(BF16)
            bm_ref[c, :, half:2 * half] = jnp.where(mask, tile(bb_im[:, cols]), 0.0).astype(BF16)
            cm_ref[c, :, 0:half] = jnp.where(mask, cr_ref[rows, :], 0.0).astype(BF16)
            cm_ref[c, :, half:2 * half] = jnp.where(mask, -ci_ref[rows, :], 0.0).astype(BF16)

    blocked = _sds((S5_BLOCKS, S5_BLOCK_WIDTH, 2 * NSTATE // S5_BLOCKS), BF16)
    return pl.pallas_call(
        body, name=name,
        out_shape=[_sds((2 * S5_GROUP, NSTATE)), _sds((2, TAB_ROWS, SUBLANES, NSTATE)),
                   _sds((2, TAB_ROWS, SUBLANES, NSTATE)), blocked, blocked],
        compiler_params=_params(),
    )(lam_re, lam_im, ldt, bt_re, bt_im, ct_re, ct_im)


def _s5_discretise_bwd(name, lam_re, lam_im, ldt, bt_re, bt_im, d_abar8, d_bbar):
    def body(lr_ref, li_ref, ldt_ref, br_ref, bi_ref, da_ref, db_ref, dl_ref, dbt_ref):
        lam_re, lam_im = lr_ref[...], li_ref[...]
        dt, _, _, a_re, a_im, den, n_re, f_re, f_im = _disc(lam_re, lam_im, ldt_ref[...])
        bre, bim = br_ref[...], bi_ref[...]
        dbr, dbi = db_ref[0:S5_GROUP, :], db_ref[S5_GROUP:2 * S5_GROUP, :]
        dbt_ref[0:S5_GROUP, :] = f_re * dbr + f_im * dbi
        dbt_ref[S5_GROUP:2 * S5_GROUP, :] = f_re * dbi - f_im * dbr
        df_re = jnp.sum(bre * dbr + bim * dbi, axis=0, keepdims=True)
        df_im = jnp.sum(bre * dbi - bim * dbr, axis=0, keepdims=True)
        da = da_ref[...]
        da_re = jnp.sum(da[:, 0:NSTATE], axis=0, keepdims=True)
        da_im = jnp.sum(da[:, NSTATE:2 * NSTATE], axis=0, keepdims=True)
        da_re = da_re + (df_re * lam_re - df_im * lam_im) / den
        da_im = da_im + (df_re * lam_im + df_im * lam_re) / den
        ff = (f_re * df_re + f_im * df_im) * 2.0 / den
        d_lr = (df_re * n_re + df_im * a_im) / den - ff * lam_re
        d_li = (df_re * a_im - df_im * n_re) / den - ff * lam_im
        d_mag_mag = da_re * a_re + da_im * a_im
        d_th = da_im * a_re - da_re * a_im
        d_lr = d_lr + d_mag_mag * dt
        d_li = d_li + d_th * dt
        d_ldt = (d_mag_mag * lam_re + d_th * lam_im) * dt
        row = lax.broadcasted_iota(jnp.int32, (SUBLANES, NSTATE), 0)
        dl_ref[...] = jnp.where(row == 0, d_lr, jnp.where(row == 1, d_li, jnp.where(row == 2, d_ldt, 0.0)))

    return pl.pallas_call(
        body, name=name, out_shape=[_sds((SUBLANES, NSTATE)), _sds((2 * S5_GROUP, NSTATE))],
        compiler_params=_params(),
    )(lam_re, lam_im, ldt, bt_re, bt_im, d_abar8, d_bbar)


def _segment_permutation(reverse_time):
    rho = jnp.arange(ROW_BLOCK)
    src = STEPS * (rho % SEGMENTS) + rho // SEGMENTS
    if reverse_time:
        src = ROW_BLOCK - 1 - src
    return (src[:, None] == jnp.arange(ROW_BLOCK)[None, :]).astype(BF16)


def _permute_rows(perm_ref, v):
    return _dot(perm_ref[...], v, "nn").astype(BF16)


def _unpermute_rows(perm_t_ref, v):
    hi = v.astype(BF16)
    lo = (v - hi.astype(F32)).astype(BF16)
    return _dot(perm_t_ref[...], hi, "nn") + _dot(perm_t_ref[...], lo, "nn")


def _unrolled_loop(step, init):
    def trip(o, state):
        for u in range(SCAN_UNROLL):
            state = step(o * SCAN_UNROLL + u, state)
        return state

    if SCAN_UNROLL == STEPS:
        return trip(0, init)
    return lax.fori_loop(0, STEPS // SCAN_UNROLL, trip, init)


def _scan_chunk(x_ref, out_ref, tab_ref, carry_re, carry_im, ascending, pair_ref=None, acc_ref=None, lane_chunks=None):
    w = SCAN_LANES
    half = NSTATE // S5_BLOCKS
    row = lax.broadcasted_iota(jnp.int32, (SUBLANES, w), 0)
    last = (SEGMENTS - 1) if ascending else 0

    def from_previous_segment(v, k, fill):
        if ascending:
            return jnp.where(row >= k, pltpu.roll(v, k, 0), fill)
        return jnp.where(row < SEGMENTS - k, pltpu.roll(v, SEGMENTS - k, 0), fill)

    def tile_rows(k):
        return pl.ds(pl.multiple_of((k if ascending else STEPS - 1 - k) * SUBLANES, SUBLANES), SUBLANES)

    for j in (range(NSTATE // w) if lane_chunks is None else lane_chunks):
        n_l = pl.ds(j * w, w)
        lane0 = (j * w // half) * 2 * half + (j * w) % half
        re_l, im_l = pl.ds(lane0, w), pl.ds(lane0 + half, w)
        tab = lambda t, n_l=n_l: (tab_ref[0, t, :, n_l], tab_ref[1, t, :, n_l])
        a_re, a_im = tab(TAB_A)

        def local_step(k, h):
            rs = tile_rows(k)
            h_re = a_re * h[0] - a_im * h[1] + x_ref[rs, re_l]
            h_im = a_re * h[1] + a_im * h[0] + x_ref[rs, im_l]
            out_ref[rs, re_l] = h_re
            out_ref[rs, im_l] = h_im
            return h_re, h_im

        zero = jnp.zeros((SUBLANES, w), F32)
        end_re, end_im = _unrolled_loop(local_step, (zero, zero))
        for t, k in ((TAB_BIG, 1), (TAB_BIG + 1, 2), (TAB_BIG + 2, 4)):
            p_re, p_im = tab(t)
            s_re, s_im = from_previous_segment(end_re, k, 0.0), from_previous_segment(end_im, k, 0.0)
            end_re, end_im = end_re + (p_re * s_re - p_im * s_im), end_im + (p_re * s_im + p_im * s_re)
        c0_re, c0_im = carry_re[:, n_l], carry_im[:, n_l]
        p_re, p_im = tab(TAB_SEG)
        end_re = end_re + (p_re * c0_re - p_im * c0_im)
        end_im = end_im + (p_re * c0_im + p_im * c0_re)
        carry_re[:, n_l] = jnp.broadcast_to(end_re[last:last + 1, :], end_re.shape)
        carry_im[:, n_l] = jnp.broadcast_to(end_im[last:last + 1, :], end_im.shape)
        in_re = from_previous_segment(end_re, 1, c0_re)
        in_im = from_previous_segment(end_im, 1, c0_im)

        def carry_step(k, st):
            rs = tile_rows(k)
            p_re, p_im = tab_ref[0, TAB_PW + k, :, n_l], tab_ref[1, TAB_PW + k, :, n_l]
            o_re = out_ref[rs, re_l] + (p_re * in_re - p_im * in_im)
            o_im = out_ref[rs, im_l] + (p_re * in_im + p_im * in_re)
            out_ref[rs, re_l] = o_re
            out_ref[rs, im_l] = o_im
            if pair_ref is None:
                return st
            s_re, s_im = pair_ref[rs, re_l], pair_ref[rs, im_l]
            return (o_re, o_im, st[2] + (st[0] * s_re + st[1] * s_im), st[3] + (st[1] * s_re - st[0] * s_im))

        if pair_ref is None:
            _unrolled_loop(carry_step, 0)
        else:
            fin = _unrolled_loop(carry_step, (in_re, in_im, zero, zero))
            acc_ref[:, n_l] += fin[2]
            acc_ref[:, pl.ds(NSTATE + j * w, w)] += fin[3]


def _scan_block_index(i, n_lat, ctx_first_then_ascending):
    if ctx_first_then_ascending:
        return jnp.where(i == 0, n_lat, i - 1)
    return jnp.where(i == 0, n_lat, n_lat - i)


def _full_spec(shape):
    return pl.BlockSpec(shape, lambda i: (0,) * len(shape))


_S5_BLOCKED = (S5_BLOCKS, S5_BLOCK_WIDTH, 2 * NSTATE // S5_BLOCKS)
_S5_TABLES = (2, TAB_ROWS, SUBLANES, NSTATE)
_S5_DIAG = (S5_BLOCKS, S5_GROUP, 2 * NSTATE // S5_BLOCKS)


def _s5_scan_fwd(name, ascending, z_all, bmat, cmat, tab, perm, perm_t, y_other=None, d_skip=None, w_glu=None):
    rows = z_all.shape[0]
    nb = rows // ROW_BLOCK
    n_lat = nb - 1
    bw, sw = S5_BLOCK_WIDTH, 2 * NSTATE // S5_BLOCKS
    gated = y_other is not None

    def body(*refs):
        u_ref, bm_ref, cm_ref, tab_ref, p_ref, pt_ref = refs[:6]
        extra = refs[6:9] if gated else ()
        s_ref, y_ref = refs[6 + len(extra):8 + len(extra)]
        bu, yp, carry_re, carry_im = refs[-4:]

        @pl.when(pl.program_id(0) == 0)
        def _():
            carry_re[...] = jnp.zeros_like(carry_re)
            carry_im[...] = jnp.zeros_like(carry_im)

        up = _permute_rows(p_ref, u_ref[...].astype(BF16))
        for c in range(S5_BLOCKS):
            bu[:, c * sw:(c + 1) * sw] = _dot(up[:, c * bw:(c + 1) * bw], bm_ref[c], "nn")
        _scan_chunk(bu, s_ref, tab_ref, carry_re, carry_im, False)
        for c in range(S5_BLOCKS):
            yp[:, c * bw:(c + 1) * bw] = _dot(s_ref[:, c * sw:(c + 1) * sw].astype(BF16), cm_ref[c], "nt")
        y = _unpermute_rows(pt_ref, yp[...])
        y_ref[...] = y
        if gated:
            y_other_ref, d_ref, w_ref = extra
            gel = _gelu(d_ref[...] * u_ref[...] + y_other_ref[...] + y)
            refs[8 + len(extra)][...] = (gel * _sigmoid(_dot(gel.astype(BF16), w_ref[...], "nn"))).astype(BF16)

    blk = lambda i: (_scan_block_index(i, n_lat, ascending), 0)
    in_specs = [pl.BlockSpec((ROW_BLOCK, S5_WIDTH), blk), _full_spec(_S5_BLOCKED), _full_spec(_S5_BLOCKED),
                _full_spec(_S5_TABLES), _full_spec((ROW_BLOCK, ROW_BLOCK)), _full_spec((ROW_BLOCK, ROW_BLOCK))]
    args = [z_all, bmat, cmat, tab, perm, perm_t]
    out_specs = [pl.BlockSpec((ROW_BLOCK, 2 * NSTATE), blk), pl.BlockSpec((ROW_BLOCK, S5_WIDTH), blk)]
    out_shape = [_sds((rows, 2 * NSTATE)), _sds((rows, S5_WIDTH))]
    if gated:
        in_specs += [pl.BlockSpec((ROW_BLOCK, S5_WIDTH), blk), _full_spec((1, S5_WIDTH)), _full_spec((S5_WIDTH, S5_WIDTH))]
        args += [y_other, d_skip, w_glu]
        out_specs.append(pl.BlockSpec((ROW_BLOCK, S5_WIDTH),
                                      lambda i: (jnp.minimum(_scan_block_index(i, n_lat, ascending), n_lat - 1), 0)))
        out_shape.append(_sds((n_lat * ROW_BLOCK, S5_WIDTH + CONV_WIDTH), BF16))
    return pl.pallas_call(
        body, name=name, grid=(nb,), in_specs=in_specs, out_specs=out_specs, out_shape=out_shape,
        scratch_shapes=[pltpu.VMEM((ROW_BLOCK, 2 * NSTATE), F32), pltpu.VMEM((ROW_BLOCK, S5_WIDTH), F32),
                        pltpu.VMEM((SUBLANES, NSTATE), F32), pltpu.VMEM((SUBLANES, NSTATE), F32)],
        compiler_params=_params(("arbitrary",)),
    )(*args)


def _s5_scan_bwd(name, ascending, dy, z_all, states, bmat, cmat, adj, perm, perm_t, du_other=None, d_skip=None):
    rows = states.shape[0]
    nb = rows // ROW_BLOCK
    n_lat = nb - 1
    bw, sw = S5_BLOCK_WIDTH, 2 * NSTATE // S5_BLOCKS
    finish = du_other is not None

    def block_index(i):
        if ascending:
            return jnp.where(i == nb - 1, n_lat, n_lat - 1 - i)
        return jnp.where(i == nb - 1, n_lat, i)

    def body(*refs):
        dy_ref, u_ref, s_ref, bm_ref, cm_ref, adj_ref, p_ref, pt_ref = refs[:8]
        extra = refs[8:10] if finish else ()
        du_ref, db_ref, dc_ref, da_ref, g, dup, db_acc, dc_acc, carry_re, carry_im = refs[8 + len(extra):]
        i = pl.program_id(0)

        @pl.when(i == 0)
        def _():
            carry_re[...] = jnp.zeros_like(carry_re)
            carry_im[...] = jnp.zeros_like(carry_im)
            da_ref[...] = jnp.zeros_like(da_ref)
            db_acc[...] = jnp.zeros_like(db_acc)
            dc_acc[...] = jnp.zeros_like(dc_acc)

        has_dy = (i < nb - 1).astype(F32)
        dyp = _permute_rows(p_ref, (dy_ref[...] * has_dy).astype(BF16))
        up = _permute_rows(p_ref, u_ref[...].astype(BF16))
        for c in range(S5_BLOCKS):
            g[:, c * sw:(c + 1) * sw] = _dot(dyp[:, c * bw:(c + 1) * bw], cm_ref[c], "nn")
            dc_acc[c] += _dot(dyp[:, c * bw:(c + 1) * bw], s_ref[:, c * sw:(c + 1) * sw].astype(BF16), "tn")
            _scan_chunk(g, g, adj_ref, carry_re, carry_im, True, pair_ref=s_ref, acc_ref=da_ref, lane_chunks=[c])
            gc = g[:, c * sw:(c + 1) * sw].astype(BF16)
            dup[:, c * bw:(c + 1) * bw] = _dot(gc, bm_ref[c], "nt")
            db_acc[c] += _dot(up[:, c * bw:(c + 1) * bw], gc, "tn")
        du = _unpermute_rows(pt_ref, dup[...])
        if finish:
            du = du + extra[0][...] + (dy_ref[...] * has_dy) * extra[1][...]
        du_ref[...] = du.astype(du_ref.dtype)

        @pl.when(i == nb - 1)
        def _():
            mask = _block_diag_mask((bw, sw // 2))
            for acc, out in ((db_acc, db_ref), (dc_acc, dc_ref)):
                for c in range(S5_BLOCKS):
                    for part in range(2):
                        cols = slice(part * (sw // 2), (part + 1) * (sw // 2))
                        kept = jnp.where(mask, acc[c, :, cols], 0.0)
                        out[c, :, cols] = kept.reshape(bw // S5_GROUP, S5_GROUP, sw // 2).sum(axis=0)

    blk = lambda i: (block_index(i), 0)
    in_specs = [pl.BlockSpec((ROW_BLOCK, S5_WIDTH), lambda i: (jnp.minimum(block_index(i), n_lat - 1), 0)),
                pl.BlockSpec((ROW_BLOCK, S5_WIDTH), blk), pl.BlockSpec((ROW_BLOCK, 2 * NSTATE), blk),
                _full_spec(_S5_BLOCKED), _full_spec(_S5_BLOCKED), _full_spec(_S5_TABLES),
                _full_spec((ROW_BLOCK, ROW_BLOCK)), _full_spec((ROW_BLOCK, ROW_BLOCK))]
    args = [dy, z_all, states, bmat, cmat, adj, perm, perm_t]
    if finish:
        in_specs += [pl.BlockSpec((ROW_BLOCK, S5_WIDTH), blk), _full_spec((1, S5_WIDTH))]
        args += [du_other, d_skip]
    return pl.pallas_call(
        body, name=name, grid=(nb,), in_specs=in_specs,
        out_specs=[pl.BlockSpec((ROW_BLOCK, S5_WIDTH), blk), _full_spec(_S5_DIAG), _full_spec(_S5_DIAG),
                   _full_spec((SUBLANES, 2 * NSTATE))],
        out_shape=[_sds((rows, S5_WIDTH), BF16 if finish else F32), _sds(_S5_DIAG), _sds(_S5_DIAG),
                   _sds((SUBLANES, 2 * NSTATE))],
        scratch_shapes=[pltpu.VMEM((ROW_BLOCK, 2 * NSTATE), F32), pltpu.VMEM((ROW_BLOCK, S5_WIDTH), F32),
                        pltpu.VMEM(_S5_BLOCKED, F32), pltpu.VMEM(_S5_BLOCKED, F32),
                        pltpu.VMEM((SUBLANES, NSTATE), F32), pltpu.VMEM((SUBLANES, NSTATE), F32)],
        compiler_params=_params(("arbitrary",)),
    )(*args)


def _latent_row_tile(n_rows):
    return 512 if n_rows % 512 == 0 else ROW_BLOCK


def _glu_bwd(d_ycat, z_all, y0, y1, d_skip, w_glu, n_rows):
    def body(do_ref, u_ref, y0_ref, y1_ref, d_ref, w_ref, dy_ref, dw_ref, dd_ref):
        @pl.when(pl.program_id(0) == 0)
        def _():
            dw_ref[...] = jnp.zeros_like(dw_ref)
            dd_ref[...] = jnp.zeros_like(dd_ref)

        u = u_ref[...]
        y = d_ref[...] * u + y0_ref[...] + y1_ref[...]
        g = _gelu(y)
        gb = g.astype(BF16)
        w = w_ref[...]
        sg = _sigmoid(_dot(gb, w, "nn"))
        do = do_ref[...]
        dt = do * g * sg * (1.0 - sg)
        dtb = dt.astype(BF16)
        dg = do * sg + _dot(dtb, w, "nt")
        dy = dg * _dgelu(y)
        dy_ref[...] = dy
        dw_ref[...] += _dot(gb, dtb, "tn")
        dd_ref[...] += _fold8(dy * u)

    rows = _latent_row_tile(n_rows)
    row = pl.BlockSpec((rows, S5_WIDTH), lambda i: (i, 0))
    sq = pl.BlockSpec((S5_WIDTH, S5_WIDTH), lambda i: (0, 0))
    return pl.pallas_call(
        body, name="glu_bwd", grid=(n_rows // rows,),
        in_specs=[row, row, row, row, pl.BlockSpec((1, S5_WIDTH), lambda i: (0, 0)), sq],
        out_specs=[row, sq, pl.BlockSpec((SUBLANES, S5_WIDTH), lambda i: (0, 0))],
        out_shape=[_sds((n_rows, S5_WIDTH)), _sds((S5_WIDTH, S5_WIDTH)), _sds((SUBLANES, S5_WIDTH))],
        compiler_params=_params(("arbitrary",)),
    )(d_ycat, z_all, y0, y1, d_skip, w_glu)


CONV_HALF = CONV_K // 2


def _conv_block(n_rows):
    blk = min(1024, n_rows)
    assert blk >= CONV_HALF * GRID_W and n_rows % blk == 0
    return blk


def _conv_gate(z_all, n_rows):
    blk = _conv_block(n_rows)
    nb = n_rows // blk

    def body(v_ref, g_ref, o_ref):
        i = pl.program_id(0)
        inside = jnp.logical_and(i >= 1, i <= nb)

        @pl.when(inside)
        def _():
            o_ref[...] = v_ref[...] * _sigmoid(g_ref[...])

        @pl.when(jnp.logical_not(inside))
        def _():
            o_ref[...] = jnp.zeros_like(o_ref)

    src = lambda col: pl.BlockSpec((blk, CONV_WIDTH), lambda i: (jnp.clip(i - 1, 0, nb - 1), col))
    return pl.pallas_call(
        body, name="conv_gate", grid=(nb + 2,), in_specs=[src(1), src(2)],
        out_specs=pl.BlockSpec((blk, CONV_WIDTH), lambda i: (i, 0)),
        out_shape=_sds(((nb + 2) * blk, CONV_WIDTH)), compiler_params=_params(("parallel",)),
    )(z_all, z_all)


def _stream_padded(pad_ref, buf, sems, blk, n_blocks):
    i = pl.program_id(0)

    def copy(b):
        rows = pl.ds(pl.multiple_of(b * blk, blk), blk)
        return pltpu.make_async_copy(pad_ref.at[rows, :], buf.at[rows, :], sems.at[b])

    @pl.when(i == 0)
    def _():
        for b in range(n_blocks):
            copy(b).start()
        copy(0).wait()
        copy(1).wait()

    copy(i + 2).wait()
    return pl.multiple_of(i * blk, blk)


def _conv_fwd(hh_pad, w, b, ln_g, ln_b, ycat, n_rows):
    blk = _conv_block(n_rows)
    nblk = n_rows // blk + 2

    def body(hh_ref, w_ref, b_ref, g_ref, lb_ref, ycat_ref, hc_ref, y_ref, win, sems):
        base = _stream_padded(hh_ref, win, sems, blk, nblk)

        def tile(t, _):
            r0 = pl.multiple_of(t * CONV_ROWS, CONV_ROWS)
            acc = jnp.zeros((CONV_ROWS, CONV_WIDTH), F32)
            for k in range(CONV_K):
                acc = acc + w_ref[k:k + 1, :] * win[pl.ds(base + r0 + blk + (k - CONV_HALF) * GRID_W, CONV_ROWS), :]
            hc = acc + b_ref[...]
            hc_ref[pl.ds(r0, CONV_ROWS), :] = hc
            mu = jnp.mean(hc, axis=-1, keepdims=True)
            xc = hc - mu
            ln = xc * lax.rsqrt(jnp.mean(xc * xc, axis=-1, keepdims=True) + EPS_LN) * g_ref[...] + lb_ref[...]
            y_ref[pl.ds(r0, CONV_ROWS), :] = _silu(ln).astype(y_ref.dtype)
            return 0

        lax.fori_loop(0, blk // CONV_ROWS, tile, 0)

    vec = pl.BlockSpec((1, CONV_WIDTH), lambda i: (0, 0))
    row = pl.BlockSpec((blk, CONV_WIDTH), lambda i: (i, 0))
    return pl.pallas_call(
        body, name="conv_fwd", grid=(n_rows // blk,),
        in_specs=[ANY, pl.BlockSpec((CONV_K, CONV_WIDTH), lambda i: (0, 0)), vec, vec, vec, ANY],
        out_specs=[row, pl.BlockSpec((blk, CONV_WIDTH), lambda i: (i, 1))],
        out_shape=[_sds((n_rows, CONV_WIDTH)), _sds(ycat.shape, ycat.dtype)], input_output_aliases={5: 1},
        scratch_shapes=[pltpu.VMEM((nblk * blk, CONV_WIDTH), F32), pltpu.SemaphoreType.DMA((nblk,))],
        compiler_params=_params(("arbitrary",)),
    )(hh_pad, w, b, ln_g, ln_b, ycat)


def _conv_bwd_norm(d_ycat, hc, ln_g, ln_b, n_rows):
    blk = _conv_block(n_rows)
    nb = n_rows // blk

    def body(dy_ref, hc_ref, g_ref, lb_ref, o_ref, sums):
        i = pl.program_id(0)

        @pl.when(i == 0)
        def _():
            sums[...] = jnp.zeros_like(sums)

        inside = jnp.logical_and(i >= 1, i <= nb)

        @pl.when(inside)
        def _():
            hcv = hc_ref[...]
            mu = jnp.mean(hcv, axis=-1, keepdims=True)
            xc = hcv - mu
            rstd = lax.rsqrt(jnp.mean(xc * xc, axis=-1, keepdims=True) + EPS_LN)
            xh = xc * rstd
            g = g_ref[...]
            dln = dy_ref[...] * _dsilu(xh * g + lb_ref[...])
            dxh = dln * g
            dhc = rstd * (dxh - jnp.mean(dxh, axis=-1, keepdims=True) - xh * jnp.mean(dxh * xh, axis=-1, keepdims=True))
            o_ref[...] = dhc
            sums[0] += _fold8(dhc)
            sums[1] += _fold8(dln * xh)
            sums[2] += _fold8(dln)

        @pl.when(jnp.logical_not(inside))
        def _():
            o_ref[...] = jnp.zeros_like(o_ref)

    vec = pl.BlockSpec((1, CONV_WIDTH), lambda i: (0, 0))
    return pl.pallas_call(
        body, name="conv_bwd_norm", grid=(nb + 2,),
        in_specs=[pl.BlockSpec((blk, CONV_WIDTH), lambda i: (jnp.clip(i - 1, 0, nb - 1), 1)),
                  pl.BlockSpec((blk, CONV_WIDTH), lambda i: (jnp.clip(i - 1, 0, nb - 1), 0)), vec, vec],
        out_specs=[pl.BlockSpec((blk, CONV_WIDTH), lambda i: (i, 0)),
                   pl.BlockSpec((3, SUBLANES, CONV_WIDTH), lambda i: (0, 0, 0))],
        out_shape=[_sds(((nb + 2) * blk, CONV_WIDTH)), _sds((3, SUBLANES, CONV_WIDTH))],
        compiler_params=_params(("arbitrary",)),
    )(d_ycat, hc, ln_g, ln_b)


def _conv_bwd_taps(dhc_pad, hh_pad, z_all, w, n_rows):
    blk = _conv_block(n_rows)
    nblk = n_rows // blk + 2

    def body(dhc_ref, hh_ref, v_ref, g_ref, w_ref, dv_ref, dg_ref, dw_ref, dwin, hwin, dsems, hsems):
        @pl.when(pl.program_id(0) == 0)
        def _():
            dw_ref[...] = jnp.zeros_like(dw_ref)

        base = _stream_padded(dhc_ref, dwin, dsems, blk, nblk)
        _stream_padded(hh_ref, hwin, hsems, blk, nblk)

        def tile(t, _):
            r0 = pl.multiple_of(t * CONV_BWD_ROWS, CONV_BWD_ROWS) + base
            dh = dwin[pl.ds(r0 + blk, CONV_BWD_ROWS), :]
            acc = jnp.zeros((CONV_BWD_ROWS, CONV_WIDTH), F32)
            for k in range(CONV_K):
                off = (k - CONV_HALF) * GRID_W
                acc = acc + w_ref[k:k + 1, :] * dwin[pl.ds(r0 + blk - off, CONV_BWD_ROWS), :]
                dw_ref[k] += _fold8(dh * hwin[pl.ds(r0 + blk + off, CONV_BWD_ROWS), :])
            rs = pl.ds(pl.multiple_of(t * CONV_BWD_ROWS, CONV_BWD_ROWS), CONV_BWD_ROWS)
            sg = _sigmoid(g_ref[rs, :])
            vv = v_ref[rs, :]
            dv_ref[rs, :] = (acc * sg).astype(dv_ref.dtype)
            dg_ref[rs, :] = (acc * vv * sg * (1.0 - sg)).astype(dg_ref.dtype)
            return 0

        lax.fori_loop(0, blk // CONV_BWD_ROWS, tile, 0)

    row = pl.BlockSpec((blk, CONV_WIDTH), lambda i: (i, 0))
    return pl.pallas_call(
        body, name="conv_bwd_taps", grid=(n_rows // blk,),
        in_specs=[ANY, ANY,
            pl.BlockSpec((blk, CONV_WIDTH), lambda i: (i, 1)), pl.BlockSpec((blk, CONV_WIDTH), lambda i: (i, 2)),
            pl.BlockSpec((CONV_K, CONV_WIDTH), lambda i: (0, 0))],
        out_specs=[row, row, pl.BlockSpec((CONV_K, SUBLANES, CONV_WIDTH), lambda i: (0, 0, 0))],
        out_shape=[_sds((n_rows, CONV_WIDTH), BF16), _sds((n_rows, CONV_WIDTH), BF16),
                   _sds((CONV_K, SUBLANES, CONV_WIDTH))],
        scratch_shapes=[pltpu.VMEM((nblk * blk, CONV_WIDTH), F32), pltpu.VMEM((nblk * blk, CONV_WIDTH), F32),
                        pltpu.SemaphoreType.DMA((nblk,)), pltpu.SemaphoreType.DMA((nblk,))],
        compiler_params=_params(("arbitrary",)),
    )(dhc_pad, hh_pad, z_all, z_all, w)


def _sum_parts(parts):
    _, r, c = parts.shape

    def body(p_ref, o_ref):
        acc = p_ref[0]
        for q in range(1, NDEV):
            acc = acc + p_ref[q]
        o_ref[...] = acc

    return pl.pallas_call(body, name="sum_parts", out_shape=_sds((r, c)), compiler_params=_params())(parts)


def _row_tile(r, c):
    best = r
    for t in (1024, 512, 256, 128, 64, 32, 16, 8):
        if r % t == 0 and t * c <= 128 * 1024:
            return t
    return best


def _adamw(name, w, gparts, m, v):
    r, c = w.shape
    np_ = gparts.shape[0]
    tr = _row_tile(r, c)

    def body(w_ref, g_ref, m_ref, v_ref, go_ref, d_ref, mo_ref, vo_ref):
        g = g_ref[0].astype(F32)
        for q in range(1, np_):
            g = g + g_ref[q].astype(F32)
        m2 = ADAM_B1 * m_ref[...] + (1.0 - ADAM_B1) * g
        v2 = ADAM_B2 * v_ref[...] + (1.0 - ADAM_B2) * jnp.square(g)
        m_hat = m2 / (1.0 - ADAM_B1 ** ADAM_STEP)
        v_hat = v2 / (1.0 - ADAM_B2 ** ADAM_STEP)
        go_ref[...] = g
        d_ref[...] = -ADAM_LR * (m_hat / (jnp.sqrt(v_hat) + ADAM_EPS) + ADAM_WD * w_ref[...])
        mo_ref[...] = m2
        vo_ref[...] = v2

    row = pl.BlockSpec((tr, c), lambda i: (i, 0))
    return pl.pallas_call(
        body, name=name, grid=(r // tr,),
        in_specs=[row, pl.BlockSpec((np_, tr, c), lambda i: (0, i, 0)), row, row],
        out_specs=[row] * 4, out_shape=[_sds((r, c))] * 4, compiler_params=_params(("parallel",)),
    )(w, gparts, m, v)


def _adamw_native(name, w, g, m, v):
    def body(w_ref, g_ref, m_ref, v_ref, d_ref, mo_ref, vo_ref):
        gv = g_ref[...]
        m2 = ADAM_B1 * m_ref[...] + (1.0 - ADAM_B1) * gv
        v2 = ADAM_B2 * v_ref[...] + (1.0 - ADAM_B2) * jnp.square(gv)
        m_hat = m2 / (1.0 - ADAM_B1 ** ADAM_STEP)
        v_hat = v2 / (1.0 - ADAM_B2 ** ADAM_STEP)
        d_ref[...] = -ADAM_LR * (m_hat / (jnp.sqrt(v_hat) + ADAM_EPS) + ADAM_WD * w_ref[...])
        mo_ref[...] = m2
        vo_ref[...] = v2

    return pl.pallas_call(body, name=name, out_shape=[_sds(w.shape)] * 3, compiler_params=_params())(w, g, m, v)


SMALL = ["c_ctx", "ada_b", "norm1_g", "s5_lam_re", "s5_lam_im", "s5_log_dt", "s5_d", "conv_b", "conv_ln_g", "conv_ln_b",
         "norm2_g", "final_g"]
SMALL_PACKED_ROWS = 24


def _pack_rows(parts, rows):
    flat = jnp.concatenate([p.reshape(-1).astype(F32) for p in parts])
    return jnp.pad(flat, (0, rows * D_MODEL - flat.shape[0])).reshape(rows, D_MODEL)


def _unpack_rows(packed, shapes):
    flat = packed.reshape(-1)
    out, off = [], 0
    for shape in shapes:
        size = 1
        for s in shape:
            size *= s
        out.append(flat[off:off + size].reshape(shape))
        off += size
    return out


def kernel(x, c, ctx, c_ctx, ada_w, ada_b, norm1_g, w_in, s5_lam_re, s5_lam_im, s5_log_dt, s5_b_re, s5_b_im, s5_c_re, s5_c_im, s5_d, s5_w_glu, conv_w, conv_b, conv_ln_g, conv_ln_b, w_out, norm2_g, mlp_w1, mlp_w2, final_g, loss_target, m_c_ctx, m_ada_w, m_ada_b, m_norm1_g, m_w_in, m_s5_lam_re, m_s5_lam_im, m_s5_log_dt, m_s5_b_re, m_s5_b_im, m_s5_c_re, m_s5_c_im, m_s5_d, m_s5_w_glu, m_conv_w, m_conv_b, m_conv_ln_g, m_conv_ln_b, m_w_out, m_norm2_g, m_mlp_w1, m_mlp_w2, m_final_g, v_c_ctx, v_ada_w, v_ada_b, v_norm1_g, v_w_in, v_s5_lam_re, v_s5_lam_im, v_s5_log_dt, v_s5_b_re, v_s5_b_im, v_s5_c_re, v_s5_c_im, v_s5_d, v_s5_w_glu, v_conv_w, v_conv_b, v_conv_ln_g, v_conv_ln_b, v_w_out, v_norm2_g, v_mlp_w1, v_mlp_w2, v_final_g):
    weights = dict(c_ctx=c_ctx, ada_w=ada_w, ada_b=ada_b, norm1_g=norm1_g, w_in=w_in, s5_lam_re=s5_lam_re, s5_lam_im=s5_lam_im, s5_log_dt=s5_log_dt, s5_b_re=s5_b_re, s5_b_im=s5_b_im, s5_c_re=s5_c_re, s5_c_im=s5_c_im, s5_d=s5_d, s5_w_glu=s5_w_glu, conv_w=conv_w, conv_b=conv_b, conv_ln_g=conv_ln_g, conv_ln_b=conv_ln_b, w_out=w_out, norm2_g=norm2_g, mlp_w1=mlp_w1, mlp_w2=mlp_w2, final_g=final_g)
    mom1 = dict(c_ctx=m_c_ctx, ada_w=m_ada_w, ada_b=m_ada_b, norm1_g=m_norm1_g, w_in=m_w_in, s5_lam_re=m_s5_lam_re, s5_lam_im=m_s5_lam_im, s5_log_dt=m_s5_log_dt, s5_b_re=m_s5_b_re, s5_b_im=m_s5_b_im, s5_c_re=m_s5_c_re, s5_c_im=m_s5_c_im, s5_d=m_s5_d, s5_w_glu=m_s5_w_glu, conv_w=m_conv_w, conv_b=m_conv_b, conv_ln_g=m_conv_ln_g, conv_ln_b=m_conv_ln_b, w_out=m_w_out, norm2_g=m_norm2_g, mlp_w1=m_mlp_w1, mlp_w2=m_mlp_w2, final_g=m_final_g)
    mom2 = dict(c_ctx=v_c_ctx, ada_w=v_ada_w, ada_b=v_ada_b, norm1_g=v_norm1_g, w_in=v_w_in, s5_lam_re=v_s5_lam_re, s5_lam_im=v_s5_lam_im, s5_log_dt=v_s5_log_dt, s5_b_re=v_s5_b_re, s5_b_im=v_s5_b_im, s5_c_re=v_s5_c_re, s5_c_im=v_s5_c_im, s5_d=v_s5_d, s5_w_glu=v_s5_w_glu, conv_w=v_conv_w, conv_b=v_conv_b, conv_ln_g=v_conv_ln_g, conv_ln_b=v_conv_ln_b, w_out=v_w_out, norm2_g=v_norm2_g, mlp_w1=v_mlp_w1, mlp_w2=v_mlp_w2, final_g=v_final_g)
    order = list(weights)

    me = 4 * lax.axis_index("x") + 2 * lax.axis_index("y") + lax.axis_index("c")
    xs, cs, tgt = x[0], ctx[0], loss_target[0]
    n_lat_rows, n_ctx_rows = xs.shape[0], cs.shape[0]
    n_rows = n_lat_rows + n_ctx_rows
    n_lat = n_lat_rows // ROW_BLOCK
    ada_cols = ada_w.shape[2]

    (c_all,), _ = _exchange("gather_c", [c], [True])
    c_all = c_all.reshape(NDEV, D_MODEL)

    cond_fwd = jnp.concatenate([c_all, c_ctx[None], jnp.zeros((7, D_MODEL), F32)])
    ada_b_loc = lax.dynamic_slice(ada_b, (0, me * ada_cols), (1, ada_cols))
    (mod_g,), mod_token = _exchange("gather_mod", [_ada_fwd(cond_fwd, ada_w[0], ada_b_loc)], [True])
    weight_groups, weights_token = _exchange_start_groups("gather_weights_start", [
        ([w_in[0].astype(BF16)], [True]),
        ([s5_w_glu[0].astype(BF16), conv_w[0] + mod_token[0:1, 0:1], w_out[0].astype(BF16)], [True] * 3),
        ([mlp_w1[0].astype(BF16), mlp_w2[0].astype(BF16)], [True] * 2)])
    (wi_send, wi_recv, wi_src, wi_land), (mixer_send, mixer_recv, mixer_src, mixer_land), \
        (mlpw_send, mlpw_recv, mlpw_src, mlpw_land) = weight_groups
    mod_rows = jnp.transpose(mod_g, (1, 0, 2)).reshape(16, 6 * D_MODEL) + weights_token[0:1, 0:1]
    mod = lax.dynamic_slice(mod_rows, (me, 0), (1, 6 * D_MODEL)).reshape(6, D_MODEL)
    modc = mod_rows[8, :2 * D_MODEL].reshape(2, D_MODEL)
    sh1, sc1, g1, sh2, sc2, g2 = [mod[i:i + 1] for i in range(6)]

    lam_re, lam_im = s5_lam_re[0].reshape(2, 1, NSTATE), s5_lam_im[0].reshape(2, 1, NSTATE)
    ldt = jnp.repeat(s5_log_dt[0], S5_STATE, axis=-1).reshape(2, 1, NSTATE)
    bt_re = jnp.transpose(s5_b_re[0], (0, 3, 1, 2)).reshape(2, S5_GROUP, NSTATE)
    bt_im = jnp.transpose(s5_b_im[0], (0, 3, 1, 2)).reshape(2, S5_GROUP, NSTATE)
    groups_per_block = S5_GROUPS // S5_BLOCKS
    ct_re = jnp.tile(s5_c_re[0].reshape(2, S5_WIDTH, S5_STATE), (1, 1, groups_per_block))
    ct_im = jnp.tile(s5_c_im[0].reshape(2, S5_WIDTH, S5_STATE), (1, 1, groups_per_block))
    d_skip = s5_d[0].reshape(1, S5_WIDTH)
    perms = [_segment_permutation(reverse_time=(d == 0)) for d in range(2)]
    perms_t = [p.T for p in perms]
    disc = [_s5_discretise(f"s5_disc{d}", False, lam_re[d], lam_im[d], ldt[d], bt_re[d], bt_im[d], ct_re[d], ct_im[d])
            for d in range(2)]

    a_all, a_all_t = _prenorm("prenorm1", xs, cs, norm1_g, jnp.stack([mod[0:2], modc]))
    before_w_in = a_all[0:SUBLANES, 0:LANES].astype(F32) + disc[0][0][0:SUBLANES, 0:LANES] + disc[1][0][0:SUBLANES, 0:LANES]
    wi_own, wi_landed = _exchange_wait("gather_w_in_wait", wi_send, wi_recv, wi_src, wi_land, [True], before_w_in)
    w_in_full = jnp.transpose(wi_landed[0], (1, 0, 2)).reshape(D_MODEL, IN_COLS)
    tm_all = 1088 if n_rows % 1088 == 0 else ROW_BLOCK
    (z_all,) = _matmul("in_proj", a_all, w_in_full, "nn", (n_rows, IN_COLS, D_MODEL), (tm_all, IN_COLS, D_MODEL),
                       [((n_rows, IN_COLS), F32)])

    _, tab, _, bmat, cmat = disc[0]
    s0, y0 = _s5_scan_fwd("s5_scan_fwd0", True, z_all, bmat, cmat, tab, perms[0], perms_t[0])
    mixer_own, mixer_landed = _exchange_wait("gather_mixer_wait", mixer_send, mixer_recv, mixer_src, mixer_land,
                                             [True] * 3, y0)
    glu_g, conv_w_g, w_out_g = mixer_landed
    glu_full = glu_g.reshape(S5_WIDTH, S5_WIDTH)
    conv_w_full = jnp.transpose(conv_w_g, (1, 0, 2)).reshape(CONV_K, CONV_WIDTH)
    w_out_full = w_out_g.reshape(D_MODEL, D_MODEL)
    _, tab, _, bmat, cmat = disc[1]
    s1, y1, ycat = _s5_scan_fwd("s5_scan_fwd1", False, z_all, bmat, cmat, tab, perms[1], perms_t[1],
                                y_other=y0, d_skip=d_skip, w_glu=glu_full)
    states, y_dir = [s0, s1], [y0, y1]

    hh_pad = _conv_gate(z_all, n_lat_rows)
    hc, ycat = _conv_fwd(hh_pad, conv_w_full, conv_b, conv_ln_g, conv_ln_b, ycat, n_lat_rows)

    tm = min(1024, n_lat_rows)
    tm_e = min(512, n_lat_rows)
    w1_cols = D_FF // NDEV
    row_vec = lambda tn: pl.BlockSpec((1, tn), lambda i, j, k: (0, j))
    out_tile = lambda t_m, t_n: pl.BlockSpec((t_m, t_n), lambda i, j, k: (i, j))
    full_rows = ((n_lat_rows, D_MODEL), F32)
    sums = ((n_lat_rows // tm_e, SUBLANES, D_MODEL), F32)
    sums_spec = pl.BlockSpec((None, SUBLANES, D_MODEL), lambda i, j, k: (i, 0, 0))
    vec = lambda v: (v, row_vec(D_MODEL))
    transposed_tile = lambda t_m, t_n: pl.BlockSpec((t_n, t_m), lambda i, j, k: (j, i))
    mix, h1, a2, a2_t = _matmul(
        "out_proj", ycat, w_out_full, "nn", (n_lat_rows, D_MODEL, D_MODEL), (tm_e, D_MODEL, D_MODEL),
        [full_rows, full_rows, ((n_lat_rows, D_MODEL), BF16), ((D_MODEL, n_lat_rows), BF16)],
        epi=_epi_residual_prenorm,
        epi_extra=[(xs, out_tile(tm_e, D_MODEL)), vec(g1), vec(norm2_g), vec(sc2), vec(sh2)],
        out_specs=[out_tile(tm_e, D_MODEL)] * 3 + [transposed_tile(tm_e, D_MODEL)])
    mlpw_own, mlpw_landed = _exchange_wait("gather_mlp_wait", mlpw_send, mlpw_recv, mlpw_src, mlpw_land, [True] * 2, a2)
    w1_g, w2_g = mlpw_landed
    w2_full = w2_g.reshape(D_FF, D_MODEL)
    tm_up = min(2048, n_lat_rows)
    f, f_t = _matmul("mlp_up", a2, w1_g, "nn", (n_lat_rows, D_FF, D_MODEL), (tm_up, w1_cols, D_MODEL),
                     [((n_lat_rows, D_FF), BF16), ((D_FF, n_lat_rows), BF16)], epi=lambda acc: (acc, acc.T),
                     b_spec=pl.BlockSpec((None, D_MODEL, w1_cols), lambda i, j, k: (j, 0, 0)),
                     out_specs=[out_tile(tm_up, w1_cols), transposed_tile(tm_up, w1_cols)])
    sq_relu = lambda t: jnp.square(jnp.maximum(t, 0.0))
    mlp_out, d_h2, dm2, err_sums, d_final_g8 = _matmul(
        "mlp_down", f, w2_full, "nn", (n_lat_rows, D_MODEL, D_FF), (tm_e, D_MODEL, 2048),
        [full_rows, full_rows, ((n_lat_rows, D_MODEL), BF16), sums, sums], a_fn=sq_relu, epi=_epi_residual_loss,
        epi_extra=[(h1, out_tile(tm_e, D_MODEL)), vec(g2), (tgt, out_tile(tm_e, D_MODEL)), vec(final_g[None])],
        out_specs=[out_tile(tm_e, D_MODEL)] * 3 + [sums_spec] * 2)

    (d_f,) = _matmul("mlp_down_dx", dm2, w2_full, "nt", (n_lat_rows, D_FF, D_MODEL), (tm_up, 1024, D_MODEL),
                     [((n_lat_rows, D_FF), BF16)],
                     epi=lambda acc, ft: (acc * 2.0 * jnp.maximum(ft.astype(F32), 0.0),),
                     epi_extra=[(f, out_tile(tm_up, 1024))])
    tk_dw = min(2048, n_lat_rows)
    (g_w2,) = _matmul("mlp_down_dw", f_t, dm2, "nn", (D_FF, D_MODEL, n_lat_rows), (1024, D_MODEL, tk_dw),
                      [((D_FF, D_MODEL), F32)], a_fn=sq_relu)
    (g_w1,) = _matmul("mlp_up_dw", a2_t, d_f, "nn", (D_MODEL, D_FF, n_lat_rows), (D_MODEL, w1_cols, n_lat_rows),
                      [((NDEV, D_MODEL, w1_cols), F32)],
                      out_specs=[pl.BlockSpec((None, D_MODEL, w1_cols), lambda i, j, k: (j, 0, 0))])
    mlp_send, mlp_recv, mlp_src, mlp_land, mlp_token = _exchange_start(
        "scatter_mlp_start", [g_w1, g_w2.reshape(NDEV, D_FF // NDEV, D_MODEL)], [False] * 2)
    d_h1, dm1, *sums2 = _matmul(
        "mlp_up_dx", d_f, w1_g, "nt", (n_lat_rows, D_MODEL, D_FF), (tm_e, D_MODEL, 4 * w1_cols),
        [full_rows, ((n_lat_rows, D_MODEL), BF16)] + [sums] * 4, epi=_epi_norm_bwd,
        epi_extra=[(h1, out_tile(tm_e, D_MODEL)), (d_h2, out_tile(tm_e, D_MODEL)), (mlp_out, out_tile(tm_e, D_MODEL)),
                   vec(norm2_g), vec(sc2 + mlp_token[0:1, 0:1]), vec(g1)],
        b_spec=pl.BlockSpec((4, D_MODEL, w1_cols), lambda i, j, k: (k, 0, 0)), b_slabs=4,
        out_specs=[out_tile(tm_e, D_MODEL)] * 2 + [sums_spec] * 4)

    (d_ycat,) = _matmul("out_proj_dx", dm1, w_out_full, "nt", (n_lat_rows, D_MODEL, D_MODEL), (tm, D_MODEL, D_MODEL),
                        [((n_lat_rows, D_MODEL), F32)])
    (g_w_out,) = _matmul("out_proj_dw", ycat, dm1, "tn", (D_MODEL, D_MODEL, n_lat_rows), (D_MODEL, D_MODEL, 512),
                         [((D_MODEL, D_MODEL), F32)])

    dy, g_glu, dd8 = _glu_bwd(d_ycat, z_all, y_dir[0], y_dir[1], d_skip, glu_full, n_lat_rows)
    proj_send, proj_recv, proj_src, proj_land, proj_token = _exchange_start(
        "scatter_proj_start",
        [g_w_out.reshape(NDEV, D_MODEL // NDEV, D_MODEL), g_glu.reshape(NDEV, S5_WIDTH // NDEV, S5_WIDTH)], [False] * 2)
    perms = [p + proj_token[0:1, 0:1].astype(BF16) for p in perms]
    du, g_lam_re, g_lam_im, g_ldt, g_bt, g_cdiag = None, [], [], [], [], []
    for d in range(2):
        _, _, adj, bmat, cmat = disc[d]
        du, d_bdiag, d_cdiag, d_abar8 = _s5_scan_bwd(f"s5_scan_bwd{d}", d == 0, dy, z_all, states[d], bmat, cmat, adj,
                                                     perms[d], perms_t[d], du_other=du, d_skip=d_skip if d else None)
        d_bbar = jnp.transpose(d_bdiag.reshape(S5_BLOCKS, S5_GROUP, 2, NSTATE // S5_BLOCKS), (2, 1, 0, 3)).reshape(
            2 * S5_GROUP, NSTATE)
        d_lam8, d_bt = _s5_discretise_bwd(f"s5_disc_bwd{d}", lam_re[d], lam_im[d], ldt[d], bt_re[d], bt_im[d], d_abar8, d_bbar)
        g_lam_re.append(d_lam8[0].reshape(S5_GROUPS, S5_STATE))
        g_lam_im.append(d_lam8[1].reshape(S5_GROUPS, S5_STATE))
        g_ldt.append(d_lam8[2].reshape(S5_GROUPS, S5_STATE).sum(axis=-1))
        g_bt.append(d_bt)
        g_cdiag.append(d_cdiag)

    dhc_pad, conv_sums = _conv_bwd_norm(d_ycat, hc, conv_ln_g, conv_ln_b, n_lat_rows)
    d_v, d_gate, g_conv_w8 = _conv_bwd_taps(dhc_pad, hh_pad, z_all, conv_w_full, n_lat_rows)

    no_ctx = jnp.zeros((n_ctx_rows, CONV_WIDTH), BF16)
    dz_all = jnp.concatenate([du, jnp.concatenate([d_v, no_ctx]), jnp.concatenate([d_gate, no_ctx])], axis=1)
    (g_w_in_full,) = _matmul("in_proj_dw", a_all_t, dz_all, "nn", (D_MODEL, IN_COLS, n_rows),
                             (D_MODEL, IN_COLS, n_rows // 2), [((D_MODEL, IN_COLS), BF16)])
    g_w_in_parts = jnp.transpose(g_w_in_full.reshape(D_MODEL, NDEV, IN_COLS // NDEV), (1, 0, 2))
    win_send, win_recv, win_src, win_land, win_token = _exchange_start("scatter_w_in_start", [g_w_in_parts], [False])
    w_in_late = w_in_full + win_token[0:1, 0:1].astype(BF16)
    grad_x, *sums1 = _matmul(
        "in_proj_dx", dz_all, w_in_late, "nt", (n_lat_rows, D_MODEL, IN_COLS), (tm_e, D_MODEL, IN_COLS),
        [full_rows] + [sums] * 4, epi=_epi_norm_bwd,
        epi_extra=[(xs, out_tile(tm_e, D_MODEL)), (d_h1, out_tile(tm_e, D_MODEL)), (mix, out_tile(tm_e, D_MODEL)),
                   vec(norm1_g), vec(sc1)],
        out_specs=[out_tile(tm_e, D_MODEL)] + [sums_spec] * 4)
    (d_a_ctx,) = _matmul("in_proj_dx_ctx", dz_all, w_in_late, "nt", (n_ctx_rows, D_MODEL, IN_COLS),
                         (ROW_BLOCK, D_MODEL, IN_COLS), [((n_ctx_rows, D_MODEL), F32)],
                         a_spec=pl.BlockSpec((ROW_BLOCK, IN_COLS), lambda i, j, k: (i + n_lat, 0)))
    (sums1c,) = _norm_bwd("norm1_bwd_ctx", cs, d_a_ctx, 0, norm1_g, modc[1:2])

    s1, s1c, s2 = [p.sum(axis=(0, 1)) for p in sums1], sums1c.sum(axis=1), [p.sum(axis=(0, 1)) for p in sums2]
    d_mod = jnp.concatenate([s1[0], s1[1], s1[3], s2[0], s2[1], s2[3]])
    d_modc = jnp.concatenate([s1c[0], s1c[1], jnp.zeros((4 * D_MODEL,), F32)])
    (dmod_g,), _ = _exchange("gather_dmod", [jnp.stack([d_mod, d_modc])], [True])
    dmod16 = jnp.concatenate([dmod_g[:, 0], dmod_g[:, 1]])
    dmod16_loc = lax.dynamic_slice(dmod16, (0, me * ada_cols), (16, ada_cols))
    cond_bwd = jnp.concatenate([c_all, jnp.broadcast_to(c_ctx[None], (NDEV, D_MODEL))])
    g_ada_w, g_c_ctx8 = _ada_bwd(cond_bwd, dmod16_loc, ada_w[0], c_ctx[None])

    small_parts = dict(
        c_ctx=g_c_ctx8[0], ada_b=d_mod + d_modc, norm1_g=s1[2] + s1c[2],
        s5_lam_re=jnp.stack(g_lam_re), s5_lam_im=jnp.stack(g_lam_im), s5_log_dt=jnp.stack(g_ldt),
        s5_d=dd8.sum(axis=0), conv_b=conv_sums[0].sum(axis=0), conv_ln_g=conv_sums[1].sum(axis=0),
        conv_ln_b=conv_sums[2].sum(axis=0), norm2_g=s2[2], final_g=d_final_g8.sum(axis=(0, 1)))
    reduced_shapes = [(SMALL_PACKED_ROWS, D_MODEL), (2, 2 * S5_GROUP, NSTATE), (2,) + _S5_DIAG, (1,)]
    small_g = _pack_rows(
        [_pack_rows([small_parts[n] for n in SMALL], SMALL_PACKED_ROWS), jnp.stack(g_bt), jnp.stack(g_cdiag),
         (0.5 / D_MODEL * jnp.sum(err_sums)).reshape(1)], SMALL_ROWS).reshape(NDEV, SMALL_ROWS // NDEV, D_MODEL)
    g_conv_w_parts = jnp.transpose(g_conv_w8.sum(axis=1).reshape(CONV_K, NDEV, CONV_WIDTH // NDEV), (1, 0, 2))

    res = {}

    def adamw_big(name, parts):
        outs = _adamw("adamw_" + name, weights[name][0], parts, mom1[name][0], mom2[name][0])
        res[name] = [o[None] for o in outs]
        return outs[0]

    sm_send, sm_recv, sm_src, sm_land, sm_token = _exchange_start("scatter_small_start", [g_conv_w_parts, small_g],
                                                                  [False] * 2)
    _, (p_w1, p_w2) = _exchange_wait("scatter_mlp_wait", mlp_send, mlp_recv, mlp_src, mlp_land, [False] * 2, sm_token)
    adamw_big("ada_w", g_ada_w[None])
    adamw_big("mlp_w1", p_w1)
    done = adamw_big("mlp_w2", p_w2)
    _, (p_conv_w, p_small) = _exchange_wait("scatter_small_wait", sm_send, sm_recv, sm_src, sm_land, [False] * 2, done)
    ga_send, ga_recv, ga_src, ga_land, ga_token = _exchange_start("gather_small_start", [_sum_parts(p_small)], [True])
    _, (p_w_out, p_glu) = _exchange_wait("scatter_proj_wait", proj_send, proj_recv, proj_src, proj_land, [False] * 2,
                                         ga_token)
    adamw_big("w_out", p_w_out)
    done = adamw_big("s5_w_glu", p_glu)
    _, (p_w_in,) = _exchange_wait("scatter_w_in_wait", win_send, win_recv, win_src, win_land, [False], done)
    adamw_big("w_in", p_w_in)
    done = adamw_big("conv_w", p_conv_w)
    _, (small_all,) = _exchange_wait("gather_small_wait", ga_send, ga_recv, ga_src, ga_land, [True], done)
    small_all = small_all.reshape(1, SMALL_ROWS, D_MODEL)
    _, r_bt, r_cdiag, loss = _unpack_rows(small_all, reduced_shapes)
    loss = loss.reshape(())
    pack = lambda src: _pack_rows([src[n] for n in SMALL], SMALL_PACKED_ROWS)
    outs = _adamw("adamw_small", pack(weights), small_all, pack(mom1), pack(mom2))
    unpacked = [_unpack_rows(o, [weights[n].shape for n in SMALL]) for o in outs]
    for i, name in enumerate(SMALL):
        res[name] = [u[i] for u in unpacked]
    to_ghp = lambda t: jnp.transpose(t.reshape(2, S5_GROUP, S5_GROUPS, S5_STATE), (0, 2, 1, 3))[None]
    r_c = jnp.transpose(r_cdiag.reshape(2, S5_BLOCKS, S5_GROUP, 2, groups_per_block, S5_STATE), (3, 0, 1, 4, 2, 5)).reshape(
        2, 1, 2, S5_GROUPS, S5_GROUP, S5_STATE)
    swap = lambda t: jnp.swapaxes(t, -1, -2)
    for name, grad in (("s5_b_re", to_ghp(r_bt[:, :S5_GROUP])), ("s5_b_im", to_ghp(r_bt[:, S5_GROUP:]))):
        outs = _adamw_native("adamw_" + name, swap(weights[name]), grad, swap(mom1[name]), swap(mom2[name]))
        res[name] = [swap(grad), *[swap(o) for o in outs]]
    for name, grad in (("s5_c_re", r_c[0]), ("s5_c_im", -r_c[1])):
        res[name] = [grad, *_adamw_native("adamw_" + name, weights[name], grad, mom1[name], mom2[name])]

    return (loss, grad_x[None], *[res[n][0] for n in order], *[res[n][1] for n in order],
            *[res[n][2] for n in order], *[res[n][3] for n in order])
```

```python
import jax
import jax.numpy as jnp
from jax import lax
from jax.experimental import pallas as pl
from jax.experimental.pallas import tpu as pltpu

F32 = jnp.float32
BF16 = jnp.bfloat16
MESH = pl.DeviceIdType.MESH
ANY = pl.BlockSpec(memory_space=pl.ANY)

NDEV = 8
D_MODEL = 1024
GRID_W = 64
S5_WIDTH = 512
S5_GROUP = 16
S5_GROUPS = 32
S5_STATE = 64
NSTATE = S5_GROUPS * S5_STATE
CONV_WIDTH = 512
CONV_K = 31
IN_COLS = S5_WIDTH + 2 * CONV_WIDTH
D_FF = 4 * D_MODEL
EPS_RMS = 1e-6
EPS_LN = 1e-5
ADAM_LR = 0.001
ADAM_B1 = 0.9
ADAM_B2 = 0.999
ADAM_EPS = 1e-08
ADAM_WD = 0.01
ADAM_STEP = 10

SUBLANES = 8
LANES = 128
ROW_BLOCK = 256
SCAN_LANES = 512
SCAN_UNROLL = 32
SEGMENTS = SUBLANES
STEPS = ROW_BLOCK // SEGMENTS
S5_BLOCKS = 4
S5_BLOCK_WIDTH = S5_WIDTH // S5_BLOCKS
CONV_ROWS = 64
CONV_BWD_ROWS = 32
VMEM_LIMIT = 48 * 1024 * 1024
SMALL_ROWS = 320


def _params(sem=None):
    kw = dict(vmem_limit_bytes=VMEM_LIMIT)
    if sem is not None:
        kw["dimension_semantics"] = sem
    return pltpu.CompilerParams(**kw)


def _sds(shape, dtype=F32):
    return jax.ShapeDtypeStruct(tuple(shape), dtype)


def _fold8(x):
    return x.reshape(x.shape[0] // SUBLANES, SUBLANES, x.shape[1]).sum(axis=0)


def _sigmoid(x):
    return 1.0 / (1.0 + jnp.exp(-x))


def _silu(x):
    return x * _sigmoid(x)


def _dsilu(x):
    s = _sigmoid(x)
    return s * (1.0 + x * (1.0 - s))


_GELU_C = 0.7978845608028654


def _gelu(x):
    return 0.5 * x * (1.0 + jnp.tanh(_GELU_C * (x + 0.044715 * x * x * x)))


def _dgelu(x):
    t = jnp.tanh(_GELU_C * (x + 0.044715 * x * x * x))
    return 0.5 * (1.0 + t) + 0.5 * x * (1.0 - t * t) * _GELU_C * (1.0 + 3.0 * 0.044715 * x * x)


def _rms(x):
    rstd = lax.rsqrt(jnp.mean(x * x, axis=-1, keepdims=True) + EPS_RMS)
    return x * rstd, rstd


def _epi_residual_prenorm(acc, res, gate, gain, scale, shift):
    h = res + gate * acc
    xh, _ = _rms(h)
    a = (xh * gain) * (1.0 + scale) + shift
    return acc, h, a, a.T


def _epi_residual_loss(acc, res, gate, target, gain):
    h = res + gate * acc
    xh, rstd = _rms(h)
    err = xh * gain - target
    dy = err * (1.0 / h.shape[-1])
    dxh = dy * gain
    dh = rstd * (dxh - xh * jnp.mean(dxh * xh, axis=-1, keepdims=True))
    return acc, dh, dh * gate, _fold8(err * err), _fold8(dy * xh)


def _epi_norm_bwd(d_act, x, res, aux, gain, scale, gate=None):
    xh, rstd = _rms(x)
    dn = d_act * (1.0 + scale)
    dxh = dn * gain
    dx = res + rstd * (dxh - xh * jnp.mean(dxh * xh, axis=-1, keepdims=True))
    sums = (_fold8(d_act), _fold8(d_act * (xh * gain)), _fold8(dn * xh), _fold8(res * aux))
    return (dx, *sums) if gate is None else (dx, dx * gate, *sums)


def _dot(a, b, mode):
    dims = {"nn": (((1,), (0,)), ((), ())), "nt": (((1,), (1,)), ((), ())), "tn": (((0,), (0,)), ((), ()))}[mode]
    return lax.dot_general(a, b, dims, preferred_element_type=F32)


def _peers(x, y, c):
    out = []
    for k in range(1, NDEV):
        px = 1 - x if k & 4 else x
        py = 1 - y if k & 2 else y
        pc = 1 - c if k & 1 else c
        out.append(((px, py, pc), 4 * px + 2 * py + pc))
    return out


def _exchange_copies(src, land, send_sems, recv_sems, gather):
    x, y, c = lax.axis_index("x"), lax.axis_index("y"), lax.axis_index("c")
    me = 4 * x + 2 * y + c
    out = []
    for a in range(len(src)):
        for k, (peer, plin) in enumerate(_peers(x, y, c)):
            chunk = src[a] if gather[a] else src[a].at[plin]
            sems = dict(send_sem=send_sems.at[a * (NDEV - 1) + k], recv_sem=recv_sems.at[a * (NDEV - 1) + k],
                        device_id=peer, device_id_type=MESH)
            out.append((pltpu.make_async_remote_copy(src_ref=chunk, dst_ref=land[a].at[me], **sems),
                        pltpu.make_async_remote_copy(src_ref=chunk, dst_ref=land[a].at[plin], **sems)))
    return out


def _exchange(name, srcs, gather):
    n = len(srcs)
    outs = [_sds(((NDEV,) + s.shape) if g else s.shape, s.dtype) for s, g in zip(srcs, gather)]

    def body(*refs):
        src, dst, token = refs[:n], refs[n:2 * n], refs[2 * n]
        send_sems, recv_sems, local_sems = refs[2 * n + 1:]
        me = 4 * lax.axis_index("x") + 2 * lax.axis_index("y") + lax.axis_index("c")
        local = [pltpu.make_async_copy(src[a] if gather[a] else src[a].at[me], dst[a].at[me], local_sems.at[a])
                 for a in range(n)]
        for copy in local:
            copy.start()
        copies = _exchange_copies(src, dst, send_sems, recv_sems, gather)
        for copy, _ in copies:
            copy.start()
        token[...] = jnp.zeros_like(token)
        for copy, landing in copies:
            copy.wait_send()
            landing.wait_recv()
        for copy in local:
            copy.wait()

    nsem = n * (NDEV - 1)
    out = pl.pallas_call(
        body, name=name, out_shape=outs + [_sds((SUBLANES, LANES))], in_specs=[ANY] * n,
        out_specs=[ANY] * n + [pl.BlockSpec(memory_space=pltpu.VMEM)],
        scratch_shapes=[pltpu.SemaphoreType.DMA((nsem,)), pltpu.SemaphoreType.DMA((nsem,)), pltpu.SemaphoreType.DMA((n,))],
    )(*srcs)
    return out[:n], out[n]


HBM = pl.BlockSpec(memory_space=pltpu.HBM)
SEM = pl.BlockSpec(memory_space=pltpu.SEMAPHORE)
EFFECT = pltpu.SideEffectType.DATAFLOW_SIDE_EFFECTING


def _own_chunk_copies(src, land, local_sems, gather):
    me = 4 * lax.axis_index("x") + 2 * lax.axis_index("y") + lax.axis_index("c")
    return [pltpu.make_async_copy(src[a] if gather[a] else src[a].at[me], land[a].at[me], local_sems.at[a])
            for a in range(len(src))]


def _exchange_start_groups(name, groups):
    srcs = [s for g_srcs, _ in groups for s in g_srcs]
    gathers = [g for _, g_gather in groups for g in g_gather]
    lands = [lax.empty(((NDEV,) + s.shape) if g else s.shape, s.dtype) for s, g in zip(srcs, gathers)]
    n, ng = len(srcs), len(groups)

    def body(*refs):
        src, land = refs[:n], refs[n:2 * n]
        sems = refs[2 * n:2 * n + 3 * ng]
        token = refs[-1]
        first = 0
        for g, (g_srcs, g_gather) in enumerate(groups):
            last = first + len(g_srcs)
            send, recv, local = sems[3 * g:3 * g + 3]
            for copy, _ in _exchange_copies(src[first:last], land[first:last], send, recv, g_gather):
                copy.start()
            for copy in _own_chunk_copies(src[first:last], land[first:last], local, g_gather):
                copy.start()
            first = last
        token[...] = jnp.zeros_like(token)

    hbm = lambda v: pltpu.HBM(v.shape, v.dtype)
    sem_shapes = []
    for g_srcs, _ in groups:
        sem_shapes += [pltpu.SemaphoreType.DMA((len(g_srcs) * (NDEV - 1),))] * 2 + [pltpu.SemaphoreType.DMA((len(g_srcs),))]
    out = pl.pallas_call(
        body, name=name,
        out_shape=(*sem_shapes, *[hbm(v) for v in srcs], *[hbm(v) for v in lands], _sds((SUBLANES, LANES))),
        in_specs=[HBM] * (2 * n),
        out_specs=(*([SEM] * (3 * ng)), *([HBM] * (2 * n)), pl.BlockSpec(memory_space=pltpu.VMEM)),
        input_output_aliases={i: 3 * ng + i for i in range(2 * n)},
        compiler_params=pltpu.CompilerParams(has_side_effects=EFFECT),
    )(*[pltpu.with_memory_space_constraint(v, pltpu.HBM) for v in srcs + lands])
    src_out, land_out = out[3 * ng:3 * ng + n], out[3 * ng + n:3 * ng + 2 * n]
    result, first = [], 0
    for g, (g_srcs, _) in enumerate(groups):
        last = first + len(g_srcs)
        result.append(((out[3 * g], out[3 * g + 2]), out[3 * g + 1], src_out[first:last], land_out[first:last]))
        first = last
    return result, out[-1]


def _exchange_start(name, srcs, gather):
    (group,), token = _exchange_start_groups(name, [(srcs, gather)])
    return (*group, token)


def _exchange_wait(name, send_sems, recv_sems, srcs, lands, gather, after):
    n = len(srcs)

    def body(*refs):
        src, land = refs[:n], refs[n:2 * n]
        send_ref, local_ref, recv_ref = refs[2 * n:2 * n + 3]
        for copy, landing in _exchange_copies(src, land, send_ref, recv_ref, gather):
            copy.wait_send()
            landing.wait_recv()
        for copy in _own_chunk_copies(src, land, local_ref, gather):
            copy.wait()

    hbm = lambda v: pltpu.HBM(v.shape, v.dtype)
    out = pl.pallas_call(
        body, name=name, out_shape=[hbm(v) for v in list(srcs) + list(lands)],
        in_specs=[HBM] * (2 * n) + [SEM, SEM, SEM, ANY], out_specs=[HBM] * (2 * n),
        input_output_aliases={i: i for i in range(2 * n)},
        compiler_params=pltpu.CompilerParams(has_side_effects=EFFECT),
    )(*srcs, *lands, send_sems[0], send_sems[1], recv_sems, after)
    return out[:n], out[n:]


def _matmul(name, a, b, mode, mnk, tiles, outs, a_spec=None, b_spec=None, a_fn=None, a_extra=(),
            epi=None, epi_extra=(), out_specs=None, b_slabs=1):
    m_, n_, k_ = mnk
    tm, tn, tk = tiles
    nk = k_ // tk
    if a_spec is None:
        a_spec = (pl.BlockSpec((tk, tm), lambda i, j, k: (k, i)) if mode == "tn"
                  else pl.BlockSpec((tm, tk), lambda i, j, k: (i, k)))
    if b_spec is None:
        b_spec = (pl.BlockSpec((tn, tk), lambda i, j, k: (j, k)) if mode == "nt"
                  else pl.BlockSpec((tk, tn), lambda i, j, k: (k, j)))
    if out_specs is None:
        out_specs = [pl.BlockSpec((tm, tn), lambda i, j, k: (i, j)) for _ in outs]
    na, ne, no = len(a_extra), len(epi_extra), len(outs)

    def body(*refs):
        a_ref, b_ref = refs[0], refs[1]
        ax = refs[2:2 + na]
        ex = refs[2 + na:2 + na + ne]
        o = refs[2 + na + ne:2 + na + ne + no]

        def finish(res):
            res = epi(res, *[r[...] for r in ex]) if epi is not None else (res,)
            for ref, val in zip(o, res):
                ref[...] = val.astype(ref.dtype)

        at = a_ref[...]
        if a_fn is not None:
            at = a_fn(at, *[r[...] for r in ax])
        at = at.astype(BF16)
        if b_slabs == 1:
            part = _dot(at, b_ref[...].astype(BF16), mode)
        else:
            ks = tk // b_slabs
            part = _dot(at[:, 0:ks], b_ref[0].astype(BF16), mode)
            for s in range(1, b_slabs):
                part = part + _dot(at[:, s * ks:(s + 1) * ks], b_ref[s].astype(BF16), mode)
        if nk == 1:
            finish(part)
            return
        acc = refs[-1]
        k = pl.program_id(2)

        @pl.when(k == 0)
        def _():
            acc[...] = part

        @pl.when(k > 0)
        def _():
            acc[...] += part

        @pl.when(k == nk - 1)
        def _():
            finish(acc[...])

    return pl.pallas_call(
        body, name=name, grid=(m_ // tm, n_ // tn, nk),
        in_specs=[a_spec, b_spec] + [s for _, s in a_extra] + [s for _, s in epi_extra],
        out_specs=out_specs, out_shape=[_sds(s, d) for s, d in outs],
        scratch_shapes=[pltpu.VMEM((tm, tn), F32)] if nk > 1 else [],
        compiler_params=_params(("parallel", "parallel", "arbitrary")),
    )(a, b, *[x for x, _ in a_extra], *[x for x, _ in epi_extra])


def _prenorm(name, x, ctx, gain, shsc):
    n_lat, n_ctx = x.shape[0] // ROW_BLOCK, ctx.shape[0] // ROW_BLOCK
    rows, d = (n_lat + n_ctx) * ROW_BLOCK, x.shape[1]

    def norm(src, g_ref, m_ref, o_ref, t_ref):
        xv = src[...]
        xh = xv * lax.rsqrt(jnp.mean(xv * xv, axis=-1, keepdims=True) + EPS_RMS)
        a = (xh * g_ref[...]) * (1.0 + m_ref[1:2, :]) + m_ref[0:1, :]
        o_ref[...] = a.astype(o_ref.dtype)
        t_ref[...] = a.T.astype(t_ref.dtype)

    def body(x_ref, c_ref, g_ref, m_ref, o_ref, t_ref):
        i = pl.program_id(0)

        @pl.when(i < n_lat)
        def _():
            norm(x_ref, g_ref, m_ref, o_ref, t_ref)

        @pl.when(i >= n_lat)
        def _():
            norm(c_ref, g_ref, m_ref, o_ref, t_ref)

    return pl.pallas_call(
        body, name=name, grid=(n_lat + n_ctx,),
        in_specs=[pl.BlockSpec((ROW_BLOCK, d), lambda i: (jnp.minimum(i, n_lat - 1), 0)),
                  pl.BlockSpec((ROW_BLOCK, d), lambda i: (jnp.maximum(i - n_lat, 0), 0)),
                  pl.BlockSpec((1, d), lambda i: (0, 0)),
                  pl.BlockSpec((None, 2, d), lambda i: (jnp.minimum(i // n_lat, 1), 0, 0))],
        out_specs=[pl.BlockSpec((ROW_BLOCK, d), lambda i: (i, 0)), pl.BlockSpec((d, ROW_BLOCK), lambda i: (0, i))],
        out_shape=[_sds((rows, d), BF16), _sds((d, rows), BF16)],
        compiler_params=_params(("parallel",)),
    )(x, ctx, gain, shsc)


def _norm_bwd(name, x, d_act, d_act_row0, gain, scale, res=None, aux=None):
    rows, d = x.shape
    nb = rows // ROW_BLOCK
    has_res = res is not None

    def body(*refs):
        if has_res:
            x_ref, da_ref, g_ref, sc_ref, r_ref, aux_ref, dx_ref, sums = refs
        else:
            x_ref, da_ref, g_ref, sc_ref, sums = refs
        i = pl.program_id(0)

        @pl.when(i == 0)
        def _():
            sums[...] = jnp.zeros_like(sums)

        xv, da = x_ref[...], da_ref[...]
        rstd = lax.rsqrt(jnp.mean(xv * xv, axis=-1, keepdims=True) + EPS_RMS)
        xh = xv * rstd
        g = g_ref[...]
        dn = da * (1.0 + sc_ref[...])
        sums[0] += _fold8(da)
        sums[1] += _fold8(da * (xh * g))
        sums[2] += _fold8(dn * xh)
        if has_res:
            dxh = dn * g
            dx = rstd * (dxh - xh * jnp.mean(dxh * xh, axis=-1, keepdims=True))
            rv = r_ref[...]
            dx_ref[...] = rv + dx
            sums[3] += _fold8(rv * aux_ref[...])

    row = lambda i: (i, 0)
    vec = pl.BlockSpec((1, d), lambda i: (0, 0))
    in_specs = [pl.BlockSpec((ROW_BLOCK, d), row), pl.BlockSpec((ROW_BLOCK, d), lambda i: (i + d_act_row0, 0)), vec, vec]
    args = [x, d_act, gain, scale]
    out_shape = [_sds((4, SUBLANES, d))]
    out_specs = [pl.BlockSpec((4, SUBLANES, d), lambda i: (0, 0, 0))]
    if has_res:
        in_specs += [pl.BlockSpec((ROW_BLOCK, d), row), pl.BlockSpec((ROW_BLOCK, d), row)]
        args += [res, aux]
        out_shape = [_sds((rows, d))] + out_shape
        out_specs = [pl.BlockSpec((ROW_BLOCK, d), row)] + out_specs
    return pl.pallas_call(
        body, name=name, grid=(nb,), in_specs=in_specs, out_specs=out_specs, out_shape=out_shape,
        compiler_params=_params(("arbitrary",)),
    )(*args)


def _ada_fwd(cond16, ada_w_loc, ada_b_loc):
    cols = ada_w_loc.shape[1]

    def body(c_ref, w_ref, b_ref, o_ref):
        s = _silu(c_ref[...]).astype(BF16)
        o_ref[...] = _dot(s, w_ref[...].astype(BF16), "nn") + b_ref[...]

    return pl.pallas_call(body, name="ada_fwd", out_shape=_sds((16, cols)), compiler_params=_params())(
        cond16, ada_w_loc, ada_b_loc)


def _ada_bwd(cond16, dmod16, ada_w_loc, c_ctx_row):
    k_, cols = ada_w_loc.shape

    def body(c_ref, dm_ref, w_ref, cc_ref, gw_ref, gc_ref):
        s = _silu(c_ref[...]).astype(BF16)
        dm = dm_ref[...]
        gw_ref[...] = _dot(s, dm.astype(BF16), "tn")
        dmc = jnp.sum(dm[8:16, :], axis=0, keepdims=True)
        dmc8 = jnp.broadcast_to(dmc, (SUBLANES, cols)).astype(BF16)
        ds = _dot(dmc8, w_ref[...].astype(BF16), "nt")
        row = lax.broadcasted_iota(jnp.int32, ds.shape, 0)
        gc_ref[...] = jnp.where(row == 0, ds * _dsilu(cc_ref[...]), 0.0)

    return pl.pallas_call(body, name="ada_bwd", out_shape=[_sds((k_, cols)), _sds((SUBLANES, k_))],
                          compiler_params=_params())(cond16, dmod16, ada_w_loc, c_ctx_row)


def _cmul(a, b):
    return a[0] * b[0] - a[1] * b[1], a[0] * b[1] + a[1] * b[0]


def _disc(lam_re, lam_im, ldt):
    dt = jnp.exp(ldt)
    mag = jnp.exp(lam_re * dt)
    th = lam_im * dt
    a_re, a_im = mag * jnp.cos(th), mag * jnp.sin(th)
    den = lam_re * lam_re + lam_im * lam_im
    n_re = a_re - 1.0
    f_re = (n_re * lam_re + a_im * lam_im) / den
    f_im = (a_im * lam_re - n_re * lam_im) / den
    return dt, mag, th, a_re, a_im, den, n_re, f_re, f_im


def _block_diag_mask(shape):
    row = lax.broadcasted_iota(jnp.int32, shape, 0)
    col = lax.broadcasted_iota(jnp.int32, shape, 1)
    return lax.shift_right_logical(row, 4) == lax.shift_right_logical(col, 6)


TAB_A = 0
TAB_BIG = 1
TAB_SEG = 4
TAB_PW = 5
TAB_ROWS = TAB_PW + STEPS


def _s5_discretise(name, ascending, lam_re, lam_im, ldt, bt_re, bt_im, ct_re, ct_im):
    def write_tables(ref, pw, big, asc, sign):
        row = lax.broadcasted_iota(jnp.int32, (SUBLANES, NSTATE), 0)
        full = lambda v: jnp.broadcast_to(v, (SUBLANES, NSTATE))

        def put(t, p):
            ref[0, t] = full(p[0])
            ref[1, t] = full(sign * p[1])

        put(TAB_A, pw[0])
        for t in range(3):
            put(TAB_BIG + t, big[t])
        seg = [big[0]]
        for _ in range(SEGMENTS - 1):
            seg.append(_cmul(seg[-1], big[0]))
        seg_re = jnp.zeros((SUBLANES, NSTATE), F32)
        seg_im = jnp.zeros((SUBLANES, NSTATE), F32)
        for r in range(SEGMENTS):
            p = seg[r] if asc else seg[SEGMENTS - 1 - r]
            seg_re = jnp.where(row == r, p[0], seg_re)
            seg_im = jnp.where(row == r, sign * p[1], seg_im)
        ref[0, TAB_SEG] = seg_re
        ref[1, TAB_SEG] = seg_im
        for k in range(STEPS):
            put(TAB_PW + k, pw[k])

    def body(lr_ref, li_ref, ldt_ref, br_ref, bi_ref, cr_ref, ci_ref, bb_ref, tab_ref, adj_ref, bm_ref, cm_ref):
        _, _, _, a_re, a_im, _, _, f_re, f_im = _disc(lr_ref[...], li_ref[...], ldt_ref[...])
        bre, bim = br_ref[...], bi_ref[...]
        bb_re = f_re * bre - f_im * bim
        bb_im = f_re * bim + f_im * bre
        bb_ref[0:S5_GROUP, :] = bb_re
        bb_ref[S5_GROUP:2 * S5_GROUP, :] = bb_im
        pw = [(a_re, a_im)]
        for _ in range(STEPS - 1):
            pw.append(_cmul(pw[-1], (a_re, a_im)))
        big = [pw[STEPS - 1]]
        for _ in range(2):
            big.append(_cmul(big[-1], big[-1]))
        write_tables(tab_ref, pw, big, ascending, 1.0)
        write_tables(adj_ref, pw, big, not ascending, -1.0)
        half = NSTATE // S5_BLOCKS
        mask = _block_diag_mask((S5_BLOCK_WIDTH, half))
        tile = lambda v: jnp.broadcast_to(v[None], (S5_BLOCK_WIDTH // S5_GROUP, S5_GROUP, half)).reshape(S5_BLOCK_WIDTH, half)
        for c in range(S5_BLOCKS):
            cols = slice(c * half, (c + 1) * half)
            rows = slice(c * S5_BLOCK_WIDTH, (c + 1) * S5_BLOCK_WIDTH)
            bm_ref[c, :, 0:half] = jnp.where(mask, tile(bb_re[:, cols]), 0.0).astype(BF16)
            bm_ref[c, :, half:2 * half] = jnp.where(mask, tile(bb_im[:, cols]), 0.0).astype(BF16)
            cm_ref[c, :, 0:half] = jnp.where(mask, cr_ref[rows, :], 0.0).astype(BF16)
            cm_ref[c, :, half:2 * half] = jnp.where(mask, -ci_ref[rows, :], 0.0).astype(BF16)

    blocked = _sds((S5_BLOCKS, S5_BLOCK_WIDTH, 2 * NSTATE // S5_BLOCKS), BF16)
    return pl.pallas_call(
        body, name=name,
        out_shape=[_sds((2 * S5_GROUP, NSTATE)), _sds((2, TAB_ROWS, SUBLANES, NSTATE)),
                   _sds((2, TAB_ROWS, SUBLANES, NSTATE)), blocked, blocked],
        compiler_params=_params(),
    )(lam_re, lam_im, ldt, bt_re, bt_im, ct_re, ct_im)


def _s5_discretise_bwd(name, lam_re, lam_im, ldt, bt_re, bt_im, d_abar8, d_bbar):
    def body(lr_ref, li_ref, ldt_ref, br_ref, bi_ref, da_ref, db_ref, dl_ref, dbt_ref):
        lam_re, lam_im = lr_ref[...], li_ref[...]
        dt, _, _, a_re, a_im, den, n_re, f_re, f_im = _disc(lam_re, lam_im, ldt_ref[...])
        bre, bim = br_ref[...], bi_ref[...]
        dbr, dbi = db_ref[0:S5_GROUP, :], db_ref[S5_GROUP:2 * S5_GROUP, :]
        dbt_ref[0:S5_GROUP, :] = f_re * dbr + f_im * dbi
        dbt_ref[S5_GROUP:2 * S5_GROUP, :] = f_re * dbi - f_im * dbr
        df_re = jnp.sum(bre * dbr + bim * dbi, axis=0, keepdims=True)
        df_im = jnp.sum(bre * dbi - bim * dbr, axis=0, keepdims=True)
        da = da_ref[...]
        da_re = jnp.sum(da[:, 0:NSTATE], axis=0, keepdims=True)
        da_im = jnp.sum(da[:, NSTATE:2 * NSTATE], axis=0, keepdims=True)
        da_re = da_re + (df_re * lam_re - df_im * lam_im) / den
        da_im = da_im + (df_re * lam_im + df_im * lam_re) / den
        ff = (f_re * df_re + f_im * df_im) * 2.0 / den
        d_lr = (df_re * n_re + df_im * a_im) / den - ff * lam_re
        d_li = (df_re * a_im - df_im * n_re) / den - ff * lam_im
        d_mag_mag = da_re * a_re + da_im * a_im
        d_th = da_im * a_re - da_re * a_im
        d_lr = d_lr + d_mag_mag * dt
        d_li = d_li + d_th * dt
        d_ldt = (d_mag_mag * lam_re + d_th * lam_im) * dt
        row = lax.broadcasted_iota(jnp.int32, (SUBLANES, NSTATE), 0)
        dl_ref[...] = jnp.where(row == 0, d_lr, jnp.where(row == 1, d_li, jnp.where(row == 2, d_ldt, 0.0)))

    return pl.pallas_call(
        body, name=name, out_shape=[_sds((SUBLANES, NSTATE)), _sds((2 * S5_GROUP, NSTATE))],
        compiler_params=_params(),
    )(lam_re, lam_im, ldt, bt_re, bt_im, d_abar8, d_bbar)


def _segment_permutation(reverse_time):
    rho = jnp.arange(ROW_BLOCK)
    src = STEPS * (rho % SEGMENTS) + rho // SEGMENTS
    if reverse_time:
        src = ROW_BLOCK - 1 - src
    return (src[:, None] == jnp.arange(ROW_BLOCK)[None, :]).astype(BF16)


def _permute_rows(perm_ref, v):
    return _dot(perm_ref[...], v, "nn").astype(BF16)


def _unpermute_rows(perm_t_ref, v):
    hi = v.astype(BF16)
    lo = (v - hi.astype(F32)).astype(BF16)
    return _dot(perm_t_ref[...], hi, "nn") + _dot(perm_t_ref[...], lo, "nn")


def _unrolled_loop(step, init):
    def trip(o, state):
        for u in range(SCAN_UNROLL):
            state = step(o * SCAN_UNROLL + u, state)
        return state

    if SCAN_UNROLL == STEPS:
        return trip(0, init)
    return lax.fori_loop(0, STEPS // SCAN_UNROLL, trip, init)


def _scan_chunk(x_ref, out_ref, tab_ref, carry_re, carry_im, ascending, pair_ref=None, acc_ref=None, lane_chunks=None):
    w = SCAN_LANES
    half = NSTATE // S5_BLOCKS
    row = lax.broadcasted_iota(jnp.int32, (SUBLANES, w), 0)
    last = (SEGMENTS - 1) if ascending else 0

    def from_previous_segment(v, k, fill):
        if ascending:
            return jnp.where(row >= k, pltpu.roll(v, k, 0), fill)
        return jnp.where(row < SEGMENTS - k, pltpu.roll(v, SEGMENTS - k, 0), fill)

    def tile_rows(k):
        return pl.ds(pl.multiple_of((k if ascending else STEPS - 1 - k) * SUBLANES, SUBLANES), SUBLANES)

    for j in (range(NSTATE // w) if lane_chunks is None else lane_chunks):
        n_l = pl.ds(j * w, w)
        lane0 = (j * w // half) * 2 * half + (j * w) % half
        re_l, im_l = pl.ds(lane0, w), pl.ds(lane0 + half, w)
        tab = lambda t, n_l=n_l: (tab_ref[0, t, :, n_l], tab_ref[1, t, :, n_l])
        a_re, a_im = tab(TAB_A)

        def local_step(k, h):
            rs = tile_rows(k)
            h_re = a_re * h[0] - a_im * h[1] + x_ref[rs, re_l]
            h_im = a_re * h[1] + a_im * h[0] + x_ref[rs, im_l]
            out_ref[rs, re_l] = h_re
            out_ref[rs, im_l] = h_im
            return h_re, h_im

        zero = jnp.zeros((SUBLANES, w), F32)
        end_re, end_im = _unrolled_loop(local_step, (zero, zero))
        for t, k in ((TAB_BIG, 1), (TAB_BIG + 1, 2), (TAB_BIG + 2, 4)):
            p_re, p_im = tab(t)
            s_re, s_im = from_previous_segment(end_re, k, 0.0), from_previous_segment(end_im, k, 0.0)
            end_re, end_im = end_re + (p_re * s_re - p_im * s_im), end_im + (p_re * s_im + p_im * s_re)
        c0_re, c0_im = carry_re[:, n_l], carry_im[:, n_l]
        p_re, p_im = tab(TAB_SEG)
        end_re = end_re + (p_re * c0_re - p_im * c0_im)
        end_im = end_im + (p_re * c0_im + p_im * c0_re)
        carry_re[:, n_l] = jnp.broadcast_to(end_re[last:last + 1, :], end_re.shape)
        carry_im[:, n_l] = jnp.broadcast_to(end_im[last:last + 1, :], end_im.shape)
        in_re = from_previous_segment(end_re, 1, c0_re)
        in_im = from_previous_segment(end_im, 1, c0_im)

        def carry_step(k, st):
            rs = tile_rows(k)
            p_re, p_im = tab_ref[0, TAB_PW + k, :, n_l], tab_ref[1, TAB_PW + k, :, n_l]
            o_re = out_ref[rs, re_l] + (p_re * in_re - p_im * in_im)
            o_im = out_ref[rs, im_l] + (p_re * in_im + p_im * in_re)
            out_ref[rs, re_l] = o_re
            out_ref[rs, im_l] = o_im
            if pair_ref is None:
                return st
            s_re, s_im = pair_ref[rs, re_l], pair_ref[rs, im_l]
            return (o_re, o_im, st[2] + (st[0] * s_re + st[1] * s_im), st[3] + (st[1] * s_re - st[0] * s_im))

        if pair_ref is None:
            _unrolled_loop(carry_step, 0)
        else:
            fin = _unrolled_loop(carry_step, (in_re, in_im, zero, zero))
            acc_ref[:, n_l] += fin[2]
            acc_ref[:, pl.ds(NSTATE + j * w, w)] += fin[3]


def _scan_block_index(i, n_lat, ctx_first_then_ascending):
    if ctx_first_then_ascending:
        return jnp.where(i == 0, n_lat, i - 1)
    return jnp.where(i == 0, n_lat, n_lat - i)


def _full_spec(shape):
    return pl.BlockSpec(shape, lambda i: (0,) * len(shape))


_S5_BLOCKED = (S5_BLOCKS, S5_BLOCK_WIDTH, 2 * NSTATE // S5_BLOCKS)
_S5_TABLES = (2, TAB_ROWS, SUBLANES, NSTATE)
_S5_DIAG = (S5_BLOCKS, S5_GROUP, 2 * NSTATE // S5_BLOCKS)


def _s5_scan_fwd(name, ascending, z_all, bmat, cmat, tab, perm, perm_t, y_other=None, d_skip=None, w_glu=None):
    rows = z_all.shape[0]
    nb = rows // ROW_BLOCK
    n_lat = nb - 1
    bw, sw = S5_BLOCK_WIDTH, 2 * NSTATE // S5_BLOCKS
    gated = y_other is not None

    def body(*refs):
        u_ref, bm_ref, cm_ref, tab_ref, p_ref, pt_ref = refs[:6]
        extra = refs[6:9] if gated else ()
        s_ref, y_ref = refs[6 + len(extra):8 + len(extra)]
        bu, yp, carry_re, carry_im = refs[-4:]

        @pl.when(pl.program_id(0) == 0)
        def _():
            carry_re[...] = jnp.zeros_like(carry_re)
            carry_im[...] = jnp.zeros_like(carry_im)

        up = _permute_rows(p_ref, u_ref[...].astype(BF16))
        for c in range(S5_BLOCKS):
            bu[:, c * sw:(c + 1) * sw] = _dot(up[:, c * bw:(c + 1) * bw], bm_ref[c], "nn")
        _scan_chunk(bu, s_ref, tab_ref, carry_re, carry_im, False)
        for c in range(S5_BLOCKS):
            yp[:, c * bw:(c + 1) * bw] = _dot(s_ref[:, c * sw:(c + 1) * sw].astype(BF16), cm_ref[c], "nt")
        y = _unpermute_rows(pt_ref, yp[...])
        y_ref[...] = y
        if gated:
            y_other_ref, d_ref, w_ref = extra
            gel = _gelu(d_ref[...] * u_ref[...] + y_other_ref[...] + y)
            refs[8 + len(extra)][...] = (gel * _sigmoid(_dot(gel.astype(BF16), w_ref[...], "nn"))).astype(BF16)

    blk = lambda i: (_scan_block_index(i, n_lat, ascending), 0)
    in_specs = [pl.BlockSpec((ROW_BLOCK, S5_WIDTH), blk), _full_spec(_S5_BLOCKED), _full_spec(_S5_BLOCKED),
                _full_spec(_S5_TABLES), _full_spec((ROW_BLOCK, ROW_BLOCK)), _full_spec((ROW_BLOCK, ROW_BLOCK))]
    args = [z_all, bmat, cmat, tab, perm, perm_t]
    out_specs = [pl.BlockSpec((ROW_BLOCK, 2 * NSTATE), blk), pl.BlockSpec((ROW_BLOCK, S5_WIDTH), blk)]
    out_shape = [_sds((rows, 2 * NSTATE)), _sds((rows, S5_WIDTH))]
    if gated:
        in_specs += [pl.BlockSpec((ROW_BLOCK, S5_WIDTH), blk), _full_spec((1, S5_WIDTH)), _full_spec((S5_WIDTH, S5_WIDTH))]
        args += [y_other, d_skip, w_glu]
        out_specs.append(pl.BlockSpec((ROW_BLOCK, S5_WIDTH),
                                      lambda i: (jnp.minimum(_scan_block_index(i, n_lat, ascending), n_lat - 1), 0)))
        out_shape.append(_sds((n_lat * ROW_BLOCK, S5_WIDTH + CONV_WIDTH), BF16))
    return pl.pallas_call(
        body, name=name, grid=(nb,), in_specs=in_specs, out_specs=out_specs, out_shape=out_shape,
        scratch_shapes=[pltpu.VMEM((ROW_BLOCK, 2 * NSTATE), F32), pltpu.VMEM((ROW_BLOCK, S5_WIDTH), F32),
                        pltpu.VMEM((SUBLANES, NSTATE), F32), pltpu.VMEM((SUBLANES, NSTATE), F32)],
        compiler_params=_params(("arbitrary",)),
    )(*args)


def _s5_scan_bwd(name, ascending, dy, z_all, states, bmat, cmat, adj, perm, perm_t, du_other=None, d_skip=None):
    rows = states.shape[0]
    nb = rows // ROW_BLOCK
    n_lat = nb - 1
    bw, sw = S5_BLOCK_WIDTH, 2 * NSTATE // S5_BLOCKS
    finish = du_other is not None

    def block_index(i):
        if ascending:
            return jnp.where(i == nb - 1, n_lat, n_lat - 1 - i)
        return jnp.where(i == nb - 1, n_lat, i)

    def body(*refs):
        dy_ref, u_ref, s_ref, bm_ref, cm_ref, adj_ref, p_ref, pt_ref = refs[:8]
        extra = refs[8:10] if finish else ()
        du_ref, db_ref, dc_ref, da_ref, g, dup, db_acc, dc_acc, carry_re, carry_im = refs[8 + len(extra):]
        i = pl.program_id(0)

        @pl.when(i == 0)
        def _():
            carry_re[...] = jnp.zeros_like(carry_re)
            carry_im[...] = jnp.zeros_like(carry_im)
            da_ref[...] = jnp.zeros_like(da_ref)
            db_acc[...] = jnp.zeros_like(db_acc)
            dc_acc[...] = jnp.zeros_like(dc_acc)

        has_dy = (i < nb - 1).astype(F32)
        dyp = _permute_rows(p_ref, (dy_ref[...] * has_dy).astype(BF16))
        up = _permute_rows(p_ref, u_ref[...].astype(BF16))
        for c in range(S5_BLOCKS):
            g[:, c * sw:(c + 1) * sw] = _dot(dyp[:, c * bw:(c + 1) * bw], cm_ref[c], "nn")
            dc_acc[c] += _dot(dyp[:, c * bw:(c + 1) * bw], s_ref[:, c * sw:(c + 1) * sw].astype(BF16), "tn")
            _scan_chunk(g, g, adj_ref, carry_re, carry_im, True, pair_ref=s_ref, acc_ref=da_ref, lane_chunks=[c])
            gc = g[:, c * sw:(c + 1) * sw].astype(BF16)
            dup[:, c * bw:(c + 1) * bw] = _dot(gc, bm_ref[c], "nt")
            db_acc[c] += _dot(up[:, c * bw:(c + 1) * bw], gc, "tn")
        du = _unpermute_rows(pt_ref, dup[...])
        if finish:
            du = du + extra[0][...] + (dy_ref[...] * has_dy) * extra[1][...]
        du_ref[...] = du.astype(du_ref.dtype)

        @pl.when(i == nb - 1)
        def _():
            mask = _block_diag_mask((bw, sw // 2))
            for acc, out in ((db_acc, db_ref), (dc_acc, dc_ref)):
                for c in range(S5_BLOCKS):
                    for part in range(2):
                        cols = slice(part * (sw // 2), (part + 1) * (sw // 2))
                        kept = jnp.where(mask, acc[c, :, cols], 0.0)
                        out[c, :, cols] = kept.reshape(bw // S5_GROUP, S5_GROUP, sw // 2).sum(axis=0)

    blk = lambda i: (block_index(i), 0)
    in_specs = [pl.BlockSpec((ROW_BLOCK, S5_WIDTH), lambda i: (jnp.minimum(block_index(i), n_lat - 1), 0)),
                pl.BlockSpec((ROW_BLOCK, S5_WIDTH), blk), pl.BlockSpec((ROW_BLOCK, 2 * NSTATE), blk),
                _full_spec(_S5_BLOCKED), _full_spec(_S5_BLOCKED), _full_spec(_S5_TABLES),
                _full_spec((ROW_BLOCK, ROW_BLOCK)), _full_spec((ROW_BLOCK, ROW_BLOCK))]
    args = [dy, z_all, states, bmat, cmat, adj, perm, perm_t]
    if finish:
        in_specs += [pl.BlockSpec((ROW_BLOCK, S5_WIDTH), blk), _full_spec((1, S5_WIDTH))]
        args += [du_other, d_skip]
    return pl.pallas_call(
        body, name=name, grid=(nb,), in_specs=in_specs,
        out_specs=[pl.BlockSpec((ROW_BLOCK, S5_WIDTH), blk), _full_spec(_S5_DIAG), _full_spec(_S5_DIAG),
                   _full_spec((SUBLANES, 2 * NSTATE))],
        out_shape=[_sds((rows, S5_WIDTH), BF16 if finish else F32), _sds(_S5_DIAG), _sds(_S5_DIAG),
                   _sds((SUBLANES, 2 * NSTATE))],
        scratch_shapes=[pltpu.VMEM((ROW_BLOCK, 2 * NSTATE), F32), pltpu.VMEM((ROW_BLOCK, S5_WIDTH), F32),
                        pltpu.VMEM(_S5_BLOCKED, F32), pltpu.VMEM(_S5_BLOCKED, F32),
                        pltpu.VMEM((SUBLANES, NSTATE), F32), pltpu.VMEM((SUBLANES, NSTATE), F32)],
        compiler_params=_params(("arbitrary",)),
    )(*args)


def _latent_row_tile(n_rows):
    return 512 if n_rows % 512 == 0 else ROW_BLOCK


def _glu_bwd(d_ycat, z_all, y0, y1, d_skip, w_glu, n_rows):
    def body(do_ref, u_ref, y0_ref, y1_ref, d_ref, w_ref, dy_ref, dw_ref, dd_ref):
        @pl.when(pl.program_id(0) == 0)
        def _():
            dw_ref[...] = jnp.zeros_like(dw_ref)
            dd_ref[...] = jnp.zeros_like(dd_ref)

        u = u_ref[...]
        y = d_ref[...] * u + y0_ref[...] + y1_ref[...]
        g = _gelu(y)
        gb = g.astype(BF16)
        w = w_ref[...]
        sg = _sigmoid(_dot(gb, w, "nn"))
        do = do_ref[...]
        dt = do * g * sg * (1.0 - sg)
        dtb = dt.astype(BF16)
        dg = do * sg + _dot(dtb, w, "nt")
        dy = dg * _dgelu(y)
        dy_ref[...] = dy
        dw_ref[...] += _dot(gb, dtb, "tn")
        dd_ref[...] += _fold8(dy * u)

    rows = _latent_row_tile(n_rows)
    row = pl.BlockSpec((rows, S5_WIDTH), lambda i: (i, 0))
    sq = pl.BlockSpec((S5_WIDTH, S5_WIDTH), lambda i: (0, 0))
    return pl.pallas_call(
        body, name="glu_bwd", grid=(n_rows // rows,),
        in_specs=[row, row, row, row, pl.BlockSpec((1, S5_WIDTH), lambda i: (0, 0)), sq],
        out_specs=[row, sq, pl.BlockSpec((SUBLANES, S5_WIDTH), lambda i: (0, 0))],
        out_shape=[_sds((n_rows, S5_WIDTH)), _sds((S5_WIDTH, S5_WIDTH)), _sds((SUBLANES, S5_WIDTH))],
        compiler_params=_params(("arbitrary",)),
    )(d_ycat, z_all, y0, y1, d_skip, w_glu)


CONV_HALF = CONV_K // 2


def _conv_block(n_rows):
    blk = min(1024, n_rows)
    assert blk >= CONV_HALF * GRID_W and n_rows % blk == 0
    return blk


def _conv_gate(z_all, n_rows):
    blk = _conv_block(n_rows)
    nb = n_rows // blk

    def body(v_ref, g_ref, o_ref):
        i = pl.program_id(0)
        inside = jnp.logical_and(i >= 1, i <= nb)

        @pl.when(inside)
        def _():
            o_ref[...] = v_ref[...] * _sigmoid(g_ref[...])

        @pl.when(jnp.logical_not(inside))
        def _():
            o_ref[...] = jnp.zeros_like(o_ref)

    src = lambda col: pl.BlockSpec((blk, CONV_WIDTH), lambda i: (jnp.clip(i - 1, 0, nb - 1), col))
    return pl.pallas_call(
        body, name="conv_gate", grid=(nb + 2,), in_specs=[src(1), src(2)],
        out_specs=pl.BlockSpec((blk, CONV_WIDTH), lambda i: (i, 0)),
        out_shape=_sds(((nb + 2) * blk, CONV_WIDTH)), compiler_params=_params(("parallel",)),
    )(z_all, z_all)


def _stream_padded(pad_ref, buf, sems, blk, n_blocks):
    i = pl.program_id(0)

    def copy(b):
        rows = pl.ds(pl.multiple_of(b * blk, blk), blk)
        return pltpu.make_async_copy(pad_ref.at[rows, :], buf.at[rows, :], sems.at[b])

    @pl.when(i == 0)
    def _():
        for b in range(n_blocks):
            copy(b).start()
        copy(0).wait()
        copy(1).wait()

    copy(i + 2).wait()
    return pl.multiple_of(i * blk, blk)


def _conv_fwd(hh_pad, w, b, ln_g, ln_b, ycat, n_rows):
    blk = _conv_block(n_rows)
    nblk = n_rows // blk + 2

    def body(hh_ref, w_ref, b_ref, g_ref, lb_ref, ycat_ref, hc_ref, y_ref, win, sems):
        base = _stream_padded(hh_ref, win, sems, blk, nblk)

        def tile(t, _):
            r0 = pl.multiple_of(t * CONV_ROWS, CONV_ROWS)
            acc = jnp.zeros((CONV_ROWS, CONV_WIDTH), F32)
            for k in range(CONV_K):
                acc = acc + w_ref[k:k + 1, :] * win[pl.ds(base + r0 + blk + (k - CONV_HALF) * GRID_W, CONV_ROWS), :]
            hc = acc + b_ref[...]
            hc_ref[pl.ds(r0, CONV_ROWS), :] = hc
            mu = jnp.mean(hc, axis=-1, keepdims=True)
            xc = hc - mu
            ln = xc * lax.rsqrt(jnp.mean(xc * xc, axis=-1, keepdims=True) + EPS_LN) * g_ref[...] + lb_ref[...]
            y_ref[pl.ds(r0, CONV_ROWS), :] = _silu(ln).astype(y_ref.dtype)
            return 0

        lax.fori_loop(0, blk // CONV_ROWS, tile, 0)

    vec = pl.BlockSpec((1, CONV_WIDTH), lambda i: (0, 0))
    row = pl.BlockSpec((blk, CONV_WIDTH), lambda i: (i, 0))
    return pl.pallas_call(
        body, name="conv_fwd", grid=(n_rows // blk,),
        in_specs=[ANY, pl.BlockSpec((CONV_K, CONV_WIDTH), lambda i: (0, 0)), vec, vec, vec, ANY],
        out_specs=[row, pl.BlockSpec((blk, CONV_WIDTH), lambda i: (i, 1))],
        out_shape=[_sds((n_rows, CONV_WIDTH)), _sds(ycat.shape, ycat.dtype)], input_output_aliases={5: 1},
        scratch_shapes=[pltpu.VMEM((nblk * blk, CONV_WIDTH), F32), pltpu.SemaphoreType.DMA((nblk,))],
        compiler_params=_params(("arbitrary",)),
    )(hh_pad, w, b, ln_g, ln_b, ycat)


def _conv_bwd_norm(d_ycat, hc, ln_g, ln_b, n_rows):
    blk = _conv_block(n_rows)
    nb = n_rows // blk

    def body(dy_ref, hc_ref, g_ref, lb_ref, o_ref, sums):
        i = pl.program_id(0)

        @pl.when(i == 0)
        def _():
            sums[...] = jnp.zeros_like(sums)

        inside = jnp.logical_and(i >= 1, i <= nb)

        @pl.when(inside)
        def _():
            hcv = hc_ref[...]
            mu = jnp.mean(hcv, axis=-1, keepdims=True)
            xc = hcv - mu
            rstd = lax.rsqrt(jnp.mean(xc * xc, axis=-1, keepdims=True) + EPS_LN)
            xh = xc * rstd
            g = g_ref[...]
            dln = dy_ref[...] * _dsilu(xh * g + lb_ref[...])
            dxh = dln * g
            dhc = rstd * (dxh - jnp.mean(dxh, axis=-1, keepdims=True) - xh * jnp.mean(dxh * xh, axis=-1, keepdims=True))
            o_ref[...] = dhc
            sums[0] += _fold8(dhc)
            sums[1] += _fold8(dln * xh)
            sums[2] += _fold8(dln)

        @pl.when(jnp.logical_not(inside))
        def _():
            o_ref[...] = jnp.zeros_like(o_ref)

    vec = pl.BlockSpec((1, CONV_WIDTH), lambda i: (0, 0))
    return pl.pallas_call(
        body, name="conv_bwd_norm", grid=(nb + 2,),
        in_specs=[pl.BlockSpec((blk, CONV_WIDTH), lambda i: (jnp.clip(i - 1, 0, nb - 1), 1)),
                  pl.BlockSpec((blk, CONV_WIDTH), lambda i: (jnp.clip(i - 1, 0, nb - 1), 0)), vec, vec],
        out_specs=[pl.BlockSpec((blk, CONV_WIDTH), lambda i: (i, 0)),
                   pl.BlockSpec((3, SUBLANES, CONV_WIDTH), lambda i: (0, 0, 0))],
        out_shape=[_sds(((nb + 2) * blk, CONV_WIDTH)), _sds((3, SUBLANES, CONV_WIDTH))],
        compiler_params=_params(("arbitrary",)),
    )(d_ycat, hc, ln_g, ln_b)


def _conv_bwd_taps(dhc_pad, hh_pad, z_all, w, n_rows):
    blk = _conv_block(n_rows)
    nblk = n_rows // blk + 2

    def body(dhc_ref, hh_ref, v_ref, g_ref, w_ref, dv_ref, dg_ref, dw_ref, dwin, hwin, dsems, hsems):
        @pl.when(pl.program_id(0) == 0)
        def _():
            dw_ref[...] = jnp.zeros_like(dw_ref)

        base = _stream_padded(dhc_ref, dwin, dsems, blk, nblk)
        _stream_padded(hh_ref, hwin, hsems, blk, nblk)

        def tile(t, _):
            r0 = pl.multiple_of(t * CONV_BWD_ROWS, CONV_BWD_ROWS) + base
            dh = dwin[pl.ds(r0 + blk, CONV_BWD_ROWS), :]
            acc = jnp.zeros((CONV_BWD_ROWS, CONV_WIDTH), F32)
            for k in range(CONV_K):
                off = (k - CONV_HALF) * GRID_W
                acc = acc + w_ref[k:k + 1, :] * dwin[pl.ds(r0 + blk - off, CONV_BWD_ROWS), :]
                dw_ref[k] += _fold8(dh * hwin[pl.ds(r0 + blk + off, CONV_BWD_ROWS), :])
            rs = pl.ds(pl.multiple_of(t * CONV_BWD_ROWS, CONV_BWD_ROWS), CONV_BWD_ROWS)
            sg = _sigmoid(g_ref[rs, :])
            vv = v_ref[rs, :]
            dv_ref[rs, :] = (acc * sg).astype(dv_ref.dtype)
            dg_ref[rs, :] = (acc * vv * sg * (1.0 - sg)).astype(dg_ref.dtype)
            return 0

        lax.fori_loop(0, blk // CONV_BWD_ROWS, tile, 0)

    row = pl.BlockSpec((blk, CONV_WIDTH), lambda i: (i, 0))
    return pl.pallas_call(
        body, name="conv_bwd_taps", grid=(n_rows // blk,),
        in_specs=[ANY, ANY,
            pl.BlockSpec((blk, CONV_WIDTH), lambda i: (i, 1)), pl.BlockSpec((blk, CONV_WIDTH), lambda i: (i, 2)),
            pl.BlockSpec((CONV_K, CONV_WIDTH), lambda i: (0, 0))],
        out_specs=[row, row, pl.BlockSpec((CONV_K, SUBLANES, CONV_WIDTH), lambda i: (0, 0, 0))],
        out_shape=[_sds((n_rows, CONV_WIDTH), BF16), _sds((n_rows, CONV_WIDTH), BF16),
                   _sds((CONV_K, SUBLANES, CONV_WIDTH))],
        scratch_shapes=[pltpu.VMEM((nblk * blk, CONV_WIDTH), F32), pltpu.VMEM((nblk * blk, CONV_WIDTH), F32),
                        pltpu.SemaphoreType.DMA((nblk,)), pltpu.SemaphoreType.DMA((nblk,))],
        compiler_params=_params(("arbitrary",)),
    )(dhc_pad, hh_pad, z_all, z_all, w)


def _sum_parts(parts):
    _, r, c = parts.shape

    def body(p_ref, o_ref):
        acc = p_ref[0]
        for q in range(1, NDEV):
            acc = acc + p_ref[q]
        o_ref[...] = acc

    return pl.pallas_call(body, name="sum_parts", out_shape=_sds((r, c)), compiler_params=_params())(parts)


def _row_tile(r, c):
    best = r
    for t in (1024, 512, 256, 128, 64, 32, 16, 8):
        if r % t == 0 and t * c <= 128 * 1024:
            return t
    return best


def _adamw(name, w, gparts, m, v):
    r, c = w.shape
    np_ = gparts.shape[0]
    tr = _row_tile(r, c)

    def body(w_ref, g_ref, m_ref, v_ref, go_ref, d_ref, mo_ref, vo_ref):
        g = g_ref[0].astype(F32)
        for q in range(1, np_):
            g = g + g_ref[q].astype(F32)
        m2 = ADAM_B1 * m_ref[...] + (1.0 - ADAM_B1) * g
        v2 = ADAM_B2 * v_ref[...] + (1.0 - ADAM_B2) * jnp.square(g)
        m_hat = m2 / (1.0 - ADAM_B1 ** ADAM_STEP)
        v_hat = v2 / (1.0 - ADAM_B2 ** ADAM_STEP)
        go_ref[...] = g
        d_ref[...] = -ADAM_LR * (m_hat / (jnp.sqrt(v_hat) + ADAM_EPS) + ADAM_WD * w_ref[...])
        mo_ref[...] = m2
        vo_ref[...] = v2

    row = pl.BlockSpec((tr, c), lambda i: (i, 0))
    return pl.pallas_call(
        body, name=name, grid=(r // tr,),
        in_specs=[row, pl.BlockSpec((np_, tr, c), lambda i: (0, i, 0)), row, row],
        out_specs=[row] * 4, out_shape=[_sds((r, c))] * 4, compiler_params=_params(("parallel",)),
    )(w, gparts, m, v)


def _adamw_native(name, w, g, m, v):
    def body(w_ref, g_ref, m_ref, v_ref, d_ref, mo_ref, vo_ref):
        gv = g_ref[...]
        m2 = ADAM_B1 * m_ref[...] + (1.0 - ADAM_B1) * gv
        v2 = ADAM_B2 * v_ref[...] + (1.0 - ADAM_B2) * jnp.square(gv)
        m_hat = m2 / (1.0 - ADAM_B1 ** ADAM_STEP)
        v_hat = v2 / (1.0 - ADAM_B2 ** ADAM_STEP)
        d_ref[...] = -ADAM_LR * (m_hat / (jnp.sqrt(v_hat) + ADAM_EPS) + ADAM_WD * w_ref[...])
        mo_ref[...] = m2
        vo_ref[...] = v2

    return pl.pallas_call(body, name=name, out_shape=[_sds(w.shape)] * 3, compiler_params=_params())(w, g, m, v)


SMALL = ["c_ctx", "ada_b", "norm1_g", "s5_lam_re", "s5_lam_im", "s5_log_dt", "s5_d", "conv_b", "conv_ln_g", "conv_ln_b",
         "norm2_g", "final_g"]
SMALL_PACKED_ROWS = 24


def _pack_rows(parts, rows):
    flat = jnp.concatenate([p.reshape(-1).astype(F32) for p in parts])
    return jnp.pad(flat, (0, rows * D_MODEL - flat.shape[0])).reshape(rows, D_MODEL)


def _unpack_rows(packed, shapes):
    flat = packed.reshape(-1)
    out, off = [], 0
    for shape in shapes:
        size = 1
        for s in shape:
            size *= s
        out.append(flat[off:off + size].reshape(shape))
        off += size
    return out


def kernel(x, c, ctx, c_ctx, ada_w, ada_b, norm1_g, w_in, s5_lam_re, s5_lam_im, s5_log_dt, s5_b_re, s5_b_im, s5_c_re, s5_c_im, s5_d, s5_w_glu, conv_w, conv_b, conv_ln_g, conv_ln_b, w_out, norm2_g, mlp_w1, mlp_w2, final_g, loss_target, m_c_ctx, m_ada_w, m_ada_b, m_norm1_g, m_w_in, m_s5_lam_re, m_s5_lam_im, m_s5_log_dt, m_s5_b_re, m_s5_b_im, m_s5_c_re, m_s5_c_im, m_s5_d, m_s5_w_glu, m_conv_w, m_conv_b, m_conv_ln_g, m_conv_ln_b, m_w_out, m_norm2_g, m_mlp_w1, m_mlp_w2, m_final_g, v_c_ctx, v_ada_w, v_ada_b, v_norm1_g, v_w_in, v_s5_lam_re, v_s5_lam_im, v_s5_log_dt, v_s5_b_re, v_s5_b_im, v_s5_c_re, v_s5_c_im, v_s5_d, v_s5_w_glu, v_conv_w, v_conv_b, v_conv_ln_g, v_conv_ln_b, v_w_out, v_norm2_g, v_mlp_w1, v_mlp_w2, v_final_g):
    weights = dict(c_ctx=c_ctx, ada_w=ada_w, ada_b=ada_b, norm1_g=norm1_g, w_in=w_in, s5_lam_re=s5_lam_re, s5_lam_im=s5_lam_im, s5_log_dt=s5_log_dt, s5_b_re=s5_b_re, s5_b_im=s5_b_im, s5_c_re=s5_c_re, s5_c_im=s5_c_im, s5_d=s5_d, s5_w_glu=s5_w_glu, conv_w=conv_w, conv_b=conv_b, conv_ln_g=conv_ln_g, conv_ln_b=conv_ln_b, w_out=w_out, norm2_g=norm2_g, mlp_w1=mlp_w1, mlp_w2=mlp_w2, final_g=final_g)
    mom1 = dict(c_ctx=m_c_ctx, ada_w=m_ada_w, ada_b=m_ada_b, norm1_g=m_norm1_g, w_in=m_w_in, s5_lam_re=m_s5_lam_re, s5_lam_im=m_s5_lam_im, s5_log_dt=m_s5_log_dt, s5_b_re=m_s5_b_re, s5_b_im=m_s5_b_im, s5_c_re=m_s5_c_re, s5_c_im=m_s5_c_im, s5_d=m_s5_d, s5_w_glu=m_s5_w_glu, conv_w=m_conv_w, conv_b=m_conv_b, conv_ln_g=m_conv_ln_g, conv_ln_b=m_conv_ln_b, w_out=m_w_out, norm2_g=m_norm2_g, mlp_w1=m_mlp_w1, mlp_w2=m_mlp_w2, final_g=m_final_g)
    mom2 = dict(c_ctx=v_c_ctx, ada_w=v_ada_w, ada_b=v_ada_b, norm1_g=v_norm1_g, w_in=v_w_in, s5_lam_re=v_s5_lam_re, s5_lam_im=v_s5_lam_im, s5_log_dt=v_s5_log_dt, s5_b_re=v_s5_b_re, s5_b_im=v_s5_b_im, s5_c_re=v_s5_c_re, s5_c_im=v_s5_c_im, s5_d=v_s5_d, s5_w_glu=v_s5_w_glu, conv_w=v_conv_w, conv_b=v_conv_b, conv_ln_g=v_conv_ln_g, conv_ln_b=v_conv_ln_b, w_out=v_w_out, norm2_g=v_norm2_g, mlp_w1=v_mlp_w1, mlp_w2=v_mlp_w2, final_g=v_final_g)
    order = list(weights)

    me = 4 * lax.axis_index("x") + 2 * lax.axis_index("y") + lax.axis_index("c")
    xs, cs, tgt = x[0], ctx[0], loss_target[0]
    n_lat_rows, n_ctx_rows = xs.shape[0], cs.shape[0]
    n_rows = n_lat_rows + n_ctx_rows
    n_lat = n_lat_rows // ROW_BLOCK
    ada_cols = ada_w.shape[2]

    (c_all,), _ = _exchange("gather_c", [c], [True])
    c_all = c_all.reshape(NDEV, D_MODEL)

    cond_fwd = jnp.concatenate([c_all, c_ctx[None], jnp.zeros((7, D_MODEL), F32)])
    ada_b_loc = lax.dynamic_slice(ada_b, (0, me * ada_cols), (1, ada_cols))
    (mod_g,), mod_token = _exchange("gather_mod", [_ada_fwd(cond_fwd, ada_w[0], ada_b_loc)], [True])
    weight_groups, weights_token = _exchange_start_groups("gather_weights_start", [
        ([w_in[0].astype(BF16)], [True]),
        ([s5_w_glu[0].astype(BF16), conv_w[0] + mod_token[0:1, 0:1], w_out[0].astype(BF16)], [True] * 3),
        ([mlp_w1[0].astype(BF16), mlp_w2[0].astype(BF16)], [True] * 2)])
    (wi_send, wi_recv, wi_src, wi_land), (mixer_send, mixer_recv, mixer_src, mixer_land), \
        (mlpw_send, mlpw_recv, mlpw_src, mlpw_land) = weight_groups
    mod_rows = jnp.transpose(mod_g, (1, 0, 2)).reshape(16, 6 * D_MODEL) + weights_token[0:1, 0:1]
    mod = lax.dynamic_slice(mod_rows, (me, 0), (1, 6 * D_MODEL)).reshape(6, D_MODEL)
    modc = mod_rows[8, :2 * D_MODEL].reshape(2, D_MODEL)
    sh1, sc1, g1, sh2, sc2, g2 = [mod[i:i + 1] for i in range(6)]

    lam_re, lam_im = s5_lam_re[0].reshape(2, 1, NSTATE), s5_lam_im[0].reshape(2, 1, NSTATE)
    ldt = jnp.repeat(s5_log_dt[0], S5_STATE, axis=-1).reshape(2, 1, NSTATE)
    bt_re = jnp.transpose(s5_b_re[0], (0, 3, 1, 2)).reshape(2, S5_GROUP, NSTATE)
    bt_im = jnp.transpose(s5_b_im[0], (0, 3, 1, 2)).reshape(2, S5_GROUP, NSTATE)
    groups_per_block = S5_GROUPS // S5_BLOCKS
    ct_re = jnp.tile(s5_c_re[0].reshape(2, S5_WIDTH, S5_STATE), (1, 1, groups_per_block))
    ct_im = jnp.tile(s5_c_im[0].reshape(2, S5_WIDTH, S5_STATE), (1, 1, groups_per_block))
    d_skip = s5_d[0].reshape(1, S5_WIDTH)
    perms = [_segment_permutation(reverse_time=(d == 0)) for d in range(2)]
    perms_t = [p.T for p in perms]
    disc = [_s5_discretise(f"s5_disc{d}", False, lam_re[d], lam_im[d], ldt[d], bt_re[d], bt_im[d], ct_re[d], ct_im[d])
            for d in range(2)]

    a_all, a_all_t = _prenorm("prenorm1", xs, cs, norm1_g, jnp.stack([mod[0:2], modc]))
    before_w_in = a_all[0:SUBLANES, 0:LANES].astype(F32) + disc[0][0][0:SUBLANES, 0:LANES] + disc[1][0][0:SUBLANES, 0:LANES]
    wi_own, wi_landed = _exchange_wait("gather_w_in_wait", wi_send, wi_recv, wi_src, wi_land, [True], before_w_in)
    w_in_full = jnp.transpose(wi_landed[0], (1, 0, 2)).reshape(D_MODEL, IN_COLS)
    tm_all = 1088 if n_rows % 1088 == 0 else ROW_BLOCK
    (z_all,) = _matmul("in_proj", a_all, w_in_full, "nn", (n_rows, IN_COLS, D_MODEL), (tm_all, IN_COLS, D_MODEL),
                       [((n_rows, IN_COLS), F32)])

    _, tab, _, bmat, cmat = disc[0]
    s0, y0 = _s5_scan_fwd("s5_scan_fwd0", True, z_all, bmat, cmat, tab, perms[0], perms_t[0])
    mixer_own, mixer_landed = _exchange_wait("gather_mixer_wait", mixer_send, mixer_recv, mixer_src, mixer_land,
                                             [True] * 3, y0)
    glu_g, conv_w_g, w_out_g = mixer_landed
    glu_full = glu_g.reshape(S5_WIDTH, S5_WIDTH)
    conv_w_full = jnp.transpose(conv_w_g, (1, 0, 2)).reshape(CONV_K, CONV_WIDTH)
    w_out_full = w_out_g.reshape(D_MODEL, D_MODEL)
    _, tab, _, bmat, cmat = disc[1]
    s1, y1, ycat = _s5_scan_fwd("s5_scan_fwd1", False, z_all, bmat, cmat, tab, perms[1], perms_t[1],
                                y_other=y0, d_skip=d_skip, w_glu=glu_full)
    states, y_dir = [s0, s1], [y0, y1]

    hh_pad = _conv_gate(z_all, n_lat_rows)
    hc, ycat = _conv_fwd(hh_pad, conv_w_full, conv_b, conv_ln_g, conv_ln_b, ycat, n_lat_rows)

    tm = min(1024, n_lat_rows)
    tm_e = min(512, n_lat_rows)
    w1_cols = D_FF // NDEV
    row_vec = lambda tn: pl.BlockSpec((1, tn), lambda i, j, k: (0, j))
    out_tile = lambda t_m, t_n: pl.BlockSpec((t_m, t_n), lambda i, j, k: (i, j))
    full_rows = ((n_lat_rows, D_MODEL), F32)
    sums = ((n_lat_rows // tm_e, SUBLANES, D_MODEL), F32)
    sums_spec = pl.BlockSpec((None, SUBLANES, D_MODEL), lambda i, j, k: (i, 0, 0))
    vec = lambda v: (v, row_vec(D_MODEL))
    transposed_tile = lambda t_m, t_n: pl.BlockSpec((t_n, t_m), lambda i, j, k: (j, i))
    mix, h1, a2, a2_t = _matmul(
        "out_proj", ycat, w_out_full, "nn", (n_lat_rows, D_MODEL, D_MODEL), (tm_e, D_MODEL, D_MODEL),
        [full_rows, full_rows, ((n_lat_rows, D_MODEL), BF16), ((D_MODEL, n_lat_rows), BF16)],
        epi=_epi_residual_prenorm,
        epi_extra=[(xs, out_tile(tm_e, D_MODEL)), vec(g1), vec(norm2_g), vec(sc2), vec(sh2)],
        out_specs=[out_tile(tm_e, D_MODEL)] * 3 + [transposed_tile(tm_e, D_MODEL)])
    mlpw_own, mlpw_landed = _exchange_wait("gather_mlp_wait", mlpw_send, mlpw_recv, mlpw_src, mlpw_land, [True] * 2, a2)
    w1_g, w2_g = mlpw_landed
    w2_full = w2_g.reshape(D_FF, D_MODEL)
    tm_up = min(2048, n_lat_rows)
    f, f_t = _matmul("mlp_up", a2, w1_g, "nn", (n_lat_rows, D_FF, D_MODEL), (tm_up, w1_cols, D_MODEL),
                     [((n_lat_rows, D_FF), BF16), ((D_FF, n_lat_rows), BF16)], epi=lambda acc: (acc, acc.T),
                     b_spec=pl.BlockSpec((None, D_MODEL, w1_cols), lambda i, j, k: (j, 0, 0)),
                     out_specs=[out_tile(tm_up, w1_cols), transposed_tile(tm_up, w1_cols)])
    sq_relu = lambda t: jnp.square(jnp.maximum(t, 0.0))
    mlp_out, d_h2, dm2, err_sums, d_final_g8 = _matmul(
        "mlp_down", f, w2_full, "nn", (n_lat_rows, D_MODEL, D_FF), (tm_e, D_MODEL, 2048),
        [full_rows, full_rows, ((n_lat_rows, D_MODEL), BF16), sums, sums], a_fn=sq_relu, epi=_epi_residual_loss,
        epi_extra=[(h1, out_tile(tm_e, D_MODEL)), vec(g2), (tgt, out_tile(tm_e, D_MODEL)), vec(final_g[None])],
        out_specs=[out_tile(tm_e, D_MODEL)] * 3 + [sums_spec] * 2)

    (d_f,) = _matmul("mlp_down_dx", dm2, w2_full, "nt", (n_lat_rows, D_FF, D_MODEL), (tm_up, 1024, D_MODEL),
                     [((n_lat_rows, D_FF), BF16)],
                     epi=lambda acc, ft: (acc * 2.0 * jnp.maximum(ft.astype(F32), 0.0),),
                     epi_extra=[(f, out_tile(tm_up, 1024))])
    tk_dw = min(2048, n_lat_rows)
    (g_w2,) = _matmul("mlp_down_dw", f_t, dm2, "nn", (D_FF, D_MODEL, n_lat_rows), (1024, D_MODEL, tk_dw),
                      [((D_FF, D_MODEL), F32)], a_fn=sq_relu)
    (g_w1,) = _matmul("mlp_up_dw", a2_t, d_f, "nn", (D_MODEL, D_FF, n_lat_rows), (D_MODEL, w1_cols, n_lat_rows),
                      [((NDEV, D_MODEL, w1_cols), F32)],
                      out_specs=[pl.BlockSpec((None, D_MODEL, w1_cols), lambda i, j, k: (j, 0, 0))])
    mlp_send, mlp_recv, mlp_src, mlp_land, mlp_token = _exchange_start(
        "scatter_mlp_start", [g_w1, g_w2.reshape(NDEV, D_FF // NDEV, D_MODEL)], [False] * 2)
    d_h1, dm1, *sums2 = _matmul(
        "mlp_up_dx", d_f, w1_g, "nt", (n_lat_rows, D_MODEL, D_FF), (tm_e, D_MODEL, 4 * w1_cols),
        [full_rows, ((n_lat_rows, D_MODEL), BF16)] + [sums] * 4, epi=_epi_norm_bwd,
        epi_extra=[(h1, out_tile(tm_e, D_MODEL)), (d_h2, out_tile(tm_e, D_MODEL)), (mlp_out, out_tile(tm_e, D_MODEL)),
                   vec(norm2_g), vec(sc2 + mlp_token[0:1, 0:1]), vec(g1)],
        b_spec=pl.BlockSpec((4, D_MODEL, w1_cols), lambda i, j, k: (k, 0, 0)), b_slabs=4,
        out_specs=[out_tile(tm_e, D_MODEL)] * 2 + [sums_spec] * 4)

    (d_ycat,) = _matmul("out_proj_dx", dm1, w_out_full, "nt", (n_lat_rows, D_MODEL, D_MODEL), (tm, D_MODEL, D_MODEL),
                        [((n_lat_rows, D_MODEL), F32)])
    (g_w_out,) = _matmul("out_proj_dw", ycat, dm1, "tn", (D_MODEL, D_MODEL, n_lat_rows), (D_MODEL, D_MODEL, tm),
                         [((D_MODEL, D_MODEL), F32)])

    dy, g_glu, dd8 = _glu_bwd(d_ycat, z_all, y_dir[0], y_dir[1], d_skip, glu_full, n_lat_rows)
    proj_send, proj_recv, proj_src, proj_land, proj_token = _exchange_start(
        "scatter_proj_start",
        [g_w_out.reshape(NDEV, D_MODEL // NDEV, D_MODEL), g_glu.reshape(NDEV, S5_WIDTH // NDEV, S5_WIDTH)], [False] * 2)
    perms = [p + proj_token[0:1, 0:1].astype(BF16) for p in perms]
    du, g_lam_re, g_lam_im, g_ldt, g_bt, g_cdiag = None, [], [], [], [], []
    for d in range(2):
        _, _, adj, bmat, cmat = disc[d]
        du, d_bdiag, d_cdiag, d_abar8 = _s5_scan_bwd(f"s5_scan_bwd{d}", d == 0, dy, z_all, states[d], bmat, cmat, adj,
                                                     perms[d], perms_t[d], du_other=du, d_skip=d_skip if d else None)
        d_bbar = jnp.transpose(d_bdiag.reshape(S5_BLOCKS, S5_GROUP, 2, NSTATE // S5_BLOCKS), (2, 1, 0, 3)).reshape(
            2 * S5_GROUP, NSTATE)
        d_lam8, d_bt = _s5_discretise_bwd(f"s5_disc_bwd{d}", lam_re[d], lam_im[d], ldt[d], bt_re[d], bt_im[d], d_abar8, d_bbar)
        g_lam_re.append(d_lam8[0].reshape(S5_GROUPS, S5_STATE))
        g_lam_im.append(d_lam8[1].reshape(S5_GROUPS, S5_STATE))
        g_ldt.append(d_lam8[2].reshape(S5_GROUPS, S5_STATE).sum(axis=-1))
        g_bt.append(d_bt)
        g_cdiag.append(d_cdiag)

    dhc_pad, conv_sums = _conv_bwd_norm(d_ycat, hc, conv_ln_g, conv_ln_b, n_lat_rows)
    d_v, d_gate, g_conv_w8 = _conv_bwd_taps(dhc_pad, hh_pad, z_all, conv_w_full, n_lat_rows)

    no_ctx = jnp.zeros((n_ctx_rows, CONV_WIDTH), BF16)
    dz_all = jnp.concatenate([du, jnp.concatenate([d_v, no_ctx]), jnp.concatenate([d_gate, no_ctx])], axis=1)
    (g_w_in_full,) = _matmul("in_proj_dw", a_all_t, dz_all, "nn", (D_MODEL, IN_COLS, n_rows),
                             (D_MODEL, IN_COLS, n_rows // 2), [((D_MODEL, IN_COLS), BF16)])
    g_w_in_parts = jnp.transpose(g_w_in_full.reshape(D_MODEL, NDEV, IN_COLS // NDEV), (1, 0, 2))
    win_send, win_recv, win_src, win_land, win_token = _exchange_start("scatter_w_in_start", [g_w_in_parts], [False])
    w_in_late = w_in_full + win_token[0:1, 0:1].astype(BF16)
    grad_x, *sums1 = _matmul(
        "in_proj_dx", dz_all, w_in_late, "nt", (n_lat_rows, D_MODEL, IN_COLS), (tm_e, D_MODEL, IN_COLS),
        [full_rows] + [sums] * 4, epi=_epi_norm_bwd,
        epi_extra=[(xs, out_tile(tm_e, D_MODEL)), (d_h1, out_tile(tm_e, D_MODEL)), (mix, out_tile(tm_e, D_MODEL)),
                   vec(norm1_g), vec(sc1)],
        out_specs=[out_tile(tm_e, D_MODEL)] + [sums_spec] * 4)
    (d_a_ctx,) = _matmul("in_proj_dx_ctx", dz_all, w_in_late, "nt", (n_ctx_rows, D_MODEL, IN_COLS),
                         (ROW_BLOCK, D_MODEL, IN_COLS), [((n_ctx_rows, D_MODEL), F32)],
                         a_spec=pl.BlockSpec((ROW_BLOCK, IN_COLS), lambda i, j, k: (i + n_lat, 0)))
    (sums1c,) = _norm_bwd("norm1_bwd_ctx", cs, d_a_ctx, 0, norm1_g, modc[1:2])

    s1, s1c, s2 = [p.sum(axis=(0, 1)) for p in sums1], sums1c.sum(axis=1), [p.sum(axis=(0, 1)) for p in sums2]
    d_mod = jnp.concatenate([s1[0], s1[1], s1[3], s2[0], s2[1], s2[3]])
    d_modc = jnp.concatenate([s1c[0], s1c[1], jnp.zeros((4 * D_MODEL,), F32)])
    (dmod_g,), _ = _exchange("gather_dmod", [jnp.stack([d_mod, d_modc])], [True])
    dmod16 = jnp.concatenate([dmod_g[:, 0], dmod_g[:, 1]])
    dmod16_loc = lax.dynamic_slice(dmod16, (0, me * ada_cols), (16, ada_cols))
    cond_bwd = jnp.concatenate([c_all, jnp.broadcast_to(c_ctx[None], (NDEV, D_MODEL))])
    g_ada_w, g_c_ctx8 = _ada_bwd(cond_bwd, dmod16_loc, ada_w[0], c_ctx[None])

    small_parts = dict(
        c_ctx=g_c_ctx8[0], ada_b=d_mod + d_modc, norm1_g=s1[2] + s1c[2],
        s5_lam_re=jnp.stack(g_lam_re), s5_lam_im=jnp.stack(g_lam_im), s5_log_dt=jnp.stack(g_ldt),
        s5_d=dd8.sum(axis=0), conv_b=conv_sums[0].sum(axis=0), conv_ln_g=conv_sums[1].sum(axis=0),
        conv_ln_b=conv_sums[2].sum(axis=0), norm2_g=s2[2], final_g=d_final_g8.sum(axis=(0, 1)))
    reduced_shapes = [(SMALL_PACKED_ROWS, D_MODEL), (2, 2 * S5_GROUP, NSTATE), (2,) + _S5_DIAG, (1,)]
    small_g = _pack_rows(
        [_pack_rows([small_parts[n] for n in SMALL], SMALL_PACKED_ROWS), jnp.stack(g_bt), jnp.stack(g_cdiag),
         (0.5 / D_MODEL * jnp.sum(err_sums)).reshape(1)], SMALL_ROWS).reshape(NDEV, SMALL_ROWS // NDEV, D_MODEL)
    g_conv_w_parts = jnp.transpose(g_conv_w8.sum(axis=1).reshape(CONV_K, NDEV, CONV_WIDTH // NDEV), (1, 0, 2))

    res = {}

    def adamw_big(name, parts):
        outs = _adamw("adamw_" + name, weights[name][0], parts, mom1[name][0], mom2[name][0])
        res[name] = [o[None] for o in outs]
        return outs[0]

    sm_send, sm_recv, sm_src, sm_land, sm_token = _exchange_start("scatter_small_start", [g_conv_w_parts, small_g],
                                                                  [False] * 2)
    _, (p_w1, p_w2) = _exchange_wait("scatter_mlp_wait", mlp_send, mlp_recv, mlp_src, mlp_land, [False] * 2, sm_token)
    adamw_big("ada_w", g_ada_w[None])
    adamw_big("mlp_w1", p_w1)
    done = adamw_big("mlp_w2", p_w2)
    _, (p_conv_w, p_small) = _exchange_wait("scatter_small_wait", sm_send, sm_recv, sm_src, sm_land, [False] * 2, done)
    ga_send, ga_recv, ga_src, ga_land, ga_token = _exchange_start("gather_small_start", [_sum_parts(p_small)], [True])
    _, (p_w_out, p_glu) = _exchange_wait("scatter_proj_wait", proj_send, proj_recv, proj_src, proj_land, [False] * 2,
                                         ga_token)
    adamw_big("w_out", p_w_out)
    done = adamw_big("s5_w_glu", p_glu)
    _, (p_w_in,) = _exchange_wait("scatter_w_in_wait", win_send, win_recv, win_src, win_land, [False], done)
    adamw_big("w_in", p_w_in)
    done = adamw_big("conv_w", p_conv_w)
    _, (small_all,) = _exchange_wait("gather_small_wait", ga_send, ga_recv, ga_src, ga_land, [True], done)
    small_all = small_all.reshape(1, SMALL_ROWS, D_MODEL)
    _, r_bt, r_cdiag, loss = _unpack_rows(small_all, reduced_shapes)
    loss = loss.reshape(())
    pack = lambda src: _pack_rows([src[n] for n in SMALL], SMALL_PACKED_ROWS)
    outs = _adamw("adamw_small", pack(weights), small_all, pack(mom1), pack(mom2))
    unpacked = [_unpack_rows(o, [weights[n].shape for n in SMALL]) for o in outs]
    for i, name in enumerate(SMALL):
        res[name] = [u[i] for u in unpacked]
    to_ghp = lambda t: jnp.transpose(t.reshape(2, S5_GROUP, S5_GROUPS, S5_STATE), (0, 2, 1, 3))[None]
    r_c = jnp.transpose(r_cdiag.reshape(2, S5_BLOCKS, S5_GROUP, 2, groups_per_block, S5_STATE), (3, 0, 1, 4, 2, 5)).reshape(
        2, 1, 2, S5_GROUPS, S5_GROUP, S5_STATE)
    swap = lambda t: jnp.swapaxes(t, -1, -2)
    for name, grad in (("s5_b_re", to_ghp(r_bt[:, :S5_GROUP])), ("s5_b_im", to_ghp(r_bt[:, S5_GROUP:]))):
        outs = _adamw_native("adamw_" + name, swap(weights[name]), grad, swap(mom1[name]), swap(mom2[name]))
        res[name] = [swap(grad), *[swap(o) for o in outs]]
    for name, grad in (("s5_c_re", r_c[0]), ("s5_c_im", -r_c[1])):
        res[name] = [grad, *_adamw_native("adamw_" + name, weights[name], grad, mom1[name], mom2[name])]

    return (loss, grad_x[None], *[res[n][0] for n in order], *[res[n][1] for n in order],
            *[res[n][2] for n in order], *[res[n][3] for n in order])
```

```python
import jax
import jax.numpy as jnp
from jax import lax
from jax.experimental import pallas as pl
from jax.experimental.pallas import tpu as pltpu

F32 = jnp.float32
BF16 = jnp.bfloat16
MESH = pl.DeviceIdType.MESH
ANY = pl.BlockSpec(memory_space=pl.ANY)

NDEV = 8
D_MODEL = 1024
GRID_W = 64
S5_WIDTH = 512
S5_GROUP = 16
S5_GROUPS = 32
S5_STATE = 64
NSTATE = S5_GROUPS * S5_STATE
CONV_WIDTH = 512
CONV_K = 31
IN_COLS = S5_WIDTH + 2 * CONV_WIDTH
D_FF = 4 * D_MODEL
EPS_RMS = 1e-6
EPS_LN = 1e-5
ADAM_LR = 0.001
ADAM_B1 = 0.9
ADAM_B2 = 0.999
ADAM_EPS = 1e-08
ADAM_WD = 0.01
ADAM_STEP = 10

SUBLANES = 8
LANES = 128
ROW_BLOCK = 256
SCAN_LANES = 512
SCAN_UNROLL = 32
SEGMENTS = SUBLANES
STEPS = ROW_BLOCK // SEGMENTS
S5_BLOCKS = 4
S5_BLOCK_WIDTH = S5_WIDTH // S5_BLOCKS
CONV_ROWS = 64
CONV_BWD_ROWS = 32
VMEM_LIMIT = 48 * 1024 * 1024
SMALL_ROWS = 320


def _params(sem=None):
    kw = dict(vmem_limit_bytes=VMEM_LIMIT)
    if sem is not None:
        kw["dimension_semantics"] = sem
    return pltpu.CompilerParams(**kw)


def _sds(shape, dtype=F32):
    return jax.ShapeDtypeStruct(tuple(shape), dtype)


def _fold8(x):
    return x.reshape(x.shape[0] // SUBLANES, SUBLANES, x.shape[1]).sum(axis=0)


def _sigmoid(x):
    return 1.0 / (1.0 + jnp.exp(-x))


def _silu(x):
    return x * _sigmoid(x)


def _dsilu(x):
    s = _sigmoid(x)
    return s * (1.0 + x * (1.0 - s))


_GELU_C = 0.7978845608028654


def _gelu(x):
    return 0.5 * x * (1.0 + jnp.tanh(_GELU_C * (x + 0.044715 * x * x * x)))


def _dgelu(x):
    t = jnp.tanh(_GELU_C * (x + 0.044715 * x * x * x))
    return 0.5 * (1.0 + t) + 0.5 * x * (1.0 - t * t) * _GELU_C * (1.0 + 3.0 * 0.044715 * x * x)


def _rms(x):
    rstd = lax.rsqrt(jnp.mean(x * x, axis=-1, keepdims=True) + EPS_RMS)
    return x * rstd, rstd


def _epi_residual_prenorm(acc, res, gate, gain, scale, shift):
    h = res + gate * acc
    xh, _ = _rms(h)
    a = (xh * gain) * (1.0 + scale) + shift
    return acc, h, a, a.T


def _epi_residual_loss(acc, res, gate, target, gain):
    h = res + gate * acc
    xh, rstd = _rms(h)
    err = xh * gain - target
    dy = err * (1.0 / h.shape[-1])
    dxh = dy * gain
    dh = rstd * (dxh - xh * jnp.mean(dxh * xh, axis=-1, keepdims=True))
    return acc, dh, dh * gate, _fold8(err * err), _fold8(dy * xh)


def _epi_norm_bwd(d_act, x, res, aux, gain, scale, gate=None):
    xh, rstd = _rms(x)
    dn = d_act * (1.0 + scale)
    dxh = dn * gain
    dx = res + rstd * (dxh - xh * jnp.mean(dxh * xh, axis=-1, keepdims=True))
    sums = (_fold8(d_act), _fold8(d_act * (xh * gain)), _fold8(dn * xh), _fold8(res * aux))
    return (dx, *sums) if gate is None else (dx, dx * gate, *sums)


def _dot(a, b, mode):
    dims = {"nn": (((1,), (0,)), ((), ())), "nt": (((1,), (1,)), ((), ())), "tn": (((0,), (0,)), ((), ()))}[mode]
    return lax.dot_general(a, b, dims, preferred_element_type=F32)


def _peers(x, y, c):
    out = []
    for k in range(1, NDEV):
        px = 1 - x if k & 4 else x
        py = 1 - y if k & 2 else y
        pc = 1 - c if k & 1 else c
        out.append(((px, py, pc), 4 * px + 2 * py + pc))
    return out


def _exchange_copies(src, land, send_sems, recv_sems, gather):
    x, y, c = lax.axis_index("x"), lax.axis_index("y"), lax.axis_index("c")
    me = 4 * x + 2 * y + c
    out = []
    for a in range(len(src)):
        for k, (peer, plin) in enumerate(_peers(x, y, c)):
            chunk = src[a] if gather[a] else src[a].at[plin]
            sems = dict(send_sem=send_sems.at[a * (NDEV - 1) + k], recv_sem=recv_sems.at[a * (NDEV - 1) + k],
                        device_id=peer, device_id_type=MESH)
            out.append((pltpu.make_async_remote_copy(src_ref=chunk, dst_ref=land[a].at[me], **sems),
                        pltpu.make_async_remote_copy(src_ref=chunk, dst_ref=land[a].at[plin], **sems)))
    return out


def _exchange(name, srcs, gather):
    n = len(srcs)
    outs = [_sds(((NDEV,) + s.shape) if g else s.shape, s.dtype) for s, g in zip(srcs, gather)]

    def body(*refs):
        src, dst, token = refs[:n], refs[n:2 * n], refs[2 * n]
        send_sems, recv_sems, local_sems = refs[2 * n + 1:]
        me = 4 * lax.axis_index("x") + 2 * lax.axis_index("y") + lax.axis_index("c")
        local = [pltpu.make_async_copy(src[a] if gather[a] else src[a].at[me], dst[a].at[me], local_sems.at[a])
                 for a in range(n)]
        for copy in local:
            copy.start()
        copies = _exchange_copies(src, dst, send_sems, recv_sems, gather)
        for copy, _ in copies:
            copy.start()
        token[...] = jnp.zeros_like(token)
        for copy, landing in copies:
            copy.wait_send()
            landing.wait_recv()
        for copy in local:
            copy.wait()

    nsem = n * (NDEV - 1)
    out = pl.pallas_call(
        body, name=name, out_shape=outs + [_sds((SUBLANES, LANES))], in_specs=[ANY] * n,
        out_specs=[ANY] * n + [pl.BlockSpec(memory_space=pltpu.VMEM)],
        scratch_shapes=[pltpu.SemaphoreType.DMA((nsem,)), pltpu.SemaphoreType.DMA((nsem,)), pltpu.SemaphoreType.DMA((n,))],
    )(*srcs)
    return out[:n], out[n]


HBM = pl.BlockSpec(memory_space=pltpu.HBM)
SEM = pl.BlockSpec(memory_space=pltpu.SEMAPHORE)
EFFECT = pltpu.SideEffectType.DATAFLOW_SIDE_EFFECTING


def _own_chunk_copies(src, land, local_sems, gather):
    me = 4 * lax.axis_index("x") + 2 * lax.axis_index("y") + lax.axis_index("c")
    return [pltpu.make_async_copy(src[a] if gather[a] else src[a].at[me], land[a].at[me], local_sems.at[a])
            for a in range(len(src))]


def _exchange_start_groups(name, groups):
    srcs = [s for g_srcs, _ in groups for s in g_srcs]
    gathers = [g for _, g_gather in groups for g in g_gather]
    lands = [lax.empty(((NDEV,) + s.shape) if g else s.shape, s.dtype) for s, g in zip(srcs, gathers)]
    n, ng = len(srcs), len(groups)

    def body(*refs):
        src, land = refs[:n], refs[n:2 * n]
        sems = refs[2 * n:2 * n + 3 * ng]
        token = refs[-1]
        first = 0
        for g, (g_srcs, g_gather) in enumerate(groups):
            last = first + len(g_srcs)
            send, recv, local = sems[3 * g:3 * g + 3]
            for copy, _ in _exchange_copies(src[first:last], land[first:last], send, recv, g_gather):
                copy.start()
            for copy in _own_chunk_copies(src[first:last], land[first:last], local, g_gather):
                copy.start()
            first = last
        token[...] = jnp.zeros_like(token)

    hbm = lambda v: pltpu.HBM(v.shape, v.dtype)
    sem_shapes = []
    for g_srcs, _ in groups:
        sem_shapes += [pltpu.SemaphoreType.DMA((len(g_srcs) * (NDEV - 1),))] * 2 + [pltpu.SemaphoreType.DMA((len(g_srcs),))]
    out = pl.pallas_call(
        body, name=name,
        out_shape=(*sem_shapes, *[hbm(v) for v in srcs], *[hbm(v) for v in lands], _sds((SUBLANES, LANES))),
        in_specs=[HBM] * (2 * n),
        out_specs=(*([SEM] * (3 * ng)), *([HBM] * (2 * n)), pl.BlockSpec(memory_space=pltpu.VMEM)),
        input_output_aliases={i: 3 * ng + i for i in range(2 * n)},
        compiler_params=pltpu.CompilerParams(has_side_effects=EFFECT),
    )(*[pltpu.with_memory_space_constraint(v, pltpu.HBM) for v in srcs + lands])
    src_out, land_out = out[3 * ng:3 * ng + n], out[3 * ng + n:3 * ng + 2 * n]
    result, first = [], 0
    for g, (g_srcs, _) in enumerate(groups):
        last = first + len(g_srcs)
        result.append(((out[3 * g], out[3 * g + 2]), out[3 * g + 1], src_out[first:last], land_out[first:last]))
        first = last
    return result, out[-1]


def _exchange_start(name, srcs, gather):
    (group,), token = _exchange_start_groups(name, [(srcs, gather)])
    return (*group, token)


def _exchange_wait(name, send_sems, recv_sems, srcs, lands, gather, after):
    n = len(srcs)

    def body(*refs):
        src, land = refs[:n], refs[n:2 * n]
        send_ref, local_ref, recv_ref = refs[2 * n:2 * n + 3]
        for copy, landing in _exchange_copies(src, land, send_ref, recv_ref, gather):
            copy.wait_send()
            landing.wait_recv()
        for copy in _own_chunk_copies(src, land, local_ref, gather):
            copy.wait()

    hbm = lambda v: pltpu.HBM(v.shape, v.dtype)
    out = pl.pallas_call(
        body, name=name, out_shape=[hbm(v) for v in list(srcs) + list(lands)],
        in_specs=[HBM] * (2 * n) + [SEM, SEM, SEM, ANY], out_specs=[HBM] * (2 * n),
        input_output_aliases={i: i for i in range(2 * n)},
        compiler_params=pltpu.CompilerParams(has_side_effects=EFFECT),
    )(*srcs, *lands, send_sems[0], send_sems[1], recv_sems, after)
    return out[:n], out[n:]


def _matmul(name, a, b, mode, mnk, tiles, outs, a_spec=None, b_spec=None, a_fn=None, a_extra=(),
            epi=None, epi_extra=(), out_specs=None, b_slabs=1):
    m_, n_, k_ = mnk
    tm, tn, tk = tiles
    nk = k_ // tk
    if a_spec is None:
        a_spec = (pl.BlockSpec((tk, tm), lambda i, j, k: (k, i)) if mode == "tn"
                  else pl.BlockSpec((tm, tk), lambda i, j, k: (i, k)))
    if b_spec is None:
        b_spec = (pl.BlockSpec((tn, tk), lambda i, j, k: (j, k)) if mode == "nt"
                  else pl.BlockSpec((tk, tn), lambda i, j, k: (k, j)))
    if out_specs is None:
        out_specs = [pl.BlockSpec((tm, tn), lambda i, j, k: (i, j)) for _ in outs]
    na, ne, no = len(a_extra), len(epi_extra), len(outs)

    def body(*refs):
        a_ref, b_ref = refs[0], refs[1]
        ax = refs[2:2 + na]
        ex = refs[2 + na:2 + na + ne]
        o = refs[2 + na + ne:2 + na + ne + no]

        def finish(res):
            res = epi(res, *[r[...] for r in ex]) if epi is not None else (res,)
            for ref, val in zip(o, res):
                ref[...] = val.astype(ref.dtype)

        at = a_ref[...]
        if a_fn is not None:
            at = a_fn(at, *[r[...] for r in ax])
        at = at.astype(BF16)
        if b_slabs == 1:
            part = _dot(at, b_ref[...].astype(BF16), mode)
        else:
            ks = tk // b_slabs
            part = _dot(at[:, 0:ks], b_ref[0].astype(BF16), mode)
            for s in range(1, b_slabs):
                part = part + _dot(at[:, s * ks:(s + 1) * ks], b_ref[s].astype(BF16), mode)
        if nk == 1:
            finish(part)
            return
        acc = refs[-1]
        k = pl.program_id(2)

        @pl.when(k == 0)
        def _():
            acc[...] = part

        @pl.when(k > 0)
        def _():
            acc[...] += part

        @pl.when(k == nk - 1)
        def _():
            finish(acc[...])

    return pl.pallas_call(
        body, name=name, grid=(m_ // tm, n_ // tn, nk),
        in_specs=[a_spec, b_spec] + [s for _, s in a_extra] + [s for _, s in epi_extra],
        out_specs=out_specs, out_shape=[_sds(s, d) for s, d in outs],
        scratch_shapes=[pltpu.VMEM((tm, tn), F32)] if nk > 1 else [],
        compiler_params=_params(("parallel", "parallel", "arbitrary")),
    )(a, b, *[x for x, _ in a_extra], *[x for x, _ in epi_extra])


def _prenorm(name, x, ctx, gain, shsc):
    n_lat, n_ctx = x.shape[0] // ROW_BLOCK, ctx.shape[0] // ROW_BLOCK
    rows, d = (n_lat + n_ctx) * ROW_BLOCK, x.shape[1]

    def norm(src, g_ref, m_ref, o_ref, t_ref):
        xv = src[...]
        xh = xv * lax.rsqrt(jnp.mean(xv * xv, axis=-1, keepdims=True) + EPS_RMS)
        a = (xh * g_ref[...]) * (1.0 + m_ref[1:2, :]) + m_ref[0:1, :]
        o_ref[...] = a.astype(o_ref.dtype)
        t_ref[...] = a.T.astype(t_ref.dtype)

    def body(x_ref, c_ref, g_ref, m_ref, o_ref, t_ref):
        i = pl.program_id(0)

        @pl.when(i < n_lat)
        def _():
            norm(x_ref, g_ref, m_ref, o_ref, t_ref)

        @pl.when(i >= n_lat)
        def _():
            norm(c_ref, g_ref, m_ref, o_ref, t_ref)

    return pl.pallas_call(
        body, name=name, grid=(n_lat + n_ctx,),
        in_specs=[pl.BlockSpec((ROW_BLOCK, d), lambda i: (jnp.minimum(i, n_lat - 1), 0)),
                  pl.BlockSpec((ROW_BLOCK, d), lambda i: (jnp.maximum(i - n_lat, 0), 0)),
                  pl.BlockSpec((1, d), lambda i: (0, 0)),
                  pl.BlockSpec((None, 2, d), lambda i: (jnp.minimum(i // n_lat, 1), 0, 0))],
        out_specs=[pl.BlockSpec((ROW_BLOCK, d), lambda i: (i, 0)), pl.BlockSpec((d, ROW_BLOCK), lambda i: (0, i))],
        out_shape=[_sds((rows, d), BF16), _sds((d, rows), BF16)],
        compiler_params=_params(("parallel",)),
    )(x, ctx, gain, shsc)


def _norm_bwd(name, x, d_act, d_act_row0, gain, scale, res=None, aux=None):
    rows, d = x.shape
    nb = rows // ROW_BLOCK
    has_res = res is not None

    def body(*refs):
        if has_res:
            x_ref, da_ref, g_ref, sc_ref, r_ref, aux_ref, dx_ref, sums = refs
        else:
            x_ref, da_ref, g_ref, sc_ref, sums = refs
        i = pl.program_id(0)

        @pl.when(i == 0)
        def _():
            sums[...] = jnp.zeros_like(sums)

        xv, da = x_ref[...], da_ref[...]
        rstd = lax.rsqrt(jnp.mean(xv * xv, axis=-1, keepdims=True) + EPS_RMS)
        xh = xv * rstd
        g = g_ref[...]
        dn = da * (1.0 + sc_ref[...])
        sums[0] += _fold8(da)
        sums[1] += _fold8(da * (xh * g))
        sums[2] += _fold8(dn * xh)
        if has_res:
            dxh = dn * g
            dx = rstd * (dxh - xh * jnp.mean(dxh * xh, axis=-1, keepdims=True))
            rv = r_ref[...]
            dx_ref[...] = rv + dx
            sums[3] += _fold8(rv * aux_ref[...])

    row = lambda i: (i, 0)
    vec = pl.BlockSpec((1, d), lambda i: (0, 0))
    in_specs = [pl.BlockSpec((ROW_BLOCK, d), row), pl.BlockSpec((ROW_BLOCK, d), lambda i: (i + d_act_row0, 0)), vec, vec]
    args = [x, d_act, gain, scale]
    out_shape = [_sds((4, SUBLANES, d))]
    out_specs = [pl.BlockSpec((4, SUBLANES, d), lambda i: (0, 0, 0))]
    if has_res:
        in_specs += [pl.BlockSpec((ROW_BLOCK, d), row), pl.BlockSpec((ROW_BLOCK, d), row)]
        args += [res, aux]
        out_shape = [_sds((rows, d))] + out_shape
        out_specs = [pl.BlockSpec((ROW_BLOCK, d), row)] + out_specs
    return pl.pallas_call(
        body, name=name, grid=(nb,), in_specs=in_specs, out_specs=out_specs, out_shape=out_shape,
        compiler_params=_params(("arbitrary",)),
    )(*args)


def _ada_fwd(cond16, ada_w_loc, ada_b_loc):
    cols = ada_w_loc.shape[1]

    def body(c_ref, w_ref, b_ref, o_ref):
        s = _silu(c_ref[...]).astype(BF16)
        o_ref[...] = _dot(s, w_ref[...].astype(BF16), "nn") + b_ref[...]

    return pl.pallas_call(body, name="ada_fwd", out_shape=_sds((16, cols)), compiler_params=_params())(
        cond16, ada_w_loc, ada_b_loc)


def _ada_bwd(cond16, dmod16, ada_w_loc, c_ctx_row):
    k_, cols = ada_w_loc.shape

    def body(c_ref, dm_ref, w_ref, cc_ref, gw_ref, gc_ref):
        s = _silu(c_ref[...]).astype(BF16)
        dm = dm_ref[...]
        gw_ref[...] = _dot(s, dm.astype(BF16), "tn")
        dmc = jnp.sum(dm[8:16, :], axis=0, keepdims=True)
        dmc8 = jnp.broadcast_to(dmc, (SUBLANES, cols)).astype(BF16)
        ds = _dot(dmc8, w_ref[...].astype(BF16), "nt")
        row = lax.broadcasted_iota(jnp.int32, ds.shape, 0)
        gc_ref[...] = jnp.where(row == 0, ds * _dsilu(cc_ref[...]), 0.0)

    return pl.pallas_call(body, name="ada_bwd", out_shape=[_sds((k_, cols)), _sds((SUBLANES, k_))],
                          compiler_params=_params())(cond16, dmod16, ada_w_loc, c_ctx_row)


def _cmul(a, b):
    return a[0] * b[0] - a[1] * b[1], a[0] * b[1] + a[1] * b[0]


def _disc(lam_re, lam_im, ldt):
    dt = jnp.exp(ldt)
    mag = jnp.exp(lam_re * dt)
    th = lam_im * dt
    a_re, a_im = mag * jnp.cos(th), mag * jnp.sin(th)
    den = lam_re * lam_re + lam_im * lam_im
    n_re = a_re - 1.0
    f_re = (n_re * lam_re + a_im * lam_im) / den
    f_im = (a_im * lam_re - n_re * lam_im) / den
    return dt, mag, th, a_re, a_im, den, n_re, f_re, f_im


def _block_diag_mask(shape):
    row = lax.broadcasted_iota(jnp.int32, shape, 0)
    col = lax.broadcasted_iota(jnp.int32, shape, 1)
    return lax.shift_right_logical(row, 4) == lax.shift_right_logical(col, 6)


TAB_A = 0
TAB_BIG = 1
TAB_SEG = 4
TAB_PW = 5
TAB_ROWS = TAB_PW + STEPS


def _s5_discretise(name, ascending, lam_re, lam_im, ldt, bt_re, bt_im, ct_re, ct_im):
    def write_tables(ref, pw, big, asc, sign):
        row = lax.broadcasted_iota(jnp.int32, (SUBLANES, NSTATE), 0)
        full = lambda v: jnp.broadcast_to(v, (SUBLANES, NSTATE))

        def put(t, p):
            ref[0, t] = full(p[0])
            ref[1, t] = full(sign * p[1])

        put(TAB_A, pw[0])
        for t in range(3):
            put(TAB_BIG + t, big[t])
        seg = [big[0]]
        for _ in range(SEGMENTS - 1):
            seg.append(_cmul(seg[-1], big[0]))
        seg_re = jnp.zeros((SUBLANES, NSTATE), F32)
        seg_im = jnp.zeros((SUBLANES, NSTATE), F32)
        for r in range(SEGMENTS):
            p = seg[r] if asc else seg[SEGMENTS - 1 - r]
            seg_re = jnp.where(row == r, p[0], seg_re)
            seg_im = jnp.where(row == r, sign * p[1], seg_im)
        ref[0, TAB_SEG] = seg_re
        ref[1, TAB_SEG] = seg_im
        for k in range(STEPS):
            put(TAB_PW + k, pw[k])

    def body(lr_ref, li_ref, ldt_ref, br_ref, bi_ref, cr_ref, ci_ref, bb_ref, tab_ref, adj_ref, bm_ref, cm_ref):
        _, _, _, a_re, a_im, _, _, f_re, f_im = _disc(lr_ref[...], li_ref[...], ldt_ref[...])
        bre, bim = br_ref[...], bi_ref[...]
        bb_re = f_re * bre - f_im * bim
        bb_im = f_re * bim + f_im * bre
        bb_ref[0:S5_GROUP, :] = bb_re
        bb_ref[S5_GROUP:2 * S5_GROUP, :] = bb_im
        pw = [(a_re, a_im)]
        for _ in range(STEPS - 1):
            pw.append(_cmul(pw[-1], (a_re, a_im)))
        big = [pw[STEPS - 1]]
        for _ in range(2):
            big.append(_cmul(big[-1], big[-1]))
        write_tables(tab_ref, pw, big, ascending, 1.0)
        write_tables(adj_ref, pw, big, not ascending, -1.0)
        half = NSTATE // S5_BLOCKS
        mask = _block_diag_mask((S5_BLOCK_WIDTH, half))
        tile = lambda v: jnp.broadcast_to(v[None], (S5_BLOCK_WIDTH // S5_GROUP, S5_GROUP, half)).reshape(S5_BLOCK_WIDTH, half)
        for c in range(S5_BLOCKS):
            cols = slice(c * half, (c + 1) * half)
            rows = slice(c * S5_BLOCK_WIDTH, (c + 1) * S5_BLOCK_WIDTH)
            bm_ref[c, :, 0:half] = jnp.where(mask, tile(bb_re[:, cols]), 0.0).astype(BF16)
            bm_ref[c, :, half:2 * half] = jnp.where(mask, tile(bb_im[:, cols]), 0.0).astype(BF16)
            cm_ref[c, :, 0:half] = jnp.where(mask, cr_ref[rows, :], 0.0).astype(BF16)
            cm_ref[c, :, half:2 * half] = jnp.where(mask, -ci_ref[rows, :], 0.0).astype(BF16)

    blocked = _sds((S5_BLOCKS, S5_BLOCK_WIDTH, 2 * NSTATE // S5_BLOCKS), BF16)
    return pl.pallas_call(
        body, name=name,
        out_shape=[_sds((2 * S5_GROUP, NSTATE)), _sds((2, TAB_ROWS, SUBLANES, NSTATE)),
                   _sds((2, TAB_ROWS, SUBLANES, NSTATE)), blocked, blocked],
        compiler_params=_params(),
    )(lam_re, lam_im, ldt, bt_re, bt_im, ct_re, ct_im)


def _s5_discretise_bwd(name, lam_re, lam_im, ldt, bt_re, bt_im, d_abar8, d_bbar):
    def body(lr_ref, li_ref, ldt_ref, br_ref, bi_ref, da_ref, db_ref, dl_ref, dbt_ref):
        lam_re, lam_im = lr_ref[...], li_ref[...]
        dt, _, _, a_re, a_im, den, n_re, f_re, f_im = _disc(lam_re, lam_im, ldt_ref[...])
        bre, bim = br_ref[...], bi_ref[...]
        dbr, dbi = db_ref[0:S5_GROUP, :], db_ref[S5_GROUP:2 * S5_GROUP, :]
        dbt_ref[0:S5_GROUP, :] = f_re * dbr + f_im * dbi
        dbt_ref[S5_GROUP:2 * S5_GROUP, :] = f_re * dbi - f_im * dbr
        df_re = jnp.sum(bre * dbr + bim * dbi, axis=0, keepdims=True)
        df_im = jnp.sum(bre * dbi - bim * dbr, axis=0, keepdims=True)
        da = da_ref[...]
        da_re = jnp.sum(da[:, 0:NSTATE], axis=0, keepdims=True)
        da_im = jnp.sum(da[:, NSTATE:2 * NSTATE], axis=0, keepdims=True)
        da_re = da_re + (df_re * lam_re - df_im * lam_im) / den
        da_im = da_im + (df_re * lam_im + df_im * lam_re) / den
        ff = (f_re * df_re + f_im * df_im) * 2.0 / den
        d_lr = (df_re * n_re + df_im * a_im) / den - ff * lam_re
        d_li = (df_re * a_im - df_im * n_re) / den - ff * lam_im
        d_mag_mag = da_re * a_re + da_im * a_im
        d_th = da_im * a_re - da_re * a_im
        d_lr = d_lr + d_mag_mag * dt
        d_li = d_li + d_th * dt
        d_ldt = (d_mag_mag * lam_re + d_th * lam_im) * dt
        row = lax.broadcasted_iota(jnp.int32, (SUBLANES, NSTATE), 0)
        dl_ref[...] = jnp.where(row == 0, d_lr, jnp.where(row == 1, d_li, jnp.where(row == 2, d_ldt, 0.0)))

    return pl.pallas_call(
        body, name=name, out_shape=[_sds((SUBLANES, NSTATE)), _sds((2 * S5_GROUP, NSTATE))],
        compiler_params=_params(),
    )(lam_re, lam_im, ldt, bt_re, bt_im, d_abar8, d_bbar)


def _segment_permutation(reverse_time):
    rho = jnp.arange(ROW_BLOCK)
    src = STEPS * (rho % SEGMENTS) + rho // SEGMENTS
    if reverse_time:
        src = ROW_BLOCK - 1 - src
    return (src[:, None] == jnp.arange(ROW_BLOCK)[None, :]).astype(BF16)


def _permute_rows(perm_ref, v):
    return _dot(perm_ref[...], v, "nn").astype(BF16)


def _unpermute_rows(perm_t_ref, v):
    hi = v.astype(BF16)
    lo = (v - hi.astype(F32)).astype(BF16)
    return _dot(perm_t_ref[...], hi, "nn") + _dot(perm_t_ref[...], lo, "nn")


def _unrolled_loop(step, init):
    def trip(o, state):
        for u in range(SCAN_UNROLL):
            state = step(o * SCAN_UNROLL + u, state)
        return state

    if SCAN_UNROLL == STEPS:
        return trip(0, init)
    return lax.fori_loop(0, STEPS // SCAN_UNROLL, trip, init)


def _scan_chunk(x_ref, out_ref, tab_ref, carry_re, carry_im, ascending, pair_ref=None, acc_ref=None, lane_chunks=None):
    w = SCAN_LANES
    half = NSTATE // S5_BLOCKS
    row = lax.broadcasted_iota(jnp.int32, (SUBLANES, w), 0)
    last = (SEGMENTS - 1) if ascending else 0

    def from_previous_segment(v, k, fill):
        if ascending:
            return jnp.where(row >= k, pltpu.roll(v, k, 0), fill)
        return jnp.where(row < SEGMENTS - k, pltpu.roll(v, SEGMENTS - k, 0), fill)

    def tile_rows(k):
        return pl.ds(pl.multiple_of((k if ascending else STEPS - 1 - k) * SUBLANES, SUBLANES), SUBLANES)

    for j in (range(NSTATE // w) if lane_chunks is None else lane_chunks):
        n_l = pl.ds(j * w, w)
        lane0 = (j * w // half) * 2 * half + (j * w) % half
        re_l, im_l = pl.ds(lane0, w), pl.ds(lane0 + half, w)
        tab = lambda t, n_l=n_l: (tab_ref[0, t, :, n_l], tab_ref[1, t, :, n_l])
        a_re, a_im = tab(TAB_A)

        def local_step(k, h):
            rs = tile_rows(k)
            h_re = a_re * h[0] - a_im * h[1] + x_ref[rs, re_l]
            h_im = a_re * h[1] + a_im * h[0] + x_ref[rs, im_l]
            out_ref[rs, re_l] = h_re
            out_ref[rs, im_l] = h_im
            return h_re, h_im

        zero = jnp.zeros((SUBLANES, w), F32)
        end_re, end_im = _unrolled_loop(local_step, (zero, zero))
        for t, k in ((TAB_BIG, 1), (TAB_BIG + 1, 2), (TAB_BIG + 2, 4)):
            p_re, p_im = tab(t)
            s_re, s_im = from_previous_segment(end_re, k, 0.0), from_previous_segment(end_im, k, 0.0)
            end_re, end_im = end_re + (p_re * s_re - p_im * s_im), end_im + (p_re * s_im + p_im * s_re)
        c0_re, c0_im = carry_re[:, n_l], carry_im[:, n_l]
        p_re, p_im = tab(TAB_SEG)
        end_re = end_re + (p_re * c0_re - p_im * c0_im)
        end_im = end_im + (p_re * c0_im + p_im * c0_re)
        carry_re[:, n_l] = jnp.broadcast_to(end_re[last:last + 1, :], end_re.shape)
        carry_im[:, n_l] = jnp.broadcast_to(end_im[last:last + 1, :], end_im.shape)
        in_re = from_previous_segment(end_re, 1, c0_re)
        in_im = from_previous_segment(end_im, 1, c0_im)

        def carry_step(k, st):
            rs = tile_rows(k)
            p_re, p_im = tab_ref[0, TAB_PW + k, :, n_l], tab_ref[1, TAB_PW + k, :, n_l]
            o_re = out_ref[rs, re_l] + (p_re * in_re - p_im * in_im)
            o_im = out_ref[rs, im_l] + (p_re * in_im + p_im * in_re)
            out_ref[rs, re_l] = o_re
            out_ref[rs, im_l] = o_im
            return st

        _unrolled_loop(carry_step, 0)
        if pair_ref is not None:
            def pair_terms(p_re, p_im, rs):
                s_re, s_im = pair_ref[rs, re_l], pair_ref[rs, im_l]
                return p_re * s_re + p_im * s_im, p_im * s_re - p_re * s_im

            def pair_step(k, acc):
                prev = tile_rows(k - 1)
                t_re, t_im = pair_terms(out_ref[prev, re_l], out_ref[prev, im_l], tile_rows(k))
                return acc[0] + t_re, acc[1] + t_im

            fin = lax.fori_loop(1, STEPS, pair_step, pair_terms(in_re, in_im, tile_rows(0)))
            acc_ref[:, n_l] += fin[0]
            acc_ref[:, pl.ds(NSTATE + j * w, w)] += fin[1]


def _scan_block_index(i, n_lat, ctx_first_then_ascending):
    if ctx_first_then_ascending:
        return jnp.where(i == 0, n_lat, i - 1)
    return jnp.where(i == 0, n_lat, n_lat - i)


def _full_spec(shape):
    return pl.BlockSpec(shape, lambda i: (0,) * len(shape))


_S5_BLOCKED = (S5_BLOCKS, S5_BLOCK_WIDTH, 2 * NSTATE // S5_BLOCKS)
_S5_TABLES = (2, TAB_ROWS, SUBLANES, NSTATE)
_S5_DIAG = (S5_BLOCKS, S5_GROUP, 2 * NSTATE // S5_BLOCKS)


def _s5_scan_fwd(name, ascending, z_all, bmat, cmat, tab, perm, perm_t, y_other=None, d_skip=None, w_glu=None):
    rows = z_all.shape[0]
    nb = rows // ROW_BLOCK
    n_lat = nb - 1
    bw, sw = S5_BLOCK_WIDTH, 2 * NSTATE // S5_BLOCKS
    gated = y_other is not None

    def body(*refs):
        u_ref, bm_ref, cm_ref, tab_ref, p_ref, pt_ref = refs[:6]
        extra = refs[6:9] if gated else ()
        s_ref, y_ref = refs[6 + len(extra):8 + len(extra)]
        bu, yp, carry_re, carry_im = refs[-4:]

        @pl.when(pl.program_id(0) == 0)
        def _():
            carry_re[...] = jnp.zeros_like(carry_re)
            carry_im[...] = jnp.zeros_like(carry_im)

        up = _permute_rows(p_ref, u_ref[...].astype(BF16))
        for c in range(S5_BLOCKS):
            bu[:, c * sw:(c + 1) * sw] = _dot(up[:, c * bw:(c + 1) * bw], bm_ref[c], "nn")
        _scan_chunk(bu, s_ref, tab_ref, carry_re, carry_im, False)
        for c in range(S5_BLOCKS):
            yp[:, c * bw:(c + 1) * bw] = _dot(s_ref[:, c * sw:(c + 1) * sw].astype(BF16), cm_ref[c], "nt")
        y = _unpermute_rows(pt_ref, yp[...])
        y_ref[...] = y
        if gated:
            y_other_ref, d_ref, w_ref = extra
            gel = _gelu(d_ref[...] * u_ref[...] + y_other_ref[...] + y)
            refs[8 + len(extra)][...] = (gel * _sigmoid(_dot(gel.astype(BF16), w_ref[...], "nn"))).astype(BF16)

    blk = lambda i: (_scan_block_index(i, n_lat, ascending), 0)
    in_specs = [pl.BlockSpec((ROW_BLOCK, S5_WIDTH), blk), _full_spec(_S5_BLOCKED), _full_spec(_S5_BLOCKED),
                _full_spec(_S5_TABLES), _full_spec((ROW_BLOCK, ROW_BLOCK)), _full_spec((ROW_BLOCK, ROW_BLOCK))]
    args = [z_all, bmat, cmat, tab, perm, perm_t]
    out_specs = [pl.BlockSpec((ROW_BLOCK, 2 * NSTATE), blk), pl.BlockSpec((ROW_BLOCK, S5_WIDTH), blk)]
    out_shape = [_sds((rows, 2 * NSTATE)), _sds((rows, S5_WIDTH))]
    if gated:
        in_specs += [pl.BlockSpec((ROW_BLOCK, S5_WIDTH), blk), _full_spec((1, S5_WIDTH)), _full_spec((S5_WIDTH, S5_WIDTH))]
        args += [y_other, d_skip, w_glu]
        out_specs.append(pl.BlockSpec((ROW_BLOCK, S5_WIDTH),
                                      lambda i: (jnp.minimum(_scan_block_index(i, n_lat, ascending), n_lat - 1), 0)))
        out_shape.append(_sds((n_lat * ROW_BLOCK, S5_WIDTH + CONV_WIDTH), BF16))
    return pl.pallas_call(
        body, name=name, grid=(nb,), in_specs=in_specs, out_specs=out_specs, out_shape=out_shape,
        scratch_shapes=[pltpu.VMEM((ROW_BLOCK, 2 * NSTATE), F32), pltpu.VMEM((ROW_BLOCK, S5_WIDTH), F32),
                        pltpu.VMEM((SUBLANES, NSTATE), F32), pltpu.VMEM((SUBLANES, NSTATE), F32)],
        compiler_params=_params(("arbitrary",)),
    )(*args)


def _s5_scan_bwd(name, ascending, dy, z_all, states, bmat, cmat, adj, perm, perm_t, du_other=None, d_skip=None):
    rows = states.shape[0]
    nb = rows // ROW_BLOCK
    n_lat = nb - 1
    bw, sw = S5_BLOCK_WIDTH, 2 * NSTATE // S5_BLOCKS
    finish = du_other is not None

    def block_index(i):
        if ascending:
            return jnp.where(i == nb - 1, n_lat, n_lat - 1 - i)
        return jnp.where(i == nb - 1, n_lat, i)

    def body(*refs):
        dy_ref, u_ref, s_ref, bm_ref, cm_ref, adj_ref, p_ref, pt_ref = refs[:8]
        extra = refs[8:10] if finish else ()
        du_ref, db_ref, dc_ref, da_ref, g, dup, db_acc, dc_acc, carry_re, carry_im = refs[8 + len(extra):]
        i = pl.program_id(0)

        @pl.when(i == 0)
        def _():
            carry_re[...] = jnp.zeros_like(carry_re)
            carry_im[...] = jnp.zeros_like(carry_im)
            da_ref[...] = jnp.zeros_like(da_ref)
            db_acc[...] = jnp.zeros_like(db_acc)
            dc_acc[...] = jnp.zeros_like(dc_acc)

        has_dy = (i < nb - 1).astype(F32)
        dyp = _permute_rows(p_ref, (dy_ref[...] * has_dy).astype(BF16))
        up = _permute_rows(p_ref, u_ref[...].astype(BF16))
        for c in range(S5_BLOCKS):
            g[:, c * sw:(c + 1) * sw] = _dot(dyp[:, c * bw:(c + 1) * bw], cm_ref[c], "nn")
            dc_acc[c] += _dot(dyp[:, c * bw:(c + 1) * bw], s_ref[:, c * sw:(c + 1) * sw].astype(BF16), "tn")
            _scan_chunk(g, g, adj_ref, carry_re, carry_im, True, pair_ref=s_ref, acc_ref=da_ref, lane_chunks=[c])
            gc = g[:, c * sw:(c + 1) * sw].astype(BF16)
            dup[:, c * bw:(c + 1) * bw] = _dot(gc, bm_ref[c], "nt")
            db_acc[c] += _dot(up[:, c * bw:(c + 1) * bw], gc, "tn")
        du = _unpermute_rows(pt_ref, dup[...])
        if finish:
            du = du + extra[0][...] + (dy_ref[...] * has_dy) * extra[1][...]
        du_ref[...] = du.astype(du_ref.dtype)

        @pl.when(i == nb - 1)
        def _():
            mask = _block_diag_mask((bw, sw // 2))
            for acc, out in ((db_acc, db_ref), (dc_acc, dc_ref)):
                for c in range(S5_BLOCKS):
                    for part in range(2):
                        cols = slice(part * (sw // 2), (part + 1) * (sw // 2))
                        kept = jnp.where(mask, acc[c, :, cols], 0.0)
                        out[c, :, cols] = kept.reshape(bw // S5_GROUP, S5_GROUP, sw // 2).sum(axis=0)

    blk = lambda i: (block_index(i), 0)
    in_specs = [pl.BlockSpec((ROW_BLOCK, S5_WIDTH), lambda i: (jnp.minimum(block_index(i), n_lat - 1), 0)),
                pl.BlockSpec((ROW_BLOCK, S5_WIDTH), blk), pl.BlockSpec((ROW_BLOCK, 2 * NSTATE), blk),
                _full_spec(_S5_BLOCKED), _full_spec(_S5_BLOCKED), _full_spec(_S5_TABLES),
                _full_spec((ROW_BLOCK, ROW_BLOCK)), _full_spec((ROW_BLOCK, ROW_BLOCK))]
    args = [dy, z_all, states, bmat, cmat, adj, perm, perm_t]
    if finish:
        in_specs += [pl.BlockSpec((ROW_BLOCK, S5_WIDTH), blk), _full_spec((1, S5_WIDTH))]
        args += [du_other, d_skip]
    return pl.pallas_call(
        body, name=name, grid=(nb,), in_specs=in_specs,
        out_specs=[pl.BlockSpec((ROW_BLOCK, S5_WIDTH), blk), _full_spec(_S5_DIAG), _full_spec(_S5_DIAG),
                   _full_spec((SUBLANES, 2 * NSTATE))],
        out_shape=[_sds((rows, S5_WIDTH), BF16 if finish else F32), _sds(_S5_DIAG), _sds(_S5_DIAG),
                   _sds((SUBLANES, 2 * NSTATE))],
        scratch_shapes=[pltpu.VMEM((ROW_BLOCK, 2 * NSTATE), F32), pltpu.VMEM((ROW_BLOCK, S5_WIDTH), F32),
                        pltpu.VMEM(_S5_BLOCKED, F32), pltpu.VMEM(_S5_BLOCKED, F32),
                        pltpu.VMEM((SUBLANES, NSTATE), F32), pltpu.VMEM((SUBLANES, NSTATE), F32)],
        compiler_params=_params(("arbitrary",)),
    )(*args)


def _latent_row_tile(n_rows):
    return 512 if n_rows % 512 == 0 else ROW_BLOCK


def _glu_bwd(d_ycat, z_all, y0, y1, d_skip, w_glu, n_rows):
    def body(do_ref, u_ref, y0_ref, y1_ref, d_ref, w_ref, dy_ref, dw_ref, dd_ref):
        @pl.when(pl.program_id(0) == 0)
        def _():
            dw_ref[...] = jnp.zeros_like(dw_ref)
            dd_ref[...] = jnp.zeros_like(dd_ref)

        u = u_ref[...]
        y = d_ref[...] * u + y0_ref[...] + y1_ref[...]
        g = _gelu(y)
        gb = g.astype(BF16)
        w = w_ref[...]
        sg = _sigmoid(_dot(gb, w, "nn"))
        do = do_ref[...]
        dt = do * g * sg * (1.0 - sg)
        dtb = dt.astype(BF16)
        dg = do * sg + _dot(dtb, w, "nt")
        dy = dg * _dgelu(y)
        dy_ref[...] = dy
        dw_ref[...] += _dot(gb, dtb, "tn")
        dd_ref[...] += _fold8(dy * u)

    rows = _latent_row_tile(n_rows)
    row = pl.BlockSpec((rows, S5_WIDTH), lambda i: (i, 0))
    sq = pl.BlockSpec((S5_WIDTH, S5_WIDTH), lambda i: (0, 0))
    return pl.pallas_call(
        body, name="glu_bwd", grid=(n_rows // rows,),
        in_specs=[row, row, row, row, pl.BlockSpec((1, S5_WIDTH), lambda i: (0, 0)), sq],
        out_specs=[row, sq, pl.BlockSpec((SUBLANES, S5_WIDTH), lambda i: (0, 0))],
        out_shape=[_sds((n_rows, S5_WIDTH)), _sds((S5_WIDTH, S5_WIDTH)), _sds((SUBLANES, S5_WIDTH))],
        compiler_params=_params(("arbitrary",)),
    )(d_ycat, z_all, y0, y1, d_skip, w_glu)


CONV_HALF = CONV_K // 2


def _conv_block(n_rows):
    blk = min(1024, n_rows)
    assert blk >= CONV_HALF * GRID_W and n_rows % blk == 0
    return blk


def _conv_gate(z_all, n_rows):
    blk = _conv_block(n_rows)
    nb = n_rows // blk

    def body(v_ref, g_ref, o_ref):
        i = pl.program_id(0)
        inside = jnp.logical_and(i >= 1, i <= nb)

        @pl.when(inside)
        def _():
            o_ref[...] = v_ref[...] * _sigmoid(g_ref[...])

        @pl.when(jnp.logical_not(inside))
        def _():
            o_ref[...] = jnp.zeros_like(o_ref)

    src = lambda col: pl.BlockSpec((blk, CONV_WIDTH), lambda i: (jnp.clip(i - 1, 0, nb - 1), col))
    return pl.pallas_call(
        body, name="conv_gate", grid=(nb + 2,), in_specs=[src(1), src(2)],
        out_specs=pl.BlockSpec((blk, CONV_WIDTH), lambda i: (i, 0)),
        out_shape=_sds(((nb + 2) * blk, CONV_WIDTH)), compiler_params=_params(("parallel",)),
    )(z_all, z_all)


def _stream_padded(pad_ref, buf, sems, blk, n_blocks):
    i = pl.program_id(0)

    def copy(b):
        rows = pl.ds(pl.multiple_of(b * blk, blk), blk)
        return pltpu.make_async_copy(pad_ref.at[rows, :], buf.at[rows, :], sems.at[b])

    @pl.when(i == 0)
    def _():
        for b in range(n_blocks):
            copy(b).start()
        copy(0).wait()
        copy(1).wait()

    copy(i + 2).wait()
    return pl.multiple_of(i * blk, blk)


def _conv_fwd(hh_pad, w, b, ln_g, ln_b, ycat, n_rows):
    blk = _conv_block(n_rows)
    nblk = n_rows // blk + 2

    def body(hh_ref, w_ref, b_ref, g_ref, lb_ref, ycat_ref, hc_ref, y_ref, win, sems):
        base = _stream_padded(hh_ref, win, sems, blk, nblk)

        def tile(t, _):
            r0 = pl.multiple_of(t * CONV_ROWS, CONV_ROWS)
            acc = jnp.zeros((CONV_ROWS, CONV_WIDTH), F32)
            for k in range(CONV_K):
                acc = acc + w_ref[k:k + 1, :] * win[pl.ds(base + r0 + blk + (k - CONV_HALF) * GRID_W, CONV_ROWS), :]
            hc = acc + b_ref[...]
            hc_ref[pl.ds(r0, CONV_ROWS), :] = hc
            mu = jnp.mean(hc, axis=-1, keepdims=True)
            xc = hc - mu
            ln = xc * lax.rsqrt(jnp.mean(xc * xc, axis=-1, keepdims=True) + EPS_LN) * g_ref[...] + lb_ref[...]
            y_ref[pl.ds(r0, CONV_ROWS), :] = _silu(ln).astype(y_ref.dtype)
            return 0

        lax.fori_loop(0, blk // CONV_ROWS, tile, 0)

    vec = pl.BlockSpec((1, CONV_WIDTH), lambda i: (0, 0))
    row = pl.BlockSpec((blk, CONV_WIDTH), lambda i: (i, 0))
    return pl.pallas_call(
        body, name="conv_fwd", grid=(n_rows // blk,),
        in_specs=[ANY, pl.BlockSpec((CONV_K, CONV_WIDTH), lambda i: (0, 0)), vec, vec, vec, ANY],
        out_specs=[row, pl.BlockSpec((blk, CONV_WIDTH), lambda i: (i, 1))],
        out_shape=[_sds((n_rows, CONV_WIDTH)), _sds(ycat.shape, ycat.dtype)], input_output_aliases={5: 1},
        scratch_shapes=[pltpu.VMEM((nblk * blk, CONV_WIDTH), F32), pltpu.SemaphoreType.DMA((nblk,))],
        compiler_params=_params(("arbitrary",)),
    )(hh_pad, w, b, ln_g, ln_b, ycat)


def _conv_bwd_norm(d_ycat, hc, ln_g, ln_b, n_rows):
    blk = _conv_block(n_rows)
    nb = n_rows // blk

    def body(dy_ref, hc_ref, g_ref, lb_ref, o_ref, sums):
        i = pl.program_id(0)

        @pl.when(i == 0)
        def _():
            sums[...] = jnp.zeros_like(sums)

        inside = jnp.logical_and(i >= 1, i <= nb)

        @pl.when(inside)
        def _():
            hcv = hc_ref[...]
            mu = jnp.mean(hcv, axis=-1, keepdims=True)
            xc = hcv - mu
            rstd = lax.rsqrt(jnp.mean(xc * xc, axis=-1, keepdims=True) + EPS_LN)
            xh = xc * rstd
            g = g_ref[...]
            dln = dy_ref[...] * _dsilu(xh * g + lb_ref[...])
            dxh = dln * g
            dhc = rstd * (dxh - jnp.mean(dxh, axis=-1, keepdims=True) - xh * jnp.mean(dxh * xh, axis=-1, keepdims=True))
            o_ref[...] = dhc
            sums[0] += _fold8(dhc)
            sums[1] += _fold8(dln * xh)
            sums[2] += _fold8(dln)

        @pl.when(jnp.logical_not(inside))
        def _():
            o_ref[...] = jnp.zeros_like(o_ref)

    vec = pl.BlockSpec((1, CONV_WIDTH), lambda i: (0, 0))
    return pl.pallas_call(
        body, name="conv_bwd_norm", grid=(nb + 2,),
        in_specs=[pl.BlockSpec((blk, CONV_WIDTH), lambda i: (jnp.clip(i - 1, 0, nb - 1), 1)),
                  pl.BlockSpec((blk, CONV_WIDTH), lambda i: (jnp.clip(i - 1, 0, nb - 1), 0)), vec, vec],
        out_specs=[pl.BlockSpec((blk, CONV_WIDTH), lambda i: (i, 0)),
                   pl.BlockSpec((3, SUBLANES, CONV_WIDTH), lambda i: (0, 0, 0))],
        out_shape=[_sds(((nb + 2) * blk, CONV_WIDTH)), _sds((3, SUBLANES, CONV_WIDTH))],
        compiler_params=_params(("arbitrary",)),
    )(d_ycat, hc, ln_g, ln_b)


def _conv_bwd_taps(dhc_pad, hh_pad, z_all, w, n_rows):
    blk = _conv_block(n_rows)
    nblk = n_rows // blk + 2

    def body(dhc_ref, hh_ref, v_ref, g_ref, w_ref, dv_ref, dg_ref, dw_ref, dwin, hwin, dsems, hsems):
        @pl.when(pl.program_id(0) == 0)
        def _():
            dw_ref[...] = jnp.zeros_like(dw_ref)

        base = _stream_padded(dhc_ref, dwin, dsems, blk, nblk)
        _stream_padded(hh_ref, hwin, hsems, blk, nblk)

        def tile(t, _):
            r0 = pl.multiple_of(t * CONV_BWD_ROWS, CONV_BWD_ROWS) + base
            dh = dwin[pl.ds(r0 + blk, CONV_BWD_ROWS), :]
            acc = jnp.zeros((CONV_BWD_ROWS, CONV_WIDTH), F32)
            for k in range(CONV_K):
                off = (k - CONV_HALF) * GRID_W
                acc = acc + w_ref[k:k + 1, :] * dwin[pl.ds(r0 + blk - off, CONV_BWD_ROWS), :]
                dw_ref[k] += _fold8(dh * hwin[pl.ds(r0 + blk + off, CONV_BWD_ROWS), :])
            rs = pl.ds(pl.multiple_of(t * CONV_BWD_ROWS, CONV_BWD_ROWS), CONV_BWD_ROWS)
            sg = _sigmoid(g_ref[rs, :])
            vv = v_ref[rs, :]
            dv_ref[rs, :] = (acc * sg).astype(dv_ref.dtype)
            dg_ref[rs, :] = (acc * vv * sg * (1.0 - sg)).astype(dg_ref.dtype)
            return 0

        lax.fori_loop(0, blk // CONV_BWD_ROWS, tile, 0)

    row = pl.BlockSpec((blk, CONV_WIDTH), lambda i: (i, 0))
    return pl.pallas_call(
        body, name="conv_bwd_taps", grid=(n_rows // blk,),
        in_specs=[ANY, ANY,
            pl.BlockSpec((blk, CONV_WIDTH), lambda i: (i, 1)), pl.BlockSpec((blk, CONV_WIDTH), lambda i: (i, 2)),
            pl.BlockSpec((CONV_K, CONV_WIDTH), lambda i: (0, 0))],
        out_specs=[row, row, pl.BlockSpec((CONV_K, SUBLANES, CONV_WIDTH), lambda i: (0, 0, 0))],
        out_shape=[_sds((n_rows, CONV_WIDTH), BF16), _sds((n_rows, CONV_WIDTH), BF16),
                   _sds((CONV_K, SUBLANES, CONV_WIDTH))],
        scratch_shapes=[pltpu.VMEM((nblk * blk, CONV_WIDTH), F32), pltpu.VMEM((nblk * blk, CONV_WIDTH), F32),
                        pltpu.SemaphoreType.DMA((nblk,)), pltpu.SemaphoreType.DMA((nblk,))],
        compiler_params=_params(("arbitrary",)),
    )(dhc_pad, hh_pad, z_all, z_all, w)


def _sum_parts(parts):
    _, r, c = parts.shape

    def body(p_ref, o_ref):
        acc = p_ref[0]
        for q in range(1, NDEV):
            acc = acc + p_ref[q]
        o_ref[...] = acc

    return pl.pallas_call(body, name="sum_parts", out_shape=_sds((r, c)), compiler_params=_params())(parts)


def _row_tile(r, c):
    best = r
    for t in (1024, 512, 256, 128, 64, 32, 16, 8):
        if r % t == 0 and t * c <= 128 * 1024:
            return t
    return best


def _adamw(name, w, gparts, m, v):
    r, c = w.shape
    np_ = gparts.shape[0]
    tr = _row_tile(r, c)

    def body(w_ref, g_ref, m_ref, v_ref, go_ref, d_ref, mo_ref, vo_ref):
        g = g_ref[0].astype(F32)
        for q in range(1, np_):
            g = g + g_ref[q].astype(F32)
        m2 = ADAM_B1 * m_ref[...] + (1.0 - ADAM_B1) * g
        v2 = ADAM_B2 * v_ref[...] + (1.0 - ADAM_B2) * jnp.square(g)
        m_hat = m2 / (1.0 - ADAM_B1 ** ADAM_STEP)
        v_hat = v2 / (1.0 - ADAM_B2 ** ADAM_STEP)
        go_ref[...] = g
        d_ref[...] = -ADAM_LR * (m_hat / (jnp.sqrt(v_hat) + ADAM_EPS) + ADAM_WD * w_ref[...])
        mo_ref[...] = m2
        vo_ref[...] = v2

    row = pl.BlockSpec((tr, c), lambda i: (i, 0))
    return pl.pallas_call(
        body, name=name, grid=(r // tr,),
        in_specs=[row, pl.BlockSpec((np_, tr, c), lambda i: (0, i, 0)), row, row],
        out_specs=[row] * 4, out_shape=[_sds((r, c))] * 4, compiler_params=_params(("parallel",)),
    )(w, gparts, m, v)


def _adamw_native(name, w, g, m, v):
    def body(w_ref, g_ref, m_ref, v_ref, d_ref, mo_ref, vo_ref):
        gv = g_ref[...]
        m2 = ADAM_B1 * m_ref[...] + (1.0 - ADAM_B1) * gv
        v2 = ADAM_B2 * v_ref[...] + (1.0 - ADAM_B2) * jnp.square(gv)
        m_hat = m2 / (1.0 - ADAM_B1 ** ADAM_STEP)
        v_hat = v2 / (1.0 - ADAM_B2 ** ADAM_STEP)
        d_ref[...] = -ADAM_LR * (m_hat / (jnp.sqrt(v_hat) + ADAM_EPS) + ADAM_WD * w_ref[...])
        mo_ref[...] = m2
        vo_ref[...] = v2

    return pl.pallas_call(body, name=name, out_shape=[_sds(w.shape)] * 3, compiler_params=_params())(w, g, m, v)


SMALL = ["c_ctx", "ada_b", "norm1_g", "s5_lam_re", "s5_lam_im", "s5_log_dt", "s5_d", "conv_b", "conv_ln_g", "conv_ln_b",
         "norm2_g", "final_g"]
SMALL_PACKED_ROWS = 24


def _pack_rows(parts, rows):
    flat = jnp.concatenate([p.reshape(-1).astype(F32) for p in parts])
    return jnp.pad(flat, (0, rows * D_MODEL - flat.shape[0])).reshape(rows, D_MODEL)


def _unpack_rows(packed, shapes):
    flat = packed.reshape(-1)
    out, off = [], 0
    for shape in shapes:
        size = 1
        for s in shape:
            size *= s
        out.append(flat[off:off + size].reshape(shape))
        off += size
    return out


def kernel(x, c, ctx, c_ctx, ada_w, ada_b, norm1_g, w_in, s5_lam_re, s5_lam_im, s5_log_dt, s5_b_re, s5_b_im, s5_c_re, s5_c_im, s5_d, s5_w_glu, conv_w, conv_b, conv_ln_g, conv_ln_b, w_out, norm2_g, mlp_w1, mlp_w2, final_g, loss_target, m_c_ctx, m_ada_w, m_ada_b, m_norm1_g, m_w_in, m_s5_lam_re, m_s5_lam_im, m_s5_log_dt, m_s5_b_re, m_s5_b_im, m_s5_c_re, m_s5_c_im, m_s5_d, m_s5_w_glu, m_conv_w, m_conv_b, m_conv_ln_g, m_conv_ln_b, m_w_out, m_norm2_g, m_mlp_w1, m_mlp_w2, m_final_g, v_c_ctx, v_ada_w, v_ada_b, v_norm1_g, v_w_in, v_s5_lam_re, v_s5_lam_im, v_s5_log_dt, v_s5_b_re, v_s5_b_im, v_s5_c_re, v_s5_c_im, v_s5_d, v_s5_w_glu, v_conv_w, v_conv_b, v_conv_ln_g, v_conv_ln_b, v_w_out, v_norm2_g, v_mlp_w1, v_mlp_w2, v_final_g):
    weights = dict(c_ctx=c_ctx, ada_w=ada_w, ada_b=ada_b, norm1_g=norm1_g, w_in=w_in, s5_lam_re=s5_lam_re, s5_lam_im=s5_lam_im, s5_log_dt=s5_log_dt, s5_b_re=s5_b_re, s5_b_im=s5_b_im, s5_c_re=s5_c_re, s5_c_im=s5_c_im, s5_d=s5_d, s5_w_glu=s5_w_glu, conv_w=conv_w, conv_b=conv_b, conv_ln_g=conv_ln_g, conv_ln_b=conv_ln_b, w_out=w_out, norm2_g=norm2_g, mlp_w1=mlp_w1, mlp_w2=mlp_w2, final_g=final_g)
    mom1 = dict(c_ctx=m_c_ctx, ada_w=m_ada_w, ada_b=m_ada_b, norm1_g=m_norm1_g, w_in=m_w_in, s5_lam_re=m_s5_lam_re, s5_lam_im=m_s5_lam_im, s5_log_dt=m_s5_log_dt, s5_b_re=m_s5_b_re, s5_b_im=m_s5_b_im, s5_c_re=m_s5_c_re, s5_c_im=m_s5_c_im, s5_d=m_s5_d, s5_w_glu=m_s5_w_glu, conv_w=m_conv_w, conv_b=m_conv_b, conv_ln_g=m_conv_ln_g, conv_ln_b=m_conv_ln_b, w_out=m_w_out, norm2_g=m_norm2_g, mlp_w1=m_mlp_w1, mlp_w2=m_mlp_w2, final_g=m_final_g)
    mom2 = dict(c_ctx=v_c_ctx, ada_w=v_ada_w, ada_b=v_ada_b, norm1_g=v_norm1_g, w_in=v_w_in, s5_lam_re=v_s5_lam_re, s5_lam_im=v_s5_lam_im, s5_log_dt=v_s5_log_dt, s5_b_re=v_s5_b_re, s5_b_im=v_s5_b_im, s5_c_re=v_s5_c_re, s5_c_im=v_s5_c_im, s5_d=v_s5_d, s5_w_glu=v_s5_w_glu, conv_w=v_conv_w, conv_b=v_conv_b, conv_ln_g=v_conv_ln_g, conv_ln_b=v_conv_ln_b, w_out=v_w_out, norm2_g=v_norm2_g, mlp_w1=v_mlp_w1, mlp_w2=v_mlp_w2, final_g=v_final_g)
    order = list(weights)

    me = 4 * lax.axis_index("x") + 2 * lax.axis_index("y") + lax.axis_index("c")
    xs, cs, tgt = x[0], ctx[0], loss_target[0]
    n_lat_rows, n_ctx_rows = xs.shape[0], cs.shape[0]
    n_rows = n_lat_rows + n_ctx_rows
    n_lat = n_lat_rows // ROW_BLOCK
    ada_cols = ada_w.shape[2]

    (c_all,), _ = _exchange("gather_c", [c], [True])
    c_all = c_all.reshape(NDEV, D_MODEL)

    cond_fwd = jnp.concatenate([c_all, c_ctx[None], jnp.zeros((7, D_MODEL), F32)])
    ada_b_loc = lax.dynamic_slice(ada_b, (0, me * ada_cols), (1, ada_cols))
    (mod_g,), mod_token = _exchange("gather_mod", [_ada_fwd(cond_fwd, ada_w[0], ada_b_loc)], [True])
    weight_groups, weights_token = _exchange_start_groups("gather_weights_start", [
        ([w_in[0].astype(BF16)], [True]),
        ([s5_w_glu[0].astype(BF16), conv_w[0] + mod_token[0:1, 0:1], w_out[0].astype(BF16)], [True] * 3),
        ([mlp_w1[0].astype(BF16), mlp_w2[0].astype(BF16)], [True] * 2)])
    (wi_send, wi_recv, wi_src, wi_land), (mixer_send, mixer_recv, mixer_src, mixer_land), \
        (mlpw_send, mlpw_recv, mlpw_src, mlpw_land) = weight_groups
    mod_rows = jnp.transpose(mod_g, (1, 0, 2)).reshape(16, 6 * D_MODEL) + weights_token[0:1, 0:1]
    mod = lax.dynamic_slice(mod_rows, (me, 0), (1, 6 * D_MODEL)).reshape(6, D_MODEL)
    modc = mod_rows[8, :2 * D_MODEL].reshape(2, D_MODEL)
    sh1, sc1, g1, sh2, sc2, g2 = [mod[i:i + 1] for i in range(6)]

    lam_re, lam_im = s5_lam_re[0].reshape(2, 1, NSTATE), s5_lam_im[0].reshape(2, 1, NSTATE)
    ldt = jnp.repeat(s5_log_dt[0], S5_STATE, axis=-1).reshape(2, 1, NSTATE)
    bt_re = jnp.transpose(s5_b_re[0], (0, 3, 1, 2)).reshape(2, S5_GROUP, NSTATE)
    bt_im = jnp.transpose(s5_b_im[0], (0, 3, 1, 2)).reshape(2, S5_GROUP, NSTATE)
    groups_per_block = S5_GROUPS // S5_BLOCKS
    ct_re = jnp.tile(s5_c_re[0].reshape(2, S5_WIDTH, S5_STATE), (1, 1, groups_per_block))
    ct_im = jnp.tile(s5_c_im[0].reshape(2, S5_WIDTH, S5_STATE), (1, 1, groups_per_block))
    d_skip = s5_d[0].reshape(1, S5_WIDTH)
    perms = [_segment_permutation(reverse_time=(d == 0)) for d in range(2)]
    perms_t = [p.T for p in perms]
    disc = [_s5_discretise(f"s5_disc{d}", False, lam_re[d], lam_im[d], ldt[d], bt_re[d], bt_im[d], ct_re[d], ct_im[d])
            for d in range(2)]

    a_all, a_all_t = _prenorm("prenorm1", xs, cs, norm1_g, jnp.stack([mod[0:2], modc]))
    before_w_in = a_all[0:SUBLANES, 0:LANES].astype(F32) + disc[0][0][0:SUBLANES, 0:LANES] + disc[1][0][0:SUBLANES, 0:LANES]
    wi_own, wi_landed = _exchange_wait("gather_w_in_wait", wi_send, wi_recv, wi_src, wi_land, [True], before_w_in)
    w_in_full = jnp.transpose(wi_landed[0], (1, 0, 2)).reshape(D_MODEL, IN_COLS)
    tm_all = 1088 if n_rows % 1088 == 0 else ROW_BLOCK
    (z_all,) = _matmul("in_proj", a_all, w_in_full, "nn", (n_rows, IN_COLS, D_MODEL), (tm_all, IN_COLS, D_MODEL),
                       [((n_rows, IN_COLS), F32)])

    _, tab, _, bmat, cmat = disc[0]
    s0, y0 = _s5_scan_fwd("s5_scan_fwd0", True, z_all, bmat, cmat, tab, perms[0], perms_t[0])
    mixer_own, mixer_landed = _exchange_wait("gather_mixer_wait", mixer_send, mixer_recv, mixer_src, mixer_land,
                                             [True] * 3, y0)
    glu_g, conv_w_g, w_out_g = mixer_landed
    glu_full = glu_g.reshape(S5_WIDTH, S5_WIDTH)
    conv_w_full = jnp.transpose(conv_w_g, (1, 0, 2)).reshape(CONV_K, CONV_WIDTH)
    w_out_full = w_out_g.reshape(D_MODEL, D_MODEL)
    _, tab, _, bmat, cmat = disc[1]
    s1, y1, ycat = _s5_scan_fwd("s5_scan_fwd1", False, z_all, bmat, cmat, tab, perms[1], perms_t[1],
                                y_other=y0, d_skip=d_skip, w_glu=glu_full)
    states, y_dir = [s0, s1], [y0, y1]

    hh_pad = _conv_gate(z_all, n_lat_rows)
    hc, ycat = _conv_fwd(hh_pad, conv_w_full, conv_b, conv_ln_g, conv_ln_b, ycat, n_lat_rows)

    tm = min(1024, n_lat_rows)
    tm_e = min(512, n_lat_rows)
    w1_cols = D_FF // NDEV
    row_vec = lambda tn: pl.BlockSpec((1, tn), lambda i, j, k: (0, j))
    out_tile = lambda t_m, t_n: pl.BlockSpec((t_m, t_n), lambda i, j, k: (i, j))
    full_rows = ((n_lat_rows, D_MODEL), F32)
    sums = ((n_lat_rows // tm_e, SUBLANES, D_MODEL), F32)
    sums_spec = pl.BlockSpec((None, SUBLANES, D_MODEL), lambda i, j, k: (i, 0, 0))
    vec = lambda v: (v, row_vec(D_MODEL))
    transposed_tile = lambda t_m, t_n: pl.BlockSpec((t_n, t_m), lambda i, j, k: (j, i))
    mix, h1, a2, a2_t = _matmul(
        "out_proj", ycat, w_out_full, "nn", (n_lat_rows, D_MODEL, D_MODEL), (tm_e, D_MODEL, D_MODEL),
        [full_rows, full_rows, ((n_lat_rows, D_MODEL), BF16), ((D_MODEL, n_lat_rows), BF16)],
        epi=_epi_residual_prenorm,
        epi_extra=[(xs, out_tile(tm_e, D_MODEL)), vec(g1), vec(norm2_g), vec(sc2), vec(sh2)],
        out_specs=[out_tile(tm_e, D_MODEL)] * 3 + [transposed_tile(tm_e, D_MODEL)])
    mlpw_own, mlpw_landed = _exchange_wait("gather_mlp_wait", mlpw_send, mlpw_recv, mlpw_src, mlpw_land, [True] * 2, a2)
    w1_g, w2_g = mlpw_landed
    w2_full = w2_g.reshape(D_FF, D_MODEL)
    tm_up = min(2048, n_lat_rows)
    f, f_t = _matmul("mlp_up", a2, w1_g, "nn", (n_lat_rows, D_FF, D_MODEL), (tm_up, w1_cols, D_MODEL),
                     [((n_lat_rows, D_FF), BF16), ((D_FF, n_lat_rows), BF16)], epi=lambda acc: (acc, acc.T),
                     b_spec=pl.BlockSpec((None, D_MODEL, w1_cols), lambda i, j, k: (j, 0, 0)),
                     out_specs=[out_tile(tm_up, w1_cols), transposed_tile(tm_up, w1_cols)])
    sq_relu = lambda t: jnp.square(jnp.maximum(t, 0.0))
    mlp_out, d_h2, dm2, err_sums, d_final_g8 = _matmul(
        "mlp_down", f, w2_full, "nn", (n_lat_rows, D_MODEL, D_FF), (tm_e, D_MODEL, 2048),
        [full_rows, full_rows, ((n_lat_rows, D_MODEL), BF16), sums, sums], a_fn=sq_relu, epi=_epi_residual_loss,
        epi_extra=[(h1, out_tile(tm_e, D_MODEL)), vec(g2), (tgt, out_tile(tm_e, D_MODEL)), vec(final_g[None])],
        out_specs=[out_tile(tm_e, D_MODEL)] * 3 + [sums_spec] * 2)

    (d_f,) = _matmul("mlp_down_dx", dm2, w2_full, "nt", (n_lat_rows, D_FF, D_MODEL), (tm_up, 1024, D_MODEL),
                     [((n_lat_rows, D_FF), BF16)],
                     epi=lambda acc, ft: (acc * 2.0 * jnp.maximum(ft.astype(F32), 0.0),),
                     epi_extra=[(f, out_tile(tm_up, 1024))])
    tk_dw = min(2048, n_lat_rows)
    (g_w2,) = _matmul("mlp_down_dw", f_t, dm2, "nn", (D_FF, D_MODEL, n_lat_rows), (1024, D_MODEL, tk_dw),
                      [((D_FF, D_MODEL), F32)], a_fn=sq_relu)
    (g_w1,) = _matmul("mlp_up_dw", a2_t, d_f, "nn", (D_MODEL, D_FF, n_lat_rows), (D_MODEL, w1_cols, n_lat_rows),
                      [((NDEV, D_MODEL, w1_cols), F32)],
                      out_specs=[pl.BlockSpec((None, D_MODEL, w1_cols), lambda i, j, k: (j, 0, 0))])
    mlp_send, mlp_recv, mlp_src, mlp_land, mlp_token = _exchange_start(
        "scatter_mlp_start", [g_w1, g_w2.reshape(NDEV, D_FF // NDEV, D_MODEL)], [False] * 2)
    d_h1, dm1, *sums2 = _matmul(
        "mlp_up_dx", d_f, w1_g, "nt", (n_lat_rows, D_MODEL, D_FF), (tm_e, D_MODEL, 4 * w1_cols),
        [full_rows, ((n_lat_rows, D_MODEL), BF16)] + [sums] * 4, epi=_epi_norm_bwd,
        epi_extra=[(h1, out_tile(tm_e, D_MODEL)), (d_h2, out_tile(tm_e, D_MODEL)), (mlp_out, out_tile(tm_e, D_MODEL)),
                   vec(norm2_g), vec(sc2 + mlp_token[0:1, 0:1]), vec(g1)],
        b_spec=pl.BlockSpec((4, D_MODEL, w1_cols), lambda i, j, k: (k, 0, 0)), b_slabs=4,
        out_specs=[out_tile(tm_e, D_MODEL)] * 2 + [sums_spec] * 4)

    (d_ycat,) = _matmul("out_proj_dx", dm1, w_out_full, "nt", (n_lat_rows, D_MODEL, D_MODEL), (tm, D_MODEL, D_MODEL),
                        [((n_lat_rows, D_MODEL), F32)])
    (g_w_out,) = _matmul("out_proj_dw", ycat, dm1, "tn", (D_MODEL, D_MODEL, n_lat_rows), (D_MODEL, D_MODEL, tm),
                         [((D_MODEL, D_MODEL), F32)])

    dy, g_glu, dd8 = _glu_bwd(d_ycat, z_all, y_dir[0], y_dir[1], d_skip, glu_full, n_lat_rows)
    proj_send, proj_recv, proj_src, proj_land, proj_token = _exchange_start(
        "scatter_proj_start",
        [g_w_out.reshape(NDEV, D_MODEL // NDEV, D_MODEL), g_glu.reshape(NDEV, S5_WIDTH // NDEV, S5_WIDTH)], [False] * 2)
    perms = [p + proj_token[0:1, 0:1].astype(BF16) for p in perms]
    du, g_lam_re, g_lam_im, g_ldt, g_bt, g_cdiag = None, [], [], [], [], []
    for d in range(2):
        _, _, adj, bmat, cmat = disc[d]
        du, d_bdiag, d_cdiag, d_abar8 = _s5_scan_bwd(f"s5_scan_bwd{d}", d == 0, dy, z_all, states[d], bmat, cmat, adj,
                                                     perms[d], perms_t[d], du_other=du, d_skip=d_skip if d else None)
        d_bbar = jnp.transpose(d_bdiag.reshape(S5_BLOCKS, S5_GROUP, 2, NSTATE // S5_BLOCKS), (2, 1, 0, 3)).reshape(
            2 * S5_GROUP, NSTATE)
        d_lam8, d_bt = _s5_discretise_bwd(f"s5_disc_bwd{d}", lam_re[d], lam_im[d], ldt[d], bt_re[d], bt_im[d], d_abar8, d_bbar)
        g_lam_re.append(d_lam8[0].reshape(S5_GROUPS, S5_STATE))
        g_lam_im.append(d_lam8[1].reshape(S5_GROUPS, S5_STATE))
        g_ldt.append(d_lam8[2].reshape(S5_GROUPS, S5_STATE).sum(axis=-1))
        g_bt.append(d_bt)
        g_cdiag.append(d_cdiag)

    dhc_pad, conv_sums = _conv_bwd_norm(d_ycat, hc, conv_ln_g, conv_ln_b, n_lat_rows)
    d_v, d_gate, g_conv_w8 = _conv_bwd_taps(dhc_pad, hh_pad, z_all, conv_w_full, n_lat_rows)

    no_ctx = jnp.zeros((n_ctx_rows, CONV_WIDTH), BF16)
    dz_all = jnp.concatenate([du, jnp.concatenate([d_v, no_ctx]), jnp.concatenate([d_gate, no_ctx])], axis=1)
    (g_w_in_full,) = _matmul("in_proj_dw", a_all_t, dz_all, "nn", (D_MODEL, IN_COLS, n_rows),
                             (D_MODEL, IN_COLS, n_rows // 2), [((D_MODEL, IN_COLS), BF16)])
    g_w_in_parts = jnp.transpose(g_w_in_full.reshape(D_MODEL, NDEV, IN_COLS // NDEV), (1, 0, 2))
    win_send, win_recv, win_src, win_land, win_token = _exchange_start("scatter_w_in_start", [g_w_in_parts], [False])
    w_in_late = w_in_full + win_token[0:1, 0:1].astype(BF16)
    grad_x, *sums1 = _matmul(
        "in_proj_dx", dz_all, w_in_late, "nt", (n_lat_rows, D_MODEL, IN_COLS), (tm_e, D_MODEL, IN_COLS),
        [full_rows] + [sums] * 4, epi=_epi_norm_bwd,
        epi_extra=[(xs, out_tile(tm_e, D_MODEL)), (d_h1, out_tile(tm_e, D_MODEL)), (mix, out_tile(tm_e, D_MODEL)),
                   vec(norm1_g), vec(sc1)],
        out_specs=[out_tile(tm_e, D_MODEL)] + [sums_spec] * 4)
    (d_a_ctx,) = _matmul("in_proj_dx_ctx", dz_all, w_in_late, "nt", (n_ctx_rows, D_MODEL, IN_COLS),
                         (ROW_BLOCK, D_MODEL, IN_COLS), [((n_ctx_rows, D_MODEL), F32)],
                         a_spec=pl.BlockSpec((ROW_BLOCK, IN_COLS), lambda i, j, k: (i + n_lat, 0)))
    (sums1c,) = _norm_bwd("norm1_bwd_ctx", cs, d_a_ctx, 0, norm1_g, modc[1:2])

    s1, s1c, s2 = [p.sum(axis=(0, 1)) for p in sums1], sums1c.sum(axis=1), [p.sum(axis=(0, 1)) for p in sums2]
    d_mod = jnp.concatenate([s1[0], s1[1], s1[3], s2[0], s2[1], s2[3]])
    d_modc = jnp.concatenate([s1c[0], s1c[1], jnp.zeros((4 * D_MODEL,), F32)])
    (dmod_g,), _ = _exchange("gather_dmod", [jnp.stack([d_mod, d_modc])], [True])
    dmod16 = jnp.concatenate([dmod_g[:, 0], dmod_g[:, 1]])
    dmod16_loc = lax.dynamic_slice(dmod16, (0, me * ada_cols), (16, ada_cols))
    cond_bwd = jnp.concatenate([c_all, jnp.broadcast_to(c_ctx[None], (NDEV, D_MODEL))])
    g_ada_w, g_c_ctx8 = _ada_bwd(cond_bwd, dmod16_loc, ada_w[0], c_ctx[None])

    small_parts = dict(
        c_ctx=g_c_ctx8[0], ada_b=d_mod + d_modc, norm1_g=s1[2] + s1c[2],
        s5_lam_re=jnp.stack(g_lam_re), s5_lam_im=jnp.stack(g_lam_im), s5_log_dt=jnp.stack(g_ldt),
        s5_d=dd8.sum(axis=0), conv_b=conv_sums[0].sum(axis=0), conv_ln_g=conv_sums[1].sum(axis=0),
        conv_ln_b=conv_sums[2].sum(axis=0), norm2_g=s2[2], final_g=d_final_g8.sum(axis=(0, 1)))
    reduced_shapes = [(SMALL_PACKED_ROWS, D_MODEL), (2, 2 * S5_GROUP, NSTATE), (2,) + _S5_DIAG, (1,)]
    small_g = _pack_rows(
        [_pack_rows([small_parts[n] for n in SMALL], SMALL_PACKED_ROWS), jnp.stack(g_bt), jnp.stack(g_cdiag),
         (0.5 / D_MODEL * jnp.sum(err_sums)).reshape(1)], SMALL_ROWS).reshape(NDEV, SMALL_ROWS // NDEV, D_MODEL)
    g_conv_w_parts = jnp.transpose(g_conv_w8.sum(axis=1).reshape(CONV_K, NDEV, CONV_WIDTH // NDEV), (1, 0, 2))

    res = {}

    def adamw_big(name, parts):
        outs = _adamw("adamw_" + name, weights[name][0], parts, mom1[name][0], mom2[name][0])
        res[name] = [o[None] for o in outs]
        return outs[0]

    sm_send, sm_recv, sm_src, sm_land, sm_token = _exchange_start("scatter_small_start", [g_conv_w_parts, small_g],
                                                                  [False] * 2)
    _, (p_w1, p_w2) = _exchange_wait("scatter_mlp_wait", mlp_send, mlp_recv, mlp_src, mlp_land, [False] * 2, sm_token)
    adamw_big("ada_w", g_ada_w[None])
    adamw_big("mlp_w1", p_w1)
    done = adamw_big("mlp_w2", p_w2)
    _, (p_conv_w, p_small) = _exchange_wait("scatter_small_wait", sm_send, sm_recv, sm_src, sm_land, [False] * 2, done)
    ga_send, ga_recv, ga_src, ga_land, ga_token = _exchange_start("gather_small_start", [_sum_parts(p_small)], [True])
    _, (p_w_out, p_glu) = _exchange_wait("scatter_proj_wait", proj_send, proj_recv, proj_src, proj_land, [False] * 2,
                                         ga_token)
    adamw_big("w_out", p_w_out)
    done = adamw_big("s5_w_glu", p_glu)
    _, (p_w_in,) = _exchange_wait("scatter_w_in_wait", win_send, win_recv, win_src, win_land, [False], done)
    adamw_big("w_in", p_w_in)
    done = adamw_big("conv_w", p_conv_w)
    _, (small_all,) = _exchange_wait("gather_small_wait", ga_send, ga_recv, ga_src, ga_land, [True], done)
    small_all = small_all.reshape(1, SMALL_ROWS, D_MODEL)
    _, r_bt, r_cdiag, loss = _unpack_rows(small_all, reduced_shapes)
    loss = loss.reshape(())
    pack = lambda src: _pack_rows([src[n] for n in SMALL], SMALL_PACKED_ROWS)
    outs = _adamw("adamw_small", pack(weights), small_all, pack(mom1), pack(mom2))
    unpacked = [_unpack_rows(o, [weights[n].shape for n in SMALL]) for o in outs]
    for i, name in enumerate(SMALL):
        res[name] = [u[i] for u in unpacked]
    to_ghp = lambda t: jnp.transpose(t.reshape(2, S5_GROUP, S5_GROUPS, S5_STATE), (0, 2, 1, 3))[None]
    r_c = jnp.transpose(r_cdiag.reshape(2, S5_BLOCKS, S5_GROUP, 2, groups_per_block, S5_STATE), (3, 0, 1, 4, 2, 5)).reshape(
        2, 1, 2, S5_GROUPS, S5_GROUP, S5_STATE)
    swap = lambda t: jnp.swapaxes(t, -1, -2)
    for name, grad in (("s5_b_re", to_ghp(r_bt[:, :S5_GROUP])), ("s5_b_im", to_ghp(r_bt[:, S5_GROUP:]))):
        outs = _adamw_native("adamw_" + name, swap(weights[name]), grad, swap(mom1[name]), swap(mom2[name]))
        res[name] = [swap(grad), *[swap(o) for o in outs]]
    for name, grad in (("s5_c_re", r_c[0]), ("s5_c_im", -r_c[1])):
        res[name] = [grad, *_adamw_native("adamw_" + name, weights[name], grad, mom1[name], mom2[name])]

    return (loss, grad_x[None], *[res[n][0] for n in order], *[res[n][1] for n in order],
            *[res[n][2] for n in order], *[res[n][3] for n in order])
```
